```python
import jax, jax.numpy as jnp
from jax import lax
import numpy as np

D_MODEL = 1024
BATCH = 8
SEQ = 2048
DEPTH = 2

MEM_LEN = 256
EPS = 1e-6
MLA_HEADS = 8
QK_NOPE = 64
QK_ROPE = 32
QK_HEAD = QK_NOPE + QK_ROPE
V_HEAD = 64
Q_LORA = 256
KV_LORA = 128
ROPE_THETA = 10000.0
Q_BLOCK = 128
CONV_WIDTH = 512
CONV_K = 3
SG_WIDTH = 512
SG_GROUPS = 4
SG_CHUNK = 128
MEM_HEADS = 4
MEM_HEAD_DIM = 128
N_BRANCH = 4
BRANCH_WIDTH = 512

IN_SIZES = [Q_LORA, KV_LORA, QK_ROPE, 3 * CONV_WIDTH, 2 * SG_WIDTH,
            MEM_HEADS * MEM_HEAD_DIM, N_BRANCH * BRANCH_WIDTH, N_BRANCH * D_MODEL]
IN_WIDTH = sum(IN_SIZES)
IN_OFFSETS = [int(o) for o in np.cumsum(IN_SIZES)[:-1]]
NEG_INF = -1e30

kernel_name = "hybrid_gated_mla_conv_sgmlp_memory"


def rms_norm(x, g):
    x32 = x.astype(jnp.float32)
    y = x32 * lax.rsqrt(jnp.mean(x32 * x32, axis=-1, keepdims=True) + EPS)
    return (y * g.astype(jnp.float32)).astype(x.dtype)


def layer_norm(x, g, b):
    x32 = x.astype(jnp.float32)
    mu = jnp.mean(x32, axis=-1, keepdims=True)
    xc = x32 - mu
    y = xc * lax.rsqrt(jnp.mean(xc * xc, axis=-1, keepdims=True) + EPS)
    return (y * g.astype(jnp.float32) + b.astype(jnp.float32)).astype(x.dtype)


def apply_rope(x, pos):
    half = x.shape[-1] // 2
    inv_freq = ROPE_THETA ** (-jnp.arange(half, dtype=jnp.float32) / half)
    ang = pos.astype(jnp.float32)[..., None] * inv_freq
    cos = jnp.cos(ang)[:, :, None, :].astype(x.dtype)
    sin = jnp.sin(ang)[:, :, None, :].astype(x.dtype)
    x1, x2 = x[..., :half], x[..., half:]
    return jnp.concatenate([x1 * cos - x2 * sin, x1 * sin + x2 * cos], axis=-1)


def causal_block_attention(q, k, v, scale):
    S = q.shape[1]
    outs = []
    for i in range(S // Q_BLOCK):
        lo, hi = i * Q_BLOCK, (i + 1) * Q_BLOCK
        qb, kb, vb = q[:, lo:hi], k[:, :hi], v[:, :hi]
        s = jnp.einsum('bqhd,bkhd->bhqk', qb, kb).astype(jnp.float32) * scale
        mask = jnp.arange(hi)[None, :] <= (lo + jnp.arange(Q_BLOCK))[:, None]
        s = jnp.where(mask, s, NEG_INF)
        p = jax.nn.softmax(s, axis=-1).astype(vb.dtype)
        outs.append(jnp.einsum('bhqk,bkhd->bqhd', p, vb))
    return jnp.concatenate(outs, axis=1)


def mla_branch(c_q, c_kv, k_rope, pos, cq_g, ckv_g, w_uq, w_ukv, q_g, k_g):
    B, S, _ = c_q.shape
    q = (rms_norm(c_q, cq_g) @ w_uq).reshape(B, S, MLA_HEADS, QK_HEAD)
    kv = (rms_norm(c_kv, ckv_g) @ w_ukv).reshape(B, S, MLA_HEADS, QK_NOPE + V_HEAD)
    k_nope, v = kv[..., :QK_NOPE], kv[..., QK_NOPE:]
    k_r = jnp.broadcast_to(k_rope[:, :, None, :], (B, S, MLA_HEADS, QK_ROPE))
    k = jnp.concatenate([k_nope, k_r], axis=-1)
    q = rms_norm(q, q_g)
    k = rms_norm(k, k_g)
    q = jnp.concatenate([q[..., :QK_NOPE], apply_rope(q[..., QK_NOPE:], pos)], axis=-1)
    k = jnp.concatenate([k[..., :QK_NOPE], apply_rope(k[..., QK_NOPE:], pos)], axis=-1)
    o = causal_block_attention(q, k, v, QK_HEAD ** -0.5)
    return o.reshape(B, S, MLA_HEADS * V_HEAD)


def shortconv_branch(conv_in, conv_w, conv_b):
    b_gate, c_gate, xin = jnp.split(conv_in, 3, axis=-1)
    S = xin.shape[1]
    z = c_gate * xin
    zp = jnp.pad(z, ((0, 0), (CONV_K - 1, 0), (0, 0)))
    y = conv_b + conv_w[0] * zp[:, 0:S]
    for j in range(1, CONV_K):
        y = y + conv_w[j] * zp[:, j:j + S]
    return b_gate * y


def spatial_gating_branch(sg_in, ln_g, ln_b, w_s, b_s):
    u, v = jnp.split(sg_in, 2, axis=-1)
    B, S, _ = v.shape
    v = layer_norm(v, ln_g, ln_b)
    v = v.reshape(B, S // SG_CHUNK, SG_CHUNK, SG_GROUPS, SG_WIDTH // SG_GROUPS)
    tril = jnp.tril(jnp.ones((SG_CHUNK, SG_CHUNK), dtype=bool))
    w = jnp.where(tril[None], w_s, jnp.zeros((), w_s.dtype))
    mixed = jnp.einsum('gts,bcsgd->bctgd', w, v) + b_s.T[None, None, :, :, None]
    return u * mixed.reshape(B, S, SG_WIDTH)


def memory_branch(q_in, mem, mem_g, w_mem_kv, q_g, k_g):
    B, S, _ = q_in.shape
    M = mem.shape[1]
    q = rms_norm(q_in.reshape(B, S, MEM_HEADS, MEM_HEAD_DIM), q_g)
    kv = (rms_norm(mem, mem_g) @ w_mem_kv).reshape(B, M, 2, MEM_HEADS, MEM_HEAD_DIM)
    k = rms_norm(kv[:, :, 0], k_g)
    v = kv[:, :, 1]
    s = jnp.einsum('bqhd,bmhd->bhqm', q, k).astype(jnp.float32) * (MEM_HEAD_DIM ** -0.5)
    p = jax.nn.softmax(s, axis=-1).astype(v.dtype)
    o = jnp.einsum('bhqm,bmhd->bqhd', p, v)
    return o.reshape(B, S, MEM_HEADS * MEM_HEAD_DIM)


def hybrid_layer(x, mem, pos, norm_g, w_in, cq_g, ckv_g, w_uq, w_ukv, mla_qg, mla_kg,
                 conv_w, conv_b, sg_ln_g, sg_ln_b, w_s, b_s, mem_g, w_mem_kv, mem_qg, mem_kg,
                 b_merge, w_branch, w_out):
    B, S, D = x.shape
    h = rms_norm(x, norm_g)
    proj = h @ w_in
    c_q, c_kv, k_rope, conv_in, sg_in, mem_q, silu_gates, merge_logits = jnp.split(proj, IN_OFFSETS, axis=-1)
    y_a = mla_branch(c_q, c_kv, k_rope, pos, cq_g, ckv_g, w_uq, w_ukv, mla_qg, mla_kg)
    y_b = shortconv_branch(conv_in, conv_w, conv_b)
    y_c = spatial_gating_branch(sg_in, sg_ln_g, sg_ln_b, w_s, b_s)
    y_d = memory_branch(mem_q, mem, mem_g, w_mem_kv, mem_qg, mem_kg)
    ys = jnp.stack([y_a, y_b, y_c, y_d], axis=2)
    ys = ys * jax.nn.silu(silu_gates.reshape(B, S, N_BRANCH, BRANCH_WIDTH))
    z = jnp.einsum('bsnw,nwd->bsnd', ys, w_branch)
    gate = jax.nn.sigmoid(merge_logits.reshape(B, S, N_BRANCH, D) + b_merge)
    merged = jnp.sum(gate * z, axis=2)
    return x + merged @ w_out


def _fwd_setup_inputs(seed: int = 0) -> dict:
    key = jax.random.key(seed)
    ks = jax.random.split(key, 32)
    f32 = jnp.float32
    L = DEPTH

    def nrm(k, shape, scale):
        return jax.random.normal(k, shape, f32) * scale

    def gain(k, shape):
        return 1.0 + 0.05 * jax.random.normal(k, shape, f32)

    x = jax.random.normal(ks[0], (BATCH, SEQ, D_MODEL), f32)
    mem = jax.random.normal(ks[1], (BATCH, MEM_LEN, D_MODEL), f32)
    offset = jax.random.randint(ks[2], (BATCH, 1), 0, 1024, dtype=jnp.int32)
    positions = offset + jnp.arange(SEQ, dtype=jnp.int32)[None, :]
    return {
        "x": x,
        "mem": mem,
        "positions": positions,
        "norm_g": gain(ks[3], (L, D_MODEL)),
        "w_in": nrm(ks[4], (L, D_MODEL, IN_WIDTH), D_MODEL ** -0.5),
        "cq_norm_g": gain(ks[5], (L, Q_LORA)),
        "ckv_norm_g": gain(ks[6], (L, KV_LORA)),
        "w_uq": nrm(ks[7], (L, Q_LORA, MLA_HEADS * QK_HEAD), Q_LORA ** -0.5),
        "w_ukv": nrm(ks[8], (L, KV_LORA, MLA_HEADS * (QK_NOPE + V_HEAD)), KV_LORA ** -0.5),
        "mla_q_norm_g": gain(ks[9], (L, QK_HEAD)),
        "mla_k_norm_g": gain(ks[10], (L, QK_HEAD)),
        "conv_w": nrm(ks[11], (L, CONV_K, CONV_WIDTH), CONV_K ** -0.5),
        "conv_b": nrm(ks[12], (L, CONV_WIDTH), 0.02),
        "sg_ln_g": gain(ks[13], (L, SG_WIDTH)),
        "sg_ln_b": nrm(ks[14], (L, SG_WIDTH), 0.02),
        "w_spatial": nrm(ks[15], (L, SG_GROUPS, SG_CHUNK, SG_CHUNK), 0.5 * SG_CHUNK ** -0.5),
        "b_spatial": 1.0 + 0.1 * jax.random.normal(ks[16], (L, SG_GROUPS, SG_CHUNK), f32),
        "mem_norm_g": gain(ks[17], (L, D_MODEL)),
        "w_mem_kv": nrm(ks[18], (L, D_MODEL, 2 * MEM_HEADS * MEM_HEAD_DIM), D_MODEL ** -0.5),
        "mem_q_norm_g": gain(ks[19], (L, MEM_HEAD_DIM)),
        "mem_k_norm_g": gain(ks[20], (L, MEM_HEAD_DIM)),
        "b_merge": nrm(ks[21], (L, N_BRANCH, D_MODEL), 0.1),
        "w_branch": nrm(ks[22], (L, N_BRANCH, BRANCH_WIDTH, D_MODEL), BRANCH_WIDTH ** -0.5),
        "w_out": nrm(ks[23], (L, D_MODEL, D_MODEL), D_MODEL ** -0.5),
    }


def _fwd_reference(x, mem, positions, norm_g, w_in, cq_norm_g, ckv_norm_g, w_uq, w_ukv,
              mla_q_norm_g, mla_k_norm_g, conv_w, conv_b, sg_ln_g, sg_ln_b, w_spatial,
              b_spatial, mem_norm_g, w_mem_kv, mem_q_norm_g, mem_k_norm_g, b_merge,
              w_branch, w_out):
    for l in range(DEPTH):
        x = hybrid_layer(x, mem, positions, norm_g[l], w_in[l], cq_norm_g[l], ckv_norm_g[l],
                         w_uq[l], w_ukv[l], mla_q_norm_g[l], mla_k_norm_g[l], conv_w[l], conv_b[l],
                         sg_ln_g[l], sg_ln_b[l], w_spatial[l], b_spatial[l], mem_norm_g[l],
                         w_mem_kv[l], mem_q_norm_g[l], mem_k_norm_g[l], b_merge[l],
                         w_branch[l], w_out[l])
    return x


import jax as _jax
import jax.numpy as _jnp

TWIN_FORMAT = 'train_step'
FWD_PARAMS = ['x', 'mem', 'positions', 'norm_g', 'w_in', 'cq_norm_g', 'ckv_norm_g', 'w_uq', 'w_ukv', 'mla_q_norm_g', 'mla_k_norm_g', 'conv_w', 'conv_b', 'sg_ln_g', 'sg_ln_b', 'w_spatial', 'b_spatial', 'mem_norm_g', 'w_mem_kv', 'mem_q_norm_g', 'mem_k_norm_g', 'b_merge', 'w_branch', 'w_out']
TWIN_WEIGHTS = ['norm_g', 'w_in', 'cq_norm_g', 'ckv_norm_g', 'w_uq', 'w_ukv', 'mla_q_norm_g', 'mla_k_norm_g', 'conv_w', 'conv_b', 'sg_ln_g', 'sg_ln_b', 'w_spatial', 'b_spatial', 'mem_norm_g', 'w_mem_kv', 'mem_q_norm_g', 'mem_k_norm_g', 'b_merge', 'w_branch', 'w_out']
TWIN_DIFF_INPUT = 'x'
TWIN_INPUTS = ['x', 'mem', 'positions', 'norm_g', 'w_in', 'cq_norm_g', 'ckv_norm_g', 'w_uq', 'w_ukv', 'mla_q_norm_g', 'mla_k_norm_g', 'conv_w', 'conv_b', 'sg_ln_g', 'sg_ln_b', 'w_spatial', 'b_spatial', 'mem_norm_g', 'w_mem_kv', 'mem_q_norm_g', 'mem_k_norm_g', 'b_merge', 'w_branch', 'w_out', 'loss_target', 'm_norm_g', 'm_w_in', 'm_cq_norm_g', 'm_ckv_norm_g', 'm_w_uq', 'm_w_ukv', 'm_mla_q_norm_g', 'm_mla_k_norm_g', 'm_conv_w', 'm_conv_b', 'm_sg_ln_g', 'm_sg_ln_b', 'm_w_spatial', 'm_b_spatial', 'm_mem_norm_g', 'm_w_mem_kv', 'm_mem_q_norm_g', 'm_mem_k_norm_g', 'm_b_merge', 'm_w_branch', 'm_w_out', 'v_norm_g', 'v_w_in', 'v_cq_norm_g', 'v_ckv_norm_g', 'v_w_uq', 'v_w_ukv', 'v_mla_q_norm_g', 'v_mla_k_norm_g', 'v_conv_w', 'v_conv_b', 'v_sg_ln_g', 'v_sg_ln_b', 'v_w_spatial', 'v_b_spatial', 'v_mem_norm_g', 'v_w_mem_kv', 'v_mem_q_norm_g', 'v_mem_k_norm_g', 'v_b_merge', 'v_w_branch', 'v_w_out']
TWIN_OUTPUTS = ['loss', 'grad_x', 'grad_norm_g', 'grad_w_in', 'grad_cq_norm_g', 'grad_ckv_norm_g', 'grad_w_uq', 'grad_w_ukv', 'grad_mla_q_norm_g', 'grad_mla_k_norm_g', 'grad_conv_w', 'grad_conv_b', 'grad_sg_ln_g', 'grad_sg_ln_b', 'grad_w_spatial', 'grad_b_spatial', 'grad_mem_norm_g', 'grad_w_mem_kv', 'grad_mem_q_norm_g', 'grad_mem_k_norm_g', 'grad_b_merge', 'grad_w_branch', 'grad_w_out', 'delta_norm_g', 'delta_w_in', 'delta_cq_norm_g', 'delta_ckv_norm_g', 'delta_w_uq', 'delta_w_ukv', 'delta_mla_q_norm_g', 'delta_mla_k_norm_g', 'delta_conv_w', 'delta_conv_b', 'delta_sg_ln_g', 'delta_sg_ln_b', 'delta_w_spatial', 'delta_b_spatial', 'delta_mem_norm_g', 'delta_w_mem_kv', 'delta_mem_q_norm_g', 'delta_mem_k_norm_g', 'delta_b_merge', 'delta_w_branch', 'delta_w_out', 'new_m_norm_g', 'new_m_w_in', 'new_m_cq_norm_g', 'new_m_ckv_norm_g', 'new_m_w_uq', 'new_m_w_ukv', 'new_m_mla_q_norm_g', 'new_m_mla_k_norm_g', 'new_m_conv_w', 'new_m_conv_b', 'new_m_sg_ln_g', 'new_m_sg_ln_b', 'new_m_w_spatial', 'new_m_b_spatial', 'new_m_mem_norm_g', 'new_m_w_mem_kv', 'new_m_mem_q_norm_g', 'new_m_mem_k_norm_g', 'new_m_b_merge', 'new_m_w_branch', 'new_m_w_out', 'new_v_norm_g', 'new_v_w_in', 'new_v_cq_norm_g', 'new_v_ckv_norm_g', 'new_v_w_uq', 'new_v_w_ukv', 'new_v_mla_q_norm_g', 'new_v_mla_k_norm_g', 'new_v_conv_w', 'new_v_conv_b', 'new_v_sg_ln_g', 'new_v_sg_ln_b', 'new_v_w_spatial', 'new_v_b_spatial', 'new_v_mem_norm_g', 'new_v_w_mem_kv', 'new_v_mem_q_norm_g', 'new_v_mem_k_norm_g', 'new_v_b_merge', 'new_v_w_branch', 'new_v_w_out']
TWIN_LEAF_KINDS = {'loss': 'loss', 'grad_x': 'grad_x', 'grad_norm_g': 'grad_w', 'grad_w_in': 'grad_w', 'grad_cq_norm_g': 'grad_w', 'grad_ckv_norm_g': 'grad_w', 'grad_w_uq': 'grad_w', 'grad_w_ukv': 'grad_w', 'grad_mla_q_norm_g': 'grad_w', 'grad_mla_k_norm_g': 'grad_w', 'grad_conv_w': 'grad_w', 'grad_conv_b': 'grad_w', 'grad_sg_ln_g': 'grad_w', 'grad_sg_ln_b': 'grad_w', 'grad_w_spatial': 'grad_w', 'grad_b_spatial': 'grad_w', 'grad_mem_norm_g': 'grad_w', 'grad_w_mem_kv': 'grad_w', 'grad_mem_q_norm_g': 'grad_w', 'grad_mem_k_norm_g': 'grad_w', 'grad_b_merge': 'grad_w', 'grad_w_branch': 'grad_w', 'grad_w_out': 'grad_w', 'delta_norm_g': 'delta_w', 'delta_w_in': 'delta_w', 'delta_cq_norm_g': 'delta_w', 'delta_ckv_norm_g': 'delta_w', 'delta_w_uq': 'delta_w', 'delta_w_ukv': 'delta_w', 'delta_mla_q_norm_g': 'delta_w', 'delta_mla_k_norm_g': 'delta_w', 'delta_conv_w': 'delta_w', 'delta_conv_b': 'delta_w', 'delta_sg_ln_g': 'delta_w', 'delta_sg_ln_b': 'delta_w', 'delta_w_spatial': 'delta_w', 'delta_b_spatial': 'delta_w', 'delta_mem_norm_g': 'delta_w', 'delta_w_mem_kv': 'delta_w', 'delta_mem_q_norm_g': 'delta_w', 'delta_mem_k_norm_g': 'delta_w', 'delta_b_merge': 'delta_w', 'delta_w_branch': 'delta_w', 'delta_w_out': 'delta_w', 'new_m_norm_g': 'new_m', 'new_m_w_in': 'new_m', 'new_m_cq_norm_g': 'new_m', 'new_m_ckv_norm_g': 'new_m', 'new_m_w_uq': 'new_m', 'new_m_w_ukv': 'new_m', 'new_m_mla_q_norm_g': 'new_m', 'new_m_mla_k_norm_g': 'new_m', 'new_m_conv_w': 'new_m', 'new_m_conv_b': 'new_m', 'new_m_sg_ln_g': 'new_m', 'new_m_sg_ln_b': 'new_m', 'new_m_w_spatial': 'new_m', 'new_m_b_spatial': 'new_m', 'new_m_mem_norm_g': 'new_m', 'new_m_w_mem_kv': 'new_m', 'new_m_mem_q_norm_g': 'new_m', 'new_m_mem_k_norm_g': 'new_m', 'new_m_b_merge': 'new_m', 'new_m_w_branch': 'new_m', 'new_m_w_out': 'new_m', 'new_v_norm_g': 'new_v', 'new_v_w_in': 'new_v', 'new_v_cq_norm_g': 'new_v', 'new_v_ckv_norm_g': 'new_v', 'new_v_w_uq': 'new_v', 'new_v_w_ukv': 'new_v', 'new_v_mla_q_norm_g': 'new_v', 'new_v_mla_k_norm_g': 'new_v', 'new_v_conv_w': 'new_v', 'new_v_conv_b': 'new_v', 'new_v_sg_ln_g': 'new_v', 'new_v_sg_ln_b': 'new_v', 'new_v_w_spatial': 'new_v', 'new_v_b_spatial': 'new_v', 'new_v_mem_norm_g': 'new_v', 'new_v_w_mem_kv': 'new_v', 'new_v_mem_q_norm_g': 'new_v', 'new_v_mem_k_norm_g': 'new_v', 'new_v_b_merge': 'new_v', 'new_v_w_branch': 'new_v', 'new_v_w_out': 'new_v'}


def _forward(args):
    return _fwd_reference(*[args[k] for k in FWD_PARAMS])


def _output_shape():
    out = _jax.eval_shape(lambda: _forward(_fwd_setup_inputs(0)))
    return out.shape, out.dtype

N_MICROBATCH = 1
ADAM_LR = 0.001
ADAM_B1 = 0.9
ADAM_B2 = 0.999
ADAM_EPS = 1e-08
ADAM_WD = 0.01
ADAM_STEP = 10
PER_EXAMPLE_BATCH_AXIS = {'x': 0, 'mem': 0, 'positions': 0, 'loss_target': 0}
SHARED_INPUTS = []
_WEIGHT_DTYPES = {'norm_g': _jnp.float32, 'w_in': _jnp.float32, 'cq_norm_g': _jnp.float32, 'ckv_norm_g': _jnp.float32, 'w_uq': _jnp.float32, 'w_ukv': _jnp.float32, 'mla_q_norm_g': _jnp.float32, 'mla_k_norm_g': _jnp.float32, 'conv_w': _jnp.float32, 'conv_b': _jnp.float32, 'sg_ln_g': _jnp.float32, 'sg_ln_b': _jnp.float32, 'w_spatial': _jnp.float32, 'b_spatial': _jnp.float32, 'mem_norm_g': _jnp.float32, 'w_mem_kv': _jnp.float32, 'mem_q_norm_g': _jnp.float32, 'mem_k_norm_g': _jnp.float32, 'b_merge': _jnp.float32, 'w_branch': _jnp.float32, 'w_out': _jnp.float32}
MOMENT_SCALE = {'norm_g': 1.141609e+01, 'w_in': 1.697949e-01, 'cq_norm_g': 3.969504e-02, 'ckv_norm_g': 1.948595e-01, 'w_uq': 2.331121e-02, 'w_ukv': 3.035517e-02, 'mla_q_norm_g': 1.441121e-01, 'mla_k_norm_g': 1.425051e-01, 'conv_w': 1.868595e+00, 'conv_b': 2.258618e-01, 'sg_ln_g': 3.957576e-01, 'sg_ln_b': 8.769091e-02, 'w_spatial': 1.615563e-01, 'b_spatial': 3.310873e+00, 'mem_norm_g': 2.352596e-02, 'w_mem_kv': 1.812071e-02, 'mem_q_norm_g': 1.466487e-01, 'mem_k_norm_g': 1.452514e-01, 'b_merge': 4.412885e-01, 'w_branch': 1.034384e-01, 'w_out': 2.008986e-01}


def _to_microbatches(a, axis):
    t = _jnp.moveaxis(a, axis, 0)
    t = t.reshape((N_MICROBATCH, t.shape[0] // N_MICROBATCH) + t.shape[1:])
    return _jnp.moveaxis(t, 1, axis + 1)


def setup_inputs(seed: int = 0) -> dict:
    inp = _fwd_setup_inputs(seed)
    key = _jax.random.fold_in(_jax.random.key(seed), 7919)
    shape, _ = _output_shape()
    out = dict(inp)
    out["loss_target"] = _jax.random.normal(_jax.random.fold_in(key, 0), shape, _jnp.float32)
    for i, name in enumerate(TWIN_WEIGHTS):
        w = inp[name].astype(_jnp.float32)
        if MOMENT_SCALE is None:
            s = _jnp.sqrt(_jnp.mean(_jnp.square(w)) + 1e-30)
        else:
            s = MOMENT_SCALE[name]
        km, kv = _jax.random.split(_jax.random.fold_in(key, i + 1))
        out[name] = w
        out["m_" + name] = s * _jax.random.normal(km, w.shape, _jnp.float32)
        out["v_" + name] = (s * s) * _jax.random.uniform(kv, w.shape, _jnp.float32, 0.5, 1.5)
    if N_MICROBATCH > 1:
        for name, axis in PER_EXAMPLE_BATCH_AXIS.items():
            out[name] = _to_microbatches(out[name], axis)
    return {'x': out['x'], 'mem': out['mem'], 'positions': out['positions'], 'norm_g': out['norm_g'], 'w_in': out['w_in'], 'cq_norm_g': out['cq_norm_g'], 'ckv_norm_g': out['ckv_norm_g'], 'w_uq': out['w_uq'], 'w_ukv': out['w_ukv'], 'mla_q_norm_g': out['mla_q_norm_g'], 'mla_k_norm_g': out['mla_k_norm_g'], 'conv_w': out['conv_w'], 'conv_b': out['conv_b'], 'sg_ln_g': out['sg_ln_g'], 'sg_ln_b': out['sg_ln_b'], 'w_spatial': out['w_spatial'], 'b_spatial': out['b_spatial'], 'mem_norm_g': out['mem_norm_g'], 'w_mem_kv': out['w_mem_kv'], 'mem_q_norm_g': out['mem_q_norm_g'], 'mem_k_norm_g': out['mem_k_norm_g'], 'b_merge': out['b_merge'], 'w_branch': out['w_branch'], 'w_out': out['w_out'], 'loss_target': out['loss_target'], 'm_norm_g': out['m_norm_g'], 'm_w_in': out['m_w_in'], 'm_cq_norm_g': out['m_cq_norm_g'], 'm_ckv_norm_g': out['m_ckv_norm_g'], 'm_w_uq': out['m_w_uq'], 'm_w_ukv': out['m_w_ukv'], 'm_mla_q_norm_g': out['m_mla_q_norm_g'], 'm_mla_k_norm_g': out['m_mla_k_norm_g'], 'm_conv_w': out['m_conv_w'], 'm_conv_b': out['m_conv_b'], 'm_sg_ln_g': out['m_sg_ln_g'], 'm_sg_ln_b': out['m_sg_ln_b'], 'm_w_spatial': out['m_w_spatial'], 'm_b_spatial': out['m_b_spatial'], 'm_mem_norm_g': out['m_mem_norm_g'], 'm_w_mem_kv': out['m_w_mem_kv'], 'm_mem_q_norm_g': out['m_mem_q_norm_g'], 'm_mem_k_norm_g': out['m_mem_k_norm_g'], 'm_b_merge': out['m_b_merge'], 'm_w_branch': out['m_w_branch'], 'm_w_out': out['m_w_out'], 'v_norm_g': out['v_norm_g'], 'v_w_in': out['v_w_in'], 'v_cq_norm_g': out['v_cq_norm_g'], 'v_ckv_norm_g': out['v_ckv_norm_g'], 'v_w_uq': out['v_w_uq'], 'v_w_ukv': out['v_w_ukv'], 'v_mla_q_norm_g': out['v_mla_q_norm_g'], 'v_mla_k_norm_g': out['v_mla_k_norm_g'], 'v_conv_w': out['v_conv_w'], 'v_conv_b': out['v_conv_b'], 'v_sg_ln_g': out['v_sg_ln_g'], 'v_sg_ln_b': out['v_sg_ln_b'], 'v_w_spatial': out['v_w_spatial'], 'v_b_spatial': out['v_b_spatial'], 'v_mem_norm_g': out['v_mem_norm_g'], 'v_w_mem_kv': out['v_w_mem_kv'], 'v_mem_q_norm_g': out['v_mem_q_norm_g'], 'v_mem_k_norm_g': out['v_mem_k_norm_g'], 'v_b_merge': out['v_b_merge'], 'v_w_branch': out['v_w_branch'], 'v_w_out': out['v_w_out']}


def _loss(weights, diff, rest, loss_target):
    with _jax.named_scope("forward"):
        args = {**rest, TWIN_DIFF_INPUT: diff, **{k: w.astype(_WEIGHT_DTYPES[k]) for k, w in weights.items()}}
        y = _forward(args)
    with _jax.named_scope("loss_head"):
        err = _jnp.square(y.astype(_jnp.float32) - loss_target)
        return 0.5 * _jnp.sum(_jnp.mean(err, axis=-1)) if err.ndim else 0.5 * err


def _adamw(w, g, m, v):
    m = ADAM_B1 * m + (1.0 - ADAM_B1) * g
    v = ADAM_B2 * v + (1.0 - ADAM_B2) * _jnp.square(g)
    m_hat = m / (1.0 - ADAM_B1 ** ADAM_STEP)
    v_hat = v / (1.0 - ADAM_B2 ** ADAM_STEP)
    delta = -ADAM_LR * (m_hat / (_jnp.sqrt(v_hat) + ADAM_EPS) + ADAM_WD * w)
    return delta, m, v


def reference(x, mem, positions, norm_g, w_in, cq_norm_g, ckv_norm_g, w_uq, w_ukv, mla_q_norm_g, mla_k_norm_g, conv_w, conv_b, sg_ln_g, sg_ln_b, w_spatial, b_spatial, mem_norm_g, w_mem_kv, mem_q_norm_g, mem_k_norm_g, b_merge, w_branch, w_out, loss_target, m_norm_g, m_w_in, m_cq_norm_g, m_ckv_norm_g, m_w_uq, m_w_ukv, m_mla_q_norm_g, m_mla_k_norm_g, m_conv_w, m_conv_b, m_sg_ln_g, m_sg_ln_b, m_w_spatial, m_b_spatial, m_mem_norm_g, m_w_mem_kv, m_mem_q_norm_g, m_mem_k_norm_g, m_b_merge, m_w_branch, m_w_out, v_norm_g, v_w_in, v_cq_norm_g, v_ckv_norm_g, v_w_uq, v_w_ukv, v_mla_q_norm_g, v_mla_k_norm_g, v_conv_w, v_conv_b, v_sg_ln_g, v_sg_ln_b, v_w_spatial, v_b_spatial, v_mem_norm_g, v_w_mem_kv, v_mem_q_norm_g, v_mem_k_norm_g, v_b_merge, v_w_branch, v_w_out):
    given = dict(x=x, mem=mem, positions=positions, norm_g=norm_g, w_in=w_in, cq_norm_g=cq_norm_g, ckv_norm_g=ckv_norm_g, w_uq=w_uq, w_ukv=w_ukv, mla_q_norm_g=mla_q_norm_g, mla_k_norm_g=mla_k_norm_g, conv_w=conv_w, conv_b=conv_b, sg_ln_g=sg_ln_g, sg_ln_b=sg_ln_b, w_spatial=w_spatial, b_spatial=b_spatial, mem_norm_g=mem_norm_g, w_mem_kv=w_mem_kv, mem_q_norm_g=mem_q_norm_g, mem_k_norm_g=mem_k_norm_g, b_merge=b_merge, w_branch=w_branch, w_out=w_out, loss_target=loss_target, m_norm_g=m_norm_g, m_w_in=m_w_in, m_cq_norm_g=m_cq_norm_g, m_ckv_norm_g=m_ckv_norm_g, m_w_uq=m_w_uq, m_w_ukv=m_w_ukv, m_mla_q_norm_g=m_mla_q_norm_g, m_mla_k_norm_g=m_mla_k_norm_g, m_conv_w=m_conv_w, m_conv_b=m_conv_b, m_sg_ln_g=m_sg_ln_g, m_sg_ln_b=m_sg_ln_b, m_w_spatial=m_w_spatial, m_b_spatial=m_b_spatial, m_mem_norm_g=m_mem_norm_g, m_w_mem_kv=m_w_mem_kv, m_mem_q_norm_g=m_mem_q_norm_g, m_mem_k_norm_g=m_mem_k_norm_g, m_b_merge=m_b_merge, m_w_branch=m_w_branch, m_w_out=m_w_out, v_norm_g=v_norm_g, v_w_in=v_w_in, v_cq_norm_g=v_cq_norm_g, v_ckv_norm_g=v_ckv_norm_g, v_w_uq=v_w_uq, v_w_ukv=v_w_ukv, v_mla_q_norm_g=v_mla_q_norm_g, v_mla_k_norm_g=v_mla_k_norm_g, v_conv_w=v_conv_w, v_conv_b=v_conv_b, v_sg_ln_g=v_sg_ln_g, v_sg_ln_b=v_sg_ln_b, v_w_spatial=v_w_spatial, v_b_spatial=v_b_spatial, v_mem_norm_g=v_mem_norm_g, v_w_mem_kv=v_w_mem_kv, v_mem_q_norm_g=v_mem_q_norm_g, v_mem_k_norm_g=v_mem_k_norm_g, v_b_merge=v_b_merge, v_w_branch=v_w_branch, v_w_out=v_w_out)
    weights = {n: given[n] for n in TWIN_WEIGHTS}
    shared = {n: given[n] for n in SHARED_INPUTS}
    per_example = {n: given[n] for n in ['x', 'mem', 'positions']}
    grad_fn = _jax.value_and_grad(_loss, argnums=(0, 1))

    def one_microbatch(ex, loss_target):
        ex = dict(ex)
        diff = ex.pop(TWIN_DIFF_INPUT)
        return grad_fn(weights, diff, {**shared, **ex}, loss_target)

    if N_MICROBATCH == 1:
        loss, (grad_w, grad_x) = one_microbatch(per_example, given["loss_target"])
    else:
        def body(carry, xs):
            loss_sum, grad_sum = carry
            l_k, (gw_k, gx_k) = one_microbatch(xs[0], xs[1])
            with _jax.named_scope("update"):
                return (loss_sum + l_k, _jax.tree.map(_jnp.add, grad_sum, gw_k)), gx_k

        init = (_jnp.zeros((), _jnp.float32), _jax.tree.map(_jnp.zeros_like, weights))
        (loss, grad_w), grad_x = _jax.lax.scan(body, init, (per_example, given["loss_target"]))
    with _jax.named_scope("update"):
        delta_w, new_m, new_v = {}, {}, {}
        for n in TWIN_WEIGHTS:
            delta_w[n], new_m[n], new_v[n] = _adamw(weights[n], grad_w[n], given["m_" + n], given["v_" + n])
    return (loss, grad_x, *[grad_w[n] for n in TWIN_WEIGHTS], *[delta_w[n] for n in TWIN_WEIGHTS],
            *[new_m[n] for n in TWIN_WEIGHTS], *[new_v[n] for n in TWIN_WEIGHTS])
```

```python
import functools
import math

import jax
import jax.numpy as jnp
from jax import lax
from jax.experimental import pallas as pl
from jax.experimental.pallas import tpu as pltpu

F32 = jnp.float32
MM = jnp.bfloat16

D = 1024
DEPTH = 2
EPS = 1e-6
H = 8
NOPE = 64
ROPE = 32
QKH = 96
VH = 64
QL = 256
KVL = 128
ROPE_THETA = 10000.0
CW = 512
SGW = 512
SGG = 4
SGC = 128
MH = 4
MHD = 128
NB = 4
BW = 512
NEG_INF = -1e30
LANES = 128

R_CQ, R_CKV, R_KR, R_CV, R_SGI, R_MQ, R_SG, R_ML, R_END = 0, 256, 384, 416, 1952, 2976, 3488, 5536, 9632
OFF_ML, OFF_SG, OFF_CV, OFF_SGI, OFF_MQ, OFF_CQ, OFF_CKV, OFF_KR, NP = 0, 4096, 6144, 7680, 8704, 9216, 9472, 9600, 9728

ADAM_LR = 0.001
ADAM_B1 = 0.9
ADAM_B2 = 0.999
ADAM_EPS = 1e-08
ADAM_WD = 0.01
ADAM_STEP = 10

VMEM_LIMIT = 56 * 1024 * 1024
PACK_W = 512
MESH_ID = pl.DeviceIdType.MESH


def _cparams(n_axes):
    return pltpu.CompilerParams(dimension_semantics=("arbitrary",) * n_axes, vmem_limit_bytes=VMEM_LIMIT)


def _bs(shape, imap):
    return pl.BlockSpec(shape, imap)


@jax.custom_vjp
def _mm(a, b):
    return jnp.dot(a.astype(MM), b.astype(MM), preferred_element_type=F32)


def _mm_fwd(a, b):
    return _mm(a, b), (a, b)


def _mm_bwd(res, g):
    a, b = res
    gm = g.astype(MM)
    da = lax.dot_general(gm, b.astype(MM), (((1,), (1,)), ((), ())), preferred_element_type=F32)
    db = lax.dot_general(a.astype(MM), gm, (((0,), (0,)), ((), ())), preferred_element_type=F32)
    return da.astype(a.dtype), db.astype(b.dtype)


_mm.defvjp(_mm_fwd, _mm_bwd)


@jax.custom_vjp
def _mm_nt(a, b):
    return lax.dot_general(a.astype(MM), b.astype(MM), (((1,), (1,)), ((), ())), preferred_element_type=F32)


def _mm_nt_fwd(a, b):
    return _mm_nt(a, b), (a, b)


def _mm_nt_bwd(res, g):
    a, b = res
    gm = g.astype(MM)
    da = jnp.dot(gm, b.astype(MM), preferred_element_type=F32)
    db = lax.dot_general(gm, a.astype(MM), (((0,), (0,)), ((), ())), preferred_element_type=F32)
    return da.astype(a.dtype), db.astype(b.dtype)


_mm_nt.defvjp(_mm_nt_fwd, _mm_nt_bwd)


@functools.partial(jax.custom_vjp, nondiff_argnums=(1,))
def _lane_roll(x, shift):
    return pltpu.roll(x, shift, 1)


def _lane_roll_fwd(x, shift):
    return pltpu.roll(x, shift, 1), None


def _lane_roll_bwd(shift, _, g):
    return (pltpu.roll(g, (LANES - shift) % LANES, 1),)


_lane_roll.defvjp(_lane_roll_fwd, _lane_roll_bwd)


def _rms_n(x, g, n):
    ms = jnp.sum(x * x, axis=-1, keepdims=True) * (1.0 / n)
    return x * lax.rsqrt(ms + EPS) * g


def _softmax(s):
    m = jnp.max(s, axis=-1, keepdims=True)
    e = jnp.exp(s - m)
    return e / jnp.sum(e, axis=-1, keepdims=True)


def _rope(t, cos_t, sin_a, sin_b):
    return t * cos_t + _lane_roll(t, LANES - 16) * sin_a + _lane_roll(t, 16) * sin_b


def _mla_prep_fn(cq, ckv, kr, cos_t, sin_a, sin_b, cq_g, ckv_g, qg, kg, wuq, wkn, wv):
    cqn = _rms_n(cq, cq_g, QL)
    ckvn = _rms_n(ckv, ckv_g, KVL)
    lane = lax.broadcasted_iota(jnp.int32, kr.shape, 1)
    krm = jnp.where((lane >= NOPE) & (lane < QKH), kr, 0.0)
    qs, ks = [], []
    for h in range(H):
        qh = _rms_n(_mm(cqn, wuq[h]), qg, QKH)
        qs.append(_rope(qh, cos_t, sin_a, sin_b))
        kh = _rms_n(_mm(ckvn, wkn[h]) + krm, kg, QKH)
        ks.append(_rope(kh, cos_t, sin_a, sin_b))
    return jnp.concatenate(qs, axis=-1), jnp.concatenate(ks, axis=-1), _mm(ckvn, wv)


def _attn_pair_fn(q2, k2, v2, sg, row0):
    tq, s_len = q2.shape[0], k2.shape[0]
    rows = row0 + lax.broadcasted_iota(jnp.int32, (tq, s_len), 0)
    cols = lax.broadcasted_iota(jnp.int32, (tq, s_len), 1)
    mask = cols <= rows
    vlane = lax.broadcasted_iota(jnp.int32, (s_len, LANES), 1)
    o = jnp.zeros((tq, LANES), F32)
    for e in range(2):
        sl = slice(LANES * e, LANES * (e + 1))
        s = _mm_nt(q2[:, sl], k2[:, sl]) * (QKH ** -0.5)
        p = _softmax(jnp.where(mask, s, NEG_INF))
        ve = jnp.where((vlane >= VH * e) & (vlane < VH * (e + 1)), v2[:, sl], 0.0)
        o = o + _mm(p, ve)
    return o * jax.nn.silu(sg)


def _sg_fn(u, v, sgc, ln_g, ln_b, ws, bs):
    mu = jnp.mean(v, axis=-1, keepdims=True)
    xc = v - mu
    vn = xc * lax.rsqrt(jnp.mean(xc * xc, axis=-1, keepdims=True) + EPS) * ln_g + ln_b
    r = lax.broadcasted_iota(jnp.int32, (SGC, SGC), 0)
    c = lax.broadcasted_iota(jnp.int32, (SGC, SGC), 1)
    wt = [jnp.where(r >= c, w, 0.0) for w in ws]
    row_blocks = []
    for ch in range(u.shape[0] // SGC):
        col_blocks = []
        for g in range(SGG):
            blk = vn[SGC * ch:SGC * (ch + 1), LANES * g:LANES * (g + 1)]
            col_blocks.append(_mm(wt[g], blk) + bs[g])
        row_blocks.append(jnp.concatenate(col_blocks, axis=-1))
    mixed = jnp.concatenate(row_blocks, axis=0)
    return (u * mixed) * jax.nn.silu(sgc)


def _memkv_fn(mem, mem_g, wm, kg):
    kv = _mm(_rms_n(mem, mem_g, D), wm)
    ks = [_rms_n(kv[:, MHD * h:MHD * (h + 1)], kg, MHD) for h in range(MH)]
    return jnp.concatenate(ks, axis=-1), kv[:, MH * MHD:]


def _mem_fn(mq, sgd, k, v, qg):
    outs = []
    for h in range(MH):
        sl = slice(MHD * h, MHD * (h + 1))
        qh = _rms_n(mq[:, sl], qg, MHD)
        p = _softmax(_mm_nt(qh, k[:, sl]) * (MHD ** -0.5))
        outs.append(_mm(p, v[:, sl]))
    return jnp.concatenate(outs, axis=-1) * jax.nn.silu(sgd)


def _merge_fn(ys, logits, bm, wb, wo):
    merged = None
    for n in range(NB):
        z = _mm(ys[n], wb[n])
        gate = jax.nn.sigmoid(logits[:, D * n:D * (n + 1)] + bm[n])
        merged = gate * z if merged is None else merged + gate * z
    return _mm(merged, wo)


def _proj_call(x, g, w):
    s_len = x.shape[0]
    tm, tn = min(s_len, 1024), 512

    def body(x_ref, g_ref, w_ref, p_ref, h_ref):
        @pl.when(pl.program_id(1) == 0)
        def _():
            h_ref[...] = _rms_n(x_ref[...], g_ref[...], D).astype(h_ref.dtype)
        p_ref[...] = jnp.dot(h_ref[...], w_ref[...], preferred_element_type=F32)

    return pl.pallas_call(
        body, grid=(s_len // tm, NP // tn),
        in_specs=[_bs((tm, D), lambda i, j: (i, 0)), _bs((1, D), lambda i, j: (0, 0)), _bs((D, tn), lambda i, j: (0, j))],
        out_specs=[_bs((tm, tn), lambda i, j: (i, j)), _bs((tm, D), lambda i, j: (i, 0))],
        out_shape=[jax.ShapeDtypeStruct((s_len, NP), F32), jax.ShapeDtypeStruct((s_len, D), MM)],
        name="proj", compiler_params=_cparams(2))(x, g, w)


def _rope_tables(pos):
    half = ROPE // 2
    inv_freq = ROPE_THETA ** (-jnp.arange(half, dtype=F32) / half)
    ang = pos.astype(F32)[:, None] * inv_freq
    cos, sin = jnp.cos(ang), jnp.sin(ang)
    s_len = pos.shape[0]
    z = lambda n: jnp.zeros((s_len, n), F32)
    cos_t = jnp.concatenate([jnp.ones((s_len, NOPE), F32), cos, cos, z(LANES - QKH)], axis=1)
    sin_a = jnp.concatenate([z(NOPE), -sin, z(LANES - NOPE - half)], axis=1)
    sin_b = jnp.concatenate([z(NOPE + half), sin, z(LANES - QKH)], axis=1)
    return cos_t, sin_a, sin_b


def _mla_prep_specs(tm):
    row = lambda w, off: _bs((tm, w), lambda i: (i, off // w))
    full2 = lambda a, b: _bs((a, b), lambda i: (0, 0))
    full3 = lambda a, b, c: _bs((a, b, c), lambda i: (0, 0, 0))
    tab = _bs((tm, LANES), lambda i: (i, 0))
    return [row(QL, OFF_CQ), row(KVL, OFF_CKV), row(LANES, OFF_KR), tab, tab, tab,
            full2(1, QL), full2(1, KVL), full2(1, LANES), full2(1, LANES),
            full3(H, QL, LANES), full3(H, KVL, LANES), full2(KVL, H * LANES)]


def _mla_prep_args(body_refs, wdtype=None):
    (cq, ckv, kr, ct, sa, sb, cqg, ckvg, qg, kg, wuq, wkn, wv) = body_refs
    cast = (lambda a: a) if wdtype is None else (lambda a: a.astype(wdtype))
    return (cq[...], ckv[...], kr[...], ct[...], sa[...], sb[...], cqg[...], ckvg[...], qg[...], kg[...],
            [cast(wuq[h]) for h in range(H)], [cast(wkn[h]) for h in range(H)], cast(wv[...]))


def _mla_prep_call(proj, tabs, cq_g, ckv_g, qg, kg, wuq, wkn, wv):
    s_len = proj.shape[0]
    tm = min(s_len, 256)

    def body(*refs):
        q_ref, k_ref, v_ref = refs[13:]
        q, k, v = _mla_prep_fn(*_mla_prep_args(refs[:13]))
        q_ref[...] = q.astype(q_ref.dtype)
        k_ref[...] = k.astype(k_ref.dtype)
        v_ref[...] = v.astype(v_ref.dtype)

    out = _bs((tm, H * LANES), lambda i: (i, 0))
    return pl.pallas_call(
        body, grid=(s_len // tm,), in_specs=_mla_prep_specs(tm), out_specs=[out, out, out],
        out_shape=[jax.ShapeDtypeStruct((s_len, H * LANES), MM)] * 3,
        name="mla_prep", compiler_params=_cparams(1))(proj, proj, proj, *tabs, cq_g, ckv_g, qg, kg, wuq, wkn, wv)


def _mla_prep_bwd_call(proj, tabs, cq_g, ckv_g, qg, kg, wuq, wkn, wv, dq, dk, dv):
    s_len = proj.shape[0]
    tm = min(s_len, 256)

    def body(*refs):
        dq_ref, dk_ref, dv_ref = refs[13:16]
        dcq_ref, dckv_ref, dkr_ref, dcqg_ref, dckvg_ref, dqg_ref, dkg_ref, dwuq_ref, dwkn_ref, dwv_ref = refs[16:]
        _, vjp = jax.vjp(_mla_prep_fn, *_mla_prep_args(refs[:13], F32))
        (dcq, dckv, dkr, _, _, _, dcqg, dckvg, dqg, dkg, dwuq, dwkn, dwv) = vjp((dq_ref[...], dk_ref[...], dv_ref[...]))
        dcq_ref[...] = dcq.astype(dcq_ref.dtype)
        dckv_ref[...] = dckv.astype(dckv_ref.dtype)
        dkr_ref[...] = dkr.astype(dkr_ref.dtype)

        @pl.when(pl.program_id(0) == 0)
        def _():
            for r in (dcqg_ref, dckvg_ref, dqg_ref, dkg_ref, dwuq_ref, dwkn_ref, dwv_ref):
                r[...] = jnp.zeros_like(r)
        dcqg_ref[...] += dcqg
        dckvg_ref[...] += dckvg
        dqg_ref[...] += dqg
        dkg_ref[...] += dkg
        for h in range(H):
            dwuq_ref[h] += dwuq[h]
            dwkn_ref[h] += dwkn[h]
        dwv_ref[...] += dwv

    big = _bs((tm, H * LANES), lambda i: (i, 0))
    row = lambda w: _bs((tm, w), lambda i: (i, 0))
    full2 = lambda a, b: _bs((a, b), lambda i: (0, 0))
    full3 = lambda a, b, c: _bs((a, b, c), lambda i: (0, 0, 0))
    sd = jax.ShapeDtypeStruct
    return pl.pallas_call(
        body, grid=(s_len // tm,), in_specs=_mla_prep_specs(tm) + [big, big, big],
        out_specs=[row(QL), row(KVL), row(LANES), full2(1, QL), full2(1, KVL), full2(1, LANES), full2(1, LANES),
                   full3(H, QL, LANES), full3(H, KVL, LANES), full2(KVL, H * LANES)],
        out_shape=[sd((s_len, QL), MM), sd((s_len, KVL), MM), sd((s_len, LANES), MM), sd((1, QL), F32), sd((1, KVL), F32),
                   sd((1, LANES), F32), sd((1, LANES), F32), sd((H, QL, LANES), F32), sd((H, KVL, LANES), F32),
                   sd((KVL, H * LANES), F32)],
        name="mla_prep_bwd", compiler_params=_cparams(1))(proj, proj, proj, *tabs, cq_g, ckv_g, qg, kg, wuq, wkn, wv, dq, dk, dv)


def _attn_specs(s_len, tq):
    pair = 2 * LANES
    return [_bs((tq, pair), lambda p, i: (i, p)), _bs((s_len, pair), lambda p, i: (0, p)), _bs((s_len, pair), lambda p, i: (0, p)),
            _bs((tq, LANES), lambda p, i: (i, OFF_SG // LANES + p))]


def _attn_call(q, k, v, proj):
    s_len = q.shape[0]
    tq = min(s_len, 256)

    def body(q_ref, k_ref, v_ref, sg_ref, y_ref):
        row0 = pl.program_id(1) * tq
        y_ref[...] = _attn_pair_fn(q_ref[...], k_ref[...], v_ref[...], sg_ref[...], row0).astype(y_ref.dtype)

    return pl.pallas_call(
        body, grid=(H // 2, s_len // tq), in_specs=_attn_specs(s_len, tq),
        out_specs=_bs((tq, LANES), lambda p, i: (i, p)), out_shape=jax.ShapeDtypeStruct((s_len, BW), MM),
        name="attn", compiler_params=_cparams(2))(q, k, v, proj)


def _attn_bwd_call(q, k, v, proj, dys):
    s_len = q.shape[0]
    tq = min(s_len, 256)
    pair = 2 * LANES

    def body(q_ref, k_ref, v_ref, sg_ref, dy_ref, dq_ref, dk_ref, dv_ref, dsg_ref):
        i = pl.program_id(1)
        fn = functools.partial(_attn_pair_fn, row0=i * tq)
        _, vjp = jax.vjp(fn, q_ref[...].astype(F32), k_ref[...].astype(F32), v_ref[...].astype(F32), sg_ref[...])
        dq, dk, dv, dsg = vjp(dy_ref[...])
        dq_ref[...] = dq
        dsg_ref[...] = dsg.astype(dsg_ref.dtype)

        @pl.when(i == 0)
        def _():
            dk_ref[...] = jnp.zeros_like(dk_ref)
            dv_ref[...] = jnp.zeros_like(dv_ref)
        dk_ref[...] += dk
        dv_ref[...] += dv

    sd = jax.ShapeDtypeStruct
    return pl.pallas_call(
        body, grid=(H // 2, s_len // tq),
        in_specs=_attn_specs(s_len, tq) + [_bs((tq, LANES), lambda p, i: (i, p))],
        out_specs=[_bs((tq, pair), lambda p, i: (i, p)), _bs((s_len, pair), lambda p, i: (0, p)),
                   _bs((s_len, pair), lambda p, i: (0, p)), _bs((tq, LANES), lambda p, i: (i, p))],
        out_shape=[sd((s_len, H * LANES), F32), sd((s_len, H * LANES), F32), sd((s_len, H * LANES), F32), sd((s_len, BW), MM)],
        name="attn_bwd", compiler_params=_cparams(2))(q, k, v, proj, dys)


def _shift_down(a, n):
    r = lax.broadcasted_iota(jnp.int32, a.shape, 0)
    return jnp.where(r >= n, pltpu.roll(a, n, 0), 0.0)


def _shift_up(a, n):
    s_len = a.shape[0]
    r = lax.broadcasted_iota(jnp.int32, a.shape, 0)
    return jnp.where(r < s_len - n, pltpu.roll(a, s_len - n, 0), 0.0)


def _conv_specs(s_len):
    col = lambda off: _bs((s_len, LANES), lambda j: (0, off // LANES + j))
    return [col(OFF_CV), col(OFF_CV + CW), col(OFF_CV + 2 * CW), col(OFF_SG + BW),
            _bs((3, LANES), lambda j: (0, j)), _bs((1, LANES), lambda j: (0, j))]


def _conv_call(proj, cw, cb):
    s_len = proj.shape[0]

    def body(bg_ref, cg_ref, xi_ref, sg_ref, w_ref, b_ref, y_ref):
        z = cg_ref[...] * xi_ref[...]
        y = b_ref[...] + w_ref[0:1, :] * _shift_down(z, 2)
        y = y + w_ref[1:2, :] * _shift_down(z, 1)
        y = y + w_ref[2:3, :] * z
        y_ref[...] = ((bg_ref[...] * y) * jax.nn.silu(sg_ref[...])).astype(y_ref.dtype)

    return pl.pallas_call(
        body, grid=(CW // LANES,), in_specs=_conv_specs(s_len), out_specs=_bs((s_len, LANES), lambda j: (0, j)),
        out_shape=jax.ShapeDtypeStruct((s_len, CW), MM), name="conv", compiler_params=_cparams(1))(proj, proj, proj, proj, cw, cb)


def _conv_bwd_call(proj, cw, cb, dys):
    s_len = proj.shape[0]

    def body(bg_ref, cg_ref, xi_ref, sg_ref, w_ref, b_ref, dys_ref, dbg_ref, dcg_ref, dxi_ref, dsg_ref, dw_ref, db_ref):
        bg, cg, xi, sg = bg_ref[...], cg_ref[...], xi_ref[...], sg_ref[...]
        w0, w1, w2 = w_ref[0:1, :], w_ref[1:2, :], w_ref[2:3, :]
        z = cg * xi
        z1, z2 = _shift_down(z, 1), _shift_down(z, 2)
        y = b_ref[...] + w0 * z2
        y = y + w1 * z1
        y = y + w2 * z
        yb = bg * y
        sig = jax.nn.sigmoid(sg)
        silu = sg * sig
        dys_v = dys_ref[...]
        dsg_ref[...] = (dys_v * yb * (sig * (1.0 + sg * (1.0 - sig)))).astype(dsg_ref.dtype)
        dyb = dys_v * silu
        dbg_ref[...] = (dyb * y).astype(dbg_ref.dtype)
        dy = dyb * bg
        db_ref[...] = jnp.sum(dy, axis=0, keepdims=True)
        dw_ref[0:1, :] = jnp.sum(dy * z2, axis=0, keepdims=True)
        dw_ref[1:2, :] = jnp.sum(dy * z1, axis=0, keepdims=True)
        dw_ref[2:3, :] = jnp.sum(dy * z, axis=0, keepdims=True)
        dz = w2 * dy + w1 * _shift_up(dy, 1) + w0 * _shift_up(dy, 2)
        dcg_ref[...] = (dz * xi).astype(dcg_ref.dtype)
        dxi_ref[...] = (dz * cg).astype(dxi_ref.dtype)

    col = _bs((s_len, LANES), lambda j: (0, j))
    sd = jax.ShapeDtypeStruct
    return pl.pallas_call(
        body, grid=(CW // LANES,), in_specs=_conv_specs(s_len) + [col],
        out_specs=[col, col, col, col, _bs((3, LANES), lambda j: (0, j)), _bs((1, LANES), lambda j: (0, j))],
        out_shape=[sd((s_len, CW), MM)] * 4 + [sd((3, CW), F32), sd((1, CW), F32)],
        name="conv_bwd", compiler_params=_cparams(1))(proj, proj, proj, proj, cw, cb, dys)


def _sg_specs(tm):
    row = lambda off: _bs((tm, SGW), lambda i: (i, off // SGW))
    return [row(OFF_SGI), row(OFF_SGI + SGW), row(OFF_SG + 2 * BW), _bs((1, SGW), lambda i: (0, 0)), _bs((1, SGW), lambda i: (0, 0)),
            _bs((SGG, SGC, SGC), lambda i: (0, 0, 0)), _bs((SGG, SGC, 1), lambda i: (0, 0, 0))]


def _sg_args(refs):
    u, v, sg, lg, lb, ws, bs = refs
    return (u[...], v[...], sg[...], lg[...], lb[...], [ws[g] for g in range(SGG)], [bs[g] for g in range(SGG)])


def _sg_call(proj, ln_g, ln_b, ws, bs):
    s_len = proj.shape[0]
    tm = min(s_len, 256)

    def body(*refs):
        refs[7][...] = _sg_fn(*_sg_args(refs[:7])).astype(refs[7].dtype)

    return pl.pallas_call(
        body, grid=(s_len // tm,), in_specs=_sg_specs(tm), out_specs=_bs((tm, SGW), lambda i: (i, 0)),
        out_shape=jax.ShapeDtypeStruct((s_len, SGW), MM), name="sgmlp", compiler_params=_cparams(1))(proj, proj, proj, ln_g, ln_b, ws, bs)


def _sg_bwd_call(proj, ln_g, ln_b, ws, bs, dys):
    s_len = proj.shape[0]
    tm = min(s_len, 256)

    def body(*refs):
        dys_ref = refs[7]
        du_ref, dv_ref, dsg_ref, dlg_ref, dlb_ref, dws_ref, dbs_ref = refs[8:]
        _, vjp = jax.vjp(_sg_fn, *_sg_args(refs[:7]))
        du, dv, dsg, dlg, dlb, dws, dbs = vjp(dys_ref[...])
        du_ref[...] = du.astype(du_ref.dtype)
        dv_ref[...] = dv.astype(dv_ref.dtype)
        dsg_ref[...] = dsg.astype(dsg_ref.dtype)

        @pl.when(pl.program_id(0) == 0)
        def _():
            for r in (dlg_ref, dlb_ref, dws_ref, dbs_ref):
                r[...] = jnp.zeros_like(r)
        dlg_ref[...] += dlg
        dlb_ref[...] += dlb
        for g in range(SGG):
            dws_ref[g] += dws[g]
            dbs_ref[g] += dbs[g]

    row = _bs((tm, SGW), lambda i: (i, 0))
    sd = jax.ShapeDtypeStruct
    return pl.pallas_call(
        body, grid=(s_len // tm,), in_specs=_sg_specs(tm) + [row],
        out_specs=[row, row, row, _bs((1, SGW), lambda i: (0, 0)), _bs((1, SGW), lambda i: (0, 0)),
                   _bs((SGG, SGC, SGC), lambda i: (0, 0, 0)), _bs((SGG, SGC, 1), lambda i: (0, 0, 0))],
        out_shape=[sd((s_len, SGW), MM)] * 3 + [sd((1, SGW), F32), sd((1, SGW), F32), sd((SGG, SGC, SGC), F32), sd((SGG, SGC, 1), F32)],
        name="sgmlp_bwd", compiler_params=_cparams(1))(proj, proj, proj, ln_g, ln_b, ws, bs, dys)


def _memkv_call(mem, mem_g, wm, kg):
    m_len = mem.shape[0]

    def body(mem_ref, g_ref, w_ref, kg_ref, k_ref, v_ref):
        k, v = _memkv_fn(mem_ref[...], g_ref[...], w_ref[...], kg_ref[...])
        k_ref[...] = k.astype(k_ref.dtype)
        v_ref[...] = v.astype(v_ref.dtype)

    return pl.pallas_call(body, out_shape=[jax.ShapeDtypeStruct((m_len, MH * MHD), MM)] * 2, name="memkv",
                          compiler_params=pltpu.CompilerParams(vmem_limit_bytes=VMEM_LIMIT))(mem, mem_g, wm, kg)


def _memkv_bwd_call(mem, mem_g, wm, kg, dk, dv):
    def body(mem_ref, g_ref, w_ref, kg_ref, dk_ref, dv_ref, dg_ref, dw_ref, dkg_ref):
        _, vjp = jax.vjp(_memkv_fn, mem_ref[...], g_ref[...], w_ref[...].astype(F32), kg_ref[...])
        _, dg, dw, dkg = vjp((dk_ref[...], dv_ref[...]))
        dg_ref[...] = dg
        dw_ref[...] = dw
        dkg_ref[...] = dkg

    sd = jax.ShapeDtypeStruct
    return pl.pallas_call(body, out_shape=[sd((1, D), F32), sd((D, 2 * MH * MHD), F32), sd((1, MHD), F32)], name="memkv_bwd",
                          compiler_params=pltpu.CompilerParams(vmem_limit_bytes=VMEM_LIMIT))(mem, mem_g, wm, kg, dk, dv)


def _mem_specs(tm, m_len):
    w = MH * MHD
    return [_bs((tm, w), lambda i: (i, OFF_MQ // w)), _bs((tm, BW), lambda i: (i, (OFF_SG + 3 * BW) // BW)),
            _bs((m_len, w), lambda i: (0, 0)), _bs((m_len, w), lambda i: (0, 0)), _bs((1, MHD), lambda i: (0, 0))]


def _mem_call(proj, k, v, qg):
    s_len, m_len = proj.shape[0], k.shape[0]
    tm = min(s_len, 256)

    def body(mq_ref, sg_ref, k_ref, v_ref, qg_ref, y_ref):
        y_ref[...] = _mem_fn(mq_ref[...], sg_ref[...], k_ref[...], v_ref[...], qg_ref[...]).astype(y_ref.dtype)

    return pl.pallas_call(
        body, grid=(s_len // tm,), in_specs=_mem_specs(tm, m_len), out_specs=_bs((tm, BW), lambda i: (i, 0)),
        out_shape=jax.ShapeDtypeStruct((s_len, BW), MM), name="memattn", compiler_params=_cparams(1))(proj, proj, k, v, qg)


def _mem_bwd_call(proj, k, v, qg, dys):
    s_len, m_len = proj.shape[0], k.shape[0]
    tm = min(s_len, 256)
    w = MH * MHD

    def body(mq_ref, sg_ref, k_ref, v_ref, qg_ref, dys_ref, dmq_ref, dsg_ref, dk_ref, dv_ref, dqg_ref):
        _, vjp = jax.vjp(_mem_fn, mq_ref[...], sg_ref[...], k_ref[...].astype(F32), v_ref[...].astype(F32), qg_ref[...])
        dmq, dsg, dk, dv, dqg = vjp(dys_ref[...])
        dmq_ref[...] = dmq.astype(dmq_ref.dtype)
        dsg_ref[...] = dsg.astype(dsg_ref.dtype)

        @pl.when(pl.program_id(0) == 0)
        def _():
            for r in (dk_ref, dv_ref, dqg_ref):
                r[...] = jnp.zeros_like(r)
        dk_ref[...] += dk
        dv_ref[...] += dv
        dqg_ref[...] += dqg

    row = _bs((tm, BW), lambda i: (i, 0))
    kv = _bs((m_len, w), lambda i: (0, 0))
    sd = jax.ShapeDtypeStruct
    return pl.pallas_call(
        body, grid=(s_len // tm,), in_specs=_mem_specs(tm, m_len) + [row],
        out_specs=[row, row, kv, kv, _bs((1, MHD), lambda i: (0, 0))],
        out_shape=[sd((s_len, w), MM), sd((s_len, BW), MM), sd((m_len, w), F32), sd((m_len, w), F32), sd((1, MHD), F32)],
        name="memattn_bwd", compiler_params=_cparams(1))(proj, proj, k, v, qg, dys)


def _merge_specs(tm):
    row = _bs((tm, BW), lambda i: (i, 0))
    return [row, row, row, row, _bs((tm, NB * D), lambda i: (i, OFF_ML // (NB * D))), _bs((NB, D), lambda i: (0, 0)),
            _bs((NB, BW, D), lambda i: (0, 0, 0)), _bs((D, D), lambda i: (0, 0))]


def _merge_call(ys, proj, bm, wb, wo, x):
    s_len = proj.shape[0]
    tm = min(s_len, 256)

    def body(ya, yb, yc, yd, lg_ref, bm_ref, wb_ref, wo_ref, x_ref, o_ref):
        out = _merge_fn([r[...] for r in (ya, yb, yc, yd)], lg_ref[...], [bm_ref[n:n + 1, :] for n in range(NB)],
                        [wb_ref[n] for n in range(NB)], wo_ref[...])
        o_ref[...] = x_ref[...] + out

    xrow = _bs((tm, D), lambda i: (i, 0))
    return pl.pallas_call(
        body, grid=(s_len // tm,), in_specs=_merge_specs(tm) + [xrow], out_specs=xrow,
        out_shape=jax.ShapeDtypeStruct((s_len, D), F32), name="merge", compiler_params=_cparams(1))(*ys, proj, bm, wb, wo, x)


def _merge_bwd_call(ys, proj, bm, wb, wo, dout):
    s_len = proj.shape[0]
    tm = min(s_len, 256)

    def body(ya, yb, yc, yd, lg_ref, bm_ref, wb_ref, wo_ref, do_ref, dya, dyb, dyc, dyd, dlg_ref, dbm_ref, dwb_ref, dwo_ref):
        fn = lambda ys_, lg_, bm_, wb_, wo_: _merge_fn(ys_, lg_, bm_, wb_, wo_)
        _, vjp = jax.vjp(fn, [r[...].astype(F32) for r in (ya, yb, yc, yd)], lg_ref[...], [bm_ref[n:n + 1, :] for n in range(NB)],
                         [wb_ref[n].astype(F32) for n in range(NB)], wo_ref[...].astype(F32))
        dys, dlg, dbm, dwb, dwo = vjp(do_ref[...])
        for r, d in zip((dya, dyb, dyc, dyd), dys):
            r[...] = d
        dlg_ref[...] = dlg.astype(dlg_ref.dtype)

        @pl.when(pl.program_id(0) == 0)
        def _():
            for r in (dbm_ref, dwb_ref, dwo_ref):
                r[...] = jnp.zeros_like(r)
        for n in range(NB):
            dbm_ref[n:n + 1, :] += dbm[n]
            dwb_ref[n] += dwb[n]
        dwo_ref[...] += dwo

    row = _bs((tm, BW), lambda i: (i, 0))
    sd = jax.ShapeDtypeStruct
    return pl.pallas_call(
        body, grid=(s_len // tm,), in_specs=_merge_specs(tm) + [_bs((tm, D), lambda i: (i, 0))],
        out_specs=[row, row, row, row, _bs((tm, NB * D), lambda i: (i, 0)), _bs((NB, D), lambda i: (0, 0)),
                   _bs((NB, BW, D), lambda i: (0, 0, 0)), _bs((D, D), lambda i: (0, 0))],
        out_shape=[sd((s_len, BW), F32)] * 4 + [sd((s_len, NB * D), MM), sd((NB, D), F32), sd((NB, BW, D), F32), sd((D, D), F32)],
        name="merge_bwd", compiler_params=_cparams(1))(*ys, proj, bm, wb, wo, dout)


def _dh_call(dproj, w, x, g, dout):
    s_len = x.shape[0]
    tm, tk = min(s_len, 512), NP // 4

    def body(dp_ref, w_ref, x_ref, g_ref, do_ref, dx_ref, dg_ref, acc_ref):
        i, k = pl.program_id(0), pl.program_id(1)

        @pl.when(k == 0)
        def _():
            acc_ref[...] = jnp.zeros_like(acc_ref)
        acc_ref[...] += lax.dot_general(dp_ref[...], w_ref[...], (((1,), (1,)), ((), ())), preferred_element_type=F32)

        @pl.when(k == pl.num_programs(1) - 1)
        def _():
            _, vjp = jax.vjp(lambda x_, g_: _rms_n(x_, g_, D), x_ref[...], g_ref[...])
            dxr, dgr = vjp(acc_ref[...])
            dx_ref[...] = do_ref[...] + dxr

            @pl.when(i == 0)
            def _():
                dg_ref[...] = jnp.zeros_like(dg_ref)
            dg_ref[...] += dgr

    row = _bs((tm, D), lambda i, k: (i, 0))
    return pl.pallas_call(
        body, grid=(s_len // tm, NP // tk),
        in_specs=[_bs((tm, tk), lambda i, k: (i, k)), _bs((D, tk), lambda i, k: (0, k)), row, _bs((1, D), lambda i, k: (0, 0)), row],
        out_specs=[row, _bs((1, D), lambda i, k: (0, 0))],
        out_shape=[jax.ShapeDtypeStruct((s_len, D), F32), jax.ShapeDtypeStruct((1, D), F32)],
        scratch_shapes=[pltpu.VMEM((tm, D), F32)], name="dh", compiler_params=_cparams(2))(dproj, w, x, g, dout)


def _dw_call(h, dproj):
    s_len = h.shape[0]
    tn = 512

    def body(h_ref, dp_ref, o_ref):
        o_ref[...] = lax.dot_general(h_ref[...], dp_ref[...], (((0,), (0,)), ((), ())), preferred_element_type=F32)

    return pl.pallas_call(
        body, grid=(NP // tn,), in_specs=[_bs((s_len, D), lambda j: (0, 0)), _bs((s_len, tn), lambda j: (0, j))],
        out_specs=_bs((D, tn), lambda j: (0, j)), out_shape=jax.ShapeDtypeStruct((D, NP), F32),
        name="dw_in", compiler_params=_cparams(1))(h, dproj)


def _loss_call(y, target):
    s_len = y.shape[0]
    tm = min(s_len, 512)

    def body(y_ref, t_ref, dy_ref, l_ref):
        e = y_ref[...] - t_ref[...]
        dy_ref[...] = e * (1.0 / D)

        @pl.when(pl.program_id(0) == 0)
        def _():
            l_ref[...] = jnp.zeros_like(l_ref)
        l_ref[...] += jnp.sum(e * e, axis=0, keepdims=True)

    row = _bs((tm, D), lambda i: (i, 0))
    return pl.pallas_call(
        body, grid=(s_len // tm,), in_specs=[row, row], out_specs=[row, _bs((1, D), lambda i: (0, 0))],
        out_shape=[jax.ShapeDtypeStruct((s_len, D), F32), jax.ShapeDtypeStruct((1, D), F32)],
        name="loss", compiler_params=_cparams(1))(y, target)


def _adamw_call(w, g, m, v, name):
    rows, cols = w.shape
    tr = min(_row_tile(rows), 128)

    def body(w_ref, g_ref, m_ref, v_ref, d_ref, nm_ref, nv_ref):
        gv = g_ref[...]
        m2 = ADAM_B1 * m_ref[...] + (1.0 - ADAM_B1) * gv
        v2 = ADAM_B2 * v_ref[...] + (1.0 - ADAM_B2) * (gv * gv)
        m_hat = m2 / (1.0 - ADAM_B1 ** ADAM_STEP)
        v_hat = v2 / (1.0 - ADAM_B2 ** ADAM_STEP)
        d_ref[...] = -ADAM_LR * (m_hat / (jnp.sqrt(v_hat) + ADAM_EPS) + ADAM_WD * w_ref[...])
        nm_ref[...] = m2
        nv_ref[...] = v2

    blk = _bs((tr, cols), lambda i: (i, 0))
    return pl.pallas_call(
        body, grid=(rows // tr,), in_specs=[blk] * 4, out_specs=[blk] * 3,
        out_shape=[jax.ShapeDtypeStruct((rows, cols), F32)] * 3, name=name, compiler_params=_cparams(1))(w, g, m, v)


def _row_tile(rows):
    for cand in (512, 256, 128, 64, 32, 16, 8):
        if rows % cand == 0 and rows > cand:
            return cand
    return rows


def _pair_sum_call(g5, from_sibling, core):
    n, _, rows, cols = g5.shape
    tr = _row_tile(rows)

    def body(core_ref, a_ref, b_ref, o_ref):
        o_ref[...] = a_ref[0] + b_ref[...]

    grid_spec = pltpu.PrefetchScalarGridSpec(
        num_scalar_prefetch=1, grid=(n, rows // tr),
        in_specs=[pl.BlockSpec((1, 1, tr, cols), lambda j, i, core_ref: (j, core_ref[0], i, 0)),
                  pl.BlockSpec((1, tr, cols), lambda j, i, core_ref: (j, i, 0))],
        out_specs=pl.BlockSpec((1, tr, cols), lambda j, i, core_ref: (j, i, 0)))
    return pl.pallas_call(body, grid_spec=grid_spec, out_shape=jax.ShapeDtypeStruct((n, rows, cols), F32), name="rs_pair_sum",
                          compiler_params=_cparams(2))(core, g5, from_sibling)


def _owner_sum_call(chip_sum, from_chips, chip):
    _, rows, cols = chip_sum.shape
    tr = _row_tile(rows)

    def body(chip_ref, a_ref, b_ref, o_ref):
        o_ref[...] = ((a_ref[0] + b_ref[0]) + b_ref[1]) + b_ref[2]

    grid_spec = pltpu.PrefetchScalarGridSpec(
        num_scalar_prefetch=1, grid=(rows // tr,),
        in_specs=[pl.BlockSpec((1, tr, cols), lambda i, chip_ref: (chip_ref[0], i, 0)),
                  pl.BlockSpec((3, tr, cols), lambda i, chip_ref: (0, i, 0))],
        out_specs=pl.BlockSpec((tr, cols), lambda i, chip_ref: (i, 0)))
    return pl.pallas_call(body, grid_spec=grid_spec, out_shape=jax.ShapeDtypeStruct((rows, cols), F32), name="rs_owner_sum",
                          compiler_params=_cparams(1))(chip, chip_sum, from_chips)


def _sum8_call(parts):
    n, rows, cols = parts.shape
    tr = _row_tile(rows)

    def body(p_ref, o_ref):
        acc = p_ref[0]
        for k in range(1, n):
            acc = acc + p_ref[k]
        o_ref[...] = acc

    return pl.pallas_call(
        body, grid=(rows // tr,), in_specs=[_bs((n, tr, cols), lambda i: (0, i, 0))], out_specs=_bs((tr, cols), lambda i: (i, 0)),
        out_shape=jax.ShapeDtypeStruct((rows, cols), F32), name="sum_small_grads", compiler_params=_cparams(1))(parts)


_ANY = pl.BlockSpec(memory_space=pl.ANY)


def _all_gather8(blk, name):
    rows, cols = blk.shape

    def body(x_ref, out_ref, send_sems, recv_sems, local_sem):
        x, y, c = lax.axis_index("x"), lax.axis_index("y"), lax.axis_index("c")
        me, sibling = (x, y, c), (x, y, 1 - c)
        chips = [(1 - x, y), (x, 1 - y), (1 - x, 1 - y)]

        def slot(px, py, pc):
            return out_ref.at[4 * px + 2 * py + pc]

        def copy(k, block, to, src=None):
            return pltpu.make_async_remote_copy(
                src_ref=slot(*block) if src is None else src, dst_ref=slot(*block),
                send_sem=send_sems.at[k], recv_sem=recv_sems.at[k], device_id=to, device_id_type=MESH_ID)

        mine = pltpu.make_async_copy(x_ref, slot(*me), local_sem)
        mine.start()
        first = [copy(0, me, sibling, src=x_ref)]
        first += [copy(1 + j, me, (*chip, c), src=x_ref) for j, chip in enumerate(chips)]
        for cp in first:
            cp.start()
        passed = [copy(4 + j, (*chip, c), sibling) for j, chip in enumerate(chips)]
        for j, chip in enumerate(chips):
            copy(1 + j, (*chip, c), me).wait_recv()
            passed[j].start()
        copy(0, sibling, me).wait_recv()
        for j, chip in enumerate(chips):
            copy(4 + j, (*chip, 1 - c), me).wait_recv()
        for cp in first + passed:
            cp.wait_send()
        mine.wait()

    return pl.pallas_call(
        body, out_shape=jax.ShapeDtypeStruct((8, rows, cols), blk.dtype), in_specs=[_ANY], out_specs=_ANY,
        scratch_shapes=[pltpu.SemaphoreType.DMA((7,)), pltpu.SemaphoreType.DMA((7,)), pltpu.SemaphoreType.DMA],
        name=name)(blk)


def _pair_exchange(g5, name):
    n, _, rows, cols = g5.shape

    def body(g_ref, out_ref, send_sem, recv_sem):
        x, y, c = lax.axis_index("x"), lax.axis_index("y"), lax.axis_index("c")
        for j in range(n):
            pltpu.make_async_remote_copy(
                src_ref=g_ref.at[j, 1 - c], dst_ref=out_ref.at[j], send_sem=send_sem.at[j], recv_sem=recv_sem.at[j],
                device_id=(x, y, 1 - c), device_id_type=MESH_ID).start()
        for j in range(n):
            pltpu.make_async_remote_copy(
                src_ref=g_ref.at[j, 1 - c], dst_ref=out_ref.at[j], send_sem=send_sem.at[j], recv_sem=recv_sem.at[j],
                device_id=(x, y, 1 - c), device_id_type=MESH_ID).wait()

    return pl.pallas_call(
        body, out_shape=jax.ShapeDtypeStruct((n, rows, cols), g5.dtype), in_specs=[_ANY], out_specs=_ANY,
        scratch_shapes=[pltpu.SemaphoreType.DMA((n,)), pltpu.SemaphoreType.DMA((n,))], name=name)(g5)


def _chip_scatter(p4, name):
    _, rows, cols = p4.shape

    def body(p_ref, out_ref, send_sems, recv_sems):
        x, y, c = lax.axis_index("x"), lax.axis_index("y"), lax.axis_index("c")
        chips = [(1 - x, y), (x, 1 - y), (1 - x, 1 - y)]
        copies = [pltpu.make_async_remote_copy(
            src_ref=p_ref.at[2 * cx + cy], dst_ref=out_ref.at[k], send_sem=send_sems.at[k], recv_sem=recv_sems.at[k],
            device_id=(cx, cy, c), device_id_type=MESH_ID) for k, (cx, cy) in enumerate(chips)]
        for cp in copies:
            cp.start()
        for cp in copies:
            cp.wait()

    return pl.pallas_call(
        body, out_shape=jax.ShapeDtypeStruct((3, rows, cols), p4.dtype), in_specs=[_ANY], out_specs=_ANY,
        scratch_shapes=[pltpu.SemaphoreType.DMA((3,)), pltpu.SemaphoreType.DMA((3,))], name=name)(p4)


def _pair_gather(half, name):
    rows, cols = half.shape

    def body(h_ref, out_ref, send_sem, recv_sem, local_sem):
        x, y, c = lax.axis_index("x"), lax.axis_index("y"), lax.axis_index("c")
        mine = pltpu.make_async_copy(h_ref, out_ref.at[c], local_sem)
        mine.start()
        cp = pltpu.make_async_remote_copy(src_ref=h_ref, dst_ref=out_ref.at[c], send_sem=send_sem, recv_sem=recv_sem,
                                          device_id=(x, y, 1 - c), device_id_type=MESH_ID)
        cp.start()
        pltpu.make_async_remote_copy(src_ref=h_ref, dst_ref=out_ref.at[1 - c], send_sem=send_sem, recv_sem=recv_sem,
                                     device_id=(x, y, 1 - c), device_id_type=MESH_ID).wait()
        mine.wait()

    return pl.pallas_call(
        body, out_shape=jax.ShapeDtypeStruct((2, rows, cols), half.dtype), in_specs=[_ANY], out_specs=_ANY,
        scratch_shapes=[pltpu.SemaphoreType.DMA, pltpu.SemaphoreType.DMA, pltpu.SemaphoreType.DMA], name=name)(half)


def _pack_rows(flats, dtype, row_multiple):
    flat = jnp.concatenate([f.reshape(-1).astype(dtype) for f in flats])
    n = flat.shape[0]
    rows = -(-n // PACK_W)
    rows = -(-rows // row_multiple) * row_multiple
    return jnp.pad(flat, (0, rows * PACK_W - n)).reshape(rows, PACK_W)


def _unpack(flat, shapes):
    out, off = [], 0
    for shp in shapes:
        n = math.prod(shp)
        out.append(flat[off:off + n].reshape(shp))
        off += n
    return out


def _f32_as_mm_bits(a):
    return lax.bitcast_convert_type(a, jnp.bfloat16).reshape(-1)


def _mm_bits_as_f32(flat, shape):
    return lax.bitcast_convert_type(flat.reshape(-1, 2), F32).reshape(shape)


def _w_in_to_aligned(w):
    z = lambda n: jnp.zeros((w.shape[0], n), w.dtype)
    return jnp.concatenate([w[:, R_ML:R_END], w[:, R_SG:R_ML], w[:, R_CV:R_SGI], w[:, R_SGI:R_MQ], w[:, R_MQ:R_SG],
                            w[:, R_CQ:R_CKV], w[:, R_CKV:R_KR], z(NOPE), w[:, R_KR:R_CV], z(LANES - QKH)], axis=1)


def _w_in_from_aligned(wa):
    return jnp.concatenate([wa[:, OFF_CQ:OFF_CKV], wa[:, OFF_CKV:OFF_KR], wa[:, OFF_KR + NOPE:OFF_KR + QKH], wa[:, OFF_CV:OFF_SGI],
                            wa[:, OFF_SGI:OFF_MQ], wa[:, OFF_MQ:OFF_CQ], wa[:, OFF_SG:OFF_CV], wa[:, OFF_ML:OFF_SG]], axis=1)


def _wuq_to_heads(w):
    w3 = w.reshape(QL, H, QKH)
    w3 = jnp.pad(w3, ((0, 0), (0, 0), (0, LANES - QKH)))
    return jnp.transpose(w3, (1, 0, 2))


def _wuq_from_heads(wh):
    return jnp.transpose(wh[:, :, :QKH], (1, 0, 2)).reshape(QL, H * QKH)


def _wukv_to_heads(w):
    w3 = w.reshape(KVL, H, NOPE + VH)
    wkn = jnp.transpose(jnp.pad(w3[:, :, :NOPE], ((0, 0), (0, 0), (0, LANES - NOPE))), (1, 0, 2))
    wv3 = w3[:, :, NOPE:]
    z = jnp.zeros((KVL, VH), w.dtype)
    cols = []
    for h in range(H):
        cols += [wv3[:, h], z] if h % 2 == 0 else [z, wv3[:, h]]
    return wkn, jnp.concatenate(cols, axis=1)


def _wukv_from_heads(wkn, wv):
    kn = jnp.transpose(wkn[:, :, :NOPE], (1, 0, 2))
    vs = jnp.stack([wv[:, LANES * h + VH * (h % 2):LANES * h + VH * (h % 2) + VH] for h in range(H)], axis=1)
    return jnp.concatenate([kn, vs], axis=2).reshape(KVL, H * (NOPE + VH))


def _layer_fwd(x, mem, tabs, p):
    proj, h = _proj_call(x, p["norm_g"], p["w_in"])
    q, k, v = _mla_prep_call(proj, tabs, p["cq_g"], p["ckv_g"], p["qg"], p["kg"], p["wuq"], p["wkn"], p["wv"])
    ya = _attn_call(q, k, v, proj)
    yb = _conv_call(proj, p["conv_w"], p["conv_b"])
    yc = _sg_call(proj, p["ln_g"], p["ln_b"], p["ws"], p["bs"])
    mk, mv = _memkv_call(mem, p["mem_g"], p["wm"], p["mkg"])
    yd = _mem_call(proj, mk, mv, p["mqg"])
    out = _merge_call((ya, yb, yc, yd), proj, p["bm"], p["wb"], p["wo"], x)
    return out, dict(x=x, proj=proj, h=h, q=q, k=k, v=v, ys=(ya, yb, yc, yd), mk=mk, mv=mv)


def _layer_bwd(dout, mem, tabs, p, sv):
    proj = sv["proj"]
    dya, dyb, dyc, dyd, dml, dbm, dwb, dwo = _merge_bwd_call(sv["ys"], proj, p["bm"], p["wb"], p["wo"], dout)
    dq, dk, dv, dsg_a = _attn_bwd_call(sv["q"], sv["k"], sv["v"], proj, dya)
    dcq, dckv, dkr, dcqg, dckvg, dqg, dkg, dwuq, dwkn, dwv = _mla_prep_bwd_call(
        proj, tabs, p["cq_g"], p["ckv_g"], p["qg"], p["kg"], p["wuq"], p["wkn"], p["wv"], dq, dk, dv)
    dbg, dcg, dxi, dsg_b, dcw, dcb = _conv_bwd_call(proj, p["conv_w"], p["conv_b"], dyb)
    du, dvv, dsg_c, dlg, dlb, dws, dbs = _sg_bwd_call(proj, p["ln_g"], p["ln_b"], p["ws"], p["bs"], dyc)
    dmq, dsg_d, dmk, dmv, dmqg = _mem_bwd_call(proj, sv["mk"], sv["mv"], p["mqg"], dyd)
    dmem_g, dwm, dmkg = _memkv_bwd_call(mem, p["mem_g"], p["wm"], p["mkg"], dmk, dmv)
    dproj = jnp.concatenate([dml, dsg_a, dsg_b, dsg_c, dsg_d, dbg, dcg, dxi, du, dvv, dmq, dcq, dckv, dkr], axis=1)
    dx, dnorm_g = _dh_call(dproj, p["w_in"], sv["x"], p["norm_g"], dout)
    dw_in = _dw_call(sv["h"], dproj)
    grads = dict(norm_g=dnorm_g[0], w_in=_w_in_from_aligned(dw_in), cq_norm_g=dcqg[0], ckv_norm_g=dckvg[0],
                 w_uq=_wuq_from_heads(dwuq), w_ukv=_wukv_from_heads(dwkn, dwv), mla_q_norm_g=dqg[0, :QKH], mla_k_norm_g=dkg[0, :QKH],
                 conv_w=dcw, conv_b=dcb[0], sg_ln_g=dlg[0], sg_ln_b=dlb[0], w_spatial=dws, b_spatial=dbs[:, :, 0],
                 mem_norm_g=dmem_g[0], w_mem_kv=dwm, mem_q_norm_g=dmqg[0], mem_k_norm_g=dmkg[0], b_merge=dbm, w_branch=dwb, w_out=dwo)
    return dx, grads


def _layer_params(l, full):
    pad_g = lambda g: jnp.pad(g, (0, LANES - QKH)).reshape(1, LANES)
    wkn, wv = _wukv_to_heads(full["w_ukv"][l])
    return dict(
        norm_g=full["norm_g"][l].reshape(1, D), w_in=_w_in_to_aligned(full["w_in"][l]),
        cq_g=full["cq_norm_g"][l].reshape(1, QL), ckv_g=full["ckv_norm_g"][l].reshape(1, KVL),
        qg=pad_g(full["mla_q_norm_g"][l]), kg=pad_g(full["mla_k_norm_g"][l]),
        wuq=_wuq_to_heads(full["w_uq"][l]), wkn=wkn, wv=wv,
        conv_w=full["conv_w"][l], conv_b=full["conv_b"][l].reshape(1, CW),
        ln_g=full["sg_ln_g"][l].reshape(1, SGW), ln_b=full["sg_ln_b"][l].reshape(1, SGW),
        ws=full["w_spatial"][l], bs=full["b_spatial"][l].reshape(SGG, SGC, 1),
        mem_g=full["mem_norm_g"][l].reshape(1, D), wm=full["w_mem_kv"][l],
        mqg=full["mem_q_norm_g"][l].reshape(1, MHD), mkg=full["mem_k_norm_g"][l].reshape(1, MHD),
        bm=full["b_merge"][l], wb=full["w_branch"][l], wo=full["w_out"][l])


def _local_step(x, mem, pos, target, full):
    tabs = _rope_tables(pos)
    params = [_layer_params(l, full) for l in range(DEPTH)]
    saved = []
    act = x
    for l in range(DEPTH):
        act, sv = _layer_fwd(act, mem, tabs, params[l])
        saved.append(sv)
    dy, sq = _loss_call(act, target)
    grads = [None] * DEPTH
    for l in reversed(range(DEPTH)):
        dy, grads[l] = _layer_bwd(dy, mem, tabs, params[l], saved[l])
    return sq, dy, grads


_SHARDED_MM = ("w_in", "w_uq", "w_ukv", "w_mem_kv", "w_branch", "w_out")
_SHARDED_F32 = ("conv_w", "b_merge")
_SHARDED = _SHARDED_MM + _SHARDED_F32
_SHARD_AXIS = dict(w_in=2, w_uq=2, w_ukv=2, w_mem_kv=1, w_branch=3, w_out=1, conv_w=2, b_merge=2)
_REPLICATED = ("norm_g", "cq_norm_g", "ckv_norm_g", "mla_q_norm_g", "mla_k_norm_g", "conv_b", "sg_ln_g", "sg_ln_b",
               "w_spatial", "b_spatial", "mem_norm_g", "mem_q_norm_g", "mem_k_norm_g")
_WEIGHTS = ("norm_g", "w_in", "cq_norm_g", "ckv_norm_g", "w_uq", "w_ukv", "mla_q_norm_g", "mla_k_norm_g", "conv_w", "conv_b",
            "sg_ln_g", "sg_ln_b", "w_spatial", "b_spatial", "mem_norm_g", "w_mem_kv", "mem_q_norm_g", "mem_k_norm_g",
            "b_merge", "w_branch", "w_out")
_BIG = ("w_in", "w_uq", "w_ukv", "w_mem_kv", "w_branch", "w_out")
_SMALL = tuple(n for n in _WEIGHTS if n not in _BIG)


def _gather_weights(w):
    flats = [w[n].astype(MM) for n in _SHARDED_MM] + [_f32_as_mm_bits(w[n]) for n in _SHARDED_F32]
    packed = _pack_rows(flats, MM, 32)
    rows = packed.shape[0]
    c = lax.axis_index("c")
    mine = lax.dynamic_slice_in_dim(packed, c * (rows // 2), rows // 2, axis=0)
    got = _all_gather8(mine, "gather_weights").reshape(4, rows * PACK_W)
    shapes = [w[n].shape for n in _SHARDED_MM] + [(2 * math.prod(w[n].shape),) for n in _SHARDED_F32]
    per_chip = [_unpack(got[j], shapes) for j in range(4)]
    full = {}
    for t, n in enumerate(_SHARDED):
        parts = [per_chip[j][t] for j in range(4)]
        if n in _SHARDED_F32:
            parts = [_mm_bits_as_f32(p, w[n].shape) for p in parts]
        full[n] = jnp.concatenate(parts, axis=_SHARD_AXIS[n])
    return full


def _reduce_scatter_grads(g):
    x, y, c = lax.axis_index("x"), lax.axis_index("y"), lax.axis_index("c")
    shard_shapes, slabs = None, []
    for j in range(4):
        parts = []
        for n in _SHARDED:
            ax = _SHARD_AXIS[n]
            size = g[n].shape[ax] // 4
            parts.append(lax.slice_in_dim(g[n], j * size, (j + 1) * size, axis=ax))
        shard_shapes = [p.shape for p in parts]
        slabs.append(_pack_rows(parts, F32, 16))
    rows = slabs[0].shape[0]
    g5 = jnp.stack(slabs).reshape(4, 2, rows // 2, PACK_W)
    from_sibling = _pair_exchange(g5, "rs_pair_exchange")
    chip_sum = _pair_sum_call(g5, from_sibling, c.astype(jnp.int32).reshape(1))
    from_chips = _chip_scatter(chip_sum, "rs_chip_scatter")
    mine = _owner_sum_call(chip_sum, from_chips, (2 * x + y).astype(jnp.int32).reshape(1))
    both = _pair_gather(mine, "rs_pair_gather").reshape(-1)
    return dict(zip(_SHARDED, _unpack(both, shard_shapes)))


def _all_reduce_small(g):
    packed = _pack_rows([g[n] for n in _REPLICATED], F32, 8)
    got = _all_gather8(packed, "gather_small_grads")
    total = _sum8_call(got).reshape(-1)
    return dict(zip(_REPLICATED, _unpack(total, [g[n].shape for n in _REPLICATED])))


def _adamw_all(w, g, m, v):
    delta, new_m, new_v = {}, {}, {}
    for n in _BIG:
        shp = w[n].shape
        as2d = lambda a: a.reshape(-1, shp[-1])
        d, nm, nv = _adamw_call(as2d(w[n]), as2d(g[n]), as2d(m[n]), as2d(v[n]), "adamw_" + n)
        delta[n], new_m[n], new_v[n] = d.reshape(shp), nm.reshape(shp), nv.reshape(shp)
    shapes = [w[n].shape for n in _SMALL]
    pk = lambda t: _pack_rows([t[n] for n in _SMALL], F32, 8)
    d, nm, nv = _adamw_call(pk(w), pk(g), pk(m), pk(v), "adamw_small")
    for out, packed in ((delta, d), (new_m, nm), (new_v, nv)):
        out.update(zip(_SMALL, _unpack(packed.reshape(-1), shapes)))
    return delta, new_m, new_v


def kernel(x, mem, positions, norm_g, w_in, cq_norm_g, ckv_norm_g, w_uq, w_ukv, mla_q_norm_g, mla_k_norm_g, conv_w, conv_b, sg_ln_g, sg_ln_b, w_spatial, b_spatial, mem_norm_g, w_mem_kv, mem_q_norm_g, mem_k_norm_g, b_merge, w_branch, w_out, loss_target, m_norm_g, m_w_in, m_cq_norm_g, m_ckv_norm_g, m_w_uq, m_w_ukv, m_mla_q_norm_g, m_mla_k_norm_g, m_conv_w, m_conv_b, m_sg_ln_g, m_sg_ln_b, m_w_spatial, m_b_spatial, m_mem_norm_g, m_w_mem_kv, m_mem_q_norm_g, m_mem_k_norm_g, m_b_merge, m_w_branch, m_w_out, v_norm_g, v_w_in, v_cq_norm_g, v_ckv_norm_g, v_w_uq, v_w_ukv, v_mla_q_norm_g, v_mla_k_norm_g, v_conv_w, v_conv_b, v_sg_ln_g, v_sg_ln_b, v_w_spatial, v_b_spatial, v_mem_norm_g, v_w_mem_kv, v_mem_q_norm_g, v_mem_k_norm_g, v_b_merge, v_w_branch, v_w_out):
    w = dict(norm_g=norm_g, w_in=w_in, cq_norm_g=cq_norm_g, ckv_norm_g=ckv_norm_g, w_uq=w_uq, w_ukv=w_ukv,
             mla_q_norm_g=mla_q_norm_g, mla_k_norm_g=mla_k_norm_g, conv_w=conv_w, conv_b=conv_b, sg_ln_g=sg_ln_g,
             sg_ln_b=sg_ln_b, w_spatial=w_spatial, b_spatial=b_spatial, mem_norm_g=mem_norm_g, w_mem_kv=w_mem_kv,
             mem_q_norm_g=mem_q_norm_g, mem_k_norm_g=mem_k_norm_g, b_merge=b_merge, w_branch=w_branch, w_out=w_out)
    m = dict(norm_g=m_norm_g, w_in=m_w_in, cq_norm_g=m_cq_norm_g, ckv_norm_g=m_ckv_norm_g, w_uq=m_w_uq, w_ukv=m_w_ukv,
             mla_q_norm_g=m_mla_q_norm_g, mla_k_norm_g=m_mla_k_norm_g, conv_w=m_conv_w, conv_b=m_conv_b, sg_ln_g=m_sg_ln_g,
             sg_ln_b=m_sg_ln_b, w_spatial=m_w_spatial, b_spatial=m_b_spatial, mem_norm_g=m_mem_norm_g, w_mem_kv=m_w_mem_kv,
             mem_q_norm_g=m_mem_q_norm_g, mem_k_norm_g=m_mem_k_norm_g, b_merge=m_b_merge, w_branch=m_w_branch, w_out=m_w_out)
    v = dict(norm_g=v_norm_g, w_in=v_w_in, cq_norm_g=v_cq_norm_g, ckv_norm_g=v_ckv_norm_g, w_uq=v_w_uq, w_ukv=v_w_ukv,
             mla_q_norm_g=v_mla_q_norm_g, mla_k_norm_g=v_mla_k_norm_g, conv_w=v_conv_w, conv_b=v_conv_b, sg_ln_g=v_sg_ln_g,
             sg_ln_b=v_sg_ln_b, w_spatial=v_w_spatial, b_spatial=v_b_spatial, mem_norm_g=v_mem_norm_g, w_mem_kv=v_w_mem_kv,
             mem_q_norm_g=v_mem_q_norm_g, mem_k_norm_g=v_mem_k_norm_g, b_merge=v_b_merge, w_branch=v_w_branch, w_out=v_w_out)

    full = dict(w)
    full.update(_gather_weights(w))
    sq, grad_x, layer_grads = _local_step(x[0], mem[0], positions[0], loss_target[0], full)
    loss = lax.psum(0.5 / D * jnp.sum(sq), ("x", "y", "c"))

    g_local = {n: jnp.stack([layer_grads[l][n] for l in range(DEPTH)]) for n in _WEIGHTS}
    g = {}
    g.update(_reduce_scatter_grads(g_local))
    g.update(_all_reduce_small(g_local))
    delta, new_m, new_v = _adamw_all(w, g, m, v)
    return (loss, grad_x[None], *[g[n] for n in _WEIGHTS], *[delta[n] for n in _WEIGHTS],
            *[new_m[n] for n in _WEIGHTS], *[new_v[n] for n in _WEIGHTS])
```

```python
import functools
import math

import jax
import jax.numpy as jnp
from jax import lax
from jax.experimental import pallas as pl
from jax.experimental.pallas import tpu as pltpu

F32 = jnp.float32
MM = jnp.bfloat16

D = 1024
DEPTH = 2
EPS = 1e-6
H = 8
NOPE = 64
ROPE = 32
QKH = 96
VH = 64
QL = 256
KVL = 128
ROPE_THETA = 10000.0
CW = 512
SGW = 512
SGG = 4
SGC = 128
MH = 4
MHD = 128
NB = 4
BW = 512
NEG_INF = -1e30
LANES = 128
N_CHIPS = 4

R_CQ, R_CKV, R_KR, R_CV, R_SGI, R_MQ, R_SG, R_ML, R_END = 0, 256, 384, 416, 1952, 2976, 3488, 5536, 9632
OFF_ML, OFF_SG, OFF_CV, OFF_SGI, OFF_MQ, OFF_CQ, OFF_CKV, OFF_KR, NP = 0, 4096, 6144, 7680, 8704, 9216, 9472, 9600, 9728

ADAM_LR = 0.001
ADAM_B1 = 0.9
ADAM_B2 = 0.999
ADAM_EPS = 1e-08
ADAM_WD = 0.01
ADAM_STEP = 10

VMEM_LIMIT = 56 * 1024 * 1024
PACK_W = 512
MESH_ID = pl.DeviceIdType.MESH


def _cparams(n_axes):
    return pltpu.CompilerParams(dimension_semantics=("arbitrary",) * n_axes, vmem_limit_bytes=VMEM_LIMIT)


def _bs(shape, imap):
    return pl.BlockSpec(shape, imap)


@jax.custom_vjp
def _mm(a, b):
    return jnp.dot(a.astype(MM), b.astype(MM), preferred_element_type=F32)


def _mm_fwd(a, b):
    return _mm(a, b), (a, b)


def _mm_bwd(res, g):
    a, b = res
    gm = g.astype(MM)
    da = lax.dot_general(gm, b.astype(MM), (((1,), (1,)), ((), ())), preferred_element_type=F32)
    db = lax.dot_general(a.astype(MM), gm, (((0,), (0,)), ((), ())), preferred_element_type=F32)
    return da.astype(a.dtype), db.astype(b.dtype)


_mm.defvjp(_mm_fwd, _mm_bwd)


@jax.custom_vjp
def _mm_nt(a, b):
    return lax.dot_general(a.astype(MM), b.astype(MM), (((1,), (1,)), ((), ())), preferred_element_type=F32)


def _mm_nt_fwd(a, b):
    return _mm_nt(a, b), (a, b)


def _mm_nt_bwd(res, g):
    a, b = res
    gm = g.astype(MM)
    da = jnp.dot(gm, b.astype(MM), preferred_element_type=F32)
    db = lax.dot_general(gm, a.astype(MM), (((0,), (0,)), ((), ())), preferred_element_type=F32)
    return da.astype(a.dtype), db.astype(b.dtype)


_mm_nt.defvjp(_mm_nt_fwd, _mm_nt_bwd)


@functools.partial(jax.custom_vjp, nondiff_argnums=(1,))
def _lane_roll(x, shift):
    return pltpu.roll(x, shift, 1)


def _lane_roll_fwd(x, shift):
    return pltpu.roll(x, shift, 1), None


def _lane_roll_bwd(shift, _, g):
    return (pltpu.roll(g, (LANES - shift) % LANES, 1),)


_lane_roll.defvjp(_lane_roll_fwd, _lane_roll_bwd)


def _rms_n(x, g, n):
    ms = jnp.sum(x * x, axis=-1, keepdims=True) * (1.0 / n)
    return x * lax.rsqrt(ms + EPS) * g


def _softmax(s):
    m = jnp.max(s, axis=-1, keepdims=True)
    e = jnp.exp(s - m)
    return e / jnp.sum(e, axis=-1, keepdims=True)


def _rope(t, cos_t, sin_a, sin_b):
    return t * cos_t + _lane_roll(t, LANES - 16) * sin_a + _lane_roll(t, 16) * sin_b


def _mla_prep_fn(cq, ckv, kr, cos_t, sin_a, sin_b, cq_g, ckv_g, qg, kg, wuq, wkn, wv):
    cqn = _rms_n(cq, cq_g, QL)
    ckvn = _rms_n(ckv, ckv_g, KVL)
    lane = lax.broadcasted_iota(jnp.int32, kr.shape, 1)
    krm = jnp.where((lane >= NOPE) & (lane < QKH), kr, 0.0)
    qs, ks = [], []
    for h in range(H):
        qh = _rms_n(_mm(cqn, wuq[h]), qg, QKH)
        qs.append(_rope(qh, cos_t, sin_a, sin_b))
        kh = _rms_n(_mm(ckvn, wkn[h]) + krm, kg, QKH)
        ks.append(_rope(kh, cos_t, sin_a, sin_b))
    return jnp.concatenate(qs, axis=-1), jnp.concatenate(ks, axis=-1), _mm(ckvn, wv)


def _attn_pair_fn(q2, k2, v2, sg, row0):
    tq, s_len = q2.shape[0], k2.shape[0]
    rows = row0 + lax.broadcasted_iota(jnp.int32, (tq, s_len), 0)
    cols = lax.broadcasted_iota(jnp.int32, (tq, s_len), 1)
    mask = cols <= rows
    vlane = lax.broadcasted_iota(jnp.int32, (s_len, LANES), 1)
    o = jnp.zeros((tq, LANES), F32)
    for e in range(2):
        sl = slice(LANES * e, LANES * (e + 1))
        s = _mm_nt(q2[:, sl], k2[:, sl]) * (QKH ** -0.5)
        p = _softmax(jnp.where(mask, s, NEG_INF))
        ve = jnp.where((vlane >= VH * e) & (vlane < VH * (e + 1)), v2[:, sl], 0.0)
        o = o + _mm(p, ve)
    return o * jax.nn.silu(sg)


def _sg_fn(u, v, sgc, ln_g, ln_b, ws, bs):
    mu = jnp.mean(v, axis=-1, keepdims=True)
    xc = v - mu
    vn = xc * lax.rsqrt(jnp.mean(xc * xc, axis=-1, keepdims=True) + EPS) * ln_g + ln_b
    r = lax.broadcasted_iota(jnp.int32, (SGC, SGC), 0)
    c = lax.broadcasted_iota(jnp.int32, (SGC, SGC), 1)
    wt = [jnp.where(r >= c, w, 0.0) for w in ws]
    row_blocks = []
    for ch in range(u.shape[0] // SGC):
        col_blocks = []
        for g in range(SGG):
            blk = vn[SGC * ch:SGC * (ch + 1), LANES * g:LANES * (g + 1)]
            col_blocks.append(_mm(wt[g], blk) + bs[g])
        row_blocks.append(jnp.concatenate(col_blocks, axis=-1))
    mixed = jnp.concatenate(row_blocks, axis=0)
    return (u * mixed) * jax.nn.silu(sgc)


def _memkv_fn(mem, mem_g, wm, kg):
    kv = _mm(_rms_n(mem, mem_g, D), wm)
    ks = [_rms_n(kv[:, MHD * h:MHD * (h + 1)], kg, MHD) for h in range(MH)]
    return jnp.concatenate(ks, axis=-1), kv[:, MH * MHD:]


def _mem_fn(mq, sgd, k, v, qg):
    outs = []
    for h in range(MH):
        sl = slice(MHD * h, MHD * (h + 1))
        qh = _rms_n(mq[:, sl], qg, MHD)
        p = _softmax(_mm_nt(qh, k[:, sl]) * (MHD ** -0.5))
        outs.append(_mm(p, v[:, sl]))
    return jnp.concatenate(outs, axis=-1) * jax.nn.silu(sgd)


def _merge_fn(ys, logits, bm, wb, wo):
    merged = None
    for n in range(NB):
        z = jnp.concatenate([_mm(ys[n], wb[j][n]) for j in range(N_CHIPS)], axis=-1)
        gate = jax.nn.sigmoid(logits[:, D * n:D * (n + 1)] + bm[n])
        merged = gate * z if merged is None else merged + gate * z
    return _mm(merged, wo)


def _proj_call(x, g, w):
    s_len = x.shape[0]
    tm, tn = min(s_len, 1024), 512

    def body(x_ref, g_ref, w_ref, p_ref, h_ref):
        @pl.when(pl.program_id(1) == 0)
        def _():
            h_ref[...] = _rms_n(x_ref[...], g_ref[...], D).astype(h_ref.dtype)
        p_ref[...] = jnp.dot(h_ref[...], w_ref[...], preferred_element_type=F32)

    return pl.pallas_call(
        body, grid=(s_len // tm, NP // tn),
        in_specs=[_bs((tm, D), lambda i, j: (i, 0)), _bs((1, D), lambda i, j: (0, 0)), _bs((D, tn), lambda i, j: (0, j))],
        out_specs=[_bs((tm, tn), lambda i, j: (i, j)), _bs((tm, D), lambda i, j: (i, 0))],
        out_shape=[jax.ShapeDtypeStruct((s_len, NP), F32), jax.ShapeDtypeStruct((s_len, D), MM)],
        name="proj", compiler_params=_cparams(2))(x, g, w)


def _rope_tables(pos):
    half = ROPE // 2
    inv_freq = ROPE_THETA ** (-jnp.arange(half, dtype=F32) / half)
    ang = pos.astype(F32)[:, None] * inv_freq
    cos, sin = jnp.cos(ang), jnp.sin(ang)
    s_len = pos.shape[0]
    z = lambda n: jnp.zeros((s_len, n), F32)
    cos_t = jnp.concatenate([jnp.ones((s_len, NOPE), F32), cos, cos, z(LANES - QKH)], axis=1)
    sin_a = jnp.concatenate([z(NOPE), -sin, z(LANES - NOPE - half)], axis=1)
    sin_b = jnp.concatenate([z(NOPE + half), sin, z(LANES - QKH)], axis=1)
    return cos_t, sin_a, sin_b


def _mla_prep_specs(tm):
    row = lambda w, off: _bs((tm, w), lambda i: (i, off // w))
    full2 = lambda a, b: _bs((a, b), lambda i: (0, 0))
    full3 = lambda a, b, c: _bs((a, b, c), lambda i: (0, 0, 0))
    tab = _bs((tm, LANES), lambda i: (i, 0))
    return [row(QL, OFF_CQ), row(KVL, OFF_CKV), row(LANES, OFF_KR), tab, tab, tab,
            full2(1, QL), full2(1, KVL), full2(1, LANES), full2(1, LANES),
            full3(H, QL, LANES), full3(H, KVL, LANES), full2(KVL, H * LANES)]


def _mla_prep_args(body_refs, wdtype=None):
    (cq, ckv, kr, ct, sa, sb, cqg, ckvg, qg, kg, wuq, wkn, wv) = body_refs
    cast = (lambda a: a) if wdtype is None else (lambda a: a.astype(wdtype))
    return (cq[...], ckv[...], kr[...], ct[...], sa[...], sb[...], cqg[...], ckvg[...], qg[...], kg[...],
            [cast(wuq[h]) for h in range(H)], [cast(wkn[h]) for h in range(H)], cast(wv[...]))


def _mla_prep_call(proj, tabs, cq_g, ckv_g, qg, kg, wuq, wkn, wv):
    s_len = proj.shape[0]
    tm = min(s_len, 256)

    def body(*refs):
        q_ref, k_ref, v_ref = refs[13:]
        q, k, v = _mla_prep_fn(*_mla_prep_args(refs[:13]))
        q_ref[...] = q.astype(q_ref.dtype)
        k_ref[...] = k.astype(k_ref.dtype)
        v_ref[...] = v.astype(v_ref.dtype)

    out = _bs((tm, H * LANES), lambda i: (i, 0))
    return pl.pallas_call(
        body, grid=(s_len // tm,), in_specs=_mla_prep_specs(tm), out_specs=[out, out, out],
        out_shape=[jax.ShapeDtypeStruct((s_len, H * LANES), MM)] * 3,
        name="mla_prep", compiler_params=_cparams(1))(proj, proj, proj, *tabs, cq_g, ckv_g, qg, kg, wuq, wkn, wv)


def _mla_prep_bwd_call(proj, tabs, cq_g, ckv_g, qg, kg, wuq, wkn, wv, dq, dk, dv):
    s_len = proj.shape[0]
    tm = min(s_len, 256)

    def body(*refs):
        dq_ref, dk_ref, dv_ref = refs[13:16]
        dcq_ref, dckv_ref, dkr_ref, dcqg_ref, dckvg_ref, dqg_ref, dkg_ref, dwuq_ref, dwkn_ref, dwv_ref = refs[16:]
        _, vjp = jax.vjp(_mla_prep_fn, *_mla_prep_args(refs[:13], F32))
        (dcq, dckv, dkr, _, _, _, dcqg, dckvg, dqg, dkg, dwuq, dwkn, dwv) = vjp((dq_ref[...], dk_ref[...], dv_ref[...]))
        dcq_ref[...] = dcq.astype(dcq_ref.dtype)
        dckv_ref[...] = dckv.astype(dckv_ref.dtype)
        dkr_ref[...] = dkr.astype(dkr_ref.dtype)

        @pl.when(pl.program_id(0) == 0)
        def _():
            for r in (dcqg_ref, dckvg_ref, dqg_ref, dkg_ref, dwuq_ref, dwkn_ref, dwv_ref):
                r[...] = jnp.zeros_like(r)
        dcqg_ref[...] += dcqg
        dckvg_ref[...] += dckvg
        dqg_ref[...] += dqg
        dkg_ref[...] += dkg
        for h in range(H):
            dwuq_ref[h] += dwuq[h]
            dwkn_ref[h] += dwkn[h]
        dwv_ref[...] += dwv

    big = _bs((tm, H * LANES), lambda i: (i, 0))
    row = lambda w: _bs((tm, w), lambda i: (i, 0))
    full2 = lambda a, b: _bs((a, b), lambda i: (0, 0))
    full3 = lambda a, b, c: _bs((a, b, c), lambda i: (0, 0, 0))
    sd = jax.ShapeDtypeStruct
    return pl.pallas_call(
        body, grid=(s_len // tm,), in_specs=_mla_prep_specs(tm) + [big, big, big],
        out_specs=[row(QL), row(KVL), row(LANES), full2(1, QL), full2(1, KVL), full2(1, LANES), full2(1, LANES),
                   full3(H, QL, LANES), full3(H, KVL, LANES), full2(KVL, H * LANES)],
        out_shape=[sd((s_len, QL), MM), sd((s_len, KVL), MM), sd((s_len, LANES), MM), sd((1, QL), F32), sd((1, KVL), F32),
                   sd((1, LANES), F32), sd((1, LANES), F32), sd((H, QL, LANES), F32), sd((H, KVL, LANES), F32),
                   sd((KVL, H * LANES), F32)],
        name="mla_prep_bwd", compiler_params=_cparams(1))(proj, proj, proj, *tabs, cq_g, ckv_g, qg, kg, wuq, wkn, wv, dq, dk, dv)


def _attn_specs(s_len, tq):
    pair = 2 * LANES
    return [_bs((tq, pair), lambda p, i: (i, p)), _bs((s_len, pair), lambda p, i: (0, p)), _bs((s_len, pair), lambda p, i: (0, p)),
            _bs((tq, LANES), lambda p, i: (i, OFF_SG // LANES + p))]


def _attn_call(q, k, v, proj):
    s_len = q.shape[0]
    tq = min(s_len, 256)

    def body(q_ref, k_ref, v_ref, sg_ref, y_ref):
        row0 = pl.program_id(1) * tq
        y_ref[...] = _attn_pair_fn(q_ref[...], k_ref[...], v_ref[...], sg_ref[...], row0).astype(y_ref.dtype)

    return pl.pallas_call(
        body, grid=(H // 2, s_len // tq), in_specs=_attn_specs(s_len, tq),
        out_specs=_bs((tq, LANES), lambda p, i: (i, p)), out_shape=jax.ShapeDtypeStruct((s_len, BW), MM),
        name="attn", compiler_params=_cparams(2))(q, k, v, proj)


def _attn_bwd_call(q, k, v, proj, dys):
    s_len = q.shape[0]
    tq = min(s_len, 256)
    pair = 2 * LANES

    def body(q_ref, k_ref, v_ref, sg_ref, dy_ref, dq_ref, dk_ref, dv_ref, dsg_ref):
        i = pl.program_id(1)
        fn = functools.partial(_attn_pair_fn, row0=i * tq)
        _, vjp = jax.vjp(fn, q_ref[...].astype(F32), k_ref[...].astype(F32), v_ref[...].astype(F32), sg_ref[...])
        dq, dk, dv, dsg = vjp(dy_ref[...])
        dq_ref[...] = dq
        dsg_ref[...] = dsg.astype(dsg_ref.dtype)

        @pl.when(i == 0)
        def _():
            dk_ref[...] = jnp.zeros_like(dk_ref)
            dv_ref[...] = jnp.zeros_like(dv_ref)
        dk_ref[...] += dk
        dv_ref[...] += dv

    sd = jax.ShapeDtypeStruct
    return pl.pallas_call(
        body, grid=(H // 2, s_len // tq),
        in_specs=_attn_specs(s_len, tq) + [_bs((tq, LANES), lambda p, i: (i, p))],
        out_specs=[_bs((tq, pair), lambda p, i: (i, p)), _bs((s_len, pair), lambda p, i: (0, p)),
                   _bs((s_len, pair), lambda p, i: (0, p)), _bs((tq, LANES), lambda p, i: (i, p))],
        out_shape=[sd((s_len, H * LANES), F32), sd((s_len, H * LANES), F32), sd((s_len, H * LANES), F32), sd((s_len, BW), MM)],
        name="attn_bwd", compiler_params=_cparams(2))(q, k, v, proj, dys)


def _shift_down(a, n):
    r = lax.broadcasted_iota(jnp.int32, a.shape, 0)
    return jnp.where(r >= n, pltpu.roll(a, n, 0), 0.0)


def _shift_up(a, n):
    s_len = a.shape[0]
    r = lax.broadcasted_iota(jnp.int32, a.shape, 0)
    return jnp.where(r < s_len - n, pltpu.roll(a, s_len - n, 0), 0.0)


def _conv_specs(s_len):
    col = lambda off: _bs((s_len, LANES), lambda j: (0, off // LANES + j))
    return [col(OFF_CV), col(OFF_CV + CW), col(OFF_CV + 2 * CW), col(OFF_SG + BW),
            _bs((3, LANES), lambda j: (0, j)), _bs((1, LANES), lambda j: (0, j))]


def _conv_call(proj, cw, cb):
    s_len = proj.shape[0]

    def body(bg_ref, cg_ref, xi_ref, sg_ref, w_ref, b_ref, y_ref):
        z = cg_ref[...] * xi_ref[...]
        y = b_ref[...] + w_ref[0:1, :] * _shift_down(z, 2)
        y = y + w_ref[1:2, :] * _shift_down(z, 1)
        y = y + w_ref[2:3, :] * z
        y_ref[...] = ((bg_ref[...] * y) * jax.nn.silu(sg_ref[...])).astype(y_ref.dtype)

    return pl.pallas_call(
        body, grid=(CW // LANES,), in_specs=_conv_specs(s_len), out_specs=_bs((s_len, LANES), lambda j: (0, j)),
        out_shape=jax.ShapeDtypeStruct((s_len, CW), MM), name="conv", compiler_params=_cparams(1))(proj, proj, proj, proj, cw, cb)


def _conv_bwd_call(proj, cw, cb, dys):
    s_len = proj.shape[0]

    def body(bg_ref, cg_ref, xi_ref, sg_ref, w_ref, b_ref, dys_ref, dbg_ref, dcg_ref, dxi_ref, dsg_ref, dw_ref, db_ref):
        bg, cg, xi, sg = bg_ref[...], cg_ref[...], xi_ref[...], sg_ref[...]
        w0, w1, w2 = w_ref[0:1, :], w_ref[1:2, :], w_ref[2:3, :]
        z = cg * xi
        z1, z2 = _shift_down(z, 1), _shift_down(z, 2)
        y = b_ref[...] + w0 * z2
        y = y + w1 * z1
        y = y + w2 * z
        yb = bg * y
        sig = jax.nn.sigmoid(sg)
        silu = sg * sig
        dys_v = dys_ref[...]
        dsg_ref[...] = (dys_v * yb * (sig * (1.0 + sg * (1.0 - sig)))).astype(dsg_ref.dtype)
        dyb = dys_v * silu
        dbg_ref[...] = (dyb * y).astype(dbg_ref.dtype)
        dy = dyb * bg
        db_ref[...] = jnp.sum(dy, axis=0, keepdims=True)
        dw_ref[0:1, :] = jnp.sum(dy * z2, axis=0, keepdims=True)
        dw_ref[1:2, :] = jnp.sum(dy * z1, axis=0, keepdims=True)
        dw_ref[2:3, :] = jnp.sum(dy * z, axis=0, keepdims=True)
        dz = w2 * dy + w1 * _shift_up(dy, 1) + w0 * _shift_up(dy, 2)
        dcg_ref[...] = (dz * xi).astype(dcg_ref.dtype)
        dxi_ref[...] = (dz * cg).astype(dxi_ref.dtype)

    col = _bs((s_len, LANES), lambda j: (0, j))
    sd = jax.ShapeDtypeStruct
    return pl.pallas_call(
        body, grid=(CW // LANES,), in_specs=_conv_specs(s_len) + [col],
        out_specs=[col, col, col, col, _bs((3, LANES), lambda j: (0, j)), _bs((1, LANES), lambda j: (0, j))],
        out_shape=[sd((s_len, CW), MM)] * 4 + [sd((3, CW), F32), sd((1, CW), F32)],
        name="conv_bwd", compiler_params=_cparams(1))(proj, proj, proj, proj, cw, cb, dys)


def _sg_specs(tm):
    row = lambda off: _bs((tm, SGW), lambda i: (i, off // SGW))
    return [row(OFF_SGI), row(OFF_SGI + SGW), row(OFF_SG + 2 * BW), _bs((1, SGW), lambda i: (0, 0)), _bs((1, SGW), lambda i: (0, 0)),
            _bs((SGG, SGC, SGC), lambda i: (0, 0, 0)), _bs((SGG, SGC, 1), lambda i: (0, 0, 0))]


def _sg_args(refs):
    u, v, sg, lg, lb, ws, bs = refs
    return (u[...], v[...], sg[...], lg[...], lb[...], [ws[g] for g in range(SGG)], [bs[g] for g in range(SGG)])


def _sg_call(proj, ln_g, ln_b, ws, bs):
    s_len = proj.shape[0]
    tm = min(s_len, 256)

    def body(*refs):
        refs[7][...] = _sg_fn(*_sg_args(refs[:7])).astype(refs[7].dtype)

    return pl.pallas_call(
        body, grid=(s_len // tm,), in_specs=_sg_specs(tm), out_specs=_bs((tm, SGW), lambda i: (i, 0)),
        out_shape=jax.ShapeDtypeStruct((s_len, SGW), MM), name="sgmlp", compiler_params=_cparams(1))(proj, proj, proj, ln_g, ln_b, ws, bs)


def _sg_bwd_call(proj, ln_g, ln_b, ws, bs, dys):
    s_len = proj.shape[0]
    tm = min(s_len, 256)

    def body(*refs):
        dys_ref = refs[7]
        du_ref, dv_ref, dsg_ref, dlg_ref, dlb_ref, dws_ref, dbs_ref = refs[8:]
        _, vjp = jax.vjp(_sg_fn, *_sg_args(refs[:7]))
        du, dv, dsg, dlg, dlb, dws, dbs = vjp(dys_ref[...])
        du_ref[...] = du.astype(du_ref.dtype)
        dv_ref[...] = dv.astype(dv_ref.dtype)
        dsg_ref[...] = dsg.astype(dsg_ref.dtype)

        @pl.when(pl.program_id(0) == 0)
        def _():
            for r in (dlg_ref, dlb_ref, dws_ref, dbs_ref):
                r[...] = jnp.zeros_like(r)
        dlg_ref[...] += dlg
        dlb_ref[...] += dlb
        for g in range(SGG):
            dws_ref[g] += dws[g]
            dbs_ref[g] += dbs[g]

    row = _bs((tm, SGW), lambda i: (i, 0))
    sd = jax.ShapeDtypeStruct
    return pl.pallas_call(
        body, grid=(s_len // tm,), in_specs=_sg_specs(tm) + [row],
        out_specs=[row, row, row, _bs((1, SGW), lambda i: (0, 0)), _bs((1, SGW), lambda i: (0, 0)),
                   _bs((SGG, SGC, SGC), lambda i: (0, 0, 0)), _bs((SGG, SGC, 1), lambda i: (0, 0, 0))],
        out_shape=[sd((s_len, SGW), MM)] * 3 + [sd((1, SGW), F32), sd((1, SGW), F32), sd((SGG, SGC, SGC), F32), sd((SGG, SGC, 1), F32)],
        name="sgmlp_bwd", compiler_params=_cparams(1))(proj, proj, proj, ln_g, ln_b, ws, bs, dys)


def _memkv_call(mem, mem_g, wm, kg):
    m_len = mem.shape[0]

    def body(mem_ref, g_ref, w_ref, kg_ref, k_ref, v_ref):
        k, v = _memkv_fn(mem_ref[...], g_ref[...], w_ref[...], kg_ref[...])
        k_ref[...] = k.astype(k_ref.dtype)
        v_ref[...] = v.astype(v_ref.dtype)

    return pl.pallas_call(body, out_shape=[jax.ShapeDtypeStruct((m_len, MH * MHD), MM)] * 2, name="memkv",
                          compiler_params=pltpu.CompilerParams(vmem_limit_bytes=VMEM_LIMIT))(mem, mem_g, wm, kg)


def _memkv_bwd_call(mem, mem_g, wm, kg, dk, dv):
    def body(mem_ref, g_ref, w_ref, kg_ref, dk_ref, dv_ref, dg_ref, dw_ref, dkg_ref):
        _, vjp = jax.vjp(_memkv_fn, mem_ref[...], g_ref[...], w_ref[...].astype(F32), kg_ref[...])
        _, dg, dw, dkg = vjp((dk_ref[...], dv_ref[...]))
        dg_ref[...] = dg
        dw_ref[...] = dw
        dkg_ref[...] = dkg

    sd = jax.ShapeDtypeStruct
    return pl.pallas_call(body, out_shape=[sd((1, D), F32), sd((D, 2 * MH * MHD), F32), sd((1, MHD), F32)], name="memkv_bwd",
                          compiler_params=pltpu.CompilerParams(vmem_limit_bytes=VMEM_LIMIT))(mem, mem_g, wm, kg, dk, dv)


def _mem_specs(tm, m_len):
    w = MH * MHD
    return [_bs((tm, w), lambda i: (i, OFF_MQ // w)), _bs((tm, BW), lambda i: (i, (OFF_SG + 3 * BW) // BW)),
            _bs((m_len, w), lambda i: (0, 0)), _bs((m_len, w), lambda i: (0, 0)), _bs((1, MHD), lambda i: (0, 0))]


def _mem_call(proj, k, v, qg):
    s_len, m_len = proj.shape[0], k.shape[0]
    tm = min(s_len, 256)

    def body(mq_ref, sg_ref, k_ref, v_ref, qg_ref, y_ref):
        y_ref[...] = _mem_fn(mq_ref[...], sg_ref[...], k_ref[...], v_ref[...], qg_ref[...]).astype(y_ref.dtype)

    return pl.pallas_call(
        body, grid=(s_len // tm,), in_specs=_mem_specs(tm, m_len), out_specs=_bs((tm, BW), lambda i: (i, 0)),
        out_shape=jax.ShapeDtypeStruct((s_len, BW), MM), name="memattn", compiler_params=_cparams(1))(proj, proj, k, v, qg)


def _mem_bwd_call(proj, k, v, qg, dys):
    s_len, m_len = proj.shape[0], k.shape[0]
    tm = min(s_len, 256)
    w = MH * MHD

    def body(mq_ref, sg_ref, k_ref, v_ref, qg_ref, dys_ref, dmq_ref, dsg_ref, dk_ref, dv_ref, dqg_ref):
        _, vjp = jax.vjp(_mem_fn, mq_ref[...], sg_ref[...], k_ref[...].astype(F32), v_ref[...].astype(F32), qg_ref[...])
        dmq, dsg, dk, dv, dqg = vjp(dys_ref[...])
        dmq_ref[...] = dmq.astype(dmq_ref.dtype)
        dsg_ref[...] = dsg.astype(dsg_ref.dtype)

        @pl.when(pl.program_id(0) == 0)
        def _():
            for r in (dk_ref, dv_ref, dqg_ref):
                r[...] = jnp.zeros_like(r)
        dk_ref[...] += dk
        dv_ref[...] += dv
        dqg_ref[...] += dqg

    row = _bs((tm, BW), lambda i: (i, 0))
    kv = _bs((m_len, w), lambda i: (0, 0))
    sd = jax.ShapeDtypeStruct
    return pl.pallas_call(
        body, grid=(s_len // tm,), in_specs=_mem_specs(tm, m_len) + [row],
        out_specs=[row, row, kv, kv, _bs((1, MHD), lambda i: (0, 0))],
        out_shape=[sd((s_len, w), MM), sd((s_len, BW), MM), sd((m_len, w), F32), sd((m_len, w), F32), sd((1, MHD), F32)],
        name="memattn_bwd", compiler_params=_cparams(1))(proj, proj, k, v, qg, dys)


def _merge_specs(tm):
    row = _bs((tm, BW), lambda i: (i, 0))
    return [row, row, row, row, _bs((tm, NB * D), lambda i: (i, OFF_ML // (NB * D))), _bs((NB, D), lambda i: (0, 0)),
            _bs((N_CHIPS, NB, BW, D // N_CHIPS), lambda i: (0, 0, 0, 0)), _bs((D, D), lambda i: (0, 0))]


def _merge_call(ys, proj, bm, wb, wo, x):
    s_len = proj.shape[0]
    tm = min(s_len, 256)

    def body(ya, yb, yc, yd, lg_ref, bm_ref, wb_ref, wo_ref, x_ref, o_ref):
        out = _merge_fn([r[...] for r in (ya, yb, yc, yd)], lg_ref[...], [bm_ref[n:n + 1, :] for n in range(NB)],
                        [[wb_ref[j, n] for n in range(NB)] for j in range(N_CHIPS)], wo_ref[...])
        o_ref[...] = x_ref[...] + out

    xrow = _bs((tm, D), lambda i: (i, 0))
    return pl.pallas_call(
        body, grid=(s_len // tm,), in_specs=_merge_specs(tm) + [xrow], out_specs=xrow,
        out_shape=jax.ShapeDtypeStruct((s_len, D), F32), name="merge", compiler_params=_cparams(1))(*ys, proj, bm, wb, wo, x)


def _merge_bwd_call(ys, proj, bm, wb, wo, dout):
    s_len = proj.shape[0]
    tm = min(s_len, 256)

    def body(ya, yb, yc, yd, lg_ref, bm_ref, wb_ref, wo_ref, do_ref, dya, dyb, dyc, dyd, dlg_ref, dbm_ref, dwb_ref, dwo_ref):
        fn = lambda ys_, lg_, bm_, wb_, wo_: _merge_fn(ys_, lg_, bm_, wb_, wo_)
        _, vjp = jax.vjp(fn, [r[...].astype(F32) for r in (ya, yb, yc, yd)], lg_ref[...], [bm_ref[n:n + 1, :] for n in range(NB)],
                         [[wb_ref[j, n].astype(F32) for n in range(NB)] for j in range(N_CHIPS)], wo_ref[...].astype(F32))
        dys, dlg, dbm, dwb, dwo = vjp(do_ref[...])
        for r, d in zip((dya, dyb, dyc, dyd), dys):
            r[...] = d
        dlg_ref[...] = dlg.astype(dlg_ref.dtype)

        @pl.when(pl.program_id(0) == 0)
        def _():
            for r in (dbm_ref, dwb_ref, dwo_ref):
                r[...] = jnp.zeros_like(r)
        for n in range(NB):
            dbm_ref[n:n + 1, :] += dbm[n]
            for j in range(N_CHIPS):
                dwb_ref[j, n] += dwb[j][n]
        dwo_ref[...] += dwo

    row = _bs((tm, BW), lambda i: (i, 0))
    sd = jax.ShapeDtypeStruct
    wb_shape = (N_CHIPS, NB, BW, D // N_CHIPS)
    return pl.pallas_call(
        body, grid=(s_len // tm,), in_specs=_merge_specs(tm) + [_bs((tm, D), lambda i: (i, 0))],
        out_specs=[row, row, row, row, _bs((tm, NB * D), lambda i: (i, 0)), _bs((NB, D), lambda i: (0, 0)),
                   _bs(wb_shape, lambda i: (0, 0, 0, 0)), _bs((D, D), lambda i: (0, 0))],
        out_shape=[sd((s_len, BW), F32)] * 4 + [sd((s_len, NB * D), MM), sd((NB, D), F32), sd(wb_shape, F32), sd((D, D), F32)],
        name="merge_bwd", compiler_params=_cparams(1))(*ys, proj, bm, wb, wo, dout)


def _dh_call(dproj, w, x, g, dout):
    s_len = x.shape[0]
    tm, tk = min(s_len, 512), NP // 4

    def body(dp_ref, w_ref, x_ref, g_ref, do_ref, dx_ref, dg_ref, acc_ref):
        i, k = pl.program_id(0), pl.program_id(1)

        @pl.when(k == 0)
        def _():
            acc_ref[...] = jnp.zeros_like(acc_ref)
        acc_ref[...] += lax.dot_general(dp_ref[...], w_ref[...], (((1,), (1,)), ((), ())), preferred_element_type=F32)

        @pl.when(k == pl.num_programs(1) - 1)
        def _():
            _, vjp = jax.vjp(lambda x_, g_: _rms_n(x_, g_, D), x_ref[...], g_ref[...])
            dxr, dgr = vjp(acc_ref[...])
            dx_ref[...] = do_ref[...] + dxr

            @pl.when(i == 0)
            def _():
                dg_ref[...] = jnp.zeros_like(dg_ref)
            dg_ref[...] += dgr

    row = _bs((tm, D), lambda i, k: (i, 0))
    return pl.pallas_call(
        body, grid=(s_len // tm, NP // tk),
        in_specs=[_bs((tm, tk), lambda i, k: (i, k)), _bs((D, tk), lambda i, k: (0, k)), row, _bs((1, D), lambda i, k: (0, 0)), row],
        out_specs=[row, _bs((1, D), lambda i, k: (0, 0))],
        out_shape=[jax.ShapeDtypeStruct((s_len, D), F32), jax.ShapeDtypeStruct((1, D), F32)],
        scratch_shapes=[pltpu.VMEM((tm, D), F32)], name="dh", compiler_params=_cparams(2))(dproj, w, x, g, dout)


def _dw_call(h, dproj):
    s_len = h.shape[0]
    tn = 512

    def body(h_ref, dp_ref, o_ref):
        o_ref[...] = lax.dot_general(h_ref[...], dp_ref[...], (((0,), (0,)), ((), ())), preferred_element_type=F32)

    return pl.pallas_call(
        body, grid=(NP // tn,), in_specs=[_bs((s_len, D), lambda j: (0, 0)), _bs((s_len, tn), lambda j: (0, j))],
        out_specs=_bs((D, tn), lambda j: (0, j)), out_shape=jax.ShapeDtypeStruct((D, NP), F32),
        name="dw_in", compiler_params=_cparams(1))(h, dproj)


def _loss_call(y, target):
    s_len = y.shape[0]
    tm = min(s_len, 512)

    def body(y_ref, t_ref, dy_ref, l_ref):
        e = y_ref[...] - t_ref[...]
        dy_ref[...] = e * (1.0 / D)

        @pl.when(pl.program_id(0) == 0)
        def _():
            l_ref[...] = jnp.zeros_like(l_ref)
        l_ref[...] += jnp.sum(e * e, axis=0, keepdims=True)

    row = _bs((tm, D), lambda i: (i, 0))
    return pl.pallas_call(
        body, grid=(s_len // tm,), in_specs=[row, row], out_specs=[row, _bs((1, D), lambda i: (0, 0))],
        out_shape=[jax.ShapeDtypeStruct((s_len, D), F32), jax.ShapeDtypeStruct((1, D), F32)],
        name="loss", compiler_params=_cparams(1))(y, target)


def _adamw_call(w, g, m, v, name):
    rows, cols = w.shape
    tr = min(_row_tile(rows), 128)

    def body(w_ref, g_ref, m_ref, v_ref, d_ref, nm_ref, nv_ref):
        gv = g_ref[...]
        m2 = ADAM_B1 * m_ref[...] + (1.0 - ADAM_B1) * gv
        v2 = ADAM_B2 * v_ref[...] + (1.0 - ADAM_B2) * (gv * gv)
        m_hat = m2 / (1.0 - ADAM_B1 ** ADAM_STEP)
        v_hat = v2 / (1.0 - ADAM_B2 ** ADAM_STEP)
        d_ref[...] = -ADAM_LR * (m_hat / (jnp.sqrt(v_hat) + ADAM_EPS) + ADAM_WD * w_ref[...])
        nm_ref[...] = m2
        nv_ref[...] = v2

    blk = _bs((tr, cols), lambda i: (i, 0))
    return pl.pallas_call(
        body, grid=(rows // tr,), in_specs=[blk] * 4, out_specs=[blk] * 3,
        out_shape=[jax.ShapeDtypeStruct((rows, cols), F32)] * 3, name=name, compiler_params=_cparams(1))(w, g, m, v)


def _row_tile(rows):
    for cand in (512, 256, 128, 64, 32, 16, 8):
        if rows % cand == 0 and rows > cand:
            return cand
    return rows


def _pair_sum_call(grads, from_sibling, core, name):
    n = len(grads)

    def body(core_ref, *refs):
        for t in range(n):
            refs[2 * n + t][...] = (refs[t][...] + refs[n + t][...]).astype(MM)

    half = lambda g: (1, g.shape[1] // 2, g.shape[2])
    grid_spec = pltpu.PrefetchScalarGridSpec(
        num_scalar_prefetch=1, grid=(N_CHIPS,),
        in_specs=[pl.BlockSpec(half(g), lambda j, core_ref: (j, core_ref[0], 0)) for g in grads]
        + [pl.BlockSpec(half(g), lambda j, core_ref: (j, 0, 0)) for g in grads],
        out_specs=[pl.BlockSpec(half(g), lambda j, core_ref: (j, 0, 0)) for g in grads])
    return pl.pallas_call(
        body, grid_spec=grid_spec, out_shape=[jax.ShapeDtypeStruct((N_CHIPS,) + half(g)[1:], MM) for g in grads], name=name,
        compiler_params=_cparams(1))(core, *grads, *from_sibling)


def _owner_sum_call(chip_sums, from_chips, chip, name):
    n = len(chip_sums)
    steps = 4

    def body(chip_ref, *refs):
        for t in range(n):
            a, b = refs[t], refs[n + t]
            refs[2 * n + t][...] = ((a[0].astype(F32) + b[0].astype(F32)) + b[1].astype(F32)) + b[2].astype(F32)

    tile = lambda p: (p.shape[1] // steps, p.shape[2])
    grid_spec = pltpu.PrefetchScalarGridSpec(
        num_scalar_prefetch=1, grid=(steps,),
        in_specs=[pl.BlockSpec((1,) + tile(p), lambda i, chip_ref: (chip_ref[0], i, 0)) for p in chip_sums]
        + [pl.BlockSpec((3,) + tile(p), lambda i, chip_ref: (0, i, 0)) for p in chip_sums],
        out_specs=[pl.BlockSpec(tile(p), lambda i, chip_ref: (i, 0)) for p in chip_sums])
    return pl.pallas_call(
        body, grid_spec=grid_spec, out_shape=[jax.ShapeDtypeStruct(p.shape[1:], F32) for p in chip_sums], name=name,
        compiler_params=_cparams(1))(chip, *chip_sums, *from_chips)


def _sum8_call(parts):
    n, rows, cols = parts.shape
    tr = _row_tile(rows)

    def body(p_ref, o_ref):
        acc = p_ref[0]
        for k in range(1, n):
            acc = acc + p_ref[k]
        o_ref[...] = acc

    return pl.pallas_call(
        body, grid=(rows // tr,), in_specs=[_bs((n, tr, cols), lambda i: (0, i, 0))], out_specs=_bs((tr, cols), lambda i: (i, 0)),
        out_shape=jax.ShapeDtypeStruct((rows, cols), F32), name="sum_small_grads", compiler_params=_cparams(1))(parts)


_ANY = pl.BlockSpec(memory_space=pl.ANY)


def _all_gather8(blk, name):
    rows, cols = blk.shape

    def body(x_ref, out_ref, send_sems, recv_sems, local_sem):
        x, y, c = lax.axis_index("x"), lax.axis_index("y"), lax.axis_index("c")
        me, sibling = (x, y, c), (x, y, 1 - c)
        chips = [(1 - x, y), (x, 1 - y), (1 - x, 1 - y)]

        def slot(px, py, pc):
            return out_ref.at[4 * px + 2 * py + pc]

        def copy(k, block, to, src=None):
            return pltpu.make_async_remote_copy(
                src_ref=slot(*block) if src is None else src, dst_ref=slot(*block),
                send_sem=send_sems.at[k], recv_sem=recv_sems.at[k], device_id=to, device_id_type=MESH_ID)

        mine = pltpu.make_async_copy(x_ref, slot(*me), local_sem)
        mine.start()
        first = [copy(0, me, sibling, src=x_ref)]
        first += [copy(1 + j, me, (*chip, c), src=x_ref) for j, chip in enumerate(chips)]
        for cp in first:
            cp.start()
        passed = [copy(4 + j, (*chip, c), sibling) for j, chip in enumerate(chips)]
        for j, chip in enumerate(chips):
            copy(1 + j, (*chip, c), me).wait_recv()
            passed[j].start()
        copy(0, sibling, me).wait_recv()
        for j, chip in enumerate(chips):
            copy(4 + j, (*chip, 1 - c), me).wait_recv()
        for cp in first + passed:
            cp.wait_send()
        mine.wait()

    return pl.pallas_call(
        body, out_shape=jax.ShapeDtypeStruct((8, rows, cols), blk.dtype), in_specs=[_ANY], out_specs=_ANY,
        scratch_shapes=[pltpu.SemaphoreType.DMA((7,)), pltpu.SemaphoreType.DMA((7,)), pltpu.SemaphoreType.DMA],
        name=name)(blk)


def _half_rows(ref, lead, half, which):
    rows = pl.ds(pl.multiple_of(half * which, half), half)
    return ref.at[rows] if lead is None else ref.at[lead, rows]


def _gather_layer_call(layer, shards, name):
    n = len(shards)
    half = [s.shape[1] // 2 for s in shards]

    def body(*refs):
        srcs, outs = refs[:n], refs[n:2 * n]
        send_sems, recv_sems, local_sems = refs[2 * n:]
        x, y, c = lax.axis_index("x"), lax.axis_index("y"), lax.axis_index("c")
        sibling = (x, y, 1 - c)
        chips = [(1 - x, y), (x, 1 - y), (1 - x, 1 - y)]

        def slot(t, px, py, pc):
            return _half_rows(outs[t], 2 * px + py, half[t], pc)

        def copy(t, k, block, to, src=None):
            return pltpu.make_async_remote_copy(
                src_ref=slot(t, *block) if src is None else src, dst_ref=slot(t, *block),
                send_sem=send_sems.at[7 * t + k], recv_sem=recv_sems.at[7 * t + k], device_id=to, device_id_type=MESH_ID)

        mine = [_half_rows(srcs[t], layer, half[t], c) for t in range(n)]
        local = [pltpu.make_async_copy(mine[t], slot(t, x, y, c), local_sems.at[t]) for t in range(n)]
        for cp in local:
            cp.start()
        first = []
        for t in range(n):
            first.append(copy(t, 0, (x, y, c), sibling, src=mine[t]))
            first += [copy(t, 1 + j, (x, y, c), (*chip, c), src=mine[t]) for j, chip in enumerate(chips)]
        for cp in first:
            cp.start()
        passed = []
        for j, chip in enumerate(chips):
            for t in range(n):
                copy(t, 1 + j, (*chip, c), (x, y, c)).wait_recv()
                passed.append(copy(t, 4 + j, (*chip, c), sibling))
                passed[-1].start()
        for t in range(n):
            copy(t, 0, (x, y, 1 - c), (x, y, c)).wait_recv()
            for j, chip in enumerate(chips):
                copy(t, 4 + j, (*chip, 1 - c), (x, y, c)).wait_recv()
        for cp in first + passed:
            cp.wait_send()
        for cp in local:
            cp.wait()

    return pl.pallas_call(
        body, out_shape=[jax.ShapeDtypeStruct((N_CHIPS,) + s.shape[1:], s.dtype) for s in shards],
        in_specs=[_ANY] * n, out_specs=[_ANY] * n,
        scratch_shapes=[pltpu.SemaphoreType.DMA((7 * n,)), pltpu.SemaphoreType.DMA((7 * n,)), pltpu.SemaphoreType.DMA((n,))],
        name=name)(*shards)


def _pair_exchange_call(grads, name):
    n = len(grads)
    half = [g.shape[1] // 2 for g in grads]

    def body(*refs):
        srcs, outs, send_sems, recv_sems = refs[:n], refs[n:2 * n], refs[2 * n], refs[2 * n + 1]
        x, y, c = lax.axis_index("x"), lax.axis_index("y"), lax.axis_index("c")
        copies = [pltpu.make_async_remote_copy(
            src_ref=srcs[t].at[:, pl.ds(pl.multiple_of(half[t] * (1 - c), half[t]), half[t])], dst_ref=outs[t],
            send_sem=send_sems.at[t], recv_sem=recv_sems.at[t], device_id=(x, y, 1 - c), device_id_type=MESH_ID) for t in range(n)]
        for cp in copies:
            cp.start()
        for cp in copies:
            cp.wait()

    return pl.pallas_call(
        body, out_shape=[jax.ShapeDtypeStruct((g.shape[0], g.shape[1] // 2, g.shape[2]), g.dtype) for g in grads],
        in_specs=[_ANY] * n, out_specs=[_ANY] * n,
        scratch_shapes=[pltpu.SemaphoreType.DMA((n,)), pltpu.SemaphoreType.DMA((n,))], name=name)(*grads)


def _chip_scatter_call(chip_sums, name):
    n = len(chip_sums)

    def body(*refs):
        srcs, outs, send_sems, recv_sems = refs[:n], refs[n:2 * n], refs[2 * n], refs[2 * n + 1]
        x, y, c = lax.axis_index("x"), lax.axis_index("y"), lax.axis_index("c")
        chips = [(1 - x, y), (x, 1 - y), (1 - x, 1 - y)]
        copies = [pltpu.make_async_remote_copy(
            src_ref=srcs[t].at[2 * cx + cy], dst_ref=outs[t].at[k], send_sem=send_sems.at[3 * t + k],
            recv_sem=recv_sems.at[3 * t + k], device_id=(cx, cy, c), device_id_type=MESH_ID)
            for k, (cx, cy) in enumerate(chips) for t in range(n)]
        for cp in copies:
            cp.start()
        for cp in copies:
            cp.wait()

    return pl.pallas_call(
        body, out_shape=[jax.ShapeDtypeStruct((3,) + p.shape[1:], p.dtype) for p in chip_sums],
        in_specs=[_ANY] * n, out_specs=[_ANY] * n,
        scratch_shapes=[pltpu.SemaphoreType.DMA((3 * n,)), pltpu.SemaphoreType.DMA((3 * n,))], name=name)(*chip_sums)


def _pair_gather_call(halves, name):
    n = len(halves)
    half = [h.shape[0] for h in halves]

    def body(*refs):
        srcs, outs = refs[:n], refs[n:2 * n]
        send_sems, recv_sems, local_sems = refs[2 * n:]
        x, y, c = lax.axis_index("x"), lax.axis_index("y"), lax.axis_index("c")
        local = [pltpu.make_async_copy(srcs[t], _half_rows(outs[t], None, half[t], c), local_sems.at[t]) for t in range(n)]
        remote = [pltpu.make_async_remote_copy(
            src_ref=srcs[t], dst_ref=_half_rows(outs[t], None, half[t], c), send_sem=send_sems.at[t], recv_sem=recv_sems.at[t],
            device_id=(x, y, 1 - c), device_id_type=MESH_ID) for t in range(n)]
        for cp in local + remote:
            cp.start()
        for t in range(n):
            pltpu.make_async_remote_copy(
                src_ref=srcs[t], dst_ref=_half_rows(outs[t], None, half[t], 1 - c), send_sem=send_sems.at[t],
                recv_sem=recv_sems.at[t], device_id=(x, y, 1 - c), device_id_type=MESH_ID).wait()
        for cp in local:
            cp.wait()

    return pl.pallas_call(
        body, out_shape=[jax.ShapeDtypeStruct((2 * h.shape[0],) + h.shape[1:], h.dtype) for h in halves],
        in_specs=[_ANY] * n, out_specs=[_ANY] * n,
        scratch_shapes=[pltpu.SemaphoreType.DMA((n,)), pltpu.SemaphoreType.DMA((n,)), pltpu.SemaphoreType.DMA((n,))],
        name=name)(*halves)


def _pack_rows(flats, dtype, row_multiple):
    flat = jnp.concatenate([f.reshape(-1).astype(dtype) for f in flats])
    n = flat.shape[0]
    rows = -(-n // PACK_W)
    rows = -(-rows // row_multiple) * row_multiple
    return jnp.pad(flat, (0, rows * PACK_W - n)).reshape(rows, PACK_W)


def _unpack(flat, shapes):
    out, off = [], 0
    for shp in shapes:
        n = math.prod(shp)
        out.append(flat[off:off + n].reshape(shp))
        off += n
    return out


def _f32_as_mm_bits(a):
    return lax.bitcast_convert_type(a, jnp.bfloat16).reshape(-1)


def _mm_bits_as_f32(flat, shape):
    return lax.bitcast_convert_type(flat.reshape(-1, 2), F32).reshape(shape)


def _w_in_to_aligned(w):
    z = lambda n: jnp.zeros((w.shape[0], n), w.dtype)
    return jnp.concatenate([w[:, R_ML:R_END], w[:, R_SG:R_ML], w[:, R_CV:R_SGI], w[:, R_SGI:R_MQ], w[:, R_MQ:R_SG],
                            w[:, R_CQ:R_CKV], w[:, R_CKV:R_KR], z(NOPE), w[:, R_KR:R_CV], z(LANES - QKH)], axis=1)


def _w_in_from_aligned(wa):
    return jnp.concatenate([wa[:, OFF_CQ:OFF_CKV], wa[:, OFF_CKV:OFF_KR], wa[:, OFF_KR + NOPE:OFF_KR + QKH], wa[:, OFF_CV:OFF_SGI],
                            wa[:, OFF_SGI:OFF_MQ], wa[:, OFF_MQ:OFF_CQ], wa[:, OFF_SG:OFF_CV], wa[:, OFF_ML:OFF_SG]], axis=1)


def _wuq_to_heads(w):
    w3 = w.reshape(QL, H, QKH)
    w3 = jnp.pad(w3, ((0, 0), (0, 0), (0, LANES - QKH)))
    return jnp.transpose(w3, (1, 0, 2))


def _wuq_from_heads(wh):
    return jnp.transpose(wh[:, :, :QKH], (1, 0, 2)).reshape(QL, H * QKH)


def _wukv_to_heads(w):
    w3 = w.reshape(KVL, H, NOPE + VH)
    wkn = jnp.transpose(jnp.pad(w3[:, :, :NOPE], ((0, 0), (0, 0), (0, LANES - NOPE))), (1, 0, 2))
    wv3 = w3[:, :, NOPE:]
    z = jnp.zeros((KVL, VH), w.dtype)
    cols = []
    for h in range(H):
        cols += [wv3[:, h], z] if h % 2 == 0 else [z, wv3[:, h]]
    return wkn, jnp.concatenate(cols, axis=1)


def _wukv_from_heads(wkn, wv):
    kn = jnp.transpose(wkn[:, :, :NOPE], (1, 0, 2))
    vs = jnp.stack([wv[:, LANES * h + VH * (h % 2):LANES * h + VH * (h % 2) + VH] for h in range(H)], axis=1)
    return jnp.concatenate([kn, vs], axis=2).reshape(KVL, H * (NOPE + VH))


def _layer_fwd(x, mem, tabs, p):
    proj, h = _proj_call(x, p["norm_g"], p["w_in"])
    q, k, v = _mla_prep_call(proj, tabs, p["cq_g"], p["ckv_g"], p["qg"], p["kg"], p["wuq"], p["wkn"], p["wv"])
    ya = _attn_call(q, k, v, proj)
    yb = _conv_call(proj, p["conv_w"], p["conv_b"])
    yc = _sg_call(proj, p["ln_g"], p["ln_b"], p["ws"], p["bs"])
    mk, mv = _memkv_call(mem, p["mem_g"], p["wm"], p["mkg"])
    yd = _mem_call(proj, mk, mv, p["mqg"])
    out = _merge_call((ya, yb, yc, yd), proj, p["bm"], p["wb"], p["wo"], x)
    return out, dict(x=x, proj=proj, h=h, q=q, k=k, v=v, ys=(ya, yb, yc, yd), mk=mk, mv=mv)


def _layer_bwd(dout, mem, tabs, p, sv):
    proj = sv["proj"]
    dya, dyb, dyc, dyd, dml, dbm, dwb, dwo = _merge_bwd_call(sv["ys"], proj, p["bm"], p["wb"], p["wo"], dout)
    dq, dk, dv, dsg_a = _attn_bwd_call(sv["q"], sv["k"], sv["v"], proj, dya)
    dcq, dckv, dkr, dcqg, dckvg, dqg, dkg, dwuq, dwkn, dwv = _mla_prep_bwd_call(
        proj, tabs, p["cq_g"], p["ckv_g"], p["qg"], p["kg"], p["wuq"], p["wkn"], p["wv"], dq, dk, dv)
    dbg, dcg, dxi, dsg_b, dcw, dcb = _conv_bwd_call(proj, p["conv_w"], p["conv_b"], dyb)
    du, dvv, dsg_c, dlg, dlb, dws, dbs = _sg_bwd_call(proj, p["ln_g"], p["ln_b"], p["ws"], p["bs"], dyc)
    dmq, dsg_d, dmk, dmv, dmqg = _mem_bwd_call(proj, sv["mk"], sv["mv"], p["mqg"], dyd)
    dmem_g, dwm, dmkg = _memkv_bwd_call(mem, p["mem_g"], p["wm"], p["mkg"], dmk, dmv)
    dproj = jnp.concatenate([dml, dsg_a, dsg_b, dsg_c, dsg_d, dbg, dcg, dxi, du, dvv, dmq, dcq, dckv, dkr], axis=1)
    dx, dnorm_g = _dh_call(dproj, p["w_in"], sv["x"], p["norm_g"], dout)
    dw_in = _dw_call(sv["h"], dproj)
    grads = dict(norm_g=dnorm_g[0], cq_norm_g=dcqg[0], ckv_norm_g=dckvg[0], mla_q_norm_g=dqg[0, :QKH], mla_k_norm_g=dkg[0, :QKH],
                 conv_w=dcw, conv_b=dcb[0], sg_ln_g=dlg[0], sg_ln_b=dlb[0], w_spatial=dws, b_spatial=dbs[:, :, 0],
                 mem_norm_g=dmem_g[0], mem_q_norm_g=dmqg[0], mem_k_norm_g=dmkg[0], b_merge=dbm,
                 w_in_aligned=dw_in, wuq_heads=dwuq, wkn_heads=dwkn, wv_heads=dwv, w_mem_kv=dwm, w_branch_chips=dwb, w_out=dwo)
    return dx, grads


def _chips_to_cols(a):
    return jnp.concatenate([a[j] for j in range(N_CHIPS)], axis=1)


def _cols_to_chips(a):
    cols = a.shape[1] // N_CHIPS
    return jnp.stack([a[:, cols * j:cols * (j + 1)] for j in range(N_CHIPS)])


def _layer_params(l, rep, gathered, conv_w, b_merge):
    pad_g = lambda g: jnp.pad(g, (0, LANES - QKH)).reshape(1, LANES)
    wkn, wv = _wukv_to_heads(_chips_to_cols(gathered["w_ukv"]))
    return dict(
        norm_g=rep["norm_g"][l].reshape(1, D), w_in=_w_in_to_aligned(_chips_to_cols(gathered["w_in"])),
        cq_g=rep["cq_norm_g"][l].reshape(1, QL), ckv_g=rep["ckv_norm_g"][l].reshape(1, KVL),
        qg=pad_g(rep["mla_q_norm_g"][l]), kg=pad_g(rep["mla_k_norm_g"][l]),
        wuq=_wuq_to_heads(_chips_to_cols(gathered["w_uq"])), wkn=wkn, wv=wv,
        conv_w=conv_w, conv_b=rep["conv_b"][l].reshape(1, CW),
        ln_g=rep["sg_ln_g"][l].reshape(1, SGW), ln_b=rep["sg_ln_b"][l].reshape(1, SGW),
        ws=rep["w_spatial"][l], bs=rep["b_spatial"][l].reshape(SGG, SGC, 1),
        mem_g=rep["mem_norm_g"][l].reshape(1, D), wm=gathered["w_mem_kv"].reshape(D, 2 * MH * MHD),
        mqg=rep["mem_q_norm_g"][l].reshape(1, MHD), mkg=rep["mem_k_norm_g"][l].reshape(1, MHD),
        bm=b_merge, wb=gathered["w_branch"], wo=gathered["w_out"].reshape(D, D))


def _forward_backward(x, mem, pos, target, params, on_layer_grads=None):
    tabs = _rope_tables(pos)
    saved = []
    act = x
    for l in range(DEPTH):
        act, sv = _layer_fwd(act, mem, tabs, params[l])
        saved.append(sv)
    dy, sq = _loss_call(act, target)
    grads = [None] * DEPTH
    for l in reversed(range(DEPTH)):
        dy, grads[l] = _layer_bwd(dy, mem, tabs, params[l], saved[l])
        if on_layer_grads is not None:
            grads[l] = on_layer_grads(l, grads[l])
    return sq, dy, grads


_SHARDED_MM = ("w_in", "w_branch", "w_out", "w_mem_kv", "w_uq", "w_ukv")
_SHARDED_F32 = ("conv_w", "b_merge")
_REPLICATED = ("norm_g", "cq_norm_g", "ckv_norm_g", "mla_q_norm_g", "mla_k_norm_g", "conv_b", "sg_ln_g", "sg_ln_b",
               "w_spatial", "b_spatial", "mem_norm_g", "mem_q_norm_g", "mem_k_norm_g")
_ALL_REDUCED = _REPLICATED + _SHARDED_F32
_WEIGHTS = ("norm_g", "w_in", "cq_norm_g", "ckv_norm_g", "w_uq", "w_ukv", "mla_q_norm_g", "mla_k_norm_g", "conv_w", "conv_b",
            "sg_ln_g", "sg_ln_b", "w_spatial", "b_spatial", "mem_norm_g", "w_mem_kv", "mem_q_norm_g", "mem_k_norm_g",
            "b_merge", "w_branch", "w_out")
_BIG = ("w_in", "w_uq", "w_ukv", "w_mem_kv", "w_branch", "w_out")
_SMALL = tuple(n for n in _WEIGHTS if n not in _BIG)


def _gather_small_sharded(w):
    names = _SHARDED_F32
    packed = _pack_rows([w[n] for n in names], F32, 8)
    got = _all_gather8(packed, "gather_small_weights")
    per_chip = [_unpack(got[2 * j].reshape(-1), [w[n].shape for n in names]) for j in range(N_CHIPS)]
    return {n: jnp.concatenate([per_chip[j][t] for j in range(N_CHIPS)], axis=2) for t, n in enumerate(names)}


def _gather_layer(l, shards):
    srcs = [shards[n] for n in _SHARDED_MM]
    return dict(zip(_SHARDED_MM, _gather_layer_call(l, srcs, "gather_weights_l%d" % l)))


def _reduce_scatter_layer(l, grads):
    x, y, c = lax.axis_index("x"), lax.axis_index("y"), lax.axis_index("c")
    core = c.astype(jnp.int32).reshape(1)
    chip = (2 * x + y).astype(jnp.int32).reshape(1)
    tensors = [
        _cols_to_chips(_w_in_from_aligned(grads["w_in_aligned"])),
        grads["w_branch_chips"].reshape(N_CHIPS, NB * BW, D // N_CHIPS),
        grads["w_out"].reshape(N_CHIPS, D // N_CHIPS, D),
        grads["w_mem_kv"].reshape(N_CHIPS, D // N_CHIPS, 2 * MH * MHD),
        _cols_to_chips(_wuq_from_heads(grads["wuq_heads"])),
        _cols_to_chips(_wukv_from_heads(grads["wkn_heads"], grads["wv_heads"])),
    ]
    tag = "rs_l%d_" % l
    from_sibling = _pair_exchange_call(tensors, tag + "pair_exchange")
    chip_sums = _pair_sum_call(tensors, from_sibling, core, tag + "pair_sum")
    from_chips = _chip_scatter_call(chip_sums, tag + "chip_scatter")
    halves = _owner_sum_call(chip_sums, from_chips, chip, tag + "owner_sum")
    shard = dict(zip(_SHARDED_MM, _pair_gather_call(halves, tag + "pair_gather")))
    shard["w_branch"] = shard["w_branch"].reshape(NB, BW, D // N_CHIPS)
    small = {n: grads[n] for n in _ALL_REDUCED}
    return dict(shard, **small)


def _all_reduce_small(g):
    packed = _pack_rows([g[n] for n in _ALL_REDUCED], F32, 64)
    got = _all_gather8(packed, "gather_small_grads")
    total = _sum8_call(got).reshape(-1)
    out = dict(zip(_ALL_REDUCED, _unpack(total, [g[n].shape for n in _ALL_REDUCED])))
    chip = 2 * lax.axis_index("x") + lax.axis_index("y")
    for n in _SHARDED_F32:
        size = out[n].shape[2] // N_CHIPS
        out[n] = lax.dynamic_slice_in_dim(out[n], chip * size, size, axis=2)
    return out


def _adamw_all(w, g, m, v):
    delta, new_m, new_v = {}, {}, {}
    for n in _BIG:
        shp = w[n].shape
        as2d = lambda a: a.reshape(-1, shp[-1])
        d, nm, nv = _adamw_call(as2d(w[n]), as2d(g[n]), as2d(m[n]), as2d(v[n]), "adamw_" + n)
        delta[n], new_m[n], new_v[n] = d.reshape(shp), nm.reshape(shp), nv.reshape(shp)
    shapes = [w[n].shape for n in _SMALL]
    pk = lambda t: _pack_rows([t[n] for n in _SMALL], F32, 64)
    d, nm, nv = _adamw_call(pk(w), pk(g), pk(m), pk(v), "adamw_small")
    for out, packed in ((delta, d), (new_m, nm), (new_v, nv)):
        out.update(zip(_SMALL, _unpack(packed.reshape(-1), shapes)))
    return delta, new_m, new_v


def kernel(x, mem, positions, norm_g, w_in, cq_norm_g, ckv_norm_g, w_uq, w_ukv, mla_q_norm_g, mla_k_norm_g, conv_w, conv_b, sg_ln_g, sg_ln_b, w_spatial, b_spatial, mem_norm_g, w_mem_kv, mem_q_norm_g, mem_k_norm_g, b_merge, w_branch, w_out, loss_target, m_norm_g, m_w_in, m_cq_norm_g, m_ckv_norm_g, m_w_uq, m_w_ukv, m_mla_q_norm_g, m_mla_k_norm_g, m_conv_w, m_conv_b, m_sg_ln_g, m_sg_ln_b, m_w_spatial, m_b_spatial, m_mem_norm_g, m_w_mem_kv, m_mem_q_norm_g, m_mem_k_norm_g, m_b_merge, m_w_branch, m_w_out, v_norm_g, v_w_in, v_cq_norm_g, v_ckv_norm_g, v_w_uq, v_w_ukv, v_mla_q_norm_g, v_mla_k_norm_g, v_conv_w, v_conv_b, v_sg_ln_g, v_sg_ln_b, v_w_spatial, v_b_spatial, v_mem_norm_g, v_w_mem_kv, v_mem_q_norm_g, v_mem_k_norm_g, v_b_merge, v_w_branch, v_w_out):
    w = dict(norm_g=norm_g, w_in=w_in, cq_norm_g=cq_norm_g, ckv_norm_g=ckv_norm_g, w_uq=w_uq, w_ukv=w_ukv,
             mla_q_norm_g=mla_q_norm_g, mla_k_norm_g=mla_k_norm_g, conv_w=conv_w, conv_b=conv_b, sg_ln_g=sg_ln_g,
             sg_ln_b=sg_ln_b, w_spatial=w_spatial, b_spatial=b_spatial, mem_norm_g=mem_norm_g, w_mem_kv=w_mem_kv,
             mem_q_norm_g=mem_q_norm_g, mem_k_norm_g=mem_k_norm_g, b_merge=b_merge, w_branch=w_branch, w_out=w_out)
    m = dict(norm_g=m_norm_g, w_in=m_w_in, cq_norm_g=m_cq_norm_g, ckv_norm_g=m_ckv_norm_g, w_uq=m_w_uq, w_ukv=m_w_ukv,
             mla_q_norm_g=m_mla_q_norm_g, mla_k_norm_g=m_mla_k_norm_g, conv_w=m_conv_w, conv_b=m_conv_b, sg_ln_g=m_sg_ln_g,
             sg_ln_b=m_sg_ln_b, w_spatial=m_w_spatial, b_spatial=m_b_spatial, mem_norm_g=m_mem_norm_g, w_mem_kv=m_w_mem_kv,
             mem_q_norm_g=m_mem_q_norm_g, mem_k_norm_g=m_mem_k_norm_g, b_merge=m_b_merge, w_branch=m_w_branch, w_out=m_w_out)
    v = dict(norm_g=v_norm_g, w_in=v_w_in, cq_norm_g=v_cq_norm_g, ckv_norm_g=v_ckv_norm_g, w_uq=v_w_uq, w_ukv=v_w_ukv,
             mla_q_norm_g=v_mla_q_norm_g, mla_k_norm_g=v_mla_k_norm_g, conv_w=v_conv_w, conv_b=v_conv_b, sg_ln_g=v_sg_ln_g,
             sg_ln_b=v_sg_ln_b, w_spatial=v_w_spatial, b_spatial=v_b_spatial, mem_norm_g=v_mem_norm_g, w_mem_kv=v_w_mem_kv,
             mem_q_norm_g=v_mem_q_norm_g, mem_k_norm_g=v_mem_k_norm_g, b_merge=v_b_merge, w_branch=v_w_branch, w_out=v_w_out)

    small = _gather_small_sharded(w)
    shards = {n: w[n].astype(MM) for n in _SHARDED_MM}
    params = [_layer_params(l, w, _gather_layer(l, shards), small["conv_w"][l], small["b_merge"][l]) for l in range(DEPTH)]
    sq, grad_x, layer_grads = _forward_backward(x[0], mem[0], positions[0], loss_target[0], params, _reduce_scatter_layer)
    loss = lax.psum(0.5 / D * jnp.sum(sq), ("x", "y", "c"))

    g = {n: jnp.stack([layer_grads[l][n] for l in range(DEPTH)]) for n in _WEIGHTS}
    g.update(_all_reduce_small(g))
    delta, new_m, new_v = _adamw_all(w, g, m, v)
    return (loss, grad_x[None], *[g[n] for n in _WEIGHTS], *[delta[n] for n in _WEIGHTS],
            *[new_m[n] for n in _WEIGHTS], *[new_v[n] for n in _WEIGHTS])
```

```python
import functools
import math

import jax
import jax.numpy as jnp
from jax import lax
from jax.experimental import pallas as pl
from jax.experimental.pallas import tpu as pltpu

F32 = jnp.float32
MM = jnp.bfloat16

D = 1024
DEPTH = 2
EPS = 1e-6
H = 8
NOPE = 64
ROPE = 32
QKH = 96
VH = 64
QL = 256
KVL = 128
ROPE_THETA = 10000.0
CW = 512
SGW = 512
SGG = 4
SGC = 128
MH = 4
MHD = 128
NB = 4
BW = 512
NEG_INF = -1e30
LANES = 128
N_CHIPS = 4

R_CQ, R_CKV, R_KR, R_CV, R_SGI, R_MQ, R_SG, R_ML, R_END = 0, 256, 384, 416, 1952, 2976, 3488, 5536, 9632
OFF_ML, OFF_SG, OFF_CV, OFF_SGI, OFF_MQ, OFF_CQ, OFF_CKV, OFF_KR, NP = 0, 4096, 6144, 7680, 8704, 9216, 9472, 9600, 9728

ADAM_LR = 0.001
ADAM_B1 = 0.9
ADAM_B2 = 0.999
ADAM_EPS = 1e-08
ADAM_WD = 0.01
ADAM_STEP = 10

VMEM_LIMIT = 56 * 1024 * 1024
PACK_W = 512
MESH_ID = pl.DeviceIdType.MESH


def _cparams(n_axes):
    return pltpu.CompilerParams(dimension_semantics=("arbitrary",) * n_axes, vmem_limit_bytes=VMEM_LIMIT)


def _bs(shape, imap):
    return pl.BlockSpec(shape, imap)


@jax.custom_vjp
def _mm(a, b):
    return jnp.dot(a.astype(MM), b.astype(MM), preferred_element_type=F32)


def _mm_fwd(a, b):
    return _mm(a, b), (a, b)


def _mm_bwd(res, g):
    a, b = res
    gm = g.astype(MM)
    da = lax.dot_general(gm, b.astype(MM), (((1,), (1,)), ((), ())), preferred_element_type=F32)
    db = lax.dot_general(a.astype(MM), gm, (((0,), (0,)), ((), ())), preferred_element_type=F32)
    return da.astype(a.dtype), db.astype(b.dtype)


_mm.defvjp(_mm_fwd, _mm_bwd)


@jax.custom_vjp
def _mm_nt(a, b):
    return lax.dot_general(a.astype(MM), b.astype(MM), (((1,), (1,)), ((), ())), preferred_element_type=F32)


def _mm_nt_fwd(a, b):
    return _mm_nt(a, b), (a, b)


def _mm_nt_bwd(res, g):
    a, b = res
    gm = g.astype(MM)
    da = jnp.dot(gm, b.astype(MM), preferred_element_type=F32)
    db = lax.dot_general(gm, a.astype(MM), (((0,), (0,)), ((), ())), preferred_element_type=F32)
    return da.astype(a.dtype), db.astype(b.dtype)


_mm_nt.defvjp(_mm_nt_fwd, _mm_nt_bwd)


@functools.partial(jax.custom_vjp, nondiff_argnums=(1,))
def _lane_roll(x, shift):
    return pltpu.roll(x, shift, 1)


def _lane_roll_fwd(x, shift):
    return pltpu.roll(x, shift, 1), None


def _lane_roll_bwd(shift, _, g):
    return (pltpu.roll(g, (LANES - shift) % LANES, 1),)


_lane_roll.defvjp(_lane_roll_fwd, _lane_roll_bwd)


def _rms_n(x, g, n):
    ms = jnp.sum(x * x, axis=-1, keepdims=True) * (1.0 / n)
    return x * lax.rsqrt(ms + EPS) * g


def _softmax(s):
    m = jnp.max(s, axis=-1, keepdims=True)
    e = jnp.exp(s - m)
    return e / jnp.sum(e, axis=-1, keepdims=True)


def _rope(t, cos_t, sin_a, sin_b):
    return t * cos_t + _lane_roll(t, LANES - 16) * sin_a + _lane_roll(t, 16) * sin_b


def _mla_prep_fn(cq, ckv, kr, cos_t, sin_a, sin_b, cq_g, ckv_g, qg, kg, wuq, wkn, wv):
    cqn = _rms_n(cq, cq_g, QL)
    ckvn = _rms_n(ckv, ckv_g, KVL)
    lane = lax.broadcasted_iota(jnp.int32, kr.shape, 1)
    krm = jnp.where((lane >= NOPE) & (lane < QKH), kr, 0.0)
    qs, ks = [], []
    for h in range(H):
        qh = _rms_n(_mm(cqn, wuq[h]), qg, QKH)
        qs.append(_rope(qh, cos_t, sin_a, sin_b))
        kh = _rms_n(_mm(ckvn, wkn[h]) + krm, kg, QKH)
        ks.append(_rope(kh, cos_t, sin_a, sin_b))
    return jnp.concatenate(qs, axis=-1), jnp.concatenate(ks, axis=-1), _mm(ckvn, wv)


def _attn_pair_fn(q2, k2, v2, sg, row0):
    tq, s_len = q2.shape[0], k2.shape[0]
    rows = row0 + lax.broadcasted_iota(jnp.int32, (tq, s_len), 0)
    cols = lax.broadcasted_iota(jnp.int32, (tq, s_len), 1)
    mask = cols <= rows
    vlane = lax.broadcasted_iota(jnp.int32, (s_len, LANES), 1)
    o = jnp.zeros((tq, LANES), F32)
    for e in range(2):
        sl = slice(LANES * e, LANES * (e + 1))
        s = _mm_nt(q2[:, sl], k2[:, sl]) * (QKH ** -0.5)
        p = _softmax(jnp.where(mask, s, NEG_INF))
        ve = jnp.where((vlane >= VH * e) & (vlane < VH * (e + 1)), v2[:, sl], 0.0)
        o = o + _mm(p, ve)
    return o * jax.nn.silu(sg)


def _sg_fn(u, v, sgc, ln_g, ln_b, ws, bs):
    mu = jnp.mean(v, axis=-1, keepdims=True)
    xc = v - mu
    vn = xc * lax.rsqrt(jnp.mean(xc * xc, axis=-1, keepdims=True) + EPS) * ln_g + ln_b
    r = lax.broadcasted_iota(jnp.int32, (SGC, SGC), 0)
    c = lax.broadcasted_iota(jnp.int32, (SGC, SGC), 1)
    wt = [jnp.where(r >= c, w, 0.0) for w in ws]
    row_blocks = []
    for ch in range(u.shape[0] // SGC):
        col_blocks = []
        for g in range(SGG):
            blk = vn[SGC * ch:SGC * (ch + 1), LANES * g:LANES * (g + 1)]
            col_blocks.append(_mm(wt[g], blk) + bs[g])
        row_blocks.append(jnp.concatenate(col_blocks, axis=-1))
    mixed = jnp.concatenate(row_blocks, axis=0)
    return (u * mixed) * jax.nn.silu(sgc)


def _memkv_fn(mem, mem_g, wm, kg):
    kv = _mm(_rms_n(mem, mem_g, D), wm)
    ks = [_rms_n(kv[:, MHD * h:MHD * (h + 1)], kg, MHD) for h in range(MH)]
    return jnp.concatenate(ks, axis=-1), kv[:, MH * MHD:]


def _mem_fn(mq, sgd, k, v, qg):
    outs = []
    for h in range(MH):
        sl = slice(MHD * h, MHD * (h + 1))
        qh = _rms_n(mq[:, sl], qg, MHD)
        p = _softmax(_mm_nt(qh, k[:, sl]) * (MHD ** -0.5))
        outs.append(_mm(p, v[:, sl]))
    return jnp.concatenate(outs, axis=-1) * jax.nn.silu(sgd)


def _merge_fn(ys, logits, bm, wb, wo):
    merged = None
    for n in range(NB):
        z = jnp.concatenate([_mm(ys[n], wb[j][n]) for j in range(N_CHIPS)], axis=-1)
        gate = jax.nn.sigmoid(logits[:, D * n:D * (n + 1)] + bm[n])
        merged = gate * z if merged is None else merged + gate * z
    return _mm(merged, wo)


def _proj_call(x, g, w):
    s_len = x.shape[0]
    tm, tn = min(s_len, 1024), 512

    def body(x_ref, g_ref, w_ref, p_ref, h_ref):
        @pl.when(pl.program_id(1) == 0)
        def _():
            h_ref[...] = _rms_n(x_ref[...], g_ref[...], D).astype(h_ref.dtype)
        p_ref[...] = jnp.dot(h_ref[...], w_ref[...], preferred_element_type=F32)

    return pl.pallas_call(
        body, grid=(s_len // tm, NP // tn),
        in_specs=[_bs((tm, D), lambda i, j: (i, 0)), _bs((1, D), lambda i, j: (0, 0)), _bs((D, tn), lambda i, j: (0, j))],
        out_specs=[_bs((tm, tn), lambda i, j: (i, j)), _bs((tm, D), lambda i, j: (i, 0))],
        out_shape=[jax.ShapeDtypeStruct((s_len, NP), F32), jax.ShapeDtypeStruct((s_len, D), MM)],
        name="proj", compiler_params=_cparams(2))(x, g, w)


def _rope_tables(pos):
    half = ROPE // 2
    inv_freq = ROPE_THETA ** (-jnp.arange(half, dtype=F32) / half)
    ang = pos.astype(F32)[:, None] * inv_freq
    cos, sin = jnp.cos(ang), jnp.sin(ang)
    s_len = pos.shape[0]
    z = lambda n: jnp.zeros((s_len, n), F32)
    cos_t = jnp.concatenate([jnp.ones((s_len, NOPE), F32), cos, cos, z(LANES - QKH)], axis=1)
    sin_a = jnp.concatenate([z(NOPE), -sin, z(LANES - NOPE - half)], axis=1)
    sin_b = jnp.concatenate([z(NOPE + half), sin, z(LANES - QKH)], axis=1)
    return cos_t, sin_a, sin_b


def _mla_prep_specs(tm):
    row = lambda w, off: _bs((tm, w), lambda i: (i, off // w))
    full2 = lambda a, b: _bs((a, b), lambda i: (0, 0))
    full3 = lambda a, b, c: _bs((a, b, c), lambda i: (0, 0, 0))
    tab = _bs((tm, LANES), lambda i: (i, 0))
    return [row(QL, OFF_CQ), row(KVL, OFF_CKV), row(LANES, OFF_KR), tab, tab, tab,
            full2(1, QL), full2(1, KVL), full2(1, LANES), full2(1, LANES),
            full3(H, QL, LANES), full3(H, KVL, LANES), full2(KVL, H * LANES)]


def _mla_prep_args(body_refs, wdtype=None):
    (cq, ckv, kr, ct, sa, sb, cqg, ckvg, qg, kg, wuq, wkn, wv) = body_refs
    cast = (lambda a: a) if wdtype is None else (lambda a: a.astype(wdtype))
    return (cq[...], ckv[...], kr[...], ct[...], sa[...], sb[...], cqg[...], ckvg[...], qg[...], kg[...],
            [cast(wuq[h]) for h in range(H)], [cast(wkn[h]) for h in range(H)], cast(wv[...]))


def _mla_prep_call(proj, tabs, cq_g, ckv_g, qg, kg, wuq, wkn, wv):
    s_len = proj.shape[0]
    tm = min(s_len, 256)

    def body(*refs):
        q_ref, k_ref, v_ref = refs[13:]
        q, k, v = _mla_prep_fn(*_mla_prep_args(refs[:13]))
        q_ref[...] = q.astype(q_ref.dtype)
        k_ref[...] = k.astype(k_ref.dtype)
        v_ref[...] = v.astype(v_ref.dtype)

    out = _bs((tm, H * LANES), lambda i: (i, 0))
    return pl.pallas_call(
        body, grid=(s_len // tm,), in_specs=_mla_prep_specs(tm), out_specs=[out, out, out],
        out_shape=[jax.ShapeDtypeStruct((s_len, H * LANES), MM)] * 3,
        name="mla_prep", compiler_params=_cparams(1))(proj, proj, proj, *tabs, cq_g, ckv_g, qg, kg, wuq, wkn, wv)


def _mla_prep_bwd_call(proj, tabs, cq_g, ckv_g, qg, kg, wuq, wkn, wv, dq, dk, dv):
    s_len = proj.shape[0]
    tm = min(s_len, 256)

    def body(*refs):
        dq_ref, dk_ref, dv_ref = refs[13:16]
        dcq_ref, dckv_ref, dkr_ref, dcqg_ref, dckvg_ref, dqg_ref, dkg_ref, dwuq_ref, dwkn_ref, dwv_ref = refs[16:]
        _, vjp = jax.vjp(_mla_prep_fn, *_mla_prep_args(refs[:13], F32))
        (dcq, dckv, dkr, _, _, _, dcqg, dckvg, dqg, dkg, dwuq, dwkn, dwv) = vjp((dq_ref[...], dk_ref[...], dv_ref[...]))
        dcq_ref[...] = dcq.astype(dcq_ref.dtype)
        dckv_ref[...] = dckv.astype(dckv_ref.dtype)
        dkr_ref[...] = dkr.astype(dkr_ref.dtype)

        @pl.when(pl.program_id(0) == 0)
        def _():
            for r in (dcqg_ref, dckvg_ref, dqg_ref, dkg_ref, dwuq_ref, dwkn_ref, dwv_ref):
                r[...] = jnp.zeros_like(r)
        dcqg_ref[...] += dcqg
        dckvg_ref[...] += dckvg
        dqg_ref[...] += dqg
        dkg_ref[...] += dkg
        for h in range(H):
            dwuq_ref[h] += dwuq[h]
            dwkn_ref[h] += dwkn[h]
        dwv_ref[...] += dwv

    big = _bs((tm, H * LANES), lambda i: (i, 0))
    row = lambda w: _bs((tm, w), lambda i: (i, 0))
    full2 = lambda a, b: _bs((a, b), lambda i: (0, 0))
    full3 = lambda a, b, c: _bs((a, b, c), lambda i: (0, 0, 0))
    sd = jax.ShapeDtypeStruct
    return pl.pallas_call(
        body, grid=(s_len // tm,), in_specs=_mla_prep_specs(tm) + [big, big, big],
        out_specs=[row(QL), row(KVL), row(LANES), full2(1, QL), full2(1, KVL), full2(1, LANES), full2(1, LANES),
                   full3(H, QL, LANES), full3(H, KVL, LANES), full2(KVL, H * LANES)],
        out_shape=[sd((s_len, QL), MM), sd((s_len, KVL), MM), sd((s_len, LANES), MM), sd((1, QL), F32), sd((1, KVL), F32),
                   sd((1, LANES), F32), sd((1, LANES), F32), sd((H, QL, LANES), F32), sd((H, KVL, LANES), F32),
                   sd((KVL, H * LANES), F32)],
        name="mla_prep_bwd", compiler_params=_cparams(1))(proj, proj, proj, *tabs, cq_g, ckv_g, qg, kg, wuq, wkn, wv, dq, dk, dv)


def _attn_specs(s_len, tq):
    pair = 2 * LANES
    return [_bs((tq, pair), lambda p, i: (i, p)), _bs((s_len, pair), lambda p, i: (0, p)), _bs((s_len, pair), lambda p, i: (0, p)),
            _bs((tq, LANES), lambda p, i: (i, OFF_SG // LANES + p))]


def _attn_call(q, k, v, proj):
    s_len = q.shape[0]
    tq = min(s_len, 256)

    def body(q_ref, k_ref, v_ref, sg_ref, y_ref):
        for n in range(s_len // tq):
            @pl.when(pl.program_id(1) == n)
            def _():
                kl = (n + 1) * tq
                y_ref[...] = _attn_pair_fn(q_ref[...], k_ref[:kl, :], v_ref[:kl, :], sg_ref[...], n * tq).astype(y_ref.dtype)

    return pl.pallas_call(
        body, grid=(H // 2, s_len // tq), in_specs=_attn_specs(s_len, tq),
        out_specs=_bs((tq, LANES), lambda p, i: (i, p)), out_shape=jax.ShapeDtypeStruct((s_len, BW), MM),
        name="attn", compiler_params=_cparams(2))(q, k, v, proj)


def _attn_bwd_call(q, k, v, proj, dys):
    s_len = q.shape[0]
    tq = min(s_len, 256)
    pair = 2 * LANES

    def body(q_ref, k_ref, v_ref, sg_ref, dy_ref, dq_ref, dk_ref, dv_ref, dsg_ref):
        i = pl.program_id(1)

        @pl.when(i == 0)
        def _():
            dk_ref[...] = jnp.zeros_like(dk_ref)
            dv_ref[...] = jnp.zeros_like(dv_ref)

        for n in range(s_len // tq):
            @pl.when(i == n)
            def _():
                kl = (n + 1) * tq
                fn = functools.partial(_attn_pair_fn, row0=n * tq)
                _, vjp = jax.vjp(fn, q_ref[...].astype(F32), k_ref[:kl, :].astype(F32), v_ref[:kl, :].astype(F32), sg_ref[...])
                dq, dk, dv, dsg = vjp(dy_ref[...])
                dq_ref[...] = dq
                dsg_ref[...] = dsg.astype(dsg_ref.dtype)
                dk_ref[:kl, :] += dk
                dv_ref[:kl, :] += dv

    sd = jax.ShapeDtypeStruct
    return pl.pallas_call(
        body, grid=(H // 2, s_len // tq),
        in_specs=_attn_specs(s_len, tq) + [_bs((tq, LANES), lambda p, i: (i, p))],
        out_specs=[_bs((tq, pair), lambda p, i: (i, p)), _bs((s_len, pair), lambda p, i: (0, p)),
                   _bs((s_len, pair), lambda p, i: (0, p)), _bs((tq, LANES), lambda p, i: (i, p))],
        out_shape=[sd((s_len, H * LANES), F32), sd((s_len, H * LANES), F32), sd((s_len, H * LANES), F32), sd((s_len, BW), MM)],
        name="attn_bwd", compiler_params=_cparams(2))(q, k, v, proj, dys)


def _shift_down(a, n):
    r = lax.broadcasted_iota(jnp.int32, a.shape, 0)
    return jnp.where(r >= n, pltpu.roll(a, n, 0), 0.0)


def _shift_up(a, n):
    s_len = a.shape[0]
    r = lax.broadcasted_iota(jnp.int32, a.shape, 0)
    return jnp.where(r < s_len - n, pltpu.roll(a, s_len - n, 0), 0.0)


def _conv_specs(s_len):
    col = lambda off: _bs((s_len, LANES), lambda j: (0, off // LANES + j))
    return [col(OFF_CV), col(OFF_CV + CW), col(OFF_CV + 2 * CW), col(OFF_SG + BW),
            _bs((3, LANES), lambda j: (0, j)), _bs((1, LANES), lambda j: (0, j))]


def _conv_call(proj, cw, cb):
    s_len = proj.shape[0]

    def body(bg_ref, cg_ref, xi_ref, sg_ref, w_ref, b_ref, y_ref):
        z = cg_ref[...] * xi_ref[...]
        y = b_ref[...] + w_ref[0:1, :] * _shift_down(z, 2)
        y = y + w_ref[1:2, :] * _shift_down(z, 1)
        y = y + w_ref[2:3, :] * z
        y_ref[...] = ((bg_ref[...] * y) * jax.nn.silu(sg_ref[...])).astype(y_ref.dtype)

    return pl.pallas_call(
        body, grid=(CW // LANES,), in_specs=_conv_specs(s_len), out_specs=_bs((s_len, LANES), lambda j: (0, j)),
        out_shape=jax.ShapeDtypeStruct((s_len, CW), MM), name="conv", compiler_params=_cparams(1))(proj, proj, proj, proj, cw, cb)


def _conv_bwd_call(proj, cw, cb, dys):
    s_len = proj.shape[0]

    def body(bg_ref, cg_ref, xi_ref, sg_ref, w_ref, b_ref, dys_ref, dbg_ref, dcg_ref, dxi_ref, dsg_ref, dw_ref, db_ref):
        bg, cg, xi, sg = bg_ref[...], cg_ref[...], xi_ref[...], sg_ref[...]
        w0, w1, w2 = w_ref[0:1, :], w_ref[1:2, :], w_ref[2:3, :]
        z = cg * xi
        z1, z2 = _shift_down(z, 1), _shift_down(z, 2)
        y = b_ref[...] + w0 * z2
        y = y + w1 * z1
        y = y + w2 * z
        yb = bg * y
        sig = jax.nn.sigmoid(sg)
        silu = sg * sig
        dys_v = dys_ref[...]
        dsg_ref[...] = (dys_v * yb * (sig * (1.0 + sg * (1.0 - sig)))).astype(dsg_ref.dtype)
        dyb = dys_v * silu
        dbg_ref[...] = (dyb * y).astype(dbg_ref.dtype)
        dy = dyb * bg
        db_ref[...] = jnp.sum(dy, axis=0, keepdims=True)
        dw_ref[0:1, :] = jnp.sum(dy * z2, axis=0, keepdims=True)
        dw_ref[1:2, :] = jnp.sum(dy * z1, axis=0, keepdims=True)
        dw_ref[2:3, :] = jnp.sum(dy * z, axis=0, keepdims=True)
        dz = w2 * dy + w1 * _shift_up(dy, 1) + w0 * _shift_up(dy, 2)
        dcg_ref[...] = (dz * xi).astype(dcg_ref.dtype)
        dxi_ref[...] = (dz * cg).astype(dxi_ref.dtype)

    col = _bs((s_len, LANES), lambda j: (0, j))
    sd = jax.ShapeDtypeStruct
    return pl.pallas_call(
        body, grid=(CW // LANES,), in_specs=_conv_specs(s_len) + [col],
        out_specs=[col, col, col, col, _bs((3, LANES), lambda j: (0, j)), _bs((1, LANES), lambda j: (0, j))],
        out_shape=[sd((s_len, CW), MM)] * 4 + [sd((3, CW), F32), sd((1, CW), F32)],
        name="conv_bwd", compiler_params=_cparams(1))(proj, proj, proj, proj, cw, cb, dys)


def _sg_specs(tm):
    row = lambda off: _bs((tm, SGW), lambda i: (i, off // SGW))
    return [row(OFF_SGI), row(OFF_SGI + SGW), row(OFF_SG + 2 * BW), _bs((1, SGW), lambda i: (0, 0)), _bs((1, SGW), lambda i: (0, 0)),
            _bs((SGG, SGC, SGC), lambda i: (0, 0, 0)), _bs((SGG, SGC, 1), lambda i: (0, 0, 0))]


def _sg_args(refs):
    u, v, sg, lg, lb, ws, bs = refs
    return (u[...], v[...], sg[...], lg[...], lb[...], [ws[g] for g in range(SGG)], [bs[g] for g in range(SGG)])


def _sg_call(proj, ln_g, ln_b, ws, bs):
    s_len = proj.shape[0]
    tm = min(s_len, 256)

    def body(*refs):
        refs[7][...] = _sg_fn(*_sg_args(refs[:7])).astype(refs[7].dtype)

    return pl.pallas_call(
        body, grid=(s_len // tm,), in_specs=_sg_specs(tm), out_specs=_bs((tm, SGW), lambda i: (i, 0)),
        out_shape=jax.ShapeDtypeStruct((s_len, SGW), MM), name="sgmlp", compiler_params=_cparams(1))(proj, proj, proj, ln_g, ln_b, ws, bs)


def _sg_bwd_call(proj, ln_g, ln_b, ws, bs, dys):
    s_len = proj.shape[0]
    tm = min(s_len, 256)

    def body(*refs):
        dys_ref = refs[7]
        du_ref, dv_ref, dsg_ref, dlg_ref, dlb_ref, dws_ref, dbs_ref = refs[8:]
        _, vjp = jax.vjp(_sg_fn, *_sg_args(refs[:7]))
        du, dv, dsg, dlg, dlb, dws, dbs = vjp(dys_ref[...])
        du_ref[...] = du.astype(du_ref.dtype)
        dv_ref[...] = dv.astype(dv_ref.dtype)
        dsg_ref[...] = dsg.astype(dsg_ref.dtype)

        @pl.when(pl.program_id(0) == 0)
        def _():
            for r in (dlg_ref, dlb_ref, dws_ref, dbs_ref):
                r[...] = jnp.zeros_like(r)
        dlg_ref[...] += dlg
        dlb_ref[...] += dlb
        for g in range(SGG):
            dws_ref[g] += dws[g]
            dbs_ref[g] += dbs[g]

    row = _bs((tm, SGW), lambda i: (i, 0))
    sd = jax.ShapeDtypeStruct
    return pl.pallas_call(
        body, grid=(s_len // tm,), in_specs=_sg_specs(tm) + [row],
        out_specs=[row, row, row, _bs((1, SGW), lambda i: (0, 0)), _bs((1, SGW), lambda i: (0, 0)),
                   _bs((SGG, SGC, SGC), lambda i: (0, 0, 0)), _bs((SGG, SGC, 1), lambda i: (0, 0, 0))],
        out_shape=[sd((s_len, SGW), MM)] * 3 + [sd((1, SGW), F32), sd((1, SGW), F32), sd((SGG, SGC, SGC), F32), sd((SGG, SGC, 1), F32)],
        name="sgmlp_bwd", compiler_params=_cparams(1))(proj, proj, proj, ln_g, ln_b, ws, bs, dys)


def _memkv_call(mem, mem_g, wm, kg):
    m_len = mem.shape[0]

    def body(mem_ref, g_ref, w_ref, kg_ref, k_ref, v_ref):
        k, v = _memkv_fn(mem_ref[...], g_ref[...], w_ref[...], kg_ref[...])
        k_ref[...] = k.astype(k_ref.dtype)
        v_ref[...] = v.astype(v_ref.dtype)

    return pl.pallas_call(body, out_shape=[jax.ShapeDtypeStruct((m_len, MH * MHD), MM)] * 2, name="memkv",
                          compiler_params=pltpu.CompilerParams(vmem_limit_bytes=VMEM_LIMIT))(mem, mem_g, wm, kg)


def _memkv_bwd_call(mem, mem_g, wm, kg, dk, dv):
    def body(mem_ref, g_ref, w_ref, kg_ref, dk_ref, dv_ref, dg_ref, dw_ref, dkg_ref):
        _, vjp = jax.vjp(_memkv_fn, mem_ref[...], g_ref[...], w_ref[...].astype(F32), kg_ref[...])
        _, dg, dw, dkg = vjp((dk_ref[...], dv_ref[...]))
        dg_ref[...] = dg
        dw_ref[...] = dw
        dkg_ref[...] = dkg

    sd = jax.ShapeDtypeStruct
    return pl.pallas_call(body, out_shape=[sd((1, D), F32), sd((D, 2 * MH * MHD), F32), sd((1, MHD), F32)], name="memkv_bwd",
                          compiler_params=pltpu.CompilerParams(vmem_limit_bytes=VMEM_LIMIT))(mem, mem_g, wm, kg, dk, dv)


def _mem_specs(tm, m_len):
    w = MH * MHD
    return [_bs((tm, w), lambda i: (i, OFF_MQ // w)), _bs((tm, BW), lambda i: (i, (OFF_SG + 3 * BW) // BW)),
            _bs((m_len, w), lambda i: (0, 0)), _bs((m_len, w), lambda i: (0, 0)), _bs((1, MHD), lambda i: (0, 0))]


def _mem_call(proj, k, v, qg):
    s_len, m_len = proj.shape[0], k.shape[0]
    tm = min(s_len, 256)

    def body(mq_ref, sg_ref, k_ref, v_ref, qg_ref, y_ref):
        y_ref[...] = _mem_fn(mq_ref[...], sg_ref[...], k_ref[...], v_ref[...], qg_ref[...]).astype(y_ref.dtype)

    return pl.pallas_call(
        body, grid=(s_len // tm,), in_specs=_mem_specs(tm, m_len), out_specs=_bs((tm, BW), lambda i: (i, 0)),
        out_shape=jax.ShapeDtypeStruct((s_len, BW), MM), name="memattn", compiler_params=_cparams(1))(proj, proj, k, v, qg)


def _mem_bwd_call(proj, k, v, qg, dys):
    s_len, m_len = proj.shape[0], k.shape[0]
    tm = min(s_len, 256)
    w = MH * MHD

    def body(mq_ref, sg_ref, k_ref, v_ref, qg_ref, dys_ref, dmq_ref, dsg_ref, dk_ref, dv_ref, dqg_ref):
        _, vjp = jax.vjp(_mem_fn, mq_ref[...], sg_ref[...], k_ref[...].astype(F32), v_ref[...].astype(F32), qg_ref[...])
        dmq, dsg, dk, dv, dqg = vjp(dys_ref[...])
        dmq_ref[...] = dmq.astype(dmq_ref.dtype)
        dsg_ref[...] = dsg.astype(dsg_ref.dtype)

        @pl.when(pl.program_id(0) == 0)
        def _():
            for r in (dk_ref, dv_ref, dqg_ref):
                r[...] = jnp.zeros_like(r)
        dk_ref[...] += dk
        dv_ref[...] += dv
        dqg_ref[...] += dqg

    row = _bs((tm, BW), lambda i: (i, 0))
    kv = _bs((m_len, w), lambda i: (0, 0))
    sd = jax.ShapeDtypeStruct
    return pl.pallas_call(
        body, grid=(s_len // tm,), in_specs=_mem_specs(tm, m_len) + [row],
        out_specs=[row, row, kv, kv, _bs((1, MHD), lambda i: (0, 0))],
        out_shape=[sd((s_len, w), MM), sd((s_len, BW), MM), sd((m_len, w), F32), sd((m_len, w), F32), sd((1, MHD), F32)],
        name="memattn_bwd", compiler_params=_cparams(1))(proj, proj, k, v, qg, dys)


def _merge_specs(tm):
    row = _bs((tm, BW), lambda i: (i, 0))
    return [row, row, row, row, _bs((tm, NB * D), lambda i: (i, OFF_ML // (NB * D))), _bs((NB, D), lambda i: (0, 0)),
            _bs((N_CHIPS, NB, BW, D // N_CHIPS), lambda i: (0, 0, 0, 0)), _bs((D, D), lambda i: (0, 0))]


def _merge_call(ys, proj, bm, wb, wo, x):
    s_len = proj.shape[0]
    tm = min(s_len, 256)

    def body(ya, yb, yc, yd, lg_ref, bm_ref, wb_ref, wo_ref, x_ref, o_ref):
        out = _merge_fn([r[...] for r in (ya, yb, yc, yd)], lg_ref[...], [bm_ref[n:n + 1, :] for n in range(NB)],
                        [[wb_ref[j, n] for n in range(NB)] for j in range(N_CHIPS)], wo_ref[...])
        o_ref[...] = x_ref[...] + out

    xrow = _bs((tm, D), lambda i: (i, 0))
    return pl.pallas_call(
        body, grid=(s_len // tm,), in_specs=_merge_specs(tm) + [xrow], out_specs=xrow,
        out_shape=jax.ShapeDtypeStruct((s_len, D), F32), name="merge", compiler_params=_cparams(1))(*ys, proj, bm, wb, wo, x)


def _merge_bwd_call(ys, proj, bm, wb, wo, dout):
    s_len = proj.shape[0]
    tm = min(s_len, 256)

    def body(ya, yb, yc, yd, lg_ref, bm_ref, wb_ref, wo_ref, do_ref, dya, dyb, dyc, dyd, dlg_ref, dbm_ref, dwb_ref, dwo_ref):
        fn = lambda ys_, lg_, bm_, wb_, wo_: _merge_fn(ys_, lg_, bm_, wb_, wo_)
        _, vjp = jax.vjp(fn, [r[...].astype(F32) for r in (ya, yb, yc, yd)], lg_ref[...], [bm_ref[n:n + 1, :] for n in range(NB)],
                         [[wb_ref[j, n].astype(F32) for n in range(NB)] for j in range(N_CHIPS)], wo_ref[...].astype(F32))
        dys, dlg, dbm, dwb, dwo = vjp(do_ref[...])
        for r, d in zip((dya, dyb, dyc, dyd), dys):
            r[...] = d
        dlg_ref[...] = dlg.astype(dlg_ref.dtype)

        @pl.when(pl.program_id(0) == 0)
        def _():
            for r in (dbm_ref, dwb_ref, dwo_ref):
                r[...] = jnp.zeros_like(r)
        for n in range(NB):
            dbm_ref[n:n + 1, :] += dbm[n]
            for j in range(N_CHIPS):
                dwb_ref[j, n] += dwb[j][n]
        dwo_ref[...] += dwo

    row = _bs((tm, BW), lambda i: (i, 0))
    sd = jax.ShapeDtypeStruct
    wb_shape = (N_CHIPS, NB, BW, D // N_CHIPS)
    return pl.pallas_call(
        body, grid=(s_len // tm,), in_specs=_merge_specs(tm) + [_bs((tm, D), lambda i: (i, 0))],
        out_specs=[row, row, row, row, _bs((tm, NB * D), lambda i: (i, 0)), _bs((NB, D), lambda i: (0, 0)),
                   _bs(wb_shape, lambda i: (0, 0, 0, 0)), _bs((D, D), lambda i: (0, 0))],
        out_shape=[sd((s_len, BW), F32)] * 4 + [sd((s_len, NB * D), MM), sd((NB, D), F32), sd(wb_shape, F32), sd((D, D), F32)],
        name="merge_bwd", compiler_params=_cparams(1))(*ys, proj, bm, wb, wo, dout)


def _dh_call(dproj, w, x, g, dout):
    s_len = x.shape[0]
    tm, tk = min(s_len, 512), NP // 4

    def body(dp_ref, w_ref, x_ref, g_ref, do_ref, dx_ref, dg_ref, acc_ref):
        i, k = pl.program_id(0), pl.program_id(1)

        @pl.when(k == 0)
        def _():
            acc_ref[...] = jnp.zeros_like(acc_ref)
        acc_ref[...] += lax.dot_general(dp_ref[...], w_ref[...], (((1,), (1,)), ((), ())), preferred_element_type=F32)

        @pl.when(k == pl.num_programs(1) - 1)
        def _():
            _, vjp = jax.vjp(lambda x_, g_: _rms_n(x_, g_, D), x_ref[...], g_ref[...])
            dxr, dgr = vjp(acc_ref[...])
            dx_ref[...] = do_ref[...] + dxr

            @pl.when(i == 0)
            def _():
                dg_ref[...] = jnp.zeros_like(dg_ref)
            dg_ref[...] += dgr

    row = _bs((tm, D), lambda i, k: (i, 0))
    return pl.pallas_call(
        body, grid=(s_len // tm, NP // tk),
        in_specs=[_bs((tm, tk), lambda i, k: (i, k)), _bs((D, tk), lambda i, k: (0, k)), row, _bs((1, D), lambda i, k: (0, 0)), row],
        out_specs=[row, _bs((1, D), lambda i, k: (0, 0))],
        out_shape=[jax.ShapeDtypeStruct((s_len, D), F32), jax.ShapeDtypeStruct((1, D), F32)],
        scratch_shapes=[pltpu.VMEM((tm, D), F32)], name="dh", compiler_params=_cparams(2))(dproj, w, x, g, dout)


def _dw_call(h, dproj):
    s_len = h.shape[0]
    tn = 512

    def body(h_ref, dp_ref, o_ref):
        o_ref[...] = lax.dot_general(h_ref[...], dp_ref[...], (((0,), (0,)), ((), ())), preferred_element_type=F32)

    return pl.pallas_call(
        body, grid=(NP // tn,), in_specs=[_bs((s_len, D), lambda j: (0, 0)), _bs((s_len, tn), lambda j: (0, j))],
        out_specs=_bs((D, tn), lambda j: (0, j)), out_shape=jax.ShapeDtypeStruct((D, NP), F32),
        name="dw_in", compiler_params=_cparams(1))(h, dproj)


def _loss_call(y, target):
    s_len = y.shape[0]
    tm = min(s_len, 512)

    def body(y_ref, t_ref, dy_ref, l_ref):
        e = y_ref[...] - t_ref[...]
        dy_ref[...] = e * (1.0 / D)

        @pl.when(pl.program_id(0) == 0)
        def _():
            l_ref[...] = jnp.zeros_like(l_ref)
        l_ref[...] += jnp.sum(e * e, axis=0, keepdims=True)

    row = _bs((tm, D), lambda i: (i, 0))
    return pl.pallas_call(
        body, grid=(s_len // tm,), in_specs=[row, row], out_specs=[row, _bs((1, D), lambda i: (0, 0))],
        out_shape=[jax.ShapeDtypeStruct((s_len, D), F32), jax.ShapeDtypeStruct((1, D), F32)],
        name="loss", compiler_params=_cparams(1))(y, target)


def _adamw_call(w, g, m, v, name):
    rows, cols = w.shape
    tr = min(_row_tile(rows), 128)

    def body(w_ref, g_ref, m_ref, v_ref, d_ref, nm_ref, nv_ref):
        gv = g_ref[...]
        m2 = ADAM_B1 * m_ref[...] + (1.0 - ADAM_B1) * gv
        v2 = ADAM_B2 * v_ref[...] + (1.0 - ADAM_B2) * (gv * gv)
        m_hat = m2 / (1.0 - ADAM_B1 ** ADAM_STEP)
        v_hat = v2 / (1.0 - ADAM_B2 ** ADAM_STEP)
        d_ref[...] = -ADAM_LR * (m_hat / (jnp.sqrt(v_hat) + ADAM_EPS) + ADAM_WD * w_ref[...])
        nm_ref[...] = m2
        nv_ref[...] = v2

    blk = _bs((tr, cols), lambda i: (i, 0))
    return pl.pallas_call(
        body, grid=(rows // tr,), in_specs=[blk] * 4, out_specs=[blk] * 3,
        out_shape=[jax.ShapeDtypeStruct((rows, cols), F32)] * 3, name=name, compiler_params=_cparams(1))(w, g, m, v)


def _row_tile(rows):
    for cand in (512, 256, 128, 64, 32, 16, 8):
        if rows % cand == 0 and rows > cand:
            return cand
    return rows


def _pair_sum_call(grads, from_sibling, core, name):
    n = len(grads)

    def body(core_ref, *refs):
        for t in range(n):
            refs[2 * n + t][...] = (refs[t][...] + refs[n + t][...]).astype(MM)

    half = lambda g: (1, g.shape[1] // 2, g.shape[2])
    grid_spec = pltpu.PrefetchScalarGridSpec(
        num_scalar_prefetch=1, grid=(N_CHIPS,),
        in_specs=[pl.BlockSpec(half(g), lambda j, core_ref: (j, core_ref[0], 0)) for g in grads]
        + [pl.BlockSpec(half(g), lambda j, core_ref: (j, 0, 0)) for g in grads],
        out_specs=[pl.BlockSpec(half(g), lambda j, core_ref: (j, 0, 0)) for g in grads])
    return pl.pallas_call(
        body, grid_spec=grid_spec, out_shape=[jax.ShapeDtypeStruct((N_CHIPS,) + half(g)[1:], MM) for g in grads], name=name,
        compiler_params=_cparams(1))(core, *grads, *from_sibling)


def _owner_sum_call(chip_sums, from_chips, chip_core, name):
    n = len(chip_sums)
    steps = 4

    def body(ids_ref, *refs):
        for t in range(n):
            a, b = refs[t], refs[n + t]
            refs[2 * n + t][...] = ((a[0].astype(F32) + b[0].astype(F32)) + b[1].astype(F32)) + b[2].astype(F32)

    tile = lambda p: (p.shape[1] // steps, p.shape[2])
    grid_spec = pltpu.PrefetchScalarGridSpec(
        num_scalar_prefetch=1, grid=(steps,),
        in_specs=[pl.BlockSpec((1,) + tile(p), lambda i, ids_ref: (ids_ref[0], i, 0)) for p in chip_sums]
        + [pl.BlockSpec((3,) + tile(p), lambda i, ids_ref: (0, i, 0)) for p in chip_sums],
        out_specs=[pl.BlockSpec(tile(p), lambda i, ids_ref: (ids_ref[1] * steps + i, 0)) for p in chip_sums])
    return pl.pallas_call(
        body, grid_spec=grid_spec, out_shape=[jax.ShapeDtypeStruct((2 * p.shape[1], p.shape[2]), F32) for p in chip_sums],
        name=name, compiler_params=_cparams(1))(chip_core, *chip_sums, *from_chips)


def _sum8_call(parts):
    n, rows, cols = parts.shape
    tr = _row_tile(rows)

    def body(p_ref, o_ref):
        acc = p_ref[0]
        for k in range(1, n):
            acc = acc + p_ref[k]
        o_ref[...] = acc

    return pl.pallas_call(
        body, grid=(rows // tr,), in_specs=[_bs((n, tr, cols), lambda i: (0, i, 0))], out_specs=_bs((tr, cols), lambda i: (i, 0)),
        out_shape=jax.ShapeDtypeStruct((rows, cols), F32), name="sum_small_grads", compiler_params=_cparams(1))(parts)


_ANY = pl.BlockSpec(memory_space=pl.ANY)


def _all_gather8(blk, name):
    rows, cols = blk.shape

    def body(x_ref, out_ref, send_sems, recv_sems, local_sem):
        x, y, c = lax.axis_index("x"), lax.axis_index("y"), lax.axis_index("c")
        me, sibling = (x, y, c), (x, y, 1 - c)
        chips = [(1 - x, y), (x, 1 - y), (1 - x, 1 - y)]

        def slot(px, py, pc):
            return out_ref.at[4 * px + 2 * py + pc]

        def copy(k, block, to, src=None):
            return pltpu.make_async_remote_copy(
                src_ref=slot(*block) if src is None else src, dst_ref=slot(*block),
                send_sem=send_sems.at[k], recv_sem=recv_sems.at[k], device_id=to, device_id_type=MESH_ID)

        mine = pltpu.make_async_copy(x_ref, slot(*me), local_sem)
        mine.start()
        first = [copy(0, me, sibling, src=x_ref)]
        first += [copy(1 + j, me, (*chip, c), src=x_ref) for j, chip in enumerate(chips)]
        for cp in first:
            cp.start()
        passed = [copy(4 + j, (*chip, c), sibling) for j, chip in enumerate(chips)]
        for j, chip in enumerate(chips):
            copy(1 + j, (*chip, c), me).wait_recv()
            passed[j].start()
        copy(0, sibling, me).wait_recv()
        for j, chip in enumerate(chips):
            copy(4 + j, (*chip, 1 - c), me).wait_recv()
        for cp in first + passed:
            cp.wait_send()
        mine.wait()

    return pl.pallas_call(
        body, out_shape=jax.ShapeDtypeStruct((8, rows, cols), blk.dtype), in_specs=[_ANY], out_specs=_ANY,
        scratch_shapes=[pltpu.SemaphoreType.DMA((7,)), pltpu.SemaphoreType.DMA((7,)), pltpu.SemaphoreType.DMA],
        name=name)(blk)


def _half_rows(ref, lead, half, which):
    rows = pl.ds(pl.multiple_of(half * which, half), half)
    return ref.at[rows] if lead is None else ref.at[lead, rows]


def _gather_layer_call(layer, shards, name):
    n = len(shards)
    half = [s.shape[1] // 2 for s in shards]

    def body(*refs):
        srcs, outs = refs[:n], refs[n:2 * n]
        send_sems, recv_sems, local_sems = refs[2 * n:]
        x, y, c = lax.axis_index("x"), lax.axis_index("y"), lax.axis_index("c")
        sibling = (x, y, 1 - c)
        chips = [(1 - x, y), (x, 1 - y), (1 - x, 1 - y)]

        def slot(t, px, py, pc):
            return _half_rows(outs[t], 2 * px + py, half[t], pc)

        def copy(t, k, block, to, src=None):
            return pltpu.make_async_remote_copy(
                src_ref=slot(t, *block) if src is None else src, dst_ref=slot(t, *block),
                send_sem=send_sems.at[7 * t + k], recv_sem=recv_sems.at[7 * t + k], device_id=to, device_id_type=MESH_ID)

        mine = [_half_rows(srcs[t], layer, half[t], c) for t in range(n)]
        local = [pltpu.make_async_copy(mine[t], slot(t, x, y, c), local_sems.at[t]) for t in range(n)]
        for cp in local:
            cp.start()
        first = []
        for t in range(n):
            first.append(copy(t, 0, (x, y, c), sibling, src=mine[t]))
            first += [copy(t, 1 + j, (x, y, c), (*chip, c), src=mine[t]) for j, chip in enumerate(chips)]
        for cp in first:
            cp.start()
        passed = []
        for j, chip in enumerate(chips):
            for t in range(n):
                copy(t, 1 + j, (*chip, c), (x, y, c)).wait_recv()
                passed.append(copy(t, 4 + j, (*chip, c), sibling))
                passed[-1].start()
        for t in range(n):
            copy(t, 0, (x, y, 1 - c), (x, y, c)).wait_recv()
            for j, chip in enumerate(chips):
                copy(t, 4 + j, (*chip, 1 - c), (x, y, c)).wait_recv()
        for cp in first + passed:
            cp.wait_send()
        for cp in local:
            cp.wait()

    return pl.pallas_call(
        body, out_shape=[jax.ShapeDtypeStruct((N_CHIPS,) + s.shape[1:], s.dtype) for s in shards],
        in_specs=[_ANY] * n, out_specs=[_ANY] * n,
        scratch_shapes=[pltpu.SemaphoreType.DMA((7 * n,)), pltpu.SemaphoreType.DMA((7 * n,)), pltpu.SemaphoreType.DMA((n,))],
        name=name)(*shards)


def _pair_exchange_call(grads, name):
    n = len(grads)
    half = [g.shape[1] // 2 for g in grads]

    def body(*refs):
        srcs, outs, send_sems, recv_sems = refs[:n], refs[n:2 * n], refs[2 * n], refs[2 * n + 1]
        x, y, c = lax.axis_index("x"), lax.axis_index("y"), lax.axis_index("c")
        copies = [pltpu.make_async_remote_copy(
            src_ref=srcs[t].at[:, pl.ds(pl.multiple_of(half[t] * (1 - c), half[t]), half[t])], dst_ref=outs[t],
            send_sem=send_sems.at[t], recv_sem=recv_sems.at[t], device_id=(x, y, 1 - c), device_id_type=MESH_ID) for t in range(n)]
        for cp in copies:
            cp.start()
        for cp in copies:
            cp.wait()

    return pl.pallas_call(
        body, out_shape=[jax.ShapeDtypeStruct((g.shape[0], g.shape[1] // 2, g.shape[2]), g.dtype) for g in grads],
        in_specs=[_ANY] * n, out_specs=[_ANY] * n,
        scratch_shapes=[pltpu.SemaphoreType.DMA((n,)), pltpu.SemaphoreType.DMA((n,))], name=name)(*grads)


def _chip_scatter_call(chip_sums, name):
    n = len(chip_sums)

    def body(*refs):
        srcs, outs, send_sems, recv_sems = refs[:n], refs[n:2 * n], refs[2 * n], refs[2 * n + 1]
        x, y, c = lax.axis_index("x"), lax.axis_index("y"), lax.axis_index("c")
        chips = [(1 - x, y), (x, 1 - y), (1 - x, 1 - y)]
        copies = [pltpu.make_async_remote_copy(
            src_ref=srcs[t].at[2 * cx + cy], dst_ref=outs[t].at[k], send_sem=send_sems.at[3 * t + k],
            recv_sem=recv_sems.at[3 * t + k], device_id=(cx, cy, c), device_id_type=MESH_ID)
            for k, (cx, cy) in enumerate(chips) for t in range(n)]
        for cp in copies:
            cp.start()
        for cp in copies:
            cp.wait()

    return pl.pallas_call(
        body, out_shape=[jax.ShapeDtypeStruct((3,) + p.shape[1:], p.dtype) for p in chip_sums],
        in_specs=[_ANY] * n, out_specs=[_ANY] * n,
        scratch_shapes=[pltpu.SemaphoreType.DMA((3 * n,)), pltpu.SemaphoreType.DMA((3 * n,))], name=name)(*chip_sums)


def _pair_gather_call(bufs, name):
    n = len(bufs)
    half = [b.shape[0] // 2 for b in bufs]

    def body(*refs):
        srcs, outs, send_sems, recv_sems = refs[:n], refs[n:2 * n], refs[2 * n], refs[2 * n + 1]
        x, y, c = lax.axis_index("x"), lax.axis_index("y"), lax.axis_index("c")
        for t in range(n):
            pltpu.make_async_remote_copy(
                src_ref=_half_rows(srcs[t], None, half[t], c), dst_ref=_half_rows(outs[t], None, half[t], c),
                send_sem=send_sems.at[t], recv_sem=recv_sems.at[t], device_id=(x, y, 1 - c), device_id_type=MESH_ID).start()
        for t in range(n):
            pltpu.make_async_remote_copy(
                src_ref=_half_rows(srcs[t], None, half[t], c), dst_ref=_half_rows(outs[t], None, half[t], 1 - c),
                send_sem=send_sems.at[t], recv_sem=recv_sems.at[t], device_id=(x, y, 1 - c), device_id_type=MESH_ID).wait()

    return pl.pallas_call(
        body, out_shape=[jax.ShapeDtypeStruct(b.shape, b.dtype) for b in bufs], in_specs=[_ANY] * n, out_specs=[_ANY] * n,
        input_output_aliases={t: t for t in range(n)},
        scratch_shapes=[pltpu.SemaphoreType.DMA((n,)), pltpu.SemaphoreType.DMA((n,))], name=name)(*bufs)


def _pack_rows(flats, dtype, row_multiple):
    flat = jnp.concatenate([f.reshape(-1).astype(dtype) for f in flats])
    n = flat.shape[0]
    rows = -(-n // PACK_W)
    rows = -(-rows // row_multiple) * row_multiple
    return jnp.pad(flat, (0, rows * PACK_W - n)).reshape(rows, PACK_W)


def _unpack(flat, shapes):
    out, off = [], 0
    for shp in shapes:
        n = math.prod(shp)
        out.append(flat[off:off + n].reshape(shp))
        off += n
    return out


def _f32_as_mm_bits(a):
    return lax.bitcast_convert_type(a, jnp.bfloat16).reshape(-1)


def _mm_bits_as_f32(flat, shape):
    return lax.bitcast_convert_type(flat.reshape(-1, 2), F32).reshape(shape)


def _w_in_to_aligned(w):
    z = lambda n: jnp.zeros((w.shape[0], n), w.dtype)
    return jnp.concatenate([w[:, R_ML:R_END], w[:, R_SG:R_ML], w[:, R_CV:R_SGI], w[:, R_SGI:R_MQ], w[:, R_MQ:R_SG],
                            w[:, R_CQ:R_CKV], w[:, R_CKV:R_KR], z(NOPE), w[:, R_KR:R_CV], z(LANES - QKH)], axis=1)


def _w_in_from_aligned(wa):
    return jnp.concatenate([wa[:, OFF_CQ:OFF_CKV], wa[:, OFF_CKV:OFF_KR], wa[:, OFF_KR + NOPE:OFF_KR + QKH], wa[:, OFF_CV:OFF_SGI],
                            wa[:, OFF_SGI:OFF_MQ], wa[:, OFF_MQ:OFF_CQ], wa[:, OFF_SG:OFF_CV], wa[:, OFF_ML:OFF_SG]], axis=1)


def _wuq_to_heads(w):
    w3 = w.reshape(QL, H, QKH)
    w3 = jnp.pad(w3, ((0, 0), (0, 0), (0, LANES - QKH)))
    return jnp.transpose(w3, (1, 0, 2))


def _wuq_from_heads(wh):
    return jnp.transpose(wh[:, :, :QKH], (1, 0, 2)).reshape(QL, H * QKH)


def _wukv_to_heads(w):
    w3 = w.reshape(KVL, H, NOPE + VH)
    wkn = jnp.transpose(jnp.pad(w3[:, :, :NOPE], ((0, 0), (0, 0), (0, LANES - NOPE))), (1, 0, 2))
    wv3 = w3[:, :, NOPE:]
    z = jnp.zeros((KVL, VH), w.dtype)
    cols = []
    for h in range(H):
        cols += [wv3[:, h], z] if h % 2 == 0 else [z, wv3[:, h]]
    return wkn, jnp.concatenate(cols, axis=1)


def _wukv_from_heads(wkn, wv):
    kn = jnp.transpose(wkn[:, :, :NOPE], (1, 0, 2))
    vs = jnp.stack([wv[:, LANES * h + VH * (h % 2):LANES * h + VH * (h % 2) + VH] for h in range(H)], axis=1)
    return jnp.concatenate([kn, vs], axis=2).reshape(KVL, H * (NOPE + VH))


def _layer_fwd(x, mem, tabs, p):
    proj, h = _proj_call(x, p["norm_g"], p["w_in"])
    q, k, v = _mla_prep_call(proj, tabs, p["cq_g"], p["ckv_g"], p["qg"], p["kg"], p["wuq"], p["wkn"], p["wv"])
    ya = _attn_call(q, k, v, proj)
    yb = _conv_call(proj, p["conv_w"], p["conv_b"])
    yc = _sg_call(proj, p["ln_g"], p["ln_b"], p["ws"], p["bs"])
    mk, mv = _memkv_call(mem, p["mem_g"], p["wm"], p["mkg"])
    yd = _mem_call(proj, mk, mv, p["mqg"])
    out = _merge_call((ya, yb, yc, yd), proj, p["bm"], p["wb"], p["wo"], x)
    return out, dict(x=x, proj=proj, h=h, q=q, k=k, v=v, ys=(ya, yb, yc, yd), mk=mk, mv=mv)


def _layer_bwd(dout, mem, tabs, p, sv):
    proj = sv["proj"]
    dya, dyb, dyc, dyd, dml, dbm, dwb, dwo = _merge_bwd_call(sv["ys"], proj, p["bm"], p["wb"], p["wo"], dout)
    dq, dk, dv, dsg_a = _attn_bwd_call(sv["q"], sv["k"], sv["v"], proj, dya)
    dcq, dckv, dkr, dcqg, dckvg, dqg, dkg, dwuq, dwkn, dwv = _mla_prep_bwd_call(
        proj, tabs, p["cq_g"], p["ckv_g"], p["qg"], p["kg"], p["wuq"], p["wkn"], p["wv"], dq, dk, dv)
    dbg, dcg, dxi, dsg_b, dcw, dcb = _conv_bwd_call(proj, p["conv_w"], p["conv_b"], dyb)
    du, dvv, dsg_c, dlg, dlb, dws, dbs = _sg_bwd_call(proj, p["ln_g"], p["ln_b"], p["ws"], p["bs"], dyc)
    dmq, dsg_d, dmk, dmv, dmqg = _mem_bwd_call(proj, sv["mk"], sv["mv"], p["mqg"], dyd)
    dmem_g, dwm, dmkg = _memkv_bwd_call(mem, p["mem_g"], p["wm"], p["mkg"], dmk, dmv)
    dproj = jnp.concatenate([dml, dsg_a, dsg_b, dsg_c, dsg_d, dbg, dcg, dxi, du, dvv, dmq, dcq, dckv, dkr], axis=1)
    dx, dnorm_g = _dh_call(dproj, p["w_in"], sv["x"], p["norm_g"], dout)
    dw_in = _dw_call(sv["h"], dproj)
    grads = dict(norm_g=dnorm_g[0], cq_norm_g=dcqg[0], ckv_norm_g=dckvg[0], mla_q_norm_g=dqg[0, :QKH], mla_k_norm_g=dkg[0, :QKH],
                 conv_w=dcw, conv_b=dcb[0], sg_ln_g=dlg[0], sg_ln_b=dlb[0], w_spatial=dws, b_spatial=dbs[:, :, 0],
                 mem_norm_g=dmem_g[0], mem_q_norm_g=dmqg[0], mem_k_norm_g=dmkg[0], b_merge=dbm,
                 w_in_aligned=dw_in, wuq_heads=dwuq, wkn_heads=dwkn, wv_heads=dwv, w_mem_kv=dwm, w_branch_chips=dwb, w_out=dwo)
    return dx, grads


def _chips_to_cols(a):
    return jnp.concatenate([a[j] for j in range(N_CHIPS)], axis=1)


def _cols_to_chips(a):
    cols = a.shape[1] // N_CHIPS
    return jnp.stack([a[:, cols * j:cols * (j + 1)] for j in range(N_CHIPS)])


def _layer_params(l, rep, gathered, conv_w, b_merge):
    pad_g = lambda g: jnp.pad(g, (0, LANES - QKH)).reshape(1, LANES)
    wkn, wv = _wukv_to_heads(_chips_to_cols(gathered["w_ukv"]))
    return dict(
        norm_g=rep["norm_g"][l].reshape(1, D), w_in=_w_in_to_aligned(_chips_to_cols(gathered["w_in"])),
        cq_g=rep["cq_norm_g"][l].reshape(1, QL), ckv_g=rep["ckv_norm_g"][l].reshape(1, KVL),
        qg=pad_g(rep["mla_q_norm_g"][l]), kg=pad_g(rep["mla_k_norm_g"][l]),
        wuq=_wuq_to_heads(_chips_to_cols(gathered["w_uq"])), wkn=wkn, wv=wv,
        conv_w=conv_w, conv_b=rep["conv_b"][l].reshape(1, CW),
        ln_g=rep["sg_ln_g"][l].reshape(1, SGW), ln_b=rep["sg_ln_b"][l].reshape(1, SGW),
        ws=rep["w_spatial"][l], bs=rep["b_spatial"][l].reshape(SGG, SGC, 1),
        mem_g=rep["mem_norm_g"][l].reshape(1, D), wm=gathered["w_mem_kv"].reshape(D, 2 * MH * MHD),
        mqg=rep["mem_q_norm_g"][l].reshape(1, MHD), mkg=rep["mem_k_norm_g"][l].reshape(1, MHD),
        bm=b_merge, wb=gathered["w_branch"], wo=gathered["w_out"].reshape(D, D))


def _forward_backward(x, mem, pos, target, params, on_layer_grads=None):
    tabs = _rope_tables(pos)
    saved = []
    act = x
    for l in range(DEPTH):
        act, sv = _layer_fwd(act, mem, tabs, params[l])
        saved.append(sv)
    dy, sq = _loss_call(act, target)
    grads = [None] * DEPTH
    for l in reversed(range(DEPTH)):
        dy, grads[l] = _layer_bwd(dy, mem, tabs, params[l], saved[l])
        if on_layer_grads is not None:
            grads[l] = on_layer_grads(l, grads[l])
    return sq, dy, grads


_SHARDED_MM = ("w_in", "w_branch", "w_out", "w_mem_kv", "w_uq", "w_ukv")
_SHARDED_F32 = ("conv_w", "b_merge")
_REPLICATED = ("norm_g", "cq_norm_g", "ckv_norm_g", "mla_q_norm_g", "mla_k_norm_g", "conv_b", "sg_ln_g", "sg_ln_b",
               "w_spatial", "b_spatial", "mem_norm_g", "mem_q_norm_g", "mem_k_norm_g")
_ALL_REDUCED = _REPLICATED + _SHARDED_F32
_WEIGHTS = ("norm_g", "w_in", "cq_norm_g", "ckv_norm_g", "w_uq", "w_ukv", "mla_q_norm_g", "mla_k_norm_g", "conv_w", "conv_b",
            "sg_ln_g", "sg_ln_b", "w_spatial", "b_spatial", "mem_norm_g", "w_mem_kv", "mem_q_norm_g", "mem_k_norm_g",
            "b_merge", "w_branch", "w_out")
_BIG = ("w_in", "w_uq", "w_ukv", "w_mem_kv", "w_branch", "w_out")
_SMALL = tuple(n for n in _WEIGHTS if n not in _BIG)


def _gather_small_sharded(w):
    names = _SHARDED_F32
    packed = _pack_rows([w[n] for n in names], F32, 8)
    got = _all_gather8(packed, "gather_small_weights")
    per_chip = [_unpack(got[2 * j].reshape(-1), [w[n].shape for n in names]) for j in range(N_CHIPS)]
    return {n: jnp.concatenate([per_chip[j][t] for j in range(N_CHIPS)], axis=2) for t, n in enumerate(names)}


def _gather_layer(l, shards):
    srcs = [shards[n] for n in _SHARDED_MM]
    return dict(zip(_SHARDED_MM, _gather_layer_call(l, srcs, "gather_weights_l%d" % l)))


def _reduce_scatter_layer(l, grads):
    x, y, c = lax.axis_index("x"), lax.axis_index("y"), lax.axis_index("c")
    core = c.astype(jnp.int32).reshape(1)
    chip_core = jnp.stack([2 * x + y, c]).astype(jnp.int32)
    tensors = [
        _cols_to_chips(_w_in_from_aligned(grads["w_in_aligned"])),
        grads["w_branch_chips"].reshape(N_CHIPS, NB * BW, D // N_CHIPS),
        grads["w_out"].reshape(N_CHIPS, D // N_CHIPS, D),
        grads["w_mem_kv"].reshape(N_CHIPS, D // N_CHIPS, 2 * MH * MHD),
        _cols_to_chips(_wuq_from_heads(grads["wuq_heads"])),
        _cols_to_chips(_wukv_from_heads(grads["wkn_heads"], grads["wv_heads"])),
    ]
    tag = "rs_l%d_" % l
    from_sibling = _pair_exchange_call(tensors, tag + "pair_exchange")
    chip_sums = _pair_sum_call(tensors, from_sibling, core, tag + "pair_sum")
    from_chips = _chip_scatter_call(chip_sums, tag + "chip_scatter")
    mine = _owner_sum_call(chip_sums, from_chips, chip_core, tag + "owner_sum")
    shard = dict(zip(_SHARDED_MM, _pair_gather_call(mine, tag + "pair_gather")))
    shard["w_branch"] = shard["w_branch"].reshape(NB, BW, D // N_CHIPS)
    small = {n: grads[n] for n in _ALL_REDUCED}
    return dict(shard, **small)


def _all_reduce_small(g):
    packed = _pack_rows([g[n] for n in _ALL_REDUCED], F32, 64)
    got = _all_gather8(packed, "gather_small_grads")
    total = _sum8_call(got).reshape(-1)
    out = dict(zip(_ALL_REDUCED, _unpack(total, [g[n].shape for n in _ALL_REDUCED])))
    chip = 2 * lax.axis_index("x") + lax.axis_index("y")
    for n in _SHARDED_F32:
        size = out[n].shape[2] // N_CHIPS
        out[n] = lax.dynamic_slice_in_dim(out[n], chip * size, size, axis=2)
    return out


def _adamw_all(w, g, m, v):
    delta, new_m, new_v = {}, {}, {}
    for n in _BIG:
        shp = w[n].shape
        as2d = lambda a: a.reshape(-1, shp[-1])
        d, nm, nv = _adamw_call(as2d(w[n]), as2d(g[n]), as2d(m[n]), as2d(v[n]), "adamw_" + n)
        delta[n], new_m[n], new_v[n] = d.reshape(shp), nm.reshape(shp), nv.reshape(shp)
    shapes = [w[n].shape for n in _SMALL]
    pk = lambda t: _pack_rows([t[n] for n in _SMALL], F32, 64)
    d, nm, nv = _adamw_call(pk(w), pk(g), pk(m), pk(v), "adamw_small")
    for out, packed in ((delta, d), (new_m, nm), (new_v, nv)):
        out.update(zip(_SMALL, _unpack(packed.reshape(-1), shapes)))
    return delta, new_m, new_v


def kernel(x, mem, positions, norm_g, w_in, cq_norm_g, ckv_norm_g, w_uq, w_ukv, mla_q_norm_g, mla_k_norm_g, conv_w, conv_b, sg_ln_g, sg_ln_b, w_spatial, b_spatial, mem_norm_g, w_mem_kv, mem_q_norm_g, mem_k_norm_g, b_merge, w_branch, w_out, loss_target, m_norm_g, m_w_in, m_cq_norm_g, m_ckv_norm_g, m_w_uq, m_w_ukv, m_mla_q_norm_g, m_mla_k_norm_g, m_conv_w, m_conv_b, m_sg_ln_g, m_sg_ln_b, m_w_spatial, m_b_spatial, m_mem_norm_g, m_w_mem_kv, m_mem_q_norm_g, m_mem_k_norm_g, m_b_merge, m_w_branch, m_w_out, v_norm_g, v_w_in, v_cq_norm_g, v_ckv_norm_g, v_w_uq, v_w_ukv, v_mla_q_norm_g, v_mla_k_norm_g, v_conv_w, v_conv_b, v_sg_ln_g, v_sg_ln_b, v_w_spatial, v_b_spatial, v_mem_norm_g, v_w_mem_kv, v_mem_q_norm_g, v_mem_k_norm_g, v_b_merge, v_w_branch, v_w_out):
    w = dict(norm_g=norm_g, w_in=w_in, cq_norm_g=cq_norm_g, ckv_norm_g=ckv_norm_g, w_uq=w_uq, w_ukv=w_ukv,
             mla_q_norm_g=mla_q_norm_g, mla_k_norm_g=mla_k_norm_g, conv_w=conv_w, conv_b=conv_b, sg_ln_g=sg_ln_g,
             sg_ln_b=sg_ln_b, w_spatial=w_spatial, b_spatial=b_spatial, mem_norm_g=mem_norm_g, w_mem_kv=w_mem_kv,
             mem_q_norm_g=mem_q_norm_g, mem_k_norm_g=mem_k_norm_g, b_merge=b_merge, w_branch=w_branch, w_out=w_out)
    m = dict(norm_g=m_norm_g, w_in=m_w_in, cq_norm_g=m_cq_norm_g, ckv_norm_g=m_ckv_norm_g, w_uq=m_w_uq, w_ukv=m_w_ukv,
             mla_q_norm_g=m_mla_q_norm_g, mla_k_norm_g=m_mla_k_norm_g, conv_w=m_conv_w, conv_b=m_conv_b, sg_ln_g=m_sg_ln_g,
             sg_ln_b=m_sg_ln_b, w_spatial=m_w_spatial, b_spatial=m_b_spatial, mem_norm_g=m_mem_norm_g, w_mem_kv=m_w_mem_kv,
             mem_q_norm_g=m_mem_q_norm_g, mem_k_norm_g=m_mem_k_norm_g, b_merge=m_b_merge, w_branch=m_w_branch, w_out=m_w_out)
    v = dict(norm_g=v_norm_g, w_in=v_w_in, cq_norm_g=v_cq_norm_g, ckv_norm_g=v_ckv_norm_g, w_uq=v_w_uq, w_ukv=v_w_ukv,
             mla_q_norm_g=v_mla_q_norm_g, mla_k_norm_g=v_mla_k_norm_g, conv_w=v_conv_w, conv_b=v_conv_b, sg_ln_g=v_sg_ln_g,
             sg_ln_b=v_sg_ln_b, w_spatial=v_w_spatial, b_spatial=v_b_spatial, mem_norm_g=v_mem_norm_g, w_mem_kv=v_w_mem_kv,
             mem_q_norm_g=v_mem_q_norm_g, mem_k_norm_g=v_mem_k_norm_g, b_merge=v_b_merge, w_branch=v_w_branch, w_out=v_w_out)

    small = _gather_small_sharded(w)
    shards = {n: w[n].astype(MM) for n in _SHARDED_MM}
    params = [_layer_params(l, w, _gather_layer(l, shards), small["conv_w"][l], small["b_merge"][l]) for l in range(DEPTH)]
    sq, grad_x, layer_grads = _forward_backward(x[0], mem[0], positions[0], loss_target[0], params, _reduce_scatter_layer)
    loss = lax.psum(0.5 / D * jnp.sum(sq), ("x", "y", "c"))

    g = {n: jnp.stack([layer_grads[l][n] for l in range(DEPTH)]) for n in _WEIGHTS}
    g.update(_all_reduce_small(g))
    delta, new_m, new_v = _adamw_all(w, g, m, v)
    return (loss, grad_x[None], *[g[n] for n in _WEIGHTS], *[delta[n] for n in _WEIGHTS],
            *[new_m[n] for n in _WEIGHTS], *[new_v[n] for n in _WEIGHTS])
```

```python
import functools
import math

import jax
import jax.numpy as jnp
from jax import lax
from jax.experimental import pallas as pl
from jax.experimental.pallas import tpu as pltpu

F32 = jnp.float32
MM = jnp.bfloat16

D = 1024
DEPTH = 2
EPS = 1e-6
H = 8
NOPE = 64
ROPE = 32
QKH = 96
VH = 64
QL = 256
KVL = 128
ROPE_THETA = 10000.0
CW = 512
SGW = 512
SGG = 4
SGC = 128
MH = 4
MHD = 128
NB = 4
BW = 512
NEG_INF = -1e30
LANES = 128
N_CHIPS = 4

R_CQ, R_CKV, R_KR, R_CV, R_SGI, R_MQ, R_SG, R_ML, R_END = 0, 256, 384, 416, 1952, 2976, 3488, 5536, 9632
OFF_ML, OFF_SG, OFF_CV, OFF_SGI, OFF_MQ, OFF_CQ, OFF_CKV, OFF_KR, NP = 0, 4096, 6144, 7680, 8704, 9216, 9472, 9600, 9728

ADAM_LR = 0.001
ADAM_B1 = 0.9
ADAM_B2 = 0.999
ADAM_EPS = 1e-08
ADAM_WD = 0.01
ADAM_STEP = 10

VMEM_LIMIT = 56 * 1024 * 1024
PACK_W = 512
MESH_ID = pl.DeviceIdType.MESH


def _cparams(n_axes):
    return pltpu.CompilerParams(dimension_semantics=("arbitrary",) * n_axes, vmem_limit_bytes=VMEM_LIMIT)


def _bs(shape, imap):
    return pl.BlockSpec(shape, imap)


@jax.custom_vjp
def _mm(a, b):
    return jnp.dot(a.astype(MM), b.astype(MM), preferred_element_type=F32)


def _mm_fwd(a, b):
    return _mm(a, b), (a, b)


def _mm_bwd(res, g):
    a, b = res
    gm = g.astype(MM)
    da = lax.dot_general(gm, b.astype(MM), (((1,), (1,)), ((), ())), preferred_element_type=F32)
    db = lax.dot_general(a.astype(MM), gm, (((0,), (0,)), ((), ())), preferred_element_type=F32)
    return da.astype(a.dtype), db.astype(b.dtype)


_mm.defvjp(_mm_fwd, _mm_bwd)


@jax.custom_vjp
def _mm_nt(a, b):
    return lax.dot_general(a.astype(MM), b.astype(MM), (((1,), (1,)), ((), ())), preferred_element_type=F32)


def _mm_nt_fwd(a, b):
    return _mm_nt(a, b), (a, b)


def _mm_nt_bwd(res, g):
    a, b = res
    gm = g.astype(MM)
    da = jnp.dot(gm, b.astype(MM), preferred_element_type=F32)
    db = lax.dot_general(gm, a.astype(MM), (((0,), (0,)), ((), ())), preferred_element_type=F32)
    return da.astype(a.dtype), db.astype(b.dtype)


_mm_nt.defvjp(_mm_nt_fwd, _mm_nt_bwd)


@functools.partial(jax.custom_vjp, nondiff_argnums=(1,))
def _lane_roll(x, shift):
    return pltpu.roll(x, shift, 1)


def _lane_roll_fwd(x, shift):
    return pltpu.roll(x, shift, 1), None


def _lane_roll_bwd(shift, _, g):
    return (pltpu.roll(g, (LANES - shift) % LANES, 1),)


_lane_roll.defvjp(_lane_roll_fwd, _lane_roll_bwd)


def _rms_n(x, g, n):
    ms = jnp.sum(x * x, axis=-1, keepdims=True) * (1.0 / n)
    return x * lax.rsqrt(ms + EPS) * g


def _softmax(s):
    m = jnp.max(s, axis=-1, keepdims=True)
    e = jnp.exp(s - m)
    return e / jnp.sum(e, axis=-1, keepdims=True)


def _rope(t, cos_t, sin_a, sin_b):
    return t * cos_t + _lane_roll(t, LANES - 16) * sin_a + _lane_roll(t, 16) * sin_b


def _mla_prep_fn(cq, ckv, kr, cos_t, sin_a, sin_b, cq_g, ckv_g, qg, kg, wuq, wkn, wv):
    cqn = _rms_n(cq, cq_g, QL)
    ckvn = _rms_n(ckv, ckv_g, KVL)
    lane = lax.broadcasted_iota(jnp.int32, kr.shape, 1)
    krm = jnp.where((lane >= NOPE) & (lane < QKH), kr, 0.0)
    qs, ks = [], []
    for h in range(H):
        qh = _rms_n(_mm(cqn, wuq[h]), qg, QKH)
        qs.append(_rope(qh, cos_t, sin_a, sin_b))
        kh = _rms_n(_mm(ckvn, wkn[h]) + krm, kg, QKH)
        ks.append(_rope(kh, cos_t, sin_a, sin_b))
    return jnp.concatenate(qs, axis=-1), jnp.concatenate(ks, axis=-1), _mm(ckvn, wv)


def _attn_pair_fn(q2, k2, v2, sg, row0):
    tq, s_len = q2.shape[0], k2.shape[0]
    rows = row0 + lax.broadcasted_iota(jnp.int32, (tq, s_len), 0)
    cols = lax.broadcasted_iota(jnp.int32, (tq, s_len), 1)
    mask = cols <= rows
    vlane = lax.broadcasted_iota(jnp.int32, (s_len, LANES), 1)
    o = jnp.zeros((tq, LANES), F32)
    for e in range(2):
        sl = slice(LANES * e, LANES * (e + 1))
        s = _mm_nt(q2[:, sl], k2[:, sl]) * (QKH ** -0.5)
        p = _softmax(jnp.where(mask, s, NEG_INF))
        ve = jnp.where((vlane >= VH * e) & (vlane < VH * (e + 1)), v2[:, sl], 0.0)
        o = o + _mm(p, ve)
    return o * jax.nn.silu(sg)


def _sg_fn(u, v, sgc, ln_g, ln_b, ws, bs):
    mu = jnp.mean(v, axis=-1, keepdims=True)
    xc = v - mu
    vn = xc * lax.rsqrt(jnp.mean(xc * xc, axis=-1, keepdims=True) + EPS) * ln_g + ln_b
    r = lax.broadcasted_iota(jnp.int32, (SGC, SGC), 0)
    c = lax.broadcasted_iota(jnp.int32, (SGC, SGC), 1)
    wt = [jnp.where(r >= c, w, 0.0) for w in ws]
    row_blocks = []
    for ch in range(u.shape[0] // SGC):
        col_blocks = []
        for g in range(SGG):
            blk = vn[SGC * ch:SGC * (ch + 1), LANES * g:LANES * (g + 1)]
            col_blocks.append(_mm(wt[g], blk) + bs[g])
        row_blocks.append(jnp.concatenate(col_blocks, axis=-1))
    mixed = jnp.concatenate(row_blocks, axis=0)
    return (u * mixed) * jax.nn.silu(sgc)


def _memkv_fn(mem, mem_g, wm, kg):
    kv = _mm(_rms_n(mem, mem_g, D), wm)
    ks = [_rms_n(kv[:, MHD * h:MHD * (h + 1)], kg, MHD) for h in range(MH)]
    return jnp.concatenate(ks, axis=-1), kv[:, MH * MHD:]


def _mem_fn(mq, sgd, k, v, qg):
    outs = []
    for h in range(MH):
        sl = slice(MHD * h, MHD * (h + 1))
        qh = _rms_n(mq[:, sl], qg, MHD)
        p = _softmax(_mm_nt(qh, k[:, sl]) * (MHD ** -0.5))
        outs.append(_mm(p, v[:, sl]))
    return jnp.concatenate(outs, axis=-1) * jax.nn.silu(sgd)


def _merge_fn(ys, logits, bm, wb, wo):
    merged = None
    for n in range(NB):
        z = jnp.concatenate([_mm(ys[n], wb[j][n]) for j in range(N_CHIPS)], axis=-1)
        gate = jax.nn.sigmoid(logits[:, D * n:D * (n + 1)] + bm[n])
        merged = gate * z if merged is None else merged + gate * z
    return _mm(merged, wo)


def _proj_call(x, g, w):
    s_len = x.shape[0]
    tm, tn = min(s_len, 1024), 512

    def body(x_ref, g_ref, w_ref, p_ref, h_ref):
        @pl.when(pl.program_id(1) == 0)
        def _():
            h_ref[...] = _rms_n(x_ref[...], g_ref[...], D).astype(h_ref.dtype)
        p_ref[...] = jnp.dot(h_ref[...], w_ref[...], preferred_element_type=F32)

    return pl.pallas_call(
        body, grid=(s_len // tm, NP // tn),
        in_specs=[_bs((tm, D), lambda i, j: (i, 0)), _bs((1, D), lambda i, j: (0, 0)), _bs((D, tn), lambda i, j: (0, j))],
        out_specs=[_bs((tm, tn), lambda i, j: (i, j)), _bs((tm, D), lambda i, j: (i, 0))],
        out_shape=[jax.ShapeDtypeStruct((s_len, NP), F32), jax.ShapeDtypeStruct((s_len, D), MM)],
        name="proj", compiler_params=_cparams(2))(x, g, w)


def _rope_tables(pos):
    half = ROPE // 2
    inv_freq = ROPE_THETA ** (-jnp.arange(half, dtype=F32) / half)
    ang = pos.astype(F32)[:, None] * inv_freq
    cos, sin = jnp.cos(ang), jnp.sin(ang)
    s_len = pos.shape[0]
    z = lambda n: jnp.zeros((s_len, n), F32)
    cos_t = jnp.concatenate([jnp.ones((s_len, NOPE), F32), cos, cos, z(LANES - QKH)], axis=1)
    sin_a = jnp.concatenate([z(NOPE), -sin, z(LANES - NOPE - half)], axis=1)
    sin_b = jnp.concatenate([z(NOPE + half), sin, z(LANES - QKH)], axis=1)
    return cos_t, sin_a, sin_b


def _mla_prep_specs(tm):
    row = lambda w, off: _bs((tm, w), lambda i: (i, off // w))
    full2 = lambda a, b: _bs((a, b), lambda i: (0, 0))
    full3 = lambda a, b, c: _bs((a, b, c), lambda i: (0, 0, 0))
    tab = _bs((tm, LANES), lambda i: (i, 0))
    return [row(QL, OFF_CQ), row(KVL, OFF_CKV), row(LANES, OFF_KR), tab, tab, tab,
            full2(1, QL), full2(1, KVL), full2(1, LANES), full2(1, LANES),
            full3(H, QL, LANES), full3(H, KVL, LANES), full2(KVL, H * LANES)]


def _mla_prep_args(body_refs, wdtype=None):
    (cq, ckv, kr, ct, sa, sb, cqg, ckvg, qg, kg, wuq, wkn, wv) = body_refs
    cast = (lambda a: a) if wdtype is None else (lambda a: a.astype(wdtype))
    return (cq[...], ckv[...], kr[...], ct[...], sa[...], sb[...], cqg[...], ckvg[...], qg[...], kg[...],
            [cast(wuq[h]) for h in range(H)], [cast(wkn[h]) for h in range(H)], cast(wv[...]))


def _mla_prep_call(proj, tabs, cq_g, ckv_g, qg, kg, wuq, wkn, wv):
    s_len = proj.shape[0]
    tm = min(s_len, 256)

    def body(*refs):
        q_ref, k_ref, v_ref = refs[13:]
        q, k, v = _mla_prep_fn(*_mla_prep_args(refs[:13]))
        q_ref[...] = q.astype(q_ref.dtype)
        k_ref[...] = k.astype(k_ref.dtype)
        v_ref[...] = v.astype(v_ref.dtype)

    out = _bs((tm, H * LANES), lambda i: (i, 0))
    return pl.pallas_call(
        body, grid=(s_len // tm,), in_specs=_mla_prep_specs(tm), out_specs=[out, out, out],
        out_shape=[jax.ShapeDtypeStruct((s_len, H * LANES), MM)] * 3,
        name="mla_prep", compiler_params=_cparams(1))(proj, proj, proj, *tabs, cq_g, ckv_g, qg, kg, wuq, wkn, wv)


def _mla_prep_bwd_call(proj, tabs, cq_g, ckv_g, qg, kg, wuq, wkn, wv, dq, dk, dv):
    s_len = proj.shape[0]
    tm = min(s_len, 256)

    def body(*refs):
        dq_ref, dk_ref, dv_ref = refs[13:16]
        dcq_ref, dckv_ref, dkr_ref, dcqg_ref, dckvg_ref, dqg_ref, dkg_ref, dwuq_ref, dwkn_ref, dwv_ref = refs[16:]
        _, vjp = jax.vjp(_mla_prep_fn, *_mla_prep_args(refs[:13], F32))
        (dcq, dckv, dkr, _, _, _, dcqg, dckvg, dqg, dkg, dwuq, dwkn, dwv) = vjp((dq_ref[...], dk_ref[...], dv_ref[...]))
        dcq_ref[...] = dcq.astype(dcq_ref.dtype)
        dckv_ref[...] = dckv.astype(dckv_ref.dtype)
        dkr_ref[...] = dkr.astype(dkr_ref.dtype)

        @pl.when(pl.program_id(0) == 0)
        def _():
            for r in (dcqg_ref, dckvg_ref, dqg_ref, dkg_ref, dwuq_ref, dwkn_ref, dwv_ref):
                r[...] = jnp.zeros_like(r)
        dcqg_ref[...] += dcqg
        dckvg_ref[...] += dckvg
        dqg_ref[...] += dqg
        dkg_ref[...] += dkg
        for h in range(H):
            dwuq_ref[h] += dwuq[h]
            dwkn_ref[h] += dwkn[h]
        dwv_ref[...] += dwv

    big = _bs((tm, H * LANES), lambda i: (i, 0))
    row = lambda w: _bs((tm, w), lambda i: (i, 0))
    full2 = lambda a, b: _bs((a, b), lambda i: (0, 0))
    full3 = lambda a, b, c: _bs((a, b, c), lambda i: (0, 0, 0))
    sd = jax.ShapeDtypeStruct
    return pl.pallas_call(
        body, grid=(s_len // tm,), in_specs=_mla_prep_specs(tm) + [big, big, big],
        out_specs=[row(QL), row(KVL), row(LANES), full2(1, QL), full2(1, KVL), full2(1, LANES), full2(1, LANES),
                   full3(H, QL, LANES), full3(H, KVL, LANES), full2(KVL, H * LANES)],
        out_shape=[sd((s_len, QL), MM), sd((s_len, KVL), MM), sd((s_len, LANES), MM), sd((1, QL), F32), sd((1, KVL), F32),
                   sd((1, LANES), F32), sd((1, LANES), F32), sd((H, QL, LANES), F32), sd((H, KVL, LANES), F32),
                   sd((KVL, H * LANES), F32)],
        name="mla_prep_bwd", compiler_params=_cparams(1))(proj, proj, proj, *tabs, cq_g, ckv_g, qg, kg, wuq, wkn, wv, dq, dk, dv)


def _attn_specs(s_len, tq):
    pair = 2 * LANES
    return [_bs((tq, pair), lambda p, i: (i, p)), _bs((s_len, pair), lambda p, i: (0, p)), _bs((s_len, pair), lambda p, i: (0, p)),
            _bs((tq, LANES), lambda p, i: (i, OFF_SG // LANES + p))]


def _attn_call(q, k, v, proj):
    s_len = q.shape[0]
    tq = min(s_len, 256)

    def body(q_ref, k_ref, v_ref, sg_ref, y_ref):
        for n in range(s_len // tq):
            @pl.when(pl.program_id(1) == n)
            def _():
                kl = (n + 1) * tq
                y_ref[...] = _attn_pair_fn(q_ref[...], k_ref[:kl, :], v_ref[:kl, :], sg_ref[...], n * tq).astype(y_ref.dtype)

    return pl.pallas_call(
        body, grid=(H // 2, s_len // tq), in_specs=_attn_specs(s_len, tq),
        out_specs=_bs((tq, LANES), lambda p, i: (i, p)), out_shape=jax.ShapeDtypeStruct((s_len, BW), MM),
        name="attn", compiler_params=_cparams(2))(q, k, v, proj)


def _attn_bwd_call(q, k, v, proj, dys):
    s_len = q.shape[0]
    tq = min(s_len, 256)
    pair = 2 * LANES

    def body(q_ref, k_ref, v_ref, sg_ref, dy_ref, dq_ref, dk_ref, dv_ref, dsg_ref):
        i = pl.program_id(1)

        @pl.when(i == 0)
        def _():
            dk_ref[...] = jnp.zeros_like(dk_ref)
            dv_ref[...] = jnp.zeros_like(dv_ref)

        for n in range(s_len // tq):
            @pl.when(i == n)
            def _():
                kl = (n + 1) * tq
                fn = functools.partial(_attn_pair_fn, row0=n * tq)
                _, vjp = jax.vjp(fn, q_ref[...].astype(F32), k_ref[:kl, :].astype(F32), v_ref[:kl, :].astype(F32), sg_ref[...])
                dq, dk, dv, dsg = vjp(dy_ref[...])
                dq_ref[...] = dq
                dsg_ref[...] = dsg.astype(dsg_ref.dtype)
                dk_ref[:kl, :] += dk
                dv_ref[:kl, :] += dv

    sd = jax.ShapeDtypeStruct
    return pl.pallas_call(
        body, grid=(H // 2, s_len // tq),
        in_specs=_attn_specs(s_len, tq) + [_bs((tq, LANES), lambda p, i: (i, p))],
        out_specs=[_bs((tq, pair), lambda p, i: (i, p)), _bs((s_len, pair), lambda p, i: (0, p)),
                   _bs((s_len, pair), lambda p, i: (0, p)), _bs((tq, LANES), lambda p, i: (i, p))],
        out_shape=[sd((s_len, H * LANES), F32), sd((s_len, H * LANES), F32), sd((s_len, H * LANES), F32), sd((s_len, BW), MM)],
        name="attn_bwd", compiler_params=_cparams(2))(q, k, v, proj, dys)


def _shift_down(a, n):
    r = lax.broadcasted_iota(jnp.int32, a.shape, 0)
    return jnp.where(r >= n, pltpu.roll(a, n, 0), 0.0)


def _shift_up(a, n):
    s_len = a.shape[0]
    r = lax.broadcasted_iota(jnp.int32, a.shape, 0)
    return jnp.where(r < s_len - n, pltpu.roll(a, s_len - n, 0), 0.0)


def _conv_specs(s_len):
    col = lambda off: _bs((s_len, LANES), lambda j: (0, off // LANES + j))
    return [col(OFF_CV), col(OFF_CV + CW), col(OFF_CV + 2 * CW), col(OFF_SG + BW),
            _bs((3, LANES), lambda j: (0, j)), _bs((1, LANES), lambda j: (0, j))]


def _conv_call(proj, cw, cb):
    s_len = proj.shape[0]

    def body(bg_ref, cg_ref, xi_ref, sg_ref, w_ref, b_ref, y_ref):
        z = cg_ref[...] * xi_ref[...]
        y = b_ref[...] + w_ref[0:1, :] * _shift_down(z, 2)
        y = y + w_ref[1:2, :] * _shift_down(z, 1)
        y = y + w_ref[2:3, :] * z
        y_ref[...] = ((bg_ref[...] * y) * jax.nn.silu(sg_ref[...])).astype(y_ref.dtype)

    return pl.pallas_call(
        body, grid=(CW // LANES,), in_specs=_conv_specs(s_len), out_specs=_bs((s_len, LANES), lambda j: (0, j)),
        out_shape=jax.ShapeDtypeStruct((s_len, CW), MM), name="conv", compiler_params=_cparams(1))(proj, proj, proj, proj, cw, cb)


def _conv_bwd_call(proj, cw, cb, dys):
    s_len = proj.shape[0]

    def body(bg_ref, cg_ref, xi_ref, sg_ref, w_ref, b_ref, dys_ref, dbg_ref, dcg_ref, dxi_ref, dsg_ref, dw_ref, db_ref):
        bg, cg, xi, sg = bg_ref[...], cg_ref[...], xi_ref[...], sg_ref[...]
        w0, w1, w2 = w_ref[0:1, :], w_ref[1:2, :], w_ref[2:3, :]
        z = cg * xi
        z1, z2 = _shift_down(z, 1), _shift_down(z, 2)
        y = b_ref[...] + w0 * z2
        y = y + w1 * z1
        y = y + w2 * z
        yb = bg * y
        sig = jax.nn.sigmoid(sg)
        silu = sg * sig
        dys_v = dys_ref[...]
        dsg_ref[...] = (dys_v * yb * (sig * (1.0 + sg * (1.0 - sig)))).astype(dsg_ref.dtype)
        dyb = dys_v * silu
        dbg_ref[...] = (dyb * y).astype(dbg_ref.dtype)
        dy = dyb * bg
        db_ref[...] = jnp.sum(dy, axis=0, keepdims=True)
        dw_ref[0:1, :] = jnp.sum(dy * z2, axis=0, keepdims=True)
        dw_ref[1:2, :] = jnp.sum(dy * z1, axis=0, keepdims=True)
        dw_ref[2:3, :] = jnp.sum(dy * z, axis=0, keepdims=True)
        dz = w2 * dy + w1 * _shift_up(dy, 1) + w0 * _shift_up(dy, 2)
        dcg_ref[...] = (dz * xi).astype(dcg_ref.dtype)
        dxi_ref[...] = (dz * cg).astype(dxi_ref.dtype)

    col = _bs((s_len, LANES), lambda j: (0, j))
    sd = jax.ShapeDtypeStruct
    return pl.pallas_call(
        body, grid=(CW // LANES,), in_specs=_conv_specs(s_len) + [col],
        out_specs=[col, col, col, col, _bs((3, LANES), lambda j: (0, j)), _bs((1, LANES), lambda j: (0, j))],
        out_shape=[sd((s_len, CW), MM)] * 4 + [sd((3, CW), F32), sd((1, CW), F32)],
        name="conv_bwd", compiler_params=_cparams(1))(proj, proj, proj, proj, cw, cb, dys)


def _sg_specs(tm):
    row = lambda off: _bs((tm, SGW), lambda i: (i, off // SGW))
    return [row(OFF_SGI), row(OFF_SGI + SGW), row(OFF_SG + 2 * BW), _bs((1, SGW), lambda i: (0, 0)), _bs((1, SGW), lambda i: (0, 0)),
            _bs((SGG, SGC, SGC), lambda i: (0, 0, 0)), _bs((SGG, SGC, 1), lambda i: (0, 0, 0))]


def _sg_args(refs):
    u, v, sg, lg, lb, ws, bs = refs
    return (u[...], v[...], sg[...], lg[...], lb[...], [ws[g] for g in range(SGG)], [bs[g] for g in range(SGG)])


def _sg_call(proj, ln_g, ln_b, ws, bs):
    s_len = proj.shape[0]
    tm = min(s_len, 256)

    def body(*refs):
        refs[7][...] = _sg_fn(*_sg_args(refs[:7])).astype(refs[7].dtype)

    return pl.pallas_call(
        body, grid=(s_len // tm,), in_specs=_sg_specs(tm), out_specs=_bs((tm, SGW), lambda i: (i, 0)),
        out_shape=jax.ShapeDtypeStruct((s_len, SGW), MM), name="sgmlp", compiler_params=_cparams(1))(proj, proj, proj, ln_g, ln_b, ws, bs)


def _sg_bwd_call(proj, ln_g, ln_b, ws, bs, dys):
    s_len = proj.shape[0]
    tm = min(s_len, 256)

    def body(*refs):
        dys_ref = refs[7]
        du_ref, dv_ref, dsg_ref, dlg_ref, dlb_ref, dws_ref, dbs_ref = refs[8:]
        _, vjp = jax.vjp(_sg_fn, *_sg_args(refs[:7]))
        du, dv, dsg, dlg, dlb, dws, dbs = vjp(dys_ref[...])
        du_ref[...] = du.astype(du_ref.dtype)
        dv_ref[...] = dv.astype(dv_ref.dtype)
        dsg_ref[...] = dsg.astype(dsg_ref.dtype)

        @pl.when(pl.program_id(0) == 0)
        def _():
            for r in (dlg_ref, dlb_ref, dws_ref, dbs_ref):
                r[...] = jnp.zeros_like(r)
        dlg_ref[...] += dlg
        dlb_ref[...] += dlb
        for g in range(SGG):
            dws_ref[g] += dws[g]
            dbs_ref[g] += dbs[g]

    row = _bs((tm, SGW), lambda i: (i, 0))
    sd = jax.ShapeDtypeStruct
    return pl.pallas_call(
        body, grid=(s_len // tm,), in_specs=_sg_specs(tm) + [row],
        out_specs=[row, row, row, _bs((1, SGW), lambda i: (0, 0)), _bs((1, SGW), lambda i: (0, 0)),
                   _bs((SGG, SGC, SGC), lambda i: (0, 0, 0)), _bs((SGG, SGC, 1), lambda i: (0, 0, 0))],
        out_shape=[sd((s_len, SGW), MM)] * 3 + [sd((1, SGW), F32), sd((1, SGW), F32), sd((SGG, SGC, SGC), F32), sd((SGG, SGC, 1), F32)],
        name="sgmlp_bwd", compiler_params=_cparams(1))(proj, proj, proj, ln_g, ln_b, ws, bs, dys)


def _memkv_call(mem, mem_g, wm, kg):
    m_len = mem.shape[0]

    def body(mem_ref, g_ref, w_ref, kg_ref, k_ref, v_ref):
        k, v = _memkv_fn(mem_ref[...], g_ref[...], w_ref[...], kg_ref[...])
        k_ref[...] = k.astype(k_ref.dtype)
        v_ref[...] = v.astype(v_ref.dtype)

    return pl.pallas_call(body, out_shape=[jax.ShapeDtypeStruct((m_len, MH * MHD), MM)] * 2, name="memkv",
                          compiler_params=pltpu.CompilerParams(vmem_limit_bytes=VMEM_LIMIT))(mem, mem_g, wm, kg)


def _memkv_bwd_call(mem, mem_g, wm, kg, dk, dv):
    def body(mem_ref, g_ref, w_ref, kg_ref, dk_ref, dv_ref, dg_ref, dw_ref, dkg_ref):
        _, vjp = jax.vjp(_memkv_fn, mem_ref[...], g_ref[...], w_ref[...].astype(F32), kg_ref[...])
        _, dg, dw, dkg = vjp((dk_ref[...], dv_ref[...]))
        dg_ref[...] = dg
        dw_ref[...] = dw
        dkg_ref[...] = dkg

    sd = jax.ShapeDtypeStruct
    return pl.pallas_call(body, out_shape=[sd((1, D), F32), sd((D, 2 * MH * MHD), F32), sd((1, MHD), F32)], name="memkv_bwd",
                          compiler_params=pltpu.CompilerParams(vmem_limit_bytes=VMEM_LIMIT))(mem, mem_g, wm, kg, dk, dv)


def _mem_specs(tm, m_len):
    w = MH * MHD
    return [_bs((tm, w), lambda i: (i, OFF_MQ // w)), _bs((tm, BW), lambda i: (i, (OFF_SG + 3 * BW) // BW)),
            _bs((m_len, w), lambda i: (0, 0)), _bs((m_len, w), lambda i: (0, 0)), _bs((1, MHD), lambda i: (0, 0))]


def _mem_call(proj, k, v, qg):
    s_len, m_len = proj.shape[0], k.shape[0]
    tm = min(s_len, 256)

    def body(mq_ref, sg_ref, k_ref, v_ref, qg_ref, y_ref):
        y_ref[...] = _mem_fn(mq_ref[...], sg_ref[...], k_ref[...], v_ref[...], qg_ref[...]).astype(y_ref.dtype)

    return pl.pallas_call(
        body, grid=(s_len // tm,), in_specs=_mem_specs(tm, m_len), out_specs=_bs((tm, BW), lambda i: (i, 0)),
        out_shape=jax.ShapeDtypeStruct((s_len, BW), MM), name="memattn", compiler_params=_cparams(1))(proj, proj, k, v, qg)


def _mem_bwd_call(proj, k, v, qg, dys):
    s_len, m_len = proj.shape[0], k.shape[0]
    tm = min(s_len, 256)
    w = MH * MHD

    def body(mq_ref, sg_ref, k_ref, v_ref, qg_ref, dys_ref, dmq_ref, dsg_ref, dk_ref, dv_ref, dqg_ref):
        _, vjp = jax.vjp(_mem_fn, mq_ref[...], sg_ref[...], k_ref[...].astype(F32), v_ref[...].astype(F32), qg_ref[...])
        dmq, dsg, dk, dv, dqg = vjp(dys_ref[...])
        dmq_ref[...] = dmq.astype(dmq_ref.dtype)
        dsg_ref[...] = dsg.astype(dsg_ref.dtype)

        @pl.when(pl.program_id(0) == 0)
        def _():
            for r in (dk_ref, dv_ref, dqg_ref):
                r[...] = jnp.zeros_like(r)
        dk_ref[...] += dk
        dv_ref[...] += dv
        dqg_ref[...] += dqg

    row = _bs((tm, BW), lambda i: (i, 0))
    kv = _bs((m_len, w), lambda i: (0, 0))
    sd = jax.ShapeDtypeStruct
    return pl.pallas_call(
        body, grid=(s_len // tm,), in_specs=_mem_specs(tm, m_len) + [row],
        out_specs=[row, row, kv, kv, _bs((1, MHD), lambda i: (0, 0))],
        out_shape=[sd((s_len, w), MM), sd((s_len, BW), MM), sd((m_len, w), F32), sd((m_len, w), F32), sd((1, MHD), F32)],
        name="memattn_bwd", compiler_params=_cparams(1))(proj, proj, k, v, qg, dys)


def _merge_specs(tm):
    row = _bs((tm, BW), lambda i: (i, 0))
    return [row, row, row, row, _bs((tm, NB * D), lambda i: (i, OFF_ML // (NB * D))), _bs((NB, D), lambda i: (0, 0)),
            _bs((N_CHIPS, NB, BW, D // N_CHIPS), lambda i: (0, 0, 0, 0)), _bs((D, D), lambda i: (0, 0))]


def _merge_call(ys, proj, bm, wb, wo, x):
    s_len = proj.shape[0]
    tm = min(s_len, 256)

    def body(ya, yb, yc, yd, lg_ref, bm_ref, wb_ref, wo_ref, x_ref, o_ref):
        out = _merge_fn([r[...] for r in (ya, yb, yc, yd)], lg_ref[...], [bm_ref[n:n + 1, :] for n in range(NB)],
                        [[wb_ref[j, n] for n in range(NB)] for j in range(N_CHIPS)], wo_ref[...])
        o_ref[...] = x_ref[...] + out

    xrow = _bs((tm, D), lambda i: (i, 0))
    return pl.pallas_call(
        body, grid=(s_len // tm,), in_specs=_merge_specs(tm) + [xrow], out_specs=xrow,
        out_shape=jax.ShapeDtypeStruct((s_len, D), F32), name="merge", compiler_params=_cparams(1))(*ys, proj, bm, wb, wo, x)


def _merge_bwd_call(ys, proj, bm, wb, wo, dout):
    s_len = proj.shape[0]
    tm = min(s_len, 256)

    def body(ya, yb, yc, yd, lg_ref, bm_ref, wb_ref, wo_ref, do_ref, dya, dyb, dyc, dyd, dlg_ref, dbm_ref, dwb_ref, dwo_ref):
        fn = lambda ys_, lg_, bm_, wb_, wo_: _merge_fn(ys_, lg_, bm_, wb_, wo_)
        _, vjp = jax.vjp(fn, [r[...].astype(F32) for r in (ya, yb, yc, yd)], lg_ref[...], [bm_ref[n:n + 1, :] for n in range(NB)],
                         [[wb_ref[j, n].astype(F32) for n in range(NB)] for j in range(N_CHIPS)], wo_ref[...].astype(F32))
        dys, dlg, dbm, dwb, dwo = vjp(do_ref[...])
        for r, d in zip((dya, dyb, dyc, dyd), dys):
            r[...] = d
        dlg_ref[...] = dlg.astype(dlg_ref.dtype)

        @pl.when(pl.program_id(0) == 0)
        def _():
            for r in (dbm_ref, dwb_ref, dwo_ref):
                r[...] = jnp.zeros_like(r)
        for n in range(NB):
            dbm_ref[n:n + 1, :] += dbm[n]
            for j in range(N_CHIPS):
                dwb_ref[j, n] += dwb[j][n]
        dwo_ref[...] += dwo

    row = _bs((tm, BW), lambda i: (i, 0))
    sd = jax.ShapeDtypeStruct
    wb_shape = (N_CHIPS, NB, BW, D // N_CHIPS)
    return pl.pallas_call(
        body, grid=(s_len // tm,), in_specs=_merge_specs(tm) + [_bs((tm, D), lambda i: (i, 0))],
        out_specs=[row, row, row, row, _bs((tm, NB * D), lambda i: (i, 0)), _bs((NB, D), lambda i: (0, 0)),
                   _bs(wb_shape, lambda i: (0, 0, 0, 0)), _bs((D, D), lambda i: (0, 0))],
        out_shape=[sd((s_len, BW), F32)] * 4 + [sd((s_len, NB * D), MM), sd((NB, D), F32), sd(wb_shape, F32), sd((D, D), F32)],
        name="merge_bwd", compiler_params=_cparams(1))(*ys, proj, bm, wb, wo, dout)


def _dh_call(dproj, w, x, g, dout):
    s_len = x.shape[0]
    tm, tk = min(s_len, 512), NP // 4

    def body(dp_ref, w_ref, x_ref, g_ref, do_ref, dx_ref, dg_ref, acc_ref):
        i, k = pl.program_id(0), pl.program_id(1)

        @pl.when(k == 0)
        def _():
            acc_ref[...] = jnp.zeros_like(acc_ref)
        acc_ref[...] += lax.dot_general(dp_ref[...], w_ref[...], (((1,), (1,)), ((), ())), preferred_element_type=F32)

        @pl.when(k == pl.num_programs(1) - 1)
        def _():
            _, vjp = jax.vjp(lambda x_, g_: _rms_n(x_, g_, D), x_ref[...], g_ref[...])
            dxr, dgr = vjp(acc_ref[...])
            dx_ref[...] = do_ref[...] + dxr

            @pl.when(i == 0)
            def _():
                dg_ref[...] = jnp.zeros_like(dg_ref)
            dg_ref[...] += dgr

    row = _bs((tm, D), lambda i, k: (i, 0))
    return pl.pallas_call(
        body, grid=(s_len // tm, NP // tk),
        in_specs=[_bs((tm, tk), lambda i, k: (i, k)), _bs((D, tk), lambda i, k: (0, k)), row, _bs((1, D), lambda i, k: (0, 0)), row],
        out_specs=[row, _bs((1, D), lambda i, k: (0, 0))],
        out_shape=[jax.ShapeDtypeStruct((s_len, D), F32), jax.ShapeDtypeStruct((1, D), F32)],
        scratch_shapes=[pltpu.VMEM((tm, D), F32)], name="dh", compiler_params=_cparams(2))(dproj, w, x, g, dout)


def _dw_call(h, dproj):
    s_len = h.shape[0]
    tn = 512

    def body(h_ref, dp_ref, o_ref):
        o_ref[...] = lax.dot_general(h_ref[...], dp_ref[...], (((0,), (0,)), ((), ())), preferred_element_type=F32)

    return pl.pallas_call(
        body, grid=(NP // tn,), in_specs=[_bs((s_len, D), lambda j: (0, 0)), _bs((s_len, tn), lambda j: (0, j))],
        out_specs=_bs((D, tn), lambda j: (0, j)), out_shape=jax.ShapeDtypeStruct((D, NP), F32),
        name="dw_in", compiler_params=_cparams(1))(h, dproj)


def _loss_call(y, target):
    s_len = y.shape[0]
    tm = min(s_len, 512)

    def body(y_ref, t_ref, dy_ref, l_ref):
        e = y_ref[...] - t_ref[...]
        dy_ref[...] = e * (1.0 / D)

        @pl.when(pl.program_id(0) == 0)
        def _():
            l_ref[...] = jnp.zeros_like(l_ref)
        l_ref[...] += jnp.sum(e * e, axis=0, keepdims=True)

    row = _bs((tm, D), lambda i: (i, 0))
    return pl.pallas_call(
        body, grid=(s_len // tm,), in_specs=[row, row], out_specs=[row, _bs((1, D), lambda i: (0, 0))],
        out_shape=[jax.ShapeDtypeStruct((s_len, D), F32), jax.ShapeDtypeStruct((1, D), F32)],
        name="loss", compiler_params=_cparams(1))(y, target)


def _adamw_call(w, g, m, v, name):
    rows, cols = w.shape
    tr = min(_row_tile(rows), 128)

    def body(w_ref, g_ref, m_ref, v_ref, d_ref, nm_ref, nv_ref):
        gv = g_ref[...]
        m2 = ADAM_B1 * m_ref[...] + (1.0 - ADAM_B1) * gv
        v2 = ADAM_B2 * v_ref[...] + (1.0 - ADAM_B2) * (gv * gv)
        m_hat = m2 / (1.0 - ADAM_B1 ** ADAM_STEP)
        v_hat = v2 / (1.0 - ADAM_B2 ** ADAM_STEP)
        d_ref[...] = -ADAM_LR * (m_hat / (jnp.sqrt(v_hat) + ADAM_EPS) + ADAM_WD * w_ref[...])
        nm_ref[...] = m2
        nv_ref[...] = v2

    blk = _bs((tr, cols), lambda i: (i, 0))
    return pl.pallas_call(
        body, grid=(rows // tr,), in_specs=[blk] * 4, out_specs=[blk] * 3,
        out_shape=[jax.ShapeDtypeStruct((rows, cols), F32)] * 3, name=name, compiler_params=_cparams(1))(w, g, m, v)


def _row_tile(rows):
    for cand in (512, 256, 128, 64, 32, 16, 8):
        if rows % cand == 0 and rows > cand:
            return cand
    return rows


def _pair_sum_call(grads, from_sibling, core, name):
    n = len(grads)

    def body(core_ref, *refs):
        for t in range(n):
            refs[2 * n + t][...] = (refs[t][...] + refs[n + t][...]).astype(MM)

    half = lambda g: (1, g.shape[1] // 2, g.shape[2])
    grid_spec = pltpu.PrefetchScalarGridSpec(
        num_scalar_prefetch=1, grid=(N_CHIPS,),
        in_specs=[pl.BlockSpec(half(g), lambda j, core_ref: (j, core_ref[0], 0)) for g in grads]
        + [pl.BlockSpec(half(g), lambda j, core_ref: (j, 0, 0)) for g in grads],
        out_specs=[pl.BlockSpec(half(g), lambda j, core_ref: (j, 0, 0)) for g in grads])
    return pl.pallas_call(
        body, grid_spec=grid_spec, out_shape=[jax.ShapeDtypeStruct((N_CHIPS,) + half(g)[1:], MM) for g in grads], name=name,
        compiler_params=_cparams(1))(core, *grads, *from_sibling)


def _owner_sum_call(chip_sums, from_chips, chip_core, name):
    n = len(chip_sums)
    steps = 4

    def body(ids_ref, *refs):
        for t in range(n):
            a, b = refs[t], refs[n + t]
            refs[2 * n + t][...] = ((a[0].astype(F32) + b[0].astype(F32)) + b[1].astype(F32)) + b[2].astype(F32)

    tile = lambda p: (p.shape[1] // steps, p.shape[2])
    grid_spec = pltpu.PrefetchScalarGridSpec(
        num_scalar_prefetch=1, grid=(steps,),
        in_specs=[pl.BlockSpec((1,) + tile(p), lambda i, ids_ref: (ids_ref[0], i, 0)) for p in chip_sums]
        + [pl.BlockSpec((3,) + tile(p), lambda i, ids_ref: (0, i, 0)) for p in chip_sums],
        out_specs=[pl.BlockSpec(tile(p), lambda i, ids_ref: (ids_ref[1] * steps + i, 0)) for p in chip_sums])
    return pl.pallas_call(
        body, grid_spec=grid_spec, out_shape=[jax.ShapeDtypeStruct((2 * p.shape[1], p.shape[2]), F32) for p in chip_sums],
        name=name, compiler_params=_cparams(1))(chip_core, *chip_sums, *from_chips)


def _sum8_call(parts):
    n, rows, cols = parts.shape
    tr = _row_tile(rows)

    def body(p_ref, o_ref):
        acc = p_ref[0]
        for k in range(1, n):
            acc = acc + p_ref[k]
        o_ref[...] = acc

    return pl.pallas_call(
        body, grid=(rows // tr,), in_specs=[_bs((n, tr, cols), lambda i: (0, i, 0))], out_specs=_bs((tr, cols), lambda i: (i, 0)),
        out_shape=jax.ShapeDtypeStruct((rows, cols), F32), name="sum_small_grads", compiler_params=_cparams(1))(parts)


_ANY = pl.BlockSpec(memory_space=pl.ANY)


def _all_gather8(blk, name):
    rows, cols = blk.shape

    def body(x_ref, out_ref, send_sems, recv_sems, local_sem):
        x, y, c = lax.axis_index("x"), lax.axis_index("y"), lax.axis_index("c")
        me, sibling = (x, y, c), (x, y, 1 - c)
        chips = [(1 - x, y), (x, 1 - y), (1 - x, 1 - y)]

        def slot(px, py, pc):
            return out_ref.at[4 * px + 2 * py + pc]

        def copy(k, block, to, src=None):
            return pltpu.make_async_remote_copy(
                src_ref=slot(*block) if src is None else src, dst_ref=slot(*block),
                send_sem=send_sems.at[k], recv_sem=recv_sems.at[k], device_id=to, device_id_type=MESH_ID)

        mine = pltpu.make_async_copy(x_ref, slot(*me), local_sem)
        mine.start()
        first = [copy(0, me, sibling, src=x_ref)]
        first += [copy(1 + j, me, (*chip, c), src=x_ref) for j, chip in enumerate(chips)]
        for cp in first:
            cp.start()
        passed = [copy(4 + j, (*chip, c), sibling) for j, chip in enumerate(chips)]
        for j, chip in enumerate(chips):
            copy(1 + j, (*chip, c), me).wait_recv()
            passed[j].start()
        copy(0, sibling, me).wait_recv()
        for j, chip in enumerate(chips):
            copy(4 + j, (*chip, 1 - c), me).wait_recv()
        for cp in first + passed:
            cp.wait_send()
        mine.wait()

    return pl.pallas_call(
        body, out_shape=jax.ShapeDtypeStruct((8, rows, cols), blk.dtype), in_specs=[_ANY], out_specs=_ANY,
        scratch_shapes=[pltpu.SemaphoreType.DMA((7,)), pltpu.SemaphoreType.DMA((7,)), pltpu.SemaphoreType.DMA],
        name=name)(blk)


def _half_rows(ref, lead, half, which):
    rows = pl.ds(pl.multiple_of(half * which, half), half)
    return ref.at[rows] if lead is None else ref.at[lead, rows]


def _gather_layer_call(layer, shards, name):
    n = len(shards)
    half = [s.shape[1] // 2 for s in shards]

    def body(*refs):
        srcs, outs = refs[:n], refs[n:2 * n]
        send_sems, recv_sems, local_sems = refs[2 * n:]
        x, y, c = lax.axis_index("x"), lax.axis_index("y"), lax.axis_index("c")
        sibling = (x, y, 1 - c)
        chips = [(1 - x, y), (x, 1 - y), (1 - x, 1 - y)]

        def slot(t, px, py, pc):
            return _half_rows(outs[t], 2 * px + py, half[t], pc)

        def copy(t, k, block, to, src=None):
            return pltpu.make_async_remote_copy(
                src_ref=slot(t, *block) if src is None else src, dst_ref=slot(t, *block),
                send_sem=send_sems.at[7 * t + k], recv_sem=recv_sems.at[7 * t + k], device_id=to, device_id_type=MESH_ID)

        mine = [_half_rows(srcs[t], layer, half[t], c) for t in range(n)]
        local = [pltpu.make_async_copy(mine[t], slot(t, x, y, c), local_sems.at[t]) for t in range(n)]
        for cp in local:
            cp.start()
        first = []
        for t in range(n):
            first.append(copy(t, 0, (x, y, c), sibling, src=mine[t]))
            first += [copy(t, 1 + j, (x, y, c), (*chip, c), src=mine[t]) for j, chip in enumerate(chips)]
        for cp in first:
            cp.start()
        passed = []
        for j, chip in enumerate(chips):
            for t in range(n):
                copy(t, 1 + j, (*chip, c), (x, y, c)).wait_recv()
                passed.append(copy(t, 4 + j, (*chip, c), sibling))
                passed[-1].start()
        for t in range(n):
            copy(t, 0, (x, y, 1 - c), (x, y, c)).wait_recv()
            for j, chip in enumerate(chips):
                copy(t, 4 + j, (*chip, 1 - c), (x, y, c)).wait_recv()
        for cp in first + passed:
            cp.wait_send()
        for cp in local:
            cp.wait()

    return pl.pallas_call(
        body, out_shape=[jax.ShapeDtypeStruct((N_CHIPS,) + s.shape[1:], s.dtype) for s in shards],
        in_specs=[_ANY] * n, out_specs=[_ANY] * n,
        scratch_shapes=[pltpu.SemaphoreType.DMA((7 * n,)), pltpu.SemaphoreType.DMA((7 * n,)), pltpu.SemaphoreType.DMA((n,))],
        name=name)(*shards)


_HBM = pl.BlockSpec(memory_space=pltpu.HBM)
_SEM = pl.BlockSpec(memory_space=pltpu.SEMAPHORE)
_ORDERED_EFFECT = pltpu.CompilerParams(has_side_effects=pltpu.SideEffectType.DATAFLOW_SIDE_EFFECTING)


def _in_hbm(a):
    return pltpu.with_memory_space_constraint(a, pltpu.HBM)


def _place_own_call(layer, shards, chip_core, name):
    n = len(shards)

    def body(ids_ref, *refs):
        for t in range(n):
            refs[n + t][...] = refs[t][...]

    def blk(s):
        return (1, s.shape[1] // 2) + s.shape[2:]

    def imap_in(s):
        pad = (0,) * (s.ndim - 2)
        return lambda i, ids_ref: (layer, ids_ref[1]) + pad

    def imap_out(s):
        pad = (0,) * (s.ndim - 2)
        return lambda i, ids_ref: (ids_ref[0], ids_ref[1]) + pad

    grid_spec = pltpu.PrefetchScalarGridSpec(
        num_scalar_prefetch=1, grid=(1,), in_specs=[pl.BlockSpec(blk(s), imap_in(s)) for s in shards],
        out_specs=[pl.BlockSpec(blk(s), imap_out(s)) for s in shards])
    return pl.pallas_call(
        body, grid_spec=grid_spec, out_shape=[jax.ShapeDtypeStruct((N_CHIPS,) + s.shape[1:], s.dtype) for s in shards],
        name=name, compiler_params=_cparams(1))(chip_core, *shards)


def _gather_start_call(layer, shards, bufs, after, name):
    n = len(shards)
    half = [s.shape[1] // 2 for s in shards]

    def body(*refs):
        srcs, outs = refs[:n], refs[2 * n + 1:3 * n + 1]
        send_sems, recv_sib, recv_ici = refs[3 * n + 1:]
        x, y, c = lax.axis_index("x"), lax.axis_index("y"), lax.axis_index("c")
        chips = [(1 - x, y), (x, 1 - y), (1 - x, 1 - y)]
        for t in range(n):
            mine = _half_rows(srcs[t], layer, half[t], c)
            dst = _half_rows(outs[t], 2 * x + y, half[t], c)
            pltpu.make_async_remote_copy(src_ref=mine, dst_ref=dst, send_sem=send_sems.at[4 * t], recv_sem=recv_sib.at[t],
                                         device_id=(x, y, 1 - c), device_id_type=MESH_ID).start()
            for j, chip in enumerate(chips):
                pltpu.make_async_remote_copy(src_ref=mine, dst_ref=dst, send_sem=send_sems.at[4 * t + 1 + j],
                                             recv_sem=recv_ici.at[3 * t + j], device_id=(*chip, c), device_id_type=MESH_ID).start()

    dma = pltpu.SemaphoreType.DMA
    return pl.pallas_call(
        body, out_shape=[pltpu.HBM(b.shape, b.dtype) for b in bufs] + [dma((4 * n,)), dma((n,)), dma((3 * n,))],
        in_specs=[_HBM] * (2 * n) + [_ANY], out_specs=[_HBM] * n + [_SEM] * 3,
        input_output_aliases={n + t: t for t in range(n)}, name=name, compiler_params=_ORDERED_EFFECT,
    )(*[_in_hbm(s) for s in shards], *[_in_hbm(b) for b in bufs], after)


def _gather_forward_call(bufs, recv_ici, after, name):
    n = len(bufs)
    half = [b.shape[1] // 2 for b in bufs]

    def body(*refs):
        ins, recv_ici_ref = refs[:n], refs[n]
        outs = refs[n + 2:2 * n + 2]
        send_fwd, recv_fwd = refs[2 * n + 2:]
        x, y, c = lax.axis_index("x"), lax.axis_index("y"), lax.axis_index("c")
        chips = [(1 - x, y), (x, 1 - y), (1 - x, 1 - y)]
        for j, (cx, cy) in enumerate(chips):
            for t in range(n):
                landed = _half_rows(ins[t], 2 * cx + cy, half[t], c)
                dst = _half_rows(outs[t], 2 * cx + cy, half[t], c)
                pltpu.make_async_remote_copy(src_ref=landed, dst_ref=landed, send_sem=send_fwd.at[3 * t + j],
                                             recv_sem=recv_ici_ref.at[3 * t + j], device_id=(cx, cy, c),
                                             device_id_type=MESH_ID).wait_recv()
                pltpu.make_async_remote_copy(src_ref=landed, dst_ref=dst, send_sem=send_fwd.at[3 * t + j],
                                             recv_sem=recv_fwd.at[3 * t + j], device_id=(x, y, 1 - c),
                                             device_id_type=MESH_ID).start()

    dma = pltpu.SemaphoreType.DMA
    return pl.pallas_call(
        body, out_shape=[pltpu.HBM(b.shape, b.dtype) for b in bufs] + [dma((3 * n,)), dma((3 * n,))],
        in_specs=[_HBM] * n + [_SEM, _ANY], out_specs=[_HBM] * n + [_SEM] * 2,
        input_output_aliases={t: t for t in range(n)}, name=name, compiler_params=_ORDERED_EFFECT,
    )(*bufs, recv_ici, after)


def _gather_finish_call(layer, shards, bufs, send_sems, recv_sib, send_fwd, recv_fwd, after, name):
    n = len(bufs)
    half = [b.shape[1] // 2 for b in bufs]

    def body(*refs):
        srcs, ins = refs[:n], refs[n:2 * n]
        send_ref, recv_sib_ref, send_fwd_ref, recv_fwd_ref = refs[2 * n:2 * n + 4]
        x, y, c = lax.axis_index("x"), lax.axis_index("y"), lax.axis_index("c")
        chips = [(1 - x, y), (x, 1 - y), (1 - x, 1 - y)]
        sibling = (x, y, 1 - c)
        for t in range(n):
            mine = _half_rows(srcs[t], layer, half[t], c)
            for k in range(4):
                pltpu.make_async_remote_copy(src_ref=mine, dst_ref=mine, send_sem=send_ref.at[4 * t + k],
                                             recv_sem=recv_sib_ref.at[t], device_id=sibling, device_id_type=MESH_ID).wait_send()
            from_sibling = _half_rows(ins[t], 2 * x + y, half[t], 1 - c)
            pltpu.make_async_remote_copy(src_ref=from_sibling, dst_ref=from_sibling, send_sem=send_ref.at[4 * t],
                                         recv_sem=recv_sib_ref.at[t], device_id=sibling, device_id_type=MESH_ID).wait_recv()
            for j, (cx, cy) in enumerate(chips):
                sent = _half_rows(ins[t], 2 * cx + cy, half[t], c)
                passed = _half_rows(ins[t], 2 * cx + cy, half[t], 1 - c)
                pltpu.make_async_remote_copy(src_ref=sent, dst_ref=passed, send_sem=send_fwd_ref.at[3 * t + j],
                                             recv_sem=recv_fwd_ref.at[3 * t + j], device_id=sibling, device_id_type=MESH_ID).wait()

    return pl.pallas_call(
        body, out_shape=[pltpu.HBM(b.shape, b.dtype) for b in bufs],
        in_specs=[_HBM] * (2 * n) + [_SEM] * 4 + [_ANY], out_specs=[_HBM] * n,
        input_output_aliases={n + t: t for t in range(n)}, name=name, compiler_params=_ORDERED_EFFECT,
    )(*[_in_hbm(s) for s in shards], *bufs, send_sems, recv_sib, send_fwd, recv_fwd, after)


def _pair_exchange_call(grads, name):
    n = len(grads)
    half = [g.shape[1] // 2 for g in grads]

    def body(*refs):
        srcs, outs, send_sems, recv_sems = refs[:n], refs[n:2 * n], refs[2 * n], refs[2 * n + 1]
        x, y, c = lax.axis_index("x"), lax.axis_index("y"), lax.axis_index("c")
        copies = [pltpu.make_async_remote_copy(
            src_ref=srcs[t].at[:, pl.ds(pl.multiple_of(half[t] * (1 - c), half[t]), half[t])], dst_ref=outs[t],
            send_sem=send_sems.at[t], recv_sem=recv_sems.at[t], device_id=(x, y, 1 - c), device_id_type=MESH_ID) for t in range(n)]
        for cp in copies:
            cp.start()
        for cp in copies:
            cp.wait()

    return pl.pallas_call(
        body, out_shape=[jax.ShapeDtypeStruct((g.shape[0], g.shape[1] // 2, g.shape[2]), g.dtype) for g in grads],
        in_specs=[_ANY] * n, out_specs=[_ANY] * n,
        scratch_shapes=[pltpu.SemaphoreType.DMA((n,)), pltpu.SemaphoreType.DMA((n,))], name=name)(*grads)


def _chip_scatter_call(chip_sums, name):
    n = len(chip_sums)

    def body(*refs):
        srcs, outs, send_sems, recv_sems = refs[:n], refs[n:2 * n], refs[2 * n], refs[2 * n + 1]
        x, y, c = lax.axis_index("x"), lax.axis_index("y"), lax.axis_index("c")
        chips = [(1 - x, y), (x, 1 - y), (1 - x, 1 - y)]
        copies = [pltpu.make_async_remote_copy(
            src_ref=srcs[t].at[2 * cx + cy], dst_ref=outs[t].at[k], send_sem=send_sems.at[3 * t + k],
            recv_sem=recv_sems.at[3 * t + k], device_id=(cx, cy, c), device_id_type=MESH_ID)
            for k, (cx, cy) in enumerate(chips) for t in range(n)]
        for cp in copies:
            cp.start()
        for cp in copies:
            cp.wait()

    return pl.pallas_call(
        body, out_shape=[jax.ShapeDtypeStruct((3,) + p.shape[1:], p.dtype) for p in chip_sums],
        in_specs=[_ANY] * n, out_specs=[_ANY] * n,
        scratch_shapes=[pltpu.SemaphoreType.DMA((3 * n,)), pltpu.SemaphoreType.DMA((3 * n,))], name=name)(*chip_sums)


def _pair_gather_call(bufs, name):
    n = len(bufs)
    half = [b.shape[0] // 2 for b in bufs]

    def body(*refs):
        srcs, outs, send_sems, recv_sems = refs[:n], refs[n:2 * n], refs[2 * n], refs[2 * n + 1]
        x, y, c = lax.axis_index("x"), lax.axis_index("y"), lax.axis_index("c")
        for t in range(n):
            pltpu.make_async_remote_copy(
                src_ref=_half_rows(srcs[t], None, half[t], c), dst_ref=_half_rows(outs[t], None, half[t], c),
                send_sem=send_sems.at[t], recv_sem=recv_sems.at[t], device_id=(x, y, 1 - c), device_id_type=MESH_ID).start()
        for t in range(n):
            pltpu.make_async_remote_copy(
                src_ref=_half_rows(srcs[t], None, half[t], c), dst_ref=_half_rows(outs[t], None, half[t], 1 - c),
                send_sem=send_sems.at[t], recv_sem=recv_sems.at[t], device_id=(x, y, 1 - c), device_id_type=MESH_ID).wait()

    return pl.pallas_call(
        body, out_shape=[jax.ShapeDtypeStruct(b.shape, b.dtype) for b in bufs], in_specs=[_ANY] * n, out_specs=[_ANY] * n,
        input_output_aliases={t: t for t in range(n)},
        scratch_shapes=[pltpu.SemaphoreType.DMA((n,)), pltpu.SemaphoreType.DMA((n,))], name=name)(*bufs)


def _pack_rows(flats, dtype, row_multiple):
    flat = jnp.concatenate([f.reshape(-1).astype(dtype) for f in flats])
    n = flat.shape[0]
    rows = -(-n // PACK_W)
    rows = -(-rows // row_multiple) * row_multiple
    return jnp.pad(flat, (0, rows * PACK_W - n)).reshape(rows, PACK_W)


def _unpack(flat, shapes):
    out, off = [], 0
    for shp in shapes:
        n = math.prod(shp)
        out.append(flat[off:off + n].reshape(shp))
        off += n
    return out


def _f32_as_mm_bits(a):
    return lax.bitcast_convert_type(a, jnp.bfloat16).reshape(-1)


def _mm_bits_as_f32(flat, shape):
    return lax.bitcast_convert_type(flat.reshape(-1, 2), F32).reshape(shape)


def _w_in_to_aligned(w):
    z = lambda n: jnp.zeros((w.shape[0], n), w.dtype)
    return jnp.concatenate([w[:, R_ML:R_END], w[:, R_SG:R_ML], w[:, R_CV:R_SGI], w[:, R_SGI:R_MQ], w[:, R_MQ:R_SG],
                            w[:, R_CQ:R_CKV], w[:, R_CKV:R_KR], z(NOPE), w[:, R_KR:R_CV], z(LANES - QKH)], axis=1)


def _w_in_from_aligned(wa):
    return jnp.concatenate([wa[:, OFF_CQ:OFF_CKV], wa[:, OFF_CKV:OFF_KR], wa[:, OFF_KR + NOPE:OFF_KR + QKH], wa[:, OFF_CV:OFF_SGI],
                            wa[:, OFF_SGI:OFF_MQ], wa[:, OFF_MQ:OFF_CQ], wa[:, OFF_SG:OFF_CV], wa[:, OFF_ML:OFF_SG]], axis=1)


def _wuq_to_heads(w):
    w3 = w.reshape(QL, H, QKH)
    w3 = jnp.pad(w3, ((0, 0), (0, 0), (0, LANES - QKH)))
    return jnp.transpose(w3, (1, 0, 2))


def _wuq_from_heads(wh):
    return jnp.transpose(wh[:, :, :QKH], (1, 0, 2)).reshape(QL, H * QKH)


def _wukv_to_heads(w):
    w3 = w.reshape(KVL, H, NOPE + VH)
    wkn = jnp.transpose(jnp.pad(w3[:, :, :NOPE], ((0, 0), (0, 0), (0, LANES - NOPE))), (1, 0, 2))
    wv3 = w3[:, :, NOPE:]
    z = jnp.zeros((KVL, VH), w.dtype)
    cols = []
    for h in range(H):
        cols += [wv3[:, h], z] if h % 2 == 0 else [z, wv3[:, h]]
    return wkn, jnp.concatenate(cols, axis=1)


def _wukv_from_heads(wkn, wv):
    kn = jnp.transpose(wkn[:, :, :NOPE], (1, 0, 2))
    vs = jnp.stack([wv[:, LANES * h + VH * (h % 2):LANES * h + VH * (h % 2) + VH] for h in range(H)], axis=1)
    return jnp.concatenate([kn, vs], axis=2).reshape(KVL, H * (NOPE + VH))


def _layer_fwd(x, mem, tabs, p):
    proj, h = _proj_call(x, p["norm_g"], p["w_in"])
    q, k, v = _mla_prep_call(proj, tabs, p["cq_g"], p["ckv_g"], p["qg"], p["kg"], p["wuq"], p["wkn"], p["wv"])
    ya = _attn_call(q, k, v, proj)
    yb = _conv_call(proj, p["conv_w"], p["conv_b"])
    yc = _sg_call(proj, p["ln_g"], p["ln_b"], p["ws"], p["bs"])
    mk, mv = _memkv_call(mem, p["mem_g"], p["wm"], p["mkg"])
    yd = _mem_call(proj, mk, mv, p["mqg"])
    out = _merge_call((ya, yb, yc, yd), proj, p["bm"], p["wb"], p["wo"], x)
    return out, dict(x=x, proj=proj, h=h, q=q, k=k, v=v, ys=(ya, yb, yc, yd), mk=mk, mv=mv)


def _layer_bwd(dout, mem, tabs, p, sv):
    proj = sv["proj"]
    dya, dyb, dyc, dyd, dml, dbm, dwb, dwo = _merge_bwd_call(sv["ys"], proj, p["bm"], p["wb"], p["wo"], dout)
    dq, dk, dv, dsg_a = _attn_bwd_call(sv["q"], sv["k"], sv["v"], proj, dya)
    dcq, dckv, dkr, dcqg, dckvg, dqg, dkg, dwuq, dwkn, dwv = _mla_prep_bwd_call(
        proj, tabs, p["cq_g"], p["ckv_g"], p["qg"], p["kg"], p["wuq"], p["wkn"], p["wv"], dq, dk, dv)
    dbg, dcg, dxi, dsg_b, dcw, dcb = _conv_bwd_call(proj, p["conv_w"], p["conv_b"], dyb)
    du, dvv, dsg_c, dlg, dlb, dws, dbs = _sg_bwd_call(proj, p["ln_g"], p["ln_b"], p["ws"], p["bs"], dyc)
    dmq, dsg_d, dmk, dmv, dmqg = _mem_bwd_call(proj, sv["mk"], sv["mv"], p["mqg"], dyd)
    dmem_g, dwm, dmkg = _memkv_bwd_call(mem, p["mem_g"], p["wm"], p["mkg"], dmk, dmv)
    dproj = jnp.concatenate([dml, dsg_a, dsg_b, dsg_c, dsg_d, dbg, dcg, dxi, du, dvv, dmq, dcq, dckv, dkr], axis=1)
    dx, dnorm_g = _dh_call(dproj, p["w_in"], sv["x"], p["norm_g"], dout)
    dw_in = _dw_call(sv["h"], dproj)
    grads = dict(norm_g=dnorm_g[0], cq_norm_g=dcqg[0], ckv_norm_g=dckvg[0], mla_q_norm_g=dqg[0, :QKH], mla_k_norm_g=dkg[0, :QKH],
                 conv_w=dcw, conv_b=dcb[0], sg_ln_g=dlg[0], sg_ln_b=dlb[0], w_spatial=dws, b_spatial=dbs[:, :, 0],
                 mem_norm_g=dmem_g[0], mem_q_norm_g=dmqg[0], mem_k_norm_g=dmkg[0], b_merge=dbm,
                 w_in_aligned=dw_in, wuq_heads=dwuq, wkn_heads=dwkn, wv_heads=dwv, w_mem_kv=dwm, w_branch_chips=dwb, w_out=dwo)
    return dx, grads


def _chips_to_cols(a):
    return jnp.concatenate([a[j] for j in range(N_CHIPS)], axis=1)


def _cols_to_chips(a):
    cols = a.shape[1] // N_CHIPS
    return jnp.stack([a[:, cols * j:cols * (j + 1)] for j in range(N_CHIPS)])


def _layer_params(l, rep, gathered, conv_w, b_merge):
    pad_g = lambda g: jnp.pad(g, (0, LANES - QKH)).reshape(1, LANES)
    wkn, wv = _wukv_to_heads(_chips_to_cols(gathered["w_ukv"]))
    return dict(
        norm_g=rep["norm_g"][l].reshape(1, D), w_in=_w_in_to_aligned(_chips_to_cols(gathered["w_in"])),
        cq_g=rep["cq_norm_g"][l].reshape(1, QL), ckv_g=rep["ckv_norm_g"][l].reshape(1, KVL),
        qg=pad_g(rep["mla_q_norm_g"][l]), kg=pad_g(rep["mla_k_norm_g"][l]),
        wuq=_wuq_to_heads(_chips_to_cols(gathered["w_uq"])), wkn=wkn, wv=wv,
        conv_w=conv_w, conv_b=rep["conv_b"][l].reshape(1, CW),
        ln_g=rep["sg_ln_g"][l].reshape(1, SGW), ln_b=rep["sg_ln_b"][l].reshape(1, SGW),
        ws=rep["w_spatial"][l], bs=rep["b_spatial"][l].reshape(SGG, SGC, 1),
        mem_g=rep["mem_norm_g"][l].reshape(1, D), wm=gathered["w_mem_kv"].reshape(D, 2 * MH * MHD),
        mqg=rep["mem_q_norm_g"][l].reshape(1, MHD), mkg=rep["mem_k_norm_g"][l].reshape(1, MHD),
        bm=b_merge, wb=gathered["w_branch"], wo=gathered["w_out"].reshape(D, D))


def _forward_backward(x, mem, pos, target, params, on_layer_grads=None):
    tabs = _rope_tables(pos)
    params = list(params)
    saved = []
    act = x
    for l in range(DEPTH):
        if callable(params[l]):
            params[l] = params[l](saved[-1], act)
        act, sv = _layer_fwd(act, mem, tabs, params[l])
        saved.append(sv)
    dy, sq = _loss_call(act, target)
    grads = [None] * DEPTH
    for l in reversed(range(DEPTH)):
        dy, grads[l] = _layer_bwd(dy, mem, tabs, params[l], saved[l])
        if on_layer_grads is not None:
            grads[l] = on_layer_grads(l, grads[l])
    return sq, dy, grads


_SHARDED_MM = ("w_in", "w_branch", "w_out", "w_mem_kv", "w_uq", "w_ukv")
_SHARDED_F32 = ("conv_w", "b_merge")
_REPLICATED = ("norm_g", "cq_norm_g", "ckv_norm_g", "mla_q_norm_g", "mla_k_norm_g", "conv_b", "sg_ln_g", "sg_ln_b",
               "w_spatial", "b_spatial", "mem_norm_g", "mem_q_norm_g", "mem_k_norm_g")
_ALL_REDUCED = _REPLICATED + _SHARDED_F32
_WEIGHTS = ("norm_g", "w_in", "cq_norm_g", "ckv_norm_g", "w_uq", "w_ukv", "mla_q_norm_g", "mla_k_norm_g", "conv_w", "conv_b",
            "sg_ln_g", "sg_ln_b", "w_spatial", "b_spatial", "mem_norm_g", "w_mem_kv", "mem_q_norm_g", "mem_k_norm_g",
            "b_merge", "w_branch", "w_out")
_BIG = ("w_in", "w_uq", "w_ukv", "w_mem_kv", "w_branch", "w_out")
_SMALL = tuple(n for n in _WEIGHTS if n not in _BIG)


def _gather_small_sharded(w):
    names = _SHARDED_F32
    packed = _pack_rows([w[n] for n in names], F32, 8)
    got = _all_gather8(packed, "gather_small_weights")
    per_chip = [_unpack(got[2 * j].reshape(-1), [w[n].shape for n in names]) for j in range(N_CHIPS)]
    return {n: jnp.concatenate([per_chip[j][t] for j in range(N_CHIPS)], axis=2) for t, n in enumerate(names)}


def _gather_layer(l, shards):
    srcs = [shards[n] for n in _SHARDED_MM]
    return dict(zip(_SHARDED_MM, _gather_layer_call(l, srcs, "gather_weights_l%d" % l)))


def _reduce_scatter_layer(l, grads):
    x, y, c = lax.axis_index("x"), lax.axis_index("y"), lax.axis_index("c")
    core = c.astype(jnp.int32).reshape(1)
    chip_core = jnp.stack([2 * x + y, c]).astype(jnp.int32)
    tensors = [
        _cols_to_chips(_w_in_from_aligned(grads["w_in_aligned"])),
        grads["w_branch_chips"].reshape(N_CHIPS, NB * BW, D // N_CHIPS),
        grads["w_out"].reshape(N_CHIPS, D // N_CHIPS, D),
        grads["w_mem_kv"].reshape(N_CHIPS, D // N_CHIPS, 2 * MH * MHD),
        _cols_to_chips(_wuq_from_heads(grads["wuq_heads"])),
        _cols_to_chips(_wukv_from_heads(grads["wkn_heads"], grads["wv_heads"])),
    ]
    tag = "rs_l%d_" % l
    from_sibling = _pair_exchange_call(tensors, tag + "pair_exchange")
    chip_sums = _pair_sum_call(tensors, from_sibling, core, tag + "pair_sum")
    from_chips = _chip_scatter_call(chip_sums, tag + "chip_scatter")
    mine = _owner_sum_call(chip_sums, from_chips, chip_core, tag + "owner_sum")
    shard = dict(zip(_SHARDED_MM, _pair_gather_call(mine, tag + "pair_gather")))
    shard["w_branch"] = shard["w_branch"].reshape(NB, BW, D // N_CHIPS)
    small = {n: grads[n] for n in _ALL_REDUCED}
    return dict(shard, **small)


def _all_reduce_small(g):
    packed = _pack_rows([g[n] for n in _ALL_REDUCED], F32, 64)
    got = _all_gather8(packed, "gather_small_grads")
    total = _sum8_call(got).reshape(-1)
    out = dict(zip(_ALL_REDUCED, _unpack(total, [g[n].shape for n in _ALL_REDUCED])))
    chip = 2 * lax.axis_index("x") + lax.axis_index("y")
    for n in _SHARDED_F32:
        size = out[n].shape[2] // N_CHIPS
        out[n] = lax.dynamic_slice_in_dim(out[n], chip * size, size, axis=2)
    return out


def _adamw_all(w, g, m, v):
    delta, new_m, new_v = {}, {}, {}
    for n in _BIG:
        shp = w[n].shape
        as2d = lambda a: a.reshape(-1, shp[-1])
        d, nm, nv = _adamw_call(as2d(w[n]), as2d(g[n]), as2d(m[n]), as2d(v[n]), "adamw_" + n)
        delta[n], new_m[n], new_v[n] = d.reshape(shp), nm.reshape(shp), nv.reshape(shp)
    shapes = [w[n].shape for n in _SMALL]
    pk = lambda t: _pack_rows([t[n] for n in _SMALL], F32, 64)
    d, nm, nv = _adamw_call(pk(w), pk(g), pk(m), pk(v), "adamw_small")
    for out, packed in ((delta, d), (new_m, nm), (new_v, nv)):
        out.update(zip(_SMALL, _unpack(packed.reshape(-1), shapes)))
    return delta, new_m, new_v


def kernel(x, mem, positions, norm_g, w_in, cq_norm_g, ckv_norm_g, w_uq, w_ukv, mla_q_norm_g, mla_k_norm_g, conv_w, conv_b, sg_ln_g, sg_ln_b, w_spatial, b_spatial, mem_norm_g, w_mem_kv, mem_q_norm_g, mem_k_norm_g, b_merge, w_branch, w_out, loss_target, m_norm_g, m_w_in, m_cq_norm_g, m_ckv_norm_g, m_w_uq, m_w_ukv, m_mla_q_norm_g, m_mla_k_norm_g, m_conv_w, m_conv_b, m_sg_ln_g, m_sg_ln_b, m_w_spatial, m_b_spatial, m_mem_norm_g, m_w_mem_kv, m_mem_q_norm_g, m_mem_k_norm_g, m_b_merge, m_w_branch, m_w_out, v_norm_g, v_w_in, v_cq_norm_g, v_ckv_norm_g, v_w_uq, v_w_ukv, v_mla_q_norm_g, v_mla_k_norm_g, v_conv_w, v_conv_b, v_sg_ln_g, v_sg_ln_b, v_w_spatial, v_b_spatial, v_mem_norm_g, v_w_mem_kv, v_mem_q_norm_g, v_mem_k_norm_g, v_b_merge, v_w_branch, v_w_out):
    w = dict(norm_g=norm_g, w_in=w_in, cq_norm_g=cq_norm_g, ckv_norm_g=ckv_norm_g, w_uq=w_uq, w_ukv=w_ukv,
             mla_q_norm_g=mla_q_norm_g, mla_k_norm_g=mla_k_norm_g, conv_w=conv_w, conv_b=conv_b, sg_ln_g=sg_ln_g,
             sg_ln_b=sg_ln_b, w_spatial=w_spatial, b_spatial=b_spatial, mem_norm_g=mem_norm_g, w_mem_kv=w_mem_kv,
             mem_q_norm_g=mem_q_norm_g, mem_k_norm_g=mem_k_norm_g, b_merge=b_merge, w_branch=w_branch, w_out=w_out)
    m = dict(norm_g=m_norm_g, w_in=m_w_in, cq_norm_g=m_cq_norm_g, ckv_norm_g=m_ckv_norm_g, w_uq=m_w_uq, w_ukv=m_w_ukv,
             mla_q_norm_g=m_mla_q_norm_g, mla_k_norm_g=m_mla_k_norm_g, conv_w=m_conv_w, conv_b=m_conv_b, sg_ln_g=m_sg_ln_g,
             sg_ln_b=m_sg_ln_b, w_spatial=m_w_spatial, b_spatial=m_b_spatial, mem_norm_g=m_mem_norm_g, w_mem_kv=m_w_mem_kv,
             mem_q_norm_g=m_mem_q_norm_g, mem_k_norm_g=m_mem_k_norm_g, b_merge=m_b_merge, w_branch=m_w_branch, w_out=m_w_out)
    v = dict(norm_g=v_norm_g, w_in=v_w_in, cq_norm_g=v_cq_norm_g, ckv_norm_g=v_ckv_norm_g, w_uq=v_w_uq, w_ukv=v_w_ukv,
             mla_q_norm_g=v_mla_q_norm_g, mla_k_norm_g=v_mla_k_norm_g, conv_w=v_conv_w, conv_b=v_conv_b, sg_ln_g=v_sg_ln_g,
             sg_ln_b=v_sg_ln_b, w_spatial=v_w_spatial, b_spatial=v_b_spatial, mem_norm_g=v_mem_norm_g, w_mem_kv=v_w_mem_kv,
             mem_q_norm_g=v_mem_q_norm_g, mem_k_norm_g=v_mem_k_norm_g, b_merge=v_b_merge, w_branch=v_w_branch, w_out=v_w_out)

    small = _gather_small_sharded(w)
    shards = {n: w[n].astype(MM) for n in _SHARDED_MM}
    srcs = [shards[n] for n in _SHARDED_MM]
    n_t = len(srcs)
    chip_core = jnp.stack([2 * lax.axis_index("x") + lax.axis_index("y"), lax.axis_index("c")]).astype(jnp.int32)
    gathered0 = _gather_layer(0, shards)
    bufs = _place_own_call(1, srcs, chip_core, "gather_l1_place_own")
    started = _gather_start_call(1, srcs, bufs, gathered0["w_ukv"], "gather_l1_start")
    bufs, send_sems, recv_sib, recv_ici = started[:n_t], started[n_t], started[n_t + 1], started[n_t + 2]

    def layer1_params(saved0, act0):
        passed = _gather_forward_call(bufs, recv_ici, saved0["ys"][0], "gather_l1_forward")
        got = _gather_finish_call(1, srcs, passed[:n_t], send_sems, recv_sib, passed[n_t], passed[n_t + 1], act0, "gather_l1_finish")
        return _layer_params(1, w, dict(zip(_SHARDED_MM, got)), small["conv_w"][1], small["b_merge"][1])

    params = [_layer_params(0, w, gathered0, small["conv_w"][0], small["b_merge"][0]), layer1_params]
    sq, grad_x, layer_grads = _forward_backward(x[0], mem[0], positions[0], loss_target[0], params, _reduce_scatter_layer)
    loss = lax.psum(0.5 / D * jnp.sum(sq), ("x", "y", "c"))

    g = {n: jnp.stack([layer_grads[l][n] for l in range(DEPTH)]) for n in _WEIGHTS}
    g.update(_all_reduce_small(g))
    delta, new_m, new_v = _adamw_all(w, g, m, v)
    return (loss, grad_x[None], *[g[n] for n in _WEIGHTS], *[delta[n] for n in _WEIGHTS],
            *[new_m[n] for n in _WEIGHTS], *[new_v[n] for n in _WEIGHTS])
```

```python
import functools
import math

import jax
import jax.numpy as jnp
from jax import lax
from jax.experimental import pallas as pl
from jax.experimental.pallas import tpu as pltpu

F32 = jnp.float32
MM = jnp.bfloat16

D = 1024
DEPTH = 2
EPS = 1e-6
H = 8
NOPE = 64
ROPE = 32
QKH = 96
VH = 64
QL = 256
KVL = 128
ROPE_THETA = 10000.0
CW = 512
SGW = 512
SGG = 4
SGC = 128
MH = 4
MHD = 128
NB = 4
BW = 512
NEG_INF = -1e30
LANES = 128
N_CHIPS = 4

R_CQ, R_CKV, R_KR, R_CV, R_SGI, R_MQ, R_SG, R_ML, R_END = 0, 256, 384, 416, 1952, 2976, 3488, 5536, 9632
OFF_ML, OFF_SG, OFF_CV, OFF_SGI, OFF_MQ, OFF_CQ, OFF_CKV, OFF_KR, NP = 0, 4096, 6144, 7680, 8704, 9216, 9472, 9600, 9728

ADAM_LR = 0.001
ADAM_B1 = 0.9
ADAM_B2 = 0.999
ADAM_EPS = 1e-08
ADAM_WD = 0.01
ADAM_STEP = 10

VMEM_LIMIT = 56 * 1024 * 1024
PACK_W = 512
MESH_ID = pl.DeviceIdType.MESH


def _cparams(n_axes):
    return pltpu.CompilerParams(dimension_semantics=("arbitrary",) * n_axes, vmem_limit_bytes=VMEM_LIMIT)


def _bs(shape, imap):
    return pl.BlockSpec(shape, imap)


@jax.custom_vjp
def _mm(a, b):
    return jnp.dot(a.astype(MM), b.astype(MM), preferred_element_type=F32)


def _mm_fwd(a, b):
    return _mm(a, b), (a, b)


def _mm_bwd(res, g):
    a, b = res
    gm = g.astype(MM)
    da = lax.dot_general(gm, b.astype(MM), (((1,), (1,)), ((), ())), preferred_element_type=F32)
    db = lax.dot_general(a.astype(MM), gm, (((0,), (0,)), ((), ())), preferred_element_type=F32)
    return da.astype(a.dtype), db.astype(b.dtype)


_mm.defvjp(_mm_fwd, _mm_bwd)


@jax.custom_vjp
def _mm_nt(a, b):
    return lax.dot_general(a.astype(MM), b.astype(MM), (((1,), (1,)), ((), ())), preferred_element_type=F32)


def _mm_nt_fwd(a, b):
    return _mm_nt(a, b), (a, b)


def _mm_nt_bwd(res, g):
    a, b = res
    gm = g.astype(MM)
    da = jnp.dot(gm, b.astype(MM), preferred_element_type=F32)
    db = lax.dot_general(gm, a.astype(MM), (((0,), (0,)), ((), ())), preferred_element_type=F32)
    return da.astype(a.dtype), db.astype(b.dtype)


_mm_nt.defvjp(_mm_nt_fwd, _mm_nt_bwd)


@functools.partial(jax.custom_vjp, nondiff_argnums=(1,))
def _lane_roll(x, shift):
    return pltpu.roll(x, shift, 1)


def _lane_roll_fwd(x, shift):
    return pltpu.roll(x, shift, 1), None


def _lane_roll_bwd(shift, _, g):
    return (pltpu.roll(g, (LANES - shift) % LANES, 1),)


_lane_roll.defvjp(_lane_roll_fwd, _lane_roll_bwd)


def _rms_n(x, g, n):
    ms = jnp.sum(x * x, axis=-1, keepdims=True) * (1.0 / n)
    return x * lax.rsqrt(ms + EPS) * g


def _softmax(s):
    m = jnp.max(s, axis=-1, keepdims=True)
    e = jnp.exp(s - m)
    return e / jnp.sum(e, axis=-1, keepdims=True)


def _rope(t, cos_t, sin_a, sin_b):
    return t * cos_t + _lane_roll(t, LANES - 16) * sin_a + _lane_roll(t, 16) * sin_b


def _mla_prep_fn(cq, ckv, kr, cos_t, sin_a, sin_b, cq_g, ckv_g, qg, kg, wuq, wkn, wv):
    cqn = _rms_n(cq, cq_g, QL)
    ckvn = _rms_n(ckv, ckv_g, KVL)
    lane = lax.broadcasted_iota(jnp.int32, kr.shape, 1)
    krm = jnp.where((lane >= NOPE) & (lane < QKH), kr, 0.0)
    qs, ks = [], []
    for h in range(H):
        qh = _rms_n(_mm(cqn, wuq[h]), qg, QKH)
        qs.append(_rope(qh, cos_t, sin_a, sin_b))
        kh = _rms_n(_mm(ckvn, wkn[h]) + krm, kg, QKH)
        ks.append(_rope(kh, cos_t, sin_a, sin_b))
    return jnp.concatenate(qs, axis=-1), jnp.concatenate(ks, axis=-1), _mm(ckvn, wv)


def _attn_pair_fn(q2, k2, v2, sg, row0):
    tq, s_len = q2.shape[0], k2.shape[0]
    rows = row0 + lax.broadcasted_iota(jnp.int32, (tq, s_len), 0)
    cols = lax.broadcasted_iota(jnp.int32, (tq, s_len), 1)
    mask = cols <= rows
    vlane = lax.broadcasted_iota(jnp.int32, (s_len, LANES), 1)
    o = jnp.zeros((tq, LANES), F32)
    for e in range(2):
        sl = slice(LANES * e, LANES * (e + 1))
        s = _mm_nt(q2[:, sl], k2[:, sl]) * (QKH ** -0.5)
        p = _softmax(jnp.where(mask, s, NEG_INF))
        ve = jnp.where((vlane >= VH * e) & (vlane < VH * (e + 1)), v2[:, sl], 0.0)
        o = o + _mm(p, ve)
    return o * jax.nn.silu(sg)


def _sg_fn(u, v, sgc, ln_g, ln_b, ws, bs):
    mu = jnp.mean(v, axis=-1, keepdims=True)
    xc = v - mu
    vn = xc * lax.rsqrt(jnp.mean(xc * xc, axis=-1, keepdims=True) + EPS) * ln_g + ln_b
    r = lax.broadcasted_iota(jnp.int32, (SGC, SGC), 0)
    c = lax.broadcasted_iota(jnp.int32, (SGC, SGC), 1)
    wt = [jnp.where(r >= c, w, 0.0) for w in ws]
    row_blocks = []
    for ch in range(u.shape[0] // SGC):
        col_blocks = []
        for g in range(SGG):
            blk = vn[SGC * ch:SGC * (ch + 1), LANES * g:LANES * (g + 1)]
            col_blocks.append(_mm(wt[g], blk) + bs[g])
        row_blocks.append(jnp.concatenate(col_blocks, axis=-1))
    mixed = jnp.concatenate(row_blocks, axis=0)
    return (u * mixed) * jax.nn.silu(sgc)


def _memkv_fn(mem, mem_g, wm, kg):
    kv = _mm(_rms_n(mem, mem_g, D), wm)
    ks = [_rms_n(kv[:, MHD * h:MHD * (h + 1)], kg, MHD) for h in range(MH)]
    return jnp.concatenate(ks, axis=-1), kv[:, MH * MHD:]


def _mem_fn(mq, sgd, k, v, qg):
    outs = []
    for h in range(MH):
        sl = slice(MHD * h, MHD * (h + 1))
        qh = _rms_n(mq[:, sl], qg, MHD)
        p = _softmax(_mm_nt(qh, k[:, sl]) * (MHD ** -0.5))
        outs.append(_mm(p, v[:, sl]))
    return jnp.concatenate(outs, axis=-1) * jax.nn.silu(sgd)


def _merge_fn(ys, logits, bm, wb, wo):
    merged = None
    for n in range(NB):
        z = jnp.concatenate([_mm(ys[n], wb[j][n]) for j in range(N_CHIPS)], axis=-1)
        gate = jax.nn.sigmoid(logits[:, D * n:D * (n + 1)] + bm[n])
        merged = gate * z if merged is None else merged + gate * z
    return _mm(merged, wo)


def _proj_call(x, g, w):
    s_len = x.shape[0]
    tm, tn = min(s_len, 1024), 512

    def body(x_ref, g_ref, w_ref, p_ref, h_ref):
        @pl.when(pl.program_id(1) == 0)
        def _():
            h_ref[...] = _rms_n(x_ref[...], g_ref[...], D).astype(h_ref.dtype)
        p_ref[...] = jnp.dot(h_ref[...], w_ref[...], preferred_element_type=F32)

    return pl.pallas_call(
        body, grid=(s_len // tm, NP // tn),
        in_specs=[_bs((tm, D), lambda i, j: (i, 0)), _bs((1, D), lambda i, j: (0, 0)), _bs((D, tn), lambda i, j: (0, j))],
        out_specs=[_bs((tm, tn), lambda i, j: (i, j)), _bs((tm, D), lambda i, j: (i, 0))],
        out_shape=[jax.ShapeDtypeStruct((s_len, NP), F32), jax.ShapeDtypeStruct((s_len, D), MM)],
        name="proj", compiler_params=_cparams(2))(x, g, w)


def _rope_tables(pos):
    half = ROPE // 2
    inv_freq = ROPE_THETA ** (-jnp.arange(half, dtype=F32) / half)
    ang = pos.astype(F32)[:, None] * inv_freq
    cos, sin = jnp.cos(ang), jnp.sin(ang)
    s_len = pos.shape[0]
    z = lambda n: jnp.zeros((s_len, n), F32)
    cos_t = jnp.concatenate([jnp.ones((s_len, NOPE), F32), cos, cos, z(LANES - QKH)], axis=1)
    sin_a = jnp.concatenate([z(NOPE), -sin, z(LANES - NOPE - half)], axis=1)
    sin_b = jnp.concatenate([z(NOPE + half), sin, z(LANES - QKH)], axis=1)
    return cos_t, sin_a, sin_b


def _mla_prep_specs(tm):
    row = lambda w, off: _bs((tm, w), lambda i: (i, off // w))
    full2 = lambda a, b: _bs((a, b), lambda i: (0, 0))
    full3 = lambda a, b, c: _bs((a, b, c), lambda i: (0, 0, 0))
    tab = _bs((tm, LANES), lambda i: (i, 0))
    return [row(QL, OFF_CQ), row(KVL, OFF_CKV), row(LANES, OFF_KR), tab, tab, tab,
            full2(1, QL), full2(1, KVL), full2(1, LANES), full2(1, LANES),
            full3(H, QL, LANES), full3(H, KVL, LANES), full2(KVL, H * LANES)]


def _mla_prep_args(body_refs, wdtype=None):
    (cq, ckv, kr, ct, sa, sb, cqg, ckvg, qg, kg, wuq, wkn, wv) = body_refs
    cast = (lambda a: a) if wdtype is None else (lambda a: a.astype(wdtype))
    return (cq[...], ckv[...], kr[...], ct[...], sa[...], sb[...], cqg[...], ckvg[...], qg[...], kg[...],
            [cast(wuq[h]) for h in range(H)], [cast(wkn[h]) for h in range(H)], cast(wv[...]))


def _mla_prep_call(proj, tabs, cq_g, ckv_g, qg, kg, wuq, wkn, wv):
    s_len = proj.shape[0]
    tm = min(s_len, 256)

    def body(*refs):
        q_ref, k_ref, v_ref = refs[13:]
        q, k, v = _mla_prep_fn(*_mla_prep_args(refs[:13]))
        q_ref[...] = q.astype(q_ref.dtype)
        k_ref[...] = k.astype(k_ref.dtype)
        v_ref[...] = v.astype(v_ref.dtype)

    out = _bs((tm, H * LANES), lambda i: (i, 0))
    return pl.pallas_call(
        body, grid=(s_len // tm,), in_specs=_mla_prep_specs(tm), out_specs=[out, out, out],
        out_shape=[jax.ShapeDtypeStruct((s_len, H * LANES), MM)] * 3,
        name="mla_prep", compiler_params=_cparams(1))(proj, proj, proj, *tabs, cq_g, ckv_g, qg, kg, wuq, wkn, wv)


def _mla_prep_bwd_call(proj, tabs, cq_g, ckv_g, qg, kg, wuq, wkn, wv, dq, dk, dv):
    s_len = proj.shape[0]
    tm = min(s_len, 256)

    def body(*refs):
        dq_ref, dk_ref, dv_ref = refs[13:16]
        dcq_ref, dckv_ref, dkr_ref, dcqg_ref, dckvg_ref, dqg_ref, dkg_ref, dwuq_ref, dwkn_ref, dwv_ref = refs[16:]
        _, vjp = jax.vjp(_mla_prep_fn, *_mla_prep_args(refs[:13], F32))
        (dcq, dckv, dkr, _, _, _, dcqg, dckvg, dqg, dkg, dwuq, dwkn, dwv) = vjp((dq_ref[...], dk_ref[...], dv_ref[...]))
        dcq_ref[...] = dcq.astype(dcq_ref.dtype)
        dckv_ref[...] = dckv.astype(dckv_ref.dtype)
        dkr_ref[...] = dkr.astype(dkr_ref.dtype)

        @pl.when(pl.program_id(0) == 0)
        def _():
            for r in (dcqg_ref, dckvg_ref, dqg_ref, dkg_ref, dwuq_ref, dwkn_ref, dwv_ref):
                r[...] = jnp.zeros_like(r)
        dcqg_ref[...] += dcqg
        dckvg_ref[...] += dckvg
        dqg_ref[...] += dqg
        dkg_ref[...] += dkg
        for h in range(H):
            dwuq_ref[h] += dwuq[h]
            dwkn_ref[h] += dwkn[h]
        dwv_ref[...] += dwv

    big = _bs((tm, H * LANES), lambda i: (i, 0))
    row = lambda w: _bs((tm, w), lambda i: (i, 0))
    full2 = lambda a, b: _bs((a, b), lambda i: (0, 0))
    full3 = lambda a, b, c: _bs((a, b, c), lambda i: (0, 0, 0))
    sd = jax.ShapeDtypeStruct
    return pl.pallas_call(
        body, grid=(s_len // tm,), in_specs=_mla_prep_specs(tm) + [big, big, big],
        out_specs=[row(QL), row(KVL), row(LANES), full2(1, QL), full2(1, KVL), full2(1, LANES), full2(1, LANES),
                   full3(H, QL, LANES), full3(H, KVL, LANES), full2(KVL, H * LANES)],
        out_shape=[sd((s_len, QL), MM), sd((s_len, KVL), MM), sd((s_len, LANES), MM), sd((1, QL), F32), sd((1, KVL), F32),
                   sd((1, LANES), F32), sd((1, LANES), F32), sd((H, QL, LANES), F32), sd((H, KVL, LANES), F32),
                   sd((KVL, H * LANES), F32)],
        name="mla_prep_bwd", compiler_params=_cparams(1))(proj, proj, proj, *tabs, cq_g, ckv_g, qg, kg, wuq, wkn, wv, dq, dk, dv)


def _attn_specs(s_len, tq):
    pair = 2 * LANES
    return [_bs((tq, pair), lambda p, i: (i, p)), _bs((s_len, pair), lambda p, i: (0, p)), _bs((s_len, pair), lambda p, i: (0, p)),
            _bs((tq, LANES), lambda p, i: (i, OFF_SG // LANES + p))]


def _attn_call(q, k, v, proj):
    s_len = q.shape[0]
    tq = min(s_len, 256)

    def body(q_ref, k_ref, v_ref, sg_ref, y_ref):
        for n in range(s_len // tq):
            @pl.when(pl.program_id(1) == n)
            def _():
                kl = (n + 1) * tq
                y_ref[...] = _attn_pair_fn(q_ref[...], k_ref[:kl, :], v_ref[:kl, :], sg_ref[...], n * tq).astype(y_ref.dtype)

    return pl.pallas_call(
        body, grid=(H // 2, s_len // tq), in_specs=_attn_specs(s_len, tq),
        out_specs=_bs((tq, LANES), lambda p, i: (i, p)), out_shape=jax.ShapeDtypeStruct((s_len, BW), MM),
        name="attn", compiler_params=_cparams(2))(q, k, v, proj)


def _attn_bwd_call(q, k, v, proj, dys):
    s_len = q.shape[0]
    tq = min(s_len, 256)
    pair = 2 * LANES

    def body(q_ref, k_ref, v_ref, sg_ref, dy_ref, dq_ref, dk_ref, dv_ref, dsg_ref):
        i = pl.program_id(1)

        @pl.when(i == 0)
        def _():
            dk_ref[...] = jnp.zeros_like(dk_ref)
            dv_ref[...] = jnp.zeros_like(dv_ref)

        for n in range(s_len // tq):
            @pl.when(i == n)
            def _():
                kl = (n + 1) * tq
                fn = functools.partial(_attn_pair_fn, row0=n * tq)
                _, vjp = jax.vjp(fn, q_ref[...].astype(F32), k_ref[:kl, :].astype(F32), v_ref[:kl, :].astype(F32), sg_ref[...])
                dq, dk, dv, dsg = vjp(dy_ref[...])
                dq_ref[...] = dq
                dsg_ref[...] = dsg.astype(dsg_ref.dtype)
                dk_ref[:kl, :] += dk
                dv_ref[:kl, :] += dv

    sd = jax.ShapeDtypeStruct
    return pl.pallas_call(
        body, grid=(H // 2, s_len // tq),
        in_specs=_attn_specs(s_len, tq) + [_bs((tq, LANES), lambda p, i: (i, p))],
        out_specs=[_bs((tq, pair), lambda p, i: (i, p)), _bs((s_len, pair), lambda p, i: (0, p)),
                   _bs((s_len, pair), lambda p, i: (0, p)), _bs((tq, LANES), lambda p, i: (i, p))],
        out_shape=[sd((s_len, H * LANES), F32), sd((s_len, H * LANES), F32), sd((s_len, H * LANES), F32), sd((s_len, BW), MM)],
        name="attn_bwd", compiler_params=_cparams(2))(q, k, v, proj, dys)


def _shift_down(a, n):
    r = lax.broadcasted_iota(jnp.int32, a.shape, 0)
    return jnp.where(r >= n, pltpu.roll(a, n, 0), 0.0)


def _shift_up(a, n):
    s_len = a.shape[0]
    r = lax.broadcasted_iota(jnp.int32, a.shape, 0)
    return jnp.where(r < s_len - n, pltpu.roll(a, s_len - n, 0), 0.0)


def _conv_specs(s_len):
    col = lambda off: _bs((s_len, LANES), lambda j: (0, off // LANES + j))
    return [col(OFF_CV), col(OFF_CV + CW), col(OFF_CV + 2 * CW), col(OFF_SG + BW),
            _bs((3, LANES), lambda j: (0, j)), _bs((1, LANES), lambda j: (0, j))]


def _conv_call(proj, cw, cb):
    s_len = proj.shape[0]

    def body(bg_ref, cg_ref, xi_ref, sg_ref, w_ref, b_ref, y_ref):
        z = cg_ref[...] * xi_ref[...]
        y = b_ref[...] + w_ref[0:1, :] * _shift_down(z, 2)
        y = y + w_ref[1:2, :] * _shift_down(z, 1)
        y = y + w_ref[2:3, :] * z
        y_ref[...] = ((bg_ref[...] * y) * jax.nn.silu(sg_ref[...])).astype(y_ref.dtype)

    return pl.pallas_call(
        body, grid=(CW // LANES,), in_specs=_conv_specs(s_len), out_specs=_bs((s_len, LANES), lambda j: (0, j)),
        out_shape=jax.ShapeDtypeStruct((s_len, CW), MM), name="conv", compiler_params=_cparams(1))(proj, proj, proj, proj, cw, cb)


def _conv_bwd_call(proj, cw, cb, dys):
    s_len = proj.shape[0]

    def body(bg_ref, cg_ref, xi_ref, sg_ref, w_ref, b_ref, dys_ref, dbg_ref, dcg_ref, dxi_ref, dsg_ref, dw_ref, db_ref):
        bg, cg, xi, sg = bg_ref[...], cg_ref[...], xi_ref[...], sg_ref[...]
        w0, w1, w2 = w_ref[0:1, :], w_ref[1:2, :], w_ref[2:3, :]
        z = cg * xi
        z1, z2 = _shift_down(z, 1), _shift_down(z, 2)
        y = b_ref[...] + w0 * z2
        y = y + w1 * z1
        y = y + w2 * z
        yb = bg * y
        sig = jax.nn.sigmoid(sg)
        silu = sg * sig
        dys_v = dys_ref[...]
        dsg_ref[...] = (dys_v * yb * (sig * (1.0 + sg * (1.0 - sig)))).astype(dsg_ref.dtype)
        dyb = dys_v * silu
        dbg_ref[...] = (dyb * y).astype(dbg_ref.dtype)
        dy = dyb * bg
        db_ref[...] = jnp.sum(dy, axis=0, keepdims=True)
        dw_ref[0:1, :] = jnp.sum(dy * z2, axis=0, keepdims=True)
        dw_ref[1:2, :] = jnp.sum(dy * z1, axis=0, keepdims=True)
        dw_ref[2:3, :] = jnp.sum(dy * z, axis=0, keepdims=True)
        dz = w2 * dy + w1 * _shift_up(dy, 1) + w0 * _shift_up(dy, 2)
        dcg_ref[...] = (dz * xi).astype(dcg_ref.dtype)
        dxi_ref[...] = (dz * cg).astype(dxi_ref.dtype)

    col = _bs((s_len, LANES), lambda j: (0, j))
    sd = jax.ShapeDtypeStruct
    return pl.pallas_call(
        body, grid=(CW // LANES,), in_specs=_conv_specs(s_len) + [col],
        out_specs=[col, col, col, col, _bs((3, LANES), lambda j: (0, j)), _bs((1, LANES), lambda j: (0, j))],
        out_shape=[sd((s_len, CW), MM)] * 4 + [sd((3, CW), F32), sd((1, CW), F32)],
        name="conv_bwd", compiler_params=_cparams(1))(proj, proj, proj, proj, cw, cb, dys)


def _sg_specs(tm):
    row = lambda off: _bs((tm, SGW), lambda i: (i, off // SGW))
    return [row(OFF_SGI), row(OFF_SGI + SGW), row(OFF_SG + 2 * BW), _bs((1, SGW), lambda i: (0, 0)), _bs((1, SGW), lambda i: (0, 0)),
            _bs((SGG, SGC, SGC), lambda i: (0, 0, 0)), _bs((SGG, SGC, 1), lambda i: (0, 0, 0))]


def _sg_args(refs):
    u, v, sg, lg, lb, ws, bs = refs
    return (u[...], v[...], sg[...], lg[...], lb[...], [ws[g] for g in range(SGG)], [bs[g] for g in range(SGG)])


def _sg_call(proj, ln_g, ln_b, ws, bs):
    s_len = proj.shape[0]
    tm = min(s_len, 256)

    def body(*refs):
        refs[7][...] = _sg_fn(*_sg_args(refs[:7])).astype(refs[7].dtype)

    return pl.pallas_call(
        body, grid=(s_len // tm,), in_specs=_sg_specs(tm), out_specs=_bs((tm, SGW), lambda i: (i, 0)),
        out_shape=jax.ShapeDtypeStruct((s_len, SGW), MM), name="sgmlp", compiler_params=_cparams(1))(proj, proj, proj, ln_g, ln_b, ws, bs)


def _sg_bwd_call(proj, ln_g, ln_b, ws, bs, dys):
    s_len = proj.shape[0]
    tm = min(s_len, 256)

    def body(*refs):
        dys_ref = refs[7]
        du_ref, dv_ref, dsg_ref, dlg_ref, dlb_ref, dws_ref, dbs_ref = refs[8:]
        _, vjp = jax.vjp(_sg_fn, *_sg_args(refs[:7]))
        du, dv, dsg, dlg, dlb, dws, dbs = vjp(dys_ref[...])
        du_ref[...] = du.astype(du_ref.dtype)
        dv_ref[...] = dv.astype(dv_ref.dtype)
        dsg_ref[...] = dsg.astype(dsg_ref.dtype)

        @pl.when(pl.program_id(0) == 0)
        def _():
            for r in (dlg_ref, dlb_ref, dws_ref, dbs_ref):
                r[...] = jnp.zeros_like(r)
        dlg_ref[...] += dlg
        dlb_ref[...] += dlb
        for g in range(SGG):
            dws_ref[g] += dws[g]
            dbs_ref[g] += dbs[g]

    row = _bs((tm, SGW), lambda i: (i, 0))
    sd = jax.ShapeDtypeStruct
    return pl.pallas_call(
        body, grid=(s_len // tm,), in_specs=_sg_specs(tm) + [row],
        out_specs=[row, row, row, _bs((1, SGW), lambda i: (0, 0)), _bs((1, SGW), lambda i: (0, 0)),
                   _bs((SGG, SGC, SGC), lambda i: (0, 0, 0)), _bs((SGG, SGC, 1), lambda i: (0, 0, 0))],
        out_shape=[sd((s_len, SGW), MM)] * 3 + [sd((1, SGW), F32), sd((1, SGW), F32), sd((SGG, SGC, SGC), F32), sd((SGG, SGC, 1), F32)],
        name="sgmlp_bwd", compiler_params=_cparams(1))(proj, proj, proj, ln_g, ln_b, ws, bs, dys)


def _memkv_call(mem, mem_g, wm, kg):
    m_len = mem.shape[0]

    def body(mem_ref, g_ref, w_ref, kg_ref, k_ref, v_ref):
        k, v = _memkv_fn(mem_ref[...], g_ref[...], w_ref[...], kg_ref[...])
        k_ref[...] = k.astype(k_ref.dtype)
        v_ref[...] = v.astype(v_ref.dtype)

    return pl.pallas_call(body, out_shape=[jax.ShapeDtypeStruct((m_len, MH * MHD), MM)] * 2, name="memkv",
                          compiler_params=pltpu.CompilerParams(vmem_limit_bytes=VMEM_LIMIT))(mem, mem_g, wm, kg)


def _memkv_bwd_call(mem, mem_g, wm, kg, dk, dv):
    def body(mem_ref, g_ref, w_ref, kg_ref, dk_ref, dv_ref, dg_ref, dw_ref, dkg_ref):
        _, vjp = jax.vjp(_memkv_fn, mem_ref[...], g_ref[...], w_ref[...].astype(F32), kg_ref[...])
        _, dg, dw, dkg = vjp((dk_ref[...], dv_ref[...]))
        dg_ref[...] = dg
        dw_ref[...] = dw
        dkg_ref[...] = dkg

    sd = jax.ShapeDtypeStruct
    return pl.pallas_call(body, out_shape=[sd((1, D), F32), sd((D, 2 * MH * MHD), F32), sd((1, MHD), F32)], name="memkv_bwd",
                          compiler_params=pltpu.CompilerParams(vmem_limit_bytes=VMEM_LIMIT))(mem, mem_g, wm, kg, dk, dv)


def _mem_specs(tm, m_len):
    w = MH * MHD
    return [_bs((tm, w), lambda i: (i, OFF_MQ // w)), _bs((tm, BW), lambda i: (i, (OFF_SG + 3 * BW) // BW)),
            _bs((m_len, w), lambda i: (0, 0)), _bs((m_len, w), lambda i: (0, 0)), _bs((1, MHD), lambda i: (0, 0))]


def _mem_call(proj, k, v, qg):
    s_len, m_len = proj.shape[0], k.shape[0]
    tm = min(s_len, 256)

    def body(mq_ref, sg_ref, k_ref, v_ref, qg_ref, y_ref):
        y_ref[...] = _mem_fn(mq_ref[...], sg_ref[...], k_ref[...], v_ref[...], qg_ref[...]).astype(y_ref.dtype)

    return pl.pallas_call(
        body, grid=(s_len // tm,), in_specs=_mem_specs(tm, m_len), out_specs=_bs((tm, BW), lambda i: (i, 0)),
        out_shape=jax.ShapeDtypeStruct((s_len, BW), MM), name="memattn", compiler_params=_cparams(1))(proj, proj, k, v, qg)


def _mem_bwd_call(proj, k, v, qg, dys):
    s_len, m_len = proj.shape[0], k.shape[0]
    tm = min(s_len, 256)
    w = MH * MHD

    def body(mq_ref, sg_ref, k_ref, v_ref, qg_ref, dys_ref, dmq_ref, dsg_ref, dk_ref, dv_ref, dqg_ref):
        _, vjp = jax.vjp(_mem_fn, mq_ref[...], sg_ref[...], k_ref[...].astype(F32), v_ref[...].astype(F32), qg_ref[...])
        dmq, dsg, dk, dv, dqg = vjp(dys_ref[...])
        dmq_ref[...] = dmq.astype(dmq_ref.dtype)
        dsg_ref[...] = dsg.astype(dsg_ref.dtype)

        @pl.when(pl.program_id(0) == 0)
        def _():
            for r in (dk_ref, dv_ref, dqg_ref):
                r[...] = jnp.zeros_like(r)
        dk_ref[...] += dk
        dv_ref[...] += dv
        dqg_ref[...] += dqg

    row = _bs((tm, BW), lambda i: (i, 0))
    kv = _bs((m_len, w), lambda i: (0, 0))
    sd = jax.ShapeDtypeStruct
    return pl.pallas_call(
        body, grid=(s_len // tm,), in_specs=_mem_specs(tm, m_len) + [row],
        out_specs=[row, row, kv, kv, _bs((1, MHD), lambda i: (0, 0))],
        out_shape=[sd((s_len, w), MM), sd((s_len, BW), MM), sd((m_len, w), F32), sd((m_len, w), F32), sd((1, MHD), F32)],
        name="memattn_bwd", compiler_params=_cparams(1))(proj, proj, k, v, qg, dys)


def _merge_specs(tm):
    row = _bs((tm, BW), lambda i: (i, 0))
    return [row, row, row, row, _bs((tm, NB * D), lambda i: (i, OFF_ML // (NB * D))), _bs((NB, D), lambda i: (0, 0)),
            _bs((N_CHIPS, NB, BW, D // N_CHIPS), lambda i: (0, 0, 0, 0)), _bs((D, D), lambda i: (0, 0))]


def _merge_call(ys, proj, bm, wb, wo, x):
    s_len = proj.shape[0]
    tm = min(s_len, 256)

    def body(ya, yb, yc, yd, lg_ref, bm_ref, wb_ref, wo_ref, x_ref, o_ref):
        out = _merge_fn([r[...] for r in (ya, yb, yc, yd)], lg_ref[...], [bm_ref[n:n + 1, :] for n in range(NB)],
                        [[wb_ref[j, n] for n in range(NB)] for j in range(N_CHIPS)], wo_ref[...])
        o_ref[...] = x_ref[...] + out

    xrow = _bs((tm, D), lambda i: (i, 0))
    return pl.pallas_call(
        body, grid=(s_len // tm,), in_specs=_merge_specs(tm) + [xrow], out_specs=xrow,
        out_shape=jax.ShapeDtypeStruct((s_len, D), F32), name="merge", compiler_params=_cparams(1))(*ys, proj, bm, wb, wo, x)


def _merge_bwd_call(ys, proj, bm, wb, wo, dout):
    s_len = proj.shape[0]
    tm = min(s_len, 256)

    def body(ya, yb, yc, yd, lg_ref, bm_ref, wb_ref, wo_ref, do_ref, dya, dyb, dyc, dyd, dlg_ref, dbm_ref, dwb_ref, dwo_ref):
        fn = lambda ys_, lg_, bm_, wb_, wo_: _merge_fn(ys_, lg_, bm_, wb_, wo_)
        _, vjp = jax.vjp(fn, [r[...].astype(F32) for r in (ya, yb, yc, yd)], lg_ref[...], [bm_ref[n:n + 1, :] for n in range(NB)],
                         [[wb_ref[j, n].astype(F32) for n in range(NB)] for j in range(N_CHIPS)], wo_ref[...].astype(F32))
        dys, dlg, dbm, dwb, dwo = vjp(do_ref[...])
        for r, d in zip((dya, dyb, dyc, dyd), dys):
            r[...] = d
        dlg_ref[...] = dlg.astype(dlg_ref.dtype)

        @pl.when(pl.program_id(0) == 0)
        def _():
            for r in (dbm_ref, dwb_ref, dwo_ref):
                r[...] = jnp.zeros_like(r)
        for n in range(NB):
            dbm_ref[n:n + 1, :] += dbm[n]
            for j in range(N_CHIPS):
                dwb_ref[j, n] += dwb[j][n]
        dwo_ref[...] += dwo

    row = _bs((tm, BW), lambda i: (i, 0))
    sd = jax.ShapeDtypeStruct
    wb_shape = (N_CHIPS, NB, BW, D // N_CHIPS)
    return pl.pallas_call(
        body, grid=(s_len // tm,), in_specs=_merge_specs(tm) + [_bs((tm, D), lambda i: (i, 0))],
        out_specs=[row, row, row, row, _bs((tm, NB * D), lambda i: (i, 0)), _bs((NB, D), lambda i: (0, 0)),
                   _bs(wb_shape, lambda i: (0, 0, 0, 0)), _bs((D, D), lambda i: (0, 0))],
        out_shape=[sd((s_len, BW), F32)] * 4 + [sd((s_len, NB * D), MM), sd((NB, D), F32), sd(wb_shape, F32), sd((D, D), F32)],
        name="merge_bwd", compiler_params=_cparams(1))(*ys, proj, bm, wb, wo, dout)


def _dh_call(dproj, w, x, g, dout):
    s_len = x.shape[0]
    tm, tk = min(s_len, 512), NP // 4

    def body(dp_ref, w_ref, x_ref, g_ref, do_ref, dx_ref, dg_ref, acc_ref):
        i, k = pl.program_id(0), pl.program_id(1)

        @pl.when(k == 0)
        def _():
            acc_ref[...] = jnp.zeros_like(acc_ref)
        acc_ref[...] += lax.dot_general(dp_ref[...], w_ref[...], (((1,), (1,)), ((), ())), preferred_element_type=F32)

        @pl.when(k == pl.num_programs(1) - 1)
        def _():
            _, vjp = jax.vjp(lambda x_, g_: _rms_n(x_, g_, D), x_ref[...], g_ref[...])
            dxr, dgr = vjp(acc_ref[...])
            dx_ref[...] = do_ref[...] + dxr

            @pl.when(i == 0)
            def _():
                dg_ref[...] = jnp.zeros_like(dg_ref)
            dg_ref[...] += dgr

    row = _bs((tm, D), lambda i, k: (i, 0))
    return pl.pallas_call(
        body, grid=(s_len // tm, NP // tk),
        in_specs=[_bs((tm, tk), lambda i, k: (i, k)), _bs((D, tk), lambda i, k: (0, k)), row, _bs((1, D), lambda i, k: (0, 0)), row],
        out_specs=[row, _bs((1, D), lambda i, k: (0, 0))],
        out_shape=[jax.ShapeDtypeStruct((s_len, D), F32), jax.ShapeDtypeStruct((1, D), F32)],
        scratch_shapes=[pltpu.VMEM((tm, D), F32)], name="dh", compiler_params=_cparams(2))(dproj, w, x, g, dout)


def _dw_call(h, dproj):
    s_len = h.shape[0]
    tn = 512

    def body(h_ref, dp_ref, o_ref):
        o_ref[...] = lax.dot_general(h_ref[...], dp_ref[...], (((0,), (0,)), ((), ())), preferred_element_type=F32)

    return pl.pallas_call(
        body, grid=(NP // tn,), in_specs=[_bs((s_len, D), lambda j: (0, 0)), _bs((s_len, tn), lambda j: (0, j))],
        out_specs=_bs((D, tn), lambda j: (0, j)), out_shape=jax.ShapeDtypeStruct((D, NP), F32),
        name="dw_in", compiler_params=_cparams(1))(h, dproj)


def _loss_call(y, target):
    s_len = y.shape[0]
    tm = min(s_len, 512)

    def body(y_ref, t_ref, dy_ref, l_ref):
        e = y_ref[...] - t_ref[...]
        dy_ref[...] = e * (1.0 / D)

        @pl.when(pl.program_id(0) == 0)
        def _():
            l_ref[...] = jnp.zeros_like(l_ref)
        l_ref[...] += jnp.sum(e * e, axis=0, keepdims=True)

    row = _bs((tm, D), lambda i: (i, 0))
    return pl.pallas_call(
        body, grid=(s_len // tm,), in_specs=[row, row], out_specs=[row, _bs((1, D), lambda i: (0, 0))],
        out_shape=[jax.ShapeDtypeStruct((s_len, D), F32), jax.ShapeDtypeStruct((1, D), F32)],
        name="loss", compiler_params=_cparams(1))(y, target)


def _adamw_call(w, g, m, v, name):
    rows, cols = w.shape
    tr = min(_row_tile(rows), 128)

    def body(w_ref, g_ref, m_ref, v_ref, d_ref, nm_ref, nv_ref):
        gv = g_ref[...]
        m2 = ADAM_B1 * m_ref[...] + (1.0 - ADAM_B1) * gv
        v2 = ADAM_B2 * v_ref[...] + (1.0 - ADAM_B2) * (gv * gv)
        m_hat = m2 / (1.0 - ADAM_B1 ** ADAM_STEP)
        v_hat = v2 / (1.0 - ADAM_B2 ** ADAM_STEP)
        d_ref[...] = -ADAM_LR * (m_hat / (jnp.sqrt(v_hat) + ADAM_EPS) + ADAM_WD * w_ref[...])
        nm_ref[...] = m2
        nv_ref[...] = v2

    blk = _bs((tr, cols), lambda i: (i, 0))
    return pl.pallas_call(
        body, grid=(rows // tr,), in_specs=[blk] * 4, out_specs=[blk] * 3,
        out_shape=[jax.ShapeDtypeStruct((rows, cols), F32)] * 3, name=name, compiler_params=_cparams(1))(w, g, m, v)


def _row_tile(rows):
    for cand in (512, 256, 128, 64, 32, 16, 8):
        if rows % cand == 0 and rows > cand:
            return cand
    return rows


def _pair_sum_call(grads, from_sibling, core, name):
    n = len(grads)

    def body(core_ref, *refs):
        for t in range(n):
            refs[2 * n + t][...] = (refs[t][...] + refs[n + t][...]).astype(MM)

    half = lambda g: (1, g.shape[1] // 2, g.shape[2])
    grid_spec = pltpu.PrefetchScalarGridSpec(
        num_scalar_prefetch=1, grid=(N_CHIPS,),
        in_specs=[pl.BlockSpec(half(g), lambda j, core_ref: (j, core_ref[0], 0)) for g in grads]
        + [pl.BlockSpec(half(g), lambda j, core_ref: (j, 0, 0)) for g in grads],
        out_specs=[pl.BlockSpec(half(g), lambda j, core_ref: (j, 0, 0)) for g in grads])
    return pl.pallas_call(
        body, grid_spec=grid_spec, out_shape=[jax.ShapeDtypeStruct((N_CHIPS,) + half(g)[1:], MM) for g in grads], name=name,
        compiler_params=_cparams(1))(core, *grads, *from_sibling)


def _owner_sum_call(chip_sums, from_chips, chip_core, name):
    n = len(chip_sums)
    steps = 4

    def body(ids_ref, *refs):
        for t in range(n):
            a, b = refs[t], refs[n + t]
            refs[2 * n + t][...] = ((a[0].astype(F32) + b[0].astype(F32)) + b[1].astype(F32)) + b[2].astype(F32)

    tile = lambda p: (p.shape[1] // steps, p.shape[2])
    grid_spec = pltpu.PrefetchScalarGridSpec(
        num_scalar_prefetch=1, grid=(steps,),
        in_specs=[pl.BlockSpec((1,) + tile(p), lambda i, ids_ref: (ids_ref[0], i, 0)) for p in chip_sums]
        + [pl.BlockSpec((3,) + tile(p), lambda i, ids_ref: (0, i, 0)) for p in chip_sums],
        out_specs=[pl.BlockSpec(tile(p), lambda i, ids_ref: (ids_ref[1] * steps + i, 0)) for p in chip_sums])
    return pl.pallas_call(
        body, grid_spec=grid_spec, out_shape=[jax.ShapeDtypeStruct((2 * p.shape[1], p.shape[2]), F32) for p in chip_sums],
        name=name, compiler_params=_cparams(1))(chip_core, *chip_sums, *from_chips)


def _sum8_call(parts):
    n, rows, cols = parts.shape
    tr = _row_tile(rows)

    def body(p_ref, o_ref):
        acc = p_ref[0]
        for k in range(1, n):
            acc = acc + p_ref[k]
        o_ref[...] = acc

    return pl.pallas_call(
        body, grid=(rows // tr,), in_specs=[_bs((n, tr, cols), lambda i: (0, i, 0))], out_specs=_bs((tr, cols), lambda i: (i, 0)),
        out_shape=jax.ShapeDtypeStruct((rows, cols), F32), name="sum_small_grads", compiler_params=_cparams(1))(parts)


_ANY = pl.BlockSpec(memory_space=pl.ANY)


def _all_gather8(blk, name):
    rows, cols = blk.shape

    def body(x_ref, out_ref, send_sems, recv_sems, local_sem):
        x, y, c = lax.axis_index("x"), lax.axis_index("y"), lax.axis_index("c")
        me, sibling = (x, y, c), (x, y, 1 - c)
        chips = [(1 - x, y), (x, 1 - y), (1 - x, 1 - y)]

        def slot(px, py, pc):
            return out_ref.at[4 * px + 2 * py + pc]

        def copy(k, block, to, src=None):
            return pltpu.make_async_remote_copy(
                src_ref=slot(*block) if src is None else src, dst_ref=slot(*block),
                send_sem=send_sems.at[k], recv_sem=recv_sems.at[k], device_id=to, device_id_type=MESH_ID)

        mine = pltpu.make_async_copy(x_ref, slot(*me), local_sem)
        mine.start()
        first = [copy(0, me, sibling, src=x_ref)]
        first += [copy(1 + j, me, (*chip, c), src=x_ref) for j, chip in enumerate(chips)]
        for cp in first:
            cp.start()
        passed = [copy(4 + j, (*chip, c), sibling) for j, chip in enumerate(chips)]
        for j, chip in enumerate(chips):
            copy(1 + j, (*chip, c), me).wait_recv()
            passed[j].start()
        copy(0, sibling, me).wait_recv()
        for j, chip in enumerate(chips):
            copy(4 + j, (*chip, 1 - c), me).wait_recv()
        for cp in first + passed:
            cp.wait_send()
        mine.wait()

    return pl.pallas_call(
        body, out_shape=jax.ShapeDtypeStruct((8, rows, cols), blk.dtype), in_specs=[_ANY], out_specs=_ANY,
        scratch_shapes=[pltpu.SemaphoreType.DMA((7,)), pltpu.SemaphoreType.DMA((7,)), pltpu.SemaphoreType.DMA],
        name=name)(blk)


def _half_rows(ref, lead, half, which):
    rows = pl.ds(pl.multiple_of(half * which, half), half)
    return ref.at[rows] if lead is None else ref.at[lead, rows]


def _gather_layer_call(layer, shards, name):
    n = len(shards)
    half = [s.shape[1] // 2 for s in shards]

    def body(*refs):
        srcs, outs = refs[:n], refs[n:2 * n]
        send_sems, recv_sems, local_sems = refs[2 * n:]
        x, y, c = lax.axis_index("x"), lax.axis_index("y"), lax.axis_index("c")
        sibling = (x, y, 1 - c)
        chips = [(1 - x, y), (x, 1 - y), (1 - x, 1 - y)]

        def slot(t, px, py, pc):
            return _half_rows(outs[t], 2 * px + py, half[t], pc)

        def copy(t, k, block, to, src=None):
            return pltpu.make_async_remote_copy(
                src_ref=slot(t, *block) if src is None else src, dst_ref=slot(t, *block),
                send_sem=send_sems.at[7 * t + k], recv_sem=recv_sems.at[7 * t + k], device_id=to, device_id_type=MESH_ID)

        mine = [_half_rows(srcs[t], layer, half[t], c) for t in range(n)]
        local = [pltpu.make_async_copy(mine[t], slot(t, x, y, c), local_sems.at[t]) for t in range(n)]
        for cp in local:
            cp.start()
        first = []
        for t in range(n):
            first.append(copy(t, 0, (x, y, c), sibling, src=mine[t]))
            first += [copy(t, 1 + j, (x, y, c), (*chip, c), src=mine[t]) for j, chip in enumerate(chips)]
        for cp in first:
            cp.start()
        passed = []
        for j, chip in enumerate(chips):
            for t in range(n):
                copy(t, 1 + j, (*chip, c), (x, y, c)).wait_recv()
                passed.append(copy(t, 4 + j, (*chip, c), sibling))
                passed[-1].start()
        for t in range(n):
            copy(t, 0, (x, y, 1 - c), (x, y, c)).wait_recv()
            for j, chip in enumerate(chips):
                copy(t, 4 + j, (*chip, 1 - c), (x, y, c)).wait_recv()
        for cp in first + passed:
            cp.wait_send()
        for cp in local:
            cp.wait()

    return pl.pallas_call(
        body, out_shape=[jax.ShapeDtypeStruct((N_CHIPS,) + s.shape[1:], s.dtype) for s in shards],
        in_specs=[_ANY] * n, out_specs=[_ANY] * n,
        scratch_shapes=[pltpu.SemaphoreType.DMA((7 * n,)), pltpu.SemaphoreType.DMA((7 * n,)), pltpu.SemaphoreType.DMA((n,))],
        name=name)(*shards)


_HBM = pl.BlockSpec(memory_space=pltpu.HBM)
_SEM = pl.BlockSpec(memory_space=pltpu.SEMAPHORE)
_ORDERED_EFFECT = pltpu.CompilerParams(has_side_effects=pltpu.SideEffectType.DATAFLOW_SIDE_EFFECTING)


_VMEM = pl.BlockSpec(memory_space=pltpu.VMEM)
_TOKEN = jax.ShapeDtypeStruct((8, LANES), F32)


def _in_hbm(a):
    return pltpu.with_memory_space_constraint(a, pltpu.HBM)


def _order(value, *tokens):
    return lax.optimization_barrier((value, *tokens))[0]


def _chip_scatter_start_call(chip_sums, name):
    n = len(chip_sums)

    def body(*refs):
        srcs, outs = refs[:n], refs[n:2 * n]
        send_sems, recv_sems, token = refs[2 * n:]
        x, y, c = lax.axis_index("x"), lax.axis_index("y"), lax.axis_index("c")
        chips = [(1 - x, y), (x, 1 - y), (1 - x, 1 - y)]
        for k, (cx, cy) in enumerate(chips):
            for t in range(n):
                pltpu.make_async_remote_copy(
                    src_ref=srcs[t].at[2 * cx + cy], dst_ref=outs[t].at[k], send_sem=send_sems.at[3 * t + k],
                    recv_sem=recv_sems.at[3 * t + k], device_id=(cx, cy, c), device_id_type=MESH_ID).start()
        token[...] = jnp.zeros_like(token)

    dma = pltpu.SemaphoreType.DMA
    return pl.pallas_call(
        body, out_shape=[pltpu.HBM((3,) + p.shape[1:], p.dtype) for p in chip_sums] + [dma((3 * n,)), dma((3 * n,)), _TOKEN],
        in_specs=[_HBM] * n, out_specs=[_HBM] * n + [_SEM, _SEM, _VMEM], name=name, compiler_params=_ORDERED_EFFECT,
    )(*[_in_hbm(p) for p in chip_sums])


def _chip_scatter_finish_call(chip_sums, bufs, send_sems, recv_sems, after, name):
    n = len(chip_sums)

    def body(*refs):
        srcs, ins, send_ref, recv_ref = refs[:n], refs[n:2 * n], refs[2 * n], refs[2 * n + 1]
        x, y, c = lax.axis_index("x"), lax.axis_index("y"), lax.axis_index("c")
        chips = [(1 - x, y), (x, 1 - y), (1 - x, 1 - y)]
        for k, (cx, cy) in enumerate(chips):
            for t in range(n):
                pltpu.make_async_remote_copy(
                    src_ref=srcs[t].at[2 * cx + cy], dst_ref=ins[t].at[k], send_sem=send_ref.at[3 * t + k],
                    recv_sem=recv_ref.at[3 * t + k], device_id=(cx, cy, c), device_id_type=MESH_ID).wait()

    return pl.pallas_call(
        body, out_shape=[pltpu.HBM(b.shape, b.dtype) for b in bufs],
        in_specs=[_HBM] * (2 * n) + [_SEM, _SEM, _ANY], out_specs=[_HBM] * n,
        input_output_aliases={n + t: t for t in range(n)}, name=name, compiler_params=_ORDERED_EFFECT,
    )(*[_in_hbm(p) for p in chip_sums], *bufs, send_sems, recv_sems, after)


def _place_own_call(layer, shards, chip_core, name):
    n = len(shards)

    def body(ids_ref, *refs):
        for t in range(n):
            refs[n + t][...] = refs[t][...]

    def blk(s):
        return (1, s.shape[1] // 2) + s.shape[2:]

    def imap_in(s):
        pad = (0,) * (s.ndim - 2)
        return lambda i, ids_ref: (layer, ids_ref[1]) + pad

    def imap_out(s):
        pad = (0,) * (s.ndim - 2)
        return lambda i, ids_ref: (ids_ref[0], ids_ref[1]) + pad

    grid_spec = pltpu.PrefetchScalarGridSpec(
        num_scalar_prefetch=1, grid=(1,), in_specs=[pl.BlockSpec(blk(s), imap_in(s)) for s in shards],
        out_specs=[pl.BlockSpec(blk(s), imap_out(s)) for s in shards])
    return pl.pallas_call(
        body, grid_spec=grid_spec, out_shape=[jax.ShapeDtypeStruct((N_CHIPS,) + s.shape[1:], s.dtype) for s in shards],
        name=name, compiler_params=_cparams(1))(chip_core, *shards)


def _gather_start_call(layer, shards, bufs, after, name):
    n = len(shards)
    half = [s.shape[1] // 2 for s in shards]

    def body(*refs):
        srcs, outs = refs[:n], refs[2 * n + 1:3 * n + 1]
        send_sems, recv_sib, recv_ici, token = refs[3 * n + 1:]
        x, y, c = lax.axis_index("x"), lax.axis_index("y"), lax.axis_index("c")
        chips = [(1 - x, y), (x, 1 - y), (1 - x, 1 - y)]
        for t in range(n):
            mine = _half_rows(srcs[t], layer, half[t], c)
            dst = _half_rows(outs[t], 2 * x + y, half[t], c)
            pltpu.make_async_remote_copy(src_ref=mine, dst_ref=dst, send_sem=send_sems.at[4 * t], recv_sem=recv_sib.at[t],
                                         device_id=(x, y, 1 - c), device_id_type=MESH_ID).start()
            for j, chip in enumerate(chips):
                pltpu.make_async_remote_copy(src_ref=mine, dst_ref=dst, send_sem=send_sems.at[4 * t + 1 + j],
                                             recv_sem=recv_ici.at[3 * t + j], device_id=(*chip, c), device_id_type=MESH_ID).start()
        token[...] = jnp.zeros_like(token)

    dma = pltpu.SemaphoreType.DMA
    return pl.pallas_call(
        body, out_shape=[pltpu.HBM(b.shape, b.dtype) for b in bufs] + [dma((4 * n,)), dma((n,)), dma((3 * n,)), _TOKEN],
        in_specs=[_HBM] * (2 * n) + [_ANY], out_specs=[_HBM] * n + [_SEM] * 3 + [_VMEM],
        input_output_aliases={n + t: t for t in range(n)}, name=name, compiler_params=_ORDERED_EFFECT,
    )(*[_in_hbm(s) for s in shards], *[_in_hbm(b) for b in bufs], after)


def _gather_forward_call(bufs, recv_ici, after, name):
    n = len(bufs)
    half = [b.shape[1] // 2 for b in bufs]

    def body(*refs):
        ins, recv_ici_ref = refs[:n], refs[n]
        outs = refs[n + 2:2 * n + 2]
        send_fwd, recv_fwd, token = refs[2 * n + 2:]
        x, y, c = lax.axis_index("x"), lax.axis_index("y"), lax.axis_index("c")
        chips = [(1 - x, y), (x, 1 - y), (1 - x, 1 - y)]
        for j, (cx, cy) in enumerate(chips):
            for t in range(n):
                landed = _half_rows(ins[t], 2 * cx + cy, half[t], c)
                dst = _half_rows(outs[t], 2 * cx + cy, half[t], c)
                pltpu.make_async_remote_copy(src_ref=landed, dst_ref=landed, send_sem=send_fwd.at[3 * t + j],
                                             recv_sem=recv_ici_ref.at[3 * t + j], device_id=(cx, cy, c),
                                             device_id_type=MESH_ID).wait_recv()
                pltpu.make_async_remote_copy(src_ref=landed, dst_ref=dst, send_sem=send_fwd.at[3 * t + j],
                                             recv_sem=recv_fwd.at[3 * t + j], device_id=(x, y, 1 - c),
                                             device_id_type=MESH_ID).start()
        token[...] = jnp.zeros_like(token)

    dma = pltpu.SemaphoreType.DMA
    return pl.pallas_call(
        body, out_shape=[pltpu.HBM(b.shape, b.dtype) for b in bufs] + [dma((3 * n,)), dma((3 * n,)), _TOKEN],
        in_specs=[_HBM] * n + [_SEM, _ANY], out_specs=[_HBM] * n + [_SEM] * 2 + [_VMEM],
        input_output_aliases={t: t for t in range(n)}, name=name, compiler_params=_ORDERED_EFFECT,
    )(*bufs, recv_ici, after)


def _gather_finish_call(layer, shards, bufs, send_sems, recv_sib, send_fwd, recv_fwd, after, name):
    n = len(bufs)
    half = [b.shape[1] // 2 for b in bufs]

    def body(*refs):
        srcs, ins = refs[:n], refs[n:2 * n]
        send_ref, recv_sib_ref, send_fwd_ref, recv_fwd_ref = refs[2 * n:2 * n + 4]
        x, y, c = lax.axis_index("x"), lax.axis_index("y"), lax.axis_index("c")
        chips = [(1 - x, y), (x, 1 - y), (1 - x, 1 - y)]
        sibling = (x, y, 1 - c)
        for t in range(n):
            mine = _half_rows(srcs[t], layer, half[t], c)
            for k in range(4):
                pltpu.make_async_remote_copy(src_ref=mine, dst_ref=mine, send_sem=send_ref.at[4 * t + k],
                                             recv_sem=recv_sib_ref.at[t], device_id=sibling, device_id_type=MESH_ID).wait_send()
            from_sibling = _half_rows(ins[t], 2 * x + y, half[t], 1 - c)
            pltpu.make_async_remote_copy(src_ref=from_sibling, dst_ref=from_sibling, send_sem=send_ref.at[4 * t],
                                         recv_sem=recv_sib_ref.at[t], device_id=sibling, device_id_type=MESH_ID).wait_recv()
            for j, (cx, cy) in enumerate(chips):
                sent = _half_rows(ins[t], 2 * cx + cy, half[t], c)
                passed = _half_rows(ins[t], 2 * cx + cy, half[t], 1 - c)
                pltpu.make_async_remote_copy(src_ref=sent, dst_ref=passed, send_sem=send_fwd_ref.at[3 * t + j],
                                             recv_sem=recv_fwd_ref.at[3 * t + j], device_id=sibling, device_id_type=MESH_ID).wait()

    return pl.pallas_call(
        body, out_shape=[pltpu.HBM(b.shape, b.dtype) for b in bufs],
        in_specs=[_HBM] * (2 * n) + [_SEM] * 4 + [_ANY], out_specs=[_HBM] * n,
        input_output_aliases={n + t: t for t in range(n)}, name=name, compiler_params=_ORDERED_EFFECT,
    )(*[_in_hbm(s) for s in shards], *bufs, send_sems, recv_sib, send_fwd, recv_fwd, after)


def _pair_exchange_call(grads, name):
    n = len(grads)
    half = [g.shape[1] // 2 for g in grads]

    def body(*refs):
        srcs, outs, send_sems, recv_sems = refs[:n], refs[n:2 * n], refs[2 * n], refs[2 * n + 1]
        x, y, c = lax.axis_index("x"), lax.axis_index("y"), lax.axis_index("c")
        copies = [pltpu.make_async_remote_copy(
            src_ref=srcs[t].at[:, pl.ds(pl.multiple_of(half[t] * (1 - c), half[t]), half[t])], dst_ref=outs[t],
            send_sem=send_sems.at[t], recv_sem=recv_sems.at[t], device_id=(x, y, 1 - c), device_id_type=MESH_ID) for t in range(n)]
        for cp in copies:
            cp.start()
        for cp in copies:
            cp.wait()

    return pl.pallas_call(
        body, out_shape=[jax.ShapeDtypeStruct((g.shape[0], g.shape[1] // 2, g.shape[2]), g.dtype) for g in grads],
        in_specs=[_ANY] * n, out_specs=[_ANY] * n,
        scratch_shapes=[pltpu.SemaphoreType.DMA((n,)), pltpu.SemaphoreType.DMA((n,))], name=name)(*grads)


def _chip_scatter_call(chip_sums, name):
    n = len(chip_sums)

    def body(*refs):
        srcs, outs, send_sems, recv_sems = refs[:n], refs[n:2 * n], refs[2 * n], refs[2 * n + 1]
        x, y, c = lax.axis_index("x"), lax.axis_index("y"), lax.axis_index("c")
        chips = [(1 - x, y), (x, 1 - y), (1 - x, 1 - y)]
        copies = [pltpu.make_async_remote_copy(
            src_ref=srcs[t].at[2 * cx + cy], dst_ref=outs[t].at[k], send_sem=send_sems.at[3 * t + k],
            recv_sem=recv_sems.at[3 * t + k], device_id=(cx, cy, c), device_id_type=MESH_ID)
            for k, (cx, cy) in enumerate(chips) for t in range(n)]
        for cp in copies:
            cp.start()
        for cp in copies:
            cp.wait()

    return pl.pallas_call(
        body, out_shape=[jax.ShapeDtypeStruct((3,) + p.shape[1:], p.dtype) for p in chip_sums],
        in_specs=[_ANY] * n, out_specs=[_ANY] * n,
        scratch_shapes=[pltpu.SemaphoreType.DMA((3 * n,)), pltpu.SemaphoreType.DMA((3 * n,))], name=name)(*chip_sums)


def _pair_gather_call(bufs, name):
    n = len(bufs)
    half = [b.shape[0] // 2 for b in bufs]

    def body(*refs):
        srcs, outs, send_sems, recv_sems = refs[:n], refs[n:2 * n], refs[2 * n], refs[2 * n + 1]
        x, y, c = lax.axis_index("x"), lax.axis_index("y"), lax.axis_index("c")
        for t in range(n):
            pltpu.make_async_remote_copy(
                src_ref=_half_rows(srcs[t], None, half[t], c), dst_ref=_half_rows(outs[t], None, half[t], c),
                send_sem=send_sems.at[t], recv_sem=recv_sems.at[t], device_id=(x, y, 1 - c), device_id_type=MESH_ID).start()
        for t in range(n):
            pltpu.make_async_remote_copy(
                src_ref=_half_rows(srcs[t], None, half[t], c), dst_ref=_half_rows(outs[t], None, half[t], 1 - c),
                send_sem=send_sems.at[t], recv_sem=recv_sems.at[t], device_id=(x, y, 1 - c), device_id_type=MESH_ID).wait()

    return pl.pallas_call(
        body, out_shape=[jax.ShapeDtypeStruct(b.shape, b.dtype) for b in bufs], in_specs=[_ANY] * n, out_specs=[_ANY] * n,
        input_output_aliases={t: t for t in range(n)},
        scratch_shapes=[pltpu.SemaphoreType.DMA((n,)), pltpu.SemaphoreType.DMA((n,))], name=name)(*bufs)


def _pack_rows(flats, dtype, row_multiple):
    flat = jnp.concatenate([f.reshape(-1).astype(dtype) for f in flats])
    n = flat.shape[0]
    rows = -(-n // PACK_W)
    rows = -(-rows // row_multiple) * row_multiple
    return jnp.pad(flat, (0, rows * PACK_W - n)).reshape(rows, PACK_W)


def _unpack(flat, shapes):
    out, off = [], 0
    for shp in shapes:
        n = math.prod(shp)
        out.append(flat[off:off + n].reshape(shp))
        off += n
    return out


def _f32_as_mm_bits(a):
    return lax.bitcast_convert_type(a, jnp.bfloat16).reshape(-1)


def _mm_bits_as_f32(flat, shape):
    return lax.bitcast_convert_type(flat.reshape(-1, 2), F32).reshape(shape)


def _w_in_to_aligned(w):
    z = lambda n: jnp.zeros((w.shape[0], n), w.dtype)
    return jnp.concatenate([w[:, R_ML:R_END], w[:, R_SG:R_ML], w[:, R_CV:R_SGI], w[:, R_SGI:R_MQ], w[:, R_MQ:R_SG],
                            w[:, R_CQ:R_CKV], w[:, R_CKV:R_KR], z(NOPE), w[:, R_KR:R_CV], z(LANES - QKH)], axis=1)


def _w_in_from_aligned(wa):
    return jnp.concatenate([wa[:, OFF_CQ:OFF_CKV], wa[:, OFF_CKV:OFF_KR], wa[:, OFF_KR + NOPE:OFF_KR + QKH], wa[:, OFF_CV:OFF_SGI],
                            wa[:, OFF_SGI:OFF_MQ], wa[:, OFF_MQ:OFF_CQ], wa[:, OFF_SG:OFF_CV], wa[:, OFF_ML:OFF_SG]], axis=1)


def _wuq_to_heads(w):
    w3 = w.reshape(QL, H, QKH)
    w3 = jnp.pad(w3, ((0, 0), (0, 0), (0, LANES - QKH)))
    return jnp.transpose(w3, (1, 0, 2))


def _wuq_from_heads(wh):
    return jnp.transpose(wh[:, :, :QKH], (1, 0, 2)).reshape(QL, H * QKH)


def _wukv_to_heads(w):
    w3 = w.reshape(KVL, H, NOPE + VH)
    wkn = jnp.transpose(jnp.pad(w3[:, :, :NOPE], ((0, 0), (0, 0), (0, LANES - NOPE))), (1, 0, 2))
    wv3 = w3[:, :, NOPE:]
    z = jnp.zeros((KVL, VH), w.dtype)
    cols = []
    for h in range(H):
        cols += [wv3[:, h], z] if h % 2 == 0 else [z, wv3[:, h]]
    return wkn, jnp.concatenate(cols, axis=1)


def _wukv_from_heads(wkn, wv):
    kn = jnp.transpose(wkn[:, :, :NOPE], (1, 0, 2))
    vs = jnp.stack([wv[:, LANES * h + VH * (h % 2):LANES * h + VH * (h % 2) + VH] for h in range(H)], axis=1)
    return jnp.concatenate([kn, vs], axis=2).reshape(KVL, H * (NOPE + VH))


def _layer_fwd(x, mem, tabs, p):
    proj, h = _proj_call(x, p["norm_g"], p["w_in"])
    q, k, v = _mla_prep_call(proj, tabs, p["cq_g"], p["ckv_g"], p["qg"], p["kg"], p["wuq"], p["wkn"], p["wv"])
    ya = _attn_call(q, k, v, proj)
    if p.get("after_attn") is not None:
        ya = _order(ya, p["after_attn"](ya))
    yb = _conv_call(proj, p["conv_w"], p["conv_b"])
    yc = _sg_call(proj, p["ln_g"], p["ln_b"], p["ws"], p["bs"])
    mk, mv = _memkv_call(mem, p["mem_g"], p["wm"], p["mkg"])
    yd = _mem_call(proj, mk, mv, p["mqg"])
    out = _merge_call((ya, yb, yc, yd), proj, p["bm"], p["wb"], p["wo"], x)
    return out, dict(x=x, proj=proj, h=h, q=q, k=k, v=v, ys=(ya, yb, yc, yd), mk=mk, mv=mv)


def _layer_bwd(dout, mem, tabs, p, sv, after_mla=None, on_grads=None):
    proj = sv["proj"]
    dya, dyb, dyc, dyd, dml, dbm, dwb, dwo = _merge_bwd_call(sv["ys"], proj, p["bm"], p["wb"], p["wo"], dout)
    dq, dk, dv, dsg_a = _attn_bwd_call(sv["q"], sv["k"], sv["v"], proj, dya)
    dcq, dckv, dkr, dcqg, dckvg, dqg, dkg, dwuq, dwkn, dwv = _mla_prep_bwd_call(
        proj, tabs, p["cq_g"], p["ckv_g"], p["qg"], p["kg"], p["wuq"], p["wkn"], p["wv"], dq, dk, dv)
    if after_mla is not None:
        dyb = _order(dyb, after_mla(dcq))
    dbg, dcg, dxi, dsg_b, dcw, dcb = _conv_bwd_call(proj, p["conv_w"], p["conv_b"], dyb)
    du, dvv, dsg_c, dlg, dlb, dws, dbs = _sg_bwd_call(proj, p["ln_g"], p["ln_b"], p["ws"], p["bs"], dyc)
    dmq, dsg_d, dmk, dmv, dmqg = _mem_bwd_call(proj, sv["mk"], sv["mv"], p["mqg"], dyd)
    dmem_g, dwm, dmkg = _memkv_bwd_call(mem, p["mem_g"], p["wm"], p["mkg"], dmk, dmv)
    dproj = jnp.concatenate([dml, dsg_a, dsg_b, dsg_c, dsg_d, dbg, dcg, dxi, du, dvv, dmq, dcq, dckv, dkr], axis=1)
    dw_in = _dw_call(sv["h"], dproj)
    grads = dict(cq_norm_g=dcqg[0], ckv_norm_g=dckvg[0], mla_q_norm_g=dqg[0, :QKH], mla_k_norm_g=dkg[0, :QKH],
                 conv_w=dcw, conv_b=dcb[0], sg_ln_g=dlg[0], sg_ln_b=dlb[0], w_spatial=dws, b_spatial=dbs[:, :, 0],
                 mem_norm_g=dmem_g[0], mem_q_norm_g=dmqg[0], mem_k_norm_g=dmkg[0], b_merge=dbm,
                 w_in_aligned=dw_in, wuq_heads=dwuq, wkn_heads=dwkn, wv_heads=dwv, w_mem_kv=dwm, w_branch_chips=dwb, w_out=dwo)
    if on_grads is not None:
        dproj = _order(dproj, on_grads(grads))
    dx, dnorm_g = _dh_call(dproj, p["w_in"], sv["x"], p["norm_g"], dout)
    grads["norm_g"] = dnorm_g[0]
    return dx, grads


def _chips_to_cols(a):
    return jnp.concatenate([a[j] for j in range(N_CHIPS)], axis=1)


def _cols_to_chips(a):
    cols = a.shape[1] // N_CHIPS
    return jnp.stack([a[:, cols * j:cols * (j + 1)] for j in range(N_CHIPS)])


def _layer_params(l, rep, gathered, conv_w, b_merge):
    pad_g = lambda g: jnp.pad(g, (0, LANES - QKH)).reshape(1, LANES)
    wkn, wv = _wukv_to_heads(_chips_to_cols(gathered["w_ukv"]))
    return dict(
        norm_g=rep["norm_g"][l].reshape(1, D), w_in=_w_in_to_aligned(_chips_to_cols(gathered["w_in"])),
        cq_g=rep["cq_norm_g"][l].reshape(1, QL), ckv_g=rep["ckv_norm_g"][l].reshape(1, KVL),
        qg=pad_g(rep["mla_q_norm_g"][l]), kg=pad_g(rep["mla_k_norm_g"][l]),
        wuq=_wuq_to_heads(_chips_to_cols(gathered["w_uq"])), wkn=wkn, wv=wv,
        conv_w=conv_w, conv_b=rep["conv_b"][l].reshape(1, CW),
        ln_g=rep["sg_ln_g"][l].reshape(1, SGW), ln_b=rep["sg_ln_b"][l].reshape(1, SGW),
        ws=rep["w_spatial"][l], bs=rep["b_spatial"][l].reshape(SGG, SGC, 1),
        mem_g=rep["mem_norm_g"][l].reshape(1, D), wm=gathered["w_mem_kv"].reshape(D, 2 * MH * MHD),
        mqg=rep["mem_q_norm_g"][l].reshape(1, MHD), mkg=rep["mem_k_norm_g"][l].reshape(1, MHD),
        bm=b_merge, wb=gathered["w_branch"], wo=gathered["w_out"].reshape(D, D))


def _forward_backward(x, mem, pos, target, params, bwd_hooks=None):
    tabs = _rope_tables(pos)
    params = list(params)
    saved = []
    act = x
    for l in range(DEPTH):
        if callable(params[l]):
            params[l] = params[l](saved[-1], act)
        act, sv = _layer_fwd(act, mem, tabs, params[l])
        saved.append(sv)
    dy, sq = _loss_call(act, target)
    grads = [None] * DEPTH
    for l in reversed(range(DEPTH)):
        dy, grads[l] = _layer_bwd(dy, mem, tabs, params[l], saved[l], **(bwd_hooks[l] if bwd_hooks else {}))
    return sq, dy, grads


_SHARDED_MM = ("w_in", "w_branch", "w_out", "w_mem_kv", "w_uq", "w_ukv")
_SHARDED_F32 = ("conv_w", "b_merge")
_REPLICATED = ("norm_g", "cq_norm_g", "ckv_norm_g", "mla_q_norm_g", "mla_k_norm_g", "conv_b", "sg_ln_g", "sg_ln_b",
               "w_spatial", "b_spatial", "mem_norm_g", "mem_q_norm_g", "mem_k_norm_g")
_ALL_REDUCED = _REPLICATED + _SHARDED_F32
_WEIGHTS = ("norm_g", "w_in", "cq_norm_g", "ckv_norm_g", "w_uq", "w_ukv", "mla_q_norm_g", "mla_k_norm_g", "conv_w", "conv_b",
            "sg_ln_g", "sg_ln_b", "w_spatial", "b_spatial", "mem_norm_g", "w_mem_kv", "mem_q_norm_g", "mem_k_norm_g",
            "b_merge", "w_branch", "w_out")
_BIG = ("w_in", "w_uq", "w_ukv", "w_mem_kv", "w_branch", "w_out")
_SMALL = tuple(n for n in _WEIGHTS if n not in _BIG)


def _gather_small_sharded(w):
    names = _SHARDED_F32
    packed = _pack_rows([w[n] for n in names], F32, 8)
    got = _all_gather8(packed, "gather_small_weights")
    per_chip = [_unpack(got[2 * j].reshape(-1), [w[n].shape for n in names]) for j in range(N_CHIPS)]
    return {n: jnp.concatenate([per_chip[j][t] for j in range(N_CHIPS)], axis=2) for t, n in enumerate(names)}


def _gather_layer(l, shards):
    srcs = [shards[n] for n in _SHARDED_MM]
    return dict(zip(_SHARDED_MM, _gather_layer_call(l, srcs, "gather_weights_l%d" % l)))


class _ReduceScatter:
    def __init__(self, layer):
        self.tag = "rs_l%d_" % layer

    def start(self, grads):
        c = lax.axis_index("c")
        tensors = [
            _cols_to_chips(_w_in_from_aligned(grads["w_in_aligned"])),
            grads["w_branch_chips"].reshape(N_CHIPS, NB * BW, D // N_CHIPS),
            grads["w_out"].reshape(N_CHIPS, D // N_CHIPS, D),
            grads["w_mem_kv"].reshape(N_CHIPS, D // N_CHIPS, 2 * MH * MHD),
            _cols_to_chips(_wuq_from_heads(grads["wuq_heads"])),
            _cols_to_chips(_wukv_from_heads(grads["wkn_heads"], grads["wv_heads"])),
        ]
        n = len(tensors)
        from_sibling = _pair_exchange_call(tensors, self.tag + "pair_exchange")
        self.chip_sums = _pair_sum_call(tensors, from_sibling, c.astype(jnp.int32).reshape(1), self.tag + "pair_sum")
        out = _chip_scatter_start_call(self.chip_sums, self.tag + "scatter_start")
        self.bufs, self.send_sems, self.recv_sems, token = out[:n], out[n], out[n + 1], out[n + 2]
        return token

    def finish(self, after):
        x, y, c = lax.axis_index("x"), lax.axis_index("y"), lax.axis_index("c")
        chip_core = jnp.stack([2 * x + y, c]).astype(jnp.int32)
        from_chips = _chip_scatter_finish_call(self.chip_sums, self.bufs, self.send_sems, self.recv_sems, after,
                                               self.tag + "scatter_finish")
        mine = _owner_sum_call(self.chip_sums, from_chips, chip_core, self.tag + "owner_sum")
        shard = dict(zip(_SHARDED_MM, _pair_gather_call(mine, self.tag + "pair_gather")))
        shard["w_branch"] = shard["w_branch"].reshape(NB, BW, D // N_CHIPS)
        return shard


def _all_reduce_small(g):
    packed = _pack_rows([g[n] for n in _ALL_REDUCED], F32, 64)
    got = _all_gather8(packed, "gather_small_grads")
    total = _sum8_call(got).reshape(-1)
    out = dict(zip(_ALL_REDUCED, _unpack(total, [g[n].shape for n in _ALL_REDUCED])))
    chip = 2 * lax.axis_index("x") + lax.axis_index("y")
    for n in _SHARDED_F32:
        size = out[n].shape[2] // N_CHIPS
        out[n] = lax.dynamic_slice_in_dim(out[n], chip * size, size, axis=2)
    return out


def _adamw_all(w, g, m, v):
    delta, new_m, new_v = {}, {}, {}
    for n in _BIG:
        shp = w[n].shape
        as2d = lambda a: a.reshape(-1, shp[-1])
        d, nm, nv = _adamw_call(as2d(w[n]), as2d(g[n]), as2d(m[n]), as2d(v[n]), "adamw_" + n)
        delta[n], new_m[n], new_v[n] = d.reshape(shp), nm.reshape(shp), nv.reshape(shp)
    shapes = [w[n].shape for n in _SMALL]
    pk = lambda t: _pack_rows([t[n] for n in _SMALL], F32, 64)
    d, nm, nv = _adamw_call(pk(w), pk(g), pk(m), pk(v), "adamw_small")
    for out, packed in ((delta, d), (new_m, nm), (new_v, nv)):
        out.update(zip(_SMALL, _unpack(packed.reshape(-1), shapes)))
    return delta, new_m, new_v


def kernel(x, mem, positions, norm_g, w_in, cq_norm_g, ckv_norm_g, w_uq, w_ukv, mla_q_norm_g, mla_k_norm_g, conv_w, conv_b, sg_ln_g, sg_ln_b, w_spatial, b_spatial, mem_norm_g, w_mem_kv, mem_q_norm_g, mem_k_norm_g, b_merge, w_branch, w_out, loss_target, m_norm_g, m_w_in, m_cq_norm_g, m_ckv_norm_g, m_w_uq, m_w_ukv, m_mla_q_norm_g, m_mla_k_norm_g, m_conv_w, m_conv_b, m_sg_ln_g, m_sg_ln_b, m_w_spatial, m_b_spatial, m_mem_norm_g, m_w_mem_kv, m_mem_q_norm_g, m_mem_k_norm_g, m_b_merge, m_w_branch, m_w_out, v_norm_g, v_w_in, v_cq_norm_g, v_ckv_norm_g, v_w_uq, v_w_ukv, v_mla_q_norm_g, v_mla_k_norm_g, v_conv_w, v_conv_b, v_sg_ln_g, v_sg_ln_b, v_w_spatial, v_b_spatial, v_mem_norm_g, v_w_mem_kv, v_mem_q_norm_g, v_mem_k_norm_g, v_b_merge, v_w_branch, v_w_out):
    w = dict(norm_g=norm_g, w_in=w_in, cq_norm_g=cq_norm_g, ckv_norm_g=ckv_norm_g, w_uq=w_uq, w_ukv=w_ukv,
             mla_q_norm_g=mla_q_norm_g, mla_k_norm_g=mla_k_norm_g, conv_w=conv_w, conv_b=conv_b, sg_ln_g=sg_ln_g,
             sg_ln_b=sg_ln_b, w_spatial=w_spatial, b_spatial=b_spatial, mem_norm_g=mem_norm_g, w_mem_kv=w_mem_kv,
             mem_q_norm_g=mem_q_norm_g, mem_k_norm_g=mem_k_norm_g, b_merge=b_merge, w_branch=w_branch, w_out=w_out)
    m = dict(norm_g=m_norm_g, w_in=m_w_in, cq_norm_g=m_cq_norm_g, ckv_norm_g=m_ckv_norm_g, w_uq=m_w_uq, w_ukv=m_w_ukv,
             mla_q_norm_g=m_mla_q_norm_g, mla_k_norm_g=m_mla_k_norm_g, conv_w=m_conv_w, conv_b=m_conv_b, sg_ln_g=m_sg_ln_g,
             sg_ln_b=m_sg_ln_b, w_spatial=m_w_spatial, b_spatial=m_b_spatial, mem_norm_g=m_mem_norm_g, w_mem_kv=m_w_mem_kv,
             mem_q_norm_g=m_mem_q_norm_g, mem_k_norm_g=m_mem_k_norm_g, b_merge=m_b_merge, w_branch=m_w_branch, w_out=m_w_out)
    v = dict(norm_g=v_norm_g, w_in=v_w_in, cq_norm_g=v_cq_norm_g, ckv_norm_g=v_ckv_norm_g, w_uq=v_w_uq, w_ukv=v_w_ukv,
             mla_q_norm_g=v_mla_q_norm_g, mla_k_norm_g=v_mla_k_norm_g, conv_w=v_conv_w, conv_b=v_conv_b, sg_ln_g=v_sg_ln_g,
             sg_ln_b=v_sg_ln_b, w_spatial=v_w_spatial, b_spatial=v_b_spatial, mem_norm_g=v_mem_norm_g, w_mem_kv=v_w_mem_kv,
             mem_q_norm_g=v_mem_q_norm_g, mem_k_norm_g=v_mem_k_norm_g, b_merge=v_b_merge, w_branch=v_w_branch, w_out=v_w_out)

    small = _gather_small_sharded(w)
    shards = {n: w[n].astype(MM) for n in _SHARDED_MM}
    srcs = [shards[n] for n in _SHARDED_MM]
    n_t = len(srcs)
    chip_core = jnp.stack([2 * lax.axis_index("x") + lax.axis_index("y"), lax.axis_index("c")]).astype(jnp.int32)
    gathered0 = _gather_layer(0, shards)
    bufs = _place_own_call(1, srcs, chip_core, "gather_l1_place_own")
    started = _gather_start_call(1, srcs, bufs, gathered0["w_ukv"], "gather_l1_start")
    bufs, send_sems, recv_sib, recv_ici = started[:n_t], started[n_t], started[n_t + 1], started[n_t + 2]
    x0 = _order(x[0], started[n_t + 3])
    passed = []

    def pass_on(ya0):
        passed.extend(_gather_forward_call(bufs, recv_ici, ya0, "gather_l1_forward"))
        return passed[n_t + 2]

    def layer1_params(saved0, act0):
        got = _gather_finish_call(1, srcs, passed[:n_t], send_sems, recv_sib, passed[n_t], passed[n_t + 1], act0, "gather_l1_finish")
        return _layer_params(1, w, dict(zip(_SHARDED_MM, got)), small["conv_w"][1], small["b_merge"][1])

    params = [dict(_layer_params(0, w, gathered0, small["conv_w"][0], small["b_merge"][0]), after_attn=pass_on), layer1_params]
    rs = [_ReduceScatter(l) for l in range(DEPTH)]
    shard_grads = {}

    def land_l1(value):
        shard_grads[1] = rs[1].finish(value)
        return shard_grads[1]["w_ukv"]

    hooks = [dict(on_grads=rs[0].start, after_mla=land_l1), dict(on_grads=rs[1].start)]
    sq, grad_x, layer_grads = _forward_backward(x0, mem[0], positions[0], loss_target[0], params, hooks)
    shard_grads[0] = rs[0].finish(grad_x)
    loss = lax.psum(0.5 / D * jnp.sum(sq), ("x", "y", "c"))

    g = {n: jnp.stack([dict(layer_grads[l], **shard_grads[l])[n] for l in range(DEPTH)]) for n in _WEIGHTS}
    g.update(_all_reduce_small(g))
    delta, new_m, new_v = _adamw_all(w, g, m, v)
    return (loss, grad_x[None], *[g[n] for n in _WEIGHTS], *[delta[n] for n in _WEIGHTS],
            *[new_m[n] for n in _WEIGHTS], *[new_v[n] for n in _WEIGHTS])
```

```python
import functools
import math

import jax
import jax.numpy as jnp
from jax import lax
from jax.experimental import pallas as pl
from jax.experimental.pallas import tpu as pltpu

F32 = jnp.float32
MM = jnp.bfloat16

D = 1024
DEPTH = 2
EPS = 1e-6
H = 8
NOPE = 64
ROPE = 32
QKH = 96
VH = 64
QL = 256
KVL = 128
ROPE_THETA = 10000.0
CW = 512
SGW = 512
SGG = 4
SGC = 128
MH = 4
MHD = 128
NB = 4
BW = 512
NEG_INF = -1e30
LANES = 128
N_CHIPS = 4

R_CQ, R_CKV, R_KR, R_CV, R_SGI, R_MQ, R_SG, R_ML, R_END = 0, 256, 384, 416, 1952, 2976, 3488, 5536, 9632
OFF_ML, OFF_SG, OFF_CV, OFF_SGI, OFF_MQ, OFF_CQ, OFF_CKV, OFF_KR, NP = 0, 4096, 6144, 7680, 8704, 9216, 9472, 9600, 9728

ADAM_LR = 0.001
ADAM_B1 = 0.9
ADAM_B2 = 0.999
ADAM_EPS = 1e-08
ADAM_WD = 0.01
ADAM_STEP = 10

VMEM_LIMIT = 56 * 1024 * 1024
PACK_W = 512
MESH_ID = pl.DeviceIdType.MESH


def _cparams(n_axes):
    return pltpu.CompilerParams(dimension_semantics=("arbitrary",) * n_axes, vmem_limit_bytes=VMEM_LIMIT)


def _bs(shape, imap):
    return pl.BlockSpec(shape, imap)


@jax.custom_vjp
def _mm(a, b):
    return jnp.dot(a.astype(MM), b.astype(MM), preferred_element_type=F32)


def _mm_fwd(a, b):
    return _mm(a, b), (a, b)


def _mm_bwd(res, g):
    a, b = res
    gm = g.astype(MM)
    da = lax.dot_general(gm, b.astype(MM), (((1,), (1,)), ((), ())), preferred_element_type=F32)
    db = lax.dot_general(a.astype(MM), gm, (((0,), (0,)), ((), ())), preferred_element_type=F32)
    return da.astype(a.dtype), db.astype(b.dtype)


_mm.defvjp(_mm_fwd, _mm_bwd)


@jax.custom_vjp
def _mm_nt(a, b):
    return lax.dot_general(a.astype(MM), b.astype(MM), (((1,), (1,)), ((), ())), preferred_element_type=F32)


def _mm_nt_fwd(a, b):
    return _mm_nt(a, b), (a, b)


def _mm_nt_bwd(res, g):
    a, b = res
    gm = g.astype(MM)
    da = jnp.dot(gm, b.astype(MM), preferred_element_type=F32)
    db = lax.dot_general(gm, a.astype(MM), (((0,), (0,)), ((), ())), preferred_element_type=F32)
    return da.astype(a.dtype), db.astype(b.dtype)


_mm_nt.defvjp(_mm_nt_fwd, _mm_nt_bwd)


@functools.partial(jax.custom_vjp, nondiff_argnums=(1,))
def _lane_roll(x, shift):
    return pltpu.roll(x, shift, 1)


def _lane_roll_fwd(x, shift):
    return pltpu.roll(x, shift, 1), None


def _lane_roll_bwd(shift, _, g):
    return (pltpu.roll(g, (LANES - shift) % LANES, 1),)


_lane_roll.defvjp(_lane_roll_fwd, _lane_roll_bwd)


def _rms_n(x, g, n):
    ms = jnp.sum(x * x, axis=-1, keepdims=True) * (1.0 / n)
    return x * lax.rsqrt(ms + EPS) * g


def _softmax(s):
    m = jnp.max(s, axis=-1, keepdims=True)
    e = jnp.exp(s - m)
    return e / jnp.sum(e, axis=-1, keepdims=True)


def _rope(t, cos_t, sin_a, sin_b):
    return t * cos_t + _lane_roll(t, LANES - 16) * sin_a + _lane_roll(t, 16) * sin_b


def _mla_prep_fn(cq, ckv, kr, cos_t, sin_a, sin_b, cq_g, ckv_g, qg, kg, wuq, wkn, wv):
    cqn = _rms_n(cq, cq_g, QL)
    ckvn = _rms_n(ckv, ckv_g, KVL)
    lane = lax.broadcasted_iota(jnp.int32, kr.shape, 1)
    krm = jnp.where((lane >= NOPE) & (lane < QKH), kr, 0.0)
    qs, ks = [], []
    for h in range(H):
        qh = _rms_n(_mm(cqn, wuq[h]), qg, QKH)
        qs.append(_rope(qh, cos_t, sin_a, sin_b))
        kh = _rms_n(_mm(ckvn, wkn[h]) + krm, kg, QKH)
        ks.append(_rope(kh, cos_t, sin_a, sin_b))
    return jnp.concatenate(qs, axis=-1), jnp.concatenate(ks, axis=-1), _mm(ckvn, wv)


def _attn_pair_fn(q2, k2, v2, sg, row0):
    tq, s_len = q2.shape[0], k2.shape[0]
    rows = row0 + lax.broadcasted_iota(jnp.int32, (tq, s_len), 0)
    cols = lax.broadcasted_iota(jnp.int32, (tq, s_len), 1)
    mask = cols <= rows
    vlane = lax.broadcasted_iota(jnp.int32, (s_len, LANES), 1)
    o = jnp.zeros((tq, LANES), F32)
    for e in range(2):
        sl = slice(LANES * e, LANES * (e + 1))
        s = _mm_nt(q2[:, sl], k2[:, sl]) * (QKH ** -0.5)
        p = _softmax(jnp.where(mask, s, NEG_INF))
        ve = jnp.where((vlane >= VH * e) & (vlane < VH * (e + 1)), v2[:, sl], 0.0)
        o = o + _mm(p, ve)
    return o * jax.nn.silu(sg)


def _sg_fn(u, v, sgc, ln_g, ln_b, ws, bs):
    mu = jnp.mean(v, axis=-1, keepdims=True)
    xc = v - mu
    vn = xc * lax.rsqrt(jnp.mean(xc * xc, axis=-1, keepdims=True) + EPS) * ln_g + ln_b
    r = lax.broadcasted_iota(jnp.int32, (SGC, SGC), 0)
    c = lax.broadcasted_iota(jnp.int32, (SGC, SGC), 1)
    wt = [jnp.where(r >= c, w, 0.0) for w in ws]
    row_blocks = []
    for ch in range(u.shape[0] // SGC):
        col_blocks = []
        for g in range(SGG):
            blk = vn[SGC * ch:SGC * (ch + 1), LANES * g:LANES * (g + 1)]
            col_blocks.append(_mm(wt[g], blk) + bs[g])
        row_blocks.append(jnp.concatenate(col_blocks, axis=-1))
    mixed = jnp.concatenate(row_blocks, axis=0)
    return (u * mixed) * jax.nn.silu(sgc)


def _memkv_fn(mem, mem_g, wm, kg):
    kv = _mm(_rms_n(mem, mem_g, D), wm)
    ks = [_rms_n(kv[:, MHD * h:MHD * (h + 1)], kg, MHD) for h in range(MH)]
    return jnp.concatenate(ks, axis=-1), kv[:, MH * MHD:]


def _mem_fn(mq, sgd, k, v, qg):
    outs = []
    for h in range(MH):
        sl = slice(MHD * h, MHD * (h + 1))
        qh = _rms_n(mq[:, sl], qg, MHD)
        p = _softmax(_mm_nt(qh, k[:, sl]) * (MHD ** -0.5))
        outs.append(_mm(p, v[:, sl]))
    return jnp.concatenate(outs, axis=-1) * jax.nn.silu(sgd)


def _merge_fn(ys, logits, bm, wb, wo):
    merged = None
    for n in range(NB):
        z = jnp.concatenate([_mm(ys[n], wb[j][n]) for j in range(N_CHIPS)], axis=-1)
        gate = jax.nn.sigmoid(logits[:, D * n:D * (n + 1)] + bm[n])
        merged = gate * z if merged is None else merged + gate * z
    return _mm(merged, wo)


def _proj_call(x, g, w):
    s_len = x.shape[0]
    tm, tn = min(s_len, 1024), 512

    def body(x_ref, g_ref, w_ref, p_ref, h_ref):
        @pl.when(pl.program_id(1) == 0)
        def _():
            h_ref[...] = _rms_n(x_ref[...], g_ref[...], D).astype(h_ref.dtype)
        p_ref[...] = jnp.dot(h_ref[...], w_ref[...], preferred_element_type=F32)

    return pl.pallas_call(
        body, grid=(s_len // tm, NP // tn),
        in_specs=[_bs((tm, D), lambda i, j: (i, 0)), _bs((1, D), lambda i, j: (0, 0)), _bs((D, tn), lambda i, j: (0, j))],
        out_specs=[_bs((tm, tn), lambda i, j: (i, j)), _bs((tm, D), lambda i, j: (i, 0))],
        out_shape=[jax.ShapeDtypeStruct((s_len, NP), F32), jax.ShapeDtypeStruct((s_len, D), MM)],
        name="proj", compiler_params=_cparams(2))(x, g, w)


def _rope_tables(pos):
    half = ROPE // 2
    inv_freq = ROPE_THETA ** (-jnp.arange(half, dtype=F32) / half)
    ang = pos.astype(F32)[:, None] * inv_freq
    cos, sin = jnp.cos(ang), jnp.sin(ang)
    s_len = pos.shape[0]
    z = lambda n: jnp.zeros((s_len, n), F32)
    cos_t = jnp.concatenate([jnp.ones((s_len, NOPE), F32), cos, cos, z(LANES - QKH)], axis=1)
    sin_a = jnp.concatenate([z(NOPE), -sin, z(LANES - NOPE - half)], axis=1)
    sin_b = jnp.concatenate([z(NOPE + half), sin, z(LANES - QKH)], axis=1)
    return cos_t, sin_a, sin_b


def _mla_prep_specs(tm):
    row = lambda w, off: _bs((tm, w), lambda i: (i, off // w))
    full2 = lambda a, b: _bs((a, b), lambda i: (0, 0))
    full3 = lambda a, b, c: _bs((a, b, c), lambda i: (0, 0, 0))
    tab = _bs((tm, LANES), lambda i: (i, 0))
    return [row(QL, OFF_CQ), row(KVL, OFF_CKV), row(LANES, OFF_KR), tab, tab, tab,
            full2(1, QL), full2(1, KVL), full2(1, LANES), full2(1, LANES),
            full3(H, QL, LANES), full3(H, KVL, LANES), full2(KVL, H * LANES)]


def _mla_prep_args(body_refs, wdtype=None):
    (cq, ckv, kr, ct, sa, sb, cqg, ckvg, qg, kg, wuq, wkn, wv) = body_refs
    cast = (lambda a: a) if wdtype is None else (lambda a: a.astype(wdtype))
    return (cq[...], ckv[...], kr[...], ct[...], sa[...], sb[...], cqg[...], ckvg[...], qg[...], kg[...],
            [cast(wuq[h]) for h in range(H)], [cast(wkn[h]) for h in range(H)], cast(wv[...]))


def _mla_prep_call(proj, tabs, cq_g, ckv_g, qg, kg, wuq, wkn, wv):
    s_len = proj.shape[0]
    tm = min(s_len, 256)

    def body(*refs):
        q_ref, k_ref, v_ref = refs[13:]
        q, k, v = _mla_prep_fn(*_mla_prep_args(refs[:13]))
        q_ref[...] = q.astype(q_ref.dtype)
        k_ref[...] = k.astype(k_ref.dtype)
        v_ref[...] = v.astype(v_ref.dtype)

    out = _bs((tm, H * LANES), lambda i: (i, 0))
    return pl.pallas_call(
        body, grid=(s_len // tm,), in_specs=_mla_prep_specs(tm), out_specs=[out, out, out],
        out_shape=[jax.ShapeDtypeStruct((s_len, H * LANES), MM)] * 3,
        name="mla_prep", compiler_params=_cparams(1))(proj, proj, proj, *tabs, cq_g, ckv_g, qg, kg, wuq, wkn, wv)


def _mla_prep_bwd_call(proj, tabs, cq_g, ckv_g, qg, kg, wuq, wkn, wv, dq, dk, dv):
    s_len = proj.shape[0]
    tm = min(s_len, 256)

    def body(*refs):
        dq_ref, dk_ref, dv_ref = refs[13:16]
        dcq_ref, dckv_ref, dkr_ref, dcqg_ref, dckvg_ref, dqg_ref, dkg_ref, dwuq_ref, dwkn_ref, dwv_ref = refs[16:]
        _, vjp = jax.vjp(_mla_prep_fn, *_mla_prep_args(refs[:13], F32))
        (dcq, dckv, dkr, _, _, _, dcqg, dckvg, dqg, dkg, dwuq, dwkn, dwv) = vjp((dq_ref[...], dk_ref[...], dv_ref[...]))
        dcq_ref[...] = dcq.astype(dcq_ref.dtype)
        dckv_ref[...] = dckv.astype(dckv_ref.dtype)
        dkr_ref[...] = dkr.astype(dkr_ref.dtype)

        @pl.when(pl.program_id(0) == 0)
        def _():
            for r in (dcqg_ref, dckvg_ref, dqg_ref, dkg_ref, dwuq_ref, dwkn_ref, dwv_ref):
                r[...] = jnp.zeros_like(r)
        dcqg_ref[...] += dcqg
        dckvg_ref[...] += dckvg
        dqg_ref[...] += dqg
        dkg_ref[...] += dkg
        for h in range(H):
            dwuq_ref[h] += dwuq[h]
            dwkn_ref[h] += dwkn[h]
        dwv_ref[...] += dwv

    big = _bs((tm, H * LANES), lambda i: (i, 0))
    row = lambda w: _bs((tm, w), lambda i: (i, 0))
    full2 = lambda a, b: _bs((a, b), lambda i: (0, 0))
    full3 = lambda a, b, c: _bs((a, b, c), lambda i: (0, 0, 0))
    sd = jax.ShapeDtypeStruct
    return pl.pallas_call(
        body, grid=(s_len // tm,), in_specs=_mla_prep_specs(tm) + [big, big, big],
        out_specs=[row(QL), row(KVL), row(LANES), full2(1, QL), full2(1, KVL), full2(1, LANES), full2(1, LANES),
                   full3(H, QL, LANES), full3(H, KVL, LANES), full2(KVL, H * LANES)],
        out_shape=[sd((s_len, QL), MM), sd((s_len, KVL), MM), sd((s_len, LANES), MM), sd((1, QL), F32), sd((1, KVL), F32),
                   sd((1, LANES), F32), sd((1, LANES), F32), sd((H, QL, LANES), F32), sd((H, KVL, LANES), F32),
                   sd((KVL, H * LANES), F32)],
        name="mla_prep_bwd", compiler_params=_cparams(1))(proj, proj, proj, *tabs, cq_g, ckv_g, qg, kg, wuq, wkn, wv, dq, dk, dv)


def _attn_specs(s_len, tq):
    pair = 2 * LANES
    return [_bs((tq, pair), lambda p, i: (i, p)), _bs((s_len, pair), lambda p, i: (0, p)), _bs((s_len, pair), lambda p, i: (0, p)),
            _bs((tq, LANES), lambda p, i: (i, OFF_SG // LANES + p))]


def _attn_call(q, k, v, proj):
    s_len = q.shape[0]
    tq = min(s_len, 256)

    def body(q_ref, k_ref, v_ref, sg_ref, y_ref):
        for n in range(s_len // tq):
            @pl.when(pl.program_id(1) == n)
            def _():
                kl = (n + 1) * tq
                y_ref[...] = _attn_pair_fn(q_ref[...], k_ref[:kl, :], v_ref[:kl, :], sg_ref[...], n * tq).astype(y_ref.dtype)

    return pl.pallas_call(
        body, grid=(H // 2, s_len // tq), in_specs=_attn_specs(s_len, tq),
        out_specs=_bs((tq, LANES), lambda p, i: (i, p)), out_shape=jax.ShapeDtypeStruct((s_len, BW), MM),
        name="attn", compiler_params=_cparams(2))(q, k, v, proj)


def _attn_bwd_call(q, k, v, proj, dys):
    s_len = q.shape[0]
    tq = min(s_len, 256)
    pair = 2 * LANES

    def body(q_ref, k_ref, v_ref, sg_ref, dy_ref, dq_ref, dk_ref, dv_ref, dsg_ref):
        i = pl.program_id(1)

        @pl.when(i == 0)
        def _():
            dk_ref[...] = jnp.zeros_like(dk_ref)
            dv_ref[...] = jnp.zeros_like(dv_ref)

        for n in range(s_len // tq):
            @pl.when(i == n)
            def _():
                kl = (n + 1) * tq
                fn = functools.partial(_attn_pair_fn, row0=n * tq)
                _, vjp = jax.vjp(fn, q_ref[...].astype(F32), k_ref[:kl, :].astype(F32), v_ref[:kl, :].astype(F32), sg_ref[...])
                dq, dk, dv, dsg = vjp(dy_ref[...])
                dq_ref[...] = dq
                dsg_ref[...] = dsg.astype(dsg_ref.dtype)
                dk_ref[:kl, :] += dk
                dv_ref[:kl, :] += dv

    sd = jax.ShapeDtypeStruct
    return pl.pallas_call(
        body, grid=(H // 2, s_len // tq),
        in_specs=_attn_specs(s_len, tq) + [_bs((tq, LANES), lambda p, i: (i, p))],
        out_specs=[_bs((tq, pair), lambda p, i: (i, p)), _bs((s_len, pair), lambda p, i: (0, p)),
                   _bs((s_len, pair), lambda p, i: (0, p)), _bs((tq, LANES), lambda p, i: (i, p))],
        out_shape=[sd((s_len, H * LANES), F32), sd((s_len, H * LANES), F32), sd((s_len, H * LANES), F32), sd((s_len, BW), MM)],
        name="attn_bwd", compiler_params=_cparams(2))(q, k, v, proj, dys)


def _shift_down(a, n):
    r = lax.broadcasted_iota(jnp.int32, a.shape, 0)
    return jnp.where(r >= n, pltpu.roll(a, n, 0), 0.0)


def _shift_up(a, n):
    s_len = a.shape[0]
    r = lax.broadcasted_iota(jnp.int32, a.shape, 0)
    return jnp.where(r < s_len - n, pltpu.roll(a, s_len - n, 0), 0.0)


def _conv_specs(s_len):
    col = lambda off: _bs((s_len, LANES), lambda j: (0, off // LANES + j))
    return [col(OFF_CV), col(OFF_CV + CW), col(OFF_CV + 2 * CW), col(OFF_SG + BW),
            _bs((3, LANES), lambda j: (0, j)), _bs((1, LANES), lambda j: (0, j))]


def _conv_call(proj, cw, cb):
    s_len = proj.shape[0]

    def body(bg_ref, cg_ref, xi_ref, sg_ref, w_ref, b_ref, y_ref):
        z = cg_ref[...] * xi_ref[...]
        y = b_ref[...] + w_ref[0:1, :] * _shift_down(z, 2)
        y = y + w_ref[1:2, :] * _shift_down(z, 1)
        y = y + w_ref[2:3, :] * z
        y_ref[...] = ((bg_ref[...] * y) * jax.nn.silu(sg_ref[...])).astype(y_ref.dtype)

    return pl.pallas_call(
        body, grid=(CW // LANES,), in_specs=_conv_specs(s_len), out_specs=_bs((s_len, LANES), lambda j: (0, j)),
        out_shape=jax.ShapeDtypeStruct((s_len, CW), MM), name="conv", compiler_params=_cparams(1))(proj, proj, proj, proj, cw, cb)


def _conv_bwd_call(proj, cw, cb, dys):
    s_len = proj.shape[0]

    def body(bg_ref, cg_ref, xi_ref, sg_ref, w_ref, b_ref, dys_ref, dbg_ref, dcg_ref, dxi_ref, dsg_ref, dw_ref, db_ref):
        bg, cg, xi, sg = bg_ref[...], cg_ref[...], xi_ref[...], sg_ref[...]
        w0, w1, w2 = w_ref[0:1, :], w_ref[1:2, :], w_ref[2:3, :]
        z = cg * xi
        z1, z2 = _shift_down(z, 1), _shift_down(z, 2)
        y = b_ref[...] + w0 * z2
        y = y + w1 * z1
        y = y + w2 * z
        yb = bg * y
        sig = jax.nn.sigmoid(sg)
        silu = sg * sig
        dys_v = dys_ref[...]
        dsg_ref[...] = (dys_v * yb * (sig * (1.0 + sg * (1.0 - sig)))).astype(dsg_ref.dtype)
        dyb = dys_v * silu
        dbg_ref[...] = (dyb * y).astype(dbg_ref.dtype)
        dy = dyb * bg
        db_ref[...] = jnp.sum(dy, axis=0, keepdims=True)
        dw_ref[0:1, :] = jnp.sum(dy * z2, axis=0, keepdims=True)
        dw_ref[1:2, :] = jnp.sum(dy * z1, axis=0, keepdims=True)
        dw_ref[2:3, :] = jnp.sum(dy * z, axis=0, keepdims=True)
        dz = w2 * dy + w1 * _shift_up(dy, 1) + w0 * _shift_up(dy, 2)
        dcg_ref[...] = (dz * xi).astype(dcg_ref.dtype)
        dxi_ref[...] = (dz * cg).astype(dxi_ref.dtype)

    col = _bs((s_len, LANES), lambda j: (0, j))
    sd = jax.ShapeDtypeStruct
    return pl.pallas_call(
        body, grid=(CW // LANES,), in_specs=_conv_specs(s_len) + [col],
        out_specs=[col, col, col, col, _bs((3, LANES), lambda j: (0, j)), _bs((1, LANES), lambda j: (0, j))],
        out_shape=[sd((s_len, CW), MM)] * 4 + [sd((3, CW), F32), sd((1, CW), F32)],
        name="conv_bwd", compiler_params=_cparams(1))(proj, proj, proj, proj, cw, cb, dys)


def _sg_specs(tm):
    row = lambda off: _bs((tm, SGW), lambda i: (i, off // SGW))
    return [row(OFF_SGI), row(OFF_SGI + SGW), row(OFF_SG + 2 * BW), _bs((1, SGW), lambda i: (0, 0)), _bs((1, SGW), lambda i: (0, 0)),
            _bs((SGG, SGC, SGC), lambda i: (0, 0, 0)), _bs((SGG, SGC, 1), lambda i: (0, 0, 0))]


def _sg_args(refs):
    u, v, sg, lg, lb, ws, bs = refs
    return (u[...], v[...], sg[...], lg[...], lb[...], [ws[g] for g in range(SGG)], [bs[g] for g in range(SGG)])


def _sg_call(proj, ln_g, ln_b, ws, bs):
    s_len = proj.shape[0]
    tm = min(s_len, 256)

    def body(*refs):
        refs[7][...] = _sg_fn(*_sg_args(refs[:7])).astype(refs[7].dtype)

    return pl.pallas_call(
        body, grid=(s_len // tm,), in_specs=_sg_specs(tm), out_specs=_bs((tm, SGW), lambda i: (i, 0)),
        out_shape=jax.ShapeDtypeStruct((s_len, SGW), MM), name="sgmlp", compiler_params=_cparams(1))(proj, proj, proj, ln_g, ln_b, ws, bs)


def _sg_bwd_call(proj, ln_g, ln_b, ws, bs, dys):
    s_len = proj.shape[0]
    tm = min(s_len, 256)

    def body(*refs):
        dys_ref = refs[7]
        du_ref, dv_ref, dsg_ref, dlg_ref, dlb_ref, dws_ref, dbs_ref = refs[8:]
        _, vjp = jax.vjp(_sg_fn, *_sg_args(refs[:7]))
        du, dv, dsg, dlg, dlb, dws, dbs = vjp(dys_ref[...])
        du_ref[...] = du.astype(du_ref.dtype)
        dv_ref[...] = dv.astype(dv_ref.dtype)
        dsg_ref[...] = dsg.astype(dsg_ref.dtype)

        @pl.when(pl.program_id(0) == 0)
        def _():
            for r in (dlg_ref, dlb_ref, dws_ref, dbs_ref):
                r[...] = jnp.zeros_like(r)
        dlg_ref[...] += dlg
        dlb_ref[...] += dlb
        for g in range(SGG):
            dws_ref[g] += dws[g]
            dbs_ref[g] += dbs[g]

    row = _bs((tm, SGW), lambda i: (i, 0))
    sd = jax.ShapeDtypeStruct
    return pl.pallas_call(
        body, grid=(s_len // tm,), in_specs=_sg_specs(tm) + [row],
        out_specs=[row, row, row, _bs((1, SGW), lambda i: (0, 0)), _bs((1, SGW), lambda i: (0, 0)),
                   _bs((SGG, SGC, SGC), lambda i: (0, 0, 0)), _bs((SGG, SGC, 1), lambda i: (0, 0, 0))],
        out_shape=[sd((s_len, SGW), MM)] * 3 + [sd((1, SGW), F32), sd((1, SGW), F32), sd((SGG, SGC, SGC), F32), sd((SGG, SGC, 1), F32)],
        name="sgmlp_bwd", compiler_params=_cparams(1))(proj, proj, proj, ln_g, ln_b, ws, bs, dys)


def _memkv_call(mem, mem_g, wm, kg):
    m_len = mem.shape[0]

    def body(mem_ref, g_ref, w_ref, kg_ref, k_ref, v_ref):
        k, v = _memkv_fn(mem_ref[...], g_ref[...], w_ref[...], kg_ref[...])
        k_ref[...] = k.astype(k_ref.dtype)
        v_ref[...] = v.astype(v_ref.dtype)

    return pl.pallas_call(body, out_shape=[jax.ShapeDtypeStruct((m_len, MH * MHD), MM)] * 2, name="memkv",
                          compiler_params=pltpu.CompilerParams(vmem_limit_bytes=VMEM_LIMIT))(mem, mem_g, wm, kg)


def _memkv_bwd_call(mem, mem_g, wm, kg, dk, dv):
    def body(mem_ref, g_ref, w_ref, kg_ref, dk_ref, dv_ref, dg_ref, dw_ref, dkg_ref):
        _, vjp = jax.vjp(_memkv_fn, mem_ref[...], g_ref[...], w_ref[...].astype(F32), kg_ref[...])
        _, dg, dw, dkg = vjp((dk_ref[...], dv_ref[...]))
        dg_ref[...] = dg
        dw_ref[...] = dw
        dkg_ref[...] = dkg

    sd = jax.ShapeDtypeStruct
    return pl.pallas_call(body, out_shape=[sd((1, D), F32), sd((D, 2 * MH * MHD), F32), sd((1, MHD), F32)], name="memkv_bwd",
                          compiler_params=pltpu.CompilerParams(vmem_limit_bytes=VMEM_LIMIT))(mem, mem_g, wm, kg, dk, dv)


def _mem_specs(tm, m_len):
    w = MH * MHD
    return [_bs((tm, w), lambda i: (i, OFF_MQ // w)), _bs((tm, BW), lambda i: (i, (OFF_SG + 3 * BW) // BW)),
            _bs((m_len, w), lambda i: (0, 0)), _bs((m_len, w), lambda i: (0, 0)), _bs((1, MHD), lambda i: (0, 0))]


def _mem_call(proj, k, v, qg):
    s_len, m_len = proj.shape[0], k.shape[0]
    tm = min(s_len, 256)

    def body(mq_ref, sg_ref, k_ref, v_ref, qg_ref, y_ref):
        y_ref[...] = _mem_fn(mq_ref[...], sg_ref[...], k_ref[...], v_ref[...], qg_ref[...]).astype(y_ref.dtype)

    return pl.pallas_call(
        body, grid=(s_len // tm,), in_specs=_mem_specs(tm, m_len), out_specs=_bs((tm, BW), lambda i: (i, 0)),
        out_shape=jax.ShapeDtypeStruct((s_len, BW), MM), name="memattn", compiler_params=_cparams(1))(proj, proj, k, v, qg)


def _mem_bwd_call(proj, k, v, qg, dys):
    s_len, m_len = proj.shape[0], k.shape[0]
    tm = min(s_len, 256)
    w = MH * MHD

    def body(mq_ref, sg_ref, k_ref, v_ref, qg_ref, dys_ref, dmq_ref, dsg_ref, dk_ref, dv_ref, dqg_ref):
        _, vjp = jax.vjp(_mem_fn, mq_ref[...], sg_ref[...], k_ref[...].astype(F32), v_ref[...].astype(F32), qg_ref[...])
        dmq, dsg, dk, dv, dqg = vjp(dys_ref[...])
        dmq_ref[...] = dmq.astype(dmq_ref.dtype)
        dsg_ref[...] = dsg.astype(dsg_ref.dtype)

        @pl.when(pl.program_id(0) == 0)
        def _():
            for r in (dk_ref, dv_ref, dqg_ref):
                r[...] = jnp.zeros_like(r)
        dk_ref[...] += dk
        dv_ref[...] += dv
        dqg_ref[...] += dqg

    row = _bs((tm, BW), lambda i: (i, 0))
    kv = _bs((m_len, w), lambda i: (0, 0))
    sd = jax.ShapeDtypeStruct
    return pl.pallas_call(
        body, grid=(s_len // tm,), in_specs=_mem_specs(tm, m_len) + [row],
        out_specs=[row, row, kv, kv, _bs((1, MHD), lambda i: (0, 0))],
        out_shape=[sd((s_len, w), MM), sd((s_len, BW), MM), sd((m_len, w), F32), sd((m_len, w), F32), sd((1, MHD), F32)],
        name="memattn_bwd", compiler_params=_cparams(1))(proj, proj, k, v, qg, dys)


def _merge_specs(tm):
    row = _bs((tm, BW), lambda i: (i, 0))
    return [row, row, row, row, _bs((tm, NB * D), lambda i: (i, OFF_ML // (NB * D))), _bs((NB, D), lambda i: (0, 0)),
            _bs((N_CHIPS, NB, BW, D // N_CHIPS), lambda i: (0, 0, 0, 0)), _bs((D, D), lambda i: (0, 0))]


def _merge_call(ys, proj, bm, wb, wo, x):
    s_len = proj.shape[0]
    tm = min(s_len, 256)

    def body(ya, yb, yc, yd, lg_ref, bm_ref, wb_ref, wo_ref, x_ref, o_ref):
        out = _merge_fn([r[...] for r in (ya, yb, yc, yd)], lg_ref[...], [bm_ref[n:n + 1, :] for n in range(NB)],
                        [[wb_ref[j, n] for n in range(NB)] for j in range(N_CHIPS)], wo_ref[...])
        o_ref[...] = x_ref[...] + out

    xrow = _bs((tm, D), lambda i: (i, 0))
    return pl.pallas_call(
        body, grid=(s_len // tm,), in_specs=_merge_specs(tm) + [xrow], out_specs=xrow,
        out_shape=jax.ShapeDtypeStruct((s_len, D), F32), name="merge", compiler_params=_cparams(1))(*ys, proj, bm, wb, wo, x)


def _merge_bwd_call(ys, proj, bm, wb, wo, dout):
    s_len = proj.shape[0]
    tm = min(s_len, 256)

    def body(ya, yb, yc, yd, lg_ref, bm_ref, wb_ref, wo_ref, do_ref, dya, dyb, dyc, dyd, dlg_ref, dbm_ref, dwb_ref, dwo_ref):
        fn = lambda ys_, lg_, bm_, wb_, wo_: _merge_fn(ys_, lg_, bm_, wb_, wo_)
        _, vjp = jax.vjp(fn, [r[...].astype(F32) for r in (ya, yb, yc, yd)], lg_ref[...], [bm_ref[n:n + 1, :] for n in range(NB)],
                         [[wb_ref[j, n].astype(F32) for n in range(NB)] for j in range(N_CHIPS)], wo_ref[...].astype(F32))
        dys, dlg, dbm, dwb, dwo = vjp(do_ref[...])
        for r, d in zip((dya, dyb, dyc, dyd), dys):
            r[...] = d
        dlg_ref[...] = dlg.astype(dlg_ref.dtype)

        @pl.when(pl.program_id(0) == 0)
        def _():
            for r in (dbm_ref, dwb_ref, dwo_ref):
                r[...] = jnp.zeros_like(r)
        for n in range(NB):
            dbm_ref[n:n + 1, :] += dbm[n]
            for j in range(N_CHIPS):
                dwb_ref[j, n] += dwb[j][n]
        dwo_ref[...] += dwo

    row = _bs((tm, BW), lambda i: (i, 0))
    sd = jax.ShapeDtypeStruct
    wb_shape = (N_CHIPS, NB, BW, D // N_CHIPS)
    return pl.pallas_call(
        body, grid=(s_len // tm,), in_specs=_merge_specs(tm) + [_bs((tm, D), lambda i: (i, 0))],
        out_specs=[row, row, row, row, _bs((tm, NB * D), lambda i: (i, 0)), _bs((NB, D), lambda i: (0, 0)),
                   _bs(wb_shape, lambda i: (0, 0, 0, 0)), _bs((D, D), lambda i: (0, 0))],
        out_shape=[sd((s_len, BW), F32)] * 4 + [sd((s_len, NB * D), MM), sd((NB, D), F32), sd(wb_shape, F32), sd((D, D), F32)],
        name="merge_bwd", compiler_params=_cparams(1))(*ys, proj, bm, wb, wo, dout)


def _dh_call(dproj, w, x, g, dout):
    s_len = x.shape[0]
    tm, tk = min(s_len, 512), NP // 4

    def body(dp_ref, w_ref, x_ref, g_ref, do_ref, dx_ref, dg_ref, acc_ref):
        i, k = pl.program_id(0), pl.program_id(1)

        @pl.when(k == 0)
        def _():
            acc_ref[...] = jnp.zeros_like(acc_ref)
        acc_ref[...] += lax.dot_general(dp_ref[...], w_ref[...], (((1,), (1,)), ((), ())), preferred_element_type=F32)

        @pl.when(k == pl.num_programs(1) - 1)
        def _():
            _, vjp = jax.vjp(lambda x_, g_: _rms_n(x_, g_, D), x_ref[...], g_ref[...])
            dxr, dgr = vjp(acc_ref[...])
            dx_ref[...] = do_ref[...] + dxr

            @pl.when(i == 0)
            def _():
                dg_ref[...] = jnp.zeros_like(dg_ref)
            dg_ref[...] += dgr

    row = _bs((tm, D), lambda i, k: (i, 0))
    return pl.pallas_call(
        body, grid=(s_len // tm, NP // tk),
        in_specs=[_bs((tm, tk), lambda i, k: (i, k)), _bs((D, tk), lambda i, k: (0, k)), row, _bs((1, D), lambda i, k: (0, 0)), row],
        out_specs=[row, _bs((1, D), lambda i, k: (0, 0))],
        out_shape=[jax.ShapeDtypeStruct((s_len, D), F32), jax.ShapeDtypeStruct((1, D), F32)],
        scratch_shapes=[pltpu.VMEM((tm, D), F32)], name="dh", compiler_params=_cparams(2))(dproj, w, x, g, dout)


def _dw_call(h, dproj):
    s_len = h.shape[0]
    tn = 512

    def body(h_ref, dp_ref, o_ref):
        o_ref[...] = lax.dot_general(h_ref[...], dp_ref[...], (((0,), (0,)), ((), ())), preferred_element_type=F32)

    return pl.pallas_call(
        body, grid=(NP // tn,), in_specs=[_bs((s_len, D), lambda j: (0, 0)), _bs((s_len, tn), lambda j: (0, j))],
        out_specs=_bs((D, tn), lambda j: (0, j)), out_shape=jax.ShapeDtypeStruct((D, NP), F32),
        name="dw_in", compiler_params=_cparams(1))(h, dproj)


def _loss_call(y, target):
    s_len = y.shape[0]
    tm = min(s_len, 512)

    def body(y_ref, t_ref, dy_ref, l_ref):
        e = y_ref[...] - t_ref[...]
        dy_ref[...] = e * (1.0 / D)

        @pl.when(pl.program_id(0) == 0)
        def _():
            l_ref[...] = jnp.zeros_like(l_ref)
        l_ref[...] += jnp.sum(e * e, axis=0, keepdims=True)

    row = _bs((tm, D), lambda i: (i, 0))
    return pl.pallas_call(
        body, grid=(s_len // tm,), in_specs=[row, row], out_specs=[row, _bs((1, D), lambda i: (0, 0))],
        out_shape=[jax.ShapeDtypeStruct((s_len, D), F32), jax.ShapeDtypeStruct((1, D), F32)],
        name="loss", compiler_params=_cparams(1))(y, target)


def _adamw_call(w, g, m, v, name):
    rows, cols = w.shape
    tr = min(_row_tile(rows), 128)

    def body(w_ref, g_ref, m_ref, v_ref, d_ref, nm_ref, nv_ref):
        gv = g_ref[...]
        m2 = ADAM_B1 * m_ref[...] + (1.0 - ADAM_B1) * gv
        v2 = ADAM_B2 * v_ref[...] + (1.0 - ADAM_B2) * (gv * gv)
        m_hat = m2 / (1.0 - ADAM_B1 ** ADAM_STEP)
        v_hat = v2 / (1.0 - ADAM_B2 ** ADAM_STEP)
        d_ref[...] = -ADAM_LR * (m_hat / (jnp.sqrt(v_hat) + ADAM_EPS) + ADAM_WD * w_ref[...])
        nm_ref[...] = m2
        nv_ref[...] = v2

    blk = _bs((tr, cols), lambda i: (i, 0))
    return pl.pallas_call(
        body, grid=(rows // tr,), in_specs=[blk] * 4, out_specs=[blk] * 3,
        out_shape=[jax.ShapeDtypeStruct((rows, cols), F32)] * 3, name=name, compiler_params=_cparams(1))(w, g, m, v)


def _row_tile(rows):
    for cand in (512, 256, 128, 64, 32, 16, 8):
        if rows % cand == 0 and rows > cand:
            return cand
    return rows


def _pair_sum_call(grads, from_sibling, core, name):
    n = len(grads)

    def body(core_ref, *refs):
        for t in range(n):
            refs[2 * n + t][...] = (refs[t][...] + refs[n + t][...]).astype(MM)

    half = lambda g: (1, g.shape[1] // 2, g.shape[2])
    grid_spec = pltpu.PrefetchScalarGridSpec(
        num_scalar_prefetch=1, grid=(N_CHIPS,),
        in_specs=[pl.BlockSpec(half(g), lambda j, core_ref: (j, core_ref[0], 0)) for g in grads]
        + [pl.BlockSpec(half(g), lambda j, core_ref: (j, 0, 0)) for g in grads],
        out_specs=[pl.BlockSpec(half(g), lambda j, core_ref: (j, 0, 0)) for g in grads])
    return pl.pallas_call(
        body, grid_spec=grid_spec, out_shape=[jax.ShapeDtypeStruct((N_CHIPS,) + half(g)[1:], MM) for g in grads], name=name,
        compiler_params=_cparams(1))(core, *grads, *from_sibling)


def _owner_sum_call(chip_sums, from_chips, chip_core, name):
    n = len(chip_sums)
    steps = 4

    def body(ids_ref, *refs):
        for t in range(n):
            a, b = refs[t], refs[n + t]
            refs[2 * n + t][...] = ((a[0].astype(F32) + b[0].astype(F32)) + b[1].astype(F32)) + b[2].astype(F32)

    tile = lambda p: (p.shape[1] // steps, p.shape[2])
    grid_spec = pltpu.PrefetchScalarGridSpec(
        num_scalar_prefetch=1, grid=(steps,),
        in_specs=[pl.BlockSpec((1,) + tile(p), lambda i, ids_ref: (ids_ref[0], i, 0)) for p in chip_sums]
        + [pl.BlockSpec((3,) + tile(p), lambda i, ids_ref: (0, i, 0)) for p in chip_sums],
        out_specs=[pl.BlockSpec(tile(p), lambda i, ids_ref: (ids_ref[1] * steps + i, 0)) for p in chip_sums])
    return pl.pallas_call(
        body, grid_spec=grid_spec, out_shape=[jax.ShapeDtypeStruct((2 * p.shape[1], p.shape[2]), F32) for p in chip_sums],
        name=name, compiler_params=_cparams(1))(chip_core, *chip_sums, *from_chips)


def _sum8_call(parts):
    n, rows, cols = parts.shape
    tr = _row_tile(rows)

    def body(p_ref, o_ref):
        acc = p_ref[0]
        for k in range(1, n):
            acc = acc + p_ref[k]
        o_ref[...] = acc

    return pl.pallas_call(
        body, grid=(rows // tr,), in_specs=[_bs((n, tr, cols), lambda i: (0, i, 0))], out_specs=_bs((tr, cols), lambda i: (i, 0)),
        out_shape=jax.ShapeDtypeStruct((rows, cols), F32), name="sum_small_grads", compiler_params=_cparams(1))(parts)


_ANY = pl.BlockSpec(memory_space=pl.ANY)


def _all_gather8(blk, name):
    rows, cols = blk.shape

    def body(x_ref, out_ref, send_sems, recv_sems, local_sem):
        x, y, c = lax.axis_index("x"), lax.axis_index("y"), lax.axis_index("c")
        me, sibling = (x, y, c), (x, y, 1 - c)
        chips = [(1 - x, y), (x, 1 - y), (1 - x, 1 - y)]

        def slot(px, py, pc):
            return out_ref.at[4 * px + 2 * py + pc]

        def copy(k, block, to, src=None):
            return pltpu.make_async_remote_copy(
                src_ref=slot(*block) if src is None else src, dst_ref=slot(*block),
                send_sem=send_sems.at[k], recv_sem=recv_sems.at[k], device_id=to, device_id_type=MESH_ID)

        mine = pltpu.make_async_copy(x_ref, slot(*me), local_sem)
        mine.start()
        first = [copy(0, me, sibling, src=x_ref)]
        first += [copy(1 + j, me, (*chip, c), src=x_ref) for j, chip in enumerate(chips)]
        for cp in first:
            cp.start()
        passed = [copy(4 + j, (*chip, c), sibling) for j, chip in enumerate(chips)]
        for j, chip in enumerate(chips):
            copy(1 + j, (*chip, c), me).wait_recv()
            passed[j].start()
        copy(0, sibling, me).wait_recv()
        for j, chip in enumerate(chips):
            copy(4 + j, (*chip, 1 - c), me).wait_recv()
        for cp in first + passed:
            cp.wait_send()
        mine.wait()

    return pl.pallas_call(
        body, out_shape=jax.ShapeDtypeStruct((8, rows, cols), blk.dtype), in_specs=[_ANY], out_specs=_ANY,
        scratch_shapes=[pltpu.SemaphoreType.DMA((7,)), pltpu.SemaphoreType.DMA((7,)), pltpu.SemaphoreType.DMA],
        name=name)(blk)


def _half_rows(ref, lead, half, which):
    rows = pl.ds(pl.multiple_of(half * which, half), half)
    return ref.at[rows] if lead is None else ref.at[lead, rows]


def _gather_layer_call(layer, shards, name):
    n = len(shards)
    half = [s.shape[1] // 2 for s in shards]

    def body(*refs):
        srcs, outs = refs[:n], refs[n:2 * n]
        send_sems, recv_sems, local_sems = refs[2 * n:]
        x, y, c = lax.axis_index("x"), lax.axis_index("y"), lax.axis_index("c")
        sibling = (x, y, 1 - c)
        chips = [(1 - x, y), (x, 1 - y), (1 - x, 1 - y)]

        def slot(t, px, py, pc):
            return _half_rows(outs[t], 2 * px + py, half[t], pc)

        def copy(t, k, block, to, src=None):
            return pltpu.make_async_remote_copy(
                src_ref=slot(t, *block) if src is None else src, dst_ref=slot(t, *block),
                send_sem=send_sems.at[7 * t + k], recv_sem=recv_sems.at[7 * t + k], device_id=to, device_id_type=MESH_ID)

        mine = [_half_rows(srcs[t], layer, half[t], c) for t in range(n)]
        local = [pltpu.make_async_copy(mine[t], slot(t, x, y, c), local_sems.at[t]) for t in range(n)]
        for cp in local:
            cp.start()
        first = []
        for t in range(n):
            first.append(copy(t, 0, (x, y, c), sibling, src=mine[t]))
            first += [copy(t, 1 + j, (x, y, c), (*chip, c), src=mine[t]) for j, chip in enumerate(chips)]
        for cp in first:
            cp.start()
        passed = []
        for j, chip in enumerate(chips):
            for t in range(n):
                copy(t, 1 + j, (*chip, c), (x, y, c)).wait_recv()
                passed.append(copy(t, 4 + j, (*chip, c), sibling))
                passed[-1].start()
        for t in range(n):
            copy(t, 0, (x, y, 1 - c), (x, y, c)).wait_recv()
            for j, chip in enumerate(chips):
                copy(t, 4 + j, (*chip, 1 - c), (x, y, c)).wait_recv()
        for cp in first + passed:
            cp.wait_send()
        for cp in local:
            cp.wait()

    return pl.pallas_call(
        body, out_shape=[jax.ShapeDtypeStruct((N_CHIPS,) + s.shape[1:], s.dtype) for s in shards],
        in_specs=[_ANY] * n, out_specs=[_ANY] * n,
        scratch_shapes=[pltpu.SemaphoreType.DMA((7 * n,)), pltpu.SemaphoreType.DMA((7 * n,)), pltpu.SemaphoreType.DMA((n,))],
        name=name)(*shards)


_HBM = pl.BlockSpec(memory_space=pltpu.HBM)
_SEM = pl.BlockSpec(memory_space=pltpu.SEMAPHORE)
_ORDERED_EFFECT = pltpu.CompilerParams(has_side_effects=pltpu.SideEffectType.DATAFLOW_SIDE_EFFECTING)


_VMEM = pl.BlockSpec(memory_space=pltpu.VMEM)
_TOKEN = jax.ShapeDtypeStruct((8, LANES), F32)


def _in_hbm(a):
    return pltpu.with_memory_space_constraint(a, pltpu.HBM)


def _tie(small, token):
    return small + token[0:1, 0:1].reshape((1,) * small.ndim)


def _chip_scatter_start_call(chip_sums, name):
    n = len(chip_sums)

    def body(*refs):
        srcs, outs = refs[:n], refs[n:2 * n]
        send_sems, recv_sems, token = refs[2 * n:]
        x, y, c = lax.axis_index("x"), lax.axis_index("y"), lax.axis_index("c")
        chips = [(1 - x, y), (x, 1 - y), (1 - x, 1 - y)]
        for k, (cx, cy) in enumerate(chips):
            for t in range(n):
                pltpu.make_async_remote_copy(
                    src_ref=srcs[t].at[2 * cx + cy], dst_ref=outs[t].at[k], send_sem=send_sems.at[3 * t + k],
                    recv_sem=recv_sems.at[3 * t + k], device_id=(cx, cy, c), device_id_type=MESH_ID).start()
        token[...] = jnp.zeros_like(token)

    dma = pltpu.SemaphoreType.DMA
    return pl.pallas_call(
        body, out_shape=[pltpu.HBM((3,) + p.shape[1:], p.dtype) for p in chip_sums] + [dma((3 * n,)), dma((3 * n,)), _TOKEN],
        in_specs=[_HBM] * n, out_specs=[_HBM] * n + [_SEM, _SEM, _VMEM], name=name, compiler_params=_ORDERED_EFFECT,
    )(*[_in_hbm(p) for p in chip_sums])


def _chip_scatter_finish_call(chip_sums, bufs, send_sems, recv_sems, after, name):
    n = len(chip_sums)

    def body(*refs):
        srcs, ins, send_ref, recv_ref = refs[:n], refs[n:2 * n], refs[2 * n], refs[2 * n + 1]
        x, y, c = lax.axis_index("x"), lax.axis_index("y"), lax.axis_index("c")
        chips = [(1 - x, y), (x, 1 - y), (1 - x, 1 - y)]
        for k, (cx, cy) in enumerate(chips):
            for t in range(n):
                pltpu.make_async_remote_copy(
                    src_ref=srcs[t].at[2 * cx + cy], dst_ref=ins[t].at[k], send_sem=send_ref.at[3 * t + k],
                    recv_sem=recv_ref.at[3 * t + k], device_id=(cx, cy, c), device_id_type=MESH_ID).wait()

    return pl.pallas_call(
        body, out_shape=[pltpu.HBM(b.shape, b.dtype) for b in bufs],
        in_specs=[_HBM] * (2 * n) + [_SEM, _SEM, _ANY], out_specs=[_HBM] * n,
        input_output_aliases={n + t: t for t in range(n)}, name=name, compiler_params=_ORDERED_EFFECT,
    )(*[_in_hbm(p) for p in chip_sums], *bufs, send_sems, recv_sems, after)


def _place_own_call(layer, shards, chip_core, name):
    n = len(shards)

    def body(ids_ref, *refs):
        for t in range(n):
            refs[n + t][...] = refs[t][...]

    def blk(s):
        return (1, s.shape[1] // 2) + s.shape[2:]

    def imap_in(s):
        pad = (0,) * (s.ndim - 2)
        return lambda i, ids_ref: (layer, ids_ref[1]) + pad

    def imap_out(s):
        pad = (0,) * (s.ndim - 2)
        return lambda i, ids_ref: (ids_ref[0], ids_ref[1]) + pad

    grid_spec = pltpu.PrefetchScalarGridSpec(
        num_scalar_prefetch=1, grid=(1,), in_specs=[pl.BlockSpec(blk(s), imap_in(s)) for s in shards],
        out_specs=[pl.BlockSpec(blk(s), imap_out(s)) for s in shards])
    return pl.pallas_call(
        body, grid_spec=grid_spec, out_shape=[jax.ShapeDtypeStruct((N_CHIPS,) + s.shape[1:], s.dtype) for s in shards],
        name=name, compiler_params=_cparams(1))(chip_core, *shards)


def _gather_start_call(layer, shards, bufs, after, name):
    n = len(shards)
    half = [s.shape[1] // 2 for s in shards]

    def body(*refs):
        srcs, outs = refs[:n], refs[2 * n + 1:3 * n + 1]
        send_sems, recv_sib, recv_ici, token = refs[3 * n + 1:]
        x, y, c = lax.axis_index("x"), lax.axis_index("y"), lax.axis_index("c")
        chips = [(1 - x, y), (x, 1 - y), (1 - x, 1 - y)]
        for t in range(n):
            mine = _half_rows(srcs[t], layer, half[t], c)
            dst = _half_rows(outs[t], 2 * x + y, half[t], c)
            pltpu.make_async_remote_copy(src_ref=mine, dst_ref=dst, send_sem=send_sems.at[4 * t], recv_sem=recv_sib.at[t],
                                         device_id=(x, y, 1 - c), device_id_type=MESH_ID).start()
            for j, chip in enumerate(chips):
                pltpu.make_async_remote_copy(src_ref=mine, dst_ref=dst, send_sem=send_sems.at[4 * t + 1 + j],
                                             recv_sem=recv_ici.at[3 * t + j], device_id=(*chip, c), device_id_type=MESH_ID).start()
        token[...] = jnp.zeros_like(token)

    dma = pltpu.SemaphoreType.DMA
    return pl.pallas_call(
        body, out_shape=[pltpu.HBM(b.shape, b.dtype) for b in bufs] + [dma((4 * n,)), dma((n,)), dma((3 * n,)), _TOKEN],
        in_specs=[_HBM] * (2 * n) + [_ANY], out_specs=[_HBM] * n + [_SEM] * 3 + [_VMEM],
        input_output_aliases={n + t: t for t in range(n)}, name=name, compiler_params=_ORDERED_EFFECT,
    )(*[_in_hbm(s) for s in shards], *[_in_hbm(b) for b in bufs], after)


def _gather_forward_call(bufs, recv_ici, after, name):
    n = len(bufs)
    half = [b.shape[1] // 2 for b in bufs]

    def body(*refs):
        ins, recv_ici_ref = refs[:n], refs[n]
        outs = refs[n + 2:2 * n + 2]
        send_fwd, recv_fwd, token = refs[2 * n + 2:]
        x, y, c = lax.axis_index("x"), lax.axis_index("y"), lax.axis_index("c")
        chips = [(1 - x, y), (x, 1 - y), (1 - x, 1 - y)]
        for j, (cx, cy) in enumerate(chips):
            for t in range(n):
                landed = _half_rows(ins[t], 2 * cx + cy, half[t], c)
                dst = _half_rows(outs[t], 2 * cx + cy, half[t], c)
                pltpu.make_async_remote_copy(src_ref=landed, dst_ref=landed, send_sem=send_fwd.at[3 * t + j],
                                             recv_sem=recv_ici_ref.at[3 * t + j], device_id=(cx, cy, c),
                                             device_id_type=MESH_ID).wait_recv()
                pltpu.make_async_remote_copy(src_ref=landed, dst_ref=dst, send_sem=send_fwd.at[3 * t + j],
                                             recv_sem=recv_fwd.at[3 * t + j], device_id=(x, y, 1 - c),
                                             device_id_type=MESH_ID).start()
        token[...] = jnp.zeros_like(token)

    dma = pltpu.SemaphoreType.DMA
    return pl.pallas_call(
        body, out_shape=[pltpu.HBM(b.shape, b.dtype) for b in bufs] + [dma((3 * n,)), dma((3 * n,)), _TOKEN],
        in_specs=[_HBM] * n + [_SEM, _ANY], out_specs=[_HBM] * n + [_SEM] * 2 + [_VMEM],
        input_output_aliases={t: t for t in range(n)}, name=name, compiler_params=_ORDERED_EFFECT,
    )(*bufs, recv_ici, after)


def _gather_finish_call(layer, shards, bufs, send_sems, recv_sib, send_fwd, recv_fwd, after, name):
    n = len(bufs)
    half = [b.shape[1] // 2 for b in bufs]

    def body(*refs):
        srcs, ins = refs[:n], refs[n:2 * n]
        send_ref, recv_sib_ref, send_fwd_ref, recv_fwd_ref = refs[2 * n:2 * n + 4]
        x, y, c = lax.axis_index("x"), lax.axis_index("y"), lax.axis_index("c")
        chips = [(1 - x, y), (x, 1 - y), (1 - x, 1 - y)]
        sibling = (x, y, 1 - c)
        for t in range(n):
            mine = _half_rows(srcs[t], layer, half[t], c)
            for k in range(4):
                pltpu.make_async_remote_copy(src_ref=mine, dst_ref=mine, send_sem=send_ref.at[4 * t + k],
                                             recv_sem=recv_sib_ref.at[t], device_id=sibling, device_id_type=MESH_ID).wait_send()
            from_sibling = _half_rows(ins[t], 2 * x + y, half[t], 1 - c)
            pltpu.make_async_remote_copy(src_ref=from_sibling, dst_ref=from_sibling, send_sem=send_ref.at[4 * t],
                                         recv_sem=recv_sib_ref.at[t], device_id=sibling, device_id_type=MESH_ID).wait_recv()
            for j, (cx, cy) in enumerate(chips):
                sent = _half_rows(ins[t], 2 * cx + cy, half[t], c)
                passed = _half_rows(ins[t], 2 * cx + cy, half[t], 1 - c)
                pltpu.make_async_remote_copy(src_ref=sent, dst_ref=passed, send_sem=send_fwd_ref.at[3 * t + j],
                                             recv_sem=recv_fwd_ref.at[3 * t + j], device_id=sibling, device_id_type=MESH_ID).wait()

    return pl.pallas_call(
        body, out_shape=[pltpu.HBM(b.shape, b.dtype) for b in bufs],
        in_specs=[_HBM] * (2 * n) + [_SEM] * 4 + [_ANY], out_specs=[_HBM] * n,
        input_output_aliases={n + t: t for t in range(n)}, name=name, compiler_params=_ORDERED_EFFECT,
    )(*[_in_hbm(s) for s in shards], *bufs, send_sems, recv_sib, send_fwd, recv_fwd, after)


def _pair_exchange_call(grads, name):
    n = len(grads)
    half = [g.shape[1] // 2 for g in grads]

    def body(*refs):
        srcs, outs, send_sems, recv_sems = refs[:n], refs[n:2 * n], refs[2 * n], refs[2 * n + 1]
        x, y, c = lax.axis_index("x"), lax.axis_index("y"), lax.axis_index("c")
        copies = [pltpu.make_async_remote_copy(
            src_ref=srcs[t].at[:, pl.ds(pl.multiple_of(half[t] * (1 - c), half[t]), half[t])], dst_ref=outs[t],
            send_sem=send_sems.at[t], recv_sem=recv_sems.at[t], device_id=(x, y, 1 - c), device_id_type=MESH_ID) for t in range(n)]
        for cp in copies:
            cp.start()
        for cp in copies:
            cp.wait()

    return pl.pallas_call(
        body, out_shape=[jax.ShapeDtypeStruct((g.shape[0], g.shape[1] // 2, g.shape[2]), g.dtype) for g in grads],
        in_specs=[_ANY] * n, out_specs=[_ANY] * n,
        scratch_shapes=[pltpu.SemaphoreType.DMA((n,)), pltpu.SemaphoreType.DMA((n,))], name=name)(*grads)


def _chip_scatter_call(chip_sums, name):
    n = len(chip_sums)

    def body(*refs):
        srcs, outs, send_sems, recv_sems = refs[:n], refs[n:2 * n], refs[2 * n], refs[2 * n + 1]
        x, y, c = lax.axis_index("x"), lax.axis_index("y"), lax.axis_index("c")
        chips = [(1 - x, y), (x, 1 - y), (1 - x, 1 - y)]
        copies = [pltpu.make_async_remote_copy(
            src_ref=srcs[t].at[2 * cx + cy], dst_ref=outs[t].at[k], send_sem=send_sems.at[3 * t + k],
            recv_sem=recv_sems.at[3 * t + k], device_id=(cx, cy, c), device_id_type=MESH_ID)
            for k, (cx, cy) in enumerate(chips) for t in range(n)]
        for cp in copies:
            cp.start()
        for cp in copies:
            cp.wait()

    return pl.pallas_call(
        body, out_shape=[jax.ShapeDtypeStruct((3,) + p.shape[1:], p.dtype) for p in chip_sums],
        in_specs=[_ANY] * n, out_specs=[_ANY] * n,
        scratch_shapes=[pltpu.SemaphoreType.DMA((3 * n,)), pltpu.SemaphoreType.DMA((3 * n,))], name=name)(*chip_sums)


def _pair_gather_call(bufs, name):
    n = len(bufs)
    half = [b.shape[0] // 2 for b in bufs]

    def body(*refs):
        srcs, outs, send_sems, recv_sems = refs[:n], refs[n:2 * n], refs[2 * n], refs[2 * n + 1]
        x, y, c = lax.axis_index("x"), lax.axis_index("y"), lax.axis_index("c")
        for t in range(n):
            pltpu.make_async_remote_copy(
                src_ref=_half_rows(srcs[t], None, half[t], c), dst_ref=_half_rows(outs[t], None, half[t], c),
                send_sem=send_sems.at[t], recv_sem=recv_sems.at[t], device_id=(x, y, 1 - c), device_id_type=MESH_ID).start()
        for t in range(n):
            pltpu.make_async_remote_copy(
                src_ref=_half_rows(srcs[t], None, half[t], c), dst_ref=_half_rows(outs[t], None, half[t], 1 - c),
                send_sem=send_sems.at[t], recv_sem=recv_sems.at[t], device_id=(x, y, 1 - c), device_id_type=MESH_ID).wait()

    return pl.pallas_call(
        body, out_shape=[jax.ShapeDtypeStruct(b.shape, b.dtype) for b in bufs], in_specs=[_ANY] * n, out_specs=[_ANY] * n,
        input_output_aliases={t: t for t in range(n)},
        scratch_shapes=[pltpu.SemaphoreType.DMA((n,)), pltpu.SemaphoreType.DMA((n,))], name=name)(*bufs)


def _pack_rows(flats, dtype, row_multiple):
    flat = jnp.concatenate([f.reshape(-1).astype(dtype) for f in flats])
    n = flat.shape[0]
    rows = -(-n // PACK_W)
    rows = -(-rows // row_multiple) * row_multiple
    return jnp.pad(flat, (0, rows * PACK_W - n)).reshape(rows, PACK_W)


def _unpack(flat, shapes):
    out, off = [], 0
    for shp in shapes:
        n = math.prod(shp)
        out.append(flat[off:off + n].reshape(shp))
        off += n
    return out


def _f32_as_mm_bits(a):
    return lax.bitcast_convert_type(a, jnp.bfloat16).reshape(-1)


def _mm_bits_as_f32(flat, shape):
    return lax.bitcast_convert_type(flat.reshape(-1, 2), F32).reshape(shape)


def _w_in_to_aligned(w):
    z = lambda n: jnp.zeros((w.shape[0], n), w.dtype)
    return jnp.concatenate([w[:, R_ML:R_END], w[:, R_SG:R_ML], w[:, R_CV:R_SGI], w[:, R_SGI:R_MQ], w[:, R_MQ:R_SG],
                            w[:, R_CQ:R_CKV], w[:, R_CKV:R_KR], z(NOPE), w[:, R_KR:R_CV], z(LANES - QKH)], axis=1)


def _w_in_from_aligned(wa):
    return jnp.concatenate([wa[:, OFF_CQ:OFF_CKV], wa[:, OFF_CKV:OFF_KR], wa[:, OFF_KR + NOPE:OFF_KR + QKH], wa[:, OFF_CV:OFF_SGI],
                            wa[:, OFF_SGI:OFF_MQ], wa[:, OFF_MQ:OFF_CQ], wa[:, OFF_SG:OFF_CV], wa[:, OFF_ML:OFF_SG]], axis=1)


def _wuq_to_heads(w):
    w3 = w.reshape(QL, H, QKH)
    w3 = jnp.pad(w3, ((0, 0), (0, 0), (0, LANES - QKH)))
    return jnp.transpose(w3, (1, 0, 2))


def _wuq_from_heads(wh):
    return jnp.transpose(wh[:, :, :QKH], (1, 0, 2)).reshape(QL, H * QKH)


def _wukv_to_heads(w):
    w3 = w.reshape(KVL, H, NOPE + VH)
    wkn = jnp.transpose(jnp.pad(w3[:, :, :NOPE], ((0, 0), (0, 0), (0, LANES - NOPE))), (1, 0, 2))
    wv3 = w3[:, :, NOPE:]
    z = jnp.zeros((KVL, VH), w.dtype)
    cols = []
    for h in range(H):
        cols += [wv3[:, h], z] if h % 2 == 0 else [z, wv3[:, h]]
    return wkn, jnp.concatenate(cols, axis=1)


def _wukv_from_heads(wkn, wv):
    kn = jnp.transpose(wkn[:, :, :NOPE], (1, 0, 2))
    vs = jnp.stack([wv[:, LANES * h + VH * (h % 2):LANES * h + VH * (h % 2) + VH] for h in range(H)], axis=1)
    return jnp.concatenate([kn, vs], axis=2).reshape(KVL, H * (NOPE + VH))


def _layer_fwd(x, mem, tabs, p):
    proj, h = _proj_call(x, p["norm_g"], p["w_in"])
    q, k, v = _mla_prep_call(proj, tabs, p["cq_g"], p["ckv_g"], p["qg"], p["kg"], p["wuq"], p["wkn"], p["wv"])
    ya = _attn_call(q, k, v, proj)
    bm = p["bm"]
    if p.get("after_attn") is not None:
        bm = _tie(bm, p["after_attn"](ya))
    yb = _conv_call(proj, p["conv_w"], p["conv_b"])
    yc = _sg_call(proj, p["ln_g"], p["ln_b"], p["ws"], p["bs"])
    mk, mv = _memkv_call(mem, p["mem_g"], p["wm"], p["mkg"])
    yd = _mem_call(proj, mk, mv, p["mqg"])
    out = _merge_call((ya, yb, yc, yd), proj, bm, p["wb"], p["wo"], x)
    return out, dict(x=x, proj=proj, h=h, q=q, k=k, v=v, ys=(ya, yb, yc, yd), mk=mk, mv=mv)


def _layer_bwd(dout, mem, tabs, p, sv, after_mla=None, on_grads=None):
    proj = sv["proj"]
    dya, dyb, dyc, dyd, dml, dbm, dwb, dwo = _merge_bwd_call(sv["ys"], proj, p["bm"], p["wb"], p["wo"], dout)
    dq, dk, dv, dsg_a = _attn_bwd_call(sv["q"], sv["k"], sv["v"], proj, dya)
    dcq, dckv, dkr, dcqg, dckvg, dqg, dkg, dwuq, dwkn, dwv = _mla_prep_bwd_call(
        proj, tabs, p["cq_g"], p["ckv_g"], p["qg"], p["kg"], p["wuq"], p["wkn"], p["wv"], dq, dk, dv)
    if after_mla is not None:
        after_mla(dcq)
    dbg, dcg, dxi, dsg_b, dcw, dcb = _conv_bwd_call(proj, p["conv_w"], p["conv_b"], dyb)
    du, dvv, dsg_c, dlg, dlb, dws, dbs = _sg_bwd_call(proj, p["ln_g"], p["ln_b"], p["ws"], p["bs"], dyc)
    dmq, dsg_d, dmk, dmv, dmqg = _mem_bwd_call(proj, sv["mk"], sv["mv"], p["mqg"], dyd)
    dmem_g, dwm, dmkg = _memkv_bwd_call(mem, p["mem_g"], p["wm"], p["mkg"], dmk, dmv)
    dproj = jnp.concatenate([dml, dsg_a, dsg_b, dsg_c, dsg_d, dbg, dcg, dxi, du, dvv, dmq, dcq, dckv, dkr], axis=1)
    dw_in = _dw_call(sv["h"], dproj)
    grads = dict(cq_norm_g=dcqg[0], ckv_norm_g=dckvg[0], mla_q_norm_g=dqg[0, :QKH], mla_k_norm_g=dkg[0, :QKH],
                 conv_w=dcw, conv_b=dcb[0], sg_ln_g=dlg[0], sg_ln_b=dlb[0], w_spatial=dws, b_spatial=dbs[:, :, 0],
                 mem_norm_g=dmem_g[0], mem_q_norm_g=dmqg[0], mem_k_norm_g=dmkg[0], b_merge=dbm,
                 w_in_aligned=dw_in, wuq_heads=dwuq, wkn_heads=dwkn, wv_heads=dwv, w_mem_kv=dwm, w_branch_chips=dwb, w_out=dwo)
    norm_g = p["norm_g"]
    if on_grads is not None:
        norm_g = _tie(norm_g, on_grads(grads))
    dx, dnorm_g = _dh_call(dproj, p["w_in"], sv["x"], norm_g, dout)
    grads["norm_g"] = dnorm_g[0]
    return dx, grads


def _chips_to_cols(a):
    return jnp.concatenate([a[j] for j in range(N_CHIPS)], axis=1)


def _cols_to_chips(a):
    cols = a.shape[1] // N_CHIPS
    return jnp.stack([a[:, cols * j:cols * (j + 1)] for j in range(N_CHIPS)])


def _layer_params(l, rep, gathered, conv_w, b_merge):
    pad_g = lambda g: jnp.pad(g, (0, LANES - QKH)).reshape(1, LANES)
    wkn, wv = _wukv_to_heads(_chips_to_cols(gathered["w_ukv"]))
    return dict(
        norm_g=rep["norm_g"][l].reshape(1, D), w_in=_w_in_to_aligned(_chips_to_cols(gathered["w_in"])),
        cq_g=rep["cq_norm_g"][l].reshape(1, QL), ckv_g=rep["ckv_norm_g"][l].reshape(1, KVL),
        qg=pad_g(rep["mla_q_norm_g"][l]), kg=pad_g(rep["mla_k_norm_g"][l]),
        wuq=_wuq_to_heads(_chips_to_cols(gathered["w_uq"])), wkn=wkn, wv=wv,
        conv_w=conv_w, conv_b=rep["conv_b"][l].reshape(1, CW),
        ln_g=rep["sg_ln_g"][l].reshape(1, SGW), ln_b=rep["sg_ln_b"][l].reshape(1, SGW),
        ws=rep["w_spatial"][l], bs=rep["b_spatial"][l].reshape(SGG, SGC, 1),
        mem_g=rep["mem_norm_g"][l].reshape(1, D), wm=gathered["w_mem_kv"].reshape(D, 2 * MH * MHD),
        mqg=rep["mem_q_norm_g"][l].reshape(1, MHD), mkg=rep["mem_k_norm_g"][l].reshape(1, MHD),
        bm=b_merge, wb=gathered["w_branch"], wo=gathered["w_out"].reshape(D, D))


def _forward_backward(x, mem, pos, target, params, bwd_hooks=None):
    tabs = _rope_tables(pos)
    params = list(params)
    saved = []
    act = x
    for l in range(DEPTH):
        if callable(params[l]):
            params[l] = params[l](saved[-1], act)
        act, sv = _layer_fwd(act, mem, tabs, params[l])
        saved.append(sv)
    dy, sq = _loss_call(act, target)
    grads = [None] * DEPTH
    for l in reversed(range(DEPTH)):
        dy, grads[l] = _layer_bwd(dy, mem, tabs, params[l], saved[l], **(bwd_hooks[l] if bwd_hooks else {}))
    return sq, dy, grads


_SHARDED_MM = ("w_in", "w_branch", "w_out", "w_mem_kv", "w_uq", "w_ukv")
_SHARDED_F32 = ("conv_w", "b_merge")
_REPLICATED = ("norm_g", "cq_norm_g", "ckv_norm_g", "mla_q_norm_g", "mla_k_norm_g", "conv_b", "sg_ln_g", "sg_ln_b",
               "w_spatial", "b_spatial", "mem_norm_g", "mem_q_norm_g", "mem_k_norm_g")
_ALL_REDUCED = _REPLICATED + _SHARDED_F32
_WEIGHTS = ("norm_g", "w_in", "cq_norm_g", "ckv_norm_g", "w_uq", "w_ukv", "mla_q_norm_g", "mla_k_norm_g", "conv_w", "conv_b",
            "sg_ln_g", "sg_ln_b", "w_spatial", "b_spatial", "mem_norm_g", "w_mem_kv", "mem_q_norm_g", "mem_k_norm_g",
            "b_merge", "w_branch", "w_out")
_BIG = ("w_in", "w_uq", "w_ukv", "w_mem_kv", "w_branch", "w_out")
_SMALL = tuple(n for n in _WEIGHTS if n not in _BIG)


def _gather_small_sharded(w):
    names = _SHARDED_F32
    packed = _pack_rows([w[n] for n in names], F32, 8)
    got = _all_gather8(packed, "gather_small_weights")
    per_chip = [_unpack(got[2 * j].reshape(-1), [w[n].shape for n in names]) for j in range(N_CHIPS)]
    return {n: jnp.concatenate([per_chip[j][t] for j in range(N_CHIPS)], axis=2) for t, n in enumerate(names)}


def _gather_layer(l, shards):
    srcs = [shards[n] for n in _SHARDED_MM]
    return dict(zip(_SHARDED_MM, _gather_layer_call(l, srcs, "gather_weights_l%d" % l)))


class _ReduceScatter:
    def __init__(self, layer):
        self.tag = "rs_l%d_" % layer

    def start(self, grads):
        c = lax.axis_index("c")
        tensors = [
            _cols_to_chips(_w_in_from_aligned(grads["w_in_aligned"])),
            grads["w_branch_chips"].reshape(N_CHIPS, NB * BW, D // N_CHIPS),
            grads["w_out"].reshape(N_CHIPS, D // N_CHIPS, D),
            grads["w_mem_kv"].reshape(N_CHIPS, D // N_CHIPS, 2 * MH * MHD),
            _cols_to_chips(_wuq_from_heads(grads["wuq_heads"])),
            _cols_to_chips(_wukv_from_heads(grads["wkn_heads"], grads["wv_heads"])),
        ]
        n = len(tensors)
        from_sibling = _pair_exchange_call(tensors, self.tag + "pair_exchange")
        self.chip_sums = _pair_sum_call(tensors, from_sibling, c.astype(jnp.int32).reshape(1), self.tag + "pair_sum")
        out = _chip_scatter_start_call(self.chip_sums, self.tag + "scatter_start")
        self.bufs, self.send_sems, self.recv_sems, token = out[:n], out[n], out[n + 1], out[n + 2]
        return token

    def finish(self, after):
        x, y, c = lax.axis_index("x"), lax.axis_index("y"), lax.axis_index("c")
        chip_core = jnp.stack([2 * x + y, c]).astype(jnp.int32)
        from_chips = _chip_scatter_finish_call(self.chip_sums, self.bufs, self.send_sems, self.recv_sems, after,
                                               self.tag + "scatter_finish")
        mine = _owner_sum_call(self.chip_sums, from_chips, chip_core, self.tag + "owner_sum")
        shard = dict(zip(_SHARDED_MM, _pair_gather_call(mine, self.tag + "pair_gather")))
        shard["w_branch"] = shard["w_branch"].reshape(NB, BW, D // N_CHIPS)
        return shard


def _all_reduce_small(g):
    packed = _pack_rows([g[n] for n in _ALL_REDUCED], F32, 64)
    got = _all_gather8(packed, "gather_small_grads")
    total = _sum8_call(got).reshape(-1)
    out = dict(zip(_ALL_REDUCED, _unpack(total, [g[n].shape for n in _ALL_REDUCED])))
    chip = 2 * lax.axis_index("x") + lax.axis_index("y")
    for n in _SHARDED_F32:
        size = out[n].shape[2] // N_CHIPS
        out[n] = lax.dynamic_slice_in_dim(out[n], chip * size, size, axis=2)
    return out


def _adamw_all(w, g, m, v):
    delta, new_m, new_v = {}, {}, {}
    for n in _BIG:
        shp = w[n].shape
        as2d = lambda a: a.reshape(-1, shp[-1])
        d, nm, nv = _adamw_call(as2d(w[n]), as2d(g[n]), as2d(m[n]), as2d(v[n]), "adamw_" + n)
        delta[n], new_m[n], new_v[n] = d.reshape(shp), nm.reshape(shp), nv.reshape(shp)
    shapes = [w[n].shape for n in _SMALL]
    pk = lambda t: _pack_rows([t[n] for n in _SMALL], F32, 64)
    d, nm, nv = _adamw_call(pk(w), pk(g), pk(m), pk(v), "adamw_small")
    for out, packed in ((delta, d), (new_m, nm), (new_v, nv)):
        out.update(zip(_SMALL, _unpack(packed.reshape(-1), shapes)))
    return delta, new_m, new_v


def kernel(x, mem, positions, norm_g, w_in, cq_norm_g, ckv_norm_g, w_uq, w_ukv, mla_q_norm_g, mla_k_norm_g, conv_w, conv_b, sg_ln_g, sg_ln_b, w_spatial, b_spatial, mem_norm_g, w_mem_kv, mem_q_norm_g, mem_k_norm_g, b_merge, w_branch, w_out, loss_target, m_norm_g, m_w_in, m_cq_norm_g, m_ckv_norm_g, m_w_uq, m_w_ukv, m_mla_q_norm_g, m_mla_k_norm_g, m_conv_w, m_conv_b, m_sg_ln_g, m_sg_ln_b, m_w_spatial, m_b_spatial, m_mem_norm_g, m_w_mem_kv, m_mem_q_norm_g, m_mem_k_norm_g, m_b_merge, m_w_branch, m_w_out, v_norm_g, v_w_in, v_cq_norm_g, v_ckv_norm_g, v_w_uq, v_w_ukv, v_mla_q_norm_g, v_mla_k_norm_g, v_conv_w, v_conv_b, v_sg_ln_g, v_sg_ln_b, v_w_spatial, v_b_spatial, v_mem_norm_g, v_w_mem_kv, v_mem_q_norm_g, v_mem_k_norm_g, v_b_merge, v_w_branch, v_w_out):
    w = dict(norm_g=norm_g, w_in=w_in, cq_norm_g=cq_norm_g, ckv_norm_g=ckv_norm_g, w_uq=w_uq, w_ukv=w_ukv,
             mla_q_norm_g=mla_q_norm_g, mla_k_norm_g=mla_k_norm_g, conv_w=conv_w, conv_b=conv_b, sg_ln_g=sg_ln_g,
             sg_ln_b=sg_ln_b, w_spatial=w_spatial, b_spatial=b_spatial, mem_norm_g=mem_norm_g, w_mem_kv=w_mem_kv,
             mem_q_norm_g=mem_q_norm_g, mem_k_norm_g=mem_k_norm_g, b_merge=b_merge, w_branch=w_branch, w_out=w_out)
    m = dict(norm_g=m_norm_g, w_in=m_w_in, cq_norm_g=m_cq_norm_g, ckv_norm_g=m_ckv_norm_g, w_uq=m_w_uq, w_ukv=m_w_ukv,
             mla_q_norm_g=m_mla_q_norm_g, mla_k_norm_g=m_mla_k_norm_g, conv_w=m_conv_w, conv_b=m_conv_b, sg_ln_g=m_sg_ln_g,
             sg_ln_b=m_sg_ln_b, w_spatial=m_w_spatial, b_spatial=m_b_spatial, mem_norm_g=m_mem_norm_g, w_mem_kv=m_w_mem_kv,
             mem_q_norm_g=m_mem_q_norm_g, mem_k_norm_g=m_mem_k_norm_g, b_merge=m_b_merge, w_branch=m_w_branch, w_out=m_w_out)
    v = dict(norm_g=v_norm_g, w_in=v_w_in, cq_norm_g=v_cq_norm_g, ckv_norm_g=v_ckv_norm_g, w_uq=v_w_uq, w_ukv=v_w_ukv,
             mla_q_norm_g=v_mla_q_norm_g, mla_k_norm_g=v_mla_k_norm_g, conv_w=v_conv_w, conv_b=v_conv_b, sg_ln_g=v_sg_ln_g,
             sg_ln_b=v_sg_ln_b, w_spatial=v_w_spatial, b_spatial=v_b_spatial, mem_norm_g=v_mem_norm_g, w_mem_kv=v_w_mem_kv,
             mem_q_norm_g=v_mem_q_norm_g, mem_k_norm_g=v_mem_k_norm_g, b_merge=v_b_merge, w_branch=v_w_branch, w_out=v_w_out)

    small = _gather_small_sharded(w)
    shards = {n: w[n].astype(MM) for n in _SHARDED_MM}
    srcs = [shards[n] for n in _SHARDED_MM]
    n_t = len(srcs)
    chip_core = jnp.stack([2 * lax.axis_index("x") + lax.axis_index("y"), lax.axis_index("c")]).astype(jnp.int32)
    gathered0 = _gather_layer(0, shards)
    bufs = _place_own_call(1, srcs, chip_core, "gather_l1_place_own")
    started = _gather_start_call(1, srcs, bufs, gathered0["w_ukv"], "gather_l1_start")
    bufs, send_sems, recv_sib, recv_ici = started[:n_t], started[n_t], started[n_t + 1], started[n_t + 2]
    passed = []

    def pass_on(ya0):
        passed.extend(_gather_forward_call(bufs, recv_ici, ya0, "gather_l1_forward"))
        return passed[n_t + 2]

    def layer1_params(saved0, act0):
        got = _gather_finish_call(1, srcs, passed[:n_t], send_sems, recv_sib, passed[n_t], passed[n_t + 1], act0, "gather_l1_finish")
        return _layer_params(1, w, dict(zip(_SHARDED_MM, got)), small["conv_w"][1], small["b_merge"][1])

    params0 = _layer_params(0, w, gathered0, small["conv_w"][0], small["b_merge"][0])
    params0["norm_g"] = _tie(params0["norm_g"], started[n_t + 3])
    params = [dict(params0, after_attn=pass_on), layer1_params]
    rs = [_ReduceScatter(l) for l in range(DEPTH)]
    shard_grads = {}

    def land_l1(value):
        shard_grads[1] = rs[1].finish(value)

    hooks = [dict(on_grads=rs[0].start, after_mla=land_l1), dict(on_grads=rs[1].start)]
    sq, grad_x, layer_grads = _forward_backward(x[0], mem[0], positions[0], loss_target[0], params, hooks)
    shard_grads[0] = rs[0].finish(grad_x)
    loss = lax.psum(0.5 / D * jnp.sum(sq), ("x", "y", "c"))

    g = {n: jnp.stack([dict(layer_grads[l], **shard_grads[l])[n] for l in range(DEPTH)]) for n in _WEIGHTS}
    g.update(_all_reduce_small(g))
    delta, new_m, new_v = _adamw_all(w, g, m, v)
    return (loss, grad_x[None], *[g[n] for n in _WEIGHTS], *[delta[n] for n in _WEIGHTS],
            *[new_m[n] for n in _WEIGHTS], *[new_v[n] for n in _WEIGHTS])
```

```python
import functools
import math

import jax
import jax.numpy as jnp
from jax import lax
from jax.experimental import pallas as pl
from jax.experimental.pallas import tpu as pltpu

F32 = jnp.float32
MM = jnp.bfloat16

D = 1024
DEPTH = 2
EPS = 1e-6
H = 8
NOPE = 64
ROPE = 32
QKH = 96
VH = 64
QL = 256
KVL = 128
ROPE_THETA = 10000.0
CW = 512
SGW = 512
SGG = 4
SGC = 128
MH = 4
MHD = 128
NB = 4
BW = 512
NEG_INF = -1e30
LANES = 128
N_CHIPS = 4

R_CQ, R_CKV, R_KR, R_CV, R_SGI, R_MQ, R_SG, R_ML, R_END = 0, 256, 384, 416, 1952, 2976, 3488, 5536, 9632
OFF_ML, OFF_SG, OFF_CV, OFF_SGI, OFF_MQ, OFF_CQ, OFF_CKV, OFF_KR, NP = 0, 4096, 6144, 7680, 8704, 9216, 9472, 9600, 9728

ADAM_LR = 0.001
ADAM_B1 = 0.9
ADAM_B2 = 0.999
ADAM_EPS = 1e-08
ADAM_WD = 0.01
ADAM_STEP = 10

VMEM_LIMIT = 56 * 1024 * 1024
PACK_W = 512
MESH_ID = pl.DeviceIdType.MESH


def _cparams(n_axes):
    return pltpu.CompilerParams(dimension_semantics=("arbitrary",) * n_axes, vmem_limit_bytes=VMEM_LIMIT)


def _bs(shape, imap):
    return pl.BlockSpec(shape, imap)


@jax.custom_vjp
def _mm(a, b):
    return jnp.dot(a.astype(MM), b.astype(MM), preferred_element_type=F32)


def _mm_fwd(a, b):
    return _mm(a, b), (a, b)


def _mm_bwd(res, g):
    a, b = res
    gm = g.astype(MM)
    da = lax.dot_general(gm, b.astype(MM), (((1,), (1,)), ((), ())), preferred_element_type=F32)
    db = lax.dot_general(a.astype(MM), gm, (((0,), (0,)), ((), ())), preferred_element_type=F32)
    return da.astype(a.dtype), db.astype(b.dtype)


_mm.defvjp(_mm_fwd, _mm_bwd)


@jax.custom_vjp
def _mm_nt(a, b):
    return lax.dot_general(a.astype(MM), b.astype(MM), (((1,), (1,)), ((), ())), preferred_element_type=F32)


def _mm_nt_fwd(a, b):
    return _mm_nt(a, b), (a, b)


def _mm_nt_bwd(res, g):
    a, b = res
    gm = g.astype(MM)
    da = jnp.dot(gm, b.astype(MM), preferred_element_type=F32)
    db = lax.dot_general(gm, a.astype(MM), (((0,), (0,)), ((), ())), preferred_element_type=F32)
    return da.astype(a.dtype), db.astype(b.dtype)


_mm_nt.defvjp(_mm_nt_fwd, _mm_nt_bwd)


@functools.partial(jax.custom_vjp, nondiff_argnums=(1,))
def _lane_roll(x, shift):
    return pltpu.roll(x, shift, 1)


def _lane_roll_fwd(x, shift):
    return pltpu.roll(x, shift, 1), None


def _lane_roll_bwd(shift, _, g):
    return (pltpu.roll(g, (LANES - shift) % LANES, 1),)


_lane_roll.defvjp(_lane_roll_fwd, _lane_roll_bwd)


def _rms_n(x, g, n):
    ms = jnp.sum(x * x, axis=-1, keepdims=True) * (1.0 / n)
    return x * lax.rsqrt(ms + EPS) * g


def _softmax(s):
    m = jnp.max(s, axis=-1, keepdims=True)
    e = jnp.exp(s - m)
    return e / jnp.sum(e, axis=-1, keepdims=True)


def _rope(t, cos_t, sin_a, sin_b):
    return t * cos_t + _lane_roll(t, LANES - 16) * sin_a + _lane_roll(t, 16) * sin_b


def _mla_prep_fn(cq, ckv, kr, cos_t, sin_a, sin_b, cq_g, ckv_g, qg, kg, wuq, wkn, wv):
    cqn = _rms_n(cq, cq_g, QL)
    ckvn = _rms_n(ckv, ckv_g, KVL)
    lane = lax.broadcasted_iota(jnp.int32, kr.shape, 1)
    krm = jnp.where((lane >= NOPE) & (lane < QKH), kr, 0.0)
    qs, ks = [], []
    for h in range(H):
        qh = _rms_n(_mm(cqn, wuq[h]), qg, QKH)
        qs.append(_rope(qh, cos_t, sin_a, sin_b))
        kh = _rms_n(_mm(ckvn, wkn[h]) + krm, kg, QKH)
        ks.append(_rope(kh, cos_t, sin_a, sin_b))
    return jnp.concatenate(qs, axis=-1), jnp.concatenate(ks, axis=-1), _mm(ckvn, wv)


def _attn_pair_fn(q2, k2, v2, sg, row0):
    tq, s_len = q2.shape[0], k2.shape[0]
    rows = row0 + lax.broadcasted_iota(jnp.int32, (tq, s_len), 0)
    cols = lax.broadcasted_iota(jnp.int32, (tq, s_len), 1)
    mask = cols <= rows
    vlane = lax.broadcasted_iota(jnp.int32, (s_len, LANES), 1)
    o = jnp.zeros((tq, LANES), F32)
    for e in range(2):
        sl = slice(LANES * e, LANES * (e + 1))
        s = _mm_nt(q2[:, sl], k2[:, sl]) * (QKH ** -0.5)
        p = _softmax(jnp.where(mask, s, NEG_INF))
        ve = jnp.where((vlane >= VH * e) & (vlane < VH * (e + 1)), v2[:, sl], 0.0)
        o = o + _mm(p, ve)
    return o * jax.nn.silu(sg)


def _sg_fn(u, v, sgc, ln_g, ln_b, ws, bs):
    mu = jnp.mean(v, axis=-1, keepdims=True)
    xc = v - mu
    vn = xc * lax.rsqrt(jnp.mean(xc * xc, axis=-1, keepdims=True) + EPS) * ln_g + ln_b
    r = lax.broadcasted_iota(jnp.int32, (SGC, SGC), 0)
    c = lax.broadcasted_iota(jnp.int32, (SGC, SGC), 1)
    wt = [jnp.where(r >= c, w, 0.0) for w in ws]
    row_blocks = []
    for ch in range(u.shape[0] // SGC):
        col_blocks = []
        for g in range(SGG):
            blk = vn[SGC * ch:SGC * (ch + 1), LANES * g:LANES * (g + 1)]
            col_blocks.append(_mm(wt[g], blk) + bs[g])
        row_blocks.append(jnp.concatenate(col_blocks, axis=-1))
    mixed = jnp.concatenate(row_blocks, axis=0)
    return (u * mixed) * jax.nn.silu(sgc)


def _memkv_fn(mem, mem_g, wm, kg):
    kv = _mm(_rms_n(mem, mem_g, D), wm)
    ks = [_rms_n(kv[:, MHD * h:MHD * (h + 1)], kg, MHD) for h in range(MH)]
    return jnp.concatenate(ks, axis=-1), kv[:, MH * MHD:]


def _mem_fn(mq, sgd, k, v, qg):
    outs = []
    for h in range(MH):
        sl = slice(MHD * h, MHD * (h + 1))
        qh = _rms_n(mq[:, sl], qg, MHD)
        p = _softmax(_mm_nt(qh, k[:, sl]) * (MHD ** -0.5))
        outs.append(_mm(p, v[:, sl]))
    return jnp.concatenate(outs, axis=-1) * jax.nn.silu(sgd)


def _merge_fn(ys, logits, bm, wb, wo):
    merged = None
    for n in range(NB):
        z = jnp.concatenate([_mm(ys[n], wb[j][n]) for j in range(N_CHIPS)], axis=-1)
        gate = jax.nn.sigmoid(logits[:, D * n:D * (n + 1)] + bm[n])
        merged = gate * z if merged is None else merged + gate * z
    return _mm(merged, wo)


def _proj_call(x, g, w):
    s_len = x.shape[0]
    tm, tn = min(s_len, 1024), 512

    def body(x_ref, g_ref, w_ref, p_ref, h_ref):
        @pl.when(pl.program_id(1) == 0)
        def _():
            h_ref[...] = _rms_n(x_ref[...], g_ref[...], D).astype(h_ref.dtype)
        p_ref[...] = jnp.dot(h_ref[...], w_ref[...], preferred_element_type=F32)

    return pl.pallas_call(
        body, grid=(s_len // tm, NP // tn),
        in_specs=[_bs((tm, D), lambda i, j: (i, 0)), _bs((1, D), lambda i, j: (0, 0)), _bs((D, tn), lambda i, j: (0, j))],
        out_specs=[_bs((tm, tn), lambda i, j: (i, j)), _bs((tm, D), lambda i, j: (i, 0))],
        out_shape=[jax.ShapeDtypeStruct((s_len, NP), F32), jax.ShapeDtypeStruct((s_len, D), MM)],
        name="proj", compiler_params=_cparams(2))(x, g, w)


def _rope_tables(pos):
    half = ROPE // 2
    inv_freq = ROPE_THETA ** (-jnp.arange(half, dtype=F32) / half)
    ang = pos.astype(F32)[:, None] * inv_freq
    cos, sin = jnp.cos(ang), jnp.sin(ang)
    s_len = pos.shape[0]
    z = lambda n: jnp.zeros((s_len, n), F32)
    cos_t = jnp.concatenate([jnp.ones((s_len, NOPE), F32), cos, cos, z(LANES - QKH)], axis=1)
    sin_a = jnp.concatenate([z(NOPE), -sin, z(LANES - NOPE - half)], axis=1)
    sin_b = jnp.concatenate([z(NOPE + half), sin, z(LANES - QKH)], axis=1)
    return cos_t, sin_a, sin_b


def _mla_prep_specs(tm):
    row = lambda w, off: _bs((tm, w), lambda i: (i, off // w))
    full2 = lambda a, b: _bs((a, b), lambda i: (0, 0))
    full3 = lambda a, b, c: _bs((a, b, c), lambda i: (0, 0, 0))
    tab = _bs((tm, LANES), lambda i: (i, 0))
    return [row(QL, OFF_CQ), row(KVL, OFF_CKV), row(LANES, OFF_KR), tab, tab, tab,
            full2(1, QL), full2(1, KVL), full2(1, LANES), full2(1, LANES),
            full3(H, QL, LANES), full3(H, KVL, LANES), full2(KVL, H * LANES)]


def _mla_prep_args(body_refs, wdtype=None):
    (cq, ckv, kr, ct, sa, sb, cqg, ckvg, qg, kg, wuq, wkn, wv) = body_refs
    cast = (lambda a: a) if wdtype is None else (lambda a: a.astype(wdtype))
    return (cq[...], ckv[...], kr[...], ct[...], sa[...], sb[...], cqg[...], ckvg[...], qg[...], kg[...],
            [cast(wuq[h]) for h in range(H)], [cast(wkn[h]) for h in range(H)], cast(wv[...]))


def _mla_prep_call(proj, tabs, cq_g, ckv_g, qg, kg, wuq, wkn, wv):
    s_len = proj.shape[0]
    tm = min(s_len, 256)

    def body(*refs):
        q_ref, k_ref, v_ref = refs[13:]
        q, k, v = _mla_prep_fn(*_mla_prep_args(refs[:13]))
        q_ref[...] = q.astype(q_ref.dtype)
        k_ref[...] = k.astype(k_ref.dtype)
        v_ref[...] = v.astype(v_ref.dtype)

    out = _bs((tm, H * LANES), lambda i: (i, 0))
    return pl.pallas_call(
        body, grid=(s_len // tm,), in_specs=_mla_prep_specs(tm), out_specs=[out, out, out],
        out_shape=[jax.ShapeDtypeStruct((s_len, H * LANES), MM)] * 3,
        name="mla_prep", compiler_params=_cparams(1))(proj, proj, proj, *tabs, cq_g, ckv_g, qg, kg, wuq, wkn, wv)


def _mla_prep_bwd_call(proj, tabs, cq_g, ckv_g, qg, kg, wuq, wkn, wv, dq, dk, dv):
    s_len = proj.shape[0]
    tm = min(s_len, 256)

    def body(*refs):
        dq_ref, dk_ref, dv_ref = refs[13:16]
        dcq_ref, dckv_ref, dkr_ref, dcqg_ref, dckvg_ref, dqg_ref, dkg_ref, dwuq_ref, dwkn_ref, dwv_ref = refs[16:]
        _, vjp = jax.vjp(_mla_prep_fn, *_mla_prep_args(refs[:13], F32))
        (dcq, dckv, dkr, _, _, _, dcqg, dckvg, dqg, dkg, dwuq, dwkn, dwv) = vjp((dq_ref[...], dk_ref[...], dv_ref[...]))
        dcq_ref[...] = dcq.astype(dcq_ref.dtype)
        dckv_ref[...] = dckv.astype(dckv_ref.dtype)
        dkr_ref[...] = dkr.astype(dkr_ref.dtype)

        @pl.when(pl.program_id(0) == 0)
        def _():
            for r in (dcqg_ref, dckvg_ref, dqg_ref, dkg_ref, dwuq_ref, dwkn_ref, dwv_ref):
                r[...] = jnp.zeros_like(r)
        dcqg_ref[...] += dcqg
        dckvg_ref[...] += dckvg
        dqg_ref[...] += dqg
        dkg_ref[...] += dkg
        for h in range(H):
            dwuq_ref[h] += dwuq[h]
            dwkn_ref[h] += dwkn[h]
        dwv_ref[...] += dwv

    big = _bs((tm, H * LANES), lambda i: (i, 0))
    row = lambda w: _bs((tm, w), lambda i: (i, 0))
    full2 = lambda a, b: _bs((a, b), lambda i: (0, 0))
    full3 = lambda a, b, c: _bs((a, b, c), lambda i: (0, 0, 0))
    sd = jax.ShapeDtypeStruct
    return pl.pallas_call(
        body, grid=(s_len // tm,), in_specs=_mla_prep_specs(tm) + [big, big, big],
        out_specs=[row(QL), row(KVL), row(LANES), full2(1, QL), full2(1, KVL), full2(1, LANES), full2(1, LANES),
                   full3(H, QL, LANES), full3(H, KVL, LANES), full2(KVL, H * LANES)],
        out_shape=[sd((s_len, QL), MM), sd((s_len, KVL), MM), sd((s_len, LANES), MM), sd((1, QL), F32), sd((1, KVL), F32),
                   sd((1, LANES), F32), sd((1, LANES), F32), sd((H, QL, LANES), F32), sd((H, KVL, LANES), F32),
                   sd((KVL, H * LANES), F32)],
        name="mla_prep_bwd", compiler_params=_cparams(1))(proj, proj, proj, *tabs, cq_g, ckv_g, qg, kg, wuq, wkn, wv, dq, dk, dv)


def _attn_specs(s_len, tq):
    pair = 2 * LANES
    return [_bs((tq, pair), lambda p, i: (i, p)), _bs((s_len, pair), lambda p, i: (0, p)), _bs((s_len, pair), lambda p, i: (0, p)),
            _bs((tq, LANES), lambda p, i: (i, OFF_SG // LANES + p))]


def _attn_call(q, k, v, proj):
    s_len = q.shape[0]
    tq = min(s_len, 256)

    def body(q_ref, k_ref, v_ref, sg_ref, y_ref):
        for n in range(s_len // tq):
            @pl.when(pl.program_id(1) == n)
            def _():
                kl = (n + 1) * tq
                y_ref[...] = _attn_pair_fn(q_ref[...], k_ref[:kl, :], v_ref[:kl, :], sg_ref[...], n * tq).astype(y_ref.dtype)

    return pl.pallas_call(
        body, grid=(H // 2, s_len // tq), in_specs=_attn_specs(s_len, tq),
        out_specs=_bs((tq, LANES), lambda p, i: (i, p)), out_shape=jax.ShapeDtypeStruct((s_len, BW), MM),
        name="attn", compiler_params=_cparams(2))(q, k, v, proj)


def _attn_bwd_call(q, k, v, proj, dys):
    s_len = q.shape[0]
    tq = min(s_len, 256)
    pair = 2 * LANES

    def body(q_ref, k_ref, v_ref, sg_ref, dy_ref, dq_ref, dk_ref, dv_ref, dsg_ref):
        i = pl.program_id(1)

        @pl.when(i == 0)
        def _():
            dk_ref[...] = jnp.zeros_like(dk_ref)
            dv_ref[...] = jnp.zeros_like(dv_ref)

        for n in range(s_len // tq):
            @pl.when(i == n)
            def _():
                kl = (n + 1) * tq
                fn = functools.partial(_attn_pair_fn, row0=n * tq)
                _, vjp = jax.vjp(fn, q_ref[...].astype(F32), k_ref[:kl, :].astype(F32), v_ref[:kl, :].astype(F32), sg_ref[...])
                dq, dk, dv, dsg = vjp(dy_ref[...])
                dq_ref[...] = dq
                dsg_ref[...] = dsg.astype(dsg_ref.dtype)
                dk_ref[:kl, :] += dk
                dv_ref[:kl, :] += dv

    sd = jax.ShapeDtypeStruct
    return pl.pallas_call(
        body, grid=(H // 2, s_len // tq),
        in_specs=_attn_specs(s_len, tq) + [_bs((tq, LANES), lambda p, i: (i, p))],
        out_specs=[_bs((tq, pair), lambda p, i: (i, p)), _bs((s_len, pair), lambda p, i: (0, p)),
                   _bs((s_len, pair), lambda p, i: (0, p)), _bs((tq, LANES), lambda p, i: (i, p))],
        out_shape=[sd((s_len, H * LANES), F32), sd((s_len, H * LANES), F32), sd((s_len, H * LANES), F32), sd((s_len, BW), MM)],
        name="attn_bwd", compiler_params=_cparams(2))(q, k, v, proj, dys)


def _shift_down(a, n):
    r = lax.broadcasted_iota(jnp.int32, a.shape, 0)
    return jnp.where(r >= n, pltpu.roll(a, n, 0), 0.0)


def _shift_up(a, n):
    s_len = a.shape[0]
    r = lax.broadcasted_iota(jnp.int32, a.shape, 0)
    return jnp.where(r < s_len - n, pltpu.roll(a, s_len - n, 0), 0.0)


def _conv_specs(s_len):
    col = lambda off: _bs((s_len, LANES), lambda j: (0, off // LANES + j))
    return [col(OFF_CV), col(OFF_CV + CW), col(OFF_CV + 2 * CW), col(OFF_SG + BW),
            _bs((3, LANES), lambda j: (0, j)), _bs((1, LANES), lambda j: (0, j))]


def _conv_call(proj, cw, cb):
    s_len = proj.shape[0]

    def body(bg_ref, cg_ref, xi_ref, sg_ref, w_ref, b_ref, y_ref):
        z = cg_ref[...] * xi_ref[...]
        y = b_ref[...] + w_ref[0:1, :] * _shift_down(z, 2)
        y = y + w_ref[1:2, :] * _shift_down(z, 1)
        y = y + w_ref[2:3, :] * z
        y_ref[...] = ((bg_ref[...] * y) * jax.nn.silu(sg_ref[...])).astype(y_ref.dtype)

    return pl.pallas_call(
        body, grid=(CW // LANES,), in_specs=_conv_specs(s_len), out_specs=_bs((s_len, LANES), lambda j: (0, j)),
        out_shape=jax.ShapeDtypeStruct((s_len, CW), MM), name="conv", compiler_params=_cparams(1))(proj, proj, proj, proj, cw, cb)


def _conv_bwd_call(proj, cw, cb, dys):
    s_len = proj.shape[0]

    def body(bg_ref, cg_ref, xi_ref, sg_ref, w_ref, b_ref, dys_ref, dbg_ref, dcg_ref, dxi_ref, dsg_ref, dw_ref, db_ref):
        bg, cg, xi, sg = bg_ref[...], cg_ref[...], xi_ref[...], sg_ref[...]
        w0, w1, w2 = w_ref[0:1, :], w_ref[1:2, :], w_ref[2:3, :]
        z = cg * xi
        z1, z2 = _shift_down(z, 1), _shift_down(z, 2)
        y = b_ref[...] + w0 * z2
        y = y + w1 * z1
        y = y + w2 * z
        yb = bg * y
        sig = jax.nn.sigmoid(sg)
        silu = sg * sig
        dys_v = dys_ref[...]
        dsg_ref[...] = (dys_v * yb * (sig * (1.0 + sg * (1.0 - sig)))).astype(dsg_ref.dtype)
        dyb = dys_v * silu
        dbg_ref[...] = (dyb * y).astype(dbg_ref.dtype)
        dy = dyb * bg
        db_ref[...] = jnp.sum(dy, axis=0, keepdims=True)
        dw_ref[0:1, :] = jnp.sum(dy * z2, axis=0, keepdims=True)
        dw_ref[1:2, :] = jnp.sum(dy * z1, axis=0, keepdims=True)
        dw_ref[2:3, :] = jnp.sum(dy * z, axis=0, keepdims=True)
        dz = w2 * dy + w1 * _shift_up(dy, 1) + w0 * _shift_up(dy, 2)
        dcg_ref[...] = (dz * xi).astype(dcg_ref.dtype)
        dxi_ref[...] = (dz * cg).astype(dxi_ref.dtype)

    col = _bs((s_len, LANES), lambda j: (0, j))
    sd = jax.ShapeDtypeStruct
    return pl.pallas_call(
        body, grid=(CW // LANES,), in_specs=_conv_specs(s_len) + [col],
        out_specs=[col, col, col, col, _bs((3, LANES), lambda j: (0, j)), _bs((1, LANES), lambda j: (0, j))],
        out_shape=[sd((s_len, CW), MM)] * 4 + [sd((3, CW), F32), sd((1, CW), F32)],
        name="conv_bwd", compiler_params=_cparams(1))(proj, proj, proj, proj, cw, cb, dys)


def _sg_specs(tm):
    row = lambda off: _bs((tm, SGW), lambda i: (i, off // SGW))
    return [row(OFF_SGI), row(OFF_SGI + SGW), row(OFF_SG + 2 * BW), _bs((1, SGW), lambda i: (0, 0)), _bs((1, SGW), lambda i: (0, 0)),
            _bs((SGG, SGC, SGC), lambda i: (0, 0, 0)), _bs((SGG, SGC, 1), lambda i: (0, 0, 0))]


def _sg_args(refs):
    u, v, sg, lg, lb, ws, bs = refs
    return (u[...], v[...], sg[...], lg[...], lb[...], [ws[g] for g in range(SGG)], [bs[g] for g in range(SGG)])


def _sg_call(proj, ln_g, ln_b, ws, bs):
    s_len = proj.shape[0]
    tm = min(s_len, 256)

    def body(*refs):
        refs[7][...] = _sg_fn(*_sg_args(refs[:7])).astype(refs[7].dtype)

    return pl.pallas_call(
        body, grid=(s_len // tm,), in_specs=_sg_specs(tm), out_specs=_bs((tm, SGW), lambda i: (i, 0)),
        out_shape=jax.ShapeDtypeStruct((s_len, SGW), MM), name="sgmlp", compiler_params=_cparams(1))(proj, proj, proj, ln_g, ln_b, ws, bs)


def _sg_bwd_call(proj, ln_g, ln_b, ws, bs, dys):
    s_len = proj.shape[0]
    tm = min(s_len, 256)

    def body(*refs):
        dys_ref = refs[7]
        du_ref, dv_ref, dsg_ref, dlg_ref, dlb_ref, dws_ref, dbs_ref = refs[8:]
        _, vjp = jax.vjp(_sg_fn, *_sg_args(refs[:7]))
        du, dv, dsg, dlg, dlb, dws, dbs = vjp(dys_ref[...])
        du_ref[...] = du.astype(du_ref.dtype)
        dv_ref[...] = dv.astype(dv_ref.dtype)
        dsg_ref[...] = dsg.astype(dsg_ref.dtype)

        @pl.when(pl.program_id(0) == 0)
        def _():
            for r in (dlg_ref, dlb_ref, dws_ref, dbs_ref):
                r[...] = jnp.zeros_like(r)
        dlg_ref[...] += dlg
        dlb_ref[...] += dlb
        for g in range(SGG):
            dws_ref[g] += dws[g]
            dbs_ref[g] += dbs[g]

    row = _bs((tm, SGW), lambda i: (i, 0))
    sd = jax.ShapeDtypeStruct
    return pl.pallas_call(
        body, grid=(s_len // tm,), in_specs=_sg_specs(tm) + [row],
        out_specs=[row, row, row, _bs((1, SGW), lambda i: (0, 0)), _bs((1, SGW), lambda i: (0, 0)),
                   _bs((SGG, SGC, SGC), lambda i: (0, 0, 0)), _bs((SGG, SGC, 1), lambda i: (0, 0, 0))],
        out_shape=[sd((s_len, SGW), MM)] * 3 + [sd((1, SGW), F32), sd((1, SGW), F32), sd((SGG, SGC, SGC), F32), sd((SGG, SGC, 1), F32)],
        name="sgmlp_bwd", compiler_params=_cparams(1))(proj, proj, proj, ln_g, ln_b, ws, bs, dys)


def _memkv_call(mem, mem_g, wm, kg):
    m_len = mem.shape[0]

    def body(mem_ref, g_ref, w_ref, kg_ref, k_ref, v_ref):
        k, v = _memkv_fn(mem_ref[...], g_ref[...], w_ref[...], kg_ref[...])
        k_ref[...] = k.astype(k_ref.dtype)
        v_ref[...] = v.astype(v_ref.dtype)

    return pl.pallas_call(body, out_shape=[jax.ShapeDtypeStruct((m_len, MH * MHD), MM)] * 2, name="memkv",
                          compiler_params=pltpu.CompilerParams(vmem_limit_bytes=VMEM_LIMIT))(mem, mem_g, wm, kg)


def _memkv_bwd_call(mem, mem_g, wm, kg, dk, dv):
    def body(mem_ref, g_ref, w_ref, kg_ref, dk_ref, dv_ref, dg_ref, dw_ref, dkg_ref):
        _, vjp = jax.vjp(_memkv_fn, mem_ref[...], g_ref[...], w_ref[...].astype(F32), kg_ref[...])
        _, dg, dw, dkg = vjp((dk_ref[...], dv_ref[...]))
        dg_ref[...] = dg
        dw_ref[...] = dw
        dkg_ref[...] = dkg

    sd = jax.ShapeDtypeStruct
    return pl.pallas_call(body, out_shape=[sd((1, D), F32), sd((D, 2 * MH * MHD), F32), sd((1, MHD), F32)], name="memkv_bwd",
                          compiler_params=pltpu.CompilerParams(vmem_limit_bytes=VMEM_LIMIT))(mem, mem_g, wm, kg, dk, dv)


def _mem_specs(tm, m_len):
    w = MH * MHD
    return [_bs((tm, w), lambda i: (i, OFF_MQ // w)), _bs((tm, BW), lambda i: (i, (OFF_SG + 3 * BW) // BW)),
            _bs((m_len, w), lambda i: (0, 0)), _bs((m_len, w), lambda i: (0, 0)), _bs((1, MHD), lambda i: (0, 0))]


def _mem_call(proj, k, v, qg):
    s_len, m_len = proj.shape[0], k.shape[0]
    tm = min(s_len, 256)

    def body(mq_ref, sg_ref, k_ref, v_ref, qg_ref, y_ref):
        y_ref[...] = _mem_fn(mq_ref[...], sg_ref[...], k_ref[...], v_ref[...], qg_ref[...]).astype(y_ref.dtype)

    return pl.pallas_call(
        body, grid=(s_len // tm,), in_specs=_mem_specs(tm, m_len), out_specs=_bs((tm, BW), lambda i: (i, 0)),
        out_shape=jax.ShapeDtypeStruct((s_len, BW), MM), name="memattn", compiler_params=_cparams(1))(proj, proj, k, v, qg)


def _mem_bwd_call(proj, k, v, qg, dys):
    s_len, m_len = proj.shape[0], k.shape[0]
    tm = min(s_len, 256)
    w = MH * MHD

    def body(mq_ref, sg_ref, k_ref, v_ref, qg_ref, dys_ref, dmq_ref, dsg_ref, dk_ref, dv_ref, dqg_ref):
        _, vjp = jax.vjp(_mem_fn, mq_ref[...], sg_ref[...], k_ref[...].astype(F32), v_ref[...].astype(F32), qg_ref[...])
        dmq, dsg, dk, dv, dqg = vjp(dys_ref[...])
        dmq_ref[...] = dmq.astype(dmq_ref.dtype)
        dsg_ref[...] = dsg.astype(dsg_ref.dtype)

        @pl.when(pl.program_id(0) == 0)
        def _():
            for r in (dk_ref, dv_ref, dqg_ref):
                r[...] = jnp.zeros_like(r)
        dk_ref[...] += dk
        dv_ref[...] += dv
        dqg_ref[...] += dqg

    row = _bs((tm, BW), lambda i: (i, 0))
    kv = _bs((m_len, w), lambda i: (0, 0))
    sd = jax.ShapeDtypeStruct
    return pl.pallas_call(
        body, grid=(s_len // tm,), in_specs=_mem_specs(tm, m_len) + [row],
        out_specs=[row, row, kv, kv, _bs((1, MHD), lambda i: (0, 0))],
        out_shape=[sd((s_len, w), MM), sd((s_len, BW), MM), sd((m_len, w), F32), sd((m_len, w), F32), sd((1, MHD), F32)],
        name="memattn_bwd", compiler_params=_cparams(1))(proj, proj, k, v, qg, dys)


def _merge_specs(tm):
    row = _bs((tm, BW), lambda i: (i, 0))
    return [row, row, row, row, _bs((tm, NB * D), lambda i: (i, OFF_ML // (NB * D))), _bs((NB, D), lambda i: (0, 0)),
            _bs((N_CHIPS, NB, BW, D // N_CHIPS), lambda i: (0, 0, 0, 0)), _bs((D, D), lambda i: (0, 0))]


def _merge_call(ys, proj, bm, wb, wo, x):
    s_len = proj.shape[0]
    tm = min(s_len, 256)

    def body(ya, yb, yc, yd, lg_ref, bm_ref, wb_ref, wo_ref, x_ref, o_ref):
        out = _merge_fn([r[...] for r in (ya, yb, yc, yd)], lg_ref[...], [bm_ref[n:n + 1, :] for n in range(NB)],
                        [[wb_ref[j, n] for n in range(NB)] for j in range(N_CHIPS)], wo_ref[...])
        o_ref[...] = x_ref[...] + out

    xrow = _bs((tm, D), lambda i: (i, 0))
    return pl.pallas_call(
        body, grid=(s_len // tm,), in_specs=_merge_specs(tm) + [xrow], out_specs=xrow,
        out_shape=jax.ShapeDtypeStruct((s_len, D), F32), name="merge", compiler_params=_cparams(1))(*ys, proj, bm, wb, wo, x)


def _merge_bwd_call(ys, proj, bm, wb, wo, dout):
    s_len = proj.shape[0]
    tm = min(s_len, 256)

    def body(ya, yb, yc, yd, lg_ref, bm_ref, wb_ref, wo_ref, do_ref, dya, dyb, dyc, dyd, dlg_ref, dbm_ref, dwb_ref, dwo_ref):
        fn = lambda ys_, lg_, bm_, wb_, wo_: _merge_fn(ys_, lg_, bm_, wb_, wo_)
        _, vjp = jax.vjp(fn, [r[...].astype(F32) for r in (ya, yb, yc, yd)], lg_ref[...], [bm_ref[n:n + 1, :] for n in range(NB)],
                         [[wb_ref[j, n].astype(F32) for n in range(NB)] for j in range(N_CHIPS)], wo_ref[...].astype(F32))
        dys, dlg, dbm, dwb, dwo = vjp(do_ref[...])
        for r, d in zip((dya, dyb, dyc, dyd), dys):
            r[...] = d
        dlg_ref[...] = dlg.astype(dlg_ref.dtype)

        @pl.when(pl.program_id(0) == 0)
        def _():
            for r in (dbm_ref, dwb_ref, dwo_ref):
                r[...] = jnp.zeros_like(r)
        for n in range(NB):
            dbm_ref[n:n + 1, :] += dbm[n]
            for j in range(N_CHIPS):
                dwb_ref[j, n] += dwb[j][n]
        dwo_ref[...] += dwo

    row = _bs((tm, BW), lambda i: (i, 0))
    sd = jax.ShapeDtypeStruct
    wb_shape = (N_CHIPS, NB, BW, D // N_CHIPS)
    return pl.pallas_call(
        body, grid=(s_len // tm,), in_specs=_merge_specs(tm) + [_bs((tm, D), lambda i: (i, 0))],
        out_specs=[row, row, row, row, _bs((tm, NB * D), lambda i: (i, 0)), _bs((NB, D), lambda i: (0, 0)),
                   _bs(wb_shape, lambda i: (0, 0, 0, 0)), _bs((D, D), lambda i: (0, 0))],
        out_shape=[sd((s_len, BW), F32)] * 4 + [sd((s_len, NB * D), MM), sd((NB, D), F32), sd(wb_shape, F32), sd((D, D), F32)],
        name="merge_bwd", compiler_params=_cparams(1))(*ys, proj, bm, wb, wo, dout)


def _dh_call(dproj, w, x, g, dout):
    s_len = x.shape[0]
    tm, tk = min(s_len, 512), NP // 4

    def body(dp_ref, w_ref, x_ref, g_ref, do_ref, dx_ref, dg_ref, acc_ref):
        i, k = pl.program_id(0), pl.program_id(1)

        @pl.when(k == 0)
        def _():
            acc_ref[...] = jnp.zeros_like(acc_ref)
        acc_ref[...] += lax.dot_general(dp_ref[...], w_ref[...], (((1,), (1,)), ((), ())), preferred_element_type=F32)

        @pl.when(k == pl.num_programs(1) - 1)
        def _():
            _, vjp = jax.vjp(lambda x_, g_: _rms_n(x_, g_, D), x_ref[...], g_ref[...])
            dxr, dgr = vjp(acc_ref[...])
            dx_ref[...] = do_ref[...] + dxr

            @pl.when(i == 0)
            def _():
                dg_ref[...] = jnp.zeros_like(dg_ref)
            dg_ref[...] += dgr

    row = _bs((tm, D), lambda i, k: (i, 0))
    return pl.pallas_call(
        body, grid=(s_len // tm, NP // tk),
        in_specs=[_bs((tm, tk), lambda i, k: (i, k)), _bs((D, tk), lambda i, k: (0, k)), row, _bs((1, D), lambda i, k: (0, 0)), row],
        out_specs=[row, _bs((1, D), lambda i, k: (0, 0))],
        out_shape=[jax.ShapeDtypeStruct((s_len, D), F32), jax.ShapeDtypeStruct((1, D), F32)],
        scratch_shapes=[pltpu.VMEM((tm, D), F32)], name="dh", compiler_params=_cparams(2))(dproj, w, x, g, dout)


def _dw_call(h, dproj):
    s_len = h.shape[0]
    tn = 512

    def body(h_ref, dp_ref, o_ref):
        o_ref[...] = lax.dot_general(h_ref[...], dp_ref[...], (((0,), (0,)), ((), ())), preferred_element_type=F32)

    return pl.pallas_call(
        body, grid=(NP // tn,), in_specs=[_bs((s_len, D), lambda j: (0, 0)), _bs((s_len, tn), lambda j: (0, j))],
        out_specs=_bs((D, tn), lambda j: (0, j)), out_shape=jax.ShapeDtypeStruct((D, NP), F32),
        name="dw_in", compiler_params=_cparams(1))(h, dproj)


def _loss_call(y, target):
    s_len = y.shape[0]
    tm = min(s_len, 512)

    def body(y_ref, t_ref, dy_ref, l_ref):
        e = y_ref[...] - t_ref[...]
        dy_ref[...] = e * (1.0 / D)

        @pl.when(pl.program_id(0) == 0)
        def _():
            l_ref[...] = jnp.zeros_like(l_ref)
        l_ref[...] += jnp.sum(e * e, axis=0, keepdims=True)

    row = _bs((tm, D), lambda i: (i, 0))
    return pl.pallas_call(
        body, grid=(s_len // tm,), in_specs=[row, row], out_specs=[row, _bs((1, D), lambda i: (0, 0))],
        out_shape=[jax.ShapeDtypeStruct((s_len, D), F32), jax.ShapeDtypeStruct((1, D), F32)],
        name="loss", compiler_params=_cparams(1))(y, target)


def _adamw_call(w, g, m, v, name):
    rows, cols = w.shape
    tr = min(_row_tile(rows), 128)

    def body(w_ref, g_ref, m_ref, v_ref, d_ref, nm_ref, nv_ref):
        gv = g_ref[...]
        m2 = ADAM_B1 * m_ref[...] + (1.0 - ADAM_B1) * gv
        v2 = ADAM_B2 * v_ref[...] + (1.0 - ADAM_B2) * (gv * gv)
        m_hat = m2 / (1.0 - ADAM_B1 ** ADAM_STEP)
        v_hat = v2 / (1.0 - ADAM_B2 ** ADAM_STEP)
        d_ref[...] = -ADAM_LR * (m_hat / (jnp.sqrt(v_hat) + ADAM_EPS) + ADAM_WD * w_ref[...])
        nm_ref[...] = m2
        nv_ref[...] = v2

    blk = _bs((tr, cols), lambda i: (i, 0))
    return pl.pallas_call(
        body, grid=(rows // tr,), in_specs=[blk] * 4, out_specs=[blk] * 3,
        out_shape=[jax.ShapeDtypeStruct((rows, cols), F32)] * 3, name=name, compiler_params=_cparams(1))(w, g, m, v)


def _adamw_layer_call(layer, ws, gs, ms, vs, prev, after, name):
    n, steps = len(ws), 8
    after = list(after)
    n_prev = 4 * n if prev is not None else 0

    def body(*refs):
        outs = refs[len(refs) - 4 * n:]
        for t in range(n):
            w_ref, g_ref, m_ref, v_ref = refs[t], refs[n + t], refs[2 * n + t], refs[3 * n + t]
            g_out, d_out, m_out, v_out = outs[4 * t:4 * t + 4]
            gv = g_ref[...]
            m2 = ADAM_B1 * m_ref[0] + (1.0 - ADAM_B1) * gv
            v2 = ADAM_B2 * v_ref[0] + (1.0 - ADAM_B2) * (gv * gv)
            m_hat = m2 / (1.0 - ADAM_B1 ** ADAM_STEP)
            v_hat = v2 / (1.0 - ADAM_B2 ** ADAM_STEP)
            g_out[0] = gv
            d_out[0] = -ADAM_LR * (m_hat / (jnp.sqrt(v_hat) + ADAM_EPS) + ADAM_WD * w_ref[0])
            m_out[0] = m2
            v_out[0] = v2

    def lay(a):
        return _bs((1, a.shape[1] // steps, a.shape[2]), lambda i: (layer, i, 0))

    in_specs = ([lay(a) for a in ws] + [_bs((g.shape[0] // steps, g.shape[1]), lambda i: (i, 0)) for g in gs]
                + [lay(a) for a in ms] + [lay(a) for a in vs] + [_ANY] * (n_prev + len(after)))
    return pl.pallas_call(
        body, grid=(steps,), in_specs=in_specs, out_specs=[lay(ws[t]) for t in range(n) for _ in range(4)],
        out_shape=[jax.ShapeDtypeStruct(ws[t].shape, F32) for t in range(n) for _ in range(4)],
        input_output_aliases={4 * n + q: q for q in range(n_prev)}, name=name, compiler_params=_cparams(1),
    )(*ws, *gs, *ms, *vs, *(prev if prev is not None else []), *after)


def _row_tile(rows):
    for cand in (512, 256, 128, 64, 32, 16, 8):
        if rows % cand == 0 and rows > cand:
            return cand
    return rows


def _pair_sum_call(grads, from_sibling, core, name):
    n = len(grads)

    def body(core_ref, *refs):
        for t in range(n):
            refs[2 * n + t][...] = (refs[t][...] + refs[n + t][...]).astype(MM)

    half = lambda g: (1, g.shape[1] // 2, g.shape[2])
    grid_spec = pltpu.PrefetchScalarGridSpec(
        num_scalar_prefetch=1, grid=(N_CHIPS,),
        in_specs=[pl.BlockSpec(half(g), lambda j, core_ref: (j, core_ref[0], 0)) for g in grads]
        + [pl.BlockSpec(half(g), lambda j, core_ref: (j, 0, 0)) for g in grads],
        out_specs=[pl.BlockSpec(half(g), lambda j, core_ref: (j, 0, 0)) for g in grads])
    return pl.pallas_call(
        body, grid_spec=grid_spec, out_shape=[jax.ShapeDtypeStruct((N_CHIPS,) + half(g)[1:], MM) for g in grads], name=name,
        compiler_params=_cparams(1))(core, *grads, *from_sibling)


def _owner_sum_call(chip_sums, from_chips, chip_core, name):
    n = len(chip_sums)
    steps = 4

    def body(ids_ref, *refs):
        for t in range(n):
            a, b = refs[t], refs[n + t]
            refs[2 * n + t][...] = ((a[0].astype(F32) + b[0].astype(F32)) + b[1].astype(F32)) + b[2].astype(F32)

    tile = lambda p: (p.shape[1] // steps, p.shape[2])
    grid_spec = pltpu.PrefetchScalarGridSpec(
        num_scalar_prefetch=1, grid=(steps,),
        in_specs=[pl.BlockSpec((1,) + tile(p), lambda i, ids_ref: (ids_ref[0], i, 0)) for p in chip_sums]
        + [pl.BlockSpec((3,) + tile(p), lambda i, ids_ref: (0, i, 0)) for p in chip_sums],
        out_specs=[pl.BlockSpec(tile(p), lambda i, ids_ref: (ids_ref[1] * steps + i, 0)) for p in chip_sums])
    return pl.pallas_call(
        body, grid_spec=grid_spec, out_shape=[jax.ShapeDtypeStruct((2 * p.shape[1], p.shape[2]), F32) for p in chip_sums],
        name=name, compiler_params=_cparams(1))(chip_core, *chip_sums, *from_chips)


def _sum8_call(parts):
    n, rows, cols = parts.shape
    tr = _row_tile(rows)

    def body(p_ref, o_ref):
        acc = p_ref[0]
        for k in range(1, n):
            acc = acc + p_ref[k]
        o_ref[...] = acc

    return pl.pallas_call(
        body, grid=(rows // tr,), in_specs=[_bs((n, tr, cols), lambda i: (0, i, 0))], out_specs=_bs((tr, cols), lambda i: (i, 0)),
        out_shape=jax.ShapeDtypeStruct((rows, cols), F32), name="sum_small_grads", compiler_params=_cparams(1))(parts)


_ANY = pl.BlockSpec(memory_space=pl.ANY)


def _all_gather8(blk, name):
    rows, cols = blk.shape

    def body(x_ref, out_ref, send_sems, recv_sems, local_sem):
        x, y, c = lax.axis_index("x"), lax.axis_index("y"), lax.axis_index("c")
        me, sibling = (x, y, c), (x, y, 1 - c)
        chips = [(1 - x, y), (x, 1 - y), (1 - x, 1 - y)]

        def slot(px, py, pc):
            return out_ref.at[4 * px + 2 * py + pc]

        def copy(k, block, to, src=None):
            return pltpu.make_async_remote_copy(
                src_ref=slot(*block) if src is None else src, dst_ref=slot(*block),
                send_sem=send_sems.at[k], recv_sem=recv_sems.at[k], device_id=to, device_id_type=MESH_ID)

        mine = pltpu.make_async_copy(x_ref, slot(*me), local_sem)
        mine.start()
        first = [copy(0, me, sibling, src=x_ref)]
        first += [copy(1 + j, me, (*chip, c), src=x_ref) for j, chip in enumerate(chips)]
        for cp in first:
            cp.start()
        passed = [copy(4 + j, (*chip, c), sibling) for j, chip in enumerate(chips)]
        for j, chip in enumerate(chips):
            copy(1 + j, (*chip, c), me).wait_recv()
            passed[j].start()
        copy(0, sibling, me).wait_recv()
        for j, chip in enumerate(chips):
            copy(4 + j, (*chip, 1 - c), me).wait_recv()
        for cp in first + passed:
            cp.wait_send()
        mine.wait()

    return pl.pallas_call(
        body, out_shape=jax.ShapeDtypeStruct((8, rows, cols), blk.dtype), in_specs=[_ANY], out_specs=_ANY,
        scratch_shapes=[pltpu.SemaphoreType.DMA((7,)), pltpu.SemaphoreType.DMA((7,)), pltpu.SemaphoreType.DMA],
        name=name)(blk)


def _half_rows(ref, lead, half, which):
    rows = pl.ds(pl.multiple_of(half * which, half), half)
    return ref.at[rows] if lead is None else ref.at[lead, rows]


def _gather_layer_call(layer, shards, name):
    n = len(shards)
    half = [s.shape[1] // 2 for s in shards]

    def body(*refs):
        srcs, outs = refs[:n], refs[n:2 * n]
        send_sems, recv_sems, local_sems = refs[2 * n:]
        x, y, c = lax.axis_index("x"), lax.axis_index("y"), lax.axis_index("c")
        sibling = (x, y, 1 - c)
        chips = [(1 - x, y), (x, 1 - y), (1 - x, 1 - y)]

        def slot(t, px, py, pc):
            return _half_rows(outs[t], 2 * px + py, half[t], pc)

        def copy(t, k, block, to, src=None):
            return pltpu.make_async_remote_copy(
                src_ref=slot(t, *block) if src is None else src, dst_ref=slot(t, *block),
                send_sem=send_sems.at[7 * t + k], recv_sem=recv_sems.at[7 * t + k], device_id=to, device_id_type=MESH_ID)

        mine = [_half_rows(srcs[t], layer, half[t], c) for t in range(n)]
        local = [pltpu.make_async_copy(mine[t], slot(t, x, y, c), local_sems.at[t]) for t in range(n)]
        for cp in local:
            cp.start()
        first = []
        for t in range(n):
            first.append(copy(t, 0, (x, y, c), sibling, src=mine[t]))
            first += [copy(t, 1 + j, (x, y, c), (*chip, c), src=mine[t]) for j, chip in enumerate(chips)]
        for cp in first:
            cp.start()
        passed = []
        for j, chip in enumerate(chips):
            for t in range(n):
                copy(t, 1 + j, (*chip, c), (x, y, c)).wait_recv()
                passed.append(copy(t, 4 + j, (*chip, c), sibling))
                passed[-1].start()
        for t in range(n):
            copy(t, 0, (x, y, 1 - c), (x, y, c)).wait_recv()
            for j, chip in enumerate(chips):
                copy(t, 4 + j, (*chip, 1 - c), (x, y, c)).wait_recv()
        for cp in first + passed:
            cp.wait_send()
        for cp in local:
            cp.wait()

    return pl.pallas_call(
        body, out_shape=[jax.ShapeDtypeStruct((N_CHIPS,) + s.shape[1:], s.dtype) for s in shards],
        in_specs=[_ANY] * n, out_specs=[_ANY] * n,
        scratch_shapes=[pltpu.SemaphoreType.DMA((7 * n,)), pltpu.SemaphoreType.DMA((7 * n,)), pltpu.SemaphoreType.DMA((n,))],
        name=name)(*shards)


_HBM = pl.BlockSpec(memory_space=pltpu.HBM)
_SEM = pl.BlockSpec(memory_space=pltpu.SEMAPHORE)
_ORDERED_EFFECT = pltpu.CompilerParams(has_side_effects=pltpu.SideEffectType.DATAFLOW_SIDE_EFFECTING)


_VMEM = pl.BlockSpec(memory_space=pltpu.VMEM)
_TOKEN = jax.ShapeDtypeStruct((8, LANES), F32)


def _in_hbm(a):
    return pltpu.with_memory_space_constraint(a, pltpu.HBM)


def _tie(small, token):
    return small + token[0:1, 0:1].reshape((1,) * small.ndim)


def _chip_scatter_start_call(chip_sums, name):
    n = len(chip_sums)

    def body(*refs):
        srcs, outs = refs[:n], refs[n:2 * n]
        send_sems, recv_sems, token = refs[2 * n:]
        x, y, c = lax.axis_index("x"), lax.axis_index("y"), lax.axis_index("c")
        chips = [(1 - x, y), (x, 1 - y), (1 - x, 1 - y)]
        for k, (cx, cy) in enumerate(chips):
            for t in range(n):
                pltpu.make_async_remote_copy(
                    src_ref=srcs[t].at[2 * cx + cy], dst_ref=outs[t].at[k], send_sem=send_sems.at[3 * t + k],
                    recv_sem=recv_sems.at[3 * t + k], device_id=(cx, cy, c), device_id_type=MESH_ID).start()
        token[...] = jnp.zeros_like(token)

    dma = pltpu.SemaphoreType.DMA
    return pl.pallas_call(
        body, out_shape=[pltpu.HBM((3,) + p.shape[1:], p.dtype) for p in chip_sums] + [dma((3 * n,)), dma((3 * n,)), _TOKEN],
        in_specs=[_HBM] * n, out_specs=[_HBM] * n + [_SEM, _SEM, _VMEM], name=name, compiler_params=_ORDERED_EFFECT,
    )(*[_in_hbm(p) for p in chip_sums])


def _chip_scatter_finish_call(chip_sums, bufs, send_sems, recv_sems, after, name):
    n = len(chip_sums)
    after = list(after)

    def body(*refs):
        srcs, ins, send_ref, recv_ref = refs[:n], refs[n:2 * n], refs[2 * n], refs[2 * n + 1]
        x, y, c = lax.axis_index("x"), lax.axis_index("y"), lax.axis_index("c")
        chips = [(1 - x, y), (x, 1 - y), (1 - x, 1 - y)]
        for k, (cx, cy) in enumerate(chips):
            for t in range(n):
                pltpu.make_async_remote_copy(
                    src_ref=srcs[t].at[2 * cx + cy], dst_ref=ins[t].at[k], send_sem=send_ref.at[3 * t + k],
                    recv_sem=recv_ref.at[3 * t + k], device_id=(cx, cy, c), device_id_type=MESH_ID).wait()

    return pl.pallas_call(
        body, out_shape=[pltpu.HBM(b.shape, b.dtype) for b in bufs],
        in_specs=[_HBM] * (2 * n) + [_SEM, _SEM] + [_ANY] * len(after), out_specs=[_HBM] * n,
        input_output_aliases={n + t: t for t in range(n)}, name=name, compiler_params=_ORDERED_EFFECT,
    )(*[_in_hbm(p) for p in chip_sums], *bufs, send_sems, recv_sems, *after)


def _place_own_call(layer, shards, chip_core, name):
    n = len(shards)

    def body(ids_ref, *refs):
        for t in range(n):
            refs[n + t][...] = refs[t][...]

    def blk(s):
        return (1, s.shape[1] // 2) + s.shape[2:]

    def imap_in(s):
        pad = (0,) * (s.ndim - 2)
        return lambda i, ids_ref: (layer, ids_ref[1]) + pad

    def imap_out(s):
        pad = (0,) * (s.ndim - 2)
        return lambda i, ids_ref: (ids_ref[0], ids_ref[1]) + pad

    grid_spec = pltpu.PrefetchScalarGridSpec(
        num_scalar_prefetch=1, grid=(1,), in_specs=[pl.BlockSpec(blk(s), imap_in(s)) for s in shards],
        out_specs=[pl.BlockSpec(blk(s), imap_out(s)) for s in shards])
    return pl.pallas_call(
        body, grid_spec=grid_spec, out_shape=[jax.ShapeDtypeStruct((N_CHIPS,) + s.shape[1:], s.dtype) for s in shards],
        name=name, compiler_params=_cparams(1))(chip_core, *shards)


def _gather_start_call(layer, shards, bufs, after, name):
    n = len(shards)
    half = [s.shape[1] // 2 for s in shards]

    def body(*refs):
        srcs, outs = refs[:n], refs[2 * n + 1:3 * n + 1]
        send_sems, recv_sib, recv_ici, token = refs[3 * n + 1:]
        x, y, c = lax.axis_index("x"), lax.axis_index("y"), lax.axis_index("c")
        chips = [(1 - x, y), (x, 1 - y), (1 - x, 1 - y)]
        for t in range(n):
            mine = _half_rows(srcs[t], layer, half[t], c)
            dst = _half_rows(outs[t], 2 * x + y, half[t], c)
            pltpu.make_async_remote_copy(src_ref=mine, dst_ref=dst, send_sem=send_sems.at[4 * t], recv_sem=recv_sib.at[t],
                                         device_id=(x, y, 1 - c), device_id_type=MESH_ID).start()
            for j, chip in enumerate(chips):
                pltpu.make_async_remote_copy(src_ref=mine, dst_ref=dst, send_sem=send_sems.at[4 * t + 1 + j],
                                             recv_sem=recv_ici.at[3 * t + j], device_id=(*chip, c), device_id_type=MESH_ID).start()
        token[...] = jnp.zeros_like(token)

    dma = pltpu.SemaphoreType.DMA
    return pl.pallas_call(
        body, out_shape=[pltpu.HBM(b.shape, b.dtype) for b in bufs] + [dma((4 * n,)), dma((n,)), dma((3 * n,)), _TOKEN],
        in_specs=[_HBM] * (2 * n) + [_ANY], out_specs=[_HBM] * n + [_SEM] * 3 + [_VMEM],
        input_output_aliases={n + t: t for t in range(n)}, name=name, compiler_params=_ORDERED_EFFECT,
    )(*[_in_hbm(s) for s in shards], *[_in_hbm(b) for b in bufs], after)


def _gather_forward_call(bufs, recv_ici, after, name):
    n = len(bufs)
    half = [b.shape[1] // 2 for b in bufs]

    def body(*refs):
        ins, recv_ici_ref = refs[:n], refs[n]
        outs = refs[n + 2:2 * n + 2]
        send_fwd, recv_fwd, token = refs[2 * n + 2:]
        x, y, c = lax.axis_index("x"), lax.axis_index("y"), lax.axis_index("c")
        chips = [(1 - x, y), (x, 1 - y), (1 - x, 1 - y)]
        for j, (cx, cy) in enumerate(chips):
            for t in range(n):
                landed = _half_rows(ins[t], 2 * cx + cy, half[t], c)
                dst = _half_rows(outs[t], 2 * cx + cy, half[t], c)
                pltpu.make_async_remote_copy(src_ref=landed, dst_ref=landed, send_sem=send_fwd.at[3 * t + j],
                                             recv_sem=recv_ici_ref.at[3 * t + j], device_id=(cx, cy, c),
                                             device_id_type=MESH_ID).wait_recv()
                pltpu.make_async_remote_copy(src_ref=landed, dst_ref=dst, send_sem=send_fwd.at[3 * t + j],
                                             recv_sem=recv_fwd.at[3 * t + j], device_id=(x, y, 1 - c),
                                             device_id_type=MESH_ID).start()
        token[...] = jnp.zeros_like(token)

    dma = pltpu.SemaphoreType.DMA
    return pl.pallas_call(
        body, out_shape=[pltpu.HBM(b.shape, b.dtype) for b in bufs] + [dma((3 * n,)), dma((3 * n,)), _TOKEN],
        in_specs=[_HBM] * n + [_SEM, _ANY], out_specs=[_HBM] * n + [_SEM] * 2 + [_VMEM],
        input_output_aliases={t: t for t in range(n)}, name=name, compiler_params=_ORDERED_EFFECT,
    )(*bufs, recv_ici, after)


def _gather_finish_call(layer, shards, bufs, send_sems, recv_sib, send_fwd, recv_fwd, after, name):
    n = len(bufs)
    half = [b.shape[1] // 2 for b in bufs]

    def body(*refs):
        srcs, ins = refs[:n], refs[n:2 * n]
        send_ref, recv_sib_ref, send_fwd_ref, recv_fwd_ref = refs[2 * n:2 * n + 4]
        x, y, c = lax.axis_index("x"), lax.axis_index("y"), lax.axis_index("c")
        chips = [(1 - x, y), (x, 1 - y), (1 - x, 1 - y)]
        sibling = (x, y, 1 - c)
        for t in range(n):
            mine = _half_rows(srcs[t], layer, half[t], c)
            for k in range(4):
                pltpu.make_async_remote_copy(src_ref=mine, dst_ref=mine, send_sem=send_ref.at[4 * t + k],
                                             recv_sem=recv_sib_ref.at[t], device_id=sibling, device_id_type=MESH_ID).wait_send()
            from_sibling = _half_rows(ins[t], 2 * x + y, half[t], 1 - c)
            pltpu.make_async_remote_copy(src_ref=from_sibling, dst_ref=from_sibling, send_sem=send_ref.at[4 * t],
                                         recv_sem=recv_sib_ref.at[t], device_id=sibling, device_id_type=MESH_ID).wait_recv()
            for j, (cx, cy) in enumerate(chips):
                sent = _half_rows(ins[t], 2 * cx + cy, half[t], c)
                passed = _half_rows(ins[t], 2 * cx + cy, half[t], 1 - c)
                pltpu.make_async_remote_copy(src_ref=sent, dst_ref=passed, send_sem=send_fwd_ref.at[3 * t + j],
                                             recv_sem=recv_fwd_ref.at[3 * t + j], device_id=sibling, device_id_type=MESH_ID).wait()

    return pl.pallas_call(
        body, out_shape=[pltpu.HBM(b.shape, b.dtype) for b in bufs],
        in_specs=[_HBM] * (2 * n) + [_SEM] * 4 + [_ANY], out_specs=[_HBM] * n,
        input_output_aliases={n + t: t for t in range(n)}, name=name, compiler_params=_ORDERED_EFFECT,
    )(*[_in_hbm(s) for s in shards], *bufs, send_sems, recv_sib, send_fwd, recv_fwd, after)


def _pair_exchange_call(grads, name):
    n = len(grads)
    half = [g.shape[1] // 2 for g in grads]

    def body(*refs):
        srcs, outs, send_sems, recv_sems = refs[:n], refs[n:2 * n], refs[2 * n], refs[2 * n + 1]
        x, y, c = lax.axis_index("x"), lax.axis_index("y"), lax.axis_index("c")
        copies = [pltpu.make_async_remote_copy(
            src_ref=srcs[t].at[:, pl.ds(pl.multiple_of(half[t] * (1 - c), half[t]), half[t])], dst_ref=outs[t],
            send_sem=send_sems.at[t], recv_sem=recv_sems.at[t], device_id=(x, y, 1 - c), device_id_type=MESH_ID) for t in range(n)]
        for cp in copies:
            cp.start()
        for cp in copies:
            cp.wait()

    return pl.pallas_call(
        body, out_shape=[jax.ShapeDtypeStruct((g.shape[0], g.shape[1] // 2, g.shape[2]), g.dtype) for g in grads],
        in_specs=[_ANY] * n, out_specs=[_ANY] * n,
        scratch_shapes=[pltpu.SemaphoreType.DMA((n,)), pltpu.SemaphoreType.DMA((n,))], name=name)(*grads)


def _chip_scatter_call(chip_sums, name):
    n = len(chip_sums)

    def body(*refs):
        srcs, outs, send_sems, recv_sems = refs[:n], refs[n:2 * n], refs[2 * n], refs[2 * n + 1]
        x, y, c = lax.axis_index("x"), lax.axis_index("y"), lax.axis_index("c")
        chips = [(1 - x, y), (x, 1 - y), (1 - x, 1 - y)]
        copies = [pltpu.make_async_remote_copy(
            src_ref=srcs[t].at[2 * cx + cy], dst_ref=outs[t].at[k], send_sem=send_sems.at[3 * t + k],
            recv_sem=recv_sems.at[3 * t + k], device_id=(cx, cy, c), device_id_type=MESH_ID)
            for k, (cx, cy) in enumerate(chips) for t in range(n)]
        for cp in copies:
            cp.start()
        for cp in copies:
            cp.wait()

    return pl.pallas_call(
        body, out_shape=[jax.ShapeDtypeStruct((3,) + p.shape[1:], p.dtype) for p in chip_sums],
        in_specs=[_ANY] * n, out_specs=[_ANY] * n,
        scratch_shapes=[pltpu.SemaphoreType.DMA((3 * n,)), pltpu.SemaphoreType.DMA((3 * n,))], name=name)(*chip_sums)


def _pair_gather_call(bufs, name):
    n = len(bufs)
    half = [b.shape[0] // 2 for b in bufs]

    def body(*refs):
        srcs, outs, send_sems, recv_sems = refs[:n], refs[n:2 * n], refs[2 * n], refs[2 * n + 1]
        x, y, c = lax.axis_index("x"), lax.axis_index("y"), lax.axis_index("c")
        for t in range(n):
            pltpu.make_async_remote_copy(
                src_ref=_half_rows(srcs[t], None, half[t], c), dst_ref=_half_rows(outs[t], None, half[t], c),
                send_sem=send_sems.at[t], recv_sem=recv_sems.at[t], device_id=(x, y, 1 - c), device_id_type=MESH_ID).start()
        for t in range(n):
            pltpu.make_async_remote_copy(
                src_ref=_half_rows(srcs[t], None, half[t], c), dst_ref=_half_rows(outs[t], None, half[t], 1 - c),
                send_sem=send_sems.at[t], recv_sem=recv_sems.at[t], device_id=(x, y, 1 - c), device_id_type=MESH_ID).wait()

    return pl.pallas_call(
        body, out_shape=[jax.ShapeDtypeStruct(b.shape, b.dtype) for b in bufs], in_specs=[_ANY] * n, out_specs=[_ANY] * n,
        input_output_aliases={t: t for t in range(n)},
        scratch_shapes=[pltpu.SemaphoreType.DMA((n,)), pltpu.SemaphoreType.DMA((n,))], name=name)(*bufs)


def _pack_rows(flats, dtype, row_multiple):
    flat = jnp.concatenate([f.reshape(-1).astype(dtype) for f in flats])
    n = flat.shape[0]
    rows = -(-n // PACK_W)
    rows = -(-rows // row_multiple) * row_multiple
    return jnp.pad(flat, (0, rows * PACK_W - n)).reshape(rows, PACK_W)


def _unpack(flat, shapes):
    out, off = [], 0
    for shp in shapes:
        n = math.prod(shp)
        out.append(flat[off:off + n].reshape(shp))
        off += n
    return out


def _f32_as_mm_bits(a):
    return lax.bitcast_convert_type(a, jnp.bfloat16).reshape(-1)


def _mm_bits_as_f32(flat, shape):
    return lax.bitcast_convert_type(flat.reshape(-1, 2), F32).reshape(shape)


_W_IN_SEGMENTS = ((R_ML, R_END, OFF_ML), (R_SG, R_ML, OFF_SG), (R_CV, R_SGI, OFF_CV), (R_SGI, R_MQ, OFF_SGI), (R_MQ, R_SG, OFF_MQ),
                  (R_CQ, R_CKV, OFF_CQ), (R_CKV, R_KR, OFF_CKV), (R_KR, R_CV, OFF_KR + NOPE))
W_IN_SHARD = R_END // N_CHIPS


def _realign_call(wg):
    tr = 128

    def body(w_ref, o_ref):
        pieces, pos = [], 0
        for r0, r1, a0 in _W_IN_SEGMENTS:
            if a0 > pos:
                pieces.append(jnp.zeros((tr, a0 - pos), o_ref.dtype))
            while r0 < r1:
                j = r0 // W_IN_SHARD
                hi = min(r1, (j + 1) * W_IN_SHARD)
                pieces.append(w_ref[j, :, r0 - j * W_IN_SHARD:hi - j * W_IN_SHARD])
                a0, r0 = a0 + hi - r0, hi
            pos = a0
        pieces.append(jnp.zeros((tr, NP - pos), o_ref.dtype))
        o_ref[...] = jnp.concatenate(pieces, axis=1)

    return pl.pallas_call(
        body, grid=(D // tr,), in_specs=[_bs((N_CHIPS, tr, W_IN_SHARD), lambda i: (0, i, 0))],
        out_specs=_bs((tr, NP), lambda i: (i, 0)), out_shape=jax.ShapeDtypeStruct((D, NP), wg.dtype),
        name="w_in_realign", compiler_params=_cparams(1))(wg)


def _unalign_call(dw):
    tr = 128
    by_ref = sorted(_W_IN_SEGMENTS)

    def body(dw_ref, o_ref):
        for j in range(N_CHIPS):
            lo_j, hi_j = j * W_IN_SHARD, (j + 1) * W_IN_SHARD
            pieces = []
            for r0, r1, a0 in by_ref:
                lo, hi = max(r0, lo_j), min(r1, hi_j)
                if lo < hi:
                    pieces.append(dw_ref[:, a0 + lo - r0:a0 + hi - r0])
            o_ref[j] = jnp.concatenate(pieces, axis=1)

    return pl.pallas_call(
        body, grid=(D // tr,), in_specs=[_bs((tr, NP), lambda i: (i, 0))],
        out_specs=_bs((N_CHIPS, tr, W_IN_SHARD), lambda i: (0, i, 0)),
        out_shape=jax.ShapeDtypeStruct((N_CHIPS, D, W_IN_SHARD), dw.dtype), name="w_in_unalign", compiler_params=_cparams(1))(dw)


def _w_in_to_aligned(w):
    z = lambda n: jnp.zeros((w.shape[0], n), w.dtype)
    return jnp.concatenate([w[:, R_ML:R_END], w[:, R_SG:R_ML], w[:, R_CV:R_SGI], w[:, R_SGI:R_MQ], w[:, R_MQ:R_SG],
                            w[:, R_CQ:R_CKV], w[:, R_CKV:R_KR], z(NOPE), w[:, R_KR:R_CV], z(LANES - QKH)], axis=1)


def _w_in_from_aligned(wa):
    return jnp.concatenate([wa[:, OFF_CQ:OFF_CKV], wa[:, OFF_CKV:OFF_KR], wa[:, OFF_KR + NOPE:OFF_KR + QKH], wa[:, OFF_CV:OFF_SGI],
                            wa[:, OFF_SGI:OFF_MQ], wa[:, OFF_MQ:OFF_CQ], wa[:, OFF_SG:OFF_CV], wa[:, OFF_ML:OFF_SG]], axis=1)


def _wuq_to_heads(w):
    w3 = w.reshape(QL, H, QKH)
    w3 = jnp.pad(w3, ((0, 0), (0, 0), (0, LANES - QKH)))
    return jnp.transpose(w3, (1, 0, 2))


def _wuq_from_heads(wh):
    return jnp.transpose(wh[:, :, :QKH], (1, 0, 2)).reshape(QL, H * QKH)


def _wukv_to_heads(w):
    w3 = w.reshape(KVL, H, NOPE + VH)
    wkn = jnp.transpose(jnp.pad(w3[:, :, :NOPE], ((0, 0), (0, 0), (0, LANES - NOPE))), (1, 0, 2))
    wv3 = w3[:, :, NOPE:]
    z = jnp.zeros((KVL, VH), w.dtype)
    cols = []
    for h in range(H):
        cols += [wv3[:, h], z] if h % 2 == 0 else [z, wv3[:, h]]
    return wkn, jnp.concatenate(cols, axis=1)


def _wukv_from_heads(wkn, wv):
    kn = jnp.transpose(wkn[:, :, :NOPE], (1, 0, 2))
    vs = jnp.stack([wv[:, LANES * h + VH * (h % 2):LANES * h + VH * (h % 2) + VH] for h in range(H)], axis=1)
    return jnp.concatenate([kn, vs], axis=2).reshape(KVL, H * (NOPE + VH))


def _layer_fwd(x, mem, tabs, p):
    proj, h = _proj_call(x, p["norm_g"], p["w_in"])
    q, k, v = _mla_prep_call(proj, tabs, p["cq_g"], p["ckv_g"], p["qg"], p["kg"], p["wuq"], p["wkn"], p["wv"])
    ya = _attn_call(q, k, v, proj)
    bm = p["bm"]
    if p.get("after_attn") is not None:
        bm = _tie(bm, p["after_attn"](ya))
    yb = _conv_call(proj, p["conv_w"], p["conv_b"])
    yc = _sg_call(proj, p["ln_g"], p["ln_b"], p["ws"], p["bs"])
    mk, mv = _memkv_call(mem, p["mem_g"], p["wm"], p["mkg"])
    yd = _mem_call(proj, mk, mv, p["mqg"])
    out = _merge_call((ya, yb, yc, yd), proj, bm, p["wb"], p["wo"], x)
    return out, dict(x=x, proj=proj, h=h, q=q, k=k, v=v, ys=(ya, yb, yc, yd), mk=mk, mv=mv)


def _layer_bwd(dout, mem, tabs, p, sv, after_mla=None, on_grads=None):
    proj = sv["proj"]
    dya, dyb, dyc, dyd, dml, dbm, dwb, dwo = _merge_bwd_call(sv["ys"], proj, p["bm"], p["wb"], p["wo"], dout)
    dq, dk, dv, dsg_a = _attn_bwd_call(sv["q"], sv["k"], sv["v"], proj, dya)
    dcq, dckv, dkr, dcqg, dckvg, dqg, dkg, dwuq, dwkn, dwv = _mla_prep_bwd_call(
        proj, tabs, p["cq_g"], p["ckv_g"], p["qg"], p["kg"], p["wuq"], p["wkn"], p["wv"], dq, dk, dv)
    if after_mla is not None:
        after_mla(dcq)
    dbg, dcg, dxi, dsg_b, dcw, dcb = _conv_bwd_call(proj, p["conv_w"], p["conv_b"], dyb)
    du, dvv, dsg_c, dlg, dlb, dws, dbs = _sg_bwd_call(proj, p["ln_g"], p["ln_b"], p["ws"], p["bs"], dyc)
    dmq, dsg_d, dmk, dmv, dmqg = _mem_bwd_call(proj, sv["mk"], sv["mv"], p["mqg"], dyd)
    dmem_g, dwm, dmkg = _memkv_bwd_call(mem, p["mem_g"], p["wm"], p["mkg"], dmk, dmv)
    dproj = jnp.concatenate([dml, dsg_a, dsg_b, dsg_c, dsg_d, dbg, dcg, dxi, du, dvv, dmq, dcq, dckv, dkr], axis=1)
    dw_in = _dw_call(sv["h"], dproj)
    grads = dict(cq_norm_g=dcqg[0], ckv_norm_g=dckvg[0], mla_q_norm_g=dqg[0, :QKH], mla_k_norm_g=dkg[0, :QKH],
                 conv_w=dcw, conv_b=dcb[0], sg_ln_g=dlg[0], sg_ln_b=dlb[0], w_spatial=dws, b_spatial=dbs[:, :, 0],
                 mem_norm_g=dmem_g[0], mem_q_norm_g=dmqg[0], mem_k_norm_g=dmkg[0], b_merge=dbm,
                 w_in_aligned=dw_in, wuq_heads=dwuq, wkn_heads=dwkn, wv_heads=dwv, w_mem_kv=dwm, w_branch_chips=dwb, w_out=dwo)
    norm_g = p["norm_g"]
    if on_grads is not None:
        norm_g = _tie(norm_g, on_grads(grads))
    dx, dnorm_g = _dh_call(dproj, p["w_in"], sv["x"], norm_g, dout)
    grads["norm_g"] = dnorm_g[0]
    return dx, grads


def _chips_to_cols(a):
    return jnp.concatenate([a[j] for j in range(N_CHIPS)], axis=1)


def _cols_to_chips(a):
    cols = a.shape[1] // N_CHIPS
    return jnp.stack([a[:, cols * j:cols * (j + 1)] for j in range(N_CHIPS)])


def _layer_params(l, rep, gathered, conv_w, b_merge):
    pad_g = lambda g: jnp.pad(g, (0, LANES - QKH)).reshape(1, LANES)
    wkn, wv = _wukv_to_heads(_chips_to_cols(gathered["w_ukv"]))
    return dict(
        norm_g=rep["norm_g"][l].reshape(1, D), w_in=_realign_call(gathered["w_in"]),
        cq_g=rep["cq_norm_g"][l].reshape(1, QL), ckv_g=rep["ckv_norm_g"][l].reshape(1, KVL),
        qg=pad_g(rep["mla_q_norm_g"][l]), kg=pad_g(rep["mla_k_norm_g"][l]),
        wuq=_wuq_to_heads(_chips_to_cols(gathered["w_uq"])), wkn=wkn, wv=wv,
        conv_w=conv_w, conv_b=rep["conv_b"][l].reshape(1, CW),
        ln_g=rep["sg_ln_g"][l].reshape(1, SGW), ln_b=rep["sg_ln_b"][l].reshape(1, SGW),
        ws=rep["w_spatial"][l], bs=rep["b_spatial"][l].reshape(SGG, SGC, 1),
        mem_g=rep["mem_norm_g"][l].reshape(1, D), wm=gathered["w_mem_kv"].reshape(D, 2 * MH * MHD),
        mqg=rep["mem_q_norm_g"][l].reshape(1, MHD), mkg=rep["mem_k_norm_g"][l].reshape(1, MHD),
        bm=b_merge, wb=gathered["w_branch"], wo=gathered["w_out"].reshape(D, D))


def _forward_backward(x, mem, pos, target, params, bwd_hooks=None):
    tabs = _rope_tables(pos)
    params = list(params)
    saved = []
    act = x
    for l in range(DEPTH):
        if callable(params[l]):
            params[l] = params[l](saved[-1], act)
        act, sv = _layer_fwd(act, mem, tabs, params[l])
        saved.append(sv)
    dy, sq = _loss_call(act, target)
    grads = [None] * DEPTH
    for l in reversed(range(DEPTH)):
        dy, grads[l] = _layer_bwd(dy, mem, tabs, params[l], saved[l], **(bwd_hooks[l] if bwd_hooks else {}))
    return sq, dy, grads


_SHARDED_MM = ("w_in", "w_branch", "w_out", "w_mem_kv", "w_uq", "w_ukv")
_SHARDED_F32 = ("conv_w", "b_merge")
_REPLICATED = ("norm_g", "cq_norm_g", "ckv_norm_g", "mla_q_norm_g", "mla_k_norm_g", "conv_b", "sg_ln_g", "sg_ln_b",
               "w_spatial", "b_spatial", "mem_norm_g", "mem_q_norm_g", "mem_k_norm_g")
_ALL_REDUCED = _REPLICATED + _SHARDED_F32
_WEIGHTS = ("norm_g", "w_in", "cq_norm_g", "ckv_norm_g", "w_uq", "w_ukv", "mla_q_norm_g", "mla_k_norm_g", "conv_w", "conv_b",
            "sg_ln_g", "sg_ln_b", "w_spatial", "b_spatial", "mem_norm_g", "w_mem_kv", "mem_q_norm_g", "mem_k_norm_g",
            "b_merge", "w_branch", "w_out")
_BIG = ("w_in", "w_uq", "w_ukv", "w_mem_kv", "w_branch", "w_out")
_SMALL = tuple(n for n in _WEIGHTS if n not in _BIG)


def _gather_small_sharded(w):
    names = _SHARDED_F32
    packed = _pack_rows([w[n] for n in names], F32, 8)
    got = _all_gather8(packed, "gather_small_weights")
    per_chip = [_unpack(got[2 * j].reshape(-1), [w[n].shape for n in names]) for j in range(N_CHIPS)]
    return {n: jnp.concatenate([per_chip[j][t] for j in range(N_CHIPS)], axis=2) for t, n in enumerate(names)}


def _gather_layer(l, shards):
    srcs = [shards[n] for n in _SHARDED_MM]
    return dict(zip(_SHARDED_MM, _gather_layer_call(l, srcs, "gather_weights_l%d" % l)))


class _ReduceScatter:
    def __init__(self, layer):
        self.tag = "rs_l%d_" % layer

    def start(self, grads):
        c = lax.axis_index("c")
        tensors = [
            _unalign_call(grads["w_in_aligned"]),
            grads["w_branch_chips"].reshape(N_CHIPS, NB * BW, D // N_CHIPS),
            grads["w_out"].reshape(N_CHIPS, D // N_CHIPS, D),
            grads["w_mem_kv"].reshape(N_CHIPS, D // N_CHIPS, 2 * MH * MHD),
            _cols_to_chips(_wuq_from_heads(grads["wuq_heads"])),
            _cols_to_chips(_wukv_from_heads(grads["wkn_heads"], grads["wv_heads"])),
        ]
        n = len(tensors)
        from_sibling = _pair_exchange_call(tensors, self.tag + "pair_exchange")
        self.chip_sums = _pair_sum_call(tensors, from_sibling, c.astype(jnp.int32).reshape(1), self.tag + "pair_sum")
        out = _chip_scatter_start_call(self.chip_sums, self.tag + "scatter_start")
        self.bufs, self.send_sems, self.recv_sems, self.token = out[:n], out[n], out[n + 1], out[n + 2]
        return self.token

    def finish(self, after):
        x, y, c = lax.axis_index("x"), lax.axis_index("y"), lax.axis_index("c")
        chip_core = jnp.stack([2 * x + y, c]).astype(jnp.int32)
        from_chips = _chip_scatter_finish_call(self.chip_sums, self.bufs, self.send_sems, self.recv_sems, after,
                                               self.tag + "scatter_finish")
        mine = _owner_sum_call(self.chip_sums, from_chips, chip_core, self.tag + "owner_sum")
        shard = dict(zip(_SHARDED_MM, _pair_gather_call(mine, self.tag + "pair_gather")))
        shard["w_branch"] = shard["w_branch"].reshape(NB, BW, D // N_CHIPS)
        return shard


def _all_reduce_small(g):
    packed = _pack_rows([g[n] for n in _ALL_REDUCED], F32, 64)
    got = _all_gather8(packed, "gather_small_grads")
    total = _sum8_call(got).reshape(-1)
    out = dict(zip(_ALL_REDUCED, _unpack(total, [g[n].shape for n in _ALL_REDUCED])))
    chip = 2 * lax.axis_index("x") + lax.axis_index("y")
    for n in _SHARDED_F32:
        size = out[n].shape[2] // N_CHIPS
        out[n] = lax.dynamic_slice_in_dim(out[n], chip * size, size, axis=2)
    return out


def _adamw_small(w, g, m, v):
    delta, new_m, new_v = {}, {}, {}
    shapes = [w[n].shape for n in _SMALL]
    pk = lambda t: _pack_rows([t[n] for n in _SMALL], F32, 64)
    d, nm, nv = _adamw_call(pk(w), pk(g), pk(m), pk(v), "adamw_small")
    for out, packed in ((delta, d), (new_m, nm), (new_v, nv)):
        out.update(zip(_SMALL, _unpack(packed.reshape(-1), shapes)))
    return delta, new_m, new_v


def kernel(x, mem, positions, norm_g, w_in, cq_norm_g, ckv_norm_g, w_uq, w_ukv, mla_q_norm_g, mla_k_norm_g, conv_w, conv_b, sg_ln_g, sg_ln_b, w_spatial, b_spatial, mem_norm_g, w_mem_kv, mem_q_norm_g, mem_k_norm_g, b_merge, w_branch, w_out, loss_target, m_norm_g, m_w_in, m_cq_norm_g, m_ckv_norm_g, m_w_uq, m_w_ukv, m_mla_q_norm_g, m_mla_k_norm_g, m_conv_w, m_conv_b, m_sg_ln_g, m_sg_ln_b, m_w_spatial, m_b_spatial, m_mem_norm_g, m_w_mem_kv, m_mem_q_norm_g, m_mem_k_norm_g, m_b_merge, m_w_branch, m_w_out, v_norm_g, v_w_in, v_cq_norm_g, v_ckv_norm_g, v_w_uq, v_w_ukv, v_mla_q_norm_g, v_mla_k_norm_g, v_conv_w, v_conv_b, v_sg_ln_g, v_sg_ln_b, v_w_spatial, v_b_spatial, v_mem_norm_g, v_w_mem_kv, v_mem_q_norm_g, v_mem_k_norm_g, v_b_merge, v_w_branch, v_w_out):
    w = dict(norm_g=norm_g, w_in=w_in, cq_norm_g=cq_norm_g, ckv_norm_g=ckv_norm_g, w_uq=w_uq, w_ukv=w_ukv,
             mla_q_norm_g=mla_q_norm_g, mla_k_norm_g=mla_k_norm_g, conv_w=conv_w, conv_b=conv_b, sg_ln_g=sg_ln_g,
             sg_ln_b=sg_ln_b, w_spatial=w_spatial, b_spatial=b_spatial, mem_norm_g=mem_norm_g, w_mem_kv=w_mem_kv,
             mem_q_norm_g=mem_q_norm_g, mem_k_norm_g=mem_k_norm_g, b_merge=b_merge, w_branch=w_branch, w_out=w_out)
    m = dict(norm_g=m_norm_g, w_in=m_w_in, cq_norm_g=m_cq_norm_g, ckv_norm_g=m_ckv_norm_g, w_uq=m_w_uq, w_ukv=m_w_ukv,
             mla_q_norm_g=m_mla_q_norm_g, mla_k_norm_g=m_mla_k_norm_g, conv_w=m_conv_w, conv_b=m_conv_b, sg_ln_g=m_sg_ln_g,
             sg_ln_b=m_sg_ln_b, w_spatial=m_w_spatial, b_spatial=m_b_spatial, mem_norm_g=m_mem_norm_g, w_mem_kv=m_w_mem_kv,
             mem_q_norm_g=m_mem_q_norm_g, mem_k_norm_g=m_mem_k_norm_g, b_merge=m_b_merge, w_branch=m_w_branch, w_out=m_w_out)
    v = dict(norm_g=v_norm_g, w_in=v_w_in, cq_norm_g=v_cq_norm_g, ckv_norm_g=v_ckv_norm_g, w_uq=v_w_uq, w_ukv=v_w_ukv,
             mla_q_norm_g=v_mla_q_norm_g, mla_k_norm_g=v_mla_k_norm_g, conv_w=v_conv_w, conv_b=v_conv_b, sg_ln_g=v_sg_ln_g,
             sg_ln_b=v_sg_ln_b, w_spatial=v_w_spatial, b_spatial=v_b_spatial, mem_norm_g=v_mem_norm_g, w_mem_kv=v_w_mem_kv,
             mem_q_norm_g=v_mem_q_norm_g, mem_k_norm_g=v_mem_k_norm_g, b_merge=v_b_merge, w_branch=v_w_branch, w_out=v_w_out)

    small = _gather_small_sharded(w)
    shards = {n: w[n].astype(MM) for n in _SHARDED_MM}
    srcs = [shards[n] for n in _SHARDED_MM]
    n_t = len(srcs)
    chip_core = jnp.stack([2 * lax.axis_index("x") + lax.axis_index("y"), lax.axis_index("c")]).astype(jnp.int32)
    gathered0 = _gather_layer(0, shards)
    bufs = _place_own_call(1, srcs, chip_core, "gather_l1_place_own")
    started = _gather_start_call(1, srcs, bufs, gathered0["w_ukv"], "gather_l1_start")
    bufs, send_sems, recv_sib, recv_ici = started[:n_t], started[n_t], started[n_t + 1], started[n_t + 2]
    passed = []

    def pass_on(ya0):
        passed.extend(_gather_forward_call(bufs, recv_ici, ya0, "gather_l1_forward"))
        return passed[n_t + 2]

    def layer1_params(saved0, act0):
        got = _gather_finish_call(1, srcs, passed[:n_t], send_sems, recv_sib, passed[n_t], passed[n_t + 1], act0, "gather_l1_finish")
        return _layer_params(1, w, dict(zip(_SHARDED_MM, got)), small["conv_w"][1], small["b_merge"][1])

    params0 = _layer_params(0, w, gathered0, small["conv_w"][0], small["b_merge"][0])
    params0["norm_g"] = _tie(params0["norm_g"], started[n_t + 3])
    params = [dict(params0, after_attn=pass_on), layer1_params]
    rs = [_ReduceScatter(l) for l in range(DEPTH)]
    shard_grads = {}

    def land_l1(value):
        shard_grads[1] = rs[1].finish([value])

    hooks = [dict(on_grads=rs[0].start, after_mla=land_l1), dict(on_grads=rs[1].start)]
    sq, grad_x, layer_grads = _forward_backward(x[0], mem[0], positions[0], loss_target[0], params, hooks)
    loss = lax.psum(0.5 / D * jnp.sum(sq), ("x", "y", "c"))

    g = _all_reduce_small({n: jnp.stack([layer_grads[l][n] for l in range(DEPTH)]) for n in _ALL_REDUCED})
    delta, new_m, new_v = _adamw_small(w, g, m, v)
    as3d = lambda a: a.reshape(DEPTH, -1, a.shape[-1])
    big = lambda t: [as3d(t[n]) for n in _SHARDED_MM]
    as2d = lambda a: a.reshape(-1, a.shape[-1])
    upd1 = _adamw_layer_call(1, big(w), [as2d(shard_grads[1][n]) for n in _SHARDED_MM], big(m), big(v), None, [rs[0].token],
                             "adamw_l1")
    shard_grads[0] = rs[0].finish([grad_x, upd1[0], delta["norm_g"]])
    upd = _adamw_layer_call(0, big(w), [as2d(shard_grads[0][n]) for n in _SHARDED_MM], big(m), big(v), upd1, [], "adamw_l0")
    for t, n in enumerate(_SHARDED_MM):
        g[n], delta[n], new_m[n], new_v[n] = [a.reshape(w[n].shape) for a in upd[4 * t:4 * t + 4]]
    return (loss, grad_x[None], *[g[n] for n in _WEIGHTS], *[delta[n] for n in _WEIGHTS],
            *[new_m[n] for n in _WEIGHTS], *[new_v[n] for n in _WEIGHTS])
```

```python
import functools
import math

import jax
import jax.numpy as jnp
from jax import lax
from jax.experimental import pallas as pl
from jax.experimental.pallas import tpu as pltpu

F32 = jnp.float32
MM = jnp.bfloat16

D = 1024
DEPTH = 2
EPS = 1e-6
H = 8
NOPE = 64
ROPE = 32
QKH = 96
VH = 64
QL = 256
KVL = 128
ROPE_THETA = 10000.0
CW = 512
SGW = 512
SGG = 4
SGC = 128
MH = 4
MHD = 128
NB = 4
BW = 512
NEG_INF = -1e30
LANES = 128
N_CHIPS = 4

R_CQ, R_CKV, R_KR, R_CV, R_SGI, R_MQ, R_SG, R_ML, R_END = 0, 256, 384, 416, 1952, 2976, 3488, 5536, 9632
OFF_ML, OFF_SG, OFF_CV, OFF_SGI, OFF_MQ, OFF_CQ, OFF_CKV, OFF_KR, NP = 0, 4096, 6144, 7680, 8704, 9216, 9472, 9600, 9728

ADAM_LR = 0.001
ADAM_B1 = 0.9
ADAM_B2 = 0.999
ADAM_EPS = 1e-08
ADAM_WD = 0.01
ADAM_STEP = 10

VMEM_LIMIT = 56 * 1024 * 1024
PACK_W = 512
MESH_ID = pl.DeviceIdType.MESH


def _cparams(n_axes):
    return pltpu.CompilerParams(dimension_semantics=("arbitrary",) * n_axes, vmem_limit_bytes=VMEM_LIMIT)


def _bs(shape, imap):
    return pl.BlockSpec(shape, imap)


@jax.custom_vjp
def _mm_plain(a, b):
    return jnp.dot(a.astype(MM), b.astype(MM), preferred_element_type=F32)


def _mm_plain_fwd(a, b):
    return _mm_plain(a, b), (a, b)


def _mm_plain_bwd(res, g):
    a, b = res
    gm = g.astype(MM)
    da = lax.dot_general(gm, b.astype(MM), (((1,), (1,)), ((), ())), preferred_element_type=F32)
    db = lax.dot_general(a.astype(MM), gm, (((0,), (0,)), ((), ())), preferred_element_type=F32)
    return da.astype(a.dtype), db.astype(b.dtype)


_mm_plain.defvjp(_mm_plain_fwd, _mm_plain_bwd)


@jax.custom_vjp
def _mm_slot(a, w, slot):
    return jnp.dot(a.astype(MM), w.astype(MM), preferred_element_type=F32)


def _mm_slot_fwd(a, w, slot):
    return _mm_slot(a, w, slot), (a, w)


def _mm_slot_bwd(res, g):
    a, w = res
    gm = g.astype(MM)
    da = lax.dot_general(gm, w.astype(MM), (((1,), (1,)), ((), ())), preferred_element_type=F32)
    dw = lax.dot_general(a.astype(MM), gm, (((0,), (0,)), ((), ())), preferred_element_type=F32)
    return da.astype(a.dtype), jnp.zeros_like(w), dw


_mm_slot.defvjp(_mm_slot_fwd, _mm_slot_bwd)


def _mm(a, b):
    if isinstance(b, tuple):
        return _mm_slot(a, b[0], b[1])
    return _mm_plain(a, b)


def _with_slot(w):
    return (w, jnp.zeros(w.shape, F32))


@jax.custom_vjp
def _mm_nt(a, b):
    return lax.dot_general(a.astype(MM), b.astype(MM), (((1,), (1,)), ((), ())), preferred_element_type=F32)


def _mm_nt_fwd(a, b):
    return _mm_nt(a, b), (a, b)


def _mm_nt_bwd(res, g):
    a, b = res
    gm = g.astype(MM)
    da = jnp.dot(gm, b.astype(MM), preferred_element_type=F32)
    db = lax.dot_general(gm, a.astype(MM), (((0,), (0,)), ((), ())), preferred_element_type=F32)
    return da.astype(a.dtype), db.astype(b.dtype)


_mm_nt.defvjp(_mm_nt_fwd, _mm_nt_bwd)


@functools.partial(jax.custom_vjp, nondiff_argnums=(1,))
def _lane_roll(x, shift):
    return pltpu.roll(x, shift, 1)


def _lane_roll_fwd(x, shift):
    return pltpu.roll(x, shift, 1), None


def _lane_roll_bwd(shift, _, g):
    return (pltpu.roll(g, (LANES - shift) % LANES, 1),)


_lane_roll.defvjp(_lane_roll_fwd, _lane_roll_bwd)


def _rms_n(x, g, n):
    ms = jnp.sum(x * x, axis=-1, keepdims=True) * (1.0 / n)
    return x * lax.rsqrt(ms + EPS) * g


def _softmax(s):
    m = jnp.max(s, axis=-1, keepdims=True)
    e = jnp.exp(s - m)
    return e / jnp.sum(e, axis=-1, keepdims=True)


def _rope(t, cos_t, sin_a, sin_b):
    return t * cos_t + _lane_roll(t, LANES - 16) * sin_a + _lane_roll(t, 16) * sin_b


def _mla_prep_fn(cq, ckv, kr, cos_t, sin_a, sin_b, cq_g, ckv_g, qg, kg, wuq, wkn, wv):
    cqn = _rms_n(cq, cq_g, QL)
    ckvn = _rms_n(ckv, ckv_g, KVL)
    lane = lax.broadcasted_iota(jnp.int32, kr.shape, 1)
    krm = jnp.where((lane >= NOPE) & (lane < QKH), kr, 0.0)
    qs, ks = [], []
    for h in range(H):
        qh = _rms_n(_mm(cqn, wuq[h]), qg, QKH)
        qs.append(_rope(qh, cos_t, sin_a, sin_b))
        kh = _rms_n(_mm(ckvn, wkn[h]) + krm, kg, QKH)
        ks.append(_rope(kh, cos_t, sin_a, sin_b))
    return jnp.concatenate(qs, axis=-1), jnp.concatenate(ks, axis=-1), _mm(ckvn, wv)


def _attn_pair_fn(q2, k2, v2, sg, row0):
    tq, s_len = q2.shape[0], k2.shape[0]
    rows = row0 + lax.broadcasted_iota(jnp.int32, (tq, s_len), 0)
    cols = lax.broadcasted_iota(jnp.int32, (tq, s_len), 1)
    mask = cols <= rows
    vlane = lax.broadcasted_iota(jnp.int32, (s_len, LANES), 1)
    o = jnp.zeros((tq, LANES), F32)
    for e in range(2):
        sl = slice(LANES * e, LANES * (e + 1))
        s = _mm_nt(q2[:, sl], k2[:, sl]) * (QKH ** -0.5)
        p = _softmax(jnp.where(mask, s, NEG_INF))
        ve = jnp.where((vlane >= VH * e) & (vlane < VH * (e + 1)), v2[:, sl], 0.0)
        o = o + _mm(p, ve)
    return o * jax.nn.silu(sg)


def _sg_fn(u, v, sgc, ln_g, ln_b, ws, bs):
    mu = jnp.mean(v, axis=-1, keepdims=True)
    xc = v - mu
    vn = xc * lax.rsqrt(jnp.mean(xc * xc, axis=-1, keepdims=True) + EPS) * ln_g + ln_b
    r = lax.broadcasted_iota(jnp.int32, (SGC, SGC), 0)
    c = lax.broadcasted_iota(jnp.int32, (SGC, SGC), 1)
    wt = [jnp.where(r >= c, w, 0.0) for w in ws]
    row_blocks = []
    for ch in range(u.shape[0] // SGC):
        col_blocks = []
        for g in range(SGG):
            blk = vn[SGC * ch:SGC * (ch + 1), LANES * g:LANES * (g + 1)]
            col_blocks.append(_mm(wt[g], blk) + bs[g])
        row_blocks.append(jnp.concatenate(col_blocks, axis=-1))
    mixed = jnp.concatenate(row_blocks, axis=0)
    return (u * mixed) * jax.nn.silu(sgc)


def _memkv_fn(mem, mem_g, wm, kg):
    kv = _mm(_rms_n(mem, mem_g, D), wm)
    ks = [_rms_n(kv[:, MHD * h:MHD * (h + 1)], kg, MHD) for h in range(MH)]
    return jnp.concatenate(ks, axis=-1), kv[:, MH * MHD:]


def _mem_fn(mq, sgd, k, v, qg):
    outs = []
    for h in range(MH):
        sl = slice(MHD * h, MHD * (h + 1))
        qh = _rms_n(mq[:, sl], qg, MHD)
        p = _softmax(_mm_nt(qh, k[:, sl]) * (MHD ** -0.5))
        outs.append(_mm(p, v[:, sl]))
    return jnp.concatenate(outs, axis=-1) * jax.nn.silu(sgd)


def _merge_fn(ys, logits, bm, wb, wo):
    merged = None
    for n in range(NB):
        z = jnp.concatenate([_mm(ys[n], wb[j][n]) for j in range(N_CHIPS)], axis=-1)
        gate = jax.nn.sigmoid(logits[:, D * n:D * (n + 1)] + bm[n])
        merged = gate * z if merged is None else merged + gate * z
    return _mm(merged, wo)


def _proj_call(x, g, w):
    s_len = x.shape[0]
    tm, tn = min(s_len, 1024), NP // 4

    def body(x_ref, g_ref, w_ref, p_ref, h_ref):
        @pl.when(pl.program_id(1) == 0)
        def _():
            h_ref[...] = _rms_n(x_ref[...], g_ref[...], D).astype(h_ref.dtype)
        p_ref[...] = jnp.dot(h_ref[...], w_ref[...], preferred_element_type=F32)

    return pl.pallas_call(
        body, grid=(s_len // tm, NP // tn),
        in_specs=[_bs((tm, D), lambda i, j: (i, 0)), _bs((1, D), lambda i, j: (0, 0)), _bs((D, tn), lambda i, j: (0, j))],
        out_specs=[_bs((tm, tn), lambda i, j: (i, j)), _bs((tm, D), lambda i, j: (i, 0))],
        out_shape=[jax.ShapeDtypeStruct((s_len, NP), F32), jax.ShapeDtypeStruct((s_len, D), MM)],
        name="proj", compiler_params=_cparams(2))(x, g, w)


def _rope_tables(pos):
    half = ROPE // 2
    inv_freq = ROPE_THETA ** (-jnp.arange(half, dtype=F32) / half)
    ang = pos.astype(F32)[:, None] * inv_freq
    cos, sin = jnp.cos(ang), jnp.sin(ang)
    s_len = pos.shape[0]
    z = lambda n: jnp.zeros((s_len, n), F32)
    cos_t = jnp.concatenate([jnp.ones((s_len, NOPE), F32), cos, cos, z(LANES - QKH)], axis=1)
    sin_a = jnp.concatenate([z(NOPE), -sin, z(LANES - NOPE - half)], axis=1)
    sin_b = jnp.concatenate([z(NOPE + half), sin, z(LANES - QKH)], axis=1)
    return cos_t, sin_a, sin_b


def _mla_prep_specs(tm):
    row = lambda w, off: _bs((tm, w), lambda i: (i, off // w))
    full2 = lambda a, b: _bs((a, b), lambda i: (0, 0))
    full3 = lambda a, b, c: _bs((a, b, c), lambda i: (0, 0, 0))
    tab = _bs((tm, LANES), lambda i: (i, 0))
    return [row(QL, OFF_CQ), row(KVL, OFF_CKV), row(LANES, OFF_KR), tab, tab, tab,
            full2(1, QL), full2(1, KVL), full2(1, LANES), full2(1, LANES),
            full3(H, QL, LANES), full3(H, KVL, LANES), full2(KVL, H * LANES)]


def _mla_prep_args(body_refs, wrap=lambda w: w):
    (cq, ckv, kr, ct, sa, sb, cqg, ckvg, qg, kg, wuq, wkn, wv) = body_refs
    return (cq[...], ckv[...], kr[...], ct[...], sa[...], sb[...], cqg[...], ckvg[...], qg[...], kg[...],
            [wrap(wuq[h]) for h in range(H)], [wrap(wkn[h]) for h in range(H)], wrap(wv[...]))


def _mla_prep_call(proj, tabs, cq_g, ckv_g, qg, kg, wuq, wkn, wv):
    s_len = proj.shape[0]
    tm = min(s_len, 256)

    def body(*refs):
        q_ref, k_ref, v_ref = refs[13:]
        q, k, v = _mla_prep_fn(*_mla_prep_args(refs[:13]))
        q_ref[...] = q.astype(q_ref.dtype)
        k_ref[...] = k.astype(k_ref.dtype)
        v_ref[...] = v.astype(v_ref.dtype)

    out = _bs((tm, H * LANES), lambda i: (i, 0))
    return pl.pallas_call(
        body, grid=(s_len // tm,), in_specs=_mla_prep_specs(tm), out_specs=[out, out, out],
        out_shape=[jax.ShapeDtypeStruct((s_len, H * LANES), MM)] * 3,
        name="mla_prep", compiler_params=_cparams(1))(proj, proj, proj, *tabs, cq_g, ckv_g, qg, kg, wuq, wkn, wv)


def _mla_prep_bwd_call(proj, tabs, cq_g, ckv_g, qg, kg, wuq, wkn, wv, dq, dk, dv):
    s_len = proj.shape[0]
    tm = min(s_len, 256)

    def body(*refs):
        dq_ref, dk_ref, dv_ref = refs[13:16]
        dcq_ref, dckv_ref, dkr_ref, dcqg_ref, dckvg_ref, dqg_ref, dkg_ref, dwuq_ref, dwkn_ref, dwv_ref = refs[16:]
        _, vjp = jax.vjp(_mla_prep_fn, *_mla_prep_args(refs[:13], _with_slot))
        (dcq, dckv, dkr, _, _, _, dcqg, dckvg, dqg, dkg, dwuq, dwkn, dwv) = vjp((dq_ref[...], dk_ref[...], dv_ref[...]))
        dwuq, dwkn, dwv = [d[1] for d in dwuq], [d[1] for d in dwkn], dwv[1]
        dcq_ref[...] = dcq.astype(dcq_ref.dtype)
        dckv_ref[...] = dckv.astype(dckv_ref.dtype)
        dkr_ref[...] = dkr.astype(dkr_ref.dtype)

        @pl.when(pl.program_id(0) == 0)
        def _():
            for r in (dcqg_ref, dckvg_ref, dqg_ref, dkg_ref, dwuq_ref, dwkn_ref, dwv_ref):
                r[...] = jnp.zeros_like(r)
        dcqg_ref[...] += dcqg
        dckvg_ref[...] += dckvg
        dqg_ref[...] += dqg
        dkg_ref[...] += dkg
        for h in range(H):
            dwuq_ref[h] += dwuq[h]
            dwkn_ref[h] += dwkn[h]
        dwv_ref[...] += dwv

    big = _bs((tm, H * LANES), lambda i: (i, 0))
    row = lambda w: _bs((tm, w), lambda i: (i, 0))
    full2 = lambda a, b: _bs((a, b), lambda i: (0, 0))
    full3 = lambda a, b, c: _bs((a, b, c), lambda i: (0, 0, 0))
    sd = jax.ShapeDtypeStruct
    return pl.pallas_call(
        body, grid=(s_len // tm,), in_specs=_mla_prep_specs(tm) + [big, big, big],
        out_specs=[row(QL), row(KVL), row(LANES), full2(1, QL), full2(1, KVL), full2(1, LANES), full2(1, LANES),
                   full3(H, QL, LANES), full3(H, KVL, LANES), full2(KVL, H * LANES)],
        out_shape=[sd((s_len, QL), MM), sd((s_len, KVL), MM), sd((s_len, LANES), MM), sd((1, QL), F32), sd((1, KVL), F32),
                   sd((1, LANES), F32), sd((1, LANES), F32), sd((H, QL, LANES), F32), sd((H, KVL, LANES), F32),
                   sd((KVL, H * LANES), F32)],
        name="mla_prep_bwd", compiler_params=_cparams(1))(proj, proj, proj, *tabs, cq_g, ckv_g, qg, kg, wuq, wkn, wv, dq, dk, dv)


def _attn_specs(s_len, tq):
    pair = 2 * LANES
    return [_bs((tq, pair), lambda p, i: (i, p)), _bs((s_len, pair), lambda p, i: (0, p)), _bs((s_len, pair), lambda p, i: (0, p)),
            _bs((tq, LANES), lambda p, i: (i, OFF_SG // LANES + p))]


def _attn_call(q, k, v, proj):
    s_len = q.shape[0]
    tq = min(s_len, 256)

    def body(q_ref, k_ref, v_ref, sg_ref, y_ref):
        for n in range(s_len // tq):
            @pl.when(pl.program_id(1) == n)
            def _():
                kl = (n + 1) * tq
                y_ref[...] = _attn_pair_fn(q_ref[...], k_ref[:kl, :], v_ref[:kl, :], sg_ref[...], n * tq).astype(y_ref.dtype)

    return pl.pallas_call(
        body, grid=(H // 2, s_len // tq), in_specs=_attn_specs(s_len, tq),
        out_specs=_bs((tq, LANES), lambda p, i: (i, p)), out_shape=jax.ShapeDtypeStruct((s_len, BW), MM),
        name="attn", compiler_params=_cparams(2))(q, k, v, proj)


def _attn_bwd_call(q, k, v, proj, dys):
    s_len = q.shape[0]
    tq = min(s_len, 256)
    pair = 2 * LANES

    def body(q_ref, k_ref, v_ref, sg_ref, dy_ref, dq_ref, dk_ref, dv_ref, dsg_ref):
        i = pl.program_id(1)

        @pl.when(i == 0)
        def _():
            dk_ref[...] = jnp.zeros_like(dk_ref)
            dv_ref[...] = jnp.zeros_like(dv_ref)

        for n in range(s_len // tq):
            @pl.when(i == n)
            def _():
                kl = (n + 1) * tq
                fn = functools.partial(_attn_pair_fn, row0=n * tq)
                _, vjp = jax.vjp(fn, q_ref[...].astype(F32), k_ref[:kl, :].astype(F32), v_ref[:kl, :].astype(F32), sg_ref[...])
                dq, dk, dv, dsg = vjp(dy_ref[...])
                dq_ref[...] = dq
                dsg_ref[...] = dsg.astype(dsg_ref.dtype)
                dk_ref[:kl, :] += dk
                dv_ref[:kl, :] += dv

    sd = jax.ShapeDtypeStruct
    return pl.pallas_call(
        body, grid=(H // 2, s_len // tq),
        in_specs=_attn_specs(s_len, tq) + [_bs((tq, LANES), lambda p, i: (i, p))],
        out_specs=[_bs((tq, pair), lambda p, i: (i, p)), _bs((s_len, pair), lambda p, i: (0, p)),
                   _bs((s_len, pair), lambda p, i: (0, p)), _bs((tq, LANES), lambda p, i: (i, p))],
        out_shape=[sd((s_len, H * LANES), F32), sd((s_len, H * LANES), F32), sd((s_len, H * LANES), F32), sd((s_len, BW), MM)],
        name="attn_bwd", compiler_params=_cparams(2))(q, k, v, proj, dys)


def _shift_down(a, n):
    r = lax.broadcasted_iota(jnp.int32, a.shape, 0)
    return jnp.where(r >= n, pltpu.roll(a, n, 0), 0.0)


def _shift_up(a, n):
    s_len = a.shape[0]
    r = lax.broadcasted_iota(jnp.int32, a.shape, 0)
    return jnp.where(r < s_len - n, pltpu.roll(a, s_len - n, 0), 0.0)


def _conv_specs(s_len):
    col = lambda off: _bs((s_len, LANES), lambda j: (0, off // LANES + j))
    return [col(OFF_CV), col(OFF_CV + CW), col(OFF_CV + 2 * CW), col(OFF_SG + BW),
            _bs((3, LANES), lambda j: (0, j)), _bs((1, LANES), lambda j: (0, j))]


def _conv_call(proj, cw, cb):
    s_len = proj.shape[0]

    def body(bg_ref, cg_ref, xi_ref, sg_ref, w_ref, b_ref, y_ref):
        z = cg_ref[...] * xi_ref[...]
        y = b_ref[...] + w_ref[0:1, :] * _shift_down(z, 2)
        y = y + w_ref[1:2, :] * _shift_down(z, 1)
        y = y + w_ref[2:3, :] * z
        y_ref[...] = ((bg_ref[...] * y) * jax.nn.silu(sg_ref[...])).astype(y_ref.dtype)

    return pl.pallas_call(
        body, grid=(CW // LANES,), in_specs=_conv_specs(s_len), out_specs=_bs((s_len, LANES), lambda j: (0, j)),
        out_shape=jax.ShapeDtypeStruct((s_len, CW), MM), name="conv", compiler_params=_cparams(1))(proj, proj, proj, proj, cw, cb)


def _conv_bwd_call(proj, cw, cb, dys):
    s_len = proj.shape[0]

    def body(bg_ref, cg_ref, xi_ref, sg_ref, w_ref, b_ref, dys_ref, dbg_ref, dcg_ref, dxi_ref, dsg_ref, dw_ref, db_ref):
        bg, cg, xi, sg = bg_ref[...], cg_ref[...], xi_ref[...], sg_ref[...]
        w0, w1, w2 = w_ref[0:1, :], w_ref[1:2, :], w_ref[2:3, :]
        z = cg * xi
        z1, z2 = _shift_down(z, 1), _shift_down(z, 2)
        y = b_ref[...] + w0 * z2
        y = y + w1 * z1
        y = y + w2 * z
        yb = bg * y
        sig = jax.nn.sigmoid(sg)
        silu = sg * sig
        dys_v = dys_ref[...]
        dsg_ref[...] = (dys_v * yb * (sig * (1.0 + sg * (1.0 - sig)))).astype(dsg_ref.dtype)
        dyb = dys_v * silu
        dbg_ref[...] = (dyb * y).astype(dbg_ref.dtype)
        dy = dyb * bg
        db_ref[...] = jnp.sum(dy, axis=0, keepdims=True)
        dw_ref[0:1, :] = jnp.sum(dy * z2, axis=0, keepdims=True)
        dw_ref[1:2, :] = jnp.sum(dy * z1, axis=0, keepdims=True)
        dw_ref[2:3, :] = jnp.sum(dy * z, axis=0, keepdims=True)
        dz = w2 * dy + w1 * _shift_up(dy, 1) + w0 * _shift_up(dy, 2)
        dcg_ref[...] = (dz * xi).astype(dcg_ref.dtype)
        dxi_ref[...] = (dz * cg).astype(dxi_ref.dtype)

    col = _bs((s_len, LANES), lambda j: (0, j))
    sd = jax.ShapeDtypeStruct
    return pl.pallas_call(
        body, grid=(CW // LANES,), in_specs=_conv_specs(s_len) + [col],
        out_specs=[col, col, col, col, _bs((3, LANES), lambda j: (0, j)), _bs((1, LANES), lambda j: (0, j))],
        out_shape=[sd((s_len, CW), MM)] * 4 + [sd((3, CW), F32), sd((1, CW), F32)],
        name="conv_bwd", compiler_params=_cparams(1))(proj, proj, proj, proj, cw, cb, dys)


def _sg_specs(tm):
    row = lambda off: _bs((tm, SGW), lambda i: (i, off // SGW))
    return [row(OFF_SGI), row(OFF_SGI + SGW), row(OFF_SG + 2 * BW), _bs((1, SGW), lambda i: (0, 0)), _bs((1, SGW), lambda i: (0, 0)),
            _bs((SGG, SGC, SGC), lambda i: (0, 0, 0)), _bs((SGG, SGC, 1), lambda i: (0, 0, 0))]


def _sg_args(refs):
    u, v, sg, lg, lb, ws, bs = refs
    return (u[...], v[...], sg[...], lg[...], lb[...], [ws[g] for g in range(SGG)], [bs[g] for g in range(SGG)])


def _sg_call(proj, ln_g, ln_b, ws, bs):
    s_len = proj.shape[0]
    tm = min(s_len, 256)

    def body(*refs):
        refs[7][...] = _sg_fn(*_sg_args(refs[:7])).astype(refs[7].dtype)

    return pl.pallas_call(
        body, grid=(s_len // tm,), in_specs=_sg_specs(tm), out_specs=_bs((tm, SGW), lambda i: (i, 0)),
        out_shape=jax.ShapeDtypeStruct((s_len, SGW), MM), name="sgmlp", compiler_params=_cparams(1))(proj, proj, proj, ln_g, ln_b, ws, bs)


def _sg_bwd_call(proj, ln_g, ln_b, ws, bs, dys):
    s_len = proj.shape[0]
    tm = min(s_len, 256)

    def body(*refs):
        dys_ref = refs[7]
        du_ref, dv_ref, dsg_ref, dlg_ref, dlb_ref, dws_ref, dbs_ref = refs[8:]
        _, vjp = jax.vjp(_sg_fn, *_sg_args(refs[:7]))
        du, dv, dsg, dlg, dlb, dws, dbs = vjp(dys_ref[...])
        du_ref[...] = du.astype(du_ref.dtype)
        dv_ref[...] = dv.astype(dv_ref.dtype)
        dsg_ref[...] = dsg.astype(dsg_ref.dtype)

        @pl.when(pl.program_id(0) == 0)
        def _():
            for r in (dlg_ref, dlb_ref, dws_ref, dbs_ref):
                r[...] = jnp.zeros_like(r)
        dlg_ref[...] += dlg
        dlb_ref[...] += dlb
        for g in range(SGG):
            dws_ref[g] += dws[g]
            dbs_ref[g] += dbs[g]

    row = _bs((tm, SGW), lambda i: (i, 0))
    sd = jax.ShapeDtypeStruct
    return pl.pallas_call(
        body, grid=(s_len // tm,), in_specs=_sg_specs(tm) + [row],
        out_specs=[row, row, row, _bs((1, SGW), lambda i: (0, 0)), _bs((1, SGW), lambda i: (0, 0)),
                   _bs((SGG, SGC, SGC), lambda i: (0, 0, 0)), _bs((SGG, SGC, 1), lambda i: (0, 0, 0))],
        out_shape=[sd((s_len, SGW), MM)] * 3 + [sd((1, SGW), F32), sd((1, SGW), F32), sd((SGG, SGC, SGC), F32), sd((SGG, SGC, 1), F32)],
        name="sgmlp_bwd", compiler_params=_cparams(1))(proj, proj, proj, ln_g, ln_b, ws, bs, dys)


def _memkv_call(mem, mem_g, wm, kg):
    m_len = mem.shape[0]

    def body(mem_ref, g_ref, w_ref, kg_ref, k_ref, v_ref):
        k, v = _memkv_fn(mem_ref[...], g_ref[...], w_ref[...], kg_ref[...])
        k_ref[...] = k.astype(k_ref.dtype)
        v_ref[...] = v.astype(v_ref.dtype)

    return pl.pallas_call(body, out_shape=[jax.ShapeDtypeStruct((m_len, MH * MHD), MM)] * 2, name="memkv",
                          compiler_params=pltpu.CompilerParams(vmem_limit_bytes=VMEM_LIMIT))(mem, mem_g, wm, kg)


def _memkv_bwd_call(mem, mem_g, wm, kg, dk, dv):
    def body(mem_ref, g_ref, w_ref, kg_ref, dk_ref, dv_ref, dg_ref, dw_ref, dkg_ref):
        _, vjp = jax.vjp(_memkv_fn, mem_ref[...], g_ref[...], _with_slot(w_ref[...]), kg_ref[...])
        _, dg, dw, dkg = vjp((dk_ref[...], dv_ref[...]))
        dg_ref[...] = dg
        dw_ref[...] = dw[1]
        dkg_ref[...] = dkg

    sd = jax.ShapeDtypeStruct
    return pl.pallas_call(body, out_shape=[sd((1, D), F32), sd((D, 2 * MH * MHD), F32), sd((1, MHD), F32)], name="memkv_bwd",
                          compiler_params=pltpu.CompilerParams(vmem_limit_bytes=VMEM_LIMIT))(mem, mem_g, wm, kg, dk, dv)


def _mem_specs(tm, m_len):
    w = MH * MHD
    return [_bs((tm, w), lambda i: (i, OFF_MQ // w)), _bs((tm, BW), lambda i: (i, (OFF_SG + 3 * BW) // BW)),
            _bs((m_len, w), lambda i: (0, 0)), _bs((m_len, w), lambda i: (0, 0)), _bs((1, MHD), lambda i: (0, 0))]


def _mem_call(proj, k, v, qg):
    s_len, m_len = proj.shape[0], k.shape[0]
    tm = min(s_len, 256)

    def body(mq_ref, sg_ref, k_ref, v_ref, qg_ref, y_ref):
        y_ref[...] = _mem_fn(mq_ref[...], sg_ref[...], k_ref[...], v_ref[...], qg_ref[...]).astype(y_ref.dtype)

    return pl.pallas_call(
        body, grid=(s_len // tm,), in_specs=_mem_specs(tm, m_len), out_specs=_bs((tm, BW), lambda i: (i, 0)),
        out_shape=jax.ShapeDtypeStruct((s_len, BW), MM), name="memattn", compiler_params=_cparams(1))(proj, proj, k, v, qg)


def _mem_bwd_call(proj, k, v, qg, dys):
    s_len, m_len = proj.shape[0], k.shape[0]
    tm = min(s_len, 256)
    w = MH * MHD

    def body(mq_ref, sg_ref, k_ref, v_ref, qg_ref, dys_ref, dmq_ref, dsg_ref, dk_ref, dv_ref, dqg_ref):
        _, vjp = jax.vjp(_mem_fn, mq_ref[...], sg_ref[...], k_ref[...].astype(F32), v_ref[...].astype(F32), qg_ref[...])
        dmq, dsg, dk, dv, dqg = vjp(dys_ref[...])
        dmq_ref[...] = dmq.astype(dmq_ref.dtype)
        dsg_ref[...] = dsg.astype(dsg_ref.dtype)

        @pl.when(pl.program_id(0) == 0)
        def _():
            for r in (dk_ref, dv_ref, dqg_ref):
                r[...] = jnp.zeros_like(r)
        dk_ref[...] += dk
        dv_ref[...] += dv
        dqg_ref[...] += dqg

    row = _bs((tm, BW), lambda i: (i, 0))
    kv = _bs((m_len, w), lambda i: (0, 0))
    sd = jax.ShapeDtypeStruct
    return pl.pallas_call(
        body, grid=(s_len // tm,), in_specs=_mem_specs(tm, m_len) + [row],
        out_specs=[row, row, kv, kv, _bs((1, MHD), lambda i: (0, 0))],
        out_shape=[sd((s_len, w), MM), sd((s_len, BW), MM), sd((m_len, w), F32), sd((m_len, w), F32), sd((1, MHD), F32)],
        name="memattn_bwd", compiler_params=_cparams(1))(proj, proj, k, v, qg, dys)


def _merge_specs(tm):
    row = _bs((tm, BW), lambda i: (i, 0))
    return [row, row, row, row, _bs((tm, NB * D), lambda i: (i, OFF_ML // (NB * D))), _bs((NB, D), lambda i: (0, 0)),
            _bs((N_CHIPS, NB, BW, D // N_CHIPS), lambda i: (0, 0, 0, 0)), _bs((D, D), lambda i: (0, 0))]


def _merge_call(ys, proj, bm, wb, wo, x):
    s_len = proj.shape[0]
    tm = min(s_len, 256)

    def body(ya, yb, yc, yd, lg_ref, bm_ref, wb_ref, wo_ref, x_ref, o_ref):
        out = _merge_fn([r[...] for r in (ya, yb, yc, yd)], lg_ref[...], [bm_ref[n:n + 1, :] for n in range(NB)],
                        [[wb_ref[j, n] for n in range(NB)] for j in range(N_CHIPS)], wo_ref[...])
        o_ref[...] = x_ref[...] + out

    xrow = _bs((tm, D), lambda i: (i, 0))
    return pl.pallas_call(
        body, grid=(s_len // tm,), in_specs=_merge_specs(tm) + [xrow], out_specs=xrow,
        out_shape=jax.ShapeDtypeStruct((s_len, D), F32), name="merge", compiler_params=_cparams(1))(*ys, proj, bm, wb, wo, x)


def _merge_bwd_call(ys, proj, bm, wb, wo, dout):
    s_len = proj.shape[0]
    tm = min(s_len, 256)

    def body(ya, yb, yc, yd, lg_ref, bm_ref, wb_ref, wo_ref, do_ref, dya, dyb, dyc, dyd, dlg_ref, dbm_ref, dwb_ref, dwo_ref):
        fn = lambda ys_, lg_, bm_, wb_, wo_: _merge_fn(ys_, lg_, bm_, wb_, wo_)
        _, vjp = jax.vjp(fn, [r[...].astype(F32) for r in (ya, yb, yc, yd)], lg_ref[...], [bm_ref[n:n + 1, :] for n in range(NB)],
                         [[_with_slot(wb_ref[j, n]) for n in range(NB)] for j in range(N_CHIPS)], _with_slot(wo_ref[...]))
        dys, dlg, dbm, dwb, dwo = vjp(do_ref[...])
        dwb, dwo = [[d[1] for d in row] for row in dwb], dwo[1]
        for r, d in zip((dya, dyb, dyc, dyd), dys):
            r[...] = d
        dlg_ref[...] = dlg.astype(dlg_ref.dtype)

        @pl.when(pl.program_id(0) == 0)
        def _():
            for r in (dbm_ref, dwb_ref, dwo_ref):
                r[...] = jnp.zeros_like(r)
        for n in range(NB):
            dbm_ref[n:n + 1, :] += dbm[n]
            for j in range(N_CHIPS):
                dwb_ref[j, n] += dwb[j][n]
        dwo_ref[...] += dwo

    row = _bs((tm, BW), lambda i: (i, 0))
    sd = jax.ShapeDtypeStruct
    wb_shape = (N_CHIPS, NB, BW, D // N_CHIPS)
    return pl.pallas_call(
        body, grid=(s_len // tm,), in_specs=_merge_specs(tm) + [_bs((tm, D), lambda i: (i, 0))],
        out_specs=[row, row, row, row, _bs((tm, NB * D), lambda i: (i, 0)), _bs((NB, D), lambda i: (0, 0)),
                   _bs(wb_shape, lambda i: (0, 0, 0, 0)), _bs((D, D), lambda i: (0, 0))],
        out_shape=[sd((s_len, BW), F32)] * 4 + [sd((s_len, NB * D), MM), sd((NB, D), F32), sd(wb_shape, F32), sd((D, D), F32)],
        name="merge_bwd", compiler_params=_cparams(1))(*ys, proj, bm, wb, wo, dout)


def _dh_call(dproj, w, x, g, dout):
    s_len = x.shape[0]
    tm, tk = min(s_len, 512), NP // 4

    def body(dp_ref, w_ref, x_ref, g_ref, do_ref, dx_ref, dg_ref, acc_ref):
        i, k = pl.program_id(0), pl.program_id(1)

        @pl.when(k == 0)
        def _():
            acc_ref[...] = jnp.zeros_like(acc_ref)
        acc_ref[...] += lax.dot_general(dp_ref[...], w_ref[...], (((1,), (1,)), ((), ())), preferred_element_type=F32)

        @pl.when(k == pl.num_programs(1) - 1)
        def _():
            _, vjp = jax.vjp(lambda x_, g_: _rms_n(x_, g_, D), x_ref[...], g_ref[...])
            dxr, dgr = vjp(acc_ref[...])
            dx_ref[...] = do_ref[...] + dxr

            @pl.when(i == 0)
            def _():
                dg_ref[...] = jnp.zeros_like(dg_ref)
            dg_ref[...] += dgr

    row = _bs((tm, D), lambda i, k: (i, 0))
    return pl.pallas_call(
        body, grid=(s_len // tm, NP // tk),
        in_specs=[_bs((tm, tk), lambda i, k: (i, k)), _bs((D, tk), lambda i, k: (0, k)), row, _bs((1, D), lambda i, k: (0, 0)), row],
        out_specs=[row, _bs((1, D), lambda i, k: (0, 0))],
        out_shape=[jax.ShapeDtypeStruct((s_len, D), F32), jax.ShapeDtypeStruct((1, D), F32)],
        scratch_shapes=[pltpu.VMEM((tm, D), F32)], name="dh", compiler_params=_cparams(2))(dproj, w, x, g, dout)


def _dw_call(h, dproj):
    s_len = h.shape[0]
    tn = 512

    def body(h_ref, dp_ref, o_ref):
        o_ref[...] = lax.dot_general(h_ref[...], dp_ref[...], (((0,), (0,)), ((), ())), preferred_element_type=F32)

    return pl.pallas_call(
        body, grid=(NP // tn,), in_specs=[_bs((s_len, D), lambda j: (0, 0)), _bs((s_len, tn), lambda j: (0, j))],
        out_specs=_bs((D, tn), lambda j: (0, j)), out_shape=jax.ShapeDtypeStruct((D, NP), F32),
        name="dw_in", compiler_params=_cparams(1))(h, dproj)


def _loss_call(y, target):
    s_len = y.shape[0]
    tm = min(s_len, 512)

    def body(y_ref, t_ref, dy_ref, l_ref):
        e = y_ref[...] - t_ref[...]
        dy_ref[...] = e * (1.0 / D)

        @pl.when(pl.program_id(0) == 0)
        def _():
            l_ref[...] = jnp.zeros_like(l_ref)
        l_ref[...] += jnp.sum(e * e, axis=0, keepdims=True)

    row = _bs((tm, D), lambda i: (i, 0))
    return pl.pallas_call(
        body, grid=(s_len // tm,), in_specs=[row, row], out_specs=[row, _bs((1, D), lambda i: (0, 0))],
        out_shape=[jax.ShapeDtypeStruct((s_len, D), F32), jax.ShapeDtypeStruct((1, D), F32)],
        name="loss", compiler_params=_cparams(1))(y, target)


def _adamw_call(w, g, m, v, name):
    rows, cols = w.shape
    tr = min(_row_tile(rows), 128)

    def body(w_ref, g_ref, m_ref, v_ref, d_ref, nm_ref, nv_ref):
        gv = g_ref[...]
        m2 = ADAM_B1 * m_ref[...] + (1.0 - ADAM_B1) * gv
        v2 = ADAM_B2 * v_ref[...] + (1.0 - ADAM_B2) * (gv * gv)
        m_hat = m2 / (1.0 - ADAM_B1 ** ADAM_STEP)
        v_hat = v2 / (1.0 - ADAM_B2 ** ADAM_STEP)
        d_ref[...] = -ADAM_LR * (m_hat / (jnp.sqrt(v_hat) + ADAM_EPS) + ADAM_WD * w_ref[...])
        nm_ref[...] = m2
        nv_ref[...] = v2

    blk = _bs((tr, cols), lambda i: (i, 0))
    return pl.pallas_call(
        body, grid=(rows // tr,), in_specs=[blk] * 4, out_specs=[blk] * 3,
        out_shape=[jax.ShapeDtypeStruct((rows, cols), F32)] * 3, name=name, compiler_params=_cparams(1))(w, g, m, v)


def _adamw_layer_call(layer, ws, gs, ms, vs, prev, after, name):
    n, steps = len(ws), 8
    after = list(after)
    n_prev = 4 * n if prev is not None else 0

    def body(*refs):
        outs = refs[len(refs) - 4 * n:]
        for t in range(n):
            w_ref, g_ref, m_ref, v_ref = refs[t], refs[n + t], refs[2 * n + t], refs[3 * n + t]
            g_out, d_out, m_out, v_out = outs[4 * t:4 * t + 4]
            gv = g_ref[...]
            m2 = ADAM_B1 * m_ref[0] + (1.0 - ADAM_B1) * gv
            v2 = ADAM_B2 * v_ref[0] + (1.0 - ADAM_B2) * (gv * gv)
            m_hat = m2 / (1.0 - ADAM_B1 ** ADAM_STEP)
            v_hat = v2 / (1.0 - ADAM_B2 ** ADAM_STEP)
            g_out[0] = gv
            d_out[0] = -ADAM_LR * (m_hat / (jnp.sqrt(v_hat) + ADAM_EPS) + ADAM_WD * w_ref[0])
            m_out[0] = m2
            v_out[0] = v2

    def lay(a):
        return _bs((1, a.shape[1] // steps, a.shape[2]), lambda i: (layer, i, 0))

    in_specs = ([lay(a) for a in ws] + [_bs((g.shape[0] // steps, g.shape[1]), lambda i: (i, 0)) for g in gs]
                + [lay(a) for a in ms] + [lay(a) for a in vs] + [_ANY] * (n_prev + len(after)))
    return pl.pallas_call(
        body, grid=(steps,), in_specs=in_specs, out_specs=[lay(ws[t]) for t in range(n) for _ in range(4)],
        out_shape=[jax.ShapeDtypeStruct(ws[t].shape, F32) for t in range(n) for _ in range(4)],
        input_output_aliases={4 * n + q: q for q in range(n_prev)}, name=name, compiler_params=_cparams(1),
    )(*ws, *gs, *ms, *vs, *(prev if prev is not None else []), *after)


def _row_tile(rows):
    for cand in (512, 256, 128, 64, 32, 16, 8):
        if rows % cand == 0 and rows > cand:
            return cand
    return rows


def _pair_sum_call(grads, from_sibling, core, name):
    n = len(grads)

    def body(core_ref, *refs):
        for t in range(n):
            refs[2 * n + t][...] = (refs[t][...] + refs[n + t][...]).astype(MM)

    half = lambda g: (1, g.shape[1] // 2, g.shape[2])
    grid_spec = pltpu.PrefetchScalarGridSpec(
        num_scalar_prefetch=1, grid=(N_CHIPS,),
        in_specs=[pl.BlockSpec(half(g), lambda j, core_ref: (j, core_ref[0], 0)) for g in grads]
        + [pl.BlockSpec(half(g), lambda j, core_ref: (j, 0, 0)) for g in grads],
        out_specs=[pl.BlockSpec(half(g), lambda j, core_ref: (j, 0, 0)) for g in grads])
    return pl.pallas_call(
        body, grid_spec=grid_spec, out_shape=[jax.ShapeDtypeStruct((N_CHIPS,) + half(g)[1:], MM) for g in grads], name=name,
        compiler_params=_cparams(1))(core, *grads, *from_sibling)


def _owner_sum_call(chip_sums, from_chips, chip_core, name):
    n = len(chip_sums)
    steps = 4

    def body(ids_ref, *refs):
        for t in range(n):
            a, b = refs[t], refs[n + t]
            refs[2 * n + t][...] = ((a[0].astype(F32) + b[0].astype(F32)) + b[1].astype(F32)) + b[2].astype(F32)

    tile = lambda p: (p.shape[1] // steps, p.shape[2])
    grid_spec = pltpu.PrefetchScalarGridSpec(
        num_scalar_prefetch=1, grid=(steps,),
        in_specs=[pl.BlockSpec((1,) + tile(p), lambda i, ids_ref: (ids_ref[0], i, 0)) for p in chip_sums]
        + [pl.BlockSpec((3,) + tile(p), lambda i, ids_ref: (0, i, 0)) for p in chip_sums],
        out_specs=[pl.BlockSpec(tile(p), lambda i, ids_ref: (ids_ref[1] * steps + i, 0)) for p in chip_sums])
    return pl.pallas_call(
        body, grid_spec=grid_spec, out_shape=[jax.ShapeDtypeStruct((2 * p.shape[1], p.shape[2]), F32) for p in chip_sums],
        name=name, compiler_params=_cparams(1))(chip_core, *chip_sums, *from_chips)


def _sum8_call(parts):
    n, rows, cols = parts.shape
    tr = _row_tile(rows)

    def body(p_ref, o_ref):
        acc = p_ref[0]
        for k in range(1, n):
            acc = acc + p_ref[k]
        o_ref[...] = acc

    return pl.pallas_call(
        body, grid=(rows // tr,), in_specs=[_bs((n, tr, cols), lambda i: (0, i, 0))], out_specs=_bs((tr, cols), lambda i: (i, 0)),
        out_shape=jax.ShapeDtypeStruct((rows, cols), F32), name="sum_small_grads", compiler_params=_cparams(1))(parts)


_ANY = pl.BlockSpec(memory_space=pl.ANY)


def _all_gather8(blk, name):
    rows, cols = blk.shape

    def body(x_ref, out_ref, send_sems, recv_sems, local_sem):
        x, y, c = lax.axis_index("x"), lax.axis_index("y"), lax.axis_index("c")
        me, sibling = (x, y, c), (x, y, 1 - c)
        chips = [(1 - x, y), (x, 1 - y), (1 - x, 1 - y)]

        def slot(px, py, pc):
            return out_ref.at[4 * px + 2 * py + pc]

        def copy(k, block, to, src=None):
            return pltpu.make_async_remote_copy(
                src_ref=slot(*block) if src is None else src, dst_ref=slot(*block),
                send_sem=send_sems.at[k], recv_sem=recv_sems.at[k], device_id=to, device_id_type=MESH_ID)

        mine = pltpu.make_async_copy(x_ref, slot(*me), local_sem)
        mine.start()
        first = [copy(0, me, sibling, src=x_ref)]
        first += [copy(1 + j, me, (*chip, c), src=x_ref) for j, chip in enumerate(chips)]
        for cp in first:
            cp.start()
        passed = [copy(4 + j, (*chip, c), sibling) for j, chip in enumerate(chips)]
        for j, chip in enumerate(chips):
            copy(1 + j, (*chip, c), me).wait_recv()
            passed[j].start()
        copy(0, sibling, me).wait_recv()
        for j, chip in enumerate(chips):
            copy(4 + j, (*chip, 1 - c), me).wait_recv()
        for cp in first + passed:
            cp.wait_send()
        mine.wait()

    return pl.pallas_call(
        body, out_shape=jax.ShapeDtypeStruct((8, rows, cols), blk.dtype), in_specs=[_ANY], out_specs=_ANY,
        scratch_shapes=[pltpu.SemaphoreType.DMA((7,)), pltpu.SemaphoreType.DMA((7,)), pltpu.SemaphoreType.DMA],
        name=name)(blk)


def _half_rows(ref, lead, half, which):
    rows = pl.ds(pl.multiple_of(half * which, half), half)
    return ref.at[rows] if lead is None else ref.at[lead, rows]


def _gather_layer_call(layer, shards, name):
    n = len(shards)
    half = [s.shape[1] // 2 for s in shards]

    def body(*refs):
        srcs, outs = refs[:n], refs[n:2 * n]
        send_sems, recv_sems, local_sems = refs[2 * n:]
        x, y, c = lax.axis_index("x"), lax.axis_index("y"), lax.axis_index("c")
        sibling = (x, y, 1 - c)
        chips = [(1 - x, y), (x, 1 - y), (1 - x, 1 - y)]

        def slot(t, px, py, pc):
            return _half_rows(outs[t], 2 * px + py, half[t], pc)

        def copy(t, k, block, to, src=None):
            return pltpu.make_async_remote_copy(
                src_ref=slot(t, *block) if src is None else src, dst_ref=slot(t, *block),
                send_sem=send_sems.at[7 * t + k], recv_sem=recv_sems.at[7 * t + k], device_id=to, device_id_type=MESH_ID)

        mine = [_half_rows(srcs[t], layer, half[t], c) for t in range(n)]
        local = [pltpu.make_async_copy(mine[t], slot(t, x, y, c), local_sems.at[t]) for t in range(n)]
        for cp in local:
            cp.start()
        first = []
        for t in range(n):
            first.append(copy(t, 0, (x, y, c), sibling, src=mine[t]))
            first += [copy(t, 1 + j, (x, y, c), (*chip, c), src=mine[t]) for j, chip in enumerate(chips)]
        for cp in first:
            cp.start()
        passed = []
        for j, chip in enumerate(chips):
            for t in range(n):
                copy(t, 1 + j, (*chip, c), (x, y, c)).wait_recv()
                passed.append(copy(t, 4 + j, (*chip, c), sibling))
                passed[-1].start()
        for t in range(n):
            copy(t, 0, (x, y, 1 - c), (x, y, c)).wait_recv()
            for j, chip in enumerate(chips):
                copy(t, 4 + j, (*chip, 1 - c), (x, y, c)).wait_recv()
        for cp in first + passed:
            cp.wait_send()
        for cp in local:
            cp.wait()

    return pl.pallas_call(
        body, out_shape=[jax.ShapeDtypeStruct((N_CHIPS,) + s.shape[1:], s.dtype) for s in shards],
        in_specs=[_ANY] * n, out_specs=[_ANY] * n,
        scratch_shapes=[pltpu.SemaphoreType.DMA((7 * n,)), pltpu.SemaphoreType.DMA((7 * n,)), pltpu.SemaphoreType.DMA((n,))],
        name=name)(*shards)


_HBM = pl.BlockSpec(memory_space=pltpu.HBM)
_SEM = pl.BlockSpec(memory_space=pltpu.SEMAPHORE)
_ORDERED_EFFECT = pltpu.CompilerParams(has_side_effects=pltpu.SideEffectType.DATAFLOW_SIDE_EFFECTING)


_VMEM = pl.BlockSpec(memory_space=pltpu.VMEM)
_TOKEN = jax.ShapeDtypeStruct((8, LANES), F32)


def _in_hbm(a):
    return pltpu.with_memory_space_constraint(a, pltpu.HBM)


def _tie(small, token):
    return small + token[0:1, 0:1].reshape((1,) * small.ndim)


def _pair_exchange_start_call(grads, name):
    n = len(grads)
    half = [g.shape[1] // 2 for g in grads]

    def body(*refs):
        srcs, outs = refs[:n], refs[n:2 * n]
        send_sems, recv_sems, token = refs[2 * n:]
        x, y, c = lax.axis_index("x"), lax.axis_index("y"), lax.axis_index("c")
        for t in range(n):
            pltpu.make_async_remote_copy(
                src_ref=srcs[t].at[:, pl.ds(pl.multiple_of(half[t] * (1 - c), half[t]), half[t])], dst_ref=outs[t],
                send_sem=send_sems.at[t], recv_sem=recv_sems.at[t], device_id=(x, y, 1 - c), device_id_type=MESH_ID).start()
        token[...] = jnp.zeros_like(token)

    dma = pltpu.SemaphoreType.DMA
    return pl.pallas_call(
        body, out_shape=[pltpu.HBM((g.shape[0], g.shape[1] // 2, g.shape[2]), g.dtype) for g in grads] + [dma((n,)), dma((n,)), _TOKEN],
        in_specs=[_HBM] * n, out_specs=[_HBM] * n + [_SEM, _SEM, _VMEM], name=name, compiler_params=_ORDERED_EFFECT,
    )(*[_in_hbm(g) for g in grads])


def _pair_exchange_finish_call(grads, bufs, send_sems, recv_sems, after, name):
    n = len(grads)
    after = list(after)
    half = [g.shape[1] // 2 for g in grads]

    def body(*refs):
        srcs, ins, send_ref, recv_ref = refs[:n], refs[n:2 * n], refs[2 * n], refs[2 * n + 1]
        x, y, c = lax.axis_index("x"), lax.axis_index("y"), lax.axis_index("c")
        for t in range(n):
            pltpu.make_async_remote_copy(
                src_ref=srcs[t].at[:, pl.ds(pl.multiple_of(half[t] * (1 - c), half[t]), half[t])], dst_ref=ins[t],
                send_sem=send_ref.at[t], recv_sem=recv_ref.at[t], device_id=(x, y, 1 - c), device_id_type=MESH_ID).wait()

    return pl.pallas_call(
        body, out_shape=[pltpu.HBM(b.shape, b.dtype) for b in bufs],
        in_specs=[_HBM] * (2 * n) + [_SEM, _SEM] + [_ANY] * len(after), out_specs=[_HBM] * n,
        input_output_aliases={n + t: t for t in range(n)}, name=name, compiler_params=_ORDERED_EFFECT,
    )(*[_in_hbm(g) for g in grads], *bufs, send_sems, recv_sems, *after)


def _chip_scatter_start_call(chip_sums, name):
    n = len(chip_sums)

    def body(*refs):
        srcs, outs = refs[:n], refs[n:2 * n]
        send_sems, recv_sems, token = refs[2 * n:]
        x, y, c = lax.axis_index("x"), lax.axis_index("y"), lax.axis_index("c")
        chips = [(1 - x, y), (x, 1 - y), (1 - x, 1 - y)]
        for k, (cx, cy) in enumerate(chips):
            for t in range(n):
                pltpu.make_async_remote_copy(
                    src_ref=srcs[t].at[2 * cx + cy], dst_ref=outs[t].at[k], send_sem=send_sems.at[3 * t + k],
                    recv_sem=recv_sems.at[3 * t + k], device_id=(cx, cy, c), device_id_type=MESH_ID).start()
        token[...] = jnp.zeros_like(token)

    dma = pltpu.SemaphoreType.DMA
    return pl.pallas_call(
        body, out_shape=[pltpu.HBM((3,) + p.shape[1:], p.dtype) for p in chip_sums] + [dma((3 * n,)), dma((3 * n,)), _TOKEN],
        in_specs=[_HBM] * n, out_specs=[_HBM] * n + [_SEM, _SEM, _VMEM], name=name, compiler_params=_ORDERED_EFFECT,
    )(*[_in_hbm(p) for p in chip_sums])


def _chip_scatter_finish_call(chip_sums, bufs, send_sems, recv_sems, after, name):
    n = len(chip_sums)
    after = list(after)

    def body(*refs):
        srcs, ins, send_ref, recv_ref = refs[:n], refs[n:2 * n], refs[2 * n], refs[2 * n + 1]
        x, y, c = lax.axis_index("x"), lax.axis_index("y"), lax.axis_index("c")
        chips = [(1 - x, y), (x, 1 - y), (1 - x, 1 - y)]
        for k, (cx, cy) in enumerate(chips):
            for t in range(n):
                pltpu.make_async_remote_copy(
                    src_ref=srcs[t].at[2 * cx + cy], dst_ref=ins[t].at[k], send_sem=send_ref.at[3 * t + k],
                    recv_sem=recv_ref.at[3 * t + k], device_id=(cx, cy, c), device_id_type=MESH_ID).wait()

    return pl.pallas_call(
        body, out_shape=[pltpu.HBM(b.shape, b.dtype) for b in bufs],
        in_specs=[_HBM] * (2 * n) + [_SEM, _SEM] + [_ANY] * len(after), out_specs=[_HBM] * n,
        input_output_aliases={n + t: t for t in range(n)}, name=name, compiler_params=_ORDERED_EFFECT,
    )(*[_in_hbm(p) for p in chip_sums], *bufs, send_sems, recv_sems, *after)


def _place_own_call(layer, shards, chip_core, name):
    n = len(shards)

    def body(ids_ref, *refs):
        for t in range(n):
            refs[n + t][...] = refs[t][...]

    def blk(s):
        return (1, s.shape[1] // 2) + s.shape[2:]

    def imap_in(s):
        pad = (0,) * (s.ndim - 2)
        return lambda i, ids_ref: (layer, ids_ref[1]) + pad

    def imap_out(s):
        pad = (0,) * (s.ndim - 2)
        return lambda i, ids_ref: (ids_ref[0], ids_ref[1]) + pad

    grid_spec = pltpu.PrefetchScalarGridSpec(
        num_scalar_prefetch=1, grid=(1,), in_specs=[pl.BlockSpec(blk(s), imap_in(s)) for s in shards],
        out_specs=[pl.BlockSpec(blk(s), imap_out(s)) for s in shards])
    return pl.pallas_call(
        body, grid_spec=grid_spec, out_shape=[jax.ShapeDtypeStruct((N_CHIPS,) + s.shape[1:], s.dtype) for s in shards],
        name=name, compiler_params=_cparams(1))(chip_core, *shards)


def _gather_start_call(layer, shards, bufs, after, name):
    n = len(shards)
    half = [s.shape[1] // 2 for s in shards]

    def body(*refs):
        srcs, outs = refs[:n], refs[2 * n + 1:3 * n + 1]
        send_sems, recv_sib, recv_ici, token = refs[3 * n + 1:]
        x, y, c = lax.axis_index("x"), lax.axis_index("y"), lax.axis_index("c")
        chips = [(1 - x, y), (x, 1 - y), (1 - x, 1 - y)]
        for t in range(n):
            mine = _half_rows(srcs[t], layer, half[t], c)
            dst = _half_rows(outs[t], 2 * x + y, half[t], c)
            pltpu.make_async_remote_copy(src_ref=mine, dst_ref=dst, send_sem=send_sems.at[4 * t], recv_sem=recv_sib.at[t],
                                         device_id=(x, y, 1 - c), device_id_type=MESH_ID).start()
            for j, chip in enumerate(chips):
                pltpu.make_async_remote_copy(src_ref=mine, dst_ref=dst, send_sem=send_sems.at[4 * t + 1 + j],
                                             recv_sem=recv_ici.at[3 * t + j], device_id=(*chip, c), device_id_type=MESH_ID).start()
        token[...] = jnp.zeros_like(token)

    dma = pltpu.SemaphoreType.DMA
    return pl.pallas_call(
        body, out_shape=[pltpu.HBM(b.shape, b.dtype) for b in bufs] + [dma((4 * n,)), dma((n,)), dma((3 * n,)), _TOKEN],
        in_specs=[_HBM] * (2 * n) + [_ANY], out_specs=[_HBM] * n + [_SEM] * 3 + [_VMEM],
        input_output_aliases={n + t: t for t in range(n)}, name=name, compiler_params=_ORDERED_EFFECT,
    )(*[_in_hbm(s) for s in shards], *[_in_hbm(b) for b in bufs], after)


def _gather_forward_call(bufs, recv_ici, after, name):
    n = len(bufs)
    half = [b.shape[1] // 2 for b in bufs]

    def body(*refs):
        ins, recv_ici_ref = refs[:n], refs[n]
        outs = refs[n + 2:2 * n + 2]
        send_fwd, recv_fwd, token = refs[2 * n + 2:]
        x, y, c = lax.axis_index("x"), lax.axis_index("y"), lax.axis_index("c")
        chips = [(1 - x, y), (x, 1 - y), (1 - x, 1 - y)]
        for j, (cx, cy) in enumerate(chips):
            for t in range(n):
                landed = _half_rows(ins[t], 2 * cx + cy, half[t], c)
                dst = _half_rows(outs[t], 2 * cx + cy, half[t], c)
                pltpu.make_async_remote_copy(src_ref=landed, dst_ref=landed, send_sem=send_fwd.at[3 * t + j],
                                             recv_sem=recv_ici_ref.at[3 * t + j], device_id=(cx, cy, c),
                                             device_id_type=MESH_ID).wait_recv()
                pltpu.make_async_remote_copy(src_ref=landed, dst_ref=dst, send_sem=send_fwd.at[3 * t + j],
                                             recv_sem=recv_fwd.at[3 * t + j], device_id=(x, y, 1 - c),
                                             device_id_type=MESH_ID).start()
        token[...] = jnp.zeros_like(token)

    dma = pltpu.SemaphoreType.DMA
    return pl.pallas_call(
        body, out_shape=[pltpu.HBM(b.shape, b.dtype) for b in bufs] + [dma((3 * n,)), dma((3 * n,)), _TOKEN],
        in_specs=[_HBM] * n + [_SEM, _ANY], out_specs=[_HBM] * n + [_SEM] * 2 + [_VMEM],
        input_output_aliases={t: t for t in range(n)}, name=name, compiler_params=_ORDERED_EFFECT,
    )(*bufs, recv_ici, after)


def _gather_finish_call(layer, shards, bufs, send_sems, recv_sib, send_fwd, recv_fwd, after, name):
    n = len(bufs)
    half = [b.shape[1] // 2 for b in bufs]

    def body(*refs):
        srcs, ins = refs[:n], refs[n:2 * n]
        send_ref, recv_sib_ref, send_fwd_ref, recv_fwd_ref = refs[2 * n:2 * n + 4]
        x, y, c = lax.axis_index("x"), lax.axis_index("y"), lax.axis_index("c")
        chips = [(1 - x, y), (x, 1 - y), (1 - x, 1 - y)]
        sibling = (x, y, 1 - c)
        for t in range(n):
            mine = _half_rows(srcs[t], layer, half[t], c)
            for k in range(4):
                pltpu.make_async_remote_copy(src_ref=mine, dst_ref=mine, send_sem=send_ref.at[4 * t + k],
                                             recv_sem=recv_sib_ref.at[t], device_id=sibling, device_id_type=MESH_ID).wait_send()
            from_sibling = _half_rows(ins[t], 2 * x + y, half[t], 1 - c)
            pltpu.make_async_remote_copy(src_ref=from_sibling, dst_ref=from_sibling, send_sem=send_ref.at[4 * t],
                                         recv_sem=recv_sib_ref.at[t], device_id=sibling, device_id_type=MESH_ID).wait_recv()
            for j, (cx, cy) in enumerate(chips):
                sent = _half_rows(ins[t], 2 * cx + cy, half[t], c)
                passed = _half_rows(ins[t], 2 * cx + cy, half[t], 1 - c)
                pltpu.make_async_remote_copy(src_ref=sent, dst_ref=passed, send_sem=send_fwd_ref.at[3 * t + j],
                                             recv_sem=recv_fwd_ref.at[3 * t + j], device_id=sibling, device_id_type=MESH_ID).wait()

    return pl.pallas_call(
        body, out_shape=[pltpu.HBM(b.shape, b.dtype) for b in bufs],
        in_specs=[_HBM] * (2 * n) + [_SEM] * 4 + [_ANY], out_specs=[_HBM] * n,
        input_output_aliases={n + t: t for t in range(n)}, name=name, compiler_params=_ORDERED_EFFECT,
    )(*[_in_hbm(s) for s in shards], *bufs, send_sems, recv_sib, send_fwd, recv_fwd, after)


def _pair_exchange_call(grads, name):
    n = len(grads)
    half = [g.shape[1] // 2 for g in grads]

    def body(*refs):
        srcs, outs, send_sems, recv_sems = refs[:n], refs[n:2 * n], refs[2 * n], refs[2 * n + 1]
        x, y, c = lax.axis_index("x"), lax.axis_index("y"), lax.axis_index("c")
        copies = [pltpu.make_async_remote_copy(
            src_ref=srcs[t].at[:, pl.ds(pl.multiple_of(half[t] * (1 - c), half[t]), half[t])], dst_ref=outs[t],
            send_sem=send_sems.at[t], recv_sem=recv_sems.at[t], device_id=(x, y, 1 - c), device_id_type=MESH_ID) for t in range(n)]
        for cp in copies:
            cp.start()
        for cp in copies:
            cp.wait()

    return pl.pallas_call(
        body, out_shape=[jax.ShapeDtypeStruct((g.shape[0], g.shape[1] // 2, g.shape[2]), g.dtype) for g in grads],
        in_specs=[_ANY] * n, out_specs=[_ANY] * n,
        scratch_shapes=[pltpu.SemaphoreType.DMA((n,)), pltpu.SemaphoreType.DMA((n,))], name=name)(*grads)


def _chip_scatter_call(chip_sums, name):
    n = len(chip_sums)

    def body(*refs):
        srcs, outs, send_sems, recv_sems = refs[:n], refs[n:2 * n], refs[2 * n], refs[2 * n + 1]
        x, y, c = lax.axis_index("x"), lax.axis_index("y"), lax.axis_index("c")
        chips = [(1 - x, y), (x, 1 - y), (1 - x, 1 - y)]
        copies = [pltpu.make_async_remote_copy(
            src_ref=srcs[t].at[2 * cx + cy], dst_ref=outs[t].at[k], send_sem=send_sems.at[3 * t + k],
            recv_sem=recv_sems.at[3 * t + k], device_id=(cx, cy, c), device_id_type=MESH_ID)
            for k, (cx, cy) in enumerate(chips) for t in range(n)]
        for cp in copies:
            cp.start()
        for cp in copies:
            cp.wait()

    return pl.pallas_call(
        body, out_shape=[jax.ShapeDtypeStruct((3,) + p.shape[1:], p.dtype) for p in chip_sums],
        in_specs=[_ANY] * n, out_specs=[_ANY] * n,
        scratch_shapes=[pltpu.SemaphoreType.DMA((3 * n,)), pltpu.SemaphoreType.DMA((3 * n,))], name=name)(*chip_sums)


def _pair_gather_call(bufs, name):
    n = len(bufs)
    half = [b.shape[0] // 2 for b in bufs]

    def body(*refs):
        srcs, outs, send_sems, recv_sems = refs[:n], refs[n:2 * n], refs[2 * n], refs[2 * n + 1]
        x, y, c = lax.axis_index("x"), lax.axis_index("y"), lax.axis_index("c")
        for t in range(n):
            pltpu.make_async_remote_copy(
                src_ref=_half_rows(srcs[t], None, half[t], c), dst_ref=_half_rows(outs[t], None, half[t], c),
                send_sem=send_sems.at[t], recv_sem=recv_sems.at[t], device_id=(x, y, 1 - c), device_id_type=MESH_ID).start()
        for t in range(n):
            pltpu.make_async_remote_copy(
                src_ref=_half_rows(srcs[t], None, half[t], c), dst_ref=_half_rows(outs[t], None, half[t], 1 - c),
                send_sem=send_sems.at[t], recv_sem=recv_sems.at[t], device_id=(x, y, 1 - c), device_id_type=MESH_ID).wait()

    return pl.pallas_call(
        body, out_shape=[jax.ShapeDtypeStruct(b.shape, b.dtype) for b in bufs], in_specs=[_ANY] * n, out_specs=[_ANY] * n,
        input_output_aliases={t: t for t in range(n)},
        scratch_shapes=[pltpu.SemaphoreType.DMA((n,)), pltpu.SemaphoreType.DMA((n,))], name=name)(*bufs)


def _pack_rows(flats, dtype, row_multiple):
    flat = jnp.concatenate([f.reshape(-1).astype(dtype) for f in flats])
    n = flat.shape[0]
    rows = -(-n // PACK_W)
    rows = -(-rows // row_multiple) * row_multiple
    return jnp.pad(flat, (0, rows * PACK_W - n)).reshape(rows, PACK_W)


def _unpack(flat, shapes):
    out, off = [], 0
    for shp in shapes:
        n = math.prod(shp)
        out.append(flat[off:off + n].reshape(shp))
        off += n
    return out


def _f32_as_mm_bits(a):
    return lax.bitcast_convert_type(a, jnp.bfloat16).reshape(-1)


def _mm_bits_as_f32(flat, shape):
    return lax.bitcast_convert_type(flat.reshape(-1, 2), F32).reshape(shape)


_W_IN_SEGMENTS = ((R_ML, R_END, OFF_ML), (R_SG, R_ML, OFF_SG), (R_CV, R_SGI, OFF_CV), (R_SGI, R_MQ, OFF_SGI), (R_MQ, R_SG, OFF_MQ),
                  (R_CQ, R_CKV, OFF_CQ), (R_CKV, R_KR, OFF_CKV), (R_KR, R_CV, OFF_KR + NOPE))
W_IN_SHARD = R_END // N_CHIPS


def _realign_call(wg):
    tr = 128

    def body(w_ref, o_ref):
        pieces, pos = [], 0
        for r0, r1, a0 in _W_IN_SEGMENTS:
            if a0 > pos:
                pieces.append(jnp.zeros((tr, a0 - pos), o_ref.dtype))
            while r0 < r1:
                j = r0 // W_IN_SHARD
                hi = min(r1, (j + 1) * W_IN_SHARD)
                pieces.append(w_ref[j, :, r0 - j * W_IN_SHARD:hi - j * W_IN_SHARD])
                a0, r0 = a0 + hi - r0, hi
            pos = a0
        pieces.append(jnp.zeros((tr, NP - pos), o_ref.dtype))
        o_ref[...] = jnp.concatenate(pieces, axis=1)

    return pl.pallas_call(
        body, grid=(D // tr,), in_specs=[_bs((N_CHIPS, tr, W_IN_SHARD), lambda i: (0, i, 0))],
        out_specs=_bs((tr, NP), lambda i: (i, 0)), out_shape=jax.ShapeDtypeStruct((D, NP), wg.dtype),
        name="w_in_realign", compiler_params=_cparams(1))(wg)


def _unalign_call(dw):
    tr = 128
    by_ref = sorted(_W_IN_SEGMENTS)

    def body(dw_ref, o_ref):
        for j in range(N_CHIPS):
            lo_j, hi_j = j * W_IN_SHARD, (j + 1) * W_IN_SHARD
            pieces = []
            for r0, r1, a0 in by_ref:
                lo, hi = max(r0, lo_j), min(r1, hi_j)
                if lo < hi:
                    pieces.append(dw_ref[:, a0 + lo - r0:a0 + hi - r0])
            o_ref[j] = jnp.concatenate(pieces, axis=1)

    return pl.pallas_call(
        body, grid=(D // tr,), in_specs=[_bs((tr, NP), lambda i: (i, 0))],
        out_specs=_bs((N_CHIPS, tr, W_IN_SHARD), lambda i: (0, i, 0)),
        out_shape=jax.ShapeDtypeStruct((N_CHIPS, D, W_IN_SHARD), dw.dtype), name="w_in_unalign", compiler_params=_cparams(1))(dw)


def _w_in_to_aligned(w):
    z = lambda n: jnp.zeros((w.shape[0], n), w.dtype)
    return jnp.concatenate([w[:, R_ML:R_END], w[:, R_SG:R_ML], w[:, R_CV:R_SGI], w[:, R_SGI:R_MQ], w[:, R_MQ:R_SG],
                            w[:, R_CQ:R_CKV], w[:, R_CKV:R_KR], z(NOPE), w[:, R_KR:R_CV], z(LANES - QKH)], axis=1)


def _w_in_from_aligned(wa):
    return jnp.concatenate([wa[:, OFF_CQ:OFF_CKV], wa[:, OFF_CKV:OFF_KR], wa[:, OFF_KR + NOPE:OFF_KR + QKH], wa[:, OFF_CV:OFF_SGI],
                            wa[:, OFF_SGI:OFF_MQ], wa[:, OFF_MQ:OFF_CQ], wa[:, OFF_SG:OFF_CV], wa[:, OFF_ML:OFF_SG]], axis=1)


def _wuq_to_heads(w):
    w3 = w.reshape(QL, H, QKH)
    w3 = jnp.pad(w3, ((0, 0), (0, 0), (0, LANES - QKH)))
    return jnp.transpose(w3, (1, 0, 2))


def _wuq_from_heads(wh):
    return jnp.transpose(wh[:, :, :QKH], (1, 0, 2)).reshape(QL, H * QKH)


def _wukv_to_heads(w):
    w3 = w.reshape(KVL, H, NOPE + VH)
    wkn = jnp.transpose(jnp.pad(w3[:, :, :NOPE], ((0, 0), (0, 0), (0, LANES - NOPE))), (1, 0, 2))
    wv3 = w3[:, :, NOPE:]
    z = jnp.zeros((KVL, VH), w.dtype)
    cols = []
    for h in range(H):
        cols += [wv3[:, h], z] if h % 2 == 0 else [z, wv3[:, h]]
    return wkn, jnp.concatenate(cols, axis=1)


def _wukv_from_heads(wkn, wv):
    kn = jnp.transpose(wkn[:, :, :NOPE], (1, 0, 2))
    vs = jnp.stack([wv[:, LANES * h + VH * (h % 2):LANES * h + VH * (h % 2) + VH] for h in range(H)], axis=1)
    return jnp.concatenate([kn, vs], axis=2).reshape(KVL, H * (NOPE + VH))


def _layer_fwd(x, mem, tabs, p):
    proj, h = _proj_call(x, p["norm_g"], p["w_in"])
    q, k, v = _mla_prep_call(proj, tabs, p["cq_g"], p["ckv_g"], p["qg"], p["kg"], p["wuq"], p["wkn"], p["wv"])
    ya = _attn_call(q, k, v, proj)
    bm = p["bm"]
    if p.get("after_attn") is not None:
        bm = _tie(bm, p["after_attn"](ya))
    yb = _conv_call(proj, p["conv_w"], p["conv_b"])
    yc = _sg_call(proj, p["ln_g"], p["ln_b"], p["ws"], p["bs"])
    mk, mv = _memkv_call(mem, p["mem_g"], p["wm"], p["mkg"])
    yd = _mem_call(proj, mk, mv, p["mqg"])
    out = _merge_call((ya, yb, yc, yd), proj, bm, p["wb"], p["wo"], x)
    return out, dict(x=x, proj=proj, h=h, q=q, k=k, v=v, ys=(ya, yb, yc, yd), mk=mk, mv=mv)


def _layer_bwd(dout, mem, tabs, p, sv, start_after=None, after_mla=None, on_grads=None):
    proj = sv["proj"]
    bm = p["bm"] if start_after is None else _tie(p["bm"], start_after)
    dya, dyb, dyc, dyd, dml, dbm, dwb, dwo = _merge_bwd_call(sv["ys"], proj, bm, p["wb"], p["wo"], dout)
    dq, dk, dv, dsg_a = _attn_bwd_call(sv["q"], sv["k"], sv["v"], proj, dya)
    dcq, dckv, dkr, dcqg, dckvg, dqg, dkg, dwuq, dwkn, dwv = _mla_prep_bwd_call(
        proj, tabs, p["cq_g"], p["ckv_g"], p["qg"], p["kg"], p["wuq"], p["wkn"], p["wv"], dq, dk, dv)
    if after_mla is not None:
        after_mla(dcq)
    dbg, dcg, dxi, dsg_b, dcw, dcb = _conv_bwd_call(proj, p["conv_w"], p["conv_b"], dyb)
    du, dvv, dsg_c, dlg, dlb, dws, dbs = _sg_bwd_call(proj, p["ln_g"], p["ln_b"], p["ws"], p["bs"], dyc)
    dmq, dsg_d, dmk, dmv, dmqg = _mem_bwd_call(proj, sv["mk"], sv["mv"], p["mqg"], dyd)
    dmem_g, dwm, dmkg = _memkv_bwd_call(mem, p["mem_g"], p["wm"], p["mkg"], dmk, dmv)
    dproj = jnp.concatenate([dml, dsg_a, dsg_b, dsg_c, dsg_d, dbg, dcg, dxi, du, dvv, dmq, dcq, dckv, dkr], axis=1)
    dw_in = _dw_call(sv["h"], dproj)
    grads = dict(cq_norm_g=dcqg[0], ckv_norm_g=dckvg[0], mla_q_norm_g=dqg[0, :QKH], mla_k_norm_g=dkg[0, :QKH],
                 conv_w=dcw, conv_b=dcb[0], sg_ln_g=dlg[0], sg_ln_b=dlb[0], w_spatial=dws, b_spatial=dbs[:, :, 0],
                 mem_norm_g=dmem_g[0], mem_q_norm_g=dmqg[0], mem_k_norm_g=dmkg[0], b_merge=dbm,
                 w_in_aligned=dw_in, wuq_heads=dwuq, wkn_heads=dwkn, wv_heads=dwv, w_mem_kv=dwm, w_branch_chips=dwb, w_out=dwo)
    norm_g = p["norm_g"]
    if on_grads is not None:
        norm_g = _tie(norm_g, on_grads(grads))
    dx, dnorm_g = _dh_call(dproj, p["w_in"], sv["x"], norm_g, dout)
    grads["norm_g"] = dnorm_g[0]
    return dx, grads


def _chips_to_cols(a):
    return jnp.concatenate([a[j] for j in range(N_CHIPS)], axis=1)


def _cols_to_chips(a):
    cols = a.shape[1] // N_CHIPS
    return jnp.stack([a[:, cols * j:cols * (j + 1)] for j in range(N_CHIPS)])


def _layer_params(l, rep, gathered, conv_w, b_merge):
    pad_g = lambda g: jnp.pad(g, (0, LANES - QKH)).reshape(1, LANES)
    wkn, wv = _wukv_to_heads(_chips_to_cols(gathered["w_ukv"]))
    return dict(
        norm_g=rep["norm_g"][l].reshape(1, D), w_in=_realign_call(gathered["w_in"]),
        cq_g=rep["cq_norm_g"][l].reshape(1, QL), ckv_g=rep["ckv_norm_g"][l].reshape(1, KVL),
        qg=pad_g(rep["mla_q_norm_g"][l]), kg=pad_g(rep["mla_k_norm_g"][l]),
        wuq=_wuq_to_heads(_chips_to_cols(gathered["w_uq"])), wkn=wkn, wv=wv,
        conv_w=conv_w, conv_b=rep["conv_b"][l].reshape(1, CW),
        ln_g=rep["sg_ln_g"][l].reshape(1, SGW), ln_b=rep["sg_ln_b"][l].reshape(1, SGW),
        ws=rep["w_spatial"][l], bs=rep["b_spatial"][l].reshape(SGG, SGC, 1),
        mem_g=rep["mem_norm_g"][l].reshape(1, D), wm=gathered["w_mem_kv"].reshape(D, 2 * MH * MHD),
        mqg=rep["mem_q_norm_g"][l].reshape(1, MHD), mkg=rep["mem_k_norm_g"][l].reshape(1, MHD),
        bm=b_merge, wb=gathered["w_branch"], wo=gathered["w_out"].reshape(D, D))


def _forward_backward(x, mem, pos, target, params, bwd_hooks=None):
    tabs = _rope_tables(pos)
    params = list(params)
    saved = []
    act = x
    for l in range(DEPTH):
        if callable(params[l]):
            params[l] = params[l](saved[-1], act)
        act, sv = _layer_fwd(act, mem, tabs, params[l])
        saved.append(sv)
    dy, sq = _loss_call(act, target)
    grads = [None] * DEPTH
    token = None
    for l in reversed(range(DEPTH)):
        hooks = dict(bwd_hooks[l]) if bwd_hooks else {}
        after_layer = hooks.pop("after_layer", None)
        dy, grads[l] = _layer_bwd(dy, mem, tabs, params[l], saved[l], start_after=token, **hooks)
        token = after_layer(dy) if after_layer is not None else None
    return sq, dy, grads


_SHARDED_MM = ("w_in", "w_branch", "w_out", "w_mem_kv", "w_uq", "w_ukv")
_SHARDED_F32 = ("conv_w", "b_merge")
_REPLICATED = ("norm_g", "cq_norm_g", "ckv_norm_g", "mla_q_norm_g", "mla_k_norm_g", "conv_b", "sg_ln_g", "sg_ln_b",
               "w_spatial", "b_spatial", "mem_norm_g", "mem_q_norm_g", "mem_k_norm_g")
_ALL_REDUCED = _REPLICATED + _SHARDED_F32
_WEIGHTS = ("norm_g", "w_in", "cq_norm_g", "ckv_norm_g", "w_uq", "w_ukv", "mla_q_norm_g", "mla_k_norm_g", "conv_w", "conv_b",
            "sg_ln_g", "sg_ln_b", "w_spatial", "b_spatial", "mem_norm_g", "w_mem_kv", "mem_q_norm_g", "mem_k_norm_g",
            "b_merge", "w_branch", "w_out")
_BIG = ("w_in", "w_uq", "w_ukv", "w_mem_kv", "w_branch", "w_out")
_SMALL = tuple(n for n in _WEIGHTS if n not in _BIG)


def _gather_small_sharded(w):
    names = _SHARDED_F32
    packed = _pack_rows([w[n] for n in names], F32, 8)
    got = _all_gather8(packed, "gather_small_weights")
    per_chip = [_unpack(got[2 * j].reshape(-1), [w[n].shape for n in names]) for j in range(N_CHIPS)]
    return {n: jnp.concatenate([per_chip[j][t] for j in range(N_CHIPS)], axis=2) for t, n in enumerate(names)}


def _gather_layer(l, shards):
    srcs = [shards[n] for n in _SHARDED_MM]
    return dict(zip(_SHARDED_MM, _gather_layer_call(l, srcs, "gather_weights_l%d" % l)))


class _ReduceScatter:
    def __init__(self, layer):
        self.tag = "rs_l%d_" % layer

    def exchange(self, grads):
        self.tensors = [
            _unalign_call(grads["w_in_aligned"]),
            grads["w_branch_chips"].reshape(N_CHIPS, NB * BW, D // N_CHIPS),
            grads["w_out"].reshape(N_CHIPS, D // N_CHIPS, D),
            grads["w_mem_kv"].reshape(N_CHIPS, D // N_CHIPS, 2 * MH * MHD),
            _cols_to_chips(_wuq_from_heads(grads["wuq_heads"])),
            _cols_to_chips(_wukv_from_heads(grads["wkn_heads"], grads["wv_heads"])),
        ]
        n = len(self.tensors)
        out = _pair_exchange_start_call(self.tensors, self.tag + "exchange_start")
        self.ex_bufs, self.ex_send, self.ex_recv = out[:n], out[n], out[n + 1]
        return out[n + 2]

    def scatter(self, after):
        n = len(self.tensors)
        c = lax.axis_index("c")
        from_sibling = _pair_exchange_finish_call(self.tensors, self.ex_bufs, self.ex_send, self.ex_recv, after,
                                                  self.tag + "exchange_finish")
        self.chip_sums = _pair_sum_call(self.tensors, from_sibling, c.astype(jnp.int32).reshape(1), self.tag + "pair_sum")
        out = _chip_scatter_start_call(self.chip_sums, self.tag + "scatter_start")
        self.bufs, self.send_sems, self.recv_sems, self.token = out[:n], out[n], out[n + 1], out[n + 2]
        return self.token

    def finish(self, after):
        x, y, c = lax.axis_index("x"), lax.axis_index("y"), lax.axis_index("c")
        chip_core = jnp.stack([2 * x + y, c]).astype(jnp.int32)
        from_chips = _chip_scatter_finish_call(self.chip_sums, self.bufs, self.send_sems, self.recv_sems, after,
                                               self.tag + "scatter_finish")
        mine = _owner_sum_call(self.chip_sums, from_chips, chip_core, self.tag + "owner_sum")
        shard = dict(zip(_SHARDED_MM, _pair_gather_call(mine, self.tag + "pair_gather")))
        shard["w_branch"] = shard["w_branch"].reshape(NB, BW, D // N_CHIPS)
        return shard


def _all_reduce_small(g):
    packed = _pack_rows([g[n] for n in _ALL_REDUCED], F32, 64)
    got = _all_gather8(packed, "gather_small_grads")
    total = _sum8_call(got).reshape(-1)
    out = dict(zip(_ALL_REDUCED, _unpack(total, [g[n].shape for n in _ALL_REDUCED])))
    chip = 2 * lax.axis_index("x") + lax.axis_index("y")
    for n in _SHARDED_F32:
        size = out[n].shape[2] // N_CHIPS
        out[n] = lax.dynamic_slice_in_dim(out[n], chip * size, size, axis=2)
    return out


def _adamw_small(w, g, m, v, token):
    delta, new_m, new_v = {}, {}, {}
    shapes = [w[n].shape for n in _SMALL]
    pk = lambda t: _pack_rows([t[n] for n in _SMALL], F32, 64)
    d, nm, nv = _adamw_call(pk(w), _tie(pk(g), token), pk(m), pk(v), "adamw_small")
    for out, packed in ((delta, d), (new_m, nm), (new_v, nv)):
        out.update(zip(_SMALL, _unpack(packed.reshape(-1), shapes)))
    return delta, new_m, new_v


def kernel(x, mem, positions, norm_g, w_in, cq_norm_g, ckv_norm_g, w_uq, w_ukv, mla_q_norm_g, mla_k_norm_g, conv_w, conv_b, sg_ln_g, sg_ln_b, w_spatial, b_spatial, mem_norm_g, w_mem_kv, mem_q_norm_g, mem_k_norm_g, b_merge, w_branch, w_out, loss_target, m_norm_g, m_w_in, m_cq_norm_g, m_ckv_norm_g, m_w_uq, m_w_ukv, m_mla_q_norm_g, m_mla_k_norm_g, m_conv_w, m_conv_b, m_sg_ln_g, m_sg_ln_b, m_w_spatial, m_b_spatial, m_mem_norm_g, m_w_mem_kv, m_mem_q_norm_g, m_mem_k_norm_g, m_b_merge, m_w_branch, m_w_out, v_norm_g, v_w_in, v_cq_norm_g, v_ckv_norm_g, v_w_uq, v_w_ukv, v_mla_q_norm_g, v_mla_k_norm_g, v_conv_w, v_conv_b, v_sg_ln_g, v_sg_ln_b, v_w_spatial, v_b_spatial, v_mem_norm_g, v_w_mem_kv, v_mem_q_norm_g, v_mem_k_norm_g, v_b_merge, v_w_branch, v_w_out):
    w = dict(norm_g=norm_g, w_in=w_in, cq_norm_g=cq_norm_g, ckv_norm_g=ckv_norm_g, w_uq=w_uq, w_ukv=w_ukv,
             mla_q_norm_g=mla_q_norm_g, mla_k_norm_g=mla_k_norm_g, conv_w=conv_w, conv_b=conv_b, sg_ln_g=sg_ln_g,
             sg_ln_b=sg_ln_b, w_spatial=w_spatial, b_spatial=b_spatial, mem_norm_g=mem_norm_g, w_mem_kv=w_mem_kv,
             mem_q_norm_g=mem_q_norm_g, mem_k_norm_g=mem_k_norm_g, b_merge=b_merge, w_branch=w_branch, w_out=w_out)
    m = dict(norm_g=m_norm_g, w_in=m_w_in, cq_norm_g=m_cq_norm_g, ckv_norm_g=m_ckv_norm_g, w_uq=m_w_uq, w_ukv=m_w_ukv,
             mla_q_norm_g=m_mla_q_norm_g, mla_k_norm_g=m_mla_k_norm_g, conv_w=m_conv_w, conv_b=m_conv_b, sg_ln_g=m_sg_ln_g,
             sg_ln_b=m_sg_ln_b, w_spatial=m_w_spatial, b_spatial=m_b_spatial, mem_norm_g=m_mem_norm_g, w_mem_kv=m_w_mem_kv,
             mem_q_norm_g=m_mem_q_norm_g, mem_k_norm_g=m_mem_k_norm_g, b_merge=m_b_merge, w_branch=m_w_branch, w_out=m_w_out)
    v = dict(norm_g=v_norm_g, w_in=v_w_in, cq_norm_g=v_cq_norm_g, ckv_norm_g=v_ckv_norm_g, w_uq=v_w_uq, w_ukv=v_w_ukv,
             mla_q_norm_g=v_mla_q_norm_g, mla_k_norm_g=v_mla_k_norm_g, conv_w=v_conv_w, conv_b=v_conv_b, sg_ln_g=v_sg_ln_g,
             sg_ln_b=v_sg_ln_b, w_spatial=v_w_spatial, b_spatial=v_b_spatial, mem_norm_g=v_mem_norm_g, w_mem_kv=v_w_mem_kv,
             mem_q_norm_g=v_mem_q_norm_g, mem_k_norm_g=v_mem_k_norm_g, b_merge=v_b_merge, w_branch=v_w_branch, w_out=v_w_out)

    small = _gather_small_sharded(w)
    shards = {n: w[n].astype(MM) for n in _SHARDED_MM}
    srcs = [shards[n] for n in _SHARDED_MM]
    n_t = len(srcs)
    chip_core = jnp.stack([2 * lax.axis_index("x") + lax.axis_index("y"), lax.axis_index("c")]).astype(jnp.int32)
    gathered0 = _gather_layer(0, shards)
    bufs = _place_own_call(1, srcs, chip_core, "gather_l1_place_own")
    started = _gather_start_call(1, srcs, bufs, gathered0["w_ukv"], "gather_l1_start")
    bufs, send_sems, recv_sib, recv_ici = started[:n_t], started[n_t], started[n_t + 1], started[n_t + 2]
    passed = []

    def pass_on(ya0):
        passed.extend(_gather_forward_call(bufs, recv_ici, ya0, "gather_l1_forward"))
        return passed[n_t + 2]

    def layer1_params(saved0, act0):
        got = _gather_finish_call(1, srcs, passed[:n_t], send_sems, recv_sib, passed[n_t], passed[n_t + 1], act0, "gather_l1_finish")
        return _layer_params(1, w, dict(zip(_SHARDED_MM, got)), small["conv_w"][1], small["b_merge"][1])

    params0 = _layer_params(0, w, gathered0, small["conv_w"][0], small["b_merge"][0])
    params0["norm_g"] = _tie(params0["norm_g"], started[n_t + 3])
    params = [dict(params0, after_attn=pass_on), layer1_params]
    rs = [_ReduceScatter(l) for l in range(DEPTH)]
    shard_grads = {}
    hooks = [dict(on_grads=rs[0].exchange), dict(on_grads=rs[1].exchange, after_layer=lambda dy: rs[1].scatter([dy]))]
    sq, grad_x, layer_grads = _forward_backward(x[0], mem[0], positions[0], loss_target[0], params, hooks)
    loss = lax.psum(0.5 / D * jnp.sum(sq), ("x", "y", "c"))

    g = _all_reduce_small({n: jnp.stack([layer_grads[l][n] for l in range(DEPTH)]) for n in _ALL_REDUCED})
    scattering = rs[0].scatter([grad_x, g["norm_g"]])
    delta, new_m, new_v = _adamw_small(w, g, m, v, scattering)
    shard_grads[1] = rs[1].finish([scattering])
    as3d = lambda a: a.reshape(DEPTH, -1, a.shape[-1])
    big = lambda t: [as3d(t[n]) for n in _SHARDED_MM]
    as2d = lambda a: a.reshape(-1, a.shape[-1])
    upd1 = _adamw_layer_call(1, big(w), [as2d(shard_grads[1][n]) for n in _SHARDED_MM], big(m), big(v), None, [], "adamw_l1")
    shard_grads[0] = rs[0].finish([grad_x, upd1[0], delta["norm_g"]])
    upd = _adamw_layer_call(0, big(w), [as2d(shard_grads[0][n]) for n in _SHARDED_MM], big(m), big(v), upd1, [], "adamw_l0")
    for t, n in enumerate(_SHARDED_MM):
        g[n], delta[n], new_m[n], new_v[n] = [a.reshape(w[n].shape) for a in upd[4 * t:4 * t + 4]]
    return (loss, grad_x[None], *[g[n] for n in _WEIGHTS], *[delta[n] for n in _WEIGHTS],
            *[new_m[n] for n in _WEIGHTS], *[new_v[n] for n in _WEIGHTS])
```

```python
import functools
import math

import jax
import jax.numpy as jnp
from jax import lax
from jax.experimental import pallas as pl
from jax.experimental.pallas import tpu as pltpu

F32 = jnp.float32
MM = jnp.bfloat16

D = 1024
DEPTH = 2
EPS = 1e-6
H = 8
NOPE = 64
ROPE = 32
QKH = 96
VH = 64
QL = 256
KVL = 128
ROPE_THETA = 10000.0
CW = 512
SGW = 512
SGG = 4
SGC = 128
MH = 4
MHD = 128
NB = 4
BW = 512
NEG_INF = -1e30
LANES = 128
N_CHIPS = 4

R_CQ, R_CKV, R_KR, R_CV, R_SGI, R_MQ, R_SG, R_ML, R_END = 0, 256, 384, 416, 1952, 2976, 3488, 5536, 9632
OFF_ML, OFF_SG, OFF_CV, OFF_SGI, OFF_MQ, OFF_CQ, OFF_CKV, OFF_KR, NP = 0, 4096, 6144, 7680, 8704, 9216, 9472, 9600, 9728

ADAM_LR = 0.001
ADAM_B1 = 0.9
ADAM_B2 = 0.999
ADAM_EPS = 1e-08
ADAM_WD = 0.01
ADAM_STEP = 10

VMEM_LIMIT = 56 * 1024 * 1024
PACK_W = 512
MESH_ID = pl.DeviceIdType.MESH


def _cparams(n_axes):
    return pltpu.CompilerParams(dimension_semantics=("arbitrary",) * n_axes, vmem_limit_bytes=VMEM_LIMIT)


def _bs(shape, imap):
    return pl.BlockSpec(shape, imap)


@jax.custom_vjp
def _mm_plain(a, b):
    return jnp.dot(a.astype(MM), b.astype(MM), preferred_element_type=F32)


def _mm_plain_fwd(a, b):
    return _mm_plain(a, b), (a, b)


def _mm_plain_bwd(res, g):
    a, b = res
    gm = g.astype(MM)
    da = lax.dot_general(gm, b.astype(MM), (((1,), (1,)), ((), ())), preferred_element_type=F32)
    db = lax.dot_general(a.astype(MM), gm, (((0,), (0,)), ((), ())), preferred_element_type=F32)
    return da.astype(a.dtype), db.astype(b.dtype)


_mm_plain.defvjp(_mm_plain_fwd, _mm_plain_bwd)


@jax.custom_vjp
def _mm_slot(a, w, slot):
    return jnp.dot(a.astype(MM), w.astype(MM), preferred_element_type=F32)


def _mm_slot_fwd(a, w, slot):
    return _mm_slot(a, w, slot), (a, w)


def _mm_slot_bwd(res, g):
    a, w = res
    gm = g.astype(MM)
    da = lax.dot_general(gm, w.astype(MM), (((1,), (1,)), ((), ())), preferred_element_type=F32)
    dw = lax.dot_general(a.astype(MM), gm, (((0,), (0,)), ((), ())), preferred_element_type=F32)
    return da.astype(a.dtype), jnp.zeros_like(w), dw


_mm_slot.defvjp(_mm_slot_fwd, _mm_slot_bwd)


def _mm(a, b):
    if isinstance(b, tuple):
        return _mm_slot(a, b[0], b[1])
    return _mm_plain(a, b)


def _with_slot(w):
    return (w, jnp.zeros(w.shape, F32))


@jax.custom_vjp
def _mm_nt(a, b):
    return lax.dot_general(a.astype(MM), b.astype(MM), (((1,), (1,)), ((), ())), preferred_element_type=F32)


def _mm_nt_fwd(a, b):
    return _mm_nt(a, b), (a, b)


def _mm_nt_bwd(res, g):
    a, b = res
    gm = g.astype(MM)
    da = jnp.dot(gm, b.astype(MM), preferred_element_type=F32)
    db = lax.dot_general(gm, a.astype(MM), (((0,), (0,)), ((), ())), preferred_element_type=F32)
    return da.astype(a.dtype), db.astype(b.dtype)


_mm_nt.defvjp(_mm_nt_fwd, _mm_nt_bwd)


@functools.partial(jax.custom_vjp, nondiff_argnums=(1,))
def _lane_roll(x, shift):
    return pltpu.roll(x, shift, 1)


def _lane_roll_fwd(x, shift):
    return pltpu.roll(x, shift, 1), None


def _lane_roll_bwd(shift, _, g):
    return (pltpu.roll(g, (LANES - shift) % LANES, 1),)


_lane_roll.defvjp(_lane_roll_fwd, _lane_roll_bwd)


def _rms_n(x, g, n):
    ms = jnp.sum(x * x, axis=-1, keepdims=True) * (1.0 / n)
    return x * lax.rsqrt(ms + EPS) * g


def _softmax(s):
    m = jnp.max(s, axis=-1, keepdims=True)
    e = jnp.exp(s - m)
    return e / jnp.sum(e, axis=-1, keepdims=True)


def _rope(t, cos_t, sin_a, sin_b):
    return t * cos_t + _lane_roll(t, LANES - 16) * sin_a + _lane_roll(t, 16) * sin_b


def _mla_prep_fn(cq, ckv, kr, cos_t, sin_a, sin_b, cq_g, ckv_g, qg, kg, wuq, wkn, wv):
    cqn = _rms_n(cq, cq_g, QL)
    ckvn = _rms_n(ckv, ckv_g, KVL)
    lane = lax.broadcasted_iota(jnp.int32, kr.shape, 1)
    krm = jnp.where((lane >= NOPE) & (lane < QKH), kr, 0.0)
    qs, ks = [], []
    for h in range(H):
        qh = _rms_n(_mm(cqn, wuq[h]), qg, QKH)
        qs.append(_rope(qh, cos_t, sin_a, sin_b))
        kh = _rms_n(_mm(ckvn, wkn[h]) + krm, kg, QKH)
        ks.append(_rope(kh, cos_t, sin_a, sin_b))
    return jnp.concatenate(qs, axis=-1), jnp.concatenate(ks, axis=-1), _mm(ckvn, wv)


def _dot_nt(a, b):
    return lax.dot_general(a.astype(MM), b.astype(MM), (((1,), (1,)), ((), ())), preferred_element_type=F32)


def _dot_tn(a, b):
    return lax.dot_general(a.astype(MM), b.astype(MM), (((0,), (0,)), ((), ())), preferred_element_type=F32)


def _causal_scores(qe, ke, row0):
    tq, kl = qe.shape[0], ke.shape[0]
    rows = row0 + lax.broadcasted_iota(jnp.int32, (tq, kl), 0)
    cols = lax.broadcasted_iota(jnp.int32, (tq, kl), 1)
    return jnp.where(cols <= rows, _dot_nt(qe, ke) * (QKH ** -0.5), NEG_INF)


def _head_lanes(e, shape):
    lane = lax.broadcasted_iota(jnp.int32, shape, len(shape) - 1)
    return (lane >= VH * e) & (lane < VH * (e + 1))


def _attn_pair_fwd(q2, k2, v2, row0):
    tq = q2.shape[0]
    o = jnp.zeros((tq, LANES), F32)
    lse = jnp.zeros((tq, LANES), F32)
    for e in range(2):
        sl = slice(LANES * e, LANES * (e + 1))
        s = _causal_scores(q2[:, sl], k2[:, sl], row0)
        m = jnp.max(s, axis=-1, keepdims=True)
        ex = jnp.exp(s - m)
        l = jnp.sum(ex, axis=-1, keepdims=True)
        ve = jnp.where(_head_lanes(e, v2[:, sl].shape), v2[:, sl], 0.0)
        o = o + jnp.dot((ex / l).astype(MM), ve.astype(MM), preferred_element_type=F32)
        lse = jnp.where(_head_lanes(e, lse.shape), m + jnp.log(l), lse)
    return o, lse


def _attn_pair_bwd(q2, k2, v2, sg, dys, o, lse, row0):
    sig = jax.nn.sigmoid(sg)
    do = dys * (sg * sig)
    dsg = dys * o * (sig * (1.0 + sg * (1.0 - sig)))
    dqs, dks, dvs = [], [], []
    for e in range(2):
        sl = slice(LANES * e, LANES * (e + 1))
        qe, ke = q2[:, sl], k2[:, sl]
        hm = _head_lanes(e, o.shape)
        lse_e = jnp.max(jnp.where(hm, lse, NEG_INF), axis=-1, keepdims=True)
        do_e = jnp.where(hm, do, 0.0)
        delta = jnp.sum(do_e * o, axis=-1, keepdims=True)
        p = jnp.exp(_causal_scores(qe, ke, row0) - lse_e)
        ve = jnp.where(_head_lanes(e, v2[:, sl].shape), v2[:, sl], 0.0)
        dvs.append(_dot_tn(p, do_e))
        ds = (p * (_dot_nt(do_e, ve) - delta)) * (QKH ** -0.5)
        dqs.append(jnp.dot(ds.astype(MM), ke.astype(MM), preferred_element_type=F32))
        dks.append(_dot_tn(ds, qe))
    return jnp.concatenate(dqs, axis=-1), jnp.concatenate(dks, axis=-1), jnp.concatenate(dvs, axis=-1), dsg


def _sg_fn(u, v, sgc, ln_g, ln_b, ws, bs):
    mu = jnp.mean(v, axis=-1, keepdims=True)
    xc = v - mu
    vn = xc * lax.rsqrt(jnp.mean(xc * xc, axis=-1, keepdims=True) + EPS) * ln_g + ln_b
    r = lax.broadcasted_iota(jnp.int32, (SGC, SGC), 0)
    c = lax.broadcasted_iota(jnp.int32, (SGC, SGC), 1)
    wt = [jnp.where(r >= c, w, 0.0) for w in ws]
    row_blocks = []
    for ch in range(u.shape[0] // SGC):
        col_blocks = []
        for g in range(SGG):
            blk = vn[SGC * ch:SGC * (ch + 1), LANES * g:LANES * (g + 1)]
            col_blocks.append(_mm(wt[g], blk) + bs[g])
        row_blocks.append(jnp.concatenate(col_blocks, axis=-1))
    mixed = jnp.concatenate(row_blocks, axis=0)
    return (u * mixed) * jax.nn.silu(sgc)


def _memkv_fn(mem, mem_g, wm, kg):
    kv = _mm(_rms_n(mem, mem_g, D), wm)
    ks = [_rms_n(kv[:, MHD * h:MHD * (h + 1)], kg, MHD) for h in range(MH)]
    return jnp.concatenate(ks, axis=-1), kv[:, MH * MHD:]


def _mem_fn(mq, sgd, k, v, qg):
    outs = []
    for h in range(MH):
        sl = slice(MHD * h, MHD * (h + 1))
        qh = _rms_n(mq[:, sl], qg, MHD)
        p = _softmax(_mm_nt(qh, k[:, sl]) * (MHD ** -0.5))
        outs.append(_mm(p, v[:, sl]))
    return jnp.concatenate(outs, axis=-1) * jax.nn.silu(sgd)


def _merge_fn(ys, logits, bm, wb, wo):
    merged = None
    for n in range(NB):
        z = jnp.concatenate([_mm(ys[n], wb[j][n]) for j in range(N_CHIPS)], axis=-1)
        gate = jax.nn.sigmoid(logits[:, D * n:D * (n + 1)] + bm[n])
        merged = gate * z if merged is None else merged + gate * z
    return _mm(merged, wo)


def _proj_call(x, g, w):
    s_len = x.shape[0]
    tm, tn = min(s_len, 1024), NP // 4

    def body(x_ref, g_ref, w_ref, p_ref, h_ref):
        @pl.when(pl.program_id(1) == 0)
        def _():
            h_ref[...] = _rms_n(x_ref[...], g_ref[...], D).astype(h_ref.dtype)
        p_ref[...] = jnp.dot(h_ref[...], w_ref[...], preferred_element_type=F32)

    return pl.pallas_call(
        body, grid=(s_len // tm, NP // tn),
        in_specs=[_bs((tm, D), lambda i, j: (i, 0)), _bs((1, D), lambda i, j: (0, 0)), _bs((D, tn), lambda i, j: (0, j))],
        out_specs=[_bs((tm, tn), lambda i, j: (i, j)), _bs((tm, D), lambda i, j: (i, 0))],
        out_shape=[jax.ShapeDtypeStruct((s_len, NP), F32), jax.ShapeDtypeStruct((s_len, D), MM)],
        name="proj", compiler_params=_cparams(2))(x, g, w)


def _rope_tables(pos):
    half = ROPE // 2
    inv_freq = ROPE_THETA ** (-jnp.arange(half, dtype=F32) / half)
    ang = pos.astype(F32)[:, None] * inv_freq
    cos, sin = jnp.cos(ang), jnp.sin(ang)
    s_len = pos.shape[0]
    z = lambda n: jnp.zeros((s_len, n), F32)
    cos_t = jnp.concatenate([jnp.ones((s_len, NOPE), F32), cos, cos, z(LANES - QKH)], axis=1)
    sin_a = jnp.concatenate([z(NOPE), -sin, z(LANES - NOPE - half)], axis=1)
    sin_b = jnp.concatenate([z(NOPE + half), sin, z(LANES - QKH)], axis=1)
    return cos_t, sin_a, sin_b


def _mla_prep_specs(tm):
    row = lambda w, off: _bs((tm, w), lambda i: (i, off // w))
    full2 = lambda a, b: _bs((a, b), lambda i: (0, 0))
    full3 = lambda a, b, c: _bs((a, b, c), lambda i: (0, 0, 0))
    tab = _bs((tm, LANES), lambda i: (i, 0))
    return [row(QL, OFF_CQ), row(KVL, OFF_CKV), row(LANES, OFF_KR), tab, tab, tab,
            full2(1, QL), full2(1, KVL), full2(1, LANES), full2(1, LANES),
            full3(H, QL, LANES), full3(H, KVL, LANES), full2(KVL, H * LANES)]


def _mla_prep_args(body_refs, wrap=lambda w: w):
    (cq, ckv, kr, ct, sa, sb, cqg, ckvg, qg, kg, wuq, wkn, wv) = body_refs
    return (cq[...], ckv[...], kr[...], ct[...], sa[...], sb[...], cqg[...], ckvg[...], qg[...], kg[...],
            [wrap(wuq[h]) for h in range(H)], [wrap(wkn[h]) for h in range(H)], wrap(wv[...]))


def _mla_prep_call(proj, tabs, cq_g, ckv_g, qg, kg, wuq, wkn, wv):
    s_len = proj.shape[0]
    tm = min(s_len, 256)

    def body(*refs):
        q_ref, k_ref, v_ref = refs[13:]
        q, k, v = _mla_prep_fn(*_mla_prep_args(refs[:13]))
        q_ref[...] = q.astype(q_ref.dtype)
        k_ref[...] = k.astype(k_ref.dtype)
        v_ref[...] = v.astype(v_ref.dtype)

    out = _bs((tm, H * LANES), lambda i: (i, 0))
    return pl.pallas_call(
        body, grid=(s_len // tm,), in_specs=_mla_prep_specs(tm), out_specs=[out, out, out],
        out_shape=[jax.ShapeDtypeStruct((s_len, H * LANES), MM)] * 3,
        name="mla_prep", compiler_params=_cparams(1))(proj, proj, proj, *tabs, cq_g, ckv_g, qg, kg, wuq, wkn, wv)


def _mla_prep_bwd_call(proj, tabs, cq_g, ckv_g, qg, kg, wuq, wkn, wv, dq, dk, dv):
    s_len = proj.shape[0]
    tm = min(s_len, 256)

    def body(*refs):
        dq_ref, dk_ref, dv_ref = refs[13:16]
        dcq_ref, dckv_ref, dkr_ref, dcqg_ref, dckvg_ref, dqg_ref, dkg_ref, dwuq_ref, dwkn_ref, dwv_ref = refs[16:]
        _, vjp = jax.vjp(_mla_prep_fn, *_mla_prep_args(refs[:13], _with_slot))
        (dcq, dckv, dkr, _, _, _, dcqg, dckvg, dqg, dkg, dwuq, dwkn, dwv) = vjp((dq_ref[...], dk_ref[...], dv_ref[...]))
        dwuq, dwkn, dwv = [d[1] for d in dwuq], [d[1] for d in dwkn], dwv[1]
        dcq_ref[...] = dcq.astype(dcq_ref.dtype)
        dckv_ref[...] = dckv.astype(dckv_ref.dtype)
        dkr_ref[...] = dkr.astype(dkr_ref.dtype)

        @pl.when(pl.program_id(0) == 0)
        def _():
            for r in (dcqg_ref, dckvg_ref, dqg_ref, dkg_ref, dwuq_ref, dwkn_ref, dwv_ref):
                r[...] = jnp.zeros_like(r)
        dcqg_ref[...] += dcqg
        dckvg_ref[...] += dckvg
        dqg_ref[...] += dqg
        dkg_ref[...] += dkg
        for h in range(H):
            dwuq_ref[h] += dwuq[h]
            dwkn_ref[h] += dwkn[h]
        dwv_ref[...] += dwv

    big = _bs((tm, H * LANES), lambda i: (i, 0))
    row = lambda w: _bs((tm, w), lambda i: (i, 0))
    full2 = lambda a, b: _bs((a, b), lambda i: (0, 0))
    full3 = lambda a, b, c: _bs((a, b, c), lambda i: (0, 0, 0))
    sd = jax.ShapeDtypeStruct
    return pl.pallas_call(
        body, grid=(s_len // tm,), in_specs=_mla_prep_specs(tm) + [big, big, big],
        out_specs=[row(QL), row(KVL), row(LANES), full2(1, QL), full2(1, KVL), full2(1, LANES), full2(1, LANES),
                   full3(H, QL, LANES), full3(H, KVL, LANES), full2(KVL, H * LANES)],
        out_shape=[sd((s_len, QL), MM), sd((s_len, KVL), MM), sd((s_len, LANES), MM), sd((1, QL), F32), sd((1, KVL), F32),
                   sd((1, LANES), F32), sd((1, LANES), F32), sd((H, QL, LANES), F32), sd((H, KVL, LANES), F32),
                   sd((KVL, H * LANES), F32)],
        name="mla_prep_bwd", compiler_params=_cparams(1))(proj, proj, proj, *tabs, cq_g, ckv_g, qg, kg, wuq, wkn, wv, dq, dk, dv)


def _attn_specs(s_len, tq):
    pair = 2 * LANES
    return [_bs((tq, pair), lambda p, i: (i, p)), _bs((s_len, pair), lambda p, i: (0, p)), _bs((s_len, pair), lambda p, i: (0, p)),
            _bs((tq, LANES), lambda p, i: (i, OFF_SG // LANES + p))]


def _attn_call(q, k, v, proj):
    s_len = q.shape[0]
    tq = min(s_len, 256)

    def body(q_ref, k_ref, v_ref, sg_ref, y_ref, o_ref, lse_ref):
        for n in range(s_len // tq):
            @pl.when(pl.program_id(1) == n)
            def _():
                kl = (n + 1) * tq
                o, lse = _attn_pair_fwd(q_ref[...], k_ref[:kl, :], v_ref[:kl, :], n * tq)
                y_ref[...] = (o * jax.nn.silu(sg_ref[...])).astype(y_ref.dtype)
                o_ref[...] = o
                lse_ref[...] = lse

    tile = _bs((tq, LANES), lambda p, i: (i, p))
    sd = jax.ShapeDtypeStruct
    return pl.pallas_call(
        body, grid=(H // 2, s_len // tq), in_specs=_attn_specs(s_len, tq), out_specs=[tile, tile, tile],
        out_shape=[sd((s_len, BW), MM), sd((s_len, BW), F32), sd((s_len, BW), F32)],
        name="attn", compiler_params=_cparams(2))(q, k, v, proj)


def _attn_bwd_call(q, k, v, proj, dys, o, lse):
    s_len = q.shape[0]
    tq = min(s_len, 256)
    pair = 2 * LANES

    def body(q_ref, k_ref, v_ref, sg_ref, dy_ref, o_ref, lse_ref, dq_ref, dk_ref, dv_ref, dsg_ref):
        i = pl.program_id(1)

        @pl.when(i == 0)
        def _():
            dk_ref[...] = jnp.zeros_like(dk_ref)
            dv_ref[...] = jnp.zeros_like(dv_ref)

        for n in range(s_len // tq):
            @pl.when(i == n)
            def _():
                kl = (n + 1) * tq
                dq, dk, dv, dsg = _attn_pair_bwd(q_ref[...], k_ref[:kl, :], v_ref[:kl, :], sg_ref[...], dy_ref[...],
                                                 o_ref[...], lse_ref[...], n * tq)
                dq_ref[...] = dq
                dsg_ref[...] = dsg.astype(dsg_ref.dtype)
                dk_ref[:kl, :] += dk
                dv_ref[:kl, :] += dv

    sd = jax.ShapeDtypeStruct
    tile = _bs((tq, LANES), lambda p, i: (i, p))
    return pl.pallas_call(
        body, grid=(H // 2, s_len // tq),
        in_specs=_attn_specs(s_len, tq) + [tile, tile, tile],
        out_specs=[_bs((tq, pair), lambda p, i: (i, p)), _bs((s_len, pair), lambda p, i: (0, p)),
                   _bs((s_len, pair), lambda p, i: (0, p)), tile],
        out_shape=[sd((s_len, H * LANES), F32), sd((s_len, H * LANES), F32), sd((s_len, H * LANES), F32), sd((s_len, BW), MM)],
        name="attn_bwd", compiler_params=_cparams(2))(q, k, v, proj, dys, o, lse)


def _shift_down(a, n):
    r = lax.broadcasted_iota(jnp.int32, a.shape, 0)
    return jnp.where(r >= n, pltpu.roll(a, n, 0), 0.0)


def _shift_up(a, n):
    s_len = a.shape[0]
    r = lax.broadcasted_iota(jnp.int32, a.shape, 0)
    return jnp.where(r < s_len - n, pltpu.roll(a, s_len - n, 0), 0.0)


def _conv_specs(s_len):
    col = lambda off: _bs((s_len, LANES), lambda j: (0, off // LANES + j))
    return [col(OFF_CV), col(OFF_CV + CW), col(OFF_CV + 2 * CW), col(OFF_SG + BW),
            _bs((3, LANES), lambda j: (0, j)), _bs((1, LANES), lambda j: (0, j))]


def _conv_call(proj, cw, cb):
    s_len = proj.shape[0]

    def body(bg_ref, cg_ref, xi_ref, sg_ref, w_ref, b_ref, y_ref):
        z = cg_ref[...] * xi_ref[...]
        y = b_ref[...] + w_ref[0:1, :] * _shift_down(z, 2)
        y = y + w_ref[1:2, :] * _shift_down(z, 1)
        y = y + w_ref[2:3, :] * z
        y_ref[...] = ((bg_ref[...] * y) * jax.nn.silu(sg_ref[...])).astype(y_ref.dtype)

    return pl.pallas_call(
        body, grid=(CW // LANES,), in_specs=_conv_specs(s_len), out_specs=_bs((s_len, LANES), lambda j: (0, j)),
        out_shape=jax.ShapeDtypeStruct((s_len, CW), MM), name="conv", compiler_params=_cparams(1))(proj, proj, proj, proj, cw, cb)


def _conv_bwd_call(proj, cw, cb, dys):
    s_len = proj.shape[0]

    def body(bg_ref, cg_ref, xi_ref, sg_ref, w_ref, b_ref, dys_ref, dbg_ref, dcg_ref, dxi_ref, dsg_ref, dw_ref, db_ref):
        bg, cg, xi, sg = bg_ref[...], cg_ref[...], xi_ref[...], sg_ref[...]
        w0, w1, w2 = w_ref[0:1, :], w_ref[1:2, :], w_ref[2:3, :]
        z = cg * xi
        z1, z2 = _shift_down(z, 1), _shift_down(z, 2)
        y = b_ref[...] + w0 * z2
        y = y + w1 * z1
        y = y + w2 * z
        yb = bg * y
        sig = jax.nn.sigmoid(sg)
        silu = sg * sig
        dys_v = dys_ref[...]
        dsg_ref[...] = (dys_v * yb * (sig * (1.0 + sg * (1.0 - sig)))).astype(dsg_ref.dtype)
        dyb = dys_v * silu
        dbg_ref[...] = (dyb * y).astype(dbg_ref.dtype)
        dy = dyb * bg
        db_ref[...] = jnp.sum(dy, axis=0, keepdims=True)
        dw_ref[0:1, :] = jnp.sum(dy * z2, axis=0, keepdims=True)
        dw_ref[1:2, :] = jnp.sum(dy * z1, axis=0, keepdims=True)
        dw_ref[2:3, :] = jnp.sum(dy * z, axis=0, keepdims=True)
        dz = w2 * dy + w1 * _shift_up(dy, 1) + w0 * _shift_up(dy, 2)
        dcg_ref[...] = (dz * xi).astype(dcg_ref.dtype)
        dxi_ref[...] = (dz * cg).astype(dxi_ref.dtype)

    col = _bs((s_len, LANES), lambda j: (0, j))
    sd = jax.ShapeDtypeStruct
    return pl.pallas_call(
        body, grid=(CW // LANES,), in_specs=_conv_specs(s_len) + [col],
        out_specs=[col, col, col, col, _bs((3, LANES), lambda j: (0, j)), _bs((1, LANES), lambda j: (0, j))],
        out_shape=[sd((s_len, CW), MM)] * 4 + [sd((3, CW), F32), sd((1, CW), F32)],
        name="conv_bwd", compiler_params=_cparams(1))(proj, proj, proj, proj, cw, cb, dys)


def _sg_specs(tm):
    row = lambda off: _bs((tm, SGW), lambda i: (i, off // SGW))
    return [row(OFF_SGI), row(OFF_SGI + SGW), row(OFF_SG + 2 * BW), _bs((1, SGW), lambda i: (0, 0)), _bs((1, SGW), lambda i: (0, 0)),
            _bs((SGG, SGC, SGC), lambda i: (0, 0, 0)), _bs((SGG, SGC, 1), lambda i: (0, 0, 0))]


def _sg_args(refs):
    u, v, sg, lg, lb, ws, bs = refs
    return (u[...], v[...], sg[...], lg[...], lb[...], [ws[g] for g in range(SGG)], [bs[g] for g in range(SGG)])


def _sg_call(proj, ln_g, ln_b, ws, bs):
    s_len = proj.shape[0]
    tm = min(s_len, 256)

    def body(*refs):
        refs[7][...] = _sg_fn(*_sg_args(refs[:7])).astype(refs[7].dtype)

    return pl.pallas_call(
        body, grid=(s_len // tm,), in_specs=_sg_specs(tm), out_specs=_bs((tm, SGW), lambda i: (i, 0)),
        out_shape=jax.ShapeDtypeStruct((s_len, SGW), MM), name="sgmlp", compiler_params=_cparams(1))(proj, proj, proj, ln_g, ln_b, ws, bs)


def _sg_bwd_call(proj, ln_g, ln_b, ws, bs, dys):
    s_len = proj.shape[0]
    tm = min(s_len, 256)

    def body(*refs):
        dys_ref = refs[7]
        du_ref, dv_ref, dsg_ref, dlg_ref, dlb_ref, dws_ref, dbs_ref = refs[8:]
        _, vjp = jax.vjp(_sg_fn, *_sg_args(refs[:7]))
        du, dv, dsg, dlg, dlb, dws, dbs = vjp(dys_ref[...])
        du_ref[...] = du.astype(du_ref.dtype)
        dv_ref[...] = dv.astype(dv_ref.dtype)
        dsg_ref[...] = dsg.astype(dsg_ref.dtype)

        @pl.when(pl.program_id(0) == 0)
        def _():
            for r in (dlg_ref, dlb_ref, dws_ref, dbs_ref):
                r[...] = jnp.zeros_like(r)
        dlg_ref[...] += dlg
        dlb_ref[...] += dlb
        for g in range(SGG):
            dws_ref[g] += dws[g]
            dbs_ref[g] += dbs[g]

    row = _bs((tm, SGW), lambda i: (i, 0))
    sd = jax.ShapeDtypeStruct
    return pl.pallas_call(
        body, grid=(s_len // tm,), in_specs=_sg_specs(tm) + [row],
        out_specs=[row, row, row, _bs((1, SGW), lambda i: (0, 0)), _bs((1, SGW), lambda i: (0, 0)),
                   _bs((SGG, SGC, SGC), lambda i: (0, 0, 0)), _bs((SGG, SGC, 1), lambda i: (0, 0, 0))],
        out_shape=[sd((s_len, SGW), MM)] * 3 + [sd((1, SGW), F32), sd((1, SGW), F32), sd((SGG, SGC, SGC), F32), sd((SGG, SGC, 1), F32)],
        name="sgmlp_bwd", compiler_params=_cparams(1))(proj, proj, proj, ln_g, ln_b, ws, bs, dys)


def _memkv_call(mem, mem_g, wm, kg):
    m_len = mem.shape[0]

    def body(mem_ref, g_ref, w_ref, kg_ref, k_ref, v_ref):
        k, v = _memkv_fn(mem_ref[...], g_ref[...], w_ref[...], kg_ref[...])
        k_ref[...] = k.astype(k_ref.dtype)
        v_ref[...] = v.astype(v_ref.dtype)

    return pl.pallas_call(body, out_shape=[jax.ShapeDtypeStruct((m_len, MH * MHD), MM)] * 2, name="memkv",
                          compiler_params=pltpu.CompilerParams(vmem_limit_bytes=VMEM_LIMIT))(mem, mem_g, wm, kg)


def _memkv_bwd_call(mem, mem_g, wm, kg, dk, dv):
    def body(mem_ref, g_ref, w_ref, kg_ref, dk_ref, dv_ref, dg_ref, dw_ref, dkg_ref):
        _, vjp = jax.vjp(_memkv_fn, mem_ref[...], g_ref[...], _with_slot(w_ref[...]), kg_ref[...])
        _, dg, dw, dkg = vjp((dk_ref[...], dv_ref[...]))
        dg_ref[...] = dg
        dw_ref[...] = dw[1]
        dkg_ref[...] = dkg

    sd = jax.ShapeDtypeStruct
    return pl.pallas_call(body, out_shape=[sd((1, D), F32), sd((D, 2 * MH * MHD), F32), sd((1, MHD), F32)], name="memkv_bwd",
                          compiler_params=pltpu.CompilerParams(vmem_limit_bytes=VMEM_LIMIT))(mem, mem_g, wm, kg, dk, dv)


def _mem_specs(tm, m_len):
    w = MH * MHD
    return [_bs((tm, w), lambda i: (i, OFF_MQ // w)), _bs((tm, BW), lambda i: (i, (OFF_SG + 3 * BW) // BW)),
            _bs((m_len, w), lambda i: (0, 0)), _bs((m_len, w), lambda i: (0, 0)), _bs((1, MHD), lambda i: (0, 0))]


def _mem_call(proj, k, v, qg):
    s_len, m_len = proj.shape[0], k.shape[0]
    tm = min(s_len, 256)

    def body(mq_ref, sg_ref, k_ref, v_ref, qg_ref, y_ref):
        y_ref[...] = _mem_fn(mq_ref[...], sg_ref[...], k_ref[...], v_ref[...], qg_ref[...]).astype(y_ref.dtype)

    return pl.pallas_call(
        body, grid=(s_len // tm,), in_specs=_mem_specs(tm, m_len), out_specs=_bs((tm, BW), lambda i: (i, 0)),
        out_shape=jax.ShapeDtypeStruct((s_len, BW), MM), name="memattn", compiler_params=_cparams(1))(proj, proj, k, v, qg)


def _mem_bwd_call(proj, k, v, qg, dys):
    s_len, m_len = proj.shape[0], k.shape[0]
    tm = min(s_len, 256)
    w = MH * MHD

    def body(mq_ref, sg_ref, k_ref, v_ref, qg_ref, dys_ref, dmq_ref, dsg_ref, dk_ref, dv_ref, dqg_ref):
        _, vjp = jax.vjp(_mem_fn, mq_ref[...], sg_ref[...], k_ref[...].astype(F32), v_ref[...].astype(F32), qg_ref[...])
        dmq, dsg, dk, dv, dqg = vjp(dys_ref[...])
        dmq_ref[...] = dmq.astype(dmq_ref.dtype)
        dsg_ref[...] = dsg.astype(dsg_ref.dtype)

        @pl.when(pl.program_id(0) == 0)
        def _():
            for r in (dk_ref, dv_ref, dqg_ref):
                r[...] = jnp.zeros_like(r)
        dk_ref[...] += dk
        dv_ref[...] += dv
        dqg_ref[...] += dqg

    row = _bs((tm, BW), lambda i: (i, 0))
    kv = _bs((m_len, w), lambda i: (0, 0))
    sd = jax.ShapeDtypeStruct
    return pl.pallas_call(
        body, grid=(s_len // tm,), in_specs=_mem_specs(tm, m_len) + [row],
        out_specs=[row, row, kv, kv, _bs((1, MHD), lambda i: (0, 0))],
        out_shape=[sd((s_len, w), MM), sd((s_len, BW), MM), sd((m_len, w), F32), sd((m_len, w), F32), sd((1, MHD), F32)],
        name="memattn_bwd", compiler_params=_cparams(1))(proj, proj, k, v, qg, dys)


def _merge_specs(tm):
    row = _bs((tm, BW), lambda i: (i, 0))
    return [row, row, row, row, _bs((tm, NB * D), lambda i: (i, OFF_ML // (NB * D))), _bs((NB, D), lambda i: (0, 0)),
            _bs((N_CHIPS, NB, BW, D // N_CHIPS), lambda i: (0, 0, 0, 0)), _bs((D, D), lambda i: (0, 0))]


def _merge_call(ys, proj, bm, wb, wo, x):
    s_len = proj.shape[0]
    tm = min(s_len, 256)

    def body(ya, yb, yc, yd, lg_ref, bm_ref, wb_ref, wo_ref, x_ref, o_ref):
        out = _merge_fn([r[...] for r in (ya, yb, yc, yd)], lg_ref[...], [bm_ref[n:n + 1, :] for n in range(NB)],
                        [[wb_ref[j, n] for n in range(NB)] for j in range(N_CHIPS)], wo_ref[...])
        o_ref[...] = x_ref[...] + out

    xrow = _bs((tm, D), lambda i: (i, 0))
    return pl.pallas_call(
        body, grid=(s_len // tm,), in_specs=_merge_specs(tm) + [xrow], out_specs=xrow,
        out_shape=jax.ShapeDtypeStruct((s_len, D), F32), name="merge", compiler_params=_cparams(1))(*ys, proj, bm, wb, wo, x)


def _merge_bwd_call(ys, proj, bm, wb, wo, dout):
    s_len = proj.shape[0]
    tm = min(s_len, 256)

    def body(ya, yb, yc, yd, lg_ref, bm_ref, wb_ref, wo_ref, do_ref, dya, dyb, dyc, dyd, dlg_ref, dbm_ref, dwb_ref, dwo_ref):
        fn = lambda ys_, lg_, bm_, wb_, wo_: _merge_fn(ys_, lg_, bm_, wb_, wo_)
        _, vjp = jax.vjp(fn, [r[...].astype(F32) for r in (ya, yb, yc, yd)], lg_ref[...], [bm_ref[n:n + 1, :] for n in range(NB)],
                         [[_with_slot(wb_ref[j, n]) for n in range(NB)] for j in range(N_CHIPS)], _with_slot(wo_ref[...]))
        dys, dlg, dbm, dwb, dwo = vjp(do_ref[...])
        dwb, dwo = [[d[1] for d in row] for row in dwb], dwo[1]
        for r, d in zip((dya, dyb, dyc, dyd), dys):
            r[...] = d
        dlg_ref[...] = dlg.astype(dlg_ref.dtype)

        @pl.when(pl.program_id(0) == 0)
        def _():
            for r in (dbm_ref, dwb_ref, dwo_ref):
                r[...] = jnp.zeros_like(r)
        for n in range(NB):
            dbm_ref[n:n + 1, :] += dbm[n]
            for j in range(N_CHIPS):
                dwb_ref[j, n] += dwb[j][n]
        dwo_ref[...] += dwo

    row = _bs((tm, BW), lambda i: (i, 0))
    sd = jax.ShapeDtypeStruct
    wb_shape = (N_CHIPS, NB, BW, D // N_CHIPS)
    return pl.pallas_call(
        body, grid=(s_len // tm,), in_specs=_merge_specs(tm) + [_bs((tm, D), lambda i: (i, 0))],
        out_specs=[row, row, row, row, _bs((tm, NB * D), lambda i: (i, 0)), _bs((NB, D), lambda i: (0, 0)),
                   _bs(wb_shape, lambda i: (0, 0, 0, 0)), _bs((D, D), lambda i: (0, 0))],
        out_shape=[sd((s_len, BW), F32)] * 4 + [sd((s_len, NB * D), MM), sd((NB, D), F32), sd(wb_shape, F32), sd((D, D), F32)],
        name="merge_bwd", compiler_params=_cparams(1))(*ys, proj, bm, wb, wo, dout)


def _dh_call(dproj, w, x, g, dout):
    s_len = x.shape[0]
    tm, tk = min(s_len, 512), NP // 4

    def body(dp_ref, w_ref, x_ref, g_ref, do_ref, dx_ref, dg_ref, acc_ref):
        i, k = pl.program_id(0), pl.program_id(1)

        @pl.when(k == 0)
        def _():
            acc_ref[...] = jnp.zeros_like(acc_ref)
        acc_ref[...] += lax.dot_general(dp_ref[...], w_ref[...], (((1,), (1,)), ((), ())), preferred_element_type=F32)

        @pl.when(k == pl.num_programs(1) - 1)
        def _():
            _, vjp = jax.vjp(lambda x_, g_: _rms_n(x_, g_, D), x_ref[...], g_ref[...])
            dxr, dgr = vjp(acc_ref[...])
            dx_ref[...] = do_ref[...] + dxr

            @pl.when(i == 0)
            def _():
                dg_ref[...] = jnp.zeros_like(dg_ref)
            dg_ref[...] += dgr

    row = _bs((tm, D), lambda i, k: (i, 0))
    return pl.pallas_call(
        body, grid=(s_len // tm, NP // tk),
        in_specs=[_bs((tm, tk), lambda i, k: (i, k)), _bs((D, tk), lambda i, k: (0, k)), row, _bs((1, D), lambda i, k: (0, 0)), row],
        out_specs=[row, _bs((1, D), lambda i, k: (0, 0))],
        out_shape=[jax.ShapeDtypeStruct((s_len, D), F32), jax.ShapeDtypeStruct((1, D), F32)],
        scratch_shapes=[pltpu.VMEM((tm, D), F32)], name="dh", compiler_params=_cparams(2))(dproj, w, x, g, dout)


def _dw_call(h, dproj):
    s_len = h.shape[0]
    tn = 512

    def body(h_ref, dp_ref, o_ref):
        o_ref[...] = lax.dot_general(h_ref[...], dp_ref[...], (((0,), (0,)), ((), ())), preferred_element_type=F32)

    return pl.pallas_call(
        body, grid=(NP // tn,), in_specs=[_bs((s_len, D), lambda j: (0, 0)), _bs((s_len, tn), lambda j: (0, j))],
        out_specs=_bs((D, tn), lambda j: (0, j)), out_shape=jax.ShapeDtypeStruct((D, NP), F32),
        name="dw_in", compiler_params=_cparams(1))(h, dproj)


def _loss_call(y, target):
    s_len = y.shape[0]
    tm = min(s_len, 512)

    def body(y_ref, t_ref, dy_ref, l_ref):
        e = y_ref[...] - t_ref[...]
        dy_ref[...] = e * (1.0 / D)

        @pl.when(pl.program_id(0) == 0)
        def _():
            l_ref[...] = jnp.zeros_like(l_ref)
        l_ref[...] += jnp.sum(e * e, axis=0, keepdims=True)

    row = _bs((tm, D), lambda i: (i, 0))
    return pl.pallas_call(
        body, grid=(s_len // tm,), in_specs=[row, row], out_specs=[row, _bs((1, D), lambda i: (0, 0))],
        out_shape=[jax.ShapeDtypeStruct((s_len, D), F32), jax.ShapeDtypeStruct((1, D), F32)],
        name="loss", compiler_params=_cparams(1))(y, target)


def _adamw_call(w, g, m, v, name):
    rows, cols = w.shape
    tr = min(_row_tile(rows), 128)

    def body(w_ref, g_ref, m_ref, v_ref, d_ref, nm_ref, nv_ref):
        gv = g_ref[...]
        m2 = ADAM_B1 * m_ref[...] + (1.0 - ADAM_B1) * gv
        v2 = ADAM_B2 * v_ref[...] + (1.0 - ADAM_B2) * (gv * gv)
        m_hat = m2 / (1.0 - ADAM_B1 ** ADAM_STEP)
        v_hat = v2 / (1.0 - ADAM_B2 ** ADAM_STEP)
        d_ref[...] = -ADAM_LR * (m_hat / (jnp.sqrt(v_hat) + ADAM_EPS) + ADAM_WD * w_ref[...])
        nm_ref[...] = m2
        nv_ref[...] = v2

    blk = _bs((tr, cols), lambda i: (i, 0))
    return pl.pallas_call(
        body, grid=(rows // tr,), in_specs=[blk] * 4, out_specs=[blk] * 3,
        out_shape=[jax.ShapeDtypeStruct((rows, cols), F32)] * 3, name=name, compiler_params=_cparams(1))(w, g, m, v)


def _adamw_layer_call(layer, ws, gs, ms, vs, prev, after, name):
    n, steps = len(ws), 8
    after = list(after)
    n_prev = 4 * n if prev is not None else 0

    def body(*refs):
        outs = refs[len(refs) - 4 * n:]
        for t in range(n):
            w_ref, g_ref, m_ref, v_ref = refs[t], refs[n + t], refs[2 * n + t], refs[3 * n + t]
            g_out, d_out, m_out, v_out = outs[4 * t:4 * t + 4]
            gv = g_ref[...]
            m2 = ADAM_B1 * m_ref[0] + (1.0 - ADAM_B1) * gv
            v2 = ADAM_B2 * v_ref[0] + (1.0 - ADAM_B2) * (gv * gv)
            m_hat = m2 / (1.0 - ADAM_B1 ** ADAM_STEP)
            v_hat = v2 / (1.0 - ADAM_B2 ** ADAM_STEP)
            g_out[0] = gv
            d_out[0] = -ADAM_LR * (m_hat / (jnp.sqrt(v_hat) + ADAM_EPS) + ADAM_WD * w_ref[0])
            m_out[0] = m2
            v_out[0] = v2

    def lay(a):
        return _bs((1, a.shape[1] // steps, a.shape[2]), lambda i: (layer, i, 0))

    in_specs = ([lay(a) for a in ws] + [_bs((g.shape[0] // steps, g.shape[1]), lambda i: (i, 0)) for g in gs]
                + [lay(a) for a in ms] + [lay(a) for a in vs] + [_ANY] * (n_prev + len(after)))
    return pl.pallas_call(
        body, grid=(steps,), in_specs=in_specs, out_specs=[lay(ws[t]) for t in range(n) for _ in range(4)],
        out_shape=[jax.ShapeDtypeStruct(ws[t].shape, F32) for t in range(n) for _ in range(4)],
        input_output_aliases={4 * n + q: q for q in range(n_prev)}, name=name, compiler_params=_cparams(1),
    )(*ws, *gs, *ms, *vs, *(prev if prev is not None else []), *after)


def _row_tile(rows):
    for cand in (512, 256, 128, 64, 32, 16, 8):
        if rows % cand == 0 and rows > cand:
            return cand
    return rows


def _pair_sum_call(grads, from_sibling, core, name):
    n = len(grads)

    def body(core_ref, *refs):
        for t in range(n):
            refs[2 * n + t][...] = (refs[t][...] + refs[n + t][...]).astype(MM)

    half = lambda g: (1, g.shape[1] // 2, g.shape[2])
    grid_spec = pltpu.PrefetchScalarGridSpec(
        num_scalar_prefetch=1, grid=(N_CHIPS,),
        in_specs=[pl.BlockSpec(half(g), lambda j, core_ref: (j, core_ref[0], 0)) for g in grads]
        + [pl.BlockSpec(half(g), lambda j, core_ref: (j, 0, 0)) for g in grads],
        out_specs=[pl.BlockSpec(half(g), lambda j, core_ref: (j, 0, 0)) for g in grads])
    return pl.pallas_call(
        body, grid_spec=grid_spec, out_shape=[jax.ShapeDtypeStruct((N_CHIPS,) + half(g)[1:], MM) for g in grads], name=name,
        compiler_params=_cparams(1))(core, *grads, *from_sibling)


def _owner_sum_call(chip_sums, from_chips, chip_core, name):
    n = len(chip_sums)
    steps = 4

    def body(ids_ref, *refs):
        for t in range(n):
            a, b = refs[t], refs[n + t]
            refs[2 * n + t][...] = ((a[0].astype(F32) + b[0].astype(F32)) + b[1].astype(F32)) + b[2].astype(F32)

    tile = lambda p: (p.shape[1] // steps, p.shape[2])
    grid_spec = pltpu.PrefetchScalarGridSpec(
        num_scalar_prefetch=1, grid=(steps,),
        in_specs=[pl.BlockSpec((1,) + tile(p), lambda i, ids_ref: (ids_ref[0], i, 0)) for p in chip_sums]
        + [pl.BlockSpec((3,) + tile(p), lambda i, ids_ref: (0, i, 0)) for p in chip_sums],
        out_specs=[pl.BlockSpec(tile(p), lambda i, ids_ref: (ids_ref[1] * steps + i, 0)) for p in chip_sums])
    return pl.pallas_call(
        body, grid_spec=grid_spec, out_shape=[jax.ShapeDtypeStruct((2 * p.shape[1], p.shape[2]), F32) for p in chip_sums],
        name=name, compiler_params=_cparams(1))(chip_core, *chip_sums, *from_chips)


def _sum8_call(parts):
    n, rows, cols = parts.shape
    tr = _row_tile(rows)

    def body(p_ref, o_ref):
        acc = p_ref[0]
        for k in range(1, n):
            acc = acc + p_ref[k]
        o_ref[...] = acc

    return pl.pallas_call(
        body, grid=(rows // tr,), in_specs=[_bs((n, tr, cols), lambda i: (0, i, 0))], out_specs=_bs((tr, cols), lambda i: (i, 0)),
        out_shape=jax.ShapeDtypeStruct((rows, cols), F32), name="sum_small_grads", compiler_params=_cparams(1))(parts)


_ANY = pl.BlockSpec(memory_space=pl.ANY)


def _all_gather8(blk, name):
    rows, cols = blk.shape

    def body(x_ref, out_ref, send_sems, recv_sems, local_sem):
        x, y, c = lax.axis_index("x"), lax.axis_index("y"), lax.axis_index("c")
        me, sibling = (x, y, c), (x, y, 1 - c)
        chips = [(1 - x, y), (x, 1 - y), (1 - x, 1 - y)]

        def slot(px, py, pc):
            return out_ref.at[4 * px + 2 * py + pc]

        def copy(k, block, to, src=None):
            return pltpu.make_async_remote_copy(
                src_ref=slot(*block) if src is None else src, dst_ref=slot(*block),
                send_sem=send_sems.at[k], recv_sem=recv_sems.at[k], device_id=to, device_id_type=MESH_ID)

        mine = pltpu.make_async_copy(x_ref, slot(*me), local_sem)
        mine.start()
        first = [copy(0, me, sibling, src=x_ref)]
        first += [copy(1 + j, me, (*chip, c), src=x_ref) for j, chip in enumerate(chips)]
        for cp in first:
            cp.start()
        passed = [copy(4 + j, (*chip, c), sibling) for j, chip in enumerate(chips)]
        for j, chip in enumerate(chips):
            copy(1 + j, (*chip, c), me).wait_recv()
            passed[j].start()
        copy(0, sibling, me).wait_recv()
        for j, chip in enumerate(chips):
            copy(4 + j, (*chip, 1 - c), me).wait_recv()
        for cp in first + passed:
            cp.wait_send()
        mine.wait()

    return pl.pallas_call(
        body, out_shape=jax.ShapeDtypeStruct((8, rows, cols), blk.dtype), in_specs=[_ANY], out_specs=_ANY,
        scratch_shapes=[pltpu.SemaphoreType.DMA((7,)), pltpu.SemaphoreType.DMA((7,)), pltpu.SemaphoreType.DMA],
        name=name)(blk)


def _half_rows(ref, lead, half, which):
    rows = pl.ds(pl.multiple_of(half * which, half), half)
    return ref.at[rows] if lead is None else ref.at[lead, rows]


def _gather_layer_call(layer, shards, name):
    n = len(shards)
    half = [s.shape[1] // 2 for s in shards]

    def body(*refs):
        srcs, outs = refs[:n], refs[n:2 * n]
        send_sems, recv_sems, local_sems = refs[2 * n:]
        x, y, c = lax.axis_index("x"), lax.axis_index("y"), lax.axis_index("c")
        sibling = (x, y, 1 - c)
        chips = [(1 - x, y), (x, 1 - y), (1 - x, 1 - y)]

        def slot(t, px, py, pc):
            return _half_rows(outs[t], 2 * px + py, half[t], pc)

        def copy(t, k, block, to, src=None):
            return pltpu.make_async_remote_copy(
                src_ref=slot(t, *block) if src is None else src, dst_ref=slot(t, *block),
                send_sem=send_sems.at[7 * t + k], recv_sem=recv_sems.at[7 * t + k], device_id=to, device_id_type=MESH_ID)

        mine = [_half_rows(srcs[t], layer, half[t], c) for t in range(n)]
        local = [pltpu.make_async_copy(mine[t], slot(t, x, y, c), local_sems.at[t]) for t in range(n)]
        for cp in local:
            cp.start()
        first = []
        for t in range(n):
            first.append(copy(t, 0, (x, y, c), sibling, src=mine[t]))
            first += [copy(t, 1 + j, (x, y, c), (*chip, c), src=mine[t]) for j, chip in enumerate(chips)]
        for cp in first:
            cp.start()
        passed = []
        for j, chip in enumerate(chips):
            for t in range(n):
                copy(t, 1 + j, (*chip, c), (x, y, c)).wait_recv()
                passed.append(copy(t, 4 + j, (*chip, c), sibling))
                passed[-1].start()
        for t in range(n):
            copy(t, 0, (x, y, 1 - c), (x, y, c)).wait_recv()
            for j, chip in enumerate(chips):
                copy(t, 4 + j, (*chip, 1 - c), (x, y, c)).wait_recv()
        for cp in first + passed:
            cp.wait_send()
        for cp in local:
            cp.wait()

    return pl.pallas_call(
        body, out_shape=[jax.ShapeDtypeStruct((N_CHIPS,) + s.shape[1:], s.dtype) for s in shards],
        in_specs=[_ANY] * n, out_specs=[_ANY] * n,
        scratch_shapes=[pltpu.SemaphoreType.DMA((7 * n,)), pltpu.SemaphoreType.DMA((7 * n,)), pltpu.SemaphoreType.DMA((n,))],
        name=name)(*shards)


_HBM = pl.BlockSpec(memory_space=pltpu.HBM)
_SEM = pl.BlockSpec(memory_space=pltpu.SEMAPHORE)
_ORDERED_EFFECT = pltpu.CompilerParams(has_side_effects=pltpu.SideEffectType.DATAFLOW_SIDE_EFFECTING)


_VMEM = pl.BlockSpec(memory_space=pltpu.VMEM)
_TOKEN = jax.ShapeDtypeStruct((8, LANES), F32)


def _in_hbm(a):
    return pltpu.with_memory_space_constraint(a, pltpu.HBM)


def _tie(small, token):
    return small + token[0:1, 0:1].reshape((1,) * small.ndim)


def _pair_exchange_start_call(grads, name):
    n = len(grads)
    half = [g.shape[1] // 2 for g in grads]

    def body(*refs):
        srcs, outs = refs[:n], refs[n:2 * n]
        send_sems, recv_sems, token = refs[2 * n:]
        x, y, c = lax.axis_index("x"), lax.axis_index("y"), lax.axis_index("c")
        for t in range(n):
            pltpu.make_async_remote_copy(
                src_ref=srcs[t].at[:, pl.ds(pl.multiple_of(half[t] * (1 - c), half[t]), half[t])], dst_ref=outs[t],
                send_sem=send_sems.at[t], recv_sem=recv_sems.at[t], device_id=(x, y, 1 - c), device_id_type=MESH_ID).start()
        token[...] = jnp.zeros_like(token)

    dma = pltpu.SemaphoreType.DMA
    return pl.pallas_call(
        body, out_shape=[pltpu.HBM((g.shape[0], g.shape[1] // 2, g.shape[2]), g.dtype) for g in grads] + [dma((n,)), dma((n,)), _TOKEN],
        in_specs=[_HBM] * n, out_specs=[_HBM] * n + [_SEM, _SEM, _VMEM], name=name, compiler_params=_ORDERED_EFFECT,
    )(*[_in_hbm(g) for g in grads])


def _pair_exchange_finish_call(grads, bufs, send_sems, recv_sems, after, name):
    n = len(grads)
    after = list(after)
    half = [g.shape[1] // 2 for g in grads]

    def body(*refs):
        srcs, ins, send_ref, recv_ref = refs[:n], refs[n:2 * n], refs[2 * n], refs[2 * n + 1]
        x, y, c = lax.axis_index("x"), lax.axis_index("y"), lax.axis_index("c")
        for t in range(n):
            pltpu.make_async_remote_copy(
                src_ref=srcs[t].at[:, pl.ds(pl.multiple_of(half[t] * (1 - c), half[t]), half[t])], dst_ref=ins[t],
                send_sem=send_ref.at[t], recv_sem=recv_ref.at[t], device_id=(x, y, 1 - c), device_id_type=MESH_ID).wait()

    return pl.pallas_call(
        body, out_shape=[pltpu.HBM(b.shape, b.dtype) for b in bufs],
        in_specs=[_HBM] * (2 * n) + [_SEM, _SEM] + [_ANY] * len(after), out_specs=[_HBM] * n,
        input_output_aliases={n + t: t for t in range(n)}, name=name, compiler_params=_ORDERED_EFFECT,
    )(*[_in_hbm(g) for g in grads], *bufs, send_sems, recv_sems, *after)


def _chip_scatter_start_call(chip_sums, name):
    n = len(chip_sums)

    def body(*refs):
        srcs, outs = refs[:n], refs[n:2 * n]
        send_sems, recv_sems, token = refs[2 * n:]
        x, y, c = lax.axis_index("x"), lax.axis_index("y"), lax.axis_index("c")
        chips = [(1 - x, y), (x, 1 - y), (1 - x, 1 - y)]
        for k, (cx, cy) in enumerate(chips):
            for t in range(n):
                pltpu.make_async_remote_copy(
                    src_ref=srcs[t].at[2 * cx + cy], dst_ref=outs[t].at[k], send_sem=send_sems.at[3 * t + k],
                    recv_sem=recv_sems.at[3 * t + k], device_id=(cx, cy, c), device_id_type=MESH_ID).start()
        token[...] = jnp.zeros_like(token)

    dma = pltpu.SemaphoreType.DMA
    return pl.pallas_call(
        body, out_shape=[pltpu.HBM((3,) + p.shape[1:], p.dtype) for p in chip_sums] + [dma((3 * n,)), dma((3 * n,)), _TOKEN],
        in_specs=[_HBM] * n, out_specs=[_HBM] * n + [_SEM, _SEM, _VMEM], name=name, compiler_params=_ORDERED_EFFECT,
    )(*[_in_hbm(p) for p in chip_sums])


def _chip_scatter_finish_call(chip_sums, bufs, send_sems, recv_sems, after, name):
    n = len(chip_sums)
    after = list(after)

    def body(*refs):
        srcs, ins, send_ref, recv_ref = refs[:n], refs[n:2 * n], refs[2 * n], refs[2 * n + 1]
        x, y, c = lax.axis_index("x"), lax.axis_index("y"), lax.axis_index("c")
        chips = [(1 - x, y), (x, 1 - y), (1 - x, 1 - y)]
        for k, (cx, cy) in enumerate(chips):
            for t in range(n):
                pltpu.make_async_remote_copy(
                    src_ref=srcs[t].at[2 * cx + cy], dst_ref=ins[t].at[k], send_sem=send_ref.at[3 * t + k],
                    recv_sem=recv_ref.at[3 * t + k], device_id=(cx, cy, c), device_id_type=MESH_ID).wait()

    return pl.pallas_call(
        body, out_shape=[pltpu.HBM(b.shape, b.dtype) for b in bufs],
        in_specs=[_HBM] * (2 * n) + [_SEM, _SEM] + [_ANY] * len(after), out_specs=[_HBM] * n,
        input_output_aliases={n + t: t for t in range(n)}, name=name, compiler_params=_ORDERED_EFFECT,
    )(*[_in_hbm(p) for p in chip_sums], *bufs, send_sems, recv_sems, *after)


def _place_own_call(layer, shards, chip_core, name):
    n = len(shards)

    def body(ids_ref, *refs):
        for t in range(n):
            refs[n + t][...] = refs[t][...]

    def blk(s):
        return (1, s.shape[1] // 2) + s.shape[2:]

    def imap_in(s):
        pad = (0,) * (s.ndim - 2)
        return lambda i, ids_ref: (layer, ids_ref[1]) + pad

    def imap_out(s):
        pad = (0,) * (s.ndim - 2)
        return lambda i, ids_ref: (ids_ref[0], ids_ref[1]) + pad

    grid_spec = pltpu.PrefetchScalarGridSpec(
        num_scalar_prefetch=1, grid=(1,), in_specs=[pl.BlockSpec(blk(s), imap_in(s)) for s in shards],
        out_specs=[pl.BlockSpec(blk(s), imap_out(s)) for s in shards])
    return pl.pallas_call(
        body, grid_spec=grid_spec, out_shape=[jax.ShapeDtypeStruct((N_CHIPS,) + s.shape[1:], s.dtype) for s in shards],
        name=name, compiler_params=_cparams(1))(chip_core, *shards)


def _gather_start_call(layer, shards, bufs, after, name):
    n = len(shards)
    half = [s.shape[1] // 2 for s in shards]

    def body(*refs):
        srcs, outs = refs[:n], refs[2 * n + 1:3 * n + 1]
        send_sems, recv_sib, recv_ici, token = refs[3 * n + 1:]
        x, y, c = lax.axis_index("x"), lax.axis_index("y"), lax.axis_index("c")
        chips = [(1 - x, y), (x, 1 - y), (1 - x, 1 - y)]
        for t in range(n):
            mine = _half_rows(srcs[t], layer, half[t], c)
            dst = _half_rows(outs[t], 2 * x + y, half[t], c)
            pltpu.make_async_remote_copy(src_ref=mine, dst_ref=dst, send_sem=send_sems.at[4 * t], recv_sem=recv_sib.at[t],
                                         device_id=(x, y, 1 - c), device_id_type=MESH_ID).start()
            for j, chip in enumerate(chips):
                pltpu.make_async_remote_copy(src_ref=mine, dst_ref=dst, send_sem=send_sems.at[4 * t + 1 + j],
                                             recv_sem=recv_ici.at[3 * t + j], device_id=(*chip, c), device_id_type=MESH_ID).start()
        token[...] = jnp.zeros_like(token)

    dma = pltpu.SemaphoreType.DMA
    return pl.pallas_call(
        body, out_shape=[pltpu.HBM(b.shape, b.dtype) for b in bufs] + [dma((4 * n,)), dma((n,)), dma((3 * n,)), _TOKEN],
        in_specs=[_HBM] * (2 * n) + [_ANY], out_specs=[_HBM] * n + [_SEM] * 3 + [_VMEM],
        input_output_aliases={n + t: t for t in range(n)}, name=name, compiler_params=_ORDERED_EFFECT,
    )(*[_in_hbm(s) for s in shards], *[_in_hbm(b) for b in bufs], after)


def _gather_forward_call(bufs, recv_ici, after, name):
    n = len(bufs)
    half = [b.shape[1] // 2 for b in bufs]

    def body(*refs):
        ins, recv_ici_ref = refs[:n], refs[n]
        outs = refs[n + 2:2 * n + 2]
        send_fwd, recv_fwd, token = refs[2 * n + 2:]
        x, y, c = lax.axis_index("x"), lax.axis_index("y"), lax.axis_index("c")
        chips = [(1 - x, y), (x, 1 - y), (1 - x, 1 - y)]
        for j, (cx, cy) in enumerate(chips):
            for t in range(n):
                landed = _half_rows(ins[t], 2 * cx + cy, half[t], c)
                dst = _half_rows(outs[t], 2 * cx + cy, half[t], c)
                pltpu.make_async_remote_copy(src_ref=landed, dst_ref=landed, send_sem=send_fwd.at[3 * t + j],
                                             recv_sem=recv_ici_ref.at[3 * t + j], device_id=(cx, cy, c),
                                             device_id_type=MESH_ID).wait_recv()
                pltpu.make_async_remote_copy(src_ref=landed, dst_ref=dst, send_sem=send_fwd.at[3 * t + j],
                                             recv_sem=recv_fwd.at[3 * t + j], device_id=(x, y, 1 - c),
                                             device_id_type=MESH_ID).start()
        token[...] = jnp.zeros_like(token)

    dma = pltpu.SemaphoreType.DMA
    return pl.pallas_call(
        body, out_shape=[pltpu.HBM(b.shape, b.dtype) for b in bufs] + [dma((3 * n,)), dma((3 * n,)), _TOKEN],
        in_specs=[_HBM] * n + [_SEM, _ANY], out_specs=[_HBM] * n + [_SEM] * 2 + [_VMEM],
        input_output_aliases={t: t for t in range(n)}, name=name, compiler_params=_ORDERED_EFFECT,
    )(*bufs, recv_ici, after)


def _gather_finish_call(layer, shards, bufs, send_sems, recv_sib, send_fwd, recv_fwd, after, name):
    n = len(bufs)
    half = [b.shape[1] // 2 for b in bufs]

    def body(*refs):
        srcs, ins = refs[:n], refs[n:2 * n]
        send_ref, recv_sib_ref, send_fwd_ref, recv_fwd_ref = refs[2 * n:2 * n + 4]
        x, y, c = lax.axis_index("x"), lax.axis_index("y"), lax.axis_index("c")
        chips = [(1 - x, y), (x, 1 - y), (1 - x, 1 - y)]
        sibling = (x, y, 1 - c)
        for t in range(n):
            mine = _half_rows(srcs[t], layer, half[t], c)
            for k in range(4):
                pltpu.make_async_remote_copy(src_ref=mine, dst_ref=mine, send_sem=send_ref.at[4 * t + k],
                                             recv_sem=recv_sib_ref.at[t], device_id=sibling, device_id_type=MESH_ID).wait_send()
            from_sibling = _half_rows(ins[t], 2 * x + y, half[t], 1 - c)
            pltpu.make_async_remote_copy(src_ref=from_sibling, dst_ref=from_sibling, send_sem=send_ref.at[4 * t],
                                         recv_sem=recv_sib_ref.at[t], device_id=sibling, device_id_type=MESH_ID).wait_recv()
            for j, (cx, cy) in enumerate(chips):
                sent = _half_rows(ins[t], 2 * cx + cy, half[t], c)
                passed = _half_rows(ins[t], 2 * cx + cy, half[t], 1 - c)
                pltpu.make_async_remote_copy(src_ref=sent, dst_ref=passed, send_sem=send_fwd_ref.at[3 * t + j],
                                             recv_sem=recv_fwd_ref.at[3 * t + j], device_id=sibling, device_id_type=MESH_ID).wait()

    return pl.pallas_call(
        body, out_shape=[pltpu.HBM(b.shape, b.dtype) for b in bufs],
        in_specs=[_HBM] * (2 * n) + [_SEM] * 4 + [_ANY], out_specs=[_HBM] * n,
        input_output_aliases={n + t: t for t in range(n)}, name=name, compiler_params=_ORDERED_EFFECT,
    )(*[_in_hbm(s) for s in shards], *bufs, send_sems, recv_sib, send_fwd, recv_fwd, after)


def _pair_exchange_call(grads, name):
    n = len(grads)
    half = [g.shape[1] // 2 for g in grads]

    def body(*refs):
        srcs, outs, send_sems, recv_sems = refs[:n], refs[n:2 * n], refs[2 * n], refs[2 * n + 1]
        x, y, c = lax.axis_index("x"), lax.axis_index("y"), lax.axis_index("c")
        copies = [pltpu.make_async_remote_copy(
            src_ref=srcs[t].at[:, pl.ds(pl.multiple_of(half[t] * (1 - c), half[t]), half[t])], dst_ref=outs[t],
            send_sem=send_sems.at[t], recv_sem=recv_sems.at[t], device_id=(x, y, 1 - c), device_id_type=MESH_ID) for t in range(n)]
        for cp in copies:
            cp.start()
        for cp in copies:
            cp.wait()

    return pl.pallas_call(
        body, out_shape=[jax.ShapeDtypeStruct((g.shape[0], g.shape[1] // 2, g.shape[2]), g.dtype) for g in grads],
        in_specs=[_ANY] * n, out_specs=[_ANY] * n,
        scratch_shapes=[pltpu.SemaphoreType.DMA((n,)), pltpu.SemaphoreType.DMA((n,))], name=name)(*grads)


def _chip_scatter_call(chip_sums, name):
    n = len(chip_sums)

    def body(*refs):
        srcs, outs, send_sems, recv_sems = refs[:n], refs[n:2 * n], refs[2 * n], refs[2 * n + 1]
        x, y, c = lax.axis_index("x"), lax.axis_index("y"), lax.axis_index("c")
        chips = [(1 - x, y), (x, 1 - y), (1 - x, 1 - y)]
        copies = [pltpu.make_async_remote_copy(
            src_ref=srcs[t].at[2 * cx + cy], dst_ref=outs[t].at[k], send_sem=send_sems.at[3 * t + k],
            recv_sem=recv_sems.at[3 * t + k], device_id=(cx, cy, c), device_id_type=MESH_ID)
            for k, (cx, cy) in enumerate(chips) for t in range(n)]
        for cp in copies:
            cp.start()
        for cp in copies:
            cp.wait()

    return pl.pallas_call(
        body, out_shape=[jax.ShapeDtypeStruct((3,) + p.shape[1:], p.dtype) for p in chip_sums],
        in_specs=[_ANY] * n, out_specs=[_ANY] * n,
        scratch_shapes=[pltpu.SemaphoreType.DMA((3 * n,)), pltpu.SemaphoreType.DMA((3 * n,))], name=name)(*chip_sums)


def _pair_gather_call(bufs, name):
    n = len(bufs)
    half = [b.shape[0] // 2 for b in bufs]

    def body(*refs):
        srcs, outs, send_sems, recv_sems = refs[:n], refs[n:2 * n], refs[2 * n], refs[2 * n + 1]
        x, y, c = lax.axis_index("x"), lax.axis_index("y"), lax.axis_index("c")
        for t in range(n):
            pltpu.make_async_remote_copy(
                src_ref=_half_rows(srcs[t], None, half[t], c), dst_ref=_half_rows(outs[t], None, half[t], c),
                send_sem=send_sems.at[t], recv_sem=recv_sems.at[t], device_id=(x, y, 1 - c), device_id_type=MESH_ID).start()
        for t in range(n):
            pltpu.make_async_remote_copy(
                src_ref=_half_rows(srcs[t], None, half[t], c), dst_ref=_half_rows(outs[t], None, half[t], 1 - c),
                send_sem=send_sems.at[t], recv_sem=recv_sems.at[t], device_id=(x, y, 1 - c), device_id_type=MESH_ID).wait()

    return pl.pallas_call(
        body, out_shape=[jax.ShapeDtypeStruct(b.shape, b.dtype) for b in bufs], in_specs=[_ANY] * n, out_specs=[_ANY] * n,
        input_output_aliases={t: t for t in range(n)},
        scratch_shapes=[pltpu.SemaphoreType.DMA((n,)), pltpu.SemaphoreType.DMA((n,))], name=name)(*bufs)


def _pack_rows(flats, dtype, row_multiple):
    flat = jnp.concatenate([f.reshape(-1).astype(dtype) for f in flats])
    n = flat.shape[0]
    rows = -(-n // PACK_W)
    rows = -(-rows // row_multiple) * row_multiple
    return jnp.pad(flat, (0, rows * PACK_W - n)).reshape(rows, PACK_W)


def _unpack(flat, shapes):
    out, off = [], 0
    for shp in shapes:
        n = math.prod(shp)
        out.append(flat[off:off + n].reshape(shp))
        off += n
    return out


def _f32_as_mm_bits(a):
    return lax.bitcast_convert_type(a, jnp.bfloat16).reshape(-1)


def _mm_bits_as_f32(flat, shape):
    return lax.bitcast_convert_type(flat.reshape(-1, 2), F32).reshape(shape)


_W_IN_SEGMENTS = ((R_ML, R_END, OFF_ML), (R_SG, R_ML, OFF_SG), (R_CV, R_SGI, OFF_CV), (R_SGI, R_MQ, OFF_SGI), (R_MQ, R_SG, OFF_MQ),
                  (R_CQ, R_CKV, OFF_CQ), (R_CKV, R_KR, OFF_CKV), (R_KR, R_CV, OFF_KR + NOPE))
W_IN_SHARD = R_END // N_CHIPS


def _realign_call(wg):
    tr = 128

    def body(w_ref, o_ref):
        pieces, pos = [], 0
        for r0, r1, a0 in _W_IN_SEGMENTS:
            if a0 > pos:
                pieces.append(jnp.zeros((tr, a0 - pos), o_ref.dtype))
            while r0 < r1:
                j = r0 // W_IN_SHARD
                hi = min(r1, (j + 1) * W_IN_SHARD)
                pieces.append(w_ref[j, :, r0 - j * W_IN_SHARD:hi - j * W_IN_SHARD])
                a0, r0 = a0 + hi - r0, hi
            pos = a0
        pieces.append(jnp.zeros((tr, NP - pos), o_ref.dtype))
        o_ref[...] = jnp.concatenate(pieces, axis=1)

    return pl.pallas_call(
        body, grid=(D // tr,), in_specs=[_bs((N_CHIPS, tr, W_IN_SHARD), lambda i: (0, i, 0))],
        out_specs=_bs((tr, NP), lambda i: (i, 0)), out_shape=jax.ShapeDtypeStruct((D, NP), wg.dtype),
        name="w_in_realign", compiler_params=_cparams(1))(wg)


def _unalign_call(dw):
    tr = 128
    by_ref = sorted(_W_IN_SEGMENTS)

    def body(dw_ref, o_ref):
        for j in range(N_CHIPS):
            lo_j, hi_j = j * W_IN_SHARD, (j + 1) * W_IN_SHARD
            pieces = []
            for r0, r1, a0 in by_ref:
                lo, hi = max(r0, lo_j), min(r1, hi_j)
                if lo < hi:
                    pieces.append(dw_ref[:, a0 + lo - r0:a0 + hi - r0])
            o_ref[j] = jnp.concatenate(pieces, axis=1)

    return pl.pallas_call(
        body, grid=(D // tr,), in_specs=[_bs((tr, NP), lambda i: (i, 0))],
        out_specs=_bs((N_CHIPS, tr, W_IN_SHARD), lambda i: (0, i, 0)),
        out_shape=jax.ShapeDtypeStruct((N_CHIPS, D, W_IN_SHARD), dw.dtype), name="w_in_unalign", compiler_params=_cparams(1))(dw)


def _w_in_to_aligned(w):
    z = lambda n: jnp.zeros((w.shape[0], n), w.dtype)
    return jnp.concatenate([w[:, R_ML:R_END], w[:, R_SG:R_ML], w[:, R_CV:R_SGI], w[:, R_SGI:R_MQ], w[:, R_MQ:R_SG],
                            w[:, R_CQ:R_CKV], w[:, R_CKV:R_KR], z(NOPE), w[:, R_KR:R_CV], z(LANES - QKH)], axis=1)


def _w_in_from_aligned(wa):
    return jnp.concatenate([wa[:, OFF_CQ:OFF_CKV], wa[:, OFF_CKV:OFF_KR], wa[:, OFF_KR + NOPE:OFF_KR + QKH], wa[:, OFF_CV:OFF_SGI],
                            wa[:, OFF_SGI:OFF_MQ], wa[:, OFF_MQ:OFF_CQ], wa[:, OFF_SG:OFF_CV], wa[:, OFF_ML:OFF_SG]], axis=1)


def _wuq_to_heads(w):
    w3 = w.reshape(QL, H, QKH)
    w3 = jnp.pad(w3, ((0, 0), (0, 0), (0, LANES - QKH)))
    return jnp.transpose(w3, (1, 0, 2))


def _wuq_from_heads(wh):
    return jnp.transpose(wh[:, :, :QKH], (1, 0, 2)).reshape(QL, H * QKH)


def _wukv_to_heads(w):
    w3 = w.reshape(KVL, H, NOPE + VH)
    wkn = jnp.transpose(jnp.pad(w3[:, :, :NOPE], ((0, 0), (0, 0), (0, LANES - NOPE))), (1, 0, 2))
    wv3 = w3[:, :, NOPE:]
    z = jnp.zeros((KVL, VH), w.dtype)
    cols = []
    for h in range(H):
        cols += [wv3[:, h], z] if h % 2 == 0 else [z, wv3[:, h]]
    return wkn, jnp.concatenate(cols, axis=1)


def _wukv_from_heads(wkn, wv):
    kn = jnp.transpose(wkn[:, :, :NOPE], (1, 0, 2))
    vs = jnp.stack([wv[:, LANES * h + VH * (h % 2):LANES * h + VH * (h % 2) + VH] for h in range(H)], axis=1)
    return jnp.concatenate([kn, vs], axis=2).reshape(KVL, H * (NOPE + VH))


def _layer_fwd(x, mem, tabs, p):
    proj, h = _proj_call(x, p["norm_g"], p["w_in"])
    q, k, v = _mla_prep_call(proj, tabs, p["cq_g"], p["ckv_g"], p["qg"], p["kg"], p["wuq"], p["wkn"], p["wv"])
    ya, attn_o, attn_lse = _attn_call(q, k, v, proj)
    bm = p["bm"]
    if p.get("after_attn") is not None:
        bm = _tie(bm, p["after_attn"](ya))
    yb = _conv_call(proj, p["conv_w"], p["conv_b"])
    yc = _sg_call(proj, p["ln_g"], p["ln_b"], p["ws"], p["bs"])
    mk, mv = _memkv_call(mem, p["mem_g"], p["wm"], p["mkg"])
    yd = _mem_call(proj, mk, mv, p["mqg"])
    out = _merge_call((ya, yb, yc, yd), proj, bm, p["wb"], p["wo"], x)
    return out, dict(x=x, proj=proj, h=h, q=q, k=k, v=v, attn_o=attn_o, attn_lse=attn_lse, ys=(ya, yb, yc, yd), mk=mk, mv=mv)


def _layer_bwd(dout, mem, tabs, p, sv, start_after=None, after_mla=None, on_grads=None):
    proj = sv["proj"]
    bm = p["bm"] if start_after is None else _tie(p["bm"], start_after)
    dya, dyb, dyc, dyd, dml, dbm, dwb, dwo = _merge_bwd_call(sv["ys"], proj, bm, p["wb"], p["wo"], dout)
    dq, dk, dv, dsg_a = _attn_bwd_call(sv["q"], sv["k"], sv["v"], proj, dya, sv["attn_o"], sv["attn_lse"])
    dcq, dckv, dkr, dcqg, dckvg, dqg, dkg, dwuq, dwkn, dwv = _mla_prep_bwd_call(
        proj, tabs, p["cq_g"], p["ckv_g"], p["qg"], p["kg"], p["wuq"], p["wkn"], p["wv"], dq, dk, dv)
    if after_mla is not None:
        after_mla(dcq)
    dbg, dcg, dxi, dsg_b, dcw, dcb = _conv_bwd_call(proj, p["conv_w"], p["conv_b"], dyb)
    du, dvv, dsg_c, dlg, dlb, dws, dbs = _sg_bwd_call(proj, p["ln_g"], p["ln_b"], p["ws"], p["bs"], dyc)
    dmq, dsg_d, dmk, dmv, dmqg = _mem_bwd_call(proj, sv["mk"], sv["mv"], p["mqg"], dyd)
    dmem_g, dwm, dmkg = _memkv_bwd_call(mem, p["mem_g"], p["wm"], p["mkg"], dmk, dmv)
    dproj = jnp.concatenate([dml, dsg_a, dsg_b, dsg_c, dsg_d, dbg, dcg, dxi, du, dvv, dmq, dcq, dckv, dkr], axis=1)
    dw_in = _dw_call(sv["h"], dproj)
    grads = dict(cq_norm_g=dcqg[0], ckv_norm_g=dckvg[0], mla_q_norm_g=dqg[0, :QKH], mla_k_norm_g=dkg[0, :QKH],
                 conv_w=dcw, conv_b=dcb[0], sg_ln_g=dlg[0], sg_ln_b=dlb[0], w_spatial=dws, b_spatial=dbs[:, :, 0],
                 mem_norm_g=dmem_g[0], mem_q_norm_g=dmqg[0], mem_k_norm_g=dmkg[0], b_merge=dbm,
                 w_in_aligned=dw_in, wuq_heads=dwuq, wkn_heads=dwkn, wv_heads=dwv, w_mem_kv=dwm, w_branch_chips=dwb, w_out=dwo)
    norm_g = p["norm_g"]
    if on_grads is not None:
        norm_g = _tie(norm_g, on_grads(grads))
    dx, dnorm_g = _dh_call(dproj, p["w_in"], sv["x"], norm_g, dout)
    grads["norm_g"] = dnorm_g[0]
    return dx, grads


def _chips_to_cols(a):
    return jnp.concatenate([a[j] for j in range(N_CHIPS)], axis=1)


def _cols_to_chips(a):
    cols = a.shape[1] // N_CHIPS
    return jnp.stack([a[:, cols * j:cols * (j + 1)] for j in range(N_CHIPS)])


def _layer_params(l, rep, gathered, conv_w, b_merge):
    pad_g = lambda g: jnp.pad(g, (0, LANES - QKH)).reshape(1, LANES)
    wkn, wv = _wukv_to_heads(_chips_to_cols(gathered["w_ukv"]))
    return dict(
        norm_g=rep["norm_g"][l].reshape(1, D), w_in=_realign_call(gathered["w_in"]),
        cq_g=rep["cq_norm_g"][l].reshape(1, QL), ckv_g=rep["ckv_norm_g"][l].reshape(1, KVL),
        qg=pad_g(rep["mla_q_norm_g"][l]), kg=pad_g(rep["mla_k_norm_g"][l]),
        wuq=_wuq_to_heads(_chips_to_cols(gathered["w_uq"])), wkn=wkn, wv=wv,
        conv_w=conv_w, conv_b=rep["conv_b"][l].reshape(1, CW),
        ln_g=rep["sg_ln_g"][l].reshape(1, SGW), ln_b=rep["sg_ln_b"][l].reshape(1, SGW),
        ws=rep["w_spatial"][l], bs=rep["b_spatial"][l].reshape(SGG, SGC, 1),
        mem_g=rep["mem_norm_g"][l].reshape(1, D), wm=gathered["w_mem_kv"].reshape(D, 2 * MH * MHD),
        mqg=rep["mem_q_norm_g"][l].reshape(1, MHD), mkg=rep["mem_k_norm_g"][l].reshape(1, MHD),
        bm=b_merge, wb=gathered["w_branch"], wo=gathered["w_out"].reshape(D, D))


def _forward_backward(x, mem, pos, target, params, bwd_hooks=None):
    tabs = _rope_tables(pos)
    params = list(params)
    saved = []
    act = x
    for l in range(DEPTH):
        if callable(params[l]):
            params[l] = params[l](saved[-1], act)
        act, sv = _layer_fwd(act, mem, tabs, params[l])
        saved.append(sv)
    dy, sq = _loss_call(act, target)
    grads = [None] * DEPTH
    token = None
    for l in reversed(range(DEPTH)):
        hooks = dict(bwd_hooks[l]) if bwd_hooks else {}
        after_layer = hooks.pop("after_layer", None)
        dy, grads[l] = _layer_bwd(dy, mem, tabs, params[l], saved[l], start_after=token, **hooks)
        token = after_layer(dy) if after_layer is not None else None
    return sq, dy, grads


_SHARDED_MM = ("w_in", "w_branch", "w_out", "w_mem_kv", "w_uq", "w_ukv")
_SHARDED_F32 = ("conv_w", "b_merge")
_REPLICATED = ("norm_g", "cq_norm_g", "ckv_norm_g", "mla_q_norm_g", "mla_k_norm_g", "conv_b", "sg_ln_g", "sg_ln_b",
               "w_spatial", "b_spatial", "mem_norm_g", "mem_q_norm_g", "mem_k_norm_g")
_ALL_REDUCED = _REPLICATED + _SHARDED_F32
_WEIGHTS = ("norm_g", "w_in", "cq_norm_g", "ckv_norm_g", "w_uq", "w_ukv", "mla_q_norm_g", "mla_k_norm_g", "conv_w", "conv_b",
            "sg_ln_g", "sg_ln_b", "w_spatial", "b_spatial", "mem_norm_g", "w_mem_kv", "mem_q_norm_g", "mem_k_norm_g",
            "b_merge", "w_branch", "w_out")
_BIG = ("w_in", "w_uq", "w_ukv", "w_mem_kv", "w_branch", "w_out")
_SMALL = tuple(n for n in _WEIGHTS if n not in _BIG)


def _gather_small_sharded(w):
    names = _SHARDED_F32
    packed = _pack_rows([w[n] for n in names], F32, 8)
    got = _all_gather8(packed, "gather_small_weights")
    per_chip = [_unpack(got[2 * j].reshape(-1), [w[n].shape for n in names]) for j in range(N_CHIPS)]
    return {n: jnp.concatenate([per_chip[j][t] for j in range(N_CHIPS)], axis=2) for t, n in enumerate(names)}


def _gather_layer(l, shards):
    srcs = [shards[n] for n in _SHARDED_MM]
    return dict(zip(_SHARDED_MM, _gather_layer_call(l, srcs, "gather_weights_l%d" % l)))


class _ReduceScatter:
    def __init__(self, layer):
        self.tag = "rs_l%d_" % layer

    def exchange(self, grads):
        self.tensors = [
            _unalign_call(grads["w_in_aligned"]),
            grads["w_branch_chips"].reshape(N_CHIPS, NB * BW, D // N_CHIPS),
            grads["w_out"].reshape(N_CHIPS, D // N_CHIPS, D),
            grads["w_mem_kv"].reshape(N_CHIPS, D // N_CHIPS, 2 * MH * MHD),
            _cols_to_chips(_wuq_from_heads(grads["wuq_heads"])),
            _cols_to_chips(_wukv_from_heads(grads["wkn_heads"], grads["wv_heads"])),
        ]
        n = len(self.tensors)
        out = _pair_exchange_start_call(self.tensors, self.tag + "exchange_start")
        self.ex_bufs, self.ex_send, self.ex_recv = out[:n], out[n], out[n + 1]
        return out[n + 2]

    def scatter(self, after):
        n = len(self.tensors)
        c = lax.axis_index("c")
        from_sibling = _pair_exchange_finish_call(self.tensors, self.ex_bufs, self.ex_send, self.ex_recv, after,
                                                  self.tag + "exchange_finish")
        self.chip_sums = _pair_sum_call(self.tensors, from_sibling, c.astype(jnp.int32).reshape(1), self.tag + "pair_sum")
        out = _chip_scatter_start_call(self.chip_sums, self.tag + "scatter_start")
        self.bufs, self.send_sems, self.recv_sems, self.token = out[:n], out[n], out[n + 1], out[n + 2]
        return self.token

    def finish(self, after):
        x, y, c = lax.axis_index("x"), lax.axis_index("y"), lax.axis_index("c")
        chip_core = jnp.stack([2 * x + y, c]).astype(jnp.int32)
        from_chips = _chip_scatter_finish_call(self.chip_sums, self.bufs, self.send_sems, self.recv_sems, after,
                                               self.tag + "scatter_finish")
        mine = _owner_sum_call(self.chip_sums, from_chips, chip_core, self.tag + "owner_sum")
        shard = dict(zip(_SHARDED_MM, _pair_gather_call(mine, self.tag + "pair_gather")))
        shard["w_branch"] = shard["w_branch"].reshape(NB, BW, D // N_CHIPS)
        return shard


def _all_reduce_small(g):
    packed = _pack_rows([g[n] for n in _ALL_REDUCED], F32, 64)
    got = _all_gather8(packed, "gather_small_grads")
    total = _sum8_call(got).reshape(-1)
    out = dict(zip(_ALL_REDUCED, _unpack(total, [g[n].shape for n in _ALL_REDUCED])))
    chip = 2 * lax.axis_index("x") + lax.axis_index("y")
    for n in _SHARDED_F32:
        size = out[n].shape[2] // N_CHIPS
        out[n] = lax.dynamic_slice_in_dim(out[n], chip * size, size, axis=2)
    return out


def _adamw_small(w, g, m, v, token):
    delta, new_m, new_v = {}, {}, {}
    shapes = [w[n].shape for n in _SMALL]
    pk = lambda t: _pack_rows([t[n] for n in _SMALL], F32, 64)
    d, nm, nv = _adamw_call(pk(w), _tie(pk(g), token), pk(m), pk(v), "adamw_small")
    for out, packed in ((delta, d), (new_m, nm), (new_v, nv)):
        out.update(zip(_SMALL, _unpack(packed.reshape(-1), shapes)))
    return delta, new_m, new_v


def kernel(x, mem, positions, norm_g, w_in, cq_norm_g, ckv_norm_g, w_uq, w_ukv, mla_q_norm_g, mla_k_norm_g, conv_w, conv_b, sg_ln_g, sg_ln_b, w_spatial, b_spatial, mem_norm_g, w_mem_kv, mem_q_norm_g, mem_k_norm_g, b_merge, w_branch, w_out, loss_target, m_norm_g, m_w_in, m_cq_norm_g, m_ckv_norm_g, m_w_uq, m_w_ukv, m_mla_q_norm_g, m_mla_k_norm_g, m_conv_w, m_conv_b, m_sg_ln_g, m_sg_ln_b, m_w_spatial, m_b_spatial, m_mem_norm_g, m_w_mem_kv, m_mem_q_norm_g, m_mem_k_norm_g, m_b_merge, m_w_branch, m_w_out, v_norm_g, v_w_in, v_cq_norm_g, v_ckv_norm_g, v_w_uq, v_w_ukv, v_mla_q_norm_g, v_mla_k_norm_g, v_conv_w, v_conv_b, v_sg_ln_g, v_sg_ln_b, v_w_spatial, v_b_spatial, v_mem_norm_g, v_w_mem_kv, v_mem_q_norm_g, v_mem_k_norm_g, v_b_merge, v_w_branch, v_w_out):
    w = dict(norm_g=norm_g, w_in=w_in, cq_norm_g=cq_norm_g, ckv_norm_g=ckv_norm_g, w_uq=w_uq, w_ukv=w_ukv,
             mla_q_norm_g=mla_q_norm_g, mla_k_norm_g=mla_k_norm_g, conv_w=conv_w, conv_b=conv_b, sg_ln_g=sg_ln_g,
             sg_ln_b=sg_ln_b, w_spatial=w_spatial, b_spatial=b_spatial, mem_norm_g=mem_norm_g, w_mem_kv=w_mem_kv,
             mem_q_norm_g=mem_q_norm_g, mem_k_norm_g=mem_k_norm_g, b_merge=b_merge, w_branch=w_branch, w_out=w_out)
    m = dict(norm_g=m_norm_g, w_in=m_w_in, cq_norm_g=m_cq_norm_g, ckv_norm_g=m_ckv_norm_g, w_uq=m_w_uq, w_ukv=m_w_ukv,
             mla_q_norm_g=m_mla_q_norm_g, mla_k_norm_g=m_mla_k_norm_g, conv_w=m_conv_w, conv_b=m_conv_b, sg_ln_g=m_sg_ln_g,
             sg_ln_b=m_sg_ln_b, w_spatial=m_w_spatial, b_spatial=m_b_spatial, mem_norm_g=m_mem_norm_g, w_mem_kv=m_w_mem_kv,
             mem_q_norm_g=m_mem_q_norm_g, mem_k_norm_g=m_mem_k_norm_g, b_merge=m_b_merge, w_branch=m_w_branch, w_out=m_w_out)
    v = dict(norm_g=v_norm_g, w_in=v_w_in, cq_norm_g=v_cq_norm_g, ckv_norm_g=v_ckv_norm_g, w_uq=v_w_uq, w_ukv=v_w_ukv,
             mla_q_norm_g=v_mla_q_norm_g, mla_k_norm_g=v_mla_k_norm_g, conv_w=v_conv_w, conv_b=v_conv_b, sg_ln_g=v_sg_ln_g,
             sg_ln_b=v_sg_ln_b, w_spatial=v_w_spatial, b_spatial=v_b_spatial, mem_norm_g=v_mem_norm_g, w_mem_kv=v_w_mem_kv,
             mem_q_norm_g=v_mem_q_norm_g, mem_k_norm_g=v_mem_k_norm_g, b_merge=v_b_merge, w_branch=v_w_branch, w_out=v_w_out)

    small = _gather_small_sharded(w)
    shards = {n: w[n].astype(MM) for n in _SHARDED_MM}
    srcs = [shards[n] for n in _SHARDED_MM]
    n_t = len(srcs)
    chip_core = jnp.stack([2 * lax.axis_index("x") + lax.axis_index("y"), lax.axis_index("c")]).astype(jnp.int32)
    gathered0 = _gather_layer(0, shards)
    bufs = _place_own_call(1, srcs, chip_core, "gather_l1_place_own")
    started = _gather_start_call(1, srcs, bufs, gathered0["w_ukv"], "gather_l1_start")
    bufs, send_sems, recv_sib, recv_ici = started[:n_t], started[n_t], started[n_t + 1], started[n_t + 2]
    passed = []

    def pass_on(ya0):
        passed.extend(_gather_forward_call(bufs, recv_ici, ya0, "gather_l1_forward"))
        return passed[n_t + 2]

    def layer1_params(saved0, act0):
        got = _gather_finish_call(1, srcs, passed[:n_t], send_sems, recv_sib, passed[n_t], passed[n_t + 1], act0, "gather_l1_finish")
        return _layer_params(1, w, dict(zip(_SHARDED_MM, got)), small["conv_w"][1], small["b_merge"][1])

    params0 = _layer_params(0, w, gathered0, small["conv_w"][0], small["b_merge"][0])
    params0["norm_g"] = _tie(params0["norm_g"], started[n_t + 3])
    params = [dict(params0, after_attn=pass_on), layer1_params]
    rs = [_ReduceScatter(l) for l in range(DEPTH)]
    shard_grads = {}
    hooks = [dict(on_grads=rs[0].exchange), dict(on_grads=rs[1].exchange, after_layer=lambda dy: rs[1].scatter([dy]))]
    sq, grad_x, layer_grads = _forward_backward(x[0], mem[0], positions[0], loss_target[0], params, hooks)
    loss = lax.psum(0.5 / D * jnp.sum(sq), ("x", "y", "c"))

    g = _all_reduce_small({n: jnp.stack([layer_grads[l][n] for l in range(DEPTH)]) for n in _ALL_REDUCED})
    scattering = rs[0].scatter([grad_x, g["norm_g"]])
    delta, new_m, new_v = _adamw_small(w, g, m, v, scattering)
    shard_grads[1] = rs[1].finish([scattering])
    as3d = lambda a: a.reshape(DEPTH, -1, a.shape[-1])
    big = lambda t: [as3d(t[n]) for n in _SHARDED_MM]
    as2d = lambda a: a.reshape(-1, a.shape[-1])
    upd1 = _adamw_layer_call(1, big(w), [as2d(shard_grads[1][n]) for n in _SHARDED_MM], big(m), big(v), None, [], "adamw_l1")
    shard_grads[0] = rs[0].finish([grad_x, upd1[0], delta["norm_g"]])
    upd = _adamw_layer_call(0, big(w), [as2d(shard_grads[0][n]) for n in _SHARDED_MM], big(m), big(v), upd1, [], "adamw_l0")
    for t, n in enumerate(_SHARDED_MM):
        g[n], delta[n], new_m[n], new_v[n] = [a.reshape(w[n].shape) for a in upd[4 * t:4 * t + 4]]
    return (loss, grad_x[None], *[g[n] for n in _WEIGHTS], *[delta[n] for n in _WEIGHTS],
            *[new_m[n] for n in _WEIGHTS], *[new_v[n] for n in _WEIGHTS])
```

```python
import functools
import math

import jax
import jax.numpy as jnp
from jax import lax
from jax.experimental import pallas as pl
from jax.experimental.pallas import tpu as pltpu

F32 = jnp.float32
MM = jnp.bfloat16

D = 1024
DEPTH = 2
EPS = 1e-6
H = 8
NOPE = 64
ROPE = 32
QKH = 96
VH = 64
QL = 256
KVL = 128
ROPE_THETA = 10000.0
CW = 512
SGW = 512
SGG = 4
SGC = 128
MH = 4
MHD = 128
NB = 4
BW = 512
NEG_INF = -1e30
LANES = 128
N_CHIPS = 4

R_CQ, R_CKV, R_KR, R_CV, R_SGI, R_MQ, R_SG, R_ML, R_END = 0, 256, 384, 416, 1952, 2976, 3488, 5536, 9632
OFF_ML, OFF_SG, OFF_CV, OFF_SGI, OFF_MQ, OFF_CQ, OFF_CKV, OFF_KR, NP = 0, 4096, 6144, 7680, 8704, 9216, 9472, 9600, 9728

ADAM_LR = 0.001
ADAM_B1 = 0.9
ADAM_B2 = 0.999
ADAM_EPS = 1e-08
ADAM_WD = 0.01
ADAM_STEP = 10

VMEM_LIMIT = 56 * 1024 * 1024
PACK_W = 512
MESH_ID = pl.DeviceIdType.MESH


def _cparams(n_axes):
    return pltpu.CompilerParams(dimension_semantics=("arbitrary",) * n_axes, vmem_limit_bytes=VMEM_LIMIT)


def _bs(shape, imap):
    return pl.BlockSpec(shape, imap)


@jax.custom_vjp
def _mm_plain(a, b):
    return jnp.dot(a.astype(MM), b.astype(MM), preferred_element_type=F32)


def _mm_plain_fwd(a, b):
    return _mm_plain(a, b), (a, b)


def _mm_plain_bwd(res, g):
    a, b = res
    gm = g.astype(MM)
    da = lax.dot_general(gm, b.astype(MM), (((1,), (1,)), ((), ())), preferred_element_type=F32)
    db = lax.dot_general(a.astype(MM), gm, (((0,), (0,)), ((), ())), preferred_element_type=F32)
    return da.astype(a.dtype), db.astype(b.dtype)


_mm_plain.defvjp(_mm_plain_fwd, _mm_plain_bwd)


@jax.custom_vjp
def _mm_slot(a, w, slot):
    return jnp.dot(a.astype(MM), w.astype(MM), preferred_element_type=F32)


def _mm_slot_fwd(a, w, slot):
    return _mm_slot(a, w, slot), (a, w)


def _mm_slot_bwd(res, g):
    a, w = res
    gm = g.astype(MM)
    da = lax.dot_general(gm, w.astype(MM), (((1,), (1,)), ((), ())), preferred_element_type=F32)
    dw = lax.dot_general(a.astype(MM), gm, (((0,), (0,)), ((), ())), preferred_element_type=F32)
    return da.astype(a.dtype), jnp.zeros_like(w), dw


_mm_slot.defvjp(_mm_slot_fwd, _mm_slot_bwd)


def _mm(a, b):
    if isinstance(b, tuple):
        return _mm_slot(a, b[0], b[1])
    return _mm_plain(a, b)


def _with_slot(w):
    return (w, jnp.zeros(w.shape, F32))


@jax.custom_vjp
def _mm_nt(a, b):
    return lax.dot_general(a.astype(MM), b.astype(MM), (((1,), (1,)), ((), ())), preferred_element_type=F32)


def _mm_nt_fwd(a, b):
    return _mm_nt(a, b), (a, b)


def _mm_nt_bwd(res, g):
    a, b = res
    gm = g.astype(MM)
    da = jnp.dot(gm, b.astype(MM), preferred_element_type=F32)
    db = lax.dot_general(gm, a.astype(MM), (((0,), (0,)), ((), ())), preferred_element_type=F32)
    return da.astype(a.dtype), db.astype(b.dtype)


_mm_nt.defvjp(_mm_nt_fwd, _mm_nt_bwd)


@functools.partial(jax.custom_vjp, nondiff_argnums=(1,))
def _lane_roll(x, shift):
    return pltpu.roll(x, shift, 1)


def _lane_roll_fwd(x, shift):
    return pltpu.roll(x, shift, 1), None


def _lane_roll_bwd(shift, _, g):
    return (pltpu.roll(g, (LANES - shift) % LANES, 1),)


_lane_roll.defvjp(_lane_roll_fwd, _lane_roll_bwd)


def _rms_n(x, g, n):
    ms = jnp.sum(x * x, axis=-1, keepdims=True) * (1.0 / n)
    return x * lax.rsqrt(ms + EPS) * g


def _softmax(s):
    m = jnp.max(s, axis=-1, keepdims=True)
    e = jnp.exp(s - m)
    return e / jnp.sum(e, axis=-1, keepdims=True)


def _rope(t, cos_t, sin_a, sin_b):
    return t * cos_t + _lane_roll(t, LANES - 16) * sin_a + _lane_roll(t, 16) * sin_b


def _mla_prep_fn(cq, ckv, kr, cos_t, sin_a, sin_b, cq_g, ckv_g, qg, kg, wuq, wkn, wv):
    cqn = _rms_n(cq, cq_g, QL)
    ckvn = _rms_n(ckv, ckv_g, KVL)
    lane = lax.broadcasted_iota(jnp.int32, kr.shape, 1)
    krm = jnp.where((lane >= NOPE) & (lane < QKH), kr, 0.0)
    qs, ks = [], []
    for h in range(H):
        qh = _rms_n(_mm(cqn, wuq[h]), qg, QKH)
        qs.append(_rope(qh, cos_t, sin_a, sin_b))
        kh = _rms_n(_mm(ckvn, wkn[h]) + krm, kg, QKH)
        ks.append(_rope(kh, cos_t, sin_a, sin_b))
    return jnp.concatenate(qs, axis=-1), jnp.concatenate(ks, axis=-1), _mm(ckvn, wv)


def _dot_nt(a, b):
    return lax.dot_general(a.astype(MM), b.astype(MM), (((1,), (1,)), ((), ())), preferred_element_type=F32)


def _dot_tn(a, b):
    return lax.dot_general(a.astype(MM), b.astype(MM), (((0,), (0,)), ((), ())), preferred_element_type=F32)


def _causal_scores(qe, ke, row0):
    tq, kl = qe.shape[0], ke.shape[0]
    rows = row0 + lax.broadcasted_iota(jnp.int32, (tq, kl), 0)
    cols = lax.broadcasted_iota(jnp.int32, (tq, kl), 1)
    return jnp.where(cols <= rows, _dot_nt(qe, ke) * (QKH ** -0.5), NEG_INF)


def _head_lanes(e, shape):
    lane = lax.broadcasted_iota(jnp.int32, shape, len(shape) - 1)
    return (lane >= VH * e) & (lane < VH * (e + 1))


def _attn_pair_fwd(q2, k2, v2, row0):
    tq = q2.shape[0]
    o = jnp.zeros((tq, LANES), F32)
    lse = jnp.zeros((tq, LANES), F32)
    for e in range(2):
        sl = slice(LANES * e, LANES * (e + 1))
        s = _causal_scores(q2[:, sl], k2[:, sl], row0)
        m = jnp.max(s, axis=-1, keepdims=True)
        ex = jnp.exp(s - m)
        l = jnp.sum(ex, axis=-1, keepdims=True)
        ve = jnp.where(_head_lanes(e, v2[:, sl].shape), v2[:, sl], 0.0)
        o = o + jnp.dot((ex / l).astype(MM), ve.astype(MM), preferred_element_type=F32)
        lse = jnp.where(_head_lanes(e, lse.shape), m + jnp.log(l), lse)
    return o, lse


def _attn_pair_bwd(q2, k2, v2, sg, dys, o, lse, row0):
    sig = jax.nn.sigmoid(sg)
    do = dys * (sg * sig)
    dsg = dys * o * (sig * (1.0 + sg * (1.0 - sig)))
    dqs, dks, dvs = [], [], []
    for e in range(2):
        sl = slice(LANES * e, LANES * (e + 1))
        qe, ke = q2[:, sl], k2[:, sl]
        hm = _head_lanes(e, o.shape)
        lse_e = jnp.max(jnp.where(hm, lse, NEG_INF), axis=-1, keepdims=True)
        do_e = jnp.where(hm, do, 0.0)
        delta = jnp.sum(do_e * o, axis=-1, keepdims=True)
        p = jnp.exp(_causal_scores(qe, ke, row0) - lse_e)
        ve = jnp.where(_head_lanes(e, v2[:, sl].shape), v2[:, sl], 0.0)
        dvs.append(_dot_tn(p, do_e))
        ds = (p * (_dot_nt(do_e, ve) - delta)) * (QKH ** -0.5)
        dqs.append(jnp.dot(ds.astype(MM), ke.astype(MM), preferred_element_type=F32))
        dks.append(_dot_tn(ds, qe))
    return jnp.concatenate(dqs, axis=-1), jnp.concatenate(dks, axis=-1), jnp.concatenate(dvs, axis=-1), dsg


def _sg_fn(u, v, sgc, ln_g, ln_b, ws, bs):
    mu = jnp.mean(v, axis=-1, keepdims=True)
    xc = v - mu
    vn = xc * lax.rsqrt(jnp.mean(xc * xc, axis=-1, keepdims=True) + EPS) * ln_g + ln_b
    r = lax.broadcasted_iota(jnp.int32, (SGC, SGC), 0)
    c = lax.broadcasted_iota(jnp.int32, (SGC, SGC), 1)
    wt = [jnp.where(r >= c, w, 0.0) for w in ws]
    row_blocks = []
    for ch in range(u.shape[0] // SGC):
        col_blocks = []
        for g in range(SGG):
            blk = vn[SGC * ch:SGC * (ch + 1), LANES * g:LANES * (g + 1)]
            col_blocks.append(_mm(wt[g], blk) + bs[g])
        row_blocks.append(jnp.concatenate(col_blocks, axis=-1))
    mixed = jnp.concatenate(row_blocks, axis=0)
    return (u * mixed) * jax.nn.silu(sgc)


def _memkv_fn(mem, mem_g, wm, kg):
    kv = _mm(_rms_n(mem, mem_g, D), wm)
    ks = [_rms_n(kv[:, MHD * h:MHD * (h + 1)], kg, MHD) for h in range(MH)]
    return jnp.concatenate(ks, axis=-1), kv[:, MH * MHD:]


def _mem_fn(mq, sgd, k, v, qg):
    outs = []
    for h in range(MH):
        sl = slice(MHD * h, MHD * (h + 1))
        qh = _rms_n(mq[:, sl], qg, MHD)
        p = _softmax(_mm_nt(qh, k[:, sl]) * (MHD ** -0.5))
        outs.append(_mm(p, v[:, sl]))
    return jnp.concatenate(outs, axis=-1) * jax.nn.silu(sgd)


def _merge_fn(ys, logits, bm, wb, wo):
    merged = None
    for n in range(NB):
        z = jnp.concatenate([_mm(ys[n], wb[j][n]) for j in range(N_CHIPS)], axis=-1)
        gate = jax.nn.sigmoid(logits[:, D * n:D * (n + 1)] + bm[n])
        merged = gate * z if merged is None else merged + gate * z
    return _mm(merged, wo)


def _proj_call(x, g, w):
    s_len = x.shape[0]
    tm, tn = min(s_len, 1024), NP // 4

    def body(x_ref, g_ref, w_ref, p_ref, h_ref):
        @pl.when(pl.program_id(1) == 0)
        def _():
            h_ref[...] = _rms_n(x_ref[...], g_ref[...], D).astype(h_ref.dtype)
        p_ref[...] = jnp.dot(h_ref[...], w_ref[...], preferred_element_type=F32)

    return pl.pallas_call(
        body, grid=(s_len // tm, NP // tn),
        in_specs=[_bs((tm, D), lambda i, j: (i, 0)), _bs((1, D), lambda i, j: (0, 0)), _bs((D, tn), lambda i, j: (0, j))],
        out_specs=[_bs((tm, tn), lambda i, j: (i, j)), _bs((tm, D), lambda i, j: (i, 0))],
        out_shape=[jax.ShapeDtypeStruct((s_len, NP), F32), jax.ShapeDtypeStruct((s_len, D), MM)],
        name="proj", compiler_params=_cparams(2))(x, g, w)


def _rope_tables(pos):
    half = ROPE // 2
    inv_freq = ROPE_THETA ** (-jnp.arange(half, dtype=F32) / half)
    ang = pos.astype(F32)[:, None] * inv_freq
    cos, sin = jnp.cos(ang), jnp.sin(ang)
    s_len = pos.shape[0]
    z = lambda n: jnp.zeros((s_len, n), F32)
    cos_t = jnp.concatenate([jnp.ones((s_len, NOPE), F32), cos, cos, z(LANES - QKH)], axis=1)
    sin_a = jnp.concatenate([z(NOPE), -sin, z(LANES - NOPE - half)], axis=1)
    sin_b = jnp.concatenate([z(NOPE + half), sin, z(LANES - QKH)], axis=1)
    return cos_t, sin_a, sin_b


def _mla_prep_specs(tm):
    row = lambda w, off: _bs((tm, w), lambda i: (i, off // w))
    full2 = lambda a, b: _bs((a, b), lambda i: (0, 0))
    full3 = lambda a, b, c: _bs((a, b, c), lambda i: (0, 0, 0))
    tab = _bs((tm, LANES), lambda i: (i, 0))
    return [row(QL, OFF_CQ), row(KVL, OFF_CKV), row(LANES, OFF_KR), tab, tab, tab,
            full2(1, QL), full2(1, KVL), full2(1, LANES), full2(1, LANES),
            full3(H, QL, LANES), full3(H, KVL, LANES), full2(KVL, H * LANES)]


def _mla_prep_args(body_refs, wrap=lambda w: w):
    (cq, ckv, kr, ct, sa, sb, cqg, ckvg, qg, kg, wuq, wkn, wv) = body_refs
    return (cq[...], ckv[...], kr[...], ct[...], sa[...], sb[...], cqg[...], ckvg[...], qg[...], kg[...],
            [wrap(wuq[h]) for h in range(H)], [wrap(wkn[h]) for h in range(H)], wrap(wv[...]))


def _mla_prep_call(proj, tabs, cq_g, ckv_g, qg, kg, wuq, wkn, wv):
    s_len = proj.shape[0]
    tm = min(s_len, 256)

    def body(*refs):
        q_ref, k_ref, v_ref = refs[13:]
        q, k, v = _mla_prep_fn(*_mla_prep_args(refs[:13]))
        q_ref[...] = q.astype(q_ref.dtype)
        k_ref[...] = k.astype(k_ref.dtype)
        v_ref[...] = v.astype(v_ref.dtype)

    out = _bs((tm, H * LANES), lambda i: (i, 0))
    return pl.pallas_call(
        body, grid=(s_len // tm,), in_specs=_mla_prep_specs(tm), out_specs=[out, out, out],
        out_shape=[jax.ShapeDtypeStruct((s_len, H * LANES), MM)] * 3,
        name="mla_prep", compiler_params=_cparams(1))(proj, proj, proj, *tabs, cq_g, ckv_g, qg, kg, wuq, wkn, wv)


def _mla_prep_bwd_call(proj, tabs, cq_g, ckv_g, qg, kg, wuq, wkn, wv, dq, dk, dv):
    s_len = proj.shape[0]
    tm = min(s_len, 256)

    def body(*refs):
        dq_ref, dk_ref, dv_ref = refs[13:16]
        dcq_ref, dckv_ref, dkr_ref, dcqg_ref, dckvg_ref, dqg_ref, dkg_ref, dwuq_ref, dwkn_ref, dwv_ref = refs[16:]
        _, vjp = jax.vjp(_mla_prep_fn, *_mla_prep_args(refs[:13], _with_slot))
        (dcq, dckv, dkr, _, _, _, dcqg, dckvg, dqg, dkg, dwuq, dwkn, dwv) = vjp((dq_ref[...], dk_ref[...], dv_ref[...]))
        dwuq, dwkn, dwv = [d[1] for d in dwuq], [d[1] for d in dwkn], dwv[1]
        dcq_ref[...] = dcq.astype(dcq_ref.dtype)
        dckv_ref[...] = dckv.astype(dckv_ref.dtype)
        dkr_ref[...] = dkr.astype(dkr_ref.dtype)

        @pl.when(pl.program_id(0) == 0)
        def _():
            for r in (dcqg_ref, dckvg_ref, dqg_ref, dkg_ref, dwuq_ref, dwkn_ref, dwv_ref):
                r[...] = jnp.zeros_like(r)
        dcqg_ref[...] += dcqg
        dckvg_ref[...] += dckvg
        dqg_ref[...] += dqg
        dkg_ref[...] += dkg
        for h in range(H):
            dwuq_ref[h] += dwuq[h]
            dwkn_ref[h] += dwkn[h]
        dwv_ref[...] += dwv

    big = _bs((tm, H * LANES), lambda i: (i, 0))
    row = lambda w: _bs((tm, w), lambda i: (i, 0))
    full2 = lambda a, b: _bs((a, b), lambda i: (0, 0))
    full3 = lambda a, b, c: _bs((a, b, c), lambda i: (0, 0, 0))
    sd = jax.ShapeDtypeStruct
    return pl.pallas_call(
        body, grid=(s_len // tm,), in_specs=_mla_prep_specs(tm) + [big, big, big],
        out_specs=[row(QL), row(KVL), row(LANES), full2(1, QL), full2(1, KVL), full2(1, LANES), full2(1, LANES),
                   full3(H, QL, LANES), full3(H, KVL, LANES), full2(KVL, H * LANES)],
        out_shape=[sd((s_len, QL), MM), sd((s_len, KVL), MM), sd((s_len, LANES), MM), sd((1, QL), F32), sd((1, KVL), F32),
                   sd((1, LANES), F32), sd((1, LANES), F32), sd((H, QL, LANES), F32), sd((H, KVL, LANES), F32),
                   sd((KVL, H * LANES), F32)],
        name="mla_prep_bwd", compiler_params=_cparams(1))(proj, proj, proj, *tabs, cq_g, ckv_g, qg, kg, wuq, wkn, wv, dq, dk, dv)


def _attn_specs(s_len, tq):
    pair = 2 * LANES
    return [_bs((tq, pair), lambda p, i: (i, p)), _bs((s_len, pair), lambda p, i: (0, p)), _bs((s_len, pair), lambda p, i: (0, p)),
            _bs((tq, LANES), lambda p, i: (i, OFF_SG // LANES + p))]


def _attn_call(q, k, v, proj):
    s_len = q.shape[0]
    tq = min(s_len, 256)

    def body(q_ref, k_ref, v_ref, sg_ref, y_ref, o_ref, lse_ref):
        for n in range(s_len // tq):
            @pl.when(pl.program_id(1) == n)
            def _():
                kl = (n + 1) * tq
                o, lse = _attn_pair_fwd(q_ref[...], k_ref[:kl, :], v_ref[:kl, :], n * tq)
                y_ref[...] = (o * jax.nn.silu(sg_ref[...])).astype(y_ref.dtype)
                o_ref[...] = o
                lse_ref[...] = lse

    tile = _bs((tq, LANES), lambda p, i: (i, p))
    sd = jax.ShapeDtypeStruct
    return pl.pallas_call(
        body, grid=(H // 2, s_len // tq), in_specs=_attn_specs(s_len, tq), out_specs=[tile, tile, tile],
        out_shape=[sd((s_len, BW), MM), sd((s_len, BW), F32), sd((s_len, BW), F32)],
        name="attn", compiler_params=_cparams(2))(q, k, v, proj)


def _attn_bwd_call(q, k, v, proj, dys, o, lse):
    s_len = q.shape[0]
    tq = min(s_len, 256)
    pair = 2 * LANES

    def body(q_ref, k_ref, v_ref, sg_ref, dy_ref, o_ref, lse_ref, dq_ref, dk_ref, dv_ref, dsg_ref):
        i = pl.program_id(1)

        @pl.when(i == 0)
        def _():
            dk_ref[...] = jnp.zeros_like(dk_ref)
            dv_ref[...] = jnp.zeros_like(dv_ref)

        for n in range(s_len // tq):
            @pl.when(i == n)
            def _():
                kl = (n + 1) * tq
                dq, dk, dv, dsg = _attn_pair_bwd(q_ref[...], k_ref[:kl, :], v_ref[:kl, :], sg_ref[...], dy_ref[...],
                                                 o_ref[...], lse_ref[...], n * tq)
                dq_ref[...] = dq
                dsg_ref[...] = dsg.astype(dsg_ref.dtype)
                dk_ref[:kl, :] += dk
                dv_ref[:kl, :] += dv

    sd = jax.ShapeDtypeStruct
    tile = _bs((tq, LANES), lambda p, i: (i, p))
    return pl.pallas_call(
        body, grid=(H // 2, s_len // tq),
        in_specs=_attn_specs(s_len, tq) + [tile, tile, tile],
        out_specs=[_bs((tq, pair), lambda p, i: (i, p)), _bs((s_len, pair), lambda p, i: (0, p)),
                   _bs((s_len, pair), lambda p, i: (0, p)), tile],
        out_shape=[sd((s_len, H * LANES), F32), sd((s_len, H * LANES), F32), sd((s_len, H * LANES), F32), sd((s_len, BW), MM)],
        name="attn_bwd", compiler_params=_cparams(2))(q, k, v, proj, dys, o, lse)


def _shift_down(a, n):
    r = lax.broadcasted_iota(jnp.int32, a.shape, 0)
    return jnp.where(r >= n, pltpu.roll(a, n, 0), 0.0)


def _shift_up(a, n):
    s_len = a.shape[0]
    r = lax.broadcasted_iota(jnp.int32, a.shape, 0)
    return jnp.where(r < s_len - n, pltpu.roll(a, s_len - n, 0), 0.0)


def _conv_specs(s_len):
    col = lambda off: _bs((s_len, LANES), lambda j: (0, off // LANES + j))
    return [col(OFF_CV), col(OFF_CV + CW), col(OFF_CV + 2 * CW), col(OFF_SG + BW),
            _bs((3, LANES), lambda j: (0, j)), _bs((1, LANES), lambda j: (0, j))]


def _conv_call(proj, cw, cb):
    s_len = proj.shape[0]

    def body(bg_ref, cg_ref, xi_ref, sg_ref, w_ref, b_ref, y_ref):
        z = cg_ref[...] * xi_ref[...]
        y = b_ref[...] + w_ref[0:1, :] * _shift_down(z, 2)
        y = y + w_ref[1:2, :] * _shift_down(z, 1)
        y = y + w_ref[2:3, :] * z
        y_ref[...] = ((bg_ref[...] * y) * jax.nn.silu(sg_ref[...])).astype(y_ref.dtype)

    return pl.pallas_call(
        body, grid=(CW // LANES,), in_specs=_conv_specs(s_len), out_specs=_bs((s_len, LANES), lambda j: (0, j)),
        out_shape=jax.ShapeDtypeStruct((s_len, CW), MM), name="conv", compiler_params=_cparams(1))(proj, proj, proj, proj, cw, cb)


def _conv_bwd_call(proj, cw, cb, dys):
    s_len = proj.shape[0]

    def body(bg_ref, cg_ref, xi_ref, sg_ref, w_ref, b_ref, dys_ref, dbg_ref, dcg_ref, dxi_ref, dsg_ref, dw_ref, db_ref):
        bg, cg, xi, sg = bg_ref[...], cg_ref[...], xi_ref[...], sg_ref[...]
        w0, w1, w2 = w_ref[0:1, :], w_ref[1:2, :], w_ref[2:3, :]
        z = cg * xi
        z1, z2 = _shift_down(z, 1), _shift_down(z, 2)
        y = b_ref[...] + w0 * z2
        y = y + w1 * z1
        y = y + w2 * z
        yb = bg * y
        sig = jax.nn.sigmoid(sg)
        silu = sg * sig
        dys_v = dys_ref[...]
        dsg_ref[...] = (dys_v * yb * (sig * (1.0 + sg * (1.0 - sig)))).astype(dsg_ref.dtype)
        dyb = dys_v * silu
        dbg_ref[...] = (dyb * y).astype(dbg_ref.dtype)
        dy = dyb * bg
        db_ref[...] = jnp.sum(dy, axis=0, keepdims=True)
        dw_ref[0:1, :] = jnp.sum(dy * z2, axis=0, keepdims=True)
        dw_ref[1:2, :] = jnp.sum(dy * z1, axis=0, keepdims=True)
        dw_ref[2:3, :] = jnp.sum(dy * z, axis=0, keepdims=True)
        dz = w2 * dy + w1 * _shift_up(dy, 1) + w0 * _shift_up(dy, 2)
        dcg_ref[...] = (dz * xi).astype(dcg_ref.dtype)
        dxi_ref[...] = (dz * cg).astype(dxi_ref.dtype)

    col = _bs((s_len, LANES), lambda j: (0, j))
    sd = jax.ShapeDtypeStruct
    return pl.pallas_call(
        body, grid=(CW // LANES,), in_specs=_conv_specs(s_len) + [col],
        out_specs=[col, col, col, col, _bs((3, LANES), lambda j: (0, j)), _bs((1, LANES), lambda j: (0, j))],
        out_shape=[sd((s_len, CW), MM)] * 4 + [sd((3, CW), F32), sd((1, CW), F32)],
        name="conv_bwd", compiler_params=_cparams(1))(proj, proj, proj, proj, cw, cb, dys)


def _sg_specs(tm):
    row = lambda off: _bs((tm, SGW), lambda i: (i, off // SGW))
    return [row(OFF_SGI), row(OFF_SGI + SGW), row(OFF_SG + 2 * BW), _bs((1, SGW), lambda i: (0, 0)), _bs((1, SGW), lambda i: (0, 0)),
            _bs((SGG, SGC, SGC), lambda i: (0, 0, 0)), _bs((SGG, SGC, 1), lambda i: (0, 0, 0))]


def _sg_args(refs):
    u, v, sg, lg, lb, ws, bs = refs
    return (u[...], v[...], sg[...], lg[...], lb[...], [ws[g] for g in range(SGG)], [bs[g] for g in range(SGG)])


def _sg_call(proj, ln_g, ln_b, ws, bs):
    s_len = proj.shape[0]
    tm = min(s_len, 256)

    def body(*refs):
        refs[7][...] = _sg_fn(*_sg_args(refs[:7])).astype(refs[7].dtype)

    return pl.pallas_call(
        body, grid=(s_len // tm,), in_specs=_sg_specs(tm), out_specs=_bs((tm, SGW), lambda i: (i, 0)),
        out_shape=jax.ShapeDtypeStruct((s_len, SGW), MM), name="sgmlp", compiler_params=_cparams(1))(proj, proj, proj, ln_g, ln_b, ws, bs)


def _sg_bwd_call(proj, ln_g, ln_b, ws, bs, dys):
    s_len = proj.shape[0]
    tm = min(s_len, 256)

    def body(*refs):
        dys_ref = refs[7]
        du_ref, dv_ref, dsg_ref, dlg_ref, dlb_ref, dws_ref, dbs_ref = refs[8:]
        _, vjp = jax.vjp(_sg_fn, *_sg_args(refs[:7]))
        du, dv, dsg, dlg, dlb, dws, dbs = vjp(dys_ref[...])
        du_ref[...] = du.astype(du_ref.dtype)
        dv_ref[...] = dv.astype(dv_ref.dtype)
        dsg_ref[...] = dsg.astype(dsg_ref.dtype)

        @pl.when(pl.program_id(0) == 0)
        def _():
            for r in (dlg_ref, dlb_ref, dws_ref, dbs_ref):
                r[...] = jnp.zeros_like(r)
        dlg_ref[...] += dlg
        dlb_ref[...] += dlb
        for g in range(SGG):
            dws_ref[g] += dws[g]
            dbs_ref[g] += dbs[g]

    row = _bs((tm, SGW), lambda i: (i, 0))
    sd = jax.ShapeDtypeStruct
    return pl.pallas_call(
        body, grid=(s_len // tm,), in_specs=_sg_specs(tm) + [row],
        out_specs=[row, row, row, _bs((1, SGW), lambda i: (0, 0)), _bs((1, SGW), lambda i: (0, 0)),
                   _bs((SGG, SGC, SGC), lambda i: (0, 0, 0)), _bs((SGG, SGC, 1), lambda i: (0, 0, 0))],
        out_shape=[sd((s_len, SGW), MM)] * 3 + [sd((1, SGW), F32), sd((1, SGW), F32), sd((SGG, SGC, SGC), F32), sd((SGG, SGC, 1), F32)],
        name="sgmlp_bwd", compiler_params=_cparams(1))(proj, proj, proj, ln_g, ln_b, ws, bs, dys)


def _memkv_call(mem, mem_g, wm, kg):
    m_len = mem.shape[0]

    def body(mem_ref, g_ref, w_ref, kg_ref, k_ref, v_ref):
        k, v = _memkv_fn(mem_ref[...], g_ref[...], w_ref[...], kg_ref[...])
        k_ref[...] = k.astype(k_ref.dtype)
        v_ref[...] = v.astype(v_ref.dtype)

    return pl.pallas_call(body, out_shape=[jax.ShapeDtypeStruct((m_len, MH * MHD), MM)] * 2, name="memkv",
                          compiler_params=pltpu.CompilerParams(vmem_limit_bytes=VMEM_LIMIT))(mem, mem_g, wm, kg)


def _memkv_bwd_call(mem, mem_g, wm, kg, dk, dv):
    def body(mem_ref, g_ref, w_ref, kg_ref, dk_ref, dv_ref, dg_ref, dw_ref, dkg_ref):
        _, vjp = jax.vjp(_memkv_fn, mem_ref[...], g_ref[...], _with_slot(w_ref[...]), kg_ref[...])
        _, dg, dw, dkg = vjp((dk_ref[...], dv_ref[...]))
        dg_ref[...] = dg
        dw_ref[...] = dw[1]
        dkg_ref[...] = dkg

    sd = jax.ShapeDtypeStruct
    return pl.pallas_call(body, out_shape=[sd((1, D), F32), sd((D, 2 * MH * MHD), F32), sd((1, MHD), F32)], name="memkv_bwd",
                          compiler_params=pltpu.CompilerParams(vmem_limit_bytes=VMEM_LIMIT))(mem, mem_g, wm, kg, dk, dv)


def _mem_specs(tm, m_len):
    w = MH * MHD
    return [_bs((tm, w), lambda i: (i, OFF_MQ // w)), _bs((tm, BW), lambda i: (i, (OFF_SG + 3 * BW) // BW)),
            _bs((m_len, w), lambda i: (0, 0)), _bs((m_len, w), lambda i: (0, 0)), _bs((1, MHD), lambda i: (0, 0))]


def _mem_call(proj, k, v, qg):
    s_len, m_len = proj.shape[0], k.shape[0]
    tm = min(s_len, 256)

    def body(mq_ref, sg_ref, k_ref, v_ref, qg_ref, y_ref):
        y_ref[...] = _mem_fn(mq_ref[...], sg_ref[...], k_ref[...], v_ref[...], qg_ref[...]).astype(y_ref.dtype)

    return pl.pallas_call(
        body, grid=(s_len // tm,), in_specs=_mem_specs(tm, m_len), out_specs=_bs((tm, BW), lambda i: (i, 0)),
        out_shape=jax.ShapeDtypeStruct((s_len, BW), MM), name="memattn", compiler_params=_cparams(1))(proj, proj, k, v, qg)


def _mem_bwd_call(proj, k, v, qg, dys):
    s_len, m_len = proj.shape[0], k.shape[0]
    tm = min(s_len, 256)
    w = MH * MHD

    def body(mq_ref, sg_ref, k_ref, v_ref, qg_ref, dys_ref, dmq_ref, dsg_ref, dk_ref, dv_ref, dqg_ref):
        _, vjp = jax.vjp(_mem_fn, mq_ref[...], sg_ref[...], k_ref[...].astype(F32), v_ref[...].astype(F32), qg_ref[...])
        dmq, dsg, dk, dv, dqg = vjp(dys_ref[...])
        dmq_ref[...] = dmq.astype(dmq_ref.dtype)
        dsg_ref[...] = dsg.astype(dsg_ref.dtype)

        @pl.when(pl.program_id(0) == 0)
        def _():
            for r in (dk_ref, dv_ref, dqg_ref):
                r[...] = jnp.zeros_like(r)
        dk_ref[...] += dk
        dv_ref[...] += dv
        dqg_ref[...] += dqg

    row = _bs((tm, BW), lambda i: (i, 0))
    kv = _bs((m_len, w), lambda i: (0, 0))
    sd = jax.ShapeDtypeStruct
    return pl.pallas_call(
        body, grid=(s_len // tm,), in_specs=_mem_specs(tm, m_len) + [row],
        out_specs=[row, row, kv, kv, _bs((1, MHD), lambda i: (0, 0))],
        out_shape=[sd((s_len, w), MM), sd((s_len, BW), MM), sd((m_len, w), F32), sd((m_len, w), F32), sd((1, MHD), F32)],
        name="memattn_bwd", compiler_params=_cparams(1))(proj, proj, k, v, qg, dys)


def _merge_specs(tm):
    row = _bs((tm, BW), lambda i: (i, 0))
    return [row, row, row, row, _bs((tm, NB * D), lambda i: (i, OFF_ML // (NB * D))), _bs((NB, D), lambda i: (0, 0)),
            _bs((N_CHIPS, NB, BW, D // N_CHIPS), lambda i: (0, 0, 0, 0)), _bs((D, D), lambda i: (0, 0))]


def _merge_call(ys, proj, bm, wb, wo, x):
    s_len = proj.shape[0]
    tm = min(s_len, 256)

    def body(ya, yb, yc, yd, lg_ref, bm_ref, wb_ref, wo_ref, x_ref, o_ref):
        out = _merge_fn([r[...] for r in (ya, yb, yc, yd)], lg_ref[...], [bm_ref[n:n + 1, :] for n in range(NB)],
                        [[wb_ref[j, n] for n in range(NB)] for j in range(N_CHIPS)], wo_ref[...])
        o_ref[...] = x_ref[...] + out

    xrow = _bs((tm, D), lambda i: (i, 0))
    return pl.pallas_call(
        body, grid=(s_len // tm,), in_specs=_merge_specs(tm) + [xrow], out_specs=xrow,
        out_shape=jax.ShapeDtypeStruct((s_len, D), F32), name="merge", compiler_params=_cparams(1))(*ys, proj, bm, wb, wo, x)


def _merge_bwd_call(ys, proj, bm, wb, wo, dout):
    s_len = proj.shape[0]
    tm = min(s_len, 256)

    def body(ya, yb, yc, yd, lg_ref, bm_ref, wb_ref, wo_ref, do_ref, dya, dyb, dyc, dyd, dlg_ref, dbm_ref, dwb_ref, dwo_ref):
        fn = lambda ys_, lg_, bm_, wb_, wo_: _merge_fn(ys_, lg_, bm_, wb_, wo_)
        _, vjp = jax.vjp(fn, [r[...].astype(F32) for r in (ya, yb, yc, yd)], lg_ref[...], [bm_ref[n:n + 1, :] for n in range(NB)],
                         [[_with_slot(wb_ref[j, n]) for n in range(NB)] for j in range(N_CHIPS)], _with_slot(wo_ref[...]))
        dys, dlg, dbm, dwb, dwo = vjp(do_ref[...])
        dwb, dwo = [[d[1] for d in row] for row in dwb], dwo[1]
        for r, d in zip((dya, dyb, dyc, dyd), dys):
            r[...] = d
        dlg_ref[...] = dlg.astype(dlg_ref.dtype)

        @pl.when(pl.program_id(0) == 0)
        def _():
            for r in (dbm_ref, dwb_ref, dwo_ref):
                r[...] = jnp.zeros_like(r)
        for n in range(NB):
            dbm_ref[n:n + 1, :] += dbm[n]
            for j in range(N_CHIPS):
                dwb_ref[j, n] += dwb[j][n]
        dwo_ref[...] += dwo

    row = _bs((tm, BW), lambda i: (i, 0))
    sd = jax.ShapeDtypeStruct
    wb_shape = (N_CHIPS, NB, BW, D // N_CHIPS)
    return pl.pallas_call(
        body, grid=(s_len // tm,), in_specs=_merge_specs(tm) + [_bs((tm, D), lambda i: (i, 0))],
        out_specs=[row, row, row, row, _bs((tm, NB * D), lambda i: (i, 0)), _bs((NB, D), lambda i: (0, 0)),
                   _bs(wb_shape, lambda i: (0, 0, 0, 0)), _bs((D, D), lambda i: (0, 0))],
        out_shape=[sd((s_len, BW), F32)] * 4 + [sd((s_len, NB * D), MM), sd((NB, D), F32), sd(wb_shape, F32), sd((D, D), F32)],
        name="merge_bwd", compiler_params=_cparams(1))(*ys, proj, bm, wb, wo, dout)


def _dh_call(dproj, w, x, g, dout):
    s_len = x.shape[0]
    tm, tk = min(s_len, 512), NP // 4

    def body(dp_ref, w_ref, x_ref, g_ref, do_ref, dx_ref, dg_ref, acc_ref):
        i, k = pl.program_id(0), pl.program_id(1)

        @pl.when(k == 0)
        def _():
            acc_ref[...] = jnp.zeros_like(acc_ref)
        acc_ref[...] += lax.dot_general(dp_ref[...], w_ref[...], (((1,), (1,)), ((), ())), preferred_element_type=F32)

        @pl.when(k == pl.num_programs(1) - 1)
        def _():
            _, vjp = jax.vjp(lambda x_, g_: _rms_n(x_, g_, D), x_ref[...], g_ref[...])
            dxr, dgr = vjp(acc_ref[...])
            dx_ref[...] = do_ref[...] + dxr

            @pl.when(i == 0)
            def _():
                dg_ref[...] = jnp.zeros_like(dg_ref)
            dg_ref[...] += dgr

    row = _bs((tm, D), lambda i, k: (i, 0))
    return pl.pallas_call(
        body, grid=(s_len // tm, NP // tk),
        in_specs=[_bs((tm, tk), lambda i, k: (i, k)), _bs((D, tk), lambda i, k: (0, k)), row, _bs((1, D), lambda i, k: (0, 0)), row],
        out_specs=[row, _bs((1, D), lambda i, k: (0, 0))],
        out_shape=[jax.ShapeDtypeStruct((s_len, D), F32), jax.ShapeDtypeStruct((1, D), F32)],
        scratch_shapes=[pltpu.VMEM((tm, D), F32)], name="dh", compiler_params=_cparams(2))(dproj, w, x, g, dout)


def _dw_call(h, dproj):
    s_len = h.shape[0]
    tn = 512

    def body(h_ref, dp_ref, o_ref):
        o_ref[...] = lax.dot_general(h_ref[...], dp_ref[...], (((0,), (0,)), ((), ())), preferred_element_type=F32)

    return pl.pallas_call(
        body, grid=(NP // tn,), in_specs=[_bs((s_len, D), lambda j: (0, 0)), _bs((s_len, tn), lambda j: (0, j))],
        out_specs=_bs((D, tn), lambda j: (0, j)), out_shape=jax.ShapeDtypeStruct((D, NP), F32),
        name="dw_in", compiler_params=_cparams(1))(h, dproj)


def _loss_call(y, target):
    s_len = y.shape[0]
    tm = min(s_len, 512)

    def body(y_ref, t_ref, dy_ref, l_ref):
        e = y_ref[...] - t_ref[...]
        dy_ref[...] = e * (1.0 / D)

        @pl.when(pl.program_id(0) == 0)
        def _():
            l_ref[...] = jnp.zeros_like(l_ref)
        l_ref[...] += jnp.sum(e * e, axis=0, keepdims=True)

    row = _bs((tm, D), lambda i: (i, 0))
    return pl.pallas_call(
        body, grid=(s_len // tm,), in_specs=[row, row], out_specs=[row, _bs((1, D), lambda i: (0, 0))],
        out_shape=[jax.ShapeDtypeStruct((s_len, D), F32), jax.ShapeDtypeStruct((1, D), F32)],
        name="loss", compiler_params=_cparams(1))(y, target)


def _adamw_call(w, g, m, v, name):
    rows, cols = w.shape
    tr = min(_row_tile(rows), 128)

    def body(w_ref, g_ref, m_ref, v_ref, d_ref, nm_ref, nv_ref):
        gv = g_ref[...]
        m2 = ADAM_B1 * m_ref[...] + (1.0 - ADAM_B1) * gv
        v2 = ADAM_B2 * v_ref[...] + (1.0 - ADAM_B2) * (gv * gv)
        m_hat = m2 / (1.0 - ADAM_B1 ** ADAM_STEP)
        v_hat = v2 / (1.0 - ADAM_B2 ** ADAM_STEP)
        d_ref[...] = -ADAM_LR * (m_hat / (jnp.sqrt(v_hat) + ADAM_EPS) + ADAM_WD * w_ref[...])
        nm_ref[...] = m2
        nv_ref[...] = v2

    blk = _bs((tr, cols), lambda i: (i, 0))
    return pl.pallas_call(
        body, grid=(rows // tr,), in_specs=[blk] * 4, out_specs=[blk] * 3,
        out_shape=[jax.ShapeDtypeStruct((rows, cols), F32)] * 3, name=name, compiler_params=_cparams(1))(w, g, m, v)


def _adamw_layer_call(layer, ws, gs, ms, vs, prev, after, name):
    n, steps = len(ws), 8
    after = list(after)
    n_prev = 4 * n if prev is not None else 0

    def body(*refs):
        outs = refs[len(refs) - 4 * n:]
        for t in range(n):
            w_ref, g_ref, m_ref, v_ref = refs[t], refs[n + t], refs[2 * n + t], refs[3 * n + t]
            g_out, d_out, m_out, v_out = outs[4 * t:4 * t + 4]
            gv = g_ref[...]
            m2 = ADAM_B1 * m_ref[0] + (1.0 - ADAM_B1) * gv
            v2 = ADAM_B2 * v_ref[0] + (1.0 - ADAM_B2) * (gv * gv)
            m_hat = m2 / (1.0 - ADAM_B1 ** ADAM_STEP)
            v_hat = v2 / (1.0 - ADAM_B2 ** ADAM_STEP)
            g_out[0] = gv
            d_out[0] = -ADAM_LR * (m_hat / (jnp.sqrt(v_hat) + ADAM_EPS) + ADAM_WD * w_ref[0])
            m_out[0] = m2
            v_out[0] = v2

    def lay(a):
        return _bs((1, a.shape[1] // steps, a.shape[2]), lambda i: (layer, i, 0))

    in_specs = ([lay(a) for a in ws] + [_bs((g.shape[0] // steps, g.shape[1]), lambda i: (i, 0)) for g in gs]
                + [lay(a) for a in ms] + [lay(a) for a in vs] + [_ANY] * (n_prev + len(after)))
    return pl.pallas_call(
        body, grid=(steps,), in_specs=in_specs, out_specs=[lay(ws[t]) for t in range(n) for _ in range(4)],
        out_shape=[jax.ShapeDtypeStruct(ws[t].shape, F32) for t in range(n) for _ in range(4)],
        input_output_aliases={4 * n + q: q for q in range(n_prev)}, name=name, compiler_params=_cparams(1),
    )(*ws, *gs, *ms, *vs, *(prev if prev is not None else []), *after)


def _row_tile(rows):
    for cand in (512, 256, 128, 64, 32, 16, 8):
        if rows % cand == 0 and rows > cand:
            return cand
    return rows


def _pair_sum_call(grads, from_sibling, core, name):
    n = len(grads)

    def body(core_ref, *refs):
        for t in range(n):
            refs[2 * n + t][...] = (refs[t][...] + refs[n + t][...]).astype(MM)

    half = lambda g: (1, g.shape[1] // 2, g.shape[2])
    grid_spec = pltpu.PrefetchScalarGridSpec(
        num_scalar_prefetch=1, grid=(N_CHIPS,),
        in_specs=[pl.BlockSpec(half(g), lambda j, core_ref: (j, core_ref[0], 0)) for g in grads]
        + [pl.BlockSpec(half(g), lambda j, core_ref: (j, 0, 0)) for g in grads],
        out_specs=[pl.BlockSpec(half(g), lambda j, core_ref: (j, 0, 0)) for g in grads])
    return pl.pallas_call(
        body, grid_spec=grid_spec, out_shape=[jax.ShapeDtypeStruct((N_CHIPS,) + half(g)[1:], MM) for g in grads], name=name,
        compiler_params=_cparams(1))(core, *grads, *from_sibling)


def _owner_sum_call(chip_sums, from_chips, chip_core, name):
    n = len(chip_sums)
    steps = 4

    def body(ids_ref, *refs):
        for t in range(n):
            a, b = refs[t], refs[n + t]
            refs[2 * n + t][...] = ((a[0].astype(F32) + b[0].astype(F32)) + b[1].astype(F32)) + b[2].astype(F32)

    tile = lambda p: (p.shape[1] // steps, p.shape[2])
    grid_spec = pltpu.PrefetchScalarGridSpec(
        num_scalar_prefetch=1, grid=(steps,),
        in_specs=[pl.BlockSpec((1,) + tile(p), lambda i, ids_ref: (ids_ref[0], i, 0)) for p in chip_sums]
        + [pl.BlockSpec((3,) + tile(p), lambda i, ids_ref: (0, i, 0)) for p in chip_sums],
        out_specs=[pl.BlockSpec(tile(p), lambda i, ids_ref: (ids_ref[1] * steps + i, 0)) for p in chip_sums])
    return pl.pallas_call(
        body, grid_spec=grid_spec, out_shape=[jax.ShapeDtypeStruct((2 * p.shape[1], p.shape[2]), F32) for p in chip_sums],
        name=name, compiler_params=_cparams(1))(chip_core, *chip_sums, *from_chips)


def _sum8_call(parts):
    n, rows, cols = parts.shape
    tr = _row_tile(rows)

    def body(p_ref, o_ref):
        acc = p_ref[0]
        for k in range(1, n):
            acc = acc + p_ref[k]
        o_ref[...] = acc

    return pl.pallas_call(
        body, grid=(rows // tr,), in_specs=[_bs((n, tr, cols), lambda i: (0, i, 0))], out_specs=_bs((tr, cols), lambda i: (i, 0)),
        out_shape=jax.ShapeDtypeStruct((rows, cols), F32), name="sum_small_grads", compiler_params=_cparams(1))(parts)


_ANY = pl.BlockSpec(memory_space=pl.ANY)


def _all_gather8(blk, name):
    rows, cols = blk.shape

    def body(x_ref, out_ref, send_sems, recv_sems, local_sem):
        x, y, c = lax.axis_index("x"), lax.axis_index("y"), lax.axis_index("c")
        me, sibling = (x, y, c), (x, y, 1 - c)
        chips = [(1 - x, y), (x, 1 - y), (1 - x, 1 - y)]

        def slot(px, py, pc):
            return out_ref.at[4 * px + 2 * py + pc]

        def copy(k, block, to, src=None):
            return pltpu.make_async_remote_copy(
                src_ref=slot(*block) if src is None else src, dst_ref=slot(*block),
                send_sem=send_sems.at[k], recv_sem=recv_sems.at[k], device_id=to, device_id_type=MESH_ID)

        mine = pltpu.make_async_copy(x_ref, slot(*me), local_sem)
        mine.start()
        first = [copy(0, me, sibling, src=x_ref)]
        first += [copy(1 + j, me, (*chip, c), src=x_ref) for j, chip in enumerate(chips)]
        for cp in first:
            cp.start()
        passed = [copy(4 + j, (*chip, c), sibling) for j, chip in enumerate(chips)]
        for j, chip in enumerate(chips):
            copy(1 + j, (*chip, c), me).wait_recv()
            passed[j].start()
        copy(0, sibling, me).wait_recv()
        for j, chip in enumerate(chips):
            copy(4 + j, (*chip, 1 - c), me).wait_recv()
        for cp in first + passed:
            cp.wait_send()
        mine.wait()

    return pl.pallas_call(
        body, out_shape=jax.ShapeDtypeStruct((8, rows, cols), blk.dtype), in_specs=[_ANY], out_specs=_ANY,
        scratch_shapes=[pltpu.SemaphoreType.DMA((7,)), pltpu.SemaphoreType.DMA((7,)), pltpu.SemaphoreType.DMA],
        name=name)(blk)


def _half_rows(ref, lead, half, which):
    rows = pl.ds(pl.multiple_of(half * which, half), half)
    return ref.at[rows] if lead is None else ref.at[lead, rows]


def _gather_layer_call(layer, shards, name):
    n = len(shards)
    half = [s.shape[1] // 2 for s in shards]

    def body(*refs):
        srcs, outs = refs[:n], refs[n:2 * n]
        send_sems, recv_sems, local_sems = refs[2 * n:]
        x, y, c = lax.axis_index("x"), lax.axis_index("y"), lax.axis_index("c")
        sibling = (x, y, 1 - c)
        chips = [(1 - x, y), (x, 1 - y), (1 - x, 1 - y)]

        def slot(t, px, py, pc):
            return _half_rows(outs[t], 2 * px + py, half[t], pc)

        def copy(t, k, block, to, src=None):
            return pltpu.make_async_remote_copy(
                src_ref=slot(t, *block) if src is None else src, dst_ref=slot(t, *block),
                send_sem=send_sems.at[7 * t + k], recv_sem=recv_sems.at[7 * t + k], device_id=to, device_id_type=MESH_ID)

        mine = [_half_rows(srcs[t], layer, half[t], c) for t in range(n)]
        local = [pltpu.make_async_copy(mine[t], slot(t, x, y, c), local_sems.at[t]) for t in range(n)]
        for cp in local:
            cp.start()
        first = []
        for t in range(n):
            first.append(copy(t, 0, (x, y, c), sibling, src=mine[t]))
            first += [copy(t, 1 + j, (x, y, c), (*chip, c), src=mine[t]) for j, chip in enumerate(chips)]
        for cp in first:
            cp.start()
        passed = []
        for j, chip in enumerate(chips):
            for t in range(n):
                copy(t, 1 + j, (*chip, c), (x, y, c)).wait_recv()
                passed.append(copy(t, 4 + j, (*chip, c), sibling))
                passed[-1].start()
        for t in range(n):
            copy(t, 0, (x, y, 1 - c), (x, y, c)).wait_recv()
            for j, chip in enumerate(chips):
                copy(t, 4 + j, (*chip, 1 - c), (x, y, c)).wait_recv()
        for cp in first + passed:
            cp.wait_send()
        for cp in local:
            cp.wait()

    return pl.pallas_call(
        body, out_shape=[jax.ShapeDtypeStruct((N_CHIPS,) + s.shape[1:], s.dtype) for s in shards],
        in_specs=[_ANY] * n, out_specs=[_ANY] * n,
        scratch_shapes=[pltpu.SemaphoreType.DMA((7 * n,)), pltpu.SemaphoreType.DMA((7 * n,)), pltpu.SemaphoreType.DMA((n,))],
        name=name)(*shards)


_HBM = pl.BlockSpec(memory_space=pltpu.HBM)
_SEM = pl.BlockSpec(memory_space=pltpu.SEMAPHORE)
_ORDERED_EFFECT = pltpu.CompilerParams(has_side_effects=pltpu.SideEffectType.DATAFLOW_SIDE_EFFECTING)


_VMEM = pl.BlockSpec(memory_space=pltpu.VMEM)
_TOKEN = jax.ShapeDtypeStruct((8, LANES), F32)


def _in_hbm(a):
    return pltpu.with_memory_space_constraint(a, pltpu.HBM)


def _tie(small, token):
    return small + token[0:1, 0:1].reshape((1,) * small.ndim)


def _pair_exchange_start_call(grads, name):
    n = len(grads)
    half = [g.shape[1] // 2 for g in grads]

    def body(*refs):
        srcs, outs = refs[:n], refs[n:2 * n]
        send_sems, recv_sems, token = refs[2 * n:]
        x, y, c = lax.axis_index("x"), lax.axis_index("y"), lax.axis_index("c")
        for t in range(n):
            pltpu.make_async_remote_copy(
                src_ref=srcs[t].at[:, pl.ds(pl.multiple_of(half[t] * (1 - c), half[t]), half[t])], dst_ref=outs[t],
                send_sem=send_sems.at[t], recv_sem=recv_sems.at[t], device_id=(x, y, 1 - c), device_id_type=MESH_ID).start()
        token[...] = jnp.zeros_like(token)

    dma = pltpu.SemaphoreType.DMA
    return pl.pallas_call(
        body, out_shape=[pltpu.HBM((g.shape[0], g.shape[1] // 2, g.shape[2]), g.dtype) for g in grads] + [dma((n,)), dma((n,)), _TOKEN],
        in_specs=[_HBM] * n, out_specs=[_HBM] * n + [_SEM, _SEM, _VMEM], name=name, compiler_params=_ORDERED_EFFECT,
    )(*[_in_hbm(g) for g in grads])


def _pair_exchange_finish_call(grads, bufs, send_sems, recv_sems, after, name):
    n = len(grads)
    after = list(after)
    half = [g.shape[1] // 2 for g in grads]

    def body(*refs):
        srcs, ins, send_ref, recv_ref = refs[:n], refs[n:2 * n], refs[2 * n], refs[2 * n + 1]
        x, y, c = lax.axis_index("x"), lax.axis_index("y"), lax.axis_index("c")
        for t in range(n):
            pltpu.make_async_remote_copy(
                src_ref=srcs[t].at[:, pl.ds(pl.multiple_of(half[t] * (1 - c), half[t]), half[t])], dst_ref=ins[t],
                send_sem=send_ref.at[t], recv_sem=recv_ref.at[t], device_id=(x, y, 1 - c), device_id_type=MESH_ID).wait()

    return pl.pallas_call(
        body, out_shape=[pltpu.HBM(b.shape, b.dtype) for b in bufs],
        in_specs=[_HBM] * (2 * n) + [_SEM, _SEM] + [_ANY] * len(after), out_specs=[_HBM] * n,
        input_output_aliases={n + t: t for t in range(n)}, name=name, compiler_params=_ORDERED_EFFECT,
    )(*[_in_hbm(g) for g in grads], *bufs, send_sems, recv_sems, *after)


def _chip_scatter_start_call(chip_sums, name):
    n = len(chip_sums)

    def body(*refs):
        srcs, outs = refs[:n], refs[n:2 * n]
        send_sems, recv_sems, token = refs[2 * n:]
        x, y, c = lax.axis_index("x"), lax.axis_index("y"), lax.axis_index("c")
        chips = [(1 - x, y), (x, 1 - y), (1 - x, 1 - y)]
        for k, (cx, cy) in enumerate(chips):
            for t in range(n):
                pltpu.make_async_remote_copy(
                    src_ref=srcs[t].at[2 * cx + cy], dst_ref=outs[t].at[k], send_sem=send_sems.at[3 * t + k],
                    recv_sem=recv_sems.at[3 * t + k], device_id=(cx, cy, c), device_id_type=MESH_ID).start()
        token[...] = jnp.zeros_like(token)

    dma = pltpu.SemaphoreType.DMA
    return pl.pallas_call(
        body, out_shape=[pltpu.HBM((3,) + p.shape[1:], p.dtype) for p in chip_sums] + [dma((3 * n,)), dma((3 * n,)), _TOKEN],
        in_specs=[_HBM] * n, out_specs=[_HBM] * n + [_SEM, _SEM, _VMEM], name=name, compiler_params=_ORDERED_EFFECT,
    )(*[_in_hbm(p) for p in chip_sums])


def _chip_scatter_finish_call(chip_sums, bufs, send_sems, recv_sems, after, name):
    n = len(chip_sums)
    after = list(after)

    def body(*refs):
        srcs, ins, send_ref, recv_ref = refs[:n], refs[n:2 * n], refs[2 * n], refs[2 * n + 1]
        x, y, c = lax.axis_index("x"), lax.axis_index("y"), lax.axis_index("c")
        chips = [(1 - x, y), (x, 1 - y), (1 - x, 1 - y)]
        for k, (cx, cy) in enumerate(chips):
            for t in range(n):
                pltpu.make_async_remote_copy(
                    src_ref=srcs[t].at[2 * cx + cy], dst_ref=ins[t].at[k], send_sem=send_ref.at[3 * t + k],
                    recv_sem=recv_ref.at[3 * t + k], device_id=(cx, cy, c), device_id_type=MESH_ID).wait()

    return pl.pallas_call(
        body, out_shape=[pltpu.HBM(b.shape, b.dtype) for b in bufs],
        in_specs=[_HBM] * (2 * n) + [_SEM, _SEM] + [_ANY] * len(after), out_specs=[_HBM] * n,
        input_output_aliases={n + t: t for t in range(n)}, name=name, compiler_params=_ORDERED_EFFECT,
    )(*[_in_hbm(p) for p in chip_sums], *bufs, send_sems, recv_sems, *after)


def _place_own_call(shards, chip_core, name):
    n = len(shards)

    def body(ids_ref, *refs):
        for t in range(n):
            refs[n + t][0] = refs[t][...]

    def imap_in(s):
        pad = (0,) * (s.ndim - 1)
        return lambda i, ids_ref: (ids_ref[1],) + pad

    def imap_out(s):
        pad = (0,) * (s.ndim - 1)
        return lambda i, ids_ref: (ids_ref[0], ids_ref[1]) + pad

    half = lambda s: (s.shape[0] // 2,) + s.shape[1:]
    grid_spec = pltpu.PrefetchScalarGridSpec(
        num_scalar_prefetch=1, grid=(1,), in_specs=[pl.BlockSpec(half(s), imap_in(s)) for s in shards],
        out_specs=[pl.BlockSpec((1,) + half(s), imap_out(s)) for s in shards])
    return pl.pallas_call(
        body, grid_spec=grid_spec, out_shape=[jax.ShapeDtypeStruct((N_CHIPS,) + s.shape, s.dtype) for s in shards],
        name=name, compiler_params=_cparams(1))(chip_core, *shards)


def _gather_start_call(shards, bufs, after, name):
    n = len(shards)
    half = [s.shape[0] // 2 for s in shards]

    def body(*refs):
        srcs, outs = refs[:n], refs[2 * n + 1:3 * n + 1]
        send_sems, recv_sib, recv_ici, token = refs[3 * n + 1:]
        x, y, c = lax.axis_index("x"), lax.axis_index("y"), lax.axis_index("c")
        chips = [(1 - x, y), (x, 1 - y), (1 - x, 1 - y)]
        for t in range(n):
            mine = _half_rows(srcs[t], None, half[t], c)
            dst = _half_rows(outs[t], 2 * x + y, half[t], c)
            pltpu.make_async_remote_copy(src_ref=mine, dst_ref=dst, send_sem=send_sems.at[4 * t], recv_sem=recv_sib.at[t],
                                         device_id=(x, y, 1 - c), device_id_type=MESH_ID).start()
            for j, chip in enumerate(chips):
                pltpu.make_async_remote_copy(src_ref=mine, dst_ref=dst, send_sem=send_sems.at[4 * t + 1 + j],
                                             recv_sem=recv_ici.at[3 * t + j], device_id=(*chip, c), device_id_type=MESH_ID).start()
        token[...] = jnp.zeros_like(token)

    dma = pltpu.SemaphoreType.DMA
    return pl.pallas_call(
        body, out_shape=[pltpu.HBM(b.shape, b.dtype) for b in bufs] + [dma((4 * n,)), dma((n,)), dma((3 * n,)), _TOKEN],
        in_specs=[_HBM] * (2 * n) + [_ANY], out_specs=[_HBM] * n + [_SEM] * 3 + [_VMEM],
        input_output_aliases={n + t: t for t in range(n)}, name=name, compiler_params=_ORDERED_EFFECT,
    )(*[_in_hbm(s) for s in shards], *[_in_hbm(b) for b in bufs], after)


def _gather_forward_call(bufs, recv_ici, after, name):
    n = len(bufs)
    half = [b.shape[1] // 2 for b in bufs]

    def body(*refs):
        ins, recv_ici_ref = refs[:n], refs[n]
        outs = refs[n + 2:2 * n + 2]
        send_fwd, recv_fwd, token = refs[2 * n + 2:]
        x, y, c = lax.axis_index("x"), lax.axis_index("y"), lax.axis_index("c")
        chips = [(1 - x, y), (x, 1 - y), (1 - x, 1 - y)]
        for j, (cx, cy) in enumerate(chips):
            for t in range(n):
                landed = _half_rows(ins[t], 2 * cx + cy, half[t], c)
                dst = _half_rows(outs[t], 2 * cx + cy, half[t], c)
                pltpu.make_async_remote_copy(src_ref=landed, dst_ref=landed, send_sem=send_fwd.at[3 * t + j],
                                             recv_sem=recv_ici_ref.at[3 * t + j], device_id=(cx, cy, c),
                                             device_id_type=MESH_ID).wait_recv()
                pltpu.make_async_remote_copy(src_ref=landed, dst_ref=dst, send_sem=send_fwd.at[3 * t + j],
                                             recv_sem=recv_fwd.at[3 * t + j], device_id=(x, y, 1 - c),
                                             device_id_type=MESH_ID).start()
        token[...] = jnp.zeros_like(token)

    dma = pltpu.SemaphoreType.DMA
    return pl.pallas_call(
        body, out_shape=[pltpu.HBM(b.shape, b.dtype) for b in bufs] + [dma((3 * n,)), dma((3 * n,)), _TOKEN],
        in_specs=[_HBM] * n + [_SEM, _ANY], out_specs=[_HBM] * n + [_SEM] * 2 + [_VMEM],
        input_output_aliases={t: t for t in range(n)}, name=name, compiler_params=_ORDERED_EFFECT,
    )(*bufs, recv_ici, after)


def _gather_finish_call(shards, bufs, send_sems, recv_sib, send_fwd, recv_fwd, after, name):
    n = len(bufs)
    half = [b.shape[1] // 2 for b in bufs]

    def body(*refs):
        srcs, ins = refs[:n], refs[n:2 * n]
        send_ref, recv_sib_ref, send_fwd_ref, recv_fwd_ref = refs[2 * n:2 * n + 4]
        x, y, c = lax.axis_index("x"), lax.axis_index("y"), lax.axis_index("c")
        chips = [(1 - x, y), (x, 1 - y), (1 - x, 1 - y)]
        sibling = (x, y, 1 - c)
        for t in range(n):
            mine = _half_rows(srcs[t], None, half[t], c)
            for k in range(4):
                pltpu.make_async_remote_copy(src_ref=mine, dst_ref=mine, send_sem=send_ref.at[4 * t + k],
                                             recv_sem=recv_sib_ref.at[t], device_id=sibling, device_id_type=MESH_ID).wait_send()
            from_sibling = _half_rows(ins[t], 2 * x + y, half[t], 1 - c)
            pltpu.make_async_remote_copy(src_ref=from_sibling, dst_ref=from_sibling, send_sem=send_ref.at[4 * t],
                                         recv_sem=recv_sib_ref.at[t], device_id=sibling, device_id_type=MESH_ID).wait_recv()
            for j, (cx, cy) in enumerate(chips):
                sent = _half_rows(ins[t], 2 * cx + cy, half[t], c)
                passed = _half_rows(ins[t], 2 * cx + cy, half[t], 1 - c)
                pltpu.make_async_remote_copy(src_ref=sent, dst_ref=passed, send_sem=send_fwd_ref.at[3 * t + j],
                                             recv_sem=recv_fwd_ref.at[3 * t + j], device_id=sibling, device_id_type=MESH_ID).wait()

    return pl.pallas_call(
        body, out_shape=[pltpu.HBM(b.shape, b.dtype) for b in bufs],
        in_specs=[_HBM] * (2 * n) + [_SEM] * 4 + [_ANY], out_specs=[_HBM] * n,
        input_output_aliases={n + t: t for t in range(n)}, name=name, compiler_params=_ORDERED_EFFECT,
    )(*[_in_hbm(s) for s in shards], *bufs, send_sems, recv_sib, send_fwd, recv_fwd, after)


def _pair_exchange_call(grads, name):
    n = len(grads)
    half = [g.shape[1] // 2 for g in grads]

    def body(*refs):
        srcs, outs, send_sems, recv_sems = refs[:n], refs[n:2 * n], refs[2 * n], refs[2 * n + 1]
        x, y, c = lax.axis_index("x"), lax.axis_index("y"), lax.axis_index("c")
        copies = [pltpu.make_async_remote_copy(
            src_ref=srcs[t].at[:, pl.ds(pl.multiple_of(half[t] * (1 - c), half[t]), half[t])], dst_ref=outs[t],
            send_sem=send_sems.at[t], recv_sem=recv_sems.at[t], device_id=(x, y, 1 - c), device_id_type=MESH_ID) for t in range(n)]
        for cp in copies:
            cp.start()
        for cp in copies:
            cp.wait()

    return pl.pallas_call(
        body, out_shape=[jax.ShapeDtypeStruct((g.shape[0], g.shape[1] // 2, g.shape[2]), g.dtype) for g in grads],
        in_specs=[_ANY] * n, out_specs=[_ANY] * n,
        scratch_shapes=[pltpu.SemaphoreType.DMA((n,)), pltpu.SemaphoreType.DMA((n,))], name=name)(*grads)


def _chip_scatter_call(chip_sums, name):
    n = len(chip_sums)

    def body(*refs):
        srcs, outs, send_sems, recv_sems = refs[:n], refs[n:2 * n], refs[2 * n], refs[2 * n + 1]
        x, y, c = lax.axis_index("x"), lax.axis_index("y"), lax.axis_index("c")
        chips = [(1 - x, y), (x, 1 - y), (1 - x, 1 - y)]
        copies = [pltpu.make_async_remote_copy(
            src_ref=srcs[t].at[2 * cx + cy], dst_ref=outs[t].at[k], send_sem=send_sems.at[3 * t + k],
            recv_sem=recv_sems.at[3 * t + k], device_id=(cx, cy, c), device_id_type=MESH_ID)
            for k, (cx, cy) in enumerate(chips) for t in range(n)]
        for cp in copies:
            cp.start()
        for cp in copies:
            cp.wait()

    return pl.pallas_call(
        body, out_shape=[jax.ShapeDtypeStruct((3,) + p.shape[1:], p.dtype) for p in chip_sums],
        in_specs=[_ANY] * n, out_specs=[_ANY] * n,
        scratch_shapes=[pltpu.SemaphoreType.DMA((3 * n,)), pltpu.SemaphoreType.DMA((3 * n,))], name=name)(*chip_sums)


def _pair_gather_call(bufs, name):
    n = len(bufs)
    half = [b.shape[0] // 2 for b in bufs]

    def body(*refs):
        srcs, outs, send_sems, recv_sems = refs[:n], refs[n:2 * n], refs[2 * n], refs[2 * n + 1]
        x, y, c = lax.axis_index("x"), lax.axis_index("y"), lax.axis_index("c")
        for t in range(n):
            pltpu.make_async_remote_copy(
                src_ref=_half_rows(srcs[t], None, half[t], c), dst_ref=_half_rows(outs[t], None, half[t], c),
                send_sem=send_sems.at[t], recv_sem=recv_sems.at[t], device_id=(x, y, 1 - c), device_id_type=MESH_ID).start()
        for t in range(n):
            pltpu.make_async_remote_copy(
                src_ref=_half_rows(srcs[t], None, half[t], c), dst_ref=_half_rows(outs[t], None, half[t], 1 - c),
                send_sem=send_sems.at[t], recv_sem=recv_sems.at[t], device_id=(x, y, 1 - c), device_id_type=MESH_ID).wait()

    return pl.pallas_call(
        body, out_shape=[jax.ShapeDtypeStruct(b.shape, b.dtype) for b in bufs], in_specs=[_ANY] * n, out_specs=[_ANY] * n,
        input_output_aliases={t: t for t in range(n)},
        scratch_shapes=[pltpu.SemaphoreType.DMA((n,)), pltpu.SemaphoreType.DMA((n,))], name=name)(*bufs)


def _pack_rows(flats, dtype, row_multiple):
    flat = jnp.concatenate([f.reshape(-1).astype(dtype) for f in flats])
    n = flat.shape[0]
    rows = -(-n // PACK_W)
    rows = -(-rows // row_multiple) * row_multiple
    return jnp.pad(flat, (0, rows * PACK_W - n)).reshape(rows, PACK_W)


def _unpack(flat, shapes):
    out, off = [], 0
    for shp in shapes:
        n = math.prod(shp)
        out.append(flat[off:off + n].reshape(shp))
        off += n
    return out


def _f32_as_mm_bits(a):
    return lax.bitcast_convert_type(a, jnp.bfloat16).reshape(-1)


def _mm_bits_as_f32(flat, shape):
    return lax.bitcast_convert_type(flat.reshape(-1, 2), F32).reshape(shape)


_W_IN_SEGMENTS = ((R_ML, R_END, OFF_ML), (R_SG, R_ML, OFF_SG), (R_CV, R_SGI, OFF_CV), (R_SGI, R_MQ, OFF_SGI), (R_MQ, R_SG, OFF_MQ),
                  (R_CQ, R_CKV, OFF_CQ), (R_CKV, R_KR, OFF_CKV), (R_KR, R_CV, OFF_KR + NOPE))
W_IN_SHARD = R_END // N_CHIPS


def _realign_call(wg):
    tr = 128

    def body(w_ref, o_ref):
        pieces, pos = [], 0
        for r0, r1, a0 in _W_IN_SEGMENTS:
            if a0 > pos:
                pieces.append(jnp.zeros((tr, a0 - pos), o_ref.dtype))
            while r0 < r1:
                j = r0 // W_IN_SHARD
                hi = min(r1, (j + 1) * W_IN_SHARD)
                pieces.append(w_ref[j, :, r0 - j * W_IN_SHARD:hi - j * W_IN_SHARD])
                a0, r0 = a0 + hi - r0, hi
            pos = a0
        pieces.append(jnp.zeros((tr, NP - pos), o_ref.dtype))
        o_ref[...] = jnp.concatenate(pieces, axis=1)

    return pl.pallas_call(
        body, grid=(D // tr,), in_specs=[_bs((N_CHIPS, tr, W_IN_SHARD), lambda i: (0, i, 0))],
        out_specs=_bs((tr, NP), lambda i: (i, 0)), out_shape=jax.ShapeDtypeStruct((D, NP), wg.dtype),
        name="w_in_realign", compiler_params=_cparams(1))(wg)


def _unalign_call(dw):
    tr = 128
    by_ref = sorted(_W_IN_SEGMENTS)

    def body(dw_ref, o_ref):
        for j in range(N_CHIPS):
            lo_j, hi_j = j * W_IN_SHARD, (j + 1) * W_IN_SHARD
            pieces = []
            for r0, r1, a0 in by_ref:
                lo, hi = max(r0, lo_j), min(r1, hi_j)
                if lo < hi:
                    pieces.append(dw_ref[:, a0 + lo - r0:a0 + hi - r0])
            o_ref[j] = jnp.concatenate(pieces, axis=1)

    return pl.pallas_call(
        body, grid=(D // tr,), in_specs=[_bs((tr, NP), lambda i: (i, 0))],
        out_specs=_bs((N_CHIPS, tr, W_IN_SHARD), lambda i: (0, i, 0)),
        out_shape=jax.ShapeDtypeStruct((N_CHIPS, D, W_IN_SHARD), dw.dtype), name="w_in_unalign", compiler_params=_cparams(1))(dw)


def _w_in_to_aligned(w):
    z = lambda n: jnp.zeros((w.shape[0], n), w.dtype)
    return jnp.concatenate([w[:, R_ML:R_END], w[:, R_SG:R_ML], w[:, R_CV:R_SGI], w[:, R_SGI:R_MQ], w[:, R_MQ:R_SG],
                            w[:, R_CQ:R_CKV], w[:, R_CKV:R_KR], z(NOPE), w[:, R_KR:R_CV], z(LANES - QKH)], axis=1)


def _w_in_from_aligned(wa):
    return jnp.concatenate([wa[:, OFF_CQ:OFF_CKV], wa[:, OFF_CKV:OFF_KR], wa[:, OFF_KR + NOPE:OFF_KR + QKH], wa[:, OFF_CV:OFF_SGI],
                            wa[:, OFF_SGI:OFF_MQ], wa[:, OFF_MQ:OFF_CQ], wa[:, OFF_SG:OFF_CV], wa[:, OFF_ML:OFF_SG]], axis=1)


def _wuq_to_heads(w):
    w3 = w.reshape(QL, H, QKH)
    w3 = jnp.pad(w3, ((0, 0), (0, 0), (0, LANES - QKH)))
    return jnp.transpose(w3, (1, 0, 2))


def _wuq_from_heads(wh):
    return jnp.transpose(wh[:, :, :QKH], (1, 0, 2)).reshape(QL, H * QKH)


def _wukv_to_heads(w):
    w3 = w.reshape(KVL, H, NOPE + VH)
    wkn = jnp.transpose(jnp.pad(w3[:, :, :NOPE], ((0, 0), (0, 0), (0, LANES - NOPE))), (1, 0, 2))
    wv3 = w3[:, :, NOPE:]
    z = jnp.zeros((KVL, VH), w.dtype)
    cols = []
    for h in range(H):
        cols += [wv3[:, h], z] if h % 2 == 0 else [z, wv3[:, h]]
    return wkn, jnp.concatenate(cols, axis=1)


def _wukv_from_heads(wkn, wv):
    kn = jnp.transpose(wkn[:, :, :NOPE], (1, 0, 2))
    vs = jnp.stack([wv[:, LANES * h + VH * (h % 2):LANES * h + VH * (h % 2) + VH] for h in range(H)], axis=1)
    return jnp.concatenate([kn, vs], axis=2).reshape(KVL, H * (NOPE + VH))


def _layer_fwd(x, mem, tabs, p):
    proj, h = _proj_call(x, p["norm_g"], p["w_in"])
    if p.get("late") is not None:
        p = dict(p, **p["late"](proj))
    q, k, v = _mla_prep_call(proj, tabs, p["cq_g"], p["ckv_g"], p["qg"], p["kg"], p["wuq"], p["wkn"], p["wv"])
    ya, attn_o, attn_lse = _attn_call(q, k, v, proj)
    bm = p["bm"]
    if p.get("after_attn") is not None:
        bm = _tie(bm, p["after_attn"](ya))
    yb = _conv_call(proj, p["conv_w"], p["conv_b"])
    yc = _sg_call(proj, p["ln_g"], p["ln_b"], p["ws"], p["bs"])
    mk, mv = _memkv_call(mem, p["mem_g"], p["wm"], p["mkg"])
    yd = _mem_call(proj, mk, mv, p["mqg"])
    out = _merge_call((ya, yb, yc, yd), proj, bm, p["wb"], p["wo"], x)
    return out, dict(p=p, x=x, proj=proj, h=h, q=q, k=k, v=v, attn_o=attn_o, attn_lse=attn_lse, ys=(ya, yb, yc, yd), mk=mk, mv=mv)


def _layer_bwd(dout, mem, tabs, p, sv, start_after=None, after_mla=None, on_grads=None):
    proj = sv["proj"]
    bm = p["bm"] if start_after is None else _tie(p["bm"], start_after)
    dya, dyb, dyc, dyd, dml, dbm, dwb, dwo = _merge_bwd_call(sv["ys"], proj, bm, p["wb"], p["wo"], dout)
    dq, dk, dv, dsg_a = _attn_bwd_call(sv["q"], sv["k"], sv["v"], proj, dya, sv["attn_o"], sv["attn_lse"])
    dcq, dckv, dkr, dcqg, dckvg, dqg, dkg, dwuq, dwkn, dwv = _mla_prep_bwd_call(
        proj, tabs, p["cq_g"], p["ckv_g"], p["qg"], p["kg"], p["wuq"], p["wkn"], p["wv"], dq, dk, dv)
    if after_mla is not None:
        after_mla(dcq)
    dbg, dcg, dxi, dsg_b, dcw, dcb = _conv_bwd_call(proj, p["conv_w"], p["conv_b"], dyb)
    du, dvv, dsg_c, dlg, dlb, dws, dbs = _sg_bwd_call(proj, p["ln_g"], p["ln_b"], p["ws"], p["bs"], dyc)
    dmq, dsg_d, dmk, dmv, dmqg = _mem_bwd_call(proj, sv["mk"], sv["mv"], p["mqg"], dyd)
    dmem_g, dwm, dmkg = _memkv_bwd_call(mem, p["mem_g"], p["wm"], p["mkg"], dmk, dmv)
    dproj = jnp.concatenate([dml, dsg_a, dsg_b, dsg_c, dsg_d, dbg, dcg, dxi, du, dvv, dmq, dcq, dckv, dkr], axis=1)
    dw_in = _dw_call(sv["h"], dproj)
    grads = dict(cq_norm_g=dcqg[0], ckv_norm_g=dckvg[0], mla_q_norm_g=dqg[0, :QKH], mla_k_norm_g=dkg[0, :QKH],
                 conv_w=dcw, conv_b=dcb[0], sg_ln_g=dlg[0], sg_ln_b=dlb[0], w_spatial=dws, b_spatial=dbs[:, :, 0],
                 mem_norm_g=dmem_g[0], mem_q_norm_g=dmqg[0], mem_k_norm_g=dmkg[0], b_merge=dbm,
                 w_in_aligned=dw_in, wuq_heads=dwuq, wkn_heads=dwkn, wv_heads=dwv, w_mem_kv=dwm, w_branch_chips=dwb, w_out=dwo)
    norm_g = p["norm_g"]
    if on_grads is not None:
        norm_g = _tie(norm_g, on_grads(grads))
    dx, dnorm_g = _dh_call(dproj, p["w_in"], sv["x"], norm_g, dout)
    grads["norm_g"] = dnorm_g[0]
    return dx, grads


def _chips_to_cols(a):
    return jnp.concatenate([a[j] for j in range(N_CHIPS)], axis=1)


def _cols_to_chips(a):
    cols = a.shape[1] // N_CHIPS
    return jnp.stack([a[:, cols * j:cols * (j + 1)] for j in range(N_CHIPS)])


def _layer_params_first(l, rep, w_in_gathered, conv_w, b_merge):
    pad_g = lambda g: jnp.pad(g, (0, LANES - QKH)).reshape(1, LANES)
    return dict(
        norm_g=rep["norm_g"][l].reshape(1, D), w_in=_realign_call(w_in_gathered),
        cq_g=rep["cq_norm_g"][l].reshape(1, QL), ckv_g=rep["ckv_norm_g"][l].reshape(1, KVL),
        qg=pad_g(rep["mla_q_norm_g"][l]), kg=pad_g(rep["mla_k_norm_g"][l]),
        conv_w=conv_w, conv_b=rep["conv_b"][l].reshape(1, CW),
        ln_g=rep["sg_ln_g"][l].reshape(1, SGW), ln_b=rep["sg_ln_b"][l].reshape(1, SGW),
        ws=rep["w_spatial"][l], bs=rep["b_spatial"][l].reshape(SGG, SGC, 1),
        mem_g=rep["mem_norm_g"][l].reshape(1, D),
        mqg=rep["mem_q_norm_g"][l].reshape(1, MHD), mkg=rep["mem_k_norm_g"][l].reshape(1, MHD), bm=b_merge)


def _layer_params_rest(gathered):
    wkn, wv = _wukv_to_heads(_chips_to_cols(gathered["w_ukv"]))
    return dict(wuq=_wuq_to_heads(_chips_to_cols(gathered["w_uq"])), wkn=wkn, wv=wv,
                wm=gathered["w_mem_kv"].reshape(D, 2 * MH * MHD), wb=gathered["w_branch"], wo=gathered["w_out"].reshape(D, D))


def _layer_params(l, rep, gathered, conv_w, b_merge):
    return dict(_layer_params_first(l, rep, gathered["w_in"], conv_w, b_merge), **_layer_params_rest(gathered))


def _forward_backward(x, mem, pos, target, params, bwd_hooks=None):
    tabs = _rope_tables(pos)
    params = list(params)
    saved = []
    act = x
    for l in range(DEPTH):
        if callable(params[l]):
            params[l] = params[l](saved[-1], act)
        act, sv = _layer_fwd(act, mem, tabs, params[l])
        saved.append(sv)
    dy, sq = _loss_call(act, target)
    grads = [None] * DEPTH
    token = None
    for l in reversed(range(DEPTH)):
        hooks = dict(bwd_hooks[l]) if bwd_hooks else {}
        after_layer = hooks.pop("after_layer", None)
        dy, grads[l] = _layer_bwd(dy, mem, tabs, saved[l]["p"], saved[l], start_after=token, **hooks)
        token = after_layer(dy) if after_layer is not None else None
    return sq, dy, grads


_SHARDED_MM = ("w_in", "w_branch", "w_out", "w_mem_kv", "w_uq", "w_ukv")
_SHARDED_F32 = ("conv_w", "b_merge")
_REPLICATED = ("norm_g", "cq_norm_g", "ckv_norm_g", "mla_q_norm_g", "mla_k_norm_g", "conv_b", "sg_ln_g", "sg_ln_b",
               "w_spatial", "b_spatial", "mem_norm_g", "mem_q_norm_g", "mem_k_norm_g")
_ALL_REDUCED = _REPLICATED + _SHARDED_F32
_WEIGHTS = ("norm_g", "w_in", "cq_norm_g", "ckv_norm_g", "w_uq", "w_ukv", "mla_q_norm_g", "mla_k_norm_g", "conv_w", "conv_b",
            "sg_ln_g", "sg_ln_b", "w_spatial", "b_spatial", "mem_norm_g", "w_mem_kv", "mem_q_norm_g", "mem_k_norm_g",
            "b_merge", "w_branch", "w_out")
_BIG = ("w_in", "w_uq", "w_ukv", "w_mem_kv", "w_branch", "w_out")
_SMALL = tuple(n for n in _WEIGHTS if n not in _BIG)


def _gather_small_sharded(w):
    names = _SHARDED_F32
    packed = _pack_rows([w[n] for n in names], F32, 8)
    got = _all_gather8(packed, "gather_small_weights")
    per_chip = [_unpack(got[2 * j].reshape(-1), [w[n].shape for n in names]) for j in range(N_CHIPS)]
    return {n: jnp.concatenate([per_chip[j][t] for j in range(N_CHIPS)], axis=2) for t, n in enumerate(names)}


def _gather_layer(l, shards):
    srcs = [shards[n] for n in _SHARDED_MM]
    return dict(zip(_SHARDED_MM, _gather_layer_call(l, srcs, "gather_weights_l%d" % l)))


class _ReduceScatter:
    def __init__(self, layer):
        self.tag = "rs_l%d_" % layer

    def exchange(self, grads):
        self.tensors = [
            _unalign_call(grads["w_in_aligned"]),
            grads["w_branch_chips"].reshape(N_CHIPS, NB * BW, D // N_CHIPS),
            grads["w_out"].reshape(N_CHIPS, D // N_CHIPS, D),
            grads["w_mem_kv"].reshape(N_CHIPS, D // N_CHIPS, 2 * MH * MHD),
            _cols_to_chips(_wuq_from_heads(grads["wuq_heads"])),
            _cols_to_chips(_wukv_from_heads(grads["wkn_heads"], grads["wv_heads"])),
        ]
        n = len(self.tensors)
        out = _pair_exchange_start_call(self.tensors, self.tag + "exchange_start")
        self.ex_bufs, self.ex_send, self.ex_recv = out[:n], out[n], out[n + 1]
        return out[n + 2]

    def scatter(self, after):
        n = len(self.tensors)
        c = lax.axis_index("c")
        from_sibling = _pair_exchange_finish_call(self.tensors, self.ex_bufs, self.ex_send, self.ex_recv, after,
                                                  self.tag + "exchange_finish")
        self.chip_sums = _pair_sum_call(self.tensors, from_sibling, c.astype(jnp.int32).reshape(1), self.tag + "pair_sum")
        out = _chip_scatter_start_call(self.chip_sums, self.tag + "scatter_start")
        self.bufs, self.send_sems, self.recv_sems, self.token = out[:n], out[n], out[n + 1], out[n + 2]
        return self.token

    def finish(self, after):
        x, y, c = lax.axis_index("x"), lax.axis_index("y"), lax.axis_index("c")
        chip_core = jnp.stack([2 * x + y, c]).astype(jnp.int32)
        from_chips = _chip_scatter_finish_call(self.chip_sums, self.bufs, self.send_sems, self.recv_sems, after,
                                               self.tag + "scatter_finish")
        mine = _owner_sum_call(self.chip_sums, from_chips, chip_core, self.tag + "owner_sum")
        shard = dict(zip(_SHARDED_MM, _pair_gather_call(mine, self.tag + "pair_gather")))
        shard["w_branch"] = shard["w_branch"].reshape(NB, BW, D // N_CHIPS)
        return shard


def _all_reduce_small(g):
    packed = _pack_rows([g[n] for n in _ALL_REDUCED], F32, 64)
    got = _all_gather8(packed, "gather_small_grads")
    total = _sum8_call(got).reshape(-1)
    out = dict(zip(_ALL_REDUCED, _unpack(total, [g[n].shape for n in _ALL_REDUCED])))
    chip = 2 * lax.axis_index("x") + lax.axis_index("y")
    for n in _SHARDED_F32:
        size = out[n].shape[2] // N_CHIPS
        out[n] = lax.dynamic_slice_in_dim(out[n], chip * size, size, axis=2)
    return out


def _adamw_small(w, g, m, v, token):
    delta, new_m, new_v = {}, {}, {}
    shapes = [w[n].shape for n in _SMALL]
    pk = lambda t: _pack_rows([t[n] for n in _SMALL], F32, 64)
    d, nm, nv = _adamw_call(pk(w), _tie(pk(g), token), pk(m), pk(v), "adamw_small")
    for out, packed in ((delta, d), (new_m, nm), (new_v, nv)):
        out.update(zip(_SMALL, _unpack(packed.reshape(-1), shapes)))
    return delta, new_m, new_v


def kernel(x, mem, positions, norm_g, w_in, cq_norm_g, ckv_norm_g, w_uq, w_ukv, mla_q_norm_g, mla_k_norm_g, conv_w, conv_b, sg_ln_g, sg_ln_b, w_spatial, b_spatial, mem_norm_g, w_mem_kv, mem_q_norm_g, mem_k_norm_g, b_merge, w_branch, w_out, loss_target, m_norm_g, m_w_in, m_cq_norm_g, m_ckv_norm_g, m_w_uq, m_w_ukv, m_mla_q_norm_g, m_mla_k_norm_g, m_conv_w, m_conv_b, m_sg_ln_g, m_sg_ln_b, m_w_spatial, m_b_spatial, m_mem_norm_g, m_w_mem_kv, m_mem_q_norm_g, m_mem_k_norm_g, m_b_merge, m_w_branch, m_w_out, v_norm_g, v_w_in, v_cq_norm_g, v_ckv_norm_g, v_w_uq, v_w_ukv, v_mla_q_norm_g, v_mla_k_norm_g, v_conv_w, v_conv_b, v_sg_ln_g, v_sg_ln_b, v_w_spatial, v_b_spatial, v_mem_norm_g, v_w_mem_kv, v_mem_q_norm_g, v_mem_k_norm_g, v_b_merge, v_w_branch, v_w_out):
    w = dict(norm_g=norm_g, w_in=w_in, cq_norm_g=cq_norm_g, ckv_norm_g=ckv_norm_g, w_uq=w_uq, w_ukv=w_ukv,
             mla_q_norm_g=mla_q_norm_g, mla_k_norm_g=mla_k_norm_g, conv_w=conv_w, conv_b=conv_b, sg_ln_g=sg_ln_g,
             sg_ln_b=sg_ln_b, w_spatial=w_spatial, b_spatial=b_spatial, mem_norm_g=mem_norm_g, w_mem_kv=w_mem_kv,
             mem_q_norm_g=mem_q_norm_g, mem_k_norm_g=mem_k_norm_g, b_merge=b_merge, w_branch=w_branch, w_out=w_out)
    m = dict(norm_g=m_norm_g, w_in=m_w_in, cq_norm_g=m_cq_norm_g, ckv_norm_g=m_ckv_norm_g, w_uq=m_w_uq, w_ukv=m_w_ukv,
             mla_q_norm_g=m_mla_q_norm_g, mla_k_norm_g=m_mla_k_norm_g, conv_w=m_conv_w, conv_b=m_conv_b, sg_ln_g=m_sg_ln_g,
             sg_ln_b=m_sg_ln_b, w_spatial=m_w_spatial, b_spatial=m_b_spatial, mem_norm_g=m_mem_norm_g, w_mem_kv=m_w_mem_kv,
             mem_q_norm_g=m_mem_q_norm_g, mem_k_norm_g=m_mem_k_norm_g, b_merge=m_b_merge, w_branch=m_w_branch, w_out=m_w_out)
    v = dict(norm_g=v_norm_g, w_in=v_w_in, cq_norm_g=v_cq_norm_g, ckv_norm_g=v_ckv_norm_g, w_uq=v_w_uq, w_ukv=v_w_ukv,
             mla_q_norm_g=v_mla_q_norm_g, mla_k_norm_g=v_mla_k_norm_g, conv_w=v_conv_w, conv_b=v_conv_b, sg_ln_g=v_sg_ln_g,
             sg_ln_b=v_sg_ln_b, w_spatial=v_w_spatial, b_spatial=v_b_spatial, mem_norm_g=v_mem_norm_g, w_mem_kv=v_w_mem_kv,
             mem_q_norm_g=v_mem_q_norm_g, mem_k_norm_g=v_mem_k_norm_g, b_merge=v_b_merge, w_branch=v_w_branch, w_out=v_w_out)

    chip_core = jnp.stack([2 * lax.axis_index("x") + lax.axis_index("y"), lax.axis_index("c")]).astype(jnp.int32)

    class Gather:
        def __init__(self, layer, names, after, tag):
            self.names, self.tag = names, tag
            self.srcs = [w[n][layer].astype(MM) for n in names]
            k = len(names)
            out = _gather_start_call(self.srcs, _place_own_call(self.srcs, chip_core, tag + "place_own"), after, tag + "start")
            self.bufs, self.send, self.recv_sib, self.recv_ici, self.token = out[:k], out[k], out[k + 1], out[k + 2], out[k + 3]

        def pass_on(self, after):
            k = len(self.names)
            out = _gather_forward_call(self.bufs, self.recv_ici, after, self.tag + "forward")
            self.bufs, self.send_fwd, self.recv_fwd = out[:k], out[k], out[k + 1]
            return out[k + 2]

        def finish(self, after):
            got = _gather_finish_call(self.srcs, self.bufs, self.send, self.recv_sib, self.send_fwd, self.recv_fwd, after,
                                      self.tag + "finish")
            return dict(zip(self.names, got))

    first = Gather(0, ("w_in",), chip_core, "gather_l0_w_in_")
    rest = Gather(0, _SHARDED_MM[1:], first.token, "gather_l0_rest_")
    later = Gather(1, _SHARDED_MM, rest.token, "gather_l1_")
    small = _gather_small_sharded(w)
    w_in0 = first.finish(first.pass_on(later.token))["w_in"]

    def rest_of_layer0(proj0):
        return _layer_params_rest(rest.finish(rest.pass_on(proj0)))

    def layer1_params(saved0, act0):
        return _layer_params(1, w, later.finish(act0), small["conv_w"][1], small["b_merge"][1])

    params0 = _layer_params_first(0, w, w_in0, small["conv_w"][0], small["b_merge"][0])
    params = [dict(params0, late=rest_of_layer0, after_attn=later.pass_on), layer1_params]
    rs = [_ReduceScatter(l) for l in range(DEPTH)]
    shard_grads = {}
    hooks = [dict(on_grads=rs[0].exchange), dict(on_grads=rs[1].exchange, after_layer=lambda dy: rs[1].scatter([dy]))]
    sq, grad_x, layer_grads = _forward_backward(x[0], mem[0], positions[0], loss_target[0], params, hooks)
    loss = lax.psum(0.5 / D * jnp.sum(sq), ("x", "y", "c"))

    g = _all_reduce_small({n: jnp.stack([layer_grads[l][n] for l in range(DEPTH)]) for n in _ALL_REDUCED})
    scattering = rs[0].scatter([grad_x, g["norm_g"]])
    delta, new_m, new_v = _adamw_small(w, g, m, v, scattering)
    shard_grads[1] = rs[1].finish([scattering])
    as3d = lambda a: a.reshape(DEPTH, -1, a.shape[-1])
    big = lambda t: [as3d(t[n]) for n in _SHARDED_MM]
    as2d = lambda a: a.reshape(-1, a.shape[-1])
    upd1 = _adamw_layer_call(1, big(w), [as2d(shard_grads[1][n]) for n in _SHARDED_MM], big(m), big(v), None, [], "adamw_l1")
    shard_grads[0] = rs[0].finish([grad_x, upd1[0], delta["norm_g"]])
    upd = _adamw_layer_call(0, big(w), [as2d(shard_grads[0][n]) for n in _SHARDED_MM], big(m), big(v), upd1, [], "adamw_l0")
    for t, n in enumerate(_SHARDED_MM):
        g[n], delta[n], new_m[n], new_v[n] = [a.reshape(w[n].shape) for a in upd[4 * t:4 * t + 4]]
    return (loss, grad_x[None], *[g[n] for n in _WEIGHTS], *[delta[n] for n in _WEIGHTS],
            *[new_m[n] for n in _WEIGHTS], *[new_v[n] for n in _WEIGHTS])
```

```python
import functools
import math

import jax
import jax.numpy as jnp
from jax import lax
from jax.experimental import pallas as pl
from jax.experimental.pallas import tpu as pltpu

F32 = jnp.float32
MM = jnp.bfloat16

D = 1024
DEPTH = 2
EPS = 1e-6
H = 8
NOPE = 64
ROPE = 32
QKH = 96
VH = 64
QL = 256
KVL = 128
ROPE_THETA = 10000.0
CW = 512
SGW = 512
SGG = 4
SGC = 128
MH = 4
MHD = 128
NB = 4
BW = 512
NEG_INF = -1e30
LANES = 128
N_CHIPS = 4

R_CQ, R_CKV, R_KR, R_CV, R_SGI, R_MQ, R_SG, R_ML, R_END = 0, 256, 384, 416, 1952, 2976, 3488, 5536, 9632
OFF_ML, OFF_SG, OFF_CV, OFF_SGI, OFF_MQ, OFF_CQ, OFF_CKV, OFF_KR, NP = 0, 4096, 6144, 7680, 8704, 9216, 9472, 9600, 9728

ADAM_LR = 0.001
ADAM_B1 = 0.9
ADAM_B2 = 0.999
ADAM_EPS = 1e-08
ADAM_WD = 0.01
ADAM_STEP = 10

VMEM_LIMIT = 56 * 1024 * 1024
PACK_W = 512
MESH_ID = pl.DeviceIdType.MESH


def _cparams(n_axes):
    return pltpu.CompilerParams(dimension_semantics=("arbitrary",) * n_axes, vmem_limit_bytes=VMEM_LIMIT)


def _bs(shape, imap):
    return pl.BlockSpec(shape, imap)


@jax.custom_vjp
def _mm_plain(a, b):
    return jnp.dot(a.astype(MM), b.astype(MM), preferred_element_type=F32)


def _mm_plain_fwd(a, b):
    return _mm_plain(a, b), (a, b)


def _mm_plain_bwd(res, g):
    a, b = res
    gm = g.astype(MM)
    da = lax.dot_general(gm, b.astype(MM), (((1,), (1,)), ((), ())), preferred_element_type=F32)
    db = lax.dot_general(a.astype(MM), gm, (((0,), (0,)), ((), ())), preferred_element_type=F32)
    return da.astype(a.dtype), db.astype(b.dtype)


_mm_plain.defvjp(_mm_plain_fwd, _mm_plain_bwd)


@jax.custom_vjp
def _mm_slot(a, w, slot):
    return jnp.dot(a.astype(MM), w.astype(MM), preferred_element_type=F32)


def _mm_slot_fwd(a, w, slot):
    return _mm_slot(a, w, slot), (a, w)


def _mm_slot_bwd(res, g):
    a, w = res
    gm = g.astype(MM)
    da = lax.dot_general(gm, w.astype(MM), (((1,), (1,)), ((), ())), preferred_element_type=F32)
    dw = lax.dot_general(a.astype(MM), gm, (((0,), (0,)), ((), ())), preferred_element_type=F32)
    return da.astype(a.dtype), jnp.zeros_like(w), dw


_mm_slot.defvjp(_mm_slot_fwd, _mm_slot_bwd)


def _mm(a, b):
    if isinstance(b, tuple):
        return _mm_slot(a, b[0], b[1])
    return _mm_plain(a, b)


def _with_slot(w):
    return (w, jnp.zeros(w.shape, F32))


@jax.custom_vjp
def _mm_nt(a, b):
    return lax.dot_general(a.astype(MM), b.astype(MM), (((1,), (1,)), ((), ())), preferred_element_type=F32)


def _mm_nt_fwd(a, b):
    return _mm_nt(a, b), (a, b)


def _mm_nt_bwd(res, g):
    a, b = res
    gm = g.astype(MM)
    da = jnp.dot(gm, b.astype(MM), preferred_element_type=F32)
    db = lax.dot_general(gm, a.astype(MM), (((0,), (0,)), ((), ())), preferred_element_type=F32)
    return da.astype(a.dtype), db.astype(b.dtype)


_mm_nt.defvjp(_mm_nt_fwd, _mm_nt_bwd)


@functools.partial(jax.custom_vjp, nondiff_argnums=(1,))
def _lane_roll(x, shift):
    return pltpu.roll(x, shift, 1)


def _lane_roll_fwd(x, shift):
    return pltpu.roll(x, shift, 1), None


def _lane_roll_bwd(shift, _, g):
    return (pltpu.roll(g, (LANES - shift) % LANES, 1),)


_lane_roll.defvjp(_lane_roll_fwd, _lane_roll_bwd)


def _rms_n(x, g, n):
    ms = jnp.sum(x * x, axis=-1, keepdims=True) * (1.0 / n)
    return x * lax.rsqrt(ms + EPS) * g


def _softmax(s):
    m = jnp.max(s, axis=-1, keepdims=True)
    e = jnp.exp(s - m)
    return e / jnp.sum(e, axis=-1, keepdims=True)


def _rope(t, cos_t, sin_a, sin_b):
    return t * cos_t + _lane_roll(t, LANES - 16) * sin_a + _lane_roll(t, 16) * sin_b


def _mla_prep_fn(cq, ckv, kr, cos_t, sin_a, sin_b, cq_g, ckv_g, qg, kg, wuq, wkn, wv):
    cqn = _rms_n(cq, cq_g, QL)
    ckvn = _rms_n(ckv, ckv_g, KVL)
    lane = lax.broadcasted_iota(jnp.int32, kr.shape, 1)
    krm = jnp.where((lane >= NOPE) & (lane < QKH), kr, 0.0)
    qs, ks = [], []
    for h in range(H):
        qh = _rms_n(_mm(cqn, wuq[h]), qg, QKH)
        qs.append(_rope(qh, cos_t, sin_a, sin_b))
        kh = _rms_n(_mm(ckvn, wkn[h]) + krm, kg, QKH)
        ks.append(_rope(kh, cos_t, sin_a, sin_b))
    return jnp.concatenate(qs, axis=-1), jnp.concatenate(ks, axis=-1), _mm(ckvn, wv)


def _dot_nt(a, b):
    return lax.dot_general(a.astype(MM), b.astype(MM), (((1,), (1,)), ((), ())), preferred_element_type=F32)


def _dot_tn(a, b):
    return lax.dot_general(a.astype(MM), b.astype(MM), (((0,), (0,)), ((), ())), preferred_element_type=F32)


def _causal_scores(qe, ke, row0):
    tq, kl = qe.shape[0], ke.shape[0]
    rows = row0 + lax.broadcasted_iota(jnp.int32, (tq, kl), 0)
    cols = lax.broadcasted_iota(jnp.int32, (tq, kl), 1)
    return jnp.where(cols <= rows, _dot_nt(qe, ke) * (QKH ** -0.5), NEG_INF)


def _head_lanes(e, shape):
    lane = lax.broadcasted_iota(jnp.int32, shape, len(shape) - 1)
    return (lane >= VH * e) & (lane < VH * (e + 1))


def _attn_pair_fwd(q2, k2, v2, row0):
    tq = q2.shape[0]
    o = jnp.zeros((tq, LANES), F32)
    lse = jnp.zeros((tq, LANES), F32)
    for e in range(2):
        sl = slice(LANES * e, LANES * (e + 1))
        s = _causal_scores(q2[:, sl], k2[:, sl], row0)
        m = jnp.max(s, axis=-1, keepdims=True)
        ex = jnp.exp(s - m)
        l = jnp.sum(ex, axis=-1, keepdims=True)
        ve = jnp.where(_head_lanes(e, v2[:, sl].shape), v2[:, sl], 0.0)
        o = o + jnp.dot((ex / l).astype(MM), ve.astype(MM), preferred_element_type=F32)
        lse = jnp.where(_head_lanes(e, lse.shape), m + jnp.log(l), lse)
    return o, lse


def _attn_pair_bwd(q2, k2, v2, sg, dys, o, lse, row0):
    sig = jax.nn.sigmoid(sg)
    do = dys * (sg * sig)
    dsg = dys * o * (sig * (1.0 + sg * (1.0 - sig)))
    dqs, dks, dvs = [], [], []
    for e in range(2):
        sl = slice(LANES * e, LANES * (e + 1))
        qe, ke = q2[:, sl], k2[:, sl]
        hm = _head_lanes(e, o.shape)
        lse_e = jnp.max(jnp.where(hm, lse, NEG_INF), axis=-1, keepdims=True)
        do_e = jnp.where(hm, do, 0.0)
        delta = jnp.sum(do_e * o, axis=-1, keepdims=True)
        p = jnp.exp(_causal_scores(qe, ke, row0) - lse_e)
        ve = jnp.where(_head_lanes(e, v2[:, sl].shape), v2[:, sl], 0.0)
        dvs.append(_dot_tn(p, do_e))
        ds = (p * (_dot_nt(do_e, ve) - delta)) * (QKH ** -0.5)
        dqs.append(jnp.dot(ds.astype(MM), ke.astype(MM), preferred_element_type=F32))
        dks.append(_dot_tn(ds, qe))
    return jnp.concatenate(dqs, axis=-1), jnp.concatenate(dks, axis=-1), jnp.concatenate(dvs, axis=-1), dsg


def _sg_fn(u, v, sgc, ln_g, ln_b, ws, bs):
    mu = jnp.mean(v, axis=-1, keepdims=True)
    xc = v - mu
    vn = xc * lax.rsqrt(jnp.mean(xc * xc, axis=-1, keepdims=True) + EPS) * ln_g + ln_b
    r = lax.broadcasted_iota(jnp.int32, (SGC, SGC), 0)
    c = lax.broadcasted_iota(jnp.int32, (SGC, SGC), 1)
    wt = [jnp.where(r >= c, w, 0.0) for w in ws]
    row_blocks = []
    for ch in range(u.shape[0] // SGC):
        col_blocks = []
        for g in range(SGG):
            blk = vn[SGC * ch:SGC * (ch + 1), LANES * g:LANES * (g + 1)]
            col_blocks.append(_mm(wt[g], blk) + bs[g])
        row_blocks.append(jnp.concatenate(col_blocks, axis=-1))
    mixed = jnp.concatenate(row_blocks, axis=0)
    return (u * mixed) * jax.nn.silu(sgc)


def _memkv_fn(mem, mem_g, wm, kg):
    kv = _mm(_rms_n(mem, mem_g, D), wm)
    ks = [_rms_n(kv[:, MHD * h:MHD * (h + 1)], kg, MHD) for h in range(MH)]
    return jnp.concatenate(ks, axis=-1), kv[:, MH * MHD:]


def _mem_fn(mq, sgd, k, v, qg):
    outs = []
    for h in range(MH):
        sl = slice(MHD * h, MHD * (h + 1))
        qh = _rms_n(mq[:, sl], qg, MHD)
        p = _softmax(_mm_nt(qh, k[:, sl]) * (MHD ** -0.5))
        outs.append(_mm(p, v[:, sl]))
    return jnp.concatenate(outs, axis=-1) * jax.nn.silu(sgd)


def _merge_fn(ys, logits, bm, wb, wo):
    merged = None
    for n in range(NB):
        z = jnp.concatenate([_mm(ys[n], wb[j][n]) for j in range(N_CHIPS)], axis=-1)
        gate = jax.nn.sigmoid(logits[:, D * n:D * (n + 1)] + bm[n])
        merged = gate * z if merged is None else merged + gate * z
    return _mm(merged, wo)


def _proj_call(x, g, w):
    s_len = x.shape[0]
    tm, tn = min(s_len, 1024), NP // 4

    def body(x_ref, g_ref, w_ref, p_ref, h_ref):
        @pl.when(pl.program_id(1) == 0)
        def _():
            h_ref[...] = _rms_n(x_ref[...], g_ref[...], D).astype(h_ref.dtype)
        p_ref[...] = jnp.dot(h_ref[...], w_ref[...], preferred_element_type=F32)

    return pl.pallas_call(
        body, grid=(s_len // tm, NP // tn),
        in_specs=[_bs((tm, D), lambda i, j: (i, 0)), _bs((1, D), lambda i, j: (0, 0)), _bs((D, tn), lambda i, j: (0, j))],
        out_specs=[_bs((tm, tn), lambda i, j: (i, j)), _bs((tm, D), lambda i, j: (i, 0))],
        out_shape=[jax.ShapeDtypeStruct((s_len, NP), F32), jax.ShapeDtypeStruct((s_len, D), MM)],
        name="proj", compiler_params=_cparams(2))(x, g, w)


def _rope_tables(pos):
    half = ROPE // 2
    inv_freq = ROPE_THETA ** (-jnp.arange(half, dtype=F32) / half)
    ang = pos.astype(F32)[:, None] * inv_freq
    cos, sin = jnp.cos(ang), jnp.sin(ang)
    s_len = pos.shape[0]
    z = lambda n: jnp.zeros((s_len, n), F32)
    cos_t = jnp.concatenate([jnp.ones((s_len, NOPE), F32), cos, cos, z(LANES - QKH)], axis=1)
    sin_a = jnp.concatenate([z(NOPE), -sin, z(LANES - NOPE - half)], axis=1)
    sin_b = jnp.concatenate([z(NOPE + half), sin, z(LANES - QKH)], axis=1)
    return cos_t, sin_a, sin_b


def _mla_prep_specs(tm):
    row = lambda w, off: _bs((tm, w), lambda i: (i, off // w))
    full2 = lambda a, b: _bs((a, b), lambda i: (0, 0))
    full3 = lambda a, b, c: _bs((a, b, c), lambda i: (0, 0, 0))
    tab = _bs((tm, LANES), lambda i: (i, 0))
    return [row(QL, OFF_CQ), row(KVL, OFF_CKV), row(LANES, OFF_KR), tab, tab, tab,
            full2(1, QL), full2(1, KVL), full2(1, LANES), full2(1, LANES),
            full3(H, QL, LANES), full3(H, KVL, LANES), full2(KVL, H * LANES)]


def _mla_prep_args(body_refs, wrap=lambda w: w):
    (cq, ckv, kr, ct, sa, sb, cqg, ckvg, qg, kg, wuq, wkn, wv) = body_refs
    return (cq[...], ckv[...], kr[...], ct[...], sa[...], sb[...], cqg[...], ckvg[...], qg[...], kg[...],
            [wrap(wuq[h]) for h in range(H)], [wrap(wkn[h]) for h in range(H)], wrap(wv[...]))


def _mla_prep_call(proj, tabs, cq_g, ckv_g, qg, kg, wuq, wkn, wv):
    s_len = proj.shape[0]
    tm = min(s_len, 256)

    def body(*refs):
        q_ref, k_ref, v_ref = refs[13:]
        q, k, v = _mla_prep_fn(*_mla_prep_args(refs[:13]))
        q_ref[...] = q.astype(q_ref.dtype)
        k_ref[...] = k.astype(k_ref.dtype)
        v_ref[...] = v.astype(v_ref.dtype)

    out = _bs((tm, H * LANES), lambda i: (i, 0))
    return pl.pallas_call(
        body, grid=(s_len // tm,), in_specs=_mla_prep_specs(tm), out_specs=[out, out, out],
        out_shape=[jax.ShapeDtypeStruct((s_len, H * LANES), MM)] * 3,
        name="mla_prep", compiler_params=_cparams(1))(proj, proj, proj, *tabs, cq_g, ckv_g, qg, kg, wuq, wkn, wv)


def _mla_prep_bwd_call(proj, tabs, cq_g, ckv_g, qg, kg, wuq, wkn, wv, dq, dk, dv):
    s_len = proj.shape[0]
    tm = min(s_len, 256)

    def body(*refs):
        dq_ref, dk_ref, dv_ref = refs[13:16]
        dcq_ref, dckv_ref, dkr_ref, dcqg_ref, dckvg_ref, dqg_ref, dkg_ref, dwuq_ref, dwkn_ref, dwv_ref = refs[16:]
        _, vjp = jax.vjp(_mla_prep_fn, *_mla_prep_args(refs[:13], _with_slot))
        (dcq, dckv, dkr, _, _, _, dcqg, dckvg, dqg, dkg, dwuq, dwkn, dwv) = vjp((dq_ref[...], dk_ref[...], dv_ref[...]))
        dwuq, dwkn, dwv = [d[1] for d in dwuq], [d[1] for d in dwkn], dwv[1]
        dcq_ref[...] = dcq.astype(dcq_ref.dtype)
        dckv_ref[...] = dckv.astype(dckv_ref.dtype)
        dkr_ref[...] = dkr.astype(dkr_ref.dtype)

        @pl.when(pl.program_id(0) == 0)
        def _():
            for r in (dcqg_ref, dckvg_ref, dqg_ref, dkg_ref, dwuq_ref, dwkn_ref, dwv_ref):
                r[...] = jnp.zeros_like(r)
        dcqg_ref[...] += dcqg
        dckvg_ref[...] += dckvg
        dqg_ref[...] += dqg
        dkg_ref[...] += dkg
        for h in range(H):
            dwuq_ref[h] += dwuq[h]
            dwkn_ref[h] += dwkn[h]
        dwv_ref[...] += dwv

    big = _bs((tm, H * LANES), lambda i: (i, 0))
    row = lambda w: _bs((tm, w), lambda i: (i, 0))
    full2 = lambda a, b: _bs((a, b), lambda i: (0, 0))
    full3 = lambda a, b, c: _bs((a, b, c), lambda i: (0, 0, 0))
    sd = jax.ShapeDtypeStruct
    return pl.pallas_call(
        body, grid=(s_len // tm,), in_specs=_mla_prep_specs(tm) + [big, big, big],
        out_specs=[row(QL), row(KVL), row(LANES), full2(1, QL), full2(1, KVL), full2(1, LANES), full2(1, LANES),
                   full3(H, QL, LANES), full3(H, KVL, LANES), full2(KVL, H * LANES)],
        out_shape=[sd((s_len, QL), MM), sd((s_len, KVL), MM), sd((s_len, LANES), MM), sd((1, QL), F32), sd((1, KVL), F32),
                   sd((1, LANES), F32), sd((1, LANES), F32), sd((H, QL, LANES), F32), sd((H, KVL, LANES), F32),
                   sd((KVL, H * LANES), F32)],
        name="mla_prep_bwd", compiler_params=_cparams(1))(proj, proj, proj, *tabs, cq_g, ckv_g, qg, kg, wuq, wkn, wv, dq, dk, dv)


def _attn_specs(s_len, tq):
    pair = 2 * LANES
    return [_bs((tq, pair), lambda p, i: (i, p)), _bs((s_len, pair), lambda p, i: (0, p)), _bs((s_len, pair), lambda p, i: (0, p)),
            _bs((tq, LANES), lambda p, i: (i, OFF_SG // LANES + p))]


def _attn_call(q, k, v, proj):
    s_len = q.shape[0]
    tq = min(s_len, 256)

    def body(q_ref, k_ref, v_ref, sg_ref, y_ref, o_ref, lse_ref):
        for n in range(s_len // tq):
            @pl.when(pl.program_id(1) == n)
            def _():
                kl = (n + 1) * tq
                o, lse = _attn_pair_fwd(q_ref[...], k_ref[:kl, :], v_ref[:kl, :], n * tq)
                y_ref[...] = (o * jax.nn.silu(sg_ref[...])).astype(y_ref.dtype)
                o_ref[...] = o
                lse_ref[...] = lse

    tile = _bs((tq, LANES), lambda p, i: (i, p))
    sd = jax.ShapeDtypeStruct
    return pl.pallas_call(
        body, grid=(H // 2, s_len // tq), in_specs=_attn_specs(s_len, tq), out_specs=[tile, tile, tile],
        out_shape=[sd((s_len, BW), MM), sd((s_len, BW), F32), sd((s_len, BW), F32)],
        name="attn", compiler_params=_cparams(2))(q, k, v, proj)


def _attn_bwd_call(q, k, v, proj, dys, o, lse):
    s_len = q.shape[0]
    tq = min(s_len, 256)
    pair = 2 * LANES

    def body(q_ref, k_ref, v_ref, sg_ref, dy_ref, o_ref, lse_ref, dq_ref, dk_ref, dv_ref, dsg_ref):
        i = pl.program_id(1)

        @pl.when(i == 0)
        def _():
            dk_ref[...] = jnp.zeros_like(dk_ref)
            dv_ref[...] = jnp.zeros_like(dv_ref)

        for n in range(s_len // tq):
            @pl.when(i == n)
            def _():
                kl = (n + 1) * tq
                dq, dk, dv, dsg = _attn_pair_bwd(q_ref[...], k_ref[:kl, :], v_ref[:kl, :], sg_ref[...], dy_ref[...],
                                                 o_ref[...], lse_ref[...], n * tq)
                dq_ref[...] = dq
                dsg_ref[...] = dsg.astype(dsg_ref.dtype)
                dk_ref[:kl, :] += dk
                dv_ref[:kl, :] += dv

    sd = jax.ShapeDtypeStruct
    tile = _bs((tq, LANES), lambda p, i: (i, p))
    return pl.pallas_call(
        body, grid=(H // 2, s_len // tq),
        in_specs=_attn_specs(s_len, tq) + [tile, tile, tile],
        out_specs=[_bs((tq, pair), lambda p, i: (i, p)), _bs((s_len, pair), lambda p, i: (0, p)),
                   _bs((s_len, pair), lambda p, i: (0, p)), tile],
        out_shape=[sd((s_len, H * LANES), F32), sd((s_len, H * LANES), F32), sd((s_len, H * LANES), F32), sd((s_len, BW), MM)],
        name="attn_bwd", compiler_params=_cparams(2))(q, k, v, proj, dys, o, lse)


def _shift_down(a, n):
    r = lax.broadcasted_iota(jnp.int32, a.shape, 0)
    return jnp.where(r >= n, pltpu.roll(a, n, 0), 0.0)


def _shift_up(a, n):
    s_len = a.shape[0]
    r = lax.broadcasted_iota(jnp.int32, a.shape, 0)
    return jnp.where(r < s_len - n, pltpu.roll(a, s_len - n, 0), 0.0)


def _conv_specs(s_len):
    col = lambda off: _bs((s_len, LANES), lambda j: (0, off // LANES + j))
    return [col(OFF_CV), col(OFF_CV + CW), col(OFF_CV + 2 * CW), col(OFF_SG + BW),
            _bs((3, LANES), lambda j: (0, j)), _bs((1, LANES), lambda j: (0, j))]


def _conv_call(proj, cw, cb):
    s_len = proj.shape[0]

    def body(bg_ref, cg_ref, xi_ref, sg_ref, w_ref, b_ref, y_ref):
        z = cg_ref[...] * xi_ref[...]
        y = b_ref[...] + w_ref[0:1, :] * _shift_down(z, 2)
        y = y + w_ref[1:2, :] * _shift_down(z, 1)
        y = y + w_ref[2:3, :] * z
        y_ref[...] = ((bg_ref[...] * y) * jax.nn.silu(sg_ref[...])).astype(y_ref.dtype)

    return pl.pallas_call(
        body, grid=(CW // LANES,), in_specs=_conv_specs(s_len), out_specs=_bs((s_len, LANES), lambda j: (0, j)),
        out_shape=jax.ShapeDtypeStruct((s_len, CW), MM), name="conv", compiler_params=_cparams(1))(proj, proj, proj, proj, cw, cb)


def _conv_bwd_call(proj, cw, cb, dys):
    s_len = proj.shape[0]

    def body(bg_ref, cg_ref, xi_ref, sg_ref, w_ref, b_ref, dys_ref, dbg_ref, dcg_ref, dxi_ref, dsg_ref, dw_ref, db_ref):
        bg, cg, xi, sg = bg_ref[...], cg_ref[...], xi_ref[...], sg_ref[...]
        w0, w1, w2 = w_ref[0:1, :], w_ref[1:2, :], w_ref[2:3, :]
        z = cg * xi
        z1, z2 = _shift_down(z, 1), _shift_down(z, 2)
        y = b_ref[...] + w0 * z2
        y = y + w1 * z1
        y = y + w2 * z
        yb = bg * y
        sig = jax.nn.sigmoid(sg)
        silu = sg * sig
        dys_v = dys_ref[...]
        dsg_ref[...] = (dys_v * yb * (sig * (1.0 + sg * (1.0 - sig)))).astype(dsg_ref.dtype)
        dyb = dys_v * silu
        dbg_ref[...] = (dyb * y).astype(dbg_ref.dtype)
        dy = dyb * bg
        db_ref[...] = jnp.sum(dy, axis=0, keepdims=True)
        dw_ref[0:1, :] = jnp.sum(dy * z2, axis=0, keepdims=True)
        dw_ref[1:2, :] = jnp.sum(dy * z1, axis=0, keepdims=True)
        dw_ref[2:3, :] = jnp.sum(dy * z, axis=0, keepdims=True)
        dz = w2 * dy + w1 * _shift_up(dy, 1) + w0 * _shift_up(dy, 2)
        dcg_ref[...] = (dz * xi).astype(dcg_ref.dtype)
        dxi_ref[...] = (dz * cg).astype(dxi_ref.dtype)

    col = _bs((s_len, LANES), lambda j: (0, j))
    sd = jax.ShapeDtypeStruct
    return pl.pallas_call(
        body, grid=(CW // LANES,), in_specs=_conv_specs(s_len) + [col],
        out_specs=[col, col, col, col, _bs((3, LANES), lambda j: (0, j)), _bs((1, LANES), lambda j: (0, j))],
        out_shape=[sd((s_len, CW), MM)] * 4 + [sd((3, CW), F32), sd((1, CW), F32)],
        name="conv_bwd", compiler_params=_cparams(1))(proj, proj, proj, proj, cw, cb, dys)


def _sg_specs(tm):
    row = lambda off: _bs((tm, SGW), lambda i: (i, off // SGW))
    return [row(OFF_SGI), row(OFF_SGI + SGW), row(OFF_SG + 2 * BW), _bs((1, SGW), lambda i: (0, 0)), _bs((1, SGW), lambda i: (0, 0)),
            _bs((SGG, SGC, SGC), lambda i: (0, 0, 0)), _bs((SGG, SGC, 1), lambda i: (0, 0, 0))]


def _sg_args(refs):
    u, v, sg, lg, lb, ws, bs = refs
    return (u[...], v[...], sg[...], lg[...], lb[...], [ws[g] for g in range(SGG)], [bs[g] for g in range(SGG)])


def _sg_call(proj, ln_g, ln_b, ws, bs):
    s_len = proj.shape[0]
    tm = min(s_len, 256)

    def body(*refs):
        refs[7][...] = _sg_fn(*_sg_args(refs[:7])).astype(refs[7].dtype)

    return pl.pallas_call(
        body, grid=(s_len // tm,), in_specs=_sg_specs(tm), out_specs=_bs((tm, SGW), lambda i: (i, 0)),
        out_shape=jax.ShapeDtypeStruct((s_len, SGW), MM), name="sgmlp", compiler_params=_cparams(1))(proj, proj, proj, ln_g, ln_b, ws, bs)


def _sg_bwd_call(proj, ln_g, ln_b, ws, bs, dys):
    s_len = proj.shape[0]
    tm = min(s_len, 256)

    def body(*refs):
        dys_ref = refs[7]
        du_ref, dv_ref, dsg_ref, dlg_ref, dlb_ref, dws_ref, dbs_ref = refs[8:]
        _, vjp = jax.vjp(_sg_fn, *_sg_args(refs[:7]))
        du, dv, dsg, dlg, dlb, dws, dbs = vjp(dys_ref[...])
        du_ref[...] = du.astype(du_ref.dtype)
        dv_ref[...] = dv.astype(dv_ref.dtype)
        dsg_ref[...] = dsg.astype(dsg_ref.dtype)

        @pl.when(pl.program_id(0) == 0)
        def _():
            for r in (dlg_ref, dlb_ref, dws_ref, dbs_ref):
                r[...] = jnp.zeros_like(r)
        dlg_ref[...] += dlg
        dlb_ref[...] += dlb
        for g in range(SGG):
            dws_ref[g] += dws[g]
            dbs_ref[g] += dbs[g]

    row = _bs((tm, SGW), lambda i: (i, 0))
    sd = jax.ShapeDtypeStruct
    return pl.pallas_call(
        body, grid=(s_len // tm,), in_specs=_sg_specs(tm) + [row],
        out_specs=[row, row, row, _bs((1, SGW), lambda i: (0, 0)), _bs((1, SGW), lambda i: (0, 0)),
                   _bs((SGG, SGC, SGC), lambda i: (0, 0, 0)), _bs((SGG, SGC, 1), lambda i: (0, 0, 0))],
        out_shape=[sd((s_len, SGW), MM)] * 3 + [sd((1, SGW), F32), sd((1, SGW), F32), sd((SGG, SGC, SGC), F32), sd((SGG, SGC, 1), F32)],
        name="sgmlp_bwd", compiler_params=_cparams(1))(proj, proj, proj, ln_g, ln_b, ws, bs, dys)


def _memkv_call(mem, mem_g, wm, kg):
    m_len = mem.shape[0]

    def body(mem_ref, g_ref, w_ref, kg_ref, k_ref, v_ref):
        k, v = _memkv_fn(mem_ref[...], g_ref[...], w_ref[...], kg_ref[...])
        k_ref[...] = k.astype(k_ref.dtype)
        v_ref[...] = v.astype(v_ref.dtype)

    return pl.pallas_call(body, out_shape=[jax.ShapeDtypeStruct((m_len, MH * MHD), MM)] * 2, name="memkv",
                          compiler_params=pltpu.CompilerParams(vmem_limit_bytes=VMEM_LIMIT))(mem, mem_g, wm, kg)


def _memkv_bwd_call(mem, mem_g, wm, kg, dk, dv):
    def body(mem_ref, g_ref, w_ref, kg_ref, dk_ref, dv_ref, dg_ref, dw_ref, dkg_ref):
        _, vjp = jax.vjp(_memkv_fn, mem_ref[...], g_ref[...], _with_slot(w_ref[...]), kg_ref[...])
        _, dg, dw, dkg = vjp((dk_ref[...], dv_ref[...]))
        dg_ref[...] = dg
        dw_ref[...] = dw[1]
        dkg_ref[...] = dkg

    sd = jax.ShapeDtypeStruct
    return pl.pallas_call(body, out_shape=[sd((1, D), F32), sd((D, 2 * MH * MHD), F32), sd((1, MHD), F32)], name="memkv_bwd",
                          compiler_params=pltpu.CompilerParams(vmem_limit_bytes=VMEM_LIMIT))(mem, mem_g, wm, kg, dk, dv)


def _mem_specs(tm, m_len):
    w = MH * MHD
    return [_bs((tm, w), lambda i: (i, OFF_MQ // w)), _bs((tm, BW), lambda i: (i, (OFF_SG + 3 * BW) // BW)),
            _bs((m_len, w), lambda i: (0, 0)), _bs((m_len, w), lambda i: (0, 0)), _bs((1, MHD), lambda i: (0, 0))]


def _mem_call(proj, k, v, qg):
    s_len, m_len = proj.shape[0], k.shape[0]
    tm = min(s_len, 256)

    def body(mq_ref, sg_ref, k_ref, v_ref, qg_ref, y_ref):
        y_ref[...] = _mem_fn(mq_ref[...], sg_ref[...], k_ref[...], v_ref[...], qg_ref[...]).astype(y_ref.dtype)

    return pl.pallas_call(
        body, grid=(s_len // tm,), in_specs=_mem_specs(tm, m_len), out_specs=_bs((tm, BW), lambda i: (i, 0)),
        out_shape=jax.ShapeDtypeStruct((s_len, BW), MM), name="memattn", compiler_params=_cparams(1))(proj, proj, k, v, qg)


def _mem_bwd_call(proj, k, v, qg, dys):
    s_len, m_len = proj.shape[0], k.shape[0]
    tm = min(s_len, 256)
    w = MH * MHD

    def body(mq_ref, sg_ref, k_ref, v_ref, qg_ref, dys_ref, dmq_ref, dsg_ref, dk_ref, dv_ref, dqg_ref):
        _, vjp = jax.vjp(_mem_fn, mq_ref[...], sg_ref[...], k_ref[...].astype(F32), v_ref[...].astype(F32), qg_ref[...])
        dmq, dsg, dk, dv, dqg = vjp(dys_ref[...])
        dmq_ref[...] = dmq.astype(dmq_ref.dtype)
        dsg_ref[...] = dsg.astype(dsg_ref.dtype)

        @pl.when(pl.program_id(0) == 0)
        def _():
            for r in (dk_ref, dv_ref, dqg_ref):
                r[...] = jnp.zeros_like(r)
        dk_ref[...] += dk
        dv_ref[...] += dv
        dqg_ref[...] += dqg

    row = _bs((tm, BW), lambda i: (i, 0))
    kv = _bs((m_len, w), lambda i: (0, 0))
    sd = jax.ShapeDtypeStruct
    return pl.pallas_call(
        body, grid=(s_len // tm,), in_specs=_mem_specs(tm, m_len) + [row],
        out_specs=[row, row, kv, kv, _bs((1, MHD), lambda i: (0, 0))],
        out_shape=[sd((s_len, w), MM), sd((s_len, BW), MM), sd((m_len, w), F32), sd((m_len, w), F32), sd((1, MHD), F32)],
        name="memattn_bwd", compiler_params=_cparams(1))(proj, proj, k, v, qg, dys)


def _merge_specs(tm):
    row = _bs((tm, BW), lambda i: (i, 0))
    return [row, row, row, row, _bs((tm, NB * D), lambda i: (i, OFF_ML // (NB * D))), _bs((NB, D), lambda i: (0, 0)),
            _bs((N_CHIPS, NB, BW, D // N_CHIPS), lambda i: (0, 0, 0, 0)), _bs((D, D), lambda i: (0, 0))]


def _merge_call(ys, proj, bm, wb, wo, x):
    s_len = proj.shape[0]
    tm = min(s_len, 256)

    def body(ya, yb, yc, yd, lg_ref, bm_ref, wb_ref, wo_ref, x_ref, o_ref):
        out = _merge_fn([r[...] for r in (ya, yb, yc, yd)], lg_ref[...], [bm_ref[n:n + 1, :] for n in range(NB)],
                        [[wb_ref[j, n] for n in range(NB)] for j in range(N_CHIPS)], wo_ref[...])
        o_ref[...] = x_ref[...] + out

    xrow = _bs((tm, D), lambda i: (i, 0))
    return pl.pallas_call(
        body, grid=(s_len // tm,), in_specs=_merge_specs(tm) + [xrow], out_specs=xrow,
        out_shape=jax.ShapeDtypeStruct((s_len, D), F32), name="merge", compiler_params=_cparams(1))(*ys, proj, bm, wb, wo, x)


def _merge_bwd_call(ys, proj, bm, wb, wo, dout):
    s_len = proj.shape[0]
    tm = min(s_len, 256)

    def body(ya, yb, yc, yd, lg_ref, bm_ref, wb_ref, wo_ref, do_ref, dya, dyb, dyc, dyd, dlg_ref, dbm_ref, dwb_ref, dwo_ref):
        fn = lambda ys_, lg_, bm_, wb_, wo_: _merge_fn(ys_, lg_, bm_, wb_, wo_)
        _, vjp = jax.vjp(fn, [r[...].astype(F32) for r in (ya, yb, yc, yd)], lg_ref[...], [bm_ref[n:n + 1, :] for n in range(NB)],
                         [[_with_slot(wb_ref[j, n]) for n in range(NB)] for j in range(N_CHIPS)], _with_slot(wo_ref[...]))
        dys, dlg, dbm, dwb, dwo = vjp(do_ref[...])
        dwb, dwo = [[d[1] for d in row] for row in dwb], dwo[1]
        for r, d in zip((dya, dyb, dyc, dyd), dys):
            r[...] = d
        dlg_ref[...] = dlg.astype(dlg_ref.dtype)

        @pl.when(pl.program_id(0) == 0)
        def _():
            for r in (dbm_ref, dwb_ref, dwo_ref):
                r[...] = jnp.zeros_like(r)
        for n in range(NB):
            dbm_ref[n:n + 1, :] += dbm[n]
            for j in range(N_CHIPS):
                dwb_ref[j, n] += dwb[j][n]
        dwo_ref[...] += dwo

    row = _bs((tm, BW), lambda i: (i, 0))
    sd = jax.ShapeDtypeStruct
    wb_shape = (N_CHIPS, NB, BW, D // N_CHIPS)
    return pl.pallas_call(
        body, grid=(s_len // tm,), in_specs=_merge_specs(tm) + [_bs((tm, D), lambda i: (i, 0))],
        out_specs=[row, row, row, row, _bs((tm, NB * D), lambda i: (i, 0)), _bs((NB, D), lambda i: (0, 0)),
                   _bs(wb_shape, lambda i: (0, 0, 0, 0)), _bs((D, D), lambda i: (0, 0))],
        out_shape=[sd((s_len, BW), F32)] * 4 + [sd((s_len, NB * D), MM), sd((NB, D), F32), sd(wb_shape, F32), sd((D, D), F32)],
        name="merge_bwd", compiler_params=_cparams(1))(*ys, proj, bm, wb, wo, dout)


def _dh_call(dproj, w, x, g, dout):
    s_len = x.shape[0]
    tm, tk = min(s_len, 512), NP // 4

    def body(dp_ref, w_ref, x_ref, g_ref, do_ref, dx_ref, dg_ref, acc_ref):
        i, k = pl.program_id(0), pl.program_id(1)

        @pl.when(k == 0)
        def _():
            acc_ref[...] = jnp.zeros_like(acc_ref)
        acc_ref[...] += lax.dot_general(dp_ref[...], w_ref[...], (((1,), (1,)), ((), ())), preferred_element_type=F32)

        @pl.when(k == pl.num_programs(1) - 1)
        def _():
            _, vjp = jax.vjp(lambda x_, g_: _rms_n(x_, g_, D), x_ref[...], g_ref[...])
            dxr, dgr = vjp(acc_ref[...])
            dx_ref[...] = do_ref[...] + dxr

            @pl.when(i == 0)
            def _():
                dg_ref[...] = jnp.zeros_like(dg_ref)
            dg_ref[...] += dgr

    row = _bs((tm, D), lambda i, k: (i, 0))
    return pl.pallas_call(
        body, grid=(s_len // tm, NP // tk),
        in_specs=[_bs((tm, tk), lambda i, k: (i, k)), _bs((D, tk), lambda i, k: (0, k)), row, _bs((1, D), lambda i, k: (0, 0)), row],
        out_specs=[row, _bs((1, D), lambda i, k: (0, 0))],
        out_shape=[jax.ShapeDtypeStruct((s_len, D), F32), jax.ShapeDtypeStruct((1, D), F32)],
        scratch_shapes=[pltpu.VMEM((tm, D), F32)], name="dh", compiler_params=_cparams(2))(dproj, w, x, g, dout)


def _dw_call(h, dproj):
    s_len = h.shape[0]
    tn = 512

    def body(h_ref, dp_ref, o_ref):
        o_ref[...] = lax.dot_general(h_ref[...], dp_ref[...], (((0,), (0,)), ((), ())), preferred_element_type=F32)

    return pl.pallas_call(
        body, grid=(NP // tn,), in_specs=[_bs((s_len, D), lambda j: (0, 0)), _bs((s_len, tn), lambda j: (0, j))],
        out_specs=_bs((D, tn), lambda j: (0, j)), out_shape=jax.ShapeDtypeStruct((D, NP), F32),
        name="dw_in", compiler_params=_cparams(1))(h, dproj)


def _loss_call(y, target):
    s_len = y.shape[0]
    tm = min(s_len, 512)

    def body(y_ref, t_ref, dy_ref, l_ref):
        e = y_ref[...] - t_ref[...]
        dy_ref[...] = e * (1.0 / D)

        @pl.when(pl.program_id(0) == 0)
        def _():
            l_ref[...] = jnp.zeros_like(l_ref)
        l_ref[...] += jnp.sum(e * e, axis=0, keepdims=True)

    row = _bs((tm, D), lambda i: (i, 0))
    return pl.pallas_call(
        body, grid=(s_len // tm,), in_specs=[row, row], out_specs=[row, _bs((1, D), lambda i: (0, 0))],
        out_shape=[jax.ShapeDtypeStruct((s_len, D), F32), jax.ShapeDtypeStruct((1, D), F32)],
        name="loss", compiler_params=_cparams(1))(y, target)


def _adamw_call(w, g, m, v, name):
    rows, cols = w.shape
    tr = min(_row_tile(rows), 128)

    def body(w_ref, g_ref, m_ref, v_ref, d_ref, nm_ref, nv_ref):
        gv = g_ref[...]
        m2 = ADAM_B1 * m_ref[...] + (1.0 - ADAM_B1) * gv
        v2 = ADAM_B2 * v_ref[...] + (1.0 - ADAM_B2) * (gv * gv)
        m_hat = m2 / (1.0 - ADAM_B1 ** ADAM_STEP)
        v_hat = v2 / (1.0 - ADAM_B2 ** ADAM_STEP)
        d_ref[...] = -ADAM_LR * (m_hat / (jnp.sqrt(v_hat) + ADAM_EPS) + ADAM_WD * w_ref[...])
        nm_ref[...] = m2
        nv_ref[...] = v2

    blk = _bs((tr, cols), lambda i: (i, 0))
    return pl.pallas_call(
        body, grid=(rows // tr,), in_specs=[blk] * 4, out_specs=[blk] * 3,
        out_shape=[jax.ShapeDtypeStruct((rows, cols), F32)] * 3, name=name, compiler_params=_cparams(1))(w, g, m, v)


def _adamw_layer_call(layer, ws, gs, ms, vs, prev, after, name, steps=8):
    n = len(ws)
    after = list(after)
    n_prev = 4 * n if prev is not None else 0

    def body(*refs):
        outs = refs[len(refs) - 4 * n:]
        for t in range(n):
            w_ref, g_ref, m_ref, v_ref = refs[t], refs[n + t], refs[2 * n + t], refs[3 * n + t]
            g_out, d_out, m_out, v_out = outs[4 * t:4 * t + 4]
            gv = g_ref[...]
            m2 = ADAM_B1 * m_ref[0] + (1.0 - ADAM_B1) * gv
            v2 = ADAM_B2 * v_ref[0] + (1.0 - ADAM_B2) * (gv * gv)
            m_hat = m2 / (1.0 - ADAM_B1 ** ADAM_STEP)
            v_hat = v2 / (1.0 - ADAM_B2 ** ADAM_STEP)
            g_out[0] = gv
            d_out[0] = -ADAM_LR * (m_hat / (jnp.sqrt(v_hat) + ADAM_EPS) + ADAM_WD * w_ref[0])
            m_out[0] = m2
            v_out[0] = v2

    def lay(a):
        return _bs((1, a.shape[1] // steps, a.shape[2]), lambda i: (layer, i, 0))

    in_specs = ([lay(a) for a in ws] + [_bs((g.shape[0] // steps, g.shape[1]), lambda i: (i, 0)) for g in gs]
                + [lay(a) for a in ms] + [lay(a) for a in vs] + [_ANY] * (n_prev + len(after)))
    return pl.pallas_call(
        body, grid=(steps,), in_specs=in_specs, out_specs=[lay(ws[t]) for t in range(n) for _ in range(4)],
        out_shape=[jax.ShapeDtypeStruct(ws[t].shape, F32) for t in range(n) for _ in range(4)],
        input_output_aliases={4 * n + q: q for q in range(n_prev)}, name=name, compiler_params=_cparams(1),
    )(*ws, *gs, *ms, *vs, *(prev if prev is not None else []), *after)


def _row_tile(rows):
    for cand in (512, 256, 128, 64, 32, 16, 8):
        if rows % cand == 0 and rows > cand:
            return cand
    return rows


def _pair_sum_call(grads, from_sibling, core, name):
    n = len(grads)

    def body(core_ref, *refs):
        for t in range(n):
            refs[2 * n + t][...] = (refs[t][...] + refs[n + t][...]).astype(MM)

    half = lambda g: (1, g.shape[1] // 2, g.shape[2])
    grid_spec = pltpu.PrefetchScalarGridSpec(
        num_scalar_prefetch=1, grid=(N_CHIPS,),
        in_specs=[pl.BlockSpec(half(g), lambda j, core_ref: (j, core_ref[0], 0)) for g in grads]
        + [pl.BlockSpec(half(g), lambda j, core_ref: (j, 0, 0)) for g in grads],
        out_specs=[pl.BlockSpec(half(g), lambda j, core_ref: (j, 0, 0)) for g in grads])
    return pl.pallas_call(
        body, grid_spec=grid_spec, out_shape=[jax.ShapeDtypeStruct((N_CHIPS,) + half(g)[1:], MM) for g in grads], name=name,
        compiler_params=_cparams(1))(core, *grads, *from_sibling)


def _owner_sum_call(chip_sums, from_chips, chip_core, name):
    n = len(chip_sums)
    steps = 4

    def body(ids_ref, *refs):
        for t in range(n):
            a, b = refs[t], refs[n + t]
            refs[2 * n + t][...] = ((a[0].astype(F32) + b[0].astype(F32)) + b[1].astype(F32)) + b[2].astype(F32)

    tile = lambda p: (p.shape[1] // steps, p.shape[2])
    grid_spec = pltpu.PrefetchScalarGridSpec(
        num_scalar_prefetch=1, grid=(steps,),
        in_specs=[pl.BlockSpec((1,) + tile(p), lambda i, ids_ref: (ids_ref[0], i, 0)) for p in chip_sums]
        + [pl.BlockSpec((3,) + tile(p), lambda i, ids_ref: (0, i, 0)) for p in chip_sums],
        out_specs=[pl.BlockSpec(tile(p), lambda i, ids_ref: (ids_ref[1] * steps + i, 0)) for p in chip_sums])
    return pl.pallas_call(
        body, grid_spec=grid_spec, out_shape=[jax.ShapeDtypeStruct((2 * p.shape[1], p.shape[2]), F32) for p in chip_sums],
        name=name, compiler_params=_cparams(1))(chip_core, *chip_sums, *from_chips)


def _sum8_call(parts):
    n, rows, cols = parts.shape
    tr = _row_tile(rows)

    def body(p_ref, o_ref):
        acc = p_ref[0]
        for k in range(1, n):
            acc = acc + p_ref[k]
        o_ref[...] = acc

    return pl.pallas_call(
        body, grid=(rows // tr,), in_specs=[_bs((n, tr, cols), lambda i: (0, i, 0))], out_specs=_bs((tr, cols), lambda i: (i, 0)),
        out_shape=jax.ShapeDtypeStruct((rows, cols), F32), name="sum_small_grads", compiler_params=_cparams(1))(parts)


_ANY = pl.BlockSpec(memory_space=pl.ANY)


def _all_gather8(blk, name):
    rows, cols = blk.shape

    def body(x_ref, out_ref, send_sems, recv_sems, local_sem):
        x, y, c = lax.axis_index("x"), lax.axis_index("y"), lax.axis_index("c")
        me, sibling = (x, y, c), (x, y, 1 - c)
        chips = [(1 - x, y), (x, 1 - y), (1 - x, 1 - y)]

        def slot(px, py, pc):
            return out_ref.at[4 * px + 2 * py + pc]

        def copy(k, block, to, src=None):
            return pltpu.make_async_remote_copy(
                src_ref=slot(*block) if src is None else src, dst_ref=slot(*block),
                send_sem=send_sems.at[k], recv_sem=recv_sems.at[k], device_id=to, device_id_type=MESH_ID)

        mine = pltpu.make_async_copy(x_ref, slot(*me), local_sem)
        mine.start()
        first = [copy(0, me, sibling, src=x_ref)]
        first += [copy(1 + j, me, (*chip, c), src=x_ref) for j, chip in enumerate(chips)]
        for cp in first:
            cp.start()
        passed = [copy(4 + j, (*chip, c), sibling) for j, chip in enumerate(chips)]
        for j, chip in enumerate(chips):
            copy(1 + j, (*chip, c), me).wait_recv()
            passed[j].start()
        copy(0, sibling, me).wait_recv()
        for j, chip in enumerate(chips):
            copy(4 + j, (*chip, 1 - c), me).wait_recv()
        for cp in first + passed:
            cp.wait_send()
        mine.wait()

    return pl.pallas_call(
        body, out_shape=jax.ShapeDtypeStruct((8, rows, cols), blk.dtype), in_specs=[_ANY], out_specs=_ANY,
        scratch_shapes=[pltpu.SemaphoreType.DMA((7,)), pltpu.SemaphoreType.DMA((7,)), pltpu.SemaphoreType.DMA],
        name=name)(blk)


def _half_rows(ref, lead, half, which):
    rows = pl.ds(pl.multiple_of(half * which, half), half)
    return ref.at[rows] if lead is None else ref.at[lead, rows]


def _gather_layer_call(layer, shards, name):
    n = len(shards)
    half = [s.shape[1] // 2 for s in shards]

    def body(*refs):
        srcs, outs = refs[:n], refs[n:2 * n]
        send_sems, recv_sems, local_sems = refs[2 * n:]
        x, y, c = lax.axis_index("x"), lax.axis_index("y"), lax.axis_index("c")
        sibling = (x, y, 1 - c)
        chips = [(1 - x, y), (x, 1 - y), (1 - x, 1 - y)]

        def slot(t, px, py, pc):
            return _half_rows(outs[t], 2 * px + py, half[t], pc)

        def copy(t, k, block, to, src=None):
            return pltpu.make_async_remote_copy(
                src_ref=slot(t, *block) if src is None else src, dst_ref=slot(t, *block),
                send_sem=send_sems.at[7 * t + k], recv_sem=recv_sems.at[7 * t + k], device_id=to, device_id_type=MESH_ID)

        mine = [_half_rows(srcs[t], layer, half[t], c) for t in range(n)]
        local = [pltpu.make_async_copy(mine[t], slot(t, x, y, c), local_sems.at[t]) for t in range(n)]
        for cp in local:
            cp.start()
        first = []
        for t in range(n):
            first.append(copy(t, 0, (x, y, c), sibling, src=mine[t]))
            first += [copy(t, 1 + j, (x, y, c), (*chip, c), src=mine[t]) for j, chip in enumerate(chips)]
        for cp in first:
            cp.start()
        passed = []
        for j, chip in enumerate(chips):
            for t in range(n):
                copy(t, 1 + j, (*chip, c), (x, y, c)).wait_recv()
                passed.append(copy(t, 4 + j, (*chip, c), sibling))
                passed[-1].start()
        for t in range(n):
            copy(t, 0, (x, y, 1 - c), (x, y, c)).wait_recv()
            for j, chip in enumerate(chips):
                copy(t, 4 + j, (*chip, 1 - c), (x, y, c)).wait_recv()
        for cp in first + passed:
            cp.wait_send()
        for cp in local:
            cp.wait()

    return pl.pallas_call(
        body, out_shape=[jax.ShapeDtypeStruct((N_CHIPS,) + s.shape[1:], s.dtype) for s in shards],
        in_specs=[_ANY] * n, out_specs=[_ANY] * n,
        scratch_shapes=[pltpu.SemaphoreType.DMA((7 * n,)), pltpu.SemaphoreType.DMA((7 * n,)), pltpu.SemaphoreType.DMA((n,))],
        name=name)(*shards)


_HBM = pl.BlockSpec(memory_space=pltpu.HBM)
_SEM = pl.BlockSpec(memory_space=pltpu.SEMAPHORE)
_ORDERED_EFFECT = pltpu.CompilerParams(has_side_effects=pltpu.SideEffectType.DATAFLOW_SIDE_EFFECTING)


_VMEM = pl.BlockSpec(memory_space=pltpu.VMEM)
_TOKEN = jax.ShapeDtypeStruct((8, LANES), F32)


def _in_hbm(a):
    return pltpu.with_memory_space_constraint(a, pltpu.HBM)


def _tie(small, token):
    return small + token[0:1, 0:1].reshape((1,) * small.ndim)


def _pair_exchange_start_call(grads, name):
    n = len(grads)
    half = [g.shape[1] // 2 for g in grads]

    def body(*refs):
        srcs, outs = refs[:n], refs[n:2 * n]
        send_sems, recv_sems, token = refs[2 * n:]
        x, y, c = lax.axis_index("x"), lax.axis_index("y"), lax.axis_index("c")
        for t in range(n):
            pltpu.make_async_remote_copy(
                src_ref=srcs[t].at[:, pl.ds(pl.multiple_of(half[t] * (1 - c), half[t]), half[t])], dst_ref=outs[t],
                send_sem=send_sems.at[t], recv_sem=recv_sems.at[t], device_id=(x, y, 1 - c), device_id_type=MESH_ID).start()
        token[...] = jnp.zeros_like(token)

    dma = pltpu.SemaphoreType.DMA
    return pl.pallas_call(
        body, out_shape=[pltpu.HBM((g.shape[0], g.shape[1] // 2, g.shape[2]), g.dtype) for g in grads] + [dma((n,)), dma((n,)), _TOKEN],
        in_specs=[_HBM] * n, out_specs=[_HBM] * n + [_SEM, _SEM, _VMEM], name=name, compiler_params=_ORDERED_EFFECT,
    )(*[_in_hbm(g) for g in grads])


def _pair_exchange_finish_call(grads, bufs, send_sems, recv_sems, after, name):
    n = len(grads)
    after = list(after)
    half = [g.shape[1] // 2 for g in grads]

    def body(*refs):
        srcs, ins, send_ref, recv_ref = refs[:n], refs[n:2 * n], refs[2 * n], refs[2 * n + 1]
        x, y, c = lax.axis_index("x"), lax.axis_index("y"), lax.axis_index("c")
        for t in range(n):
            pltpu.make_async_remote_copy(
                src_ref=srcs[t].at[:, pl.ds(pl.multiple_of(half[t] * (1 - c), half[t]), half[t])], dst_ref=ins[t],
                send_sem=send_ref.at[t], recv_sem=recv_ref.at[t], device_id=(x, y, 1 - c), device_id_type=MESH_ID).wait()

    return pl.pallas_call(
        body, out_shape=[pltpu.HBM(b.shape, b.dtype) for b in bufs],
        in_specs=[_HBM] * (2 * n) + [_SEM, _SEM] + [_ANY] * len(after), out_specs=[_HBM] * n,
        input_output_aliases={n + t: t for t in range(n)}, name=name, compiler_params=_ORDERED_EFFECT,
    )(*[_in_hbm(g) for g in grads], *bufs, send_sems, recv_sems, *after)


def _chip_scatter_start_call(chip_sums, name):
    n = len(chip_sums)

    def body(*refs):
        srcs, outs = refs[:n], refs[n:2 * n]
        send_sems, recv_sems, token = refs[2 * n:]
        x, y, c = lax.axis_index("x"), lax.axis_index("y"), lax.axis_index("c")
        chips = [(1 - x, y), (x, 1 - y), (1 - x, 1 - y)]
        for k, (cx, cy) in enumerate(chips):
            for t in range(n):
                pltpu.make_async_remote_copy(
                    src_ref=srcs[t].at[2 * cx + cy], dst_ref=outs[t].at[k], send_sem=send_sems.at[3 * t + k],
                    recv_sem=recv_sems.at[3 * t + k], device_id=(cx, cy, c), device_id_type=MESH_ID).start()
        token[...] = jnp.zeros_like(token)

    dma = pltpu.SemaphoreType.DMA
    return pl.pallas_call(
        body, out_shape=[pltpu.HBM((3,) + p.shape[1:], p.dtype) for p in chip_sums] + [dma((3 * n,)), dma((3 * n,)), _TOKEN],
        in_specs=[_HBM] * n, out_specs=[_HBM] * n + [_SEM, _SEM, _VMEM], name=name, compiler_params=_ORDERED_EFFECT,
    )(*[_in_hbm(p) for p in chip_sums])


def _chip_scatter_finish_call(chip_sums, bufs, send_sems, recv_sems, after, name):
    n = len(chip_sums)
    after = list(after)

    def body(*refs):
        srcs, ins, send_ref, recv_ref = refs[:n], refs[n:2 * n], refs[2 * n], refs[2 * n + 1]
        x, y, c = lax.axis_index("x"), lax.axis_index("y"), lax.axis_index("c")
        chips = [(1 - x, y), (x, 1 - y), (1 - x, 1 - y)]
        for k, (cx, cy) in enumerate(chips):
            for t in range(n):
                pltpu.make_async_remote_copy(
                    src_ref=srcs[t].at[2 * cx + cy], dst_ref=ins[t].at[k], send_sem=send_ref.at[3 * t + k],
                    recv_sem=recv_ref.at[3 * t + k], device_id=(cx, cy, c), device_id_type=MESH_ID).wait()

    return pl.pallas_call(
        body, out_shape=[pltpu.HBM(b.shape, b.dtype) for b in bufs],
        in_specs=[_HBM] * (2 * n) + [_SEM, _SEM] + [_ANY] * len(after), out_specs=[_HBM] * n,
        input_output_aliases={n + t: t for t in range(n)}, name=name, compiler_params=_ORDERED_EFFECT,
    )(*[_in_hbm(p) for p in chip_sums], *bufs, send_sems, recv_sems, *after)


def _place_own_call(mine, chip_core, name):
    n = len(mine)

    def body(ids_ref, *refs):
        for t in range(n):
            refs[n + t][0] = refs[t][...]

    def imap_out(s):
        pad = (0,) * (s.ndim - 1)
        return lambda i, ids_ref: (ids_ref[0], ids_ref[1]) + pad

    grid_spec = pltpu.PrefetchScalarGridSpec(
        num_scalar_prefetch=1, grid=(1,), in_specs=[pl.BlockSpec(s.shape, lambda i, ids_ref, k=s.ndim: (0,) * k) for s in mine],
        out_specs=[pl.BlockSpec((1,) + s.shape, imap_out(s)) for s in mine])
    return pl.pallas_call(
        body, grid_spec=grid_spec,
        out_shape=[jax.ShapeDtypeStruct((N_CHIPS, 2 * s.shape[0]) + s.shape[1:], s.dtype) for s in mine],
        name=name, compiler_params=_cparams(1))(chip_core, *mine)


def _gather_start_call(mine, bufs, after, name):
    n = len(mine)
    half = [s.shape[0] for s in mine]

    def body(*refs):
        srcs, outs = refs[:n], refs[2 * n + 1:3 * n + 1]
        send_sems, recv_sib, recv_ici, token = refs[3 * n + 1:]
        x, y, c = lax.axis_index("x"), lax.axis_index("y"), lax.axis_index("c")
        chips = [(1 - x, y), (x, 1 - y), (1 - x, 1 - y)]
        for t in range(n):
            dst = _half_rows(outs[t], 2 * x + y, half[t], c)
            pltpu.make_async_remote_copy(src_ref=srcs[t], dst_ref=dst, send_sem=send_sems.at[4 * t], recv_sem=recv_sib.at[t],
                                         device_id=(x, y, 1 - c), device_id_type=MESH_ID).start()
            for j, chip in enumerate(chips):
                pltpu.make_async_remote_copy(src_ref=srcs[t], dst_ref=dst, send_sem=send_sems.at[4 * t + 1 + j],
                                             recv_sem=recv_ici.at[3 * t + j], device_id=(*chip, c), device_id_type=MESH_ID).start()
        token[...] = jnp.zeros_like(token)

    dma = pltpu.SemaphoreType.DMA
    return pl.pallas_call(
        body, out_shape=[pltpu.HBM(b.shape, b.dtype) for b in bufs] + [dma((4 * n,)), dma((n,)), dma((3 * n,)), _TOKEN],
        in_specs=[_HBM] * (2 * n) + [_ANY], out_specs=[_HBM] * n + [_SEM] * 3 + [_VMEM],
        input_output_aliases={n + t: t for t in range(n)}, name=name, compiler_params=_ORDERED_EFFECT,
    )(*[_in_hbm(s) for s in mine], *[_in_hbm(b) for b in bufs], after)


def _gather_forward_call(bufs, recv_ici, after, name):
    n = len(bufs)
    half = [b.shape[1] // 2 for b in bufs]

    def body(*refs):
        ins, recv_ici_ref = refs[:n], refs[n]
        outs = refs[n + 2:2 * n + 2]
        send_fwd, recv_fwd, token = refs[2 * n + 2:]
        x, y, c = lax.axis_index("x"), lax.axis_index("y"), lax.axis_index("c")
        chips = [(1 - x, y), (x, 1 - y), (1 - x, 1 - y)]
        for j, (cx, cy) in enumerate(chips):
            for t in range(n):
                landed = _half_rows(ins[t], 2 * cx + cy, half[t], c)
                dst = _half_rows(outs[t], 2 * cx + cy, half[t], c)
                pltpu.make_async_remote_copy(src_ref=landed, dst_ref=landed, send_sem=send_fwd.at[3 * t + j],
                                             recv_sem=recv_ici_ref.at[3 * t + j], device_id=(cx, cy, c),
                                             device_id_type=MESH_ID).wait_recv()
                pltpu.make_async_remote_copy(src_ref=landed, dst_ref=dst, send_sem=send_fwd.at[3 * t + j],
                                             recv_sem=recv_fwd.at[3 * t + j], device_id=(x, y, 1 - c),
                                             device_id_type=MESH_ID).start()
        token[...] = jnp.zeros_like(token)

    dma = pltpu.SemaphoreType.DMA
    return pl.pallas_call(
        body, out_shape=[pltpu.HBM(b.shape, b.dtype) for b in bufs] + [dma((3 * n,)), dma((3 * n,)), _TOKEN],
        in_specs=[_HBM] * n + [_SEM, _ANY], out_specs=[_HBM] * n + [_SEM] * 2 + [_VMEM],
        input_output_aliases={t: t for t in range(n)}, name=name, compiler_params=_ORDERED_EFFECT,
    )(*bufs, recv_ici, after)


def _gather_finish_call(shards, bufs, send_sems, recv_sib, send_fwd, recv_fwd, after, name):
    n = len(bufs)
    half = [b.shape[1] // 2 for b in bufs]

    def body(*refs):
        srcs, ins = refs[:n], refs[n:2 * n]
        send_ref, recv_sib_ref, send_fwd_ref, recv_fwd_ref = refs[2 * n:2 * n + 4]
        x, y, c = lax.axis_index("x"), lax.axis_index("y"), lax.axis_index("c")
        chips = [(1 - x, y), (x, 1 - y), (1 - x, 1 - y)]
        sibling = (x, y, 1 - c)
        for t in range(n):
            for k in range(4):
                pltpu.make_async_remote_copy(src_ref=srcs[t], dst_ref=srcs[t], send_sem=send_ref.at[4 * t + k],
                                             recv_sem=recv_sib_ref.at[t], device_id=sibling, device_id_type=MESH_ID).wait_send()
            from_sibling = _half_rows(ins[t], 2 * x + y, half[t], 1 - c)
            pltpu.make_async_remote_copy(src_ref=from_sibling, dst_ref=from_sibling, send_sem=send_ref.at[4 * t],
                                         recv_sem=recv_sib_ref.at[t], device_id=sibling, device_id_type=MESH_ID).wait_recv()
            for j, (cx, cy) in enumerate(chips):
                sent = _half_rows(ins[t], 2 * cx + cy, half[t], c)
                passed = _half_rows(ins[t], 2 * cx + cy, half[t], 1 - c)
                pltpu.make_async_remote_copy(src_ref=sent, dst_ref=passed, send_sem=send_fwd_ref.at[3 * t + j],
                                             recv_sem=recv_fwd_ref.at[3 * t + j], device_id=sibling, device_id_type=MESH_ID).wait()

    return pl.pallas_call(
        body, out_shape=[pltpu.HBM(b.shape, b.dtype) for b in bufs],
        in_specs=[_HBM] * (2 * n) + [_SEM] * 4 + [_ANY], out_specs=[_HBM] * n,
        input_output_aliases={n + t: t for t in range(n)}, name=name, compiler_params=_ORDERED_EFFECT,
    )(*[_in_hbm(s) for s in shards], *bufs, send_sems, recv_sib, send_fwd, recv_fwd, after)


def _pair_exchange_call(grads, name):
    n = len(grads)
    half = [g.shape[1] // 2 for g in grads]

    def body(*refs):
        srcs, outs, send_sems, recv_sems = refs[:n], refs[n:2 * n], refs[2 * n], refs[2 * n + 1]
        x, y, c = lax.axis_index("x"), lax.axis_index("y"), lax.axis_index("c")
        copies = [pltpu.make_async_remote_copy(
            src_ref=srcs[t].at[:, pl.ds(pl.multiple_of(half[t] * (1 - c), half[t]), half[t])], dst_ref=outs[t],
            send_sem=send_sems.at[t], recv_sem=recv_sems.at[t], device_id=(x, y, 1 - c), device_id_type=MESH_ID) for t in range(n)]
        for cp in copies:
            cp.start()
        for cp in copies:
            cp.wait()

    return pl.pallas_call(
        body, out_shape=[jax.ShapeDtypeStruct((g.shape[0], g.shape[1] // 2, g.shape[2]), g.dtype) for g in grads],
        in_specs=[_ANY] * n, out_specs=[_ANY] * n,
        scratch_shapes=[pltpu.SemaphoreType.DMA((n,)), pltpu.SemaphoreType.DMA((n,))], name=name)(*grads)


def _chip_scatter_call(chip_sums, name):
    n = len(chip_sums)

    def body(*refs):
        srcs, outs, send_sems, recv_sems = refs[:n], refs[n:2 * n], refs[2 * n], refs[2 * n + 1]
        x, y, c = lax.axis_index("x"), lax.axis_index("y"), lax.axis_index("c")
        chips = [(1 - x, y), (x, 1 - y), (1 - x, 1 - y)]
        copies = [pltpu.make_async_remote_copy(
            src_ref=srcs[t].at[2 * cx + cy], dst_ref=outs[t].at[k], send_sem=send_sems.at[3 * t + k],
            recv_sem=recv_sems.at[3 * t + k], device_id=(cx, cy, c), device_id_type=MESH_ID)
            for k, (cx, cy) in enumerate(chips) for t in range(n)]
        for cp in copies:
            cp.start()
        for cp in copies:
            cp.wait()

    return pl.pallas_call(
        body, out_shape=[jax.ShapeDtypeStruct((3,) + p.shape[1:], p.dtype) for p in chip_sums],
        in_specs=[_ANY] * n, out_specs=[_ANY] * n,
        scratch_shapes=[pltpu.SemaphoreType.DMA((3 * n,)), pltpu.SemaphoreType.DMA((3 * n,))], name=name)(*chip_sums)


def _pair_gather_call(bufs, name):
    n = len(bufs)
    half = [b.shape[0] // 2 for b in bufs]

    def body(*refs):
        srcs, outs, send_sems, recv_sems = refs[:n], refs[n:2 * n], refs[2 * n], refs[2 * n + 1]
        x, y, c = lax.axis_index("x"), lax.axis_index("y"), lax.axis_index("c")
        for t in range(n):
            pltpu.make_async_remote_copy(
                src_ref=_half_rows(srcs[t], None, half[t], c), dst_ref=_half_rows(outs[t], None, half[t], c),
                send_sem=send_sems.at[t], recv_sem=recv_sems.at[t], device_id=(x, y, 1 - c), device_id_type=MESH_ID).start()
        for t in range(n):
            pltpu.make_async_remote_copy(
                src_ref=_half_rows(srcs[t], None, half[t], c), dst_ref=_half_rows(outs[t], None, half[t], 1 - c),
                send_sem=send_sems.at[t], recv_sem=recv_sems.at[t], device_id=(x, y, 1 - c), device_id_type=MESH_ID).wait()

    return pl.pallas_call(
        body, out_shape=[jax.ShapeDtypeStruct(b.shape, b.dtype) for b in bufs], in_specs=[_ANY] * n, out_specs=[_ANY] * n,
        input_output_aliases={t: t for t in range(n)},
        scratch_shapes=[pltpu.SemaphoreType.DMA((n,)), pltpu.SemaphoreType.DMA((n,))], name=name)(*bufs)


def _pack_rows(flats, dtype, row_multiple):
    flat = jnp.concatenate([f.reshape(-1).astype(dtype) for f in flats])
    n = flat.shape[0]
    rows = -(-n // PACK_W)
    rows = -(-rows // row_multiple) * row_multiple
    return jnp.pad(flat, (0, rows * PACK_W - n)).reshape(rows, PACK_W)


def _unpack(flat, shapes):
    out, off = [], 0
    for shp in shapes:
        n = math.prod(shp)
        out.append(flat[off:off + n].reshape(shp))
        off += n
    return out


def _f32_as_mm_bits(a):
    return lax.bitcast_convert_type(a, jnp.bfloat16).reshape(-1)


def _mm_bits_as_f32(flat, shape):
    return lax.bitcast_convert_type(flat.reshape(-1, 2), F32).reshape(shape)


_W_IN_SEGMENTS = ((R_ML, R_END, OFF_ML), (R_SG, R_ML, OFF_SG), (R_CV, R_SGI, OFF_CV), (R_SGI, R_MQ, OFF_SGI), (R_MQ, R_SG, OFF_MQ),
                  (R_CQ, R_CKV, OFF_CQ), (R_CKV, R_KR, OFF_CKV), (R_KR, R_CV, OFF_KR + NOPE))
W_IN_SHARD = R_END // N_CHIPS


def _realign_call(wg):
    tr = 128

    def body(w_ref, o_ref):
        pieces, pos = [], 0
        for r0, r1, a0 in _W_IN_SEGMENTS:
            if a0 > pos:
                pieces.append(jnp.zeros((tr, a0 - pos), o_ref.dtype))
            while r0 < r1:
                j = r0 // W_IN_SHARD
                hi = min(r1, (j + 1) * W_IN_SHARD)
                pieces.append(w_ref[j, :, r0 - j * W_IN_SHARD:hi - j * W_IN_SHARD])
                a0, r0 = a0 + hi - r0, hi
            pos = a0
        pieces.append(jnp.zeros((tr, NP - pos), o_ref.dtype))
        o_ref[...] = jnp.concatenate(pieces, axis=1)

    return pl.pallas_call(
        body, grid=(D // tr,), in_specs=[_bs((N_CHIPS, tr, W_IN_SHARD), lambda i: (0, i, 0))],
        out_specs=_bs((tr, NP), lambda i: (i, 0)), out_shape=jax.ShapeDtypeStruct((D, NP), wg.dtype),
        name="w_in_realign", compiler_params=_cparams(1))(wg)


def _unalign_call(dw):
    tr = 128
    by_ref = sorted(_W_IN_SEGMENTS)

    def body(dw_ref, o_ref):
        for j in range(N_CHIPS):
            lo_j, hi_j = j * W_IN_SHARD, (j + 1) * W_IN_SHARD
            pieces = []
            for r0, r1, a0 in by_ref:
                lo, hi = max(r0, lo_j), min(r1, hi_j)
                if lo < hi:
                    pieces.append(dw_ref[:, a0 + lo - r0:a0 + hi - r0])
            o_ref[j] = jnp.concatenate(pieces, axis=1)

    return pl.pallas_call(
        body, grid=(D // tr,), in_specs=[_bs((tr, NP), lambda i: (i, 0))],
        out_specs=_bs((N_CHIPS, tr, W_IN_SHARD), lambda i: (0, i, 0)),
        out_shape=jax.ShapeDtypeStruct((N_CHIPS, D, W_IN_SHARD), dw.dtype), name="w_in_unalign", compiler_params=_cparams(1))(dw)


def _w_in_to_aligned(w):
    z = lambda n: jnp.zeros((w.shape[0], n), w.dtype)
    return jnp.concatenate([w[:, R_ML:R_END], w[:, R_SG:R_ML], w[:, R_CV:R_SGI], w[:, R_SGI:R_MQ], w[:, R_MQ:R_SG],
                            w[:, R_CQ:R_CKV], w[:, R_CKV:R_KR], z(NOPE), w[:, R_KR:R_CV], z(LANES - QKH)], axis=1)


def _w_in_from_aligned(wa):
    return jnp.concatenate([wa[:, OFF_CQ:OFF_CKV], wa[:, OFF_CKV:OFF_KR], wa[:, OFF_KR + NOPE:OFF_KR + QKH], wa[:, OFF_CV:OFF_SGI],
                            wa[:, OFF_SGI:OFF_MQ], wa[:, OFF_MQ:OFF_CQ], wa[:, OFF_SG:OFF_CV], wa[:, OFF_ML:OFF_SG]], axis=1)


def _wuq_to_heads(w):
    w3 = w.reshape(QL, H, QKH)
    w3 = jnp.pad(w3, ((0, 0), (0, 0), (0, LANES - QKH)))
    return jnp.transpose(w3, (1, 0, 2))


def _wuq_from_heads(wh):
    return jnp.transpose(wh[:, :, :QKH], (1, 0, 2)).reshape(QL, H * QKH)


def _wukv_to_heads(w):
    w3 = w.reshape(KVL, H, NOPE + VH)
    wkn = jnp.transpose(jnp.pad(w3[:, :, :NOPE], ((0, 0), (0, 0), (0, LANES - NOPE))), (1, 0, 2))
    wv3 = w3[:, :, NOPE:]
    z = jnp.zeros((KVL, VH), w.dtype)
    cols = []
    for h in range(H):
        cols += [wv3[:, h], z] if h % 2 == 0 else [z, wv3[:, h]]
    return wkn, jnp.concatenate(cols, axis=1)


def _wukv_from_heads(wkn, wv):
    kn = jnp.transpose(wkn[:, :, :NOPE], (1, 0, 2))
    vs = jnp.stack([wv[:, LANES * h + VH * (h % 2):LANES * h + VH * (h % 2) + VH] for h in range(H)], axis=1)
    return jnp.concatenate([kn, vs], axis=2).reshape(KVL, H * (NOPE + VH))


def _layer_fwd(x, mem, tabs, p):
    proj, h = _proj_call(x, p["norm_g"], p["w_in"])
    if p.get("late") is not None:
        p = dict(p, **p["late"](proj))
    q, k, v = _mla_prep_call(proj, tabs, p["cq_g"], p["ckv_g"], p["qg"], p["kg"], p["wuq"], p["wkn"], p["wv"])
    ya, attn_o, attn_lse = _attn_call(q, k, v, proj)
    bm = p["bm"]
    if p.get("after_attn") is not None:
        bm = _tie(bm, p["after_attn"](ya))
    yb = _conv_call(proj, p["conv_w"], p["conv_b"])
    yc = _sg_call(proj, p["ln_g"], p["ln_b"], p["ws"], p["bs"])
    mk, mv = _memkv_call(mem, p["mem_g"], p["wm"], p["mkg"])
    yd = _mem_call(proj, mk, mv, p["mqg"])
    out = _merge_call((ya, yb, yc, yd), proj, bm, p["wb"], p["wo"], x)
    return out, dict(p=p, x=x, proj=proj, h=h, q=q, k=k, v=v, attn_o=attn_o, attn_lse=attn_lse, ys=(ya, yb, yc, yd), mk=mk, mv=mv)


def _layer_bwd(dout, mem, tabs, p, sv, start_after=None, after_mla=None, on_grads=None):
    proj = sv["proj"]
    bm = p["bm"] if start_after is None else _tie(p["bm"], start_after)
    dya, dyb, dyc, dyd, dml, dbm, dwb, dwo = _merge_bwd_call(sv["ys"], proj, bm, p["wb"], p["wo"], dout)
    dq, dk, dv, dsg_a = _attn_bwd_call(sv["q"], sv["k"], sv["v"], proj, dya, sv["attn_o"], sv["attn_lse"])
    dcq, dckv, dkr, dcqg, dckvg, dqg, dkg, dwuq, dwkn, dwv = _mla_prep_bwd_call(
        proj, tabs, p["cq_g"], p["ckv_g"], p["qg"], p["kg"], p["wuq"], p["wkn"], p["wv"], dq, dk, dv)
    if after_mla is not None:
        after_mla(dcq)
    dbg, dcg, dxi, dsg_b, dcw, dcb = _conv_bwd_call(proj, p["conv_w"], p["conv_b"], dyb)
    du, dvv, dsg_c, dlg, dlb, dws, dbs = _sg_bwd_call(proj, p["ln_g"], p["ln_b"], p["ws"], p["bs"], dyc)
    dmq, dsg_d, dmk, dmv, dmqg = _mem_bwd_call(proj, sv["mk"], sv["mv"], p["mqg"], dyd)
    dmem_g, dwm, dmkg = _memkv_bwd_call(mem, p["mem_g"], p["wm"], p["mkg"], dmk, dmv)
    dproj = jnp.concatenate([dml, dsg_a, dsg_b, dsg_c, dsg_d, dbg, dcg, dxi, du, dvv, dmq, dcq, dckv, dkr], axis=1)
    dw_in = _dw_call(sv["h"], dproj)
    grads = dict(cq_norm_g=dcqg[0], ckv_norm_g=dckvg[0], mla_q_norm_g=dqg[0, :QKH], mla_k_norm_g=dkg[0, :QKH],
                 conv_w=dcw, conv_b=dcb[0], sg_ln_g=dlg[0], sg_ln_b=dlb[0], w_spatial=dws, b_spatial=dbs[:, :, 0],
                 mem_norm_g=dmem_g[0], mem_q_norm_g=dmqg[0], mem_k_norm_g=dmkg[0], b_merge=dbm,
                 w_in_aligned=dw_in, wuq_heads=dwuq, wkn_heads=dwkn, wv_heads=dwv, w_mem_kv=dwm, w_branch_chips=dwb, w_out=dwo)
    norm_g = p["norm_g"]
    if on_grads is not None:
        norm_g = _tie(norm_g, on_grads(grads))
    dx, dnorm_g = _dh_call(dproj, p["w_in"], sv["x"], norm_g, dout)
    grads["norm_g"] = dnorm_g[0]
    return dx, grads


def _chips_to_cols(a):
    return jnp.concatenate([a[j] for j in range(N_CHIPS)], axis=1)


def _cols_to_chips(a):
    cols = a.shape[1] // N_CHIPS
    return jnp.stack([a[:, cols * j:cols * (j + 1)] for j in range(N_CHIPS)])


def _layer_params_first(l, rep, w_in_gathered, conv_w, b_merge):
    pad_g = lambda g: jnp.pad(g, (0, LANES - QKH)).reshape(1, LANES)
    return dict(
        norm_g=rep["norm_g"][l].reshape(1, D), w_in=_realign_call(w_in_gathered),
        cq_g=rep["cq_norm_g"][l].reshape(1, QL), ckv_g=rep["ckv_norm_g"][l].reshape(1, KVL),
        qg=pad_g(rep["mla_q_norm_g"][l]), kg=pad_g(rep["mla_k_norm_g"][l]),
        conv_w=conv_w, conv_b=rep["conv_b"][l].reshape(1, CW),
        ln_g=rep["sg_ln_g"][l].reshape(1, SGW), ln_b=rep["sg_ln_b"][l].reshape(1, SGW),
        ws=rep["w_spatial"][l], bs=rep["b_spatial"][l].reshape(SGG, SGC, 1),
        mem_g=rep["mem_norm_g"][l].reshape(1, D),
        mqg=rep["mem_q_norm_g"][l].reshape(1, MHD), mkg=rep["mem_k_norm_g"][l].reshape(1, MHD), bm=b_merge)


def _layer_params_rest(gathered):
    wkn, wv = _wukv_to_heads(_chips_to_cols(gathered["w_ukv"]))
    return dict(wuq=_wuq_to_heads(_chips_to_cols(gathered["w_uq"])), wkn=wkn, wv=wv,
                wm=gathered["w_mem_kv"].reshape(D, 2 * MH * MHD), wb=gathered["w_branch"], wo=gathered["w_out"].reshape(D, D))


def _layer_params(l, rep, gathered, conv_w, b_merge):
    return dict(_layer_params_first(l, rep, gathered["w_in"], conv_w, b_merge), **_layer_params_rest(gathered))


def _forward_backward(x, mem, pos, target, params, bwd_hooks=None):
    tabs = _rope_tables(pos)
    params = list(params)
    saved = []
    act = x
    for l in range(DEPTH):
        if callable(params[l]):
            params[l] = params[l](saved[-1], act)
        act, sv = _layer_fwd(act, mem, tabs, params[l])
        saved.append(sv)
    dy, sq = _loss_call(act, target)
    grads = [None] * DEPTH
    token = None
    for l in reversed(range(DEPTH)):
        hooks = dict(bwd_hooks[l]) if bwd_hooks else {}
        after_layer = hooks.pop("after_layer", None)
        dy, grads[l] = _layer_bwd(dy, mem, tabs, saved[l]["p"], saved[l], start_after=token, **hooks)
        token = after_layer(dy) if after_layer is not None else None
    return sq, dy, grads


_SHARDED_MM = ("w_in", "w_branch", "w_out", "w_mem_kv", "w_uq", "w_ukv")
_SHARDED_F32 = ("conv_w", "b_merge")
_REPLICATED = ("norm_g", "cq_norm_g", "ckv_norm_g", "mla_q_norm_g", "mla_k_norm_g", "conv_b", "sg_ln_g", "sg_ln_b",
               "w_spatial", "b_spatial", "mem_norm_g", "mem_q_norm_g", "mem_k_norm_g")
_ALL_REDUCED = _REPLICATED + _SHARDED_F32
_WEIGHTS = ("norm_g", "w_in", "cq_norm_g", "ckv_norm_g", "w_uq", "w_ukv", "mla_q_norm_g", "mla_k_norm_g", "conv_w", "conv_b",
            "sg_ln_g", "sg_ln_b", "w_spatial", "b_spatial", "mem_norm_g", "w_mem_kv", "mem_q_norm_g", "mem_k_norm_g",
            "b_merge", "w_branch", "w_out")
_BIG = ("w_in", "w_uq", "w_ukv", "w_mem_kv", "w_branch", "w_out")
_SMALL = tuple(n for n in _WEIGHTS if n not in _BIG)


def _gather_small_sharded(w):
    names = _SHARDED_F32
    packed = _pack_rows([w[n] for n in names], F32, 8)
    got = _all_gather8(packed, "gather_small_weights")
    per_chip = [_unpack(got[2 * j].reshape(-1), [w[n].shape for n in names]) for j in range(N_CHIPS)]
    return {n: jnp.concatenate([per_chip[j][t] for j in range(N_CHIPS)], axis=2) for t, n in enumerate(names)}


def _gather_layer(l, shards):
    srcs = [shards[n] for n in _SHARDED_MM]
    return dict(zip(_SHARDED_MM, _gather_layer_call(l, srcs, "gather_weights_l%d" % l)))


class _ReduceScatter:
    def __init__(self, layer):
        self.tag = "rs_l%d_" % layer

    def exchange(self, grads):
        self.tensors = [
            _unalign_call(grads["w_in_aligned"]),
            grads["w_branch_chips"].reshape(N_CHIPS, NB * BW, D // N_CHIPS),
            grads["w_out"].reshape(N_CHIPS, D // N_CHIPS, D),
            grads["w_mem_kv"].reshape(N_CHIPS, D // N_CHIPS, 2 * MH * MHD),
            _cols_to_chips(_wuq_from_heads(grads["wuq_heads"])),
            _cols_to_chips(_wukv_from_heads(grads["wkn_heads"], grads["wv_heads"])),
        ]
        n = len(self.tensors)
        out = _pair_exchange_start_call(self.tensors, self.tag + "exchange_start")
        self.ex_bufs, self.ex_send, self.ex_recv = out[:n], out[n], out[n + 1]
        return out[n + 2]

    def scatter(self, after):
        n = len(self.tensors)
        c = lax.axis_index("c")
        from_sibling = _pair_exchange_finish_call(self.tensors, self.ex_bufs, self.ex_send, self.ex_recv, after,
                                                  self.tag + "exchange_finish")
        self.chip_sums = _pair_sum_call(self.tensors, from_sibling, c.astype(jnp.int32).reshape(1), self.tag + "pair_sum")
        out = _chip_scatter_start_call(self.chip_sums, self.tag + "scatter_start")
        self.bufs, self.send_sems, self.recv_sems, self.token = out[:n], out[n], out[n + 1], out[n + 2]
        return self.token

    def finish(self, after):
        x, y, c = lax.axis_index("x"), lax.axis_index("y"), lax.axis_index("c")
        chip_core = jnp.stack([2 * x + y, c]).astype(jnp.int32)
        from_chips = _chip_scatter_finish_call(self.chip_sums, self.bufs, self.send_sems, self.recv_sems, after,
                                               self.tag + "scatter_finish")
        mine = _owner_sum_call(self.chip_sums, from_chips, chip_core, self.tag + "owner_sum")
        shard = dict(zip(_SHARDED_MM, _pair_gather_call(mine, self.tag + "pair_gather")))
        shard["w_branch"] = shard["w_branch"].reshape(NB, BW, D // N_CHIPS)
        return shard


def _all_reduce_small(g, sq):
    packed = _pack_rows([g[n] for n in _ALL_REDUCED] + [sq], F32, 64)
    got = _all_gather8(packed, "gather_small_grads")
    total = _sum8_call(got).reshape(-1)
    parts = _unpack(total, [g[n].shape for n in _ALL_REDUCED] + [sq.shape])
    out = dict(zip(_ALL_REDUCED, parts))
    sq_total = parts[-1]
    chip = 2 * lax.axis_index("x") + lax.axis_index("y")
    for n in _SHARDED_F32:
        size = out[n].shape[2] // N_CHIPS
        out[n] = lax.dynamic_slice_in_dim(out[n], chip * size, size, axis=2)
    return out, sq_total


def _adamw_small(w, g, m, v, token):
    delta, new_m, new_v = {}, {}, {}
    shapes = [w[n].shape for n in _SMALL]
    pk = lambda t: _pack_rows([t[n] for n in _SMALL], F32, 64)
    d, nm, nv = _adamw_call(pk(w), _tie(pk(g), token), pk(m), pk(v), "adamw_small")
    for out, packed in ((delta, d), (new_m, nm), (new_v, nv)):
        out.update(zip(_SMALL, _unpack(packed.reshape(-1), shapes)))
    return delta, new_m, new_v


def kernel(x, mem, positions, norm_g, w_in, cq_norm_g, ckv_norm_g, w_uq, w_ukv, mla_q_norm_g, mla_k_norm_g, conv_w, conv_b, sg_ln_g, sg_ln_b, w_spatial, b_spatial, mem_norm_g, w_mem_kv, mem_q_norm_g, mem_k_norm_g, b_merge, w_branch, w_out, loss_target, m_norm_g, m_w_in, m_cq_norm_g, m_ckv_norm_g, m_w_uq, m_w_ukv, m_mla_q_norm_g, m_mla_k_norm_g, m_conv_w, m_conv_b, m_sg_ln_g, m_sg_ln_b, m_w_spatial, m_b_spatial, m_mem_norm_g, m_w_mem_kv, m_mem_q_norm_g, m_mem_k_norm_g, m_b_merge, m_w_branch, m_w_out, v_norm_g, v_w_in, v_cq_norm_g, v_ckv_norm_g, v_w_uq, v_w_ukv, v_mla_q_norm_g, v_mla_k_norm_g, v_conv_w, v_conv_b, v_sg_ln_g, v_sg_ln_b, v_w_spatial, v_b_spatial, v_mem_norm_g, v_w_mem_kv, v_mem_q_norm_g, v_mem_k_norm_g, v_b_merge, v_w_branch, v_w_out):
    w = dict(norm_g=norm_g, w_in=w_in, cq_norm_g=cq_norm_g, ckv_norm_g=ckv_norm_g, w_uq=w_uq, w_ukv=w_ukv,
             mla_q_norm_g=mla_q_norm_g, mla_k_norm_g=mla_k_norm_g, conv_w=conv_w, conv_b=conv_b, sg_ln_g=sg_ln_g,
             sg_ln_b=sg_ln_b, w_spatial=w_spatial, b_spatial=b_spatial, mem_norm_g=mem_norm_g, w_mem_kv=w_mem_kv,
             mem_q_norm_g=mem_q_norm_g, mem_k_norm_g=mem_k_norm_g, b_merge=b_merge, w_branch=w_branch, w_out=w_out)
    m = dict(norm_g=m_norm_g, w_in=m_w_in, cq_norm_g=m_cq_norm_g, ckv_norm_g=m_ckv_norm_g, w_uq=m_w_uq, w_ukv=m_w_ukv,
             mla_q_norm_g=m_mla_q_norm_g, mla_k_norm_g=m_mla_k_norm_g, conv_w=m_conv_w, conv_b=m_conv_b, sg_ln_g=m_sg_ln_g,
             sg_ln_b=m_sg_ln_b, w_spatial=m_w_spatial, b_spatial=m_b_spatial, mem_norm_g=m_mem_norm_g, w_mem_kv=m_w_mem_kv,
             mem_q_norm_g=m_mem_q_norm_g, mem_k_norm_g=m_mem_k_norm_g, b_merge=m_b_merge, w_branch=m_w_branch, w_out=m_w_out)
    v = dict(norm_g=v_norm_g, w_in=v_w_in, cq_norm_g=v_cq_norm_g, ckv_norm_g=v_ckv_norm_g, w_uq=v_w_uq, w_ukv=v_w_ukv,
             mla_q_norm_g=v_mla_q_norm_g, mla_k_norm_g=v_mla_k_norm_g, conv_w=v_conv_w, conv_b=v_conv_b, sg_ln_g=v_sg_ln_g,
             sg_ln_b=v_sg_ln_b, w_spatial=v_w_spatial, b_spatial=v_b_spatial, mem_norm_g=v_mem_norm_g, w_mem_kv=v_w_mem_kv,
             mem_q_norm_g=v_mem_q_norm_g, mem_k_norm_g=v_mem_k_norm_g, b_merge=v_b_merge, w_branch=v_w_branch, w_out=v_w_out)

    chip_core = jnp.stack([2 * lax.axis_index("x") + lax.axis_index("y"), lax.axis_index("c")]).astype(jnp.int32)

    class Gather:
        def __init__(self, layer, names, after, tag):
            self.names, self.tag = names, tag
            core = lax.axis_index("c")
            halves = [w[n].shape[1] // 2 for n in names]
            self.srcs = [lax.dynamic_slice_in_dim(w[n][layer], core * h, h, axis=0).astype(MM) for n, h in zip(names, halves)]
            k = len(names)
            out = _gather_start_call(self.srcs, _place_own_call(self.srcs, chip_core, tag + "place_own"), after, tag + "start")
            self.bufs, self.send, self.recv_sib, self.recv_ici, self.token = out[:k], out[k], out[k + 1], out[k + 2], out[k + 3]

        def pass_on(self, after):
            k = len(self.names)
            out = _gather_forward_call(self.bufs, self.recv_ici, after, self.tag + "forward")
            self.bufs, self.send_fwd, self.recv_fwd = out[:k], out[k], out[k + 1]
            return out[k + 2]

        def finish(self, after):
            got = _gather_finish_call(self.srcs, self.bufs, self.send, self.recv_sib, self.send_fwd, self.recv_fwd, after,
                                      self.tag + "finish")
            return dict(zip(self.names, got))

    first = Gather(0, ("w_in",), chip_core, "gather_l0_w_in_")
    rest = Gather(0, _SHARDED_MM[1:], first.token, "gather_l0_rest_")
    later = Gather(1, _SHARDED_MM, rest.token, "gather_l1_")
    small = _gather_small_sharded(w)
    w_in0 = first.finish(first.pass_on(later.token))["w_in"]

    def rest_of_layer0(proj0):
        return _layer_params_rest(rest.finish(rest.pass_on(proj0)))

    def layer1_params(saved0, act0):
        return _layer_params(1, w, later.finish(act0), small["conv_w"][1], small["b_merge"][1])

    params0 = _layer_params_first(0, w, w_in0, small["conv_w"][0], small["b_merge"][0])
    params = [dict(params0, late=rest_of_layer0, after_attn=later.pass_on), layer1_params]
    rs = [_ReduceScatter(l) for l in range(DEPTH)]
    shard_grads = {}
    hooks = [dict(on_grads=rs[0].exchange), dict(on_grads=rs[1].exchange, after_layer=lambda dy: rs[1].scatter([dy]))]
    sq, grad_x, layer_grads = _forward_backward(x[0], mem[0], positions[0], loss_target[0], params, hooks)

    g, sq_total = _all_reduce_small({n: jnp.stack([layer_grads[l][n] for l in range(DEPTH)]) for n in _ALL_REDUCED}, sq)
    loss = 0.5 / D * jnp.sum(sq_total)
    scattering = rs[0].scatter([grad_x, g["norm_g"]])
    delta, new_m, new_v = _adamw_small(w, g, m, v, scattering)
    shard_grads[1] = rs[1].finish([scattering])
    as3d = lambda a: a.reshape(DEPTH, -1, a.shape[-1])
    as2d = lambda a: a.reshape(-1, a.shape[-1])
    others = _SHARDED_MM[1:]
    big = lambda t: [as3d(t[n]) for n in others]
    turned = lambda t: [jnp.swapaxes(t["w_in"], 1, 2)]
    assert W_IN_SHARD % (8 * 7) == 0

    def update(l, prev_in, prev_others):
        grads = shard_grads[l]
        upd_in = _adamw_layer_call(l, turned(w), [grads["w_in"].T], turned(m), turned(v), prev_in, [], "adamw_w_in_l%d" % l,
                                   steps=7)
        upd_others = _adamw_layer_call(l, big(w), [as2d(grads[n]) for n in others], big(m), big(v), prev_others, [],
                                       "adamw_l%d" % l)
        return upd_in, upd_others

    upd_in1, upd1 = update(1, None, None)
    shard_grads[0] = rs[0].finish([grad_x, upd_in1[0], upd1[0], delta["norm_g"]])
    upd_in, upd = update(0, upd_in1, upd1)
    g["w_in"], delta["w_in"], new_m["w_in"], new_v["w_in"] = [jnp.swapaxes(a, 1, 2) for a in upd_in]
    for t, n in enumerate(others):
        g[n], delta[n], new_m[n], new_v[n] = [a.reshape(w[n].shape) for a in upd[4 * t:4 * t + 4]]
    return (loss, grad_x[None], *[g[n] for n in _WEIGHTS], *[delta[n] for n in _WEIGHTS],
            *[new_m[n] for n in _WEIGHTS], *[new_v[n] for n in _WEIGHTS])
```

```python
import functools
import math

import jax
import jax.numpy as jnp
from jax import lax
from jax.experimental import pallas as pl
from jax.experimental.pallas import tpu as pltpu

F32 = jnp.float32
MM = jnp.bfloat16

D = 1024
DEPTH = 2
EPS = 1e-6
H = 8
NOPE = 64
ROPE = 32
QKH = 96
VH = 64
QL = 256
KVL = 128
ROPE_THETA = 10000.0
CW = 512
SGW = 512
SGG = 4
SGC = 128
MH = 4
MHD = 128
NB = 4
BW = 512
NEG_INF = -1e30
LANES = 128
N_CHIPS = 4

R_CQ, R_CKV, R_KR, R_CV, R_SGI, R_MQ, R_SG, R_ML, R_END = 0, 256, 384, 416, 1952, 2976, 3488, 5536, 9632
OFF_ML, OFF_SG, OFF_CV, OFF_SGI, OFF_MQ, OFF_CQ, OFF_CKV, OFF_KR, NP = 0, 4096, 6144, 7680, 8704, 9216, 9472, 9600, 9728

ADAM_LR = 0.001
ADAM_B1 = 0.9
ADAM_B2 = 0.999
ADAM_EPS = 1e-08
ADAM_WD = 0.01
ADAM_STEP = 10

VMEM_LIMIT = 56 * 1024 * 1024
PACK_W = 512
MESH_ID = pl.DeviceIdType.MESH


def _cparams(n_axes):
    return pltpu.CompilerParams(dimension_semantics=("arbitrary",) * n_axes, vmem_limit_bytes=VMEM_LIMIT)


def _bs(shape, imap):
    return pl.BlockSpec(shape, imap)


@jax.custom_vjp
def _mm_plain(a, b):
    return jnp.dot(a.astype(MM), b.astype(MM), preferred_element_type=F32)


def _mm_plain_fwd(a, b):
    return _mm_plain(a, b), (a, b)


def _mm_plain_bwd(res, g):
    a, b = res
    gm = g.astype(MM)
    da = lax.dot_general(gm, b.astype(MM), (((1,), (1,)), ((), ())), preferred_element_type=F32)
    db = lax.dot_general(a.astype(MM), gm, (((0,), (0,)), ((), ())), preferred_element_type=F32)
    return da.astype(a.dtype), db.astype(b.dtype)


_mm_plain.defvjp(_mm_plain_fwd, _mm_plain_bwd)


@jax.custom_vjp
def _mm_slot(a, w, slot):
    return jnp.dot(a.astype(MM), w.astype(MM), preferred_element_type=F32)


def _mm_slot_fwd(a, w, slot):
    return _mm_slot(a, w, slot), (a, w)


def _mm_slot_bwd(res, g):
    a, w = res
    gm = g.astype(MM)
    da = lax.dot_general(gm, w.astype(MM), (((1,), (1,)), ((), ())), preferred_element_type=F32)
    dw = lax.dot_general(a.astype(MM), gm, (((0,), (0,)), ((), ())), preferred_element_type=F32)
    return da.astype(a.dtype), jnp.zeros_like(w), dw


_mm_slot.defvjp(_mm_slot_fwd, _mm_slot_bwd)


def _mm(a, b):
    if isinstance(b, tuple):
        return _mm_slot(a, b[0], b[1])
    return _mm_plain(a, b)


def _with_slot(w):
    return (w, jnp.zeros(w.shape, F32))


@jax.custom_vjp
def _mm_nt(a, b):
    return lax.dot_general(a.astype(MM), b.astype(MM), (((1,), (1,)), ((), ())), preferred_element_type=F32)


def _mm_nt_fwd(a, b):
    return _mm_nt(a, b), (a, b)


def _mm_nt_bwd(res, g):
    a, b = res
    gm = g.astype(MM)
    da = jnp.dot(gm, b.astype(MM), preferred_element_type=F32)
    db = lax.dot_general(gm, a.astype(MM), (((0,), (0,)), ((), ())), preferred_element_type=F32)
    return da.astype(a.dtype), db.astype(b.dtype)


_mm_nt.defvjp(_mm_nt_fwd, _mm_nt_bwd)


@functools.partial(jax.custom_vjp, nondiff_argnums=(1,))
def _lane_roll(x, shift):
    return pltpu.roll(x, shift, 1)


def _lane_roll_fwd(x, shift):
    return pltpu.roll(x, shift, 1), None


def _lane_roll_bwd(shift, _, g):
    return (pltpu.roll(g, (LANES - shift) % LANES, 1),)


_lane_roll.defvjp(_lane_roll_fwd, _lane_roll_bwd)


def _rms_n(x, g, n):
    ms = jnp.sum(x * x, axis=-1, keepdims=True) * (1.0 / n)
    return x * lax.rsqrt(ms + EPS) * g


def _softmax(s):
    m = jnp.max(s, axis=-1, keepdims=True)
    e = jnp.exp(s - m)
    return e / jnp.sum(e, axis=-1, keepdims=True)


def _rope(t, cos_t, sin_a, sin_b):
    return t * cos_t + _lane_roll(t, LANES - 16) * sin_a + _lane_roll(t, 16) * sin_b


def _mla_prep_fn(cq, ckv, kr, cos_t, sin_a, sin_b, cq_g, ckv_g, qg, kg, wuq, wkn, wv):
    cqn = _rms_n(cq, cq_g, QL)
    ckvn = _rms_n(ckv, ckv_g, KVL)
    lane = lax.broadcasted_iota(jnp.int32, kr.shape, 1)
    krm = jnp.where((lane >= NOPE) & (lane < QKH), kr, 0.0)
    qs, ks = [], []
    for h in range(H):
        qh = _rms_n(_mm(cqn, wuq[h]), qg, QKH)
        qs.append(_rope(qh, cos_t, sin_a, sin_b))
        kh = _rms_n(_mm(ckvn, wkn[h]) + krm, kg, QKH)
        ks.append(_rope(kh, cos_t, sin_a, sin_b))
    return jnp.concatenate(qs, axis=-1), jnp.concatenate(ks, axis=-1), _mm(ckvn, wv)


def _dot_nt(a, b):
    return lax.dot_general(a.astype(MM), b.astype(MM), (((1,), (1,)), ((), ())), preferred_element_type=F32)


def _dot_tn(a, b):
    return lax.dot_general(a.astype(MM), b.astype(MM), (((0,), (0,)), ((), ())), preferred_element_type=F32)


def _causal_scores(qe, ke, row0):
    tq, kl = qe.shape[0], ke.shape[0]
    rows = row0 + lax.broadcasted_iota(jnp.int32, (tq, kl), 0)
    cols = lax.broadcasted_iota(jnp.int32, (tq, kl), 1)
    return jnp.where(cols <= rows, _dot_nt(qe, ke) * (QKH ** -0.5), NEG_INF)


def _head_lanes(e, shape):
    lane = lax.broadcasted_iota(jnp.int32, shape, len(shape) - 1)
    return (lane >= VH * e) & (lane < VH * (e + 1))


def _attn_pair_fwd(q2, k2, v2, row0):
    tq = q2.shape[0]
    o = jnp.zeros((tq, LANES), F32)
    lse = jnp.zeros((tq, LANES), F32)
    for e in range(2):
        sl = slice(LANES * e, LANES * (e + 1))
        s = _causal_scores(q2[:, sl], k2[:, sl], row0)
        m = jnp.max(s, axis=-1, keepdims=True)
        ex = jnp.exp(s - m)
        l = jnp.sum(ex, axis=-1, keepdims=True)
        ve = jnp.where(_head_lanes(e, v2[:, sl].shape), v2[:, sl], 0.0)
        o = o + jnp.dot((ex / l).astype(MM), ve.astype(MM), preferred_element_type=F32)
        lse = jnp.where(_head_lanes(e, lse.shape), m + jnp.log(l), lse)
    return o, lse


def _attn_pair_bwd(q2, k2, v2, sg, dys, o, lse, row0):
    sig = jax.nn.sigmoid(sg)
    do = dys * (sg * sig)
    dsg = dys * o * (sig * (1.0 + sg * (1.0 - sig)))
    dqs, dks, dvs = [], [], []
    for e in range(2):
        sl = slice(LANES * e, LANES * (e + 1))
        qe, ke = q2[:, sl], k2[:, sl]
        hm = _head_lanes(e, o.shape)
        lse_e = jnp.max(jnp.where(hm, lse, NEG_INF), axis=-1, keepdims=True)
        do_e = jnp.where(hm, do, 0.0)
        delta = jnp.sum(do_e * o, axis=-1, keepdims=True)
        p = jnp.exp(_causal_scores(qe, ke, row0) - lse_e)
        ve = jnp.where(_head_lanes(e, v2[:, sl].shape), v2[:, sl], 0.0)
        dvs.append(_dot_tn(p, do_e))
        ds = (p * (_dot_nt(do_e, ve) - delta)) * (QKH ** -0.5)
        dqs.append(jnp.dot(ds.astype(MM), ke.astype(MM), preferred_element_type=F32))
        dks.append(_dot_tn(ds, qe))
    return jnp.concatenate(dqs, axis=-1), jnp.concatenate(dks, axis=-1), jnp.concatenate(dvs, axis=-1), dsg


def _sg_fn(u, v, sgc, ln_g, ln_b, ws, bs):
    mu = jnp.mean(v, axis=-1, keepdims=True)
    xc = v - mu
    vn = xc * lax.rsqrt(jnp.mean(xc * xc, axis=-1, keepdims=True) + EPS) * ln_g + ln_b
    r = lax.broadcasted_iota(jnp.int32, (SGC, SGC), 0)
    c = lax.broadcasted_iota(jnp.int32, (SGC, SGC), 1)
    wt = [jnp.where(r >= c, w, 0.0) for w in ws]
    row_blocks = []
    for ch in range(u.shape[0] // SGC):
        col_blocks = []
        for g in range(SGG):
            blk = vn[SGC * ch:SGC * (ch + 1), LANES * g:LANES * (g + 1)]
            col_blocks.append(_mm(wt[g], blk) + bs[g])
        row_blocks.append(jnp.concatenate(col_blocks, axis=-1))
    mixed = jnp.concatenate(row_blocks, axis=0)
    return (u * mixed) * jax.nn.silu(sgc)


def _memkv_fn(mem, mem_g, wm, kg):
    kv = _mm(_rms_n(mem, mem_g, D), wm)
    ks = [_rms_n(kv[:, MHD * h:MHD * (h + 1)], kg, MHD) for h in range(MH)]
    return jnp.concatenate(ks, axis=-1), kv[:, MH * MHD:]


def _mem_fn(mq, sgd, k, v, qg):
    outs = []
    for h in range(MH):
        sl = slice(MHD * h, MHD * (h + 1))
        qh = _rms_n(mq[:, sl], qg, MHD)
        p = _softmax(_mm_nt(qh, k[:, sl]) * (MHD ** -0.5))
        outs.append(_mm(p, v[:, sl]))
    return jnp.concatenate(outs, axis=-1) * jax.nn.silu(sgd)


def _merge_fn(ys, logits, bm, wb, wo):
    merged = None
    for n in range(NB):
        z = jnp.concatenate([_mm(ys[n], wb[j][n]) for j in range(N_CHIPS)], axis=-1)
        gate = jax.nn.sigmoid(logits[:, D * n:D * (n + 1)] + bm[n])
        merged = gate * z if merged is None else merged + gate * z
    return _mm(merged, wo)


def _proj_call(x, g, w):
    s_len = x.shape[0]
    tm, tn = min(s_len, 1024), NP // 4

    def body(x_ref, g_ref, w_ref, p_ref, h_ref):
        @pl.when(pl.program_id(1) == 0)
        def _():
            h_ref[...] = _rms_n(x_ref[...], g_ref[...], D).astype(h_ref.dtype)
        p_ref[...] = jnp.dot(h_ref[...], w_ref[...], preferred_element_type=F32)

    return pl.pallas_call(
        body, grid=(s_len // tm, NP // tn),
        in_specs=[_bs((tm, D), lambda i, j: (i, 0)), _bs((1, D), lambda i, j: (0, 0)), _bs((D, tn), lambda i, j: (0, j))],
        out_specs=[_bs((tm, tn), lambda i, j: (i, j)), _bs((tm, D), lambda i, j: (i, 0))],
        out_shape=[jax.ShapeDtypeStruct((s_len, NP), F32), jax.ShapeDtypeStruct((s_len, D), MM)],
        name="proj", compiler_params=_cparams(2))(x, g, w)


def _rope_tables(pos):
    half = ROPE // 2
    inv_freq = ROPE_THETA ** (-jnp.arange(half, dtype=F32) / half)
    ang = pos.astype(F32)[:, None] * inv_freq
    cos, sin = jnp.cos(ang), jnp.sin(ang)
    s_len = pos.shape[0]
    z = lambda n: jnp.zeros((s_len, n), F32)
    cos_t = jnp.concatenate([jnp.ones((s_len, NOPE), F32), cos, cos, z(LANES - QKH)], axis=1)
    sin_a = jnp.concatenate([z(NOPE), -sin, z(LANES - NOPE - half)], axis=1)
    sin_b = jnp.concatenate([z(NOPE + half), sin, z(LANES - QKH)], axis=1)
    return cos_t, sin_a, sin_b


def _mla_prep_specs(tm):
    row = lambda w, off: _bs((tm, w), lambda i: (i, off // w))
    full2 = lambda a, b: _bs((a, b), lambda i: (0, 0))
    full3 = lambda a, b, c: _bs((a, b, c), lambda i: (0, 0, 0))
    tab = _bs((tm, LANES), lambda i: (i, 0))
    return [row(QL, OFF_CQ), row(KVL, OFF_CKV), row(LANES, OFF_KR), tab, tab, tab,
            full2(1, QL), full2(1, KVL), full2(1, LANES), full2(1, LANES),
            full3(H, QL, LANES), full3(H, KVL, LANES), full2(KVL, H * LANES)]


def _mla_prep_args(body_refs, wrap=lambda w: w):
    (cq, ckv, kr, ct, sa, sb, cqg, ckvg, qg, kg, wuq, wkn, wv) = body_refs
    return (cq[...], ckv[...], kr[...], ct[...], sa[...], sb[...], cqg[...], ckvg[...], qg[...], kg[...],
            [wrap(wuq[h]) for h in range(H)], [wrap(wkn[h]) for h in range(H)], wrap(wv[...]))


def _mla_prep_call(proj, tabs, cq_g, ckv_g, qg, kg, wuq, wkn, wv):
    s_len = proj.shape[0]
    tm = min(s_len, 256)

    def body(*refs):
        q_ref, k_ref, v_ref = refs[13:]
        q, k, v = _mla_prep_fn(*_mla_prep_args(refs[:13]))
        q_ref[...] = q.astype(q_ref.dtype)
        k_ref[...] = k.astype(k_ref.dtype)
        v_ref[...] = v.astype(v_ref.dtype)

    out = _bs((tm, H * LANES), lambda i: (i, 0))
    return pl.pallas_call(
        body, grid=(s_len // tm,), in_specs=_mla_prep_specs(tm), out_specs=[out, out, out],
        out_shape=[jax.ShapeDtypeStruct((s_len, H * LANES), MM)] * 3,
        name="mla_prep", compiler_params=_cparams(1))(proj, proj, proj, *tabs, cq_g, ckv_g, qg, kg, wuq, wkn, wv)


def _mla_prep_bwd_call(proj, tabs, cq_g, ckv_g, qg, kg, wuq, wkn, wv, dq, dk, dv):
    s_len = proj.shape[0]
    tm = min(s_len, 256)

    def body(*refs):
        dq_ref, dk_ref, dv_ref = refs[13:16]
        dcq_ref, dckv_ref, dkr_ref, dcqg_ref, dckvg_ref, dqg_ref, dkg_ref, dwuq_ref, dwkn_ref, dwv_ref = refs[16:]
        _, vjp = jax.vjp(_mla_prep_fn, *_mla_prep_args(refs[:13], _with_slot))
        (dcq, dckv, dkr, _, _, _, dcqg, dckvg, dqg, dkg, dwuq, dwkn, dwv) = vjp((dq_ref[...], dk_ref[...], dv_ref[...]))
        dwuq, dwkn, dwv = [d[1] for d in dwuq], [d[1] for d in dwkn], dwv[1]
        dcq_ref[...] = dcq.astype(dcq_ref.dtype)
        dckv_ref[...] = dckv.astype(dckv_ref.dtype)
        dkr_ref[...] = dkr.astype(dkr_ref.dtype)

        @pl.when(pl.program_id(0) == 0)
        def _():
            for r in (dcqg_ref, dckvg_ref, dqg_ref, dkg_ref, dwuq_ref, dwkn_ref, dwv_ref):
                r[...] = jnp.zeros_like(r)
        dcqg_ref[...] += dcqg
        dckvg_ref[...] += dckvg
        dqg_ref[...] += dqg
        dkg_ref[...] += dkg
        for h in range(H):
            dwuq_ref[h] += dwuq[h]
            dwkn_ref[h] += dwkn[h]
        dwv_ref[...] += dwv

    big = _bs((tm, H * LANES), lambda i: (i, 0))
    row = lambda w: _bs((tm, w), lambda i: (i, 0))
    full2 = lambda a, b: _bs((a, b), lambda i: (0, 0))
    full3 = lambda a, b, c: _bs((a, b, c), lambda i: (0, 0, 0))
    sd = jax.ShapeDtypeStruct
    return pl.pallas_call(
        body, grid=(s_len // tm,), in_specs=_mla_prep_specs(tm) + [big, big, big],
        out_specs=[row(QL), row(KVL), row(LANES), full2(1, QL), full2(1, KVL), full2(1, LANES), full2(1, LANES),
                   full3(H, QL, LANES), full3(H, KVL, LANES), full2(KVL, H * LANES)],
        out_shape=[sd((s_len, QL), MM), sd((s_len, KVL), MM), sd((s_len, LANES), MM), sd((1, QL), F32), sd((1, KVL), F32),
                   sd((1, LANES), F32), sd((1, LANES), F32), sd((H, QL, LANES), F32), sd((H, KVL, LANES), F32),
                   sd((KVL, H * LANES), F32)],
        name="mla_prep_bwd", compiler_params=_cparams(1))(proj, proj, proj, *tabs, cq_g, ckv_g, qg, kg, wuq, wkn, wv, dq, dk, dv)


def _attn_specs(s_len, tq):
    pair = 2 * LANES
    return [_bs((tq, pair), lambda p, i: (i, p)), _bs((s_len, pair), lambda p, i: (0, p)), _bs((s_len, pair), lambda p, i: (0, p)),
            _bs((tq, LANES), lambda p, i: (i, OFF_SG // LANES + p))]


def _attn_call(q, k, v, proj):
    s_len = q.shape[0]
    tq = min(s_len, 256)

    def body(q_ref, k_ref, v_ref, sg_ref, y_ref, o_ref, lse_ref):
        for n in range(s_len // tq):
            @pl.when(pl.program_id(1) == n)
            def _():
                kl = (n + 1) * tq
                o, lse = _attn_pair_fwd(q_ref[...], k_ref[:kl, :], v_ref[:kl, :], n * tq)
                y_ref[...] = (o * jax.nn.silu(sg_ref[...])).astype(y_ref.dtype)
                o_ref[...] = o
                lse_ref[...] = lse

    tile = _bs((tq, LANES), lambda p, i: (i, p))
    sd = jax.ShapeDtypeStruct
    return pl.pallas_call(
        body, grid=(H // 2, s_len // tq), in_specs=_attn_specs(s_len, tq), out_specs=[tile, tile, tile],
        out_shape=[sd((s_len, BW), MM), sd((s_len, BW), F32), sd((s_len, BW), F32)],
        name="attn", compiler_params=_cparams(2))(q, k, v, proj)


def _attn_bwd_call(q, k, v, proj, dys, o, lse):
    s_len = q.shape[0]
    tq = min(s_len, 256)
    pair = 2 * LANES

    def body(q_ref, k_ref, v_ref, sg_ref, dy_ref, o_ref, lse_ref, dq_ref, dk_ref, dv_ref, dsg_ref):
        i = pl.program_id(1)

        @pl.when(i == 0)
        def _():
            dk_ref[...] = jnp.zeros_like(dk_ref)
            dv_ref[...] = jnp.zeros_like(dv_ref)

        for n in range(s_len // tq):
            @pl.when(i == n)
            def _():
                kl = (n + 1) * tq
                dq, dk, dv, dsg = _attn_pair_bwd(q_ref[...], k_ref[:kl, :], v_ref[:kl, :], sg_ref[...], dy_ref[...],
                                                 o_ref[...], lse_ref[...], n * tq)
                dq_ref[...] = dq
                dsg_ref[...] = dsg.astype(dsg_ref.dtype)
                dk_ref[:kl, :] += dk
                dv_ref[:kl, :] += dv

    sd = jax.ShapeDtypeStruct
    tile = _bs((tq, LANES), lambda p, i: (i, p))
    return pl.pallas_call(
        body, grid=(H // 2, s_len // tq),
        in_specs=_attn_specs(s_len, tq) + [tile, tile, tile],
        out_specs=[_bs((tq, pair), lambda p, i: (i, p)), _bs((s_len, pair), lambda p, i: (0, p)),
                   _bs((s_len, pair), lambda p, i: (0, p)), tile],
        out_shape=[sd((s_len, H * LANES), F32), sd((s_len, H * LANES), F32), sd((s_len, H * LANES), F32), sd((s_len, BW), MM)],
        name="attn_bwd", compiler_params=_cparams(2))(q, k, v, proj, dys, o, lse)


def _shift_down(a, n):
    r = lax.broadcasted_iota(jnp.int32, a.shape, 0)
    return jnp.where(r >= n, pltpu.roll(a, n, 0), 0.0)


def _shift_up(a, n):
    s_len = a.shape[0]
    r = lax.broadcasted_iota(jnp.int32, a.shape, 0)
    return jnp.where(r < s_len - n, pltpu.roll(a, s_len - n, 0), 0.0)


def _conv_specs(s_len):
    col = lambda off: _bs((s_len, LANES), lambda j: (0, off // LANES + j))
    return [col(OFF_CV), col(OFF_CV + CW), col(OFF_CV + 2 * CW), col(OFF_SG + BW),
            _bs((3, LANES), lambda j: (0, j)), _bs((1, LANES), lambda j: (0, j))]


def _conv_call(proj, cw, cb):
    s_len = proj.shape[0]

    def body(bg_ref, cg_ref, xi_ref, sg_ref, w_ref, b_ref, y_ref):
        z = cg_ref[...] * xi_ref[...]
        y = b_ref[...] + w_ref[0:1, :] * _shift_down(z, 2)
        y = y + w_ref[1:2, :] * _shift_down(z, 1)
        y = y + w_ref[2:3, :] * z
        y_ref[...] = ((bg_ref[...] * y) * jax.nn.silu(sg_ref[...])).astype(y_ref.dtype)

    return pl.pallas_call(
        body, grid=(CW // LANES,), in_specs=_conv_specs(s_len), out_specs=_bs((s_len, LANES), lambda j: (0, j)),
        out_shape=jax.ShapeDtypeStruct((s_len, CW), MM), name="conv", compiler_params=_cparams(1))(proj, proj, proj, proj, cw, cb)


def _conv_bwd_call(proj, cw, cb, dys):
    s_len = proj.shape[0]

    def body(bg_ref, cg_ref, xi_ref, sg_ref, w_ref, b_ref, dys_ref, dbg_ref, dcg_ref, dxi_ref, dsg_ref, dw_ref, db_ref):
        bg, cg, xi, sg = bg_ref[...], cg_ref[...], xi_ref[...], sg_ref[...]
        w0, w1, w2 = w_ref[0:1, :], w_ref[1:2, :], w_ref[2:3, :]
        z = cg * xi
        z1, z2 = _shift_down(z, 1), _shift_down(z, 2)
        y = b_ref[...] + w0 * z2
        y = y + w1 * z1
        y = y + w2 * z
        yb = bg * y
        sig = jax.nn.sigmoid(sg)
        silu = sg * sig
        dys_v = dys_ref[...]
        dsg_ref[...] = (dys_v * yb * (sig * (1.0 + sg * (1.0 - sig)))).astype(dsg_ref.dtype)
        dyb = dys_v * silu
        dbg_ref[...] = (dyb * y).astype(dbg_ref.dtype)
        dy = dyb * bg
        db_ref[...] = jnp.sum(dy, axis=0, keepdims=True)
        dw_ref[0:1, :] = jnp.sum(dy * z2, axis=0, keepdims=True)
        dw_ref[1:2, :] = jnp.sum(dy * z1, axis=0, keepdims=True)
        dw_ref[2:3, :] = jnp.sum(dy * z, axis=0, keepdims=True)
        dz = w2 * dy + w1 * _shift_up(dy, 1) + w0 * _shift_up(dy, 2)
        dcg_ref[...] = (dz * xi).astype(dcg_ref.dtype)
        dxi_ref[...] = (dz * cg).astype(dxi_ref.dtype)

    col = _bs((s_len, LANES), lambda j: (0, j))
    sd = jax.ShapeDtypeStruct
    return pl.pallas_call(
        body, grid=(CW // LANES,), in_specs=_conv_specs(s_len) + [col],
        out_specs=[col, col, col, col, _bs((3, LANES), lambda j: (0, j)), _bs((1, LANES), lambda j: (0, j))],
        out_shape=[sd((s_len, CW), MM)] * 4 + [sd((3, CW), F32), sd((1, CW), F32)],
        name="conv_bwd", compiler_params=_cparams(1))(proj, proj, proj, proj, cw, cb, dys)


def _sg_specs(tm):
    row = lambda off: _bs((tm, SGW), lambda i: (i, off // SGW))
    return [row(OFF_SGI), row(OFF_SGI + SGW), row(OFF_SG + 2 * BW), _bs((1, SGW), lambda i: (0, 0)), _bs((1, SGW), lambda i: (0, 0)),
            _bs((SGG, SGC, SGC), lambda i: (0, 0, 0)), _bs((SGG, SGC, 1), lambda i: (0, 0, 0))]


def _sg_args(refs):
    u, v, sg, lg, lb, ws, bs = refs
    return (u[...], v[...], sg[...], lg[...], lb[...], [ws[g] for g in range(SGG)], [bs[g] for g in range(SGG)])


def _sg_call(proj, ln_g, ln_b, ws, bs):
    s_len = proj.shape[0]
    tm = min(s_len, 256)

    def body(*refs):
        refs[7][...] = _sg_fn(*_sg_args(refs[:7])).astype(refs[7].dtype)

    return pl.pallas_call(
        body, grid=(s_len // tm,), in_specs=_sg_specs(tm), out_specs=_bs((tm, SGW), lambda i: (i, 0)),
        out_shape=jax.ShapeDtypeStruct((s_len, SGW), MM), name="sgmlp", compiler_params=_cparams(1))(proj, proj, proj, ln_g, ln_b, ws, bs)


def _sg_bwd_call(proj, ln_g, ln_b, ws, bs, dys):
    s_len = proj.shape[0]
    tm = min(s_len, 256)

    def body(*refs):
        dys_ref = refs[7]
        du_ref, dv_ref, dsg_ref, dlg_ref, dlb_ref, dws_ref, dbs_ref = refs[8:]
        _, vjp = jax.vjp(_sg_fn, *_sg_args(refs[:7]))
        du, dv, dsg, dlg, dlb, dws, dbs = vjp(dys_ref[...])
        du_ref[...] = du.astype(du_ref.dtype)
        dv_ref[...] = dv.astype(dv_ref.dtype)
        dsg_ref[...] = dsg.astype(dsg_ref.dtype)

        @pl.when(pl.program_id(0) == 0)
        def _():
            for r in (dlg_ref, dlb_ref, dws_ref, dbs_ref):
                r[...] = jnp.zeros_like(r)
        dlg_ref[...] += dlg
        dlb_ref[...] += dlb
        for g in range(SGG):
            dws_ref[g] += dws[g]
            dbs_ref[g] += dbs[g]

    row = _bs((tm, SGW), lambda i: (i, 0))
    sd = jax.ShapeDtypeStruct
    return pl.pallas_call(
        body, grid=(s_len // tm,), in_specs=_sg_specs(tm) + [row],
        out_specs=[row, row, row, _bs((1, SGW), lambda i: (0, 0)), _bs((1, SGW), lambda i: (0, 0)),
                   _bs((SGG, SGC, SGC), lambda i: (0, 0, 0)), _bs((SGG, SGC, 1), lambda i: (0, 0, 0))],
        out_shape=[sd((s_len, SGW), MM)] * 3 + [sd((1, SGW), F32), sd((1, SGW), F32), sd((SGG, SGC, SGC), F32), sd((SGG, SGC, 1), F32)],
        name="sgmlp_bwd", compiler_params=_cparams(1))(proj, proj, proj, ln_g, ln_b, ws, bs, dys)


def _memkv_call(mem, mem_g, wm, kg):
    m_len = mem.shape[0]

    def body(mem_ref, g_ref, w_ref, kg_ref, k_ref, v_ref):
        k, v = _memkv_fn(mem_ref[...], g_ref[...], w_ref[...], kg_ref[...])
        k_ref[...] = k.astype(k_ref.dtype)
        v_ref[...] = v.astype(v_ref.dtype)

    return pl.pallas_call(body, out_shape=[jax.ShapeDtypeStruct((m_len, MH * MHD), MM)] * 2, name="memkv",
                          compiler_params=pltpu.CompilerParams(vmem_limit_bytes=VMEM_LIMIT))(mem, mem_g, wm, kg)


def _memkv_bwd_call(mem, mem_g, wm, kg, dk, dv):
    def body(mem_ref, g_ref, w_ref, kg_ref, dk_ref, dv_ref, dg_ref, dw_ref, dkg_ref):
        _, vjp = jax.vjp(_memkv_fn, mem_ref[...], g_ref[...], _with_slot(w_ref[...]), kg_ref[...])
        _, dg, dw, dkg = vjp((dk_ref[...], dv_ref[...]))
        dg_ref[...] = dg
        dw_ref[...] = dw[1]
        dkg_ref[...] = dkg

    sd = jax.ShapeDtypeStruct
    return pl.pallas_call(body, out_shape=[sd((1, D), F32), sd((D, 2 * MH * MHD), F32), sd((1, MHD), F32)], name="memkv_bwd",
                          compiler_params=pltpu.CompilerParams(vmem_limit_bytes=VMEM_LIMIT))(mem, mem_g, wm, kg, dk, dv)


def _mem_specs(tm, m_len):
    w = MH * MHD
    return [_bs((tm, w), lambda i: (i, OFF_MQ // w)), _bs((tm, BW), lambda i: (i, (OFF_SG + 3 * BW) // BW)),
            _bs((m_len, w), lambda i: (0, 0)), _bs((m_len, w), lambda i: (0, 0)), _bs((1, MHD), lambda i: (0, 0))]


def _mem_call(proj, k, v, qg):
    s_len, m_len = proj.shape[0], k.shape[0]
    tm = min(s_len, 256)

    def body(mq_ref, sg_ref, k_ref, v_ref, qg_ref, y_ref):
        y_ref[...] = _mem_fn(mq_ref[...], sg_ref[...], k_ref[...], v_ref[...], qg_ref[...]).astype(y_ref.dtype)

    return pl.pallas_call(
        body, grid=(s_len // tm,), in_specs=_mem_specs(tm, m_len), out_specs=_bs((tm, BW), lambda i: (i, 0)),
        out_shape=jax.ShapeDtypeStruct((s_len, BW), MM), name="memattn", compiler_params=_cparams(1))(proj, proj, k, v, qg)


def _mem_bwd_call(proj, k, v, qg, dys):
    s_len, m_len = proj.shape[0], k.shape[0]
    tm = min(s_len, 256)
    w = MH * MHD

    def body(mq_ref, sg_ref, k_ref, v_ref, qg_ref, dys_ref, dmq_ref, dsg_ref, dk_ref, dv_ref, dqg_ref):
        _, vjp = jax.vjp(_mem_fn, mq_ref[...], sg_ref[...], k_ref[...].astype(F32), v_ref[...].astype(F32), qg_ref[...])
        dmq, dsg, dk, dv, dqg = vjp(dys_ref[...])
        dmq_ref[...] = dmq.astype(dmq_ref.dtype)
        dsg_ref[...] = dsg.astype(dsg_ref.dtype)

        @pl.when(pl.program_id(0) == 0)
        def _():
            for r in (dk_ref, dv_ref, dqg_ref):
                r[...] = jnp.zeros_like(r)
        dk_ref[...] += dk
        dv_ref[...] += dv
        dqg_ref[...] += dqg

    row = _bs((tm, BW), lambda i: (i, 0))
    kv = _bs((m_len, w), lambda i: (0, 0))
    sd = jax.ShapeDtypeStruct
    return pl.pallas_call(
        body, grid=(s_len // tm,), in_specs=_mem_specs(tm, m_len) + [row],
        out_specs=[row, row, kv, kv, _bs((1, MHD), lambda i: (0, 0))],
        out_shape=[sd((s_len, w), MM), sd((s_len, BW), MM), sd((m_len, w), F32), sd((m_len, w), F32), sd((1, MHD), F32)],
        name="memattn_bwd", compiler_params=_cparams(1))(proj, proj, k, v, qg, dys)


def _merge_specs(tm):
    row = _bs((tm, BW), lambda i: (i, 0))
    return [row, row, row, row, _bs((tm, NB * D), lambda i: (i, OFF_ML // (NB * D))), _bs((NB, D), lambda i: (0, 0)),
            _bs((N_CHIPS, NB, BW, D // N_CHIPS), lambda i: (0, 0, 0, 0)), _bs((D, D), lambda i: (0, 0))]


def _merge_call(ys, proj, bm, wb, wo, x):
    s_len = proj.shape[0]
    tm = min(s_len, 256)

    def body(ya, yb, yc, yd, lg_ref, bm_ref, wb_ref, wo_ref, x_ref, o_ref):
        out = _merge_fn([r[...] for r in (ya, yb, yc, yd)], lg_ref[...], [bm_ref[n:n + 1, :] for n in range(NB)],
                        [[wb_ref[j, n] for n in range(NB)] for j in range(N_CHIPS)], wo_ref[...])
        o_ref[...] = x_ref[...] + out

    xrow = _bs((tm, D), lambda i: (i, 0))
    return pl.pallas_call(
        body, grid=(s_len // tm,), in_specs=_merge_specs(tm) + [xrow], out_specs=xrow,
        out_shape=jax.ShapeDtypeStruct((s_len, D), F32), name="merge", compiler_params=_cparams(1))(*ys, proj, bm, wb, wo, x)


def _merge_bwd_call(ys, proj, bm, wb, wo, dout):
    s_len = proj.shape[0]
    tm = min(s_len, 256)

    def body(ya, yb, yc, yd, lg_ref, bm_ref, wb_ref, wo_ref, do_ref, dya, dyb, dyc, dyd, dlg_ref, dbm_ref, dwb_ref, dwo_ref):
        fn = lambda ys_, lg_, bm_, wb_, wo_: _merge_fn(ys_, lg_, bm_, wb_, wo_)
        _, vjp = jax.vjp(fn, [r[...].astype(F32) for r in (ya, yb, yc, yd)], lg_ref[...], [bm_ref[n:n + 1, :] for n in range(NB)],
                         [[_with_slot(wb_ref[j, n]) for n in range(NB)] for j in range(N_CHIPS)], _with_slot(wo_ref[...]))
        dys, dlg, dbm, dwb, dwo = vjp(do_ref[...])
        dwb, dwo = [[d[1] for d in row] for row in dwb], dwo[1]
        for r, d in zip((dya, dyb, dyc, dyd), dys):
            r[...] = d
        dlg_ref[...] = dlg.astype(dlg_ref.dtype)

        @pl.when(pl.program_id(0) == 0)
        def _():
            for r in (dbm_ref, dwb_ref, dwo_ref):
                r[...] = jnp.zeros_like(r)
        for n in range(NB):
            dbm_ref[n:n + 1, :] += dbm[n]
            for j in range(N_CHIPS):
                dwb_ref[j, n] += dwb[j][n]
        dwo_ref[...] += dwo

    row = _bs((tm, BW), lambda i: (i, 0))
    sd = jax.ShapeDtypeStruct
    wb_shape = (N_CHIPS, NB, BW, D // N_CHIPS)
    return pl.pallas_call(
        body, grid=(s_len // tm,), in_specs=_merge_specs(tm) + [_bs((tm, D), lambda i: (i, 0))],
        out_specs=[row, row, row, row, _bs((tm, NB * D), lambda i: (i, 0)), _bs((NB, D), lambda i: (0, 0)),
                   _bs(wb_shape, lambda i: (0, 0, 0, 0)), _bs((D, D), lambda i: (0, 0))],
        out_shape=[sd((s_len, BW), F32)] * 4 + [sd((s_len, NB * D), MM), sd((NB, D), F32), sd(wb_shape, F32), sd((D, D), F32)],
        name="merge_bwd", compiler_params=_cparams(1))(*ys, proj, bm, wb, wo, dout)


def _dh_call(dproj, w, x, g, dout):
    s_len = x.shape[0]
    tm, tk = min(s_len, 512), NP // 4

    def body(dp_ref, w_ref, x_ref, g_ref, do_ref, dx_ref, dg_ref, acc_ref):
        i, k = pl.program_id(0), pl.program_id(1)

        @pl.when(k == 0)
        def _():
            acc_ref[...] = jnp.zeros_like(acc_ref)
        acc_ref[...] += lax.dot_general(dp_ref[...], w_ref[...], (((1,), (1,)), ((), ())), preferred_element_type=F32)

        @pl.when(k == pl.num_programs(1) - 1)
        def _():
            _, vjp = jax.vjp(lambda x_, g_: _rms_n(x_, g_, D), x_ref[...], g_ref[...])
            dxr, dgr = vjp(acc_ref[...])
            dx_ref[...] = do_ref[...] + dxr

            @pl.when(i == 0)
            def _():
                dg_ref[...] = jnp.zeros_like(dg_ref)
            dg_ref[...] += dgr

    row = _bs((tm, D), lambda i, k: (i, 0))
    return pl.pallas_call(
        body, grid=(s_len // tm, NP // tk),
        in_specs=[_bs((tm, tk), lambda i, k: (i, k)), _bs((D, tk), lambda i, k: (0, k)), row, _bs((1, D), lambda i, k: (0, 0)), row],
        out_specs=[row, _bs((1, D), lambda i, k: (0, 0))],
        out_shape=[jax.ShapeDtypeStruct((s_len, D), F32), jax.ShapeDtypeStruct((1, D), F32)],
        scratch_shapes=[pltpu.VMEM((tm, D), F32)], name="dh", compiler_params=_cparams(2))(dproj, w, x, g, dout)


def _dw_call(h, dproj):
    s_len = h.shape[0]
    tn = 512

    def body(h_ref, dp_ref, o_ref):
        o_ref[...] = lax.dot_general(h_ref[...], dp_ref[...], (((0,), (0,)), ((), ())), preferred_element_type=F32)

    return pl.pallas_call(
        body, grid=(NP // tn,), in_specs=[_bs((s_len, D), lambda j: (0, 0)), _bs((s_len, tn), lambda j: (0, j))],
        out_specs=_bs((D, tn), lambda j: (0, j)), out_shape=jax.ShapeDtypeStruct((D, NP), F32),
        name="dw_in", compiler_params=_cparams(1))(h, dproj)


def _loss_call(y, target):
    s_len = y.shape[0]
    tm = min(s_len, 512)

    def body(y_ref, t_ref, dy_ref, l_ref):
        e = y_ref[...] - t_ref[...]
        dy_ref[...] = e * (1.0 / D)

        @pl.when(pl.program_id(0) == 0)
        def _():
            l_ref[...] = jnp.zeros_like(l_ref)
        l_ref[...] += jnp.sum(e * e, axis=0, keepdims=True)

    row = _bs((tm, D), lambda i: (i, 0))
    return pl.pallas_call(
        body, grid=(s_len // tm,), in_specs=[row, row], out_specs=[row, _bs((1, D), lambda i: (0, 0))],
        out_shape=[jax.ShapeDtypeStruct((s_len, D), F32), jax.ShapeDtypeStruct((1, D), F32)],
        name="loss", compiler_params=_cparams(1))(y, target)


def _adamw_call(w, g, m, v, name):
    rows, cols = w.shape
    tr = min(_row_tile(rows), 128)

    def body(w_ref, g_ref, m_ref, v_ref, d_ref, nm_ref, nv_ref):
        gv = g_ref[...]
        m2 = ADAM_B1 * m_ref[...] + (1.0 - ADAM_B1) * gv
        v2 = ADAM_B2 * v_ref[...] + (1.0 - ADAM_B2) * (gv * gv)
        m_hat = m2 / (1.0 - ADAM_B1 ** ADAM_STEP)
        v_hat = v2 / (1.0 - ADAM_B2 ** ADAM_STEP)
        d_ref[...] = -ADAM_LR * (m_hat / (jnp.sqrt(v_hat) + ADAM_EPS) + ADAM_WD * w_ref[...])
        nm_ref[...] = m2
        nv_ref[...] = v2

    blk = _bs((tr, cols), lambda i: (i, 0))
    return pl.pallas_call(
        body, grid=(rows // tr,), in_specs=[blk] * 4, out_specs=[blk] * 3,
        out_shape=[jax.ShapeDtypeStruct((rows, cols), F32)] * 3, name=name, compiler_params=_cparams(1))(w, g, m, v)


def _adamw_layer_call(layer, ws, gs, ms, vs, prev, after, name, steps=8):
    n = len(ws)
    after = list(after)
    n_prev = 4 * n if prev is not None else 0

    def body(*refs):
        outs = refs[len(refs) - 4 * n:]
        for t in range(n):
            w_ref, g_ref, m_ref, v_ref = refs[t], refs[n + t], refs[2 * n + t], refs[3 * n + t]
            g_out, d_out, m_out, v_out = outs[4 * t:4 * t + 4]
            gv = g_ref[...]
            m2 = ADAM_B1 * m_ref[0] + (1.0 - ADAM_B1) * gv
            v2 = ADAM_B2 * v_ref[0] + (1.0 - ADAM_B2) * (gv * gv)
            m_hat = m2 / (1.0 - ADAM_B1 ** ADAM_STEP)
            v_hat = v2 / (1.0 - ADAM_B2 ** ADAM_STEP)
            g_out[0] = gv
            d_out[0] = -ADAM_LR * (m_hat / (jnp.sqrt(v_hat) + ADAM_EPS) + ADAM_WD * w_ref[0])
            m_out[0] = m2
            v_out[0] = v2

    def lay(a):
        return _bs((1, a.shape[1] // steps, a.shape[2]), lambda i: (layer, i, 0))

    in_specs = ([lay(a) for a in ws] + [_bs((g.shape[0] // steps, g.shape[1]), lambda i: (i, 0)) for g in gs]
                + [lay(a) for a in ms] + [lay(a) for a in vs] + [_ANY] * (n_prev + len(after)))
    return pl.pallas_call(
        body, grid=(steps,), in_specs=in_specs, out_specs=[lay(ws[t]) for t in range(n) for _ in range(4)],
        out_shape=[jax.ShapeDtypeStruct(ws[t].shape, F32) for t in range(n) for _ in range(4)],
        input_output_aliases={4 * n + q: q for q in range(n_prev)}, name=name, compiler_params=_cparams(1),
    )(*ws, *gs, *ms, *vs, *(prev if prev is not None else []), *after)


def _row_tile(rows):
    for cand in (512, 256, 128, 64, 32, 16, 8):
        if rows % cand == 0 and rows > cand:
            return cand
    return rows


def _pair_sum_call(grads, from_sibling, core, name):
    n = len(grads)

    def body(core_ref, *refs):
        for t in range(n):
            refs[2 * n + t][...] = (refs[t][...] + refs[n + t][...]).astype(MM)

    half = lambda g: (1, g.shape[1] // 2, g.shape[2])
    grid_spec = pltpu.PrefetchScalarGridSpec(
        num_scalar_prefetch=1, grid=(N_CHIPS,),
        in_specs=[pl.BlockSpec(half(g), lambda j, core_ref: (j, core_ref[0], 0)) for g in grads]
        + [pl.BlockSpec(half(g), lambda j, core_ref: (j, 0, 0)) for g in grads],
        out_specs=[pl.BlockSpec(half(g), lambda j, core_ref: (j, 0, 0)) for g in grads])
    return pl.pallas_call(
        body, grid_spec=grid_spec, out_shape=[jax.ShapeDtypeStruct((N_CHIPS,) + half(g)[1:], MM) for g in grads], name=name,
        compiler_params=_cparams(1))(core, *grads, *from_sibling)


def _owner_sum_call(chip_sums, from_chips, chip_core, name):
    n = len(chip_sums)
    steps = 4

    def body(ids_ref, *refs):
        for t in range(n):
            a, b = refs[t], refs[n + t]
            refs[2 * n + t][...] = ((a[0].astype(F32) + b[0].astype(F32)) + b[1].astype(F32)) + b[2].astype(F32)

    tile = lambda p: (p.shape[1] // steps, p.shape[2])
    grid_spec = pltpu.PrefetchScalarGridSpec(
        num_scalar_prefetch=1, grid=(steps,),
        in_specs=[pl.BlockSpec((1,) + tile(p), lambda i, ids_ref: (ids_ref[0], i, 0)) for p in chip_sums]
        + [pl.BlockSpec((3,) + tile(p), lambda i, ids_ref: (0, i, 0)) for p in chip_sums],
        out_specs=[pl.BlockSpec(tile(p), lambda i, ids_ref: (ids_ref[1] * steps + i, 0)) for p in chip_sums])
    return pl.pallas_call(
        body, grid_spec=grid_spec, out_shape=[jax.ShapeDtypeStruct((2 * p.shape[1], p.shape[2]), F32) for p in chip_sums],
        name=name, compiler_params=_cparams(1))(chip_core, *chip_sums, *from_chips)


def _sum8_call(parts):
    n, rows, cols = parts.shape
    tr = _row_tile(rows)

    def body(p_ref, o_ref):
        acc = p_ref[0]
        for k in range(1, n):
            acc = acc + p_ref[k]
        o_ref[...] = acc

    return pl.pallas_call(
        body, grid=(rows // tr,), in_specs=[_bs((n, tr, cols), lambda i: (0, i, 0))], out_specs=_bs((tr, cols), lambda i: (i, 0)),
        out_shape=jax.ShapeDtypeStruct((rows, cols), F32), name="sum_small_grads", compiler_params=_cparams(1))(parts)


_ANY = pl.BlockSpec(memory_space=pl.ANY)


def _all_gather8(blk, name):
    rows, cols = blk.shape

    def body(x_ref, out_ref, send_sems, recv_sems, local_sem):
        x, y, c = lax.axis_index("x"), lax.axis_index("y"), lax.axis_index("c")
        me, sibling = (x, y, c), (x, y, 1 - c)
        chips = [(1 - x, y), (x, 1 - y), (1 - x, 1 - y)]

        def slot(px, py, pc):
            return out_ref.at[4 * px + 2 * py + pc]

        def copy(k, block, to, src=None):
            return pltpu.make_async_remote_copy(
                src_ref=slot(*block) if src is None else src, dst_ref=slot(*block),
                send_sem=send_sems.at[k], recv_sem=recv_sems.at[k], device_id=to, device_id_type=MESH_ID)

        mine = pltpu.make_async_copy(x_ref, slot(*me), local_sem)
        mine.start()
        first = [copy(0, me, sibling, src=x_ref)]
        first += [copy(1 + j, me, (*chip, c), src=x_ref) for j, chip in enumerate(chips)]
        for cp in first:
            cp.start()
        passed = [copy(4 + j, (*chip, c), sibling) for j, chip in enumerate(chips)]
        for j, chip in enumerate(chips):
            copy(1 + j, (*chip, c), me).wait_recv()
            passed[j].start()
        copy(0, sibling, me).wait_recv()
        for j, chip in enumerate(chips):
            copy(4 + j, (*chip, 1 - c), me).wait_recv()
        for cp in first + passed:
            cp.wait_send()
        mine.wait()

    return pl.pallas_call(
        body, out_shape=jax.ShapeDtypeStruct((8, rows, cols), blk.dtype), in_specs=[_ANY], out_specs=_ANY,
        scratch_shapes=[pltpu.SemaphoreType.DMA((7,)), pltpu.SemaphoreType.DMA((7,)), pltpu.SemaphoreType.DMA],
        name=name)(blk)


def _half_rows(ref, lead, half, which):
    rows = pl.ds(pl.multiple_of(half * which, half), half)
    return ref.at[rows] if lead is None else ref.at[lead, rows]


def _gather_layer_call(layer, shards, name):
    n = len(shards)
    half = [s.shape[1] // 2 for s in shards]

    def body(*refs):
        srcs, outs = refs[:n], refs[n:2 * n]
        send_sems, recv_sems, local_sems = refs[2 * n:]
        x, y, c = lax.axis_index("x"), lax.axis_index("y"), lax.axis_index("c")
        sibling = (x, y, 1 - c)
        chips = [(1 - x, y), (x, 1 - y), (1 - x, 1 - y)]

        def slot(t, px, py, pc):
            return _half_rows(outs[t], 2 * px + py, half[t], pc)

        def copy(t, k, block, to, src=None):
            return pltpu.make_async_remote_copy(
                src_ref=slot(t, *block) if src is None else src, dst_ref=slot(t, *block),
                send_sem=send_sems.at[7 * t + k], recv_sem=recv_sems.at[7 * t + k], device_id=to, device_id_type=MESH_ID)

        mine = [_half_rows(srcs[t], layer, half[t], c) for t in range(n)]
        local = [pltpu.make_async_copy(mine[t], slot(t, x, y, c), local_sems.at[t]) for t in range(n)]
        for cp in local:
            cp.start()
        first = []
        for t in range(n):
            first.append(copy(t, 0, (x, y, c), sibling, src=mine[t]))
            first += [copy(t, 1 + j, (x, y, c), (*chip, c), src=mine[t]) for j, chip in enumerate(chips)]
        for cp in first:
            cp.start()
        passed = []
        for j, chip in enumerate(chips):
            for t in range(n):
                copy(t, 1 + j, (*chip, c), (x, y, c)).wait_recv()
                passed.append(copy(t, 4 + j, (*chip, c), sibling))
                passed[-1].start()
        for t in range(n):
            copy(t, 0, (x, y, 1 - c), (x, y, c)).wait_recv()
            for j, chip in enumerate(chips):
                copy(t, 4 + j, (*chip, 1 - c), (x, y, c)).wait_recv()
        for cp in first + passed:
            cp.wait_send()
        for cp in local:
            cp.wait()

    return pl.pallas_call(
        body, out_shape=[jax.ShapeDtypeStruct((N_CHIPS,) + s.shape[1:], s.dtype) for s in shards],
        in_specs=[_ANY] * n, out_specs=[_ANY] * n,
        scratch_shapes=[pltpu.SemaphoreType.DMA((7 * n,)), pltpu.SemaphoreType.DMA((7 * n,)), pltpu.SemaphoreType.DMA((n,))],
        name=name)(*shards)


_HBM = pl.BlockSpec(memory_space=pltpu.HBM)
_SEM = pl.BlockSpec(memory_space=pltpu.SEMAPHORE)
_ORDERED_EFFECT = pltpu.CompilerParams(has_side_effects=pltpu.SideEffectType.DATAFLOW_SIDE_EFFECTING)


_VMEM = pl.BlockSpec(memory_space=pltpu.VMEM)
_TOKEN = jax.ShapeDtypeStruct((8, LANES), F32)


def _in_hbm(a):
    return pltpu.with_memory_space_constraint(a, pltpu.HBM)


def _tie(small, token):
    return small + token[0:1, 0:1].reshape((1,) * small.ndim)


def _pair_exchange_start_call(grads, name):
    n = len(grads)
    half = [g.shape[1] // 2 for g in grads]

    def body(*refs):
        srcs, outs = refs[:n], refs[n:2 * n]
        send_sems, recv_sems, token = refs[2 * n:]
        x, y, c = lax.axis_index("x"), lax.axis_index("y"), lax.axis_index("c")
        for t in range(n):
            pltpu.make_async_remote_copy(
                src_ref=srcs[t].at[:, pl.ds(pl.multiple_of(half[t] * (1 - c), half[t]), half[t])], dst_ref=outs[t],
                send_sem=send_sems.at[t], recv_sem=recv_sems.at[t], device_id=(x, y, 1 - c), device_id_type=MESH_ID).start()
        token[...] = jnp.zeros_like(token)

    dma = pltpu.SemaphoreType.DMA
    return pl.pallas_call(
        body, out_shape=[pltpu.HBM((g.shape[0], g.shape[1] // 2, g.shape[2]), g.dtype) for g in grads] + [dma((n,)), dma((n,)), _TOKEN],
        in_specs=[_HBM] * n, out_specs=[_HBM] * n + [_SEM, _SEM, _VMEM], name=name, compiler_params=_ORDERED_EFFECT,
    )(*[_in_hbm(g) for g in grads])


def _pair_exchange_finish_call(grads, bufs, send_sems, recv_sems, after, name):
    n = len(grads)
    after = list(after)
    half = [g.shape[1] // 2 for g in grads]

    def body(*refs):
        srcs, ins, send_ref, recv_ref = refs[:n], refs[n:2 * n], refs[2 * n], refs[2 * n + 1]
        x, y, c = lax.axis_index("x"), lax.axis_index("y"), lax.axis_index("c")
        for t in range(n):
            pltpu.make_async_remote_copy(
                src_ref=srcs[t].at[:, pl.ds(pl.multiple_of(half[t] * (1 - c), half[t]), half[t])], dst_ref=ins[t],
                send_sem=send_ref.at[t], recv_sem=recv_ref.at[t], device_id=(x, y, 1 - c), device_id_type=MESH_ID).wait()

    return pl.pallas_call(
        body, out_shape=[pltpu.HBM(b.shape, b.dtype) for b in bufs],
        in_specs=[_HBM] * (2 * n) + [_SEM, _SEM] + [_ANY] * len(after), out_specs=[_HBM] * n,
        input_output_aliases={n + t: t for t in range(n)}, name=name, compiler_params=_ORDERED_EFFECT,
    )(*[_in_hbm(g) for g in grads], *bufs, send_sems, recv_sems, *after)


def _chip_scatter_start_call(chip_sums, name):
    n = len(chip_sums)

    def body(*refs):
        srcs, outs = refs[:n], refs[n:2 * n]
        send_sems, recv_sems, token = refs[2 * n:]
        x, y, c = lax.axis_index("x"), lax.axis_index("y"), lax.axis_index("c")
        chips = [(1 - x, y), (x, 1 - y), (1 - x, 1 - y)]
        for k, (cx, cy) in enumerate(chips):
            for t in range(n):
                pltpu.make_async_remote_copy(
                    src_ref=srcs[t].at[2 * cx + cy], dst_ref=outs[t].at[k], send_sem=send_sems.at[3 * t + k],
                    recv_sem=recv_sems.at[3 * t + k], device_id=(cx, cy, c), device_id_type=MESH_ID).start()
        token[...] = jnp.zeros_like(token)

    dma = pltpu.SemaphoreType.DMA
    return pl.pallas_call(
        body, out_shape=[pltpu.HBM((3,) + p.shape[1:], p.dtype) for p in chip_sums] + [dma((3 * n,)), dma((3 * n,)), _TOKEN],
        in_specs=[_HBM] * n, out_specs=[_HBM] * n + [_SEM, _SEM, _VMEM], name=name, compiler_params=_ORDERED_EFFECT,
    )(*[_in_hbm(p) for p in chip_sums])


def _chip_scatter_finish_call(chip_sums, bufs, send_sems, recv_sems, after, name):
    n = len(chip_sums)
    after = list(after)

    def body(*refs):
        srcs, ins, send_ref, recv_ref = refs[:n], refs[n:2 * n], refs[2 * n], refs[2 * n + 1]
        x, y, c = lax.axis_index("x"), lax.axis_index("y"), lax.axis_index("c")
        chips = [(1 - x, y), (x, 1 - y), (1 - x, 1 - y)]
        for k, (cx, cy) in enumerate(chips):
            for t in range(n):
                pltpu.make_async_remote_copy(
                    src_ref=srcs[t].at[2 * cx + cy], dst_ref=ins[t].at[k], send_sem=send_ref.at[3 * t + k],
                    recv_sem=recv_ref.at[3 * t + k], device_id=(cx, cy, c), device_id_type=MESH_ID).wait()

    return pl.pallas_call(
        body, out_shape=[pltpu.HBM(b.shape, b.dtype) for b in bufs],
        in_specs=[_HBM] * (2 * n) + [_SEM, _SEM] + [_ANY] * len(after), out_specs=[_HBM] * n,
        input_output_aliases={n + t: t for t in range(n)}, name=name, compiler_params=_ORDERED_EFFECT,
    )(*[_in_hbm(p) for p in chip_sums], *bufs, send_sems, recv_sems, *after)


def _place_own_call(mine, chip_core, name):
    n = len(mine)

    def body(ids_ref, *refs):
        for t in range(n):
            refs[n + t][0] = refs[t][...]

    def imap_out(s):
        pad = (0,) * (s.ndim - 1)
        return lambda i, ids_ref: (ids_ref[0], ids_ref[1]) + pad

    grid_spec = pltpu.PrefetchScalarGridSpec(
        num_scalar_prefetch=1, grid=(1,), in_specs=[pl.BlockSpec(s.shape, lambda i, ids_ref, k=s.ndim: (0,) * k) for s in mine],
        out_specs=[pl.BlockSpec((1,) + s.shape, imap_out(s)) for s in mine])
    return pl.pallas_call(
        body, grid_spec=grid_spec,
        out_shape=[jax.ShapeDtypeStruct((N_CHIPS, 2 * s.shape[0]) + s.shape[1:], s.dtype) for s in mine],
        name=name, compiler_params=_cparams(1))(chip_core, *mine)


def _gather_start_call(mine, bufs, after, name):
    n = len(mine)
    half = [s.shape[0] for s in mine]

    def body(*refs):
        srcs, outs = refs[:n], refs[2 * n + 1:3 * n + 1]
        send_sems, recv_sib, recv_ici, token = refs[3 * n + 1:]
        x, y, c = lax.axis_index("x"), lax.axis_index("y"), lax.axis_index("c")
        chips = [(1 - x, y), (x, 1 - y), (1 - x, 1 - y)]
        for t in range(n):
            dst = _half_rows(outs[t], 2 * x + y, half[t], c)
            pltpu.make_async_remote_copy(src_ref=srcs[t], dst_ref=dst, send_sem=send_sems.at[4 * t], recv_sem=recv_sib.at[t],
                                         device_id=(x, y, 1 - c), device_id_type=MESH_ID).start()
            for j, chip in enumerate(chips):
                pltpu.make_async_remote_copy(src_ref=srcs[t], dst_ref=dst, send_sem=send_sems.at[4 * t + 1 + j],
                                             recv_sem=recv_ici.at[3 * t + j], device_id=(*chip, c), device_id_type=MESH_ID).start()
        token[...] = jnp.zeros_like(token)

    dma = pltpu.SemaphoreType.DMA
    return pl.pallas_call(
        body, out_shape=[pltpu.HBM(b.shape, b.dtype) for b in bufs] + [dma((4 * n,)), dma((n,)), dma((3 * n,)), _TOKEN],
        in_specs=[_HBM] * (2 * n) + [_ANY], out_specs=[_HBM] * n + [_SEM] * 3 + [_VMEM],
        input_output_aliases={n + t: t for t in range(n)}, name=name, compiler_params=_ORDERED_EFFECT,
    )(*[_in_hbm(s) for s in mine], *[_in_hbm(b) for b in bufs], after)


def _gather_forward_call(bufs, recv_ici, after, name):
    n = len(bufs)
    half = [b.shape[1] // 2 for b in bufs]

    def body(*refs):
        ins, recv_ici_ref = refs[:n], refs[n]
        outs = refs[n + 2:2 * n + 2]
        send_fwd, recv_fwd, token = refs[2 * n + 2:]
        x, y, c = lax.axis_index("x"), lax.axis_index("y"), lax.axis_index("c")
        chips = [(1 - x, y), (x, 1 - y), (1 - x, 1 - y)]
        for j, (cx, cy) in enumerate(chips):
            for t in range(n):
                landed = _half_rows(ins[t], 2 * cx + cy, half[t], c)
                dst = _half_rows(outs[t], 2 * cx + cy, half[t], c)
                pltpu.make_async_remote_copy(src_ref=landed, dst_ref=landed, send_sem=send_fwd.at[3 * t + j],
                                             recv_sem=recv_ici_ref.at[3 * t + j], device_id=(cx, cy, c),
                                             device_id_type=MESH_ID).wait_recv()
                pltpu.make_async_remote_copy(src_ref=landed, dst_ref=dst, send_sem=send_fwd.at[3 * t + j],
                                             recv_sem=recv_fwd.at[3 * t + j], device_id=(x, y, 1 - c),
                                             device_id_type=MESH_ID).start()
        token[...] = jnp.zeros_like(token)

    dma = pltpu.SemaphoreType.DMA
    return pl.pallas_call(
        body, out_shape=[pltpu.HBM(b.shape, b.dtype) for b in bufs] + [dma((3 * n,)), dma((3 * n,)), _TOKEN],
        in_specs=[_HBM] * n + [_SEM, _ANY], out_specs=[_HBM] * n + [_SEM] * 2 + [_VMEM],
        input_output_aliases={t: t for t in range(n)}, name=name, compiler_params=_ORDERED_EFFECT,
    )(*bufs, recv_ici, after)


def _gather_finish_call(shards, bufs, send_sems, recv_sib, send_fwd, recv_fwd, after, name):
    n = len(bufs)
    half = [b.shape[1] // 2 for b in bufs]

    def body(*refs):
        srcs, ins = refs[:n], refs[n:2 * n]
        send_ref, recv_sib_ref, send_fwd_ref, recv_fwd_ref = refs[2 * n:2 * n + 4]
        x, y, c = lax.axis_index("x"), lax.axis_index("y"), lax.axis_index("c")
        chips = [(1 - x, y), (x, 1 - y), (1 - x, 1 - y)]
        sibling = (x, y, 1 - c)
        for t in range(n):
            for k in range(4):
                pltpu.make_async_remote_copy(src_ref=srcs[t], dst_ref=srcs[t], send_sem=send_ref.at[4 * t + k],
                                             recv_sem=recv_sib_ref.at[t], device_id=sibling, device_id_type=MESH_ID).wait_send()
            from_sibling = _half_rows(ins[t], 2 * x + y, half[t], 1 - c)
            pltpu.make_async_remote_copy(src_ref=from_sibling, dst_ref=from_sibling, send_sem=send_ref.at[4 * t],
                                         recv_sem=recv_sib_ref.at[t], device_id=sibling, device_id_type=MESH_ID).wait_recv()
            for j, (cx, cy) in enumerate(chips):
                sent = _half_rows(ins[t], 2 * cx + cy, half[t], c)
                passed = _half_rows(ins[t], 2 * cx + cy, half[t], 1 - c)
                pltpu.make_async_remote_copy(src_ref=sent, dst_ref=passed, send_sem=send_fwd_ref.at[3 * t + j],
                                             recv_sem=recv_fwd_ref.at[3 * t + j], device_id=sibling, device_id_type=MESH_ID).wait()

    return pl.pallas_call(
        body, out_shape=[pltpu.HBM(b.shape, b.dtype) for b in bufs],
        in_specs=[_HBM] * (2 * n) + [_SEM] * 4 + [_ANY], out_specs=[_HBM] * n,
        input_output_aliases={n + t: t for t in range(n)}, name=name, compiler_params=_ORDERED_EFFECT,
    )(*[_in_hbm(s) for s in shards], *bufs, send_sems, recv_sib, send_fwd, recv_fwd, after)


def _pair_exchange_call(grads, name):
    n = len(grads)
    half = [g.shape[1] // 2 for g in grads]

    def body(*refs):
        srcs, outs, send_sems, recv_sems = refs[:n], refs[n:2 * n], refs[2 * n], refs[2 * n + 1]
        x, y, c = lax.axis_index("x"), lax.axis_index("y"), lax.axis_index("c")
        copies = [pltpu.make_async_remote_copy(
            src_ref=srcs[t].at[:, pl.ds(pl.multiple_of(half[t] * (1 - c), half[t]), half[t])], dst_ref=outs[t],
            send_sem=send_sems.at[t], recv_sem=recv_sems.at[t], device_id=(x, y, 1 - c), device_id_type=MESH_ID) for t in range(n)]
        for cp in copies:
            cp.start()
        for cp in copies:
            cp.wait()

    return pl.pallas_call(
        body, out_shape=[jax.ShapeDtypeStruct((g.shape[0], g.shape[1] // 2, g.shape[2]), g.dtype) for g in grads],
        in_specs=[_ANY] * n, out_specs=[_ANY] * n,
        scratch_shapes=[pltpu.SemaphoreType.DMA((n,)), pltpu.SemaphoreType.DMA((n,))], name=name)(*grads)


def _chip_scatter_call(chip_sums, name):
    n = len(chip_sums)

    def body(*refs):
        srcs, outs, send_sems, recv_sems = refs[:n], refs[n:2 * n], refs[2 * n], refs[2 * n + 1]
        x, y, c = lax.axis_index("x"), lax.axis_index("y"), lax.axis_index("c")
        chips = [(1 - x, y), (x, 1 - y), (1 - x, 1 - y)]
        copies = [pltpu.make_async_remote_copy(
            src_ref=srcs[t].at[2 * cx + cy], dst_ref=outs[t].at[k], send_sem=send_sems.at[3 * t + k],
            recv_sem=recv_sems.at[3 * t + k], device_id=(cx, cy, c), device_id_type=MESH_ID)
            for k, (cx, cy) in enumerate(chips) for t in range(n)]
        for cp in copies:
            cp.start()
        for cp in copies:
            cp.wait()

    return pl.pallas_call(
        body, out_shape=[jax.ShapeDtypeStruct((3,) + p.shape[1:], p.dtype) for p in chip_sums],
        in_specs=[_ANY] * n, out_specs=[_ANY] * n,
        scratch_shapes=[pltpu.SemaphoreType.DMA((3 * n,)), pltpu.SemaphoreType.DMA((3 * n,))], name=name)(*chip_sums)


def _pair_gather_call(bufs, name):
    n = len(bufs)
    half = [b.shape[0] // 2 for b in bufs]

    def body(*refs):
        srcs, outs, send_sems, recv_sems = refs[:n], refs[n:2 * n], refs[2 * n], refs[2 * n + 1]
        x, y, c = lax.axis_index("x"), lax.axis_index("y"), lax.axis_index("c")
        for t in range(n):
            pltpu.make_async_remote_copy(
                src_ref=_half_rows(srcs[t], None, half[t], c), dst_ref=_half_rows(outs[t], None, half[t], c),
                send_sem=send_sems.at[t], recv_sem=recv_sems.at[t], device_id=(x, y, 1 - c), device_id_type=MESH_ID).start()
        for t in range(n):
            pltpu.make_async_remote_copy(
                src_ref=_half_rows(srcs[t], None, half[t], c), dst_ref=_half_rows(outs[t], None, half[t], 1 - c),
                send_sem=send_sems.at[t], recv_sem=recv_sems.at[t], device_id=(x, y, 1 - c), device_id_type=MESH_ID).wait()

    return pl.pallas_call(
        body, out_shape=[jax.ShapeDtypeStruct(b.shape, b.dtype) for b in bufs], in_specs=[_ANY] * n, out_specs=[_ANY] * n,
        input_output_aliases={t: t for t in range(n)},
        scratch_shapes=[pltpu.SemaphoreType.DMA((n,)), pltpu.SemaphoreType.DMA((n,))], name=name)(*bufs)


def _pack_rows(flats, dtype, row_multiple):
    flat = jnp.concatenate([f.reshape(-1).astype(dtype) for f in flats])
    n = flat.shape[0]
    rows = -(-n // PACK_W)
    rows = -(-rows // row_multiple) * row_multiple
    return jnp.pad(flat, (0, rows * PACK_W - n)).reshape(rows, PACK_W)


def _unpack(flat, shapes):
    out, off = [], 0
    for shp in shapes:
        n = math.prod(shp)
        out.append(flat[off:off + n].reshape(shp))
        off += n
    return out


def _f32_as_mm_bits(a):
    return lax.bitcast_convert_type(a, jnp.bfloat16).reshape(-1)


def _mm_bits_as_f32(flat, shape):
    return lax.bitcast_convert_type(flat.reshape(-1, 2), F32).reshape(shape)


_W_IN_SEGMENTS = ((R_ML, R_END, OFF_ML), (R_SG, R_ML, OFF_SG), (R_CV, R_SGI, OFF_CV), (R_SGI, R_MQ, OFF_SGI), (R_MQ, R_SG, OFF_MQ),
                  (R_CQ, R_CKV, OFF_CQ), (R_CKV, R_KR, OFF_CKV), (R_KR, R_CV, OFF_KR + NOPE))
W_IN_SHARD = R_END // N_CHIPS


def _realign_call(wg):
    tr = 128

    def body(w_ref, o_ref):
        pieces, pos = [], 0
        for r0, r1, a0 in _W_IN_SEGMENTS:
            if a0 > pos:
                pieces.append(jnp.zeros((tr, a0 - pos), o_ref.dtype))
            while r0 < r1:
                j = r0 // W_IN_SHARD
                hi = min(r1, (j + 1) * W_IN_SHARD)
                pieces.append(w_ref[j, :, r0 - j * W_IN_SHARD:hi - j * W_IN_SHARD])
                a0, r0 = a0 + hi - r0, hi
            pos = a0
        pieces.append(jnp.zeros((tr, NP - pos), o_ref.dtype))
        o_ref[...] = jnp.concatenate(pieces, axis=1)

    return pl.pallas_call(
        body, grid=(D // tr,), in_specs=[_bs((N_CHIPS, tr, W_IN_SHARD), lambda i: (0, i, 0))],
        out_specs=_bs((tr, NP), lambda i: (i, 0)), out_shape=jax.ShapeDtypeStruct((D, NP), wg.dtype),
        name="w_in_realign", compiler_params=_cparams(1))(wg)


def _unalign_call(dw):
    tr = 128
    by_ref = sorted(_W_IN_SEGMENTS)

    def body(dw_ref, o_ref):
        for j in range(N_CHIPS):
            lo_j, hi_j = j * W_IN_SHARD, (j + 1) * W_IN_SHARD
            pieces = []
            for r0, r1, a0 in by_ref:
                lo, hi = max(r0, lo_j), min(r1, hi_j)
                if lo < hi:
                    pieces.append(dw_ref[:, a0 + lo - r0:a0 + hi - r0])
            o_ref[j] = jnp.concatenate(pieces, axis=1)

    return pl.pallas_call(
        body, grid=(D // tr,), in_specs=[_bs((tr, NP), lambda i: (i, 0))],
        out_specs=_bs((N_CHIPS, tr, W_IN_SHARD), lambda i: (0, i, 0)),
        out_shape=jax.ShapeDtypeStruct((N_CHIPS, D, W_IN_SHARD), dw.dtype), name="w_in_unalign", compiler_params=_cparams(1))(dw)


def _w_in_to_aligned(w):
    z = lambda n: jnp.zeros((w.shape[0], n), w.dtype)
    return jnp.concatenate([w[:, R_ML:R_END], w[:, R_SG:R_ML], w[:, R_CV:R_SGI], w[:, R_SGI:R_MQ], w[:, R_MQ:R_SG],
                            w[:, R_CQ:R_CKV], w[:, R_CKV:R_KR], z(NOPE), w[:, R_KR:R_CV], z(LANES - QKH)], axis=1)


def _w_in_from_aligned(wa):
    return jnp.concatenate([wa[:, OFF_CQ:OFF_CKV], wa[:, OFF_CKV:OFF_KR], wa[:, OFF_KR + NOPE:OFF_KR + QKH], wa[:, OFF_CV:OFF_SGI],
                            wa[:, OFF_SGI:OFF_MQ], wa[:, OFF_MQ:OFF_CQ], wa[:, OFF_SG:OFF_CV], wa[:, OFF_ML:OFF_SG]], axis=1)


def _wuq_to_heads(w):
    w3 = w.reshape(QL, H, QKH)
    w3 = jnp.pad(w3, ((0, 0), (0, 0), (0, LANES - QKH)))
    return jnp.transpose(w3, (1, 0, 2))


def _wuq_from_heads(wh):
    return jnp.transpose(wh[:, :, :QKH], (1, 0, 2)).reshape(QL, H * QKH)


def _wukv_to_heads(w):
    w3 = w.reshape(KVL, H, NOPE + VH)
    wkn = jnp.transpose(jnp.pad(w3[:, :, :NOPE], ((0, 0), (0, 0), (0, LANES - NOPE))), (1, 0, 2))
    wv3 = w3[:, :, NOPE:]
    z = jnp.zeros((KVL, VH), w.dtype)
    cols = []
    for h in range(H):
        cols += [wv3[:, h], z] if h % 2 == 0 else [z, wv3[:, h]]
    return wkn, jnp.concatenate(cols, axis=1)


def _wukv_from_heads(wkn, wv):
    kn = jnp.transpose(wkn[:, :, :NOPE], (1, 0, 2))
    vs = jnp.stack([wv[:, LANES * h + VH * (h % 2):LANES * h + VH * (h % 2) + VH] for h in range(H)], axis=1)
    return jnp.concatenate([kn, vs], axis=2).reshape(KVL, H * (NOPE + VH))


def _layer_fwd(x, mem, tabs, p):
    proj, h = _proj_call(x, p["norm_g"], p["w_in"])
    if p.get("late") is not None:
        p = dict(p, **p["late"](proj))
    q, k, v = _mla_prep_call(proj, tabs, p["cq_g"], p["ckv_g"], p["qg"], p["kg"], p["wuq"], p["wkn"], p["wv"])
    ya, attn_o, attn_lse = _attn_call(q, k, v, proj)
    bm = p["bm"]
    if p.get("after_attn") is not None:
        bm = _tie(bm, p["after_attn"](ya))
    yb = _conv_call(proj, p["conv_w"], p["conv_b"])
    yc = _sg_call(proj, p["ln_g"], p["ln_b"], p["ws"], p["bs"])
    mk, mv = _memkv_call(mem, p["mem_g"], p["wm"], p["mkg"])
    yd = _mem_call(proj, mk, mv, p["mqg"])
    out = _merge_call((ya, yb, yc, yd), proj, bm, p["wb"], p["wo"], x)
    return out, dict(p=p, x=x, proj=proj, h=h, q=q, k=k, v=v, attn_o=attn_o, attn_lse=attn_lse, ys=(ya, yb, yc, yd), mk=mk, mv=mv)


def _layer_bwd(dout, mem, tabs, p, sv, start_after=None, on_rest_grads=None, on_grads=None):
    proj = sv["proj"]
    bm = p["bm"] if start_after is None else _tie(p["bm"], start_after)
    dya, dyb, dyc, dyd, dml, dbm, dwb, dwo = _merge_bwd_call(sv["ys"], proj, bm, p["wb"], p["wo"], dout)
    dq, dk, dv, dsg_a = _attn_bwd_call(sv["q"], sv["k"], sv["v"], proj, dya, sv["attn_o"], sv["attn_lse"])
    dcq, dckv, dkr, dcqg, dckvg, dqg, dkg, dwuq, dwkn, dwv = _mla_prep_bwd_call(
        proj, tabs, p["cq_g"], p["ckv_g"], p["qg"], p["kg"], p["wuq"], p["wkn"], p["wv"], dq, dk, dv)
    dbg, dcg, dxi, dsg_b, dcw, dcb = _conv_bwd_call(proj, p["conv_w"], p["conv_b"], dyb)
    du, dvv, dsg_c, dlg, dlb, dws, dbs = _sg_bwd_call(proj, p["ln_g"], p["ln_b"], p["ws"], p["bs"], dyc)
    dmq, dsg_d, dmk, dmv, dmqg = _mem_bwd_call(proj, sv["mk"], sv["mv"], p["mqg"], dyd)
    dmem_g, dwm, dmkg = _memkv_bwd_call(mem, p["mem_g"], p["wm"], p["mkg"], dmk, dmv)
    grads = dict(cq_norm_g=dcqg[0], ckv_norm_g=dckvg[0], mla_q_norm_g=dqg[0, :QKH], mla_k_norm_g=dkg[0, :QKH],
                 conv_w=dcw, conv_b=dcb[0], sg_ln_g=dlg[0], sg_ln_b=dlb[0], w_spatial=dws, b_spatial=dbs[:, :, 0],
                 mem_norm_g=dmem_g[0], mem_q_norm_g=dmqg[0], mem_k_norm_g=dmkg[0], b_merge=dbm,
                 wuq_heads=dwuq, wkn_heads=dwkn, wv_heads=dwv, w_mem_kv=dwm, w_branch_chips=dwb, w_out=dwo)
    if on_rest_grads is not None:
        on_rest_grads(grads)
    dproj = jnp.concatenate([dml, dsg_a, dsg_b, dsg_c, dsg_d, dbg, dcg, dxi, du, dvv, dmq, dcq, dckv, dkr], axis=1)
    grads["w_in_aligned"] = _dw_call(sv["h"], dproj)
    norm_g = p["norm_g"]
    if on_grads is not None:
        for token in on_grads(grads):
            norm_g = _tie(norm_g, token)
    dx, dnorm_g = _dh_call(dproj, p["w_in"], sv["x"], norm_g, dout)
    grads["norm_g"] = dnorm_g[0]
    return dx, grads


def _chips_to_cols(a):
    return jnp.concatenate([a[j] for j in range(N_CHIPS)], axis=1)


def _cols_to_chips(a):
    cols = a.shape[1] // N_CHIPS
    return jnp.stack([a[:, cols * j:cols * (j + 1)] for j in range(N_CHIPS)])


def _layer_params_first(l, rep, w_in_gathered, conv_w, b_merge):
    pad_g = lambda g: jnp.pad(g, (0, LANES - QKH)).reshape(1, LANES)
    return dict(
        norm_g=rep["norm_g"][l].reshape(1, D), w_in=_realign_call(w_in_gathered),
        cq_g=rep["cq_norm_g"][l].reshape(1, QL), ckv_g=rep["ckv_norm_g"][l].reshape(1, KVL),
        qg=pad_g(rep["mla_q_norm_g"][l]), kg=pad_g(rep["mla_k_norm_g"][l]),
        conv_w=conv_w, conv_b=rep["conv_b"][l].reshape(1, CW),
        ln_g=rep["sg_ln_g"][l].reshape(1, SGW), ln_b=rep["sg_ln_b"][l].reshape(1, SGW),
        ws=rep["w_spatial"][l], bs=rep["b_spatial"][l].reshape(SGG, SGC, 1),
        mem_g=rep["mem_norm_g"][l].reshape(1, D),
        mqg=rep["mem_q_norm_g"][l].reshape(1, MHD), mkg=rep["mem_k_norm_g"][l].reshape(1, MHD), bm=b_merge)


def _layer_params_rest(gathered):
    wkn, wv = _wukv_to_heads(_chips_to_cols(gathered["w_ukv"]))
    return dict(wuq=_wuq_to_heads(_chips_to_cols(gathered["w_uq"])), wkn=wkn, wv=wv,
                wm=gathered["w_mem_kv"].reshape(D, 2 * MH * MHD), wb=gathered["w_branch"], wo=gathered["w_out"].reshape(D, D))


def _layer_params(l, rep, gathered, conv_w, b_merge):
    return dict(_layer_params_first(l, rep, gathered["w_in"], conv_w, b_merge), **_layer_params_rest(gathered))


def _forward_backward(x, mem, pos, target, params, bwd_hooks=None):
    tabs = _rope_tables(pos)
    params = list(params)
    saved = []
    act = x
    for l in range(DEPTH):
        if callable(params[l]):
            params[l] = params[l](saved[-1], act)
        act, sv = _layer_fwd(act, mem, tabs, params[l])
        saved.append(sv)
    dy, sq = _loss_call(act, target)
    grads = [None] * DEPTH
    token = None
    for l in reversed(range(DEPTH)):
        hooks = dict(bwd_hooks[l]) if bwd_hooks else {}
        after_layer = hooks.pop("after_layer", None)
        dy, grads[l] = _layer_bwd(dy, mem, tabs, saved[l]["p"], saved[l], start_after=token, **hooks)
        token = after_layer(dy) if after_layer is not None else None
    return sq, dy, grads


_SHARDED_MM = ("w_in", "w_branch", "w_out", "w_mem_kv", "w_uq", "w_ukv")
_SHARDED_F32 = ("conv_w", "b_merge")
_REPLICATED = ("norm_g", "cq_norm_g", "ckv_norm_g", "mla_q_norm_g", "mla_k_norm_g", "conv_b", "sg_ln_g", "sg_ln_b",
               "w_spatial", "b_spatial", "mem_norm_g", "mem_q_norm_g", "mem_k_norm_g")
_ALL_REDUCED = _REPLICATED + _SHARDED_F32
_WEIGHTS = ("norm_g", "w_in", "cq_norm_g", "ckv_norm_g", "w_uq", "w_ukv", "mla_q_norm_g", "mla_k_norm_g", "conv_w", "conv_b",
            "sg_ln_g", "sg_ln_b", "w_spatial", "b_spatial", "mem_norm_g", "w_mem_kv", "mem_q_norm_g", "mem_k_norm_g",
            "b_merge", "w_branch", "w_out")
_BIG = ("w_in", "w_uq", "w_ukv", "w_mem_kv", "w_branch", "w_out")
_SMALL = tuple(n for n in _WEIGHTS if n not in _BIG)


def _gather_small_sharded(w):
    names = _SHARDED_F32
    packed = _pack_rows([w[n] for n in names], F32, 8)
    got = _all_gather8(packed, "gather_small_weights")
    per_chip = [_unpack(got[2 * j].reshape(-1), [w[n].shape for n in names]) for j in range(N_CHIPS)]
    return {n: jnp.concatenate([per_chip[j][t] for j in range(N_CHIPS)], axis=2) for t, n in enumerate(names)}


def _gather_layer(l, shards):
    srcs = [shards[n] for n in _SHARDED_MM]
    return dict(zip(_SHARDED_MM, _gather_layer_call(l, srcs, "gather_weights_l%d" % l)))


class _ReduceScatter:
    SLABS = dict(
        w_in=lambda g: _unalign_call(g["w_in_aligned"]),
        w_branch=lambda g: g["w_branch_chips"].reshape(N_CHIPS, NB * BW, D // N_CHIPS),
        w_out=lambda g: g["w_out"].reshape(N_CHIPS, D // N_CHIPS, D),
        w_mem_kv=lambda g: g["w_mem_kv"].reshape(N_CHIPS, D // N_CHIPS, 2 * MH * MHD),
        w_uq=lambda g: _cols_to_chips(_wuq_from_heads(g["wuq_heads"])),
        w_ukv=lambda g: _cols_to_chips(_wukv_from_heads(g["wkn_heads"], g["wv_heads"])))

    def __init__(self, tag, names):
        self.tag, self.names = tag, names

    def exchange(self, grads):
        self.tensors = [self.SLABS[n](grads) for n in self.names]
        n = len(self.tensors)
        out = _pair_exchange_start_call(self.tensors, self.tag + "exchange_start")
        self.ex_bufs, self.ex_send, self.ex_recv = out[:n], out[n], out[n + 1]
        return out[n + 2]

    def scatter(self, after):
        n = len(self.tensors)
        c = lax.axis_index("c")
        from_sibling = _pair_exchange_finish_call(self.tensors, self.ex_bufs, self.ex_send, self.ex_recv, after,
                                                  self.tag + "exchange_finish")
        self.chip_sums = _pair_sum_call(self.tensors, from_sibling, c.astype(jnp.int32).reshape(1), self.tag + "pair_sum")
        out = _chip_scatter_start_call(self.chip_sums, self.tag + "scatter_start")
        self.bufs, self.send_sems, self.recv_sems, self.token = out[:n], out[n], out[n + 1], out[n + 2]
        return self.token

    def finish(self, after):
        x, y, c = lax.axis_index("x"), lax.axis_index("y"), lax.axis_index("c")
        chip_core = jnp.stack([2 * x + y, c]).astype(jnp.int32)
        from_chips = _chip_scatter_finish_call(self.chip_sums, self.bufs, self.send_sems, self.recv_sems, after,
                                               self.tag + "scatter_finish")
        mine = _owner_sum_call(self.chip_sums, from_chips, chip_core, self.tag + "owner_sum")
        return dict(zip(self.names, _pair_gather_call(mine, self.tag + "pair_gather")))


def _all_reduce_small(g, sq):
    packed = _pack_rows([g[n] for n in _ALL_REDUCED] + [sq], F32, 64)
    got = _all_gather8(packed, "gather_small_grads")
    total = _sum8_call(got).reshape(-1)
    parts = _unpack(total, [g[n].shape for n in _ALL_REDUCED] + [sq.shape])
    out = dict(zip(_ALL_REDUCED, parts))
    sq_total = parts[-1]
    chip = 2 * lax.axis_index("x") + lax.axis_index("y")
    for n in _SHARDED_F32:
        size = out[n].shape[2] // N_CHIPS
        out[n] = lax.dynamic_slice_in_dim(out[n], chip * size, size, axis=2)
    return out, sq_total


def _adamw_small(w, g, m, v, token):
    delta, new_m, new_v = {}, {}, {}
    shapes = [w[n].shape for n in _SMALL]
    pk = lambda t: _pack_rows([t[n] for n in _SMALL], F32, 64)
    d, nm, nv = _adamw_call(pk(w), _tie(pk(g), token), pk(m), pk(v), "adamw_small")
    for out, packed in ((delta, d), (new_m, nm), (new_v, nv)):
        out.update(zip(_SMALL, _unpack(packed.reshape(-1), shapes)))
    return delta, new_m, new_v


def kernel(x, mem, positions, norm_g, w_in, cq_norm_g, ckv_norm_g, w_uq, w_ukv, mla_q_norm_g, mla_k_norm_g, conv_w, conv_b, sg_ln_g, sg_ln_b, w_spatial, b_spatial, mem_norm_g, w_mem_kv, mem_q_norm_g, mem_k_norm_g, b_merge, w_branch, w_out, loss_target, m_norm_g, m_w_in, m_cq_norm_g, m_ckv_norm_g, m_w_uq, m_w_ukv, m_mla_q_norm_g, m_mla_k_norm_g, m_conv_w, m_conv_b, m_sg_ln_g, m_sg_ln_b, m_w_spatial, m_b_spatial, m_mem_norm_g, m_w_mem_kv, m_mem_q_norm_g, m_mem_k_norm_g, m_b_merge, m_w_branch, m_w_out, v_norm_g, v_w_in, v_cq_norm_g, v_ckv_norm_g, v_w_uq, v_w_ukv, v_mla_q_norm_g, v_mla_k_norm_g, v_conv_w, v_conv_b, v_sg_ln_g, v_sg_ln_b, v_w_spatial, v_b_spatial, v_mem_norm_g, v_w_mem_kv, v_mem_q_norm_g, v_mem_k_norm_g, v_b_merge, v_w_branch, v_w_out):
    w = dict(norm_g=norm_g, w_in=w_in, cq_norm_g=cq_norm_g, ckv_norm_g=ckv_norm_g, w_uq=w_uq, w_ukv=w_ukv,
             mla_q_norm_g=mla_q_norm_g, mla_k_norm_g=mla_k_norm_g, conv_w=conv_w, conv_b=conv_b, sg_ln_g=sg_ln_g,
             sg_ln_b=sg_ln_b, w_spatial=w_spatial, b_spatial=b_spatial, mem_norm_g=mem_norm_g, w_mem_kv=w_mem_kv,
             mem_q_norm_g=mem_q_norm_g, mem_k_norm_g=mem_k_norm_g, b_merge=b_merge, w_branch=w_branch, w_out=w_out)
    m = dict(norm_g=m_norm_g, w_in=m_w_in, cq_norm_g=m_cq_norm_g, ckv_norm_g=m_ckv_norm_g, w_uq=m_w_uq, w_ukv=m_w_ukv,
             mla_q_norm_g=m_mla_q_norm_g, mla_k_norm_g=m_mla_k_norm_g, conv_w=m_conv_w, conv_b=m_conv_b, sg_ln_g=m_sg_ln_g,
             sg_ln_b=m_sg_ln_b, w_spatial=m_w_spatial, b_spatial=m_b_spatial, mem_norm_g=m_mem_norm_g, w_mem_kv=m_w_mem_kv,
             mem_q_norm_g=m_mem_q_norm_g, mem_k_norm_g=m_mem_k_norm_g, b_merge=m_b_merge, w_branch=m_w_branch, w_out=m_w_out)
    v = dict(norm_g=v_norm_g, w_in=v_w_in, cq_norm_g=v_cq_norm_g, ckv_norm_g=v_ckv_norm_g, w_uq=v_w_uq, w_ukv=v_w_ukv,
             mla_q_norm_g=v_mla_q_norm_g, mla_k_norm_g=v_mla_k_norm_g, conv_w=v_conv_w, conv_b=v_conv_b, sg_ln_g=v_sg_ln_g,
             sg_ln_b=v_sg_ln_b, w_spatial=v_w_spatial, b_spatial=v_b_spatial, mem_norm_g=v_mem_norm_g, w_mem_kv=v_w_mem_kv,
             mem_q_norm_g=v_mem_q_norm_g, mem_k_norm_g=v_mem_k_norm_g, b_merge=v_b_merge, w_branch=v_w_branch, w_out=v_w_out)

    chip_core = jnp.stack([2 * lax.axis_index("x") + lax.axis_index("y"), lax.axis_index("c")]).astype(jnp.int32)

    class Gather:
        def __init__(self, layer, names, after, tag):
            self.names, self.tag = names, tag
            core = lax.axis_index("c")
            halves = [w[n].shape[1] // 2 for n in names]
            self.srcs = [lax.dynamic_slice_in_dim(w[n][layer], core * h, h, axis=0).astype(MM) for n, h in zip(names, halves)]
            k = len(names)
            out = _gather_start_call(self.srcs, _place_own_call(self.srcs, chip_core, tag + "place_own"), after, tag + "start")
            self.bufs, self.send, self.recv_sib, self.recv_ici, self.token = out[:k], out[k], out[k + 1], out[k + 2], out[k + 3]

        def pass_on(self, after):
            k = len(self.names)
            out = _gather_forward_call(self.bufs, self.recv_ici, after, self.tag + "forward")
            self.bufs, self.send_fwd, self.recv_fwd = out[:k], out[k], out[k + 1]
            return out[k + 2]

        def finish(self, after):
            got = _gather_finish_call(self.srcs, self.bufs, self.send, self.recv_sib, self.send_fwd, self.recv_fwd, after,
                                      self.tag + "finish")
            return dict(zip(self.names, got))

    first = Gather(0, ("w_in",), chip_core, "gather_l0_w_in_")
    rest = Gather(0, _SHARDED_MM[1:], first.token, "gather_l0_rest_")
    later = Gather(1, _SHARDED_MM, rest.token, "gather_l1_")
    small = _gather_small_sharded(w)
    w_in0 = first.finish(first.pass_on(later.token))["w_in"]

    def rest_of_layer0(proj0):
        return _layer_params_rest(rest.finish(rest.pass_on(proj0)))

    def layer1_params(saved0, act0):
        return _layer_params(1, w, later.finish(act0), small["conv_w"][1], small["b_merge"][1])

    params0 = _layer_params_first(0, w, w_in0, small["conv_w"][0], small["b_merge"][0])
    params = [dict(params0, late=rest_of_layer0, after_attn=later.pass_on), layer1_params]
    others = _SHARDED_MM[1:]
    rs1 = _ReduceScatter("rs_l1_", _SHARDED_MM)
    rs0_rest, rs0_w_in = _ReduceScatter("rs_l0_rest_", others), _ReduceScatter("rs_l0_w_in_", ("w_in",))

    def layer0_grads_done(grads):
        return [rs0_rest.scatter([grads["w_in_aligned"]]), rs0_w_in.exchange(grads)]

    hooks = [dict(on_rest_grads=rs0_rest.exchange, on_grads=layer0_grads_done),
             dict(on_grads=lambda grads: [rs1.exchange(grads)], after_layer=lambda dy: rs1.scatter([dy]))]
    sq, grad_x, layer_grads = _forward_backward(x[0], mem[0], positions[0], loss_target[0], params, hooks)

    g, sq_total = _all_reduce_small({n: jnp.stack([layer_grads[l][n] for l in range(DEPTH)]) for n in _ALL_REDUCED}, sq)
    loss = 0.5 / D * jnp.sum(sq_total)
    scattering = rs0_w_in.scatter([grad_x, g["norm_g"]])
    delta, new_m, new_v = _adamw_small(w, g, m, v, scattering)
    shard_grads = {1: rs1.finish([scattering]), 0: rs0_rest.finish([scattering])}
    as3d = lambda a: a.reshape(DEPTH, -1, a.shape[-1])
    as2d = lambda a: a.reshape(-1, a.shape[-1])
    big = lambda t: [as3d(t[n]) for n in others]
    turned = lambda t: [jnp.swapaxes(t["w_in"], 1, 2)]
    assert W_IN_SHARD % (8 * 7) == 0

    def update_w_in(l, grad, prev):
        return _adamw_layer_call(l, turned(w), [grad.T], turned(m), turned(v), prev, [], "adamw_w_in_l%d" % l, steps=7)

    def update_others(l, prev):
        return _adamw_layer_call(l, big(w), [as2d(shard_grads[l][n]) for n in others], big(m), big(v), prev, [], "adamw_l%d" % l)

    upd_in1 = update_w_in(1, shard_grads[1]["w_in"], None)
    upd = update_others(0, update_others(1, None))
    w_in_grad0 = rs0_w_in.finish([grad_x, upd_in1[0], upd[0], delta["norm_g"]])["w_in"]
    upd_in = update_w_in(0, w_in_grad0, upd_in1)
    g["w_in"], delta["w_in"], new_m["w_in"], new_v["w_in"] = [jnp.swapaxes(a, 1, 2) for a in upd_in]
    for t, n in enumerate(others):
        g[n], delta[n], new_m[n], new_v[n] = [a.reshape(w[n].shape) for a in upd[4 * t:4 * t + 4]]
    return (loss, grad_x[None], *[g[n] for n in _WEIGHTS], *[delta[n] for n in _WEIGHTS],
            *[new_m[n] for n in _WEIGHTS], *[new_v[n] for n in _WEIGHTS])
```

```python
import functools
import math

import jax
import jax.numpy as jnp
from jax import lax
from jax.experimental import pallas as pl
from jax.experimental.pallas import tpu as pltpu

F32 = jnp.float32
MM = jnp.bfloat16

D = 1024
DEPTH = 2
EPS = 1e-6
H = 8
NOPE = 64
ROPE = 32
QKH = 96
VH = 64
QL = 256
KVL = 128
ROPE_THETA = 10000.0
CW = 512
SGW = 512
SGG = 4
SGC = 128
MH = 4
MHD = 128
NB = 4
BW = 512
NEG_INF = -1e30
LANES = 128
N_CHIPS = 4

R_CQ, R_CKV, R_KR, R_CV, R_SGI, R_MQ, R_SG, R_ML, R_END = 0, 256, 384, 416, 1952, 2976, 3488, 5536, 9632
OFF_ML, OFF_SG, OFF_CV, OFF_SGI, OFF_MQ, OFF_CQ, OFF_CKV, OFF_KR, NP = 0, 4096, 6144, 7680, 8704, 9216, 9472, 9600, 9728

ADAM_LR = 0.001
ADAM_B1 = 0.9
ADAM_B2 = 0.999
ADAM_EPS = 1e-08
ADAM_WD = 0.01
ADAM_STEP = 10

VMEM_LIMIT = 56 * 1024 * 1024
PACK_W = 512
MESH_ID = pl.DeviceIdType.MESH


def _cparams(n_axes):
    return pltpu.CompilerParams(dimension_semantics=("arbitrary",) * n_axes, vmem_limit_bytes=VMEM_LIMIT)


def _bs(shape, imap):
    return pl.BlockSpec(shape, imap)


@jax.custom_vjp
def _mm_plain(a, b):
    return jnp.dot(a.astype(MM), b.astype(MM), preferred_element_type=F32)


def _mm_plain_fwd(a, b):
    return _mm_plain(a, b), (a, b)


def _mm_plain_bwd(res, g):
    a, b = res
    gm = g.astype(MM)
    da = lax.dot_general(gm, b.astype(MM), (((1,), (1,)), ((), ())), preferred_element_type=F32)
    db = lax.dot_general(a.astype(MM), gm, (((0,), (0,)), ((), ())), preferred_element_type=F32)
    return da.astype(a.dtype), db.astype(b.dtype)


_mm_plain.defvjp(_mm_plain_fwd, _mm_plain_bwd)


@jax.custom_vjp
def _mm_slot(a, w, slot):
    return jnp.dot(a.astype(MM), w.astype(MM), preferred_element_type=F32)


def _mm_slot_fwd(a, w, slot):
    return _mm_slot(a, w, slot), (a, w)


def _mm_slot_bwd(res, g):
    a, w = res
    gm = g.astype(MM)
    da = lax.dot_general(gm, w.astype(MM), (((1,), (1,)), ((), ())), preferred_element_type=F32)
    dw = lax.dot_general(a.astype(MM), gm, (((0,), (0,)), ((), ())), preferred_element_type=F32)
    return da.astype(a.dtype), jnp.zeros_like(w), dw


_mm_slot.defvjp(_mm_slot_fwd, _mm_slot_bwd)


def _mm(a, b):
    if isinstance(b, tuple):
        return _mm_slot(a, b[0], b[1])
    return _mm_plain(a, b)


def _with_slot(w):
    return (w, jnp.zeros(w.shape, F32))


@jax.custom_vjp
def _mm_nt(a, b):
    return lax.dot_general(a.astype(MM), b.astype(MM), (((1,), (1,)), ((), ())), preferred_element_type=F32)


def _mm_nt_fwd(a, b):
    return _mm_nt(a, b), (a, b)


def _mm_nt_bwd(res, g):
    a, b = res
    gm = g.astype(MM)
    da = jnp.dot(gm, b.astype(MM), preferred_element_type=F32)
    db = lax.dot_general(gm, a.astype(MM), (((0,), (0,)), ((), ())), preferred_element_type=F32)
    return da.astype(a.dtype), db.astype(b.dtype)


_mm_nt.defvjp(_mm_nt_fwd, _mm_nt_bwd)


@functools.partial(jax.custom_vjp, nondiff_argnums=(1,))
def _lane_roll(x, shift):
    return pltpu.roll(x, shift, 1)


def _lane_roll_fwd(x, shift):
    return pltpu.roll(x, shift, 1), None


def _lane_roll_bwd(shift, _, g):
    return (pltpu.roll(g, (LANES - shift) % LANES, 1),)


_lane_roll.defvjp(_lane_roll_fwd, _lane_roll_bwd)


def _rms_n(x, g, n):
    ms = jnp.sum(x * x, axis=-1, keepdims=True) * (1.0 / n)
    return x * lax.rsqrt(ms + EPS) * g


def _softmax(s):
    m = jnp.max(s, axis=-1, keepdims=True)
    e = jnp.exp(s - m)
    return e / jnp.sum(e, axis=-1, keepdims=True)


def _rope(t, cos_t, sin_a, sin_b):
    return t * cos_t + _lane_roll(t, LANES - 16) * sin_a + _lane_roll(t, 16) * sin_b


def _mla_prep_fn(cq, ckv, kr, cos_t, sin_a, sin_b, cq_g, ckv_g, qg, kg, wuq, wkn, wv):
    cqn = _rms_n(cq, cq_g, QL)
    ckvn = _rms_n(ckv, ckv_g, KVL)
    lane = lax.broadcasted_iota(jnp.int32, kr.shape, 1)
    krm = jnp.where((lane >= NOPE) & (lane < QKH), kr, 0.0)
    qs, ks = [], []
    for h in range(H):
        qh = _rms_n(_mm(cqn, wuq[h]), qg, QKH)
        qs.append(_rope(qh, cos_t, sin_a, sin_b))
        kh = _rms_n(_mm(ckvn, wkn[h]) + krm, kg, QKH)
        ks.append(_rope(kh, cos_t, sin_a, sin_b))
    return jnp.concatenate(qs, axis=-1), jnp.concatenate(ks, axis=-1), _mm(ckvn, wv)


def _dot_nt(a, b):
    return lax.dot_general(a.astype(MM), b.astype(MM), (((1,), (1,)), ((), ())), preferred_element_type=F32)


def _dot_tn(a, b):
    return lax.dot_general(a.astype(MM), b.astype(MM), (((0,), (0,)), ((), ())), preferred_element_type=F32)


def _causal_scores(qe, ke):
    tq, kl = qe.shape[0], ke.shape[0]
    s = _dot_nt(qe, ke) * (QKH ** -0.5)
    rows = lax.broadcasted_iota(jnp.int32, (tq, tq), 0)
    cols = lax.broadcasted_iota(jnp.int32, (tq, tq), 1)
    own = jnp.where(cols <= rows, s[:, kl - tq:], NEG_INF)
    return own if kl == tq else jnp.concatenate([s[:, :kl - tq], own], axis=1)


def _head_lanes(e, shape):
    lane = lax.broadcasted_iota(jnp.int32, shape, len(shape) - 1)
    return (lane >= VH * e) & (lane < VH * (e + 1))


def _attn_pair_fwd(q2, k2, v2):
    tq = q2.shape[0]
    o = jnp.zeros((tq, LANES), F32)
    lse = jnp.zeros((tq, LANES), F32)
    for e in range(2):
        sl = slice(LANES * e, LANES * (e + 1))
        s = _causal_scores(q2[:, sl], k2[:, sl])
        m = jnp.max(s, axis=-1, keepdims=True)
        ex = jnp.exp(s - m)
        l = jnp.sum(ex, axis=-1, keepdims=True)
        ve = jnp.where(_head_lanes(e, v2[:, sl].shape), v2[:, sl], 0.0)
        o = o + jnp.dot((ex * (1.0 / l)).astype(MM), ve.astype(MM), preferred_element_type=F32)
        lse = jnp.where(_head_lanes(e, lse.shape), m + jnp.log(l), lse)
    return o, lse


def _attn_pair_bwd(q2, k2, v2, sg, dys, o, lse):
    sig = jax.nn.sigmoid(sg)
    do = dys * (sg * sig)
    dsg = dys * o * (sig * (1.0 + sg * (1.0 - sig)))
    dqs, dks, dvs = [], [], []
    for e in range(2):
        sl = slice(LANES * e, LANES * (e + 1))
        qe, ke = q2[:, sl], k2[:, sl]
        hm = _head_lanes(e, o.shape)
        lse_e = jnp.max(jnp.where(hm, lse, NEG_INF), axis=-1, keepdims=True)
        do_e = jnp.where(hm, do, 0.0)
        delta = jnp.sum(do_e * o, axis=-1, keepdims=True)
        p = jnp.exp(_causal_scores(qe, ke) - lse_e)
        ve = jnp.where(_head_lanes(e, v2[:, sl].shape), v2[:, sl], 0.0)
        dvs.append(_dot_tn(p, do_e))
        ds = (p * (_dot_nt(do_e, ve) - delta)) * (QKH ** -0.5)
        dqs.append(jnp.dot(ds.astype(MM), ke.astype(MM), preferred_element_type=F32))
        dks.append(_dot_tn(ds, qe))
    return jnp.concatenate(dqs, axis=-1), jnp.concatenate(dks, axis=-1), jnp.concatenate(dvs, axis=-1), dsg


def _sg_fn(u, v, sgc, ln_g, ln_b, ws, bs):
    mu = jnp.mean(v, axis=-1, keepdims=True)
    xc = v - mu
    vn = xc * lax.rsqrt(jnp.mean(xc * xc, axis=-1, keepdims=True) + EPS) * ln_g + ln_b
    r = lax.broadcasted_iota(jnp.int32, (SGC, SGC), 0)
    c = lax.broadcasted_iota(jnp.int32, (SGC, SGC), 1)
    wt = [jnp.where(r >= c, w, 0.0) for w in ws]
    row_blocks = []
    for ch in range(u.shape[0] // SGC):
        col_blocks = []
        for g in range(SGG):
            blk = vn[SGC * ch:SGC * (ch + 1), LANES * g:LANES * (g + 1)]
            col_blocks.append(_mm(wt[g], blk) + bs[g])
        row_blocks.append(jnp.concatenate(col_blocks, axis=-1))
    mixed = jnp.concatenate(row_blocks, axis=0)
    return (u * mixed) * jax.nn.silu(sgc)


def _memkv_fn(mem, mem_g, wm, kg):
    kv = _mm(_rms_n(mem, mem_g, D), wm)
    ks = [_rms_n(kv[:, MHD * h:MHD * (h + 1)], kg, MHD) for h in range(MH)]
    return jnp.concatenate(ks, axis=-1), kv[:, MH * MHD:]


def _mem_fn(mq, sgd, k, v, qg):
    outs = []
    for h in range(MH):
        sl = slice(MHD * h, MHD * (h + 1))
        qh = _rms_n(mq[:, sl], qg, MHD)
        p = _softmax(_mm_nt(qh, k[:, sl]) * (MHD ** -0.5))
        outs.append(_mm(p, v[:, sl]))
    return jnp.concatenate(outs, axis=-1) * jax.nn.silu(sgd)


def _merge_fn(ys, logits, bm, wb, wo):
    merged = None
    for n in range(NB):
        z = jnp.concatenate([_mm(ys[n], wb[j][n]) for j in range(N_CHIPS)], axis=-1)
        gate = jax.nn.sigmoid(logits[:, D * n:D * (n + 1)] + bm[n])
        merged = gate * z if merged is None else merged + gate * z
    return _mm(merged, wo)


def _proj_call(x, g, w):
    s_len = x.shape[0]
    tm, tn = min(s_len, 1024), NP // 4

    def body(x_ref, g_ref, w_ref, p_ref, h_ref):
        @pl.when(pl.program_id(1) == 0)
        def _():
            h_ref[...] = _rms_n(x_ref[...], g_ref[...], D).astype(h_ref.dtype)
        p_ref[...] = jnp.dot(h_ref[...], w_ref[...], preferred_element_type=F32)

    return pl.pallas_call(
        body, grid=(s_len // tm, NP // tn),
        in_specs=[_bs((tm, D), lambda i, j: (i, 0)), _bs((1, D), lambda i, j: (0, 0)), _bs((D, tn), lambda i, j: (0, j))],
        out_specs=[_bs((tm, tn), lambda i, j: (i, j)), _bs((tm, D), lambda i, j: (i, 0))],
        out_shape=[jax.ShapeDtypeStruct((s_len, NP), F32), jax.ShapeDtypeStruct((s_len, D), MM)],
        name="proj", compiler_params=_cparams(2))(x, g, w)


def _rope_tables(pos):
    half = ROPE // 2
    inv_freq = ROPE_THETA ** (-jnp.arange(half, dtype=F32) / half)
    ang = pos.astype(F32)[:, None] * inv_freq
    cos, sin = jnp.cos(ang), jnp.sin(ang)
    s_len = pos.shape[0]
    z = lambda n: jnp.zeros((s_len, n), F32)
    cos_t = jnp.concatenate([jnp.ones((s_len, NOPE), F32), cos, cos, z(LANES - QKH)], axis=1)
    sin_a = jnp.concatenate([z(NOPE), -sin, z(LANES - NOPE - half)], axis=1)
    sin_b = jnp.concatenate([z(NOPE + half), sin, z(LANES - QKH)], axis=1)
    return cos_t, sin_a, sin_b


def _mla_prep_specs(tm):
    row = lambda w, off: _bs((tm, w), lambda i: (i, off // w))
    full2 = lambda a, b: _bs((a, b), lambda i: (0, 0))
    full3 = lambda a, b, c: _bs((a, b, c), lambda i: (0, 0, 0))
    tab = _bs((tm, LANES), lambda i: (i, 0))
    return [row(QL, OFF_CQ), row(KVL, OFF_CKV), row(LANES, OFF_KR), tab, tab, tab,
            full2(1, QL), full2(1, KVL), full2(1, LANES), full2(1, LANES),
            full3(H, QL, LANES), full3(H, KVL, LANES), full2(KVL, H * LANES)]


def _mla_prep_args(body_refs, wrap=lambda w: w):
    (cq, ckv, kr, ct, sa, sb, cqg, ckvg, qg, kg, wuq, wkn, wv) = body_refs
    return (cq[...], ckv[...], kr[...], ct[...], sa[...], sb[...], cqg[...], ckvg[...], qg[...], kg[...],
            [wrap(wuq[h]) for h in range(H)], [wrap(wkn[h]) for h in range(H)], wrap(wv[...]))


def _mla_prep_call(proj, tabs, cq_g, ckv_g, qg, kg, wuq, wkn, wv):
    s_len = proj.shape[0]
    tm = min(s_len, 256)

    def body(*refs):
        q_ref, k_ref, v_ref = refs[13:]
        q, k, v = _mla_prep_fn(*_mla_prep_args(refs[:13]))
        q_ref[...] = q.astype(q_ref.dtype)
        k_ref[...] = k.astype(k_ref.dtype)
        v_ref[...] = v.astype(v_ref.dtype)

    out = _bs((tm, H * LANES), lambda i: (i, 0))
    return pl.pallas_call(
        body, grid=(s_len // tm,), in_specs=_mla_prep_specs(tm), out_specs=[out, out, out],
        out_shape=[jax.ShapeDtypeStruct((s_len, H * LANES), MM)] * 3,
        name="mla_prep", compiler_params=_cparams(1))(proj, proj, proj, *tabs, cq_g, ckv_g, qg, kg, wuq, wkn, wv)


def _mla_prep_bwd_call(proj, tabs, cq_g, ckv_g, qg, kg, wuq, wkn, wv, dq, dk, dv):
    s_len = proj.shape[0]
    tm = min(s_len, 256)

    def body(*refs):
        dq_ref, dk_ref, dv_ref = refs[13:16]
        dcq_ref, dckv_ref, dkr_ref, dcqg_ref, dckvg_ref, dqg_ref, dkg_ref, dwuq_ref, dwkn_ref, dwv_ref = refs[16:]
        _, vjp = jax.vjp(_mla_prep_fn, *_mla_prep_args(refs[:13], _with_slot))
        (dcq, dckv, dkr, _, _, _, dcqg, dckvg, dqg, dkg, dwuq, dwkn, dwv) = vjp((dq_ref[...], dk_ref[...], dv_ref[...]))
        dwuq, dwkn, dwv = [d[1] for d in dwuq], [d[1] for d in dwkn], dwv[1]
        dcq_ref[...] = dcq.astype(dcq_ref.dtype)
        dckv_ref[...] = dckv.astype(dckv_ref.dtype)
        dkr_ref[...] = dkr.astype(dkr_ref.dtype)

        @pl.when(pl.program_id(0) == 0)
        def _():
            for r in (dcqg_ref, dckvg_ref, dqg_ref, dkg_ref, dwuq_ref, dwkn_ref, dwv_ref):
                r[...] = jnp.zeros_like(r)
        dcqg_ref[...] += dcqg
        dckvg_ref[...] += dckvg
        dqg_ref[...] += dqg
        dkg_ref[...] += dkg
        for h in range(H):
            dwuq_ref[h] += dwuq[h]
            dwkn_ref[h] += dwkn[h]
        dwv_ref[...] += dwv

    big = _bs((tm, H * LANES), lambda i: (i, 0))
    row = lambda w: _bs((tm, w), lambda i: (i, 0))
    full2 = lambda a, b: _bs((a, b), lambda i: (0, 0))
    full3 = lambda a, b, c: _bs((a, b, c), lambda i: (0, 0, 0))
    sd = jax.ShapeDtypeStruct
    return pl.pallas_call(
        body, grid=(s_len // tm,), in_specs=_mla_prep_specs(tm) + [big, big, big],
        out_specs=[row(QL), row(KVL), row(LANES), full2(1, QL), full2(1, KVL), full2(1, LANES), full2(1, LANES),
                   full3(H, QL, LANES), full3(H, KVL, LANES), full2(KVL, H * LANES)],
        out_shape=[sd((s_len, QL), MM), sd((s_len, KVL), MM), sd((s_len, LANES), MM), sd((1, QL), F32), sd((1, KVL), F32),
                   sd((1, LANES), F32), sd((1, LANES), F32), sd((H, QL, LANES), F32), sd((H, KVL, LANES), F32),
                   sd((KVL, H * LANES), F32)],
        name="mla_prep_bwd", compiler_params=_cparams(1))(proj, proj, proj, *tabs, cq_g, ckv_g, qg, kg, wuq, wkn, wv, dq, dk, dv)


def _attn_specs(s_len, tq):
    pair = 2 * LANES
    return [_bs((tq, pair), lambda p, i: (i, p)), _bs((s_len, pair), lambda p, i: (0, p)), _bs((s_len, pair), lambda p, i: (0, p)),
            _bs((tq, LANES), lambda p, i: (i, OFF_SG // LANES + p))]


def _attn_call(q, k, v, proj):
    s_len = q.shape[0]
    tq = min(s_len, 256)

    def body(q_ref, k_ref, v_ref, sg_ref, y_ref, o_ref, lse_ref):
        for n in range(s_len // tq):
            @pl.when(pl.program_id(1) == n)
            def _():
                kl = (n + 1) * tq
                o, lse = _attn_pair_fwd(q_ref[...], k_ref[:kl, :], v_ref[:kl, :])
                y_ref[...] = (o * jax.nn.silu(sg_ref[...])).astype(y_ref.dtype)
                o_ref[...] = o
                lse_ref[...] = lse

    tile = _bs((tq, LANES), lambda p, i: (i, p))
    sd = jax.ShapeDtypeStruct
    return pl.pallas_call(
        body, grid=(H // 2, s_len // tq), in_specs=_attn_specs(s_len, tq), out_specs=[tile, tile, tile],
        out_shape=[sd((s_len, BW), MM), sd((s_len, BW), F32), sd((s_len, BW), F32)],
        name="attn", compiler_params=_cparams(2))(q, k, v, proj)


def _attn_bwd_call(q, k, v, proj, dys, o, lse):
    s_len = q.shape[0]
    tq = min(s_len, 256)
    pair = 2 * LANES

    def body(q_ref, k_ref, v_ref, sg_ref, dy_ref, o_ref, lse_ref, dq_ref, dk_ref, dv_ref, dsg_ref):
        i = pl.program_id(1)

        @pl.when(i == 0)
        def _():
            dk_ref[...] = jnp.zeros_like(dk_ref)
            dv_ref[...] = jnp.zeros_like(dv_ref)

        for n in range(s_len // tq):
            @pl.when(i == n)
            def _():
                kl = (n + 1) * tq
                dq, dk, dv, dsg = _attn_pair_bwd(q_ref[...], k_ref[:kl, :], v_ref[:kl, :], sg_ref[...], dy_ref[...],
                                                 o_ref[...], lse_ref[...])
                dq_ref[...] = dq
                dsg_ref[...] = dsg.astype(dsg_ref.dtype)
                dk_ref[:kl, :] += dk
                dv_ref[:kl, :] += dv

    sd = jax.ShapeDtypeStruct
    tile = _bs((tq, LANES), lambda p, i: (i, p))
    return pl.pallas_call(
        body, grid=(H // 2, s_len // tq),
        in_specs=_attn_specs(s_len, tq) + [tile, tile, tile],
        out_specs=[_bs((tq, pair), lambda p, i: (i, p)), _bs((s_len, pair), lambda p, i: (0, p)),
                   _bs((s_len, pair), lambda p, i: (0, p)), tile],
        out_shape=[sd((s_len, H * LANES), F32), sd((s_len, H * LANES), F32), sd((s_len, H * LANES), F32), sd((s_len, BW), MM)],
        name="attn_bwd", compiler_params=_cparams(2))(q, k, v, proj, dys, o, lse)


def _shift_down(a, n):
    r = lax.broadcasted_iota(jnp.int32, a.shape, 0)
    return jnp.where(r >= n, pltpu.roll(a, n, 0), 0.0)


def _shift_up(a, n):
    s_len = a.shape[0]
    r = lax.broadcasted_iota(jnp.int32, a.shape, 0)
    return jnp.where(r < s_len - n, pltpu.roll(a, s_len - n, 0), 0.0)


def _conv_specs(s_len):
    col = lambda off: _bs((s_len, LANES), lambda j: (0, off // LANES + j))
    return [col(OFF_CV), col(OFF_CV + CW), col(OFF_CV + 2 * CW), col(OFF_SG + BW),
            _bs((3, LANES), lambda j: (0, j)), _bs((1, LANES), lambda j: (0, j))]


def _conv_call(proj, cw, cb):
    s_len = proj.shape[0]

    def body(bg_ref, cg_ref, xi_ref, sg_ref, w_ref, b_ref, y_ref):
        z = cg_ref[...] * xi_ref[...]
        y = b_ref[...] + w_ref[0:1, :] * _shift_down(z, 2)
        y = y + w_ref[1:2, :] * _shift_down(z, 1)
        y = y + w_ref[2:3, :] * z
        y_ref[...] = ((bg_ref[...] * y) * jax.nn.silu(sg_ref[...])).astype(y_ref.dtype)

    return pl.pallas_call(
        body, grid=(CW // LANES,), in_specs=_conv_specs(s_len), out_specs=_bs((s_len, LANES), lambda j: (0, j)),
        out_shape=jax.ShapeDtypeStruct((s_len, CW), MM), name="conv", compiler_params=_cparams(1))(proj, proj, proj, proj, cw, cb)


def _conv_bwd_call(proj, cw, cb, dys):
    s_len = proj.shape[0]

    def body(bg_ref, cg_ref, xi_ref, sg_ref, w_ref, b_ref, dys_ref, dbg_ref, dcg_ref, dxi_ref, dsg_ref, dw_ref, db_ref):
        bg, cg, xi, sg = bg_ref[...], cg_ref[...], xi_ref[...], sg_ref[...]
        w0, w1, w2 = w_ref[0:1, :], w_ref[1:2, :], w_ref[2:3, :]
        z = cg * xi
        z1, z2 = _shift_down(z, 1), _shift_down(z, 2)
        y = b_ref[...] + w0 * z2
        y = y + w1 * z1
        y = y + w2 * z
        yb = bg * y
        sig = jax.nn.sigmoid(sg)
        silu = sg * sig
        dys_v = dys_ref[...]
        dsg_ref[...] = (dys_v * yb * (sig * (1.0 + sg * (1.0 - sig)))).astype(dsg_ref.dtype)
        dyb = dys_v * silu
        dbg_ref[...] = (dyb * y).astype(dbg_ref.dtype)
        dy = dyb * bg
        db_ref[...] = jnp.sum(dy, axis=0, keepdims=True)
        dw_ref[0:1, :] = jnp.sum(dy * z2, axis=0, keepdims=True)
        dw_ref[1:2, :] = jnp.sum(dy * z1, axis=0, keepdims=True)
        dw_ref[2:3, :] = jnp.sum(dy * z, axis=0, keepdims=True)
        dz = w2 * dy + w1 * _shift_up(dy, 1) + w0 * _shift_up(dy, 2)
        dcg_ref[...] = (dz * xi).astype(dcg_ref.dtype)
        dxi_ref[...] = (dz * cg).astype(dxi_ref.dtype)

    col = _bs((s_len, LANES), lambda j: (0, j))
    sd = jax.ShapeDtypeStruct
    return pl.pallas_call(
        body, grid=(CW // LANES,), in_specs=_conv_specs(s_len) + [col],
        out_specs=[col, col, col, col, _bs((3, LANES), lambda j: (0, j)), _bs((1, LANES), lambda j: (0, j))],
        out_shape=[sd((s_len, CW), MM)] * 4 + [sd((3, CW), F32), sd((1, CW), F32)],
        name="conv_bwd", compiler_params=_cparams(1))(proj, proj, proj, proj, cw, cb, dys)


def _sg_specs(tm):
    row = lambda off: _bs((tm, SGW), lambda i: (i, off // SGW))
    return [row(OFF_SGI), row(OFF_SGI + SGW), row(OFF_SG + 2 * BW), _bs((1, SGW), lambda i: (0, 0)), _bs((1, SGW), lambda i: (0, 0)),
            _bs((SGG, SGC, SGC), lambda i: (0, 0, 0)), _bs((SGG, SGC, 1), lambda i: (0, 0, 0))]


def _sg_args(refs):
    u, v, sg, lg, lb, ws, bs = refs
    return (u[...], v[...], sg[...], lg[...], lb[...], [ws[g] for g in range(SGG)], [bs[g] for g in range(SGG)])


def _sg_call(proj, ln_g, ln_b, ws, bs):
    s_len = proj.shape[0]
    tm = min(s_len, 256)

    def body(*refs):
        refs[7][...] = _sg_fn(*_sg_args(refs[:7])).astype(refs[7].dtype)

    return pl.pallas_call(
        body, grid=(s_len // tm,), in_specs=_sg_specs(tm), out_specs=_bs((tm, SGW), lambda i: (i, 0)),
        out_shape=jax.ShapeDtypeStruct((s_len, SGW), MM), name="sgmlp", compiler_params=_cparams(1))(proj, proj, proj, ln_g, ln_b, ws, bs)


def _sg_bwd_call(proj, ln_g, ln_b, ws, bs, dys):
    s_len = proj.shape[0]
    tm = min(s_len, 256)

    def body(*refs):
        dys_ref = refs[7]
        du_ref, dv_ref, dsg_ref, dlg_ref, dlb_ref, dws_ref, dbs_ref = refs[8:]
        _, vjp = jax.vjp(_sg_fn, *_sg_args(refs[:7]))
        du, dv, dsg, dlg, dlb, dws, dbs = vjp(dys_ref[...])
        du_ref[...] = du.astype(du_ref.dtype)
        dv_ref[...] = dv.astype(dv_ref.dtype)
        dsg_ref[...] = dsg.astype(dsg_ref.dtype)

        @pl.when(pl.program_id(0) == 0)
        def _():
            for r in (dlg_ref, dlb_ref, dws_ref, dbs_ref):
                r[...] = jnp.zeros_like(r)
        dlg_ref[...] += dlg
        dlb_ref[...] += dlb
        for g in range(SGG):
            dws_ref[g] += dws[g]
            dbs_ref[g] += dbs[g]

    row = _bs((tm, SGW), lambda i: (i, 0))
    sd = jax.ShapeDtypeStruct
    return pl.pallas_call(
        body, grid=(s_len // tm,), in_specs=_sg_specs(tm) + [row],
        out_specs=[row, row, row, _bs((1, SGW), lambda i: (0, 0)), _bs((1, SGW), lambda i: (0, 0)),
                   _bs((SGG, SGC, SGC), lambda i: (0, 0, 0)), _bs((SGG, SGC, 1), lambda i: (0, 0, 0))],
        out_shape=[sd((s_len, SGW), MM)] * 3 + [sd((1, SGW), F32), sd((1, SGW), F32), sd((SGG, SGC, SGC), F32), sd((SGG, SGC, 1), F32)],
        name="sgmlp_bwd", compiler_params=_cparams(1))(proj, proj, proj, ln_g, ln_b, ws, bs, dys)


def _memkv_call(mem, mem_g, wm, kg):
    m_len = mem.shape[0]

    def body(mem_ref, g_ref, w_ref, kg_ref, k_ref, v_ref):
        k, v = _memkv_fn(mem_ref[...], g_ref[...], w_ref[...], kg_ref[...])
        k_ref[...] = k.astype(k_ref.dtype)
        v_ref[...] = v.astype(v_ref.dtype)

    return pl.pallas_call(body, out_shape=[jax.ShapeDtypeStruct((m_len, MH * MHD), MM)] * 2, name="memkv",
                          compiler_params=pltpu.CompilerParams(vmem_limit_bytes=VMEM_LIMIT))(mem, mem_g, wm, kg)


def _memkv_bwd_call(mem, mem_g, wm, kg, dk, dv):
    def body(mem_ref, g_ref, w_ref, kg_ref, dk_ref, dv_ref, dg_ref, dw_ref, dkg_ref):
        _, vjp = jax.vjp(_memkv_fn, mem_ref[...], g_ref[...], _with_slot(w_ref[...]), kg_ref[...])
        _, dg, dw, dkg = vjp((dk_ref[...], dv_ref[...]))
        dg_ref[...] = dg
        dw_ref[...] = dw[1]
        dkg_ref[...] = dkg

    sd = jax.ShapeDtypeStruct
    return pl.pallas_call(body, out_shape=[sd((1, D), F32), sd((D, 2 * MH * MHD), F32), sd((1, MHD), F32)], name="memkv_bwd",
                          compiler_params=pltpu.CompilerParams(vmem_limit_bytes=VMEM_LIMIT))(mem, mem_g, wm, kg, dk, dv)


def _mem_specs(tm, m_len):
    w = MH * MHD
    return [_bs((tm, w), lambda i: (i, OFF_MQ // w)), _bs((tm, BW), lambda i: (i, (OFF_SG + 3 * BW) // BW)),
            _bs((m_len, w), lambda i: (0, 0)), _bs((m_len, w), lambda i: (0, 0)), _bs((1, MHD), lambda i: (0, 0))]


def _mem_call(proj, k, v, qg):
    s_len, m_len = proj.shape[0], k.shape[0]
    tm = min(s_len, 256)

    def body(mq_ref, sg_ref, k_ref, v_ref, qg_ref, y_ref):
        y_ref[...] = _mem_fn(mq_ref[...], sg_ref[...], k_ref[...], v_ref[...], qg_ref[...]).astype(y_ref.dtype)

    return pl.pallas_call(
        body, grid=(s_len // tm,), in_specs=_mem_specs(tm, m_len), out_specs=_bs((tm, BW), lambda i: (i, 0)),
        out_shape=jax.ShapeDtypeStruct((s_len, BW), MM), name="memattn", compiler_params=_cparams(1))(proj, proj, k, v, qg)


def _mem_bwd_call(proj, k, v, qg, dys):
    s_len, m_len = proj.shape[0], k.shape[0]
    tm = min(s_len, 256)
    w = MH * MHD

    def body(mq_ref, sg_ref, k_ref, v_ref, qg_ref, dys_ref, dmq_ref, dsg_ref, dk_ref, dv_ref, dqg_ref):
        _, vjp = jax.vjp(_mem_fn, mq_ref[...], sg_ref[...], k_ref[...].astype(F32), v_ref[...].astype(F32), qg_ref[...])
        dmq, dsg, dk, dv, dqg = vjp(dys_ref[...])
        dmq_ref[...] = dmq.astype(dmq_ref.dtype)
        dsg_ref[...] = dsg.astype(dsg_ref.dtype)

        @pl.when(pl.program_id(0) == 0)
        def _():
            for r in (dk_ref, dv_ref, dqg_ref):
                r[...] = jnp.zeros_like(r)
        dk_ref[...] += dk
        dv_ref[...] += dv
        dqg_ref[...] += dqg

    row = _bs((tm, BW), lambda i: (i, 0))
    kv = _bs((m_len, w), lambda i: (0, 0))
    sd = jax.ShapeDtypeStruct
    return pl.pallas_call(
        body, grid=(s_len // tm,), in_specs=_mem_specs(tm, m_len) + [row],
        out_specs=[row, row, kv, kv, _bs((1, MHD), lambda i: (0, 0))],
        out_shape=[sd((s_len, w), MM), sd((s_len, BW), MM), sd((m_len, w), F32), sd((m_len, w), F32), sd((1, MHD), F32)],
        name="memattn_bwd", compiler_params=_cparams(1))(proj, proj, k, v, qg, dys)


def _merge_specs(tm):
    row = _bs((tm, BW), lambda i: (i, 0))
    return [row, row, row, row, _bs((tm, NB * D), lambda i: (i, OFF_ML // (NB * D))), _bs((NB, D), lambda i: (0, 0)),
            _bs((N_CHIPS, NB, BW, D // N_CHIPS), lambda i: (0, 0, 0, 0)), _bs((D, D), lambda i: (0, 0))]


def _merge_call(ys, proj, bm, wb, wo, x):
    s_len = proj.shape[0]
    tm = min(s_len, 256)

    def body(ya, yb, yc, yd, lg_ref, bm_ref, wb_ref, wo_ref, x_ref, o_ref):
        out = _merge_fn([r[...] for r in (ya, yb, yc, yd)], lg_ref[...], [bm_ref[n:n + 1, :] for n in range(NB)],
                        [[wb_ref[j, n] for n in range(NB)] for j in range(N_CHIPS)], wo_ref[...])
        o_ref[...] = x_ref[...] + out

    xrow = _bs((tm, D), lambda i: (i, 0))
    return pl.pallas_call(
        body, grid=(s_len // tm,), in_specs=_merge_specs(tm) + [xrow], out_specs=xrow,
        out_shape=jax.ShapeDtypeStruct((s_len, D), F32), name="merge", compiler_params=_cparams(1))(*ys, proj, bm, wb, wo, x)


def _merge_bwd_call(ys, proj, bm, wb, wo, dout):
    s_len = proj.shape[0]
    tm = min(s_len, 256)

    def body(ya, yb, yc, yd, lg_ref, bm_ref, wb_ref, wo_ref, do_ref, dya, dyb, dyc, dyd, dlg_ref, dbm_ref, dwb_ref, dwo_ref):
        fn = lambda ys_, lg_, bm_, wb_, wo_: _merge_fn(ys_, lg_, bm_, wb_, wo_)
        _, vjp = jax.vjp(fn, [r[...].astype(F32) for r in (ya, yb, yc, yd)], lg_ref[...], [bm_ref[n:n + 1, :] for n in range(NB)],
                         [[_with_slot(wb_ref[j, n]) for n in range(NB)] for j in range(N_CHIPS)], _with_slot(wo_ref[...]))
        dys, dlg, dbm, dwb, dwo = vjp(do_ref[...])
        dwb, dwo = [[d[1] for d in row] for row in dwb], dwo[1]
        for r, d in zip((dya, dyb, dyc, dyd), dys):
            r[...] = d
        dlg_ref[...] = dlg.astype(dlg_ref.dtype)

        @pl.when(pl.program_id(0) == 0)
        def _():
            for r in (dbm_ref, dwb_ref, dwo_ref):
                r[...] = jnp.zeros_like(r)
        for n in range(NB):
            dbm_ref[n:n + 1, :] += dbm[n]
            for j in range(N_CHIPS):
                dwb_ref[j, n] += dwb[j][n]
        dwo_ref[...] += dwo

    row = _bs((tm, BW), lambda i: (i, 0))
    sd = jax.ShapeDtypeStruct
    wb_shape = (N_CHIPS, NB, BW, D // N_CHIPS)
    return pl.pallas_call(
        body, grid=(s_len // tm,), in_specs=_merge_specs(tm) + [_bs((tm, D), lambda i: (i, 0))],
        out_specs=[row, row, row, row, _bs((tm, NB * D), lambda i: (i, 0)), _bs((NB, D), lambda i: (0, 0)),
                   _bs(wb_shape, lambda i: (0, 0, 0, 0)), _bs((D, D), lambda i: (0, 0))],
        out_shape=[sd((s_len, BW), F32)] * 4 + [sd((s_len, NB * D), MM), sd((NB, D), F32), sd(wb_shape, F32), sd((D, D), F32)],
        name="merge_bwd", compiler_params=_cparams(1))(*ys, proj, bm, wb, wo, dout)


def _dh_call(dproj, w, x, g, dout):
    s_len = x.shape[0]
    tm, tk = min(s_len, 512), NP // 4

    def body(dp_ref, w_ref, x_ref, g_ref, do_ref, dx_ref, dg_ref, acc_ref):
        i, k = pl.program_id(0), pl.program_id(1)

        @pl.when(k == 0)
        def _():
            acc_ref[...] = jnp.zeros_like(acc_ref)
        acc_ref[...] += lax.dot_general(dp_ref[...], w_ref[...], (((1,), (1,)), ((), ())), preferred_element_type=F32)

        @pl.when(k == pl.num_programs(1) - 1)
        def _():
            _, vjp = jax.vjp(lambda x_, g_: _rms_n(x_, g_, D), x_ref[...], g_ref[...])
            dxr, dgr = vjp(acc_ref[...])
            dx_ref[...] = do_ref[...] + dxr

            @pl.when(i == 0)
            def _():
                dg_ref[...] = jnp.zeros_like(dg_ref)
            dg_ref[...] += dgr

    row = _bs((tm, D), lambda i, k: (i, 0))
    return pl.pallas_call(
        body, grid=(s_len // tm, NP // tk),
        in_specs=[_bs((tm, tk), lambda i, k: (i, k)), _bs((D, tk), lambda i, k: (0, k)), row, _bs((1, D), lambda i, k: (0, 0)), row],
        out_specs=[row, _bs((1, D), lambda i, k: (0, 0))],
        out_shape=[jax.ShapeDtypeStruct((s_len, D), F32), jax.ShapeDtypeStruct((1, D), F32)],
        scratch_shapes=[pltpu.VMEM((tm, D), F32)], name="dh", compiler_params=_cparams(2))(dproj, w, x, g, dout)


def _dw_call(h, dproj):
    s_len = h.shape[0]
    tn = 512

    def body(h_ref, dp_ref, o_ref):
        o_ref[...] = lax.dot_general(h_ref[...], dp_ref[...], (((0,), (0,)), ((), ())), preferred_element_type=F32)

    return pl.pallas_call(
        body, grid=(NP // tn,), in_specs=[_bs((s_len, D), lambda j: (0, 0)), _bs((s_len, tn), lambda j: (0, j))],
        out_specs=_bs((D, tn), lambda j: (0, j)), out_shape=jax.ShapeDtypeStruct((D, NP), F32),
        name="dw_in", compiler_params=_cparams(1))(h, dproj)


def _loss_call(y, target):
    s_len = y.shape[0]
    tm = min(s_len, 512)

    def body(y_ref, t_ref, dy_ref, l_ref):
        e = y_ref[...] - t_ref[...]
        dy_ref[...] = e * (1.0 / D)

        @pl.when(pl.program_id(0) == 0)
        def _():
            l_ref[...] = jnp.zeros_like(l_ref)
        l_ref[...] += jnp.sum(e * e, axis=0, keepdims=True)

    row = _bs((tm, D), lambda i: (i, 0))
    return pl.pallas_call(
        body, grid=(s_len // tm,), in_specs=[row, row], out_specs=[row, _bs((1, D), lambda i: (0, 0))],
        out_shape=[jax.ShapeDtypeStruct((s_len, D), F32), jax.ShapeDtypeStruct((1, D), F32)],
        name="loss", compiler_params=_cparams(1))(y, target)


def _adamw_call(w, g, m, v, name):
    rows, cols = w.shape
    tr = min(_row_tile(rows), 128)

    def body(w_ref, g_ref, m_ref, v_ref, d_ref, nm_ref, nv_ref):
        gv = g_ref[...]
        m2 = ADAM_B1 * m_ref[...] + (1.0 - ADAM_B1) * gv
        v2 = ADAM_B2 * v_ref[...] + (1.0 - ADAM_B2) * (gv * gv)
        m_hat = m2 / (1.0 - ADAM_B1 ** ADAM_STEP)
        v_hat = v2 / (1.0 - ADAM_B2 ** ADAM_STEP)
        d_ref[...] = -ADAM_LR * (m_hat / (jnp.sqrt(v_hat) + ADAM_EPS) + ADAM_WD * w_ref[...])
        nm_ref[...] = m2
        nv_ref[...] = v2

    blk = _bs((tr, cols), lambda i: (i, 0))
    return pl.pallas_call(
        body, grid=(rows // tr,), in_specs=[blk] * 4, out_specs=[blk] * 3,
        out_shape=[jax.ShapeDtypeStruct((rows, cols), F32)] * 3, name=name, compiler_params=_cparams(1))(w, g, m, v)


def _adamw_layer_call(layer, ws, gs, ms, vs, prev, after, name, steps=8):
    n = len(ws)
    after = list(after)
    n_prev = 4 * n if prev is not None else 0

    def body(*refs):
        outs = refs[len(refs) - 4 * n:]
        for t in range(n):
            w_ref, g_ref, m_ref, v_ref = refs[t], refs[n + t], refs[2 * n + t], refs[3 * n + t]
            g_out, d_out, m_out, v_out = outs[4 * t:4 * t + 4]
            gv = g_ref[...]
            m2 = ADAM_B1 * m_ref[0] + (1.0 - ADAM_B1) * gv
            v2 = ADAM_B2 * v_ref[0] + (1.0 - ADAM_B2) * (gv * gv)
            m_hat = m2 / (1.0 - ADAM_B1 ** ADAM_STEP)
            v_hat = v2 / (1.0 - ADAM_B2 ** ADAM_STEP)
            g_out[0] = gv
            d_out[0] = -ADAM_LR * (m_hat / (jnp.sqrt(v_hat) + ADAM_EPS) + ADAM_WD * w_ref[0])
            m_out[0] = m2
            v_out[0] = v2

    def lay(a):
        return _bs((1, a.shape[1] // steps, a.shape[2]), lambda i: (layer, i, 0))

    in_specs = ([lay(a) for a in ws] + [_bs((g.shape[0] // steps, g.shape[1]), lambda i: (i, 0)) for g in gs]
                + [lay(a) for a in ms] + [lay(a) for a in vs] + [_ANY] * (n_prev + len(after)))
    return pl.pallas_call(
        body, grid=(steps,), in_specs=in_specs, out_specs=[lay(ws[t]) for t in range(n) for _ in range(4)],
        out_shape=[jax.ShapeDtypeStruct(ws[t].shape, F32) for t in range(n) for _ in range(4)],
        input_output_aliases={4 * n + q: q for q in range(n_prev)}, name=name, compiler_params=_cparams(1),
    )(*ws, *gs, *ms, *vs, *(prev if prev is not None else []), *after)


def _row_tile(rows):
    for cand in (512, 256, 128, 64, 32, 16, 8):
        if rows % cand == 0 and rows > cand:
            return cand
    return rows


def _pair_sum_call(grads, from_sibling, core, name):
    n = len(grads)

    def body(core_ref, *refs):
        for t in range(n):
            refs[2 * n + t][...] = (refs[t][...] + refs[n + t][...]).astype(MM)

    half = lambda g: (1, g.shape[1] // 2, g.shape[2])
    grid_spec = pltpu.PrefetchScalarGridSpec(
        num_scalar_prefetch=1, grid=(N_CHIPS,),
        in_specs=[pl.BlockSpec(half(g), lambda j, core_ref: (j, core_ref[0], 0)) for g in grads]
        + [pl.BlockSpec(half(g), lambda j, core_ref: (j, 0, 0)) for g in grads],
        out_specs=[pl.BlockSpec(half(g), lambda j, core_ref: (j, 0, 0)) for g in grads])
    return pl.pallas_call(
        body, grid_spec=grid_spec, out_shape=[jax.ShapeDtypeStruct((N_CHIPS,) + half(g)[1:], MM) for g in grads], name=name,
        compiler_params=_cparams(1))(core, *grads, *from_sibling)


def _owner_sum_call(chip_sums, from_chips, chip_core, name):
    n = len(chip_sums)
    steps = 4

    def body(ids_ref, *refs):
        for t in range(n):
            a, b = refs[t], refs[n + t]
            refs[2 * n + t][...] = ((a[0].astype(F32) + b[0].astype(F32)) + b[1].astype(F32)) + b[2].astype(F32)

    tile = lambda p: (p.shape[1] // steps, p.shape[2])
    grid_spec = pltpu.PrefetchScalarGridSpec(
        num_scalar_prefetch=1, grid=(steps,),
        in_specs=[pl.BlockSpec((1,) + tile(p), lambda i, ids_ref: (ids_ref[0], i, 0)) for p in chip_sums]
        + [pl.BlockSpec((3,) + tile(p), lambda i, ids_ref: (0, i, 0)) for p in chip_sums],
        out_specs=[pl.BlockSpec(tile(p), lambda i, ids_ref: (ids_ref[1] * steps + i, 0)) for p in chip_sums])
    return pl.pallas_call(
        body, grid_spec=grid_spec, out_shape=[jax.ShapeDtypeStruct((2 * p.shape[1], p.shape[2]), F32) for p in chip_sums],
        name=name, compiler_params=_cparams(1))(chip_core, *chip_sums, *from_chips)


def _sum8_call(parts):
    n, rows, cols = parts.shape
    tr = _row_tile(rows)

    def body(p_ref, o_ref):
        acc = p_ref[0]
        for k in range(1, n):
            acc = acc + p_ref[k]
        o_ref[...] = acc

    return pl.pallas_call(
        body, grid=(rows // tr,), in_specs=[_bs((n, tr, cols), lambda i: (0, i, 0))], out_specs=_bs((tr, cols), lambda i: (i, 0)),
        out_shape=jax.ShapeDtypeStruct((rows, cols), F32), name="sum_small_grads", compiler_params=_cparams(1))(parts)


_ANY = pl.BlockSpec(memory_space=pl.ANY)


def _all_gather8(blk, name):
    rows, cols = blk.shape

    def body(x_ref, out_ref, send_sems, recv_sems, local_sem):
        x, y, c = lax.axis_index("x"), lax.axis_index("y"), lax.axis_index("c")
        me, sibling = (x, y, c), (x, y, 1 - c)
        chips = [(1 - x, y), (x, 1 - y), (1 - x, 1 - y)]

        def slot(px, py, pc):
            return out_ref.at[4 * px + 2 * py + pc]

        def copy(k, block, to, src=None):
            return pltpu.make_async_remote_copy(
                src_ref=slot(*block) if src is None else src, dst_ref=slot(*block),
                send_sem=send_sems.at[k], recv_sem=recv_sems.at[k], device_id=to, device_id_type=MESH_ID)

        mine = pltpu.make_async_copy(x_ref, slot(*me), local_sem)
        mine.start()
        first = [copy(0, me, sibling, src=x_ref)]
        first += [copy(1 + j, me, (*chip, c), src=x_ref) for j, chip in enumerate(chips)]
        for cp in first:
            cp.start()
        passed = [copy(4 + j, (*chip, c), sibling) for j, chip in enumerate(chips)]
        for j, chip in enumerate(chips):
            copy(1 + j, (*chip, c), me).wait_recv()
            passed[j].start()
        copy(0, sibling, me).wait_recv()
        for j, chip in enumerate(chips):
            copy(4 + j, (*chip, 1 - c), me).wait_recv()
        for cp in first + passed:
            cp.wait_send()
        mine.wait()

    return pl.pallas_call(
        body, out_shape=jax.ShapeDtypeStruct((8, rows, cols), blk.dtype), in_specs=[_ANY], out_specs=_ANY,
        scratch_shapes=[pltpu.SemaphoreType.DMA((7,)), pltpu.SemaphoreType.DMA((7,)), pltpu.SemaphoreType.DMA],
        name=name)(blk)


def _half_rows(ref, lead, half, which):
    rows = pl.ds(pl.multiple_of(half * which, half), half)
    return ref.at[rows] if lead is None else ref.at[lead, rows]


_HBM = pl.BlockSpec(memory_space=pltpu.HBM)
_SEM = pl.BlockSpec(memory_space=pltpu.SEMAPHORE)
_ORDERED_EFFECT = pltpu.CompilerParams(has_side_effects=pltpu.SideEffectType.DATAFLOW_SIDE_EFFECTING)


_VMEM = pl.BlockSpec(memory_space=pltpu.VMEM)
_TOKEN = jax.ShapeDtypeStruct((8, LANES), F32)


def _in_hbm(a):
    return pltpu.with_memory_space_constraint(a, pltpu.HBM)


def _tie(small, token):
    return small + token[0:1, 0:1].reshape((1,) * small.ndim)


def _pair_exchange_start_call(grads, name):
    n = len(grads)
    half = [g.shape[1] // 2 for g in grads]

    def body(*refs):
        srcs, outs = refs[:n], refs[n:2 * n]
        send_sems, recv_sems, token = refs[2 * n:]
        x, y, c = lax.axis_index("x"), lax.axis_index("y"), lax.axis_index("c")
        for t in range(n):
            pltpu.make_async_remote_copy(
                src_ref=srcs[t].at[:, pl.ds(pl.multiple_of(half[t] * (1 - c), half[t]), half[t])], dst_ref=outs[t],
                send_sem=send_sems.at[t], recv_sem=recv_sems.at[t], device_id=(x, y, 1 - c), device_id_type=MESH_ID).start()
        token[...] = jnp.zeros_like(token)

    dma = pltpu.SemaphoreType.DMA
    return pl.pallas_call(
        body, out_shape=[pltpu.HBM((g.shape[0], g.shape[1] // 2, g.shape[2]), g.dtype) for g in grads] + [dma((n,)), dma((n,)), _TOKEN],
        in_specs=[_HBM] * n, out_specs=[_HBM] * n + [_SEM, _SEM, _VMEM], name=name, compiler_params=_ORDERED_EFFECT,
    )(*[_in_hbm(g) for g in grads])


def _pair_exchange_finish_call(grads, bufs, send_sems, recv_sems, after, name):
    n = len(grads)
    after = list(after)
    half = [g.shape[1] // 2 for g in grads]

    def body(*refs):
        srcs, ins, send_ref, recv_ref = refs[:n], refs[n:2 * n], refs[2 * n], refs[2 * n + 1]
        x, y, c = lax.axis_index("x"), lax.axis_index("y"), lax.axis_index("c")
        for t in range(n):
            pltpu.make_async_remote_copy(
                src_ref=srcs[t].at[:, pl.ds(pl.multiple_of(half[t] * (1 - c), half[t]), half[t])], dst_ref=ins[t],
                send_sem=send_ref.at[t], recv_sem=recv_ref.at[t], device_id=(x, y, 1 - c), device_id_type=MESH_ID).wait()

    return pl.pallas_call(
        body, out_shape=[pltpu.HBM(b.shape, b.dtype) for b in bufs],
        in_specs=[_HBM] * (2 * n) + [_SEM, _SEM] + [_ANY] * len(after), out_specs=[_HBM] * n,
        input_output_aliases={n + t: t for t in range(n)}, name=name, compiler_params=_ORDERED_EFFECT,
    )(*[_in_hbm(g) for g in grads], *bufs, send_sems, recv_sems, *after)


def _chip_scatter_start_call(chip_sums, name):
    n = len(chip_sums)

    def body(*refs):
        srcs, outs = refs[:n], refs[n:2 * n]
        send_sems, recv_sems, token = refs[2 * n:]
        x, y, c = lax.axis_index("x"), lax.axis_index("y"), lax.axis_index("c")
        chips = [(1 - x, y), (x, 1 - y), (1 - x, 1 - y)]
        for k, (cx, cy) in enumerate(chips):
            for t in range(n):
                pltpu.make_async_remote_copy(
                    src_ref=srcs[t].at[2 * cx + cy], dst_ref=outs[t].at[k], send_sem=send_sems.at[3 * t + k],
                    recv_sem=recv_sems.at[3 * t + k], device_id=(cx, cy, c), device_id_type=MESH_ID).start()
        token[...] = jnp.zeros_like(token)

    dma = pltpu.SemaphoreType.DMA
    return pl.pallas_call(
        body, out_shape=[pltpu.HBM((3,) + p.shape[1:], p.dtype) for p in chip_sums] + [dma((3 * n,)), dma((3 * n,)), _TOKEN],
        in_specs=[_HBM] * n, out_specs=[_HBM] * n + [_SEM, _SEM, _VMEM], name=name, compiler_params=_ORDERED_EFFECT,
    )(*[_in_hbm(p) for p in chip_sums])


def _chip_scatter_finish_call(chip_sums, bufs, send_sems, recv_sems, after, name):
    n = len(chip_sums)
    after = list(after)

    def body(*refs):
        srcs, ins, send_ref, recv_ref = refs[:n], refs[n:2 * n], refs[2 * n], refs[2 * n + 1]
        x, y, c = lax.axis_index("x"), lax.axis_index("y"), lax.axis_index("c")
        chips = [(1 - x, y), (x, 1 - y), (1 - x, 1 - y)]
        for k, (cx, cy) in enumerate(chips):
            for t in range(n):
                pltpu.make_async_remote_copy(
                    src_ref=srcs[t].at[2 * cx + cy], dst_ref=ins[t].at[k], send_sem=send_ref.at[3 * t + k],
                    recv_sem=recv_ref.at[3 * t + k], device_id=(cx, cy, c), device_id_type=MESH_ID).wait()

    return pl.pallas_call(
        body, out_shape=[pltpu.HBM(b.shape, b.dtype) for b in bufs],
        in_specs=[_HBM] * (2 * n) + [_SEM, _SEM] + [_ANY] * len(after), out_specs=[_HBM] * n,
        input_output_aliases={n + t: t for t in range(n)}, name=name, compiler_params=_ORDERED_EFFECT,
    )(*[_in_hbm(p) for p in chip_sums], *bufs, send_sems, recv_sems, *after)


def _place_own_call(mine, chip_core, name):
    n = len(mine)

    def body(ids_ref, *refs):
        for t in range(n):
            refs[n + t][0] = refs[t][...]

    def imap_out(s):
        pad = (0,) * (s.ndim - 1)
        return lambda i, ids_ref: (ids_ref[0], ids_ref[1]) + pad

    grid_spec = pltpu.PrefetchScalarGridSpec(
        num_scalar_prefetch=1, grid=(1,), in_specs=[pl.BlockSpec(s.shape, lambda i, ids_ref, k=s.ndim: (0,) * k) for s in mine],
        out_specs=[pl.BlockSpec((1,) + s.shape, imap_out(s)) for s in mine])
    return pl.pallas_call(
        body, grid_spec=grid_spec,
        out_shape=[jax.ShapeDtypeStruct((N_CHIPS, 2 * s.shape[0]) + s.shape[1:], s.dtype) for s in mine],
        name=name, compiler_params=_cparams(1))(chip_core, *mine)


def _gather_start_call(mine, bufs, after, name):
    n = len(mine)
    half = [s.shape[0] for s in mine]

    def body(*refs):
        srcs, outs = refs[:n], refs[2 * n + 1:3 * n + 1]
        send_sems, recv_sib, recv_ici, token = refs[3 * n + 1:]
        x, y, c = lax.axis_index("x"), lax.axis_index("y"), lax.axis_index("c")
        chips = [(1 - x, y), (x, 1 - y), (1 - x, 1 - y)]
        for t in range(n):
            dst = _half_rows(outs[t], 2 * x + y, half[t], c)
            pltpu.make_async_remote_copy(src_ref=srcs[t], dst_ref=dst, send_sem=send_sems.at[4 * t], recv_sem=recv_sib.at[t],
                                         device_id=(x, y, 1 - c), device_id_type=MESH_ID).start()
            for j, chip in enumerate(chips):
                pltpu.make_async_remote_copy(src_ref=srcs[t], dst_ref=dst, send_sem=send_sems.at[4 * t + 1 + j],
                                             recv_sem=recv_ici.at[3 * t + j], device_id=(*chip, c), device_id_type=MESH_ID).start()
        token[...] = jnp.zeros_like(token)

    dma = pltpu.SemaphoreType.DMA
    return pl.pallas_call(
        body, out_shape=[pltpu.HBM(b.shape, b.dtype) for b in bufs] + [dma((4 * n,)), dma((n,)), dma((3 * n,)), _TOKEN],
        in_specs=[_HBM] * (2 * n) + [_ANY], out_specs=[_HBM] * n + [_SEM] * 3 + [_VMEM],
        input_output_aliases={n + t: t for t in range(n)}, name=name, compiler_params=_ORDERED_EFFECT,
    )(*[_in_hbm(s) for s in mine], *[_in_hbm(b) for b in bufs], after)


def _gather_forward_call(bufs, recv_ici, after, name):
    n = len(bufs)
    half = [b.shape[1] // 2 for b in bufs]

    def body(*refs):
        ins, recv_ici_ref = refs[:n], refs[n]
        outs = refs[n + 2:2 * n + 2]
        send_fwd, recv_fwd, token = refs[2 * n + 2:]
        x, y, c = lax.axis_index("x"), lax.axis_index("y"), lax.axis_index("c")
        chips = [(1 - x, y), (x, 1 - y), (1 - x, 1 - y)]
        for j, (cx, cy) in enumerate(chips):
            for t in range(n):
                landed = _half_rows(ins[t], 2 * cx + cy, half[t], c)
                dst = _half_rows(outs[t], 2 * cx + cy, half[t], c)
                pltpu.make_async_remote_copy(src_ref=landed, dst_ref=landed, send_sem=send_fwd.at[3 * t + j],
                                             recv_sem=recv_ici_ref.at[3 * t + j], device_id=(cx, cy, c),
                                             device_id_type=MESH_ID).wait_recv()
                pltpu.make_async_remote_copy(src_ref=landed, dst_ref=dst, send_sem=send_fwd.at[3 * t + j],
                                             recv_sem=recv_fwd.at[3 * t + j], device_id=(x, y, 1 - c),
                                             device_id_type=MESH_ID).start()
        token[...] = jnp.zeros_like(token)

    dma = pltpu.SemaphoreType.DMA
    return pl.pallas_call(
        body, out_shape=[pltpu.HBM(b.shape, b.dtype) for b in bufs] + [dma((3 * n,)), dma((3 * n,)), _TOKEN],
        in_specs=[_HBM] * n + [_SEM, _ANY], out_specs=[_HBM] * n + [_SEM] * 2 + [_VMEM],
        input_output_aliases={t: t for t in range(n)}, name=name, compiler_params=_ORDERED_EFFECT,
    )(*bufs, recv_ici, after)


def _gather_finish_call(shards, bufs, send_sems, recv_sib, send_fwd, recv_fwd, after, name):
    n = len(bufs)
    half = [b.shape[1] // 2 for b in bufs]

    def body(*refs):
        srcs, ins = refs[:n], refs[n:2 * n]
        send_ref, recv_sib_ref, send_fwd_ref, recv_fwd_ref = refs[2 * n:2 * n + 4]
        x, y, c = lax.axis_index("x"), lax.axis_index("y"), lax.axis_index("c")
        chips = [(1 - x, y), (x, 1 - y), (1 - x, 1 - y)]
        sibling = (x, y, 1 - c)
        for t in range(n):
            for k in range(4):
                pltpu.make_async_remote_copy(src_ref=srcs[t], dst_ref=srcs[t], send_sem=send_ref.at[4 * t + k],
                                             recv_sem=recv_sib_ref.at[t], device_id=sibling, device_id_type=MESH_ID).wait_send()
            from_sibling = _half_rows(ins[t], 2 * x + y, half[t], 1 - c)
            pltpu.make_async_remote_copy(src_ref=from_sibling, dst_ref=from_sibling, send_sem=send_ref.at[4 * t],
                                         recv_sem=recv_sib_ref.at[t], device_id=sibling, device_id_type=MESH_ID).wait_recv()
            for j, (cx, cy) in enumerate(chips):
                sent = _half_rows(ins[t], 2 * cx + cy, half[t], c)
                passed = _half_rows(ins[t], 2 * cx + cy, half[t], 1 - c)
                pltpu.make_async_remote_copy(src_ref=sent, dst_ref=passed, send_sem=send_fwd_ref.at[3 * t + j],
                                             recv_sem=recv_fwd_ref.at[3 * t + j], device_id=sibling, device_id_type=MESH_ID).wait()

    return pl.pallas_call(
        body, out_shape=[pltpu.HBM(b.shape, b.dtype) for b in bufs],
        in_specs=[_HBM] * (2 * n) + [_SEM] * 4 + [_ANY], out_specs=[_HBM] * n,
        input_output_aliases={n + t: t for t in range(n)}, name=name, compiler_params=_ORDERED_EFFECT,
    )(*[_in_hbm(s) for s in shards], *bufs, send_sems, recv_sib, send_fwd, recv_fwd, after)


def _pair_gather_call(bufs, name):
    n = len(bufs)
    half = [b.shape[0] // 2 for b in bufs]

    def body(*refs):
        srcs, outs, send_sems, recv_sems = refs[:n], refs[n:2 * n], refs[2 * n], refs[2 * n + 1]
        x, y, c = lax.axis_index("x"), lax.axis_index("y"), lax.axis_index("c")
        for t in range(n):
            pltpu.make_async_remote_copy(
                src_ref=_half_rows(srcs[t], None, half[t], c), dst_ref=_half_rows(outs[t], None, half[t], c),
                send_sem=send_sems.at[t], recv_sem=recv_sems.at[t], device_id=(x, y, 1 - c), device_id_type=MESH_ID).start()
        for t in range(n):
            pltpu.make_async_remote_copy(
                src_ref=_half_rows(srcs[t], None, half[t], c), dst_ref=_half_rows(outs[t], None, half[t], 1 - c),
                send_sem=send_sems.at[t], recv_sem=recv_sems.at[t], device_id=(x, y, 1 - c), device_id_type=MESH_ID).wait()

    return pl.pallas_call(
        body, out_shape=[jax.ShapeDtypeStruct(b.shape, b.dtype) for b in bufs], in_specs=[_ANY] * n, out_specs=[_ANY] * n,
        input_output_aliases={t: t for t in range(n)},
        scratch_shapes=[pltpu.SemaphoreType.DMA((n,)), pltpu.SemaphoreType.DMA((n,))], name=name)(*bufs)


def _pack_rows(flats, dtype, row_multiple):
    flat = jnp.concatenate([f.reshape(-1).astype(dtype) for f in flats])
    n = flat.shape[0]
    rows = -(-n // PACK_W)
    rows = -(-rows // row_multiple) * row_multiple
    return jnp.pad(flat, (0, rows * PACK_W - n)).reshape(rows, PACK_W)


def _unpack(flat, shapes):
    out, off = [], 0
    for shp in shapes:
        n = math.prod(shp)
        out.append(flat[off:off + n].reshape(shp))
        off += n
    return out


_W_IN_SEGMENTS = ((R_ML, R_END, OFF_ML), (R_SG, R_ML, OFF_SG), (R_CV, R_SGI, OFF_CV), (R_SGI, R_MQ, OFF_SGI), (R_MQ, R_SG, OFF_MQ),
                  (R_CQ, R_CKV, OFF_CQ), (R_CKV, R_KR, OFF_CKV), (R_KR, R_CV, OFF_KR + NOPE))
W_IN_SHARD = R_END // N_CHIPS


def _realign_call(wg):
    tr = 128

    def body(w_ref, o_ref):
        pieces, pos = [], 0
        for r0, r1, a0 in _W_IN_SEGMENTS:
            if a0 > pos:
                pieces.append(jnp.zeros((tr, a0 - pos), o_ref.dtype))
            while r0 < r1:
                j = r0 // W_IN_SHARD
                hi = min(r1, (j + 1) * W_IN_SHARD)
                pieces.append(w_ref[j, :, r0 - j * W_IN_SHARD:hi - j * W_IN_SHARD])
                a0, r0 = a0 + hi - r0, hi
            pos = a0
        pieces.append(jnp.zeros((tr, NP - pos), o_ref.dtype))
        o_ref[...] = jnp.concatenate(pieces, axis=1)

    return pl.pallas_call(
        body, grid=(D // tr,), in_specs=[_bs((N_CHIPS, tr, W_IN_SHARD), lambda i: (0, i, 0))],
        out_specs=_bs((tr, NP), lambda i: (i, 0)), out_shape=jax.ShapeDtypeStruct((D, NP), wg.dtype),
        name="w_in_realign", compiler_params=_cparams(1))(wg)


def _unalign_call(dw):
    tr = 128
    by_ref = sorted(_W_IN_SEGMENTS)

    def body(dw_ref, o_ref):
        for j in range(N_CHIPS):
            lo_j, hi_j = j * W_IN_SHARD, (j + 1) * W_IN_SHARD
            pieces = []
            for r0, r1, a0 in by_ref:
                lo, hi = max(r0, lo_j), min(r1, hi_j)
                if lo < hi:
                    pieces.append(dw_ref[:, a0 + lo - r0:a0 + hi - r0])
            o_ref[j] = jnp.concatenate(pieces, axis=1)

    return pl.pallas_call(
        body, grid=(D // tr,), in_specs=[_bs((tr, NP), lambda i: (i, 0))],
        out_specs=_bs((N_CHIPS, tr, W_IN_SHARD), lambda i: (0, i, 0)),
        out_shape=jax.ShapeDtypeStruct((N_CHIPS, D, W_IN_SHARD), dw.dtype), name="w_in_unalign", compiler_params=_cparams(1))(dw)


def _wuq_to_heads(w):
    w3 = w.reshape(QL, H, QKH)
    w3 = jnp.pad(w3, ((0, 0), (0, 0), (0, LANES - QKH)))
    return jnp.transpose(w3, (1, 0, 2))


def _wuq_from_heads(wh):
    return jnp.transpose(wh[:, :, :QKH], (1, 0, 2)).reshape(QL, H * QKH)


def _wukv_to_heads(w):
    w3 = w.reshape(KVL, H, NOPE + VH)
    wkn = jnp.transpose(jnp.pad(w3[:, :, :NOPE], ((0, 0), (0, 0), (0, LANES - NOPE))), (1, 0, 2))
    wv3 = w3[:, :, NOPE:]
    z = jnp.zeros((KVL, VH), w.dtype)
    cols = []
    for h in range(H):
        cols += [wv3[:, h], z] if h % 2 == 0 else [z, wv3[:, h]]
    return wkn, jnp.concatenate(cols, axis=1)


def _wukv_from_heads(wkn, wv):
    kn = jnp.transpose(wkn[:, :, :NOPE], (1, 0, 2))
    vs = jnp.stack([wv[:, LANES * h + VH * (h % 2):LANES * h + VH * (h % 2) + VH] for h in range(H)], axis=1)
    return jnp.concatenate([kn, vs], axis=2).reshape(KVL, H * (NOPE + VH))


def _layer_fwd(x, mem, tabs, p):
    proj, h = _proj_call(x, p["norm_g"], p["w_in"])
    if p.get("late") is not None:
        p = dict(p, **p["late"](proj))
    q, k, v = _mla_prep_call(proj, tabs, p["cq_g"], p["ckv_g"], p["qg"], p["kg"], p["wuq"], p["wkn"], p["wv"])
    ya, attn_o, attn_lse = _attn_call(q, k, v, proj)
    bm = p["bm"]
    if p.get("after_attn") is not None:
        bm = _tie(bm, p["after_attn"](ya))
    yb = _conv_call(proj, p["conv_w"], p["conv_b"])
    yc = _sg_call(proj, p["ln_g"], p["ln_b"], p["ws"], p["bs"])
    mk, mv = _memkv_call(mem, p["mem_g"], p["wm"], p["mkg"])
    yd = _mem_call(proj, mk, mv, p["mqg"])
    out = _merge_call((ya, yb, yc, yd), proj, bm, p["wb"], p["wo"], x)
    return out, dict(p=p, x=x, proj=proj, h=h, q=q, k=k, v=v, attn_o=attn_o, attn_lse=attn_lse, ys=(ya, yb, yc, yd), mk=mk, mv=mv)


def _layer_bwd(dout, mem, tabs, p, sv, start_after=None, on_rest_grads=None, on_grads=None):
    proj = sv["proj"]
    bm = p["bm"] if start_after is None else _tie(p["bm"], start_after)
    dya, dyb, dyc, dyd, dml, dbm, dwb, dwo = _merge_bwd_call(sv["ys"], proj, bm, p["wb"], p["wo"], dout)
    dq, dk, dv, dsg_a = _attn_bwd_call(sv["q"], sv["k"], sv["v"], proj, dya, sv["attn_o"], sv["attn_lse"])
    dcq, dckv, dkr, dcqg, dckvg, dqg, dkg, dwuq, dwkn, dwv = _mla_prep_bwd_call(
        proj, tabs, p["cq_g"], p["ckv_g"], p["qg"], p["kg"], p["wuq"], p["wkn"], p["wv"], dq, dk, dv)
    dbg, dcg, dxi, dsg_b, dcw, dcb = _conv_bwd_call(proj, p["conv_w"], p["conv_b"], dyb)
    du, dvv, dsg_c, dlg, dlb, dws, dbs = _sg_bwd_call(proj, p["ln_g"], p["ln_b"], p["ws"], p["bs"], dyc)
    dmq, dsg_d, dmk, dmv, dmqg = _mem_bwd_call(proj, sv["mk"], sv["mv"], p["mqg"], dyd)
    dmem_g, dwm, dmkg = _memkv_bwd_call(mem, p["mem_g"], p["wm"], p["mkg"], dmk, dmv)
    grads = dict(cq_norm_g=dcqg[0], ckv_norm_g=dckvg[0], mla_q_norm_g=dqg[0, :QKH], mla_k_norm_g=dkg[0, :QKH],
                 conv_w=dcw, conv_b=dcb[0], sg_ln_g=dlg[0], sg_ln_b=dlb[0], w_spatial=dws, b_spatial=dbs[:, :, 0],
                 mem_norm_g=dmem_g[0], mem_q_norm_g=dmqg[0], mem_k_norm_g=dmkg[0], b_merge=dbm,
                 wuq_heads=dwuq, wkn_heads=dwkn, wv_heads=dwv, w_mem_kv=dwm, w_branch_chips=dwb, w_out=dwo)
    if on_rest_grads is not None:
        on_rest_grads(grads)
    dproj = jnp.concatenate([dml, dsg_a, dsg_b, dsg_c, dsg_d, dbg, dcg, dxi, du, dvv, dmq, dcq, dckv, dkr], axis=1)
    grads["w_in_aligned"] = _dw_call(sv["h"], dproj)
    norm_g = p["norm_g"]
    if on_grads is not None:
        for token in on_grads(grads):
            norm_g = _tie(norm_g, token)
    dx, dnorm_g = _dh_call(dproj, p["w_in"], sv["x"], norm_g, dout)
    grads["norm_g"] = dnorm_g[0]
    return dx, grads


def _chips_to_cols(a):
    return jnp.concatenate([a[j] for j in range(N_CHIPS)], axis=1)


def _cols_to_chips(a):
    cols = a.shape[1] // N_CHIPS
    return jnp.stack([a[:, cols * j:cols * (j + 1)] for j in range(N_CHIPS)])


def _layer_params_first(l, rep, w_in_gathered, conv_w, b_merge):
    pad_g = lambda g: jnp.pad(g, (0, LANES - QKH)).reshape(1, LANES)
    return dict(
        norm_g=rep["norm_g"][l].reshape(1, D), w_in=_realign_call(w_in_gathered),
        cq_g=rep["cq_norm_g"][l].reshape(1, QL), ckv_g=rep["ckv_norm_g"][l].reshape(1, KVL),
        qg=pad_g(rep["mla_q_norm_g"][l]), kg=pad_g(rep["mla_k_norm_g"][l]),
        conv_w=conv_w, conv_b=rep["conv_b"][l].reshape(1, CW),
        ln_g=rep["sg_ln_g"][l].reshape(1, SGW), ln_b=rep["sg_ln_b"][l].reshape(1, SGW),
        ws=rep["w_spatial"][l], bs=rep["b_spatial"][l].reshape(SGG, SGC, 1),
        mem_g=rep["mem_norm_g"][l].reshape(1, D),
        mqg=rep["mem_q_norm_g"][l].reshape(1, MHD), mkg=rep["mem_k_norm_g"][l].reshape(1, MHD), bm=b_merge)


def _layer_params_rest(gathered):
    wkn, wv = _wukv_to_heads(_chips_to_cols(gathered["w_ukv"]))
    return dict(wuq=_wuq_to_heads(_chips_to_cols(gathered["w_uq"])), wkn=wkn, wv=wv,
                wm=gathered["w_mem_kv"].reshape(D, 2 * MH * MHD), wb=gathered["w_branch"], wo=gathered["w_out"].reshape(D, D))


def _layer_params(l, rep, gathered, conv_w, b_merge):
    return dict(_layer_params_first(l, rep, gathered["w_in"], conv_w, b_merge), **_layer_params_rest(gathered))


def _forward_backward(x, mem, pos, target, params, bwd_hooks=None):
    tabs = _rope_tables(pos)
    params = list(params)
    saved = []
    act = x
    for l in range(DEPTH):
        if callable(params[l]):
            params[l] = params[l](saved[-1], act)
        act, sv = _layer_fwd(act, mem, tabs, params[l])
        saved.append(sv)
    dy, sq = _loss_call(act, target)
    grads = [None] * DEPTH
    token = None
    for l in reversed(range(DEPTH)):
        hooks = dict(bwd_hooks[l]) if bwd_hooks else {}
        after_layer = hooks.pop("after_layer", None)
        dy, grads[l] = _layer_bwd(dy, mem, tabs, saved[l]["p"], saved[l], start_after=token, **hooks)
        token = after_layer(dy) if after_layer is not None else None
    return sq, dy, grads


_SHARDED_MM = ("w_in", "w_branch", "w_out", "w_mem_kv", "w_uq", "w_ukv")
_SHARDED_F32 = ("conv_w", "b_merge")
_REPLICATED = ("norm_g", "cq_norm_g", "ckv_norm_g", "mla_q_norm_g", "mla_k_norm_g", "conv_b", "sg_ln_g", "sg_ln_b",
               "w_spatial", "b_spatial", "mem_norm_g", "mem_q_norm_g", "mem_k_norm_g")
_ALL_REDUCED = _REPLICATED + _SHARDED_F32
_WEIGHTS = ("norm_g", "w_in", "cq_norm_g", "ckv_norm_g", "w_uq", "w_ukv", "mla_q_norm_g", "mla_k_norm_g", "conv_w", "conv_b",
            "sg_ln_g", "sg_ln_b", "w_spatial", "b_spatial", "mem_norm_g", "w_mem_kv", "mem_q_norm_g", "mem_k_norm_g",
            "b_merge", "w_branch", "w_out")
_SMALL = tuple(n for n in _WEIGHTS if n not in _SHARDED_MM)


def _gather_small_sharded(w):
    names = _SHARDED_F32
    packed = _pack_rows([w[n] for n in names], F32, 8)
    got = _all_gather8(packed, "gather_small_weights")
    per_chip = [_unpack(got[2 * j].reshape(-1), [w[n].shape for n in names]) for j in range(N_CHIPS)]
    return {n: jnp.concatenate([per_chip[j][t] for j in range(N_CHIPS)], axis=2) for t, n in enumerate(names)}


class _Gather:
    def __init__(self, w, layer, names, after, tag):
        self.names, self.tag = names, tag
        x, y, c = lax.axis_index("x"), lax.axis_index("y"), lax.axis_index("c")
        chip_core = jnp.stack([2 * x + y, c]).astype(jnp.int32)
        halves = [w[n].shape[1] // 2 for n in names]
        self.srcs = [lax.dynamic_slice_in_dim(w[n][layer], c * h, h, axis=0).astype(MM) for n, h in zip(names, halves)]
        k = len(names)
        out = _gather_start_call(self.srcs, _place_own_call(self.srcs, chip_core, tag + "place_own"), after, tag + "start")
        self.bufs, self.send, self.recv_sib, self.recv_ici, self.token = out[:k], out[k], out[k + 1], out[k + 2], out[k + 3]

    def pass_on(self, after):
        k = len(self.names)
        out = _gather_forward_call(self.bufs, self.recv_ici, after, self.tag + "forward")
        self.bufs, self.send_fwd, self.recv_fwd = out[:k], out[k], out[k + 1]
        return out[k + 2]

    def finish(self, after):
        got = _gather_finish_call(self.srcs, self.bufs, self.send, self.recv_sib, self.send_fwd, self.recv_fwd, after,
                                  self.tag + "finish")
        return dict(zip(self.names, got))


class _ReduceScatter:
    SLABS = dict(
        w_in=lambda g: _unalign_call(g["w_in_aligned"]),
        w_branch=lambda g: g["w_branch_chips"].reshape(N_CHIPS, NB * BW, D // N_CHIPS),
        w_out=lambda g: g["w_out"].reshape(N_CHIPS, D // N_CHIPS, D),
        w_mem_kv=lambda g: g["w_mem_kv"].reshape(N_CHIPS, D // N_CHIPS, 2 * MH * MHD),
        w_uq=lambda g: _cols_to_chips(_wuq_from_heads(g["wuq_heads"])),
        w_ukv=lambda g: _cols_to_chips(_wukv_from_heads(g["wkn_heads"], g["wv_heads"])))

    def __init__(self, tag, names):
        self.tag, self.names = tag, names

    def exchange(self, grads):
        self.tensors = [self.SLABS[n](grads) for n in self.names]
        n = len(self.tensors)
        out = _pair_exchange_start_call(self.tensors, self.tag + "exchange_start")
        self.ex_bufs, self.ex_send, self.ex_recv = out[:n], out[n], out[n + 1]
        return out[n + 2]

    def scatter(self, after):
        n = len(self.tensors)
        c = lax.axis_index("c")
        from_sibling = _pair_exchange_finish_call(self.tensors, self.ex_bufs, self.ex_send, self.ex_recv, after,
                                                  self.tag + "exchange_finish")
        self.chip_sums = _pair_sum_call(self.tensors, from_sibling, c.astype(jnp.int32).reshape(1), self.tag + "pair_sum")
        out = _chip_scatter_start_call(self.chip_sums, self.tag + "scatter_start")
        self.bufs, self.send_sems, self.recv_sems, self.token = out[:n], out[n], out[n + 1], out[n + 2]
        return self.token

    def finish(self, after):
        x, y, c = lax.axis_index("x"), lax.axis_index("y"), lax.axis_index("c")
        chip_core = jnp.stack([2 * x + y, c]).astype(jnp.int32)
        from_chips = _chip_scatter_finish_call(self.chip_sums, self.bufs, self.send_sems, self.recv_sems, after,
                                               self.tag + "scatter_finish")
        mine = _owner_sum_call(self.chip_sums, from_chips, chip_core, self.tag + "owner_sum")
        return dict(zip(self.names, _pair_gather_call(mine, self.tag + "pair_gather")))


def _all_reduce_small(g, sq):
    packed = _pack_rows([g[n] for n in _ALL_REDUCED] + [sq], F32, 64)
    got = _all_gather8(packed, "gather_small_grads")
    total = _sum8_call(got).reshape(-1)
    parts = _unpack(total, [g[n].shape for n in _ALL_REDUCED] + [sq.shape])
    out = dict(zip(_ALL_REDUCED, parts))
    sq_total = parts[-1]
    chip = 2 * lax.axis_index("x") + lax.axis_index("y")
    for n in _SHARDED_F32:
        size = out[n].shape[2] // N_CHIPS
        out[n] = lax.dynamic_slice_in_dim(out[n], chip * size, size, axis=2)
    return out, sq_total


def _adamw_small(w, g, m, v, token):
    delta, new_m, new_v = {}, {}, {}
    shapes = [w[n].shape for n in _SMALL]
    pk = lambda t: _pack_rows([t[n] for n in _SMALL], F32, 64)
    d, nm, nv = _adamw_call(pk(w), _tie(pk(g), token), pk(m), pk(v), "adamw_small")
    for out, packed in ((delta, d), (new_m, nm), (new_v, nv)):
        out.update(zip(_SMALL, _unpack(packed.reshape(-1), shapes)))
    return delta, new_m, new_v


def kernel(x, mem, positions, norm_g, w_in, cq_norm_g, ckv_norm_g, w_uq, w_ukv, mla_q_norm_g, mla_k_norm_g, conv_w, conv_b, sg_ln_g, sg_ln_b, w_spatial, b_spatial, mem_norm_g, w_mem_kv, mem_q_norm_g, mem_k_norm_g, b_merge, w_branch, w_out, loss_target, m_norm_g, m_w_in, m_cq_norm_g, m_ckv_norm_g, m_w_uq, m_w_ukv, m_mla_q_norm_g, m_mla_k_norm_g, m_conv_w, m_conv_b, m_sg_ln_g, m_sg_ln_b, m_w_spatial, m_b_spatial, m_mem_norm_g, m_w_mem_kv, m_mem_q_norm_g, m_mem_k_norm_g, m_b_merge, m_w_branch, m_w_out, v_norm_g, v_w_in, v_cq_norm_g, v_ckv_norm_g, v_w_uq, v_w_ukv, v_mla_q_norm_g, v_mla_k_norm_g, v_conv_w, v_conv_b, v_sg_ln_g, v_sg_ln_b, v_w_spatial, v_b_spatial, v_mem_norm_g, v_w_mem_kv, v_mem_q_norm_g, v_mem_k_norm_g, v_b_merge, v_w_branch, v_w_out):
    w = dict(norm_g=norm_g, w_in=w_in, cq_norm_g=cq_norm_g, ckv_norm_g=ckv_norm_g, w_uq=w_uq, w_ukv=w_ukv,
             mla_q_norm_g=mla_q_norm_g, mla_k_norm_g=mla_k_norm_g, conv_w=conv_w, conv_b=conv_b, sg_ln_g=sg_ln_g,
             sg_ln_b=sg_ln_b, w_spatial=w_spatial, b_spatial=b_spatial, mem_norm_g=mem_norm_g, w_mem_kv=w_mem_kv,
             mem_q_norm_g=mem_q_norm_g, mem_k_norm_g=mem_k_norm_g, b_merge=b_merge, w_branch=w_branch, w_out=w_out)
    m = dict(norm_g=m_norm_g, w_in=m_w_in, cq_norm_g=m_cq_norm_g, ckv_norm_g=m_ckv_norm_g, w_uq=m_w_uq, w_ukv=m_w_ukv,
             mla_q_norm_g=m_mla_q_norm_g, mla_k_norm_g=m_mla_k_norm_g, conv_w=m_conv_w, conv_b=m_conv_b, sg_ln_g=m_sg_ln_g,
             sg_ln_b=m_sg_ln_b, w_spatial=m_w_spatial, b_spatial=m_b_spatial, mem_norm_g=m_mem_norm_g, w_mem_kv=m_w_mem_kv,
             mem_q_norm_g=m_mem_q_norm_g, mem_k_norm_g=m_mem_k_norm_g, b_merge=m_b_merge, w_branch=m_w_branch, w_out=m_w_out)
    v = dict(norm_g=v_norm_g, w_in=v_w_in, cq_norm_g=v_cq_norm_g, ckv_norm_g=v_ckv_norm_g, w_uq=v_w_uq, w_ukv=v_w_ukv,
             mla_q_norm_g=v_mla_q_norm_g, mla_k_norm_g=v_mla_k_norm_g, conv_w=v_conv_w, conv_b=v_conv_b, sg_ln_g=v_sg_ln_g,
             sg_ln_b=v_sg_ln_b, w_spatial=v_w_spatial, b_spatial=v_b_spatial, mem_norm_g=v_mem_norm_g, w_mem_kv=v_w_mem_kv,
             mem_q_norm_g=v_mem_q_norm_g, mem_k_norm_g=v_mem_k_norm_g, b_merge=v_b_merge, w_branch=v_w_branch, w_out=v_w_out)

    chip_core = jnp.stack([2 * lax.axis_index("x") + lax.axis_index("y"), lax.axis_index("c")]).astype(jnp.int32)

    first = _Gather(w, 0, ("w_in",), chip_core, "gather_l0_w_in_")
    rest = _Gather(w, 0, _SHARDED_MM[1:], first.token, "gather_l0_rest_")
    later = _Gather(w, 1, _SHARDED_MM, rest.token, "gather_l1_")
    small = _gather_small_sharded(w)
    w_in0 = first.finish(first.pass_on(later.token))["w_in"]

    def rest_of_layer0(proj0):
        return _layer_params_rest(rest.finish(rest.pass_on(proj0)))

    def layer1_params(saved0, act0):
        return _layer_params(1, w, later.finish(act0), small["conv_w"][1], small["b_merge"][1])

    params0 = _layer_params_first(0, w, w_in0, small["conv_w"][0], small["b_merge"][0])
    params = [dict(params0, late=rest_of_layer0, after_attn=later.pass_on), layer1_params]
    others = _SHARDED_MM[1:]
    rs1 = _ReduceScatter("rs_l1_", _SHARDED_MM)
    rs0_rest, rs0_w_in = _ReduceScatter("rs_l0_rest_", others), _ReduceScatter("rs_l0_w_in_", ("w_in",))

    def layer0_grads_done(grads):
        return [rs0_rest.scatter([grads["w_in_aligned"]]), rs0_w_in.exchange(grads)]

    hooks = [dict(on_rest_grads=rs0_rest.exchange, on_grads=layer0_grads_done),
             dict(on_grads=lambda grads: [rs1.exchange(grads)], after_layer=lambda dy: rs1.scatter([dy]))]
    sq, grad_x, layer_grads = _forward_backward(x[0], mem[0], positions[0], loss_target[0], params, hooks)

    g, sq_total = _all_reduce_small({n: jnp.stack([layer_grads[l][n] for l in range(DEPTH)]) for n in _ALL_REDUCED}, sq)
    loss = 0.5 / D * jnp.sum(sq_total)
    scattering = rs0_w_in.scatter([grad_x, g["norm_g"]])
    delta, new_m, new_v = _adamw_small(w, g, m, v, scattering)
    shard_grads = {1: rs1.finish([scattering]), 0: rs0_rest.finish([scattering])}
    as3d = lambda a: a.reshape(DEPTH, -1, a.shape[-1])
    as2d = lambda a: a.reshape(-1, a.shape[-1])
    big = lambda t: [as3d(t[n]) for n in others]
    turned = lambda t: [jnp.swapaxes(t["w_in"], 1, 2)]
    assert W_IN_SHARD % (8 * 7) == 0

    def update_w_in(l, grad, prev):
        return _adamw_layer_call(l, turned(w), [grad.T], turned(m), turned(v), prev, [], "adamw_w_in_l%d" % l, steps=7)

    def update_others(l, prev):
        return _adamw_layer_call(l, big(w), [as2d(shard_grads[l][n]) for n in others], big(m), big(v), prev, [], "adamw_l%d" % l)

    upd_in1 = update_w_in(1, shard_grads[1]["w_in"], None)
    upd = update_others(0, update_others(1, None))
    w_in_grad0 = rs0_w_in.finish([grad_x, upd_in1[0], upd[0], delta["norm_g"]])["w_in"]
    upd_in = update_w_in(0, w_in_grad0, upd_in1)
    g["w_in"], delta["w_in"], new_m["w_in"], new_v["w_in"] = [jnp.swapaxes(a, 1, 2) for a in upd_in]
    for t, n in enumerate(others):
        g[n], delta[n], new_m[n], new_v[n] = [a.reshape(w[n].shape) for a in upd[4 * t:4 * t + 4]]
    return (loss, grad_x[None], *[g[n] for n in _WEIGHTS], *[delta[n] for n in _WEIGHTS],
            *[new_m[n] for n in _WEIGHTS], *[new_v[n] for n in _WEIGHTS])
```

```python
import functools
import math

import jax
import jax.numpy as jnp
from jax import lax
from jax.experimental import pallas as pl
from jax.experimental.pallas import tpu as pltpu

F32 = jnp.float32
MM = jnp.bfloat16

D = 1024
DEPTH = 2
EPS = 1e-6
H = 8
NOPE = 64
ROPE = 32
QKH = 96
VH = 64
QL = 256
KVL = 128
ROPE_THETA = 10000.0
CW = 512
SGW = 512
SGG = 4
SGC = 128
MH = 4
MHD = 128
NB = 4
BW = 512
NEG_INF = -1e30
LANES = 128
N_CHIPS = 4

R_CQ, R_CKV, R_KR, R_CV, R_SGI, R_MQ, R_SG, R_ML, R_END = 0, 256, 384, 416, 1952, 2976, 3488, 5536, 9632
OFF_ML, OFF_SG, OFF_CV, OFF_SGI, OFF_MQ, OFF_CQ, OFF_CKV, OFF_KR, NP = 0, 4096, 6144, 7680, 8704, 9216, 9472, 9600, 9728

ADAM_LR = 0.001
ADAM_B1 = 0.9
ADAM_B2 = 0.999
ADAM_EPS = 1e-08
ADAM_WD = 0.01
ADAM_STEP = 10

VMEM_LIMIT = 56 * 1024 * 1024
PACK_W = 512
MESH_ID = pl.DeviceIdType.MESH


def _cparams(n_axes):
    return pltpu.CompilerParams(dimension_semantics=("arbitrary",) * n_axes, vmem_limit_bytes=VMEM_LIMIT)


def _bs(shape, imap):
    return pl.BlockSpec(shape, imap)


@jax.custom_vjp
def _mm_plain(a, b):
    return jnp.dot(a.astype(MM), b.astype(MM), preferred_element_type=F32)


def _mm_plain_fwd(a, b):
    return _mm_plain(a, b), (a, b)


def _mm_plain_bwd(res, g):
    a, b = res
    gm = g.astype(MM)
    da = lax.dot_general(gm, b.astype(MM), (((1,), (1,)), ((), ())), preferred_element_type=F32)
    db = lax.dot_general(a.astype(MM), gm, (((0,), (0,)), ((), ())), preferred_element_type=F32)
    return da.astype(a.dtype), db.astype(b.dtype)


_mm_plain.defvjp(_mm_plain_fwd, _mm_plain_bwd)


@jax.custom_vjp
def _mm_slot(a, w, slot):
    return jnp.dot(a.astype(MM), w.astype(MM), preferred_element_type=F32)


def _mm_slot_fwd(a, w, slot):
    return _mm_slot(a, w, slot), (a, w)


def _mm_slot_bwd(res, g):
    a, w = res
    gm = g.astype(MM)
    da = lax.dot_general(gm, w.astype(MM), (((1,), (1,)), ((), ())), preferred_element_type=F32)
    dw = lax.dot_general(a.astype(MM), gm, (((0,), (0,)), ((), ())), preferred_element_type=F32)
    return da.astype(a.dtype), jnp.zeros_like(w), dw


_mm_slot.defvjp(_mm_slot_fwd, _mm_slot_bwd)


def _mm(a, b):
    if isinstance(b, tuple):
        return _mm_slot(a, b[0], b[1])
    return _mm_plain(a, b)


def _with_slot(w):
    return (w, jnp.zeros(w.shape, F32))


@jax.custom_vjp
def _mm_nt(a, b):
    return lax.dot_general(a.astype(MM), b.astype(MM), (((1,), (1,)), ((), ())), preferred_element_type=F32)


def _mm_nt_fwd(a, b):
    return _mm_nt(a, b), (a, b)


def _mm_nt_bwd(res, g):
    a, b = res
    gm = g.astype(MM)
    da = jnp.dot(gm, b.astype(MM), preferred_element_type=F32)
    db = lax.dot_general(gm, a.astype(MM), (((0,), (0,)), ((), ())), preferred_element_type=F32)
    return da.astype(a.dtype), db.astype(b.dtype)


_mm_nt.defvjp(_mm_nt_fwd, _mm_nt_bwd)


@functools.partial(jax.custom_vjp, nondiff_argnums=(1,))
def _lane_roll(x, shift):
    return pltpu.roll(x, shift, 1)


def _lane_roll_fwd(x, shift):
    return pltpu.roll(x, shift, 1), None


def _lane_roll_bwd(shift, _, g):
    return (pltpu.roll(g, (LANES - shift) % LANES, 1),)


_lane_roll.defvjp(_lane_roll_fwd, _lane_roll_bwd)


def _rms_n(x, g, n):
    ms = jnp.sum(x * x, axis=-1, keepdims=True) * (1.0 / n)
    return x * lax.rsqrt(ms + EPS) * g


def _softmax(s):
    m = jnp.max(s, axis=-1, keepdims=True)
    e = jnp.exp(s - m)
    return e / jnp.sum(e, axis=-1, keepdims=True)


def _rope(t, cos_t, sin_a, sin_b):
    return t * cos_t + _lane_roll(t, LANES - 16) * sin_a + _lane_roll(t, 16) * sin_b


def _mla_prep_fn(cq, ckv, kr, cos_t, sin_a, sin_b, cq_g, ckv_g, qg, kg, wuq, wkn, wv):
    cqn = _rms_n(cq, cq_g, QL)
    ckvn = _rms_n(ckv, ckv_g, KVL)
    lane = lax.broadcasted_iota(jnp.int32, kr.shape, 1)
    krm = jnp.where((lane >= NOPE) & (lane < QKH), kr, 0.0)
    qs, ks = [], []
    for h in range(H):
        qh = _rms_n(_mm(cqn, wuq[h]), qg, QKH)
        qs.append(_rope(qh, cos_t, sin_a, sin_b))
        kh = _rms_n(_mm(ckvn, wkn[h]) + krm, kg, QKH)
        ks.append(_rope(kh, cos_t, sin_a, sin_b))
    return jnp.concatenate(qs, axis=-1), jnp.concatenate(ks, axis=-1), _mm(ckvn, wv)


def _dot_nt(a, b):
    return lax.dot_general(a.astype(MM), b.astype(MM), (((1,), (1,)), ((), ())), preferred_element_type=F32)


def _dot_tn(a, b):
    return lax.dot_general(a.astype(MM), b.astype(MM), (((0,), (0,)), ((), ())), preferred_element_type=F32)


def _causal_scores(qe, ke):
    tq, kl = qe.shape[0], ke.shape[0]
    s = _dot_nt(qe, ke) * (QKH ** -0.5)
    rows = lax.broadcasted_iota(jnp.int32, (tq, tq), 0)
    cols = lax.broadcasted_iota(jnp.int32, (tq, tq), 1)
    own = jnp.where(cols <= rows, s[:, kl - tq:], NEG_INF)
    return own if kl == tq else jnp.concatenate([s[:, :kl - tq], own], axis=1)


def _head_lanes(e, shape):
    lane = lax.broadcasted_iota(jnp.int32, shape, len(shape) - 1)
    return (lane >= VH * e) & (lane < VH * (e + 1))


def _attn_pair_fwd(q2, k2, v2):
    tq = q2.shape[0]
    o = jnp.zeros((tq, LANES), F32)
    lse = jnp.zeros((tq, LANES), F32)
    for e in range(2):
        sl = slice(LANES * e, LANES * (e + 1))
        s = _causal_scores(q2[:, sl], k2[:, sl])
        m = jnp.max(s, axis=-1, keepdims=True)
        ex = jnp.exp(s - m)
        l = jnp.sum(ex, axis=-1, keepdims=True)
        ve = jnp.where(_head_lanes(e, v2[:, sl].shape), v2[:, sl], 0.0)
        o = o + jnp.dot((ex * (1.0 / l)).astype(MM), ve.astype(MM), preferred_element_type=F32)
        lse = jnp.where(_head_lanes(e, lse.shape), m + jnp.log(l), lse)
    return o, lse


def _attn_pair_bwd(q2, k2, v2, sg, dys, o, lse):
    sig = jax.nn.sigmoid(sg)
    do = dys * (sg * sig)
    dsg = dys * o * (sig * (1.0 + sg * (1.0 - sig)))
    dqs, dks, dvs = [], [], []
    for e in range(2):
        sl = slice(LANES * e, LANES * (e + 1))
        qe, ke = q2[:, sl], k2[:, sl]
        hm = _head_lanes(e, o.shape)
        lse_e = jnp.max(jnp.where(hm, lse, NEG_INF), axis=-1, keepdims=True)
        do_e = jnp.where(hm, do, 0.0)
        delta = jnp.sum(do_e * o, axis=-1, keepdims=True)
        p = jnp.exp(_causal_scores(qe, ke) - lse_e)
        ve = jnp.where(_head_lanes(e, v2[:, sl].shape), v2[:, sl], 0.0)
        dvs.append(_dot_tn(p, do_e))
        ds = (p * (_dot_nt(do_e, ve) - delta)) * (QKH ** -0.5)
        dqs.append(jnp.dot(ds.astype(MM), ke.astype(MM), preferred_element_type=F32))
        dks.append(_dot_tn(ds, qe))
    return jnp.concatenate(dqs, axis=-1), jnp.concatenate(dks, axis=-1), jnp.concatenate(dvs, axis=-1), dsg


def _sg_fn(u, v, sgc, ln_g, ln_b, ws, bs):
    mu = jnp.mean(v, axis=-1, keepdims=True)
    xc = v - mu
    vn = xc * lax.rsqrt(jnp.mean(xc * xc, axis=-1, keepdims=True) + EPS) * ln_g + ln_b
    r = lax.broadcasted_iota(jnp.int32, (SGC, SGC), 0)
    c = lax.broadcasted_iota(jnp.int32, (SGC, SGC), 1)
    wt = [jnp.where(r >= c, w, 0.0) for w in ws]
    row_blocks = []
    for ch in range(u.shape[0] // SGC):
        col_blocks = []
        for g in range(SGG):
            blk = vn[SGC * ch:SGC * (ch + 1), LANES * g:LANES * (g + 1)]
            col_blocks.append(_mm(wt[g], blk) + bs[g])
        row_blocks.append(jnp.concatenate(col_blocks, axis=-1))
    mixed = jnp.concatenate(row_blocks, axis=0)
    return (u * mixed) * jax.nn.silu(sgc)


def _memkv_fn(mem, mem_g, wm, kg):
    kv = _mm(_rms_n(mem, mem_g, D), wm)
    ks = [_rms_n(kv[:, MHD * h:MHD * (h + 1)], kg, MHD) for h in range(MH)]
    return jnp.concatenate(ks, axis=-1), kv[:, MH * MHD:]


def _mem_fn(mq, sgd, k, v, qg):
    outs = []
    for h in range(MH):
        sl = slice(MHD * h, MHD * (h + 1))
        qh = _rms_n(mq[:, sl], qg, MHD)
        p = _softmax(_mm_nt(qh, k[:, sl]) * (MHD ** -0.5))
        outs.append(_mm(p, v[:, sl]))
    return jnp.concatenate(outs, axis=-1) * jax.nn.silu(sgd)


def _merge_fn(ys, logits, bm, wb, wo):
    merged = None
    for n in range(NB):
        z = jnp.concatenate([_mm(ys[n], wb[j][n]) for j in range(N_CHIPS)], axis=-1)
        gate = jax.nn.sigmoid(logits[:, D * n:D * (n + 1)] + bm[n])
        merged = gate * z if merged is None else merged + gate * z
    return _mm(merged, wo)


def _proj_call(x, g, w):
    s_len = x.shape[0]
    tm, tn = min(s_len, 1024), NP // 4

    def body(x_ref, g_ref, w_ref, p_ref, h_ref):
        @pl.when(pl.program_id(1) == 0)
        def _():
            h_ref[...] = _rms_n(x_ref[...], g_ref[...], D).astype(h_ref.dtype)
        p_ref[...] = jnp.dot(h_ref[...], w_ref[...], preferred_element_type=F32)

    return pl.pallas_call(
        body, grid=(s_len // tm, NP // tn),
        in_specs=[_bs((tm, D), lambda i, j: (i, 0)), _bs((1, D), lambda i, j: (0, 0)), _bs((D, tn), lambda i, j: (0, j))],
        out_specs=[_bs((tm, tn), lambda i, j: (i, j)), _bs((tm, D), lambda i, j: (i, 0))],
        out_shape=[jax.ShapeDtypeStruct((s_len, NP), F32), jax.ShapeDtypeStruct((s_len, D), MM)],
        name="proj", compiler_params=_cparams(2))(x, g, w)


def _rope_tables(pos):
    half = ROPE // 2
    inv_freq = ROPE_THETA ** (-jnp.arange(half, dtype=F32) / half)
    ang = pos.astype(F32)[:, None] * inv_freq
    cos, sin = jnp.cos(ang), jnp.sin(ang)
    s_len = pos.shape[0]
    z = lambda n: jnp.zeros((s_len, n), F32)
    cos_t = jnp.concatenate([jnp.ones((s_len, NOPE), F32), cos, cos, z(LANES - QKH)], axis=1)
    sin_a = jnp.concatenate([z(NOPE), -sin, z(LANES - NOPE - half)], axis=1)
    sin_b = jnp.concatenate([z(NOPE + half), sin, z(LANES - QKH)], axis=1)
    return cos_t, sin_a, sin_b


def _mla_prep_specs(tm):
    row = lambda w, off: _bs((tm, w), lambda i: (i, off // w))
    full2 = lambda a, b: _bs((a, b), lambda i: (0, 0))
    full3 = lambda a, b, c: _bs((a, b, c), lambda i: (0, 0, 0))
    tab = _bs((tm, LANES), lambda i: (i, 0))
    return [row(QL, OFF_CQ), row(KVL, OFF_CKV), row(LANES, OFF_KR), tab, tab, tab,
            full2(1, QL), full2(1, KVL), full2(1, LANES), full2(1, LANES),
            full3(H, QL, LANES), full3(H, KVL, LANES), full2(KVL, H * LANES)]


def _mla_prep_args(body_refs, wrap=lambda w: w):
    (cq, ckv, kr, ct, sa, sb, cqg, ckvg, qg, kg, wuq, wkn, wv) = body_refs
    return (cq[...], ckv[...], kr[...], ct[...], sa[...], sb[...], cqg[...], ckvg[...], qg[...], kg[...],
            [wrap(wuq[h]) for h in range(H)], [wrap(wkn[h]) for h in range(H)], wrap(wv[...]))


def _mla_prep_call(proj, tabs, cq_g, ckv_g, qg, kg, wuq, wkn, wv):
    s_len = proj.shape[0]
    tm = min(s_len, 256)

    def body(*refs):
        q_ref, k_ref, v_ref = refs[13:]
        q, k, v = _mla_prep_fn(*_mla_prep_args(refs[:13]))
        q_ref[...] = q.astype(q_ref.dtype)
        k_ref[...] = k.astype(k_ref.dtype)
        v_ref[...] = v.astype(v_ref.dtype)

    out = _bs((tm, H * LANES), lambda i: (i, 0))
    return pl.pallas_call(
        body, grid=(s_len // tm,), in_specs=_mla_prep_specs(tm), out_specs=[out, out, out],
        out_shape=[jax.ShapeDtypeStruct((s_len, H * LANES), MM)] * 3,
        name="mla_prep", compiler_params=_cparams(1))(proj, proj, proj, *tabs, cq_g, ckv_g, qg, kg, wuq, wkn, wv)


def _mla_prep_bwd_call(proj, tabs, cq_g, ckv_g, qg, kg, wuq, wkn, wv, dq, dk, dv):
    s_len = proj.shape[0]
    tm = min(s_len, 256)

    def body(*refs):
        dq_ref, dk_ref, dv_ref = refs[13:16]
        dcq_ref, dckv_ref, dkr_ref, dcqg_ref, dckvg_ref, dqg_ref, dkg_ref, dwuq_ref, dwkn_ref, dwv_ref = refs[16:]
        _, vjp = jax.vjp(_mla_prep_fn, *_mla_prep_args(refs[:13], _with_slot))
        (dcq, dckv, dkr, _, _, _, dcqg, dckvg, dqg, dkg, dwuq, dwkn, dwv) = vjp((dq_ref[...], dk_ref[...], dv_ref[...]))
        dwuq, dwkn, dwv = [d[1] for d in dwuq], [d[1] for d in dwkn], dwv[1]
        dcq_ref[...] = dcq.astype(dcq_ref.dtype)
        dckv_ref[...] = dckv.astype(dckv_ref.dtype)
        dkr_ref[...] = dkr.astype(dkr_ref.dtype)

        @pl.when(pl.program_id(0) == 0)
        def _():
            for r in (dcqg_ref, dckvg_ref, dqg_ref, dkg_ref, dwuq_ref, dwkn_ref, dwv_ref):
                r[...] = jnp.zeros_like(r)
        dcqg_ref[...] += dcqg
        dckvg_ref[...] += dckvg
        dqg_ref[...] += dqg
        dkg_ref[...] += dkg
        for h in range(H):
            dwuq_ref[h] += dwuq[h]
            dwkn_ref[h] += dwkn[h]
        dwv_ref[...] += dwv

    big = _bs((tm, H * LANES), lambda i: (i, 0))
    row = lambda w: _bs((tm, w), lambda i: (i, 0))
    full2 = lambda a, b: _bs((a, b), lambda i: (0, 0))
    full3 = lambda a, b, c: _bs((a, b, c), lambda i: (0, 0, 0))
    sd = jax.ShapeDtypeStruct
    return pl.pallas_call(
        body, grid=(s_len // tm,), in_specs=_mla_prep_specs(tm) + [big, big, big],
        out_specs=[row(QL), row(KVL), row(LANES), full2(1, QL), full2(1, KVL), full2(1, LANES), full2(1, LANES),
                   full3(H, QL, LANES), full3(H, KVL, LANES), full2(KVL, H * LANES)],
        out_shape=[sd((s_len, QL), MM), sd((s_len, KVL), MM), sd((s_len, LANES), MM), sd((1, QL), F32), sd((1, KVL), F32),
                   sd((1, LANES), F32), sd((1, LANES), F32), sd((H, QL, LANES), F32), sd((H, KVL, LANES), F32),
                   sd((KVL, H * LANES), F32)],
        name="mla_prep_bwd", compiler_params=_cparams(1))(proj, proj, proj, *tabs, cq_g, ckv_g, qg, kg, wuq, wkn, wv, dq, dk, dv)


def _attn_specs(s_len, tq):
    pair = 2 * LANES
    return [_bs((tq, pair), lambda p, i: (i, p)), _bs((s_len, pair), lambda p, i: (0, p)), _bs((s_len, pair), lambda p, i: (0, p)),
            _bs((tq, LANES), lambda p, i: (i, OFF_SG // LANES + p))]


def _attn_call(q, k, v, proj):
    s_len = q.shape[0]
    tq = min(s_len, 256)

    def body(q_ref, k_ref, v_ref, sg_ref, y_ref, o_ref, lse_ref):
        for n in range(s_len // tq):
            @pl.when(pl.program_id(1) == n)
            def _():
                kl = (n + 1) * tq
                o, lse = _attn_pair_fwd(q_ref[...], k_ref[:kl, :], v_ref[:kl, :])
                y_ref[...] = (o * jax.nn.silu(sg_ref[...])).astype(y_ref.dtype)
                o_ref[...] = o
                lse_ref[...] = lse

    tile = _bs((tq, LANES), lambda p, i: (i, p))
    sd = jax.ShapeDtypeStruct
    return pl.pallas_call(
        body, grid=(H // 2, s_len // tq), in_specs=_attn_specs(s_len, tq), out_specs=[tile, tile, tile],
        out_shape=[sd((s_len, BW), MM), sd((s_len, BW), F32), sd((s_len, BW), F32)],
        name="attn", compiler_params=_cparams(2))(q, k, v, proj)


def _attn_bwd_call(q, k, v, proj, dys, o, lse):
    s_len = q.shape[0]
    tq = min(s_len, 256)
    pair = 2 * LANES

    def body(q_ref, k_ref, v_ref, sg_ref, dy_ref, o_ref, lse_ref, dq_ref, dk_ref, dv_ref, dsg_ref):
        i = pl.program_id(1)

        @pl.when(i == 0)
        def _():
            dk_ref[...] = jnp.zeros_like(dk_ref)
            dv_ref[...] = jnp.zeros_like(dv_ref)

        for n in range(s_len // tq):
            @pl.when(i == n)
            def _():
                kl = (n + 1) * tq
                dq, dk, dv, dsg = _attn_pair_bwd(q_ref[...], k_ref[:kl, :], v_ref[:kl, :], sg_ref[...], dy_ref[...],
                                                 o_ref[...], lse_ref[...])
                dq_ref[...] = dq
                dsg_ref[...] = dsg.astype(dsg_ref.dtype)
                dk_ref[:kl, :] += dk
                dv_ref[:kl, :] += dv

    sd = jax.ShapeDtypeStruct
    tile = _bs((tq, LANES), lambda p, i: (i, p))
    return pl.pallas_call(
        body, grid=(H // 2, s_len // tq),
        in_specs=_attn_specs(s_len, tq) + [tile, tile, tile],
        out_specs=[_bs((tq, pair), lambda p, i: (i, p)), _bs((s_len, pair), lambda p, i: (0, p)),
                   _bs((s_len, pair), lambda p, i: (0, p)), tile],
        out_shape=[sd((s_len, H * LANES), F32), sd((s_len, H * LANES), F32), sd((s_len, H * LANES), F32), sd((s_len, BW), MM)],
        name="attn_bwd", compiler_params=_cparams(2))(q, k, v, proj, dys, o, lse)


def _shift_down(a, n):
    r = lax.broadcasted_iota(jnp.int32, a.shape, 0)
    return jnp.where(r >= n, pltpu.roll(a, n, 0), 0.0)


def _shift_up(a, n):
    s_len = a.shape[0]
    r = lax.broadcasted_iota(jnp.int32, a.shape, 0)
    return jnp.where(r < s_len - n, pltpu.roll(a, s_len - n, 0), 0.0)


def _conv_specs(s_len):
    col = lambda off: _bs((s_len, LANES), lambda j: (0, off // LANES + j))
    return [col(OFF_CV), col(OFF_CV + CW), col(OFF_CV + 2 * CW), col(OFF_SG + BW),
            _bs((3, LANES), lambda j: (0, j)), _bs((1, LANES), lambda j: (0, j))]


def _conv_call(proj, cw, cb):
    s_len = proj.shape[0]

    def body(bg_ref, cg_ref, xi_ref, sg_ref, w_ref, b_ref, y_ref):
        z = cg_ref[...] * xi_ref[...]
        y = b_ref[...] + w_ref[0:1, :] * _shift_down(z, 2)
        y = y + w_ref[1:2, :] * _shift_down(z, 1)
        y = y + w_ref[2:3, :] * z
        y_ref[...] = ((bg_ref[...] * y) * jax.nn.silu(sg_ref[...])).astype(y_ref.dtype)

    return pl.pallas_call(
        body, grid=(CW // LANES,), in_specs=_conv_specs(s_len), out_specs=_bs((s_len, LANES), lambda j: (0, j)),
        out_shape=jax.ShapeDtypeStruct((s_len, CW), MM), name="conv", compiler_params=_cparams(1))(proj, proj, proj, proj, cw, cb)


def _conv_bwd_call(proj, cw, cb, dys):
    s_len = proj.shape[0]

    def body(bg_ref, cg_ref, xi_ref, sg_ref, w_ref, b_ref, dys_ref, dbg_ref, dcg_ref, dxi_ref, dsg_ref, dw_ref, db_ref):
        bg, cg, xi, sg = bg_ref[...], cg_ref[...], xi_ref[...], sg_ref[...]
        w0, w1, w2 = w_ref[0:1, :], w_ref[1:2, :], w_ref[2:3, :]
        z = cg * xi
        z1, z2 = _shift_down(z, 1), _shift_down(z, 2)
        y = b_ref[...] + w0 * z2
        y = y + w1 * z1
        y = y + w2 * z
        yb = bg * y
        sig = jax.nn.sigmoid(sg)
        silu = sg * sig
        dys_v = dys_ref[...]
        dsg_ref[...] = (dys_v * yb * (sig * (1.0 + sg * (1.0 - sig)))).astype(dsg_ref.dtype)
        dyb = dys_v * silu
        dbg_ref[...] = (dyb * y).astype(dbg_ref.dtype)
        dy = dyb * bg
        db_ref[...] = jnp.sum(dy, axis=0, keepdims=True)
        dw_ref[0:1, :] = jnp.sum(dy * z2, axis=0, keepdims=True)
        dw_ref[1:2, :] = jnp.sum(dy * z1, axis=0, keepdims=True)
        dw_ref[2:3, :] = jnp.sum(dy * z, axis=0, keepdims=True)
        dz = w2 * dy + w1 * _shift_up(dy, 1) + w0 * _shift_up(dy, 2)
        dcg_ref[...] = (dz * xi).astype(dcg_ref.dtype)
        dxi_ref[...] = (dz * cg).astype(dxi_ref.dtype)

    col = _bs((s_len, LANES), lambda j: (0, j))
    sd = jax.ShapeDtypeStruct
    return pl.pallas_call(
        body, grid=(CW // LANES,), in_specs=_conv_specs(s_len) + [col],
        out_specs=[col, col, col, col, _bs((3, LANES), lambda j: (0, j)), _bs((1, LANES), lambda j: (0, j))],
        out_shape=[sd((s_len, CW), MM)] * 4 + [sd((3, CW), F32), sd((1, CW), F32)],
        name="conv_bwd", compiler_params=_cparams(1))(proj, proj, proj, proj, cw, cb, dys)


def _sg_specs(tm):
    row = lambda off: _bs((tm, SGW), lambda i: (i, off // SGW))
    return [row(OFF_SGI), row(OFF_SGI + SGW), row(OFF_SG + 2 * BW), _bs((1, SGW), lambda i: (0, 0)), _bs((1, SGW), lambda i: (0, 0)),
            _bs((SGG, SGC, SGC), lambda i: (0, 0, 0)), _bs((SGG, SGC, 1), lambda i: (0, 0, 0))]


def _sg_args(refs):
    u, v, sg, lg, lb, ws, bs = refs
    return (u[...], v[...], sg[...], lg[...], lb[...], [ws[g] for g in range(SGG)], [bs[g] for g in range(SGG)])


def _sg_call(proj, ln_g, ln_b, ws, bs):
    s_len = proj.shape[0]
    tm = min(s_len, 256)

    def body(*refs):
        refs[7][...] = _sg_fn(*_sg_args(refs[:7])).astype(refs[7].dtype)

    return pl.pallas_call(
        body, grid=(s_len // tm,), in_specs=_sg_specs(tm), out_specs=_bs((tm, SGW), lambda i: (i, 0)),
        out_shape=jax.ShapeDtypeStruct((s_len, SGW), MM), name="sgmlp", compiler_params=_cparams(1))(proj, proj, proj, ln_g, ln_b, ws, bs)


def _sg_bwd_call(proj, ln_g, ln_b, ws, bs, dys):
    s_len = proj.shape[0]
    tm = min(s_len, 256)

    def body(*refs):
        dys_ref = refs[7]
        du_ref, dv_ref, dsg_ref, dlg_ref, dlb_ref, dws_ref, dbs_ref = refs[8:]
        _, vjp = jax.vjp(_sg_fn, *_sg_args(refs[:7]))
        du, dv, dsg, dlg, dlb, dws, dbs = vjp(dys_ref[...])
        du_ref[...] = du.astype(du_ref.dtype)
        dv_ref[...] = dv.astype(dv_ref.dtype)
        dsg_ref[...] = dsg.astype(dsg_ref.dtype)

        @pl.when(pl.program_id(0) == 0)
        def _():
            for r in (dlg_ref, dlb_ref, dws_ref, dbs_ref):
                r[...] = jnp.zeros_like(r)
        dlg_ref[...] += dlg
        dlb_ref[...] += dlb
        for g in range(SGG):
            dws_ref[g] += dws[g]
            dbs_ref[g] += dbs[g]

    row = _bs((tm, SGW), lambda i: (i, 0))
    sd = jax.ShapeDtypeStruct
    return pl.pallas_call(
        body, grid=(s_len // tm,), in_specs=_sg_specs(tm) + [row],
        out_specs=[row, row, row, _bs((1, SGW), lambda i: (0, 0)), _bs((1, SGW), lambda i: (0, 0)),
                   _bs((SGG, SGC, SGC), lambda i: (0, 0, 0)), _bs((SGG, SGC, 1), lambda i: (0, 0, 0))],
        out_shape=[sd((s_len, SGW), MM)] * 3 + [sd((1, SGW), F32), sd((1, SGW), F32), sd((SGG, SGC, SGC), F32), sd((SGG, SGC, 1), F32)],
        name="sgmlp_bwd", compiler_params=_cparams(1))(proj, proj, proj, ln_g, ln_b, ws, bs, dys)


def _memkv_call(mem, mem_g, wm, kg):
    m_len = mem.shape[0]

    def body(mem_ref, g_ref, w_ref, kg_ref, k_ref, v_ref):
        k, v = _memkv_fn(mem_ref[...], g_ref[...], w_ref[...], kg_ref[...])
        k_ref[...] = k.astype(k_ref.dtype)
        v_ref[...] = v.astype(v_ref.dtype)

    return pl.pallas_call(body, out_shape=[jax.ShapeDtypeStruct((m_len, MH * MHD), MM)] * 2, name="memkv",
                          compiler_params=pltpu.CompilerParams(vmem_limit_bytes=VMEM_LIMIT))(mem, mem_g, wm, kg)


def _memkv_bwd_call(mem, mem_g, wm, kg, dk, dv):
    def body(mem_ref, g_ref, w_ref, kg_ref, dk_ref, dv_ref, dg_ref, dw_ref, dkg_ref):
        _, vjp = jax.vjp(_memkv_fn, mem_ref[...], g_ref[...], _with_slot(w_ref[...]), kg_ref[...])
        _, dg, dw, dkg = vjp((dk_ref[...], dv_ref[...]))
        dg_ref[...] = dg
        dw_ref[...] = dw[1]
        dkg_ref[...] = dkg

    sd = jax.ShapeDtypeStruct
    return pl.pallas_call(body, out_shape=[sd((1, D), F32), sd((D, 2 * MH * MHD), F32), sd((1, MHD), F32)], name="memkv_bwd",
                          compiler_params=pltpu.CompilerParams(vmem_limit_bytes=VMEM_LIMIT))(mem, mem_g, wm, kg, dk, dv)


def _mem_specs(tm, m_len):
    w = MH * MHD
    return [_bs((tm, w), lambda i: (i, OFF_MQ // w)), _bs((tm, BW), lambda i: (i, (OFF_SG + 3 * BW) // BW)),
            _bs((m_len, w), lambda i: (0, 0)), _bs((m_len, w), lambda i: (0, 0)), _bs((1, MHD), lambda i: (0, 0))]


def _mem_call(proj, k, v, qg):
    s_len, m_len = proj.shape[0], k.shape[0]
    tm = min(s_len, 256)

    def body(mq_ref, sg_ref, k_ref, v_ref, qg_ref, y_ref):
        y_ref[...] = _mem_fn(mq_ref[...], sg_ref[...], k_ref[...], v_ref[...], qg_ref[...]).astype(y_ref.dtype)

    return pl.pallas_call(
        body, grid=(s_len // tm,), in_specs=_mem_specs(tm, m_len), out_specs=_bs((tm, BW), lambda i: (i, 0)),
        out_shape=jax.ShapeDtypeStruct((s_len, BW), MM), name="memattn", compiler_params=_cparams(1))(proj, proj, k, v, qg)


def _mem_bwd_call(proj, k, v, qg, dys):
    s_len, m_len = proj.shape[0], k.shape[0]
    tm = min(s_len, 256)
    w = MH * MHD

    def body(mq_ref, sg_ref, k_ref, v_ref, qg_ref, dys_ref, dmq_ref, dsg_ref, dk_ref, dv_ref, dqg_ref):
        _, vjp = jax.vjp(_mem_fn, mq_ref[...], sg_ref[...], k_ref[...].astype(F32), v_ref[...].astype(F32), qg_ref[...])
        dmq, dsg, dk, dv, dqg = vjp(dys_ref[...])
        dmq_ref[...] = dmq.astype(dmq_ref.dtype)
        dsg_ref[...] = dsg.astype(dsg_ref.dtype)

        @pl.when(pl.program_id(0) == 0)
        def _():
            for r in (dk_ref, dv_ref, dqg_ref):
                r[...] = jnp.zeros_like(r)
        dk_ref[...] += dk
        dv_ref[...] += dv
        dqg_ref[...] += dqg

    row = _bs((tm, BW), lambda i: (i, 0))
    kv = _bs((m_len, w), lambda i: (0, 0))
    sd = jax.ShapeDtypeStruct
    return pl.pallas_call(
        body, grid=(s_len // tm,), in_specs=_mem_specs(tm, m_len) + [row],
        out_specs=[row, row, kv, kv, _bs((1, MHD), lambda i: (0, 0))],
        out_shape=[sd((s_len, w), MM), sd((s_len, BW), MM), sd((m_len, w), F32), sd((m_len, w), F32), sd((1, MHD), F32)],
        name="memattn_bwd", compiler_params=_cparams(1))(proj, proj, k, v, qg, dys)


def _merge_specs(tm):
    row = _bs((tm, BW), lambda i: (i, 0))
    return [row, row, row, row, _bs((tm, NB * D), lambda i: (i, OFF_ML // (NB * D))), _bs((NB, D), lambda i: (0, 0)),
            _bs((N_CHIPS, NB, BW, D // N_CHIPS), lambda i: (0, 0, 0, 0)), _bs((D, D), lambda i: (0, 0))]


def _merge_call(ys, proj, bm, wb, wo, x):
    s_len = proj.shape[0]
    tm = min(s_len, 256)

    def body(ya, yb, yc, yd, lg_ref, bm_ref, wb_ref, wo_ref, x_ref, o_ref):
        out = _merge_fn([r[...] for r in (ya, yb, yc, yd)], lg_ref[...], [bm_ref[n:n + 1, :] for n in range(NB)],
                        [[wb_ref[j, n] for n in range(NB)] for j in range(N_CHIPS)], wo_ref[...])
        o_ref[...] = x_ref[...] + out

    xrow = _bs((tm, D), lambda i: (i, 0))
    return pl.pallas_call(
        body, grid=(s_len // tm,), in_specs=_merge_specs(tm) + [xrow], out_specs=xrow,
        out_shape=jax.ShapeDtypeStruct((s_len, D), F32), name="merge", compiler_params=_cparams(1))(*ys, proj, bm, wb, wo, x)


def _merge_bwd_call(ys, proj, bm, wb, wo, dout):
    s_len = proj.shape[0]
    tm = min(s_len, 256)

    def body(ya, yb, yc, yd, lg_ref, bm_ref, wb_ref, wo_ref, do_ref, dya, dyb, dyc, dyd, dlg_ref, dbm_ref, dwb_ref, dwo_ref):
        fn = lambda ys_, lg_, bm_, wb_, wo_: _merge_fn(ys_, lg_, bm_, wb_, wo_)
        _, vjp = jax.vjp(fn, [r[...].astype(F32) for r in (ya, yb, yc, yd)], lg_ref[...], [bm_ref[n:n + 1, :] for n in range(NB)],
                         [[_with_slot(wb_ref[j, n]) for n in range(NB)] for j in range(N_CHIPS)], _with_slot(wo_ref[...]))
        dys, dlg, dbm, dwb, dwo = vjp(do_ref[...])
        dwb, dwo = [[d[1] for d in row] for row in dwb], dwo[1]
        for r, d in zip((dya, dyb, dyc, dyd), dys):
            r[...] = d
        dlg_ref[...] = dlg.astype(dlg_ref.dtype)

        @pl.when(pl.program_id(0) == 0)
        def _():
            for r in (dbm_ref, dwb_ref, dwo_ref):
                r[...] = jnp.zeros_like(r)
        for n in range(NB):
            dbm_ref[n:n + 1, :] += dbm[n]
            for j in range(N_CHIPS):
                dwb_ref[j, n] += dwb[j][n]
        dwo_ref[...] += dwo

    row = _bs((tm, BW), lambda i: (i, 0))
    sd = jax.ShapeDtypeStruct
    wb_shape = (N_CHIPS, NB, BW, D // N_CHIPS)
    return pl.pallas_call(
        body, grid=(s_len // tm,), in_specs=_merge_specs(tm) + [_bs((tm, D), lambda i: (i, 0))],
        out_specs=[row, row, row, row, _bs((tm, NB * D), lambda i: (i, 0)), _bs((NB, D), lambda i: (0, 0)),
                   _bs(wb_shape, lambda i: (0, 0, 0, 0)), _bs((D, D), lambda i: (0, 0))],
        out_shape=[sd((s_len, BW), F32)] * 4 + [sd((s_len, NB * D), MM), sd((NB, D), F32), sd(wb_shape, F32), sd((D, D), F32)],
        name="merge_bwd", compiler_params=_cparams(1))(*ys, proj, bm, wb, wo, dout)


def _dh_call(dproj, w, x, g, dout, after=()):
    s_len = x.shape[0]
    tk = NP // 4
    after = list(after)

    def matmul_body(dp_ref, w_ref, *rest):
        o_ref = rest[-1]

        @pl.when(pl.program_id(0) == 0)
        def _():
            o_ref[...] = jnp.zeros_like(o_ref)
        o_ref[...] += lax.dot_general(dp_ref[...], w_ref[...], (((1,), (1,)), ((), ())), preferred_element_type=F32)

    dh = pl.pallas_call(
        matmul_body, grid=(NP // tk,),
        in_specs=[_bs((s_len, tk), lambda k: (0, k)), _bs((D, tk), lambda k: (0, k))] + [_ANY] * len(after),
        out_specs=_bs((s_len, D), lambda k: (0, 0)), out_shape=jax.ShapeDtypeStruct((s_len, D), F32),
        name="dh", compiler_params=_cparams(1))(dproj, w, *after)

    tm = min(s_len, 512)

    def norm_body(dh_ref, x_ref, g_ref, do_ref, dx_ref, dg_ref):
        _, vjp = jax.vjp(lambda x_, g_: _rms_n(x_, g_, D), x_ref[...], g_ref[...])
        dxr, dgr = vjp(dh_ref[...])
        dx_ref[...] = do_ref[...] + dxr

        @pl.when(pl.program_id(0) == 0)
        def _():
            dg_ref[...] = jnp.zeros_like(dg_ref)
        dg_ref[...] += dgr

    row = _bs((tm, D), lambda i: (i, 0))
    return pl.pallas_call(
        norm_body, grid=(s_len // tm,), in_specs=[row, row, _bs((1, D), lambda i: (0, 0)), row],
        out_specs=[row, _bs((1, D), lambda i: (0, 0))],
        out_shape=[jax.ShapeDtypeStruct((s_len, D), F32), jax.ShapeDtypeStruct((1, D), F32)],
        name="norm_bwd", compiler_params=_cparams(1))(dh, x, g, dout)


def _dw_call(h, dproj, after=()):
    s_len = h.shape[0]
    tn = 512
    after = list(after)

    def body(h_ref, dp_ref, *rest):
        rest[-1][...] = lax.dot_general(h_ref[...], dp_ref[...], (((0,), (0,)), ((), ())), preferred_element_type=F32)

    return pl.pallas_call(
        body, grid=(NP // tn,),
        in_specs=[_bs((s_len, D), lambda j: (0, 0)), _bs((s_len, tn), lambda j: (0, j))] + [_ANY] * len(after),
        out_specs=_bs((D, tn), lambda j: (0, j)), out_shape=jax.ShapeDtypeStruct((D, NP), F32),
        name="dw_in", compiler_params=_cparams(1))(h, dproj, *after)


def _loss_call(y, target):
    s_len = y.shape[0]
    tm = min(s_len, 512)

    def body(y_ref, t_ref, dy_ref, l_ref):
        e = y_ref[...] - t_ref[...]
        dy_ref[...] = e * (1.0 / D)

        @pl.when(pl.program_id(0) == 0)
        def _():
            l_ref[...] = jnp.zeros_like(l_ref)
        l_ref[...] += jnp.sum(e * e, axis=0, keepdims=True)

    row = _bs((tm, D), lambda i: (i, 0))
    return pl.pallas_call(
        body, grid=(s_len // tm,), in_specs=[row, row], out_specs=[row, _bs((1, D), lambda i: (0, 0))],
        out_shape=[jax.ShapeDtypeStruct((s_len, D), F32), jax.ShapeDtypeStruct((1, D), F32)],
        name="loss", compiler_params=_cparams(1))(y, target)


def _adamw_call(w, g, m, v, name):
    rows, cols = w.shape
    tr = min(_row_tile(rows), 128)

    def body(w_ref, g_ref, m_ref, v_ref, d_ref, nm_ref, nv_ref):
        gv = g_ref[...]
        m2 = ADAM_B1 * m_ref[...] + (1.0 - ADAM_B1) * gv
        v2 = ADAM_B2 * v_ref[...] + (1.0 - ADAM_B2) * (gv * gv)
        m_hat = m2 / (1.0 - ADAM_B1 ** ADAM_STEP)
        v_hat = v2 / (1.0 - ADAM_B2 ** ADAM_STEP)
        d_ref[...] = -ADAM_LR * (m_hat / (jnp.sqrt(v_hat) + ADAM_EPS) + ADAM_WD * w_ref[...])
        nm_ref[...] = m2
        nv_ref[...] = v2

    blk = _bs((tr, cols), lambda i: (i, 0))
    return pl.pallas_call(
        body, grid=(rows // tr,), in_specs=[blk] * 4, out_specs=[blk] * 3,
        out_shape=[jax.ShapeDtypeStruct((rows, cols), F32)] * 3, name=name, compiler_params=_cparams(1))(w, g, m, v)


def _adamw_layer_call(layer, ws, gs, ms, vs, prev, after, name, steps=8):
    n = len(ws)
    after = list(after)
    n_prev = 4 * n if prev is not None else 0

    def body(*refs):
        outs = refs[len(refs) - 4 * n:]
        for t in range(n):
            w_ref, g_ref, m_ref, v_ref = refs[t], refs[n + t], refs[2 * n + t], refs[3 * n + t]
            g_out, d_out, m_out, v_out = outs[4 * t:4 * t + 4]
            gv = g_ref[...]
            m2 = ADAM_B1 * m_ref[0] + (1.0 - ADAM_B1) * gv
            v2 = ADAM_B2 * v_ref[0] + (1.0 - ADAM_B2) * (gv * gv)
            m_hat = m2 / (1.0 - ADAM_B1 ** ADAM_STEP)
            v_hat = v2 / (1.0 - ADAM_B2 ** ADAM_STEP)
            g_out[0] = gv
            d_out[0] = -ADAM_LR * (m_hat / (jnp.sqrt(v_hat) + ADAM_EPS) + ADAM_WD * w_ref[0])
            m_out[0] = m2
            v_out[0] = v2

    def lay(a):
        return _bs((1, a.shape[1] // steps, a.shape[2]), lambda i: (layer, i, 0))

    in_specs = ([lay(a) for a in ws] + [_bs((g.shape[0] // steps, g.shape[1]), lambda i: (i, 0)) for g in gs]
                + [lay(a) for a in ms] + [lay(a) for a in vs] + [_ANY] * (n_prev + len(after)))
    return pl.pallas_call(
        body, grid=(steps,), in_specs=in_specs, out_specs=[lay(ws[t]) for t in range(n) for _ in range(4)],
        out_shape=[jax.ShapeDtypeStruct(ws[t].shape, F32) for t in range(n) for _ in range(4)],
        input_output_aliases={4 * n + q: q for q in range(n_prev)}, name=name, compiler_params=_cparams(1),
    )(*ws, *gs, *ms, *vs, *(prev if prev is not None else []), *after)


def _row_tile(rows):
    for cand in (512, 256, 128, 64, 32, 16, 8):
        if rows % cand == 0 and rows > cand:
            return cand
    return rows


def _pair_sum_call(grads, from_sibling, core, name):
    n = len(grads)

    def body(core_ref, *refs):
        for t in range(n):
            refs[2 * n + t][...] = (refs[t][...] + refs[n + t][...]).astype(MM)

    half = lambda g: (1, g.shape[1] // 2, g.shape[2])
    grid_spec = pltpu.PrefetchScalarGridSpec(
        num_scalar_prefetch=1, grid=(N_CHIPS,),
        in_specs=[pl.BlockSpec(half(g), lambda j, core_ref: (j, core_ref[0], 0)) for g in grads]
        + [pl.BlockSpec(half(g), lambda j, core_ref: (j, 0, 0)) for g in grads],
        out_specs=[pl.BlockSpec(half(g), lambda j, core_ref: (j, 0, 0)) for g in grads])
    return pl.pallas_call(
        body, grid_spec=grid_spec, out_shape=[jax.ShapeDtypeStruct((N_CHIPS,) + half(g)[1:], MM) for g in grads], name=name,
        compiler_params=_cparams(1))(core, *grads, *from_sibling)


def _owner_sum_call(chip_sums, from_chips, chip_core, name):
    n = len(chip_sums)
    steps = 4

    def body(ids_ref, *refs):
        for t in range(n):
            a, b = refs[t], refs[n + t]
            refs[2 * n + t][...] = ((a[0].astype(F32) + b[0].astype(F32)) + b[1].astype(F32)) + b[2].astype(F32)

    tile = lambda p: (p.shape[1] // steps, p.shape[2])
    grid_spec = pltpu.PrefetchScalarGridSpec(
        num_scalar_prefetch=1, grid=(steps,),
        in_specs=[pl.BlockSpec((1,) + tile(p), lambda i, ids_ref: (ids_ref[0], i, 0)) for p in chip_sums]
        + [pl.BlockSpec((3,) + tile(p), lambda i, ids_ref: (0, i, 0)) for p in chip_sums],
        out_specs=[pl.BlockSpec(tile(p), lambda i, ids_ref: (ids_ref[1] * steps + i, 0)) for p in chip_sums])
    return pl.pallas_call(
        body, grid_spec=grid_spec, out_shape=[jax.ShapeDtypeStruct((2 * p.shape[1], p.shape[2]), F32) for p in chip_sums],
        name=name, compiler_params=_cparams(1))(chip_core, *chip_sums, *from_chips)


def _sum8_call(parts):
    n, rows, cols = parts.shape
    tr = _row_tile(rows)

    def body(p_ref, o_ref):
        acc = p_ref[0]
        for k in range(1, n):
            acc = acc + p_ref[k]
        o_ref[...] = acc

    return pl.pallas_call(
        body, grid=(rows // tr,), in_specs=[_bs((n, tr, cols), lambda i: (0, i, 0))], out_specs=_bs((tr, cols), lambda i: (i, 0)),
        out_shape=jax.ShapeDtypeStruct((rows, cols), F32), name="sum_small_grads", compiler_params=_cparams(1))(parts)


_ANY = pl.BlockSpec(memory_space=pl.ANY)


def _all_gather8(blk, name):
    rows, cols = blk.shape

    def body(x_ref, out_ref, send_sems, recv_sems, local_sem):
        x, y, c = lax.axis_index("x"), lax.axis_index("y"), lax.axis_index("c")
        me, sibling = (x, y, c), (x, y, 1 - c)
        chips = [(1 - x, y), (x, 1 - y), (1 - x, 1 - y)]

        def slot(px, py, pc):
            return out_ref.at[4 * px + 2 * py + pc]

        def copy(k, block, to, src=None):
            return pltpu.make_async_remote_copy(
                src_ref=slot(*block) if src is None else src, dst_ref=slot(*block),
                send_sem=send_sems.at[k], recv_sem=recv_sems.at[k], device_id=to, device_id_type=MESH_ID)

        mine = pltpu.make_async_copy(x_ref, slot(*me), local_sem)
        mine.start()
        first = [copy(0, me, sibling, src=x_ref)]
        first += [copy(1 + j, me, (*chip, c), src=x_ref) for j, chip in enumerate(chips)]
        for cp in first:
            cp.start()
        passed = [copy(4 + j, (*chip, c), sibling) for j, chip in enumerate(chips)]
        for j, chip in enumerate(chips):
            copy(1 + j, (*chip, c), me).wait_recv()
            passed[j].start()
        copy(0, sibling, me).wait_recv()
        for j, chip in enumerate(chips):
            copy(4 + j, (*chip, 1 - c), me).wait_recv()
        for cp in first + passed:
            cp.wait_send()
        mine.wait()

    return pl.pallas_call(
        body, out_shape=jax.ShapeDtypeStruct((8, rows, cols), blk.dtype), in_specs=[_ANY], out_specs=_ANY,
        scratch_shapes=[pltpu.SemaphoreType.DMA((7,)), pltpu.SemaphoreType.DMA((7,)), pltpu.SemaphoreType.DMA],
        name=name)(blk)


def _half_rows(ref, lead, half, which):
    rows = pl.ds(pl.multiple_of(half * which, half), half)
    return ref.at[rows] if lead is None else ref.at[lead, rows]


_HBM = pl.BlockSpec(memory_space=pltpu.HBM)
_SEM = pl.BlockSpec(memory_space=pltpu.SEMAPHORE)
_ORDERED_EFFECT = pltpu.CompilerParams(has_side_effects=pltpu.SideEffectType.DATAFLOW_SIDE_EFFECTING)


_VMEM = pl.BlockSpec(memory_space=pltpu.VMEM)
_TOKEN = jax.ShapeDtypeStruct((8, LANES), F32)


def _in_hbm(a):
    return pltpu.with_memory_space_constraint(a, pltpu.HBM)


def _tie(small, token):
    return small + token[0:1, 0:1].reshape((1,) * small.ndim)


def _pair_exchange_start_call(grads, name):
    n = len(grads)
    half = [g.shape[1] // 2 for g in grads]

    def body(*refs):
        srcs, outs = refs[:n], refs[n:2 * n]
        send_sems, recv_sems, token = refs[2 * n:]
        x, y, c = lax.axis_index("x"), lax.axis_index("y"), lax.axis_index("c")
        for t in range(n):
            pltpu.make_async_remote_copy(
                src_ref=srcs[t].at[:, pl.ds(pl.multiple_of(half[t] * (1 - c), half[t]), half[t])], dst_ref=outs[t],
                send_sem=send_sems.at[t], recv_sem=recv_sems.at[t], device_id=(x, y, 1 - c), device_id_type=MESH_ID).start()
        token[...] = jnp.zeros_like(token)

    dma = pltpu.SemaphoreType.DMA
    return pl.pallas_call(
        body, out_shape=[pltpu.HBM((g.shape[0], g.shape[1] // 2, g.shape[2]), g.dtype) for g in grads] + [dma((n,)), dma((n,)), _TOKEN],
        in_specs=[_HBM] * n, out_specs=[_HBM] * n + [_SEM, _SEM, _VMEM], name=name, compiler_params=_ORDERED_EFFECT,
    )(*[_in_hbm(g) for g in grads])


def _pair_exchange_finish_call(grads, bufs, send_sems, recv_sems, after, name):
    n = len(grads)
    after = list(after)
    half = [g.shape[1] // 2 for g in grads]

    def body(*refs):
        srcs, ins, send_ref, recv_ref = refs[:n], refs[n:2 * n], refs[2 * n], refs[2 * n + 1]
        x, y, c = lax.axis_index("x"), lax.axis_index("y"), lax.axis_index("c")
        for t in range(n):
            pltpu.make_async_remote_copy(
                src_ref=srcs[t].at[:, pl.ds(pl.multiple_of(half[t] * (1 - c), half[t]), half[t])], dst_ref=ins[t],
                send_sem=send_ref.at[t], recv_sem=recv_ref.at[t], device_id=(x, y, 1 - c), device_id_type=MESH_ID).wait()

    return pl.pallas_call(
        body, out_shape=[pltpu.HBM(b.shape, b.dtype) for b in bufs],
        in_specs=[_HBM] * (2 * n) + [_SEM, _SEM] + [_ANY] * len(after), out_specs=[_HBM] * n,
        input_output_aliases={n + t: t for t in range(n)}, name=name, compiler_params=_ORDERED_EFFECT,
    )(*[_in_hbm(g) for g in grads], *bufs, send_sems, recv_sems, *after)


def _chip_scatter_start_call(chip_sums, name):
    n = len(chip_sums)

    def body(*refs):
        srcs, outs = refs[:n], refs[n:2 * n]
        send_sems, recv_sems, token = refs[2 * n:]
        x, y, c = lax.axis_index("x"), lax.axis_index("y"), lax.axis_index("c")
        chips = [(1 - x, y), (x, 1 - y), (1 - x, 1 - y)]
        for k, (cx, cy) in enumerate(chips):
            for t in range(n):
                pltpu.make_async_remote_copy(
                    src_ref=srcs[t].at[2 * cx + cy], dst_ref=outs[t].at[k], send_sem=send_sems.at[3 * t + k],
                    recv_sem=recv_sems.at[3 * t + k], device_id=(cx, cy, c), device_id_type=MESH_ID).start()
        token[...] = jnp.zeros_like(token)

    dma = pltpu.SemaphoreType.DMA
    return pl.pallas_call(
        body, out_shape=[pltpu.HBM((3,) + p.shape[1:], p.dtype) for p in chip_sums] + [dma((3 * n,)), dma((3 * n,)), _TOKEN],
        in_specs=[_HBM] * n, out_specs=[_HBM] * n + [_SEM, _SEM, _VMEM], name=name, compiler_params=_ORDERED_EFFECT,
    )(*[_in_hbm(p) for p in chip_sums])


def _chip_scatter_finish_call(chip_sums, bufs, send_sems, recv_sems, after, name):
    n = len(chip_sums)
    after = list(after)

    def body(*refs):
        srcs, ins, send_ref, recv_ref = refs[:n], refs[n:2 * n], refs[2 * n], refs[2 * n + 1]
        x, y, c = lax.axis_index("x"), lax.axis_index("y"), lax.axis_index("c")
        chips = [(1 - x, y), (x, 1 - y), (1 - x, 1 - y)]
        for k, (cx, cy) in enumerate(chips):
            for t in range(n):
                pltpu.make_async_remote_copy(
                    src_ref=srcs[t].at[2 * cx + cy], dst_ref=ins[t].at[k], send_sem=send_ref.at[3 * t + k],
                    recv_sem=recv_ref.at[3 * t + k], device_id=(cx, cy, c), device_id_type=MESH_ID).wait()

    return pl.pallas_call(
        body, out_shape=[pltpu.HBM(b.shape, b.dtype) for b in bufs],
        in_specs=[_HBM] * (2 * n) + [_SEM, _SEM] + [_ANY] * len(after), out_specs=[_HBM] * n,
        input_output_aliases={n + t: t for t in range(n)}, name=name, compiler_params=_ORDERED_EFFECT,
    )(*[_in_hbm(p) for p in chip_sums], *bufs, send_sems, recv_sems, *after)


def _place_own_call(mine, chip_core, name):
    n = len(mine)

    def body(ids_ref, *refs):
        for t in range(n):
            refs[n + t][0] = refs[t][...]

    def imap_out(s):
        pad = (0,) * (s.ndim - 1)
        return lambda i, ids_ref: (ids_ref[0], ids_ref[1]) + pad

    grid_spec = pltpu.PrefetchScalarGridSpec(
        num_scalar_prefetch=1, grid=(1,), in_specs=[pl.BlockSpec(s.shape, lambda i, ids_ref, k=s.ndim: (0,) * k) for s in mine],
        out_specs=[pl.BlockSpec((1,) + s.shape, imap_out(s)) for s in mine])
    return pl.pallas_call(
        body, grid_spec=grid_spec,
        out_shape=[jax.ShapeDtypeStruct((N_CHIPS, 2 * s.shape[0]) + s.shape[1:], s.dtype) for s in mine],
        name=name, compiler_params=_cparams(1))(chip_core, *mine)


def _gather_start_call(mine, bufs, after, name):
    n = len(mine)
    half = [s.shape[0] for s in mine]

    def body(*refs):
        srcs, outs = refs[:n], refs[2 * n + 1:3 * n + 1]
        send_sems, recv_sib, recv_ici, token = refs[3 * n + 1:]
        x, y, c = lax.axis_index("x"), lax.axis_index("y"), lax.axis_index("c")
        chips = [(1 - x, y), (x, 1 - y), (1 - x, 1 - y)]
        for t in range(n):
            dst = _half_rows(outs[t], 2 * x + y, half[t], c)
            pltpu.make_async_remote_copy(src_ref=srcs[t], dst_ref=dst, send_sem=send_sems.at[4 * t], recv_sem=recv_sib.at[t],
                                         device_id=(x, y, 1 - c), device_id_type=MESH_ID).start()
            for j, chip in enumerate(chips):
                pltpu.make_async_remote_copy(src_ref=srcs[t], dst_ref=dst, send_sem=send_sems.at[4 * t + 1 + j],
                                             recv_sem=recv_ici.at[3 * t + j], device_id=(*chip, c), device_id_type=MESH_ID).start()
        token[...] = jnp.zeros_like(token)

    dma = pltpu.SemaphoreType.DMA
    return pl.pallas_call(
        body, out_shape=[pltpu.HBM(b.shape, b.dtype) for b in bufs] + [dma((4 * n,)), dma((n,)), dma((3 * n,)), _TOKEN],
        in_specs=[_HBM] * (2 * n) + [_ANY], out_specs=[_HBM] * n + [_SEM] * 3 + [_VMEM],
        input_output_aliases={n + t: t for t in range(n)}, name=name, compiler_params=_ORDERED_EFFECT,
    )(*[_in_hbm(s) for s in mine], *[_in_hbm(b) for b in bufs], after)


def _gather_forward_call(bufs, recv_ici, after, name):
    n = len(bufs)
    half = [b.shape[1] // 2 for b in bufs]

    def body(*refs):
        ins, recv_ici_ref = refs[:n], refs[n]
        outs = refs[n + 2:2 * n + 2]
        send_fwd, recv_fwd, token = refs[2 * n + 2:]
        x, y, c = lax.axis_index("x"), lax.axis_index("y"), lax.axis_index("c")
        chips = [(1 - x, y), (x, 1 - y), (1 - x, 1 - y)]
        for j, (cx, cy) in enumerate(chips):
            for t in range(n):
                landed = _half_rows(ins[t], 2 * cx + cy, half[t], c)
                dst = _half_rows(outs[t], 2 * cx + cy, half[t], c)
                pltpu.make_async_remote_copy(src_ref=landed, dst_ref=landed, send_sem=send_fwd.at[3 * t + j],
                                             recv_sem=recv_ici_ref.at[3 * t + j], device_id=(cx, cy, c),
                                             device_id_type=MESH_ID).wait_recv()
                pltpu.make_async_remote_copy(src_ref=landed, dst_ref=dst, send_sem=send_fwd.at[3 * t + j],
                                             recv_sem=recv_fwd.at[3 * t + j], device_id=(x, y, 1 - c),
                                             device_id_type=MESH_ID).start()
        token[...] = jnp.zeros_like(token)

    dma = pltpu.SemaphoreType.DMA
    return pl.pallas_call(
        body, out_shape=[pltpu.HBM(b.shape, b.dtype) for b in bufs] + [dma((3 * n,)), dma((3 * n,)), _TOKEN],
        in_specs=[_HBM] * n + [_SEM, _ANY], out_specs=[_HBM] * n + [_SEM] * 2 + [_VMEM],
        input_output_aliases={t: t for t in range(n)}, name=name, compiler_params=_ORDERED_EFFECT,
    )(*bufs, recv_ici, after)


def _gather_finish_call(shards, bufs, send_sems, recv_sib, send_fwd, recv_fwd, after, name):
    n = len(bufs)
    half = [b.shape[1] // 2 for b in bufs]

    def body(*refs):
        srcs, ins = refs[:n], refs[n:2 * n]
        send_ref, recv_sib_ref, send_fwd_ref, recv_fwd_ref = refs[2 * n:2 * n + 4]
        x, y, c = lax.axis_index("x"), lax.axis_index("y"), lax.axis_index("c")
        chips = [(1 - x, y), (x, 1 - y), (1 - x, 1 - y)]
        sibling = (x, y, 1 - c)
        for t in range(n):
            for k in range(4):
                pltpu.make_async_remote_copy(src_ref=srcs[t], dst_ref=srcs[t], send_sem=send_ref.at[4 * t + k],
                                             recv_sem=recv_sib_ref.at[t], device_id=sibling, device_id_type=MESH_ID).wait_send()
            from_sibling = _half_rows(ins[t], 2 * x + y, half[t], 1 - c)
            pltpu.make_async_remote_copy(src_ref=from_sibling, dst_ref=from_sibling, send_sem=send_ref.at[4 * t],
                                         recv_sem=recv_sib_ref.at[t], device_id=sibling, device_id_type=MESH_ID).wait_recv()
            for j, (cx, cy) in enumerate(chips):
                sent = _half_rows(ins[t], 2 * cx + cy, half[t], c)
                passed = _half_rows(ins[t], 2 * cx + cy, half[t], 1 - c)
                pltpu.make_async_remote_copy(src_ref=sent, dst_ref=passed, send_sem=send_fwd_ref.at[3 * t + j],
                                             recv_sem=recv_fwd_ref.at[3 * t + j], device_id=sibling, device_id_type=MESH_ID).wait()

    return pl.pallas_call(
        body, out_shape=[pltpu.HBM(b.shape, b.dtype) for b in bufs],
        in_specs=[_HBM] * (2 * n) + [_SEM] * 4 + [_ANY], out_specs=[_HBM] * n,
        input_output_aliases={n + t: t for t in range(n)}, name=name, compiler_params=_ORDERED_EFFECT,
    )(*[_in_hbm(s) for s in shards], *bufs, send_sems, recv_sib, send_fwd, recv_fwd, after)


def _pair_gather_call(bufs, name):
    n = len(bufs)
    half = [b.shape[0] // 2 for b in bufs]

    def body(*refs):
        srcs, outs, send_sems, recv_sems = refs[:n], refs[n:2 * n], refs[2 * n], refs[2 * n + 1]
        x, y, c = lax.axis_index("x"), lax.axis_index("y"), lax.axis_index("c")
        for t in range(n):
            pltpu.make_async_remote_copy(
                src_ref=_half_rows(srcs[t], None, half[t], c), dst_ref=_half_rows(outs[t], None, half[t], c),
                send_sem=send_sems.at[t], recv_sem=recv_sems.at[t], device_id=(x, y, 1 - c), device_id_type=MESH_ID).start()
        for t in range(n):
            pltpu.make_async_remote_copy(
                src_ref=_half_rows(srcs[t], None, half[t], c), dst_ref=_half_rows(outs[t], None, half[t], 1 - c),
                send_sem=send_sems.at[t], recv_sem=recv_sems.at[t], device_id=(x, y, 1 - c), device_id_type=MESH_ID).wait()

    return pl.pallas_call(
        body, out_shape=[jax.ShapeDtypeStruct(b.shape, b.dtype) for b in bufs], in_specs=[_ANY] * n, out_specs=[_ANY] * n,
        input_output_aliases={t: t for t in range(n)},
        scratch_shapes=[pltpu.SemaphoreType.DMA((n,)), pltpu.SemaphoreType.DMA((n,))], name=name)(*bufs)


def _pack_rows(flats, dtype, row_multiple):
    flat = jnp.concatenate([f.reshape(-1).astype(dtype) for f in flats])
    n = flat.shape[0]
    rows = -(-n // PACK_W)
    rows = -(-rows // row_multiple) * row_multiple
    return jnp.pad(flat, (0, rows * PACK_W - n)).reshape(rows, PACK_W)


def _unpack(flat, shapes):
    out, off = [], 0
    for shp in shapes:
        n = math.prod(shp)
        out.append(flat[off:off + n].reshape(shp))
        off += n
    return out


_W_IN_SEGMENTS = ((R_ML, R_END, OFF_ML), (R_SG, R_ML, OFF_SG), (R_CV, R_SGI, OFF_CV), (R_SGI, R_MQ, OFF_SGI), (R_MQ, R_SG, OFF_MQ),
                  (R_CQ, R_CKV, OFF_CQ), (R_CKV, R_KR, OFF_CKV), (R_KR, R_CV, OFF_KR + NOPE))
W_IN_SHARD = R_END // N_CHIPS


def _realign_call(wg):
    tr = 128

    def body(w_ref, o_ref):
        pieces, pos = [], 0
        for r0, r1, a0 in _W_IN_SEGMENTS:
            if a0 > pos:
                pieces.append(jnp.zeros((tr, a0 - pos), o_ref.dtype))
            while r0 < r1:
                j = r0 // W_IN_SHARD
                hi = min(r1, (j + 1) * W_IN_SHARD)
                pieces.append(w_ref[j, :, r0 - j * W_IN_SHARD:hi - j * W_IN_SHARD])
                a0, r0 = a0 + hi - r0, hi
            pos = a0
        pieces.append(jnp.zeros((tr, NP - pos), o_ref.dtype))
        o_ref[...] = jnp.concatenate(pieces, axis=1)

    return pl.pallas_call(
        body, grid=(D // tr,), in_specs=[_bs((N_CHIPS, tr, W_IN_SHARD), lambda i: (0, i, 0))],
        out_specs=_bs((tr, NP), lambda i: (i, 0)), out_shape=jax.ShapeDtypeStruct((D, NP), wg.dtype),
        name="w_in_realign", compiler_params=_cparams(1))(wg)


def _unalign_call(dw):
    tr = 128
    by_ref = sorted(_W_IN_SEGMENTS)

    def body(dw_ref, o_ref):
        for j in range(N_CHIPS):
            lo_j, hi_j = j * W_IN_SHARD, (j + 1) * W_IN_SHARD
            pieces = []
            for r0, r1, a0 in by_ref:
                lo, hi = max(r0, lo_j), min(r1, hi_j)
                if lo < hi:
                    pieces.append(dw_ref[:, a0 + lo - r0:a0 + hi - r0])
            o_ref[j] = jnp.concatenate(pieces, axis=1)

    return pl.pallas_call(
        body, grid=(D // tr,), in_specs=[_bs((tr, NP), lambda i: (i, 0))],
        out_specs=_bs((N_CHIPS, tr, W_IN_SHARD), lambda i: (0, i, 0)),
        out_shape=jax.ShapeDtypeStruct((N_CHIPS, D, W_IN_SHARD), dw.dtype), name="w_in_unalign", compiler_params=_cparams(1))(dw)


def _wuq_to_heads(w):
    w3 = w.reshape(QL, H, QKH)
    w3 = jnp.pad(w3, ((0, 0), (0, 0), (0, LANES - QKH)))
    return jnp.transpose(w3, (1, 0, 2))


def _wuq_from_heads(wh):
    return jnp.transpose(wh[:, :, :QKH], (1, 0, 2)).reshape(QL, H * QKH)


def _wukv_to_heads(w):
    w3 = w.reshape(KVL, H, NOPE + VH)
    wkn = jnp.transpose(jnp.pad(w3[:, :, :NOPE], ((0, 0), (0, 0), (0, LANES - NOPE))), (1, 0, 2))
    wv3 = w3[:, :, NOPE:]
    z = jnp.zeros((KVL, VH), w.dtype)
    cols = []
    for h in range(H):
        cols += [wv3[:, h], z] if h % 2 == 0 else [z, wv3[:, h]]
    return wkn, jnp.concatenate(cols, axis=1)


def _wukv_from_heads(wkn, wv):
    kn = jnp.transpose(wkn[:, :, :NOPE], (1, 0, 2))
    vs = jnp.stack([wv[:, LANES * h + VH * (h % 2):LANES * h + VH * (h % 2) + VH] for h in range(H)], axis=1)
    return jnp.concatenate([kn, vs], axis=2).reshape(KVL, H * (NOPE + VH))


def _layer_fwd(x, mem, tabs, p):
    proj, h = _proj_call(x, p["norm_g"], p["w_in"])
    if p.get("late") is not None:
        p = dict(p, **p["late"](proj))
    q, k, v = _mla_prep_call(proj, tabs, p["cq_g"], p["ckv_g"], p["qg"], p["kg"], p["wuq"], p["wkn"], p["wv"])
    ya, attn_o, attn_lse = _attn_call(q, k, v, proj)
    bm = p["bm"]
    if p.get("after_attn") is not None:
        bm = _tie(bm, p["after_attn"](ya))
    yb = _conv_call(proj, p["conv_w"], p["conv_b"])
    yc = _sg_call(proj, p["ln_g"], p["ln_b"], p["ws"], p["bs"])
    mk, mv = _memkv_call(mem, p["mem_g"], p["wm"], p["mkg"])
    yd = _mem_call(proj, mk, mv, p["mqg"])
    out = _merge_call((ya, yb, yc, yd), proj, bm, p["wb"], p["wo"], x)
    return out, dict(p=p, x=x, proj=proj, h=h, q=q, k=k, v=v, attn_o=attn_o, attn_lse=attn_lse, ys=(ya, yb, yc, yd), mk=mk, mv=mv)


def _layer_bwd(dout, mem, tabs, p, sv, start_after=None, on_rest_grads=None, on_grads=None):
    proj = sv["proj"]
    bm = p["bm"] if start_after is None else _tie(p["bm"], start_after)
    dya, dyb, dyc, dyd, dml, dbm, dwb, dwo = _merge_bwd_call(sv["ys"], proj, bm, p["wb"], p["wo"], dout)
    dq, dk, dv, dsg_a = _attn_bwd_call(sv["q"], sv["k"], sv["v"], proj, dya, sv["attn_o"], sv["attn_lse"])
    dcq, dckv, dkr, dcqg, dckvg, dqg, dkg, dwuq, dwkn, dwv = _mla_prep_bwd_call(
        proj, tabs, p["cq_g"], p["ckv_g"], p["qg"], p["kg"], p["wuq"], p["wkn"], p["wv"], dq, dk, dv)
    dbg, dcg, dxi, dsg_b, dcw, dcb = _conv_bwd_call(proj, p["conv_w"], p["conv_b"], dyb)
    du, dvv, dsg_c, dlg, dlb, dws, dbs = _sg_bwd_call(proj, p["ln_g"], p["ln_b"], p["ws"], p["bs"], dyc)
    dmq, dsg_d, dmk, dmv, dmqg = _mem_bwd_call(proj, sv["mk"], sv["mv"], p["mqg"], dyd)
    dmem_g, dwm, dmkg = _memkv_bwd_call(mem, p["mem_g"], p["wm"], p["mkg"], dmk, dmv)
    grads = dict(cq_norm_g=dcqg[0], ckv_norm_g=dckvg[0], mla_q_norm_g=dqg[0, :QKH], mla_k_norm_g=dkg[0, :QKH],
                 conv_w=dcw, conv_b=dcb[0], sg_ln_g=dlg[0], sg_ln_b=dlb[0], w_spatial=dws, b_spatial=dbs[:, :, 0],
                 mem_norm_g=dmem_g[0], mem_q_norm_g=dmqg[0], mem_k_norm_g=dmkg[0], b_merge=dbm,
                 wuq_heads=dwuq, wkn_heads=dwkn, wv_heads=dwv, w_mem_kv=dwm, w_branch_chips=dwb, w_out=dwo)
    started = [on_rest_grads(grads)] if on_rest_grads is not None else []
    dproj = jnp.concatenate([dml, dsg_a, dsg_b, dsg_c, dsg_d, dbg, dcg, dxi, du, dvv, dmq, dcq, dckv, dkr], axis=1)
    grads["w_in_aligned"] = _dw_call(sv["h"], dproj, started)
    tokens = on_grads(grads) if on_grads is not None else ()
    dx, dnorm_g = _dh_call(dproj, p["w_in"], sv["x"], p["norm_g"], dout, tokens)
    grads["norm_g"] = dnorm_g[0]
    return dx, grads


def _chips_to_cols(a):
    return jnp.concatenate([a[j] for j in range(N_CHIPS)], axis=1)


def _cols_to_chips(a):
    cols = a.shape[1] // N_CHIPS
    return jnp.stack([a[:, cols * j:cols * (j + 1)] for j in range(N_CHIPS)])


def _layer_params_first(l, rep, w_in_gathered, conv_w, b_merge):
    pad_g = lambda g: jnp.pad(g, (0, LANES - QKH)).reshape(1, LANES)
    return dict(
        norm_g=rep["norm_g"][l].reshape(1, D), w_in=_realign_call(w_in_gathered),
        cq_g=rep["cq_norm_g"][l].reshape(1, QL), ckv_g=rep["ckv_norm_g"][l].reshape(1, KVL),
        qg=pad_g(rep["mla_q_norm_g"][l]), kg=pad_g(rep["mla_k_norm_g"][l]),
        conv_w=conv_w, conv_b=rep["conv_b"][l].reshape(1, CW),
        ln_g=rep["sg_ln_g"][l].reshape(1, SGW), ln_b=rep["sg_ln_b"][l].reshape(1, SGW),
        ws=rep["w_spatial"][l], bs=rep["b_spatial"][l].reshape(SGG, SGC, 1),
        mem_g=rep["mem_norm_g"][l].reshape(1, D),
        mqg=rep["mem_q_norm_g"][l].reshape(1, MHD), mkg=rep["mem_k_norm_g"][l].reshape(1, MHD), bm=b_merge)


def _layer_params_rest(gathered):
    wkn, wv = _wukv_to_heads(_chips_to_cols(gathered["w_ukv"]))
    return dict(wuq=_wuq_to_heads(_chips_to_cols(gathered["w_uq"])), wkn=wkn, wv=wv,
                wm=gathered["w_mem_kv"].reshape(D, 2 * MH * MHD), wb=gathered["w_branch"], wo=gathered["w_out"].reshape(D, D))


def _layer_params(l, rep, gathered, conv_w, b_merge):
    return dict(_layer_params_first(l, rep, gathered["w_in"], conv_w, b_merge), **_layer_params_rest(gathered))


def _forward_backward(x, mem, pos, target, params, bwd_hooks=None):
    tabs = _rope_tables(pos)
    params = list(params)
    saved = []
    act = x
    for l in range(DEPTH):
        if callable(params[l]):
            params[l] = params[l](saved[-1], act)
        act, sv = _layer_fwd(act, mem, tabs, params[l])
        saved.append(sv)
    dy, sq = _loss_call(act, target)
    grads = [None] * DEPTH
    token = None
    for l in reversed(range(DEPTH)):
        hooks = dict(bwd_hooks[l]) if bwd_hooks else {}
        after_layer = hooks.pop("after_layer", None)
        dy, grads[l] = _layer_bwd(dy, mem, tabs, saved[l]["p"], saved[l], start_after=token, **hooks)
        token = after_layer(dy) if after_layer is not None else None
    return sq, dy, grads


_SHARDED_MM = ("w_in", "w_branch", "w_out", "w_mem_kv", "w_uq", "w_ukv")
_SHARDED_F32 = ("conv_w", "b_merge")
_REPLICATED = ("norm_g", "cq_norm_g", "ckv_norm_g", "mla_q_norm_g", "mla_k_norm_g", "conv_b", "sg_ln_g", "sg_ln_b",
               "w_spatial", "b_spatial", "mem_norm_g", "mem_q_norm_g", "mem_k_norm_g")
_ALL_REDUCED = _REPLICATED + _SHARDED_F32
_WEIGHTS = ("norm_g", "w_in", "cq_norm_g", "ckv_norm_g", "w_uq", "w_ukv", "mla_q_norm_g", "mla_k_norm_g", "conv_w", "conv_b",
            "sg_ln_g", "sg_ln_b", "w_spatial", "b_spatial", "mem_norm_g", "w_mem_kv", "mem_q_norm_g", "mem_k_norm_g",
            "b_merge", "w_branch", "w_out")
_SMALL = tuple(n for n in _WEIGHTS if n not in _SHARDED_MM)


def _gather_small_sharded(w):
    names = _SHARDED_F32
    packed = _pack_rows([w[n] for n in names], F32, 8)
    got = _all_gather8(packed, "gather_small_weights")
    per_chip = [_unpack(got[2 * j].reshape(-1), [w[n].shape for n in names]) for j in range(N_CHIPS)]
    return {n: jnp.concatenate([per_chip[j][t] for j in range(N_CHIPS)], axis=2) for t, n in enumerate(names)}


class _Gather:
    def __init__(self, w, layer, names, after, tag):
        self.names, self.tag = names, tag
        x, y, c = lax.axis_index("x"), lax.axis_index("y"), lax.axis_index("c")
        chip_core = jnp.stack([2 * x + y, c]).astype(jnp.int32)
        halves = [w[n].shape[1] // 2 for n in names]
        self.srcs = [lax.dynamic_slice_in_dim(w[n][layer], c * h, h, axis=0).astype(MM) for n, h in zip(names, halves)]
        k = len(names)
        out = _gather_start_call(self.srcs, _place_own_call(self.srcs, chip_core, tag + "place_own"), after, tag + "start")
        self.bufs, self.send, self.recv_sib, self.recv_ici, self.token = out[:k], out[k], out[k + 1], out[k + 2], out[k + 3]

    def pass_on(self, after):
        k = len(self.names)
        out = _gather_forward_call(self.bufs, self.recv_ici, after, self.tag + "forward")
        self.bufs, self.send_fwd, self.recv_fwd = out[:k], out[k], out[k + 1]
        return out[k + 2]

    def finish(self, after):
        got = _gather_finish_call(self.srcs, self.bufs, self.send, self.recv_sib, self.send_fwd, self.recv_fwd, after,
                                  self.tag + "finish")
        return dict(zip(self.names, got))


class _ReduceScatter:
    SLABS = dict(
        w_in=lambda g: _unalign_call(g["w_in_aligned"]),
        w_branch=lambda g: g["w_branch_chips"].reshape(N_CHIPS, NB * BW, D // N_CHIPS),
        w_out=lambda g: g["w_out"].reshape(N_CHIPS, D // N_CHIPS, D),
        w_mem_kv=lambda g: g["w_mem_kv"].reshape(N_CHIPS, D // N_CHIPS, 2 * MH * MHD),
        w_uq=lambda g: _cols_to_chips(_wuq_from_heads(g["wuq_heads"])),
        w_ukv=lambda g: _cols_to_chips(_wukv_from_heads(g["wkn_heads"], g["wv_heads"])))

    def __init__(self, tag, names):
        self.tag, self.names = tag, names

    def exchange(self, grads):
        self.tensors = [self.SLABS[n](grads) for n in self.names]
        n = len(self.tensors)
        out = _pair_exchange_start_call(self.tensors, self.tag + "exchange_start")
        self.ex_bufs, self.ex_send, self.ex_recv = out[:n], out[n], out[n + 1]
        return out[n + 2]

    def scatter(self, after):
        n = len(self.tensors)
        c = lax.axis_index("c")
        from_sibling = _pair_exchange_finish_call(self.tensors, self.ex_bufs, self.ex_send, self.ex_recv, after,
                                                  self.tag + "exchange_finish")
        self.chip_sums = _pair_sum_call(self.tensors, from_sibling, c.astype(jnp.int32).reshape(1), self.tag + "pair_sum")
        out = _chip_scatter_start_call(self.chip_sums, self.tag + "scatter_start")
        self.bufs, self.send_sems, self.recv_sems, self.token = out[:n], out[n], out[n + 1], out[n + 2]
        return self.token

    def finish(self, after):
        x, y, c = lax.axis_index("x"), lax.axis_index("y"), lax.axis_index("c")
        chip_core = jnp.stack([2 * x + y, c]).astype(jnp.int32)
        from_chips = _chip_scatter_finish_call(self.chip_sums, self.bufs, self.send_sems, self.recv_sems, after,
                                               self.tag + "scatter_finish")
        mine = _owner_sum_call(self.chip_sums, from_chips, chip_core, self.tag + "owner_sum")
        return dict(zip(self.names, _pair_gather_call(mine, self.tag + "pair_gather")))


def _all_reduce_small(g, sq):
    packed = _pack_rows([g[n] for n in _ALL_REDUCED] + [sq], F32, 64)
    got = _all_gather8(packed, "gather_small_grads")
    total = _sum8_call(got).reshape(-1)
    parts = _unpack(total, [g[n].shape for n in _ALL_REDUCED] + [sq.shape])
    out = dict(zip(_ALL_REDUCED, parts))
    sq_total = parts[-1]
    chip = 2 * lax.axis_index("x") + lax.axis_index("y")
    for n in _SHARDED_F32:
        size = out[n].shape[2] // N_CHIPS
        out[n] = lax.dynamic_slice_in_dim(out[n], chip * size, size, axis=2)
    return out, sq_total


def _adamw_small(w, g, m, v, token):
    delta, new_m, new_v = {}, {}, {}
    shapes = [w[n].shape for n in _SMALL]
    pk = lambda t: _pack_rows([t[n] for n in _SMALL], F32, 64)
    d, nm, nv = _adamw_call(pk(w), _tie(pk(g), token), pk(m), pk(v), "adamw_small")
    for out, packed in ((delta, d), (new_m, nm), (new_v, nv)):
        out.update(zip(_SMALL, _unpack(packed.reshape(-1), shapes)))
    return delta, new_m, new_v


def kernel(x, mem, positions, norm_g, w_in, cq_norm_g, ckv_norm_g, w_uq, w_ukv, mla_q_norm_g, mla_k_norm_g, conv_w, conv_b, sg_ln_g, sg_ln_b, w_spatial, b_spatial, mem_norm_g, w_mem_kv, mem_q_norm_g, mem_k_norm_g, b_merge, w_branch, w_out, loss_target, m_norm_g, m_w_in, m_cq_norm_g, m_ckv_norm_g, m_w_uq, m_w_ukv, m_mla_q_norm_g, m_mla_k_norm_g, m_conv_w, m_conv_b, m_sg_ln_g, m_sg_ln_b, m_w_spatial, m_b_spatial, m_mem_norm_g, m_w_mem_kv, m_mem_q_norm_g, m_mem_k_norm_g, m_b_merge, m_w_branch, m_w_out, v_norm_g, v_w_in, v_cq_norm_g, v_ckv_norm_g, v_w_uq, v_w_ukv, v_mla_q_norm_g, v_mla_k_norm_g, v_conv_w, v_conv_b, v_sg_ln_g, v_sg_ln_b, v_w_spatial, v_b_spatial, v_mem_norm_g, v_w_mem_kv, v_mem_q_norm_g, v_mem_k_norm_g, v_b_merge, v_w_branch, v_w_out):
    w = dict(norm_g=norm_g, w_in=w_in, cq_norm_g=cq_norm_g, ckv_norm_g=ckv_norm_g, w_uq=w_uq, w_ukv=w_ukv,
             mla_q_norm_g=mla_q_norm_g, mla_k_norm_g=mla_k_norm_g, conv_w=conv_w, conv_b=conv_b, sg_ln_g=sg_ln_g,
             sg_ln_b=sg_ln_b, w_spatial=w_spatial, b_spatial=b_spatial, mem_norm_g=mem_norm_g, w_mem_kv=w_mem_kv,
             mem_q_norm_g=mem_q_norm_g, mem_k_norm_g=mem_k_norm_g, b_merge=b_merge, w_branch=w_branch, w_out=w_out)
    m = dict(norm_g=m_norm_g, w_in=m_w_in, cq_norm_g=m_cq_norm_g, ckv_norm_g=m_ckv_norm_g, w_uq=m_w_uq, w_ukv=m_w_ukv,
             mla_q_norm_g=m_mla_q_norm_g, mla_k_norm_g=m_mla_k_norm_g, conv_w=m_conv_w, conv_b=m_conv_b, sg_ln_g=m_sg_ln_g,
             sg_ln_b=m_sg_ln_b, w_spatial=m_w_spatial, b_spatial=m_b_spatial, mem_norm_g=m_mem_norm_g, w_mem_kv=m_w_mem_kv,
             mem_q_norm_g=m_mem_q_norm_g, mem_k_norm_g=m_mem_k_norm_g, b_merge=m_b_merge, w_branch=m_w_branch, w_out=m_w_out)
    v = dict(norm_g=v_norm_g, w_in=v_w_in, cq_norm_g=v_cq_norm_g, ckv_norm_g=v_ckv_norm_g, w_uq=v_w_uq, w_ukv=v_w_ukv,
             mla_q_norm_g=v_mla_q_norm_g, mla_k_norm_g=v_mla_k_norm_g, conv_w=v_conv_w, conv_b=v_conv_b, sg_ln_g=v_sg_ln_g,
             sg_ln_b=v_sg_ln_b, w_spatial=v_w_spatial, b_spatial=v_b_spatial, mem_norm_g=v_mem_norm_g, w_mem_kv=v_w_mem_kv,
             mem_q_norm_g=v_mem_q_norm_g, mem_k_norm_g=v_mem_k_norm_g, b_merge=v_b_merge, w_branch=v_w_branch, w_out=v_w_out)

    chip_core = jnp.stack([2 * lax.axis_index("x") + lax.axis_index("y"), lax.axis_index("c")]).astype(jnp.int32)

    first = _Gather(w, 0, ("w_in",), chip_core, "gather_l0_w_in_")
    rest = _Gather(w, 0, _SHARDED_MM[1:], first.token, "gather_l0_rest_")
    later = _Gather(w, 1, _SHARDED_MM, rest.token, "gather_l1_")
    small = _gather_small_sharded(w)
    w_in0 = first.finish(first.pass_on(later.token))["w_in"]

    def rest_of_layer0(proj0):
        return _layer_params_rest(rest.finish(rest.pass_on(proj0)))

    def layer1_params(saved0, act0):
        return _layer_params(1, w, later.finish(act0), small["conv_w"][1], small["b_merge"][1])

    params0 = _layer_params_first(0, w, w_in0, small["conv_w"][0], small["b_merge"][0])
    params = [dict(params0, late=rest_of_layer0, after_attn=later.pass_on), layer1_params]
    others = _SHARDED_MM[1:]
    rs1 = _ReduceScatter("rs_l1_", _SHARDED_MM)
    rs0_rest, rs0_w_in = _ReduceScatter("rs_l0_rest_", others), _ReduceScatter("rs_l0_w_in_", ("w_in",))

    def layer0_grads_done(grads):
        return [rs0_rest.scatter([grads["w_in_aligned"]]), rs0_w_in.exchange(grads)]

    hooks = [dict(on_rest_grads=rs0_rest.exchange, on_grads=layer0_grads_done),
             dict(on_grads=lambda grads: [rs1.exchange(grads)], after_layer=lambda dy: rs1.scatter([dy]))]
    sq, grad_x, layer_grads = _forward_backward(x[0], mem[0], positions[0], loss_target[0], params, hooks)

    g, sq_total = _all_reduce_small({n: jnp.stack([layer_grads[l][n] for l in range(DEPTH)]) for n in _ALL_REDUCED}, sq)
    loss = 0.5 / D * jnp.sum(sq_total)
    scattering = rs0_w_in.scatter([grad_x, g["norm_g"]])
    delta, new_m, new_v = _adamw_small(w, g, m, v, scattering)
    shard_grads = {1: rs1.finish([scattering]), 0: rs0_rest.finish([scattering])}
    as3d = lambda a: a.reshape(DEPTH, -1, a.shape[-1])
    as2d = lambda a: a.reshape(-1, a.shape[-1])
    big = lambda t: [as3d(t[n]) for n in others]
    turned = lambda t: [jnp.swapaxes(t["w_in"], 1, 2)]
    assert W_IN_SHARD % (8 * 7) == 0

    def update_w_in(l, grad, prev):
        return _adamw_layer_call(l, turned(w), [grad.T], turned(m), turned(v), prev, [], "adamw_w_in_l%d" % l, steps=7)

    def update_others(l, prev):
        return _adamw_layer_call(l, big(w), [as2d(shard_grads[l][n]) for n in others], big(m), big(v), prev, [], "adamw_l%d" % l)

    upd_in1 = update_w_in(1, shard_grads[1]["w_in"], None)
    upd = update_others(0, update_others(1, None))
    w_in_grad0 = rs0_w_in.finish([grad_x, upd_in1[0], upd[0], delta["norm_g"]])["w_in"]
    upd_in = update_w_in(0, w_in_grad0, upd_in1)
    g["w_in"], delta["w_in"], new_m["w_in"], new_v["w_in"] = [jnp.swapaxes(a, 1, 2) for a in upd_in]
    for t, n in enumerate(others):
        g[n], delta[n], new_m[n], new_v[n] = [a.reshape(w[n].shape) for a in upd[4 * t:4 * t + 4]]
    return (loss, grad_x[None], *[g[n] for n in _WEIGHTS], *[delta[n] for n in _WEIGHTS],
            *[new_m[n] for n in _WEIGHTS], *[new_v[n] for n in _WEIGHTS])
```

```python
import functools
import math

import jax
import jax.numpy as jnp
from jax import lax
from jax.experimental import pallas as pl
from jax.experimental.pallas import tpu as pltpu

F32 = jnp.float32
MM = jnp.bfloat16

D = 1024
DEPTH = 2
EPS = 1e-6
H = 8
NOPE = 64
ROPE = 32
QKH = 96
VH = 64
QL = 256
KVL = 128
ROPE_THETA = 10000.0
CW = 512
SGW = 512
SGG = 4
SGC = 128
MH = 4
MHD = 128
NB = 4
BW = 512
NEG_INF = -1e30
LANES = 128
N_CHIPS = 4

R_CQ, R_CKV, R_KR, R_CV, R_SGI, R_MQ, R_SG, R_ML, R_END = 0, 256, 384, 416, 1952, 2976, 3488, 5536, 9632
OFF_ML, OFF_SG, OFF_CV, OFF_SGI, OFF_MQ, OFF_CQ, OFF_CKV, OFF_KR, NP = 0, 4096, 6144, 7680, 8704, 9216, 9472, 9600, 9728

ADAM_LR = 0.001
ADAM_B1 = 0.9
ADAM_B2 = 0.999
ADAM_EPS = 1e-08
ADAM_WD = 0.01
ADAM_STEP = 10

VMEM_LIMIT = 56 * 1024 * 1024
PACK_W = 512
MESH_ID = pl.DeviceIdType.MESH


def _cparams(n_axes):
    return pltpu.CompilerParams(dimension_semantics=("arbitrary",) * n_axes, vmem_limit_bytes=VMEM_LIMIT)


def _bs(shape, imap):
    return pl.BlockSpec(shape, imap)


@jax.custom_vjp
def _mm_plain(a, b):
    return jnp.dot(a.astype(MM), b.astype(MM), preferred_element_type=F32)


def _mm_plain_fwd(a, b):
    return _mm_plain(a, b), (a, b)


def _mm_plain_bwd(res, g):
    a, b = res
    gm = g.astype(MM)
    da = lax.dot_general(gm, b.astype(MM), (((1,), (1,)), ((), ())), preferred_element_type=F32)
    db = lax.dot_general(a.astype(MM), gm, (((0,), (0,)), ((), ())), preferred_element_type=F32)
    return da.astype(a.dtype), db.astype(b.dtype)


_mm_plain.defvjp(_mm_plain_fwd, _mm_plain_bwd)


@jax.custom_vjp
def _mm_slot(a, w, slot):
    return jnp.dot(a.astype(MM), w.astype(MM), preferred_element_type=F32)


def _mm_slot_fwd(a, w, slot):
    return _mm_slot(a, w, slot), (a, w)


def _mm_slot_bwd(res, g):
    a, w = res
    gm = g.astype(MM)
    da = lax.dot_general(gm, w.astype(MM), (((1,), (1,)), ((), ())), preferred_element_type=F32)
    dw = lax.dot_general(a.astype(MM), gm, (((0,), (0,)), ((), ())), preferred_element_type=F32)
    return da.astype(a.dtype), jnp.zeros_like(w), dw


_mm_slot.defvjp(_mm_slot_fwd, _mm_slot_bwd)


def _mm(a, b):
    if isinstance(b, tuple):
        return _mm_slot(a, b[0], b[1])
    return _mm_plain(a, b)


def _with_slot(w):
    return (w, jnp.zeros(w.shape, F32))


@jax.custom_vjp
def _mm_nt(a, b):
    return lax.dot_general(a.astype(MM), b.astype(MM), (((1,), (1,)), ((), ())), preferred_element_type=F32)


def _mm_nt_fwd(a, b):
    return _mm_nt(a, b), (a, b)


def _mm_nt_bwd(res, g):
    a, b = res
    gm = g.astype(MM)
    da = jnp.dot(gm, b.astype(MM), preferred_element_type=F32)
    db = lax.dot_general(gm, a.astype(MM), (((0,), (0,)), ((), ())), preferred_element_type=F32)
    return da.astype(a.dtype), db.astype(b.dtype)


_mm_nt.defvjp(_mm_nt_fwd, _mm_nt_bwd)


@functools.partial(jax.custom_vjp, nondiff_argnums=(1,))
def _lane_roll(x, shift):
    return pltpu.roll(x, shift, 1)


def _lane_roll_fwd(x, shift):
    return pltpu.roll(x, shift, 1), None


def _lane_roll_bwd(shift, _, g):
    return (pltpu.roll(g, (LANES - shift) % LANES, 1),)


_lane_roll.defvjp(_lane_roll_fwd, _lane_roll_bwd)


def _rms_n(x, g, n):
    ms = jnp.sum(x * x, axis=-1, keepdims=True) * (1.0 / n)
    return x * lax.rsqrt(ms + EPS) * g


def _softmax(s):
    m = jnp.max(s, axis=-1, keepdims=True)
    e = jnp.exp(s - m)
    return e / jnp.sum(e, axis=-1, keepdims=True)


def _rope(t, cos_t, sin_a, sin_b):
    return t * cos_t + _lane_roll(t, LANES - 16) * sin_a + _lane_roll(t, 16) * sin_b


def _mla_prep_fn(cq, ckv, kr, cos_t, sin_a, sin_b, cq_g, ckv_g, qg, kg, wuq, wkn, wv):
    cqn = _rms_n(cq, cq_g, QL)
    ckvn = _rms_n(ckv, ckv_g, KVL)
    lane = lax.broadcasted_iota(jnp.int32, kr.shape, 1)
    krm = jnp.where((lane >= NOPE) & (lane < QKH), kr, 0.0)
    qs, ks = [], []
    for h in range(H):
        qh = _rms_n(_mm(cqn, wuq[h]), qg, QKH)
        qs.append(_rope(qh, cos_t, sin_a, sin_b))
        kh = _rms_n(_mm(ckvn, wkn[h]) + krm, kg, QKH)
        ks.append(_rope(kh, cos_t, sin_a, sin_b))
    return jnp.concatenate(qs, axis=-1), jnp.concatenate(ks, axis=-1), _mm(ckvn, wv)


def _dot_nt(a, b):
    return lax.dot_general(a.astype(MM), b.astype(MM), (((1,), (1,)), ((), ())), preferred_element_type=F32)


def _dot_tn(a, b):
    return lax.dot_general(a.astype(MM), b.astype(MM), (((0,), (0,)), ((), ())), preferred_element_type=F32)


def _causal_scores(qe, ke):
    tq, kl = qe.shape[0], ke.shape[0]
    s = _dot_nt(qe, ke) * (QKH ** -0.5)
    rows = lax.broadcasted_iota(jnp.int32, (tq, tq), 0)
    cols = lax.broadcasted_iota(jnp.int32, (tq, tq), 1)
    own = jnp.where(cols <= rows, s[:, kl - tq:], NEG_INF)
    return own if kl == tq else jnp.concatenate([s[:, :kl - tq], own], axis=1)


def _head_lanes(e, shape):
    lane = lax.broadcasted_iota(jnp.int32, shape, len(shape) - 1)
    return (lane >= VH * e) & (lane < VH * (e + 1))


def _attn_pair_fwd(q2, k2, v2):
    tq = q2.shape[0]
    o = jnp.zeros((tq, LANES), F32)
    lse = jnp.zeros((tq, LANES), F32)
    for e in range(2):
        sl = slice(LANES * e, LANES * (e + 1))
        s = _causal_scores(q2[:, sl], k2[:, sl])
        m = jnp.max(s, axis=-1, keepdims=True)
        ex = jnp.exp(s - m)
        l = jnp.sum(ex, axis=-1, keepdims=True)
        ve = jnp.where(_head_lanes(e, v2[:, sl].shape), v2[:, sl], 0.0)
        o = o + jnp.dot((ex * (1.0 / l)).astype(MM), ve.astype(MM), preferred_element_type=F32)
        lse = jnp.where(_head_lanes(e, lse.shape), m + jnp.log(l), lse)
    return o, lse


def _attn_pair_bwd(q2, k2, v2, sg, dys, o, lse):
    sig = jax.nn.sigmoid(sg)
    do = dys * (sg * sig)
    dsg = dys * o * (sig * (1.0 + sg * (1.0 - sig)))
    dqs, dks, dvs = [], [], []
    for e in range(2):
        sl = slice(LANES * e, LANES * (e + 1))
        qe, ke = q2[:, sl], k2[:, sl]
        hm = _head_lanes(e, o.shape)
        lse_e = jnp.max(jnp.where(hm, lse, NEG_INF), axis=-1, keepdims=True)
        do_e = jnp.where(hm, do, 0.0)
        delta = jnp.sum(do_e * o, axis=-1, keepdims=True)
        p = jnp.exp(_causal_scores(qe, ke) - lse_e)
        ve = jnp.where(_head_lanes(e, v2[:, sl].shape), v2[:, sl], 0.0)
        dvs.append(_dot_tn(p, do_e))
        ds = (p * (_dot_nt(do_e, ve) - delta)) * (QKH ** -0.5)
        dqs.append(jnp.dot(ds.astype(MM), ke.astype(MM), preferred_element_type=F32))
        dks.append(_dot_tn(ds, qe))
    return jnp.concatenate(dqs, axis=-1), jnp.concatenate(dks, axis=-1), jnp.concatenate(dvs, axis=-1), dsg


def _sg_fn(u, v, sgc, ln_g, ln_b, ws, bs):
    mu = jnp.mean(v, axis=-1, keepdims=True)
    xc = v - mu
    vn = xc * lax.rsqrt(jnp.mean(xc * xc, axis=-1, keepdims=True) + EPS) * ln_g + ln_b
    r = lax.broadcasted_iota(jnp.int32, (SGC, SGC), 0)
    c = lax.broadcasted_iota(jnp.int32, (SGC, SGC), 1)
    wt = [jnp.where(r >= c, w, 0.0) for w in ws]
    row_blocks = []
    for ch in range(u.shape[0] // SGC):
        col_blocks = []
        for g in range(SGG):
            blk = vn[SGC * ch:SGC * (ch + 1), LANES * g:LANES * (g + 1)]
            col_blocks.append(_mm(wt[g], blk) + bs[g])
        row_blocks.append(jnp.concatenate(col_blocks, axis=-1))
    mixed = jnp.concatenate(row_blocks, axis=0)
    return (u * mixed) * jax.nn.silu(sgc)


def _memkv_fn(mem, mem_g, wm, kg):
    kv = _mm(_rms_n(mem, mem_g, D), wm)
    ks = [_rms_n(kv[:, MHD * h:MHD * (h + 1)], kg, MHD) for h in range(MH)]
    return jnp.concatenate(ks, axis=-1), kv[:, MH * MHD:]


def _mem_fn(mq, sgd, k, v, qg):
    outs = []
    for h in range(MH):
        sl = slice(MHD * h, MHD * (h + 1))
        qh = _rms_n(mq[:, sl], qg, MHD)
        p = _softmax(_mm_nt(qh, k[:, sl]) * (MHD ** -0.5))
        outs.append(_mm(p, v[:, sl]))
    return jnp.concatenate(outs, axis=-1) * jax.nn.silu(sgd)


def _merge_fn(ys, logits, bm, wb, wo):
    merged = None
    for n in range(NB):
        z = jnp.concatenate([_mm(ys[n], wb[j][n]) for j in range(N_CHIPS)], axis=-1)
        gate = jax.nn.sigmoid(logits[:, D * n:D * (n + 1)] + bm[n])
        merged = gate * z if merged is None else merged + gate * z
    return _mm(merged, wo)


def _proj_call(x, g, w):
    s_len = x.shape[0]
    tm, tn = min(s_len, 1024), NP // 4

    def body(x_ref, g_ref, w_ref, p_ref, h_ref):
        @pl.when(pl.program_id(1) == 0)
        def _():
            h_ref[...] = _rms_n(x_ref[...], g_ref[...], D).astype(h_ref.dtype)
        p_ref[...] = jnp.dot(h_ref[...], w_ref[...], preferred_element_type=F32)

    return pl.pallas_call(
        body, grid=(s_len // tm, NP // tn),
        in_specs=[_bs((tm, D), lambda i, j: (i, 0)), _bs((1, D), lambda i, j: (0, 0)), _bs((D, tn), lambda i, j: (0, j))],
        out_specs=[_bs((tm, tn), lambda i, j: (i, j)), _bs((tm, D), lambda i, j: (i, 0))],
        out_shape=[jax.ShapeDtypeStruct((s_len, NP), F32), jax.ShapeDtypeStruct((s_len, D), MM)],
        name="proj", compiler_params=_cparams(2))(x, g, w)


def _rope_tables(pos):
    half = ROPE // 2
    inv_freq = ROPE_THETA ** (-jnp.arange(half, dtype=F32) / half)
    ang = pos.astype(F32)[:, None] * inv_freq
    cos, sin = jnp.cos(ang), jnp.sin(ang)
    s_len = pos.shape[0]
    z = lambda n: jnp.zeros((s_len, n), F32)
    cos_t = jnp.concatenate([jnp.ones((s_len, NOPE), F32), cos, cos, z(LANES - QKH)], axis=1)
    sin_a = jnp.concatenate([z(NOPE), -sin, z(LANES - NOPE - half)], axis=1)
    sin_b = jnp.concatenate([z(NOPE + half), sin, z(LANES - QKH)], axis=1)
    return cos_t, sin_a, sin_b


def _mla_prep_specs(tm):
    row = lambda w, off: _bs((tm, w), lambda i: (i, off // w))
    full2 = lambda a, b: _bs((a, b), lambda i: (0, 0))
    full3 = lambda a, b, c: _bs((a, b, c), lambda i: (0, 0, 0))
    tab = _bs((tm, LANES), lambda i: (i, 0))
    return [row(QL, OFF_CQ), row(KVL, OFF_CKV), row(LANES, OFF_KR), tab, tab, tab,
            full2(1, QL), full2(1, KVL), full2(1, LANES), full2(1, LANES),
            full3(H, QL, LANES), full3(H, KVL, LANES), full2(KVL, H * LANES)]


def _mla_prep_args(body_refs, wrap=lambda w: w):
    (cq, ckv, kr, ct, sa, sb, cqg, ckvg, qg, kg, wuq, wkn, wv) = body_refs
    return (cq[...], ckv[...], kr[...], ct[...], sa[...], sb[...], cqg[...], ckvg[...], qg[...], kg[...],
            [wrap(wuq[h]) for h in range(H)], [wrap(wkn[h]) for h in range(H)], wrap(wv[...]))


def _mla_prep_call(proj, tabs, cq_g, ckv_g, qg, kg, wuq, wkn, wv):
    s_len = proj.shape[0]
    tm = min(s_len, 256)

    def body(*refs):
        q_ref, k_ref, v_ref = refs[13:]
        q, k, v = _mla_prep_fn(*_mla_prep_args(refs[:13]))
        q_ref[...] = q.astype(q_ref.dtype)
        k_ref[...] = k.astype(k_ref.dtype)
        v_ref[...] = v.astype(v_ref.dtype)

    out = _bs((tm, H * LANES), lambda i: (i, 0))
    return pl.pallas_call(
        body, grid=(s_len // tm,), in_specs=_mla_prep_specs(tm), out_specs=[out, out, out],
        out_shape=[jax.ShapeDtypeStruct((s_len, H * LANES), MM)] * 3,
        name="mla_prep", compiler_params=_cparams(1))(proj, proj, proj, *tabs, cq_g, ckv_g, qg, kg, wuq, wkn, wv)


def _mla_prep_bwd_call(proj, tabs, cq_g, ckv_g, qg, kg, wuq, wkn, wv, dq, dk, dv):
    s_len = proj.shape[0]
    tm = min(s_len, 256)

    def body(*refs):
        dq_ref, dk_ref, dv_ref = refs[13:16]
        dcq_ref, dckv_ref, dkr_ref, dcqg_ref, dckvg_ref, dqg_ref, dkg_ref, dwuq_ref, dwkn_ref, dwv_ref = refs[16:]
        _, vjp = jax.vjp(_mla_prep_fn, *_mla_prep_args(refs[:13], _with_slot))
        (dcq, dckv, dkr, _, _, _, dcqg, dckvg, dqg, dkg, dwuq, dwkn, dwv) = vjp((dq_ref[...], dk_ref[...], dv_ref[...]))
        dwuq, dwkn, dwv = [d[1] for d in dwuq], [d[1] for d in dwkn], dwv[1]
        dcq_ref[...] = dcq.astype(dcq_ref.dtype)
        dckv_ref[...] = dckv.astype(dckv_ref.dtype)
        dkr_ref[...] = dkr.astype(dkr_ref.dtype)

        @pl.when(pl.program_id(0) == 0)
        def _():
            for r in (dcqg_ref, dckvg_ref, dqg_ref, dkg_ref, dwuq_ref, dwkn_ref, dwv_ref):
                r[...] = jnp.zeros_like(r)
        dcqg_ref[...] += dcqg
        dckvg_ref[...] += dckvg
        dqg_ref[...] += dqg
        dkg_ref[...] += dkg
        for h in range(H):
            dwuq_ref[h] += dwuq[h]
            dwkn_ref[h] += dwkn[h]
        dwv_ref[...] += dwv

    big = _bs((tm, H * LANES), lambda i: (i, 0))
    row = lambda w: _bs((tm, w), lambda i: (i, 0))
    full2 = lambda a, b: _bs((a, b), lambda i: (0, 0))
    full3 = lambda a, b, c: _bs((a, b, c), lambda i: (0, 0, 0))
    sd = jax.ShapeDtypeStruct
    return pl.pallas_call(
        body, grid=(s_len // tm,), in_specs=_mla_prep_specs(tm) + [big, big, big],
        out_specs=[row(QL), row(KVL), row(LANES), full2(1, QL), full2(1, KVL), full2(1, LANES), full2(1, LANES),
                   full3(H, QL, LANES), full3(H, KVL, LANES), full2(KVL, H * LANES)],
        out_shape=[sd((s_len, QL), MM), sd((s_len, KVL), MM), sd((s_len, LANES), MM), sd((1, QL), F32), sd((1, KVL), F32),
                   sd((1, LANES), F32), sd((1, LANES), F32), sd((H, QL, LANES), F32), sd((H, KVL, LANES), F32),
                   sd((KVL, H * LANES), F32)],
        name="mla_prep_bwd", compiler_params=_cparams(1))(proj, proj, proj, *tabs, cq_g, ckv_g, qg, kg, wuq, wkn, wv, dq, dk, dv)


def _attn_specs(s_len, tq):
    pair = 2 * LANES
    return [_bs((tq, pair), lambda p, i: (i, p)), _bs((s_len, pair), lambda p, i: (0, p)), _bs((s_len, pair), lambda p, i: (0, p)),
            _bs((tq, LANES), lambda p, i: (i, OFF_SG // LANES + p))]


def _attn_call(q, k, v, proj):
    s_len = q.shape[0]
    tq = min(s_len, 256)

    def body(q_ref, k_ref, v_ref, sg_ref, y_ref, o_ref, lse_ref):
        for n in range(s_len // tq):
            @pl.when(pl.program_id(1) == n)
            def _():
                kl = (n + 1) * tq
                o, lse = _attn_pair_fwd(q_ref[...], k_ref[:kl, :], v_ref[:kl, :])
                y_ref[...] = (o * jax.nn.silu(sg_ref[...])).astype(y_ref.dtype)
                o_ref[...] = o
                lse_ref[...] = lse

    tile = _bs((tq, LANES), lambda p, i: (i, p))
    sd = jax.ShapeDtypeStruct
    return pl.pallas_call(
        body, grid=(H // 2, s_len // tq), in_specs=_attn_specs(s_len, tq), out_specs=[tile, tile, tile],
        out_shape=[sd((s_len, BW), MM), sd((s_len, BW), F32), sd((s_len, BW), F32)],
        name="attn", compiler_params=_cparams(2))(q, k, v, proj)


def _attn_bwd_call(q, k, v, proj, dys, o, lse):
    s_len = q.shape[0]
    tq = min(s_len, 256)
    pair = 2 * LANES

    def body(q_ref, k_ref, v_ref, sg_ref, dy_ref, o_ref, lse_ref, dq_ref, dk_ref, dv_ref, dsg_ref):
        i = pl.program_id(1)

        @pl.when(i == 0)
        def _():
            dk_ref[...] = jnp.zeros_like(dk_ref)
            dv_ref[...] = jnp.zeros_like(dv_ref)

        for n in range(s_len // tq):
            @pl.when(i == n)
            def _():
                kl = (n + 1) * tq
                dq, dk, dv, dsg = _attn_pair_bwd(q_ref[...], k_ref[:kl, :], v_ref[:kl, :], sg_ref[...], dy_ref[...],
                                                 o_ref[...], lse_ref[...])
                dq_ref[...] = dq
                dsg_ref[...] = dsg.astype(dsg_ref.dtype)
                dk_ref[:kl, :] += dk
                dv_ref[:kl, :] += dv

    sd = jax.ShapeDtypeStruct
    tile = _bs((tq, LANES), lambda p, i: (i, p))
    return pl.pallas_call(
        body, grid=(H // 2, s_len // tq),
        in_specs=_attn_specs(s_len, tq) + [tile, tile, tile],
        out_specs=[_bs((tq, pair), lambda p, i: (i, p)), _bs((s_len, pair), lambda p, i: (0, p)),
                   _bs((s_len, pair), lambda p, i: (0, p)), tile],
        out_shape=[sd((s_len, H * LANES), F32), sd((s_len, H * LANES), F32), sd((s_len, H * LANES), F32), sd((s_len, BW), MM)],
        name="attn_bwd", compiler_params=_cparams(2))(q, k, v, proj, dys, o, lse)


def _shift_down(a, n):
    r = lax.broadcasted_iota(jnp.int32, a.shape, 0)
    return jnp.where(r >= n, pltpu.roll(a, n, 0), 0.0)


def _shift_up(a, n):
    s_len = a.shape[0]
    r = lax.broadcasted_iota(jnp.int32, a.shape, 0)
    return jnp.where(r < s_len - n, pltpu.roll(a, s_len - n, 0), 0.0)


def _conv_specs(s_len):
    col = lambda off: _bs((s_len, LANES), lambda j: (0, off // LANES + j))
    return [col(OFF_CV), col(OFF_CV + CW), col(OFF_CV + 2 * CW), col(OFF_SG + BW),
            _bs((3, LANES), lambda j: (0, j)), _bs((1, LANES), lambda j: (0, j))]


def _conv_call(proj, cw, cb):
    s_len = proj.shape[0]

    def body(bg_ref, cg_ref, xi_ref, sg_ref, w_ref, b_ref, y_ref):
        z = cg_ref[...] * xi_ref[...]
        y = b_ref[...] + w_ref[0:1, :] * _shift_down(z, 2)
        y = y + w_ref[1:2, :] * _shift_down(z, 1)
        y = y + w_ref[2:3, :] * z
        y_ref[...] = ((bg_ref[...] * y) * jax.nn.silu(sg_ref[...])).astype(y_ref.dtype)

    return pl.pallas_call(
        body, grid=(CW // LANES,), in_specs=_conv_specs(s_len), out_specs=_bs((s_len, LANES), lambda j: (0, j)),
        out_shape=jax.ShapeDtypeStruct((s_len, CW), MM), name="conv", compiler_params=_cparams(1))(proj, proj, proj, proj, cw, cb)


def _conv_bwd_call(proj, cw, cb, dys):
    s_len = proj.shape[0]

    def body(bg_ref, cg_ref, xi_ref, sg_ref, w_ref, b_ref, dys_ref, dbg_ref, dcg_ref, dxi_ref, dsg_ref, dw_ref, db_ref):
        bg, cg, xi, sg = bg_ref[...], cg_ref[...], xi_ref[...], sg_ref[...]
        w0, w1, w2 = w_ref[0:1, :], w_ref[1:2, :], w_ref[2:3, :]
        z = cg * xi
        z1, z2 = _shift_down(z, 1), _shift_down(z, 2)
        y = b_ref[...] + w0 * z2
        y = y + w1 * z1
        y = y + w2 * z
        yb = bg * y
        sig = jax.nn.sigmoid(sg)
        silu = sg * sig
        dys_v = dys_ref[...]
        dsg_ref[...] = (dys_v * yb * (sig * (1.0 + sg * (1.0 - sig)))).astype(dsg_ref.dtype)
        dyb = dys_v * silu
        dbg_ref[...] = (dyb * y).astype(dbg_ref.dtype)
        dy = dyb * bg
        db_ref[...] = jnp.sum(dy, axis=0, keepdims=True)
        dw_ref[0:1, :] = jnp.sum(dy * z2, axis=0, keepdims=True)
        dw_ref[1:2, :] = jnp.sum(dy * z1, axis=0, keepdims=True)
        dw_ref[2:3, :] = jnp.sum(dy * z, axis=0, keepdims=True)
        dz = w2 * dy + w1 * _shift_up(dy, 1) + w0 * _shift_up(dy, 2)
        dcg_ref[...] = (dz * xi).astype(dcg_ref.dtype)
        dxi_ref[...] = (dz * cg).astype(dxi_ref.dtype)

    col = _bs((s_len, LANES), lambda j: (0, j))
    sd = jax.ShapeDtypeStruct
    return pl.pallas_call(
        body, grid=(CW // LANES,), in_specs=_conv_specs(s_len) + [col],
        out_specs=[col, col, col, col, _bs((3, LANES), lambda j: (0, j)), _bs((1, LANES), lambda j: (0, j))],
        out_shape=[sd((s_len, CW), MM)] * 4 + [sd((3, CW), F32), sd((1, CW), F32)],
        name="conv_bwd", compiler_params=_cparams(1))(proj, proj, proj, proj, cw, cb, dys)


def _sg_specs(tm):
    row = lambda off: _bs((tm, SGW), lambda i: (i, off // SGW))
    return [row(OFF_SGI), row(OFF_SGI + SGW), row(OFF_SG + 2 * BW), _bs((1, SGW), lambda i: (0, 0)), _bs((1, SGW), lambda i: (0, 0)),
            _bs((SGG, SGC, SGC), lambda i: (0, 0, 0)), _bs((SGG, SGC, 1), lambda i: (0, 0, 0))]


def _sg_args(refs):
    u, v, sg, lg, lb, ws, bs = refs
    return (u[...], v[...], sg[...], lg[...], lb[...], [ws[g] for g in range(SGG)], [bs[g] for g in range(SGG)])


def _sg_call(proj, ln_g, ln_b, ws, bs):
    s_len = proj.shape[0]
    tm = min(s_len, 256)

    def body(*refs):
        refs[7][...] = _sg_fn(*_sg_args(refs[:7])).astype(refs[7].dtype)

    return pl.pallas_call(
        body, grid=(s_len // tm,), in_specs=_sg_specs(tm), out_specs=_bs((tm, SGW), lambda i: (i, 0)),
        out_shape=jax.ShapeDtypeStruct((s_len, SGW), MM), name="sgmlp", compiler_params=_cparams(1))(proj, proj, proj, ln_g, ln_b, ws, bs)


def _sg_bwd_call(proj, ln_g, ln_b, ws, bs, dys):
    s_len = proj.shape[0]
    tm = min(s_len, 256)

    def body(*refs):
        dys_ref = refs[7]
        du_ref, dv_ref, dsg_ref, dlg_ref, dlb_ref, dws_ref, dbs_ref = refs[8:]
        _, vjp = jax.vjp(_sg_fn, *_sg_args(refs[:7]))
        du, dv, dsg, dlg, dlb, dws, dbs = vjp(dys_ref[...])
        du_ref[...] = du.astype(du_ref.dtype)
        dv_ref[...] = dv.astype(dv_ref.dtype)
        dsg_ref[...] = dsg.astype(dsg_ref.dtype)

        @pl.when(pl.program_id(0) == 0)
        def _():
            for r in (dlg_ref, dlb_ref, dws_ref, dbs_ref):
                r[...] = jnp.zeros_like(r)
        dlg_ref[...] += dlg
        dlb_ref[...] += dlb
        for g in range(SGG):
            dws_ref[g] += dws[g]
            dbs_ref[g] += dbs[g]

    row = _bs((tm, SGW), lambda i: (i, 0))
    sd = jax.ShapeDtypeStruct
    return pl.pallas_call(
        body, grid=(s_len // tm,), in_specs=_sg_specs(tm) + [row],
        out_specs=[row, row, row, _bs((1, SGW), lambda i: (0, 0)), _bs((1, SGW), lambda i: (0, 0)),
                   _bs((SGG, SGC, SGC), lambda i: (0, 0, 0)), _bs((SGG, SGC, 1), lambda i: (0, 0, 0))],
        out_shape=[sd((s_len, SGW), MM)] * 3 + [sd((1, SGW), F32), sd((1, SGW), F32), sd((SGG, SGC, SGC), F32), sd((SGG, SGC, 1), F32)],
        name="sgmlp_bwd", compiler_params=_cparams(1))(proj, proj, proj, ln_g, ln_b, ws, bs, dys)


def _memkv_call(mem, mem_g, wm, kg):
    m_len = mem.shape[0]

    def body(mem_ref, g_ref, w_ref, kg_ref, k_ref, v_ref):
        k, v = _memkv_fn(mem_ref[...], g_ref[...], w_ref[...], kg_ref[...])
        k_ref[...] = k.astype(k_ref.dtype)
        v_ref[...] = v.astype(v_ref.dtype)

    return pl.pallas_call(body, out_shape=[jax.ShapeDtypeStruct((m_len, MH * MHD), MM)] * 2, name="memkv",
                          compiler_params=pltpu.CompilerParams(vmem_limit_bytes=VMEM_LIMIT))(mem, mem_g, wm, kg)


def _memkv_bwd_call(mem, mem_g, wm, kg, dk, dv):
    def body(mem_ref, g_ref, w_ref, kg_ref, dk_ref, dv_ref, dg_ref, dw_ref, dkg_ref):
        _, vjp = jax.vjp(_memkv_fn, mem_ref[...], g_ref[...], _with_slot(w_ref[...]), kg_ref[...])
        _, dg, dw, dkg = vjp((dk_ref[...], dv_ref[...]))
        dg_ref[...] = dg
        dw_ref[...] = dw[1]
        dkg_ref[...] = dkg

    sd = jax.ShapeDtypeStruct
    return pl.pallas_call(body, out_shape=[sd((1, D), F32), sd((D, 2 * MH * MHD), F32), sd((1, MHD), F32)], name="memkv_bwd",
                          compiler_params=pltpu.CompilerParams(vmem_limit_bytes=VMEM_LIMIT))(mem, mem_g, wm, kg, dk, dv)


def _mem_specs(tm, m_len):
    w = MH * MHD
    return [_bs((tm, w), lambda i: (i, OFF_MQ // w)), _bs((tm, BW), lambda i: (i, (OFF_SG + 3 * BW) // BW)),
            _bs((m_len, w), lambda i: (0, 0)), _bs((m_len, w), lambda i: (0, 0)), _bs((1, MHD), lambda i: (0, 0))]


def _mem_call(proj, k, v, qg):
    s_len, m_len = proj.shape[0], k.shape[0]
    tm = min(s_len, 256)

    def body(mq_ref, sg_ref, k_ref, v_ref, qg_ref, y_ref):
        y_ref[...] = _mem_fn(mq_ref[...], sg_ref[...], k_ref[...], v_ref[...], qg_ref[...]).astype(y_ref.dtype)

    return pl.pallas_call(
        body, grid=(s_len // tm,), in_specs=_mem_specs(tm, m_len), out_specs=_bs((tm, BW), lambda i: (i, 0)),
        out_shape=jax.ShapeDtypeStruct((s_len, BW), MM), name="memattn", compiler_params=_cparams(1))(proj, proj, k, v, qg)


def _mem_bwd_call(proj, k, v, qg, dys):
    s_len, m_len = proj.shape[0], k.shape[0]
    tm = min(s_len, 256)
    w = MH * MHD

    def body(mq_ref, sg_ref, k_ref, v_ref, qg_ref, dys_ref, dmq_ref, dsg_ref, dk_ref, dv_ref, dqg_ref):
        _, vjp = jax.vjp(_mem_fn, mq_ref[...], sg_ref[...], k_ref[...].astype(F32), v_ref[...].astype(F32), qg_ref[...])
        dmq, dsg, dk, dv, dqg = vjp(dys_ref[...])
        dmq_ref[...] = dmq.astype(dmq_ref.dtype)
        dsg_ref[...] = dsg.astype(dsg_ref.dtype)

        @pl.when(pl.program_id(0) == 0)
        def _():
            for r in (dk_ref, dv_ref, dqg_ref):
                r[...] = jnp.zeros_like(r)
        dk_ref[...] += dk
        dv_ref[...] += dv
        dqg_ref[...] += dqg

    row = _bs((tm, BW), lambda i: (i, 0))
    kv = _bs((m_len, w), lambda i: (0, 0))
    sd = jax.ShapeDtypeStruct
    return pl.pallas_call(
        body, grid=(s_len // tm,), in_specs=_mem_specs(tm, m_len) + [row],
        out_specs=[row, row, kv, kv, _bs((1, MHD), lambda i: (0, 0))],
        out_shape=[sd((s_len, w), MM), sd((s_len, BW), MM), sd((m_len, w), F32), sd((m_len, w), F32), sd((1, MHD), F32)],
        name="memattn_bwd", compiler_params=_cparams(1))(proj, proj, k, v, qg, dys)


def _merge_specs(tm):
    row = _bs((tm, BW), lambda i: (i, 0))
    return [row, row, row, row, _bs((tm, NB * D), lambda i: (i, OFF_ML // (NB * D))), _bs((NB, D), lambda i: (0, 0)),
            _bs((N_CHIPS, NB, BW, D // N_CHIPS), lambda i: (0, 0, 0, 0)), _bs((D, D), lambda i: (0, 0))]


def _merge_call(ys, proj, bm, wb, wo, x):
    s_len = proj.shape[0]
    tm = min(s_len, 256)

    def body(ya, yb, yc, yd, lg_ref, bm_ref, wb_ref, wo_ref, x_ref, o_ref):
        out = _merge_fn([r[...] for r in (ya, yb, yc, yd)], lg_ref[...], [bm_ref[n:n + 1, :] for n in range(NB)],
                        [[wb_ref[j, n] for n in range(NB)] for j in range(N_CHIPS)], wo_ref[...])
        o_ref[...] = x_ref[...] + out

    xrow = _bs((tm, D), lambda i: (i, 0))
    return pl.pallas_call(
        body, grid=(s_len // tm,), in_specs=_merge_specs(tm) + [xrow], out_specs=xrow,
        out_shape=jax.ShapeDtypeStruct((s_len, D), F32), name="merge", compiler_params=_cparams(1))(*ys, proj, bm, wb, wo, x)


def _merge_bwd_call(ys, proj, bm, wb, wo, dout):
    s_len = proj.shape[0]
    tm = min(s_len, 256)

    def body(ya, yb, yc, yd, lg_ref, bm_ref, wb_ref, wo_ref, do_ref, dya, dyb, dyc, dyd, dlg_ref, dbm_ref, dwb_ref, dwo_ref):
        fn = lambda ys_, lg_, bm_, wb_, wo_: _merge_fn(ys_, lg_, bm_, wb_, wo_)
        _, vjp = jax.vjp(fn, [r[...].astype(F32) for r in (ya, yb, yc, yd)], lg_ref[...], [bm_ref[n:n + 1, :] for n in range(NB)],
                         [[_with_slot(wb_ref[j, n]) for n in range(NB)] for j in range(N_CHIPS)], _with_slot(wo_ref[...]))
        dys, dlg, dbm, dwb, dwo = vjp(do_ref[...])
        dwb, dwo = [[d[1] for d in row] for row in dwb], dwo[1]
        for r, d in zip((dya, dyb, dyc, dyd), dys):
            r[...] = d
        dlg_ref[...] = dlg.astype(dlg_ref.dtype)

        @pl.when(pl.program_id(0) == 0)
        def _():
            for r in (dbm_ref, dwb_ref, dwo_ref):
                r[...] = jnp.zeros_like(r)
        for n in range(NB):
            dbm_ref[n:n + 1, :] += dbm[n]
            for j in range(N_CHIPS):
                dwb_ref[j, n] += dwb[j][n]
        dwo_ref[...] += dwo

    row = _bs((tm, BW), lambda i: (i, 0))
    sd = jax.ShapeDtypeStruct
    wb_shape = (N_CHIPS, NB, BW, D // N_CHIPS)
    return pl.pallas_call(
        body, grid=(s_len // tm,), in_specs=_merge_specs(tm) + [_bs((tm, D), lambda i: (i, 0))],
        out_specs=[row, row, row, row, _bs((tm, NB * D), lambda i: (i, 0)), _bs((NB, D), lambda i: (0, 0)),
                   _bs(wb_shape, lambda i: (0, 0, 0, 0)), _bs((D, D), lambda i: (0, 0))],
        out_shape=[sd((s_len, BW), F32)] * 4 + [sd((s_len, NB * D), MM), sd((NB, D), F32), sd(wb_shape, F32), sd((D, D), F32)],
        name="merge_bwd", compiler_params=_cparams(1))(*ys, proj, bm, wb, wo, dout)


def _dh_call(dproj, w, x, g, dout, after=()):
    s_len = x.shape[0]
    tk = NP // 4
    after = list(after)

    def matmul_body(dp_ref, w_ref, *rest):
        o_ref = rest[-1]

        @pl.when(pl.program_id(0) == 0)
        def _():
            o_ref[...] = jnp.zeros_like(o_ref)
        o_ref[...] += lax.dot_general(dp_ref[...], w_ref[...], (((1,), (1,)), ((), ())), preferred_element_type=F32)

    dh = pl.pallas_call(
        matmul_body, grid=(NP // tk,),
        in_specs=[_bs((s_len, tk), lambda k: (0, k)), _bs((D, tk), lambda k: (0, k))] + [_ANY] * len(after),
        out_specs=_bs((s_len, D), lambda k: (0, 0)), out_shape=jax.ShapeDtypeStruct((s_len, D), F32),
        name="dh", compiler_params=_cparams(1))(dproj, w, *after)

    tm = min(s_len, 512)

    def norm_body(dh_ref, x_ref, g_ref, do_ref, dx_ref, dg_ref):
        _, vjp = jax.vjp(lambda x_, g_: _rms_n(x_, g_, D), x_ref[...], g_ref[...])
        dxr, dgr = vjp(dh_ref[...])
        dx_ref[...] = do_ref[...] + dxr

        @pl.when(pl.program_id(0) == 0)
        def _():
            dg_ref[...] = jnp.zeros_like(dg_ref)
        dg_ref[...] += dgr

    row = _bs((tm, D), lambda i: (i, 0))
    return pl.pallas_call(
        norm_body, grid=(s_len // tm,), in_specs=[row, row, _bs((1, D), lambda i: (0, 0)), row],
        out_specs=[row, _bs((1, D), lambda i: (0, 0))],
        out_shape=[jax.ShapeDtypeStruct((s_len, D), F32), jax.ShapeDtypeStruct((1, D), F32)],
        name="norm_bwd", compiler_params=_cparams(1))(dh, x, g, dout)


def _dw_call(h, dproj, after=()):
    s_len = h.shape[0]
    tn = 512
    after = list(after)

    def body(h_ref, dp_ref, *rest):
        rest[-1][...] = lax.dot_general(h_ref[...], dp_ref[...], (((0,), (0,)), ((), ())), preferred_element_type=F32)

    return pl.pallas_call(
        body, grid=(NP // tn,),
        in_specs=[_bs((s_len, D), lambda j: (0, 0)), _bs((s_len, tn), lambda j: (0, j))] + [_ANY] * len(after),
        out_specs=_bs((D, tn), lambda j: (0, j)), out_shape=jax.ShapeDtypeStruct((D, NP), F32),
        name="dw_in", compiler_params=_cparams(1))(h, dproj, *after)


def _loss_call(y, target):
    s_len = y.shape[0]
    tm = min(s_len, 512)

    def body(y_ref, t_ref, dy_ref, l_ref):
        e = y_ref[...] - t_ref[...]
        dy_ref[...] = e * (1.0 / D)

        @pl.when(pl.program_id(0) == 0)
        def _():
            l_ref[...] = jnp.zeros_like(l_ref)
        l_ref[...] += jnp.sum(e * e, axis=0, keepdims=True)

    row = _bs((tm, D), lambda i: (i, 0))
    return pl.pallas_call(
        body, grid=(s_len // tm,), in_specs=[row, row], out_specs=[row, _bs((1, D), lambda i: (0, 0))],
        out_shape=[jax.ShapeDtypeStruct((s_len, D), F32), jax.ShapeDtypeStruct((1, D), F32)],
        name="loss", compiler_params=_cparams(1))(y, target)


def _adamw_call(w, g, m, v, name):
    rows, cols = w.shape
    tr = min(_row_tile(rows), 128)

    def body(w_ref, g_ref, m_ref, v_ref, d_ref, nm_ref, nv_ref):
        gv = g_ref[...]
        m2 = ADAM_B1 * m_ref[...] + (1.0 - ADAM_B1) * gv
        v2 = ADAM_B2 * v_ref[...] + (1.0 - ADAM_B2) * (gv * gv)
        m_hat = m2 / (1.0 - ADAM_B1 ** ADAM_STEP)
        v_hat = v2 / (1.0 - ADAM_B2 ** ADAM_STEP)
        d_ref[...] = -ADAM_LR * (m_hat / (jnp.sqrt(v_hat) + ADAM_EPS) + ADAM_WD * w_ref[...])
        nm_ref[...] = m2
        nv_ref[...] = v2

    blk = _bs((tr, cols), lambda i: (i, 0))
    return pl.pallas_call(
        body, grid=(rows // tr,), in_specs=[blk] * 4, out_specs=[blk] * 3,
        out_shape=[jax.ShapeDtypeStruct((rows, cols), F32)] * 3, name=name, compiler_params=_cparams(1))(w, g, m, v)


def _adamw_layer_call(layer, ws, gs, ms, vs, prev, after, name, steps=8):
    n = len(ws)
    after = list(after)
    n_prev = 4 * n if prev is not None else 0

    def body(*refs):
        outs = refs[len(refs) - 4 * n:]
        for t in range(n):
            w_ref, g_ref, m_ref, v_ref = refs[t], refs[n + t], refs[2 * n + t], refs[3 * n + t]
            g_out, d_out, m_out, v_out = outs[4 * t:4 * t + 4]
            gv = g_ref[...]
            m2 = ADAM_B1 * m_ref[0] + (1.0 - ADAM_B1) * gv
            v2 = ADAM_B2 * v_ref[0] + (1.0 - ADAM_B2) * (gv * gv)
            m_hat = m2 / (1.0 - ADAM_B1 ** ADAM_STEP)
            v_hat = v2 / (1.0 - ADAM_B2 ** ADAM_STEP)
            g_out[0] = gv
            d_out[0] = -ADAM_LR * (m_hat / (jnp.sqrt(v_hat) + ADAM_EPS) + ADAM_WD * w_ref[0])
            m_out[0] = m2
            v_out[0] = v2

    def lay(a):
        return _bs((1, a.shape[1] // steps, a.shape[2]), lambda i: (layer, i, 0))

    in_specs = ([lay(a) for a in ws] + [_bs((g.shape[0] // steps, g.shape[1]), lambda i: (i, 0)) for g in gs]
                + [lay(a) for a in ms] + [lay(a) for a in vs] + [_ANY] * (n_prev + len(after)))
    return pl.pallas_call(
        body, grid=(steps,), in_specs=in_specs, out_specs=[lay(ws[t]) for t in range(n) for _ in range(4)],
        out_shape=[jax.ShapeDtypeStruct(ws[t].shape, F32) for t in range(n) for _ in range(4)],
        input_output_aliases={4 * n + q: q for q in range(n_prev)}, name=name, compiler_params=_cparams(1),
    )(*ws, *gs, *ms, *vs, *(prev if prev is not None else []), *after)


def _row_tile(rows):
    for cand in (512, 256, 128, 64, 32, 16, 8):
        if rows % cand == 0 and rows > cand:
            return cand
    return rows


def _pair_sum_call(grads, from_sibling, core, name):
    n = len(grads)

    def body(core_ref, *refs):
        for t in range(n):
            refs[2 * n + t][...] = (refs[t][...] + refs[n + t][...]).astype(MM)

    half = lambda g: (1, g.shape[1] // 2, g.shape[2])
    grid_spec = pltpu.PrefetchScalarGridSpec(
        num_scalar_prefetch=1, grid=(N_CHIPS,),
        in_specs=[pl.BlockSpec(half(g), lambda j, core_ref: (j, core_ref[0], 0)) for g in grads]
        + [pl.BlockSpec(half(g), lambda j, core_ref: (j, 0, 0)) for g in grads],
        out_specs=[pl.BlockSpec(half(g), lambda j, core_ref: (j, 0, 0)) for g in grads])
    return pl.pallas_call(
        body, grid_spec=grid_spec, out_shape=[jax.ShapeDtypeStruct((N_CHIPS,) + half(g)[1:], MM) for g in grads], name=name,
        compiler_params=_cparams(1))(core, *grads, *from_sibling)


def _owner_sum_call(chip_sums, from_chips, chip_core, name):
    n = len(chip_sums)
    steps = 4

    def body(ids_ref, *refs):
        for t in range(n):
            a, b = refs[t], refs[n + t]
            refs[2 * n + t][...] = ((a[0].astype(F32) + b[0].astype(F32)) + b[1].astype(F32)) + b[2].astype(F32)

    tile = lambda p: (p.shape[1] // steps, p.shape[2])
    grid_spec = pltpu.PrefetchScalarGridSpec(
        num_scalar_prefetch=1, grid=(steps,),
        in_specs=[pl.BlockSpec((1,) + tile(p), lambda i, ids_ref: (ids_ref[0], i, 0)) for p in chip_sums]
        + [pl.BlockSpec((3,) + tile(p), lambda i, ids_ref: (0, i, 0)) for p in chip_sums],
        out_specs=[pl.BlockSpec(tile(p), lambda i, ids_ref: (ids_ref[1] * steps + i, 0)) for p in chip_sums])
    return pl.pallas_call(
        body, grid_spec=grid_spec, out_shape=[jax.ShapeDtypeStruct((2 * p.shape[1], p.shape[2]), F32) for p in chip_sums],
        name=name, compiler_params=_cparams(1))(chip_core, *chip_sums, *from_chips)


def _sum8_call(parts):
    n, rows, cols = parts.shape
    tr = _row_tile(rows)

    def body(p_ref, o_ref):
        acc = p_ref[0]
        for k in range(1, n):
            acc = acc + p_ref[k]
        o_ref[...] = acc

    return pl.pallas_call(
        body, grid=(rows // tr,), in_specs=[_bs((n, tr, cols), lambda i: (0, i, 0))], out_specs=_bs((tr, cols), lambda i: (i, 0)),
        out_shape=jax.ShapeDtypeStruct((rows, cols), F32), name="sum_small_grads", compiler_params=_cparams(1))(parts)


_ANY = pl.BlockSpec(memory_space=pl.ANY)


def _half_rows(ref, lead, half, which):
    rows = pl.ds(pl.multiple_of(half * which, half), half)
    return ref.at[rows] if lead is None else ref.at[lead, rows]


_HBM = pl.BlockSpec(memory_space=pltpu.HBM)
_SEM = pl.BlockSpec(memory_space=pltpu.SEMAPHORE)
_ORDERED_EFFECT = pltpu.CompilerParams(has_side_effects=pltpu.SideEffectType.DATAFLOW_SIDE_EFFECTING)


_VMEM = pl.BlockSpec(memory_space=pltpu.VMEM)
_TOKEN = jax.ShapeDtypeStruct((8, LANES), F32)


def _in_hbm(a):
    return pltpu.with_memory_space_constraint(a, pltpu.HBM)


def _tie(small, token):
    return small + token[0:1, 0:1].reshape((1,) * small.ndim)


def _peer(k):
    x, y, c = lax.axis_index("x"), lax.axis_index("y"), lax.axis_index("c")
    bx, by, bc = (k >> 2) & 1, (k >> 1) & 1, k & 1
    return (x ^ bx if bx else x, y ^ by if by else y, c ^ bc if bc else c)


def _place_block_call(blk, index, name):
    rows, cols = blk.shape

    def body(idx_ref, b_ref, o_ref):
        o_ref[0] = b_ref[...]

    grid_spec = pltpu.PrefetchScalarGridSpec(
        num_scalar_prefetch=1, grid=(1,), in_specs=[pl.BlockSpec((rows, cols), lambda i, idx_ref: (0, 0))],
        out_specs=pl.BlockSpec((1, rows, cols), lambda i, idx_ref: (idx_ref[0], 0, 0)))
    return pl.pallas_call(body, grid_spec=grid_spec, out_shape=jax.ShapeDtypeStruct((8, rows, cols), blk.dtype), name=name,
                          compiler_params=_cparams(1))(index, blk)


def _small_gather_start_call(blk, buf, after, name):
    after = list(after)

    def body(*refs):
        b_ref, out_ref = refs[0], refs[2 + len(after)]
        send_sems, recv_sems, token = refs[3 + len(after):]
        x, y, c = lax.axis_index("x"), lax.axis_index("y"), lax.axis_index("c")
        for k in range(1, 8):
            pltpu.make_async_remote_copy(src_ref=b_ref, dst_ref=out_ref.at[4 * x + 2 * y + c], send_sem=send_sems.at[k - 1],
                                         recv_sem=recv_sems.at[k - 1], device_id=_peer(k), device_id_type=MESH_ID).start()
        token[...] = jnp.zeros_like(token)

    dma = pltpu.SemaphoreType.DMA
    return pl.pallas_call(
        body, out_shape=[pltpu.HBM(buf.shape, buf.dtype), dma((7,)), dma((7,)), _TOKEN],
        in_specs=[_HBM, _HBM] + [_ANY] * len(after), out_specs=[_HBM, _SEM, _SEM, _VMEM],
        input_output_aliases={1: 0}, name=name, compiler_params=_ORDERED_EFFECT)(_in_hbm(blk), _in_hbm(buf), *after)


def _small_gather_finish_call(blk, buf, send_sems, recv_sems, after, name):
    after = list(after)

    def body(*refs):
        b_ref, in_ref, send_ref, recv_ref = refs[:4]
        x, y, c = lax.axis_index("x"), lax.axis_index("y"), lax.axis_index("c")
        for k in range(1, 8):
            px, py, pc = _peer(k)
            pltpu.make_async_remote_copy(src_ref=b_ref, dst_ref=in_ref.at[4 * px + 2 * py + pc], send_sem=send_ref.at[k - 1],
                                         recv_sem=recv_ref.at[k - 1], device_id=(px, py, pc), device_id_type=MESH_ID).wait()

    return pl.pallas_call(
        body, out_shape=pltpu.HBM(buf.shape, buf.dtype), in_specs=[_HBM, _HBM, _SEM, _SEM] + [_ANY] * len(after),
        out_specs=_HBM, input_output_aliases={1: 0}, name=name, compiler_params=_ORDERED_EFFECT,
    )(_in_hbm(blk), buf, send_sems, recv_sems, *after)


def _pair_exchange_start_call(grads, name):
    n = len(grads)
    half = [g.shape[1] // 2 for g in grads]

    def body(*refs):
        srcs, outs = refs[:n], refs[n:2 * n]
        send_sems, recv_sems, token = refs[2 * n:]
        x, y, c = lax.axis_index("x"), lax.axis_index("y"), lax.axis_index("c")
        for t in range(n):
            pltpu.make_async_remote_copy(
                src_ref=srcs[t].at[:, pl.ds(pl.multiple_of(half[t] * (1 - c), half[t]), half[t])], dst_ref=outs[t],
                send_sem=send_sems.at[t], recv_sem=recv_sems.at[t], device_id=(x, y, 1 - c), device_id_type=MESH_ID).start()
        token[...] = jnp.zeros_like(token)

    dma = pltpu.SemaphoreType.DMA
    return pl.pallas_call(
        body, out_shape=[pltpu.HBM((g.shape[0], g.shape[1] // 2, g.shape[2]), g.dtype) for g in grads] + [dma((n,)), dma((n,)), _TOKEN],
        in_specs=[_HBM] * n, out_specs=[_HBM] * n + [_SEM, _SEM, _VMEM], name=name, compiler_params=_ORDERED_EFFECT,
    )(*[_in_hbm(g) for g in grads])


def _pair_exchange_finish_call(grads, bufs, send_sems, recv_sems, after, name):
    n = len(grads)
    after = list(after)
    half = [g.shape[1] // 2 for g in grads]

    def body(*refs):
        srcs, ins, send_ref, recv_ref = refs[:n], refs[n:2 * n], refs[2 * n], refs[2 * n + 1]
        x, y, c = lax.axis_index("x"), lax.axis_index("y"), lax.axis_index("c")
        for t in range(n):
            pltpu.make_async_remote_copy(
                src_ref=srcs[t].at[:, pl.ds(pl.multiple_of(half[t] * (1 - c), half[t]), half[t])], dst_ref=ins[t],
                send_sem=send_ref.at[t], recv_sem=recv_ref.at[t], device_id=(x, y, 1 - c), device_id_type=MESH_ID).wait()

    return pl.pallas_call(
        body, out_shape=[pltpu.HBM(b.shape, b.dtype) for b in bufs],
        in_specs=[_HBM] * (2 * n) + [_SEM, _SEM] + [_ANY] * len(after), out_specs=[_HBM] * n,
        input_output_aliases={n + t: t for t in range(n)}, name=name, compiler_params=_ORDERED_EFFECT,
    )(*[_in_hbm(g) for g in grads], *bufs, send_sems, recv_sems, *after)


def _chip_scatter_start_call(chip_sums, name):
    n = len(chip_sums)

    def body(*refs):
        srcs, outs = refs[:n], refs[n:2 * n]
        send_sems, recv_sems, token = refs[2 * n:]
        x, y, c = lax.axis_index("x"), lax.axis_index("y"), lax.axis_index("c")
        chips = [(1 - x, y), (x, 1 - y), (1 - x, 1 - y)]
        for k, (cx, cy) in enumerate(chips):
            for t in range(n):
                pltpu.make_async_remote_copy(
                    src_ref=srcs[t].at[2 * cx + cy], dst_ref=outs[t].at[k], send_sem=send_sems.at[3 * t + k],
                    recv_sem=recv_sems.at[3 * t + k], device_id=(cx, cy, c), device_id_type=MESH_ID).start()
        token[...] = jnp.zeros_like(token)

    dma = pltpu.SemaphoreType.DMA
    return pl.pallas_call(
        body, out_shape=[pltpu.HBM((3,) + p.shape[1:], p.dtype) for p in chip_sums] + [dma((3 * n,)), dma((3 * n,)), _TOKEN],
        in_specs=[_HBM] * n, out_specs=[_HBM] * n + [_SEM, _SEM, _VMEM], name=name, compiler_params=_ORDERED_EFFECT,
    )(*[_in_hbm(p) for p in chip_sums])


def _chip_scatter_finish_call(chip_sums, bufs, send_sems, recv_sems, after, name):
    n = len(chip_sums)
    after = list(after)

    def body(*refs):
        srcs, ins, send_ref, recv_ref = refs[:n], refs[n:2 * n], refs[2 * n], refs[2 * n + 1]
        x, y, c = lax.axis_index("x"), lax.axis_index("y"), lax.axis_index("c")
        chips = [(1 - x, y), (x, 1 - y), (1 - x, 1 - y)]
        for k, (cx, cy) in enumerate(chips):
            for t in range(n):
                pltpu.make_async_remote_copy(
                    src_ref=srcs[t].at[2 * cx + cy], dst_ref=ins[t].at[k], send_sem=send_ref.at[3 * t + k],
                    recv_sem=recv_ref.at[3 * t + k], device_id=(cx, cy, c), device_id_type=MESH_ID).wait()

    return pl.pallas_call(
        body, out_shape=[pltpu.HBM(b.shape, b.dtype) for b in bufs],
        in_specs=[_HBM] * (2 * n) + [_SEM, _SEM] + [_ANY] * len(after), out_specs=[_HBM] * n,
        input_output_aliases={n + t: t for t in range(n)}, name=name, compiler_params=_ORDERED_EFFECT,
    )(*[_in_hbm(p) for p in chip_sums], *bufs, send_sems, recv_sems, *after)


def _place_own_call(mine, chip_core, name):
    n = len(mine)

    def body(ids_ref, *refs):
        for t in range(n):
            refs[n + t][0] = refs[t][...]

    def imap_out(s):
        pad = (0,) * (s.ndim - 1)
        return lambda i, ids_ref: (ids_ref[0], ids_ref[1]) + pad

    grid_spec = pltpu.PrefetchScalarGridSpec(
        num_scalar_prefetch=1, grid=(1,), in_specs=[pl.BlockSpec(s.shape, lambda i, ids_ref, k=s.ndim: (0,) * k) for s in mine],
        out_specs=[pl.BlockSpec((1,) + s.shape, imap_out(s)) for s in mine])
    return pl.pallas_call(
        body, grid_spec=grid_spec,
        out_shape=[jax.ShapeDtypeStruct((N_CHIPS, 2 * s.shape[0]) + s.shape[1:], s.dtype) for s in mine],
        name=name, compiler_params=_cparams(1))(chip_core, *mine)


def _gather_start_call(mine, bufs, after, name):
    n = len(mine)
    half = [s.shape[0] for s in mine]

    def body(*refs):
        srcs, outs = refs[:n], refs[2 * n + 1:3 * n + 1]
        send_sems, recv_sib, recv_ici, token = refs[3 * n + 1:]
        x, y, c = lax.axis_index("x"), lax.axis_index("y"), lax.axis_index("c")
        chips = [(1 - x, y), (x, 1 - y), (1 - x, 1 - y)]
        for t in range(n):
            dst = _half_rows(outs[t], 2 * x + y, half[t], c)
            pltpu.make_async_remote_copy(src_ref=srcs[t], dst_ref=dst, send_sem=send_sems.at[4 * t], recv_sem=recv_sib.at[t],
                                         device_id=(x, y, 1 - c), device_id_type=MESH_ID).start()
            for j, chip in enumerate(chips):
                pltpu.make_async_remote_copy(src_ref=srcs[t], dst_ref=dst, send_sem=send_sems.at[4 * t + 1 + j],
                                             recv_sem=recv_ici.at[3 * t + j], device_id=(*chip, c), device_id_type=MESH_ID).start()
        token[...] = jnp.zeros_like(token)

    dma = pltpu.SemaphoreType.DMA
    return pl.pallas_call(
        body, out_shape=[pltpu.HBM(b.shape, b.dtype) for b in bufs] + [dma((4 * n,)), dma((n,)), dma((3 * n,)), _TOKEN],
        in_specs=[_HBM] * (2 * n) + [_ANY], out_specs=[_HBM] * n + [_SEM] * 3 + [_VMEM],
        input_output_aliases={n + t: t for t in range(n)}, name=name, compiler_params=_ORDERED_EFFECT,
    )(*[_in_hbm(s) for s in mine], *[_in_hbm(b) for b in bufs], after)


def _gather_forward_call(bufs, recv_ici, after, name):
    n = len(bufs)
    half = [b.shape[1] // 2 for b in bufs]

    def body(*refs):
        ins, recv_ici_ref = refs[:n], refs[n]
        outs = refs[n + 2:2 * n + 2]
        send_fwd, recv_fwd, token = refs[2 * n + 2:]
        x, y, c = lax.axis_index("x"), lax.axis_index("y"), lax.axis_index("c")
        chips = [(1 - x, y), (x, 1 - y), (1 - x, 1 - y)]
        for j, (cx, cy) in enumerate(chips):
            for t in range(n):
                landed = _half_rows(ins[t], 2 * cx + cy, half[t], c)
                dst = _half_rows(outs[t], 2 * cx + cy, half[t], c)
                pltpu.make_async_remote_copy(src_ref=landed, dst_ref=landed, send_sem=send_fwd.at[3 * t + j],
                                             recv_sem=recv_ici_ref.at[3 * t + j], device_id=(cx, cy, c),
                                             device_id_type=MESH_ID).wait_recv()
                pltpu.make_async_remote_copy(src_ref=landed, dst_ref=dst, send_sem=send_fwd.at[3 * t + j],
                                             recv_sem=recv_fwd.at[3 * t + j], device_id=(x, y, 1 - c),
                                             device_id_type=MESH_ID).start()
        token[...] = jnp.zeros_like(token)

    dma = pltpu.SemaphoreType.DMA
    return pl.pallas_call(
        body, out_shape=[pltpu.HBM(b.shape, b.dtype) for b in bufs] + [dma((3 * n,)), dma((3 * n,)), _TOKEN],
        in_specs=[_HBM] * n + [_SEM, _ANY], out_specs=[_HBM] * n + [_SEM] * 2 + [_VMEM],
        input_output_aliases={t: t for t in range(n)}, name=name, compiler_params=_ORDERED_EFFECT,
    )(*bufs, recv_ici, after)


def _gather_finish_call(shards, bufs, send_sems, recv_sib, send_fwd, recv_fwd, after, name):
    n = len(bufs)
    half = [b.shape[1] // 2 for b in bufs]

    def body(*refs):
        srcs, ins = refs[:n], refs[n:2 * n]
        send_ref, recv_sib_ref, send_fwd_ref, recv_fwd_ref = refs[2 * n:2 * n + 4]
        x, y, c = lax.axis_index("x"), lax.axis_index("y"), lax.axis_index("c")
        chips = [(1 - x, y), (x, 1 - y), (1 - x, 1 - y)]
        sibling = (x, y, 1 - c)
        for t in range(n):
            for k in range(4):
                pltpu.make_async_remote_copy(src_ref=srcs[t], dst_ref=srcs[t], send_sem=send_ref.at[4 * t + k],
                                             recv_sem=recv_sib_ref.at[t], device_id=sibling, device_id_type=MESH_ID).wait_send()
            from_sibling = _half_rows(ins[t], 2 * x + y, half[t], 1 - c)
            pltpu.make_async_remote_copy(src_ref=from_sibling, dst_ref=from_sibling, send_sem=send_ref.at[4 * t],
                                         recv_sem=recv_sib_ref.at[t], device_id=sibling, device_id_type=MESH_ID).wait_recv()
            for j, (cx, cy) in enumerate(chips):
                sent = _half_rows(ins[t], 2 * cx + cy, half[t], c)
                passed = _half_rows(ins[t], 2 * cx + cy, half[t], 1 - c)
                pltpu.make_async_remote_copy(src_ref=sent, dst_ref=passed, send_sem=send_fwd_ref.at[3 * t + j],
                                             recv_sem=recv_fwd_ref.at[3 * t + j], device_id=sibling, device_id_type=MESH_ID).wait()

    return pl.pallas_call(
        body, out_shape=[pltpu.HBM(b.shape, b.dtype) for b in bufs],
        in_specs=[_HBM] * (2 * n) + [_SEM] * 4 + [_ANY], out_specs=[_HBM] * n,
        input_output_aliases={n + t: t for t in range(n)}, name=name, compiler_params=_ORDERED_EFFECT,
    )(*[_in_hbm(s) for s in shards], *bufs, send_sems, recv_sib, send_fwd, recv_fwd, after)


def _pair_gather_call(bufs, name):
    n = len(bufs)
    half = [b.shape[0] // 2 for b in bufs]

    def body(*refs):
        srcs, outs, send_sems, recv_sems = refs[:n], refs[n:2 * n], refs[2 * n], refs[2 * n + 1]
        x, y, c = lax.axis_index("x"), lax.axis_index("y"), lax.axis_index("c")
        for t in range(n):
            pltpu.make_async_remote_copy(
                src_ref=_half_rows(srcs[t], None, half[t], c), dst_ref=_half_rows(outs[t], None, half[t], c),
                send_sem=send_sems.at[t], recv_sem=recv_sems.at[t], device_id=(x, y, 1 - c), device_id_type=MESH_ID).start()
        for t in range(n):
            pltpu.make_async_remote_copy(
                src_ref=_half_rows(srcs[t], None, half[t], c), dst_ref=_half_rows(outs[t], None, half[t], 1 - c),
                send_sem=send_sems.at[t], recv_sem=recv_sems.at[t], device_id=(x, y, 1 - c), device_id_type=MESH_ID).wait()

    return pl.pallas_call(
        body, out_shape=[jax.ShapeDtypeStruct(b.shape, b.dtype) for b in bufs], in_specs=[_ANY] * n, out_specs=[_ANY] * n,
        input_output_aliases={t: t for t in range(n)},
        scratch_shapes=[pltpu.SemaphoreType.DMA((n,)), pltpu.SemaphoreType.DMA((n,))], name=name)(*bufs)


def _pack_rows(flats, dtype, row_multiple):
    flat = jnp.concatenate([f.reshape(-1).astype(dtype) for f in flats])
    n = flat.shape[0]
    rows = -(-n // PACK_W)
    rows = -(-rows // row_multiple) * row_multiple
    return jnp.pad(flat, (0, rows * PACK_W - n)).reshape(rows, PACK_W)


def _unpack(flat, shapes):
    out, off = [], 0
    for shp in shapes:
        n = math.prod(shp)
        out.append(flat[off:off + n].reshape(shp))
        off += n
    return out


_W_IN_SEGMENTS = ((R_ML, R_END, OFF_ML), (R_SG, R_ML, OFF_SG), (R_CV, R_SGI, OFF_CV), (R_SGI, R_MQ, OFF_SGI), (R_MQ, R_SG, OFF_MQ),
                  (R_CQ, R_CKV, OFF_CQ), (R_CKV, R_KR, OFF_CKV), (R_KR, R_CV, OFF_KR + NOPE))
W_IN_SHARD = R_END // N_CHIPS


def _realign_call(wg):
    tr = 128

    def body(w_ref, o_ref):
        pieces, pos = [], 0
        for r0, r1, a0 in _W_IN_SEGMENTS:
            if a0 > pos:
                pieces.append(jnp.zeros((tr, a0 - pos), o_ref.dtype))
            while r0 < r1:
                j = r0 // W_IN_SHARD
                hi = min(r1, (j + 1) * W_IN_SHARD)
                pieces.append(w_ref[j, :, r0 - j * W_IN_SHARD:hi - j * W_IN_SHARD])
                a0, r0 = a0 + hi - r0, hi
            pos = a0
        pieces.append(jnp.zeros((tr, NP - pos), o_ref.dtype))
        o_ref[...] = jnp.concatenate(pieces, axis=1)

    return pl.pallas_call(
        body, grid=(D // tr,), in_specs=[_bs((N_CHIPS, tr, W_IN_SHARD), lambda i: (0, i, 0))],
        out_specs=_bs((tr, NP), lambda i: (i, 0)), out_shape=jax.ShapeDtypeStruct((D, NP), wg.dtype),
        name="w_in_realign", compiler_params=_cparams(1))(wg)


def _unalign_call(dw):
    tr = 128
    by_ref = sorted(_W_IN_SEGMENTS)

    def body(dw_ref, o_ref):
        for j in range(N_CHIPS):
            lo_j, hi_j = j * W_IN_SHARD, (j + 1) * W_IN_SHARD
            pieces = []
            for r0, r1, a0 in by_ref:
                lo, hi = max(r0, lo_j), min(r1, hi_j)
                if lo < hi:
                    pieces.append(dw_ref[:, a0 + lo - r0:a0 + hi - r0])
            o_ref[j] = jnp.concatenate(pieces, axis=1)

    return pl.pallas_call(
        body, grid=(D // tr,), in_specs=[_bs((tr, NP), lambda i: (i, 0))],
        out_specs=_bs((N_CHIPS, tr, W_IN_SHARD), lambda i: (0, i, 0)),
        out_shape=jax.ShapeDtypeStruct((N_CHIPS, D, W_IN_SHARD), dw.dtype), name="w_in_unalign", compiler_params=_cparams(1))(dw)


def _wuq_to_heads(w):
    w3 = w.reshape(QL, H, QKH)
    w3 = jnp.pad(w3, ((0, 0), (0, 0), (0, LANES - QKH)))
    return jnp.transpose(w3, (1, 0, 2))


def _wuq_from_heads(wh):
    return jnp.transpose(wh[:, :, :QKH], (1, 0, 2)).reshape(QL, H * QKH)


def _wukv_to_heads(w):
    w3 = w.reshape(KVL, H, NOPE + VH)
    wkn = jnp.transpose(jnp.pad(w3[:, :, :NOPE], ((0, 0), (0, 0), (0, LANES - NOPE))), (1, 0, 2))
    wv3 = w3[:, :, NOPE:]
    z = jnp.zeros((KVL, VH), w.dtype)
    cols = []
    for h in range(H):
        cols += [wv3[:, h], z] if h % 2 == 0 else [z, wv3[:, h]]
    return wkn, jnp.concatenate(cols, axis=1)


def _wukv_from_heads(wkn, wv):
    kn = jnp.transpose(wkn[:, :, :NOPE], (1, 0, 2))
    vs = jnp.stack([wv[:, LANES * h + VH * (h % 2):LANES * h + VH * (h % 2) + VH] for h in range(H)], axis=1)
    return jnp.concatenate([kn, vs], axis=2).reshape(KVL, H * (NOPE + VH))


def _layer_fwd(x, mem, tabs, p):
    proj, h = _proj_call(x, p["norm_g"], p["w_in"])
    if p.get("late") is not None:
        p = dict(p, **p["late"](proj))
    q, k, v = _mla_prep_call(proj, tabs, p["cq_g"], p["ckv_g"], p["qg"], p["kg"], p["wuq"], p["wkn"], p["wv"])
    ya, attn_o, attn_lse = _attn_call(q, k, v, proj)
    bm = p["bm"]
    if p.get("after_attn") is not None:
        bm = _tie(bm, p["after_attn"](ya))
    yb = _conv_call(proj, p["conv_w"], p["conv_b"])
    yc = _sg_call(proj, p["ln_g"], p["ln_b"], p["ws"], p["bs"])
    mk, mv = _memkv_call(mem, p["mem_g"], p["wm"], p["mkg"])
    yd = _mem_call(proj, mk, mv, p["mqg"])
    out = _merge_call((ya, yb, yc, yd), proj, bm, p["wb"], p["wo"], x)
    return out, dict(p=p, x=x, proj=proj, h=h, q=q, k=k, v=v, attn_o=attn_o, attn_lse=attn_lse, ys=(ya, yb, yc, yd), mk=mk, mv=mv)


def _layer_bwd(dout, mem, tabs, p, sv, start_after=None, on_rest_grads=None, on_grads=None):
    proj = sv["proj"]
    bm = p["bm"] if start_after is None else _tie(p["bm"], start_after)
    dya, dyb, dyc, dyd, dml, dbm, dwb, dwo = _merge_bwd_call(sv["ys"], proj, bm, p["wb"], p["wo"], dout)
    dq, dk, dv, dsg_a = _attn_bwd_call(sv["q"], sv["k"], sv["v"], proj, dya, sv["attn_o"], sv["attn_lse"])
    dcq, dckv, dkr, dcqg, dckvg, dqg, dkg, dwuq, dwkn, dwv = _mla_prep_bwd_call(
        proj, tabs, p["cq_g"], p["ckv_g"], p["qg"], p["kg"], p["wuq"], p["wkn"], p["wv"], dq, dk, dv)
    dbg, dcg, dxi, dsg_b, dcw, dcb = _conv_bwd_call(proj, p["conv_w"], p["conv_b"], dyb)
    du, dvv, dsg_c, dlg, dlb, dws, dbs = _sg_bwd_call(proj, p["ln_g"], p["ln_b"], p["ws"], p["bs"], dyc)
    dmq, dsg_d, dmk, dmv, dmqg = _mem_bwd_call(proj, sv["mk"], sv["mv"], p["mqg"], dyd)
    dmem_g, dwm, dmkg = _memkv_bwd_call(mem, p["mem_g"], p["wm"], p["mkg"], dmk, dmv)
    grads = dict(cq_norm_g=dcqg[0], ckv_norm_g=dckvg[0], mla_q_norm_g=dqg[0, :QKH], mla_k_norm_g=dkg[0, :QKH],
                 conv_w=dcw, conv_b=dcb[0], sg_ln_g=dlg[0], sg_ln_b=dlb[0], w_spatial=dws, b_spatial=dbs[:, :, 0],
                 mem_norm_g=dmem_g[0], mem_q_norm_g=dmqg[0], mem_k_norm_g=dmkg[0], b_merge=dbm,
                 wuq_heads=dwuq, wkn_heads=dwkn, wv_heads=dwv, w_mem_kv=dwm, w_branch_chips=dwb, w_out=dwo)
    started = [on_rest_grads(grads)] if on_rest_grads is not None else []
    dproj = jnp.concatenate([dml, dsg_a, dsg_b, dsg_c, dsg_d, dbg, dcg, dxi, du, dvv, dmq, dcq, dckv, dkr], axis=1)
    grads["w_in_aligned"] = _dw_call(sv["h"], dproj, started)
    tokens = on_grads(grads) if on_grads is not None else ()
    dx, dnorm_g = _dh_call(dproj, p["w_in"], sv["x"], p["norm_g"], dout, tokens)
    grads["norm_g"] = dnorm_g[0]
    return dx, grads


def _chips_to_cols(a):
    return jnp.concatenate([a[j] for j in range(N_CHIPS)], axis=1)


def _cols_to_chips(a):
    cols = a.shape[1] // N_CHIPS
    return jnp.stack([a[:, cols * j:cols * (j + 1)] for j in range(N_CHIPS)])


def _layer_params_first(l, rep, w_in_gathered, conv_w, b_merge):
    pad_g = lambda g: jnp.pad(g, (0, LANES - QKH)).reshape(1, LANES)
    return dict(
        norm_g=rep["norm_g"][l].reshape(1, D), w_in=_realign_call(w_in_gathered),
        cq_g=rep["cq_norm_g"][l].reshape(1, QL), ckv_g=rep["ckv_norm_g"][l].reshape(1, KVL),
        qg=pad_g(rep["mla_q_norm_g"][l]), kg=pad_g(rep["mla_k_norm_g"][l]),
        conv_w=conv_w, conv_b=rep["conv_b"][l].reshape(1, CW),
        ln_g=rep["sg_ln_g"][l].reshape(1, SGW), ln_b=rep["sg_ln_b"][l].reshape(1, SGW),
        ws=rep["w_spatial"][l], bs=rep["b_spatial"][l].reshape(SGG, SGC, 1),
        mem_g=rep["mem_norm_g"][l].reshape(1, D),
        mqg=rep["mem_q_norm_g"][l].reshape(1, MHD), mkg=rep["mem_k_norm_g"][l].reshape(1, MHD), bm=b_merge)


def _layer_params_rest(gathered):
    wkn, wv = _wukv_to_heads(_chips_to_cols(gathered["w_ukv"]))
    return dict(wuq=_wuq_to_heads(_chips_to_cols(gathered["w_uq"])), wkn=wkn, wv=wv,
                wm=gathered["w_mem_kv"].reshape(D, 2 * MH * MHD), wb=gathered["w_branch"], wo=gathered["w_out"].reshape(D, D))


def _layer_params(l, rep, gathered, conv_w, b_merge):
    return dict(_layer_params_first(l, rep, gathered["w_in"], conv_w, b_merge), **_layer_params_rest(gathered))


def _forward_backward(x, mem, pos, target, params, bwd_hooks=None):
    tabs = _rope_tables(pos)
    params = list(params)
    saved = []
    act = x
    for l in range(DEPTH):
        if callable(params[l]):
            params[l] = params[l](saved[-1], act)
        act, sv = _layer_fwd(act, mem, tabs, params[l])
        saved.append(sv)
    dy, sq = _loss_call(act, target)
    grads = [None] * DEPTH
    token = None
    for l in reversed(range(DEPTH)):
        hooks = dict(bwd_hooks[l]) if bwd_hooks else {}
        after_layer = hooks.pop("after_layer", None)
        dy, grads[l] = _layer_bwd(dy, mem, tabs, saved[l]["p"], saved[l], start_after=token, **hooks)
        token = after_layer(dy) if after_layer is not None else None
    return sq, dy, grads


_SHARDED_MM = ("w_in", "w_branch", "w_out", "w_mem_kv", "w_uq", "w_ukv")
_SHARDED_F32 = ("conv_w", "b_merge")
_REPLICATED = ("norm_g", "cq_norm_g", "ckv_norm_g", "mla_q_norm_g", "mla_k_norm_g", "conv_b", "sg_ln_g", "sg_ln_b",
               "w_spatial", "b_spatial", "mem_norm_g", "mem_q_norm_g", "mem_k_norm_g")
_ALL_REDUCED = _REPLICATED + _SHARDED_F32
_WEIGHTS = ("norm_g", "w_in", "cq_norm_g", "ckv_norm_g", "w_uq", "w_ukv", "mla_q_norm_g", "mla_k_norm_g", "conv_w", "conv_b",
            "sg_ln_g", "sg_ln_b", "w_spatial", "b_spatial", "mem_norm_g", "w_mem_kv", "mem_q_norm_g", "mem_k_norm_g",
            "b_merge", "w_branch", "w_out")
_SMALL = tuple(n for n in _WEIGHTS if n not in _SHARDED_MM)


class _SmallGather:
    def __init__(self, blk, after, tag):
        self.blk, self.tag = blk, tag
        x, y, c = lax.axis_index("x"), lax.axis_index("y"), lax.axis_index("c")
        own = _place_block_call(blk, (4 * x + 2 * y + c).astype(jnp.int32).reshape(1), tag + "place_own")
        self.buf, self.send, self.recv, self.token = _small_gather_start_call(blk, own, after, tag + "start")

    def finish(self, after):
        return _small_gather_finish_call(self.blk, self.buf, self.send, self.recv, after, self.tag + "finish")


def _small_sharded_weights(w, got):
    names = _SHARDED_F32
    per_chip = [_unpack(got[2 * j].reshape(-1), [w[n].shape for n in names]) for j in range(N_CHIPS)]
    return {n: jnp.concatenate([per_chip[j][t] for j in range(N_CHIPS)], axis=2) for t, n in enumerate(names)}


class _Gather:
    def __init__(self, w, layer, names, after, tag):
        self.names, self.tag = names, tag
        x, y, c = lax.axis_index("x"), lax.axis_index("y"), lax.axis_index("c")
        chip_core = jnp.stack([2 * x + y, c]).astype(jnp.int32)
        halves = [w[n].shape[1] // 2 for n in names]
        self.srcs = [lax.dynamic_slice_in_dim(w[n][layer], c * h, h, axis=0).astype(MM) for n, h in zip(names, halves)]
        k = len(names)
        out = _gather_start_call(self.srcs, _place_own_call(self.srcs, chip_core, tag + "place_own"), after, tag + "start")
        self.bufs, self.send, self.recv_sib, self.recv_ici, self.token = out[:k], out[k], out[k + 1], out[k + 2], out[k + 3]

    def pass_on(self, after):
        k = len(self.names)
        out = _gather_forward_call(self.bufs, self.recv_ici, after, self.tag + "forward")
        self.bufs, self.send_fwd, self.recv_fwd = out[:k], out[k], out[k + 1]
        return out[k + 2]

    def finish(self, after):
        got = _gather_finish_call(self.srcs, self.bufs, self.send, self.recv_sib, self.send_fwd, self.recv_fwd, after,
                                  self.tag + "finish")
        return dict(zip(self.names, got))


class _ReduceScatter:
    SLABS = dict(
        w_in=lambda g: _unalign_call(g["w_in_aligned"]),
        w_branch=lambda g: g["w_branch_chips"].reshape(N_CHIPS, NB * BW, D // N_CHIPS),
        w_out=lambda g: g["w_out"].reshape(N_CHIPS, D // N_CHIPS, D),
        w_mem_kv=lambda g: g["w_mem_kv"].reshape(N_CHIPS, D // N_CHIPS, 2 * MH * MHD),
        w_uq=lambda g: _cols_to_chips(_wuq_from_heads(g["wuq_heads"])),
        w_ukv=lambda g: _cols_to_chips(_wukv_from_heads(g["wkn_heads"], g["wv_heads"])))

    def __init__(self, tag, names):
        self.tag, self.names = tag, names

    def exchange(self, grads):
        self.tensors = [self.SLABS[n](grads) for n in self.names]
        n = len(self.tensors)
        out = _pair_exchange_start_call(self.tensors, self.tag + "exchange_start")
        self.ex_bufs, self.ex_send, self.ex_recv = out[:n], out[n], out[n + 1]
        return out[n + 2]

    def scatter(self, after):
        n = len(self.tensors)
        c = lax.axis_index("c")
        from_sibling = _pair_exchange_finish_call(self.tensors, self.ex_bufs, self.ex_send, self.ex_recv, after,
                                                  self.tag + "exchange_finish")
        self.chip_sums = _pair_sum_call(self.tensors, from_sibling, c.astype(jnp.int32).reshape(1), self.tag + "pair_sum")
        out = _chip_scatter_start_call(self.chip_sums, self.tag + "scatter_start")
        self.bufs, self.send_sems, self.recv_sems, self.token = out[:n], out[n], out[n + 1], out[n + 2]
        return self.token

    def finish(self, after):
        x, y, c = lax.axis_index("x"), lax.axis_index("y"), lax.axis_index("c")
        chip_core = jnp.stack([2 * x + y, c]).astype(jnp.int32)
        from_chips = _chip_scatter_finish_call(self.chip_sums, self.bufs, self.send_sems, self.recv_sems, after,
                                               self.tag + "scatter_finish")
        mine = _owner_sum_call(self.chip_sums, from_chips, chip_core, self.tag + "owner_sum")
        return dict(zip(self.names, _pair_gather_call(mine, self.tag + "pair_gather")))


def _small_sums(g, sq, got):
    total = _sum8_call(got).reshape(-1)
    parts = _unpack(total, [g[n].shape for n in _ALL_REDUCED] + [sq.shape])
    out = dict(zip(_ALL_REDUCED, parts))
    sq_total = parts[-1]
    chip = 2 * lax.axis_index("x") + lax.axis_index("y")
    for n in _SHARDED_F32:
        size = out[n].shape[2] // N_CHIPS
        out[n] = lax.dynamic_slice_in_dim(out[n], chip * size, size, axis=2)
    return out, sq_total


def _adamw_small(w, g, m, v):
    delta, new_m, new_v = {}, {}, {}
    shapes = [w[n].shape for n in _SMALL]
    pk = lambda t: _pack_rows([t[n] for n in _SMALL], F32, 64)
    d, nm, nv = _adamw_call(pk(w), pk(g), pk(m), pk(v), "adamw_small")
    for out, packed in ((delta, d), (new_m, nm), (new_v, nv)):
        out.update(zip(_SMALL, _unpack(packed.reshape(-1), shapes)))
    return delta, new_m, new_v


def kernel(x, mem, positions, norm_g, w_in, cq_norm_g, ckv_norm_g, w_uq, w_ukv, mla_q_norm_g, mla_k_norm_g, conv_w, conv_b, sg_ln_g, sg_ln_b, w_spatial, b_spatial, mem_norm_g, w_mem_kv, mem_q_norm_g, mem_k_norm_g, b_merge, w_branch, w_out, loss_target, m_norm_g, m_w_in, m_cq_norm_g, m_ckv_norm_g, m_w_uq, m_w_ukv, m_mla_q_norm_g, m_mla_k_norm_g, m_conv_w, m_conv_b, m_sg_ln_g, m_sg_ln_b, m_w_spatial, m_b_spatial, m_mem_norm_g, m_w_mem_kv, m_mem_q_norm_g, m_mem_k_norm_g, m_b_merge, m_w_branch, m_w_out, v_norm_g, v_w_in, v_cq_norm_g, v_ckv_norm_g, v_w_uq, v_w_ukv, v_mla_q_norm_g, v_mla_k_norm_g, v_conv_w, v_conv_b, v_sg_ln_g, v_sg_ln_b, v_w_spatial, v_b_spatial, v_mem_norm_g, v_w_mem_kv, v_mem_q_norm_g, v_mem_k_norm_g, v_b_merge, v_w_branch, v_w_out):
    w = dict(norm_g=norm_g, w_in=w_in, cq_norm_g=cq_norm_g, ckv_norm_g=ckv_norm_g, w_uq=w_uq, w_ukv=w_ukv,
             mla_q_norm_g=mla_q_norm_g, mla_k_norm_g=mla_k_norm_g, conv_w=conv_w, conv_b=conv_b, sg_ln_g=sg_ln_g,
             sg_ln_b=sg_ln_b, w_spatial=w_spatial, b_spatial=b_spatial, mem_norm_g=mem_norm_g, w_mem_kv=w_mem_kv,
             mem_q_norm_g=mem_q_norm_g, mem_k_norm_g=mem_k_norm_g, b_merge=b_merge, w_branch=w_branch, w_out=w_out)
    m = dict(norm_g=m_norm_g, w_in=m_w_in, cq_norm_g=m_cq_norm_g, ckv_norm_g=m_ckv_norm_g, w_uq=m_w_uq, w_ukv=m_w_ukv,
             mla_q_norm_g=m_mla_q_norm_g, mla_k_norm_g=m_mla_k_norm_g, conv_w=m_conv_w, conv_b=m_conv_b, sg_ln_g=m_sg_ln_g,
             sg_ln_b=m_sg_ln_b, w_spatial=m_w_spatial, b_spatial=m_b_spatial, mem_norm_g=m_mem_norm_g, w_mem_kv=m_w_mem_kv,
             mem_q_norm_g=m_mem_q_norm_g, mem_k_norm_g=m_mem_k_norm_g, b_merge=m_b_merge, w_branch=m_w_branch, w_out=m_w_out)
    v = dict(norm_g=v_norm_g, w_in=v_w_in, cq_norm_g=v_cq_norm_g, ckv_norm_g=v_ckv_norm_g, w_uq=v_w_uq, w_ukv=v_w_ukv,
             mla_q_norm_g=v_mla_q_norm_g, mla_k_norm_g=v_mla_k_norm_g, conv_w=v_conv_w, conv_b=v_conv_b, sg_ln_g=v_sg_ln_g,
             sg_ln_b=v_sg_ln_b, w_spatial=v_w_spatial, b_spatial=v_b_spatial, mem_norm_g=v_mem_norm_g, w_mem_kv=v_w_mem_kv,
             mem_q_norm_g=v_mem_q_norm_g, mem_k_norm_g=v_mem_k_norm_g, b_merge=v_b_merge, w_branch=v_w_branch, w_out=v_w_out)

    chip_core = jnp.stack([2 * lax.axis_index("x") + lax.axis_index("y"), lax.axis_index("c")]).astype(jnp.int32)

    first = _Gather(w, 0, ("w_in",), chip_core, "gather_l0_w_in_")
    rest = _Gather(w, 0, _SHARDED_MM[1:], first.token, "gather_l0_rest_")
    small_on_its_way = _SmallGather(_pack_rows([w[n] for n in _SHARDED_F32], F32, 8), [rest.token], "gather_small_weights_")
    later = _Gather(w, 1, _SHARDED_MM, small_on_its_way.token, "gather_l1_")
    w_in0 = first.finish(first.pass_on(later.token))["w_in"]
    small = {}

    def rest_of_layer0(proj0):
        landed = _layer_params_rest(rest.finish(rest.pass_on(proj0)))
        small.update(_small_sharded_weights(w, small_on_its_way.finish([landed["wo"]])))
        return dict(landed, conv_w=small["conv_w"][0], bm=small["b_merge"][0])

    def layer1_params(saved0, act0):
        return _layer_params(1, w, later.finish(act0), small["conv_w"][1], small["b_merge"][1])

    params0 = _layer_params_first(0, w, w_in0, None, None)
    params = [dict(params0, late=rest_of_layer0, after_attn=later.pass_on), layer1_params]
    others = _SHARDED_MM[1:]
    rs1 = _ReduceScatter("rs_l1_", _SHARDED_MM)
    rs0_rest, rs0_w_in = _ReduceScatter("rs_l0_rest_", others), _ReduceScatter("rs_l0_w_in_", ("w_in",))

    def layer0_grads_done(grads):
        return [rs0_rest.scatter([grads["w_in_aligned"]]), rs0_w_in.exchange(grads)]

    hooks = [dict(on_rest_grads=rs0_rest.exchange, on_grads=layer0_grads_done),
             dict(on_grads=lambda grads: [rs1.exchange(grads)], after_layer=lambda dy: rs1.scatter([dy]))]
    sq, grad_x, layer_grads = _forward_backward(x[0], mem[0], positions[0], loss_target[0], params, hooks)

    g_small = {n: jnp.stack([layer_grads[l][n] for l in range(DEPTH)]) for n in _ALL_REDUCED}
    small_grads = _SmallGather(_pack_rows([g_small[n] for n in _ALL_REDUCED] + [sq], F32, 64), [grad_x], "gather_small_grads_")
    scattering = rs0_w_in.scatter([grad_x, small_grads.token])
    shard_grads = {1: rs1.finish([scattering]), 0: rs0_rest.finish([scattering])}
    as3d = lambda a: a.reshape(DEPTH, -1, a.shape[-1])
    as2d = lambda a: a.reshape(-1, a.shape[-1])
    big = lambda t: [as3d(t[n]) for n in others]
    turned = lambda t: [jnp.swapaxes(t["w_in"], 1, 2)]
    assert W_IN_SHARD % (8 * 7) == 0

    def update_w_in(l, grad, prev):
        return _adamw_layer_call(l, turned(w), [grad.T], turned(m), turned(v), prev, [], "adamw_w_in_l%d" % l, steps=7)

    def update_others(l, prev):
        return _adamw_layer_call(l, big(w), [as2d(shard_grads[l][n]) for n in others], big(m), big(v), prev, [], "adamw_l%d" % l)

    upd = update_others(0, update_others(1, None))
    g, sq_total = _small_sums(g_small, sq, small_grads.finish([upd[0]]))
    loss = 0.5 / D * jnp.sum(sq_total)
    delta, new_m, new_v = _adamw_small(w, g, m, v)
    upd_in1 = update_w_in(1, shard_grads[1]["w_in"], None)
    w_in_grad0 = rs0_w_in.finish([grad_x, upd_in1[0], upd[0], delta["norm_g"]])["w_in"]
    upd_in = update_w_in(0, w_in_grad0, upd_in1)
    g["w_in"], delta["w_in"], new_m["w_in"], new_v["w_in"] = [jnp.swapaxes(a, 1, 2) for a in upd_in]
    for t, n in enumerate(others):
        g[n], delta[n], new_m[n], new_v[n] = [a.reshape(w[n].shape) for a in upd[4 * t:4 * t + 4]]
    return (loss, grad_x[None], *[g[n] for n in _WEIGHTS], *[delta[n] for n in _WEIGHTS],
            *[new_m[n] for n in _WEIGHTS], *[new_v[n] for n in _WEIGHTS])
```

```python
import functools
import math

import jax
import jax.numpy as jnp
from jax import lax
from jax.experimental import pallas as pl
from jax.experimental.pallas import tpu as pltpu

F32 = jnp.float32
MM = jnp.bfloat16

D = 1024
DEPTH = 2
EPS = 1e-6
H = 8
NOPE = 64
ROPE = 32
QKH = 96
VH = 64
QL = 256
KVL = 128
ROPE_THETA = 10000.0
CW = 512
SGW = 512
SGG = 4
SGC = 128
MH = 4
MHD = 128
NB = 4
BW = 512
NEG_INF = -1e30
LANES = 128
N_CHIPS = 4

R_CQ, R_CKV, R_KR, R_CV, R_SGI, R_MQ, R_SG, R_ML, R_END = 0, 256, 384, 416, 1952, 2976, 3488, 5536, 9632
OFF_ML, OFF_SG, OFF_CV, OFF_SGI, OFF_MQ, OFF_CQ, OFF_CKV, OFF_KR, NP = 0, 4096, 6144, 7680, 8704, 9216, 9472, 9600, 9728

ADAM_LR = 0.001
ADAM_B1 = 0.9
ADAM_B2 = 0.999
ADAM_EPS = 1e-08
ADAM_WD = 0.01
ADAM_STEP = 10

VMEM_LIMIT = 56 * 1024 * 1024
PACK_W = 512
MESH_ID = pl.DeviceIdType.MESH


def _cparams(n_axes):
    return pltpu.CompilerParams(dimension_semantics=("arbitrary",) * n_axes, vmem_limit_bytes=VMEM_LIMIT)


def _bs(shape, imap):
    return pl.BlockSpec(shape, imap)


@jax.custom_vjp
def _mm_plain(a, b):
    return jnp.dot(a.astype(MM), b.astype(MM), preferred_element_type=F32)


def _mm_plain_fwd(a, b):
    return _mm_plain(a, b), (a, b)


def _mm_plain_bwd(res, g):
    a, b = res
    gm = g.astype(MM)
    da = lax.dot_general(gm, b.astype(MM), (((1,), (1,)), ((), ())), preferred_element_type=F32)
    db = lax.dot_general(a.astype(MM), gm, (((0,), (0,)), ((), ())), preferred_element_type=F32)
    return da.astype(a.dtype), db.astype(b.dtype)


_mm_plain.defvjp(_mm_plain_fwd, _mm_plain_bwd)


@jax.custom_vjp
def _mm_slot(a, w, slot):
    return jnp.dot(a.astype(MM), w.astype(MM), preferred_element_type=F32)


def _mm_slot_fwd(a, w, slot):
    return _mm_slot(a, w, slot), (a, w)


def _mm_slot_bwd(res, g):
    a, w = res
    gm = g.astype(MM)
    da = lax.dot_general(gm, w.astype(MM), (((1,), (1,)), ((), ())), preferred_element_type=F32)
    dw = lax.dot_general(a.astype(MM), gm, (((0,), (0,)), ((), ())), preferred_element_type=F32)
    return da.astype(a.dtype), jnp.zeros_like(w), dw


_mm_slot.defvjp(_mm_slot_fwd, _mm_slot_bwd)


def _mm(a, b):
    if isinstance(b, tuple):
        return _mm_slot(a, b[0], b[1])
    return _mm_plain(a, b)


def _with_slot(w):
    return (w, jnp.zeros(w.shape, F32))


@jax.custom_vjp
def _mm_nt(a, b):
    return lax.dot_general(a.astype(MM), b.astype(MM), (((1,), (1,)), ((), ())), preferred_element_type=F32)


def _mm_nt_fwd(a, b):
    return _mm_nt(a, b), (a, b)


def _mm_nt_bwd(res, g):
    a, b = res
    gm = g.astype(MM)
    da = jnp.dot(gm, b.astype(MM), preferred_element_type=F32)
    db = lax.dot_general(gm, a.astype(MM), (((0,), (0,)), ((), ())), preferred_element_type=F32)
    return da.astype(a.dtype), db.astype(b.dtype)


_mm_nt.defvjp(_mm_nt_fwd, _mm_nt_bwd)


@functools.partial(jax.custom_vjp, nondiff_argnums=(1,))
def _lane_roll(x, shift):
    return pltpu.roll(x, shift, 1)


def _lane_roll_fwd(x, shift):
    return pltpu.roll(x, shift, 1), None


def _lane_roll_bwd(shift, _, g):
    return (pltpu.roll(g, (LANES - shift) % LANES, 1),)


_lane_roll.defvjp(_lane_roll_fwd, _lane_roll_bwd)


def _rms_n(x, g, n):
    ms = jnp.sum(x * x, axis=-1, keepdims=True) * (1.0 / n)
    return x * lax.rsqrt(ms + EPS) * g


def _softmax(s):
    m = jnp.max(s, axis=-1, keepdims=True)
    e = jnp.exp(s - m)
    return e / jnp.sum(e, axis=-1, keepdims=True)


def _rope(t, cos_t, sin_a, sin_b):
    return t * cos_t + _lane_roll(t, LANES - 16) * sin_a + _lane_roll(t, 16) * sin_b


def _mla_prep_fn(cq, ckv, kr, cos_t, sin_a, sin_b, cq_g, ckv_g, qg, kg, wuq, wkn, wv):
    cqn = _rms_n(cq, cq_g, QL)
    ckvn = _rms_n(ckv, ckv_g, KVL)
    lane = lax.broadcasted_iota(jnp.int32, kr.shape, 1)
    krm = jnp.where((lane >= NOPE) & (lane < QKH), kr, 0.0)
    qs, ks = [], []
    for h in range(H):
        qh = _rms_n(_mm(cqn, wuq[h]), qg, QKH)
        qs.append(_rope(qh, cos_t, sin_a, sin_b) * (QKH ** -0.5))
        kh = _rms_n(_mm(ckvn, wkn[h]) + krm, kg, QKH)
        ks.append(_rope(kh, cos_t, sin_a, sin_b))
    return jnp.concatenate(qs, axis=-1), jnp.concatenate(ks, axis=-1), _mm(ckvn, wv)


def _dot_nt(a, b):
    return lax.dot_general(a.astype(MM), b.astype(MM), (((1,), (1,)), ((), ())), preferred_element_type=F32)


def _dot_tn(a, b):
    return lax.dot_general(a.astype(MM), b.astype(MM), (((0,), (0,)), ((), ())), preferred_element_type=F32)


def _causal_scores(qe, ke):
    tq, kl = qe.shape[0], ke.shape[0]
    s = _dot_nt(qe, ke)
    rows = lax.broadcasted_iota(jnp.int32, (tq, tq), 0)
    cols = lax.broadcasted_iota(jnp.int32, (tq, tq), 1)
    own = jnp.where(cols <= rows, s[:, kl - tq:], NEG_INF)
    return own if kl == tq else jnp.concatenate([s[:, :kl - tq], own], axis=1)


def _head_lanes(e, shape):
    lane = lax.broadcasted_iota(jnp.int32, shape, len(shape) - 1)
    return (lane >= VH * e) & (lane < VH * (e + 1))


def _attn_pair_fwd(q2, k2, v2):
    tq = q2.shape[0]
    o = jnp.zeros((tq, LANES), F32)
    lse = jnp.zeros((tq, LANES), F32)
    for e in range(2):
        sl = slice(LANES * e, LANES * (e + 1))
        s = _causal_scores(q2[:, sl], k2[:, sl])
        m = jnp.max(s, axis=-1, keepdims=True)
        ex = jnp.exp(s - m)
        l = jnp.sum(ex, axis=-1, keepdims=True)
        ve = jnp.where(_head_lanes(e, v2[:, sl].shape), v2[:, sl], 0.0)
        o = o + jnp.dot((ex * (1.0 / l)).astype(MM), ve.astype(MM), preferred_element_type=F32)
        lse = jnp.where(_head_lanes(e, lse.shape), m + jnp.log(l), lse)
    return o, lse


def _attn_pair_bwd(q2, k2, v2, sg, dys, o, lse):
    sig = jax.nn.sigmoid(sg)
    do = dys * (sg * sig)
    dsg = dys * o * (sig * (1.0 + sg * (1.0 - sig)))
    dqs, dks, dvs = [], [], []
    for e in range(2):
        sl = slice(LANES * e, LANES * (e + 1))
        qe, ke = q2[:, sl], k2[:, sl]
        hm = _head_lanes(e, o.shape)
        lse_e = jnp.max(jnp.where(hm, lse, NEG_INF), axis=-1, keepdims=True)
        do_e = jnp.where(hm, do, 0.0)
        delta = jnp.sum(do_e * o, axis=-1, keepdims=True)
        p = jnp.exp(_causal_scores(qe, ke) - lse_e)
        ve = jnp.where(_head_lanes(e, v2[:, sl].shape), v2[:, sl], 0.0)
        dvs.append(_dot_tn(p, do_e))
        ds = p * (_dot_nt(do_e, ve) - delta)
        dqs.append(jnp.dot(ds.astype(MM), ke.astype(MM), preferred_element_type=F32))
        dks.append(_dot_tn(ds, qe))
    return jnp.concatenate(dqs, axis=-1), jnp.concatenate(dks, axis=-1), jnp.concatenate(dvs, axis=-1), dsg


def _sg_fn(u, v, sgc, ln_g, ln_b, ws, bs):
    mu = jnp.mean(v, axis=-1, keepdims=True)
    xc = v - mu
    vn = xc * lax.rsqrt(jnp.mean(xc * xc, axis=-1, keepdims=True) + EPS) * ln_g + ln_b
    r = lax.broadcasted_iota(jnp.int32, (SGC, SGC), 0)
    c = lax.broadcasted_iota(jnp.int32, (SGC, SGC), 1)
    wt = [jnp.where(r >= c, w, 0.0) for w in ws]
    row_blocks = []
    for ch in range(u.shape[0] // SGC):
        col_blocks = []
        for g in range(SGG):
            blk = vn[SGC * ch:SGC * (ch + 1), LANES * g:LANES * (g + 1)]
            col_blocks.append(_mm(wt[g], blk) + bs[g])
        row_blocks.append(jnp.concatenate(col_blocks, axis=-1))
    mixed = jnp.concatenate(row_blocks, axis=0)
    return (u * mixed) * jax.nn.silu(sgc)


def _memkv_fn(mem, mem_g, wm, kg):
    kv = _mm(_rms_n(mem, mem_g, D), wm)
    ks = [_rms_n(kv[:, MHD * h:MHD * (h + 1)], kg, MHD) for h in range(MH)]
    return jnp.concatenate(ks, axis=-1), kv[:, MH * MHD:]


def _mem_fn(mq, sgd, k, v, qg):
    outs = []
    for h in range(MH):
        sl = slice(MHD * h, MHD * (h + 1))
        qh = _rms_n(mq[:, sl], qg, MHD)
        p = _softmax(_mm_nt(qh, k[:, sl]) * (MHD ** -0.5))
        outs.append(_mm(p, v[:, sl]))
    return jnp.concatenate(outs, axis=-1) * jax.nn.silu(sgd)


def _merge_fn(ys, logits, bm, wb, wo):
    merged = None
    for n in range(NB):
        z = jnp.concatenate([_mm(ys[n], wb[j][n]) for j in range(N_CHIPS)], axis=-1)
        gate = jax.nn.sigmoid(logits[:, D * n:D * (n + 1)] + bm[n])
        merged = gate * z if merged is None else merged + gate * z
    return _mm(merged, wo)


def _proj_call(x, g, w):
    s_len = x.shape[0]
    tm, tn = min(s_len, 1024), NP // 4

    def body(x_ref, g_ref, w_ref, p_ref, h_ref):
        @pl.when(pl.program_id(1) == 0)
        def _():
            h_ref[...] = _rms_n(x_ref[...], g_ref[...], D).astype(h_ref.dtype)
        p_ref[...] = jnp.dot(h_ref[...], w_ref[...], preferred_element_type=F32)

    return pl.pallas_call(
        body, grid=(s_len // tm, NP // tn),
        in_specs=[_bs((tm, D), lambda i, j: (i, 0)), _bs((1, D), lambda i, j: (0, 0)), _bs((D, tn), lambda i, j: (0, j))],
        out_specs=[_bs((tm, tn), lambda i, j: (i, j)), _bs((tm, D), lambda i, j: (i, 0))],
        out_shape=[jax.ShapeDtypeStruct((s_len, NP), F32), jax.ShapeDtypeStruct((s_len, D), MM)],
        name="proj", compiler_params=_cparams(2))(x, g, w)


def _rope_tables(pos):
    half = ROPE // 2
    inv_freq = ROPE_THETA ** (-jnp.arange(half, dtype=F32) / half)
    ang = pos.astype(F32)[:, None] * inv_freq
    cos, sin = jnp.cos(ang), jnp.sin(ang)
    s_len = pos.shape[0]
    z = lambda n: jnp.zeros((s_len, n), F32)
    cos_t = jnp.concatenate([jnp.ones((s_len, NOPE), F32), cos, cos, z(LANES - QKH)], axis=1)
    sin_a = jnp.concatenate([z(NOPE), -sin, z(LANES - NOPE - half)], axis=1)
    sin_b = jnp.concatenate([z(NOPE + half), sin, z(LANES - QKH)], axis=1)
    return cos_t, sin_a, sin_b


def _mla_prep_specs(tm):
    row = lambda w, off: _bs((tm, w), lambda i: (i, off // w))
    full2 = lambda a, b: _bs((a, b), lambda i: (0, 0))
    full3 = lambda a, b, c: _bs((a, b, c), lambda i: (0, 0, 0))
    tab = _bs((tm, LANES), lambda i: (i, 0))
    return [row(QL, OFF_CQ), row(KVL, OFF_CKV), row(LANES, OFF_KR), tab, tab, tab,
            full2(1, QL), full2(1, KVL), full2(1, LANES), full2(1, LANES),
            full3(H, QL, LANES), full3(H, KVL, LANES), full2(KVL, H * LANES)]


def _mla_prep_args(body_refs, wrap=lambda w: w):
    (cq, ckv, kr, ct, sa, sb, cqg, ckvg, qg, kg, wuq, wkn, wv) = body_refs
    return (cq[...], ckv[...], kr[...], ct[...], sa[...], sb[...], cqg[...], ckvg[...], qg[...], kg[...],
            [wrap(wuq[h]) for h in range(H)], [wrap(wkn[h]) for h in range(H)], wrap(wv[...]))


def _mla_prep_call(proj, tabs, cq_g, ckv_g, qg, kg, wuq, wkn, wv):
    s_len = proj.shape[0]
    tm = min(s_len, 256)

    def body(*refs):
        q_ref, k_ref, v_ref = refs[13:]
        q, k, v = _mla_prep_fn(*_mla_prep_args(refs[:13]))
        q_ref[...] = q.astype(q_ref.dtype)
        k_ref[...] = k.astype(k_ref.dtype)
        v_ref[...] = v.astype(v_ref.dtype)

    out = _bs((tm, H * LANES), lambda i: (i, 0))
    return pl.pallas_call(
        body, grid=(s_len // tm,), in_specs=_mla_prep_specs(tm), out_specs=[out, out, out],
        out_shape=[jax.ShapeDtypeStruct((s_len, H * LANES), MM)] * 3,
        name="mla_prep", compiler_params=_cparams(1))(proj, proj, proj, *tabs, cq_g, ckv_g, qg, kg, wuq, wkn, wv)


def _mla_prep_bwd_call(proj, tabs, cq_g, ckv_g, qg, kg, wuq, wkn, wv, dq, dk, dv):
    s_len = proj.shape[0]
    tm = min(s_len, 256)

    def body(*refs):
        dq_ref, dk_ref, dv_ref = refs[13:16]
        dcq_ref, dckv_ref, dkr_ref, dcqg_ref, dckvg_ref, dqg_ref, dkg_ref, dwuq_ref, dwkn_ref, dwv_ref = refs[16:]
        _, vjp = jax.vjp(_mla_prep_fn, *_mla_prep_args(refs[:13], _with_slot))
        (dcq, dckv, dkr, _, _, _, dcqg, dckvg, dqg, dkg, dwuq, dwkn, dwv) = vjp((dq_ref[...], dk_ref[...], dv_ref[...]))
        dwuq, dwkn, dwv = [d[1] for d in dwuq], [d[1] for d in dwkn], dwv[1]
        dcq_ref[...] = dcq.astype(dcq_ref.dtype)
        dckv_ref[...] = dckv.astype(dckv_ref.dtype)
        dkr_ref[...] = dkr.astype(dkr_ref.dtype)

        @pl.when(pl.program_id(0) == 0)
        def _():
            for r in (dcqg_ref, dckvg_ref, dqg_ref, dkg_ref, dwuq_ref, dwkn_ref, dwv_ref):
                r[...] = jnp.zeros_like(r)
        dcqg_ref[...] += dcqg
        dckvg_ref[...] += dckvg
        dqg_ref[...] += dqg
        dkg_ref[...] += dkg
        for h in range(H):
            dwuq_ref[h] += dwuq[h]
            dwkn_ref[h] += dwkn[h]
        dwv_ref[...] += dwv

    big = _bs((tm, H * LANES), lambda i: (i, 0))
    row = lambda w: _bs((tm, w), lambda i: (i, 0))
    full2 = lambda a, b: _bs((a, b), lambda i: (0, 0))
    full3 = lambda a, b, c: _bs((a, b, c), lambda i: (0, 0, 0))
    sd = jax.ShapeDtypeStruct
    return pl.pallas_call(
        body, grid=(s_len // tm,), in_specs=_mla_prep_specs(tm) + [big, big, big],
        out_specs=[row(QL), row(KVL), row(LANES), full2(1, QL), full2(1, KVL), full2(1, LANES), full2(1, LANES),
                   full3(H, QL, LANES), full3(H, KVL, LANES), full2(KVL, H * LANES)],
        out_shape=[sd((s_len, QL), MM), sd((s_len, KVL), MM), sd((s_len, LANES), MM), sd((1, QL), F32), sd((1, KVL), F32),
                   sd((1, LANES), F32), sd((1, LANES), F32), sd((H, QL, LANES), F32), sd((H, KVL, LANES), F32),
                   sd((KVL, H * LANES), F32)],
        name="mla_prep_bwd", compiler_params=_cparams(1))(proj, proj, proj, *tabs, cq_g, ckv_g, qg, kg, wuq, wkn, wv, dq, dk, dv)


def _attn_specs(s_len, tq):
    pair = 2 * LANES
    return [_bs((tq, pair), lambda p, i: (i, p)), _bs((s_len, pair), lambda p, i: (0, p)), _bs((s_len, pair), lambda p, i: (0, p)),
            _bs((tq, LANES), lambda p, i: (i, OFF_SG // LANES + p))]


def _attn_call(q, k, v, proj):
    s_len = q.shape[0]
    tq = min(s_len, 256)

    def body(q_ref, k_ref, v_ref, sg_ref, y_ref, o_ref, lse_ref):
        for n in range(s_len // tq):
            @pl.when(pl.program_id(1) == n)
            def _():
                kl = (n + 1) * tq
                o, lse = _attn_pair_fwd(q_ref[...], k_ref[:kl, :], v_ref[:kl, :])
                y_ref[...] = (o * jax.nn.silu(sg_ref[...])).astype(y_ref.dtype)
                o_ref[...] = o
                lse_ref[...] = lse

    tile = _bs((tq, LANES), lambda p, i: (i, p))
    sd = jax.ShapeDtypeStruct
    return pl.pallas_call(
        body, grid=(H // 2, s_len // tq), in_specs=_attn_specs(s_len, tq), out_specs=[tile, tile, tile],
        out_shape=[sd((s_len, BW), MM), sd((s_len, BW), F32), sd((s_len, BW), F32)],
        name="attn", compiler_params=_cparams(2))(q, k, v, proj)


def _attn_bwd_call(q, k, v, proj, dys, o, lse):
    s_len = q.shape[0]
    tq = min(s_len, 256)
    pair = 2 * LANES

    def body(q_ref, k_ref, v_ref, sg_ref, dy_ref, o_ref, lse_ref, dq_ref, dk_ref, dv_ref, dsg_ref):
        i = pl.program_id(1)

        @pl.when(i == 0)
        def _():
            dk_ref[...] = jnp.zeros_like(dk_ref)
            dv_ref[...] = jnp.zeros_like(dv_ref)

        for n in range(s_len // tq):
            @pl.when(i == n)
            def _():
                kl = (n + 1) * tq
                dq, dk, dv, dsg = _attn_pair_bwd(q_ref[...], k_ref[:kl, :], v_ref[:kl, :], sg_ref[...], dy_ref[...],
                                                 o_ref[...], lse_ref[...])
                dq_ref[...] = dq
                dsg_ref[...] = dsg.astype(dsg_ref.dtype)
                dk_ref[:kl, :] += dk
                dv_ref[:kl, :] += dv

    sd = jax.ShapeDtypeStruct
    tile = _bs((tq, LANES), lambda p, i: (i, p))
    return pl.pallas_call(
        body, grid=(H // 2, s_len // tq),
        in_specs=_attn_specs(s_len, tq) + [tile, tile, tile],
        out_specs=[_bs((tq, pair), lambda p, i: (i, p)), _bs((s_len, pair), lambda p, i: (0, p)),
                   _bs((s_len, pair), lambda p, i: (0, p)), tile],
        out_shape=[sd((s_len, H * LANES), F32), sd((s_len, H * LANES), F32), sd((s_len, H * LANES), F32), sd((s_len, BW), MM)],
        name="attn_bwd", compiler_params=_cparams(2))(q, k, v, proj, dys, o, lse)


def _shift_down(a, n):
    r = lax.broadcasted_iota(jnp.int32, a.shape, 0)
    return jnp.where(r >= n, pltpu.roll(a, n, 0), 0.0)


def _shift_up(a, n):
    s_len = a.shape[0]
    r = lax.broadcasted_iota(jnp.int32, a.shape, 0)
    return jnp.where(r < s_len - n, pltpu.roll(a, s_len - n, 0), 0.0)


def _conv_specs(s_len):
    col = lambda off: _bs((s_len, LANES), lambda j: (0, off // LANES + j))
    return [col(OFF_CV), col(OFF_CV + CW), col(OFF_CV + 2 * CW), col(OFF_SG + BW),
            _bs((3, LANES), lambda j: (0, j)), _bs((1, LANES), lambda j: (0, j))]


def _conv_call(proj, cw, cb):
    s_len = proj.shape[0]

    def body(bg_ref, cg_ref, xi_ref, sg_ref, w_ref, b_ref, y_ref):
        z = cg_ref[...] * xi_ref[...]
        y = b_ref[...] + w_ref[0:1, :] * _shift_down(z, 2)
        y = y + w_ref[1:2, :] * _shift_down(z, 1)
        y = y + w_ref[2:3, :] * z
        y_ref[...] = ((bg_ref[...] * y) * jax.nn.silu(sg_ref[...])).astype(y_ref.dtype)

    return pl.pallas_call(
        body, grid=(CW // LANES,), in_specs=_conv_specs(s_len), out_specs=_bs((s_len, LANES), lambda j: (0, j)),
        out_shape=jax.ShapeDtypeStruct((s_len, CW), MM), name="conv", compiler_params=_cparams(1))(proj, proj, proj, proj, cw, cb)


def _conv_bwd_call(proj, cw, cb, dys):
    s_len = proj.shape[0]

    def body(bg_ref, cg_ref, xi_ref, sg_ref, w_ref, b_ref, dys_ref, dbg_ref, dcg_ref, dxi_ref, dsg_ref, dw_ref, db_ref):
        bg, cg, xi, sg = bg_ref[...], cg_ref[...], xi_ref[...], sg_ref[...]
        w0, w1, w2 = w_ref[0:1, :], w_ref[1:2, :], w_ref[2:3, :]
        z = cg * xi
        z1, z2 = _shift_down(z, 1), _shift_down(z, 2)
        y = b_ref[...] + w0 * z2
        y = y + w1 * z1
        y = y + w2 * z
        yb = bg * y
        sig = jax.nn.sigmoid(sg)
        silu = sg * sig
        dys_v = dys_ref[...]
        dsg_ref[...] = (dys_v * yb * (sig * (1.0 + sg * (1.0 - sig)))).astype(dsg_ref.dtype)
        dyb = dys_v * silu
        dbg_ref[...] = (dyb * y).astype(dbg_ref.dtype)
        dy = dyb * bg
        db_ref[...] = jnp.sum(dy, axis=0, keepdims=True)
        dw_ref[0:1, :] = jnp.sum(dy * z2, axis=0, keepdims=True)
        dw_ref[1:2, :] = jnp.sum(dy * z1, axis=0, keepdims=True)
        dw_ref[2:3, :] = jnp.sum(dy * z, axis=0, keepdims=True)
        dz = w2 * dy + w1 * _shift_up(dy, 1) + w0 * _shift_up(dy, 2)
        dcg_ref[...] = (dz * xi).astype(dcg_ref.dtype)
        dxi_ref[...] = (dz * cg).astype(dxi_ref.dtype)

    col = _bs((s_len, LANES), lambda j: (0, j))
    sd = jax.ShapeDtypeStruct
    return pl.pallas_call(
        body, grid=(CW // LANES,), in_specs=_conv_specs(s_len) + [col],
        out_specs=[col, col, col, col, _bs((3, LANES), lambda j: (0, j)), _bs((1, LANES), lambda j: (0, j))],
        out_shape=[sd((s_len, CW), MM)] * 4 + [sd((3, CW), F32), sd((1, CW), F32)],
        name="conv_bwd", compiler_params=_cparams(1))(proj, proj, proj, proj, cw, cb, dys)


def _sg_specs(tm):
    row = lambda off: _bs((tm, SGW), lambda i: (i, off // SGW))
    return [row(OFF_SGI), row(OFF_SGI + SGW), row(OFF_SG + 2 * BW), _bs((1, SGW), lambda i: (0, 0)), _bs((1, SGW), lambda i: (0, 0)),
            _bs((SGG, SGC, SGC), lambda i: (0, 0, 0)), _bs((SGG, SGC, 1), lambda i: (0, 0, 0))]


def _sg_args(refs):
    u, v, sg, lg, lb, ws, bs = refs
    return (u[...], v[...], sg[...], lg[...], lb[...], [ws[g] for g in range(SGG)], [bs[g] for g in range(SGG)])


def _sg_call(proj, ln_g, ln_b, ws, bs):
    s_len = proj.shape[0]
    tm = min(s_len, 256)

    def body(*refs):
        refs[7][...] = _sg_fn(*_sg_args(refs[:7])).astype(refs[7].dtype)

    return pl.pallas_call(
        body, grid=(s_len // tm,), in_specs=_sg_specs(tm), out_specs=_bs((tm, SGW), lambda i: (i, 0)),
        out_shape=jax.ShapeDtypeStruct((s_len, SGW), MM), name="sgmlp", compiler_params=_cparams(1))(proj, proj, proj, ln_g, ln_b, ws, bs)


def _sg_bwd_call(proj, ln_g, ln_b, ws, bs, dys):
    s_len = proj.shape[0]
    tm = min(s_len, 256)

    def body(*refs):
        dys_ref = refs[7]
        du_ref, dv_ref, dsg_ref, dlg_ref, dlb_ref, dws_ref, dbs_ref = refs[8:]
        _, vjp = jax.vjp(_sg_fn, *_sg_args(refs[:7]))
        du, dv, dsg, dlg, dlb, dws, dbs = vjp(dys_ref[...])
        du_ref[...] = du.astype(du_ref.dtype)
        dv_ref[...] = dv.astype(dv_ref.dtype)
        dsg_ref[...] = dsg.astype(dsg_ref.dtype)

        @pl.when(pl.program_id(0) == 0)
        def _():
            for r in (dlg_ref, dlb_ref, dws_ref, dbs_ref):
                r[...] = jnp.zeros_like(r)
        dlg_ref[...] += dlg
        dlb_ref[...] += dlb
        for g in range(SGG):
            dws_ref[g] += dws[g]
            dbs_ref[g] += dbs[g]

    row = _bs((tm, SGW), lambda i: (i, 0))
    sd = jax.ShapeDtypeStruct
    return pl.pallas_call(
        body, grid=(s_len // tm,), in_specs=_sg_specs(tm) + [row],
        out_specs=[row, row, row, _bs((1, SGW), lambda i: (0, 0)), _bs((1, SGW), lambda i: (0, 0)),
                   _bs((SGG, SGC, SGC), lambda i: (0, 0, 0)), _bs((SGG, SGC, 1), lambda i: (0, 0, 0))],
        out_shape=[sd((s_len, SGW), MM)] * 3 + [sd((1, SGW), F32), sd((1, SGW), F32), sd((SGG, SGC, SGC), F32), sd((SGG, SGC, 1), F32)],
        name="sgmlp_bwd", compiler_params=_cparams(1))(proj, proj, proj, ln_g, ln_b, ws, bs, dys)


def _memkv_call(mem, mem_g, wm, kg):
    m_len = mem.shape[0]

    def body(mem_ref, g_ref, w_ref, kg_ref, k_ref, v_ref):
        k, v = _memkv_fn(mem_ref[...], g_ref[...], w_ref[...], kg_ref[...])
        k_ref[...] = k.astype(k_ref.dtype)
        v_ref[...] = v.astype(v_ref.dtype)

    return pl.pallas_call(body, out_shape=[jax.ShapeDtypeStruct((m_len, MH * MHD), MM)] * 2, name="memkv",
                          compiler_params=pltpu.CompilerParams(vmem_limit_bytes=VMEM_LIMIT))(mem, mem_g, wm, kg)


def _memkv_bwd_call(mem, mem_g, wm, kg, dk, dv):
    def body(mem_ref, g_ref, w_ref, kg_ref, dk_ref, dv_ref, dg_ref, dw_ref, dkg_ref):
        _, vjp = jax.vjp(_memkv_fn, mem_ref[...], g_ref[...], _with_slot(w_ref[...]), kg_ref[...])
        _, dg, dw, dkg = vjp((dk_ref[...], dv_ref[...]))
        dg_ref[...] = dg
        dw_ref[...] = dw[1]
        dkg_ref[...] = dkg

    sd = jax.ShapeDtypeStruct
    return pl.pallas_call(body, out_shape=[sd((1, D), F32), sd((D, 2 * MH * MHD), F32), sd((1, MHD), F32)], name="memkv_bwd",
                          compiler_params=pltpu.CompilerParams(vmem_limit_bytes=VMEM_LIMIT))(mem, mem_g, wm, kg, dk, dv)


def _mem_specs(tm, m_len):
    w = MH * MHD
    return [_bs((tm, w), lambda i: (i, OFF_MQ // w)), _bs((tm, BW), lambda i: (i, (OFF_SG + 3 * BW) // BW)),
            _bs((m_len, w), lambda i: (0, 0)), _bs((m_len, w), lambda i: (0, 0)), _bs((1, MHD), lambda i: (0, 0))]


def _mem_call(proj, k, v, qg):
    s_len, m_len = proj.shape[0], k.shape[0]
    tm = min(s_len, 256)

    def body(mq_ref, sg_ref, k_ref, v_ref, qg_ref, y_ref):
        y_ref[...] = _mem_fn(mq_ref[...], sg_ref[...], k_ref[...], v_ref[...], qg_ref[...]).astype(y_ref.dtype)

    return pl.pallas_call(
        body, grid=(s_len // tm,), in_specs=_mem_specs(tm, m_len), out_specs=_bs((tm, BW), lambda i: (i, 0)),
        out_shape=jax.ShapeDtypeStruct((s_len, BW), MM), name="memattn", compiler_params=_cparams(1))(proj, proj, k, v, qg)


def _mem_bwd_call(proj, k, v, qg, dys):
    s_len, m_len = proj.shape[0], k.shape[0]
    tm = min(s_len, 256)
    w = MH * MHD

    def body(mq_ref, sg_ref, k_ref, v_ref, qg_ref, dys_ref, dmq_ref, dsg_ref, dk_ref, dv_ref, dqg_ref):
        _, vjp = jax.vjp(_mem_fn, mq_ref[...], sg_ref[...], k_ref[...].astype(F32), v_ref[...].astype(F32), qg_ref[...])
        dmq, dsg, dk, dv, dqg = vjp(dys_ref[...])
        dmq_ref[...] = dmq.astype(dmq_ref.dtype)
        dsg_ref[...] = dsg.astype(dsg_ref.dtype)

        @pl.when(pl.program_id(0) == 0)
        def _():
            for r in (dk_ref, dv_ref, dqg_ref):
                r[...] = jnp.zeros_like(r)
        dk_ref[...] += dk
        dv_ref[...] += dv
        dqg_ref[...] += dqg

    row = _bs((tm, BW), lambda i: (i, 0))
    kv = _bs((m_len, w), lambda i: (0, 0))
    sd = jax.ShapeDtypeStruct
    return pl.pallas_call(
        body, grid=(s_len // tm,), in_specs=_mem_specs(tm, m_len) + [row],
        out_specs=[row, row, kv, kv, _bs((1, MHD), lambda i: (0, 0))],
        out_shape=[sd((s_len, w), MM), sd((s_len, BW), MM), sd((m_len, w), F32), sd((m_len, w), F32), sd((1, MHD), F32)],
        name="memattn_bwd", compiler_params=_cparams(1))(proj, proj, k, v, qg, dys)


def _merge_specs(tm):
    row = _bs((tm, BW), lambda i: (i, 0))
    return [row, row, row, row, _bs((tm, NB * D), lambda i: (i, OFF_ML // (NB * D))), _bs((NB, D), lambda i: (0, 0)),
            _bs((N_CHIPS, NB, BW, D // N_CHIPS), lambda i: (0, 0, 0, 0)), _bs((D, D), lambda i: (0, 0))]


def _merge_call(ys, proj, bm, wb, wo, x):
    s_len = proj.shape[0]
    tm = min(s_len, 256)

    def body(ya, yb, yc, yd, lg_ref, bm_ref, wb_ref, wo_ref, x_ref, o_ref):
        out = _merge_fn([r[...] for r in (ya, yb, yc, yd)], lg_ref[...], [bm_ref[n:n + 1, :] for n in range(NB)],
                        [[wb_ref[j, n] for n in range(NB)] for j in range(N_CHIPS)], wo_ref[...])
        o_ref[...] = x_ref[...] + out

    xrow = _bs((tm, D), lambda i: (i, 0))
    return pl.pallas_call(
        body, grid=(s_len // tm,), in_specs=_merge_specs(tm) + [xrow], out_specs=xrow,
        out_shape=jax.ShapeDtypeStruct((s_len, D), F32), name="merge", compiler_params=_cparams(1))(*ys, proj, bm, wb, wo, x)


def _merge_bwd_call(ys, proj, bm, wb, wo, dout):
    s_len = proj.shape[0]
    tm = min(s_len, 256)

    def body(ya, yb, yc, yd, lg_ref, bm_ref, wb_ref, wo_ref, do_ref, dya, dyb, dyc, dyd, dlg_ref, dbm_ref, dwb_ref, dwo_ref):
        fn = lambda ys_, lg_, bm_, wb_, wo_: _merge_fn(ys_, lg_, bm_, wb_, wo_)
        _, vjp = jax.vjp(fn, [r[...].astype(F32) for r in (ya, yb, yc, yd)], lg_ref[...], [bm_ref[n:n + 1, :] for n in range(NB)],
                         [[_with_slot(wb_ref[j, n]) for n in range(NB)] for j in range(N_CHIPS)], _with_slot(wo_ref[...]))
        dys, dlg, dbm, dwb, dwo = vjp(do_ref[...])
        dwb, dwo = [[d[1] for d in row] for row in dwb], dwo[1]
        for r, d in zip((dya, dyb, dyc, dyd), dys):
            r[...] = d
        dlg_ref[...] = dlg.astype(dlg_ref.dtype)

        @pl.when(pl.program_id(0) == 0)
        def _():
            for r in (dbm_ref, dwb_ref, dwo_ref):
                r[...] = jnp.zeros_like(r)
        for n in range(NB):
            dbm_ref[n:n + 1, :] += dbm[n]
            for j in range(N_CHIPS):
                dwb_ref[j, n] += dwb[j][n]
        dwo_ref[...] += dwo

    row = _bs((tm, BW), lambda i: (i, 0))
    sd = jax.ShapeDtypeStruct
    wb_shape = (N_CHIPS, NB, BW, D // N_CHIPS)
    return pl.pallas_call(
        body, grid=(s_len // tm,), in_specs=_merge_specs(tm) + [_bs((tm, D), lambda i: (i, 0))],
        out_specs=[row, row, row, row, _bs((tm, NB * D), lambda i: (i, 0)), _bs((NB, D), lambda i: (0, 0)),
                   _bs(wb_shape, lambda i: (0, 0, 0, 0)), _bs((D, D), lambda i: (0, 0))],
        out_shape=[sd((s_len, BW), F32)] * 4 + [sd((s_len, NB * D), MM), sd((NB, D), F32), sd(wb_shape, F32), sd((D, D), F32)],
        name="merge_bwd", compiler_params=_cparams(1))(*ys, proj, bm, wb, wo, dout)


def _dh_call(dproj, w, x, g, dout, after=()):
    s_len = x.shape[0]
    tk = NP // 4
    after = list(after)

    def matmul_body(dp_ref, w_ref, *rest):
        o_ref = rest[-1]

        @pl.when(pl.program_id(0) == 0)
        def _():
            o_ref[...] = jnp.zeros_like(o_ref)
        o_ref[...] += lax.dot_general(dp_ref[...], w_ref[...], (((1,), (1,)), ((), ())), preferred_element_type=F32)

    dh = pl.pallas_call(
        matmul_body, grid=(NP // tk,),
        in_specs=[_bs((s_len, tk), lambda k: (0, k)), _bs((D, tk), lambda k: (0, k))] + [_ANY] * len(after),
        out_specs=_bs((s_len, D), lambda k: (0, 0)), out_shape=jax.ShapeDtypeStruct((s_len, D), F32),
        name="dh", compiler_params=_cparams(1))(dproj, w, *after)

    tm = min(s_len, 512)

    def norm_body(dh_ref, x_ref, g_ref, do_ref, dx_ref, dg_ref):
        _, vjp = jax.vjp(lambda x_, g_: _rms_n(x_, g_, D), x_ref[...], g_ref[...])
        dxr, dgr = vjp(dh_ref[...])
        dx_ref[...] = do_ref[...] + dxr

        @pl.when(pl.program_id(0) == 0)
        def _():
            dg_ref[...] = jnp.zeros_like(dg_ref)
        dg_ref[...] += dgr

    row = _bs((tm, D), lambda i: (i, 0))
    return pl.pallas_call(
        norm_body, grid=(s_len // tm,), in_specs=[row, row, _bs((1, D), lambda i: (0, 0)), row],
        out_specs=[row, _bs((1, D), lambda i: (0, 0))],
        out_shape=[jax.ShapeDtypeStruct((s_len, D), F32), jax.ShapeDtypeStruct((1, D), F32)],
        name="norm_bwd", compiler_params=_cparams(1))(dh, x, g, dout)


def _dw_call(h, dproj, after=()):
    s_len = h.shape[0]
    tn = 512
    after = list(after)

    def body(h_ref, dp_ref, *rest):
        rest[-1][...] = lax.dot_general(h_ref[...], dp_ref[...], (((0,), (0,)), ((), ())), preferred_element_type=F32)

    return pl.pallas_call(
        body, grid=(NP // tn,),
        in_specs=[_bs((s_len, D), lambda j: (0, 0)), _bs((s_len, tn), lambda j: (0, j))] + [_ANY] * len(after),
        out_specs=_bs((D, tn), lambda j: (0, j)), out_shape=jax.ShapeDtypeStruct((D, NP), F32),
        name="dw_in", compiler_params=_cparams(1))(h, dproj, *after)


def _loss_call(y, target):
    s_len = y.shape[0]
    tm = min(s_len, 512)

    def body(y_ref, t_ref, dy_ref, l_ref):
        e = y_ref[...] - t_ref[...]
        dy_ref[...] = e * (1.0 / D)

        @pl.when(pl.program_id(0) == 0)
        def _():
            l_ref[...] = jnp.zeros_like(l_ref)
        l_ref[...] += jnp.sum(e * e, axis=0, keepdims=True)

    row = _bs((tm, D), lambda i: (i, 0))
    return pl.pallas_call(
        body, grid=(s_len // tm,), in_specs=[row, row], out_specs=[row, _bs((1, D), lambda i: (0, 0))],
        out_shape=[jax.ShapeDtypeStruct((s_len, D), F32), jax.ShapeDtypeStruct((1, D), F32)],
        name="loss", compiler_params=_cparams(1))(y, target)


def _adamw_call(w, g, m, v, name):
    rows, cols = w.shape
    tr = min(_row_tile(rows), 128)

    def body(w_ref, g_ref, m_ref, v_ref, d_ref, nm_ref, nv_ref):
        gv = g_ref[...]
        m2 = ADAM_B1 * m_ref[...] + (1.0 - ADAM_B1) * gv
        v2 = ADAM_B2 * v_ref[...] + (1.0 - ADAM_B2) * (gv * gv)
        m_hat = m2 / (1.0 - ADAM_B1 ** ADAM_STEP)
        v_hat = v2 / (1.0 - ADAM_B2 ** ADAM_STEP)
        d_ref[...] = -ADAM_LR * (m_hat / (jnp.sqrt(v_hat) + ADAM_EPS) + ADAM_WD * w_ref[...])
        nm_ref[...] = m2
        nv_ref[...] = v2

    blk = _bs((tr, cols), lambda i: (i, 0))
    return pl.pallas_call(
        body, grid=(rows // tr,), in_specs=[blk] * 4, out_specs=[blk] * 3,
        out_shape=[jax.ShapeDtypeStruct((rows, cols), F32)] * 3, name=name, compiler_params=_cparams(1))(w, g, m, v)


def _adamw_layer_call(layer, ws, gs, ms, vs, prev, after, name, steps=8):
    n = len(ws)
    after = list(after)
    n_prev = 4 * n if prev is not None else 0

    def body(*refs):
        outs = refs[len(refs) - 4 * n:]
        for t in range(n):
            w_ref, g_ref, m_ref, v_ref = refs[t], refs[n + t], refs[2 * n + t], refs[3 * n + t]
            g_out, d_out, m_out, v_out = outs[4 * t:4 * t + 4]
            gv = g_ref[...]
            m2 = ADAM_B1 * m_ref[0] + (1.0 - ADAM_B1) * gv
            v2 = ADAM_B2 * v_ref[0] + (1.0 - ADAM_B2) * (gv * gv)
            m_hat = m2 / (1.0 - ADAM_B1 ** ADAM_STEP)
            v_hat = v2 / (1.0 - ADAM_B2 ** ADAM_STEP)
            g_out[0] = gv
            d_out[0] = -ADAM_LR * (m_hat / (jnp.sqrt(v_hat) + ADAM_EPS) + ADAM_WD * w_ref[0])
            m_out[0] = m2
            v_out[0] = v2

    def lay(a):
        return _bs((1, a.shape[1] // steps, a.shape[2]), lambda i: (layer, i, 0))

    in_specs = ([lay(a) for a in ws] + [_bs((g.shape[0] // steps, g.shape[1]), lambda i: (i, 0)) for g in gs]
                + [lay(a) for a in ms] + [lay(a) for a in vs] + [_ANY] * (n_prev + len(after)))
    return pl.pallas_call(
        body, grid=(steps,), in_specs=in_specs, out_specs=[lay(ws[t]) for t in range(n) for _ in range(4)],
        out_shape=[jax.ShapeDtypeStruct(ws[t].shape, F32) for t in range(n) for _ in range(4)],
        input_output_aliases={4 * n + q: q for q in range(n_prev)}, name=name, compiler_params=_cparams(1),
    )(*ws, *gs, *ms, *vs, *(prev if prev is not None else []), *after)


def _row_tile(rows):
    for cand in (512, 256, 128, 64, 32, 16, 8):
        if rows % cand == 0 and rows > cand:
            return cand
    return rows


def _pair_sum_call(grads, from_sibling, core, name):
    n = len(grads)

    def body(core_ref, *refs):
        for t in range(n):
            refs[2 * n + t][...] = (refs[t][...].astype(F32) + refs[n + t][...].astype(F32)).astype(MM)

    half = lambda g: (1, g.shape[1] // 2, g.shape[2])
    grid_spec = pltpu.PrefetchScalarGridSpec(
        num_scalar_prefetch=1, grid=(N_CHIPS,),
        in_specs=[pl.BlockSpec(half(g), lambda j, core_ref: (j, core_ref[0], 0)) for g in grads]
        + [pl.BlockSpec(half(g), lambda j, core_ref: (j, 0, 0)) for g in grads],
        out_specs=[pl.BlockSpec(half(g), lambda j, core_ref: (j, 0, 0)) for g in grads])
    return pl.pallas_call(
        body, grid_spec=grid_spec, out_shape=[jax.ShapeDtypeStruct((N_CHIPS,) + half(g)[1:], MM) for g in grads], name=name,
        compiler_params=_cparams(1))(core, *grads, *from_sibling)


def _owner_sum_call(chip_sums, from_chips, chip_core, name):
    n = len(chip_sums)
    steps = 4

    def body(ids_ref, *refs):
        for t in range(n):
            a, b = refs[t], refs[n + t]
            refs[2 * n + t][...] = ((a[0].astype(F32) + b[0].astype(F32)) + b[1].astype(F32)) + b[2].astype(F32)

    tile = lambda p: (p.shape[1] // steps, p.shape[2])
    grid_spec = pltpu.PrefetchScalarGridSpec(
        num_scalar_prefetch=1, grid=(steps,),
        in_specs=[pl.BlockSpec((1,) + tile(p), lambda i, ids_ref: (ids_ref[0], i, 0)) for p in chip_sums]
        + [pl.BlockSpec((3,) + tile(p), lambda i, ids_ref: (0, i, 0)) for p in chip_sums],
        out_specs=[pl.BlockSpec(tile(p), lambda i, ids_ref: (ids_ref[1] * steps + i, 0)) for p in chip_sums])
    return pl.pallas_call(
        body, grid_spec=grid_spec, out_shape=[jax.ShapeDtypeStruct((2 * p.shape[1], p.shape[2]), F32) for p in chip_sums],
        name=name, compiler_params=_cparams(1))(chip_core, *chip_sums, *from_chips)


def _sum8_call(parts):
    n, rows, cols = parts.shape
    tr = _row_tile(rows)

    def body(p_ref, o_ref):
        acc = p_ref[0]
        for k in range(1, n):
            acc = acc + p_ref[k]
        o_ref[...] = acc

    return pl.pallas_call(
        body, grid=(rows // tr,), in_specs=[_bs((n, tr, cols), lambda i: (0, i, 0))], out_specs=_bs((tr, cols), lambda i: (i, 0)),
        out_shape=jax.ShapeDtypeStruct((rows, cols), F32), name="sum_small_grads", compiler_params=_cparams(1))(parts)


_ANY = pl.BlockSpec(memory_space=pl.ANY)


def _half_rows(ref, lead, half, which):
    rows = pl.ds(pl.multiple_of(half * which, half), half)
    return ref.at[rows] if lead is None else ref.at[lead, rows]


_HBM = pl.BlockSpec(memory_space=pltpu.HBM)
_SEM = pl.BlockSpec(memory_space=pltpu.SEMAPHORE)
_ORDERED_EFFECT = pltpu.CompilerParams(has_side_effects=pltpu.SideEffectType.DATAFLOW_SIDE_EFFECTING)


_VMEM = pl.BlockSpec(memory_space=pltpu.VMEM)
_TOKEN = jax.ShapeDtypeStruct((8, LANES), F32)


def _in_hbm(a):
    return pltpu.with_memory_space_constraint(a, pltpu.HBM)


def _tie(small, token):
    return small + token[0:1, 0:1].reshape((1,) * small.ndim)


def _peer(k):
    x, y, c = lax.axis_index("x"), lax.axis_index("y"), lax.axis_index("c")
    bx, by, bc = (k >> 2) & 1, (k >> 1) & 1, k & 1
    return (x ^ bx if bx else x, y ^ by if by else y, c ^ bc if bc else c)


def _place_block_call(blk, index, name):
    rows, cols = blk.shape

    def body(idx_ref, b_ref, o_ref):
        o_ref[0] = b_ref[...]

    grid_spec = pltpu.PrefetchScalarGridSpec(
        num_scalar_prefetch=1, grid=(1,), in_specs=[pl.BlockSpec((rows, cols), lambda i, idx_ref: (0, 0))],
        out_specs=pl.BlockSpec((1, rows, cols), lambda i, idx_ref: (idx_ref[0], 0, 0)))
    return pl.pallas_call(body, grid_spec=grid_spec, out_shape=jax.ShapeDtypeStruct((8, rows, cols), blk.dtype), name=name,
                          compiler_params=_cparams(1))(index, blk)


def _small_gather_start_call(blk, buf, after, name):
    after = list(after)

    def body(*refs):
        b_ref, out_ref = refs[0], refs[2 + len(after)]
        send_sems, recv_sems, token = refs[3 + len(after):]
        x, y, c = lax.axis_index("x"), lax.axis_index("y"), lax.axis_index("c")
        for k in range(1, 8):
            pltpu.make_async_remote_copy(src_ref=b_ref, dst_ref=out_ref.at[4 * x + 2 * y + c], send_sem=send_sems.at[k - 1],
                                         recv_sem=recv_sems.at[k - 1], device_id=_peer(k), device_id_type=MESH_ID).start()
        token[...] = jnp.zeros_like(token)

    dma = pltpu.SemaphoreType.DMA
    return pl.pallas_call(
        body, out_shape=[pltpu.HBM(buf.shape, buf.dtype), dma((7,)), dma((7,)), _TOKEN],
        in_specs=[_HBM, _HBM] + [_ANY] * len(after), out_specs=[_HBM, _SEM, _SEM, _VMEM],
        input_output_aliases={1: 0}, name=name, compiler_params=_ORDERED_EFFECT)(_in_hbm(blk), _in_hbm(buf), *after)


def _small_gather_finish_call(blk, buf, send_sems, recv_sems, after, name):
    after = list(after)

    def body(*refs):
        b_ref, in_ref, send_ref, recv_ref = refs[:4]
        x, y, c = lax.axis_index("x"), lax.axis_index("y"), lax.axis_index("c")
        for k in range(1, 8):
            px, py, pc = _peer(k)
            pltpu.make_async_remote_copy(src_ref=b_ref, dst_ref=in_ref.at[4 * px + 2 * py + pc], send_sem=send_ref.at[k - 1],
                                         recv_sem=recv_ref.at[k - 1], device_id=(px, py, pc), device_id_type=MESH_ID).wait()

    return pl.pallas_call(
        body, out_shape=pltpu.HBM(buf.shape, buf.dtype), in_specs=[_HBM, _HBM, _SEM, _SEM] + [_ANY] * len(after),
        out_specs=_HBM, input_output_aliases={1: 0}, name=name, compiler_params=_ORDERED_EFFECT,
    )(_in_hbm(blk), buf, send_sems, recv_sems, *after)


def _pair_exchange_start_call(grads, name):
    n = len(grads)
    half = [g.shape[1] // 2 for g in grads]

    def body(*refs):
        srcs, outs = refs[:n], refs[n:2 * n]
        send_sems, recv_sems, token = refs[2 * n:]
        x, y, c = lax.axis_index("x"), lax.axis_index("y"), lax.axis_index("c")
        for t in range(n):
            pltpu.make_async_remote_copy(
                src_ref=srcs[t].at[:, pl.ds(pl.multiple_of(half[t] * (1 - c), half[t]), half[t])], dst_ref=outs[t],
                send_sem=send_sems.at[t], recv_sem=recv_sems.at[t], device_id=(x, y, 1 - c), device_id_type=MESH_ID).start()
        token[...] = jnp.zeros_like(token)

    dma = pltpu.SemaphoreType.DMA
    return pl.pallas_call(
        body, out_shape=[pltpu.HBM((g.shape[0], g.shape[1] // 2, g.shape[2]), g.dtype) for g in grads] + [dma((n,)), dma((n,)), _TOKEN],
        in_specs=[_HBM] * n, out_specs=[_HBM] * n + [_SEM, _SEM, _VMEM], name=name, compiler_params=_ORDERED_EFFECT,
    )(*[_in_hbm(g) for g in grads])


def _pair_exchange_finish_call(grads, bufs, send_sems, recv_sems, after, name):
    n = len(grads)
    after = list(after)
    half = [g.shape[1] // 2 for g in grads]

    def body(*refs):
        srcs, ins, send_ref, recv_ref = refs[:n], refs[n:2 * n], refs[2 * n], refs[2 * n + 1]
        x, y, c = lax.axis_index("x"), lax.axis_index("y"), lax.axis_index("c")
        for t in range(n):
            pltpu.make_async_remote_copy(
                src_ref=srcs[t].at[:, pl.ds(pl.multiple_of(half[t] * (1 - c), half[t]), half[t])], dst_ref=ins[t],
                send_sem=send_ref.at[t], recv_sem=recv_ref.at[t], device_id=(x, y, 1 - c), device_id_type=MESH_ID).wait()

    return pl.pallas_call(
        body, out_shape=[pltpu.HBM(b.shape, b.dtype) for b in bufs],
        in_specs=[_HBM] * (2 * n) + [_SEM, _SEM] + [_ANY] * len(after), out_specs=[_HBM] * n,
        input_output_aliases={n + t: t for t in range(n)}, name=name, compiler_params=_ORDERED_EFFECT,
    )(*[_in_hbm(g) for g in grads], *bufs, send_sems, recv_sems, *after)


def _chip_scatter_start_call(chip_sums, name):
    n = len(chip_sums)

    def body(*refs):
        srcs, outs = refs[:n], refs[n:2 * n]
        send_sems, recv_sems, token = refs[2 * n:]
        x, y, c = lax.axis_index("x"), lax.axis_index("y"), lax.axis_index("c")
        chips = [(1 - x, y), (x, 1 - y), (1 - x, 1 - y)]
        for k, (cx, cy) in enumerate(chips):
            for t in range(n):
                pltpu.make_async_remote_copy(
                    src_ref=srcs[t].at[2 * cx + cy], dst_ref=outs[t].at[k], send_sem=send_sems.at[3 * t + k],
                    recv_sem=recv_sems.at[3 * t + k], device_id=(cx, cy, c), device_id_type=MESH_ID).start()
        token[...] = jnp.zeros_like(token)

    dma = pltpu.SemaphoreType.DMA
    return pl.pallas_call(
        body, out_shape=[pltpu.HBM((3,) + p.shape[1:], p.dtype) for p in chip_sums] + [dma((3 * n,)), dma((3 * n,)), _TOKEN],
        in_specs=[_HBM] * n, out_specs=[_HBM] * n + [_SEM, _SEM, _VMEM], name=name, compiler_params=_ORDERED_EFFECT,
    )(*[_in_hbm(p) for p in chip_sums])


def _chip_scatter_finish_call(chip_sums, bufs, send_sems, recv_sems, after, name):
    n = len(chip_sums)
    after = list(after)

    def body(*refs):
        srcs, ins, send_ref, recv_ref = refs[:n], refs[n:2 * n], refs[2 * n], refs[2 * n + 1]
        x, y, c = lax.axis_index("x"), lax.axis_index("y"), lax.axis_index("c")
        chips = [(1 - x, y), (x, 1 - y), (1 - x, 1 - y)]
        for k, (cx, cy) in enumerate(chips):
            for t in range(n):
                pltpu.make_async_remote_copy(
                    src_ref=srcs[t].at[2 * cx + cy], dst_ref=ins[t].at[k], send_sem=send_ref.at[3 * t + k],
                    recv_sem=recv_ref.at[3 * t + k], device_id=(cx, cy, c), device_id_type=MESH_ID).wait()

    return pl.pallas_call(
        body, out_shape=[pltpu.HBM(b.shape, b.dtype) for b in bufs],
        in_specs=[_HBM] * (2 * n) + [_SEM, _SEM] + [_ANY] * len(after), out_specs=[_HBM] * n,
        input_output_aliases={n + t: t for t in range(n)}, name=name, compiler_params=_ORDERED_EFFECT,
    )(*[_in_hbm(p) for p in chip_sums], *bufs, send_sems, recv_sems, *after)


def _place_own_call(mine, chip_core, name):
    n = len(mine)

    def body(ids_ref, *refs):
        for t in range(n):
            refs[n + t][0] = refs[t][...]

    def imap_out(s):
        pad = (0,) * (s.ndim - 1)
        return lambda i, ids_ref: (ids_ref[0], ids_ref[1]) + pad

    grid_spec = pltpu.PrefetchScalarGridSpec(
        num_scalar_prefetch=1, grid=(1,), in_specs=[pl.BlockSpec(s.shape, lambda i, ids_ref, k=s.ndim: (0,) * k) for s in mine],
        out_specs=[pl.BlockSpec((1,) + s.shape, imap_out(s)) for s in mine])
    return pl.pallas_call(
        body, grid_spec=grid_spec,
        out_shape=[jax.ShapeDtypeStruct((N_CHIPS, 2 * s.shape[0]) + s.shape[1:], s.dtype) for s in mine],
        name=name, compiler_params=_cparams(1))(chip_core, *mine)


def _gather_start_call(mine, bufs, after, name):
    n = len(mine)
    half = [s.shape[0] for s in mine]

    def body(*refs):
        srcs, outs = refs[:n], refs[2 * n + 1:3 * n + 1]
        send_sems, recv_sib, recv_ici, token = refs[3 * n + 1:]
        x, y, c = lax.axis_index("x"), lax.axis_index("y"), lax.axis_index("c")
        chips = [(1 - x, y), (x, 1 - y), (1 - x, 1 - y)]
        for t in range(n):
            dst = _half_rows(outs[t], 2 * x + y, half[t], c)
            pltpu.make_async_remote_copy(src_ref=srcs[t], dst_ref=dst, send_sem=send_sems.at[4 * t], recv_sem=recv_sib.at[t],
                                         device_id=(x, y, 1 - c), device_id_type=MESH_ID).start()
            for j, chip in enumerate(chips):
                pltpu.make_async_remote_copy(src_ref=srcs[t], dst_ref=dst, send_sem=send_sems.at[4 * t + 1 + j],
                                             recv_sem=recv_ici.at[3 * t + j], device_id=(*chip, c), device_id_type=MESH_ID).start()
        token[...] = jnp.zeros_like(token)

    dma = pltpu.SemaphoreType.DMA
    return pl.pallas_call(
        body, out_shape=[pltpu.HBM(b.shape, b.dtype) for b in bufs] + [dma((4 * n,)), dma((n,)), dma((3 * n,)), _TOKEN],
        in_specs=[_HBM] * (2 * n) + [_ANY], out_specs=[_HBM] * n + [_SEM] * 3 + [_VMEM],
        input_output_aliases={n + t: t for t in range(n)}, name=name, compiler_params=_ORDERED_EFFECT,
    )(*[_in_hbm(s) for s in mine], *[_in_hbm(b) for b in bufs], after)


def _gather_forward_call(bufs, recv_ici, after, name):
    n = len(bufs)
    half = [b.shape[1] // 2 for b in bufs]

    def body(*refs):
        ins, recv_ici_ref = refs[:n], refs[n]
        outs = refs[n + 2:2 * n + 2]
        send_fwd, recv_fwd, token = refs[2 * n + 2:]
        x, y, c = lax.axis_index("x"), lax.axis_index("y"), lax.axis_index("c")
        chips = [(1 - x, y), (x, 1 - y), (1 - x, 1 - y)]
        for j, (cx, cy) in enumerate(chips):
            for t in range(n):
                landed = _half_rows(ins[t], 2 * cx + cy, half[t], c)
                dst = _half_rows(outs[t], 2 * cx + cy, half[t], c)
                pltpu.make_async_remote_copy(src_ref=landed, dst_ref=landed, send_sem=send_fwd.at[3 * t + j],
                                             recv_sem=recv_ici_ref.at[3 * t + j], device_id=(cx, cy, c),
                                             device_id_type=MESH_ID).wait_recv()
                pltpu.make_async_remote_copy(src_ref=landed, dst_ref=dst, send_sem=send_fwd.at[3 * t + j],
                                             recv_sem=recv_fwd.at[3 * t + j], device_id=(x, y, 1 - c),
                                             device_id_type=MESH_ID).start()
        token[...] = jnp.zeros_like(token)

    dma = pltpu.SemaphoreType.DMA
    return pl.pallas_call(
        body, out_shape=[pltpu.HBM(b.shape, b.dtype) for b in bufs] + [dma((3 * n,)), dma((3 * n,)), _TOKEN],
        in_specs=[_HBM] * n + [_SEM, _ANY], out_specs=[_HBM] * n + [_SEM] * 2 + [_VMEM],
        input_output_aliases={t: t for t in range(n)}, name=name, compiler_params=_ORDERED_EFFECT,
    )(*bufs, recv_ici, after)


def _gather_finish_call(shards, bufs, send_sems, recv_sib, send_fwd, recv_fwd, after, name):
    n = len(bufs)
    half = [b.shape[1] // 2 for b in bufs]

    def body(*refs):
        srcs, ins = refs[:n], refs[n:2 * n]
        send_ref, recv_sib_ref, send_fwd_ref, recv_fwd_ref = refs[2 * n:2 * n + 4]
        x, y, c = lax.axis_index("x"), lax.axis_index("y"), lax.axis_index("c")
        chips = [(1 - x, y), (x, 1 - y), (1 - x, 1 - y)]
        sibling = (x, y, 1 - c)
        for t in range(n):
            for k in range(4):
                pltpu.make_async_remote_copy(src_ref=srcs[t], dst_ref=srcs[t], send_sem=send_ref.at[4 * t + k],
                                             recv_sem=recv_sib_ref.at[t], device_id=sibling, device_id_type=MESH_ID).wait_send()
            from_sibling = _half_rows(ins[t], 2 * x + y, half[t], 1 - c)
            pltpu.make_async_remote_copy(src_ref=from_sibling, dst_ref=from_sibling, send_sem=send_ref.at[4 * t],
                                         recv_sem=recv_sib_ref.at[t], device_id=sibling, device_id_type=MESH_ID).wait_recv()
            for j, (cx, cy) in enumerate(chips):
                sent = _half_rows(ins[t], 2 * cx + cy, half[t], c)
                passed = _half_rows(ins[t], 2 * cx + cy, half[t], 1 - c)
                pltpu.make_async_remote_copy(src_ref=sent, dst_ref=passed, send_sem=send_fwd_ref.at[3 * t + j],
                                             recv_sem=recv_fwd_ref.at[3 * t + j], device_id=sibling, device_id_type=MESH_ID).wait()

    return pl.pallas_call(
        body, out_shape=[pltpu.HBM(b.shape, b.dtype) for b in bufs],
        in_specs=[_HBM] * (2 * n) + [_SEM] * 4 + [_ANY], out_specs=[_HBM] * n,
        input_output_aliases={n + t: t for t in range(n)}, name=name, compiler_params=_ORDERED_EFFECT,
    )(*[_in_hbm(s) for s in shards], *bufs, send_sems, recv_sib, send_fwd, recv_fwd, after)


def _pair_gather_call(bufs, name):
    n = len(bufs)
    half = [b.shape[0] // 2 for b in bufs]

    def body(*refs):
        srcs, outs, send_sems, recv_sems = refs[:n], refs[n:2 * n], refs[2 * n], refs[2 * n + 1]
        x, y, c = lax.axis_index("x"), lax.axis_index("y"), lax.axis_index("c")
        for t in range(n):
            pltpu.make_async_remote_copy(
                src_ref=_half_rows(srcs[t], None, half[t], c), dst_ref=_half_rows(outs[t], None, half[t], c),
                send_sem=send_sems.at[t], recv_sem=recv_sems.at[t], device_id=(x, y, 1 - c), device_id_type=MESH_ID).start()
        for t in range(n):
            pltpu.make_async_remote_copy(
                src_ref=_half_rows(srcs[t], None, half[t], c), dst_ref=_half_rows(outs[t], None, half[t], 1 - c),
                send_sem=send_sems.at[t], recv_sem=recv_sems.at[t], device_id=(x, y, 1 - c), device_id_type=MESH_ID).wait()

    return pl.pallas_call(
        body, out_shape=[jax.ShapeDtypeStruct(b.shape, b.dtype) for b in bufs], in_specs=[_ANY] * n, out_specs=[_ANY] * n,
        input_output_aliases={t: t for t in range(n)},
        scratch_shapes=[pltpu.SemaphoreType.DMA((n,)), pltpu.SemaphoreType.DMA((n,))], name=name)(*bufs)


def _pack_rows(flats, dtype, row_multiple):
    flat = jnp.concatenate([f.reshape(-1).astype(dtype) for f in flats])
    n = flat.shape[0]
    rows = -(-n // PACK_W)
    rows = -(-rows // row_multiple) * row_multiple
    return jnp.pad(flat, (0, rows * PACK_W - n)).reshape(rows, PACK_W)


def _unpack(flat, shapes):
    out, off = [], 0
    for shp in shapes:
        n = math.prod(shp)
        out.append(flat[off:off + n].reshape(shp))
        off += n
    return out


_W_IN_SEGMENTS = ((R_ML, R_END, OFF_ML), (R_SG, R_ML, OFF_SG), (R_CV, R_SGI, OFF_CV), (R_SGI, R_MQ, OFF_SGI), (R_MQ, R_SG, OFF_MQ),
                  (R_CQ, R_CKV, OFF_CQ), (R_CKV, R_KR, OFF_CKV), (R_KR, R_CV, OFF_KR + NOPE))
W_IN_SHARD = R_END // N_CHIPS


def _realign_call(wg):
    tr = 128

    def body(w_ref, o_ref):
        pieces, pos = [], 0
        for r0, r1, a0 in _W_IN_SEGMENTS:
            if a0 > pos:
                pieces.append(jnp.zeros((tr, a0 - pos), o_ref.dtype))
            while r0 < r1:
                j = r0 // W_IN_SHARD
                hi = min(r1, (j + 1) * W_IN_SHARD)
                pieces.append(w_ref[j, :, r0 - j * W_IN_SHARD:hi - j * W_IN_SHARD])
                a0, r0 = a0 + hi - r0, hi
            pos = a0
        pieces.append(jnp.zeros((tr, NP - pos), o_ref.dtype))
        o_ref[...] = jnp.concatenate(pieces, axis=1)

    return pl.pallas_call(
        body, grid=(D // tr,), in_specs=[_bs((N_CHIPS, tr, W_IN_SHARD), lambda i: (0, i, 0))],
        out_specs=_bs((tr, NP), lambda i: (i, 0)), out_shape=jax.ShapeDtypeStruct((D, NP), wg.dtype),
        name="w_in_realign", compiler_params=_cparams(1))(wg)


def _unalign_call(dw, out_dtype):
    tr = 128
    by_ref = sorted(_W_IN_SEGMENTS)

    def body(dw_ref, o_ref):
        for j in range(N_CHIPS):
            lo_j, hi_j = j * W_IN_SHARD, (j + 1) * W_IN_SHARD
            pieces = []
            for r0, r1, a0 in by_ref:
                lo, hi = max(r0, lo_j), min(r1, hi_j)
                if lo < hi:
                    pieces.append(dw_ref[:, a0 + lo - r0:a0 + hi - r0])
            o_ref[j] = jnp.concatenate(pieces, axis=1).astype(o_ref.dtype)

    return pl.pallas_call(
        body, grid=(D // tr,), in_specs=[_bs((tr, NP), lambda i: (i, 0))],
        out_specs=_bs((N_CHIPS, tr, W_IN_SHARD), lambda i: (0, i, 0)),
        out_shape=jax.ShapeDtypeStruct((N_CHIPS, D, W_IN_SHARD), out_dtype), name="w_in_unalign", compiler_params=_cparams(1))(dw)


def _wuq_to_heads(w):
    w3 = w.reshape(QL, H, QKH)
    w3 = jnp.pad(w3, ((0, 0), (0, 0), (0, LANES - QKH)))
    return jnp.transpose(w3, (1, 0, 2))


def _wuq_from_heads(wh):
    return jnp.transpose(wh[:, :, :QKH], (1, 0, 2)).reshape(QL, H * QKH)


def _wukv_to_heads(w):
    w3 = w.reshape(KVL, H, NOPE + VH)
    wkn = jnp.transpose(jnp.pad(w3[:, :, :NOPE], ((0, 0), (0, 0), (0, LANES - NOPE))), (1, 0, 2))
    wv3 = w3[:, :, NOPE:]
    z = jnp.zeros((KVL, VH), w.dtype)
    cols = []
    for h in range(H):
        cols += [wv3[:, h], z] if h % 2 == 0 else [z, wv3[:, h]]
    return wkn, jnp.concatenate(cols, axis=1)


def _wukv_from_heads(wkn, wv):
    kn = jnp.transpose(wkn[:, :, :NOPE], (1, 0, 2))
    vs = jnp.stack([wv[:, LANES * h + VH * (h % 2):LANES * h + VH * (h % 2) + VH] for h in range(H)], axis=1)
    return jnp.concatenate([kn, vs], axis=2).reshape(KVL, H * (NOPE + VH))


def _layer_fwd(x, mem, tabs, p):
    proj, h = _proj_call(x, p["norm_g"], p["w_in"])
    if p.get("late") is not None:
        p = dict(p, **p["late"](proj))
    q, k, v = _mla_prep_call(proj, tabs, p["cq_g"], p["ckv_g"], p["qg"], p["kg"], p["wuq"], p["wkn"], p["wv"])
    ya, attn_o, attn_lse = _attn_call(q, k, v, proj)
    bm = p["bm"]
    if p.get("after_attn") is not None:
        bm = _tie(bm, p["after_attn"](ya))
    yb = _conv_call(proj, p["conv_w"], p["conv_b"])
    yc = _sg_call(proj, p["ln_g"], p["ln_b"], p["ws"], p["bs"])
    mk, mv = _memkv_call(mem, p["mem_g"], p["wm"], p["mkg"])
    yd = _mem_call(proj, mk, mv, p["mqg"])
    out = _merge_call((ya, yb, yc, yd), proj, bm, p["wb"], p["wo"], x)
    return out, dict(p=p, x=x, proj=proj, h=h, q=q, k=k, v=v, attn_o=attn_o, attn_lse=attn_lse, ys=(ya, yb, yc, yd), mk=mk, mv=mv)


def _layer_bwd(dout, mem, tabs, p, sv, start_after=None, on_rest_grads=None, on_grads=None):
    proj = sv["proj"]
    bm = p["bm"] if start_after is None else _tie(p["bm"], start_after)
    dya, dyb, dyc, dyd, dml, dbm, dwb, dwo = _merge_bwd_call(sv["ys"], proj, bm, p["wb"], p["wo"], dout)
    dq, dk, dv, dsg_a = _attn_bwd_call(sv["q"], sv["k"], sv["v"], proj, dya, sv["attn_o"], sv["attn_lse"])
    dcq, dckv, dkr, dcqg, dckvg, dqg, dkg, dwuq, dwkn, dwv = _mla_prep_bwd_call(
        proj, tabs, p["cq_g"], p["ckv_g"], p["qg"], p["kg"], p["wuq"], p["wkn"], p["wv"], dq, dk, dv)
    dbg, dcg, dxi, dsg_b, dcw, dcb = _conv_bwd_call(proj, p["conv_w"], p["conv_b"], dyb)
    du, dvv, dsg_c, dlg, dlb, dws, dbs = _sg_bwd_call(proj, p["ln_g"], p["ln_b"], p["ws"], p["bs"], dyc)
    dmq, dsg_d, dmk, dmv, dmqg = _mem_bwd_call(proj, sv["mk"], sv["mv"], p["mqg"], dyd)
    dmem_g, dwm, dmkg = _memkv_bwd_call(mem, p["mem_g"], p["wm"], p["mkg"], dmk, dmv)
    grads = dict(cq_norm_g=dcqg[0], ckv_norm_g=dckvg[0], mla_q_norm_g=dqg[0, :QKH], mla_k_norm_g=dkg[0, :QKH],
                 conv_w=dcw, conv_b=dcb[0], sg_ln_g=dlg[0], sg_ln_b=dlb[0], w_spatial=dws, b_spatial=dbs[:, :, 0],
                 mem_norm_g=dmem_g[0], mem_q_norm_g=dmqg[0], mem_k_norm_g=dmkg[0], b_merge=dbm,
                 wuq_heads=dwuq, wkn_heads=dwkn, wv_heads=dwv, w_mem_kv=dwm, w_branch_chips=dwb, w_out=dwo)
    started = [on_rest_grads(grads)] if on_rest_grads is not None else []
    dproj = jnp.concatenate([dml, dsg_a, dsg_b, dsg_c, dsg_d, dbg, dcg, dxi, du, dvv, dmq, dcq, dckv, dkr], axis=1)
    grads["w_in_aligned"] = _dw_call(sv["h"], dproj, started)
    tokens = on_grads(grads) if on_grads is not None else ()
    dx, dnorm_g = _dh_call(dproj, p["w_in"], sv["x"], p["norm_g"], dout, tokens)
    grads["norm_g"] = dnorm_g[0]
    return dx, grads


def _chips_to_cols(a):
    return jnp.concatenate([a[j] for j in range(N_CHIPS)], axis=1)


def _cols_to_chips(a):
    cols = a.shape[1] // N_CHIPS
    return jnp.stack([a[:, cols * j:cols * (j + 1)] for j in range(N_CHIPS)])


def _layer_params_first(l, rep, w_in_gathered, conv_w, b_merge):
    pad_g = lambda g: jnp.pad(g, (0, LANES - QKH)).reshape(1, LANES)
    return dict(
        norm_g=rep["norm_g"][l].reshape(1, D), w_in=_realign_call(w_in_gathered),
        cq_g=rep["cq_norm_g"][l].reshape(1, QL), ckv_g=rep["ckv_norm_g"][l].reshape(1, KVL),
        qg=pad_g(rep["mla_q_norm_g"][l]), kg=pad_g(rep["mla_k_norm_g"][l]),
        conv_w=conv_w, conv_b=rep["conv_b"][l].reshape(1, CW),
        ln_g=rep["sg_ln_g"][l].reshape(1, SGW), ln_b=rep["sg_ln_b"][l].reshape(1, SGW),
        ws=rep["w_spatial"][l], bs=rep["b_spatial"][l].reshape(SGG, SGC, 1),
        mem_g=rep["mem_norm_g"][l].reshape(1, D),
        mqg=rep["mem_q_norm_g"][l].reshape(1, MHD), mkg=rep["mem_k_norm_g"][l].reshape(1, MHD), bm=b_merge)


def _layer_params_rest(gathered):
    wkn, wv = _wukv_to_heads(_chips_to_cols(gathered["w_ukv"]))
    return dict(wuq=_wuq_to_heads(_chips_to_cols(gathered["w_uq"])), wkn=wkn, wv=wv,
                wm=gathered["w_mem_kv"].reshape(D, 2 * MH * MHD), wb=gathered["w_branch"], wo=gathered["w_out"].reshape(D, D))


def _layer_params(l, rep, gathered, conv_w, b_merge):
    return dict(_layer_params_first(l, rep, gathered["w_in"], conv_w, b_merge), **_layer_params_rest(gathered))


def _forward_backward(x, mem, pos, target, params, bwd_hooks=None):
    tabs = _rope_tables(pos)
    params = list(params)
    saved = []
    act = x
    for l in range(DEPTH):
        if callable(params[l]):
            params[l] = params[l](saved[-1], act)
        act, sv = _layer_fwd(act, mem, tabs, params[l])
        saved.append(sv)
    dy, sq = _loss_call(act, target)
    grads = [None] * DEPTH
    token = None
    for l in reversed(range(DEPTH)):
        hooks = dict(bwd_hooks[l]) if bwd_hooks else {}
        after_layer = hooks.pop("after_layer", None)
        dy, grads[l] = _layer_bwd(dy, mem, tabs, saved[l]["p"], saved[l], start_after=token, **hooks)
        token = after_layer(dy) if after_layer is not None else None
    return sq, dy, grads


_SHARDED_MM = ("w_in", "w_branch", "w_out", "w_mem_kv", "w_uq", "w_ukv")
_SHARDED_F32 = ("conv_w", "b_merge")
_REPLICATED = ("norm_g", "cq_norm_g", "ckv_norm_g", "mla_q_norm_g", "mla_k_norm_g", "conv_b", "sg_ln_g", "sg_ln_b",
               "w_spatial", "b_spatial", "mem_norm_g", "mem_q_norm_g", "mem_k_norm_g")
_ALL_REDUCED = _REPLICATED + _SHARDED_F32
_WEIGHTS = ("norm_g", "w_in", "cq_norm_g", "ckv_norm_g", "w_uq", "w_ukv", "mla_q_norm_g", "mla_k_norm_g", "conv_w", "conv_b",
            "sg_ln_g", "sg_ln_b", "w_spatial", "b_spatial", "mem_norm_g", "w_mem_kv", "mem_q_norm_g", "mem_k_norm_g",
            "b_merge", "w_branch", "w_out")
_SMALL = tuple(n for n in _WEIGHTS if n not in _SHARDED_MM)


class _SmallGather:
    def __init__(self, blk, after, tag):
        self.blk, self.tag = blk, tag
        x, y, c = lax.axis_index("x"), lax.axis_index("y"), lax.axis_index("c")
        own = _place_block_call(blk, (4 * x + 2 * y + c).astype(jnp.int32).reshape(1), tag + "place_own")
        self.buf, self.send, self.recv, self.token = _small_gather_start_call(blk, own, after, tag + "start")

    def finish(self, after):
        return _small_gather_finish_call(self.blk, self.buf, self.send, self.recv, after, self.tag + "finish")


def _small_sharded_weights(w, got):
    names = _SHARDED_F32
    per_chip = [_unpack(got[2 * j].reshape(-1), [w[n].shape for n in names]) for j in range(N_CHIPS)]
    return {n: jnp.concatenate([per_chip[j][t] for j in range(N_CHIPS)], axis=2) for t, n in enumerate(names)}


class _Gather:
    def __init__(self, w, layer, names, after, tag):
        self.names, self.tag = names, tag
        x, y, c = lax.axis_index("x"), lax.axis_index("y"), lax.axis_index("c")
        chip_core = jnp.stack([2 * x + y, c]).astype(jnp.int32)
        halves = [w[n].shape[1] // 2 for n in names]
        self.srcs = [lax.dynamic_slice_in_dim(w[n][layer], c * h, h, axis=0).astype(MM) for n, h in zip(names, halves)]
        k = len(names)
        out = _gather_start_call(self.srcs, _place_own_call(self.srcs, chip_core, tag + "place_own"), after, tag + "start")
        self.bufs, self.send, self.recv_sib, self.recv_ici, self.token = out[:k], out[k], out[k + 1], out[k + 2], out[k + 3]

    def pass_on(self, after):
        k = len(self.names)
        out = _gather_forward_call(self.bufs, self.recv_ici, after, self.tag + "forward")
        self.bufs, self.send_fwd, self.recv_fwd = out[:k], out[k], out[k + 1]
        return out[k + 2]

    def finish(self, after):
        got = _gather_finish_call(self.srcs, self.bufs, self.send, self.recv_sib, self.send_fwd, self.recv_fwd, after,
                                  self.tag + "finish")
        return dict(zip(self.names, got))


class _ReduceScatter:
    SLABS = dict(
        w_in=lambda g: _unalign_call(g["w_in_aligned"], MM),
        w_branch=lambda g: g["w_branch_chips"].reshape(N_CHIPS, NB * BW, D // N_CHIPS),
        w_out=lambda g: g["w_out"].reshape(N_CHIPS, D // N_CHIPS, D),
        w_mem_kv=lambda g: g["w_mem_kv"].reshape(N_CHIPS, D // N_CHIPS, 2 * MH * MHD),
        w_uq=lambda g: _cols_to_chips(_wuq_from_heads(g["wuq_heads"])),
        w_ukv=lambda g: _cols_to_chips(_wukv_from_heads(g["wkn_heads"], g["wv_heads"])))

    def __init__(self, tag, names):
        self.tag, self.names = tag, names

    def exchange(self, grads):
        self.tensors = [self.SLABS[n](grads) for n in self.names]
        n = len(self.tensors)
        out = _pair_exchange_start_call(self.tensors, self.tag + "exchange_start")
        self.ex_bufs, self.ex_send, self.ex_recv = out[:n], out[n], out[n + 1]
        return out[n + 2]

    def scatter(self, after):
        n = len(self.tensors)
        c = lax.axis_index("c")
        from_sibling = _pair_exchange_finish_call(self.tensors, self.ex_bufs, self.ex_send, self.ex_recv, after,
                                                  self.tag + "exchange_finish")
        self.chip_sums = _pair_sum_call(self.tensors, from_sibling, c.astype(jnp.int32).reshape(1), self.tag + "pair_sum")
        out = _chip_scatter_start_call(self.chip_sums, self.tag + "scatter_start")
        self.bufs, self.send_sems, self.recv_sems, self.token = out[:n], out[n], out[n + 1], out[n + 2]
        return self.token

    def finish(self, after):
        x, y, c = lax.axis_index("x"), lax.axis_index("y"), lax.axis_index("c")
        chip_core = jnp.stack([2 * x + y, c]).astype(jnp.int32)
        from_chips = _chip_scatter_finish_call(self.chip_sums, self.bufs, self.send_sems, self.recv_sems, after,
                                               self.tag + "scatter_finish")
        mine = _owner_sum_call(self.chip_sums, from_chips, chip_core, self.tag + "owner_sum")
        return dict(zip(self.names, _pair_gather_call(mine, self.tag + "pair_gather")))


def _small_sums(g, sq, got):
    total = _sum8_call(got).reshape(-1)
    parts = _unpack(total, [g[n].shape for n in _ALL_REDUCED] + [sq.shape])
    out = dict(zip(_ALL_REDUCED, parts))
    sq_total = parts[-1]
    chip = 2 * lax.axis_index("x") + lax.axis_index("y")
    for n in _SHARDED_F32:
        size = out[n].shape[2] // N_CHIPS
        out[n] = lax.dynamic_slice_in_dim(out[n], chip * size, size, axis=2)
    return out, sq_total


def _adamw_small(w, g, m, v):
    delta, new_m, new_v = {}, {}, {}
    shapes = [w[n].shape for n in _SMALL]
    pk = lambda t: _pack_rows([t[n] for n in _SMALL], F32, 64)
    d, nm, nv = _adamw_call(pk(w), pk(g), pk(m), pk(v), "adamw_small")
    for out, packed in ((delta, d), (new_m, nm), (new_v, nv)):
        out.update(zip(_SMALL, _unpack(packed.reshape(-1), shapes)))
    return delta, new_m, new_v


def kernel(x, mem, positions, norm_g, w_in, cq_norm_g, ckv_norm_g, w_uq, w_ukv, mla_q_norm_g, mla_k_norm_g, conv_w, conv_b, sg_ln_g, sg_ln_b, w_spatial, b_spatial, mem_norm_g, w_mem_kv, mem_q_norm_g, mem_k_norm_g, b_merge, w_branch, w_out, loss_target, m_norm_g, m_w_in, m_cq_norm_g, m_ckv_norm_g, m_w_uq, m_w_ukv, m_mla_q_norm_g, m_mla_k_norm_g, m_conv_w, m_conv_b, m_sg_ln_g, m_sg_ln_b, m_w_spatial, m_b_spatial, m_mem_norm_g, m_w_mem_kv, m_mem_q_norm_g, m_mem_k_norm_g, m_b_merge, m_w_branch, m_w_out, v_norm_g, v_w_in, v_cq_norm_g, v_ckv_norm_g, v_w_uq, v_w_ukv, v_mla_q_norm_g, v_mla_k_norm_g, v_conv_w, v_conv_b, v_sg_ln_g, v_sg_ln_b, v_w_spatial, v_b_spatial, v_mem_norm_g, v_w_mem_kv, v_mem_q_norm_g, v_mem_k_norm_g, v_b_merge, v_w_branch, v_w_out):
    w = dict(norm_g=norm_g, w_in=w_in, cq_norm_g=cq_norm_g, ckv_norm_g=ckv_norm_g, w_uq=w_uq, w_ukv=w_ukv,
             mla_q_norm_g=mla_q_norm_g, mla_k_norm_g=mla_k_norm_g, conv_w=conv_w, conv_b=conv_b, sg_ln_g=sg_ln_g,
             sg_ln_b=sg_ln_b, w_spatial=w_spatial, b_spatial=b_spatial, mem_norm_g=mem_norm_g, w_mem_kv=w_mem_kv,
             mem_q_norm_g=mem_q_norm_g, mem_k_norm_g=mem_k_norm_g, b_merge=b_merge, w_branch=w_branch, w_out=w_out)
    m = dict(norm_g=m_norm_g, w_in=m_w_in, cq_norm_g=m_cq_norm_g, ckv_norm_g=m_ckv_norm_g, w_uq=m_w_uq, w_ukv=m_w_ukv,
             mla_q_norm_g=m_mla_q_norm_g, mla_k_norm_g=m_mla_k_norm_g, conv_w=m_conv_w, conv_b=m_conv_b, sg_ln_g=m_sg_ln_g,
             sg_ln_b=m_sg_ln_b, w_spatial=m_w_spatial, b_spatial=m_b_spatial, mem_norm_g=m_mem_norm_g, w_mem_kv=m_w_mem_kv,
             mem_q_norm_g=m_mem_q_norm_g, mem_k_norm_g=m_mem_k_norm_g, b_merge=m_b_merge, w_branch=m_w_branch, w_out=m_w_out)
    v = dict(norm_g=v_norm_g, w_in=v_w_in, cq_norm_g=v_cq_norm_g, ckv_norm_g=v_ckv_norm_g, w_uq=v_w_uq, w_ukv=v_w_ukv,
             mla_q_norm_g=v_mla_q_norm_g, mla_k_norm_g=v_mla_k_norm_g, conv_w=v_conv_w, conv_b=v_conv_b, sg_ln_g=v_sg_ln_g,
             sg_ln_b=v_sg_ln_b, w_spatial=v_w_spatial, b_spatial=v_b_spatial, mem_norm_g=v_mem_norm_g, w_mem_kv=v_w_mem_kv,
             mem_q_norm_g=v_mem_q_norm_g, mem_k_norm_g=v_mem_k_norm_g, b_merge=v_b_merge, w_branch=v_w_branch, w_out=v_w_out)

    chip_core = jnp.stack([2 * lax.axis_index("x") + lax.axis_index("y"), lax.axis_index("c")]).astype(jnp.int32)

    first = _Gather(w, 0, ("w_in",), chip_core, "gather_l0_w_in_")
    rest = _Gather(w, 0, _SHARDED_MM[1:], first.token, "gather_l0_rest_")
    small_on_its_way = _SmallGather(_pack_rows([w[n] for n in _SHARDED_F32], F32, 8), [rest.token], "gather_small_weights_")
    later = _Gather(w, 1, _SHARDED_MM, small_on_its_way.token, "gather_l1_")
    w_in0 = first.finish(first.pass_on(later.token))["w_in"]
    small = {}

    def rest_of_layer0(proj0):
        landed = _layer_params_rest(rest.finish(rest.pass_on(proj0)))
        small.update(_small_sharded_weights(w, small_on_its_way.finish([landed["wo"]])))
        return dict(landed, conv_w=small["conv_w"][0], bm=small["b_merge"][0])

    def layer1_params(saved0, act0):
        return _layer_params(1, w, later.finish(act0), small["conv_w"][1], small["b_merge"][1])

    params0 = _layer_params_first(0, w, w_in0, None, None)
    params = [dict(params0, late=rest_of_layer0, after_attn=later.pass_on), layer1_params]
    others = _SHARDED_MM[1:]
    rs1 = _ReduceScatter("rs_l1_", _SHARDED_MM)
    rs0_rest, rs0_w_in = _ReduceScatter("rs_l0_rest_", others), _ReduceScatter("rs_l0_w_in_", ("w_in",))

    def layer0_grads_done(grads):
        return [rs0_rest.scatter([grads["w_in_aligned"]]), rs0_w_in.exchange(grads)]

    hooks = [dict(on_rest_grads=rs0_rest.exchange, on_grads=layer0_grads_done),
             dict(on_grads=lambda grads: [rs1.exchange(grads)], after_layer=lambda dy: rs1.scatter([dy]))]
    sq, grad_x, layer_grads = _forward_backward(x[0], mem[0], positions[0], loss_target[0], params, hooks)

    g_small = {n: jnp.stack([layer_grads[l][n] for l in range(DEPTH)]) for n in _ALL_REDUCED}
    small_grads = _SmallGather(_pack_rows([g_small[n] for n in _ALL_REDUCED] + [sq], F32, 64), [grad_x], "gather_small_grads_")
    scattering = rs0_w_in.scatter([grad_x, small_grads.token])
    shard_grads = {1: rs1.finish([scattering]), 0: rs0_rest.finish([scattering])}
    as3d = lambda a: a.reshape(DEPTH, -1, a.shape[-1])
    as2d = lambda a: a.reshape(-1, a.shape[-1])
    big = lambda t: [as3d(t[n]) for n in others]
    turned = lambda t: [jnp.swapaxes(t["w_in"], 1, 2)]
    assert W_IN_SHARD % (8 * 7) == 0

    def update_w_in(l, grad, prev):
        return _adamw_layer_call(l, turned(w), [grad.T], turned(m), turned(v), prev, [], "adamw_w_in_l%d" % l, steps=7)

    def update_others(l, prev):
        return _adamw_layer_call(l, big(w), [as2d(shard_grads[l][n]) for n in others], big(m), big(v), prev, [], "adamw_l%d" % l)

    upd = update_others(0, update_others(1, None))
    g, sq_total = _small_sums(g_small, sq, small_grads.finish([upd[0]]))
    loss = 0.5 / D * jnp.sum(sq_total)
    delta, new_m, new_v = _adamw_small(w, g, m, v)
    upd_in1 = update_w_in(1, shard_grads[1]["w_in"], None)
    w_in_grad0 = rs0_w_in.finish([grad_x, upd_in1[0], upd[0], delta["norm_g"]])["w_in"]
    upd_in = update_w_in(0, w_in_grad0, upd_in1)
    g["w_in"], delta["w_in"], new_m["w_in"], new_v["w_in"] = [jnp.swapaxes(a, 1, 2) for a in upd_in]
    for t, n in enumerate(others):
        g[n], delta[n], new_m[n], new_v[n] = [a.reshape(w[n].shape) for a in upd[4 * t:4 * t + 4]]
    return (loss, grad_x[None], *[g[n] for n in _WEIGHTS], *[delta[n] for n in _WEIGHTS],
            *[new_m[n] for n in _WEIGHTS], *[new_v[n] for n in _WEIGHTS])
```

```python
import functools
import math

import jax
import jax.numpy as jnp
from jax import lax
from jax.experimental import pallas as pl
from jax.experimental.pallas import tpu as pltpu

F32 = jnp.float32
MM = jnp.bfloat16

D = 1024
DEPTH = 2
EPS = 1e-6
H = 8
NOPE = 64
ROPE = 32
QKH = 96
VH = 64
QL = 256
KVL = 128
ROPE_THETA = 10000.0
CW = 512
SGW = 512
SGG = 4
SGC = 128
MH = 4
MHD = 128
NB = 4
BW = 512
NEG_INF = -1e30
LANES = 128
N_CHIPS = 4

R_CQ, R_CKV, R_KR, R_CV, R_SGI, R_MQ, R_SG, R_ML, R_END = 0, 256, 384, 416, 1952, 2976, 3488, 5536, 9632
OFF_ML, OFF_SG, OFF_CV, OFF_SGI, OFF_MQ, OFF_CQ, OFF_CKV, OFF_KR, NP = 0, 4096, 6144, 7680, 8704, 9216, 9472, 9600, 9728

ADAM_LR = 0.001
ADAM_B1 = 0.9
ADAM_B2 = 0.999
ADAM_EPS = 1e-08
ADAM_WD = 0.01
ADAM_STEP = 10

VMEM_LIMIT = 56 * 1024 * 1024
PACK_W = 512
MESH_ID = pl.DeviceIdType.MESH


def _cparams(n_axes):
    return pltpu.CompilerParams(dimension_semantics=("arbitrary",) * n_axes, vmem_limit_bytes=VMEM_LIMIT)


def _bs(shape, imap):
    return pl.BlockSpec(shape, imap)


@jax.custom_vjp
def _mm_plain(a, b):
    return jnp.dot(a.astype(MM), b.astype(MM), preferred_element_type=F32)


def _mm_plain_fwd(a, b):
    return _mm_plain(a, b), (a, b)


def _mm_plain_bwd(res, g):
    a, b = res
    gm = g.astype(MM)
    da = lax.dot_general(gm, b.astype(MM), (((1,), (1,)), ((), ())), preferred_element_type=F32)
    db = lax.dot_general(a.astype(MM), gm, (((0,), (0,)), ((), ())), preferred_element_type=F32)
    return da.astype(a.dtype), db.astype(b.dtype)


_mm_plain.defvjp(_mm_plain_fwd, _mm_plain_bwd)


@jax.custom_vjp
def _mm_slot(a, w, slot):
    return jnp.dot(a.astype(MM), w.astype(MM), preferred_element_type=F32)


def _mm_slot_fwd(a, w, slot):
    return _mm_slot(a, w, slot), (a, w)


def _mm_slot_bwd(res, g):
    a, w = res
    gm = g.astype(MM)
    da = lax.dot_general(gm, w.astype(MM), (((1,), (1,)), ((), ())), preferred_element_type=F32)
    dw = lax.dot_general(a.astype(MM), gm, (((0,), (0,)), ((), ())), preferred_element_type=F32)
    return da.astype(a.dtype), jnp.zeros_like(w), dw


_mm_slot.defvjp(_mm_slot_fwd, _mm_slot_bwd)


def _mm(a, b):
    if isinstance(b, tuple):
        return _mm_slot(a, b[0], b[1])
    return _mm_plain(a, b)


def _with_slot(w):
    return (w, jnp.zeros(w.shape, F32))


@jax.custom_vjp
def _mm_nt(a, b):
    return lax.dot_general(a.astype(MM), b.astype(MM), (((1,), (1,)), ((), ())), preferred_element_type=F32)


def _mm_nt_fwd(a, b):
    return _mm_nt(a, b), (a, b)


def _mm_nt_bwd(res, g):
    a, b = res
    gm = g.astype(MM)
    da = jnp.dot(gm, b.astype(MM), preferred_element_type=F32)
    db = lax.dot_general(gm, a.astype(MM), (((0,), (0,)), ((), ())), preferred_element_type=F32)
    return da.astype(a.dtype), db.astype(b.dtype)


_mm_nt.defvjp(_mm_nt_fwd, _mm_nt_bwd)


@functools.partial(jax.custom_vjp, nondiff_argnums=(1,))
def _lane_roll(x, shift):
    return pltpu.roll(x, shift, 1)


def _lane_roll_fwd(x, shift):
    return pltpu.roll(x, shift, 1), None


def _lane_roll_bwd(shift, _, g):
    return (pltpu.roll(g, (LANES - shift) % LANES, 1),)


_lane_roll.defvjp(_lane_roll_fwd, _lane_roll_bwd)


def _rms_n(x, g, n):
    ms = jnp.sum(x * x, axis=-1, keepdims=True) * (1.0 / n)
    return x * lax.rsqrt(ms + EPS) * g


def _softmax(s):
    m = jnp.max(s, axis=-1, keepdims=True)
    e = jnp.exp(s - m)
    return e / jnp.sum(e, axis=-1, keepdims=True)


def _rope(t, cos_t, sin_a, sin_b):
    return t * cos_t + _lane_roll(t, LANES - 16) * sin_a + _lane_roll(t, 16) * sin_b


def _mla_prep_fn(cq, ckv, kr, cos_t, sin_a, sin_b, cq_g, ckv_g, qg, kg, wuq, wkn, wv):
    cqn = _rms_n(cq, cq_g, QL)
    ckvn = _rms_n(ckv, ckv_g, KVL)
    lane = lax.broadcasted_iota(jnp.int32, kr.shape, 1)
    krm = jnp.where((lane >= NOPE) & (lane < QKH), kr, 0.0)
    qs, ks = [], []
    for h in range(H):
        qh = _rms_n(_mm(cqn, wuq[h]), qg, QKH)
        qs.append(_rope(qh, cos_t, sin_a, sin_b) * (QKH ** -0.5))
        kh = _rms_n(_mm(ckvn, wkn[h]) + krm, kg, QKH)
        ks.append(_rope(kh, cos_t, sin_a, sin_b))
    return jnp.concatenate(qs, axis=-1), jnp.concatenate(ks, axis=-1), _mm(ckvn, wv)


def _dot_nt(a, b):
    return lax.dot_general(a.astype(MM), b.astype(MM), (((1,), (1,)), ((), ())), preferred_element_type=F32)


def _dot_tn(a, b):
    return lax.dot_general(a.astype(MM), b.astype(MM), (((0,), (0,)), ((), ())), preferred_element_type=F32)


def _causal_scores(qe, ke):
    tq, kl = qe.shape[0], ke.shape[0]
    s = _dot_nt(qe, ke)
    rows = lax.broadcasted_iota(jnp.int32, (tq, tq), 0)
    cols = lax.broadcasted_iota(jnp.int32, (tq, tq), 1)
    own = jnp.where(cols <= rows, s[:, kl - tq:], NEG_INF)
    return own if kl == tq else jnp.concatenate([s[:, :kl - tq], own], axis=1)


def _head_lanes(e, shape):
    lane = lax.broadcasted_iota(jnp.int32, shape, len(shape) - 1)
    return (lane >= VH * e) & (lane < VH * (e + 1))


def _attn_pair_fwd(q2, k2, v2):
    tq = q2.shape[0]
    o = jnp.zeros((tq, LANES), F32)
    lse = jnp.zeros((tq, LANES), F32)
    for e in range(2):
        sl = slice(LANES * e, LANES * (e + 1))
        s = _causal_scores(q2[:, sl], k2[:, sl])
        m = jnp.max(s, axis=-1, keepdims=True)
        ex = jnp.exp(s - m)
        l = jnp.sum(ex, axis=-1, keepdims=True)
        ve = jnp.where(_head_lanes(e, v2[:, sl].shape), v2[:, sl], 0.0)
        o = o + jnp.dot((ex * (1.0 / l)).astype(MM), ve.astype(MM), preferred_element_type=F32)
        lse = jnp.where(_head_lanes(e, lse.shape), m + jnp.log(l), lse)
    return o, lse


def _attn_pair_bwd(q2, k2, v2, sg, dys, o, lse):
    sig = jax.nn.sigmoid(sg)
    do = dys * (sg * sig)
    dsg = dys * o * (sig * (1.0 + sg * (1.0 - sig)))
    dqs, dks, dvs = [], [], []
    for e in range(2):
        sl = slice(LANES * e, LANES * (e + 1))
        qe, ke = q2[:, sl], k2[:, sl]
        hm = _head_lanes(e, o.shape)
        lse_e = jnp.max(jnp.where(hm, lse, NEG_INF), axis=-1, keepdims=True)
        do_e = jnp.where(hm, do, 0.0)
        delta = jnp.sum(do_e * o, axis=-1, keepdims=True)
        p = jnp.exp(_causal_scores(qe, ke) - lse_e)
        ve = jnp.where(_head_lanes(e, v2[:, sl].shape), v2[:, sl], 0.0)
        dvs.append(_dot_tn(p, do_e))
        ds = p * (_dot_nt(do_e, ve) - delta)
        dqs.append(jnp.dot(ds.astype(MM), ke.astype(MM), preferred_element_type=F32))
        dks.append(_dot_tn(ds, qe))
    return jnp.concatenate(dqs, axis=-1), jnp.concatenate(dks, axis=-1), jnp.concatenate(dvs, axis=-1), dsg


def _sg_fn(u, v, sgc, ln_g, ln_b, ws, bs):
    mu = jnp.mean(v, axis=-1, keepdims=True)
    xc = v - mu
    vn = xc * lax.rsqrt(jnp.mean(xc * xc, axis=-1, keepdims=True) + EPS) * ln_g + ln_b
    r = lax.broadcasted_iota(jnp.int32, (SGC, SGC), 0)
    c = lax.broadcasted_iota(jnp.int32, (SGC, SGC), 1)
    wt = [jnp.where(r >= c, w, 0.0) for w in ws]
    row_blocks = []
    for ch in range(u.shape[0] // SGC):
        col_blocks = []
        for g in range(SGG):
            blk = vn[SGC * ch:SGC * (ch + 1), LANES * g:LANES * (g + 1)]
            col_blocks.append(_mm(wt[g], blk) + bs[g])
        row_blocks.append(jnp.concatenate(col_blocks, axis=-1))
    mixed = jnp.concatenate(row_blocks, axis=0)
    return (u * mixed) * jax.nn.silu(sgc)


def _memkv_fn(mem, mem_g, wm, kg):
    kv = _mm(_rms_n(mem, mem_g, D), wm)
    ks = [_rms_n(kv[:, MHD * h:MHD * (h + 1)], kg, MHD) for h in range(MH)]
    return jnp.concatenate(ks, axis=-1), kv[:, MH * MHD:]


def _mem_fn(mq, sgd, k, v, qg):
    outs = []
    for h in range(MH):
        sl = slice(MHD * h, MHD * (h + 1))
        qh = _rms_n(mq[:, sl], qg, MHD)
        p = _softmax(_mm_nt(qh, k[:, sl]) * (MHD ** -0.5))
        outs.append(_mm(p, v[:, sl]))
    return jnp.concatenate(outs, axis=-1) * jax.nn.silu(sgd)


def _merge_fn(ys, logits, bm, wb, wo):
    merged = None
    for n in range(NB):
        z = jnp.concatenate([_mm(ys[n], wb[j][n]) for j in range(N_CHIPS)], axis=-1)
        gate = jax.nn.sigmoid(logits[:, D * n:D * (n + 1)] + bm[n])
        merged = gate * z if merged is None else merged + gate * z
    return _mm(merged, wo)


def _proj_call(x, g, w):
    s_len = x.shape[0]
    tm, tn = s_len, 512

    def body(x_ref, g_ref, w_ref, p_ref, h_ref):
        @pl.when(pl.program_id(1) == 0)
        def _():
            h_ref[...] = _rms_n(x_ref[...], g_ref[...], D).astype(h_ref.dtype)
        p_ref[...] = jnp.dot(h_ref[...], w_ref[...], preferred_element_type=F32)

    return pl.pallas_call(
        body, grid=(s_len // tm, NP // tn),
        in_specs=[_bs((tm, D), lambda i, j: (i, 0)), _bs((1, D), lambda i, j: (0, 0)), _bs((D, tn), lambda i, j: (0, j))],
        out_specs=[_bs((tm, tn), lambda i, j: (i, j)), _bs((tm, D), lambda i, j: (i, 0))],
        out_shape=[jax.ShapeDtypeStruct((s_len, NP), F32), jax.ShapeDtypeStruct((s_len, D), MM)],
        name="proj", compiler_params=_cparams(2))(x, g, w)


def _rope_tables(pos):
    half = ROPE // 2
    inv_freq = ROPE_THETA ** (-jnp.arange(half, dtype=F32) / half)
    ang = pos.astype(F32)[:, None] * inv_freq
    cos, sin = jnp.cos(ang), jnp.sin(ang)
    s_len = pos.shape[0]
    z = lambda n: jnp.zeros((s_len, n), F32)
    cos_t = jnp.concatenate([jnp.ones((s_len, NOPE), F32), cos, cos, z(LANES - QKH)], axis=1)
    sin_a = jnp.concatenate([z(NOPE), -sin, z(LANES - NOPE - half)], axis=1)
    sin_b = jnp.concatenate([z(NOPE + half), sin, z(LANES - QKH)], axis=1)
    return cos_t, sin_a, sin_b


def _mla_prep_specs(tm):
    row = lambda w, off: _bs((tm, w), lambda i: (i, off // w))
    full2 = lambda a, b: _bs((a, b), lambda i: (0, 0))
    full3 = lambda a, b, c: _bs((a, b, c), lambda i: (0, 0, 0))
    tab = _bs((tm, LANES), lambda i: (i, 0))
    return [row(QL, OFF_CQ), row(KVL, OFF_CKV), row(LANES, OFF_KR), tab, tab, tab,
            full2(1, QL), full2(1, KVL), full2(1, LANES), full2(1, LANES),
            full3(H, QL, LANES), full3(H, KVL, LANES), full2(KVL, H * LANES)]


def _mla_prep_args(body_refs, wrap=lambda w: w):
    (cq, ckv, kr, ct, sa, sb, cqg, ckvg, qg, kg, wuq, wkn, wv) = body_refs
    return (cq[...], ckv[...], kr[...], ct[...], sa[...], sb[...], cqg[...], ckvg[...], qg[...], kg[...],
            [wrap(wuq[h]) for h in range(H)], [wrap(wkn[h]) for h in range(H)], wrap(wv[...]))


def _mla_prep_call(proj, tabs, cq_g, ckv_g, qg, kg, wuq, wkn, wv):
    s_len = proj.shape[0]
    tm = min(s_len, 256)

    def body(*refs):
        q_ref, k_ref, v_ref = refs[13:]
        q, k, v = _mla_prep_fn(*_mla_prep_args(refs[:13]))
        q_ref[...] = q.astype(q_ref.dtype)
        k_ref[...] = k.astype(k_ref.dtype)
        v_ref[...] = v.astype(v_ref.dtype)

    out = _bs((tm, H * LANES), lambda i: (i, 0))
    return pl.pallas_call(
        body, grid=(s_len // tm,), in_specs=_mla_prep_specs(tm), out_specs=[out, out, out],
        out_shape=[jax.ShapeDtypeStruct((s_len, H * LANES), MM)] * 3,
        name="mla_prep", compiler_params=_cparams(1))(proj, proj, proj, *tabs, cq_g, ckv_g, qg, kg, wuq, wkn, wv)


def _mla_prep_bwd_call(proj, tabs, cq_g, ckv_g, qg, kg, wuq, wkn, wv, dq, dk, dv):
    s_len = proj.shape[0]
    tm = min(s_len, 256)

    def body(*refs):
        dq_ref, dk_ref, dv_ref = refs[13:16]
        dcq_ref, dckv_ref, dkr_ref, dcqg_ref, dckvg_ref, dqg_ref, dkg_ref, dwuq_ref, dwkn_ref, dwv_ref = refs[16:]
        _, vjp = jax.vjp(_mla_prep_fn, *_mla_prep_args(refs[:13], _with_slot))
        (dcq, dckv, dkr, _, _, _, dcqg, dckvg, dqg, dkg, dwuq, dwkn, dwv) = vjp((dq_ref[...], dk_ref[...], dv_ref[...]))
        dwuq, dwkn, dwv = [d[1] for d in dwuq], [d[1] for d in dwkn], dwv[1]
        dcq_ref[...] = dcq.astype(dcq_ref.dtype)
        dckv_ref[...] = dckv.astype(dckv_ref.dtype)
        dkr_ref[...] = dkr.astype(dkr_ref.dtype)

        @pl.when(pl.program_id(0) == 0)
        def _():
            for r in (dcqg_ref, dckvg_ref, dqg_ref, dkg_ref, dwuq_ref, dwkn_ref, dwv_ref):
                r[...] = jnp.zeros_like(r)
        dcqg_ref[...] += dcqg
        dckvg_ref[...] += dckvg
        dqg_ref[...] += dqg
        dkg_ref[...] += dkg
        for h in range(H):
            dwuq_ref[h] += dwuq[h]
            dwkn_ref[h] += dwkn[h]
        dwv_ref[...] += dwv

    big = _bs((tm, H * LANES), lambda i: (i, 0))
    row = lambda w: _bs((tm, w), lambda i: (i, 0))
    full2 = lambda a, b: _bs((a, b), lambda i: (0, 0))
    full3 = lambda a, b, c: _bs((a, b, c), lambda i: (0, 0, 0))
    sd = jax.ShapeDtypeStruct
    return pl.pallas_call(
        body, grid=(s_len // tm,), in_specs=_mla_prep_specs(tm) + [big, big, big],
        out_specs=[row(QL), row(KVL), row(LANES), full2(1, QL), full2(1, KVL), full2(1, LANES), full2(1, LANES),
                   full3(H, QL, LANES), full3(H, KVL, LANES), full2(KVL, H * LANES)],
        out_shape=[sd((s_len, QL), MM), sd((s_len, KVL), MM), sd((s_len, LANES), MM), sd((1, QL), F32), sd((1, KVL), F32),
                   sd((1, LANES), F32), sd((1, LANES), F32), sd((H, QL, LANES), F32), sd((H, KVL, LANES), F32),
                   sd((KVL, H * LANES), F32)],
        name="mla_prep_bwd", compiler_params=_cparams(1))(proj, proj, proj, *tabs, cq_g, ckv_g, qg, kg, wuq, wkn, wv, dq, dk, dv)


def _attn_specs(s_len, tq):
    pair = 2 * LANES
    return [_bs((tq, pair), lambda p, i: (i, p)), _bs((s_len, pair), lambda p, i: (0, p)), _bs((s_len, pair), lambda p, i: (0, p)),
            _bs((tq, LANES), lambda p, i: (i, OFF_SG // LANES + p))]


def _attn_call(q, k, v, proj):
    s_len = q.shape[0]
    tq = min(s_len, 256)

    def body(q_ref, k_ref, v_ref, sg_ref, y_ref, o_ref, lse_ref):
        for n in range(s_len // tq):
            @pl.when(pl.program_id(1) == n)
            def _():
                kl = (n + 1) * tq
                o, lse = _attn_pair_fwd(q_ref[...], k_ref[:kl, :], v_ref[:kl, :])
                y_ref[...] = (o * jax.nn.silu(sg_ref[...])).astype(y_ref.dtype)
                o_ref[...] = o
                lse_ref[...] = lse

    tile = _bs((tq, LANES), lambda p, i: (i, p))
    sd = jax.ShapeDtypeStruct
    return pl.pallas_call(
        body, grid=(H // 2, s_len // tq), in_specs=_attn_specs(s_len, tq), out_specs=[tile, tile, tile],
        out_shape=[sd((s_len, BW), MM), sd((s_len, BW), F32), sd((s_len, BW), F32)],
        name="attn", compiler_params=_cparams(2))(q, k, v, proj)


def _attn_bwd_call(q, k, v, proj, dys, o, lse):
    s_len = q.shape[0]
    tq = min(s_len, 256)
    pair = 2 * LANES

    def body(q_ref, k_ref, v_ref, sg_ref, dy_ref, o_ref, lse_ref, dq_ref, dk_ref, dv_ref, dsg_ref):
        i = pl.program_id(1)

        @pl.when(i == 0)
        def _():
            dk_ref[...] = jnp.zeros_like(dk_ref)
            dv_ref[...] = jnp.zeros_like(dv_ref)

        for n in range(s_len // tq):
            @pl.when(i == n)
            def _():
                kl = (n + 1) * tq
                dq, dk, dv, dsg = _attn_pair_bwd(q_ref[...], k_ref[:kl, :], v_ref[:kl, :], sg_ref[...], dy_ref[...],
                                                 o_ref[...], lse_ref[...])
                dq_ref[...] = dq
                dsg_ref[...] = dsg.astype(dsg_ref.dtype)
                dk_ref[:kl, :] += dk
                dv_ref[:kl, :] += dv

    sd = jax.ShapeDtypeStruct
    tile = _bs((tq, LANES), lambda p, i: (i, p))
    return pl.pallas_call(
        body, grid=(H // 2, s_len // tq),
        in_specs=_attn_specs(s_len, tq) + [tile, tile, tile],
        out_specs=[_bs((tq, pair), lambda p, i: (i, p)), _bs((s_len, pair), lambda p, i: (0, p)),
                   _bs((s_len, pair), lambda p, i: (0, p)), tile],
        out_shape=[sd((s_len, H * LANES), F32), sd((s_len, H * LANES), F32), sd((s_len, H * LANES), F32), sd((s_len, BW), MM)],
        name="attn_bwd", compiler_params=_cparams(2))(q, k, v, proj, dys, o, lse)


def _shift_down(a, n):
    r = lax.broadcasted_iota(jnp.int32, a.shape, 0)
    return jnp.where(r >= n, pltpu.roll(a, n, 0), 0.0)


def _shift_up(a, n):
    s_len = a.shape[0]
    r = lax.broadcasted_iota(jnp.int32, a.shape, 0)
    return jnp.where(r < s_len - n, pltpu.roll(a, s_len - n, 0), 0.0)


def _conv_specs(s_len):
    col = lambda off: _bs((s_len, LANES), lambda j: (0, off // LANES + j))
    return [col(OFF_CV), col(OFF_CV + CW), col(OFF_CV + 2 * CW), col(OFF_SG + BW),
            _bs((3, LANES), lambda j: (0, j)), _bs((1, LANES), lambda j: (0, j))]


def _conv_call(proj, cw, cb):
    s_len = proj.shape[0]

    def body(bg_ref, cg_ref, xi_ref, sg_ref, w_ref, b_ref, y_ref):
        z = cg_ref[...] * xi_ref[...]
        y = b_ref[...] + w_ref[0:1, :] * _shift_down(z, 2)
        y = y + w_ref[1:2, :] * _shift_down(z, 1)
        y = y + w_ref[2:3, :] * z
        y_ref[...] = ((bg_ref[...] * y) * jax.nn.silu(sg_ref[...])).astype(y_ref.dtype)

    return pl.pallas_call(
        body, grid=(CW // LANES,), in_specs=_conv_specs(s_len), out_specs=_bs((s_len, LANES), lambda j: (0, j)),
        out_shape=jax.ShapeDtypeStruct((s_len, CW), MM), name="conv", compiler_params=_cparams(1))(proj, proj, proj, proj, cw, cb)


def _conv_bwd_call(proj, cw, cb, dys):
    s_len = proj.shape[0]

    def body(bg_ref, cg_ref, xi_ref, sg_ref, w_ref, b_ref, dys_ref, dbg_ref, dcg_ref, dxi_ref, dsg_ref, dw_ref, db_ref):
        bg, cg, xi, sg = bg_ref[...], cg_ref[...], xi_ref[...], sg_ref[...]
        w0, w1, w2 = w_ref[0:1, :], w_ref[1:2, :], w_ref[2:3, :]
        z = cg * xi
        z1, z2 = _shift_down(z, 1), _shift_down(z, 2)
        y = b_ref[...] + w0 * z2
        y = y + w1 * z1
        y = y + w2 * z
        yb = bg * y
        sig = jax.nn.sigmoid(sg)
        silu = sg * sig
        dys_v = dys_ref[...]
        dsg_ref[...] = (dys_v * yb * (sig * (1.0 + sg * (1.0 - sig)))).astype(dsg_ref.dtype)
        dyb = dys_v * silu
        dbg_ref[...] = (dyb * y).astype(dbg_ref.dtype)
        dy = dyb * bg
        db_ref[...] = jnp.sum(dy, axis=0, keepdims=True)
        dw_ref[0:1, :] = jnp.sum(dy * z2, axis=0, keepdims=True)
        dw_ref[1:2, :] = jnp.sum(dy * z1, axis=0, keepdims=True)
        dw_ref[2:3, :] = jnp.sum(dy * z, axis=0, keepdims=True)
        dz = w2 * dy + w1 * _shift_up(dy, 1) + w0 * _shift_up(dy, 2)
        dcg_ref[...] = (dz * xi).astype(dcg_ref.dtype)
        dxi_ref[...] = (dz * cg).astype(dxi_ref.dtype)

    col = _bs((s_len, LANES), lambda j: (0, j))
    sd = jax.ShapeDtypeStruct
    return pl.pallas_call(
        body, grid=(CW // LANES,), in_specs=_conv_specs(s_len) + [col],
        out_specs=[col, col, col, col, _bs((3, LANES), lambda j: (0, j)), _bs((1, LANES), lambda j: (0, j))],
        out_shape=[sd((s_len, CW), MM)] * 4 + [sd((3, CW), F32), sd((1, CW), F32)],
        name="conv_bwd", compiler_params=_cparams(1))(proj, proj, proj, proj, cw, cb, dys)


def _sg_specs(tm):
    row = lambda off: _bs((tm, SGW), lambda i: (i, off // SGW))
    return [row(OFF_SGI), row(OFF_SGI + SGW), row(OFF_SG + 2 * BW), _bs((1, SGW), lambda i: (0, 0)), _bs((1, SGW), lambda i: (0, 0)),
            _bs((SGG, SGC, SGC), lambda i: (0, 0, 0)), _bs((SGG, SGC, 1), lambda i: (0, 0, 0))]


def _sg_args(refs):
    u, v, sg, lg, lb, ws, bs = refs
    return (u[...], v[...], sg[...], lg[...], lb[...], [ws[g] for g in range(SGG)], [bs[g] for g in range(SGG)])


def _sg_call(proj, ln_g, ln_b, ws, bs):
    s_len = proj.shape[0]
    tm = min(s_len, 256)

    def body(*refs):
        refs[7][...] = _sg_fn(*_sg_args(refs[:7])).astype(refs[7].dtype)

    return pl.pallas_call(
        body, grid=(s_len // tm,), in_specs=_sg_specs(tm), out_specs=_bs((tm, SGW), lambda i: (i, 0)),
        out_shape=jax.ShapeDtypeStruct((s_len, SGW), MM), name="sgmlp", compiler_params=_cparams(1))(proj, proj, proj, ln_g, ln_b, ws, bs)


def _sg_bwd_call(proj, ln_g, ln_b, ws, bs, dys):
    s_len = proj.shape[0]
    tm = min(s_len, 256)

    def body(*refs):
        dys_ref = refs[7]
        du_ref, dv_ref, dsg_ref, dlg_ref, dlb_ref, dws_ref, dbs_ref = refs[8:]
        _, vjp = jax.vjp(_sg_fn, *_sg_args(refs[:7]))
        du, dv, dsg, dlg, dlb, dws, dbs = vjp(dys_ref[...])
        du_ref[...] = du.astype(du_ref.dtype)
        dv_ref[...] = dv.astype(dv_ref.dtype)
        dsg_ref[...] = dsg.astype(dsg_ref.dtype)

        @pl.when(pl.program_id(0) == 0)
        def _():
            for r in (dlg_ref, dlb_ref, dws_ref, dbs_ref):
                r[...] = jnp.zeros_like(r)
        dlg_ref[...] += dlg
        dlb_ref[...] += dlb
        for g in range(SGG):
            dws_ref[g] += dws[g]
            dbs_ref[g] += dbs[g]

    row = _bs((tm, SGW), lambda i: (i, 0))
    sd = jax.ShapeDtypeStruct
    return pl.pallas_call(
        body, grid=(s_len // tm,), in_specs=_sg_specs(tm) + [row],
        out_specs=[row, row, row, _bs((1, SGW), lambda i: (0, 0)), _bs((1, SGW), lambda i: (0, 0)),
                   _bs((SGG, SGC, SGC), lambda i: (0, 0, 0)), _bs((SGG, SGC, 1), lambda i: (0, 0, 0))],
        out_shape=[sd((s_len, SGW), MM)] * 3 + [sd((1, SGW), F32), sd((1, SGW), F32), sd((SGG, SGC, SGC), F32), sd((SGG, SGC, 1), F32)],
        name="sgmlp_bwd", compiler_params=_cparams(1))(proj, proj, proj, ln_g, ln_b, ws, bs, dys)


def _memkv_call(mem, mem_g, wm, kg):
    m_len = mem.shape[0]

    def body(mem_ref, g_ref, w_ref, kg_ref, k_ref, v_ref):
        k, v = _memkv_fn(mem_ref[...], g_ref[...], w_ref[...], kg_ref[...])
        k_ref[...] = k.astype(k_ref.dtype)
        v_ref[...] = v.astype(v_ref.dtype)

    return pl.pallas_call(body, out_shape=[jax.ShapeDtypeStruct((m_len, MH * MHD), MM)] * 2, name="memkv",
                          compiler_params=pltpu.CompilerParams(vmem_limit_bytes=VMEM_LIMIT))(mem, mem_g, wm, kg)


def _memkv_bwd_call(mem, mem_g, wm, kg, dk, dv):
    def body(mem_ref, g_ref, w_ref, kg_ref, dk_ref, dv_ref, dg_ref, dw_ref, dkg_ref):
        _, vjp = jax.vjp(_memkv_fn, mem_ref[...], g_ref[...], _with_slot(w_ref[...]), kg_ref[...])
        _, dg, dw, dkg = vjp((dk_ref[...], dv_ref[...]))
        dg_ref[...] = dg
        dw_ref[...] = dw[1]
        dkg_ref[...] = dkg

    sd = jax.ShapeDtypeStruct
    return pl.pallas_call(body, out_shape=[sd((1, D), F32), sd((D, 2 * MH * MHD), F32), sd((1, MHD), F32)], name="memkv_bwd",
                          compiler_params=pltpu.CompilerParams(vmem_limit_bytes=VMEM_LIMIT))(mem, mem_g, wm, kg, dk, dv)


def _mem_specs(tm, m_len):
    w = MH * MHD
    return [_bs((tm, w), lambda i: (i, OFF_MQ // w)), _bs((tm, BW), lambda i: (i, (OFF_SG + 3 * BW) // BW)),
            _bs((m_len, w), lambda i: (0, 0)), _bs((m_len, w), lambda i: (0, 0)), _bs((1, MHD), lambda i: (0, 0))]


def _mem_call(proj, k, v, qg):
    s_len, m_len = proj.shape[0], k.shape[0]
    tm = min(s_len, 256)

    def body(mq_ref, sg_ref, k_ref, v_ref, qg_ref, y_ref):
        y_ref[...] = _mem_fn(mq_ref[...], sg_ref[...], k_ref[...], v_ref[...], qg_ref[...]).astype(y_ref.dtype)

    return pl.pallas_call(
        body, grid=(s_len // tm,), in_specs=_mem_specs(tm, m_len), out_specs=_bs((tm, BW), lambda i: (i, 0)),
        out_shape=jax.ShapeDtypeStruct((s_len, BW), MM), name="memattn", compiler_params=_cparams(1))(proj, proj, k, v, qg)


def _mem_bwd_call(proj, k, v, qg, dys):
    s_len, m_len = proj.shape[0], k.shape[0]
    tm = min(s_len, 256)
    w = MH * MHD

    def body(mq_ref, sg_ref, k_ref, v_ref, qg_ref, dys_ref, dmq_ref, dsg_ref, dk_ref, dv_ref, dqg_ref):
        _, vjp = jax.vjp(_mem_fn, mq_ref[...], sg_ref[...], k_ref[...].astype(F32), v_ref[...].astype(F32), qg_ref[...])
        dmq, dsg, dk, dv, dqg = vjp(dys_ref[...])
        dmq_ref[...] = dmq.astype(dmq_ref.dtype)
        dsg_ref[...] = dsg.astype(dsg_ref.dtype)

        @pl.when(pl.program_id(0) == 0)
        def _():
            for r in (dk_ref, dv_ref, dqg_ref):
                r[...] = jnp.zeros_like(r)
        dk_ref[...] += dk
        dv_ref[...] += dv
        dqg_ref[...] += dqg

    row = _bs((tm, BW), lambda i: (i, 0))
    kv = _bs((m_len, w), lambda i: (0, 0))
    sd = jax.ShapeDtypeStruct
    return pl.pallas_call(
        body, grid=(s_len // tm,), in_specs=_mem_specs(tm, m_len) + [row],
        out_specs=[row, row, kv, kv, _bs((1, MHD), lambda i: (0, 0))],
        out_shape=[sd((s_len, w), MM), sd((s_len, BW), MM), sd((m_len, w), F32), sd((m_len, w), F32), sd((1, MHD), F32)],
        name="memattn_bwd", compiler_params=_cparams(1))(proj, proj, k, v, qg, dys)


def _merge_specs(tm):
    row = _bs((tm, BW), lambda i: (i, 0))
    return [row, row, row, row, _bs((tm, NB * D), lambda i: (i, OFF_ML // (NB * D))), _bs((NB, D), lambda i: (0, 0)),
            _bs((N_CHIPS, NB, BW, D // N_CHIPS), lambda i: (0, 0, 0, 0)), _bs((D, D), lambda i: (0, 0))]


def _merge_call(ys, proj, bm, wb, wo, x):
    s_len = proj.shape[0]
    tm = min(s_len, 256)

    def body(ya, yb, yc, yd, lg_ref, bm_ref, wb_ref, wo_ref, x_ref, o_ref):
        out = _merge_fn([r[...] for r in (ya, yb, yc, yd)], lg_ref[...], [bm_ref[n:n + 1, :] for n in range(NB)],
                        [[wb_ref[j, n] for n in range(NB)] for j in range(N_CHIPS)], wo_ref[...])
        o_ref[...] = x_ref[...] + out

    xrow = _bs((tm, D), lambda i: (i, 0))
    return pl.pallas_call(
        body, grid=(s_len // tm,), in_specs=_merge_specs(tm) + [xrow], out_specs=xrow,
        out_shape=jax.ShapeDtypeStruct((s_len, D), F32), name="merge", compiler_params=_cparams(1))(*ys, proj, bm, wb, wo, x)


def _merge_bwd_call(ys, proj, bm, wb, wo, dout):
    s_len = proj.shape[0]
    tm = min(s_len, 256)

    def body(ya, yb, yc, yd, lg_ref, bm_ref, wb_ref, wo_ref, do_ref, dya, dyb, dyc, dyd, dlg_ref, dbm_ref, dwb_ref, dwo_ref):
        fn = lambda ys_, lg_, bm_, wb_, wo_: _merge_fn(ys_, lg_, bm_, wb_, wo_)
        _, vjp = jax.vjp(fn, [r[...].astype(F32) for r in (ya, yb, yc, yd)], lg_ref[...], [bm_ref[n:n + 1, :] for n in range(NB)],
                         [[_with_slot(wb_ref[j, n]) for n in range(NB)] for j in range(N_CHIPS)], _with_slot(wo_ref[...]))
        dys, dlg, dbm, dwb, dwo = vjp(do_ref[...])
        dwb, dwo = [[d[1] for d in row] for row in dwb], dwo[1]
        for r, d in zip((dya, dyb, dyc, dyd), dys):
            r[...] = d
        dlg_ref[...] = dlg.astype(dlg_ref.dtype)

        @pl.when(pl.program_id(0) == 0)
        def _():
            for r in (dbm_ref, dwb_ref, dwo_ref):
                r[...] = jnp.zeros_like(r)
        for n in range(NB):
            dbm_ref[n:n + 1, :] += dbm[n]
            for j in range(N_CHIPS):
                dwb_ref[j, n] += dwb[j][n]
        dwo_ref[...] += dwo

    row = _bs((tm, BW), lambda i: (i, 0))
    sd = jax.ShapeDtypeStruct
    wb_shape = (N_CHIPS, NB, BW, D // N_CHIPS)
    return pl.pallas_call(
        body, grid=(s_len // tm,), in_specs=_merge_specs(tm) + [_bs((tm, D), lambda i: (i, 0))],
        out_specs=[row, row, row, row, _bs((tm, NB * D), lambda i: (i, 0)), _bs((NB, D), lambda i: (0, 0)),
                   _bs(wb_shape, lambda i: (0, 0, 0, 0)), _bs((D, D), lambda i: (0, 0))],
        out_shape=[sd((s_len, BW), F32)] * 4 + [sd((s_len, NB * D), MM), sd((NB, D), F32), sd(wb_shape, F32), sd((D, D), F32)],
        name="merge_bwd", compiler_params=_cparams(1))(*ys, proj, bm, wb, wo, dout)


def _dh_call(dproj, w, x, g, dout, after=()):
    s_len = x.shape[0]
    tk = NP // 4
    after = list(after)

    def matmul_body(dp_ref, w_ref, *rest):
        o_ref = rest[-1]

        @pl.when(pl.program_id(0) == 0)
        def _():
            o_ref[...] = jnp.zeros_like(o_ref)
        o_ref[...] += lax.dot_general(dp_ref[...], w_ref[...], (((1,), (1,)), ((), ())), preferred_element_type=F32)

    dh = pl.pallas_call(
        matmul_body, grid=(NP // tk,),
        in_specs=[_bs((s_len, tk), lambda k: (0, k)), _bs((D, tk), lambda k: (0, k))] + [_ANY] * len(after),
        out_specs=_bs((s_len, D), lambda k: (0, 0)), out_shape=jax.ShapeDtypeStruct((s_len, D), F32),
        name="dh", compiler_params=_cparams(1))(dproj, w, *after)

    tm = min(s_len, 512)

    def norm_body(dh_ref, x_ref, g_ref, do_ref, dx_ref, dg_ref):
        _, vjp = jax.vjp(lambda x_, g_: _rms_n(x_, g_, D), x_ref[...], g_ref[...])
        dxr, dgr = vjp(dh_ref[...])
        dx_ref[...] = do_ref[...] + dxr

        @pl.when(pl.program_id(0) == 0)
        def _():
            dg_ref[...] = jnp.zeros_like(dg_ref)
        dg_ref[...] += dgr

    row = _bs((tm, D), lambda i: (i, 0))
    return pl.pallas_call(
        norm_body, grid=(s_len // tm,), in_specs=[row, row, _bs((1, D), lambda i: (0, 0)), row],
        out_specs=[row, _bs((1, D), lambda i: (0, 0))],
        out_shape=[jax.ShapeDtypeStruct((s_len, D), F32), jax.ShapeDtypeStruct((1, D), F32)],
        name="norm_bwd", compiler_params=_cparams(1))(dh, x, g, dout)


def _dw_call(h, dproj, after=()):
    s_len = h.shape[0]
    tn = 512
    after = list(after)

    def body(h_ref, dp_ref, *rest):
        o_ref, ht_ref = rest[-2], rest[-1]

        @pl.when(pl.program_id(0) == 0)
        def _():
            ht_ref[...] = h_ref[...].T
        o_ref[...] = jnp.dot(ht_ref[...], dp_ref[...], preferred_element_type=F32)

    return pl.pallas_call(
        body, grid=(NP // tn,),
        in_specs=[_bs((s_len, D), lambda j: (0, 0)), _bs((s_len, tn), lambda j: (0, j))] + [_ANY] * len(after),
        out_specs=_bs((D, tn), lambda j: (0, j)), out_shape=jax.ShapeDtypeStruct((D, NP), F32),
        scratch_shapes=[pltpu.VMEM((D, s_len), h.dtype)], name="dw_in", compiler_params=_cparams(1))(h, dproj, *after)


def _loss_call(y, target):
    s_len = y.shape[0]
    tm = min(s_len, 512)

    def body(y_ref, t_ref, dy_ref, l_ref):
        e = y_ref[...] - t_ref[...]
        dy_ref[...] = e * (1.0 / D)

        @pl.when(pl.program_id(0) == 0)
        def _():
            l_ref[...] = jnp.zeros_like(l_ref)
        l_ref[...] += jnp.sum(e * e, axis=0, keepdims=True)

    row = _bs((tm, D), lambda i: (i, 0))
    return pl.pallas_call(
        body, grid=(s_len // tm,), in_specs=[row, row], out_specs=[row, _bs((1, D), lambda i: (0, 0))],
        out_shape=[jax.ShapeDtypeStruct((s_len, D), F32), jax.ShapeDtypeStruct((1, D), F32)],
        name="loss", compiler_params=_cparams(1))(y, target)


def _adamw_call(w, g, m, v, name):
    rows, cols = w.shape
    tr = min(_row_tile(rows), 128)

    def body(w_ref, g_ref, m_ref, v_ref, d_ref, nm_ref, nv_ref):
        gv = g_ref[...]
        m2 = ADAM_B1 * m_ref[...] + (1.0 - ADAM_B1) * gv
        v2 = ADAM_B2 * v_ref[...] + (1.0 - ADAM_B2) * (gv * gv)
        m_hat = m2 / (1.0 - ADAM_B1 ** ADAM_STEP)
        v_hat = v2 / (1.0 - ADAM_B2 ** ADAM_STEP)
        d_ref[...] = -ADAM_LR * (m_hat / (jnp.sqrt(v_hat) + ADAM_EPS) + ADAM_WD * w_ref[...])
        nm_ref[...] = m2
        nv_ref[...] = v2

    blk = _bs((tr, cols), lambda i: (i, 0))
    return pl.pallas_call(
        body, grid=(rows // tr,), in_specs=[blk] * 4, out_specs=[blk] * 3,
        out_shape=[jax.ShapeDtypeStruct((rows, cols), F32)] * 3, name=name, compiler_params=_cparams(1))(w, g, m, v)


def _adamw_layer_call(layer, ws, gs, ms, vs, prev, after, name, steps=8):
    n = len(ws)
    after = list(after)
    n_prev = 4 * n if prev is not None else 0

    def body(*refs):
        outs = refs[len(refs) - 4 * n:]
        for t in range(n):
            w_ref, g_ref, m_ref, v_ref = refs[t], refs[n + t], refs[2 * n + t], refs[3 * n + t]
            g_out, d_out, m_out, v_out = outs[4 * t:4 * t + 4]
            gv = g_ref[...]
            m2 = ADAM_B1 * m_ref[0] + (1.0 - ADAM_B1) * gv
            v2 = ADAM_B2 * v_ref[0] + (1.0 - ADAM_B2) * (gv * gv)
            m_hat = m2 / (1.0 - ADAM_B1 ** ADAM_STEP)
            v_hat = v2 / (1.0 - ADAM_B2 ** ADAM_STEP)
            g_out[0] = gv
            d_out[0] = -ADAM_LR * (m_hat / (jnp.sqrt(v_hat) + ADAM_EPS) + ADAM_WD * w_ref[0])
            m_out[0] = m2
            v_out[0] = v2

    def lay(a):
        return _bs((1, a.shape[1] // steps, a.shape[2]), lambda i: (layer, i, 0))

    in_specs = ([lay(a) for a in ws] + [_bs((g.shape[0] // steps, g.shape[1]), lambda i: (i, 0)) for g in gs]
                + [lay(a) for a in ms] + [lay(a) for a in vs] + [_ANY] * (n_prev + len(after)))
    return pl.pallas_call(
        body, grid=(steps,), in_specs=in_specs, out_specs=[lay(ws[t]) for t in range(n) for _ in range(4)],
        out_shape=[jax.ShapeDtypeStruct(ws[t].shape, F32) for t in range(n) for _ in range(4)],
        input_output_aliases={4 * n + q: q for q in range(n_prev)}, name=name, compiler_params=_cparams(1),
    )(*ws, *gs, *ms, *vs, *(prev if prev is not None else []), *after)


def _row_tile(rows):
    for cand in (512, 256, 128, 64, 32, 16, 8):
        if rows % cand == 0 and rows > cand:
            return cand
    return rows


def _pair_sum_call(grads, from_sibling, core, name):
    n = len(grads)

    def body(core_ref, *refs):
        for t in range(n):
            refs[2 * n + t][...] = (refs[t][...].astype(F32) + refs[n + t][...].astype(F32)).astype(MM)

    half = lambda g: (1, g.shape[1] // 2, g.shape[2])
    grid_spec = pltpu.PrefetchScalarGridSpec(
        num_scalar_prefetch=1, grid=(N_CHIPS,),
        in_specs=[pl.BlockSpec(half(g), lambda j, core_ref: (j, core_ref[0], 0)) for g in grads]
        + [pl.BlockSpec(half(g), lambda j, core_ref: (j, 0, 0)) for g in grads],
        out_specs=[pl.BlockSpec(half(g), lambda j, core_ref: (j, 0, 0)) for g in grads])
    return pl.pallas_call(
        body, grid_spec=grid_spec, out_shape=[jax.ShapeDtypeStruct((N_CHIPS,) + half(g)[1:], MM) for g in grads], name=name,
        compiler_params=_cparams(1))(core, *grads, *from_sibling)


def _owner_sum_call(chip_sums, from_chips, chip_core, name):
    n = len(chip_sums)
    steps = 4

    def body(ids_ref, *refs):
        for t in range(n):
            a, b = refs[t], refs[n + t]
            refs[2 * n + t][...] = ((a[0].astype(F32) + b[0].astype(F32)) + b[1].astype(F32)) + b[2].astype(F32)

    tile = lambda p: (p.shape[1] // steps, p.shape[2])
    grid_spec = pltpu.PrefetchScalarGridSpec(
        num_scalar_prefetch=1, grid=(steps,),
        in_specs=[pl.BlockSpec((1,) + tile(p), lambda i, ids_ref: (ids_ref[0], i, 0)) for p in chip_sums]
        + [pl.BlockSpec((3,) + tile(p), lambda i, ids_ref: (0, i, 0)) for p in chip_sums],
        out_specs=[pl.BlockSpec(tile(p), lambda i, ids_ref: (ids_ref[1] * steps + i, 0)) for p in chip_sums])
    return pl.pallas_call(
        body, grid_spec=grid_spec, out_shape=[jax.ShapeDtypeStruct((2 * p.shape[1], p.shape[2]), F32) for p in chip_sums],
        name=name, compiler_params=_cparams(1))(chip_core, *chip_sums, *from_chips)


def _sum8_call(parts):
    n, rows, cols = parts.shape
    tr = _row_tile(rows)

    def body(p_ref, o_ref):
        acc = p_ref[0]
        for k in range(1, n):
            acc = acc + p_ref[k]
        o_ref[...] = acc

    return pl.pallas_call(
        body, grid=(rows // tr,), in_specs=[_bs((n, tr, cols), lambda i: (0, i, 0))], out_specs=_bs((tr, cols), lambda i: (i, 0)),
        out_shape=jax.ShapeDtypeStruct((rows, cols), F32), name="sum_small_grads", compiler_params=_cparams(1))(parts)


_ANY = pl.BlockSpec(memory_space=pl.ANY)


def _half_rows(ref, lead, half, which):
    rows = pl.ds(pl.multiple_of(half * which, half), half)
    return ref.at[rows] if lead is None else ref.at[lead, rows]


_HBM = pl.BlockSpec(memory_space=pltpu.HBM)
_SEM = pl.BlockSpec(memory_space=pltpu.SEMAPHORE)
_ORDERED_EFFECT = pltpu.CompilerParams(has_side_effects=pltpu.SideEffectType.DATAFLOW_SIDE_EFFECTING)


_VMEM = pl.BlockSpec(memory_space=pltpu.VMEM)
_TOKEN = jax.ShapeDtypeStruct((8, LANES), F32)


def _in_hbm(a):
    return pltpu.with_memory_space_constraint(a, pltpu.HBM)


def _tie(small, token):
    return small + token[0:1, 0:1].reshape((1,) * small.ndim)


def _peer(k):
    x, y, c = lax.axis_index("x"), lax.axis_index("y"), lax.axis_index("c")
    bx, by, bc = (k >> 2) & 1, (k >> 1) & 1, k & 1
    return (x ^ bx if bx else x, y ^ by if by else y, c ^ bc if bc else c)


def _place_block_call(blk, index, name):
    rows, cols = blk.shape

    def body(idx_ref, b_ref, o_ref):
        o_ref[0] = b_ref[...]

    grid_spec = pltpu.PrefetchScalarGridSpec(
        num_scalar_prefetch=1, grid=(1,), in_specs=[pl.BlockSpec((rows, cols), lambda i, idx_ref: (0, 0))],
        out_specs=pl.BlockSpec((1, rows, cols), lambda i, idx_ref: (idx_ref[0], 0, 0)))
    return pl.pallas_call(body, grid_spec=grid_spec, out_shape=jax.ShapeDtypeStruct((8, rows, cols), blk.dtype), name=name,
                          compiler_params=_cparams(1))(index, blk)


def _small_gather_start_call(blk, buf, after, name):
    after = list(after)

    def body(*refs):
        b_ref, out_ref = refs[0], refs[2 + len(after)]
        send_sems, recv_sems, token = refs[3 + len(after):]
        x, y, c = lax.axis_index("x"), lax.axis_index("y"), lax.axis_index("c")
        for k in range(1, 8):
            pltpu.make_async_remote_copy(src_ref=b_ref, dst_ref=out_ref.at[4 * x + 2 * y + c], send_sem=send_sems.at[k - 1],
                                         recv_sem=recv_sems.at[k - 1], device_id=_peer(k), device_id_type=MESH_ID).start()
        token[...] = jnp.zeros_like(token)

    dma = pltpu.SemaphoreType.DMA
    return pl.pallas_call(
        body, out_shape=[pltpu.HBM(buf.shape, buf.dtype), dma((7,)), dma((7,)), _TOKEN],
        in_specs=[_HBM, _HBM] + [_ANY] * len(after), out_specs=[_HBM, _SEM, _SEM, _VMEM],
        input_output_aliases={1: 0}, name=name, compiler_params=_ORDERED_EFFECT)(_in_hbm(blk), _in_hbm(buf), *after)


def _small_gather_finish_call(blk, buf, send_sems, recv_sems, after, name):
    after = list(after)

    def body(*refs):
        b_ref, in_ref, send_ref, recv_ref = refs[:4]
        x, y, c = lax.axis_index("x"), lax.axis_index("y"), lax.axis_index("c")
        for k in range(1, 8):
            px, py, pc = _peer(k)
            pltpu.make_async_remote_copy(src_ref=b_ref, dst_ref=in_ref.at[4 * px + 2 * py + pc], send_sem=send_ref.at[k - 1],
                                         recv_sem=recv_ref.at[k - 1], device_id=(px, py, pc), device_id_type=MESH_ID).wait()

    return pl.pallas_call(
        body, out_shape=pltpu.HBM(buf.shape, buf.dtype), in_specs=[_HBM, _HBM, _SEM, _SEM] + [_ANY] * len(after),
        out_specs=_HBM, input_output_aliases={1: 0}, name=name, compiler_params=_ORDERED_EFFECT,
    )(_in_hbm(blk), buf, send_sems, recv_sems, *after)


def _pair_exchange_start_call(grads, name):
    n = len(grads)
    half = [g.shape[1] // 2 for g in grads]

    def body(*refs):
        srcs, outs = refs[:n], refs[n:2 * n]
        send_sems, recv_sems, token = refs[2 * n:]
        x, y, c = lax.axis_index("x"), lax.axis_index("y"), lax.axis_index("c")
        for t in range(n):
            pltpu.make_async_remote_copy(
                src_ref=srcs[t].at[:, pl.ds(pl.multiple_of(half[t] * (1 - c), half[t]), half[t])], dst_ref=outs[t],
                send_sem=send_sems.at[t], recv_sem=recv_sems.at[t], device_id=(x, y, 1 - c), device_id_type=MESH_ID).start()
        token[...] = jnp.zeros_like(token)

    dma = pltpu.SemaphoreType.DMA
    return pl.pallas_call(
        body, out_shape=[pltpu.HBM((g.shape[0], g.shape[1] // 2, g.shape[2]), g.dtype) for g in grads] + [dma((n,)), dma((n,)), _TOKEN],
        in_specs=[_HBM] * n, out_specs=[_HBM] * n + [_SEM, _SEM, _VMEM], name=name, compiler_params=_ORDERED_EFFECT,
    )(*[_in_hbm(g) for g in grads])


def _pair_exchange_finish_call(grads, bufs, send_sems, recv_sems, after, name):
    n = len(grads)
    after = list(after)
    half = [g.shape[1] // 2 for g in grads]

    def body(*refs):
        srcs, ins, send_ref, recv_ref = refs[:n], refs[n:2 * n], refs[2 * n], refs[2 * n + 1]
        x, y, c = lax.axis_index("x"), lax.axis_index("y"), lax.axis_index("c")
        for t in range(n):
            pltpu.make_async_remote_copy(
                src_ref=srcs[t].at[:, pl.ds(pl.multiple_of(half[t] * (1 - c), half[t]), half[t])], dst_ref=ins[t],
                send_sem=send_ref.at[t], recv_sem=recv_ref.at[t], device_id=(x, y, 1 - c), device_id_type=MESH_ID).wait()

    return pl.pallas_call(
        body, out_shape=[pltpu.HBM(b.shape, b.dtype) for b in bufs],
        in_specs=[_HBM] * (2 * n) + [_SEM, _SEM] + [_ANY] * len(after), out_specs=[_HBM] * n,
        input_output_aliases={n + t: t for t in range(n)}, name=name, compiler_params=_ORDERED_EFFECT,
    )(*[_in_hbm(g) for g in grads], *bufs, send_sems, recv_sems, *after)


def _chip_scatter_start_call(chip_sums, name):
    n = len(chip_sums)

    def body(*refs):
        srcs, outs = refs[:n], refs[n:2 * n]
        send_sems, recv_sems, token = refs[2 * n:]
        x, y, c = lax.axis_index("x"), lax.axis_index("y"), lax.axis_index("c")
        chips = [(1 - x, y), (x, 1 - y), (1 - x, 1 - y)]
        for k, (cx, cy) in enumerate(chips):
            for t in range(n):
                pltpu.make_async_remote_copy(
                    src_ref=srcs[t].at[2 * cx + cy], dst_ref=outs[t].at[k], send_sem=send_sems.at[3 * t + k],
                    recv_sem=recv_sems.at[3 * t + k], device_id=(cx, cy, c), device_id_type=MESH_ID).start()
        token[...] = jnp.zeros_like(token)

    dma = pltpu.SemaphoreType.DMA
    return pl.pallas_call(
        body, out_shape=[pltpu.HBM((3,) + p.shape[1:], p.dtype) for p in chip_sums] + [dma((3 * n,)), dma((3 * n,)), _TOKEN],
        in_specs=[_HBM] * n, out_specs=[_HBM] * n + [_SEM, _SEM, _VMEM], name=name, compiler_params=_ORDERED_EFFECT,
    )(*[_in_hbm(p) for p in chip_sums])


def _chip_scatter_finish_call(chip_sums, bufs, send_sems, recv_sems, after, name):
    n = len(chip_sums)
    after = list(after)

    def body(*refs):
        srcs, ins, send_ref, recv_ref = refs[:n], refs[n:2 * n], refs[2 * n], refs[2 * n + 1]
        x, y, c = lax.axis_index("x"), lax.axis_index("y"), lax.axis_index("c")
        chips = [(1 - x, y), (x, 1 - y), (1 - x, 1 - y)]
        for k, (cx, cy) in enumerate(chips):
            for t in range(n):
                pltpu.make_async_remote_copy(
                    src_ref=srcs[t].at[2 * cx + cy], dst_ref=ins[t].at[k], send_sem=send_ref.at[3 * t + k],
                    recv_sem=recv_ref.at[3 * t + k], device_id=(cx, cy, c), device_id_type=MESH_ID).wait()

    return pl.pallas_call(
        body, out_shape=[pltpu.HBM(b.shape, b.dtype) for b in bufs],
        in_specs=[_HBM] * (2 * n) + [_SEM, _SEM] + [_ANY] * len(after), out_specs=[_HBM] * n,
        input_output_aliases={n + t: t for t in range(n)}, name=name, compiler_params=_ORDERED_EFFECT,
    )(*[_in_hbm(p) for p in chip_sums], *bufs, send_sems, recv_sems, *after)


def _place_own_call(mine, chip_core, name):
    n = len(mine)

    def body(ids_ref, *refs):
        for t in range(n):
            refs[n + t][0] = refs[t][...]

    def imap_out(s):
        pad = (0,) * (s.ndim - 1)
        return lambda i, ids_ref: (ids_ref[0], ids_ref[1]) + pad

    grid_spec = pltpu.PrefetchScalarGridSpec(
        num_scalar_prefetch=1, grid=(1,), in_specs=[pl.BlockSpec(s.shape, lambda i, ids_ref, k=s.ndim: (0,) * k) for s in mine],
        out_specs=[pl.BlockSpec((1,) + s.shape, imap_out(s)) for s in mine])
    return pl.pallas_call(
        body, grid_spec=grid_spec,
        out_shape=[jax.ShapeDtypeStruct((N_CHIPS, 2 * s.shape[0]) + s.shape[1:], s.dtype) for s in mine],
        name=name, compiler_params=_cparams(1))(chip_core, *mine)


def _gather_start_call(mine, bufs, after, name):
    n = len(mine)
    half = [s.shape[0] for s in mine]

    def body(*refs):
        srcs, outs = refs[:n], refs[2 * n + 1:3 * n + 1]
        send_sems, recv_sib, recv_ici, token = refs[3 * n + 1:]
        x, y, c = lax.axis_index("x"), lax.axis_index("y"), lax.axis_index("c")
        chips = [(1 - x, y), (x, 1 - y), (1 - x, 1 - y)]
        for t in range(n):
            dst = _half_rows(outs[t], 2 * x + y, half[t], c)
            pltpu.make_async_remote_copy(src_ref=srcs[t], dst_ref=dst, send_sem=send_sems.at[4 * t], recv_sem=recv_sib.at[t],
                                         device_id=(x, y, 1 - c), device_id_type=MESH_ID).start()
            for j, chip in enumerate(chips):
                pltpu.make_async_remote_copy(src_ref=srcs[t], dst_ref=dst, send_sem=send_sems.at[4 * t + 1 + j],
                                             recv_sem=recv_ici.at[3 * t + j], device_id=(*chip, c), device_id_type=MESH_ID).start()
        token[...] = jnp.zeros_like(token)

    dma = pltpu.SemaphoreType.DMA
    return pl.pallas_call(
        body, out_shape=[pltpu.HBM(b.shape, b.dtype) for b in bufs] + [dma((4 * n,)), dma((n,)), dma((3 * n,)), _TOKEN],
        in_specs=[_HBM] * (2 * n) + [_ANY], out_specs=[_HBM] * n + [_SEM] * 3 + [_VMEM],
        input_output_aliases={n + t: t for t in range(n)}, name=name, compiler_params=_ORDERED_EFFECT,
    )(*[_in_hbm(s) for s in mine], *[_in_hbm(b) for b in bufs], after)


def _gather_forward_call(bufs, recv_ici, after, name):
    n = len(bufs)
    half = [b.shape[1] // 2 for b in bufs]

    def body(*refs):
        ins, recv_ici_ref = refs[:n], refs[n]
        outs = refs[n + 2:2 * n + 2]
        send_fwd, recv_fwd, token = refs[2 * n + 2:]
        x, y, c = lax.axis_index("x"), lax.axis_index("y"), lax.axis_index("c")
        chips = [(1 - x, y), (x, 1 - y), (1 - x, 1 - y)]
        for j, (cx, cy) in enumerate(chips):
            for t in range(n):
                landed = _half_rows(ins[t], 2 * cx + cy, half[t], c)
                dst = _half_rows(outs[t], 2 * cx + cy, half[t], c)
                pltpu.make_async_remote_copy(src_ref=landed, dst_ref=landed, send_sem=send_fwd.at[3 * t + j],
                                             recv_sem=recv_ici_ref.at[3 * t + j], device_id=(cx, cy, c),
                                             device_id_type=MESH_ID).wait_recv()
                pltpu.make_async_remote_copy(src_ref=landed, dst_ref=dst, send_sem=send_fwd.at[3 * t + j],
                                             recv_sem=recv_fwd.at[3 * t + j], device_id=(x, y, 1 - c),
                                             device_id_type=MESH_ID).start()
        token[...] = jnp.zeros_like(token)

    dma = pltpu.SemaphoreType.DMA
    return pl.pallas_call(
        body, out_shape=[pltpu.HBM(b.shape, b.dtype) for b in bufs] + [dma((3 * n,)), dma((3 * n,)), _TOKEN],
        in_specs=[_HBM] * n + [_SEM, _ANY], out_specs=[_HBM] * n + [_SEM] * 2 + [_VMEM],
        input_output_aliases={t: t for t in range(n)}, name=name, compiler_params=_ORDERED_EFFECT,
    )(*bufs, recv_ici, after)


def _gather_finish_call(shards, bufs, send_sems, recv_sib, send_fwd, recv_fwd, after, name):
    n = len(bufs)
    half = [b.shape[1] // 2 for b in bufs]

    def body(*refs):
        srcs, ins = refs[:n], refs[n:2 * n]
        send_ref, recv_sib_ref, send_fwd_ref, recv_fwd_ref = refs[2 * n:2 * n + 4]
        x, y, c = lax.axis_index("x"), lax.axis_index("y"), lax.axis_index("c")
        chips = [(1 - x, y), (x, 1 - y), (1 - x, 1 - y)]
        sibling = (x, y, 1 - c)
        for t in range(n):
            for k in range(4):
                pltpu.make_async_remote_copy(src_ref=srcs[t], dst_ref=srcs[t], send_sem=send_ref.at[4 * t + k],
                                             recv_sem=recv_sib_ref.at[t], device_id=sibling, device_id_type=MESH_ID).wait_send()
            from_sibling = _half_rows(ins[t], 2 * x + y, half[t], 1 - c)
            pltpu.make_async_remote_copy(src_ref=from_sibling, dst_ref=from_sibling, send_sem=send_ref.at[4 * t],
                                         recv_sem=recv_sib_ref.at[t], device_id=sibling, device_id_type=MESH_ID).wait_recv()
            for j, (cx, cy) in enumerate(chips):
                sent = _half_rows(ins[t], 2 * cx + cy, half[t], c)
                passed = _half_rows(ins[t], 2 * cx + cy, half[t], 1 - c)
                pltpu.make_async_remote_copy(src_ref=sent, dst_ref=passed, send_sem=send_fwd_ref.at[3 * t + j],
                                             recv_sem=recv_fwd_ref.at[3 * t + j], device_id=sibling, device_id_type=MESH_ID).wait()

    return pl.pallas_call(
        body, out_shape=[pltpu.HBM(b.shape, b.dtype) for b in bufs],
        in_specs=[_HBM] * (2 * n) + [_SEM] * 4 + [_ANY], out_specs=[_HBM] * n,
        input_output_aliases={n + t: t for t in range(n)}, name=name, compiler_params=_ORDERED_EFFECT,
    )(*[_in_hbm(s) for s in shards], *bufs, send_sems, recv_sib, send_fwd, recv_fwd, after)


def _pair_gather_call(bufs, name):
    n = len(bufs)
    half = [b.shape[0] // 2 for b in bufs]

    def body(*refs):
        srcs, outs, send_sems, recv_sems = refs[:n], refs[n:2 * n], refs[2 * n], refs[2 * n + 1]
        x, y, c = lax.axis_index("x"), lax.axis_index("y"), lax.axis_index("c")
        for t in range(n):
            pltpu.make_async_remote_copy(
                src_ref=_half_rows(srcs[t], None, half[t], c), dst_ref=_half_rows(outs[t], None, half[t], c),
                send_sem=send_sems.at[t], recv_sem=recv_sems.at[t], device_id=(x, y, 1 - c), device_id_type=MESH_ID).start()
        for t in range(n):
            pltpu.make_async_remote_copy(
                src_ref=_half_rows(srcs[t], None, half[t], c), dst_ref=_half_rows(outs[t], None, half[t], 1 - c),
                send_sem=send_sems.at[t], recv_sem=recv_sems.at[t], device_id=(x, y, 1 - c), device_id_type=MESH_ID).wait()

    return pl.pallas_call(
        body, out_shape=[jax.ShapeDtypeStruct(b.shape, b.dtype) for b in bufs], in_specs=[_ANY] * n, out_specs=[_ANY] * n,
        input_output_aliases={t: t for t in range(n)},
        scratch_shapes=[pltpu.SemaphoreType.DMA((n,)), pltpu.SemaphoreType.DMA((n,))], name=name)(*bufs)


def _pack_rows(flats, dtype, row_multiple):
    flat = jnp.concatenate([f.reshape(-1).astype(dtype) for f in flats])
    n = flat.shape[0]
    rows = -(-n // PACK_W)
    rows = -(-rows // row_multiple) * row_multiple
    return jnp.pad(flat, (0, rows * PACK_W - n)).reshape(rows, PACK_W)


def _unpack(flat, shapes):
    out, off = [], 0
    for shp in shapes:
        n = math.prod(shp)
        out.append(flat[off:off + n].reshape(shp))
        off += n
    return out


_W_IN_SEGMENTS = ((R_ML, R_END, OFF_ML), (R_SG, R_ML, OFF_SG), (R_CV, R_SGI, OFF_CV), (R_SGI, R_MQ, OFF_SGI), (R_MQ, R_SG, OFF_MQ),
                  (R_CQ, R_CKV, OFF_CQ), (R_CKV, R_KR, OFF_CKV), (R_KR, R_CV, OFF_KR + NOPE))
W_IN_SHARD = R_END // N_CHIPS


def _realign_call(wg):
    tr = 128

    def body(w_ref, o_ref):
        pieces, pos = [], 0
        for r0, r1, a0 in _W_IN_SEGMENTS:
            if a0 > pos:
                pieces.append(jnp.zeros((tr, a0 - pos), o_ref.dtype))
            while r0 < r1:
                j = r0 // W_IN_SHARD
                hi = min(r1, (j + 1) * W_IN_SHARD)
                pieces.append(w_ref[j, :, r0 - j * W_IN_SHARD:hi - j * W_IN_SHARD])
                a0, r0 = a0 + hi - r0, hi
            pos = a0
        pieces.append(jnp.zeros((tr, NP - pos), o_ref.dtype))
        o_ref[...] = jnp.concatenate(pieces, axis=1)

    return pl.pallas_call(
        body, grid=(D // tr,), in_specs=[_bs((N_CHIPS, tr, W_IN_SHARD), lambda i: (0, i, 0))],
        out_specs=_bs((tr, NP), lambda i: (i, 0)), out_shape=jax.ShapeDtypeStruct((D, NP), wg.dtype),
        name="w_in_realign", compiler_params=_cparams(1))(wg)


def _unalign_call(dw, out_dtype):
    tr = 128
    by_ref = sorted(_W_IN_SEGMENTS)

    def body(dw_ref, o_ref):
        for j in range(N_CHIPS):
            lo_j, hi_j = j * W_IN_SHARD, (j + 1) * W_IN_SHARD
            pieces = []
            for r0, r1, a0 in by_ref:
                lo, hi = max(r0, lo_j), min(r1, hi_j)
                if lo < hi:
                    pieces.append(dw_ref[:, a0 + lo - r0:a0 + hi - r0])
            o_ref[j] = jnp.concatenate(pieces, axis=1).astype(o_ref.dtype)

    return pl.pallas_call(
        body, grid=(D // tr,), in_specs=[_bs((tr, NP), lambda i: (i, 0))],
        out_specs=_bs((N_CHIPS, tr, W_IN_SHARD), lambda i: (0, i, 0)),
        out_shape=jax.ShapeDtypeStruct((N_CHIPS, D, W_IN_SHARD), out_dtype), name="w_in_unalign", compiler_params=_cparams(1))(dw)


def _wuq_to_heads(w):
    w3 = w.reshape(QL, H, QKH)
    w3 = jnp.pad(w3, ((0, 0), (0, 0), (0, LANES - QKH)))
    return jnp.transpose(w3, (1, 0, 2))


def _wuq_from_heads(wh):
    return jnp.transpose(wh[:, :, :QKH], (1, 0, 2)).reshape(QL, H * QKH)


def _wukv_to_heads(w):
    w3 = w.reshape(KVL, H, NOPE + VH)
    wkn = jnp.transpose(jnp.pad(w3[:, :, :NOPE], ((0, 0), (0, 0), (0, LANES - NOPE))), (1, 0, 2))
    wv3 = w3[:, :, NOPE:]
    z = jnp.zeros((KVL, VH), w.dtype)
    cols = []
    for h in range(H):
        cols += [wv3[:, h], z] if h % 2 == 0 else [z, wv3[:, h]]
    return wkn, jnp.concatenate(cols, axis=1)


def _wukv_from_heads(wkn, wv):
    kn = jnp.transpose(wkn[:, :, :NOPE], (1, 0, 2))
    vs = jnp.stack([wv[:, LANES * h + VH * (h % 2):LANES * h + VH * (h % 2) + VH] for h in range(H)], axis=1)
    return jnp.concatenate([kn, vs], axis=2).reshape(KVL, H * (NOPE + VH))


def _layer_fwd(x, mem, tabs, p):
    proj, h = _proj_call(x, p["norm_g"], p["w_in"])
    if p.get("late") is not None:
        p = dict(p, **p["late"](proj))
    q, k, v = _mla_prep_call(proj, tabs, p["cq_g"], p["ckv_g"], p["qg"], p["kg"], p["wuq"], p["wkn"], p["wv"])
    ya, attn_o, attn_lse = _attn_call(q, k, v, proj)
    bm = p["bm"]
    if p.get("after_attn") is not None:
        bm = _tie(bm, p["after_attn"](ya))
    yb = _conv_call(proj, p["conv_w"], p["conv_b"])
    yc = _sg_call(proj, p["ln_g"], p["ln_b"], p["ws"], p["bs"])
    mk, mv = _memkv_call(mem, p["mem_g"], p["wm"], p["mkg"])
    yd = _mem_call(proj, mk, mv, p["mqg"])
    out = _merge_call((ya, yb, yc, yd), proj, bm, p["wb"], p["wo"], x)
    return out, dict(p=p, x=x, proj=proj, h=h, q=q, k=k, v=v, attn_o=attn_o, attn_lse=attn_lse, ys=(ya, yb, yc, yd), mk=mk, mv=mv)


def _layer_bwd(dout, mem, tabs, p, sv, start_after=None, on_rest_grads=None, on_grads=None):
    proj = sv["proj"]
    bm = p["bm"] if start_after is None else _tie(p["bm"], start_after)
    dya, dyb, dyc, dyd, dml, dbm, dwb, dwo = _merge_bwd_call(sv["ys"], proj, bm, p["wb"], p["wo"], dout)
    dq, dk, dv, dsg_a = _attn_bwd_call(sv["q"], sv["k"], sv["v"], proj, dya, sv["attn_o"], sv["attn_lse"])
    dcq, dckv, dkr, dcqg, dckvg, dqg, dkg, dwuq, dwkn, dwv = _mla_prep_bwd_call(
        proj, tabs, p["cq_g"], p["ckv_g"], p["qg"], p["kg"], p["wuq"], p["wkn"], p["wv"], dq, dk, dv)
    dbg, dcg, dxi, dsg_b, dcw, dcb = _conv_bwd_call(proj, p["conv_w"], p["conv_b"], dyb)
    du, dvv, dsg_c, dlg, dlb, dws, dbs = _sg_bwd_call(proj, p["ln_g"], p["ln_b"], p["ws"], p["bs"], dyc)
    dmq, dsg_d, dmk, dmv, dmqg = _mem_bwd_call(proj, sv["mk"], sv["mv"], p["mqg"], dyd)
    dmem_g, dwm, dmkg = _memkv_bwd_call(mem, p["mem_g"], p["wm"], p["mkg"], dmk, dmv)
    grads = dict(cq_norm_g=dcqg[0], ckv_norm_g=dckvg[0], mla_q_norm_g=dqg[0, :QKH], mla_k_norm_g=dkg[0, :QKH],
                 conv_w=dcw, conv_b=dcb[0], sg_ln_g=dlg[0], sg_ln_b=dlb[0], w_spatial=dws, b_spatial=dbs[:, :, 0],
                 mem_norm_g=dmem_g[0], mem_q_norm_g=dmqg[0], mem_k_norm_g=dmkg[0], b_merge=dbm,
                 wuq_heads=dwuq, wkn_heads=dwkn, wv_heads=dwv, w_mem_kv=dwm, w_branch_chips=dwb, w_out=dwo)
    started = [on_rest_grads(grads)] if on_rest_grads is not None else []
    dproj = jnp.concatenate([dml, dsg_a, dsg_b, dsg_c, dsg_d, dbg, dcg, dxi, du, dvv, dmq, dcq, dckv, dkr], axis=1)
    grads["w_in_aligned"] = _dw_call(sv["h"], dproj, started)
    tokens = on_grads(grads) if on_grads is not None else ()
    dx, dnorm_g = _dh_call(dproj, p["w_in"], sv["x"], p["norm_g"], dout, tokens)
    grads["norm_g"] = dnorm_g[0]
    return dx, grads


def _chips_to_cols(a):
    return jnp.concatenate([a[j] for j in range(N_CHIPS)], axis=1)


def _cols_to_chips(a):
    cols = a.shape[1] // N_CHIPS
    return jnp.stack([a[:, cols * j:cols * (j + 1)] for j in range(N_CHIPS)])


def _layer_params_first(l, rep, w_in_gathered, conv_w, b_merge):
    pad_g = lambda g: jnp.pad(g, (0, LANES - QKH)).reshape(1, LANES)
    return dict(
        norm_g=rep["norm_g"][l].reshape(1, D), w_in=_realign_call(w_in_gathered),
        cq_g=rep["cq_norm_g"][l].reshape(1, QL), ckv_g=rep["ckv_norm_g"][l].reshape(1, KVL),
        qg=pad_g(rep["mla_q_norm_g"][l]), kg=pad_g(rep["mla_k_norm_g"][l]),
        conv_w=conv_w, conv_b=rep["conv_b"][l].reshape(1, CW),
        ln_g=rep["sg_ln_g"][l].reshape(1, SGW), ln_b=rep["sg_ln_b"][l].reshape(1, SGW),
        ws=rep["w_spatial"][l], bs=rep["b_spatial"][l].reshape(SGG, SGC, 1),
        mem_g=rep["mem_norm_g"][l].reshape(1, D),
        mqg=rep["mem_q_norm_g"][l].reshape(1, MHD), mkg=rep["mem_k_norm_g"][l].reshape(1, MHD), bm=b_merge)


def _layer_params_rest(gathered):
    wkn, wv = _wukv_to_heads(_chips_to_cols(gathered["w_ukv"]))
    return dict(wuq=_wuq_to_heads(_chips_to_cols(gathered["w_uq"])), wkn=wkn, wv=wv,
                wm=gathered["w_mem_kv"].reshape(D, 2 * MH * MHD), wb=gathered["w_branch"], wo=gathered["w_out"].reshape(D, D))


def _layer_params(l, rep, gathered, conv_w, b_merge):
    return dict(_layer_params_first(l, rep, gathered["w_in"], conv_w, b_merge), **_layer_params_rest(gathered))


def _forward_backward(x, mem, pos, target, params, bwd_hooks=None):
    tabs = _rope_tables(pos)
    params = list(params)
    saved = []
    act = x
    for l in range(DEPTH):
        if callable(params[l]):
            params[l] = params[l](saved[-1], act)
        act, sv = _layer_fwd(act, mem, tabs, params[l])
        saved.append(sv)
    dy, sq = _loss_call(act, target)
    grads = [None] * DEPTH
    token = None
    for l in reversed(range(DEPTH)):
        hooks = dict(bwd_hooks[l]) if bwd_hooks else {}
        after_layer = hooks.pop("after_layer", None)
        dy, grads[l] = _layer_bwd(dy, mem, tabs, saved[l]["p"], saved[l], start_after=token, **hooks)
        token = after_layer(dy) if after_layer is not None else None
    return sq, dy, grads


_SHARDED_MM = ("w_in", "w_branch", "w_out", "w_mem_kv", "w_uq", "w_ukv")
_SHARDED_F32 = ("conv_w", "b_merge")
_REPLICATED = ("norm_g", "cq_norm_g", "ckv_norm_g", "mla_q_norm_g", "mla_k_norm_g", "conv_b", "sg_ln_g", "sg_ln_b",
               "w_spatial", "b_spatial", "mem_norm_g", "mem_q_norm_g", "mem_k_norm_g")
_ALL_REDUCED = _REPLICATED + _SHARDED_F32
_WEIGHTS = ("norm_g", "w_in", "cq_norm_g", "ckv_norm_g", "w_uq", "w_ukv", "mla_q_norm_g", "mla_k_norm_g", "conv_w", "conv_b",
            "sg_ln_g", "sg_ln_b", "w_spatial", "b_spatial", "mem_norm_g", "w_mem_kv", "mem_q_norm_g", "mem_k_norm_g",
            "b_merge", "w_branch", "w_out")
_SMALL = tuple(n for n in _WEIGHTS if n not in _SHARDED_MM)


class _SmallGather:
    def __init__(self, blk, after, tag):
        self.blk, self.tag = blk, tag
        x, y, c = lax.axis_index("x"), lax.axis_index("y"), lax.axis_index("c")
        own = _place_block_call(blk, (4 * x + 2 * y + c).astype(jnp.int32).reshape(1), tag + "place_own")
        self.buf, self.send, self.recv, self.token = _small_gather_start_call(blk, own, after, tag + "start")

    def finish(self, after):
        return _small_gather_finish_call(self.blk, self.buf, self.send, self.recv, after, self.tag + "finish")


def _small_sharded_weights(w, got):
    names = _SHARDED_F32
    per_chip = [_unpack(got[2 * j].reshape(-1), [w[n].shape for n in names]) for j in range(N_CHIPS)]
    return {n: jnp.concatenate([per_chip[j][t] for j in range(N_CHIPS)], axis=2) for t, n in enumerate(names)}


class _Gather:
    def __init__(self, w, layer, names, after, tag):
        self.names, self.tag = names, tag
        x, y, c = lax.axis_index("x"), lax.axis_index("y"), lax.axis_index("c")
        chip_core = jnp.stack([2 * x + y, c]).astype(jnp.int32)
        halves = [w[n].shape[1] // 2 for n in names]
        self.srcs = [lax.dynamic_slice_in_dim(w[n][layer], c * h, h, axis=0).astype(MM) for n, h in zip(names, halves)]
        k = len(names)
        out = _gather_start_call(self.srcs, _place_own_call(self.srcs, chip_core, tag + "place_own"), after, tag + "start")
        self.bufs, self.send, self.recv_sib, self.recv_ici, self.token = out[:k], out[k], out[k + 1], out[k + 2], out[k + 3]

    def pass_on(self, after):
        k = len(self.names)
        out = _gather_forward_call(self.bufs, self.recv_ici, after, self.tag + "forward")
        self.bufs, self.send_fwd, self.recv_fwd = out[:k], out[k], out[k + 1]
        return out[k + 2]

    def finish(self, after):
        got = _gather_finish_call(self.srcs, self.bufs, self.send, self.recv_sib, self.send_fwd, self.recv_fwd, after,
                                  self.tag + "finish")
        return dict(zip(self.names, got))


class _ReduceScatter:
    SLABS = dict(
        w_in=lambda g: _unalign_call(g["w_in_aligned"], MM),
        w_branch=lambda g: g["w_branch_chips"].reshape(N_CHIPS, NB * BW, D // N_CHIPS),
        w_out=lambda g: g["w_out"].reshape(N_CHIPS, D // N_CHIPS, D),
        w_mem_kv=lambda g: g["w_mem_kv"].reshape(N_CHIPS, D // N_CHIPS, 2 * MH * MHD),
        w_uq=lambda g: _cols_to_chips(_wuq_from_heads(g["wuq_heads"])),
        w_ukv=lambda g: _cols_to_chips(_wukv_from_heads(g["wkn_heads"], g["wv_heads"])))

    def __init__(self, tag, names):
        self.tag, self.names = tag, names

    def exchange(self, grads):
        self.tensors = [self.SLABS[n](grads) for n in self.names]
        n = len(self.tensors)
        out = _pair_exchange_start_call(self.tensors, self.tag + "exchange_start")
        self.ex_bufs, self.ex_send, self.ex_recv = out[:n], out[n], out[n + 1]
        return out[n + 2]

    def scatter(self, after):
        n = len(self.tensors)
        c = lax.axis_index("c")
        from_sibling = _pair_exchange_finish_call(self.tensors, self.ex_bufs, self.ex_send, self.ex_recv, after,
                                                  self.tag + "exchange_finish")
        self.chip_sums = _pair_sum_call(self.tensors, from_sibling, c.astype(jnp.int32).reshape(1), self.tag + "pair_sum")
        out = _chip_scatter_start_call(self.chip_sums, self.tag + "scatter_start")
        self.bufs, self.send_sems, self.recv_sems, self.token = out[:n], out[n], out[n + 1], out[n + 2]
        return self.token

    def finish(self, after):
        x, y, c = lax.axis_index("x"), lax.axis_index("y"), lax.axis_index("c")
        chip_core = jnp.stack([2 * x + y, c]).astype(jnp.int32)
        from_chips = _chip_scatter_finish_call(self.chip_sums, self.bufs, self.send_sems, self.recv_sems, after,
                                               self.tag + "scatter_finish")
        mine = _owner_sum_call(self.chip_sums, from_chips, chip_core, self.tag + "owner_sum")
        return dict(zip(self.names, _pair_gather_call(mine, self.tag + "pair_gather")))


def _small_sums(g, sq, got):
    total = _sum8_call(got).reshape(-1)
    parts = _unpack(total, [g[n].shape for n in _ALL_REDUCED] + [sq.shape])
    out = dict(zip(_ALL_REDUCED, parts))
    sq_total = parts[-1]
    chip = 2 * lax.axis_index("x") + lax.axis_index("y")
    for n in _SHARDED_F32:
        size = out[n].shape[2] // N_CHIPS
        out[n] = lax.dynamic_slice_in_dim(out[n], chip * size, size, axis=2)
    return out, sq_total


def _adamw_small(w, g, m, v):
    delta, new_m, new_v = {}, {}, {}
    shapes = [w[n].shape for n in _SMALL]
    pk = lambda t: _pack_rows([t[n] for n in _SMALL], F32, 64)
    d, nm, nv = _adamw_call(pk(w), pk(g), pk(m), pk(v), "adamw_small")
    for out, packed in ((delta, d), (new_m, nm), (new_v, nv)):
        out.update(zip(_SMALL, _unpack(packed.reshape(-1), shapes)))
    return delta, new_m, new_v


def kernel(x, mem, positions, norm_g, w_in, cq_norm_g, ckv_norm_g, w_uq, w_ukv, mla_q_norm_g, mla_k_norm_g, conv_w, conv_b, sg_ln_g, sg_ln_b, w_spatial, b_spatial, mem_norm_g, w_mem_kv, mem_q_norm_g, mem_k_norm_g, b_merge, w_branch, w_out, loss_target, m_norm_g, m_w_in, m_cq_norm_g, m_ckv_norm_g, m_w_uq, m_w_ukv, m_mla_q_norm_g, m_mla_k_norm_g, m_conv_w, m_conv_b, m_sg_ln_g, m_sg_ln_b, m_w_spatial, m_b_spatial, m_mem_norm_g, m_w_mem_kv, m_mem_q_norm_g, m_mem_k_norm_g, m_b_merge, m_w_branch, m_w_out, v_norm_g, v_w_in, v_cq_norm_g, v_ckv_norm_g, v_w_uq, v_w_ukv, v_mla_q_norm_g, v_mla_k_norm_g, v_conv_w, v_conv_b, v_sg_ln_g, v_sg_ln_b, v_w_spatial, v_b_spatial, v_mem_norm_g, v_w_mem_kv, v_mem_q_norm_g, v_mem_k_norm_g, v_b_merge, v_w_branch, v_w_out):
    w = dict(norm_g=norm_g, w_in=w_in, cq_norm_g=cq_norm_g, ckv_norm_g=ckv_norm_g, w_uq=w_uq, w_ukv=w_ukv,
             mla_q_norm_g=mla_q_norm_g, mla_k_norm_g=mla_k_norm_g, conv_w=conv_w, conv_b=conv_b, sg_ln_g=sg_ln_g,
             sg_ln_b=sg_ln_b, w_spatial=w_spatial, b_spatial=b_spatial, mem_norm_g=mem_norm_g, w_mem_kv=w_mem_kv,
             mem_q_norm_g=mem_q_norm_g, mem_k_norm_g=mem_k_norm_g, b_merge=b_merge, w_branch=w_branch, w_out=w_out)
    m = dict(norm_g=m_norm_g, w_in=m_w_in, cq_norm_g=m_cq_norm_g, ckv_norm_g=m_ckv_norm_g, w_uq=m_w_uq, w_ukv=m_w_ukv,
             mla_q_norm_g=m_mla_q_norm_g, mla_k_norm_g=m_mla_k_norm_g, conv_w=m_conv_w, conv_b=m_conv_b, sg_ln_g=m_sg_ln_g,
             sg_ln_b=m_sg_ln_b, w_spatial=m_w_spatial, b_spatial=m_b_spatial, mem_norm_g=m_mem_norm_g, w_mem_kv=m_w_mem_kv,
             mem_q_norm_g=m_mem_q_norm_g, mem_k_norm_g=m_mem_k_norm_g, b_merge=m_b_merge, w_branch=m_w_branch, w_out=m_w_out)
    v = dict(norm_g=v_norm_g, w_in=v_w_in, cq_norm_g=v_cq_norm_g, ckv_norm_g=v_ckv_norm_g, w_uq=v_w_uq, w_ukv=v_w_ukv,
             mla_q_norm_g=v_mla_q_norm_g, mla_k_norm_g=v_mla_k_norm_g, conv_w=v_conv_w, conv_b=v_conv_b, sg_ln_g=v_sg_ln_g,
             sg_ln_b=v_sg_ln_b, w_spatial=v_w_spatial, b_spatial=v_b_spatial, mem_norm_g=v_mem_norm_g, w_mem_kv=v_w_mem_kv,
             mem_q_norm_g=v_mem_q_norm_g, mem_k_norm_g=v_mem_k_norm_g, b_merge=v_b_merge, w_branch=v_w_branch, w_out=v_w_out)

    chip_core = jnp.stack([2 * lax.axis_index("x") + lax.axis_index("y"), lax.axis_index("c")]).astype(jnp.int32)

    first = _Gather(w, 0, ("w_in",), chip_core, "gather_l0_w_in_")
    rest = _Gather(w, 0, _SHARDED_MM[1:], first.token, "gather_l0_rest_")
    small_on_its_way = _SmallGather(_pack_rows([w[n] for n in _SHARDED_F32], F32, 8), [rest.token], "gather_small_weights_")
    later = _Gather(w, 1, _SHARDED_MM, small_on_its_way.token, "gather_l1_")
    w_in0 = first.finish(first.pass_on(later.token))["w_in"]
    small = {}

    def rest_of_layer0(proj0):
        landed = _layer_params_rest(rest.finish(rest.pass_on(proj0)))
        small.update(_small_sharded_weights(w, small_on_its_way.finish([landed["wo"]])))
        return dict(landed, conv_w=small["conv_w"][0], bm=small["b_merge"][0])

    def layer1_params(saved0, act0):
        return _layer_params(1, w, later.finish(act0), small["conv_w"][1], small["b_merge"][1])

    params0 = _layer_params_first(0, w, w_in0, None, None)
    params = [dict(params0, late=rest_of_layer0, after_attn=later.pass_on), layer1_params]
    others = _SHARDED_MM[1:]
    rs1 = _ReduceScatter("rs_l1_", _SHARDED_MM)
    rs0_rest, rs0_w_in = _ReduceScatter("rs_l0_rest_", others), _ReduceScatter("rs_l0_w_in_", ("w_in",))

    def layer0_grads_done(grads):
        return [rs0_rest.scatter([grads["w_in_aligned"]]), rs0_w_in.exchange(grads)]

    hooks = [dict(on_rest_grads=rs0_rest.exchange, on_grads=layer0_grads_done),
             dict(on_grads=lambda grads: [rs1.exchange(grads)], after_layer=lambda dy: rs1.scatter([dy]))]
    sq, grad_x, layer_grads = _forward_backward(x[0], mem[0], positions[0], loss_target[0], params, hooks)

    g_small = {n: jnp.stack([layer_grads[l][n] for l in range(DEPTH)]) for n in _ALL_REDUCED}
    small_grads = _SmallGather(_pack_rows([g_small[n] for n in _ALL_REDUCED] + [sq], F32, 64), [grad_x], "gather_small_grads_")
    scattering = rs0_w_in.scatter([grad_x, small_grads.token])
    shard_grads = {1: rs1.finish([scattering]), 0: rs0_rest.finish([scattering])}
    as3d = lambda a: a.reshape(DEPTH, -1, a.shape[-1])
    as2d = lambda a: a.reshape(-1, a.shape[-1])
    big = lambda t: [as3d(t[n]) for n in others]
    turned = lambda t: [jnp.swapaxes(t["w_in"], 1, 2)]
    assert W_IN_SHARD % (8 * 7) == 0

    def update_w_in(l, grad, prev):
        return _adamw_layer_call(l, turned(w), [grad.T], turned(m), turned(v), prev, [], "adamw_w_in_l%d" % l, steps=7)

    def update_others(l, prev):
        return _adamw_layer_call(l, big(w), [as2d(shard_grads[l][n]) for n in others], big(m), big(v), prev, [], "adamw_l%d" % l)

    upd = update_others(0, update_others(1, None))
    g, sq_total = _small_sums(g_small, sq, small_grads.finish([upd[0]]))
    loss = 0.5 / D * jnp.sum(sq_total)
    delta, new_m, new_v = _adamw_small(w, g, m, v)
    upd_in1 = update_w_in(1, shard_grads[1]["w_in"], None)
    w_in_grad0 = rs0_w_in.finish([grad_x, upd_in1[0], upd[0], delta["norm_g"]])["w_in"]
    upd_in = update_w_in(0, w_in_grad0, upd_in1)
    g["w_in"], delta["w_in"], new_m["w_in"], new_v["w_in"] = [jnp.swapaxes(a, 1, 2) for a in upd_in]
    for t, n in enumerate(others):
        g[n], delta[n], new_m[n], new_v[n] = [a.reshape(w[n].shape) for a in upd[4 * t:4 * t + 4]]
    return (loss, grad_x[None], *[g[n] for n in _WEIGHTS], *[delta[n] for n in _WEIGHTS],
            *[new_m[n] for n in _WEIGHTS], *[new_v[n] for n in _WEIGHTS])
```

```python
import functools
import math

import jax
import jax.numpy as jnp
from jax import lax
from jax.experimental import pallas as pl
from jax.experimental.pallas import tpu as pltpu

F32 = jnp.float32
MM = jnp.bfloat16

D = 1024
DEPTH = 2
EPS = 1e-6
H = 8
NOPE = 64
ROPE = 32
QKH = 96
VH = 64
QL = 256
KVL = 128
ROPE_THETA = 10000.0
CW = 512
SGW = 512
SGG = 4
SGC = 128
MH = 4
MHD = 128
NB = 4
BW = 512
NEG_INF = -1e30
LANES = 128
N_CHIPS = 4

R_CQ, R_CKV, R_KR, R_CV, R_SGI, R_MQ, R_SG, R_ML, R_END = 0, 256, 384, 416, 1952, 2976, 3488, 5536, 9632
OFF_ML, OFF_SG, OFF_CV, OFF_SGI, OFF_MQ, OFF_CQ, OFF_CKV, OFF_KR, NP = 0, 4096, 6144, 7680, 8704, 9216, 9472, 9600, 9728

ADAM_LR = 0.001
ADAM_B1 = 0.9
ADAM_B2 = 0.999
ADAM_EPS = 1e-08
ADAM_WD = 0.01
ADAM_STEP = 10

VMEM_LIMIT = 56 * 1024 * 1024
PACK_W = 512
MESH_ID = pl.DeviceIdType.MESH


def _cparams(n_axes):
    return pltpu.CompilerParams(dimension_semantics=("arbitrary",) * n_axes, vmem_limit_bytes=VMEM_LIMIT)


def _bs(shape, imap):
    return pl.BlockSpec(shape, imap)


@jax.custom_vjp
def _mm_plain(a, b):
    return jnp.dot(a.astype(MM), b.astype(MM), preferred_element_type=F32)


def _mm_plain_fwd(a, b):
    return _mm_plain(a, b), (a, b)


def _mm_plain_bwd(res, g):
    a, b = res
    gm = g.astype(MM)
    da = lax.dot_general(gm, b.astype(MM), (((1,), (1,)), ((), ())), preferred_element_type=F32)
    db = lax.dot_general(a.astype(MM), gm, (((0,), (0,)), ((), ())), preferred_element_type=F32)
    return da.astype(a.dtype), db.astype(b.dtype)


_mm_plain.defvjp(_mm_plain_fwd, _mm_plain_bwd)


@jax.custom_vjp
def _mm_slot(a, w, slot):
    return jnp.dot(a.astype(MM), w.astype(MM), preferred_element_type=F32)


def _mm_slot_fwd(a, w, slot):
    return _mm_slot(a, w, slot), (a, w)


def _mm_slot_bwd(res, g):
    a, w = res
    gm = g.astype(MM)
    da = lax.dot_general(gm, w.astype(MM), (((1,), (1,)), ((), ())), preferred_element_type=F32)
    dw = lax.dot_general(a.astype(MM), gm, (((0,), (0,)), ((), ())), preferred_element_type=F32)
    return da.astype(a.dtype), jnp.zeros_like(w), dw


_mm_slot.defvjp(_mm_slot_fwd, _mm_slot_bwd)


def _mm(a, b):
    if isinstance(b, tuple):
        return _mm_slot(a, b[0], b[1])
    return _mm_plain(a, b)


def _with_slot(w):
    return (w, jnp.zeros(w.shape, F32))


@jax.custom_vjp
def _mm_nt(a, b):
    return lax.dot_general(a.astype(MM), b.astype(MM), (((1,), (1,)), ((), ())), preferred_element_type=F32)


def _mm_nt_fwd(a, b):
    return _mm_nt(a, b), (a, b)


def _mm_nt_bwd(res, g):
    a, b = res
    gm = g.astype(MM)
    da = jnp.dot(gm, b.astype(MM), preferred_element_type=F32)
    db = lax.dot_general(gm, a.astype(MM), (((0,), (0,)), ((), ())), preferred_element_type=F32)
    return da.astype(a.dtype), db.astype(b.dtype)


_mm_nt.defvjp(_mm_nt_fwd, _mm_nt_bwd)


@functools.partial(jax.custom_vjp, nondiff_argnums=(1,))
def _lane_roll(x, shift):
    return pltpu.roll(x, shift, 1)


def _lane_roll_fwd(x, shift):
    return pltpu.roll(x, shift, 1), None


def _lane_roll_bwd(shift, _, g):
    return (pltpu.roll(g, (LANES - shift) % LANES, 1),)


_lane_roll.defvjp(_lane_roll_fwd, _lane_roll_bwd)


def _rms_n(x, g, n):
    ms = jnp.sum(x * x, axis=-1, keepdims=True) * (1.0 / n)
    return x * lax.rsqrt(ms + EPS) * g


def _softmax(s):
    m = jnp.max(s, axis=-1, keepdims=True)
    e = jnp.exp(s - m)
    return e / jnp.sum(e, axis=-1, keepdims=True)


def _rope(t, cos_t, sin_a, sin_b):
    return t * cos_t + _lane_roll(t, LANES - 16) * sin_a + _lane_roll(t, 16) * sin_b


def _mla_prep_fn(cq, ckv, kr, cos_t, sin_a, sin_b, cq_g, ckv_g, qg, kg, wuq, wkn, wv):
    cqn = _rms_n(cq, cq_g, QL)
    ckvn = _rms_n(ckv, ckv_g, KVL)
    lane = lax.broadcasted_iota(jnp.int32, kr.shape, 1)
    krm = jnp.where((lane >= NOPE) & (lane < QKH), kr, 0.0)
    qs, ks = [], []
    for h in range(H):
        qh = _rms_n(_mm(cqn, wuq[h]), qg, QKH)
        qs.append(_rope(qh, cos_t, sin_a, sin_b) * (QKH ** -0.5))
        kh = _rms_n(_mm(ckvn, wkn[h]) + krm, kg, QKH)
        ks.append(_rope(kh, cos_t, sin_a, sin_b))
    return jnp.concatenate(qs, axis=-1), jnp.concatenate(ks, axis=-1), _mm(ckvn, wv)


def _dot_nt(a, b):
    return lax.dot_general(a.astype(MM), b.astype(MM), (((1,), (1,)), ((), ())), preferred_element_type=F32)


def _dot_tn(a, b):
    return lax.dot_general(a.astype(MM), b.astype(MM), (((0,), (0,)), ((), ())), preferred_element_type=F32)


def _causal_scores(qe, ke):
    tq, kl = qe.shape[0], ke.shape[0]
    s = _dot_nt(qe, ke)
    rows = lax.broadcasted_iota(jnp.int32, (tq, tq), 0)
    cols = lax.broadcasted_iota(jnp.int32, (tq, tq), 1)
    own = jnp.where(cols <= rows, s[:, kl - tq:], NEG_INF)
    return own if kl == tq else jnp.concatenate([s[:, :kl - tq], own], axis=1)


def _head_lanes(e, shape):
    lane = lax.broadcasted_iota(jnp.int32, shape, len(shape) - 1)
    return (lane >= VH * e) & (lane < VH * (e + 1))


def _attn_pair_fwd(q2, k2, v2):
    tq = q2.shape[0]
    o = jnp.zeros((tq, LANES), F32)
    lse = jnp.zeros((tq, LANES), F32)
    for e in range(2):
        sl = slice(LANES * e, LANES * (e + 1))
        s = _causal_scores(q2[:, sl], k2[:, sl])
        m = jnp.max(s, axis=-1, keepdims=True)
        ex = jnp.exp(s - m)
        l = jnp.sum(ex, axis=-1, keepdims=True)
        ve = jnp.where(_head_lanes(e, v2[:, sl].shape), v2[:, sl], 0.0)
        o = o + jnp.dot((ex * (1.0 / l)).astype(MM), ve.astype(MM), preferred_element_type=F32)
        lse = jnp.where(_head_lanes(e, lse.shape), m + jnp.log(l), lse)
    return o, lse


def _attn_pair_bwd(q2, k2, v2, sg, dys, o, lse):
    sig = jax.nn.sigmoid(sg)
    do = dys * (sg * sig)
    dsg = dys * o * (sig * (1.0 + sg * (1.0 - sig)))
    dqs, dks, dvs = [], [], []
    for e in range(2):
        sl = slice(LANES * e, LANES * (e + 1))
        qe, ke = q2[:, sl], k2[:, sl]
        hm = _head_lanes(e, o.shape)
        lse_e = jnp.max(jnp.where(hm, lse, NEG_INF), axis=-1, keepdims=True)
        do_e = jnp.where(hm, do, 0.0)
        delta = jnp.sum(do_e * o, axis=-1, keepdims=True)
        p = jnp.exp(_causal_scores(qe, ke) - lse_e)
        ve = jnp.where(_head_lanes(e, v2[:, sl].shape), v2[:, sl], 0.0)
        dvs.append(_dot_tn(p, do_e))
        ds = p * (_dot_nt(do_e, ve) - delta)
        dqs.append(jnp.dot(ds.astype(MM), ke.astype(MM), preferred_element_type=F32))
        dks.append(_dot_tn(ds, qe))
    return jnp.concatenate(dqs, axis=-1), jnp.concatenate(dks, axis=-1), jnp.concatenate(dvs, axis=-1), dsg


def _sg_fn(u, v, sgc, ln_g, ln_b, ws, bs):
    mu = jnp.mean(v, axis=-1, keepdims=True)
    xc = v - mu
    vn = xc * lax.rsqrt(jnp.mean(xc * xc, axis=-1, keepdims=True) + EPS) * ln_g + ln_b
    r = lax.broadcasted_iota(jnp.int32, (SGC, SGC), 0)
    c = lax.broadcasted_iota(jnp.int32, (SGC, SGC), 1)
    wt = [jnp.where(r >= c, w, 0.0) for w in ws]
    row_blocks = []
    for ch in range(u.shape[0] // SGC):
        col_blocks = []
        for g in range(SGG):
            blk = vn[SGC * ch:SGC * (ch + 1), LANES * g:LANES * (g + 1)]
            col_blocks.append(_mm(wt[g], blk) + bs[g])
        row_blocks.append(jnp.concatenate(col_blocks, axis=-1))
    mixed = jnp.concatenate(row_blocks, axis=0)
    return (u * mixed) * jax.nn.silu(sgc)


def _memkv_fn(mem, mem_g, wm, kg):
    kv = _mm(_rms_n(mem, mem_g, D), wm)
    ks = [_rms_n(kv[:, MHD * h:MHD * (h + 1)], kg, MHD) for h in range(MH)]
    return jnp.concatenate(ks, axis=-1), kv[:, MH * MHD:]


def _mem_fn(mq, sgd, k, v, qg):
    outs = []
    for h in range(MH):
        sl = slice(MHD * h, MHD * (h + 1))
        qh = _rms_n(mq[:, sl], qg, MHD)
        p = _softmax(_mm_nt(qh, k[:, sl]) * (MHD ** -0.5))
        outs.append(_mm(p, v[:, sl]))
    return jnp.concatenate(outs, axis=-1) * jax.nn.silu(sgd)


def _merge_fn(ys, logits, bm, wb, wo):
    merged = None
    for n in range(NB):
        z = jnp.concatenate([_mm(ys[n], wb[j][n]) for j in range(N_CHIPS)], axis=-1)
        gate = jax.nn.sigmoid(logits[:, D * n:D * (n + 1)] + bm[n])
        merged = gate * z if merged is None else merged + gate * z
    return _mm(merged, wo)


def _proj_call(x, g, w):
    s_len = x.shape[0]
    tm, tn = s_len, 512

    def body(x_ref, g_ref, w_ref, p_ref, h_ref):
        @pl.when(pl.program_id(1) == 0)
        def _():
            h_ref[...] = _rms_n(x_ref[...], g_ref[...], D).astype(h_ref.dtype)
        p_ref[...] = jnp.dot(h_ref[...], w_ref[...], preferred_element_type=F32)

    return pl.pallas_call(
        body, grid=(s_len // tm, NP // tn),
        in_specs=[_bs((tm, D), lambda i, j: (i, 0)), _bs((1, D), lambda i, j: (0, 0)), _bs((D, tn), lambda i, j: (0, j))],
        out_specs=[_bs((tm, tn), lambda i, j: (i, j)), _bs((tm, D), lambda i, j: (i, 0))],
        out_shape=[jax.ShapeDtypeStruct((s_len, NP), F32), jax.ShapeDtypeStruct((s_len, D), MM)],
        name="proj", compiler_params=_cparams(2))(x, g, w)


def _rope_tables(pos):
    half = ROPE // 2
    inv_freq = ROPE_THETA ** (-jnp.arange(half, dtype=F32) / half)
    ang = pos.astype(F32)[:, None] * inv_freq
    cos, sin = jnp.cos(ang), jnp.sin(ang)
    s_len = pos.shape[0]
    z = lambda n: jnp.zeros((s_len, n), F32)
    cos_t = jnp.concatenate([jnp.ones((s_len, NOPE), F32), cos, cos, z(LANES - QKH)], axis=1)
    sin_a = jnp.concatenate([z(NOPE), -sin, z(LANES - NOPE - half)], axis=1)
    sin_b = jnp.concatenate([z(NOPE + half), sin, z(LANES - QKH)], axis=1)
    return cos_t, sin_a, sin_b


def _mla_prep_specs(tm):
    row = lambda w, off: _bs((tm, w), lambda i: (i, off // w))
    full2 = lambda a, b: _bs((a, b), lambda i: (0, 0))
    full3 = lambda a, b, c: _bs((a, b, c), lambda i: (0, 0, 0))
    tab = _bs((tm, LANES), lambda i: (i, 0))
    return [row(QL, OFF_CQ), row(KVL, OFF_CKV), row(LANES, OFF_KR), tab, tab, tab,
            full2(1, QL), full2(1, KVL), full2(1, LANES), full2(1, LANES),
            full3(H, QL, LANES), full3(H, KVL, LANES), full2(KVL, H * LANES)]


def _mla_prep_args(body_refs, wrap=lambda w: w):
    (cq, ckv, kr, ct, sa, sb, cqg, ckvg, qg, kg, wuq, wkn, wv) = body_refs
    return (cq[...], ckv[...], kr[...], ct[...], sa[...], sb[...], cqg[...], ckvg[...], qg[...], kg[...],
            [wrap(wuq[h]) for h in range(H)], [wrap(wkn[h]) for h in range(H)], wrap(wv[...]))


def _mla_prep_call(proj, tabs, cq_g, ckv_g, qg, kg, wuq, wkn, wv):
    s_len = proj.shape[0]
    tm = min(s_len, 256)

    def body(*refs):
        q_ref, k_ref, v_ref = refs[13:]
        q, k, v = _mla_prep_fn(*_mla_prep_args(refs[:13]))
        q_ref[...] = q.astype(q_ref.dtype)
        k_ref[...] = k.astype(k_ref.dtype)
        v_ref[...] = v.astype(v_ref.dtype)

    out = _bs((tm, H * LANES), lambda i: (i, 0))
    return pl.pallas_call(
        body, grid=(s_len // tm,), in_specs=_mla_prep_specs(tm), out_specs=[out, out, out],
        out_shape=[jax.ShapeDtypeStruct((s_len, H * LANES), MM)] * 3,
        name="mla_prep", compiler_params=_cparams(1))(proj, proj, proj, *tabs, cq_g, ckv_g, qg, kg, wuq, wkn, wv)


def _mla_prep_bwd_call(proj, tabs, cq_g, ckv_g, qg, kg, wuq, wkn, wv, dq, dk, dv):
    s_len = proj.shape[0]
    tm = min(s_len, 256)

    def body(*refs):
        dq_ref, dk_ref, dv_ref = refs[13:16]
        dcq_ref, dckv_ref, dkr_ref, dcqg_ref, dckvg_ref, dqg_ref, dkg_ref, dwuq_ref, dwkn_ref, dwv_ref = refs[16:]
        _, vjp = jax.vjp(_mla_prep_fn, *_mla_prep_args(refs[:13], _with_slot))
        (dcq, dckv, dkr, _, _, _, dcqg, dckvg, dqg, dkg, dwuq, dwkn, dwv) = vjp((dq_ref[...], dk_ref[...], dv_ref[...]))
        dwuq, dwkn, dwv = [d[1] for d in dwuq], [d[1] for d in dwkn], dwv[1]
        dcq_ref[...] = dcq.astype(dcq_ref.dtype)
        dckv_ref[...] = dckv.astype(dckv_ref.dtype)
        dkr_ref[...] = dkr.astype(dkr_ref.dtype)

        @pl.when(pl.program_id(0) == 0)
        def _():
            for r in (dcqg_ref, dckvg_ref, dqg_ref, dkg_ref, dwuq_ref, dwkn_ref, dwv_ref):
                r[...] = jnp.zeros_like(r)
        dcqg_ref[...] += dcqg
        dckvg_ref[...] += dckvg
        dqg_ref[...] += dqg
        dkg_ref[...] += dkg
        for h in range(H):
            dwuq_ref[h] += dwuq[h]
            dwkn_ref[h] += dwkn[h]
        dwv_ref[...] += dwv

    big = _bs((tm, H * LANES), lambda i: (i, 0))
    row = lambda w: _bs((tm, w), lambda i: (i, 0))
    full2 = lambda a, b: _bs((a, b), lambda i: (0, 0))
    full3 = lambda a, b, c: _bs((a, b, c), lambda i: (0, 0, 0))
    sd = jax.ShapeDtypeStruct
    return pl.pallas_call(
        body, grid=(s_len // tm,), in_specs=_mla_prep_specs(tm) + [big, big, big],
        out_specs=[row(QL), row(KVL), row(LANES), full2(1, QL), full2(1, KVL), full2(1, LANES), full2(1, LANES),
                   full3(H, QL, LANES), full3(H, KVL, LANES), full2(KVL, H * LANES)],
        out_shape=[sd((s_len, QL), MM), sd((s_len, KVL), MM), sd((s_len, LANES), MM), sd((1, QL), F32), sd((1, KVL), F32),
                   sd((1, LANES), F32), sd((1, LANES), F32), sd((H, QL, LANES), F32), sd((H, KVL, LANES), F32),
                   sd((KVL, H * LANES), F32)],
        name="mla_prep_bwd", compiler_params=_cparams(1))(proj, proj, proj, *tabs, cq_g, ckv_g, qg, kg, wuq, wkn, wv, dq, dk, dv)


def _attn_specs(s_len, tq):
    pair = 2 * LANES
    return [_bs((tq, pair), lambda p, i: (i, p)), _bs((s_len, pair), lambda p, i: (0, p)), _bs((s_len, pair), lambda p, i: (0, p)),
            _bs((tq, LANES), lambda p, i: (i, OFF_SG // LANES + p))]


def _attn_call(q, k, v, proj):
    s_len = q.shape[0]
    tq = min(s_len, 256)

    def body(q_ref, k_ref, v_ref, sg_ref, y_ref, o_ref, lse_ref):
        for n in range(s_len // tq):
            @pl.when(pl.program_id(1) == n)
            def _():
                kl = (n + 1) * tq
                o, lse = _attn_pair_fwd(q_ref[...], k_ref[:kl, :], v_ref[:kl, :])
                y_ref[...] = (o * jax.nn.silu(sg_ref[...])).astype(y_ref.dtype)
                o_ref[...] = o
                lse_ref[...] = lse

    tile = _bs((tq, LANES), lambda p, i: (i, p))
    sd = jax.ShapeDtypeStruct
    return pl.pallas_call(
        body, grid=(H // 2, s_len // tq), in_specs=_attn_specs(s_len, tq), out_specs=[tile, tile, tile],
        out_shape=[sd((s_len, BW), MM), sd((s_len, BW), F32), sd((s_len, BW), F32)],
        name="attn", compiler_params=_cparams(2))(q, k, v, proj)


def _attn_bwd_call(q, k, v, proj, dys, o, lse):
    s_len = q.shape[0]
    tq = min(s_len, 256)
    pair = 2 * LANES

    def body(q_ref, k_ref, v_ref, sg_ref, dy_ref, o_ref, lse_ref, dq_ref, dk_ref, dv_ref, dsg_ref):
        i = pl.program_id(1)

        @pl.when(i == 0)
        def _():
            dk_ref[...] = jnp.zeros_like(dk_ref)
            dv_ref[...] = jnp.zeros_like(dv_ref)

        for n in range(s_len // tq):
            @pl.when(i == n)
            def _():
                kl = (n + 1) * tq
                dq, dk, dv, dsg = _attn_pair_bwd(q_ref[...], k_ref[:kl, :], v_ref[:kl, :], sg_ref[...], dy_ref[...],
                                                 o_ref[...], lse_ref[...])
                dq_ref[...] = dq
                dsg_ref[...] = dsg.astype(dsg_ref.dtype)
                dk_ref[:kl, :] += dk
                dv_ref[:kl, :] += dv

    sd = jax.ShapeDtypeStruct
    tile = _bs((tq, LANES), lambda p, i: (i, p))
    return pl.pallas_call(
        body, grid=(H // 2, s_len // tq),
        in_specs=_attn_specs(s_len, tq) + [tile, tile, tile],
        out_specs=[_bs((tq, pair), lambda p, i: (i, p)), _bs((s_len, pair), lambda p, i: (0, p)),
                   _bs((s_len, pair), lambda p, i: (0, p)), tile],
        out_shape=[sd((s_len, H * LANES), F32), sd((s_len, H * LANES), F32), sd((s_len, H * LANES), F32), sd((s_len, BW), MM)],
        name="attn_bwd", compiler_params=_cparams(2))(q, k, v, proj, dys, o, lse)


def _shift_down(a, n):
    r = lax.broadcasted_iota(jnp.int32, a.shape, 0)
    return jnp.where(r >= n, pltpu.roll(a, n, 0), 0.0)


def _shift_up(a, n):
    s_len = a.shape[0]
    r = lax.broadcasted_iota(jnp.int32, a.shape, 0)
    return jnp.where(r < s_len - n, pltpu.roll(a, s_len - n, 0), 0.0)


def _conv_specs(s_len):
    col = lambda off: _bs((s_len, LANES), lambda j: (0, off // LANES + j))
    return [col(OFF_CV), col(OFF_CV + CW), col(OFF_CV + 2 * CW), col(OFF_SG + BW),
            _bs((3, LANES), lambda j: (0, j)), _bs((1, LANES), lambda j: (0, j))]


def _conv_call(proj, cw, cb):
    s_len = proj.shape[0]

    def body(bg_ref, cg_ref, xi_ref, sg_ref, w_ref, b_ref, y_ref):
        z = cg_ref[...] * xi_ref[...]
        y = b_ref[...] + w_ref[0:1, :] * _shift_down(z, 2)
        y = y + w_ref[1:2, :] * _shift_down(z, 1)
        y = y + w_ref[2:3, :] * z
        y_ref[...] = ((bg_ref[...] * y) * jax.nn.silu(sg_ref[...])).astype(y_ref.dtype)

    return pl.pallas_call(
        body, grid=(CW // LANES,), in_specs=_conv_specs(s_len), out_specs=_bs((s_len, LANES), lambda j: (0, j)),
        out_shape=jax.ShapeDtypeStruct((s_len, CW), MM), name="conv", compiler_params=_cparams(1))(proj, proj, proj, proj, cw, cb)


def _conv_bwd_call(proj, cw, cb, dys):
    s_len = proj.shape[0]

    def body(bg_ref, cg_ref, xi_ref, sg_ref, w_ref, b_ref, dys_ref, dbg_ref, dcg_ref, dxi_ref, dsg_ref, dw_ref, db_ref):
        bg, cg, xi, sg = bg_ref[...], cg_ref[...], xi_ref[...], sg_ref[...]
        w0, w1, w2 = w_ref[0:1, :], w_ref[1:2, :], w_ref[2:3, :]
        z = cg * xi
        z1, z2 = _shift_down(z, 1), _shift_down(z, 2)
        y = b_ref[...] + w0 * z2
        y = y + w1 * z1
        y = y + w2 * z
        yb = bg * y
        sig = jax.nn.sigmoid(sg)
        silu = sg * sig
        dys_v = dys_ref[...]
        dsg_ref[...] = (dys_v * yb * (sig * (1.0 + sg * (1.0 - sig)))).astype(dsg_ref.dtype)
        dyb = dys_v * silu
        dbg_ref[...] = (dyb * y).astype(dbg_ref.dtype)
        dy = dyb * bg
        db_ref[...] = jnp.sum(dy, axis=0, keepdims=True)
        dw_ref[0:1, :] = jnp.sum(dy * z2, axis=0, keepdims=True)
        dw_ref[1:2, :] = jnp.sum(dy * z1, axis=0, keepdims=True)
        dw_ref[2:3, :] = jnp.sum(dy * z, axis=0, keepdims=True)
        dz = w2 * dy + w1 * _shift_up(dy, 1) + w0 * _shift_up(dy, 2)
        dcg_ref[...] = (dz * xi).astype(dcg_ref.dtype)
        dxi_ref[...] = (dz * cg).astype(dxi_ref.dtype)

    col = _bs((s_len, LANES), lambda j: (0, j))
    sd = jax.ShapeDtypeStruct
    return pl.pallas_call(
        body, grid=(CW // LANES,), in_specs=_conv_specs(s_len) + [col],
        out_specs=[col, col, col, col, _bs((3, LANES), lambda j: (0, j)), _bs((1, LANES), lambda j: (0, j))],
        out_shape=[sd((s_len, CW), MM)] * 4 + [sd((3, CW), F32), sd((1, CW), F32)],
        name="conv_bwd", compiler_params=_cparams(1))(proj, proj, proj, proj, cw, cb, dys)


def _sg_specs(tm):
    row = lambda off: _bs((tm, SGW), lambda i: (i, off // SGW))
    return [row(OFF_SGI), row(OFF_SGI + SGW), row(OFF_SG + 2 * BW), _bs((1, SGW), lambda i: (0, 0)), _bs((1, SGW), lambda i: (0, 0)),
            _bs((SGG, SGC, SGC), lambda i: (0, 0, 0)), _bs((SGG, SGC, 1), lambda i: (0, 0, 0))]


def _sg_args(refs):
    u, v, sg, lg, lb, ws, bs = refs
    return (u[...], v[...], sg[...], lg[...], lb[...], [ws[g] for g in range(SGG)], [bs[g] for g in range(SGG)])


def _sg_call(proj, ln_g, ln_b, ws, bs):
    s_len = proj.shape[0]
    tm = min(s_len, 256)

    def body(*refs):
        refs[7][...] = _sg_fn(*_sg_args(refs[:7])).astype(refs[7].dtype)

    return pl.pallas_call(
        body, grid=(s_len // tm,), in_specs=_sg_specs(tm), out_specs=_bs((tm, SGW), lambda i: (i, 0)),
        out_shape=jax.ShapeDtypeStruct((s_len, SGW), MM), name="sgmlp", compiler_params=_cparams(1))(proj, proj, proj, ln_g, ln_b, ws, bs)


def _sg_bwd_call(proj, ln_g, ln_b, ws, bs, dys):
    s_len = proj.shape[0]
    tm = min(s_len, 256)

    def body(*refs):
        dys_ref = refs[7]
        du_ref, dv_ref, dsg_ref, dlg_ref, dlb_ref, dws_ref, dbs_ref = refs[8:]
        _, vjp = jax.vjp(_sg_fn, *_sg_args(refs[:7]))
        du, dv, dsg, dlg, dlb, dws, dbs = vjp(dys_ref[...])
        du_ref[...] = du.astype(du_ref.dtype)
        dv_ref[...] = dv.astype(dv_ref.dtype)
        dsg_ref[...] = dsg.astype(dsg_ref.dtype)

        @pl.when(pl.program_id(0) == 0)
        def _():
            for r in (dlg_ref, dlb_ref, dws_ref, dbs_ref):
                r[...] = jnp.zeros_like(r)
        dlg_ref[...] += dlg
        dlb_ref[...] += dlb
        for g in range(SGG):
            dws_ref[g] += dws[g]
            dbs_ref[g] += dbs[g]

    row = _bs((tm, SGW), lambda i: (i, 0))
    sd = jax.ShapeDtypeStruct
    return pl.pallas_call(
        body, grid=(s_len // tm,), in_specs=_sg_specs(tm) + [row],
        out_specs=[row, row, row, _bs((1, SGW), lambda i: (0, 0)), _bs((1, SGW), lambda i: (0, 0)),
                   _bs((SGG, SGC, SGC), lambda i: (0, 0, 0)), _bs((SGG, SGC, 1), lambda i: (0, 0, 0))],
        out_shape=[sd((s_len, SGW), MM)] * 3 + [sd((1, SGW), F32), sd((1, SGW), F32), sd((SGG, SGC, SGC), F32), sd((SGG, SGC, 1), F32)],
        name="sgmlp_bwd", compiler_params=_cparams(1))(proj, proj, proj, ln_g, ln_b, ws, bs, dys)


def _memkv_call(mem, mem_g, wm, kg):
    m_len = mem.shape[0]

    def body(mem_ref, g_ref, w_ref, kg_ref, k_ref, v_ref):
        k, v = _memkv_fn(mem_ref[...], g_ref[...], w_ref[...], kg_ref[...])
        k_ref[...] = k.astype(k_ref.dtype)
        v_ref[...] = v.astype(v_ref.dtype)

    return pl.pallas_call(body, out_shape=[jax.ShapeDtypeStruct((m_len, MH * MHD), MM)] * 2, name="memkv",
                          compiler_params=pltpu.CompilerParams(vmem_limit_bytes=VMEM_LIMIT))(mem, mem_g, wm, kg)


def _memkv_bwd_call(mem, mem_g, wm, kg, dk, dv):
    def body(mem_ref, g_ref, w_ref, kg_ref, dk_ref, dv_ref, dg_ref, dw_ref, dkg_ref):
        _, vjp = jax.vjp(_memkv_fn, mem_ref[...], g_ref[...], _with_slot(w_ref[...]), kg_ref[...])
        _, dg, dw, dkg = vjp((dk_ref[...], dv_ref[...]))
        dg_ref[...] = dg
        dw_ref[...] = dw[1]
        dkg_ref[...] = dkg

    sd = jax.ShapeDtypeStruct
    return pl.pallas_call(body, out_shape=[sd((1, D), F32), sd((D, 2 * MH * MHD), F32), sd((1, MHD), F32)], name="memkv_bwd",
                          compiler_params=pltpu.CompilerParams(vmem_limit_bytes=VMEM_LIMIT))(mem, mem_g, wm, kg, dk, dv)


def _mem_specs(tm, m_len):
    w = MH * MHD
    return [_bs((tm, w), lambda i: (i, OFF_MQ // w)), _bs((tm, BW), lambda i: (i, (OFF_SG + 3 * BW) // BW)),
            _bs((m_len, w), lambda i: (0, 0)), _bs((m_len, w), lambda i: (0, 0)), _bs((1, MHD), lambda i: (0, 0))]


def _mem_call(proj, k, v, qg):
    s_len, m_len = proj.shape[0], k.shape[0]
    tm = min(s_len, 256)

    def body(mq_ref, sg_ref, k_ref, v_ref, qg_ref, y_ref):
        y_ref[...] = _mem_fn(mq_ref[...], sg_ref[...], k_ref[...], v_ref[...], qg_ref[...]).astype(y_ref.dtype)

    return pl.pallas_call(
        body, grid=(s_len // tm,), in_specs=_mem_specs(tm, m_len), out_specs=_bs((tm, BW), lambda i: (i, 0)),
        out_shape=jax.ShapeDtypeStruct((s_len, BW), MM), name="memattn", compiler_params=_cparams(1))(proj, proj, k, v, qg)


def _mem_bwd_call(proj, k, v, qg, dys):
    s_len, m_len = proj.shape[0], k.shape[0]
    tm = min(s_len, 256)
    w = MH * MHD

    def body(mq_ref, sg_ref, k_ref, v_ref, qg_ref, dys_ref, dmq_ref, dsg_ref, dk_ref, dv_ref, dqg_ref):
        _, vjp = jax.vjp(_mem_fn, mq_ref[...], sg_ref[...], k_ref[...].astype(F32), v_ref[...].astype(F32), qg_ref[...])
        dmq, dsg, dk, dv, dqg = vjp(dys_ref[...])
        dmq_ref[...] = dmq.astype(dmq_ref.dtype)
        dsg_ref[...] = dsg.astype(dsg_ref.dtype)

        @pl.when(pl.program_id(0) == 0)
        def _():
            for r in (dk_ref, dv_ref, dqg_ref):
                r[...] = jnp.zeros_like(r)
        dk_ref[...] += dk
        dv_ref[...] += dv
        dqg_ref[...] += dqg

    row = _bs((tm, BW), lambda i: (i, 0))
    kv = _bs((m_len, w), lambda i: (0, 0))
    sd = jax.ShapeDtypeStruct
    return pl.pallas_call(
        body, grid=(s_len // tm,), in_specs=_mem_specs(tm, m_len) + [row],
        out_specs=[row, row, kv, kv, _bs((1, MHD), lambda i: (0, 0))],
        out_shape=[sd((s_len, w), MM), sd((s_len, BW), MM), sd((m_len, w), F32), sd((m_len, w), F32), sd((1, MHD), F32)],
        name="memattn_bwd", compiler_params=_cparams(1))(proj, proj, k, v, qg, dys)


def _merge_specs(tm):
    row = _bs((tm, BW), lambda i: (i, 0))
    return [row, row, row, row, _bs((tm, NB * D), lambda i: (i, OFF_ML // (NB * D))), _bs((NB, D), lambda i: (0, 0)),
            _bs((N_CHIPS, NB, BW, D // N_CHIPS), lambda i: (0, 0, 0, 0)), _bs((D, D), lambda i: (0, 0))]


def _merge_call(ys, proj, bm, wb, wo, x):
    s_len = proj.shape[0]
    tm = min(s_len, 256)

    def body(ya, yb, yc, yd, lg_ref, bm_ref, wb_ref, wo_ref, x_ref, o_ref):
        out = _merge_fn([r[...] for r in (ya, yb, yc, yd)], lg_ref[...], [bm_ref[n:n + 1, :] for n in range(NB)],
                        [[wb_ref[j, n] for n in range(NB)] for j in range(N_CHIPS)], wo_ref[...])
        o_ref[...] = x_ref[...] + out

    xrow = _bs((tm, D), lambda i: (i, 0))
    return pl.pallas_call(
        body, grid=(s_len // tm,), in_specs=_merge_specs(tm) + [xrow], out_specs=xrow,
        out_shape=jax.ShapeDtypeStruct((s_len, D), F32), name="merge", compiler_params=_cparams(1))(*ys, proj, bm, wb, wo, x)


def _merge_bwd_call(ys, proj, bm, wb, wo, dout):
    s_len = proj.shape[0]
    tm = min(s_len, 256)

    def body(ya, yb, yc, yd, lg_ref, bm_ref, wb_ref, wo_ref, do_ref, dya, dyb, dyc, dyd, dlg_ref, dbm_ref, dwb_ref, dwo_ref):
        fn = lambda ys_, lg_, bm_, wb_, wo_: _merge_fn(ys_, lg_, bm_, wb_, wo_)
        _, vjp = jax.vjp(fn, [r[...].astype(F32) for r in (ya, yb, yc, yd)], lg_ref[...], [bm_ref[n:n + 1, :] for n in range(NB)],
                         [[_with_slot(wb_ref[j, n]) for n in range(NB)] for j in range(N_CHIPS)], _with_slot(wo_ref[...]))
        dys, dlg, dbm, dwb, dwo = vjp(do_ref[...])
        dwb, dwo = [[d[1] for d in row] for row in dwb], dwo[1]
        for r, d in zip((dya, dyb, dyc, dyd), dys):
            r[...] = d
        dlg_ref[...] = dlg.astype(dlg_ref.dtype)

        @pl.when(pl.program_id(0) == 0)
        def _():
            for r in (dbm_ref, dwb_ref, dwo_ref):
                r[...] = jnp.zeros_like(r)
        for n in range(NB):
            dbm_ref[n:n + 1, :] += dbm[n]
            for j in range(N_CHIPS):
                dwb_ref[j, n] += dwb[j][n]
        dwo_ref[...] += dwo

    row = _bs((tm, BW), lambda i: (i, 0))
    sd = jax.ShapeDtypeStruct
    wb_shape = (N_CHIPS, NB, BW, D // N_CHIPS)
    return pl.pallas_call(
        body, grid=(s_len // tm,), in_specs=_merge_specs(tm) + [_bs((tm, D), lambda i: (i, 0))],
        out_specs=[row, row, row, row, _bs((tm, NB * D), lambda i: (i, 0)), _bs((NB, D), lambda i: (0, 0)),
                   _bs(wb_shape, lambda i: (0, 0, 0, 0)), _bs((D, D), lambda i: (0, 0))],
        out_shape=[sd((s_len, BW), F32)] * 4 + [sd((s_len, NP), MM), sd((NB, D), F32), sd(wb_shape, F32), sd((D, D), F32)],
        name="merge_bwd", compiler_params=_cparams(1))(*ys, proj, bm, wb, wo, dout)


def _dh_call(dproj, w, x, g, dout, after=()):
    s_len = x.shape[0]
    tk = NP // 4
    after = list(after)

    def matmul_body(dp_ref, w_ref, *rest):
        o_ref = rest[-1]

        @pl.when(pl.program_id(0) == 0)
        def _():
            o_ref[...] = jnp.zeros_like(o_ref)
        o_ref[...] += lax.dot_general(dp_ref[...], w_ref[...], (((1,), (1,)), ((), ())), preferred_element_type=F32)

    dh = pl.pallas_call(
        matmul_body, grid=(NP // tk,),
        in_specs=[_bs((s_len, tk), lambda k: (0, k)), _bs((D, tk), lambda k: (0, k))] + [_ANY] * len(after),
        out_specs=_bs((s_len, D), lambda k: (0, 0)), out_shape=jax.ShapeDtypeStruct((s_len, D), F32),
        name="dh", compiler_params=_cparams(1))(dproj, w, *after)

    tm = min(s_len, 512)

    def norm_body(dh_ref, x_ref, g_ref, do_ref, dx_ref, dg_ref):
        _, vjp = jax.vjp(lambda x_, g_: _rms_n(x_, g_, D), x_ref[...], g_ref[...])
        dxr, dgr = vjp(dh_ref[...])
        dx_ref[...] = do_ref[...] + dxr

        @pl.when(pl.program_id(0) == 0)
        def _():
            dg_ref[...] = jnp.zeros_like(dg_ref)
        dg_ref[...] += dgr

    row = _bs((tm, D), lambda i: (i, 0))
    return pl.pallas_call(
        norm_body, grid=(s_len // tm,), in_specs=[row, row, _bs((1, D), lambda i: (0, 0)), row],
        out_specs=[row, _bs((1, D), lambda i: (0, 0))],
        out_shape=[jax.ShapeDtypeStruct((s_len, D), F32), jax.ShapeDtypeStruct((1, D), F32)],
        name="norm_bwd", compiler_params=_cparams(1))(dh, x, g, dout)


def _dw_call(h, dproj, after=()):
    s_len = h.shape[0]
    tn = 512
    after = list(after)

    def body(h_ref, dp_ref, *rest):
        o_ref, ht_ref = rest[-2], rest[-1]

        @pl.when(pl.program_id(0) == 0)
        def _():
            ht_ref[...] = h_ref[...].T
        o_ref[...] = jnp.dot(ht_ref[...], dp_ref[...], preferred_element_type=F32)

    return pl.pallas_call(
        body, grid=(NP // tn,),
        in_specs=[_bs((s_len, D), lambda j: (0, 0)), _bs((s_len, tn), lambda j: (0, j))] + [_ANY] * len(after),
        out_specs=_bs((D, tn), lambda j: (0, j)), out_shape=jax.ShapeDtypeStruct((D, NP), F32),
        scratch_shapes=[pltpu.VMEM((D, s_len), h.dtype)], name="dw_in", compiler_params=_cparams(1))(h, dproj, *after)


def _loss_call(y, target):
    s_len = y.shape[0]
    tm = min(s_len, 512)

    def body(y_ref, t_ref, dy_ref, l_ref):
        e = y_ref[...] - t_ref[...]
        dy_ref[...] = e * (1.0 / D)

        @pl.when(pl.program_id(0) == 0)
        def _():
            l_ref[...] = jnp.zeros_like(l_ref)
        l_ref[...] += jnp.sum(e * e, axis=0, keepdims=True)

    row = _bs((tm, D), lambda i: (i, 0))
    return pl.pallas_call(
        body, grid=(s_len // tm,), in_specs=[row, row], out_specs=[row, _bs((1, D), lambda i: (0, 0))],
        out_shape=[jax.ShapeDtypeStruct((s_len, D), F32), jax.ShapeDtypeStruct((1, D), F32)],
        name="loss", compiler_params=_cparams(1))(y, target)


def _adamw_call(w, g, m, v, name):
    rows, cols = w.shape
    tr = min(_row_tile(rows), 128)

    def body(w_ref, g_ref, m_ref, v_ref, d_ref, nm_ref, nv_ref):
        gv = g_ref[...]
        m2 = ADAM_B1 * m_ref[...] + (1.0 - ADAM_B1) * gv
        v2 = ADAM_B2 * v_ref[...] + (1.0 - ADAM_B2) * (gv * gv)
        m_hat = m2 / (1.0 - ADAM_B1 ** ADAM_STEP)
        v_hat = v2 / (1.0 - ADAM_B2 ** ADAM_STEP)
        d_ref[...] = -ADAM_LR * (m_hat / (jnp.sqrt(v_hat) + ADAM_EPS) + ADAM_WD * w_ref[...])
        nm_ref[...] = m2
        nv_ref[...] = v2

    blk = _bs((tr, cols), lambda i: (i, 0))
    return pl.pallas_call(
        body, grid=(rows // tr,), in_specs=[blk] * 4, out_specs=[blk] * 3,
        out_shape=[jax.ShapeDtypeStruct((rows, cols), F32)] * 3, name=name, compiler_params=_cparams(1))(w, g, m, v)


def _adamw_layer_call(layer, ws, gs, ms, vs, prev, after, name, steps=8):
    n = len(ws)
    after = list(after)
    n_prev = 4 * n if prev is not None else 0

    def body(*refs):
        outs = refs[len(refs) - 4 * n:]
        for t in range(n):
            w_ref, g_ref, m_ref, v_ref = refs[t], refs[n + t], refs[2 * n + t], refs[3 * n + t]
            g_out, d_out, m_out, v_out = outs[4 * t:4 * t + 4]
            gv = g_ref[...]
            m2 = ADAM_B1 * m_ref[0] + (1.0 - ADAM_B1) * gv
            v2 = ADAM_B2 * v_ref[0] + (1.0 - ADAM_B2) * (gv * gv)
            m_hat = m2 / (1.0 - ADAM_B1 ** ADAM_STEP)
            v_hat = v2 / (1.0 - ADAM_B2 ** ADAM_STEP)
            g_out[0] = gv
            d_out[0] = -ADAM_LR * (m_hat / (jnp.sqrt(v_hat) + ADAM_EPS) + ADAM_WD * w_ref[0])
            m_out[0] = m2
            v_out[0] = v2

    def lay(a):
        return _bs((1, a.shape[1] // steps, a.shape[2]), lambda i: (layer, i, 0))

    in_specs = ([lay(a) for a in ws] + [_bs((g.shape[0] // steps, g.shape[1]), lambda i: (i, 0)) for g in gs]
                + [lay(a) for a in ms] + [lay(a) for a in vs] + [_ANY] * (n_prev + len(after)))
    return pl.pallas_call(
        body, grid=(steps,), in_specs=in_specs, out_specs=[lay(ws[t]) for t in range(n) for _ in range(4)],
        out_shape=[jax.ShapeDtypeStruct(ws[t].shape, F32) for t in range(n) for _ in range(4)],
        input_output_aliases={4 * n + q: q for q in range(n_prev)}, name=name, compiler_params=_cparams(1),
    )(*ws, *gs, *ms, *vs, *(prev if prev is not None else []), *after)


def _row_tile(rows):
    for cand in (512, 256, 128, 64, 32, 16, 8):
        if rows % cand == 0 and rows > cand:
            return cand
    return rows


def _pair_sum_call(grads, from_sibling, core, name):
    n = len(grads)

    def body(core_ref, *refs):
        for t in range(n):
            refs[2 * n + t][...] = (refs[t][...].astype(F32) + refs[n + t][...].astype(F32)).astype(MM)

    half = lambda g: (1, g.shape[1] // 2, g.shape[2])
    grid_spec = pltpu.PrefetchScalarGridSpec(
        num_scalar_prefetch=1, grid=(N_CHIPS,),
        in_specs=[pl.BlockSpec(half(g), lambda j, core_ref: (j, core_ref[0], 0)) for g in grads]
        + [pl.BlockSpec(half(g), lambda j, core_ref: (j, 0, 0)) for g in grads],
        out_specs=[pl.BlockSpec(half(g), lambda j, core_ref: (j, 0, 0)) for g in grads])
    return pl.pallas_call(
        body, grid_spec=grid_spec, out_shape=[jax.ShapeDtypeStruct((N_CHIPS,) + half(g)[1:], MM) for g in grads], name=name,
        compiler_params=_cparams(1))(core, *grads, *from_sibling)


def _owner_sum_call(chip_sums, from_chips, chip_core, name):
    n = len(chip_sums)
    steps = 4

    def body(ids_ref, *refs):
        for t in range(n):
            a, b = refs[t], refs[n + t]
            refs[2 * n + t][...] = ((a[0].astype(F32) + b[0].astype(F32)) + b[1].astype(F32)) + b[2].astype(F32)

    tile = lambda p: (p.shape[1] // steps, p.shape[2])
    grid_spec = pltpu.PrefetchScalarGridSpec(
        num_scalar_prefetch=1, grid=(steps,),
        in_specs=[pl.BlockSpec((1,) + tile(p), lambda i, ids_ref: (ids_ref[0], i, 0)) for p in chip_sums]
        + [pl.BlockSpec((3,) + tile(p), lambda i, ids_ref: (0, i, 0)) for p in chip_sums],
        out_specs=[pl.BlockSpec(tile(p), lambda i, ids_ref: (ids_ref[1] * steps + i, 0)) for p in chip_sums])
    return pl.pallas_call(
        body, grid_spec=grid_spec, out_shape=[jax.ShapeDtypeStruct((2 * p.shape[1], p.shape[2]), F32) for p in chip_sums],
        name=name, compiler_params=_cparams(1))(chip_core, *chip_sums, *from_chips)


def _sum8_call(parts):
    n, rows, cols = parts.shape
    tr = _row_tile(rows)

    def body(p_ref, o_ref):
        acc = p_ref[0]
        for k in range(1, n):
            acc = acc + p_ref[k]
        o_ref[...] = acc

    return pl.pallas_call(
        body, grid=(rows // tr,), in_specs=[_bs((n, tr, cols), lambda i: (0, i, 0))], out_specs=_bs((tr, cols), lambda i: (i, 0)),
        out_shape=jax.ShapeDtypeStruct((rows, cols), F32), name="sum_small_grads", compiler_params=_cparams(1))(parts)


_ANY = pl.BlockSpec(memory_space=pl.ANY)


def _half_rows(ref, lead, half, which):
    rows = pl.ds(pl.multiple_of(half * which, half), half)
    return ref.at[rows] if lead is None else ref.at[lead, rows]


_HBM = pl.BlockSpec(memory_space=pltpu.HBM)
_SEM = pl.BlockSpec(memory_space=pltpu.SEMAPHORE)
_ORDERED_EFFECT = pltpu.CompilerParams(has_side_effects=pltpu.SideEffectType.DATAFLOW_SIDE_EFFECTING)


_VMEM = pl.BlockSpec(memory_space=pltpu.VMEM)
_TOKEN = jax.ShapeDtypeStruct((8, LANES), F32)


def _in_hbm(a):
    return pltpu.with_memory_space_constraint(a, pltpu.HBM)


def _tie(small, token):
    return small + token[0:1, 0:1].reshape((1,) * small.ndim)


def _peer(k):
    x, y, c = lax.axis_index("x"), lax.axis_index("y"), lax.axis_index("c")
    bx, by, bc = (k >> 2) & 1, (k >> 1) & 1, k & 1
    return (x ^ bx if bx else x, y ^ by if by else y, c ^ bc if bc else c)


def _place_block_call(blk, index, name):
    rows, cols = blk.shape

    def body(idx_ref, b_ref, o_ref):
        o_ref[0] = b_ref[...]

    grid_spec = pltpu.PrefetchScalarGridSpec(
        num_scalar_prefetch=1, grid=(1,), in_specs=[pl.BlockSpec((rows, cols), lambda i, idx_ref: (0, 0))],
        out_specs=pl.BlockSpec((1, rows, cols), lambda i, idx_ref: (idx_ref[0], 0, 0)))
    return pl.pallas_call(body, grid_spec=grid_spec, out_shape=jax.ShapeDtypeStruct((8, rows, cols), blk.dtype), name=name,
                          compiler_params=_cparams(1))(index, blk)


def _small_gather_start_call(blk, buf, after, name):
    after = list(after)

    def body(*refs):
        b_ref, out_ref = refs[0], refs[2 + len(after)]
        send_sems, recv_sems, token = refs[3 + len(after):]
        x, y, c = lax.axis_index("x"), lax.axis_index("y"), lax.axis_index("c")
        for k in range(1, 8):
            pltpu.make_async_remote_copy(src_ref=b_ref, dst_ref=out_ref.at[4 * x + 2 * y + c], send_sem=send_sems.at[k - 1],
                                         recv_sem=recv_sems.at[k - 1], device_id=_peer(k), device_id_type=MESH_ID).start()
        token[...] = jnp.zeros_like(token)

    dma = pltpu.SemaphoreType.DMA
    return pl.pallas_call(
        body, out_shape=[pltpu.HBM(buf.shape, buf.dtype), dma((7,)), dma((7,)), _TOKEN],
        in_specs=[_HBM, _HBM] + [_ANY] * len(after), out_specs=[_HBM, _SEM, _SEM, _VMEM],
        input_output_aliases={1: 0}, name=name, compiler_params=_ORDERED_EFFECT)(_in_hbm(blk), _in_hbm(buf), *after)


def _small_gather_finish_call(blk, buf, send_sems, recv_sems, after, name):
    after = list(after)

    def body(*refs):
        b_ref, in_ref, send_ref, recv_ref = refs[:4]
        x, y, c = lax.axis_index("x"), lax.axis_index("y"), lax.axis_index("c")
        for k in range(1, 8):
            px, py, pc = _peer(k)
            pltpu.make_async_remote_copy(src_ref=b_ref, dst_ref=in_ref.at[4 * px + 2 * py + pc], send_sem=send_ref.at[k - 1],
                                         recv_sem=recv_ref.at[k - 1], device_id=(px, py, pc), device_id_type=MESH_ID).wait()

    return pl.pallas_call(
        body, out_shape=pltpu.HBM(buf.shape, buf.dtype), in_specs=[_HBM, _HBM, _SEM, _SEM] + [_ANY] * len(after),
        out_specs=_HBM, input_output_aliases={1: 0}, name=name, compiler_params=_ORDERED_EFFECT,
    )(_in_hbm(blk), buf, send_sems, recv_sems, *after)


def _pair_exchange_start_call(grads, name):
    n = len(grads)
    half = [g.shape[1] // 2 for g in grads]

    def body(*refs):
        srcs, outs = refs[:n], refs[n:2 * n]
        send_sems, recv_sems, token = refs[2 * n:]
        x, y, c = lax.axis_index("x"), lax.axis_index("y"), lax.axis_index("c")
        for t in range(n):
            pltpu.make_async_remote_copy(
                src_ref=srcs[t].at[:, pl.ds(pl.multiple_of(half[t] * (1 - c), half[t]), half[t])], dst_ref=outs[t],
                send_sem=send_sems.at[t], recv_sem=recv_sems.at[t], device_id=(x, y, 1 - c), device_id_type=MESH_ID).start()
        token[...] = jnp.zeros_like(token)

    dma = pltpu.SemaphoreType.DMA
    return pl.pallas_call(
        body, out_shape=[pltpu.HBM((g.shape[0], g.shape[1] // 2, g.shape[2]), g.dtype) for g in grads] + [dma((n,)), dma((n,)), _TOKEN],
        in_specs=[_HBM] * n, out_specs=[_HBM] * n + [_SEM, _SEM, _VMEM], name=name, compiler_params=_ORDERED_EFFECT,
    )(*[_in_hbm(g) for g in grads])


def _pair_exchange_finish_call(grads, bufs, send_sems, recv_sems, after, name):
    n = len(grads)
    after = list(after)
    half = [g.shape[1] // 2 for g in grads]

    def body(*refs):
        srcs, ins, send_ref, recv_ref = refs[:n], refs[n:2 * n], refs[2 * n], refs[2 * n + 1]
        x, y, c = lax.axis_index("x"), lax.axis_index("y"), lax.axis_index("c")
        for t in range(n):
            pltpu.make_async_remote_copy(
                src_ref=srcs[t].at[:, pl.ds(pl.multiple_of(half[t] * (1 - c), half[t]), half[t])], dst_ref=ins[t],
                send_sem=send_ref.at[t], recv_sem=recv_ref.at[t], device_id=(x, y, 1 - c), device_id_type=MESH_ID).wait()

    return pl.pallas_call(
        body, out_shape=[pltpu.HBM(b.shape, b.dtype) for b in bufs],
        in_specs=[_HBM] * (2 * n) + [_SEM, _SEM] + [_ANY] * len(after), out_specs=[_HBM] * n,
        input_output_aliases={n + t: t for t in range(n)}, name=name, compiler_params=_ORDERED_EFFECT,
    )(*[_in_hbm(g) for g in grads], *bufs, send_sems, recv_sems, *after)


def _chip_scatter_start_call(chip_sums, name):
    n = len(chip_sums)

    def body(*refs):
        srcs, outs = refs[:n], refs[n:2 * n]
        send_sems, recv_sems, token = refs[2 * n:]
        x, y, c = lax.axis_index("x"), lax.axis_index("y"), lax.axis_index("c")
        chips = [(1 - x, y), (x, 1 - y), (1 - x, 1 - y)]
        for k, (cx, cy) in enumerate(chips):
            for t in range(n):
                pltpu.make_async_remote_copy(
                    src_ref=srcs[t].at[2 * cx + cy], dst_ref=outs[t].at[k], send_sem=send_sems.at[3 * t + k],
                    recv_sem=recv_sems.at[3 * t + k], device_id=(cx, cy, c), device_id_type=MESH_ID).start()
        token[...] = jnp.zeros_like(token)

    dma = pltpu.SemaphoreType.DMA
    return pl.pallas_call(
        body, out_shape=[pltpu.HBM((3,) + p.shape[1:], p.dtype) for p in chip_sums] + [dma((3 * n,)), dma((3 * n,)), _TOKEN],
        in_specs=[_HBM] * n, out_specs=[_HBM] * n + [_SEM, _SEM, _VMEM], name=name, compiler_params=_ORDERED_EFFECT,
    )(*[_in_hbm(p) for p in chip_sums])


def _chip_scatter_finish_call(chip_sums, bufs, send_sems, recv_sems, after, name):
    n = len(chip_sums)
    after = list(after)

    def body(*refs):
        srcs, ins, send_ref, recv_ref = refs[:n], refs[n:2 * n], refs[2 * n], refs[2 * n + 1]
        x, y, c = lax.axis_index("x"), lax.axis_index("y"), lax.axis_index("c")
        chips = [(1 - x, y), (x, 1 - y), (1 - x, 1 - y)]
        for k, (cx, cy) in enumerate(chips):
            for t in range(n):
                pltpu.make_async_remote_copy(
                    src_ref=srcs[t].at[2 * cx + cy], dst_ref=ins[t].at[k], send_sem=send_ref.at[3 * t + k],
                    recv_sem=recv_ref.at[3 * t + k], device_id=(cx, cy, c), device_id_type=MESH_ID).wait()

    return pl.pallas_call(
        body, out_shape=[pltpu.HBM(b.shape, b.dtype) for b in bufs],
        in_specs=[_HBM] * (2 * n) + [_SEM, _SEM] + [_ANY] * len(after), out_specs=[_HBM] * n,
        input_output_aliases={n + t: t for t in range(n)}, name=name, compiler_params=_ORDERED_EFFECT,
    )(*[_in_hbm(p) for p in chip_sums], *bufs, send_sems, recv_sems, *after)


def _place_own_call(mine, chip_core, name):
    n = len(mine)

    def body(ids_ref, *refs):
        for t in range(n):
            refs[n + t][0] = refs[t][...]

    def imap_out(s):
        pad = (0,) * (s.ndim - 1)
        return lambda i, ids_ref: (ids_ref[0], ids_ref[1]) + pad

    grid_spec = pltpu.PrefetchScalarGridSpec(
        num_scalar_prefetch=1, grid=(1,), in_specs=[pl.BlockSpec(s.shape, lambda i, ids_ref, k=s.ndim: (0,) * k) for s in mine],
        out_specs=[pl.BlockSpec((1,) + s.shape, imap_out(s)) for s in mine])
    return pl.pallas_call(
        body, grid_spec=grid_spec,
        out_shape=[jax.ShapeDtypeStruct((N_CHIPS, 2 * s.shape[0]) + s.shape[1:], s.dtype) for s in mine],
        name=name, compiler_params=_cparams(1))(chip_core, *mine)


def _gather_start_call(mine, bufs, after, name):
    n = len(mine)
    half = [s.shape[0] for s in mine]

    def body(*refs):
        srcs, outs = refs[:n], refs[2 * n + 1:3 * n + 1]
        send_sems, recv_sib, recv_ici, token = refs[3 * n + 1:]
        x, y, c = lax.axis_index("x"), lax.axis_index("y"), lax.axis_index("c")
        chips = [(1 - x, y), (x, 1 - y), (1 - x, 1 - y)]
        for t in range(n):
            dst = _half_rows(outs[t], 2 * x + y, half[t], c)
            pltpu.make_async_remote_copy(src_ref=srcs[t], dst_ref=dst, send_sem=send_sems.at[4 * t], recv_sem=recv_sib.at[t],
                                         device_id=(x, y, 1 - c), device_id_type=MESH_ID).start()
            for j, chip in enumerate(chips):
                pltpu.make_async_remote_copy(src_ref=srcs[t], dst_ref=dst, send_sem=send_sems.at[4 * t + 1 + j],
                                             recv_sem=recv_ici.at[3 * t + j], device_id=(*chip, c), device_id_type=MESH_ID).start()
        token[...] = jnp.zeros_like(token)

    dma = pltpu.SemaphoreType.DMA
    return pl.pallas_call(
        body, out_shape=[pltpu.HBM(b.shape, b.dtype) for b in bufs] + [dma((4 * n,)), dma((n,)), dma((3 * n,)), _TOKEN],
        in_specs=[_HBM] * (2 * n) + [_ANY], out_specs=[_HBM] * n + [_SEM] * 3 + [_VMEM],
        input_output_aliases={n + t: t for t in range(n)}, name=name, compiler_params=_ORDERED_EFFECT,
    )(*[_in_hbm(s) for s in mine], *[_in_hbm(b) for b in bufs], after)


def _gather_forward_call(bufs, recv_ici, after, name):
    n = len(bufs)
    half = [b.shape[1] // 2 for b in bufs]

    def body(*refs):
        ins, recv_ici_ref = refs[:n], refs[n]
        outs = refs[n + 2:2 * n + 2]
        send_fwd, recv_fwd, token = refs[2 * n + 2:]
        x, y, c = lax.axis_index("x"), lax.axis_index("y"), lax.axis_index("c")
        chips = [(1 - x, y), (x, 1 - y), (1 - x, 1 - y)]
        for j, (cx, cy) in enumerate(chips):
            for t in range(n):
                landed = _half_rows(ins[t], 2 * cx + cy, half[t], c)
                dst = _half_rows(outs[t], 2 * cx + cy, half[t], c)
                pltpu.make_async_remote_copy(src_ref=landed, dst_ref=landed, send_sem=send_fwd.at[3 * t + j],
                                             recv_sem=recv_ici_ref.at[3 * t + j], device_id=(cx, cy, c),
                                             device_id_type=MESH_ID).wait_recv()
                pltpu.make_async_remote_copy(src_ref=landed, dst_ref=dst, send_sem=send_fwd.at[3 * t + j],
                                             recv_sem=recv_fwd.at[3 * t + j], device_id=(x, y, 1 - c),
                                             device_id_type=MESH_ID).start()
        token[...] = jnp.zeros_like(token)

    dma = pltpu.SemaphoreType.DMA
    return pl.pallas_call(
        body, out_shape=[pltpu.HBM(b.shape, b.dtype) for b in bufs] + [dma((3 * n,)), dma((3 * n,)), _TOKEN],
        in_specs=[_HBM] * n + [_SEM, _ANY], out_specs=[_HBM] * n + [_SEM] * 2 + [_VMEM],
        input_output_aliases={t: t for t in range(n)}, name=name, compiler_params=_ORDERED_EFFECT,
    )(*bufs, recv_ici, after)


def _gather_finish_call(shards, bufs, send_sems, recv_sib, send_fwd, recv_fwd, after, name):
    n = len(bufs)
    half = [b.shape[1] // 2 for b in bufs]

    def body(*refs):
        srcs, ins = refs[:n], refs[n:2 * n]
        send_ref, recv_sib_ref, send_fwd_ref, recv_fwd_ref = refs[2 * n:2 * n + 4]
        x, y, c = lax.axis_index("x"), lax.axis_index("y"), lax.axis_index("c")
        chips = [(1 - x, y), (x, 1 - y), (1 - x, 1 - y)]
        sibling = (x, y, 1 - c)
        for t in range(n):
            for k in range(4):
                pltpu.make_async_remote_copy(src_ref=srcs[t], dst_ref=srcs[t], send_sem=send_ref.at[4 * t + k],
                                             recv_sem=recv_sib_ref.at[t], device_id=sibling, device_id_type=MESH_ID).wait_send()
            from_sibling = _half_rows(ins[t], 2 * x + y, half[t], 1 - c)
            pltpu.make_async_remote_copy(src_ref=from_sibling, dst_ref=from_sibling, send_sem=send_ref.at[4 * t],
                                         recv_sem=recv_sib_ref.at[t], device_id=sibling, device_id_type=MESH_ID).wait_recv()
            for j, (cx, cy) in enumerate(chips):
                sent = _half_rows(ins[t], 2 * cx + cy, half[t], c)
                passed = _half_rows(ins[t], 2 * cx + cy, half[t], 1 - c)
                pltpu.make_async_remote_copy(src_ref=sent, dst_ref=passed, send_sem=send_fwd_ref.at[3 * t + j],
                                             recv_sem=recv_fwd_ref.at[3 * t + j], device_id=sibling, device_id_type=MESH_ID).wait()

    return pl.pallas_call(
        body, out_shape=[pltpu.HBM(b.shape, b.dtype) for b in bufs],
        in_specs=[_HBM] * (2 * n) + [_SEM] * 4 + [_ANY], out_specs=[_HBM] * n,
        input_output_aliases={n + t: t for t in range(n)}, name=name, compiler_params=_ORDERED_EFFECT,
    )(*[_in_hbm(s) for s in shards], *bufs, send_sems, recv_sib, send_fwd, recv_fwd, after)


def _pair_gather_call(bufs, name):
    n = len(bufs)
    half = [b.shape[0] // 2 for b in bufs]

    def body(*refs):
        srcs, outs, send_sems, recv_sems = refs[:n], refs[n:2 * n], refs[2 * n], refs[2 * n + 1]
        x, y, c = lax.axis_index("x"), lax.axis_index("y"), lax.axis_index("c")
        for t in range(n):
            pltpu.make_async_remote_copy(
                src_ref=_half_rows(srcs[t], None, half[t], c), dst_ref=_half_rows(outs[t], None, half[t], c),
                send_sem=send_sems.at[t], recv_sem=recv_sems.at[t], device_id=(x, y, 1 - c), device_id_type=MESH_ID).start()
        for t in range(n):
            pltpu.make_async_remote_copy(
                src_ref=_half_rows(srcs[t], None, half[t], c), dst_ref=_half_rows(outs[t], None, half[t], 1 - c),
                send_sem=send_sems.at[t], recv_sem=recv_sems.at[t], device_id=(x, y, 1 - c), device_id_type=MESH_ID).wait()

    return pl.pallas_call(
        body, out_shape=[jax.ShapeDtypeStruct(b.shape, b.dtype) for b in bufs], in_specs=[_ANY] * n, out_specs=[_ANY] * n,
        input_output_aliases={t: t for t in range(n)},
        scratch_shapes=[pltpu.SemaphoreType.DMA((n,)), pltpu.SemaphoreType.DMA((n,))], name=name)(*bufs)


def _pack_rows(flats, dtype, row_multiple):
    flat = jnp.concatenate([f.reshape(-1).astype(dtype) for f in flats])
    n = flat.shape[0]
    rows = -(-n // PACK_W)
    rows = -(-rows // row_multiple) * row_multiple
    return jnp.pad(flat, (0, rows * PACK_W - n)).reshape(rows, PACK_W)


def _unpack(flat, shapes):
    out, off = [], 0
    for shp in shapes:
        n = math.prod(shp)
        out.append(flat[off:off + n].reshape(shp))
        off += n
    return out


_W_IN_SEGMENTS = ((R_ML, R_END, OFF_ML), (R_SG, R_ML, OFF_SG), (R_CV, R_SGI, OFF_CV), (R_SGI, R_MQ, OFF_SGI), (R_MQ, R_SG, OFF_MQ),
                  (R_CQ, R_CKV, OFF_CQ), (R_CKV, R_KR, OFF_CKV), (R_KR, R_CV, OFF_KR + NOPE))
W_IN_SHARD = R_END // N_CHIPS


def _realign_call(wg):
    tr = 128

    def body(w_ref, o_ref):
        pieces, pos = [], 0
        for r0, r1, a0 in _W_IN_SEGMENTS:
            if a0 > pos:
                pieces.append(jnp.zeros((tr, a0 - pos), o_ref.dtype))
            while r0 < r1:
                j = r0 // W_IN_SHARD
                hi = min(r1, (j + 1) * W_IN_SHARD)
                pieces.append(w_ref[j, :, r0 - j * W_IN_SHARD:hi - j * W_IN_SHARD])
                a0, r0 = a0 + hi - r0, hi
            pos = a0
        pieces.append(jnp.zeros((tr, NP - pos), o_ref.dtype))
        o_ref[...] = jnp.concatenate(pieces, axis=1)

    return pl.pallas_call(
        body, grid=(D // tr,), in_specs=[_bs((N_CHIPS, tr, W_IN_SHARD), lambda i: (0, i, 0))],
        out_specs=_bs((tr, NP), lambda i: (i, 0)), out_shape=jax.ShapeDtypeStruct((D, NP), wg.dtype),
        name="w_in_realign", compiler_params=_cparams(1))(wg)


def _unalign_call(dw, out_dtype):
    tr = 128
    by_ref = sorted(_W_IN_SEGMENTS)

    def body(dw_ref, o_ref):
        for j in range(N_CHIPS):
            lo_j, hi_j = j * W_IN_SHARD, (j + 1) * W_IN_SHARD
            pieces = []
            for r0, r1, a0 in by_ref:
                lo, hi = max(r0, lo_j), min(r1, hi_j)
                if lo < hi:
                    pieces.append(dw_ref[:, a0 + lo - r0:a0 + hi - r0])
            o_ref[j] = jnp.concatenate(pieces, axis=1).astype(o_ref.dtype)

    return pl.pallas_call(
        body, grid=(D // tr,), in_specs=[_bs((tr, NP), lambda i: (i, 0))],
        out_specs=_bs((N_CHIPS, tr, W_IN_SHARD), lambda i: (0, i, 0)),
        out_shape=jax.ShapeDtypeStruct((N_CHIPS, D, W_IN_SHARD), out_dtype), name="w_in_unalign", compiler_params=_cparams(1))(dw)


def _wuq_to_heads(w):
    w3 = w.reshape(QL, H, QKH)
    w3 = jnp.pad(w3, ((0, 0), (0, 0), (0, LANES - QKH)))
    return jnp.transpose(w3, (1, 0, 2))


def _wuq_from_heads(wh):
    return jnp.transpose(wh[:, :, :QKH], (1, 0, 2)).reshape(QL, H * QKH)


def _wukv_to_heads(w):
    w3 = w.reshape(KVL, H, NOPE + VH)
    wkn = jnp.transpose(jnp.pad(w3[:, :, :NOPE], ((0, 0), (0, 0), (0, LANES - NOPE))), (1, 0, 2))
    wv3 = w3[:, :, NOPE:]
    z = jnp.zeros((KVL, VH), w.dtype)
    cols = []
    for h in range(H):
        cols += [wv3[:, h], z] if h % 2 == 0 else [z, wv3[:, h]]
    return wkn, jnp.concatenate(cols, axis=1)


def _wukv_from_heads(wkn, wv):
    kn = jnp.transpose(wkn[:, :, :NOPE], (1, 0, 2))
    vs = jnp.stack([wv[:, LANES * h + VH * (h % 2):LANES * h + VH * (h % 2) + VH] for h in range(H)], axis=1)
    return jnp.concatenate([kn, vs], axis=2).reshape(KVL, H * (NOPE + VH))


def _layer_fwd(x, mem, tabs, p):
    proj, h = _proj_call(x, p["norm_g"], p["w_in"])
    if p.get("late") is not None:
        p = dict(p, **p["late"](proj))
    q, k, v = _mla_prep_call(proj, tabs, p["cq_g"], p["ckv_g"], p["qg"], p["kg"], p["wuq"], p["wkn"], p["wv"])
    ya, attn_o, attn_lse = _attn_call(q, k, v, proj)
    bm = p["bm"]
    if p.get("after_attn") is not None:
        bm = _tie(bm, p["after_attn"](ya))
    yb = _conv_call(proj, p["conv_w"], p["conv_b"])
    yc = _sg_call(proj, p["ln_g"], p["ln_b"], p["ws"], p["bs"])
    mk, mv = _memkv_call(mem, p["mem_g"], p["wm"], p["mkg"])
    yd = _mem_call(proj, mk, mv, p["mqg"])
    out = _merge_call((ya, yb, yc, yd), proj, bm, p["wb"], p["wo"], x)
    return out, dict(p=p, x=x, proj=proj, h=h, q=q, k=k, v=v, attn_o=attn_o, attn_lse=attn_lse, ys=(ya, yb, yc, yd), mk=mk, mv=mv)


def _layer_bwd(dout, mem, tabs, p, sv, start_after=None, on_rest_grads=None, on_grads=None):
    proj = sv["proj"]
    bm = p["bm"] if start_after is None else _tie(p["bm"], start_after)
    dya, dyb, dyc, dyd, dml, dbm, dwb, dwo = _merge_bwd_call(sv["ys"], proj, bm, p["wb"], p["wo"], dout)
    dq, dk, dv, dsg_a = _attn_bwd_call(sv["q"], sv["k"], sv["v"], proj, dya, sv["attn_o"], sv["attn_lse"])
    dcq, dckv, dkr, dcqg, dckvg, dqg, dkg, dwuq, dwkn, dwv = _mla_prep_bwd_call(
        proj, tabs, p["cq_g"], p["ckv_g"], p["qg"], p["kg"], p["wuq"], p["wkn"], p["wv"], dq, dk, dv)
    dbg, dcg, dxi, dsg_b, dcw, dcb = _conv_bwd_call(proj, p["conv_w"], p["conv_b"], dyb)
    du, dvv, dsg_c, dlg, dlb, dws, dbs = _sg_bwd_call(proj, p["ln_g"], p["ln_b"], p["ws"], p["bs"], dyc)
    dmq, dsg_d, dmk, dmv, dmqg = _mem_bwd_call(proj, sv["mk"], sv["mv"], p["mqg"], dyd)
    dmem_g, dwm, dmkg = _memkv_bwd_call(mem, p["mem_g"], p["wm"], p["mkg"], dmk, dmv)
    grads = dict(cq_norm_g=dcqg[0], ckv_norm_g=dckvg[0], mla_q_norm_g=dqg[0, :QKH], mla_k_norm_g=dkg[0, :QKH],
                 conv_w=dcw, conv_b=dcb[0], sg_ln_g=dlg[0], sg_ln_b=dlb[0], w_spatial=dws, b_spatial=dbs[:, :, 0],
                 mem_norm_g=dmem_g[0], mem_q_norm_g=dmqg[0], mem_k_norm_g=dmkg[0], b_merge=dbm,
                 wuq_heads=dwuq, wkn_heads=dwkn, wv_heads=dwv, w_mem_kv=dwm, w_branch_chips=dwb, w_out=dwo)
    started = [on_rest_grads(grads)] if on_rest_grads is not None else []
    dproj, off = dml, NB * D
    for piece in (dsg_a, dsg_b, dsg_c, dsg_d, dbg, dcg, dxi, du, dvv, dmq, dcq, dckv, dkr):
        dproj = lax.dynamic_update_slice(dproj, piece, (0, off))
        off += piece.shape[1]
    grads["w_in_aligned"] = _dw_call(sv["h"], dproj, started)
    tokens = on_grads(grads) if on_grads is not None else ()
    dx, dnorm_g = _dh_call(dproj, p["w_in"], sv["x"], p["norm_g"], dout, tokens)
    grads["norm_g"] = dnorm_g[0]
    return dx, grads


def _chips_to_cols(a):
    return jnp.concatenate([a[j] for j in range(N_CHIPS)], axis=1)


def _cols_to_chips(a):
    cols = a.shape[1] // N_CHIPS
    return jnp.stack([a[:, cols * j:cols * (j + 1)] for j in range(N_CHIPS)])


def _layer_params_first(l, rep, w_in_gathered, conv_w, b_merge):
    pad_g = lambda g: jnp.pad(g, (0, LANES - QKH)).reshape(1, LANES)
    return dict(
        norm_g=rep["norm_g"][l].reshape(1, D), w_in=_realign_call(w_in_gathered),
        cq_g=rep["cq_norm_g"][l].reshape(1, QL), ckv_g=rep["ckv_norm_g"][l].reshape(1, KVL),
        qg=pad_g(rep["mla_q_norm_g"][l]), kg=pad_g(rep["mla_k_norm_g"][l]),
        conv_w=conv_w, conv_b=rep["conv_b"][l].reshape(1, CW),
        ln_g=rep["sg_ln_g"][l].reshape(1, SGW), ln_b=rep["sg_ln_b"][l].reshape(1, SGW),
        ws=rep["w_spatial"][l], bs=rep["b_spatial"][l].reshape(SGG, SGC, 1),
        mem_g=rep["mem_norm_g"][l].reshape(1, D),
        mqg=rep["mem_q_norm_g"][l].reshape(1, MHD), mkg=rep["mem_k_norm_g"][l].reshape(1, MHD), bm=b_merge)


def _layer_params_rest(gathered):
    wkn, wv = _wukv_to_heads(_chips_to_cols(gathered["w_ukv"]))
    return dict(wuq=_wuq_to_heads(_chips_to_cols(gathered["w_uq"])), wkn=wkn, wv=wv,
                wm=gathered["w_mem_kv"].reshape(D, 2 * MH * MHD), wb=gathered["w_branch"], wo=gathered["w_out"].reshape(D, D))


def _layer_params(l, rep, gathered, conv_w, b_merge):
    return dict(_layer_params_first(l, rep, gathered["w_in"], conv_w, b_merge), **_layer_params_rest(gathered))


def _forward_backward(x, mem, pos, target, params, bwd_hooks=None):
    tabs = _rope_tables(pos)
    params = list(params)
    saved = []
    act = x
    for l in range(DEPTH):
        if callable(params[l]):
            params[l] = params[l](saved[-1], act)
        act, sv = _layer_fwd(act, mem, tabs, params[l])
        saved.append(sv)
    dy, sq = _loss_call(act, target)
    grads = [None] * DEPTH
    token = None
    for l in reversed(range(DEPTH)):
        hooks = dict(bwd_hooks[l]) if bwd_hooks else {}
        after_layer = hooks.pop("after_layer", None)
        dy, grads[l] = _layer_bwd(dy, mem, tabs, saved[l]["p"], saved[l], start_after=token, **hooks)
        token = after_layer(dy) if after_layer is not None else None
    return sq, dy, grads


_SHARDED_MM = ("w_in", "w_branch", "w_out", "w_mem_kv", "w_uq", "w_ukv")
_SHARDED_F32 = ("conv_w", "b_merge")
_REPLICATED = ("norm_g", "cq_norm_g", "ckv_norm_g", "mla_q_norm_g", "mla_k_norm_g", "conv_b", "sg_ln_g", "sg_ln_b",
               "w_spatial", "b_spatial", "mem_norm_g", "mem_q_norm_g", "mem_k_norm_g")
_ALL_REDUCED = _REPLICATED + _SHARDED_F32
_WEIGHTS = ("norm_g", "w_in", "cq_norm_g", "ckv_norm_g", "w_uq", "w_ukv", "mla_q_norm_g", "mla_k_norm_g", "conv_w", "conv_b",
            "sg_ln_g", "sg_ln_b", "w_spatial", "b_spatial", "mem_norm_g", "w_mem_kv", "mem_q_norm_g", "mem_k_norm_g",
            "b_merge", "w_branch", "w_out")
_SMALL = tuple(n for n in _WEIGHTS if n not in _SHARDED_MM)


class _SmallGather:
    def __init__(self, blk, after, tag):
        self.blk, self.tag = blk, tag
        x, y, c = lax.axis_index("x"), lax.axis_index("y"), lax.axis_index("c")
        own = _place_block_call(blk, (4 * x + 2 * y + c).astype(jnp.int32).reshape(1), tag + "place_own")
        self.buf, self.send, self.recv, self.token = _small_gather_start_call(blk, own, after, tag + "start")

    def finish(self, after):
        return _small_gather_finish_call(self.blk, self.buf, self.send, self.recv, after, self.tag + "finish")


def _small_sharded_weights(w, got):
    names = _SHARDED_F32
    per_chip = [_unpack(got[2 * j].reshape(-1), [w[n].shape for n in names]) for j in range(N_CHIPS)]
    return {n: jnp.concatenate([per_chip[j][t] for j in range(N_CHIPS)], axis=2) for t, n in enumerate(names)}


class _Gather:
    def __init__(self, w, layer, names, after, tag):
        self.names, self.tag = names, tag
        x, y, c = lax.axis_index("x"), lax.axis_index("y"), lax.axis_index("c")
        chip_core = jnp.stack([2 * x + y, c]).astype(jnp.int32)
        halves = [w[n].shape[1] // 2 for n in names]
        self.srcs = [lax.dynamic_slice_in_dim(w[n][layer], c * h, h, axis=0).astype(MM) for n, h in zip(names, halves)]
        k = len(names)
        out = _gather_start_call(self.srcs, _place_own_call(self.srcs, chip_core, tag + "place_own"), after, tag + "start")
        self.bufs, self.send, self.recv_sib, self.recv_ici, self.token = out[:k], out[k], out[k + 1], out[k + 2], out[k + 3]

    def pass_on(self, after):
        k = len(self.names)
        out = _gather_forward_call(self.bufs, self.recv_ici, after, self.tag + "forward")
        self.bufs, self.send_fwd, self.recv_fwd = out[:k], out[k], out[k + 1]
        return out[k + 2]

    def finish(self, after):
        got = _gather_finish_call(self.srcs, self.bufs, self.send, self.recv_sib, self.send_fwd, self.recv_fwd, after,
                                  self.tag + "finish")
        return dict(zip(self.names, got))


class _ReduceScatter:
    SLABS = dict(
        w_in=lambda g: _unalign_call(g["w_in_aligned"], MM),
        w_branch=lambda g: g["w_branch_chips"].reshape(N_CHIPS, NB * BW, D // N_CHIPS),
        w_out=lambda g: g["w_out"].reshape(N_CHIPS, D // N_CHIPS, D),
        w_mem_kv=lambda g: g["w_mem_kv"].reshape(N_CHIPS, D // N_CHIPS, 2 * MH * MHD),
        w_uq=lambda g: _cols_to_chips(_wuq_from_heads(g["wuq_heads"])),
        w_ukv=lambda g: _cols_to_chips(_wukv_from_heads(g["wkn_heads"], g["wv_heads"])))

    def __init__(self, tag, names):
        self.tag, self.names = tag, names

    def exchange(self, grads):
        self.tensors = [self.SLABS[n](grads) for n in self.names]
        n = len(self.tensors)
        out = _pair_exchange_start_call(self.tensors, self.tag + "exchange_start")
        self.ex_bufs, self.ex_send, self.ex_recv = out[:n], out[n], out[n + 1]
        return out[n + 2]

    def scatter(self, after):
        n = len(self.tensors)
        c = lax.axis_index("c")
        from_sibling = _pair_exchange_finish_call(self.tensors, self.ex_bufs, self.ex_send, self.ex_recv, after,
                                                  self.tag + "exchange_finish")
        self.chip_sums = _pair_sum_call(self.tensors, from_sibling, c.astype(jnp.int32).reshape(1), self.tag + "pair_sum")
        out = _chip_scatter_start_call(self.chip_sums, self.tag + "scatter_start")
        self.bufs, self.send_sems, self.recv_sems, self.token = out[:n], out[n], out[n + 1], out[n + 2]
        return self.token

    def finish(self, after):
        x, y, c = lax.axis_index("x"), lax.axis_index("y"), lax.axis_index("c")
        chip_core = jnp.stack([2 * x + y, c]).astype(jnp.int32)
        from_chips = _chip_scatter_finish_call(self.chip_sums, self.bufs, self.send_sems, self.recv_sems, after,
                                               self.tag + "scatter_finish")
        mine = _owner_sum_call(self.chip_sums, from_chips, chip_core, self.tag + "owner_sum")
        return dict(zip(self.names, _pair_gather_call(mine, self.tag + "pair_gather")))


def _small_sums(g, sq, got):
    total = _sum8_call(got).reshape(-1)
    parts = _unpack(total, [g[n].shape for n in _ALL_REDUCED] + [sq.shape])
    out = dict(zip(_ALL_REDUCED, parts))
    sq_total = parts[-1]
    chip = 2 * lax.axis_index("x") + lax.axis_index("y")
    for n in _SHARDED_F32:
        size = out[n].shape[2] // N_CHIPS
        out[n] = lax.dynamic_slice_in_dim(out[n], chip * size, size, axis=2)
    return out, sq_total


def _adamw_small(w, g, m, v):
    delta, new_m, new_v = {}, {}, {}
    shapes = [w[n].shape for n in _SMALL]
    pk = lambda t: _pack_rows([t[n] for n in _SMALL], F32, 64)
    d, nm, nv = _adamw_call(pk(w), pk(g), pk(m), pk(v), "adamw_small")
    for out, packed in ((delta, d), (new_m, nm), (new_v, nv)):
        out.update(zip(_SMALL, _unpack(packed.reshape(-1), shapes)))
    return delta, new_m, new_v


def kernel(x, mem, positions, norm_g, w_in, cq_norm_g, ckv_norm_g, w_uq, w_ukv, mla_q_norm_g, mla_k_norm_g, conv_w, conv_b, sg_ln_g, sg_ln_b, w_spatial, b_spatial, mem_norm_g, w_mem_kv, mem_q_norm_g, mem_k_norm_g, b_merge, w_branch, w_out, loss_target, m_norm_g, m_w_in, m_cq_norm_g, m_ckv_norm_g, m_w_uq, m_w_ukv, m_mla_q_norm_g, m_mla_k_norm_g, m_conv_w, m_conv_b, m_sg_ln_g, m_sg_ln_b, m_w_spatial, m_b_spatial, m_mem_norm_g, m_w_mem_kv, m_mem_q_norm_g, m_mem_k_norm_g, m_b_merge, m_w_branch, m_w_out, v_norm_g, v_w_in, v_cq_norm_g, v_ckv_norm_g, v_w_uq, v_w_ukv, v_mla_q_norm_g, v_mla_k_norm_g, v_conv_w, v_conv_b, v_sg_ln_g, v_sg_ln_b, v_w_spatial, v_b_spatial, v_mem_norm_g, v_w_mem_kv, v_mem_q_norm_g, v_mem_k_norm_g, v_b_merge, v_w_branch, v_w_out):
    w = dict(norm_g=norm_g, w_in=w_in, cq_norm_g=cq_norm_g, ckv_norm_g=ckv_norm_g, w_uq=w_uq, w_ukv=w_ukv,
             mla_q_norm_g=mla_q_norm_g, mla_k_norm_g=mla_k_norm_g, conv_w=conv_w, conv_b=conv_b, sg_ln_g=sg_ln_g,
             sg_ln_b=sg_ln_b, w_spatial=w_spatial, b_spatial=b_spatial, mem_norm_g=mem_norm_g, w_mem_kv=w_mem_kv,
             mem_q_norm_g=mem_q_norm_g, mem_k_norm_g=mem_k_norm_g, b_merge=b_merge, w_branch=w_branch, w_out=w_out)
    m = dict(norm_g=m_norm_g, w_in=m_w_in, cq_norm_g=m_cq_norm_g, ckv_norm_g=m_ckv_norm_g, w_uq=m_w_uq, w_ukv=m_w_ukv,
             mla_q_norm_g=m_mla_q_norm_g, mla_k_norm_g=m_mla_k_norm_g, conv_w=m_conv_w, conv_b=m_conv_b, sg_ln_g=m_sg_ln_g,
             sg_ln_b=m_sg_ln_b, w_spatial=m_w_spatial, b_spatial=m_b_spatial, mem_norm_g=m_mem_norm_g, w_mem_kv=m_w_mem_kv,
             mem_q_norm_g=m_mem_q_norm_g, mem_k_norm_g=m_mem_k_norm_g, b_merge=m_b_merge, w_branch=m_w_branch, w_out=m_w_out)
    v = dict(norm_g=v_norm_g, w_in=v_w_in, cq_norm_g=v_cq_norm_g, ckv_norm_g=v_ckv_norm_g, w_uq=v_w_uq, w_ukv=v_w_ukv,
             mla_q_norm_g=v_mla_q_norm_g, mla_k_norm_g=v_mla_k_norm_g, conv_w=v_conv_w, conv_b=v_conv_b, sg_ln_g=v_sg_ln_g,
             sg_ln_b=v_sg_ln_b, w_spatial=v_w_spatial, b_spatial=v_b_spatial, mem_norm_g=v_mem_norm_g, w_mem_kv=v_w_mem_kv,
             mem_q_norm_g=v_mem_q_norm_g, mem_k_norm_g=v_mem_k_norm_g, b_merge=v_b_merge, w_branch=v_w_branch, w_out=v_w_out)

    chip_core = jnp.stack([2 * lax.axis_index("x") + lax.axis_index("y"), lax.axis_index("c")]).astype(jnp.int32)

    first = _Gather(w, 0, ("w_in",), chip_core, "gather_l0_w_in_")
    rest = _Gather(w, 0, _SHARDED_MM[1:], first.token, "gather_l0_rest_")
    small_on_its_way = _SmallGather(_pack_rows([w[n] for n in _SHARDED_F32], F32, 8), [rest.token], "gather_small_weights_")
    later = _Gather(w, 1, _SHARDED_MM, small_on_its_way.token, "gather_l1_")
    w_in0 = first.finish(first.pass_on(later.token))["w_in"]
    small = {}

    def rest_of_layer0(proj0):
        landed = _layer_params_rest(rest.finish(rest.pass_on(proj0)))
        small.update(_small_sharded_weights(w, small_on_its_way.finish([landed["wo"]])))
        return dict(landed, conv_w=small["conv_w"][0], bm=small["b_merge"][0])

    def layer1_params(saved0, act0):
        return _layer_params(1, w, later.finish(act0), small["conv_w"][1], small["b_merge"][1])

    params0 = _layer_params_first(0, w, w_in0, None, None)
    params = [dict(params0, late=rest_of_layer0, after_attn=later.pass_on), layer1_params]
    others = _SHARDED_MM[1:]
    rs1 = _ReduceScatter("rs_l1_", _SHARDED_MM)
    rs0_rest, rs0_w_in = _ReduceScatter("rs_l0_rest_", others), _ReduceScatter("rs_l0_w_in_", ("w_in",))

    def layer0_grads_done(grads):
        return [rs0_rest.scatter([grads["w_in_aligned"]]), rs0_w_in.exchange(grads)]

    hooks = [dict(on_rest_grads=rs0_rest.exchange, on_grads=layer0_grads_done),
             dict(on_grads=lambda grads: [rs1.exchange(grads)], after_layer=lambda dy: rs1.scatter([dy]))]
    sq, grad_x, layer_grads = _forward_backward(x[0], mem[0], positions[0], loss_target[0], params, hooks)

    g_small = {n: jnp.stack([layer_grads[l][n] for l in range(DEPTH)]) for n in _ALL_REDUCED}
    small_grads = _SmallGather(_pack_rows([g_small[n] for n in _ALL_REDUCED] + [sq], F32, 64), [grad_x], "gather_small_grads_")
    scattering = rs0_w_in.scatter([grad_x, small_grads.token])
    shard_grads = {1: rs1.finish([scattering]), 0: rs0_rest.finish([scattering])}
    as3d = lambda a: a.reshape(DEPTH, -1, a.shape[-1])
    as2d = lambda a: a.reshape(-1, a.shape[-1])
    big = lambda t: [as3d(t[n]) for n in others]
    turned = lambda t: [jnp.swapaxes(t["w_in"], 1, 2)]
    assert W_IN_SHARD % (8 * 7) == 0

    def update_w_in(l, grad, prev):
        return _adamw_layer_call(l, turned(w), [grad.T], turned(m), turned(v), prev, [], "adamw_w_in_l%d" % l, steps=7)

    def update_others(l, prev):
        return _adamw_layer_call(l, big(w), [as2d(shard_grads[l][n]) for n in others], big(m), big(v), prev, [], "adamw_l%d" % l)

    upd = update_others(0, update_others(1, None))
    g, sq_total = _small_sums(g_small, sq, small_grads.finish([upd[0]]))
    loss = 0.5 / D * jnp.sum(sq_total)
    delta, new_m, new_v = _adamw_small(w, g, m, v)
    upd_in1 = update_w_in(1, shard_grads[1]["w_in"], None)
    w_in_grad0 = rs0_w_in.finish([grad_x, upd_in1[0], upd[0], delta["norm_g"]])["w_in"]
    upd_in = update_w_in(0, w_in_grad0, upd_in1)
    g["w_in"], delta["w_in"], new_m["w_in"], new_v["w_in"] = [jnp.swapaxes(a, 1, 2) for a in upd_in]
    for t, n in enumerate(others):
        g[n], delta[n], new_m[n], new_v[n] = [a.reshape(w[n].shape) for a in upd[4 * t:4 * t + 4]]
    return (loss, grad_x[None], *[g[n] for n in _WEIGHTS], *[delta[n] for n in _WEIGHTS],
            *[new_m[n] for n in _WEIGHTS], *[new_v[n] for n in _WEIGHTS])
```

```python
import functools
import math

import jax
import jax.numpy as jnp
from jax import lax
from jax.experimental import pallas as pl
from jax.experimental.pallas import tpu as pltpu

F32 = jnp.float32
MM = jnp.bfloat16

D = 1024
DEPTH = 2
EPS = 1e-6
H = 8
NOPE = 64
ROPE = 32
QKH = 96
VH = 64
QL = 256
KVL = 128
ROPE_THETA = 10000.0
CW = 512
SGW = 512
SGG = 4
SGC = 128
MH = 4
MHD = 128
NB = 4
BW = 512
NEG_INF = -1e30
LANES = 128
N_CHIPS = 4

R_CQ, R_CKV, R_KR, R_CV, R_SGI, R_MQ, R_SG, R_ML, R_END = 0, 256, 384, 416, 1952, 2976, 3488, 5536, 9632
OFF_ML, OFF_SG, OFF_CV, OFF_SGI, OFF_MQ, OFF_CQ, OFF_CKV, OFF_KR, NP = 0, 4096, 6144, 7680, 8704, 9216, 9472, 9600, 9728

ADAM_LR = 0.001
ADAM_B1 = 0.9
ADAM_B2 = 0.999
ADAM_EPS = 1e-08
ADAM_WD = 0.01
ADAM_STEP = 10

VMEM_LIMIT = 56 * 1024 * 1024
PACK_W = 512
MESH_ID = pl.DeviceIdType.MESH


def _cparams(n_axes):
    return pltpu.CompilerParams(dimension_semantics=("arbitrary",) * n_axes, vmem_limit_bytes=VMEM_LIMIT)


def _bs(shape, imap):
    return pl.BlockSpec(shape, imap)


@jax.custom_vjp
def _mm_plain(a, b):
    return jnp.dot(a.astype(MM), b.astype(MM), preferred_element_type=F32)


def _mm_plain_fwd(a, b):
    return _mm_plain(a, b), (a, b)


def _mm_plain_bwd(res, g):
    a, b = res
    gm = g.astype(MM)
    da = lax.dot_general(gm, b.astype(MM), (((1,), (1,)), ((), ())), preferred_element_type=F32)
    db = lax.dot_general(a.astype(MM), gm, (((0,), (0,)), ((), ())), preferred_element_type=F32)
    return da.astype(a.dtype), db.astype(b.dtype)


_mm_plain.defvjp(_mm_plain_fwd, _mm_plain_bwd)


@jax.custom_vjp
def _mm_slot(a, w, slot):
    return jnp.dot(a.astype(MM), w.astype(MM), preferred_element_type=F32)


def _mm_slot_fwd(a, w, slot):
    return _mm_slot(a, w, slot), (a, w)


def _mm_slot_bwd(res, g):
    a, w = res
    gm = g.astype(MM)
    da = lax.dot_general(gm, w.astype(MM), (((1,), (1,)), ((), ())), preferred_element_type=F32)
    dw = lax.dot_general(a.astype(MM), gm, (((0,), (0,)), ((), ())), preferred_element_type=F32)
    return da.astype(a.dtype), jnp.zeros_like(w), dw


_mm_slot.defvjp(_mm_slot_fwd, _mm_slot_bwd)


def _mm(a, b):
    if isinstance(b, tuple):
        return _mm_slot(a, b[0], b[1])
    return _mm_plain(a, b)


def _with_slot(w):
    return (w, jnp.zeros(w.shape, F32))


@jax.custom_vjp
def _mm_nt(a, b):
    return lax.dot_general(a.astype(MM), b.astype(MM), (((1,), (1,)), ((), ())), preferred_element_type=F32)


def _mm_nt_fwd(a, b):
    return _mm_nt(a, b), (a, b)


def _mm_nt_bwd(res, g):
    a, b = res
    gm = g.astype(MM)
    da = jnp.dot(gm, b.astype(MM), preferred_element_type=F32)
    db = lax.dot_general(gm, a.astype(MM), (((0,), (0,)), ((), ())), preferred_element_type=F32)
    return da.astype(a.dtype), db.astype(b.dtype)


_mm_nt.defvjp(_mm_nt_fwd, _mm_nt_bwd)


@functools.partial(jax.custom_vjp, nondiff_argnums=(1,))
def _lane_roll(x, shift):
    return pltpu.roll(x, shift, 1)


def _lane_roll_fwd(x, shift):
    return pltpu.roll(x, shift, 1), None


def _lane_roll_bwd(shift, _, g):
    return (pltpu.roll(g, (LANES - shift) % LANES, 1),)


_lane_roll.defvjp(_lane_roll_fwd, _lane_roll_bwd)


def _rms_n(x, g, n):
    ms = jnp.sum(x * x, axis=-1, keepdims=True) * (1.0 / n)
    return x * lax.rsqrt(ms + EPS) * g


def _softmax(s):
    m = jnp.max(s, axis=-1, keepdims=True)
    e = jnp.exp(s - m)
    return e / jnp.sum(e, axis=-1, keepdims=True)


def _rope(t, cos_t, sin_a, sin_b):
    return t * cos_t + _lane_roll(t, LANES - 16) * sin_a + _lane_roll(t, 16) * sin_b


def _mla_prep_fn(cq, ckv, kr, cos_t, sin_a, sin_b, cq_g, ckv_g, qg, kg, wuq, wkn, wv):
    cqn = _rms_n(cq, cq_g, QL)
    ckvn = _rms_n(ckv, ckv_g, KVL)
    lane = lax.broadcasted_iota(jnp.int32, kr.shape, 1)
    krm = jnp.where((lane >= NOPE) & (lane < QKH), kr, 0.0)
    qs, ks = [], []
    for h in range(H):
        qh = _rms_n(_mm(cqn, wuq[h]), qg, QKH)
        qs.append(_rope(qh, cos_t, sin_a, sin_b) * (QKH ** -0.5))
        kh = _rms_n(_mm(ckvn, wkn[h]) + krm, kg, QKH)
        ks.append(_rope(kh, cos_t, sin_a, sin_b))
    return jnp.concatenate(qs, axis=-1), jnp.concatenate(ks, axis=-1), _mm(ckvn, wv)


def _dot_nt(a, b):
    return lax.dot_general(a.astype(MM), b.astype(MM), (((1,), (1,)), ((), ())), preferred_element_type=F32)


def _dot_tn(a, b):
    return lax.dot_general(a.astype(MM), b.astype(MM), (((0,), (0,)), ((), ())), preferred_element_type=F32)


def _causal_scores(qe, ke):
    tq, kl = qe.shape[0], ke.shape[0]
    s = _dot_nt(qe, ke)
    rows = lax.broadcasted_iota(jnp.int32, (tq, tq), 0)
    cols = lax.broadcasted_iota(jnp.int32, (tq, tq), 1)
    own = jnp.where(cols <= rows, s[:, kl - tq:], NEG_INF)
    return own if kl == tq else jnp.concatenate([s[:, :kl - tq], own], axis=1)


def _head_lanes(e, shape):
    lane = lax.broadcasted_iota(jnp.int32, shape, len(shape) - 1)
    return (lane >= VH * e) & (lane < VH * (e + 1))


def _attn_pair_fwd(q2, k2, v2):
    tq = q2.shape[0]
    o = jnp.zeros((tq, LANES), F32)
    lse = jnp.zeros((tq, LANES), F32)
    for e in range(2):
        sl = slice(LANES * e, LANES * (e + 1))
        s = _causal_scores(q2[:, sl], k2[:, sl])
        m = jnp.max(s, axis=-1, keepdims=True)
        ex = jnp.exp(s - m)
        l = jnp.sum(ex, axis=-1, keepdims=True)
        ve = jnp.where(_head_lanes(e, v2[:, sl].shape), v2[:, sl], 0.0)
        o = o + jnp.dot((ex * (1.0 / l)).astype(MM), ve.astype(MM), preferred_element_type=F32)
        lse = jnp.where(_head_lanes(e, lse.shape), m + jnp.log(l), lse)
    return o, lse


def _attn_pair_bwd(q2, k2, v2, sg, dys, o, lse):
    sig = jax.nn.sigmoid(sg)
    do = dys * (sg * sig)
    dsg = dys * o * (sig * (1.0 + sg * (1.0 - sig)))
    dqs, dks, dvs = [], [], []
    for e in range(2):
        sl = slice(LANES * e, LANES * (e + 1))
        qe, ke = q2[:, sl], k2[:, sl]
        hm = _head_lanes(e, o.shape)
        lse_e = jnp.max(jnp.where(hm, lse, NEG_INF), axis=-1, keepdims=True)
        do_e = jnp.where(hm, do, 0.0)
        delta = jnp.sum(do_e * o, axis=-1, keepdims=True)
        p = jnp.exp(_causal_scores(qe, ke) - lse_e)
        ve = jnp.where(_head_lanes(e, v2[:, sl].shape), v2[:, sl], 0.0)
        dvs.append(_dot_tn(p, do_e))
        ds = p * (_dot_nt(do_e, ve) - delta)
        dqs.append(jnp.dot(ds.astype(MM), ke.astype(MM), preferred_element_type=F32))
        dks.append(_dot_tn(ds, qe))
    return jnp.concatenate(dqs, axis=-1), jnp.concatenate(dks, axis=-1), jnp.concatenate(dvs, axis=-1), dsg


def _sg_fn(u, v, sgc, ln_g, ln_b, ws, bs):
    mu = jnp.mean(v, axis=-1, keepdims=True)
    xc = v - mu
    vn = xc * lax.rsqrt(jnp.mean(xc * xc, axis=-1, keepdims=True) + EPS) * ln_g + ln_b
    r = lax.broadcasted_iota(jnp.int32, (SGC, SGC), 0)
    c = lax.broadcasted_iota(jnp.int32, (SGC, SGC), 1)
    wt = [jnp.where(r >= c, w, 0.0) for w in ws]
    row_blocks = []
    for ch in range(u.shape[0] // SGC):
        col_blocks = []
        for g in range(SGG):
            blk = vn[SGC * ch:SGC * (ch + 1), LANES * g:LANES * (g + 1)]
            col_blocks.append(_mm(wt[g], blk) + bs[g])
        row_blocks.append(jnp.concatenate(col_blocks, axis=-1))
    mixed = jnp.concatenate(row_blocks, axis=0)
    return (u * mixed) * jax.nn.silu(sgc)


def _memkv_fn(mem, mem_g, wm, kg):
    kv = _mm(_rms_n(mem, mem_g, D), wm)
    ks = [_rms_n(kv[:, MHD * h:MHD * (h + 1)], kg, MHD) for h in range(MH)]
    return jnp.concatenate(ks, axis=-1), kv[:, MH * MHD:]


def _mem_fn(mq, sgd, k, v, qg):
    outs = []
    for h in range(MH):
        sl = slice(MHD * h, MHD * (h + 1))
        qh = _rms_n(mq[:, sl], qg, MHD)
        p = _softmax(_mm_nt(qh, k[:, sl]) * (MHD ** -0.5))
        outs.append(_mm(p, v[:, sl]))
    return jnp.concatenate(outs, axis=-1) * jax.nn.silu(sgd)


def _merge_fn(ys, logits, bm, wb, wo):
    merged = None
    for n in range(NB):
        z = jnp.concatenate([_mm(ys[n], wb[j][n]) for j in range(N_CHIPS)], axis=-1)
        gate = jax.nn.sigmoid(logits[:, D * n:D * (n + 1)] + bm[n])
        merged = gate * z if merged is None else merged + gate * z
    return _mm(merged, wo)


def _proj_call(x, g, w):
    s_len = x.shape[0]
    tm, tn = s_len, 512

    def body(x_ref, g_ref, w_ref, p_ref, h_ref):
        @pl.when(pl.program_id(1) == 0)
        def _():
            h_ref[...] = _rms_n(x_ref[...], g_ref[...], D).astype(h_ref.dtype)
        p_ref[...] = jnp.dot(h_ref[...], w_ref[...], preferred_element_type=F32)

    return pl.pallas_call(
        body, grid=(s_len // tm, NP // tn),
        in_specs=[_bs((tm, D), lambda i, j: (i, 0)), _bs((1, D), lambda i, j: (0, 0)), _bs((D, tn), lambda i, j: (0, j))],
        out_specs=[_bs((tm, tn), lambda i, j: (i, j)), _bs((tm, D), lambda i, j: (i, 0))],
        out_shape=[jax.ShapeDtypeStruct((s_len, NP), F32), jax.ShapeDtypeStruct((s_len, D), MM)],
        name="proj", compiler_params=_cparams(2))(x, g, w)


def _rope_tables(pos):
    half = ROPE // 2
    inv_freq = ROPE_THETA ** (-jnp.arange(half, dtype=F32) / half)
    ang = pos.astype(F32)[:, None] * inv_freq
    cos, sin = jnp.cos(ang), jnp.sin(ang)
    s_len = pos.shape[0]
    z = lambda n: jnp.zeros((s_len, n), F32)
    cos_t = jnp.concatenate([jnp.ones((s_len, NOPE), F32), cos, cos, z(LANES - QKH)], axis=1)
    sin_a = jnp.concatenate([z(NOPE), -sin, z(LANES - NOPE - half)], axis=1)
    sin_b = jnp.concatenate([z(NOPE + half), sin, z(LANES - QKH)], axis=1)
    return cos_t, sin_a, sin_b


def _mla_prep_specs(tm):
    row = lambda w, off: _bs((tm, w), lambda i: (i, off // w))
    full2 = lambda a, b: _bs((a, b), lambda i: (0, 0))
    full3 = lambda a, b, c: _bs((a, b, c), lambda i: (0, 0, 0))
    tab = _bs((tm, LANES), lambda i: (i, 0))
    return [row(QL, OFF_CQ), row(KVL, OFF_CKV), row(LANES, OFF_KR), tab, tab, tab,
            full2(1, QL), full2(1, KVL), full2(1, LANES), full2(1, LANES),
            full3(H, QL, LANES), full3(H, KVL, LANES), full2(KVL, H * LANES)]


def _mla_prep_args(body_refs, wrap=lambda w: w):
    (cq, ckv, kr, ct, sa, sb, cqg, ckvg, qg, kg, wuq, wkn, wv) = body_refs
    return (cq[...], ckv[...], kr[...], ct[...], sa[...], sb[...], cqg[...], ckvg[...], qg[...], kg[...],
            [wrap(wuq[h]) for h in range(H)], [wrap(wkn[h]) for h in range(H)], wrap(wv[...]))


def _mla_prep_call(proj, tabs, cq_g, ckv_g, qg, kg, wuq, wkn, wv):
    s_len = proj.shape[0]
    tm = min(s_len, 256)

    def body(*refs):
        q_ref, k_ref, v_ref = refs[13:]
        q, k, v = _mla_prep_fn(*_mla_prep_args(refs[:13]))
        q_ref[...] = q.astype(q_ref.dtype)
        k_ref[...] = k.astype(k_ref.dtype)
        v_ref[...] = v.astype(v_ref.dtype)

    out = _bs((tm, H * LANES), lambda i: (i, 0))
    return pl.pallas_call(
        body, grid=(s_len // tm,), in_specs=_mla_prep_specs(tm), out_specs=[out, out, out],
        out_shape=[jax.ShapeDtypeStruct((s_len, H * LANES), MM)] * 3,
        name="mla_prep", compiler_params=_cparams(1))(proj, proj, proj, *tabs, cq_g, ckv_g, qg, kg, wuq, wkn, wv)


def _mla_prep_bwd_call(proj, tabs, cq_g, ckv_g, qg, kg, wuq, wkn, wv, dq, dk, dv):
    s_len = proj.shape[0]
    tm = min(s_len, 256)

    def body(*refs):
        dq_ref, dk_ref, dv_ref = refs[13:16]
        dlat_ref, dcqg_ref, dckvg_ref, dqg_ref, dkg_ref, dwuq_ref, dwkn_ref, dwv_ref = refs[16:]
        _, vjp = jax.vjp(_mla_prep_fn, *_mla_prep_args(refs[:13], _with_slot))
        (dcq, dckv, dkr, _, _, _, dcqg, dckvg, dqg, dkg, dwuq, dwkn, dwv) = vjp((dq_ref[...], dk_ref[...], dv_ref[...]))
        dwuq, dwkn, dwv = [d[1] for d in dwuq], [d[1] for d in dwkn], dwv[1]
        dlat_ref[...] = jnp.concatenate([dcq, dckv, dkr], axis=-1).astype(dlat_ref.dtype)

        @pl.when(pl.program_id(0) == 0)
        def _():
            for r in (dcqg_ref, dckvg_ref, dqg_ref, dkg_ref, dwuq_ref, dwkn_ref, dwv_ref):
                r[...] = jnp.zeros_like(r)
        dcqg_ref[...] += dcqg
        dckvg_ref[...] += dckvg
        dqg_ref[...] += dqg
        dkg_ref[...] += dkg
        for h in range(H):
            dwuq_ref[h] += dwuq[h]
            dwkn_ref[h] += dwkn[h]
        dwv_ref[...] += dwv

    big = _bs((tm, H * LANES), lambda i: (i, 0))
    row = lambda w: _bs((tm, w), lambda i: (i, 0))
    full2 = lambda a, b: _bs((a, b), lambda i: (0, 0))
    full3 = lambda a, b, c: _bs((a, b, c), lambda i: (0, 0, 0))
    sd = jax.ShapeDtypeStruct
    return pl.pallas_call(
        body, grid=(s_len // tm,), in_specs=_mla_prep_specs(tm) + [big, big, big],
        out_specs=[row(QL + KVL + LANES), full2(1, QL), full2(1, KVL), full2(1, LANES), full2(1, LANES),
                   full3(H, QL, LANES), full3(H, KVL, LANES), full2(KVL, H * LANES)],
        out_shape=[sd((s_len, QL + KVL + LANES), MM), sd((1, QL), F32), sd((1, KVL), F32),
                   sd((1, LANES), F32), sd((1, LANES), F32), sd((H, QL, LANES), F32), sd((H, KVL, LANES), F32),
                   sd((KVL, H * LANES), F32)],
        name="mla_prep_bwd", compiler_params=_cparams(1))(proj, proj, proj, *tabs, cq_g, ckv_g, qg, kg, wuq, wkn, wv, dq, dk, dv)


def _attn_specs(s_len, tq):
    pair = 2 * LANES
    return [_bs((tq, pair), lambda p, i: (i, p)), _bs((s_len, pair), lambda p, i: (0, p)), _bs((s_len, pair), lambda p, i: (0, p)),
            _bs((tq, LANES), lambda p, i: (i, OFF_SG // LANES + p))]


def _attn_call(q, k, v, proj):
    s_len = q.shape[0]
    tq = min(s_len, 256)

    def body(q_ref, k_ref, v_ref, sg_ref, y_ref, o_ref, lse_ref):
        for n in range(s_len // tq):
            @pl.when(pl.program_id(1) == n)
            def _():
                kl = (n + 1) * tq
                o, lse = _attn_pair_fwd(q_ref[...], k_ref[:kl, :], v_ref[:kl, :])
                y_ref[...] = (o * jax.nn.silu(sg_ref[...])).astype(y_ref.dtype)
                o_ref[...] = o
                lse_ref[...] = lse

    tile = _bs((tq, LANES), lambda p, i: (i, p))
    sd = jax.ShapeDtypeStruct
    return pl.pallas_call(
        body, grid=(H // 2, s_len // tq), in_specs=_attn_specs(s_len, tq), out_specs=[tile, tile, tile],
        out_shape=[sd((s_len, BW), MM), sd((s_len, BW), F32), sd((s_len, BW), F32)],
        name="attn", compiler_params=_cparams(2))(q, k, v, proj)


def _attn_bwd_call(q, k, v, proj, dys, o, lse):
    s_len = q.shape[0]
    tq = min(s_len, 256)
    pair = 2 * LANES

    def body(q_ref, k_ref, v_ref, sg_ref, dy_ref, o_ref, lse_ref, dq_ref, dk_ref, dv_ref, dsg_ref):
        i = pl.program_id(1)

        @pl.when(i == 0)
        def _():
            dk_ref[...] = jnp.zeros_like(dk_ref)
            dv_ref[...] = jnp.zeros_like(dv_ref)

        for n in range(s_len // tq):
            @pl.when(i == n)
            def _():
                kl = (n + 1) * tq
                dq, dk, dv, dsg = _attn_pair_bwd(q_ref[...], k_ref[:kl, :], v_ref[:kl, :], sg_ref[...], dy_ref[...],
                                                 o_ref[...], lse_ref[...])
                dq_ref[...] = dq
                dsg_ref[...] = dsg.astype(dsg_ref.dtype)
                dk_ref[:kl, :] += dk
                dv_ref[:kl, :] += dv

    sd = jax.ShapeDtypeStruct
    tile = _bs((tq, LANES), lambda p, i: (i, p))
    return pl.pallas_call(
        body, grid=(H // 2, s_len // tq),
        in_specs=_attn_specs(s_len, tq) + [tile, tile, tile],
        out_specs=[_bs((tq, pair), lambda p, i: (i, p)), _bs((s_len, pair), lambda p, i: (0, p)),
                   _bs((s_len, pair), lambda p, i: (0, p)), tile],
        out_shape=[sd((s_len, H * LANES), F32), sd((s_len, H * LANES), F32), sd((s_len, H * LANES), F32), sd((s_len, BW), MM)],
        name="attn_bwd", compiler_params=_cparams(2))(q, k, v, proj, dys, o, lse)


def _shift_down(a, n):
    r = lax.broadcasted_iota(jnp.int32, a.shape, 0)
    return jnp.where(r >= n, pltpu.roll(a, n, 0), 0.0)


def _shift_up(a, n):
    s_len = a.shape[0]
    r = lax.broadcasted_iota(jnp.int32, a.shape, 0)
    return jnp.where(r < s_len - n, pltpu.roll(a, s_len - n, 0), 0.0)


def _conv_specs(s_len):
    col = lambda off: _bs((s_len, LANES), lambda j: (0, off // LANES + j))
    return [col(OFF_CV), col(OFF_CV + CW), col(OFF_CV + 2 * CW), col(OFF_SG + BW),
            _bs((3, LANES), lambda j: (0, j)), _bs((1, LANES), lambda j: (0, j))]


def _conv_call(proj, cw, cb):
    s_len = proj.shape[0]

    def body(bg_ref, cg_ref, xi_ref, sg_ref, w_ref, b_ref, y_ref):
        z = cg_ref[...] * xi_ref[...]
        y = b_ref[...] + w_ref[0:1, :] * _shift_down(z, 2)
        y = y + w_ref[1:2, :] * _shift_down(z, 1)
        y = y + w_ref[2:3, :] * z
        y_ref[...] = ((bg_ref[...] * y) * jax.nn.silu(sg_ref[...])).astype(y_ref.dtype)

    return pl.pallas_call(
        body, grid=(CW // LANES,), in_specs=_conv_specs(s_len), out_specs=_bs((s_len, LANES), lambda j: (0, j)),
        out_shape=jax.ShapeDtypeStruct((s_len, CW), MM), name="conv", compiler_params=_cparams(1))(proj, proj, proj, proj, cw, cb)


def _conv_bwd_call(proj, cw, cb, dys):
    s_len = proj.shape[0]

    def body(bg_ref, cg_ref, xi_ref, sg_ref, w_ref, b_ref, dys_ref, dbg_ref, dcg_ref, dxi_ref, dsg_ref, dw_ref, db_ref):
        bg, cg, xi, sg = bg_ref[...], cg_ref[...], xi_ref[...], sg_ref[...]
        w0, w1, w2 = w_ref[0:1, :], w_ref[1:2, :], w_ref[2:3, :]
        z = cg * xi
        z1, z2 = _shift_down(z, 1), _shift_down(z, 2)
        y = b_ref[...] + w0 * z2
        y = y + w1 * z1
        y = y + w2 * z
        yb = bg * y
        sig = jax.nn.sigmoid(sg)
        silu = sg * sig
        dys_v = dys_ref[...]
        dsg_ref[...] = (dys_v * yb * (sig * (1.0 + sg * (1.0 - sig)))).astype(dsg_ref.dtype)
        dyb = dys_v * silu
        dbg_ref[...] = (dyb * y).astype(dbg_ref.dtype)
        dy = dyb * bg
        db_ref[...] = jnp.sum(dy, axis=0, keepdims=True)
        dw_ref[0:1, :] = jnp.sum(dy * z2, axis=0, keepdims=True)
        dw_ref[1:2, :] = jnp.sum(dy * z1, axis=0, keepdims=True)
        dw_ref[2:3, :] = jnp.sum(dy * z, axis=0, keepdims=True)
        dz = w2 * dy + w1 * _shift_up(dy, 1) + w0 * _shift_up(dy, 2)
        dcg_ref[...] = (dz * xi).astype(dcg_ref.dtype)
        dxi_ref[...] = (dz * cg).astype(dxi_ref.dtype)

    col = _bs((s_len, LANES), lambda j: (0, j))
    sd = jax.ShapeDtypeStruct
    return pl.pallas_call(
        body, grid=(CW // LANES,), in_specs=_conv_specs(s_len) + [col],
        out_specs=[col, col, col, col, _bs((3, LANES), lambda j: (0, j)), _bs((1, LANES), lambda j: (0, j))],
        out_shape=[sd((s_len, CW), MM)] * 4 + [sd((3, CW), F32), sd((1, CW), F32)],
        name="conv_bwd", compiler_params=_cparams(1))(proj, proj, proj, proj, cw, cb, dys)


def _sg_specs(tm):
    row = lambda off: _bs((tm, SGW), lambda i: (i, off // SGW))
    return [row(OFF_SGI), row(OFF_SGI + SGW), row(OFF_SG + 2 * BW), _bs((1, SGW), lambda i: (0, 0)), _bs((1, SGW), lambda i: (0, 0)),
            _bs((SGG, SGC, SGC), lambda i: (0, 0, 0)), _bs((SGG, SGC, 1), lambda i: (0, 0, 0))]


def _sg_args(refs):
    u, v, sg, lg, lb, ws, bs = refs
    return (u[...], v[...], sg[...], lg[...], lb[...], [ws[g] for g in range(SGG)], [bs[g] for g in range(SGG)])


def _sg_call(proj, ln_g, ln_b, ws, bs):
    s_len = proj.shape[0]
    tm = min(s_len, 256)

    def body(*refs):
        refs[7][...] = _sg_fn(*_sg_args(refs[:7])).astype(refs[7].dtype)

    return pl.pallas_call(
        body, grid=(s_len // tm,), in_specs=_sg_specs(tm), out_specs=_bs((tm, SGW), lambda i: (i, 0)),
        out_shape=jax.ShapeDtypeStruct((s_len, SGW), MM), name="sgmlp", compiler_params=_cparams(1))(proj, proj, proj, ln_g, ln_b, ws, bs)


def _sg_bwd_call(proj, ln_g, ln_b, ws, bs, dys):
    s_len = proj.shape[0]
    tm = min(s_len, 256)

    def body(*refs):
        dys_ref = refs[7]
        duv_ref, dsg_ref, dlg_ref, dlb_ref, dws_ref, dbs_ref = refs[8:]
        _, vjp = jax.vjp(_sg_fn, *_sg_args(refs[:7]))
        du, dv, dsg, dlg, dlb, dws, dbs = vjp(dys_ref[...])
        duv_ref[...] = jnp.concatenate([du, dv], axis=-1).astype(duv_ref.dtype)
        dsg_ref[...] = dsg.astype(dsg_ref.dtype)

        @pl.when(pl.program_id(0) == 0)
        def _():
            for r in (dlg_ref, dlb_ref, dws_ref, dbs_ref):
                r[...] = jnp.zeros_like(r)
        dlg_ref[...] += dlg
        dlb_ref[...] += dlb
        for g in range(SGG):
            dws_ref[g] += dws[g]
            dbs_ref[g] += dbs[g]

    row = _bs((tm, SGW), lambda i: (i, 0))
    sd = jax.ShapeDtypeStruct
    return pl.pallas_call(
        body, grid=(s_len // tm,), in_specs=_sg_specs(tm) + [row],
        out_specs=[_bs((tm, 2 * SGW), lambda i: (i, 0)), row, _bs((1, SGW), lambda i: (0, 0)), _bs((1, SGW), lambda i: (0, 0)),
                   _bs((SGG, SGC, SGC), lambda i: (0, 0, 0)), _bs((SGG, SGC, 1), lambda i: (0, 0, 0))],
        out_shape=[sd((s_len, 2 * SGW), MM), sd((s_len, SGW), MM), sd((1, SGW), F32), sd((1, SGW), F32),
                   sd((SGG, SGC, SGC), F32), sd((SGG, SGC, 1), F32)],
        name="sgmlp_bwd", compiler_params=_cparams(1))(proj, proj, proj, ln_g, ln_b, ws, bs, dys)


def _memkv_call(mem, mem_g, wm, kg):
    m_len = mem.shape[0]

    def body(mem_ref, g_ref, w_ref, kg_ref, k_ref, v_ref):
        k, v = _memkv_fn(mem_ref[...], g_ref[...], w_ref[...], kg_ref[...])
        k_ref[...] = k.astype(k_ref.dtype)
        v_ref[...] = v.astype(v_ref.dtype)

    return pl.pallas_call(body, out_shape=[jax.ShapeDtypeStruct((m_len, MH * MHD), MM)] * 2, name="memkv",
                          compiler_params=pltpu.CompilerParams(vmem_limit_bytes=VMEM_LIMIT))(mem, mem_g, wm, kg)


def _memkv_bwd_call(mem, mem_g, wm, kg, dk, dv):
    def body(mem_ref, g_ref, w_ref, kg_ref, dk_ref, dv_ref, dg_ref, dw_ref, dkg_ref):
        _, vjp = jax.vjp(_memkv_fn, mem_ref[...], g_ref[...], _with_slot(w_ref[...]), kg_ref[...])
        _, dg, dw, dkg = vjp((dk_ref[...], dv_ref[...]))
        dg_ref[...] = dg
        dw_ref[...] = dw[1]
        dkg_ref[...] = dkg

    sd = jax.ShapeDtypeStruct
    return pl.pallas_call(body, out_shape=[sd((1, D), F32), sd((D, 2 * MH * MHD), F32), sd((1, MHD), F32)], name="memkv_bwd",
                          compiler_params=pltpu.CompilerParams(vmem_limit_bytes=VMEM_LIMIT))(mem, mem_g, wm, kg, dk, dv)


def _mem_specs(tm, m_len):
    w = MH * MHD
    return [_bs((tm, w), lambda i: (i, OFF_MQ // w)), _bs((tm, BW), lambda i: (i, (OFF_SG + 3 * BW) // BW)),
            _bs((m_len, w), lambda i: (0, 0)), _bs((m_len, w), lambda i: (0, 0)), _bs((1, MHD), lambda i: (0, 0))]


def _mem_call(proj, k, v, qg):
    s_len, m_len = proj.shape[0], k.shape[0]
    tm = min(s_len, 256)

    def body(mq_ref, sg_ref, k_ref, v_ref, qg_ref, y_ref):
        y_ref[...] = _mem_fn(mq_ref[...], sg_ref[...], k_ref[...], v_ref[...], qg_ref[...]).astype(y_ref.dtype)

    return pl.pallas_call(
        body, grid=(s_len // tm,), in_specs=_mem_specs(tm, m_len), out_specs=_bs((tm, BW), lambda i: (i, 0)),
        out_shape=jax.ShapeDtypeStruct((s_len, BW), MM), name="memattn", compiler_params=_cparams(1))(proj, proj, k, v, qg)


def _mem_bwd_call(proj, k, v, qg, dys):
    s_len, m_len = proj.shape[0], k.shape[0]
    tm = min(s_len, 256)
    w = MH * MHD

    def body(mq_ref, sg_ref, k_ref, v_ref, qg_ref, dys_ref, dmq_ref, dsg_ref, dk_ref, dv_ref, dqg_ref):
        _, vjp = jax.vjp(_mem_fn, mq_ref[...], sg_ref[...], k_ref[...].astype(F32), v_ref[...].astype(F32), qg_ref[...])
        dmq, dsg, dk, dv, dqg = vjp(dys_ref[...])
        dmq_ref[...] = dmq.astype(dmq_ref.dtype)
        dsg_ref[...] = dsg.astype(dsg_ref.dtype)

        @pl.when(pl.program_id(0) == 0)
        def _():
            for r in (dk_ref, dv_ref, dqg_ref):
                r[...] = jnp.zeros_like(r)
        dk_ref[...] += dk
        dv_ref[...] += dv
        dqg_ref[...] += dqg

    row = _bs((tm, BW), lambda i: (i, 0))
    kv = _bs((m_len, w), lambda i: (0, 0))
    sd = jax.ShapeDtypeStruct
    return pl.pallas_call(
        body, grid=(s_len // tm,), in_specs=_mem_specs(tm, m_len) + [row],
        out_specs=[row, row, kv, kv, _bs((1, MHD), lambda i: (0, 0))],
        out_shape=[sd((s_len, w), MM), sd((s_len, BW), MM), sd((m_len, w), F32), sd((m_len, w), F32), sd((1, MHD), F32)],
        name="memattn_bwd", compiler_params=_cparams(1))(proj, proj, k, v, qg, dys)


def _merge_specs(tm):
    row = _bs((tm, BW), lambda i: (i, 0))
    return [row, row, row, row, _bs((tm, NB * D), lambda i: (i, OFF_ML // (NB * D))), _bs((NB, D), lambda i: (0, 0)),
            _bs((N_CHIPS, NB, BW, D // N_CHIPS), lambda i: (0, 0, 0, 0)), _bs((D, D), lambda i: (0, 0))]


def _merge_call(ys, proj, bm, wb, wo, x):
    s_len = proj.shape[0]
    tm = min(s_len, 256)

    def body(ya, yb, yc, yd, lg_ref, bm_ref, wb_ref, wo_ref, x_ref, o_ref):
        out = _merge_fn([r[...] for r in (ya, yb, yc, yd)], lg_ref[...], [bm_ref[n:n + 1, :] for n in range(NB)],
                        [[wb_ref[j, n] for n in range(NB)] for j in range(N_CHIPS)], wo_ref[...])
        o_ref[...] = x_ref[...] + out

    xrow = _bs((tm, D), lambda i: (i, 0))
    return pl.pallas_call(
        body, grid=(s_len // tm,), in_specs=_merge_specs(tm) + [xrow], out_specs=xrow,
        out_shape=jax.ShapeDtypeStruct((s_len, D), F32), name="merge", compiler_params=_cparams(1))(*ys, proj, bm, wb, wo, x)


def _merge_bwd_call(ys, proj, bm, wb, wo, dout):
    s_len = proj.shape[0]
    tm = min(s_len, 256)

    def body(ya, yb, yc, yd, lg_ref, bm_ref, wb_ref, wo_ref, do_ref, dya, dyb, dyc, dyd, dlg_ref, dbm_ref, dwb_ref, dwo_ref):
        fn = lambda ys_, lg_, bm_, wb_, wo_: _merge_fn(ys_, lg_, bm_, wb_, wo_)
        _, vjp = jax.vjp(fn, [r[...].astype(F32) for r in (ya, yb, yc, yd)], lg_ref[...], [bm_ref[n:n + 1, :] for n in range(NB)],
                         [[_with_slot(wb_ref[j, n]) for n in range(NB)] for j in range(N_CHIPS)], _with_slot(wo_ref[...]))
        dys, dlg, dbm, dwb, dwo = vjp(do_ref[...])
        dwb, dwo = [[d[1] for d in row] for row in dwb], dwo[1]
        for r, d in zip((dya, dyb, dyc, dyd), dys):
            r[...] = d
        dlg_ref[...] = dlg.astype(dlg_ref.dtype)

        @pl.when(pl.program_id(0) == 0)
        def _():
            for r in (dbm_ref, dwb_ref, dwo_ref):
                r[...] = jnp.zeros_like(r)
        for n in range(NB):
            dbm_ref[n:n + 1, :] += dbm[n]
            for j in range(N_CHIPS):
                dwb_ref[j, n] += dwb[j][n]
        dwo_ref[...] += dwo

    row = _bs((tm, BW), lambda i: (i, 0))
    sd = jax.ShapeDtypeStruct
    wb_shape = (N_CHIPS, NB, BW, D // N_CHIPS)
    return pl.pallas_call(
        body, grid=(s_len // tm,), in_specs=_merge_specs(tm) + [_bs((tm, D), lambda i: (i, 0))],
        out_specs=[row, row, row, row, _bs((tm, NB * D), lambda i: (i, 0)), _bs((NB, D), lambda i: (0, 0)),
                   _bs(wb_shape, lambda i: (0, 0, 0, 0)), _bs((D, D), lambda i: (0, 0))],
        out_shape=[sd((s_len, BW), F32)] * 4 + [sd((s_len, NP), MM), sd((NB, D), F32), sd(wb_shape, F32), sd((D, D), F32)],
        name="merge_bwd", compiler_params=_cparams(1))(*ys, proj, bm, wb, wo, dout)


def _dh_call(dproj, w, x, g, dout, after=()):
    s_len = x.shape[0]
    tk = NP // 4
    after = list(after)

    def matmul_body(dp_ref, w_ref, *rest):
        o_ref = rest[-1]

        @pl.when(pl.program_id(0) == 0)
        def _():
            o_ref[...] = jnp.zeros_like(o_ref)
        o_ref[...] += lax.dot_general(dp_ref[...], w_ref[...], (((1,), (1,)), ((), ())), preferred_element_type=F32)

    dh = pl.pallas_call(
        matmul_body, grid=(NP // tk,),
        in_specs=[_bs((s_len, tk), lambda k: (0, k)), _bs((D, tk), lambda k: (0, k))] + [_ANY] * len(after),
        out_specs=_bs((s_len, D), lambda k: (0, 0)), out_shape=jax.ShapeDtypeStruct((s_len, D), F32),
        name="dh", compiler_params=_cparams(1))(dproj, w, *after)

    tm = min(s_len, 512)

    def norm_body(dh_ref, x_ref, g_ref, do_ref, dx_ref, dg_ref):
        _, vjp = jax.vjp(lambda x_, g_: _rms_n(x_, g_, D), x_ref[...], g_ref[...])
        dxr, dgr = vjp(dh_ref[...])
        dx_ref[...] = do_ref[...] + dxr

        @pl.when(pl.program_id(0) == 0)
        def _():
            dg_ref[...] = jnp.zeros_like(dg_ref)
        dg_ref[...] += dgr

    row = _bs((tm, D), lambda i: (i, 0))
    return pl.pallas_call(
        norm_body, grid=(s_len // tm,), in_specs=[row, row, _bs((1, D), lambda i: (0, 0)), row],
        out_specs=[row, _bs((1, D), lambda i: (0, 0))],
        out_shape=[jax.ShapeDtypeStruct((s_len, D), F32), jax.ShapeDtypeStruct((1, D), F32)],
        name="norm_bwd", compiler_params=_cparams(1))(dh, x, g, dout)


def _dw_call(h, dproj, after=()):
    s_len = h.shape[0]
    tn = 512
    after = list(after)

    def body(h_ref, dp_ref, *rest):
        o_ref, ht_ref = rest[-2], rest[-1]

        @pl.when(pl.program_id(0) == 0)
        def _():
            ht_ref[...] = h_ref[...].T
        o_ref[...] = jnp.dot(ht_ref[...], dp_ref[...], preferred_element_type=F32)

    return pl.pallas_call(
        body, grid=(NP // tn,),
        in_specs=[_bs((s_len, D), lambda j: (0, 0)), _bs((s_len, tn), lambda j: (0, j))] + [_ANY] * len(after),
        out_specs=_bs((D, tn), lambda j: (0, j)), out_shape=jax.ShapeDtypeStruct((D, NP), F32),
        scratch_shapes=[pltpu.VMEM((D, s_len), h.dtype)], name="dw_in", compiler_params=_cparams(1))(h, dproj, *after)


def _loss_call(y, target):
    s_len = y.shape[0]
    tm = min(s_len, 512)

    def body(y_ref, t_ref, dy_ref, l_ref):
        e = y_ref[...] - t_ref[...]
        dy_ref[...] = e * (1.0 / D)

        @pl.when(pl.program_id(0) == 0)
        def _():
            l_ref[...] = jnp.zeros_like(l_ref)
        l_ref[...] += jnp.sum(e * e, axis=0, keepdims=True)

    row = _bs((tm, D), lambda i: (i, 0))
    return pl.pallas_call(
        body, grid=(s_len // tm,), in_specs=[row, row], out_specs=[row, _bs((1, D), lambda i: (0, 0))],
        out_shape=[jax.ShapeDtypeStruct((s_len, D), F32), jax.ShapeDtypeStruct((1, D), F32)],
        name="loss", compiler_params=_cparams(1))(y, target)


def _adamw_call(w, g, m, v, name):
    rows, cols = w.shape
    tr = min(_row_tile(rows), 128)

    def body(w_ref, g_ref, m_ref, v_ref, d_ref, nm_ref, nv_ref):
        gv = g_ref[...]
        m2 = ADAM_B1 * m_ref[...] + (1.0 - ADAM_B1) * gv
        v2 = ADAM_B2 * v_ref[...] + (1.0 - ADAM_B2) * (gv * gv)
        m_hat = m2 / (1.0 - ADAM_B1 ** ADAM_STEP)
        v_hat = v2 / (1.0 - ADAM_B2 ** ADAM_STEP)
        d_ref[...] = -ADAM_LR * (m_hat / (jnp.sqrt(v_hat) + ADAM_EPS) + ADAM_WD * w_ref[...])
        nm_ref[...] = m2
        nv_ref[...] = v2

    blk = _bs((tr, cols), lambda i: (i, 0))
    return pl.pallas_call(
        body, grid=(rows // tr,), in_specs=[blk] * 4, out_specs=[blk] * 3,
        out_shape=[jax.ShapeDtypeStruct((rows, cols), F32)] * 3, name=name, compiler_params=_cparams(1))(w, g, m, v)


def _adamw_layer_call(layer, ws, gs, ms, vs, prev, after, name, steps=8):
    n = len(ws)
    after = list(after)
    n_prev = 4 * n if prev is not None else 0

    def body(*refs):
        outs = refs[len(refs) - 4 * n:]
        for t in range(n):
            w_ref, g_ref, m_ref, v_ref = refs[t], refs[n + t], refs[2 * n + t], refs[3 * n + t]
            g_out, d_out, m_out, v_out = outs[4 * t:4 * t + 4]
            gv = g_ref[...]
            m2 = ADAM_B1 * m_ref[0] + (1.0 - ADAM_B1) * gv
            v2 = ADAM_B2 * v_ref[0] + (1.0 - ADAM_B2) * (gv * gv)
            m_hat = m2 / (1.0 - ADAM_B1 ** ADAM_STEP)
            v_hat = v2 / (1.0 - ADAM_B2 ** ADAM_STEP)
            g_out[0] = gv
            d_out[0] = -ADAM_LR * (m_hat / (jnp.sqrt(v_hat) + ADAM_EPS) + ADAM_WD * w_ref[0])
            m_out[0] = m2
            v_out[0] = v2

    def lay(a):
        return _bs((1, a.shape[1] // steps, a.shape[2]), lambda i: (layer, i, 0))

    in_specs = ([lay(a) for a in ws] + [_bs((g.shape[0] // steps, g.shape[1]), lambda i: (i, 0)) for g in gs]
                + [lay(a) for a in ms] + [lay(a) for a in vs] + [_ANY] * (n_prev + len(after)))
    return pl.pallas_call(
        body, grid=(steps,), in_specs=in_specs, out_specs=[lay(ws[t]) for t in range(n) for _ in range(4)],
        out_shape=[jax.ShapeDtypeStruct(ws[t].shape, F32) for t in range(n) for _ in range(4)],
        input_output_aliases={4 * n + q: q for q in range(n_prev)}, name=name, compiler_params=_cparams(1),
    )(*ws, *gs, *ms, *vs, *(prev if prev is not None else []), *after)


def _row_tile(rows):
    for cand in (512, 256, 128, 64, 32, 16, 8):
        if rows % cand == 0 and rows > cand:
            return cand
    return rows


def _pair_sum_call(grads, from_sibling, core, name):
    n = len(grads)

    def body(core_ref, *refs):
        for t in range(n):
            refs[2 * n + t][...] = (refs[t][...].astype(F32) + refs[n + t][...].astype(F32)).astype(MM)

    half = lambda g: (1, g.shape[1] // 2, g.shape[2])
    grid_spec = pltpu.PrefetchScalarGridSpec(
        num_scalar_prefetch=1, grid=(N_CHIPS,),
        in_specs=[pl.BlockSpec(half(g), lambda j, core_ref: (j, core_ref[0], 0)) for g in grads]
        + [pl.BlockSpec(half(g), lambda j, core_ref: (j, 0, 0)) for g in grads],
        out_specs=[pl.BlockSpec(half(g), lambda j, core_ref: (j, 0, 0)) for g in grads])
    return pl.pallas_call(
        body, grid_spec=grid_spec, out_shape=[jax.ShapeDtypeStruct((N_CHIPS,) + half(g)[1:], MM) for g in grads], name=name,
        compiler_params=_cparams(1))(core, *grads, *from_sibling)


def _owner_sum_call(chip_sums, from_chips, chip_core, name):
    n = len(chip_sums)
    steps = 4

    def body(ids_ref, *refs):
        for t in range(n):
            a, b = refs[t], refs[n + t]
            refs[2 * n + t][...] = ((a[0].astype(F32) + b[0].astype(F32)) + b[1].astype(F32)) + b[2].astype(F32)

    tile = lambda p: (p.shape[1] // steps, p.shape[2])
    grid_spec = pltpu.PrefetchScalarGridSpec(
        num_scalar_prefetch=1, grid=(steps,),
        in_specs=[pl.BlockSpec((1,) + tile(p), lambda i, ids_ref: (ids_ref[0], i, 0)) for p in chip_sums]
        + [pl.BlockSpec((3,) + tile(p), lambda i, ids_ref: (0, i, 0)) for p in chip_sums],
        out_specs=[pl.BlockSpec(tile(p), lambda i, ids_ref: (ids_ref[1] * steps + i, 0)) for p in chip_sums])
    return pl.pallas_call(
        body, grid_spec=grid_spec, out_shape=[jax.ShapeDtypeStruct((2 * p.shape[1], p.shape[2]), F32) for p in chip_sums],
        name=name, compiler_params=_cparams(1))(chip_core, *chip_sums, *from_chips)


def _sum8_call(parts):
    n, rows, cols = parts.shape
    tr = _row_tile(rows)

    def body(p_ref, o_ref):
        acc = p_ref[0]
        for k in range(1, n):
            acc = acc + p_ref[k]
        o_ref[...] = acc

    return pl.pallas_call(
        body, grid=(rows // tr,), in_specs=[_bs((n, tr, cols), lambda i: (0, i, 0))], out_specs=_bs((tr, cols), lambda i: (i, 0)),
        out_shape=jax.ShapeDtypeStruct((rows, cols), F32), name="sum_small_grads", compiler_params=_cparams(1))(parts)


_ANY = pl.BlockSpec(memory_space=pl.ANY)


def _half_rows(ref, lead, half, which):
    rows = pl.ds(pl.multiple_of(half * which, half), half)
    return ref.at[rows] if lead is None else ref.at[lead, rows]


_HBM = pl.BlockSpec(memory_space=pltpu.HBM)
_SEM = pl.BlockSpec(memory_space=pltpu.SEMAPHORE)
_ORDERED_EFFECT = pltpu.CompilerParams(has_side_effects=pltpu.SideEffectType.DATAFLOW_SIDE_EFFECTING)


_VMEM = pl.BlockSpec(memory_space=pltpu.VMEM)
_TOKEN = jax.ShapeDtypeStruct((8, LANES), F32)


def _in_hbm(a):
    return pltpu.with_memory_space_constraint(a, pltpu.HBM)


def _tie(small, token):
    return small + token[0:1, 0:1].reshape((1,) * small.ndim)


def _peer(k):
    x, y, c = lax.axis_index("x"), lax.axis_index("y"), lax.axis_index("c")
    bx, by, bc = (k >> 2) & 1, (k >> 1) & 1, k & 1
    return (x ^ bx if bx else x, y ^ by if by else y, c ^ bc if bc else c)


def _place_block_call(blk, index, name):
    rows, cols = blk.shape

    def body(idx_ref, b_ref, o_ref):
        o_ref[0] = b_ref[...]

    grid_spec = pltpu.PrefetchScalarGridSpec(
        num_scalar_prefetch=1, grid=(1,), in_specs=[pl.BlockSpec((rows, cols), lambda i, idx_ref: (0, 0))],
        out_specs=pl.BlockSpec((1, rows, cols), lambda i, idx_ref: (idx_ref[0], 0, 0)))
    return pl.pallas_call(body, grid_spec=grid_spec, out_shape=jax.ShapeDtypeStruct((8, rows, cols), blk.dtype), name=name,
                          compiler_params=_cparams(1))(index, blk)


def _small_gather_start_call(blk, buf, after, name):
    after = list(after)

    def body(*refs):
        b_ref, out_ref = refs[0], refs[2 + len(after)]
        send_sems, recv_sems, token = refs[3 + len(after):]
        x, y, c = lax.axis_index("x"), lax.axis_index("y"), lax.axis_index("c")
        for k in range(1, 8):
            pltpu.make_async_remote_copy(src_ref=b_ref, dst_ref=out_ref.at[4 * x + 2 * y + c], send_sem=send_sems.at[k - 1],
                                         recv_sem=recv_sems.at[k - 1], device_id=_peer(k), device_id_type=MESH_ID).start()
        token[...] = jnp.zeros_like(token)

    dma = pltpu.SemaphoreType.DMA
    return pl.pallas_call(
        body, out_shape=[pltpu.HBM(buf.shape, buf.dtype), dma((7,)), dma((7,)), _TOKEN],
        in_specs=[_HBM, _HBM] + [_ANY] * len(after), out_specs=[_HBM, _SEM, _SEM, _VMEM],
        input_output_aliases={1: 0}, name=name, compiler_params=_ORDERED_EFFECT)(_in_hbm(blk), _in_hbm(buf), *after)


def _small_gather_finish_call(blk, buf, send_sems, recv_sems, after, name):
    after = list(after)

    def body(*refs):
        b_ref, in_ref, send_ref, recv_ref = refs[:4]
        x, y, c = lax.axis_index("x"), lax.axis_index("y"), lax.axis_index("c")
        for k in range(1, 8):
            px, py, pc = _peer(k)
            pltpu.make_async_remote_copy(src_ref=b_ref, dst_ref=in_ref.at[4 * px + 2 * py + pc], send_sem=send_ref.at[k - 1],
                                         recv_sem=recv_ref.at[k - 1], device_id=(px, py, pc), device_id_type=MESH_ID).wait()

    return pl.pallas_call(
        body, out_shape=pltpu.HBM(buf.shape, buf.dtype), in_specs=[_HBM, _HBM, _SEM, _SEM] + [_ANY] * len(after),
        out_specs=_HBM, input_output_aliases={1: 0}, name=name, compiler_params=_ORDERED_EFFECT,
    )(_in_hbm(blk), buf, send_sems, recv_sems, *after)


def _pair_exchange_start_call(grads, name):
    n = len(grads)
    half = [g.shape[1] // 2 for g in grads]

    def body(*refs):
        srcs, outs = refs[:n], refs[n:2 * n]
        send_sems, recv_sems, token = refs[2 * n:]
        x, y, c = lax.axis_index("x"), lax.axis_index("y"), lax.axis_index("c")
        for t in range(n):
            pltpu.make_async_remote_copy(
                src_ref=srcs[t].at[:, pl.ds(pl.multiple_of(half[t] * (1 - c), half[t]), half[t])], dst_ref=outs[t],
                send_sem=send_sems.at[t], recv_sem=recv_sems.at[t], device_id=(x, y, 1 - c), device_id_type=MESH_ID).start()
        token[...] = jnp.zeros_like(token)

    dma = pltpu.SemaphoreType.DMA
    return pl.pallas_call(
        body, out_shape=[pltpu.HBM((g.shape[0], g.shape[1] // 2, g.shape[2]), g.dtype) for g in grads] + [dma((n,)), dma((n,)), _TOKEN],
        in_specs=[_HBM] * n, out_specs=[_HBM] * n + [_SEM, _SEM, _VMEM], name=name, compiler_params=_ORDERED_EFFECT,
    )(*[_in_hbm(g) for g in grads])


def _pair_exchange_finish_call(grads, bufs, send_sems, recv_sems, after, name):
    n = len(grads)
    after = list(after)
    half = [g.shape[1] // 2 for g in grads]

    def body(*refs):
        srcs, ins, send_ref, recv_ref = refs[:n], refs[n:2 * n], refs[2 * n], refs[2 * n + 1]
        x, y, c = lax.axis_index("x"), lax.axis_index("y"), lax.axis_index("c")
        for t in range(n):
            pltpu.make_async_remote_copy(
                src_ref=srcs[t].at[:, pl.ds(pl.multiple_of(half[t] * (1 - c), half[t]), half[t])], dst_ref=ins[t],
                send_sem=send_ref.at[t], recv_sem=recv_ref.at[t], device_id=(x, y, 1 - c), device_id_type=MESH_ID).wait()

    return pl.pallas_call(
        body, out_shape=[pltpu.HBM(b.shape, b.dtype) for b in bufs],
        in_specs=[_HBM] * (2 * n) + [_SEM, _SEM] + [_ANY] * len(after), out_specs=[_HBM] * n,
        input_output_aliases={n + t: t for t in range(n)}, name=name, compiler_params=_ORDERED_EFFECT,
    )(*[_in_hbm(g) for g in grads], *bufs, send_sems, recv_sems, *after)


def _chip_scatter_start_call(chip_sums, name):
    n = len(chip_sums)

    def body(*refs):
        srcs, outs = refs[:n], refs[n:2 * n]
        send_sems, recv_sems, token = refs[2 * n:]
        x, y, c = lax.axis_index("x"), lax.axis_index("y"), lax.axis_index("c")
        chips = [(1 - x, y), (x, 1 - y), (1 - x, 1 - y)]
        for k, (cx, cy) in enumerate(chips):
            for t in range(n):
                pltpu.make_async_remote_copy(
                    src_ref=srcs[t].at[2 * cx + cy], dst_ref=outs[t].at[k], send_sem=send_sems.at[3 * t + k],
                    recv_sem=recv_sems.at[3 * t + k], device_id=(cx, cy, c), device_id_type=MESH_ID).start()
        token[...] = jnp.zeros_like(token)

    dma = pltpu.SemaphoreType.DMA
    return pl.pallas_call(
        body, out_shape=[pltpu.HBM((3,) + p.shape[1:], p.dtype) for p in chip_sums] + [dma((3 * n,)), dma((3 * n,)), _TOKEN],
        in_specs=[_HBM] * n, out_specs=[_HBM] * n + [_SEM, _SEM, _VMEM], name=name, compiler_params=_ORDERED_EFFECT,
    )(*[_in_hbm(p) for p in chip_sums])


def _chip_scatter_finish_call(chip_sums, bufs, send_sems, recv_sems, after, name):
    n = len(chip_sums)
    after = list(after)

    def body(*refs):
        srcs, ins, send_ref, recv_ref = refs[:n], refs[n:2 * n], refs[2 * n], refs[2 * n + 1]
        x, y, c = lax.axis_index("x"), lax.axis_index("y"), lax.axis_index("c")
        chips = [(1 - x, y), (x, 1 - y), (1 - x, 1 - y)]
        for k, (cx, cy) in enumerate(chips):
            for t in range(n):
                pltpu.make_async_remote_copy(
                    src_ref=srcs[t].at[2 * cx + cy], dst_ref=ins[t].at[k], send_sem=send_ref.at[3 * t + k],
                    recv_sem=recv_ref.at[3 * t + k], device_id=(cx, cy, c), device_id_type=MESH_ID).wait()

    return pl.pallas_call(
        body, out_shape=[pltpu.HBM(b.shape, b.dtype) for b in bufs],
        in_specs=[_HBM] * (2 * n) + [_SEM, _SEM] + [_ANY] * len(after), out_specs=[_HBM] * n,
        input_output_aliases={n + t: t for t in range(n)}, name=name, compiler_params=_ORDERED_EFFECT,
    )(*[_in_hbm(p) for p in chip_sums], *bufs, send_sems, recv_sems, *after)


def _place_own_call(mine, chip_core, name):
    n = len(mine)

    def body(ids_ref, *refs):
        for t in range(n):
            refs[n + t][0] = refs[t][...]

    def imap_out(s):
        pad = (0,) * (s.ndim - 1)
        return lambda i, ids_ref: (ids_ref[0], ids_ref[1]) + pad

    grid_spec = pltpu.PrefetchScalarGridSpec(
        num_scalar_prefetch=1, grid=(1,), in_specs=[pl.BlockSpec(s.shape, lambda i, ids_ref, k=s.ndim: (0,) * k) for s in mine],
        out_specs=[pl.BlockSpec((1,) + s.shape, imap_out(s)) for s in mine])
    return pl.pallas_call(
        body, grid_spec=grid_spec,
        out_shape=[jax.ShapeDtypeStruct((N_CHIPS, 2 * s.shape[0]) + s.shape[1:], s.dtype) for s in mine],
        name=name, compiler_params=_cparams(1))(chip_core, *mine)


def _gather_start_call(mine, bufs, after, name):
    n = len(mine)
    half = [s.shape[0] for s in mine]

    def body(*refs):
        srcs, outs = refs[:n], refs[2 * n + 1:3 * n + 1]
        send_sems, recv_sib, recv_ici, token = refs[3 * n + 1:]
        x, y, c = lax.axis_index("x"), lax.axis_index("y"), lax.axis_index("c")
        chips = [(1 - x, y), (x, 1 - y), (1 - x, 1 - y)]
        for t in range(n):
            dst = _half_rows(outs[t], 2 * x + y, half[t], c)
            pltpu.make_async_remote_copy(src_ref=srcs[t], dst_ref=dst, send_sem=send_sems.at[4 * t], recv_sem=recv_sib.at[t],
                                         device_id=(x, y, 1 - c), device_id_type=MESH_ID).start()
            for j, chip in enumerate(chips):
                pltpu.make_async_remote_copy(src_ref=srcs[t], dst_ref=dst, send_sem=send_sems.at[4 * t + 1 + j],
                                             recv_sem=recv_ici.at[3 * t + j], device_id=(*chip, c), device_id_type=MESH_ID).start()
        token[...] = jnp.zeros_like(token)

    dma = pltpu.SemaphoreType.DMA
    return pl.pallas_call(
        body, out_shape=[pltpu.HBM(b.shape, b.dtype) for b in bufs] + [dma((4 * n,)), dma((n,)), dma((3 * n,)), _TOKEN],
        in_specs=[_HBM] * (2 * n) + [_ANY], out_specs=[_HBM] * n + [_SEM] * 3 + [_VMEM],
        input_output_aliases={n + t: t for t in range(n)}, name=name, compiler_params=_ORDERED_EFFECT,
    )(*[_in_hbm(s) for s in mine], *[_in_hbm(b) for b in bufs], after)


def _gather_forward_call(bufs, recv_ici, after, name):
    n = len(bufs)
    half = [b.shape[1] // 2 for b in bufs]

    def body(*refs):
        ins, recv_ici_ref = refs[:n], refs[n]
        outs = refs[n + 2:2 * n + 2]
        send_fwd, recv_fwd, token = refs[2 * n + 2:]
        x, y, c = lax.axis_index("x"), lax.axis_index("y"), lax.axis_index("c")
        chips = [(1 - x, y), (x, 1 - y), (1 - x, 1 - y)]
        for j, (cx, cy) in enumerate(chips):
            for t in range(n):
                landed = _half_rows(ins[t], 2 * cx + cy, half[t], c)
                dst = _half_rows(outs[t], 2 * cx + cy, half[t], c)
                pltpu.make_async_remote_copy(src_ref=landed, dst_ref=landed, send_sem=send_fwd.at[3 * t + j],
                                             recv_sem=recv_ici_ref.at[3 * t + j], device_id=(cx, cy, c),
                                             device_id_type=MESH_ID).wait_recv()
                pltpu.make_async_remote_copy(src_ref=landed, dst_ref=dst, send_sem=send_fwd.at[3 * t + j],
                                             recv_sem=recv_fwd.at[3 * t + j], device_id=(x, y, 1 - c),
                                             device_id_type=MESH_ID).start()
        token[...] = jnp.zeros_like(token)

    dma = pltpu.SemaphoreType.DMA
    return pl.pallas_call(
        body, out_shape=[pltpu.HBM(b.shape, b.dtype) for b in bufs] + [dma((3 * n,)), dma((3 * n,)), _TOKEN],
        in_specs=[_HBM] * n + [_SEM, _ANY], out_specs=[_HBM] * n + [_SEM] * 2 + [_VMEM],
        input_output_aliases={t: t for t in range(n)}, name=name, compiler_params=_ORDERED_EFFECT,
    )(*bufs, recv_ici, after)


def _gather_finish_call(shards, bufs, send_sems, recv_sib, send_fwd, recv_fwd, after, name):
    n = len(bufs)
    half = [b.shape[1] // 2 for b in bufs]

    def body(*refs):
        srcs, ins = refs[:n], refs[n:2 * n]
        send_ref, recv_sib_ref, send_fwd_ref, recv_fwd_ref = refs[2 * n:2 * n + 4]
        x, y, c = lax.axis_index("x"), lax.axis_index("y"), lax.axis_index("c")
        chips = [(1 - x, y), (x, 1 - y), (1 - x, 1 - y)]
        sibling = (x, y, 1 - c)
        for t in range(n):
            for k in range(4):
                pltpu.make_async_remote_copy(src_ref=srcs[t], dst_ref=srcs[t], send_sem=send_ref.at[4 * t + k],
                                             recv_sem=recv_sib_ref.at[t], device_id=sibling, device_id_type=MESH_ID).wait_send()
            from_sibling = _half_rows(ins[t], 2 * x + y, half[t], 1 - c)
            pltpu.make_async_remote_copy(src_ref=from_sibling, dst_ref=from_sibling, send_sem=send_ref.at[4 * t],
                                         recv_sem=recv_sib_ref.at[t], device_id=sibling, device_id_type=MESH_ID).wait_recv()
            for j, (cx, cy) in enumerate(chips):
                sent = _half_rows(ins[t], 2 * cx + cy, half[t], c)
                passed = _half_rows(ins[t], 2 * cx + cy, half[t], 1 - c)
                pltpu.make_async_remote_copy(src_ref=sent, dst_ref=passed, send_sem=send_fwd_ref.at[3 * t + j],
                                             recv_sem=recv_fwd_ref.at[3 * t + j], device_id=sibling, device_id_type=MESH_ID).wait()

    return pl.pallas_call(
        body, out_shape=[pltpu.HBM(b.shape, b.dtype) for b in bufs],
        in_specs=[_HBM] * (2 * n) + [_SEM] * 4 + [_ANY], out_specs=[_HBM] * n,
        input_output_aliases={n + t: t for t in range(n)}, name=name, compiler_params=_ORDERED_EFFECT,
    )(*[_in_hbm(s) for s in shards], *bufs, send_sems, recv_sib, send_fwd, recv_fwd, after)


def _pair_gather_call(bufs, name):
    n = len(bufs)
    half = [b.shape[0] // 2 for b in bufs]

    def body(*refs):
        srcs, outs, send_sems, recv_sems = refs[:n], refs[n:2 * n], refs[2 * n], refs[2 * n + 1]
        x, y, c = lax.axis_index("x"), lax.axis_index("y"), lax.axis_index("c")
        for t in range(n):
            pltpu.make_async_remote_copy(
                src_ref=_half_rows(srcs[t], None, half[t], c), dst_ref=_half_rows(outs[t], None, half[t], c),
                send_sem=send_sems.at[t], recv_sem=recv_sems.at[t], device_id=(x, y, 1 - c), device_id_type=MESH_ID).start()
        for t in range(n):
            pltpu.make_async_remote_copy(
                src_ref=_half_rows(srcs[t], None, half[t], c), dst_ref=_half_rows(outs[t], None, half[t], 1 - c),
                send_sem=send_sems.at[t], recv_sem=recv_sems.at[t], device_id=(x, y, 1 - c), device_id_type=MESH_ID).wait()

    return pl.pallas_call(
        body, out_shape=[jax.ShapeDtypeStruct(b.shape, b.dtype) for b in bufs], in_specs=[_ANY] * n, out_specs=[_ANY] * n,
        input_output_aliases={t: t for t in range(n)},
        scratch_shapes=[pltpu.SemaphoreType.DMA((n,)), pltpu.SemaphoreType.DMA((n,))], name=name)(*bufs)


def _pack_rows(flats, dtype, row_multiple):
    flat = jnp.concatenate([f.reshape(-1).astype(dtype) for f in flats])
    n = flat.shape[0]
    rows = -(-n // PACK_W)
    rows = -(-rows // row_multiple) * row_multiple
    return jnp.pad(flat, (0, rows * PACK_W - n)).reshape(rows, PACK_W)


def _unpack(flat, shapes):
    out, off = [], 0
    for shp in shapes:
        n = math.prod(shp)
        out.append(flat[off:off + n].reshape(shp))
        off += n
    return out


_W_IN_SEGMENTS = ((R_ML, R_END, OFF_ML), (R_SG, R_ML, OFF_SG), (R_CV, R_SGI, OFF_CV), (R_SGI, R_MQ, OFF_SGI), (R_MQ, R_SG, OFF_MQ),
                  (R_CQ, R_CKV, OFF_CQ), (R_CKV, R_KR, OFF_CKV), (R_KR, R_CV, OFF_KR + NOPE))
W_IN_SHARD = R_END // N_CHIPS


def _realign_call(wg):
    tr = 128

    def body(w_ref, o_ref):
        pieces, pos = [], 0
        for r0, r1, a0 in _W_IN_SEGMENTS:
            if a0 > pos:
                pieces.append(jnp.zeros((tr, a0 - pos), o_ref.dtype))
            while r0 < r1:
                j = r0 // W_IN_SHARD
                hi = min(r1, (j + 1) * W_IN_SHARD)
                pieces.append(w_ref[j, :, r0 - j * W_IN_SHARD:hi - j * W_IN_SHARD])
                a0, r0 = a0 + hi - r0, hi
            pos = a0
        pieces.append(jnp.zeros((tr, NP - pos), o_ref.dtype))
        o_ref[...] = jnp.concatenate(pieces, axis=1)

    return pl.pallas_call(
        body, grid=(D // tr,), in_specs=[_bs((N_CHIPS, tr, W_IN_SHARD), lambda i: (0, i, 0))],
        out_specs=_bs((tr, NP), lambda i: (i, 0)), out_shape=jax.ShapeDtypeStruct((D, NP), wg.dtype),
        name="w_in_realign", compiler_params=_cparams(1))(wg)


def _unalign_call(dw, out_dtype):
    tr = 128
    by_ref = sorted(_W_IN_SEGMENTS)

    def body(dw_ref, o_ref):
        for j in range(N_CHIPS):
            lo_j, hi_j = j * W_IN_SHARD, (j + 1) * W_IN_SHARD
            pieces = []
            for r0, r1, a0 in by_ref:
                lo, hi = max(r0, lo_j), min(r1, hi_j)
                if lo < hi:
                    pieces.append(dw_ref[:, a0 + lo - r0:a0 + hi - r0])
            o_ref[j] = jnp.concatenate(pieces, axis=1).astype(o_ref.dtype)

    return pl.pallas_call(
        body, grid=(D // tr,), in_specs=[_bs((tr, NP), lambda i: (i, 0))],
        out_specs=_bs((N_CHIPS, tr, W_IN_SHARD), lambda i: (0, i, 0)),
        out_shape=jax.ShapeDtypeStruct((N_CHIPS, D, W_IN_SHARD), out_dtype), name="w_in_unalign", compiler_params=_cparams(1))(dw)


def _wuq_to_heads(w):
    w3 = w.reshape(QL, H, QKH)
    w3 = jnp.pad(w3, ((0, 0), (0, 0), (0, LANES - QKH)))
    return jnp.transpose(w3, (1, 0, 2))


def _wuq_from_heads(wh):
    return jnp.transpose(wh[:, :, :QKH], (1, 0, 2)).reshape(QL, H * QKH)


def _wukv_to_heads(w):
    w3 = w.reshape(KVL, H, NOPE + VH)
    wkn = jnp.transpose(jnp.pad(w3[:, :, :NOPE], ((0, 0), (0, 0), (0, LANES - NOPE))), (1, 0, 2))
    wv3 = w3[:, :, NOPE:]
    z = jnp.zeros((KVL, VH), w.dtype)
    cols = []
    for h in range(H):
        cols += [wv3[:, h], z] if h % 2 == 0 else [z, wv3[:, h]]
    return wkn, jnp.concatenate(cols, axis=1)


def _wukv_from_heads(wkn, wv):
    kn = jnp.transpose(wkn[:, :, :NOPE], (1, 0, 2))
    vs = jnp.stack([wv[:, LANES * h + VH * (h % 2):LANES * h + VH * (h % 2) + VH] for h in range(H)], axis=1)
    return jnp.concatenate([kn, vs], axis=2).reshape(KVL, H * (NOPE + VH))


def _layer_fwd(x, mem, tabs, p):
    proj, h = _proj_call(x, p["norm_g"], p["w_in"])
    if p.get("late") is not None:
        p = dict(p, **p["late"](proj))
    q, k, v = _mla_prep_call(proj, tabs, p["cq_g"], p["ckv_g"], p["qg"], p["kg"], p["wuq"], p["wkn"], p["wv"])
    ya, attn_o, attn_lse = _attn_call(q, k, v, proj)
    bm = p["bm"]
    if p.get("after_attn") is not None:
        bm = _tie(bm, p["after_attn"](ya))
    yb = _conv_call(proj, p["conv_w"], p["conv_b"])
    yc = _sg_call(proj, p["ln_g"], p["ln_b"], p["ws"], p["bs"])
    mk, mv = _memkv_call(mem, p["mem_g"], p["wm"], p["mkg"])
    yd = _mem_call(proj, mk, mv, p["mqg"])
    out = _merge_call((ya, yb, yc, yd), proj, bm, p["wb"], p["wo"], x)
    return out, dict(p=p, x=x, proj=proj, h=h, q=q, k=k, v=v, attn_o=attn_o, attn_lse=attn_lse, ys=(ya, yb, yc, yd), mk=mk, mv=mv)


def _layer_bwd(dout, mem, tabs, p, sv, start_after=None, on_rest_grads=None, on_grads=None):
    proj = sv["proj"]
    bm = p["bm"] if start_after is None else _tie(p["bm"], start_after)
    dya, dyb, dyc, dyd, dml, dbm, dwb, dwo = _merge_bwd_call(sv["ys"], proj, bm, p["wb"], p["wo"], dout)
    dq, dk, dv, dsg_a = _attn_bwd_call(sv["q"], sv["k"], sv["v"], proj, dya, sv["attn_o"], sv["attn_lse"])
    dlat, dcqg, dckvg, dqg, dkg, dwuq, dwkn, dwv = _mla_prep_bwd_call(
        proj, tabs, p["cq_g"], p["ckv_g"], p["qg"], p["kg"], p["wuq"], p["wkn"], p["wv"], dq, dk, dv)
    dbg, dcg, dxi, dsg_b, dcw, dcb = _conv_bwd_call(proj, p["conv_w"], p["conv_b"], dyb)
    duv, dsg_c, dlg, dlb, dws, dbs = _sg_bwd_call(proj, p["ln_g"], p["ln_b"], p["ws"], p["bs"], dyc)
    dmq, dsg_d, dmk, dmv, dmqg = _mem_bwd_call(proj, sv["mk"], sv["mv"], p["mqg"], dyd)
    dmem_g, dwm, dmkg = _memkv_bwd_call(mem, p["mem_g"], p["wm"], p["mkg"], dmk, dmv)
    grads = dict(cq_norm_g=dcqg[0], ckv_norm_g=dckvg[0], mla_q_norm_g=dqg[0, :QKH], mla_k_norm_g=dkg[0, :QKH],
                 conv_w=dcw, conv_b=dcb[0], sg_ln_g=dlg[0], sg_ln_b=dlb[0], w_spatial=dws, b_spatial=dbs[:, :, 0],
                 mem_norm_g=dmem_g[0], mem_q_norm_g=dmqg[0], mem_k_norm_g=dmkg[0], b_merge=dbm,
                 wuq_heads=dwuq, wkn_heads=dwkn, wv_heads=dwv, w_mem_kv=dwm, w_branch_chips=dwb, w_out=dwo)
    started = [on_rest_grads(grads)] if on_rest_grads is not None else []
    dproj, off = dml, NB * D
    for piece in (dsg_a, dsg_b, dsg_c, dsg_d, dbg, dcg, dxi, duv, dmq, dlat):
        dproj = lax.dynamic_update_slice(dproj, piece, (0, off))
        off += piece.shape[1]
    grads["w_in_aligned"] = _dw_call(sv["h"], dproj, started)
    tokens = on_grads(grads) if on_grads is not None else ()
    dx, dnorm_g = _dh_call(dproj, p["w_in"], sv["x"], p["norm_g"], dout, tokens)
    grads["norm_g"] = dnorm_g[0]
    return dx, grads


def _chips_to_cols(a):
    return jnp.concatenate([a[j] for j in range(N_CHIPS)], axis=1)


def _cols_to_chips(a):
    cols = a.shape[1] // N_CHIPS
    return jnp.stack([a[:, cols * j:cols * (j + 1)] for j in range(N_CHIPS)])


def _layer_params_first(l, rep, w_in_gathered, conv_w, b_merge):
    pad_g = lambda g: jnp.pad(g, (0, LANES - QKH)).reshape(1, LANES)
    return dict(
        norm_g=rep["norm_g"][l].reshape(1, D), w_in=_realign_call(w_in_gathered),
        cq_g=rep["cq_norm_g"][l].reshape(1, QL), ckv_g=rep["ckv_norm_g"][l].reshape(1, KVL),
        qg=pad_g(rep["mla_q_norm_g"][l]), kg=pad_g(rep["mla_k_norm_g"][l]),
        conv_w=conv_w, conv_b=rep["conv_b"][l].reshape(1, CW),
        ln_g=rep["sg_ln_g"][l].reshape(1, SGW), ln_b=rep["sg_ln_b"][l].reshape(1, SGW),
        ws=rep["w_spatial"][l], bs=rep["b_spatial"][l].reshape(SGG, SGC, 1),
        mem_g=rep["mem_norm_g"][l].reshape(1, D),
        mqg=rep["mem_q_norm_g"][l].reshape(1, MHD), mkg=rep["mem_k_norm_g"][l].reshape(1, MHD), bm=b_merge)


def _layer_params_rest(gathered):
    wkn, wv = _wukv_to_heads(_chips_to_cols(gathered["w_ukv"]))
    return dict(wuq=_wuq_to_heads(_chips_to_cols(gathered["w_uq"])), wkn=wkn, wv=wv,
                wm=gathered["w_mem_kv"].reshape(D, 2 * MH * MHD), wb=gathered["w_branch"], wo=gathered["w_out"].reshape(D, D))


def _layer_params(l, rep, gathered, conv_w, b_merge):
    return dict(_layer_params_first(l, rep, gathered["w_in"], conv_w, b_merge), **_layer_params_rest(gathered))


def _forward_backward(x, mem, pos, target, params, bwd_hooks=None):
    tabs = _rope_tables(pos)
    params = list(params)
    saved = []
    act = x
    for l in range(DEPTH):
        if callable(params[l]):
            params[l] = params[l](saved[-1], act)
        act, sv = _layer_fwd(act, mem, tabs, params[l])
        saved.append(sv)
    dy, sq = _loss_call(act, target)
    grads = [None] * DEPTH
    token = None
    for l in reversed(range(DEPTH)):
        hooks = dict(bwd_hooks[l]) if bwd_hooks else {}
        after_layer = hooks.pop("after_layer", None)
        dy, grads[l] = _layer_bwd(dy, mem, tabs, saved[l]["p"], saved[l], start_after=token, **hooks)
        token = after_layer(dy) if after_layer is not None else None
    return sq, dy, grads


_SHARDED_MM = ("w_in", "w_branch", "w_out", "w_mem_kv", "w_uq", "w_ukv")
_SHARDED_F32 = ("conv_w", "b_merge")
_REPLICATED = ("norm_g", "cq_norm_g", "ckv_norm_g", "mla_q_norm_g", "mla_k_norm_g", "conv_b", "sg_ln_g", "sg_ln_b",
               "w_spatial", "b_spatial", "mem_norm_g", "mem_q_norm_g", "mem_k_norm_g")
_ALL_REDUCED = _REPLICATED + _SHARDED_F32
_WEIGHTS = ("norm_g", "w_in", "cq_norm_g", "ckv_norm_g", "w_uq", "w_ukv", "mla_q_norm_g", "mla_k_norm_g", "conv_w", "conv_b",
            "sg_ln_g", "sg_ln_b", "w_spatial", "b_spatial", "mem_norm_g", "w_mem_kv", "mem_q_norm_g", "mem_k_norm_g",
            "b_merge", "w_branch", "w_out")
_SMALL = tuple(n for n in _WEIGHTS if n not in _SHARDED_MM)


class _SmallGather:
    def __init__(self, blk, after, tag):
        self.blk, self.tag = blk, tag
        x, y, c = lax.axis_index("x"), lax.axis_index("y"), lax.axis_index("c")
        own = _place_block_call(blk, (4 * x + 2 * y + c).astype(jnp.int32).reshape(1), tag + "place_own")
        self.buf, self.send, self.recv, self.token = _small_gather_start_call(blk, own, after, tag + "start")

    def finish(self, after):
        return _small_gather_finish_call(self.blk, self.buf, self.send, self.recv, after, self.tag + "finish")


def _small_sharded_weights(w, got):
    names = _SHARDED_F32
    per_chip = [_unpack(got[2 * j].reshape(-1), [w[n].shape for n in names]) for j in range(N_CHIPS)]
    return {n: jnp.concatenate([per_chip[j][t] for j in range(N_CHIPS)], axis=2) for t, n in enumerate(names)}


class _Gather:
    def __init__(self, w, layer, names, after, tag):
        self.names, self.tag = names, tag
        x, y, c = lax.axis_index("x"), lax.axis_index("y"), lax.axis_index("c")
        chip_core = jnp.stack([2 * x + y, c]).astype(jnp.int32)
        halves = [w[n].shape[1] // 2 for n in names]
        self.srcs = [lax.dynamic_slice_in_dim(w[n][layer], c * h, h, axis=0).astype(MM) for n, h in zip(names, halves)]
        k = len(names)
        out = _gather_start_call(self.srcs, _place_own_call(self.srcs, chip_core, tag + "place_own"), after, tag + "start")
        self.bufs, self.send, self.recv_sib, self.recv_ici, self.token = out[:k], out[k], out[k + 1], out[k + 2], out[k + 3]

    def pass_on(self, after):
        k = len(self.names)
        out = _gather_forward_call(self.bufs, self.recv_ici, after, self.tag + "forward")
        self.bufs, self.send_fwd, self.recv_fwd = out[:k], out[k], out[k + 1]
        return out[k + 2]

    def finish(self, after):
        got = _gather_finish_call(self.srcs, self.bufs, self.send, self.recv_sib, self.send_fwd, self.recv_fwd, after,
                                  self.tag + "finish")
        return dict(zip(self.names, got))


class _ReduceScatter:
    SLABS = dict(
        w_in=lambda g: _unalign_call(g["w_in_aligned"], MM),
        w_branch=lambda g: g["w_branch_chips"].reshape(N_CHIPS, NB * BW, D // N_CHIPS),
        w_out=lambda g: g["w_out"].reshape(N_CHIPS, D // N_CHIPS, D),
        w_mem_kv=lambda g: g["w_mem_kv"].reshape(N_CHIPS, D // N_CHIPS, 2 * MH * MHD),
        w_uq=lambda g: _cols_to_chips(_wuq_from_heads(g["wuq_heads"])),
        w_ukv=lambda g: _cols_to_chips(_wukv_from_heads(g["wkn_heads"], g["wv_heads"])))

    def __init__(self, tag, names):
        self.tag, self.names = tag, names

    def exchange(self, grads):
        self.tensors = [self.SLABS[n](grads) for n in self.names]
        n = len(self.tensors)
        out = _pair_exchange_start_call(self.tensors, self.tag + "exchange_start")
        self.ex_bufs, self.ex_send, self.ex_recv = out[:n], out[n], out[n + 1]
        return out[n + 2]

    def scatter(self, after):
        n = len(self.tensors)
        c = lax.axis_index("c")
        from_sibling = _pair_exchange_finish_call(self.tensors, self.ex_bufs, self.ex_send, self.ex_recv, after,
                                                  self.tag + "exchange_finish")
        self.chip_sums = _pair_sum_call(self.tensors, from_sibling, c.astype(jnp.int32).reshape(1), self.tag + "pair_sum")
        out = _chip_scatter_start_call(self.chip_sums, self.tag + "scatter_start")
        self.bufs, self.send_sems, self.recv_sems, self.token = out[:n], out[n], out[n + 1], out[n + 2]
        return self.token

    def finish(self, after):
        x, y, c = lax.axis_index("x"), lax.axis_index("y"), lax.axis_index("c")
        chip_core = jnp.stack([2 * x + y, c]).astype(jnp.int32)
        from_chips = _chip_scatter_finish_call(self.chip_sums, self.bufs, self.send_sems, self.recv_sems, after,
                                               self.tag + "scatter_finish")
        mine = _owner_sum_call(self.chip_sums, from_chips, chip_core, self.tag + "owner_sum")
        return dict(zip(self.names, _pair_gather_call(mine, self.tag + "pair_gather")))


def _small_sums(g, sq, got):
    total = _sum8_call(got).reshape(-1)
    parts = _unpack(total, [g[n].shape for n in _ALL_REDUCED] + [sq.shape])
    out = dict(zip(_ALL_REDUCED, parts))
    sq_total = parts[-1]
    chip = 2 * lax.axis_index("x") + lax.axis_index("y")
    for n in _SHARDED_F32:
        size = out[n].shape[2] // N_CHIPS
        out[n] = lax.dynamic_slice_in_dim(out[n], chip * size, size, axis=2)
    return out, sq_total


def _adamw_small(w, g, m, v):
    delta, new_m, new_v = {}, {}, {}
    shapes = [w[n].shape for n in _SMALL]
    pk = lambda t: _pack_rows([t[n] for n in _SMALL], F32, 64)
    d, nm, nv = _adamw_call(pk(w), pk(g), pk(m), pk(v), "adamw_small")
    for out, packed in ((delta, d), (new_m, nm), (new_v, nv)):
        out.update(zip(_SMALL, _unpack(packed.reshape(-1), shapes)))
    return delta, new_m, new_v


def kernel(x, mem, positions, norm_g, w_in, cq_norm_g, ckv_norm_g, w_uq, w_ukv, mla_q_norm_g, mla_k_norm_g, conv_w, conv_b, sg_ln_g, sg_ln_b, w_spatial, b_spatial, mem_norm_g, w_mem_kv, mem_q_norm_g, mem_k_norm_g, b_merge, w_branch, w_out, loss_target, m_norm_g, m_w_in, m_cq_norm_g, m_ckv_norm_g, m_w_uq, m_w_ukv, m_mla_q_norm_g, m_mla_k_norm_g, m_conv_w, m_conv_b, m_sg_ln_g, m_sg_ln_b, m_w_spatial, m_b_spatial, m_mem_norm_g, m_w_mem_kv, m_mem_q_norm_g, m_mem_k_norm_g, m_b_merge, m_w_branch, m_w_out, v_norm_g, v_w_in, v_cq_norm_g, v_ckv_norm_g, v_w_uq, v_w_ukv, v_mla_q_norm_g, v_mla_k_norm_g, v_conv_w, v_conv_b, v_sg_ln_g, v_sg_ln_b, v_w_spatial, v_b_spatial, v_mem_norm_g, v_w_mem_kv, v_mem_q_norm_g, v_mem_k_norm_g, v_b_merge, v_w_branch, v_w_out):
    w = dict(norm_g=norm_g, w_in=w_in, cq_norm_g=cq_norm_g, ckv_norm_g=ckv_norm_g, w_uq=w_uq, w_ukv=w_ukv,
             mla_q_norm_g=mla_q_norm_g, mla_k_norm_g=mla_k_norm_g, conv_w=conv_w, conv_b=conv_b, sg_ln_g=sg_ln_g,
             sg_ln_b=sg_ln_b, w_spatial=w_spatial, b_spatial=b_spatial, mem_norm_g=mem_norm_g, w_mem_kv=w_mem_kv,
             mem_q_norm_g=mem_q_norm_g, mem_k_norm_g=mem_k_norm_g, b_merge=b_merge, w_branch=w_branch, w_out=w_out)
    m = dict(norm_g=m_norm_g, w_in=m_w_in, cq_norm_g=m_cq_norm_g, ckv_norm_g=m_ckv_norm_g, w_uq=m_w_uq, w_ukv=m_w_ukv,
             mla_q_norm_g=m_mla_q_norm_g, mla_k_norm_g=m_mla_k_norm_g, conv_w=m_conv_w, conv_b=m_conv_b, sg_ln_g=m_sg_ln_g,
             sg_ln_b=m_sg_ln_b, w_spatial=m_w_spatial, b_spatial=m_b_spatial, mem_norm_g=m_mem_norm_g, w_mem_kv=m_w_mem_kv,
             mem_q_norm_g=m_mem_q_norm_g, mem_k_norm_g=m_mem_k_norm_g, b_merge=m_b_merge, w_branch=m_w_branch, w_out=m_w_out)
    v = dict(norm_g=v_norm_g, w_in=v_w_in, cq_norm_g=v_cq_norm_g, ckv_norm_g=v_ckv_norm_g, w_uq=v_w_uq, w_ukv=v_w_ukv,
             mla_q_norm_g=v_mla_q_norm_g, mla_k_norm_g=v_mla_k_norm_g, conv_w=v_conv_w, conv_b=v_conv_b, sg_ln_g=v_sg_ln_g,
             sg_ln_b=v_sg_ln_b, w_spatial=v_w_spatial, b_spatial=v_b_spatial, mem_norm_g=v_mem_norm_g, w_mem_kv=v_w_mem_kv,
             mem_q_norm_g=v_mem_q_norm_g, mem_k_norm_g=v_mem_k_norm_g, b_merge=v_b_merge, w_branch=v_w_branch, w_out=v_w_out)

    chip_core = jnp.stack([2 * lax.axis_index("x") + lax.axis_index("y"), lax.axis_index("c")]).astype(jnp.int32)

    first = _Gather(w, 0, ("w_in",), chip_core, "gather_l0_w_in_")
    rest = _Gather(w, 0, _SHARDED_MM[1:], first.token, "gather_l0_rest_")
    small_on_its_way = _SmallGather(_pack_rows([w[n] for n in _SHARDED_F32], F32, 8), [rest.token], "gather_small_weights_")
    later = _Gather(w, 1, _SHARDED_MM, small_on_its_way.token, "gather_l1_")
    w_in0 = first.finish(first.pass_on(later.token))["w_in"]
    small = {}

    def rest_of_layer0(proj0):
        landed = _layer_params_rest(rest.finish(rest.pass_on(proj0)))
        small.update(_small_sharded_weights(w, small_on_its_way.finish([landed["wo"]])))
        return dict(landed, conv_w=small["conv_w"][0], bm=small["b_merge"][0])

    def layer1_params(saved0, act0):
        return _layer_params(1, w, later.finish(act0), small["conv_w"][1], small["b_merge"][1])

    params0 = _layer_params_first(0, w, w_in0, None, None)
    params = [dict(params0, late=rest_of_layer0, after_attn=later.pass_on), layer1_params]
    others = _SHARDED_MM[1:]
    rs1 = _ReduceScatter("rs_l1_", _SHARDED_MM)
    rs0_rest, rs0_w_in = _ReduceScatter("rs_l0_rest_", others), _ReduceScatter("rs_l0_w_in_", ("w_in",))

    def layer0_grads_done(grads):
        return [rs0_rest.scatter([grads["w_in_aligned"]]), rs0_w_in.exchange(grads)]

    hooks = [dict(on_rest_grads=rs0_rest.exchange, on_grads=layer0_grads_done),
             dict(on_grads=lambda grads: [rs1.exchange(grads)], after_layer=lambda dy: rs1.scatter([dy]))]
    sq, grad_x, layer_grads = _forward_backward(x[0], mem[0], positions[0], loss_target[0], params, hooks)

    g_small = {n: jnp.stack([layer_grads[l][n] for l in range(DEPTH)]) for n in _ALL_REDUCED}
    small_grads = _SmallGather(_pack_rows([g_small[n] for n in _ALL_REDUCED] + [sq], F32, 64), [grad_x], "gather_small_grads_")
    scattering = rs0_w_in.scatter([grad_x, small_grads.token])
    shard_grads = {1: rs1.finish([scattering]), 0: rs0_rest.finish([scattering])}
    as3d = lambda a: a.reshape(DEPTH, -1, a.shape[-1])
    as2d = lambda a: a.reshape(-1, a.shape[-1])
    big = lambda t: [as3d(t[n]) for n in others]
    turned = lambda t: [jnp.swapaxes(t["w_in"], 1, 2)]
    assert W_IN_SHARD % (8 * 7) == 0

    def update_w_in(l, grad, prev):
        return _adamw_layer_call(l, turned(w), [grad.T], turned(m), turned(v), prev, [], "adamw_w_in_l%d" % l, steps=7)

    def update_others(l, prev):
        return _adamw_layer_call(l, big(w), [as2d(shard_grads[l][n]) for n in others], big(m), big(v), prev, [], "adamw_l%d" % l)

    upd = update_others(0, update_others(1, None))
    g, sq_total = _small_sums(g_small, sq, small_grads.finish([upd[0]]))
    loss = 0.5 / D * jnp.sum(sq_total)
    delta, new_m, new_v = _adamw_small(w, g, m, v)
    upd_in1 = update_w_in(1, shard_grads[1]["w_in"], None)
    w_in_grad0 = rs0_w_in.finish([grad_x, upd_in1[0], upd[0], delta["norm_g"]])["w_in"]
    upd_in = update_w_in(0, w_in_grad0, upd_in1)
    g["w_in"], delta["w_in"], new_m["w_in"], new_v["w_in"] = [jnp.swapaxes(a, 1, 2) for a in upd_in]
    for t, n in enumerate(others):
        g[n], delta[n], new_m[n], new_v[n] = [a.reshape(w[n].shape) for a in upd[4 * t:4 * t + 4]]
    return (loss, grad_x[None], *[g[n] for n in _WEIGHTS], *[delta[n] for n in _WEIGHTS],
            *[new_m[n] for n in _WEIGHTS], *[new_v[n] for n in _WEIGHTS])
```

```python
import functools
import math

import jax
import jax.numpy as jnp
from jax import lax
from jax.experimental import pallas as pl
from jax.experimental.pallas import tpu as pltpu

F32 = jnp.float32
MM = jnp.bfloat16

D = 1024
DEPTH = 2
EPS = 1e-6
H = 8
NOPE = 64
ROPE = 32
QKH = 96
VH = 64
QL = 256
KVL = 128
ROPE_THETA = 10000.0
CW = 512
SGW = 512
SGG = 4
SGC = 128
MH = 4
MHD = 128
NB = 4
BW = 512
NEG_INF = -1e30
LANES = 128
N_CHIPS = 4

R_CQ, R_CKV, R_KR, R_CV, R_SGI, R_MQ, R_SG, R_ML, R_END = 0, 256, 384, 416, 1952, 2976, 3488, 5536, 9632
OFF_ML, OFF_SG, OFF_CV, OFF_SGI, OFF_MQ, OFF_CQ, OFF_CKV, OFF_KR, NP = 0, 4096, 6144, 7680, 8704, 9216, 9472, 9600, 9728

ADAM_LR = 0.001
ADAM_B1 = 0.9
ADAM_B2 = 0.999
ADAM_EPS = 1e-08
ADAM_WD = 0.01
ADAM_STEP = 10

VMEM_LIMIT = 56 * 1024 * 1024
PACK_W = 512
MESH_ID = pl.DeviceIdType.MESH


def _cparams(n_axes):
    return pltpu.CompilerParams(dimension_semantics=("arbitrary",) * n_axes, vmem_limit_bytes=VMEM_LIMIT)


def _bs(shape, imap):
    return pl.BlockSpec(shape, imap)


@jax.custom_vjp
def _mm_plain(a, b):
    return jnp.dot(a.astype(MM), b.astype(MM), preferred_element_type=F32)


def _mm_plain_fwd(a, b):
    return _mm_plain(a, b), (a, b)


def _mm_plain_bwd(res, g):
    a, b = res
    gm = g.astype(MM)
    da = lax.dot_general(gm, b.astype(MM), (((1,), (1,)), ((), ())), preferred_element_type=F32)
    db = lax.dot_general(a.astype(MM), gm, (((0,), (0,)), ((), ())), preferred_element_type=F32)
    return da.astype(a.dtype), db.astype(b.dtype)


_mm_plain.defvjp(_mm_plain_fwd, _mm_plain_bwd)


@jax.custom_vjp
def _mm_slot(a, w, slot):
    return jnp.dot(a.astype(MM), w.astype(MM), preferred_element_type=F32)


def _mm_slot_fwd(a, w, slot):
    return _mm_slot(a, w, slot), (a, w)


def _mm_slot_bwd(res, g):
    a, w = res
    gm = g.astype(MM)
    da = lax.dot_general(gm, w.astype(MM), (((1,), (1,)), ((), ())), preferred_element_type=F32)
    dw = lax.dot_general(a.astype(MM), gm, (((0,), (0,)), ((), ())), preferred_element_type=F32)
    return da.astype(a.dtype), jnp.zeros_like(w), dw


_mm_slot.defvjp(_mm_slot_fwd, _mm_slot_bwd)


def _mm(a, b):
    if isinstance(b, tuple):
        return _mm_slot(a, b[0], b[1])
    return _mm_plain(a, b)


def _with_slot(w):
    return (w, jnp.zeros(w.shape, F32))


@jax.custom_vjp
def _mm_nt(a, b):
    return lax.dot_general(a.astype(MM), b.astype(MM), (((1,), (1,)), ((), ())), preferred_element_type=F32)


def _mm_nt_fwd(a, b):
    return _mm_nt(a, b), (a, b)


def _mm_nt_bwd(res, g):
    a, b = res
    gm = g.astype(MM)
    da = jnp.dot(gm, b.astype(MM), preferred_element_type=F32)
    db = lax.dot_general(gm, a.astype(MM), (((0,), (0,)), ((), ())), preferred_element_type=F32)
    return da.astype(a.dtype), db.astype(b.dtype)


_mm_nt.defvjp(_mm_nt_fwd, _mm_nt_bwd)


@functools.partial(jax.custom_vjp, nondiff_argnums=(1,))
def _lane_roll(x, shift):
    return pltpu.roll(x, shift, 1)


def _lane_roll_fwd(x, shift):
    return pltpu.roll(x, shift, 1), None


def _lane_roll_bwd(shift, _, g):
    return (pltpu.roll(g, (LANES - shift) % LANES, 1),)


_lane_roll.defvjp(_lane_roll_fwd, _lane_roll_bwd)


def _rms_n(x, g, n):
    ms = jnp.sum(x * x, axis=-1, keepdims=True) * (1.0 / n)
    return x * lax.rsqrt(ms + EPS) * g


def _softmax(s):
    m = jnp.max(s, axis=-1, keepdims=True)
    e = jnp.exp(s - m)
    return e / jnp.sum(e, axis=-1, keepdims=True)


def _rope(t, cos_t, sin_a, sin_b):
    return t * cos_t + _lane_roll(t, LANES - 16) * sin_a + _lane_roll(t, 16) * sin_b


def _mla_prep_fn(cq, ckv, kr, cos_t, sin_a, sin_b, cq_g, ckv_g, qg, kg, wuq, wkn, wv):
    cqn = _rms_n(cq, cq_g, QL)
    ckvn = _rms_n(ckv, ckv_g, KVL)
    lane = lax.broadcasted_iota(jnp.int32, kr.shape, 1)
    krm = jnp.where((lane >= NOPE) & (lane < QKH), kr, 0.0)
    qs, ks = [], []
    for h in range(H):
        qh = _rms_n(_mm(cqn, wuq[h]), qg, QKH)
        qs.append(_rope(qh, cos_t, sin_a, sin_b) * (QKH ** -0.5))
        kh = _rms_n(_mm(ckvn, wkn[h]) + krm, kg, QKH)
        ks.append(_rope(kh, cos_t, sin_a, sin_b))
    return jnp.concatenate(qs, axis=-1), jnp.concatenate(ks, axis=-1), _mm(ckvn, wv)


def _dot_nt(a, b):
    return lax.dot_general(a.astype(MM), b.astype(MM), (((1,), (1,)), ((), ())), preferred_element_type=F32)


def _dot_tn(a, b):
    return lax.dot_general(a.astype(MM), b.astype(MM), (((0,), (0,)), ((), ())), preferred_element_type=F32)


def _causal_scores(qe, ke):
    tq, kl = qe.shape[0], ke.shape[0]
    s = _dot_nt(qe, ke)
    rows = lax.broadcasted_iota(jnp.int32, (tq, tq), 0)
    cols = lax.broadcasted_iota(jnp.int32, (tq, tq), 1)
    own = jnp.where(cols <= rows, s[:, kl - tq:], NEG_INF)
    return own if kl == tq else jnp.concatenate([s[:, :kl - tq], own], axis=1)


def _head_lanes(e, shape):
    lane = lax.broadcasted_iota(jnp.int32, shape, len(shape) - 1)
    return (lane >= VH * e) & (lane < VH * (e + 1))


def _attn_pair_fwd(q2, k2, v2):
    tq = q2.shape[0]
    o = jnp.zeros((tq, LANES), F32)
    lse = jnp.zeros((tq, LANES), F32)
    for e in range(2):
        sl = slice(LANES * e, LANES * (e + 1))
        s = _causal_scores(q2[:, sl], k2[:, sl])
        m = jnp.max(s, axis=-1, keepdims=True)
        ex = jnp.exp(s - m)
        l = jnp.sum(ex, axis=-1, keepdims=True)
        ve = jnp.where(_head_lanes(e, v2[:, sl].shape), v2[:, sl], 0.0)
        o = o + jnp.dot((ex * (1.0 / l)).astype(MM), ve.astype(MM), preferred_element_type=F32)
        lse = jnp.where(_head_lanes(e, lse.shape), m + jnp.log(l), lse)
    return o, lse


def _attn_pair_bwd(q2, k2, v2, sg, dys, o, lse):
    sig = jax.nn.sigmoid(sg)
    do = dys * (sg * sig)
    dsg = dys * o * (sig * (1.0 + sg * (1.0 - sig)))
    dqs, dks, dvs = [], [], []
    for e in range(2):
        sl = slice(LANES * e, LANES * (e + 1))
        qe, ke = q2[:, sl], k2[:, sl]
        hm = _head_lanes(e, o.shape)
        lse_e = jnp.max(jnp.where(hm, lse, NEG_INF), axis=-1, keepdims=True)
        do_e = jnp.where(hm, do, 0.0)
        delta = jnp.sum(do_e * o, axis=-1, keepdims=True)
        p = jnp.exp(_causal_scores(qe, ke) - lse_e)
        ve = jnp.where(_head_lanes(e, v2[:, sl].shape), v2[:, sl], 0.0)
        dvs.append(_dot_tn(p, do_e))
        ds = p * (_dot_nt(do_e, ve) - delta)
        dqs.append(jnp.dot(ds.astype(MM), ke.astype(MM), preferred_element_type=F32))
        dks.append(_dot_tn(ds, qe))
    return jnp.concatenate(dqs, axis=-1), jnp.concatenate(dks, axis=-1), jnp.concatenate(dvs, axis=-1), dsg


def _sg_fn(u, v, sgc, ln_g, ln_b, ws, bs):
    mu = jnp.mean(v, axis=-1, keepdims=True)
    xc = v - mu
    vn = xc * lax.rsqrt(jnp.mean(xc * xc, axis=-1, keepdims=True) + EPS) * ln_g + ln_b
    r = lax.broadcasted_iota(jnp.int32, (SGC, SGC), 0)
    c = lax.broadcasted_iota(jnp.int32, (SGC, SGC), 1)
    wt = [jnp.where(r >= c, w, 0.0) for w in ws]
    row_blocks = []
    for ch in range(u.shape[0] // SGC):
        col_blocks = []
        for g in range(SGG):
            blk = vn[SGC * ch:SGC * (ch + 1), LANES * g:LANES * (g + 1)]
            col_blocks.append(_mm(wt[g], blk) + bs[g])
        row_blocks.append(jnp.concatenate(col_blocks, axis=-1))
    mixed = jnp.concatenate(row_blocks, axis=0)
    return (u * mixed) * jax.nn.silu(sgc)


def _memkv_fn(mem, mem_g, wm, kg):
    kv = _mm(_rms_n(mem, mem_g, D), wm)
    ks = [_rms_n(kv[:, MHD * h:MHD * (h + 1)], kg, MHD) for h in range(MH)]
    return jnp.concatenate(ks, axis=-1), kv[:, MH * MHD:]


def _mem_fn(mq, sgd, k, v, qg):
    outs = []
    for h in range(MH):
        sl = slice(MHD * h, MHD * (h + 1))
        qh = _rms_n(mq[:, sl], qg, MHD)
        p = _softmax(_mm_nt(qh, k[:, sl]) * (MHD ** -0.5))
        outs.append(_mm(p, v[:, sl]))
    return jnp.concatenate(outs, axis=-1) * jax.nn.silu(sgd)


def _merge_fn(ys, logits, bm, wb, wo):
    merged = None
    for n in range(NB):
        z = jnp.concatenate([_mm(ys[n], wb[j][n]) for j in range(N_CHIPS)], axis=-1)
        gate = jax.nn.sigmoid(logits[:, D * n:D * (n + 1)] + bm[n])
        merged = gate * z if merged is None else merged + gate * z
    return _mm(merged, wo)


def _proj_call(x, g, w):
    s_len = x.shape[0]
    tm, tn = s_len, 512

    def body(x_ref, g_ref, w_ref, p_ref, h_ref):
        @pl.when(pl.program_id(1) == 0)
        def _():
            h_ref[...] = _rms_n(x_ref[...], g_ref[...], D).astype(h_ref.dtype)
        p_ref[...] = jnp.dot(h_ref[...], w_ref[...], preferred_element_type=F32)

    return pl.pallas_call(
        body, grid=(s_len // tm, NP // tn),
        in_specs=[_bs((tm, D), lambda i, j: (i, 0)), _bs((1, D), lambda i, j: (0, 0)), _bs((D, tn), lambda i, j: (0, j))],
        out_specs=[_bs((tm, tn), lambda i, j: (i, j)), _bs((tm, D), lambda i, j: (i, 0))],
        out_shape=[jax.ShapeDtypeStruct((s_len, NP), F32), jax.ShapeDtypeStruct((s_len, D), MM)],
        name="proj", compiler_params=_cparams(2))(x, g, w)


def _rope_tables(pos):
    half = ROPE // 2
    inv_freq = ROPE_THETA ** (-jnp.arange(half, dtype=F32) / half)
    ang = pos.astype(F32)[:, None] * inv_freq
    cos, sin = jnp.cos(ang), jnp.sin(ang)
    s_len = pos.shape[0]
    z = lambda n: jnp.zeros((s_len, n), F32)
    cos_t = jnp.concatenate([jnp.ones((s_len, NOPE), F32), cos, cos, z(LANES - QKH)], axis=1)
    sin_a = jnp.concatenate([z(NOPE), -sin, z(LANES - NOPE - half)], axis=1)
    sin_b = jnp.concatenate([z(NOPE + half), sin, z(LANES - QKH)], axis=1)
    return cos_t, sin_a, sin_b


def _mla_prep_specs(tm):
    row = lambda w, off: _bs((tm, w), lambda i: (i, off // w))
    full2 = lambda a, b: _bs((a, b), lambda i: (0, 0))
    full3 = lambda a, b, c: _bs((a, b, c), lambda i: (0, 0, 0))
    tab = _bs((tm, LANES), lambda i: (i, 0))
    return [row(QL, OFF_CQ), row(KVL, OFF_CKV), row(LANES, OFF_KR), tab, tab, tab,
            full2(1, QL), full2(1, KVL), full2(1, LANES), full2(1, LANES),
            full3(H, QL, LANES), full3(H, KVL, LANES), full2(KVL, H * LANES)]


def _mla_prep_args(body_refs, wrap=lambda w: w):
    (cq, ckv, kr, ct, sa, sb, cqg, ckvg, qg, kg, wuq, wkn, wv) = body_refs
    return (cq[...], ckv[...], kr[...], ct[...], sa[...], sb[...], cqg[...], ckvg[...], qg[...], kg[...],
            [wrap(wuq[h]) for h in range(H)], [wrap(wkn[h]) for h in range(H)], wrap(wv[...]))


def _mla_prep_call(proj, tabs, cq_g, ckv_g, qg, kg, wuq, wkn, wv):
    s_len = proj.shape[0]
    tm = min(s_len, 256)

    def body(*refs):
        q_ref, k_ref, v_ref = refs[13:]
        q, k, v = _mla_prep_fn(*_mla_prep_args(refs[:13]))
        q_ref[...] = q.astype(q_ref.dtype)
        k_ref[...] = k.astype(k_ref.dtype)
        v_ref[...] = v.astype(v_ref.dtype)

    out = _bs((tm, H * LANES), lambda i: (i, 0))
    return pl.pallas_call(
        body, grid=(s_len // tm,), in_specs=_mla_prep_specs(tm), out_specs=[out, out, out],
        out_shape=[jax.ShapeDtypeStruct((s_len, H * LANES), MM)] * 3,
        name="mla_prep", compiler_params=_cparams(1))(proj, proj, proj, *tabs, cq_g, ckv_g, qg, kg, wuq, wkn, wv)


def _mla_prep_bwd_call(proj, tabs, cq_g, ckv_g, qg, kg, wuq, wkn, wv, dq, dk, dv):
    s_len = proj.shape[0]
    tm = min(s_len, 256)

    def body(*refs):
        dq_ref, dk_ref, dv_ref = refs[13:16]
        dlat_ref, dcqg_ref, dckvg_ref, dqg_ref, dkg_ref, dwuq_ref, dwkn_ref, dwv_ref = refs[16:]
        _, vjp = jax.vjp(_mla_prep_fn, *_mla_prep_args(refs[:13], _with_slot))
        (dcq, dckv, dkr, _, _, _, dcqg, dckvg, dqg, dkg, dwuq, dwkn, dwv) = vjp((dq_ref[...], dk_ref[...], dv_ref[...]))
        dwuq, dwkn, dwv = [d[1] for d in dwuq], [d[1] for d in dwkn], dwv[1]
        dlat_ref[...] = jnp.concatenate([dcq, dckv, dkr], axis=-1).astype(dlat_ref.dtype)

        @pl.when(pl.program_id(0) == 0)
        def _():
            for r in (dcqg_ref, dckvg_ref, dqg_ref, dkg_ref, dwuq_ref, dwkn_ref, dwv_ref):
                r[...] = jnp.zeros_like(r)
        dcqg_ref[...] += dcqg
        dckvg_ref[...] += dckvg
        dqg_ref[...] += dqg
        dkg_ref[...] += dkg
        for h in range(H):
            dwuq_ref[h] += dwuq[h]
            dwkn_ref[h] += dwkn[h]
        dwv_ref[...] += dwv

    big = _bs((tm, H * LANES), lambda i: (i, 0))
    row = lambda w: _bs((tm, w), lambda i: (i, 0))
    full2 = lambda a, b: _bs((a, b), lambda i: (0, 0))
    full3 = lambda a, b, c: _bs((a, b, c), lambda i: (0, 0, 0))
    sd = jax.ShapeDtypeStruct
    return pl.pallas_call(
        body, grid=(s_len // tm,), in_specs=_mla_prep_specs(tm) + [big, big, big],
        out_specs=[row(QL + KVL + LANES), full2(1, QL), full2(1, KVL), full2(1, LANES), full2(1, LANES),
                   full3(H, QL, LANES), full3(H, KVL, LANES), full2(KVL, H * LANES)],
        out_shape=[sd((s_len, QL + KVL + LANES), MM), sd((1, QL), F32), sd((1, KVL), F32),
                   sd((1, LANES), F32), sd((1, LANES), F32), sd((H, QL, LANES), F32), sd((H, KVL, LANES), F32),
                   sd((KVL, H * LANES), F32)],
        name="mla_prep_bwd", compiler_params=_cparams(1))(proj, proj, proj, *tabs, cq_g, ckv_g, qg, kg, wuq, wkn, wv, dq, dk, dv)


def _attn_specs(s_len, tq):
    pair = 2 * LANES
    return [_bs((tq, pair), lambda p, i: (i, p)), _bs((s_len, pair), lambda p, i: (0, p)), _bs((s_len, pair), lambda p, i: (0, p)),
            _bs((tq, LANES), lambda p, i: (i, OFF_SG // LANES + p))]


def _attn_call(q, k, v, proj):
    s_len = q.shape[0]
    tq = min(s_len, 256)

    def body(q_ref, k_ref, v_ref, sg_ref, y_ref, o_ref, lse_ref):
        for n in range(s_len // tq):
            @pl.when(pl.program_id(1) == n)
            def _():
                kl = (n + 1) * tq
                o, lse = _attn_pair_fwd(q_ref[...], k_ref[:kl, :], v_ref[:kl, :])
                y_ref[...] = (o * jax.nn.silu(sg_ref[...])).astype(y_ref.dtype)
                o_ref[...] = o
                lse_ref[...] = lse

    tile = _bs((tq, LANES), lambda p, i: (i, p))
    sd = jax.ShapeDtypeStruct
    return pl.pallas_call(
        body, grid=(H // 2, s_len // tq), in_specs=_attn_specs(s_len, tq), out_specs=[tile, tile, tile],
        out_shape=[sd((s_len, BW), MM), sd((s_len, BW), F32), sd((s_len, BW), F32)],
        name="attn", compiler_params=_cparams(2))(q, k, v, proj)


def _attn_bwd_call(q, k, v, proj, dys, o, lse):
    s_len = q.shape[0]
    tq = min(s_len, 256)
    pair = 2 * LANES

    def body(q_ref, k_ref, v_ref, sg_ref, dy_ref, o_ref, lse_ref, dq_ref, dk_ref, dv_ref, dsg_ref):
        i = pl.program_id(1)

        @pl.when(i == 0)
        def _():
            dk_ref[...] = jnp.zeros_like(dk_ref)
            dv_ref[...] = jnp.zeros_like(dv_ref)

        for n in range(s_len // tq):
            @pl.when(i == n)
            def _():
                kl = (n + 1) * tq
                dq, dk, dv, dsg = _attn_pair_bwd(q_ref[...], k_ref[:kl, :], v_ref[:kl, :], sg_ref[...], dy_ref[...],
                                                 o_ref[...], lse_ref[...])
                dq_ref[...] = dq
                dsg_ref[...] = dsg.astype(dsg_ref.dtype)
                dk_ref[:kl, :] += dk
                dv_ref[:kl, :] += dv

    sd = jax.ShapeDtypeStruct
    tile = _bs((tq, LANES), lambda p, i: (i, p))
    return pl.pallas_call(
        body, grid=(H // 2, s_len // tq),
        in_specs=_attn_specs(s_len, tq) + [tile, tile, tile],
        out_specs=[_bs((tq, pair), lambda p, i: (i, p)), _bs((s_len, pair), lambda p, i: (0, p)),
                   _bs((s_len, pair), lambda p, i: (0, p)), tile],
        out_shape=[sd((s_len, H * LANES), F32), sd((s_len, H * LANES), F32), sd((s_len, H * LANES), F32), sd((s_len, BW), MM)],
        name="attn_bwd", compiler_params=_cparams(2))(q, k, v, proj, dys, o, lse)


def _shift_down(a, n):
    r = lax.broadcasted_iota(jnp.int32, a.shape, 0)
    return jnp.where(r >= n, pltpu.roll(a, n, 0), 0.0)


def _shift_up(a, n):
    s_len = a.shape[0]
    r = lax.broadcasted_iota(jnp.int32, a.shape, 0)
    return jnp.where(r < s_len - n, pltpu.roll(a, s_len - n, 0), 0.0)


def _conv_specs(s_len):
    col = lambda off: _bs((s_len, LANES), lambda j: (0, off // LANES + j))
    return [col(OFF_CV), col(OFF_CV + CW), col(OFF_CV + 2 * CW), col(OFF_SG + BW),
            _bs((3, LANES), lambda j: (0, j)), _bs((1, LANES), lambda j: (0, j))]


def _conv_call(proj, cw, cb):
    s_len = proj.shape[0]

    def body(bg_ref, cg_ref, xi_ref, sg_ref, w_ref, b_ref, y_ref):
        z = cg_ref[...] * xi_ref[...]
        y = b_ref[...] + w_ref[0:1, :] * _shift_down(z, 2)
        y = y + w_ref[1:2, :] * _shift_down(z, 1)
        y = y + w_ref[2:3, :] * z
        y_ref[...] = ((bg_ref[...] * y) * jax.nn.silu(sg_ref[...])).astype(y_ref.dtype)

    return pl.pallas_call(
        body, grid=(CW // LANES,), in_specs=_conv_specs(s_len), out_specs=_bs((s_len, LANES), lambda j: (0, j)),
        out_shape=jax.ShapeDtypeStruct((s_len, CW), MM), name="conv", compiler_params=_cparams(1))(proj, proj, proj, proj, cw, cb)


def _conv_bwd_call(proj, cw, cb, dys):
    s_len = proj.shape[0]

    def body(bg_ref, cg_ref, xi_ref, sg_ref, w_ref, b_ref, dys_ref, dbg_ref, dcg_ref, dxi_ref, dsg_ref, dw_ref, db_ref):
        bg, cg, xi, sg = bg_ref[...], cg_ref[...], xi_ref[...], sg_ref[...]
        w0, w1, w2 = w_ref[0:1, :], w_ref[1:2, :], w_ref[2:3, :]
        z = cg * xi
        z1, z2 = _shift_down(z, 1), _shift_down(z, 2)
        y = b_ref[...] + w0 * z2
        y = y + w1 * z1
        y = y + w2 * z
        yb = bg * y
        sig = jax.nn.sigmoid(sg)
        silu = sg * sig
        dys_v = dys_ref[...]
        dsg_ref[...] = (dys_v * yb * (sig * (1.0 + sg * (1.0 - sig)))).astype(dsg_ref.dtype)
        dyb = dys_v * silu
        dbg_ref[...] = (dyb * y).astype(dbg_ref.dtype)
        dy = dyb * bg
        db_ref[...] = jnp.sum(dy, axis=0, keepdims=True)
        dw_ref[0:1, :] = jnp.sum(dy * z2, axis=0, keepdims=True)
        dw_ref[1:2, :] = jnp.sum(dy * z1, axis=0, keepdims=True)
        dw_ref[2:3, :] = jnp.sum(dy * z, axis=0, keepdims=True)
        dz = w2 * dy + w1 * _shift_up(dy, 1) + w0 * _shift_up(dy, 2)
        dcg_ref[...] = (dz * xi).astype(dcg_ref.dtype)
        dxi_ref[...] = (dz * cg).astype(dxi_ref.dtype)

    col = _bs((s_len, LANES), lambda j: (0, j))
    sd = jax.ShapeDtypeStruct
    return pl.pallas_call(
        body, grid=(CW // LANES,), in_specs=_conv_specs(s_len) + [col],
        out_specs=[col, col, col, col, _bs((3, LANES), lambda j: (0, j)), _bs((1, LANES), lambda j: (0, j))],
        out_shape=[sd((s_len, CW), MM)] * 4 + [sd((3, CW), F32), sd((1, CW), F32)],
        name="conv_bwd", compiler_params=_cparams(1))(proj, proj, proj, proj, cw, cb, dys)


def _sg_specs(tm):
    row = lambda off: _bs((tm, SGW), lambda i: (i, off // SGW))
    return [row(OFF_SGI), row(OFF_SGI + SGW), row(OFF_SG + 2 * BW), _bs((1, SGW), lambda i: (0, 0)), _bs((1, SGW), lambda i: (0, 0)),
            _bs((SGG, SGC, SGC), lambda i: (0, 0, 0)), _bs((SGG, SGC, 1), lambda i: (0, 0, 0))]


def _sg_args(refs):
    u, v, sg, lg, lb, ws, bs = refs
    return (u[...], v[...], sg[...], lg[...], lb[...], [ws[g] for g in range(SGG)], [bs[g] for g in range(SGG)])


def _sg_call(proj, ln_g, ln_b, ws, bs):
    s_len = proj.shape[0]
    tm = min(s_len, 256)

    def body(*refs):
        refs[7][...] = _sg_fn(*_sg_args(refs[:7])).astype(refs[7].dtype)

    return pl.pallas_call(
        body, grid=(s_len // tm,), in_specs=_sg_specs(tm), out_specs=_bs((tm, SGW), lambda i: (i, 0)),
        out_shape=jax.ShapeDtypeStruct((s_len, SGW), MM), name="sgmlp", compiler_params=_cparams(1))(proj, proj, proj, ln_g, ln_b, ws, bs)


def _sg_bwd_call(proj, ln_g, ln_b, ws, bs, dys):
    s_len = proj.shape[0]
    tm = min(s_len, 256)

    def body(*refs):
        dys_ref = refs[7]
        duv_ref, dsg_ref, dlg_ref, dlb_ref, dws_ref, dbs_ref = refs[8:]
        _, vjp = jax.vjp(_sg_fn, *_sg_args(refs[:7]))
        du, dv, dsg, dlg, dlb, dws, dbs = vjp(dys_ref[...])
        duv_ref[...] = jnp.concatenate([du, dv], axis=-1).astype(duv_ref.dtype)
        dsg_ref[...] = dsg.astype(dsg_ref.dtype)

        @pl.when(pl.program_id(0) == 0)
        def _():
            for r in (dlg_ref, dlb_ref, dws_ref, dbs_ref):
                r[...] = jnp.zeros_like(r)
        dlg_ref[...] += dlg
        dlb_ref[...] += dlb
        for g in range(SGG):
            dws_ref[g] += dws[g]
            dbs_ref[g] += dbs[g]

    row = _bs((tm, SGW), lambda i: (i, 0))
    sd = jax.ShapeDtypeStruct
    return pl.pallas_call(
        body, grid=(s_len // tm,), in_specs=_sg_specs(tm) + [row],
        out_specs=[_bs((tm, 2 * SGW), lambda i: (i, 0)), row, _bs((1, SGW), lambda i: (0, 0)), _bs((1, SGW), lambda i: (0, 0)),
                   _bs((SGG, SGC, SGC), lambda i: (0, 0, 0)), _bs((SGG, SGC, 1), lambda i: (0, 0, 0))],
        out_shape=[sd((s_len, 2 * SGW), MM), sd((s_len, SGW), MM), sd((1, SGW), F32), sd((1, SGW), F32),
                   sd((SGG, SGC, SGC), F32), sd((SGG, SGC, 1), F32)],
        name="sgmlp_bwd", compiler_params=_cparams(1))(proj, proj, proj, ln_g, ln_b, ws, bs, dys)


def _memkv_call(mem, mem_g, wm, kg):
    m_len = mem.shape[0]

    def body(mem_ref, g_ref, w_ref, kg_ref, k_ref, v_ref):
        k, v = _memkv_fn(mem_ref[...], g_ref[...], w_ref[...], kg_ref[...])
        k_ref[...] = k.astype(k_ref.dtype)
        v_ref[...] = v.astype(v_ref.dtype)

    return pl.pallas_call(body, out_shape=[jax.ShapeDtypeStruct((m_len, MH * MHD), MM)] * 2, name="memkv",
                          compiler_params=pltpu.CompilerParams(vmem_limit_bytes=VMEM_LIMIT))(mem, mem_g, wm, kg)


def _memkv_bwd_call(mem, mem_g, wm, kg, dk, dv):
    def body(mem_ref, g_ref, w_ref, kg_ref, dk_ref, dv_ref, dg_ref, dw_ref, dkg_ref):
        _, vjp = jax.vjp(_memkv_fn, mem_ref[...], g_ref[...], _with_slot(w_ref[...]), kg_ref[...])
        _, dg, dw, dkg = vjp((dk_ref[...], dv_ref[...]))
        dg_ref[...] = dg
        dw_ref[...] = dw[1]
        dkg_ref[...] = dkg

    sd = jax.ShapeDtypeStruct
    return pl.pallas_call(body, out_shape=[sd((1, D), F32), sd((D, 2 * MH * MHD), F32), sd((1, MHD), F32)], name="memkv_bwd",
                          compiler_params=pltpu.CompilerParams(vmem_limit_bytes=VMEM_LIMIT))(mem, mem_g, wm, kg, dk, dv)


def _mem_specs(tm, m_len):
    w = MH * MHD
    return [_bs((tm, w), lambda i: (i, OFF_MQ // w)), _bs((tm, BW), lambda i: (i, (OFF_SG + 3 * BW) // BW)),
            _bs((m_len, w), lambda i: (0, 0)), _bs((m_len, w), lambda i: (0, 0)), _bs((1, MHD), lambda i: (0, 0))]


def _mem_call(proj, k, v, qg):
    s_len, m_len = proj.shape[0], k.shape[0]
    tm = min(s_len, 256)

    def body(mq_ref, sg_ref, k_ref, v_ref, qg_ref, y_ref):
        y_ref[...] = _mem_fn(mq_ref[...], sg_ref[...], k_ref[...], v_ref[...], qg_ref[...]).astype(y_ref.dtype)

    return pl.pallas_call(
        body, grid=(s_len // tm,), in_specs=_mem_specs(tm, m_len), out_specs=_bs((tm, BW), lambda i: (i, 0)),
        out_shape=jax.ShapeDtypeStruct((s_len, BW), MM), name="memattn", compiler_params=_cparams(1))(proj, proj, k, v, qg)


def _mem_bwd_call(proj, k, v, qg, dys):
    s_len, m_len = proj.shape[0], k.shape[0]
    tm = min(s_len, 256)
    w = MH * MHD

    def body(mq_ref, sg_ref, k_ref, v_ref, qg_ref, dys_ref, dmq_ref, dsg_ref, dk_ref, dv_ref, dqg_ref):
        _, vjp = jax.vjp(_mem_fn, mq_ref[...], sg_ref[...], k_ref[...].astype(F32), v_ref[...].astype(F32), qg_ref[...])
        dmq, dsg, dk, dv, dqg = vjp(dys_ref[...])
        dmq_ref[...] = dmq.astype(dmq_ref.dtype)
        dsg_ref[...] = dsg.astype(dsg_ref.dtype)

        @pl.when(pl.program_id(0) == 0)
        def _():
            for r in (dk_ref, dv_ref, dqg_ref):
                r[...] = jnp.zeros_like(r)
        dk_ref[...] += dk
        dv_ref[...] += dv
        dqg_ref[...] += dqg

    row = _bs((tm, BW), lambda i: (i, 0))
    kv = _bs((m_len, w), lambda i: (0, 0))
    sd = jax.ShapeDtypeStruct
    return pl.pallas_call(
        body, grid=(s_len // tm,), in_specs=_mem_specs(tm, m_len) + [row],
        out_specs=[row, row, kv, kv, _bs((1, MHD), lambda i: (0, 0))],
        out_shape=[sd((s_len, w), MM), sd((s_len, BW), MM), sd((m_len, w), F32), sd((m_len, w), F32), sd((1, MHD), F32)],
        name="memattn_bwd", compiler_params=_cparams(1))(proj, proj, k, v, qg, dys)


def _merge_specs(tm):
    row = _bs((tm, BW), lambda i: (i, 0))
    return [row, row, row, row, _bs((tm, NB * D), lambda i: (i, OFF_ML // (NB * D))), _bs((NB, D), lambda i: (0, 0)),
            _bs((N_CHIPS, NB, BW, D // N_CHIPS), lambda i: (0, 0, 0, 0)), _bs((D, D), lambda i: (0, 0))]


def _merge_call(ys, proj, bm, wb, wo, x):
    s_len = proj.shape[0]
    tm = min(s_len, 256)

    def body(ya, yb, yc, yd, lg_ref, bm_ref, wb_ref, wo_ref, x_ref, o_ref):
        out = _merge_fn([r[...] for r in (ya, yb, yc, yd)], lg_ref[...], [bm_ref[n:n + 1, :] for n in range(NB)],
                        [[wb_ref[j, n] for n in range(NB)] for j in range(N_CHIPS)], wo_ref[...])
        o_ref[...] = x_ref[...] + out

    xrow = _bs((tm, D), lambda i: (i, 0))
    return pl.pallas_call(
        body, grid=(s_len // tm,), in_specs=_merge_specs(tm) + [xrow], out_specs=xrow,
        out_shape=jax.ShapeDtypeStruct((s_len, D), F32), name="merge", compiler_params=_cparams(1))(*ys, proj, bm, wb, wo, x)


def _merge_bwd_call(ys, proj, bm, wb, wo, dout):
    s_len = proj.shape[0]
    tm = min(s_len, 256)

    def body(ya, yb, yc, yd, lg_ref, bm_ref, wb_ref, wo_ref, do_ref, dya, dyb, dyc, dyd, dlg_ref, dbm_ref, dwb_ref, dwo_ref):
        fn = lambda ys_, lg_, bm_, wb_, wo_: _merge_fn(ys_, lg_, bm_, wb_, wo_)
        _, vjp = jax.vjp(fn, [r[...].astype(F32) for r in (ya, yb, yc, yd)], lg_ref[...], [bm_ref[n:n + 1, :] for n in range(NB)],
                         [[_with_slot(wb_ref[j, n]) for n in range(NB)] for j in range(N_CHIPS)], _with_slot(wo_ref[...]))
        dys, dlg, dbm, dwb, dwo = vjp(do_ref[...])
        dwb, dwo = [[d[1] for d in row] for row in dwb], dwo[1]
        for r, d in zip((dya, dyb, dyc, dyd), dys):
            r[...] = d
        dlg_ref[...] = dlg.astype(dlg_ref.dtype)

        @pl.when(pl.program_id(0) == 0)
        def _():
            for r in (dbm_ref, dwb_ref, dwo_ref):
                r[...] = jnp.zeros_like(r)
        for n in range(NB):
            dbm_ref[n:n + 1, :] += dbm[n]
            for j in range(N_CHIPS):
                dwb_ref[j, n] += dwb[j][n]
        dwo_ref[...] += dwo

    row = _bs((tm, BW), lambda i: (i, 0))
    sd = jax.ShapeDtypeStruct
    wb_shape = (N_CHIPS, NB, BW, D // N_CHIPS)
    return pl.pallas_call(
        body, grid=(s_len // tm,), in_specs=_merge_specs(tm) + [_bs((tm, D), lambda i: (i, 0))],
        out_specs=[row, row, row, row, _bs((tm, NB * D), lambda i: (i, 0)), _bs((NB, D), lambda i: (0, 0)),
                   _bs(wb_shape, lambda i: (0, 0, 0, 0)), _bs((D, D), lambda i: (0, 0))],
        out_shape=[sd((s_len, BW), F32)] * 4 + [sd((s_len, NP), MM), sd((NB, D), F32), sd(wb_shape, F32), sd((D, D), F32)],
        name="merge_bwd", compiler_params=_cparams(1))(*ys, proj, bm, wb, wo, dout)


def _dh_call(dproj, w, x, g, dout, after=()):
    s_len = x.shape[0]
    tk = NP // 4
    after = list(after)

    def matmul_body(dp_ref, w_ref, *rest):
        o_ref = rest[-1]

        @pl.when(pl.program_id(0) == 0)
        def _():
            o_ref[...] = jnp.zeros_like(o_ref)
        o_ref[...] += lax.dot_general(dp_ref[...], w_ref[...], (((1,), (1,)), ((), ())), preferred_element_type=F32)

    dh = pl.pallas_call(
        matmul_body, grid=(NP // tk,),
        in_specs=[_bs((s_len, tk), lambda k: (0, k)), _bs((D, tk), lambda k: (0, k))] + [_ANY] * len(after),
        out_specs=_bs((s_len, D), lambda k: (0, 0)), out_shape=jax.ShapeDtypeStruct((s_len, D), F32),
        name="dh", compiler_params=_cparams(1))(dproj, w, *after)

    tm = min(s_len, 512)

    def norm_body(dh_ref, x_ref, g_ref, do_ref, dx_ref, dg_ref):
        _, vjp = jax.vjp(lambda x_, g_: _rms_n(x_, g_, D), x_ref[...], g_ref[...])
        dxr, dgr = vjp(dh_ref[...])
        dx_ref[...] = do_ref[...] + dxr

        @pl.when(pl.program_id(0) == 0)
        def _():
            dg_ref[...] = jnp.zeros_like(dg_ref)
        dg_ref[...] += dgr

    row = _bs((tm, D), lambda i: (i, 0))
    return pl.pallas_call(
        norm_body, grid=(s_len // tm,), in_specs=[row, row, _bs((1, D), lambda i: (0, 0)), row],
        out_specs=[row, _bs((1, D), lambda i: (0, 0))],
        out_shape=[jax.ShapeDtypeStruct((s_len, D), F32), jax.ShapeDtypeStruct((1, D), F32)],
        name="norm_bwd", compiler_params=_cparams(1))(dh, x, g, dout)


def _dw_call(h, dproj, after=()):
    s_len = h.shape[0]
    tn = 512
    after = list(after)

    def body(h_ref, dp_ref, *rest):
        o_ref, ht_ref = rest[-2], rest[-1]

        @pl.when(pl.program_id(0) == 0)
        def _():
            ht_ref[...] = h_ref[...].T
        o_ref[...] = jnp.dot(ht_ref[...], dp_ref[...], preferred_element_type=F32)

    return pl.pallas_call(
        body, grid=(NP // tn,),
        in_specs=[_bs((s_len, D), lambda j: (0, 0)), _bs((s_len, tn), lambda j: (0, j))] + [_ANY] * len(after),
        out_specs=_bs((D, tn), lambda j: (0, j)), out_shape=jax.ShapeDtypeStruct((D, NP), F32),
        scratch_shapes=[pltpu.VMEM((D, s_len), h.dtype)], name="dw_in", compiler_params=_cparams(1))(h, dproj, *after)


def _loss_call(y, target):
    s_len = y.shape[0]
    tm = min(s_len, 512)

    def body(y_ref, t_ref, dy_ref, l_ref):
        e = y_ref[...] - t_ref[...]
        dy_ref[...] = e * (1.0 / D)

        @pl.when(pl.program_id(0) == 0)
        def _():
            l_ref[...] = jnp.zeros_like(l_ref)
        l_ref[...] += jnp.sum(e * e, axis=0, keepdims=True)

    row = _bs((tm, D), lambda i: (i, 0))
    return pl.pallas_call(
        body, grid=(s_len // tm,), in_specs=[row, row], out_specs=[row, _bs((1, D), lambda i: (0, 0))],
        out_shape=[jax.ShapeDtypeStruct((s_len, D), F32), jax.ShapeDtypeStruct((1, D), F32)],
        name="loss", compiler_params=_cparams(1))(y, target)


def _adamw_small_call(ws, gs, ms, vs, name):
    n = len(ws)

    def body(*refs):
        for t in range(n):
            w_ref, g_ref, m_ref, v_ref = refs[t], refs[n + t], refs[2 * n + t], refs[3 * n + t]
            d_ref, nm_ref, nv_ref = refs[4 * n + 3 * t:4 * n + 3 * t + 3]
            gv = g_ref[...]
            m2 = ADAM_B1 * m_ref[...] + (1.0 - ADAM_B1) * gv
            v2 = ADAM_B2 * v_ref[...] + (1.0 - ADAM_B2) * (gv * gv)
            m_hat = m2 / (1.0 - ADAM_B1 ** ADAM_STEP)
            v_hat = v2 / (1.0 - ADAM_B2 ** ADAM_STEP)
            d_ref[...] = -ADAM_LR * (m_hat / (jnp.sqrt(v_hat) + ADAM_EPS) + ADAM_WD * w_ref[...])
            nm_ref[...] = m2
            nv_ref[...] = v2

    return pl.pallas_call(
        body, out_shape=[jax.ShapeDtypeStruct(w.shape, F32) for w in ws for _ in range(3)], name=name,
        compiler_params=pltpu.CompilerParams(vmem_limit_bytes=VMEM_LIMIT))(*ws, *gs, *ms, *vs)


def _adamw_layer_call(layer, ws, gs, ms, vs, prev, after, name, steps=8):
    n = len(ws)
    after = list(after)
    n_prev = 4 * n if prev is not None else 0

    def body(*refs):
        outs = refs[len(refs) - 4 * n:]
        for t in range(n):
            w_ref, g_ref, m_ref, v_ref = refs[t], refs[n + t], refs[2 * n + t], refs[3 * n + t]
            g_out, d_out, m_out, v_out = outs[4 * t:4 * t + 4]
            gv = g_ref[...]
            m2 = ADAM_B1 * m_ref[0] + (1.0 - ADAM_B1) * gv
            v2 = ADAM_B2 * v_ref[0] + (1.0 - ADAM_B2) * (gv * gv)
            m_hat = m2 / (1.0 - ADAM_B1 ** ADAM_STEP)
            v_hat = v2 / (1.0 - ADAM_B2 ** ADAM_STEP)
            g_out[0] = gv
            d_out[0] = -ADAM_LR * (m_hat / (jnp.sqrt(v_hat) + ADAM_EPS) + ADAM_WD * w_ref[0])
            m_out[0] = m2
            v_out[0] = v2

    def lay(a):
        return _bs((1, a.shape[1] // steps, a.shape[2]), lambda i: (layer, i, 0))

    in_specs = ([lay(a) for a in ws] + [_bs((g.shape[0] // steps, g.shape[1]), lambda i: (i, 0)) for g in gs]
                + [lay(a) for a in ms] + [lay(a) for a in vs] + [_ANY] * (n_prev + len(after)))
    return pl.pallas_call(
        body, grid=(steps,), in_specs=in_specs, out_specs=[lay(ws[t]) for t in range(n) for _ in range(4)],
        out_shape=[jax.ShapeDtypeStruct(ws[t].shape, F32) for t in range(n) for _ in range(4)],
        input_output_aliases={4 * n + q: q for q in range(n_prev)}, name=name, compiler_params=_cparams(1),
    )(*ws, *gs, *ms, *vs, *(prev if prev is not None else []), *after)


def _row_tile(rows):
    for cand in (512, 256, 128, 64, 32, 16, 8):
        if rows % cand == 0 and rows > cand:
            return cand
    return rows


def _pair_sum_call(grads, from_sibling, core, name):
    n = len(grads)

    def body(core_ref, *refs):
        for t in range(n):
            refs[2 * n + t][...] = (refs[t][...].astype(F32) + refs[n + t][...].astype(F32)).astype(MM)

    half = lambda g: (1, g.shape[1] // 2, g.shape[2])
    grid_spec = pltpu.PrefetchScalarGridSpec(
        num_scalar_prefetch=1, grid=(N_CHIPS,),
        in_specs=[pl.BlockSpec(half(g), lambda j, core_ref: (j, core_ref[0], 0)) for g in grads]
        + [pl.BlockSpec(half(g), lambda j, core_ref: (j, 0, 0)) for g in grads],
        out_specs=[pl.BlockSpec(half(g), lambda j, core_ref: (j, 0, 0)) for g in grads])
    return pl.pallas_call(
        body, grid_spec=grid_spec, out_shape=[jax.ShapeDtypeStruct((N_CHIPS,) + half(g)[1:], MM) for g in grads], name=name,
        compiler_params=_cparams(1))(core, *grads, *from_sibling)


def _owner_sum_call(chip_sums, from_chips, chip_core, name):
    n = len(chip_sums)
    steps = 4

    def body(ids_ref, *refs):
        for t in range(n):
            a, b = refs[t], refs[n + t]
            refs[2 * n + t][...] = ((a[0].astype(F32) + b[0].astype(F32)) + b[1].astype(F32)) + b[2].astype(F32)

    tile = lambda p: (p.shape[1] // steps, p.shape[2])
    grid_spec = pltpu.PrefetchScalarGridSpec(
        num_scalar_prefetch=1, grid=(steps,),
        in_specs=[pl.BlockSpec((1,) + tile(p), lambda i, ids_ref: (ids_ref[0], i, 0)) for p in chip_sums]
        + [pl.BlockSpec((3,) + tile(p), lambda i, ids_ref: (0, i, 0)) for p in chip_sums],
        out_specs=[pl.BlockSpec(tile(p), lambda i, ids_ref: (ids_ref[1] * steps + i, 0)) for p in chip_sums])
    return pl.pallas_call(
        body, grid_spec=grid_spec, out_shape=[jax.ShapeDtypeStruct((2 * p.shape[1], p.shape[2]), F32) for p in chip_sums],
        name=name, compiler_params=_cparams(1))(chip_core, *chip_sums, *from_chips)


def _sum8_call(parts):
    n, rows, cols = parts.shape
    tr = _row_tile(rows)

    def body(p_ref, o_ref):
        acc = p_ref[0]
        for k in range(1, n):
            acc = acc + p_ref[k]
        o_ref[...] = acc

    return pl.pallas_call(
        body, grid=(rows // tr,), in_specs=[_bs((n, tr, cols), lambda i: (0, i, 0))], out_specs=_bs((tr, cols), lambda i: (i, 0)),
        out_shape=jax.ShapeDtypeStruct((rows, cols), F32), name="sum_small_grads", compiler_params=_cparams(1))(parts)


_ANY = pl.BlockSpec(memory_space=pl.ANY)


def _half_rows(ref, lead, half, which):
    rows = pl.ds(pl.multiple_of(half * which, half), half)
    return ref.at[rows] if lead is None else ref.at[lead, rows]


_HBM = pl.BlockSpec(memory_space=pltpu.HBM)
_SEM = pl.BlockSpec(memory_space=pltpu.SEMAPHORE)
_ORDERED_EFFECT = pltpu.CompilerParams(has_side_effects=pltpu.SideEffectType.DATAFLOW_SIDE_EFFECTING)


_VMEM = pl.BlockSpec(memory_space=pltpu.VMEM)
_TOKEN = jax.ShapeDtypeStruct((8, LANES), F32)


def _in_hbm(a):
    return pltpu.with_memory_space_constraint(a, pltpu.HBM)


def _tie(small, token):
    return small + token[0:1, 0:1].reshape((1,) * small.ndim)


def _peer(k):
    x, y, c = lax.axis_index("x"), lax.axis_index("y"), lax.axis_index("c")
    bx, by, bc = (k >> 2) & 1, (k >> 1) & 1, k & 1
    return (x ^ bx if bx else x, y ^ by if by else y, c ^ bc if bc else c)


def _place_block_call(blk, index, name):
    rows, cols = blk.shape

    def body(idx_ref, b_ref, o_ref):
        o_ref[0] = b_ref[...]

    grid_spec = pltpu.PrefetchScalarGridSpec(
        num_scalar_prefetch=1, grid=(1,), in_specs=[pl.BlockSpec((rows, cols), lambda i, idx_ref: (0, 0))],
        out_specs=pl.BlockSpec((1, rows, cols), lambda i, idx_ref: (idx_ref[0], 0, 0)))
    return pl.pallas_call(body, grid_spec=grid_spec, out_shape=jax.ShapeDtypeStruct((8, rows, cols), blk.dtype), name=name,
                          compiler_params=_cparams(1))(index, blk)


def _small_gather_start_call(blk, buf, after, name):
    after = list(after)

    def body(*refs):
        b_ref, out_ref = refs[0], refs[2 + len(after)]
        send_sems, recv_sems, token = refs[3 + len(after):]
        x, y, c = lax.axis_index("x"), lax.axis_index("y"), lax.axis_index("c")
        for k in range(1, 8):
            pltpu.make_async_remote_copy(src_ref=b_ref, dst_ref=out_ref.at[4 * x + 2 * y + c], send_sem=send_sems.at[k - 1],
                                         recv_sem=recv_sems.at[k - 1], device_id=_peer(k), device_id_type=MESH_ID).start()
        token[...] = jnp.zeros_like(token)

    dma = pltpu.SemaphoreType.DMA
    return pl.pallas_call(
        body, out_shape=[pltpu.HBM(buf.shape, buf.dtype), dma((7,)), dma((7,)), _TOKEN],
        in_specs=[_HBM, _HBM] + [_ANY] * len(after), out_specs=[_HBM, _SEM, _SEM, _VMEM],
        input_output_aliases={1: 0}, name=name, compiler_params=_ORDERED_EFFECT)(_in_hbm(blk), _in_hbm(buf), *after)


def _small_gather_finish_call(blk, buf, send_sems, recv_sems, after, name):
    after = list(after)

    def body(*refs):
        b_ref, in_ref, send_ref, recv_ref = refs[:4]
        x, y, c = lax.axis_index("x"), lax.axis_index("y"), lax.axis_index("c")
        for k in range(1, 8):
            px, py, pc = _peer(k)
            pltpu.make_async_remote_copy(src_ref=b_ref, dst_ref=in_ref.at[4 * px + 2 * py + pc], send_sem=send_ref.at[k - 1],
                                         recv_sem=recv_ref.at[k - 1], device_id=(px, py, pc), device_id_type=MESH_ID).wait()

    return pl.pallas_call(
        body, out_shape=pltpu.HBM(buf.shape, buf.dtype), in_specs=[_HBM, _HBM, _SEM, _SEM] + [_ANY] * len(after),
        out_specs=_HBM, input_output_aliases={1: 0}, name=name, compiler_params=_ORDERED_EFFECT,
    )(_in_hbm(blk), buf, send_sems, recv_sems, *after)


def _pair_exchange_start_call(grads, name):
    n = len(grads)
    half = [g.shape[1] // 2 for g in grads]

    def body(*refs):
        srcs, outs = refs[:n], refs[n:2 * n]
        send_sems, recv_sems, token = refs[2 * n:]
        x, y, c = lax.axis_index("x"), lax.axis_index("y"), lax.axis_index("c")
        for t in range(n):
            pltpu.make_async_remote_copy(
                src_ref=srcs[t].at[:, pl.ds(pl.multiple_of(half[t] * (1 - c), half[t]), half[t])], dst_ref=outs[t],
                send_sem=send_sems.at[t], recv_sem=recv_sems.at[t], device_id=(x, y, 1 - c), device_id_type=MESH_ID).start()
        token[...] = jnp.zeros_like(token)

    dma = pltpu.SemaphoreType.DMA
    return pl.pallas_call(
        body, out_shape=[pltpu.HBM((g.shape[0], g.shape[1] // 2, g.shape[2]), g.dtype) for g in grads] + [dma((n,)), dma((n,)), _TOKEN],
        in_specs=[_HBM] * n, out_specs=[_HBM] * n + [_SEM, _SEM, _VMEM], name=name, compiler_params=_ORDERED_EFFECT,
    )(*[_in_hbm(g) for g in grads])


def _pair_exchange_finish_call(grads, bufs, send_sems, recv_sems, after, name):
    n = len(grads)
    after = list(after)
    half = [g.shape[1] // 2 for g in grads]

    def body(*refs):
        srcs, ins, send_ref, recv_ref = refs[:n], refs[n:2 * n], refs[2 * n], refs[2 * n + 1]
        x, y, c = lax.axis_index("x"), lax.axis_index("y"), lax.axis_index("c")
        for t in range(n):
            pltpu.make_async_remote_copy(
                src_ref=srcs[t].at[:, pl.ds(pl.multiple_of(half[t] * (1 - c), half[t]), half[t])], dst_ref=ins[t],
                send_sem=send_ref.at[t], recv_sem=recv_ref.at[t], device_id=(x, y, 1 - c), device_id_type=MESH_ID).wait()

    return pl.pallas_call(
        body, out_shape=[pltpu.HBM(b.shape, b.dtype) for b in bufs],
        in_specs=[_HBM] * (2 * n) + [_SEM, _SEM] + [_ANY] * len(after), out_specs=[_HBM] * n,
        input_output_aliases={n + t: t for t in range(n)}, name=name, compiler_params=_ORDERED_EFFECT,
    )(*[_in_hbm(g) for g in grads], *bufs, send_sems, recv_sems, *after)


def _chip_scatter_start_call(chip_sums, name):
    n = len(chip_sums)

    def body(*refs):
        srcs, outs = refs[:n], refs[n:2 * n]
        send_sems, recv_sems, token = refs[2 * n:]
        x, y, c = lax.axis_index("x"), lax.axis_index("y"), lax.axis_index("c")
        chips = [(1 - x, y), (x, 1 - y), (1 - x, 1 - y)]
        for k, (cx, cy) in enumerate(chips):
            for t in range(n):
                pltpu.make_async_remote_copy(
                    src_ref=srcs[t].at[2 * cx + cy], dst_ref=outs[t].at[k], send_sem=send_sems.at[3 * t + k],
                    recv_sem=recv_sems.at[3 * t + k], device_id=(cx, cy, c), device_id_type=MESH_ID).start()
        token[...] = jnp.zeros_like(token)

    dma = pltpu.SemaphoreType.DMA
    return pl.pallas_call(
        body, out_shape=[pltpu.HBM((3,) + p.shape[1:], p.dtype) for p in chip_sums] + [dma((3 * n,)), dma((3 * n,)), _TOKEN],
        in_specs=[_HBM] * n, out_specs=[_HBM] * n + [_SEM, _SEM, _VMEM], name=name, compiler_params=_ORDERED_EFFECT,
    )(*[_in_hbm(p) for p in chip_sums])


def _chip_scatter_finish_call(chip_sums, bufs, send_sems, recv_sems, after, name):
    n = len(chip_sums)
    after = list(after)

    def body(*refs):
        srcs, ins, send_ref, recv_ref = refs[:n], refs[n:2 * n], refs[2 * n], refs[2 * n + 1]
        x, y, c = lax.axis_index("x"), lax.axis_index("y"), lax.axis_index("c")
        chips = [(1 - x, y), (x, 1 - y), (1 - x, 1 - y)]
        for k, (cx, cy) in enumerate(chips):
            for t in range(n):
                pltpu.make_async_remote_copy(
                    src_ref=srcs[t].at[2 * cx + cy], dst_ref=ins[t].at[k], send_sem=send_ref.at[3 * t + k],
                    recv_sem=recv_ref.at[3 * t + k], device_id=(cx, cy, c), device_id_type=MESH_ID).wait()

    return pl.pallas_call(
        body, out_shape=[pltpu.HBM(b.shape, b.dtype) for b in bufs],
        in_specs=[_HBM] * (2 * n) + [_SEM, _SEM] + [_ANY] * len(after), out_specs=[_HBM] * n,
        input_output_aliases={n + t: t for t in range(n)}, name=name, compiler_params=_ORDERED_EFFECT,
    )(*[_in_hbm(p) for p in chip_sums], *bufs, send_sems, recv_sems, *after)


def _place_own_call(mine, chip_core, name):
    n = len(mine)

    def body(ids_ref, *refs):
        for t in range(n):
            refs[n + t][0] = refs[t][...]

    def imap_out(s):
        pad = (0,) * (s.ndim - 1)
        return lambda i, ids_ref: (ids_ref[0], ids_ref[1]) + pad

    grid_spec = pltpu.PrefetchScalarGridSpec(
        num_scalar_prefetch=1, grid=(1,), in_specs=[pl.BlockSpec(s.shape, lambda i, ids_ref, k=s.ndim: (0,) * k) for s in mine],
        out_specs=[pl.BlockSpec((1,) + s.shape, imap_out(s)) for s in mine])
    return pl.pallas_call(
        body, grid_spec=grid_spec,
        out_shape=[jax.ShapeDtypeStruct((N_CHIPS, 2 * s.shape[0]) + s.shape[1:], s.dtype) for s in mine],
        name=name, compiler_params=_cparams(1))(chip_core, *mine)


def _gather_start_call(mine, bufs, after, name):
    n = len(mine)
    half = [s.shape[0] for s in mine]

    def body(*refs):
        srcs, outs = refs[:n], refs[2 * n + 1:3 * n + 1]
        send_sems, recv_sib, recv_ici, token = refs[3 * n + 1:]
        x, y, c = lax.axis_index("x"), lax.axis_index("y"), lax.axis_index("c")
        chips = [(1 - x, y), (x, 1 - y), (1 - x, 1 - y)]
        for t in range(n):
            dst = _half_rows(outs[t], 2 * x + y, half[t], c)
            pltpu.make_async_remote_copy(src_ref=srcs[t], dst_ref=dst, send_sem=send_sems.at[4 * t], recv_sem=recv_sib.at[t],
                                         device_id=(x, y, 1 - c), device_id_type=MESH_ID).start()
            for j, chip in enumerate(chips):
                pltpu.make_async_remote_copy(src_ref=srcs[t], dst_ref=dst, send_sem=send_sems.at[4 * t + 1 + j],
                                             recv_sem=recv_ici.at[3 * t + j], device_id=(*chip, c), device_id_type=MESH_ID).start()
        token[...] = jnp.zeros_like(token)

    dma = pltpu.SemaphoreType.DMA
    return pl.pallas_call(
        body, out_shape=[pltpu.HBM(b.shape, b.dtype) for b in bufs] + [dma((4 * n,)), dma((n,)), dma((3 * n,)), _TOKEN],
        in_specs=[_HBM] * (2 * n) + [_ANY], out_specs=[_HBM] * n + [_SEM] * 3 + [_VMEM],
        input_output_aliases={n + t: t for t in range(n)}, name=name, compiler_params=_ORDERED_EFFECT,
    )(*[_in_hbm(s) for s in mine], *[_in_hbm(b) for b in bufs], after)


def _gather_forward_call(bufs, recv_ici, after, name):
    n = len(bufs)
    half = [b.shape[1] // 2 for b in bufs]

    def body(*refs):
        ins, recv_ici_ref = refs[:n], refs[n]
        outs = refs[n + 2:2 * n + 2]
        send_fwd, recv_fwd, token = refs[2 * n + 2:]
        x, y, c = lax.axis_index("x"), lax.axis_index("y"), lax.axis_index("c")
        chips = [(1 - x, y), (x, 1 - y), (1 - x, 1 - y)]
        for j, (cx, cy) in enumerate(chips):
            for t in range(n):
                landed = _half_rows(ins[t], 2 * cx + cy, half[t], c)
                dst = _half_rows(outs[t], 2 * cx + cy, half[t], c)
                pltpu.make_async_remote_copy(src_ref=landed, dst_ref=landed, send_sem=send_fwd.at[3 * t + j],
                                             recv_sem=recv_ici_ref.at[3 * t + j], device_id=(cx, cy, c),
                                             device_id_type=MESH_ID).wait_recv()
                pltpu.make_async_remote_copy(src_ref=landed, dst_ref=dst, send_sem=send_fwd.at[3 * t + j],
                                             recv_sem=recv_fwd.at[3 * t + j], device_id=(x, y, 1 - c),
                                             device_id_type=MESH_ID).start()
        token[...] = jnp.zeros_like(token)

    dma = pltpu.SemaphoreType.DMA
    return pl.pallas_call(
        body, out_shape=[pltpu.HBM(b.shape, b.dtype) for b in bufs] + [dma((3 * n,)), dma((3 * n,)), _TOKEN],
        in_specs=[_HBM] * n + [_SEM, _ANY], out_specs=[_HBM] * n + [_SEM] * 2 + [_VMEM],
        input_output_aliases={t: t for t in range(n)}, name=name, compiler_params=_ORDERED_EFFECT,
    )(*bufs, recv_ici, after)


def _gather_finish_call(shards, bufs, send_sems, recv_sib, send_fwd, recv_fwd, after, name):
    n = len(bufs)
    half = [b.shape[1] // 2 for b in bufs]

    def body(*refs):
        srcs, ins = refs[:n], refs[n:2 * n]
        send_ref, recv_sib_ref, send_fwd_ref, recv_fwd_ref = refs[2 * n:2 * n + 4]
        x, y, c = lax.axis_index("x"), lax.axis_index("y"), lax.axis_index("c")
        chips = [(1 - x, y), (x, 1 - y), (1 - x, 1 - y)]
        sibling = (x, y, 1 - c)
        for t in range(n):
            for k in range(4):
                pltpu.make_async_remote_copy(src_ref=srcs[t], dst_ref=srcs[t], send_sem=send_ref.at[4 * t + k],
                                             recv_sem=recv_sib_ref.at[t], device_id=sibling, device_id_type=MESH_ID).wait_send()
            from_sibling = _half_rows(ins[t], 2 * x + y, half[t], 1 - c)
            pltpu.make_async_remote_copy(src_ref=from_sibling, dst_ref=from_sibling, send_sem=send_ref.at[4 * t],
                                         recv_sem=recv_sib_ref.at[t], device_id=sibling, device_id_type=MESH_ID).wait_recv()
            for j, (cx, cy) in enumerate(chips):
                sent = _half_rows(ins[t], 2 * cx + cy, half[t], c)
                passed = _half_rows(ins[t], 2 * cx + cy, half[t], 1 - c)
                pltpu.make_async_remote_copy(src_ref=sent, dst_ref=passed, send_sem=send_fwd_ref.at[3 * t + j],
                                             recv_sem=recv_fwd_ref.at[3 * t + j], device_id=sibling, device_id_type=MESH_ID).wait()

    return pl.pallas_call(
        body, out_shape=[pltpu.HBM(b.shape, b.dtype) for b in bufs],
        in_specs=[_HBM] * (2 * n) + [_SEM] * 4 + [_ANY], out_specs=[_HBM] * n,
        input_output_aliases={n + t: t for t in range(n)}, name=name, compiler_params=_ORDERED_EFFECT,
    )(*[_in_hbm(s) for s in shards], *bufs, send_sems, recv_sib, send_fwd, recv_fwd, after)


def _pair_gather_call(bufs, name):
    n = len(bufs)
    half = [b.shape[0] // 2 for b in bufs]

    def body(*refs):
        srcs, outs, send_sems, recv_sems = refs[:n], refs[n:2 * n], refs[2 * n], refs[2 * n + 1]
        x, y, c = lax.axis_index("x"), lax.axis_index("y"), lax.axis_index("c")
        for t in range(n):
            pltpu.make_async_remote_copy(
                src_ref=_half_rows(srcs[t], None, half[t], c), dst_ref=_half_rows(outs[t], None, half[t], c),
                send_sem=send_sems.at[t], recv_sem=recv_sems.at[t], device_id=(x, y, 1 - c), device_id_type=MESH_ID).start()
        for t in range(n):
            pltpu.make_async_remote_copy(
                src_ref=_half_rows(srcs[t], None, half[t], c), dst_ref=_half_rows(outs[t], None, half[t], 1 - c),
                send_sem=send_sems.at[t], recv_sem=recv_sems.at[t], device_id=(x, y, 1 - c), device_id_type=MESH_ID).wait()

    return pl.pallas_call(
        body, out_shape=[jax.ShapeDtypeStruct(b.shape, b.dtype) for b in bufs], in_specs=[_ANY] * n, out_specs=[_ANY] * n,
        input_output_aliases={t: t for t in range(n)},
        scratch_shapes=[pltpu.SemaphoreType.DMA((n,)), pltpu.SemaphoreType.DMA((n,))], name=name)(*bufs)


def _pack_rows(flats, dtype, row_multiple):
    flat = jnp.concatenate([f.reshape(-1).astype(dtype) for f in flats])
    n = flat.shape[0]
    rows = -(-n // PACK_W)
    rows = -(-rows // row_multiple) * row_multiple
    return jnp.pad(flat, (0, rows * PACK_W - n)).reshape(rows, PACK_W)


def _unpack(flat, shapes):
    out, off = [], 0
    for shp in shapes:
        n = math.prod(shp)
        out.append(flat[off:off + n].reshape(shp))
        off += n
    return out


_W_IN_SEGMENTS = ((R_ML, R_END, OFF_ML), (R_SG, R_ML, OFF_SG), (R_CV, R_SGI, OFF_CV), (R_SGI, R_MQ, OFF_SGI), (R_MQ, R_SG, OFF_MQ),
                  (R_CQ, R_CKV, OFF_CQ), (R_CKV, R_KR, OFF_CKV), (R_KR, R_CV, OFF_KR + NOPE))
W_IN_SHARD = R_END // N_CHIPS


def _realign_call(wg):
    tr = 128

    def body(w_ref, o_ref):
        pieces, pos = [], 0
        for r0, r1, a0 in _W_IN_SEGMENTS:
            if a0 > pos:
                pieces.append(jnp.zeros((tr, a0 - pos), o_ref.dtype))
            while r0 < r1:
                j = r0 // W_IN_SHARD
                hi = min(r1, (j + 1) * W_IN_SHARD)
                pieces.append(w_ref[j, :, r0 - j * W_IN_SHARD:hi - j * W_IN_SHARD])
                a0, r0 = a0 + hi - r0, hi
            pos = a0
        pieces.append(jnp.zeros((tr, NP - pos), o_ref.dtype))
        o_ref[...] = jnp.concatenate(pieces, axis=1)

    return pl.pallas_call(
        body, grid=(D // tr,), in_specs=[_bs((N_CHIPS, tr, W_IN_SHARD), lambda i: (0, i, 0))],
        out_specs=_bs((tr, NP), lambda i: (i, 0)), out_shape=jax.ShapeDtypeStruct((D, NP), wg.dtype),
        name="w_in_realign", compiler_params=_cparams(1))(wg)


def _unalign_call(dw, out_dtype):
    tr = 128
    by_ref = sorted(_W_IN_SEGMENTS)

    def body(dw_ref, o_ref):
        for j in range(N_CHIPS):
            lo_j, hi_j = j * W_IN_SHARD, (j + 1) * W_IN_SHARD
            pieces = []
            for r0, r1, a0 in by_ref:
                lo, hi = max(r0, lo_j), min(r1, hi_j)
                if lo < hi:
                    pieces.append(dw_ref[:, a0 + lo - r0:a0 + hi - r0])
            o_ref[j] = jnp.concatenate(pieces, axis=1).astype(o_ref.dtype)

    return pl.pallas_call(
        body, grid=(D // tr,), in_specs=[_bs((tr, NP), lambda i: (i, 0))],
        out_specs=_bs((N_CHIPS, tr, W_IN_SHARD), lambda i: (0, i, 0)),
        out_shape=jax.ShapeDtypeStruct((N_CHIPS, D, W_IN_SHARD), out_dtype), name="w_in_unalign", compiler_params=_cparams(1))(dw)


def _wuq_to_heads(w):
    w3 = w.reshape(QL, H, QKH)
    w3 = jnp.pad(w3, ((0, 0), (0, 0), (0, LANES - QKH)))
    return jnp.transpose(w3, (1, 0, 2))


def _wuq_from_heads(wh):
    return jnp.transpose(wh[:, :, :QKH], (1, 0, 2)).reshape(QL, H * QKH)


def _wukv_to_heads(w):
    w3 = w.reshape(KVL, H, NOPE + VH)
    wkn = jnp.transpose(jnp.pad(w3[:, :, :NOPE], ((0, 0), (0, 0), (0, LANES - NOPE))), (1, 0, 2))
    wv3 = w3[:, :, NOPE:]
    z = jnp.zeros((KVL, VH), w.dtype)
    cols = []
    for h in range(H):
        cols += [wv3[:, h], z] if h % 2 == 0 else [z, wv3[:, h]]
    return wkn, jnp.concatenate(cols, axis=1)


def _wukv_from_heads(wkn, wv):
    kn = jnp.transpose(wkn[:, :, :NOPE], (1, 0, 2))
    vs = jnp.stack([wv[:, LANES * h + VH * (h % 2):LANES * h + VH * (h % 2) + VH] for h in range(H)], axis=1)
    return jnp.concatenate([kn, vs], axis=2).reshape(KVL, H * (NOPE + VH))


def _layer_fwd(x, mem, tabs, p):
    proj, h = _proj_call(x, p["norm_g"], p["w_in"])
    if p.get("late") is not None:
        p = dict(p, **p["late"](proj))
    q, k, v = _mla_prep_call(proj, tabs, p["cq_g"], p["ckv_g"], p["qg"], p["kg"], p["wuq"], p["wkn"], p["wv"])
    ya, attn_o, attn_lse = _attn_call(q, k, v, proj)
    bm = p["bm"]
    if p.get("after_attn") is not None:
        bm = _tie(bm, p["after_attn"](ya))
    yb = _conv_call(proj, p["conv_w"], p["conv_b"])
    yc = _sg_call(proj, p["ln_g"], p["ln_b"], p["ws"], p["bs"])
    mk, mv = _memkv_call(mem, p["mem_g"], p["wm"], p["mkg"])
    yd = _mem_call(proj, mk, mv, p["mqg"])
    out = _merge_call((ya, yb, yc, yd), proj, bm, p["wb"], p["wo"], x)
    return out, dict(p=p, x=x, proj=proj, h=h, q=q, k=k, v=v, attn_o=attn_o, attn_lse=attn_lse, ys=(ya, yb, yc, yd), mk=mk, mv=mv)


def _layer_bwd(dout, mem, tabs, p, sv, start_after=None, on_rest_grads=None, on_grads=None):
    proj = sv["proj"]
    bm = p["bm"] if start_after is None else _tie(p["bm"], start_after)
    dya, dyb, dyc, dyd, dml, dbm, dwb, dwo = _merge_bwd_call(sv["ys"], proj, bm, p["wb"], p["wo"], dout)
    dq, dk, dv, dsg_a = _attn_bwd_call(sv["q"], sv["k"], sv["v"], proj, dya, sv["attn_o"], sv["attn_lse"])
    dlat, dcqg, dckvg, dqg, dkg, dwuq, dwkn, dwv = _mla_prep_bwd_call(
        proj, tabs, p["cq_g"], p["ckv_g"], p["qg"], p["kg"], p["wuq"], p["wkn"], p["wv"], dq, dk, dv)
    dbg, dcg, dxi, dsg_b, dcw, dcb = _conv_bwd_call(proj, p["conv_w"], p["conv_b"], dyb)
    duv, dsg_c, dlg, dlb, dws, dbs = _sg_bwd_call(proj, p["ln_g"], p["ln_b"], p["ws"], p["bs"], dyc)
    dmq, dsg_d, dmk, dmv, dmqg = _mem_bwd_call(proj, sv["mk"], sv["mv"], p["mqg"], dyd)
    dmem_g, dwm, dmkg = _memkv_bwd_call(mem, p["mem_g"], p["wm"], p["mkg"], dmk, dmv)
    grads = dict(cq_norm_g=dcqg[0], ckv_norm_g=dckvg[0], mla_q_norm_g=dqg[0, :QKH], mla_k_norm_g=dkg[0, :QKH],
                 conv_w=dcw, conv_b=dcb[0], sg_ln_g=dlg[0], sg_ln_b=dlb[0], w_spatial=dws, b_spatial=dbs[:, :, 0],
                 mem_norm_g=dmem_g[0], mem_q_norm_g=dmqg[0], mem_k_norm_g=dmkg[0], b_merge=dbm,
                 wuq_heads=dwuq, wkn_heads=dwkn, wv_heads=dwv, w_mem_kv=dwm, w_branch_chips=dwb, w_out=dwo)
    started = [on_rest_grads(grads)] if on_rest_grads is not None else []
    dproj, off = dml, NB * D
    for piece in (dsg_a, dsg_b, dsg_c, dsg_d, dbg, dcg, dxi, duv, dmq, dlat):
        dproj = lax.dynamic_update_slice(dproj, piece, (0, off))
        off += piece.shape[1]
    grads["w_in_aligned"] = _dw_call(sv["h"], dproj, started)
    tokens = on_grads(grads) if on_grads is not None else ()
    dx, dnorm_g = _dh_call(dproj, p["w_in"], sv["x"], p["norm_g"], dout, tokens)
    grads["norm_g"] = dnorm_g[0]
    return dx, grads


def _chips_to_cols(a):
    return jnp.concatenate([a[j] for j in range(N_CHIPS)], axis=1)


def _cols_to_chips(a):
    cols = a.shape[1] // N_CHIPS
    return jnp.stack([a[:, cols * j:cols * (j + 1)] for j in range(N_CHIPS)])


def _layer_params_first(l, rep, w_in_gathered, conv_w, b_merge):
    pad_g = lambda g: jnp.pad(g, (0, LANES - QKH)).reshape(1, LANES)
    return dict(
        norm_g=rep["norm_g"][l].reshape(1, D), w_in=_realign_call(w_in_gathered),
        cq_g=rep["cq_norm_g"][l].reshape(1, QL), ckv_g=rep["ckv_norm_g"][l].reshape(1, KVL),
        qg=pad_g(rep["mla_q_norm_g"][l]), kg=pad_g(rep["mla_k_norm_g"][l]),
        conv_w=conv_w, conv_b=rep["conv_b"][l].reshape(1, CW),
        ln_g=rep["sg_ln_g"][l].reshape(1, SGW), ln_b=rep["sg_ln_b"][l].reshape(1, SGW),
        ws=rep["w_spatial"][l], bs=rep["b_spatial"][l].reshape(SGG, SGC, 1),
        mem_g=rep["mem_norm_g"][l].reshape(1, D),
        mqg=rep["mem_q_norm_g"][l].reshape(1, MHD), mkg=rep["mem_k_norm_g"][l].reshape(1, MHD), bm=b_merge)


def _layer_params_rest(gathered):
    wkn, wv = _wukv_to_heads(_chips_to_cols(gathered["w_ukv"]))
    return dict(wuq=_wuq_to_heads(_chips_to_cols(gathered["w_uq"])), wkn=wkn, wv=wv,
                wm=gathered["w_mem_kv"].reshape(D, 2 * MH * MHD), wb=gathered["w_branch"], wo=gathered["w_out"].reshape(D, D))


def _layer_params(l, rep, gathered, conv_w, b_merge):
    return dict(_layer_params_first(l, rep, gathered["w_in"], conv_w, b_merge), **_layer_params_rest(gathered))


def _forward_backward(x, mem, pos, target, params, bwd_hooks=None):
    tabs = _rope_tables(pos)
    params = list(params)
    saved = []
    act = x
    for l in range(DEPTH):
        if callable(params[l]):
            params[l] = params[l](saved[-1], act)
        act, sv = _layer_fwd(act, mem, tabs, params[l])
        saved.append(sv)
    dy, sq = _loss_call(act, target)
    grads = [None] * DEPTH
    token = None
    for l in reversed(range(DEPTH)):
        hooks = dict(bwd_hooks[l]) if bwd_hooks else {}
        after_layer = hooks.pop("after_layer", None)
        dy, grads[l] = _layer_bwd(dy, mem, tabs, saved[l]["p"], saved[l], start_after=token, **hooks)
        token = after_layer(dy) if after_layer is not None else None
    return sq, dy, grads


_SHARDED_MM = ("w_in", "w_branch", "w_out", "w_mem_kv", "w_uq", "w_ukv")
_SHARDED_F32 = ("conv_w", "b_merge")
_REPLICATED = ("norm_g", "cq_norm_g", "ckv_norm_g", "mla_q_norm_g", "mla_k_norm_g", "conv_b", "sg_ln_g", "sg_ln_b",
               "w_spatial", "b_spatial", "mem_norm_g", "mem_q_norm_g", "mem_k_norm_g")
_ALL_REDUCED = _REPLICATED + _SHARDED_F32
_WEIGHTS = ("norm_g", "w_in", "cq_norm_g", "ckv_norm_g", "w_uq", "w_ukv", "mla_q_norm_g", "mla_k_norm_g", "conv_w", "conv_b",
            "sg_ln_g", "sg_ln_b", "w_spatial", "b_spatial", "mem_norm_g", "w_mem_kv", "mem_q_norm_g", "mem_k_norm_g",
            "b_merge", "w_branch", "w_out")
_SMALL = tuple(n for n in _WEIGHTS if n not in _SHARDED_MM)


class _SmallGather:
    def __init__(self, blk, after, tag):
        self.blk, self.tag = blk, tag
        x, y, c = lax.axis_index("x"), lax.axis_index("y"), lax.axis_index("c")
        own = _place_block_call(blk, (4 * x + 2 * y + c).astype(jnp.int32).reshape(1), tag + "place_own")
        self.buf, self.send, self.recv, self.token = _small_gather_start_call(blk, own, after, tag + "start")

    def finish(self, after):
        return _small_gather_finish_call(self.blk, self.buf, self.send, self.recv, after, self.tag + "finish")


def _small_sharded_weights(w, got):
    names = _SHARDED_F32
    per_chip = [_unpack(got[2 * j].reshape(-1), [w[n].shape for n in names]) for j in range(N_CHIPS)]
    return {n: jnp.concatenate([per_chip[j][t] for j in range(N_CHIPS)], axis=2) for t, n in enumerate(names)}


class _Gather:
    def __init__(self, w, layer, names, after, tag):
        self.names, self.tag = names, tag
        x, y, c = lax.axis_index("x"), lax.axis_index("y"), lax.axis_index("c")
        chip_core = jnp.stack([2 * x + y, c]).astype(jnp.int32)
        halves = [w[n].shape[1] // 2 for n in names]
        self.srcs = [lax.dynamic_slice_in_dim(w[n][layer], c * h, h, axis=0).astype(MM) for n, h in zip(names, halves)]
        k = len(names)
        out = _gather_start_call(self.srcs, _place_own_call(self.srcs, chip_core, tag + "place_own"), after, tag + "start")
        self.bufs, self.send, self.recv_sib, self.recv_ici, self.token = out[:k], out[k], out[k + 1], out[k + 2], out[k + 3]

    def pass_on(self, after):
        k = len(self.names)
        out = _gather_forward_call(self.bufs, self.recv_ici, after, self.tag + "forward")
        self.bufs, self.send_fwd, self.recv_fwd = out[:k], out[k], out[k + 1]
        return out[k + 2]

    def finish(self, after):
        got = _gather_finish_call(self.srcs, self.bufs, self.send, self.recv_sib, self.send_fwd, self.recv_fwd, after,
                                  self.tag + "finish")
        return dict(zip(self.names, got))


class _ReduceScatter:
    SLABS = dict(
        w_in=lambda g: _unalign_call(g["w_in_aligned"], MM),
        w_branch=lambda g: g["w_branch_chips"].reshape(N_CHIPS, NB * BW, D // N_CHIPS),
        w_out=lambda g: g["w_out"].reshape(N_CHIPS, D // N_CHIPS, D),
        w_mem_kv=lambda g: g["w_mem_kv"].reshape(N_CHIPS, D // N_CHIPS, 2 * MH * MHD),
        w_uq=lambda g: _cols_to_chips(_wuq_from_heads(g["wuq_heads"])),
        w_ukv=lambda g: _cols_to_chips(_wukv_from_heads(g["wkn_heads"], g["wv_heads"])))

    def __init__(self, tag, names):
        self.tag, self.names = tag, names

    def exchange(self, grads):
        self.tensors = [self.SLABS[n](grads) for n in self.names]
        n = len(self.tensors)
        out = _pair_exchange_start_call(self.tensors, self.tag + "exchange_start")
        self.ex_bufs, self.ex_send, self.ex_recv = out[:n], out[n], out[n + 1]
        return out[n + 2]

    def scatter(self, after):
        n = len(self.tensors)
        c = lax.axis_index("c")
        from_sibling = _pair_exchange_finish_call(self.tensors, self.ex_bufs, self.ex_send, self.ex_recv, after,
                                                  self.tag + "exchange_finish")
        self.chip_sums = _pair_sum_call(self.tensors, from_sibling, c.astype(jnp.int32).reshape(1), self.tag + "pair_sum")
        out = _chip_scatter_start_call(self.chip_sums, self.tag + "scatter_start")
        self.bufs, self.send_sems, self.recv_sems, self.token = out[:n], out[n], out[n + 1], out[n + 2]
        return self.token

    def finish(self, after):
        x, y, c = lax.axis_index("x"), lax.axis_index("y"), lax.axis_index("c")
        chip_core = jnp.stack([2 * x + y, c]).astype(jnp.int32)
        from_chips = _chip_scatter_finish_call(self.chip_sums, self.bufs, self.send_sems, self.recv_sems, after,
                                               self.tag + "scatter_finish")
        mine = _owner_sum_call(self.chip_sums, from_chips, chip_core, self.tag + "owner_sum")
        return dict(zip(self.names, _pair_gather_call(mine, self.tag + "pair_gather")))


def _small_sums(g, sq, got):
    total = _sum8_call(got).reshape(-1)
    parts = _unpack(total, [g[n].shape for n in _ALL_REDUCED] + [sq.shape])
    out = dict(zip(_ALL_REDUCED, parts))
    sq_total = parts[-1]
    chip = 2 * lax.axis_index("x") + lax.axis_index("y")
    for n in _SHARDED_F32:
        size = out[n].shape[2] // N_CHIPS
        out[n] = lax.dynamic_slice_in_dim(out[n], chip * size, size, axis=2)
    return out, sq_total


def _adamw_small(w, g, m, v):
    pick = lambda t: [t[n] for n in _SMALL]
    out = _adamw_small_call(pick(w), pick(g), pick(m), pick(v), "adamw_small")
    return tuple({n: out[3 * t + k] for t, n in enumerate(_SMALL)} for k in range(3))


def kernel(x, mem, positions, norm_g, w_in, cq_norm_g, ckv_norm_g, w_uq, w_ukv, mla_q_norm_g, mla_k_norm_g, conv_w, conv_b, sg_ln_g, sg_ln_b, w_spatial, b_spatial, mem_norm_g, w_mem_kv, mem_q_norm_g, mem_k_norm_g, b_merge, w_branch, w_out, loss_target, m_norm_g, m_w_in, m_cq_norm_g, m_ckv_norm_g, m_w_uq, m_w_ukv, m_mla_q_norm_g, m_mla_k_norm_g, m_conv_w, m_conv_b, m_sg_ln_g, m_sg_ln_b, m_w_spatial, m_b_spatial, m_mem_norm_g, m_w_mem_kv, m_mem_q_norm_g, m_mem_k_norm_g, m_b_merge, m_w_branch, m_w_out, v_norm_g, v_w_in, v_cq_norm_g, v_ckv_norm_g, v_w_uq, v_w_ukv, v_mla_q_norm_g, v_mla_k_norm_g, v_conv_w, v_conv_b, v_sg_ln_g, v_sg_ln_b, v_w_spatial, v_b_spatial, v_mem_norm_g, v_w_mem_kv, v_mem_q_norm_g, v_mem_k_norm_g, v_b_merge, v_w_branch, v_w_out):
    w = dict(norm_g=norm_g, w_in=w_in, cq_norm_g=cq_norm_g, ckv_norm_g=ckv_norm_g, w_uq=w_uq, w_ukv=w_ukv,
             mla_q_norm_g=mla_q_norm_g, mla_k_norm_g=mla_k_norm_g, conv_w=conv_w, conv_b=conv_b, sg_ln_g=sg_ln_g,
             sg_ln_b=sg_ln_b, w_spatial=w_spatial, b_spatial=b_spatial, mem_norm_g=mem_norm_g, w_mem_kv=w_mem_kv,
             mem_q_norm_g=mem_q_norm_g, mem_k_norm_g=mem_k_norm_g, b_merge=b_merge, w_branch=w_branch, w_out=w_out)
    m = dict(norm_g=m_norm_g, w_in=m_w_in, cq_norm_g=m_cq_norm_g, ckv_norm_g=m_ckv_norm_g, w_uq=m_w_uq, w_ukv=m_w_ukv,
             mla_q_norm_g=m_mla_q_norm_g, mla_k_norm_g=m_mla_k_norm_g, conv_w=m_conv_w, conv_b=m_conv_b, sg_ln_g=m_sg_ln_g,
             sg_ln_b=m_sg_ln_b, w_spatial=m_w_spatial, b_spatial=m_b_spatial, mem_norm_g=m_mem_norm_g, w_mem_kv=m_w_mem_kv,
             mem_q_norm_g=m_mem_q_norm_g, mem_k_norm_g=m_mem_k_norm_g, b_merge=m_b_merge, w_branch=m_w_branch, w_out=m_w_out)
    v = dict(norm_g=v_norm_g, w_in=v_w_in, cq_norm_g=v_cq_norm_g, ckv_norm_g=v_ckv_norm_g, w_uq=v_w_uq, w_ukv=v_w_ukv,
             mla_q_norm_g=v_mla_q_norm_g, mla_k_norm_g=v_mla_k_norm_g, conv_w=v_conv_w, conv_b=v_conv_b, sg_ln_g=v_sg_ln_g,
             sg_ln_b=v_sg_ln_b, w_spatial=v_w_spatial, b_spatial=v_b_spatial, mem_norm_g=v_mem_norm_g, w_mem_kv=v_w_mem_kv,
             mem_q_norm_g=v_mem_q_norm_g, mem_k_norm_g=v_mem_k_norm_g, b_merge=v_b_merge, w_branch=v_w_branch, w_out=v_w_out)

    chip_core = jnp.stack([2 * lax.axis_index("x") + lax.axis_index("y"), lax.axis_index("c")]).astype(jnp.int32)

    first = _Gather(w, 0, ("w_in",), chip_core, "gather_l0_w_in_")
    rest = _Gather(w, 0, _SHARDED_MM[1:], first.token, "gather_l0_rest_")
    small_on_its_way = _SmallGather(_pack_rows([w[n] for n in _SHARDED_F32], F32, 8), [rest.token], "gather_small_weights_")
    later = _Gather(w, 1, _SHARDED_MM, small_on_its_way.token, "gather_l1_")
    w_in0 = first.finish(first.pass_on(later.token))["w_in"]
    small = {}

    def rest_of_layer0(proj0):
        landed = _layer_params_rest(rest.finish(rest.pass_on(proj0)))
        small.update(_small_sharded_weights(w, small_on_its_way.finish([landed["wo"]])))
        return dict(landed, conv_w=small["conv_w"][0], bm=small["b_merge"][0])

    def layer1_params(saved0, act0):
        return _layer_params(1, w, later.finish(act0), small["conv_w"][1], small["b_merge"][1])

    params0 = _layer_params_first(0, w, w_in0, None, None)
    params = [dict(params0, late=rest_of_layer0, after_attn=later.pass_on), layer1_params]
    others = _SHARDED_MM[1:]
    rs1 = _ReduceScatter("rs_l1_", _SHARDED_MM)
    rs0_rest, rs0_w_in = _ReduceScatter("rs_l0_rest_", others), _ReduceScatter("rs_l0_w_in_", ("w_in",))

    def layer0_grads_done(grads):
        return [rs0_rest.scatter([grads["w_in_aligned"]]), rs0_w_in.exchange(grads)]

    hooks = [dict(on_rest_grads=rs0_rest.exchange, on_grads=layer0_grads_done),
             dict(on_grads=lambda grads: [rs1.exchange(grads)], after_layer=lambda dy: rs1.scatter([dy]))]
    sq, grad_x, layer_grads = _forward_backward(x[0], mem[0], positions[0], loss_target[0], params, hooks)

    g_small = {n: jnp.stack([layer_grads[l][n] for l in range(DEPTH)]) for n in _ALL_REDUCED}
    small_grads = _SmallGather(_pack_rows([g_small[n] for n in _ALL_REDUCED] + [sq], F32, 64), [grad_x], "gather_small_grads_")
    scattering = rs0_w_in.scatter([grad_x, small_grads.token])
    shard_grads = {1: rs1.finish([scattering]), 0: rs0_rest.finish([scattering])}
    as3d = lambda a: a.reshape(DEPTH, -1, a.shape[-1])
    as2d = lambda a: a.reshape(-1, a.shape[-1])
    big = lambda t: [as3d(t[n]) for n in others]
    turned = lambda t: [jnp.swapaxes(t["w_in"], 1, 2)]
    assert W_IN_SHARD % (8 * 7) == 0

    def update_w_in(l, grad, prev):
        return _adamw_layer_call(l, turned(w), [grad.T], turned(m), turned(v), prev, [], "adamw_w_in_l%d" % l, steps=7)

    def update_others(l, prev):
        return _adamw_layer_call(l, big(w), [as2d(shard_grads[l][n]) for n in others], big(m), big(v), prev, [], "adamw_l%d" % l)

    upd = update_others(0, update_others(1, None))
    g, sq_total = _small_sums(g_small, sq, small_grads.finish([upd[0]]))
    loss = 0.5 / D * jnp.sum(sq_total)
    delta, new_m, new_v = _adamw_small(w, g, m, v)
    upd_in1 = update_w_in(1, shard_grads[1]["w_in"], None)
    w_in_grad0 = rs0_w_in.finish([grad_x, upd_in1[0], upd[0], delta["norm_g"]])["w_in"]
    upd_in = update_w_in(0, w_in_grad0, upd_in1)
    g["w_in"], delta["w_in"], new_m["w_in"], new_v["w_in"] = [jnp.swapaxes(a, 1, 2) for a in upd_in]
    for t, n in enumerate(others):
        g[n], delta[n], new_m[n], new_v[n] = [a.reshape(w[n].shape) for a in upd[4 * t:4 * t + 4]]
    return (loss, grad_x[None], *[g[n] for n in _WEIGHTS], *[delta[n] for n in _WEIGHTS],
            *[new_m[n] for n in _WEIGHTS], *[new_v[n] for n in _WEIGHTS])
```

```python
import functools
import math

import jax
import jax.numpy as jnp
from jax import lax
from jax.experimental import pallas as pl
from jax.experimental.pallas import tpu as pltpu

F32 = jnp.float32
MM = jnp.bfloat16

D = 1024
DEPTH = 2
EPS = 1e-6
H = 8
NOPE = 64
ROPE = 32
QKH = 96
VH = 64
QL = 256
KVL = 128
ROPE_THETA = 10000.0
CW = 512
SGW = 512
SGG = 4
SGC = 128
MH = 4
MHD = 128
NB = 4
BW = 512
NEG_INF = -1e30
LANES = 128
N_CHIPS = 4

R_CQ, R_CKV, R_KR, R_CV, R_SGI, R_MQ, R_SG, R_ML, R_END = 0, 256, 384, 416, 1952, 2976, 3488, 5536, 9632
OFF_ML, OFF_SG, OFF_CV, OFF_SGI, OFF_MQ, OFF_CQ, OFF_CKV, OFF_KR, NP = 0, 4096, 6144, 7680, 8704, 9216, 9472, 9600, 9728

ADAM_LR = 0.001
ADAM_B1 = 0.9
ADAM_B2 = 0.999
ADAM_EPS = 1e-08
ADAM_WD = 0.01
ADAM_STEP = 10

VMEM_LIMIT = 56 * 1024 * 1024
PACK_W = 512
MESH_ID = pl.DeviceIdType.MESH


def _cparams(n_axes):
    return pltpu.CompilerParams(dimension_semantics=("arbitrary",) * n_axes, vmem_limit_bytes=VMEM_LIMIT)


def _bs(shape, imap):
    return pl.BlockSpec(shape, imap)


@jax.custom_vjp
def _mm_plain(a, b):
    return jnp.dot(a.astype(MM), b.astype(MM), preferred_element_type=F32)


def _mm_plain_fwd(a, b):
    return _mm_plain(a, b), (a, b)


def _mm_plain_bwd(res, g):
    a, b = res
    gm = g.astype(MM)
    da = lax.dot_general(gm, b.astype(MM), (((1,), (1,)), ((), ())), preferred_element_type=F32)
    db = lax.dot_general(a.astype(MM), gm, (((0,), (0,)), ((), ())), preferred_element_type=F32)
    return da.astype(a.dtype), db.astype(b.dtype)


_mm_plain.defvjp(_mm_plain_fwd, _mm_plain_bwd)


@jax.custom_vjp
def _mm_slot(a, w, slot):
    return jnp.dot(a.astype(MM), w.astype(MM), preferred_element_type=F32)


def _mm_slot_fwd(a, w, slot):
    return _mm_slot(a, w, slot), (a, w)


def _mm_slot_bwd(res, g):
    a, w = res
    gm = g.astype(MM)
    da = lax.dot_general(gm, w.astype(MM), (((1,), (1,)), ((), ())), preferred_element_type=F32)
    dw = lax.dot_general(a.astype(MM), gm, (((0,), (0,)), ((), ())), preferred_element_type=F32)
    return da.astype(a.dtype), jnp.zeros_like(w), dw


_mm_slot.defvjp(_mm_slot_fwd, _mm_slot_bwd)


def _mm(a, b):
    if isinstance(b, tuple):
        return _mm_slot(a, b[0], b[1])
    return _mm_plain(a, b)


def _with_slot(w):
    return (w, jnp.zeros(w.shape, F32))


@jax.custom_vjp
def _mm_nt(a, b):
    return lax.dot_general(a.astype(MM), b.astype(MM), (((1,), (1,)), ((), ())), preferred_element_type=F32)


def _mm_nt_fwd(a, b):
    return _mm_nt(a, b), (a, b)


def _mm_nt_bwd(res, g):
    a, b = res
    gm = g.astype(MM)
    da = jnp.dot(gm, b.astype(MM), preferred_element_type=F32)
    db = lax.dot_general(gm, a.astype(MM), (((0,), (0,)), ((), ())), preferred_element_type=F32)
    return da.astype(a.dtype), db.astype(b.dtype)


_mm_nt.defvjp(_mm_nt_fwd, _mm_nt_bwd)


@functools.partial(jax.custom_vjp, nondiff_argnums=(1,))
def _lane_roll(x, shift):
    return pltpu.roll(x, shift, 1)


def _lane_roll_fwd(x, shift):
    return pltpu.roll(x, shift, 1), None


def _lane_roll_bwd(shift, _, g):
    return (pltpu.roll(g, (LANES - shift) % LANES, 1),)


_lane_roll.defvjp(_lane_roll_fwd, _lane_roll_bwd)


def _rms_n(x, g, n):
    ms = jnp.sum(x * x, axis=-1, keepdims=True) * (1.0 / n)
    return x * lax.rsqrt(ms + EPS) * g


def _softmax(s):
    m = jnp.max(s, axis=-1, keepdims=True)
    e = jnp.exp(s - m)
    return e / jnp.sum(e, axis=-1, keepdims=True)


def _rope(t, cos_t, sin_a, sin_b):
    return t * cos_t + _lane_roll(t, LANES - 16) * sin_a + _lane_roll(t, 16) * sin_b


def _mla_prep_fn(cq, ckv, kr, cos_t, sin_a, sin_b, cq_g, ckv_g, qg, kg, wuq, wkn, wv):
    cqn = _rms_n(cq, cq_g, QL)
    ckvn = _rms_n(ckv, ckv_g, KVL)
    lane = lax.broadcasted_iota(jnp.int32, kr.shape, 1)
    krm = jnp.where((lane >= NOPE) & (lane < QKH), kr, 0.0)
    qs, ks = [], []
    for h in range(H):
        qh = _rms_n(_mm(cqn, wuq[h]), qg, QKH)
        qs.append(_rope(qh, cos_t, sin_a, sin_b) * (QKH ** -0.5))
        kh = _rms_n(_mm(ckvn, wkn[h]) + krm, kg, QKH)
        ks.append(_rope(kh, cos_t, sin_a, sin_b))
    return jnp.concatenate(qs, axis=-1), jnp.concatenate(ks, axis=-1), _mm(ckvn, wv)


def _dot_nt(a, b):
    return lax.dot_general(a.astype(MM), b.astype(MM), (((1,), (1,)), ((), ())), preferred_element_type=F32)


def _dot_tn(a, b):
    return lax.dot_general(a.astype(MM), b.astype(MM), (((0,), (0,)), ((), ())), preferred_element_type=F32)


def _causal_scores(qe, ke):
    tq, kl = qe.shape[0], ke.shape[0]
    s = _dot_nt(qe, ke)
    rows = lax.broadcasted_iota(jnp.int32, (tq, tq), 0)
    cols = lax.broadcasted_iota(jnp.int32, (tq, tq), 1)
    own = jnp.where(cols <= rows, s[:, kl - tq:], NEG_INF)
    return own if kl == tq else jnp.concatenate([s[:, :kl - tq], own], axis=1)


def _head_lanes(e, shape):
    lane = lax.broadcasted_iota(jnp.int32, shape, len(shape) - 1)
    return (lane >= VH * e) & (lane < VH * (e + 1))


def _attn_pair_fwd(q2, k2, v2):
    tq = q2.shape[0]
    o = jnp.zeros((tq, LANES), F32)
    lse = jnp.zeros((tq, LANES), F32)
    for e in range(2):
        sl = slice(LANES * e, LANES * (e + 1))
        s = _causal_scores(q2[:, sl], k2[:, sl])
        m = jnp.max(s, axis=-1, keepdims=True)
        ex = jnp.exp(s - m)
        l = jnp.sum(ex, axis=-1, keepdims=True)
        ve = jnp.where(_head_lanes(e, v2[:, sl].shape), v2[:, sl], 0.0)
        o = o + jnp.dot((ex * (1.0 / l)).astype(MM), ve.astype(MM), preferred_element_type=F32)
        lse = jnp.where(_head_lanes(e, lse.shape), m + jnp.log(l), lse)
    return o, lse


def _attn_pair_bwd(q2, k2, v2, sg, dys, o, lse):
    sig = jax.nn.sigmoid(sg)
    do = dys * (sg * sig)
    dsg = dys * o * (sig * (1.0 + sg * (1.0 - sig)))
    dqs, dks, dvs = [], [], []
    for e in range(2):
        sl = slice(LANES * e, LANES * (e + 1))
        qe, ke = q2[:, sl], k2[:, sl]
        hm = _head_lanes(e, o.shape)
        lse_e = jnp.max(jnp.where(hm, lse, NEG_INF), axis=-1, keepdims=True)
        do_e = jnp.where(hm, do, 0.0)
        delta = jnp.sum(do_e * o, axis=-1, keepdims=True)
        p = jnp.exp(_causal_scores(qe, ke) - lse_e)
        ve = jnp.where(_head_lanes(e, v2[:, sl].shape), v2[:, sl], 0.0)
        dvs.append(_dot_tn(p, do_e))
        ds = p * (_dot_nt(do_e, ve) - delta)
        dqs.append(jnp.dot(ds.astype(MM), ke.astype(MM), preferred_element_type=F32))
        dks.append(_dot_tn(ds, qe))
    return jnp.concatenate(dqs, axis=-1), jnp.concatenate(dks, axis=-1), jnp.concatenate(dvs, axis=-1), dsg


def _sg_fn(u, v, sgc, ln_g, ln_b, ws, bs):
    mu = jnp.mean(v, axis=-1, keepdims=True)
    xc = v - mu
    vn = xc * lax.rsqrt(jnp.mean(xc * xc, axis=-1, keepdims=True) + EPS) * ln_g + ln_b
    r = lax.broadcasted_iota(jnp.int32, (SGC, SGC), 0)
    c = lax.broadcasted_iota(jnp.int32, (SGC, SGC), 1)
    wt = [jnp.where(r >= c, w, 0.0) for w in ws]
    row_blocks = []
    for ch in range(u.shape[0] // SGC):
        col_blocks = []
        for g in range(SGG):
            blk = vn[SGC * ch:SGC * (ch + 1), LANES * g:LANES * (g + 1)]
            col_blocks.append(_mm(wt[g], blk) + bs[g])
        row_blocks.append(jnp.concatenate(col_blocks, axis=-1))
    mixed = jnp.concatenate(row_blocks, axis=0)
    return (u * mixed) * jax.nn.silu(sgc)


def _memkv_fn(mem, mem_g, wm, kg):
    kv = _mm(_rms_n(mem, mem_g, D), wm)
    ks = [_rms_n(kv[:, MHD * h:MHD * (h + 1)], kg, MHD) for h in range(MH)]
    return jnp.concatenate(ks, axis=-1), kv[:, MH * MHD:]


def _mem_fn(mq, sgd, k, v, qg):
    outs = []
    for h in range(MH):
        sl = slice(MHD * h, MHD * (h + 1))
        qh = _rms_n(mq[:, sl], qg, MHD)
        p = _softmax(_mm_nt(qh, k[:, sl]) * (MHD ** -0.5))
        outs.append(_mm(p, v[:, sl]))
    return jnp.concatenate(outs, axis=-1) * jax.nn.silu(sgd)


def _merge_fn(ys, logits, bm, wb, wo):
    merged = None
    for n in range(NB):
        z = jnp.concatenate([_mm(ys[n], wb[j][n]) for j in range(N_CHIPS)], axis=-1)
        gate = jax.nn.sigmoid(logits[:, D * n:D * (n + 1)] + bm[n])
        merged = gate * z if merged is None else merged + gate * z
    return _mm(merged, wo)


def _proj_call(x, g, w):
    s_len = x.shape[0]
    tm, tn = s_len, 512

    def body(x_ref, g_ref, w_ref, p_ref, h_ref):
        @pl.when(pl.program_id(1) == 0)
        def _():
            h_ref[...] = _rms_n(x_ref[...], g_ref[...], D).astype(h_ref.dtype)
        p_ref[...] = jnp.dot(h_ref[...], w_ref[...], preferred_element_type=F32)

    return pl.pallas_call(
        body, grid=(s_len // tm, NP // tn),
        in_specs=[_bs((tm, D), lambda i, j: (i, 0)), _bs((1, D), lambda i, j: (0, 0)), _bs((D, tn), lambda i, j: (0, j))],
        out_specs=[_bs((tm, tn), lambda i, j: (i, j)), _bs((tm, D), lambda i, j: (i, 0))],
        out_shape=[jax.ShapeDtypeStruct((s_len, NP), F32), jax.ShapeDtypeStruct((s_len, D), MM)],
        name="proj", compiler_params=_cparams(2))(x, g, w)


def _rope_tables(pos):
    half = ROPE // 2
    inv_freq = ROPE_THETA ** (-jnp.arange(half, dtype=F32) / half)
    ang = pos.astype(F32)[:, None] * inv_freq
    cos, sin = jnp.cos(ang), jnp.sin(ang)
    s_len = pos.shape[0]
    z = lambda n: jnp.zeros((s_len, n), F32)
    cos_t = jnp.concatenate([jnp.ones((s_len, NOPE), F32), cos, cos, z(LANES - QKH)], axis=1)
    sin_a = jnp.concatenate([z(NOPE), -sin, z(LANES - NOPE - half)], axis=1)
    sin_b = jnp.concatenate([z(NOPE + half), sin, z(LANES - QKH)], axis=1)
    return cos_t, sin_a, sin_b


def _mla_prep_specs(tm):
    row = lambda w, off: _bs((tm, w), lambda i: (i, off // w))
    full2 = lambda a, b: _bs((a, b), lambda i: (0, 0))
    full3 = lambda a, b, c: _bs((a, b, c), lambda i: (0, 0, 0))
    tab = _bs((tm, LANES), lambda i: (i, 0))
    return [row(QL, OFF_CQ), row(KVL, OFF_CKV), row(LANES, OFF_KR), tab, tab, tab,
            full2(1, QL), full2(1, KVL), full2(1, LANES), full2(1, LANES),
            full3(H, QL, LANES), full3(H, KVL, LANES), full2(KVL, H * LANES)]


def _mla_prep_args(body_refs, wrap=lambda w: w):
    (cq, ckv, kr, ct, sa, sb, cqg, ckvg, qg, kg, wuq, wkn, wv) = body_refs
    return (cq[...], ckv[...], kr[...], ct[...], sa[...], sb[...], cqg[...], ckvg[...], qg[...], kg[...],
            [wrap(wuq[h]) for h in range(H)], [wrap(wkn[h]) for h in range(H)], wrap(wv[...]))


def _mla_prep_call(proj, tabs, cq_g, ckv_g, qg, kg, wuq, wkn, wv):
    s_len = proj.shape[0]
    tm = min(s_len, 256)

    def body(*refs):
        q_ref, k_ref, v_ref = refs[13:]
        q, k, v = _mla_prep_fn(*_mla_prep_args(refs[:13]))
        q_ref[...] = q.astype(q_ref.dtype)
        k_ref[...] = k.astype(k_ref.dtype)
        v_ref[...] = v.astype(v_ref.dtype)

    out = _bs((tm, H * LANES), lambda i: (i, 0))
    return pl.pallas_call(
        body, grid=(s_len // tm,), in_specs=_mla_prep_specs(tm), out_specs=[out, out, out],
        out_shape=[jax.ShapeDtypeStruct((s_len, H * LANES), MM)] * 3,
        name="mla_prep", compiler_params=_cparams(1))(proj, proj, proj, *tabs, cq_g, ckv_g, qg, kg, wuq, wkn, wv)


def _mla_prep_bwd_call(proj, tabs, cq_g, ckv_g, qg, kg, wuq, wkn, wv, dq, dk, dv):
    s_len = proj.shape[0]
    tm = min(s_len, 256)

    def body(*refs):
        dq_ref, dk_ref, dv_ref = refs[13:16]
        dlat_ref, dcqg_ref, dckvg_ref, dqg_ref, dkg_ref, dwuq_ref, dwkn_ref, dwv_ref = refs[16:]
        _, vjp = jax.vjp(_mla_prep_fn, *_mla_prep_args(refs[:13], _with_slot))
        (dcq, dckv, dkr, _, _, _, dcqg, dckvg, dqg, dkg, dwuq, dwkn, dwv) = vjp((dq_ref[...], dk_ref[...], dv_ref[...]))
        dwuq, dwkn, dwv = [d[1] for d in dwuq], [d[1] for d in dwkn], dwv[1]
        dlat_ref[...] = jnp.concatenate([dcq, dckv, dkr], axis=-1).astype(dlat_ref.dtype)

        @pl.when(pl.program_id(0) == 0)
        def _():
            for r in (dcqg_ref, dckvg_ref, dqg_ref, dkg_ref, dwuq_ref, dwkn_ref, dwv_ref):
                r[...] = jnp.zeros_like(r)
        dcqg_ref[...] += dcqg
        dckvg_ref[...] += dckvg
        dqg_ref[...] += dqg
        dkg_ref[...] += dkg
        for h in range(H):
            dwuq_ref[h] += dwuq[h]
            dwkn_ref[h] += dwkn[h]
        dwv_ref[...] += dwv

    big = _bs((tm, H * LANES), lambda i: (i, 0))
    row = lambda w: _bs((tm, w), lambda i: (i, 0))
    full2 = lambda a, b: _bs((a, b), lambda i: (0, 0))
    full3 = lambda a, b, c: _bs((a, b, c), lambda i: (0, 0, 0))
    sd = jax.ShapeDtypeStruct
    return pl.pallas_call(
        body, grid=(s_len // tm,), in_specs=_mla_prep_specs(tm) + [big, big, big],
        out_specs=[row(QL + KVL + LANES), full2(1, QL), full2(1, KVL), full2(1, LANES), full2(1, LANES),
                   full3(H, QL, LANES), full3(H, KVL, LANES), full2(KVL, H * LANES)],
        out_shape=[sd((s_len, QL + KVL + LANES), MM), sd((1, QL), F32), sd((1, KVL), F32),
                   sd((1, LANES), F32), sd((1, LANES), F32), sd((H, QL, LANES), F32), sd((H, KVL, LANES), F32),
                   sd((KVL, H * LANES), F32)],
        name="mla_prep_bwd", compiler_params=_cparams(1))(proj, proj, proj, *tabs, cq_g, ckv_g, qg, kg, wuq, wkn, wv, dq, dk, dv)


def _attn_specs(s_len, tq):
    pair = 2 * LANES
    return [_bs((tq, pair), lambda p, i: (i, p)), _bs((s_len, pair), lambda p, i: (0, p)), _bs((s_len, pair), lambda p, i: (0, p)),
            _bs((tq, LANES), lambda p, i: (i, OFF_SG // LANES + p))]


def _attn_call(q, k, v, proj):
    s_len = q.shape[0]
    tq = min(s_len, 256)

    def body(q_ref, k_ref, v_ref, sg_ref, y_ref, o_ref, lse_ref):
        for n in range(s_len // tq):
            @pl.when(pl.program_id(1) == n)
            def _():
                kl = (n + 1) * tq
                o, lse = _attn_pair_fwd(q_ref[...], k_ref[:kl, :], v_ref[:kl, :])
                y_ref[...] = (o * jax.nn.silu(sg_ref[...])).astype(y_ref.dtype)
                o_ref[...] = o
                lse_ref[...] = lse

    tile = _bs((tq, LANES), lambda p, i: (i, p))
    sd = jax.ShapeDtypeStruct
    return pl.pallas_call(
        body, grid=(H // 2, s_len // tq), in_specs=_attn_specs(s_len, tq), out_specs=[tile, tile, tile],
        out_shape=[sd((s_len, BW), MM), sd((s_len, BW), F32), sd((s_len, BW), F32)],
        name="attn", compiler_params=_cparams(2))(q, k, v, proj)


def _attn_bwd_call(q, k, v, proj, dys, o, lse):
    s_len = q.shape[0]
    tq = min(s_len, 256)
    pair = 2 * LANES

    def body(q_ref, k_ref, v_ref, sg_ref, dy_ref, o_ref, lse_ref, dq_ref, dk_ref, dv_ref, dsg_ref):
        i = pl.program_id(1)

        @pl.when(i == 0)
        def _():
            dk_ref[...] = jnp.zeros_like(dk_ref)
            dv_ref[...] = jnp.zeros_like(dv_ref)

        for n in range(s_len // tq):
            @pl.when(i == n)
            def _():
                kl = (n + 1) * tq
                dq, dk, dv, dsg = _attn_pair_bwd(q_ref[...], k_ref[:kl, :], v_ref[:kl, :], sg_ref[...], dy_ref[...],
                                                 o_ref[...], lse_ref[...])
                dq_ref[...] = dq
                dsg_ref[...] = dsg.astype(dsg_ref.dtype)
                dk_ref[:kl, :] += dk
                dv_ref[:kl, :] += dv

    sd = jax.ShapeDtypeStruct
    tile = _bs((tq, LANES), lambda p, i: (i, p))
    return pl.pallas_call(
        body, grid=(H // 2, s_len // tq),
        in_specs=_attn_specs(s_len, tq) + [tile, tile, tile],
        out_specs=[_bs((tq, pair), lambda p, i: (i, p)), _bs((s_len, pair), lambda p, i: (0, p)),
                   _bs((s_len, pair), lambda p, i: (0, p)), tile],
        out_shape=[sd((s_len, H * LANES), F32), sd((s_len, H * LANES), F32), sd((s_len, H * LANES), F32), sd((s_len, BW), MM)],
        name="attn_bwd", compiler_params=_cparams(2))(q, k, v, proj, dys, o, lse)


def _shift_down(a, n):
    r = lax.broadcasted_iota(jnp.int32, a.shape, 0)
    return jnp.where(r >= n, pltpu.roll(a, n, 0), 0.0)


def _shift_up(a, n):
    s_len = a.shape[0]
    r = lax.broadcasted_iota(jnp.int32, a.shape, 0)
    return jnp.where(r < s_len - n, pltpu.roll(a, s_len - n, 0), 0.0)


def _conv_specs(s_len):
    col = lambda off: _bs((s_len, LANES), lambda j: (0, off // LANES + j))
    return [col(OFF_CV), col(OFF_CV + CW), col(OFF_CV + 2 * CW), col(OFF_SG + BW),
            _bs((3, LANES), lambda j: (0, j)), _bs((1, LANES), lambda j: (0, j))]


def _conv_call(proj, cw, cb):
    s_len = proj.shape[0]

    def body(bg_ref, cg_ref, xi_ref, sg_ref, w_ref, b_ref, y_ref):
        z = cg_ref[...] * xi_ref[...]
        y = b_ref[...] + w_ref[0:1, :] * _shift_down(z, 2)
        y = y + w_ref[1:2, :] * _shift_down(z, 1)
        y = y + w_ref[2:3, :] * z
        y_ref[...] = ((bg_ref[...] * y) * jax.nn.silu(sg_ref[...])).astype(y_ref.dtype)

    return pl.pallas_call(
        body, grid=(CW // LANES,), in_specs=_conv_specs(s_len), out_specs=_bs((s_len, LANES), lambda j: (0, j)),
        out_shape=jax.ShapeDtypeStruct((s_len, CW), MM), name="conv", compiler_params=_cparams(1))(proj, proj, proj, proj, cw, cb)


def _conv_bwd_call(proj, cw, cb, dys):
    s_len = proj.shape[0]

    def body(bg_ref, cg_ref, xi_ref, sg_ref, w_ref, b_ref, dys_ref, dbg_ref, dcg_ref, dxi_ref, dsg_ref, dw_ref, db_ref):
        bg, cg, xi, sg = bg_ref[...], cg_ref[...], xi_ref[...], sg_ref[...]
        w0, w1, w2 = w_ref[0:1, :], w_ref[1:2, :], w_ref[2:3, :]
        z = cg * xi
        z1, z2 = _shift_down(z, 1), _shift_down(z, 2)
        y = b_ref[...] + w0 * z2
        y = y + w1 * z1
        y = y + w2 * z
        yb = bg * y
        sig = jax.nn.sigmoid(sg)
        silu = sg * sig
        dys_v = dys_ref[...]
        dsg_ref[...] = (dys_v * yb * (sig * (1.0 + sg * (1.0 - sig)))).astype(dsg_ref.dtype)
        dyb = dys_v * silu
        dbg_ref[...] = (dyb * y).astype(dbg_ref.dtype)
        dy = dyb * bg
        db_ref[...] = jnp.sum(dy, axis=0, keepdims=True)
        dw_ref[0:1, :] = jnp.sum(dy * z2, axis=0, keepdims=True)
        dw_ref[1:2, :] = jnp.sum(dy * z1, axis=0, keepdims=True)
        dw_ref[2:3, :] = jnp.sum(dy * z, axis=0, keepdims=True)
        dz = w2 * dy + w1 * _shift_up(dy, 1) + w0 * _shift_up(dy, 2)
        dcg_ref[...] = (dz * xi).astype(dcg_ref.dtype)
        dxi_ref[...] = (dz * cg).astype(dxi_ref.dtype)

    col = _bs((s_len, LANES), lambda j: (0, j))
    sd = jax.ShapeDtypeStruct
    return pl.pallas_call(
        body, grid=(CW // LANES,), in_specs=_conv_specs(s_len) + [col],
        out_specs=[col, col, col, col, _bs((3, LANES), lambda j: (0, j)), _bs((1, LANES), lambda j: (0, j))],
        out_shape=[sd((s_len, CW), MM)] * 4 + [sd((3, CW), F32), sd((1, CW), F32)],
        name="conv_bwd", compiler_params=_cparams(1))(proj, proj, proj, proj, cw, cb, dys)


def _sg_specs(tm):
    row = lambda off: _bs((tm, SGW), lambda i: (i, off // SGW))
    return [row(OFF_SGI), row(OFF_SGI + SGW), row(OFF_SG + 2 * BW), _bs((1, SGW), lambda i: (0, 0)), _bs((1, SGW), lambda i: (0, 0)),
            _bs((SGG, SGC, SGC), lambda i: (0, 0, 0)), _bs((SGG, SGC, 1), lambda i: (0, 0, 0))]


def _sg_args(refs):
    u, v, sg, lg, lb, ws, bs = refs
    return (u[...], v[...], sg[...], lg[...], lb[...], [ws[g] for g in range(SGG)], [bs[g] for g in range(SGG)])


def _sg_call(proj, ln_g, ln_b, ws, bs):
    s_len = proj.shape[0]
    tm = min(s_len, 256)

    def body(*refs):
        refs[7][...] = _sg_fn(*_sg_args(refs[:7])).astype(refs[7].dtype)

    return pl.pallas_call(
        body, grid=(s_len // tm,), in_specs=_sg_specs(tm), out_specs=_bs((tm, SGW), lambda i: (i, 0)),
        out_shape=jax.ShapeDtypeStruct((s_len, SGW), MM), name="sgmlp", compiler_params=_cparams(1))(proj, proj, proj, ln_g, ln_b, ws, bs)


def _sg_bwd_call(proj, ln_g, ln_b, ws, bs, dys):
    s_len = proj.shape[0]
    tm = min(s_len, 256)

    def body(*refs):
        dys_ref = refs[7]
        duv_ref, dsg_ref, dlg_ref, dlb_ref, dws_ref, dbs_ref = refs[8:]
        _, vjp = jax.vjp(_sg_fn, *_sg_args(refs[:7]))
        du, dv, dsg, dlg, dlb, dws, dbs = vjp(dys_ref[...])
        duv_ref[...] = jnp.concatenate([du, dv], axis=-1).astype(duv_ref.dtype)
        dsg_ref[...] = dsg.astype(dsg_ref.dtype)

        @pl.when(pl.program_id(0) == 0)
        def _():
            for r in (dlg_ref, dlb_ref, dws_ref, dbs_ref):
                r[...] = jnp.zeros_like(r)
        dlg_ref[...] += dlg
        dlb_ref[...] += dlb
        for g in range(SGG):
            dws_ref[g] += dws[g]
            dbs_ref[g] += dbs[g]

    row = _bs((tm, SGW), lambda i: (i, 0))
    sd = jax.ShapeDtypeStruct
    return pl.pallas_call(
        body, grid=(s_len // tm,), in_specs=_sg_specs(tm) + [row],
        out_specs=[_bs((tm, 2 * SGW), lambda i: (i, 0)), row, _bs((1, SGW), lambda i: (0, 0)), _bs((1, SGW), lambda i: (0, 0)),
                   _bs((SGG, SGC, SGC), lambda i: (0, 0, 0)), _bs((SGG, SGC, 1), lambda i: (0, 0, 0))],
        out_shape=[sd((s_len, 2 * SGW), MM), sd((s_len, SGW), MM), sd((1, SGW), F32), sd((1, SGW), F32),
                   sd((SGG, SGC, SGC), F32), sd((SGG, SGC, 1), F32)],
        name="sgmlp_bwd", compiler_params=_cparams(1))(proj, proj, proj, ln_g, ln_b, ws, bs, dys)


def _memkv_call(mem, mem_g, wm, kg):
    m_len = mem.shape[0]

    def body(mem_ref, g_ref, w_ref, kg_ref, k_ref, v_ref):
        k, v = _memkv_fn(mem_ref[...], g_ref[...], w_ref[...], kg_ref[...])
        k_ref[...] = k.astype(k_ref.dtype)
        v_ref[...] = v.astype(v_ref.dtype)

    return pl.pallas_call(body, out_shape=[jax.ShapeDtypeStruct((m_len, MH * MHD), MM)] * 2, name="memkv",
                          compiler_params=pltpu.CompilerParams(vmem_limit_bytes=VMEM_LIMIT))(mem, mem_g, wm, kg)


def _memkv_bwd_call(mem, mem_g, wm, kg, dk, dv):
    def body(mem_ref, g_ref, w_ref, kg_ref, dk_ref, dv_ref, dg_ref, dw_ref, dkg_ref):
        _, vjp = jax.vjp(_memkv_fn, mem_ref[...], g_ref[...], _with_slot(w_ref[...]), kg_ref[...])
        _, dg, dw, dkg = vjp((dk_ref[...], dv_ref[...]))
        dg_ref[...] = dg
        dw_ref[...] = dw[1]
        dkg_ref[...] = dkg

    sd = jax.ShapeDtypeStruct
    return pl.pallas_call(body, out_shape=[sd((1, D), F32), sd((D, 2 * MH * MHD), F32), sd((1, MHD), F32)], name="memkv_bwd",
                          compiler_params=pltpu.CompilerParams(vmem_limit_bytes=VMEM_LIMIT))(mem, mem_g, wm, kg, dk, dv)


def _mem_specs(tm, m_len):
    w = MH * MHD
    return [_bs((tm, w), lambda i: (i, OFF_MQ // w)), _bs((tm, BW), lambda i: (i, (OFF_SG + 3 * BW) // BW)),
            _bs((m_len, w), lambda i: (0, 0)), _bs((m_len, w), lambda i: (0, 0)), _bs((1, MHD), lambda i: (0, 0))]


def _mem_call(proj, k, v, qg):
    s_len, m_len = proj.shape[0], k.shape[0]
    tm = min(s_len, 256)

    def body(mq_ref, sg_ref, k_ref, v_ref, qg_ref, y_ref):
        y_ref[...] = _mem_fn(mq_ref[...], sg_ref[...], k_ref[...], v_ref[...], qg_ref[...]).astype(y_ref.dtype)

    return pl.pallas_call(
        body, grid=(s_len // tm,), in_specs=_mem_specs(tm, m_len), out_specs=_bs((tm, BW), lambda i: (i, 0)),
        out_shape=jax.ShapeDtypeStruct((s_len, BW), MM), name="memattn", compiler_params=_cparams(1))(proj, proj, k, v, qg)


def _mem_bwd_call(proj, k, v, qg, dys):
    s_len, m_len = proj.shape[0], k.shape[0]
    tm = min(s_len, 256)
    w = MH * MHD

    def body(mq_ref, sg_ref, k_ref, v_ref, qg_ref, dys_ref, dmq_ref, dsg_ref, dk_ref, dv_ref, dqg_ref):
        _, vjp = jax.vjp(_mem_fn, mq_ref[...], sg_ref[...], k_ref[...].astype(F32), v_ref[...].astype(F32), qg_ref[...])
        dmq, dsg, dk, dv, dqg = vjp(dys_ref[...])
        dmq_ref[...] = dmq.astype(dmq_ref.dtype)
        dsg_ref[...] = dsg.astype(dsg_ref.dtype)

        @pl.when(pl.program_id(0) == 0)
        def _():
            for r in (dk_ref, dv_ref, dqg_ref):
                r[...] = jnp.zeros_like(r)
        dk_ref[...] += dk
        dv_ref[...] += dv
        dqg_ref[...] += dqg

    row = _bs((tm, BW), lambda i: (i, 0))
    kv = _bs((m_len, w), lambda i: (0, 0))
    sd = jax.ShapeDtypeStruct
    return pl.pallas_call(
        body, grid=(s_len // tm,), in_specs=_mem_specs(tm, m_len) + [row],
        out_specs=[row, row, kv, kv, _bs((1, MHD), lambda i: (0, 0))],
        out_shape=[sd((s_len, w), MM), sd((s_len, BW), MM), sd((m_len, w), F32), sd((m_len, w), F32), sd((1, MHD), F32)],
        name="memattn_bwd", compiler_params=_cparams(1))(proj, proj, k, v, qg, dys)


def _merge_specs(tm):
    row = _bs((tm, BW), lambda i: (i, 0))
    return [row, row, row, row, _bs((tm, NB * D), lambda i: (i, OFF_ML // (NB * D))), _bs((NB, D), lambda i: (0, 0)),
            _bs((N_CHIPS, NB, BW, D // N_CHIPS), lambda i: (0, 0, 0, 0)), _bs((D, D), lambda i: (0, 0))]


def _merge_call(ys, proj, bm, wb, wo, x):
    s_len = proj.shape[0]
    tm = min(s_len, 256)

    def body(ya, yb, yc, yd, lg_ref, bm_ref, wb_ref, wo_ref, x_ref, o_ref):
        out = _merge_fn([r[...] for r in (ya, yb, yc, yd)], lg_ref[...], [bm_ref[n:n + 1, :] for n in range(NB)],
                        [[wb_ref[j, n] for n in range(NB)] for j in range(N_CHIPS)], wo_ref[...])
        o_ref[...] = x_ref[...] + out

    xrow = _bs((tm, D), lambda i: (i, 0))
    return pl.pallas_call(
        body, grid=(s_len // tm,), in_specs=_merge_specs(tm) + [xrow], out_specs=xrow,
        out_shape=jax.ShapeDtypeStruct((s_len, D), F32), name="merge", compiler_params=_cparams(1))(*ys, proj, bm, wb, wo, x)


def _merge_bwd_call(ys, proj, bm, wb, wo, dout):
    s_len = proj.shape[0]
    tm = min(s_len, 256)

    def body(ya, yb, yc, yd, lg_ref, bm_ref, wb_ref, wo_ref, do_ref, dya, dyb, dyc, dyd, dlg_ref, dbm_ref, dwb_ref, dwo_ref):
        fn = lambda ys_, lg_, bm_, wb_, wo_: _merge_fn(ys_, lg_, bm_, wb_, wo_)
        _, vjp = jax.vjp(fn, [r[...].astype(F32) for r in (ya, yb, yc, yd)], lg_ref[...], [bm_ref[n:n + 1, :] for n in range(NB)],
                         [[_with_slot(wb_ref[j, n]) for n in range(NB)] for j in range(N_CHIPS)], _with_slot(wo_ref[...]))
        dys, dlg, dbm, dwb, dwo = vjp(do_ref[...])
        dwb, dwo = [[d[1] for d in row] for row in dwb], dwo[1]
        for r, d in zip((dya, dyb, dyc, dyd), dys):
            r[...] = d
        dlg_ref[...] = dlg.astype(dlg_ref.dtype)

        @pl.when(pl.program_id(0) == 0)
        def _():
            for r in (dbm_ref, dwb_ref, dwo_ref):
                r[...] = jnp.zeros_like(r)
        for n in range(NB):
            dbm_ref[n:n + 1, :] += dbm[n]
            for j in range(N_CHIPS):
                dwb_ref[j, n] += dwb[j][n]
        dwo_ref[...] += dwo

    row = _bs((tm, BW), lambda i: (i, 0))
    sd = jax.ShapeDtypeStruct
    wb_shape = (N_CHIPS, NB, BW, D // N_CHIPS)
    return pl.pallas_call(
        body, grid=(s_len // tm,), in_specs=_merge_specs(tm) + [_bs((tm, D), lambda i: (i, 0))],
        out_specs=[row, row, row, row, _bs((tm, NB * D), lambda i: (i, 0)), _bs((NB, D), lambda i: (0, 0)),
                   _bs(wb_shape, lambda i: (0, 0, 0, 0)), _bs((D, D), lambda i: (0, 0))],
        out_shape=[sd((s_len, BW), F32)] * 4 + [sd((s_len, NP), MM), sd((NB, D), F32), sd(wb_shape, F32), sd((D, D), F32)],
        name="merge_bwd", compiler_params=_cparams(1))(*ys, proj, bm, wb, wo, dout)


def _dh_call(dproj, w, x, g, dout, after=()):
    s_len = x.shape[0]
    tk = NP // 4
    after = list(after)

    def matmul_body(dp_ref, w_ref, *rest):
        o_ref = rest[-1]

        @pl.when(pl.program_id(0) == 0)
        def _():
            o_ref[...] = jnp.zeros_like(o_ref)
        o_ref[...] += lax.dot_general(dp_ref[...], w_ref[...], (((1,), (1,)), ((), ())), preferred_element_type=F32)

    dh = pl.pallas_call(
        matmul_body, grid=(NP // tk,),
        in_specs=[_bs((s_len, tk), lambda k: (0, k)), _bs((D, tk), lambda k: (0, k))] + [_ANY] * len(after),
        out_specs=_bs((s_len, D), lambda k: (0, 0)), out_shape=jax.ShapeDtypeStruct((s_len, D), F32),
        name="dh", compiler_params=_cparams(1))(dproj, w, *after)

    tm = min(s_len, 512)

    def norm_body(dh_ref, x_ref, g_ref, do_ref, dx_ref, dg_ref):
        _, vjp = jax.vjp(lambda x_, g_: _rms_n(x_, g_, D), x_ref[...], g_ref[...])
        dxr, dgr = vjp(dh_ref[...])
        dx_ref[...] = do_ref[...] + dxr

        @pl.when(pl.program_id(0) == 0)
        def _():
            dg_ref[...] = jnp.zeros_like(dg_ref)
        dg_ref[...] += dgr

    row = _bs((tm, D), lambda i: (i, 0))
    return pl.pallas_call(
        norm_body, grid=(s_len // tm,), in_specs=[row, row, _bs((1, D), lambda i: (0, 0)), row],
        out_specs=[row, _bs((1, D), lambda i: (0, 0))],
        out_shape=[jax.ShapeDtypeStruct((s_len, D), F32), jax.ShapeDtypeStruct((1, D), F32)],
        name="norm_bwd", compiler_params=_cparams(1))(dh, x, g, dout)


def _dw_call(h, dproj, after=()):
    s_len = h.shape[0]
    tn = 512
    after = list(after)

    def body(h_ref, dp_ref, *rest):
        o_ref, ht_ref = rest[-2], rest[-1]

        @pl.when(pl.program_id(0) == 0)
        def _():
            ht_ref[...] = h_ref[...].T
        o_ref[...] = jnp.dot(ht_ref[...], dp_ref[...], preferred_element_type=F32)

    return pl.pallas_call(
        body, grid=(NP // tn,),
        in_specs=[_bs((s_len, D), lambda j: (0, 0)), _bs((s_len, tn), lambda j: (0, j))] + [_ANY] * len(after),
        out_specs=_bs((D, tn), lambda j: (0, j)), out_shape=jax.ShapeDtypeStruct((D, NP), F32),
        scratch_shapes=[pltpu.VMEM((D, s_len), h.dtype)], name="dw_in", compiler_params=_cparams(1))(h, dproj, *after)


def _loss_call(y, target):
    s_len = y.shape[0]
    tm = min(s_len, 512)

    def body(y_ref, t_ref, dy_ref, l_ref):
        e = y_ref[...] - t_ref[...]
        dy_ref[...] = e * (1.0 / D)

        @pl.when(pl.program_id(0) == 0)
        def _():
            l_ref[...] = jnp.zeros_like(l_ref)
        l_ref[...] += jnp.sum(e * e, axis=0, keepdims=True)

    row = _bs((tm, D), lambda i: (i, 0))
    return pl.pallas_call(
        body, grid=(s_len // tm,), in_specs=[row, row], out_specs=[row, _bs((1, D), lambda i: (0, 0))],
        out_shape=[jax.ShapeDtypeStruct((s_len, D), F32), jax.ShapeDtypeStruct((1, D), F32)],
        name="loss", compiler_params=_cparams(1))(y, target)


def _adamw_small_call(ws, gs, ms, vs, name):
    n = len(ws)

    def body(*refs):
        for t in range(n):
            w_ref, g_ref, m_ref, v_ref = refs[t], refs[n + t], refs[2 * n + t], refs[3 * n + t]
            d_ref, nm_ref, nv_ref = refs[4 * n + 3 * t:4 * n + 3 * t + 3]
            gv = g_ref[...]
            m2 = ADAM_B1 * m_ref[...] + (1.0 - ADAM_B1) * gv
            v2 = ADAM_B2 * v_ref[...] + (1.0 - ADAM_B2) * (gv * gv)
            m_hat = m2 / (1.0 - ADAM_B1 ** ADAM_STEP)
            v_hat = v2 / (1.0 - ADAM_B2 ** ADAM_STEP)
            d_ref[...] = -ADAM_LR * (m_hat / (jnp.sqrt(v_hat) + ADAM_EPS) + ADAM_WD * w_ref[...])
            nm_ref[...] = m2
            nv_ref[...] = v2

    return pl.pallas_call(
        body, out_shape=[jax.ShapeDtypeStruct(w.shape, F32) for w in ws for _ in range(3)], name=name,
        compiler_params=pltpu.CompilerParams(vmem_limit_bytes=VMEM_LIMIT))(*ws, *gs, *ms, *vs)


def _adamw_layer_call(layer, ws, gs, ms, vs, prev, after, name, steps=8):
    n = len(ws)
    after = list(after)
    n_prev = 4 * n if prev is not None else 0

    def body(*refs):
        outs = refs[len(refs) - 4 * n:]
        for t in range(n):
            w_ref, g_ref, m_ref, v_ref = refs[t], refs[n + t], refs[2 * n + t], refs[3 * n + t]
            g_out, d_out, m_out, v_out = outs[4 * t:4 * t + 4]
            gv = g_ref[...]
            m2 = ADAM_B1 * m_ref[0] + (1.0 - ADAM_B1) * gv
            v2 = ADAM_B2 * v_ref[0] + (1.0 - ADAM_B2) * (gv * gv)
            m_hat = m2 / (1.0 - ADAM_B1 ** ADAM_STEP)
            v_hat = v2 / (1.0 - ADAM_B2 ** ADAM_STEP)
            g_out[0] = gv
            d_out[0] = -ADAM_LR * (m_hat / (jnp.sqrt(v_hat) + ADAM_EPS) + ADAM_WD * w_ref[0])
            m_out[0] = m2
            v_out[0] = v2

    def lay(a):
        return _bs((1, a.shape[1] // steps, a.shape[2]), lambda i: (layer, i, 0))

    in_specs = ([lay(a) for a in ws] + [_bs((g.shape[0] // steps, g.shape[1]), lambda i: (i, 0)) for g in gs]
                + [lay(a) for a in ms] + [lay(a) for a in vs] + [_ANY] * (n_prev + len(after)))
    return pl.pallas_call(
        body, grid=(steps,), in_specs=in_specs, out_specs=[lay(ws[t]) for t in range(n) for _ in range(4)],
        out_shape=[jax.ShapeDtypeStruct(ws[t].shape, F32) for t in range(n) for _ in range(4)],
        input_output_aliases={4 * n + q: q for q in range(n_prev)}, name=name, compiler_params=_cparams(1),
    )(*ws, *gs, *ms, *vs, *(prev if prev is not None else []), *after)


def _row_tile(rows):
    for cand in (512, 256, 128, 64, 32, 16, 8):
        if rows % cand == 0 and rows > cand:
            return cand
    return rows


def _pair_sum_call(grads, from_sibling, core, name):
    n = len(grads)

    def body(core_ref, *refs):
        for t in range(n):
            refs[2 * n + t][...] = (refs[t][...].astype(F32) + refs[n + t][...].astype(F32)).astype(MM)

    half = lambda g: (1, g.shape[1] // 2, g.shape[2])
    grid_spec = pltpu.PrefetchScalarGridSpec(
        num_scalar_prefetch=1, grid=(N_CHIPS,),
        in_specs=[pl.BlockSpec(half(g), lambda j, core_ref: (j, core_ref[0], 0)) for g in grads]
        + [pl.BlockSpec(half(g), lambda j, core_ref: (j, 0, 0)) for g in grads],
        out_specs=[pl.BlockSpec(half(g), lambda j, core_ref: (j, 0, 0)) for g in grads])
    return pl.pallas_call(
        body, grid_spec=grid_spec, out_shape=[jax.ShapeDtypeStruct((N_CHIPS,) + half(g)[1:], MM) for g in grads], name=name,
        compiler_params=_cparams(1))(core, *grads, *from_sibling)


def _owner_sum_call(chip_sums, from_chips, chip_core, name):
    n = len(chip_sums)
    steps = 4

    def body(ids_ref, *refs):
        for t in range(n):
            a, b = refs[t], refs[n + t]
            refs[2 * n + t][...] = ((a[0].astype(F32) + b[0].astype(F32)) + b[1].astype(F32)) + b[2].astype(F32)

    tile = lambda p: (p.shape[1] // steps, p.shape[2])
    grid_spec = pltpu.PrefetchScalarGridSpec(
        num_scalar_prefetch=1, grid=(steps,),
        in_specs=[pl.BlockSpec((1,) + tile(p), lambda i, ids_ref: (ids_ref[0], i, 0)) for p in chip_sums]
        + [pl.BlockSpec((3,) + tile(p), lambda i, ids_ref: (0, i, 0)) for p in chip_sums],
        out_specs=[pl.BlockSpec(tile(p), lambda i, ids_ref: (ids_ref[1] * steps + i, 0)) for p in chip_sums])
    return pl.pallas_call(
        body, grid_spec=grid_spec, out_shape=[jax.ShapeDtypeStruct((2 * p.shape[1], p.shape[2]), F32) for p in chip_sums],
        name=name, compiler_params=_cparams(1))(chip_core, *chip_sums, *from_chips)


def _sum8_call(parts):
    n, rows, cols = parts.shape
    tr = _row_tile(rows)

    def body(p_ref, o_ref):
        acc = p_ref[0]
        for k in range(1, n):
            acc = acc + p_ref[k]
        o_ref[...] = acc

    return pl.pallas_call(
        body, grid=(rows // tr,), in_specs=[_bs((n, tr, cols), lambda i: (0, i, 0))], out_specs=_bs((tr, cols), lambda i: (i, 0)),
        out_shape=jax.ShapeDtypeStruct((rows, cols), F32), name="sum_small_grads", compiler_params=_cparams(1))(parts)


_ANY = pl.BlockSpec(memory_space=pl.ANY)


def _half_rows(ref, lead, half, which):
    rows = pl.ds(pl.multiple_of(half * which, half), half)
    return ref.at[rows] if lead is None else ref.at[lead, rows]


_HBM = pl.BlockSpec(memory_space=pltpu.HBM)
_SEM = pl.BlockSpec(memory_space=pltpu.SEMAPHORE)
_ORDERED_EFFECT = pltpu.CompilerParams(has_side_effects=pltpu.SideEffectType.DATAFLOW_SIDE_EFFECTING)


_VMEM = pl.BlockSpec(memory_space=pltpu.VMEM)
_TOKEN = jax.ShapeDtypeStruct((8, LANES), F32)


def _in_hbm(a):
    return pltpu.with_memory_space_constraint(a, pltpu.HBM)


def _tie(small, token):
    return small + token[0:1, 0:1].reshape((1,) * small.ndim)


def _peer(k):
    x, y, c = lax.axis_index("x"), lax.axis_index("y"), lax.axis_index("c")
    bx, by, bc = (k >> 2) & 1, (k >> 1) & 1, k & 1
    return (x ^ bx if bx else x, y ^ by if by else y, c ^ bc if bc else c)


def _place_block_call(blk, index, name):
    rows, cols = blk.shape

    def body(idx_ref, b_ref, o_ref):
        o_ref[0] = b_ref[...]

    grid_spec = pltpu.PrefetchScalarGridSpec(
        num_scalar_prefetch=1, grid=(1,), in_specs=[pl.BlockSpec((rows, cols), lambda i, idx_ref: (0, 0))],
        out_specs=pl.BlockSpec((1, rows, cols), lambda i, idx_ref: (idx_ref[0], 0, 0)))
    return pl.pallas_call(body, grid_spec=grid_spec, out_shape=jax.ShapeDtypeStruct((8, rows, cols), blk.dtype), name=name,
                          compiler_params=_cparams(1))(index, blk)


def _small_gather_start_call(blk, buf, after, name):
    after = list(after)

    def body(*refs):
        b_ref, out_ref = refs[0], refs[2 + len(after)]
        send_sems, recv_sems, token = refs[3 + len(after):]
        x, y, c = lax.axis_index("x"), lax.axis_index("y"), lax.axis_index("c")
        for k in range(1, 8):
            pltpu.make_async_remote_copy(src_ref=b_ref, dst_ref=out_ref.at[4 * x + 2 * y + c], send_sem=send_sems.at[k - 1],
                                         recv_sem=recv_sems.at[k - 1], device_id=_peer(k), device_id_type=MESH_ID).start()
        token[...] = jnp.zeros_like(token)

    dma = pltpu.SemaphoreType.DMA
    return pl.pallas_call(
        body, out_shape=[pltpu.HBM(buf.shape, buf.dtype), dma((7,)), dma((7,)), _TOKEN],
        in_specs=[_HBM, _HBM] + [_ANY] * len(after), out_specs=[_HBM, _SEM, _SEM, _VMEM],
        input_output_aliases={1: 0}, name=name, compiler_params=_ORDERED_EFFECT)(_in_hbm(blk), _in_hbm(buf), *after)


def _small_gather_finish_call(blk, buf, send_sems, recv_sems, after, name):
    after = list(after)

    def body(*refs):
        b_ref, in_ref, send_ref, recv_ref = refs[:4]
        x, y, c = lax.axis_index("x"), lax.axis_index("y"), lax.axis_index("c")
        for k in range(1, 8):
            px, py, pc = _peer(k)
            pltpu.make_async_remote_copy(src_ref=b_ref, dst_ref=in_ref.at[4 * px + 2 * py + pc], send_sem=send_ref.at[k - 1],
                                         recv_sem=recv_ref.at[k - 1], device_id=(px, py, pc), device_id_type=MESH_ID).wait()

    return pl.pallas_call(
        body, out_shape=pltpu.HBM(buf.shape, buf.dtype), in_specs=[_HBM, _HBM, _SEM, _SEM] + [_ANY] * len(after),
        out_specs=_HBM, input_output_aliases={1: 0}, name=name, compiler_params=_ORDERED_EFFECT,
    )(_in_hbm(blk), buf, send_sems, recv_sems, *after)


def _pair_exchange_start_call(grads, name):
    n = len(grads)
    half = [g.shape[1] // 2 for g in grads]

    def body(*refs):
        srcs, outs = refs[:n], refs[n:2 * n]
        send_sems, recv_sems, token = refs[2 * n:]
        x, y, c = lax.axis_index("x"), lax.axis_index("y"), lax.axis_index("c")
        for t in range(n):
            pltpu.make_async_remote_copy(
                src_ref=srcs[t].at[:, pl.ds(pl.multiple_of(half[t] * (1 - c), half[t]), half[t])], dst_ref=outs[t],
                send_sem=send_sems.at[t], recv_sem=recv_sems.at[t], device_id=(x, y, 1 - c), device_id_type=MESH_ID).start()
        token[...] = jnp.zeros_like(token)

    dma = pltpu.SemaphoreType.DMA
    return pl.pallas_call(
        body, out_shape=[pltpu.HBM((g.shape[0], g.shape[1] // 2, g.shape[2]), g.dtype) for g in grads] + [dma((n,)), dma((n,)), _TOKEN],
        in_specs=[_HBM] * n, out_specs=[_HBM] * n + [_SEM, _SEM, _VMEM], name=name, compiler_params=_ORDERED_EFFECT,
    )(*[_in_hbm(g) for g in grads])


def _pair_exchange_finish_call(grads, bufs, send_sems, recv_sems, after, name):
    n = len(grads)
    after = list(after)
    half = [g.shape[1] // 2 for g in grads]

    def body(*refs):
        srcs, ins, send_ref, recv_ref = refs[:n], refs[n:2 * n], refs[2 * n], refs[2 * n + 1]
        x, y, c = lax.axis_index("x"), lax.axis_index("y"), lax.axis_index("c")
        for t in range(n):
            pltpu.make_async_remote_copy(
                src_ref=srcs[t].at[:, pl.ds(pl.multiple_of(half[t] * (1 - c), half[t]), half[t])], dst_ref=ins[t],
                send_sem=send_ref.at[t], recv_sem=recv_ref.at[t], device_id=(x, y, 1 - c), device_id_type=MESH_ID).wait()

    return pl.pallas_call(
        body, out_shape=[pltpu.HBM(b.shape, b.dtype) for b in bufs],
        in_specs=[_HBM] * (2 * n) + [_SEM, _SEM] + [_ANY] * len(after), out_specs=[_HBM] * n,
        input_output_aliases={n + t: t for t in range(n)}, name=name, compiler_params=_ORDERED_EFFECT,
    )(*[_in_hbm(g) for g in grads], *bufs, send_sems, recv_sems, *after)


def _chip_scatter_start_call(chip_sums, name):
    n = len(chip_sums)

    def body(*refs):
        srcs, outs = refs[:n], refs[n:2 * n]
        send_sems, recv_sems, token = refs[2 * n:]
        x, y, c = lax.axis_index("x"), lax.axis_index("y"), lax.axis_index("c")
        chips = [(1 - x, y), (x, 1 - y), (1 - x, 1 - y)]
        for k, (cx, cy) in enumerate(chips):
            for t in range(n):
                pltpu.make_async_remote_copy(
                    src_ref=srcs[t].at[2 * cx + cy], dst_ref=outs[t].at[k], send_sem=send_sems.at[3 * t + k],
                    recv_sem=recv_sems.at[3 * t + k], device_id=(cx, cy, c), device_id_type=MESH_ID).start()
        token[...] = jnp.zeros_like(token)

    dma = pltpu.SemaphoreType.DMA
    return pl.pallas_call(
        body, out_shape=[pltpu.HBM((3,) + p.shape[1:], p.dtype) for p in chip_sums] + [dma((3 * n,)), dma((3 * n,)), _TOKEN],
        in_specs=[_HBM] * n, out_specs=[_HBM] * n + [_SEM, _SEM, _VMEM], name=name, compiler_params=_ORDERED_EFFECT,
    )(*[_in_hbm(p) for p in chip_sums])


def _chip_scatter_finish_call(chip_sums, bufs, send_sems, recv_sems, after, name):
    n = len(chip_sums)
    after = list(after)

    def body(*refs):
        srcs, ins, send_ref, recv_ref = refs[:n], refs[n:2 * n], refs[2 * n], refs[2 * n + 1]
        x, y, c = lax.axis_index("x"), lax.axis_index("y"), lax.axis_index("c")
        chips = [(1 - x, y), (x, 1 - y), (1 - x, 1 - y)]
        for k, (cx, cy) in enumerate(chips):
            for t in range(n):
                pltpu.make_async_remote_copy(
                    src_ref=srcs[t].at[2 * cx + cy], dst_ref=ins[t].at[k], send_sem=send_ref.at[3 * t + k],
                    recv_sem=recv_ref.at[3 * t + k], device_id=(cx, cy, c), device_id_type=MESH_ID).wait()

    return pl.pallas_call(
        body, out_shape=[pltpu.HBM(b.shape, b.dtype) for b in bufs],
        in_specs=[_HBM] * (2 * n) + [_SEM, _SEM] + [_ANY] * len(after), out_specs=[_HBM] * n,
        input_output_aliases={n + t: t for t in range(n)}, name=name, compiler_params=_ORDERED_EFFECT,
    )(*[_in_hbm(p) for p in chip_sums], *bufs, send_sems, recv_sems, *after)


def _place_own_call(mine, chip_core, name):
    n = len(mine)

    def body(ids_ref, *refs):
        for t in range(n):
            refs[n + t][0] = refs[t][...]

    def imap_out(s):
        pad = (0,) * (s.ndim - 1)
        return lambda i, ids_ref: (ids_ref[0], ids_ref[1]) + pad

    grid_spec = pltpu.PrefetchScalarGridSpec(
        num_scalar_prefetch=1, grid=(1,), in_specs=[pl.BlockSpec(s.shape, lambda i, ids_ref, k=s.ndim: (0,) * k) for s in mine],
        out_specs=[pl.BlockSpec((1,) + s.shape, imap_out(s)) for s in mine])
    return pl.pallas_call(
        body, grid_spec=grid_spec,
        out_shape=[jax.ShapeDtypeStruct((N_CHIPS, 2 * s.shape[0]) + s.shape[1:], s.dtype) for s in mine],
        name=name, compiler_params=_cparams(1))(chip_core, *mine)


def _gather_start_call(mine, bufs, after, name):
    n = len(mine)
    half = [s.shape[0] for s in mine]

    def body(*refs):
        srcs, outs = refs[:n], refs[2 * n + 1:3 * n + 1]
        send_sems, recv_sib, recv_ici, token = refs[3 * n + 1:]
        x, y, c = lax.axis_index("x"), lax.axis_index("y"), lax.axis_index("c")
        chips = [(1 - x, y), (x, 1 - y), (1 - x, 1 - y)]
        for t in range(n):
            dst = _half_rows(outs[t], 2 * x + y, half[t], c)
            pltpu.make_async_remote_copy(src_ref=srcs[t], dst_ref=dst, send_sem=send_sems.at[4 * t], recv_sem=recv_sib.at[t],
                                         device_id=(x, y, 1 - c), device_id_type=MESH_ID).start()
            for j, chip in enumerate(chips):
                pltpu.make_async_remote_copy(src_ref=srcs[t], dst_ref=dst, send_sem=send_sems.at[4 * t + 1 + j],
                                             recv_sem=recv_ici.at[3 * t + j], device_id=(*chip, c), device_id_type=MESH_ID).start()
        token[...] = jnp.zeros_like(token)

    dma = pltpu.SemaphoreType.DMA
    return pl.pallas_call(
        body, out_shape=[pltpu.HBM(b.shape, b.dtype) for b in bufs] + [dma((4 * n,)), dma((n,)), dma((3 * n,)), _TOKEN],
        in_specs=[_HBM] * (2 * n) + [_ANY], out_specs=[_HBM] * n + [_SEM] * 3 + [_VMEM],
        input_output_aliases={n + t: t for t in range(n)}, name=name, compiler_params=_ORDERED_EFFECT,
    )(*[_in_hbm(s) for s in mine], *[_in_hbm(b) for b in bufs], after)


def _gather_forward_call(bufs, recv_ici, after, name):
    n = len(bufs)
    half = [b.shape[1] // 2 for b in bufs]

    def body(*refs):
        ins, recv_ici_ref = refs[:n], refs[n]
        outs = refs[n + 2:2 * n + 2]
        send_fwd, recv_fwd, token = refs[2 * n + 2:]
        x, y, c = lax.axis_index("x"), lax.axis_index("y"), lax.axis_index("c")
        chips = [(1 - x, y), (x, 1 - y), (1 - x, 1 - y)]
        for j, (cx, cy) in enumerate(chips):
            for t in range(n):
                landed = _half_rows(ins[t], 2 * cx + cy, half[t], c)
                dst = _half_rows(outs[t], 2 * cx + cy, half[t], c)
                pltpu.make_async_remote_copy(src_ref=landed, dst_ref=landed, send_sem=send_fwd.at[3 * t + j],
                                             recv_sem=recv_ici_ref.at[3 * t + j], device_id=(cx, cy, c),
                                             device_id_type=MESH_ID).wait_recv()
                pltpu.make_async_remote_copy(src_ref=landed, dst_ref=dst, send_sem=send_fwd.at[3 * t + j],
                                             recv_sem=recv_fwd.at[3 * t + j], device_id=(x, y, 1 - c),
                                             device_id_type=MESH_ID).start()
        token[...] = jnp.zeros_like(token)

    dma = pltpu.SemaphoreType.DMA
    return pl.pallas_call(
        body, out_shape=[pltpu.HBM(b.shape, b.dtype) for b in bufs] + [dma((3 * n,)), dma((3 * n,)), _TOKEN],
        in_specs=[_HBM] * n + [_SEM, _ANY], out_specs=[_HBM] * n + [_SEM] * 2 + [_VMEM],
        input_output_aliases={t: t for t in range(n)}, name=name, compiler_params=_ORDERED_EFFECT,
    )(*bufs, recv_ici, after)


def _gather_finish_call(shards, bufs, send_sems, recv_sib, send_fwd, recv_fwd, after, name):
    n = len(bufs)
    half = [b.shape[1] // 2 for b in bufs]

    def body(*refs):
        srcs, ins = refs[:n], refs[n:2 * n]
        send_ref, recv_sib_ref, send_fwd_ref, recv_fwd_ref = refs[2 * n:2 * n + 4]
        x, y, c = lax.axis_index("x"), lax.axis_index("y"), lax.axis_index("c")
        chips = [(1 - x, y), (x, 1 - y), (1 - x, 1 - y)]
        sibling = (x, y, 1 - c)
        for t in range(n):
            for k in range(4):
                pltpu.make_async_remote_copy(src_ref=srcs[t], dst_ref=srcs[t], send_sem=send_ref.at[4 * t + k],
                                             recv_sem=recv_sib_ref.at[t], device_id=sibling, device_id_type=MESH_ID).wait_send()
            from_sibling = _half_rows(ins[t], 2 * x + y, half[t], 1 - c)
            pltpu.make_async_remote_copy(src_ref=from_sibling, dst_ref=from_sibling, send_sem=send_ref.at[4 * t],
                                         recv_sem=recv_sib_ref.at[t], device_id=sibling, device_id_type=MESH_ID).wait_recv()
            for j, (cx, cy) in enumerate(chips):
                sent = _half_rows(ins[t], 2 * cx + cy, half[t], c)
                passed = _half_rows(ins[t], 2 * cx + cy, half[t], 1 - c)
                pltpu.make_async_remote_copy(src_ref=sent, dst_ref=passed, send_sem=send_fwd_ref.at[3 * t + j],
                                             recv_sem=recv_fwd_ref.at[3 * t + j], device_id=sibling, device_id_type=MESH_ID).wait()

    return pl.pallas_call(
        body, out_shape=[pltpu.HBM(b.shape, b.dtype) for b in bufs],
        in_specs=[_HBM] * (2 * n) + [_SEM] * 4 + [_ANY], out_specs=[_HBM] * n,
        input_output_aliases={n + t: t for t in range(n)}, name=name, compiler_params=_ORDERED_EFFECT,
    )(*[_in_hbm(s) for s in shards], *bufs, send_sems, recv_sib, send_fwd, recv_fwd, after)


def _pair_gather_call(bufs, name):
    n = len(bufs)
    half = [b.shape[0] // 2 for b in bufs]

    def body(*refs):
        srcs, outs, send_sems, recv_sems = refs[:n], refs[n:2 * n], refs[2 * n], refs[2 * n + 1]
        x, y, c = lax.axis_index("x"), lax.axis_index("y"), lax.axis_index("c")
        for t in range(n):
            pltpu.make_async_remote_copy(
                src_ref=_half_rows(srcs[t], None, half[t], c), dst_ref=_half_rows(outs[t], None, half[t], c),
                send_sem=send_sems.at[t], recv_sem=recv_sems.at[t], device_id=(x, y, 1 - c), device_id_type=MESH_ID).start()
        for t in range(n):
            pltpu.make_async_remote_copy(
                src_ref=_half_rows(srcs[t], None, half[t], c), dst_ref=_half_rows(outs[t], None, half[t], 1 - c),
                send_sem=send_sems.at[t], recv_sem=recv_sems.at[t], device_id=(x, y, 1 - c), device_id_type=MESH_ID).wait()

    return pl.pallas_call(
        body, out_shape=[jax.ShapeDtypeStruct(b.shape, b.dtype) for b in bufs], in_specs=[_ANY] * n, out_specs=[_ANY] * n,
        input_output_aliases={t: t for t in range(n)},
        scratch_shapes=[pltpu.SemaphoreType.DMA((n,)), pltpu.SemaphoreType.DMA((n,))], name=name)(*bufs)


def _pack_rows(flats, dtype, row_multiple):
    flat = jnp.concatenate([f.reshape(-1).astype(dtype) for f in flats])
    n = flat.shape[0]
    rows = -(-n // PACK_W)
    rows = -(-rows // row_multiple) * row_multiple
    return jnp.pad(flat, (0, rows * PACK_W - n)).reshape(rows, PACK_W)


def _unpack(flat, shapes):
    out, off = [], 0
    for shp in shapes:
        n = math.prod(shp)
        out.append(flat[off:off + n].reshape(shp))
        off += n
    return out


_W_IN_SEGMENTS = ((R_ML, R_END, OFF_ML), (R_SG, R_ML, OFF_SG), (R_CV, R_SGI, OFF_CV), (R_SGI, R_MQ, OFF_SGI), (R_MQ, R_SG, OFF_MQ),
                  (R_CQ, R_CKV, OFF_CQ), (R_CKV, R_KR, OFF_CKV), (R_KR, R_CV, OFF_KR + NOPE))
W_IN_SHARD = R_END // N_CHIPS


def _realign_call(wg):
    tr = 128

    def body(w_ref, o_ref):
        pieces, pos = [], 0
        for r0, r1, a0 in _W_IN_SEGMENTS:
            if a0 > pos:
                pieces.append(jnp.zeros((tr, a0 - pos), o_ref.dtype))
            while r0 < r1:
                j = r0 // W_IN_SHARD
                hi = min(r1, (j + 1) * W_IN_SHARD)
                pieces.append(w_ref[j, :, r0 - j * W_IN_SHARD:hi - j * W_IN_SHARD])
                a0, r0 = a0 + hi - r0, hi
            pos = a0
        pieces.append(jnp.zeros((tr, NP - pos), o_ref.dtype))
        o_ref[...] = jnp.concatenate(pieces, axis=1)

    return pl.pallas_call(
        body, grid=(D // tr,), in_specs=[_bs((N_CHIPS, tr, W_IN_SHARD), lambda i: (0, i, 0))],
        out_specs=_bs((tr, NP), lambda i: (i, 0)), out_shape=jax.ShapeDtypeStruct((D, NP), wg.dtype),
        name="w_in_realign", compiler_params=_cparams(1))(wg)


def _unalign_call(dw, out_dtype):
    tr = 128
    by_ref = sorted(_W_IN_SEGMENTS)

    def body(dw_ref, o_ref):
        for j in range(N_CHIPS):
            lo_j, hi_j = j * W_IN_SHARD, (j + 1) * W_IN_SHARD
            pieces = []
            for r0, r1, a0 in by_ref:
                lo, hi = max(r0, lo_j), min(r1, hi_j)
                if lo < hi:
                    pieces.append(dw_ref[:, a0 + lo - r0:a0 + hi - r0])
            o_ref[j] = jnp.concatenate(pieces, axis=1).astype(o_ref.dtype)

    return pl.pallas_call(
        body, grid=(D // tr,), in_specs=[_bs((tr, NP), lambda i: (i, 0))],
        out_specs=_bs((N_CHIPS, tr, W_IN_SHARD), lambda i: (0, i, 0)),
        out_shape=jax.ShapeDtypeStruct((N_CHIPS, D, W_IN_SHARD), out_dtype), name="w_in_unalign", compiler_params=_cparams(1))(dw)


def _wuq_to_heads(w):
    w3 = w.reshape(QL, H, QKH)
    w3 = jnp.pad(w3, ((0, 0), (0, 0), (0, LANES - QKH)))
    return jnp.transpose(w3, (1, 0, 2))


def _wuq_from_heads(wh):
    return jnp.transpose(wh[:, :, :QKH], (1, 0, 2)).reshape(QL, H * QKH)


def _wukv_to_heads(w):
    w3 = w.reshape(KVL, H, NOPE + VH)
    wkn = jnp.transpose(jnp.pad(w3[:, :, :NOPE], ((0, 0), (0, 0), (0, LANES - NOPE))), (1, 0, 2))
    wv3 = w3[:, :, NOPE:]
    z = jnp.zeros((KVL, VH), w.dtype)
    cols = []
    for h in range(H):
        cols += [wv3[:, h], z] if h % 2 == 0 else [z, wv3[:, h]]
    return wkn, jnp.concatenate(cols, axis=1)


def _wukv_from_heads(wkn, wv):
    kn = jnp.transpose(wkn[:, :, :NOPE], (1, 0, 2))
    vs = jnp.stack([wv[:, LANES * h + VH * (h % 2):LANES * h + VH * (h % 2) + VH] for h in range(H)], axis=1)
    return jnp.concatenate([kn, vs], axis=2).reshape(KVL, H * (NOPE + VH))


def _layer_fwd(x, mem, tabs, p):
    proj, h = _proj_call(x, p["norm_g"], p["w_in"])
    if p.get("late") is not None:
        p = dict(p, **p["late"](proj))
    q, k, v = _mla_prep_call(proj, tabs, p["cq_g"], p["ckv_g"], p["qg"], p["kg"], p["wuq"], p["wkn"], p["wv"])
    ya, attn_o, attn_lse = _attn_call(q, k, v, proj)
    bm = p["bm"]
    if p.get("after_attn") is not None:
        bm = _tie(bm, p["after_attn"](ya))
    yb = _conv_call(proj, p["conv_w"], p["conv_b"])
    yc = _sg_call(proj, p["ln_g"], p["ln_b"], p["ws"], p["bs"])
    mk, mv = _memkv_call(mem, p["mem_g"], p["wm"], p["mkg"])
    yd = _mem_call(proj, mk, mv, p["mqg"])
    out = _merge_call((ya, yb, yc, yd), proj, bm, p["wb"], p["wo"], x)
    return out, dict(p=p, x=x, proj=proj, h=h, q=q, k=k, v=v, attn_o=attn_o, attn_lse=attn_lse, ys=(ya, yb, yc, yd), mk=mk, mv=mv)


def _layer_bwd(dout, mem, tabs, p, sv, start_after=None, on_rest_grads=None, on_grads=None):
    proj = sv["proj"]
    bm = p["bm"] if start_after is None else _tie(p["bm"], start_after)
    dya, dyb, dyc, dyd, dml, dbm, dwb, dwo = _merge_bwd_call(sv["ys"], proj, bm, p["wb"], p["wo"], dout)
    dq, dk, dv, dsg_a = _attn_bwd_call(sv["q"], sv["k"], sv["v"], proj, dya, sv["attn_o"], sv["attn_lse"])
    dlat, dcqg, dckvg, dqg, dkg, dwuq, dwkn, dwv = _mla_prep_bwd_call(
        proj, tabs, p["cq_g"], p["ckv_g"], p["qg"], p["kg"], p["wuq"], p["wkn"], p["wv"], dq, dk, dv)
    dbg, dcg, dxi, dsg_b, dcw, dcb = _conv_bwd_call(proj, p["conv_w"], p["conv_b"], dyb)
    duv, dsg_c, dlg, dlb, dws, dbs = _sg_bwd_call(proj, p["ln_g"], p["ln_b"], p["ws"], p["bs"], dyc)
    dmq, dsg_d, dmk, dmv, dmqg = _mem_bwd_call(proj, sv["mk"], sv["mv"], p["mqg"], dyd)
    dmem_g, dwm, dmkg = _memkv_bwd_call(mem, p["mem_g"], p["wm"], p["mkg"], dmk, dmv)
    grads = dict(cq_norm_g=dcqg, ckv_norm_g=dckvg, mla_q_norm_g=dqg[:, :QKH], mla_k_norm_g=dkg[:, :QKH],
                 conv_w=dcw, conv_b=dcb, sg_ln_g=dlg, sg_ln_b=dlb, w_spatial=dws, b_spatial=dbs,
                 mem_norm_g=dmem_g, mem_q_norm_g=dmqg, mem_k_norm_g=dmkg, b_merge=dbm,
                 wuq_heads=dwuq, wkn_heads=dwkn, wv_heads=dwv, w_mem_kv=dwm, w_branch_chips=dwb, w_out=dwo)
    started = [on_rest_grads(grads)] if on_rest_grads is not None else []
    dproj, off = dml, NB * D
    for piece in (dsg_a, dsg_b, dsg_c, dsg_d, dbg, dcg, dxi, duv, dmq, dlat):
        dproj = lax.dynamic_update_slice(dproj, piece, (0, off))
        off += piece.shape[1]
    grads["w_in_aligned"] = _dw_call(sv["h"], dproj, started)
    tokens = on_grads(grads) if on_grads is not None else ()
    dx, dnorm_g = _dh_call(dproj, p["w_in"], sv["x"], p["norm_g"], dout, tokens)
    grads["norm_g"] = dnorm_g
    return dx, grads


def _chips_to_cols(a):
    return jnp.concatenate([a[j] for j in range(N_CHIPS)], axis=1)


def _cols_to_chips(a):
    cols = a.shape[1] // N_CHIPS
    return jnp.stack([a[:, cols * j:cols * (j + 1)] for j in range(N_CHIPS)])


def _layer_params_first(l, rep, w_in_gathered, conv_w, b_merge):
    pad_g = lambda g: jnp.pad(g, (0, LANES - QKH)).reshape(1, LANES)
    return dict(
        norm_g=rep["norm_g"][l].reshape(1, D), w_in=_realign_call(w_in_gathered),
        cq_g=rep["cq_norm_g"][l].reshape(1, QL), ckv_g=rep["ckv_norm_g"][l].reshape(1, KVL),
        qg=pad_g(rep["mla_q_norm_g"][l]), kg=pad_g(rep["mla_k_norm_g"][l]),
        conv_w=conv_w, conv_b=rep["conv_b"][l].reshape(1, CW),
        ln_g=rep["sg_ln_g"][l].reshape(1, SGW), ln_b=rep["sg_ln_b"][l].reshape(1, SGW),
        ws=rep["w_spatial"][l], bs=rep["b_spatial"][l].reshape(SGG, SGC, 1),
        mem_g=rep["mem_norm_g"][l].reshape(1, D),
        mqg=rep["mem_q_norm_g"][l].reshape(1, MHD), mkg=rep["mem_k_norm_g"][l].reshape(1, MHD), bm=b_merge)


def _layer_params_rest(gathered):
    wkn, wv = _wukv_to_heads(_chips_to_cols(gathered["w_ukv"]))
    return dict(wuq=_wuq_to_heads(_chips_to_cols(gathered["w_uq"])), wkn=wkn, wv=wv,
                wm=gathered["w_mem_kv"].reshape(D, 2 * MH * MHD), wb=gathered["w_branch"], wo=gathered["w_out"].reshape(D, D))


def _layer_params(l, rep, gathered, conv_w, b_merge):
    return dict(_layer_params_first(l, rep, gathered["w_in"], conv_w, b_merge), **_layer_params_rest(gathered))


def _forward_backward(x, mem, pos, target, params, bwd_hooks=None):
    tabs = _rope_tables(pos)
    params = list(params)
    saved = []
    act = x
    for l in range(DEPTH):
        if callable(params[l]):
            params[l] = params[l](saved[-1], act)
        act, sv = _layer_fwd(act, mem, tabs, params[l])
        saved.append(sv)
    dy, sq = _loss_call(act, target)
    grads = [None] * DEPTH
    token = None
    for l in reversed(range(DEPTH)):
        hooks = dict(bwd_hooks[l]) if bwd_hooks else {}
        after_layer = hooks.pop("after_layer", None)
        dy, grads[l] = _layer_bwd(dy, mem, tabs, saved[l]["p"], saved[l], start_after=token, **hooks)
        token = after_layer(dy) if after_layer is not None else None
    return sq, dy, grads


_SHARDED_MM = ("w_in", "w_branch", "w_out", "w_mem_kv", "w_uq", "w_ukv")
_SHARDED_F32 = ("conv_w", "b_merge")
_REPLICATED = ("norm_g", "cq_norm_g", "ckv_norm_g", "mla_q_norm_g", "mla_k_norm_g", "conv_b", "sg_ln_g", "sg_ln_b",
               "w_spatial", "b_spatial", "mem_norm_g", "mem_q_norm_g", "mem_k_norm_g")
_ALL_REDUCED = _REPLICATED + _SHARDED_F32
_WEIGHTS = ("norm_g", "w_in", "cq_norm_g", "ckv_norm_g", "w_uq", "w_ukv", "mla_q_norm_g", "mla_k_norm_g", "conv_w", "conv_b",
            "sg_ln_g", "sg_ln_b", "w_spatial", "b_spatial", "mem_norm_g", "w_mem_kv", "mem_q_norm_g", "mem_k_norm_g",
            "b_merge", "w_branch", "w_out")
_SMALL = tuple(n for n in _WEIGHTS if n not in _SHARDED_MM)


class _SmallGather:
    def __init__(self, blk, after, tag):
        self.blk, self.tag = blk, tag
        x, y, c = lax.axis_index("x"), lax.axis_index("y"), lax.axis_index("c")
        own = _place_block_call(blk, (4 * x + 2 * y + c).astype(jnp.int32).reshape(1), tag + "place_own")
        self.buf, self.send, self.recv, self.token = _small_gather_start_call(blk, own, after, tag + "start")

    def finish(self, after):
        return _small_gather_finish_call(self.blk, self.buf, self.send, self.recv, after, self.tag + "finish")


def _small_sharded_weights(w, got):
    names = _SHARDED_F32
    per_chip = [_unpack(got[2 * j].reshape(-1), [w[n].shape for n in names]) for j in range(N_CHIPS)]
    return {n: jnp.concatenate([per_chip[j][t] for j in range(N_CHIPS)], axis=2) for t, n in enumerate(names)}


class _Gather:
    def __init__(self, w, layer, names, after, tag):
        self.names, self.tag = names, tag
        x, y, c = lax.axis_index("x"), lax.axis_index("y"), lax.axis_index("c")
        chip_core = jnp.stack([2 * x + y, c]).astype(jnp.int32)
        halves = [w[n].shape[1] // 2 for n in names]
        self.srcs = [lax.dynamic_slice_in_dim(w[n][layer], c * h, h, axis=0).astype(MM) for n, h in zip(names, halves)]
        k = len(names)
        out = _gather_start_call(self.srcs, _place_own_call(self.srcs, chip_core, tag + "place_own"), after, tag + "start")
        self.bufs, self.send, self.recv_sib, self.recv_ici, self.token = out[:k], out[k], out[k + 1], out[k + 2], out[k + 3]

    def pass_on(self, after):
        k = len(self.names)
        out = _gather_forward_call(self.bufs, self.recv_ici, after, self.tag + "forward")
        self.bufs, self.send_fwd, self.recv_fwd = out[:k], out[k], out[k + 1]
        return out[k + 2]

    def finish(self, after):
        got = _gather_finish_call(self.srcs, self.bufs, self.send, self.recv_sib, self.send_fwd, self.recv_fwd, after,
                                  self.tag + "finish")
        return dict(zip(self.names, got))


class _ReduceScatter:
    SLABS = dict(
        w_in=lambda g: _unalign_call(g["w_in_aligned"], MM),
        w_branch=lambda g: g["w_branch_chips"].reshape(N_CHIPS, NB * BW, D // N_CHIPS),
        w_out=lambda g: g["w_out"].reshape(N_CHIPS, D // N_CHIPS, D),
        w_mem_kv=lambda g: g["w_mem_kv"].reshape(N_CHIPS, D // N_CHIPS, 2 * MH * MHD),
        w_uq=lambda g: _cols_to_chips(_wuq_from_heads(g["wuq_heads"])),
        w_ukv=lambda g: _cols_to_chips(_wukv_from_heads(g["wkn_heads"], g["wv_heads"])))

    def __init__(self, tag, names):
        self.tag, self.names = tag, names

    def exchange(self, grads):
        self.tensors = [self.SLABS[n](grads) for n in self.names]
        n = len(self.tensors)
        out = _pair_exchange_start_call(self.tensors, self.tag + "exchange_start")
        self.ex_bufs, self.ex_send, self.ex_recv = out[:n], out[n], out[n + 1]
        return out[n + 2]

    def scatter(self, after):
        n = len(self.tensors)
        c = lax.axis_index("c")
        from_sibling = _pair_exchange_finish_call(self.tensors, self.ex_bufs, self.ex_send, self.ex_recv, after,
                                                  self.tag + "exchange_finish")
        self.chip_sums = _pair_sum_call(self.tensors, from_sibling, c.astype(jnp.int32).reshape(1), self.tag + "pair_sum")
        out = _chip_scatter_start_call(self.chip_sums, self.tag + "scatter_start")
        self.bufs, self.send_sems, self.recv_sems, self.token = out[:n], out[n], out[n + 1], out[n + 2]
        return self.token

    def finish(self, after):
        x, y, c = lax.axis_index("x"), lax.axis_index("y"), lax.axis_index("c")
        chip_core = jnp.stack([2 * x + y, c]).astype(jnp.int32)
        from_chips = _chip_scatter_finish_call(self.chip_sums, self.bufs, self.send_sems, self.recv_sems, after,
                                               self.tag + "scatter_finish")
        mine = _owner_sum_call(self.chip_sums, from_chips, chip_core, self.tag + "owner_sum")
        return dict(zip(self.names, _pair_gather_call(mine, self.tag + "pair_gather")))


def _small_sums(shapes, sq, got):
    total = _sum8_call(got).reshape(-1)
    parts = _unpack(total, [shapes[n] for n in _ALL_REDUCED] + [sq.shape])
    out = dict(zip(_ALL_REDUCED, parts))
    sq_total = parts[-1]
    chip = 2 * lax.axis_index("x") + lax.axis_index("y")
    for n in _SHARDED_F32:
        size = out[n].shape[2] // N_CHIPS
        out[n] = lax.dynamic_slice_in_dim(out[n], chip * size, size, axis=2)
    return out, sq_total


def _adamw_small(w, g, m, v):
    pick = lambda t: [t[n] for n in _SMALL]
    out = _adamw_small_call(pick(w), pick(g), pick(m), pick(v), "adamw_small")
    return tuple({n: out[3 * t + k] for t, n in enumerate(_SMALL)} for k in range(3))


def kernel(x, mem, positions, norm_g, w_in, cq_norm_g, ckv_norm_g, w_uq, w_ukv, mla_q_norm_g, mla_k_norm_g, conv_w, conv_b, sg_ln_g, sg_ln_b, w_spatial, b_spatial, mem_norm_g, w_mem_kv, mem_q_norm_g, mem_k_norm_g, b_merge, w_branch, w_out, loss_target, m_norm_g, m_w_in, m_cq_norm_g, m_ckv_norm_g, m_w_uq, m_w_ukv, m_mla_q_norm_g, m_mla_k_norm_g, m_conv_w, m_conv_b, m_sg_ln_g, m_sg_ln_b, m_w_spatial, m_b_spatial, m_mem_norm_g, m_w_mem_kv, m_mem_q_norm_g, m_mem_k_norm_g, m_b_merge, m_w_branch, m_w_out, v_norm_g, v_w_in, v_cq_norm_g, v_ckv_norm_g, v_w_uq, v_w_ukv, v_mla_q_norm_g, v_mla_k_norm_g, v_conv_w, v_conv_b, v_sg_ln_g, v_sg_ln_b, v_w_spatial, v_b_spatial, v_mem_norm_g, v_w_mem_kv, v_mem_q_norm_g, v_mem_k_norm_g, v_b_merge, v_w_branch, v_w_out):
    w = dict(norm_g=norm_g, w_in=w_in, cq_norm_g=cq_norm_g, ckv_norm_g=ckv_norm_g, w_uq=w_uq, w_ukv=w_ukv,
             mla_q_norm_g=mla_q_norm_g, mla_k_norm_g=mla_k_norm_g, conv_w=conv_w, conv_b=conv_b, sg_ln_g=sg_ln_g,
             sg_ln_b=sg_ln_b, w_spatial=w_spatial, b_spatial=b_spatial, mem_norm_g=mem_norm_g, w_mem_kv=w_mem_kv,
             mem_q_norm_g=mem_q_norm_g, mem_k_norm_g=mem_k_norm_g, b_merge=b_merge, w_branch=w_branch, w_out=w_out)
    m = dict(norm_g=m_norm_g, w_in=m_w_in, cq_norm_g=m_cq_norm_g, ckv_norm_g=m_ckv_norm_g, w_uq=m_w_uq, w_ukv=m_w_ukv,
             mla_q_norm_g=m_mla_q_norm_g, mla_k_norm_g=m_mla_k_norm_g, conv_w=m_conv_w, conv_b=m_conv_b, sg_ln_g=m_sg_ln_g,
             sg_ln_b=m_sg_ln_b, w_spatial=m_w_spatial, b_spatial=m_b_spatial, mem_norm_g=m_mem_norm_g, w_mem_kv=m_w_mem_kv,
             mem_q_norm_g=m_mem_q_norm_g, mem_k_norm_g=m_mem_k_norm_g, b_merge=m_b_merge, w_branch=m_w_branch, w_out=m_w_out)
    v = dict(norm_g=v_norm_g, w_in=v_w_in, cq_norm_g=v_cq_norm_g, ckv_norm_g=v_ckv_norm_g, w_uq=v_w_uq, w_ukv=v_w_ukv,
             mla_q_norm_g=v_mla_q_norm_g, mla_k_norm_g=v_mla_k_norm_g, conv_w=v_conv_w, conv_b=v_conv_b, sg_ln_g=v_sg_ln_g,
             sg_ln_b=v_sg_ln_b, w_spatial=v_w_spatial, b_spatial=v_b_spatial, mem_norm_g=v_mem_norm_g, w_mem_kv=v_w_mem_kv,
             mem_q_norm_g=v_mem_q_norm_g, mem_k_norm_g=v_mem_k_norm_g, b_merge=v_b_merge, w_branch=v_w_branch, w_out=v_w_out)

    chip_core = jnp.stack([2 * lax.axis_index("x") + lax.axis_index("y"), lax.axis_index("c")]).astype(jnp.int32)

    first = _Gather(w, 0, ("w_in",), chip_core, "gather_l0_w_in_")
    rest = _Gather(w, 0, _SHARDED_MM[1:], first.token, "gather_l0_rest_")
    small_on_its_way = _SmallGather(_pack_rows([w[n] for n in _SHARDED_F32], F32, 8), [rest.token], "gather_small_weights_")
    later = _Gather(w, 1, _SHARDED_MM, small_on_its_way.token, "gather_l1_")
    w_in0 = first.finish(first.pass_on(later.token))["w_in"]
    small = {}

    def rest_of_layer0(proj0):
        landed = _layer_params_rest(rest.finish(rest.pass_on(proj0)))
        small.update(_small_sharded_weights(w, small_on_its_way.finish([landed["wo"]])))
        return dict(landed, conv_w=small["conv_w"][0], bm=small["b_merge"][0])

    def layer1_params(saved0, act0):
        return _layer_params(1, w, later.finish(act0), small["conv_w"][1], small["b_merge"][1])

    params0 = _layer_params_first(0, w, w_in0, None, None)
    params = [dict(params0, late=rest_of_layer0, after_attn=later.pass_on), layer1_params]
    others = _SHARDED_MM[1:]
    rs1 = _ReduceScatter("rs_l1_", _SHARDED_MM)
    rs0_rest, rs0_w_in = _ReduceScatter("rs_l0_rest_", others), _ReduceScatter("rs_l0_w_in_", ("w_in",))

    def layer0_grads_done(grads):
        return [rs0_rest.scatter([grads["w_in_aligned"]]), rs0_w_in.exchange(grads)]

    hooks = [dict(on_rest_grads=rs0_rest.exchange, on_grads=layer0_grads_done),
             dict(on_grads=lambda grads: [rs1.exchange(grads)], after_layer=lambda dy: rs1.scatter([dy]))]
    sq, grad_x, layer_grads = _forward_backward(x[0], mem[0], positions[0], loss_target[0], params, hooks)

    layered = [layer_grads[l][n] for n in _ALL_REDUCED for l in range(DEPTH)]
    small_grads = _SmallGather(_pack_rows(layered + [sq], F32, 64), [grad_x], "gather_small_grads_")
    scattering = rs0_w_in.scatter([grad_x, small_grads.token])
    shard_grads = {1: rs1.finish([scattering]), 0: rs0_rest.finish([scattering])}
    as3d = lambda a: a.reshape(DEPTH, -1, a.shape[-1])
    as2d = lambda a: a.reshape(-1, a.shape[-1])
    big = lambda t: [as3d(t[n]) for n in others]
    turned = lambda t: [jnp.swapaxes(t["w_in"], 1, 2)]
    assert W_IN_SHARD % (8 * 7) == 0

    def update_w_in(l, grad, prev):
        return _adamw_layer_call(l, turned(w), [grad.T], turned(m), turned(v), prev, [], "adamw_w_in_l%d" % l, steps=7)

    def update_others(l, prev):
        return _adamw_layer_call(l, big(w), [as2d(shard_grads[l][n]) for n in others], big(m), big(v), prev, [], "adamw_l%d" % l)

    upd = update_others(0, update_others(1, None))
    full_shapes = {n: w[n].shape for n in _REPLICATED}
    full_shapes.update(conv_w=(DEPTH, 3, CW), b_merge=(DEPTH, NB, D))
    g, sq_total = _small_sums(full_shapes, sq, small_grads.finish([upd[0]]))
    loss = 0.5 / D * jnp.sum(sq_total)
    delta, new_m, new_v = _adamw_small(w, g, m, v)
    upd_in1 = update_w_in(1, shard_grads[1]["w_in"], None)
    w_in_grad0 = rs0_w_in.finish([grad_x, upd_in1[0], upd[0], delta["norm_g"]])["w_in"]
    upd_in = update_w_in(0, w_in_grad0, upd_in1)
    g["w_in"], delta["w_in"], new_m["w_in"], new_v["w_in"] = [jnp.swapaxes(a, 1, 2) for a in upd_in]
    for t, n in enumerate(others):
        g[n], delta[n], new_m[n], new_v[n] = [a.reshape(w[n].shape) for a in upd[4 * t:4 * t + 4]]
    return (loss, grad_x[None], *[g[n] for n in _WEIGHTS], *[delta[n] for n in _WEIGHTS],
            *[new_m[n] for n in _WEIGHTS], *[new_v[n] for n in _WEIGHTS])
```

```python
import functools
import math

import jax
import jax.numpy as jnp
from jax import lax
from jax.experimental import pallas as pl
from jax.experimental.pallas import tpu as pltpu

F32 = jnp.float32
MM = jnp.bfloat16

D = 1024
DEPTH = 2
EPS = 1e-6
H = 8
NOPE = 64
ROPE = 32
QKH = 96
VH = 64
QL = 256
KVL = 128
ROPE_THETA = 10000.0
CW = 512
SGW = 512
SGG = 4
SGC = 128
MH = 4
MHD = 128
NB = 4
BW = 512
NEG_INF = -1e30
LANES = 128
N_CHIPS = 4

R_CQ, R_CKV, R_KR, R_CV, R_SGI, R_MQ, R_SG, R_ML, R_END = 0, 256, 384, 416, 1952, 2976, 3488, 5536, 9632
OFF_ML, OFF_SG, OFF_CV, OFF_SGI, OFF_MQ, OFF_CQ, OFF_CKV, OFF_KR, NP = 0, 4096, 6144, 7680, 8704, 9216, 9472, 9600, 9728

ADAM_LR = 0.001
ADAM_B1 = 0.9
ADAM_B2 = 0.999
ADAM_EPS = 1e-08
ADAM_WD = 0.01
ADAM_STEP = 10

VMEM_LIMIT = 56 * 1024 * 1024
PACK_W = 512
MESH_ID = pl.DeviceIdType.MESH


def _cparams(n_axes):
    return pltpu.CompilerParams(dimension_semantics=("arbitrary",) * n_axes, vmem_limit_bytes=VMEM_LIMIT)


def _bs(shape, imap):
    return pl.BlockSpec(shape, imap)


@jax.custom_vjp
def _mm_plain(a, b):
    return jnp.dot(a.astype(MM), b.astype(MM), preferred_element_type=F32)


def _mm_plain_fwd(a, b):
    return _mm_plain(a, b), (a, b)


def _mm_plain_bwd(res, g):
    a, b = res
    gm = g.astype(MM)
    da = lax.dot_general(gm, b.astype(MM), (((1,), (1,)), ((), ())), preferred_element_type=F32)
    db = lax.dot_general(a.astype(MM), gm, (((0,), (0,)), ((), ())), preferred_element_type=F32)
    return da.astype(a.dtype), db.astype(b.dtype)


_mm_plain.defvjp(_mm_plain_fwd, _mm_plain_bwd)


@jax.custom_vjp
def _mm_slot(a, w, slot):
    return jnp.dot(a.astype(MM), w.astype(MM), preferred_element_type=F32)


def _mm_slot_fwd(a, w, slot):
    return _mm_slot(a, w, slot), (a, w)


def _mm_slot_bwd(res, g):
    a, w = res
    gm = g.astype(MM)
    da = lax.dot_general(gm, w.astype(MM), (((1,), (1,)), ((), ())), preferred_element_type=F32)
    dw = lax.dot_general(a.astype(MM), gm, (((0,), (0,)), ((), ())), preferred_element_type=F32)
    return da.astype(a.dtype), jnp.zeros_like(w), dw


_mm_slot.defvjp(_mm_slot_fwd, _mm_slot_bwd)


def _mm(a, b):
    if isinstance(b, tuple):
        return _mm_slot(a, b[0], b[1])
    return _mm_plain(a, b)


def _with_slot(w):
    return (w, jnp.zeros(w.shape, F32))


@jax.custom_vjp
def _mm_nt(a, b):
    return lax.dot_general(a.astype(MM), b.astype(MM), (((1,), (1,)), ((), ())), preferred_element_type=F32)


def _mm_nt_fwd(a, b):
    return _mm_nt(a, b), (a, b)


def _mm_nt_bwd(res, g):
    a, b = res
    gm = g.astype(MM)
    da = jnp.dot(gm, b.astype(MM), preferred_element_type=F32)
    db = lax.dot_general(gm, a.astype(MM), (((0,), (0,)), ((), ())), preferred_element_type=F32)
    return da.astype(a.dtype), db.astype(b.dtype)


_mm_nt.defvjp(_mm_nt_fwd, _mm_nt_bwd)


@functools.partial(jax.custom_vjp, nondiff_argnums=(1,))
def _lane_roll(x, shift):
    return pltpu.roll(x, shift, 1)


def _lane_roll_fwd(x, shift):
    return pltpu.roll(x, shift, 1), None


def _lane_roll_bwd(shift, _, g):
    return (pltpu.roll(g, (LANES - shift) % LANES, 1),)


_lane_roll.defvjp(_lane_roll_fwd, _lane_roll_bwd)


def _rms_n(x, g, n):
    ms = jnp.sum(x * x, axis=-1, keepdims=True) * (1.0 / n)
    return x * lax.rsqrt(ms + EPS) * g


def _softmax(s):
    m = jnp.max(s, axis=-1, keepdims=True)
    e = jnp.exp(s - m)
    return e / jnp.sum(e, axis=-1, keepdims=True)


def _rope(t, cos_t, sin_a, sin_b):
    return t * cos_t + _lane_roll(t, LANES - 16) * sin_a + _lane_roll(t, 16) * sin_b


def _mla_prep_fn(cq, ckv, kr, cos_t, sin_a, sin_b, cq_g, ckv_g, qg, kg, wuq, wkn, wv):
    cqn = _rms_n(cq, cq_g, QL)
    ckvn = _rms_n(ckv, ckv_g, KVL)
    lane = lax.broadcasted_iota(jnp.int32, kr.shape, 1)
    krm = jnp.where((lane >= NOPE) & (lane < QKH), kr, 0.0)
    qs, ks = [], []
    for h in range(H):
        qh = _rms_n(_mm(cqn, wuq[h]), qg, QKH)
        qs.append(_rope(qh, cos_t, sin_a, sin_b) * (QKH ** -0.5))
        kh = _rms_n(_mm(ckvn, wkn[h]) + krm, kg, QKH)
        ks.append(_rope(kh, cos_t, sin_a, sin_b))
    return jnp.concatenate(qs, axis=-1), jnp.concatenate(ks, axis=-1), _mm(ckvn, wv)


def _dot_nt(a, b):
    return lax.dot_general(a.astype(MM), b.astype(MM), (((1,), (1,)), ((), ())), preferred_element_type=F32)


def _dot_tn(a, b):
    return lax.dot_general(a.astype(MM), b.astype(MM), (((0,), (0,)), ((), ())), preferred_element_type=F32)


def _causal_scores(qe, ke):
    tq, kl = qe.shape[0], ke.shape[0]
    s = _dot_nt(qe, ke)
    rows = lax.broadcasted_iota(jnp.int32, (tq, tq), 0)
    cols = lax.broadcasted_iota(jnp.int32, (tq, tq), 1)
    own = jnp.where(cols <= rows, s[:, kl - tq:], NEG_INF)
    return own if kl == tq else jnp.concatenate([s[:, :kl - tq], own], axis=1)


def _head_lanes(e, shape):
    lane = lax.broadcasted_iota(jnp.int32, shape, len(shape) - 1)
    return (lane >= VH * e) & (lane < VH * (e + 1))


def _attn_pair_fwd(q2, k2, v2):
    tq = q2.shape[0]
    o = jnp.zeros((tq, LANES), F32)
    lse = jnp.zeros((tq, LANES), F32)
    for e in range(2):
        sl = slice(LANES * e, LANES * (e + 1))
        s = _causal_scores(q2[:, sl], k2[:, sl])
        m = jnp.max(s, axis=-1, keepdims=True)
        ex = jnp.exp(s - m)
        l = jnp.sum(ex, axis=-1, keepdims=True)
        ve = jnp.where(_head_lanes(e, v2[:, sl].shape), v2[:, sl], 0.0)
        o = o + jnp.dot(ex.astype(MM), ve.astype(MM), preferred_element_type=F32) * (1.0 / l)
        lse = jnp.where(_head_lanes(e, lse.shape), m + jnp.log(l), lse)
    return o, lse


def _attn_pair_bwd(q2, k2, v2, sg, dys, o, lse):
    sig = jax.nn.sigmoid(sg)
    do = dys * (sg * sig)
    dsg = dys * o * (sig * (1.0 + sg * (1.0 - sig)))
    dqs, dks, dvs = [], [], []
    for e in range(2):
        sl = slice(LANES * e, LANES * (e + 1))
        qe, ke = q2[:, sl], k2[:, sl]
        hm = _head_lanes(e, o.shape)
        lse_e = jnp.max(jnp.where(hm, lse, NEG_INF), axis=-1, keepdims=True)
        do_e = jnp.where(hm, do, 0.0)
        delta = jnp.sum(do_e * o, axis=-1, keepdims=True)
        p = jnp.exp(_causal_scores(qe, ke) - lse_e)
        ve = jnp.where(_head_lanes(e, v2[:, sl].shape), v2[:, sl], 0.0)
        dvs.append(_dot_tn(p, do_e))
        ds = p * (_dot_nt(do_e, ve) - delta)
        dqs.append(jnp.dot(ds.astype(MM), ke.astype(MM), preferred_element_type=F32))
        dks.append(_dot_tn(ds, qe))
    return jnp.concatenate(dqs, axis=-1), jnp.concatenate(dks, axis=-1), jnp.concatenate(dvs, axis=-1), dsg


def _sg_fn(u, v, sgc, ln_g, ln_b, ws, bs):
    mu = jnp.mean(v, axis=-1, keepdims=True)
    xc = v - mu
    vn = xc * lax.rsqrt(jnp.mean(xc * xc, axis=-1, keepdims=True) + EPS) * ln_g + ln_b
    r = lax.broadcasted_iota(jnp.int32, (SGC, SGC), 0)
    c = lax.broadcasted_iota(jnp.int32, (SGC, SGC), 1)
    wt = [jnp.where(r >= c, w, 0.0) for w in ws]
    row_blocks = []
    for ch in range(u.shape[0] // SGC):
        col_blocks = []
        for g in range(SGG):
            blk = vn[SGC * ch:SGC * (ch + 1), LANES * g:LANES * (g + 1)]
            col_blocks.append(_mm(wt[g], blk) + bs[g])
        row_blocks.append(jnp.concatenate(col_blocks, axis=-1))
    mixed = jnp.concatenate(row_blocks, axis=0)
    return (u * mixed) * jax.nn.silu(sgc)


def _memkv_fn(mem, mem_g, wm, kg):
    kv = _mm(_rms_n(mem, mem_g, D), wm)
    ks = [_rms_n(kv[:, MHD * h:MHD * (h + 1)], kg, MHD) for h in range(MH)]
    return jnp.concatenate(ks, axis=-1), kv[:, MH * MHD:]


def _mem_fn(mq, sgd, k, v, qg):
    outs = []
    for h in range(MH):
        sl = slice(MHD * h, MHD * (h + 1))
        qh = _rms_n(mq[:, sl], qg, MHD)
        p = _softmax(_mm_nt(qh, k[:, sl]) * (MHD ** -0.5))
        outs.append(_mm(p, v[:, sl]))
    return jnp.concatenate(outs, axis=-1) * jax.nn.silu(sgd)


def _merge_fn(ys, logits, bm, wb, wo):
    merged = None
    for n in range(NB):
        z = jnp.concatenate([_mm(ys[n], wb[j][n]) for j in range(N_CHIPS)], axis=-1)
        gate = jax.nn.sigmoid(logits[:, D * n:D * (n + 1)] + bm[n])
        merged = gate * z if merged is None else merged + gate * z
    return _mm(merged, wo)


def _proj_call(x, g, w):
    s_len = x.shape[0]
    tm, tn = s_len, 512

    def body(x_ref, g_ref, w_ref, p_ref, h_ref):
        @pl.when(pl.program_id(1) == 0)
        def _():
            h_ref[...] = _rms_n(x_ref[...], g_ref[...], D).astype(h_ref.dtype)
        p_ref[...] = jnp.dot(h_ref[...], w_ref[...], preferred_element_type=F32)

    return pl.pallas_call(
        body, grid=(s_len // tm, NP // tn),
        in_specs=[_bs((tm, D), lambda i, j: (i, 0)), _bs((1, D), lambda i, j: (0, 0)), _bs((D, tn), lambda i, j: (0, j))],
        out_specs=[_bs((tm, tn), lambda i, j: (i, j)), _bs((tm, D), lambda i, j: (i, 0))],
        out_shape=[jax.ShapeDtypeStruct((s_len, NP), F32), jax.ShapeDtypeStruct((s_len, D), MM)],
        name="proj", compiler_params=_cparams(2))(x, g, w)


def _rope_tables(pos):
    half = ROPE // 2
    inv_freq = ROPE_THETA ** (-jnp.arange(half, dtype=F32) / half)
    ang = pos.astype(F32)[:, None] * inv_freq
    cos, sin = jnp.cos(ang), jnp.sin(ang)
    s_len = pos.shape[0]
    z = lambda n: jnp.zeros((s_len, n), F32)
    cos_t = jnp.concatenate([jnp.ones((s_len, NOPE), F32), cos, cos, z(LANES - QKH)], axis=1)
    sin_a = jnp.concatenate([z(NOPE), -sin, z(LANES - NOPE - half)], axis=1)
    sin_b = jnp.concatenate([z(NOPE + half), sin, z(LANES - QKH)], axis=1)
    return cos_t, sin_a, sin_b


def _mla_prep_specs(tm):
    row = lambda w, off: _bs((tm, w), lambda i: (i, off // w))
    full2 = lambda a, b: _bs((a, b), lambda i: (0, 0))
    full3 = lambda a, b, c: _bs((a, b, c), lambda i: (0, 0, 0))
    tab = _bs((tm, LANES), lambda i: (i, 0))
    return [row(QL, OFF_CQ), row(KVL, OFF_CKV), row(LANES, OFF_KR), tab, tab, tab,
            full2(1, QL), full2(1, KVL), full2(1, LANES), full2(1, LANES),
            full3(H, QL, LANES), full3(H, KVL, LANES), full2(KVL, H * LANES)]


def _mla_prep_args(body_refs, wrap=lambda w: w):
    (cq, ckv, kr, ct, sa, sb, cqg, ckvg, qg, kg, wuq, wkn, wv) = body_refs
    return (cq[...], ckv[...], kr[...], ct[...], sa[...], sb[...], cqg[...], ckvg[...], qg[...], kg[...],
            [wrap(wuq[h]) for h in range(H)], [wrap(wkn[h]) for h in range(H)], wrap(wv[...]))


def _mla_prep_call(proj, tabs, cq_g, ckv_g, qg, kg, wuq, wkn, wv):
    s_len = proj.shape[0]
    tm = min(s_len, 256)

    def body(*refs):
        q_ref, k_ref, v_ref = refs[13:]
        q, k, v = _mla_prep_fn(*_mla_prep_args(refs[:13]))
        q_ref[...] = q.astype(q_ref.dtype)
        k_ref[...] = k.astype(k_ref.dtype)
        v_ref[...] = v.astype(v_ref.dtype)

    out = _bs((tm, H * LANES), lambda i: (i, 0))
    return pl.pallas_call(
        body, grid=(s_len // tm,), in_specs=_mla_prep_specs(tm), out_specs=[out, out, out],
        out_shape=[jax.ShapeDtypeStruct((s_len, H * LANES), MM)] * 3,
        name="mla_prep", compiler_params=_cparams(1))(proj, proj, proj, *tabs, cq_g, ckv_g, qg, kg, wuq, wkn, wv)


def _mla_prep_bwd_call(proj, tabs, cq_g, ckv_g, qg, kg, wuq, wkn, wv, dq, dk, dv):
    s_len = proj.shape[0]
    tm = min(s_len, 256)

    def body(*refs):
        dq_ref, dk_ref, dv_ref = refs[13:16]
        dlat_ref, dcqg_ref, dckvg_ref, dqg_ref, dkg_ref, dwuq_ref, dwkn_ref, dwv_ref = refs[16:]
        _, vjp = jax.vjp(_mla_prep_fn, *_mla_prep_args(refs[:13], _with_slot))
        (dcq, dckv, dkr, _, _, _, dcqg, dckvg, dqg, dkg, dwuq, dwkn, dwv) = vjp((dq_ref[...], dk_ref[...], dv_ref[...]))
        dwuq, dwkn, dwv = [d[1] for d in dwuq], [d[1] for d in dwkn], dwv[1]
        dlat_ref[...] = jnp.concatenate([dcq, dckv, dkr], axis=-1).astype(dlat_ref.dtype)

        @pl.when(pl.program_id(0) == 0)
        def _():
            for r in (dcqg_ref, dckvg_ref, dqg_ref, dkg_ref, dwuq_ref, dwkn_ref, dwv_ref):
                r[...] = jnp.zeros_like(r)
        dcqg_ref[...] += dcqg
        dckvg_ref[...] += dckvg
        dqg_ref[...] += dqg
        dkg_ref[...] += dkg
        for h in range(H):
            dwuq_ref[h] += dwuq[h]
            dwkn_ref[h] += dwkn[h]
        dwv_ref[...] += dwv

    big = _bs((tm, H * LANES), lambda i: (i, 0))
    row = lambda w: _bs((tm, w), lambda i: (i, 0))
    full2 = lambda a, b: _bs((a, b), lambda i: (0, 0))
    full3 = lambda a, b, c: _bs((a, b, c), lambda i: (0, 0, 0))
    sd = jax.ShapeDtypeStruct
    return pl.pallas_call(
        body, grid=(s_len // tm,), in_specs=_mla_prep_specs(tm) + [big, big, big],
        out_specs=[row(QL + KVL + LANES), full2(1, QL), full2(1, KVL), full2(1, LANES), full2(1, LANES),
                   full3(H, QL, LANES), full3(H, KVL, LANES), full2(KVL, H * LANES)],
        out_shape=[sd((s_len, QL + KVL + LANES), MM), sd((1, QL), F32), sd((1, KVL), F32),
                   sd((1, LANES), F32), sd((1, LANES), F32), sd((H, QL, LANES), F32), sd((H, KVL, LANES), F32),
                   sd((KVL, H * LANES), F32)],
        name="mla_prep_bwd", compiler_params=_cparams(1))(proj, proj, proj, *tabs, cq_g, ckv_g, qg, kg, wuq, wkn, wv, dq, dk, dv)


def _attn_specs(s_len, tq):
    pair = 2 * LANES
    return [_bs((tq, pair), lambda p, i: (i, p)), _bs((s_len, pair), lambda p, i: (0, p)), _bs((s_len, pair), lambda p, i: (0, p)),
            _bs((tq, LANES), lambda p, i: (i, OFF_SG // LANES + p))]


def _attn_call(q, k, v, proj):
    s_len = q.shape[0]
    tq = min(s_len, 256)

    def body(q_ref, k_ref, v_ref, sg_ref, y_ref, o_ref, lse_ref):
        for n in range(s_len // tq):
            @pl.when(pl.program_id(1) == n)
            def _():
                kl = (n + 1) * tq
                o, lse = _attn_pair_fwd(q_ref[...], k_ref[:kl, :], v_ref[:kl, :])
                y_ref[...] = (o * jax.nn.silu(sg_ref[...])).astype(y_ref.dtype)
                o_ref[...] = o
                lse_ref[...] = lse

    tile = _bs((tq, LANES), lambda p, i: (i, p))
    sd = jax.ShapeDtypeStruct
    return pl.pallas_call(
        body, grid=(H // 2, s_len // tq), in_specs=_attn_specs(s_len, tq), out_specs=[tile, tile, tile],
        out_shape=[sd((s_len, BW), MM), sd((s_len, BW), F32), sd((s_len, BW), F32)],
        name="attn", compiler_params=_cparams(2))(q, k, v, proj)


def _attn_bwd_call(q, k, v, proj, dys, o, lse):
    s_len = q.shape[0]
    tq = min(s_len, 256)
    pair = 2 * LANES

    def body(q_ref, k_ref, v_ref, sg_ref, dy_ref, o_ref, lse_ref, dq_ref, dk_ref, dv_ref, dsg_ref):
        i = pl.program_id(1)

        @pl.when(i == 0)
        def _():
            dk_ref[...] = jnp.zeros_like(dk_ref)
            dv_ref[...] = jnp.zeros_like(dv_ref)

        for n in range(s_len // tq):
            @pl.when(i == n)
            def _():
                kl = (n + 1) * tq
                dq, dk, dv, dsg = _attn_pair_bwd(q_ref[...], k_ref[:kl, :], v_ref[:kl, :], sg_ref[...], dy_ref[...],
                                                 o_ref[...], lse_ref[...])
                dq_ref[...] = dq
                dsg_ref[...] = dsg.astype(dsg_ref.dtype)
                dk_ref[:kl, :] += dk
                dv_ref[:kl, :] += dv

    sd = jax.ShapeDtypeStruct
    tile = _bs((tq, LANES), lambda p, i: (i, p))
    return pl.pallas_call(
        body, grid=(H // 2, s_len // tq),
        in_specs=_attn_specs(s_len, tq) + [tile, tile, tile],
        out_specs=[_bs((tq, pair), lambda p, i: (i, p)), _bs((s_len, pair), lambda p, i: (0, p)),
                   _bs((s_len, pair), lambda p, i: (0, p)), tile],
        out_shape=[sd((s_len, H * LANES), F32), sd((s_len, H * LANES), F32), sd((s_len, H * LANES), F32), sd((s_len, BW), MM)],
        name="attn_bwd", compiler_params=_cparams(2))(q, k, v, proj, dys, o, lse)


def _shift_down(a, n):
    r = lax.broadcasted_iota(jnp.int32, a.shape, 0)
    return jnp.where(r >= n, pltpu.roll(a, n, 0), 0.0)


def _shift_up(a, n):
    s_len = a.shape[0]
    r = lax.broadcasted_iota(jnp.int32, a.shape, 0)
    return jnp.where(r < s_len - n, pltpu.roll(a, s_len - n, 0), 0.0)


def _conv_specs(s_len):
    col = lambda off: _bs((s_len, LANES), lambda j: (0, off // LANES + j))
    return [col(OFF_CV), col(OFF_CV + CW), col(OFF_CV + 2 * CW), col(OFF_SG + BW),
            _bs((3, LANES), lambda j: (0, j)), _bs((1, LANES), lambda j: (0, j))]


def _conv_call(proj, cw, cb):
    s_len = proj.shape[0]

    def body(bg_ref, cg_ref, xi_ref, sg_ref, w_ref, b_ref, y_ref):
        z = cg_ref[...] * xi_ref[...]
        y = b_ref[...] + w_ref[0:1, :] * _shift_down(z, 2)
        y = y + w_ref[1:2, :] * _shift_down(z, 1)
        y = y + w_ref[2:3, :] * z
        y_ref[...] = ((bg_ref[...] * y) * jax.nn.silu(sg_ref[...])).astype(y_ref.dtype)

    return pl.pallas_call(
        body, grid=(CW // LANES,), in_specs=_conv_specs(s_len), out_specs=_bs((s_len, LANES), lambda j: (0, j)),
        out_shape=jax.ShapeDtypeStruct((s_len, CW), MM), name="conv", compiler_params=_cparams(1))(proj, proj, proj, proj, cw, cb)


def _conv_bwd_call(proj, cw, cb, dys):
    s_len = proj.shape[0]

    def body(bg_ref, cg_ref, xi_ref, sg_ref, w_ref, b_ref, dys_ref, dbg_ref, dcg_ref, dxi_ref, dsg_ref, dw_ref, db_ref):
        bg, cg, xi, sg = bg_ref[...], cg_ref[...], xi_ref[...], sg_ref[...]
        w0, w1, w2 = w_ref[0:1, :], w_ref[1:2, :], w_ref[2:3, :]
        z = cg * xi
        z1, z2 = _shift_down(z, 1), _shift_down(z, 2)
        y = b_ref[...] + w0 * z2
        y = y + w1 * z1
        y = y + w2 * z
        yb = bg * y
        sig = jax.nn.sigmoid(sg)
        silu = sg * sig
        dys_v = dys_ref[...]
        dsg_ref[...] = (dys_v * yb * (sig * (1.0 + sg * (1.0 - sig)))).astype(dsg_ref.dtype)
        dyb = dys_v * silu
        dbg_ref[...] = (dyb * y).astype(dbg_ref.dtype)
        dy = dyb * bg
        db_ref[...] = jnp.sum(dy, axis=0, keepdims=True)
        dw_ref[0:1, :] = jnp.sum(dy * z2, axis=0, keepdims=True)
        dw_ref[1:2, :] = jnp.sum(dy * z1, axis=0, keepdims=True)
        dw_ref[2:3, :] = jnp.sum(dy * z, axis=0, keepdims=True)
        dz = w2 * dy + w1 * _shift_up(dy, 1) + w0 * _shift_up(dy, 2)
        dcg_ref[...] = (dz * xi).astype(dcg_ref.dtype)
        dxi_ref[...] = (dz * cg).astype(dxi_ref.dtype)

    col = _bs((s_len, LANES), lambda j: (0, j))
    sd = jax.ShapeDtypeStruct
    return pl.pallas_call(
        body, grid=(CW // LANES,), in_specs=_conv_specs(s_len) + [col],
        out_specs=[col, col, col, col, _bs((3, LANES), lambda j: (0, j)), _bs((1, LANES), lambda j: (0, j))],
        out_shape=[sd((s_len, CW), MM)] * 4 + [sd((3, CW), F32), sd((1, CW), F32)],
        name="conv_bwd", compiler_params=_cparams(1))(proj, proj, proj, proj, cw, cb, dys)


def _sg_specs(tm):
    row = lambda off: _bs((tm, SGW), lambda i: (i, off // SGW))
    return [row(OFF_SGI), row(OFF_SGI + SGW), row(OFF_SG + 2 * BW), _bs((1, SGW), lambda i: (0, 0)), _bs((1, SGW), lambda i: (0, 0)),
            _bs((SGG, SGC, SGC), lambda i: (0, 0, 0)), _bs((SGG, SGC, 1), lambda i: (0, 0, 0))]


def _sg_args(refs):
    u, v, sg, lg, lb, ws, bs = refs
    return (u[...], v[...], sg[...], lg[...], lb[...], [ws[g] for g in range(SGG)], [bs[g] for g in range(SGG)])


def _sg_call(proj, ln_g, ln_b, ws, bs):
    s_len = proj.shape[0]
    tm = min(s_len, 256)

    def body(*refs):
        refs[7][...] = _sg_fn(*_sg_args(refs[:7])).astype(refs[7].dtype)

    return pl.pallas_call(
        body, grid=(s_len // tm,), in_specs=_sg_specs(tm), out_specs=_bs((tm, SGW), lambda i: (i, 0)),
        out_shape=jax.ShapeDtypeStruct((s_len, SGW), MM), name="sgmlp", compiler_params=_cparams(1))(proj, proj, proj, ln_g, ln_b, ws, bs)


def _sg_bwd_call(proj, ln_g, ln_b, ws, bs, dys):
    s_len = proj.shape[0]
    tm = min(s_len, 256)

    def body(*refs):
        dys_ref = refs[7]
        duv_ref, dsg_ref, dlg_ref, dlb_ref, dws_ref, dbs_ref = refs[8:]
        _, vjp = jax.vjp(_sg_fn, *_sg_args(refs[:7]))
        du, dv, dsg, dlg, dlb, dws, dbs = vjp(dys_ref[...])
        duv_ref[...] = jnp.concatenate([du, dv], axis=-1).astype(duv_ref.dtype)
        dsg_ref[...] = dsg.astype(dsg_ref.dtype)

        @pl.when(pl.program_id(0) == 0)
        def _():
            for r in (dlg_ref, dlb_ref, dws_ref, dbs_ref):
                r[...] = jnp.zeros_like(r)
        dlg_ref[...] += dlg
        dlb_ref[...] += dlb
        for g in range(SGG):
            dws_ref[g] += dws[g]
            dbs_ref[g] += dbs[g]

    row = _bs((tm, SGW), lambda i: (i, 0))
    sd = jax.ShapeDtypeStruct
    return pl.pallas_call(
        body, grid=(s_len // tm,), in_specs=_sg_specs(tm) + [row],
        out_specs=[_bs((tm, 2 * SGW), lambda i: (i, 0)), row, _bs((1, SGW), lambda i: (0, 0)), _bs((1, SGW), lambda i: (0, 0)),
                   _bs((SGG, SGC, SGC), lambda i: (0, 0, 0)), _bs((SGG, SGC, 1), lambda i: (0, 0, 0))],
        out_shape=[sd((s_len, 2 * SGW), MM), sd((s_len, SGW), MM), sd((1, SGW), F32), sd((1, SGW), F32),
                   sd((SGG, SGC, SGC), F32), sd((SGG, SGC, 1), F32)],
        name="sgmlp_bwd", compiler_params=_cparams(1))(proj, proj, proj, ln_g, ln_b, ws, bs, dys)


def _memkv_call(mem, mem_g, wm, kg):
    m_len = mem.shape[0]

    def body(mem_ref, g_ref, w_ref, kg_ref, k_ref, v_ref):
        k, v = _memkv_fn(mem_ref[...], g_ref[...], w_ref[...], kg_ref[...])
        k_ref[...] = k.astype(k_ref.dtype)
        v_ref[...] = v.astype(v_ref.dtype)

    return pl.pallas_call(body, out_shape=[jax.ShapeDtypeStruct((m_len, MH * MHD), MM)] * 2, name="memkv",
                          compiler_params=pltpu.CompilerParams(vmem_limit_bytes=VMEM_LIMIT))(mem, mem_g, wm, kg)


def _memkv_bwd_call(mem, mem_g, wm, kg, dk, dv):
    def body(mem_ref, g_ref, w_ref, kg_ref, dk_ref, dv_ref, dg_ref, dw_ref, dkg_ref):
        _, vjp = jax.vjp(_memkv_fn, mem_ref[...], g_ref[...], _with_slot(w_ref[...]), kg_ref[...])
        _, dg, dw, dkg = vjp((dk_ref[...], dv_ref[...]))
        dg_ref[...] = dg
        dw_ref[...] = dw[1]
        dkg_ref[...] = dkg

    sd = jax.ShapeDtypeStruct
    return pl.pallas_call(body, out_shape=[sd((1, D), F32), sd((D, 2 * MH * MHD), F32), sd((1, MHD), F32)], name="memkv_bwd",
                          compiler_params=pltpu.CompilerParams(vmem_limit_bytes=VMEM_LIMIT))(mem, mem_g, wm, kg, dk, dv)


def _mem_specs(tm, m_len):
    w = MH * MHD
    return [_bs((tm, w), lambda i: (i, OFF_MQ // w)), _bs((tm, BW), lambda i: (i, (OFF_SG + 3 * BW) // BW)),
            _bs((m_len, w), lambda i: (0, 0)), _bs((m_len, w), lambda i: (0, 0)), _bs((1, MHD), lambda i: (0, 0))]


def _mem_call(proj, k, v, qg):
    s_len, m_len = proj.shape[0], k.shape[0]
    tm = min(s_len, 256)

    def body(mq_ref, sg_ref, k_ref, v_ref, qg_ref, y_ref):
        y_ref[...] = _mem_fn(mq_ref[...], sg_ref[...], k_ref[...], v_ref[...], qg_ref[...]).astype(y_ref.dtype)

    return pl.pallas_call(
        body, grid=(s_len // tm,), in_specs=_mem_specs(tm, m_len), out_specs=_bs((tm, BW), lambda i: (i, 0)),
        out_shape=jax.ShapeDtypeStruct((s_len, BW), MM), name="memattn", compiler_params=_cparams(1))(proj, proj, k, v, qg)


def _mem_bwd_call(proj, k, v, qg, dys):
    s_len, m_len = proj.shape[0], k.shape[0]
    tm = min(s_len, 256)
    w = MH * MHD

    def body(mq_ref, sg_ref, k_ref, v_ref, qg_ref, dys_ref, dmq_ref, dsg_ref, dk_ref, dv_ref, dqg_ref):
        _, vjp = jax.vjp(_mem_fn, mq_ref[...], sg_ref[...], k_ref[...].astype(F32), v_ref[...].astype(F32), qg_ref[...])
        dmq, dsg, dk, dv, dqg = vjp(dys_ref[...])
        dmq_ref[...] = dmq.astype(dmq_ref.dtype)
        dsg_ref[...] = dsg.astype(dsg_ref.dtype)

        @pl.when(pl.program_id(0) == 0)
        def _():
            for r in (dk_ref, dv_ref, dqg_ref):
                r[...] = jnp.zeros_like(r)
        dk_ref[...] += dk
        dv_ref[...] += dv
        dqg_ref[...] += dqg

    row = _bs((tm, BW), lambda i: (i, 0))
    kv = _bs((m_len, w), lambda i: (0, 0))
    sd = jax.ShapeDtypeStruct
    return pl.pallas_call(
        body, grid=(s_len // tm,), in_specs=_mem_specs(tm, m_len) + [row],
        out_specs=[row, row, kv, kv, _bs((1, MHD), lambda i: (0, 0))],
        out_shape=[sd((s_len, w), MM), sd((s_len, BW), MM), sd((m_len, w), F32), sd((m_len, w), F32), sd((1, MHD), F32)],
        name="memattn_bwd", compiler_params=_cparams(1))(proj, proj, k, v, qg, dys)


def _merge_specs(tm):
    row = _bs((tm, BW), lambda i: (i, 0))
    return [row, row, row, row, _bs((tm, NB * D), lambda i: (i, OFF_ML // (NB * D))), _bs((NB, D), lambda i: (0, 0)),
            _bs((N_CHIPS, NB, BW, D // N_CHIPS), lambda i: (0, 0, 0, 0)), _bs((D, D), lambda i: (0, 0))]


def _merge_call(ys, proj, bm, wb, wo, x):
    s_len = proj.shape[0]
    tm = min(s_len, 256)

    def body(ya, yb, yc, yd, lg_ref, bm_ref, wb_ref, wo_ref, x_ref, o_ref):
        out = _merge_fn([r[...] for r in (ya, yb, yc, yd)], lg_ref[...], [bm_ref[n:n + 1, :] for n in range(NB)],
                        [[wb_ref[j, n] for n in range(NB)] for j in range(N_CHIPS)], wo_ref[...])
        o_ref[...] = x_ref[...] + out

    xrow = _bs((tm, D), lambda i: (i, 0))
    return pl.pallas_call(
        body, grid=(s_len // tm,), in_specs=_merge_specs(tm) + [xrow], out_specs=xrow,
        out_shape=jax.ShapeDtypeStruct((s_len, D), F32), name="merge", compiler_params=_cparams(1))(*ys, proj, bm, wb, wo, x)


def _merge_bwd_call(ys, proj, bm, wb, wo, dout):
    s_len = proj.shape[0]
    tm = min(s_len, 256)

    def body(ya, yb, yc, yd, lg_ref, bm_ref, wb_ref, wo_ref, do_ref, dya, dyb, dyc, dyd, dlg_ref, dbm_ref, dwb_ref, dwo_ref):
        fn = lambda ys_, lg_, bm_, wb_, wo_: _merge_fn(ys_, lg_, bm_, wb_, wo_)
        _, vjp = jax.vjp(fn, [r[...].astype(F32) for r in (ya, yb, yc, yd)], lg_ref[...], [bm_ref[n:n + 1, :] for n in range(NB)],
                         [[_with_slot(wb_ref[j, n]) for n in range(NB)] for j in range(N_CHIPS)], _with_slot(wo_ref[...]))
        dys, dlg, dbm, dwb, dwo = vjp(do_ref[...])
        dwb, dwo = [[d[1] for d in row] for row in dwb], dwo[1]
        for r, d in zip((dya, dyb, dyc, dyd), dys):
            r[...] = d
        dlg_ref[...] = dlg.astype(dlg_ref.dtype)

        @pl.when(pl.program_id(0) == 0)
        def _():
            for r in (dbm_ref, dwb_ref, dwo_ref):
                r[...] = jnp.zeros_like(r)
        for n in range(NB):
            dbm_ref[n:n + 1, :] += dbm[n]
            for j in range(N_CHIPS):
                dwb_ref[j, n] += dwb[j][n]
        dwo_ref[...] += dwo

    row = _bs((tm, BW), lambda i: (i, 0))
    sd = jax.ShapeDtypeStruct
    wb_shape = (N_CHIPS, NB, BW, D // N_CHIPS)
    return pl.pallas_call(
        body, grid=(s_len // tm,), in_specs=_merge_specs(tm) + [_bs((tm, D), lambda i: (i, 0))],
        out_specs=[row, row, row, row, _bs((tm, NB * D), lambda i: (i, 0)), _bs((NB, D), lambda i: (0, 0)),
                   _bs(wb_shape, lambda i: (0, 0, 0, 0)), _bs((D, D), lambda i: (0, 0))],
        out_shape=[sd((s_len, BW), F32)] * 4 + [sd((s_len, NP), MM), sd((NB, D), F32), sd(wb_shape, F32), sd((D, D), F32)],
        name="merge_bwd", compiler_params=_cparams(1))(*ys, proj, bm, wb, wo, dout)


def _dh_call(dproj, w, x, g, dout, after=()):
    s_len = x.shape[0]
    tk = NP // 4
    after = list(after)

    def matmul_body(dp_ref, w_ref, *rest):
        o_ref = rest[-1]

        @pl.when(pl.program_id(0) == 0)
        def _():
            o_ref[...] = jnp.zeros_like(o_ref)
        o_ref[...] += lax.dot_general(dp_ref[...], w_ref[...], (((1,), (1,)), ((), ())), preferred_element_type=F32)

    dh = pl.pallas_call(
        matmul_body, grid=(NP // tk,),
        in_specs=[_bs((s_len, tk), lambda k: (0, k)), _bs((D, tk), lambda k: (0, k))] + [_ANY] * len(after),
        out_specs=_bs((s_len, D), lambda k: (0, 0)), out_shape=jax.ShapeDtypeStruct((s_len, D), F32),
        name="dh", compiler_params=_cparams(1))(dproj, w, *after)

    tm = min(s_len, 512)

    def norm_body(dh_ref, x_ref, g_ref, do_ref, dx_ref, dg_ref):
        _, vjp = jax.vjp(lambda x_, g_: _rms_n(x_, g_, D), x_ref[...], g_ref[...])
        dxr, dgr = vjp(dh_ref[...])
        dx_ref[...] = do_ref[...] + dxr

        @pl.when(pl.program_id(0) == 0)
        def _():
            dg_ref[...] = jnp.zeros_like(dg_ref)
        dg_ref[...] += dgr

    row = _bs((tm, D), lambda i: (i, 0))
    return pl.pallas_call(
        norm_body, grid=(s_len // tm,), in_specs=[row, row, _bs((1, D), lambda i: (0, 0)), row],
        out_specs=[row, _bs((1, D), lambda i: (0, 0))],
        out_shape=[jax.ShapeDtypeStruct((s_len, D), F32), jax.ShapeDtypeStruct((1, D), F32)],
        name="norm_bwd", compiler_params=_cparams(1))(dh, x, g, dout)


def _dw_call(h, dproj, after=()):
    s_len = h.shape[0]
    tn = 512
    after = list(after)

    def body(h_ref, dp_ref, *rest):
        o_ref, ht_ref = rest[-2], rest[-1]

        @pl.when(pl.program_id(0) == 0)
        def _():
            ht_ref[...] = h_ref[...].T
        o_ref[...] = jnp.dot(ht_ref[...], dp_ref[...], preferred_element_type=F32)

    return pl.pallas_call(
        body, grid=(NP // tn,),
        in_specs=[_bs((s_len, D), lambda j: (0, 0)), _bs((s_len, tn), lambda j: (0, j))] + [_ANY] * len(after),
        out_specs=_bs((D, tn), lambda j: (0, j)), out_shape=jax.ShapeDtypeStruct((D, NP), F32),
        scratch_shapes=[pltpu.VMEM((D, s_len), h.dtype)], name="dw_in", compiler_params=_cparams(1))(h, dproj, *after)


def _loss_call(y, target):
    s_len = y.shape[0]
    tm = min(s_len, 512)

    def body(y_ref, t_ref, dy_ref, l_ref):
        e = y_ref[...] - t_ref[...]
        dy_ref[...] = e * (1.0 / D)

        @pl.when(pl.program_id(0) == 0)
        def _():
            l_ref[...] = jnp.zeros_like(l_ref)
        l_ref[...] += jnp.sum(e * e, axis=0, keepdims=True)

    row = _bs((tm, D), lambda i: (i, 0))
    return pl.pallas_call(
        body, grid=(s_len // tm,), in_specs=[row, row], out_specs=[row, _bs((1, D), lambda i: (0, 0))],
        out_shape=[jax.ShapeDtypeStruct((s_len, D), F32), jax.ShapeDtypeStruct((1, D), F32)],
        name="loss", compiler_params=_cparams(1))(y, target)


def _adamw_small_call(ws, gs, ms, vs, name):
    n = len(ws)

    def body(*refs):
        for t in range(n):
            w_ref, g_ref, m_ref, v_ref = refs[t], refs[n + t], refs[2 * n + t], refs[3 * n + t]
            d_ref, nm_ref, nv_ref = refs[4 * n + 3 * t:4 * n + 3 * t + 3]
            gv = g_ref[...]
            m2 = ADAM_B1 * m_ref[...] + (1.0 - ADAM_B1) * gv
            v2 = ADAM_B2 * v_ref[...] + (1.0 - ADAM_B2) * (gv * gv)
            m_hat = m2 / (1.0 - ADAM_B1 ** ADAM_STEP)
            v_hat = v2 / (1.0 - ADAM_B2 ** ADAM_STEP)
            d_ref[...] = -ADAM_LR * (m_hat / (jnp.sqrt(v_hat) + ADAM_EPS) + ADAM_WD * w_ref[...])
            nm_ref[...] = m2
            nv_ref[...] = v2

    return pl.pallas_call(
        body, out_shape=[jax.ShapeDtypeStruct(w.shape, F32) for w in ws for _ in range(3)], name=name,
        compiler_params=pltpu.CompilerParams(vmem_limit_bytes=VMEM_LIMIT))(*ws, *gs, *ms, *vs)


def _adamw_layer_call(layer, ws, gs, ms, vs, prev, after, name, steps=8):
    n = len(ws)
    after = list(after)
    n_prev = 4 * n if prev is not None else 0

    def body(*refs):
        outs = refs[len(refs) - 4 * n:]
        for t in range(n):
            w_ref, g_ref, m_ref, v_ref = refs[t], refs[n + t], refs[2 * n + t], refs[3 * n + t]
            g_out, d_out, m_out, v_out = outs[4 * t:4 * t + 4]
            gv = g_ref[...]
            m2 = ADAM_B1 * m_ref[0] + (1.0 - ADAM_B1) * gv
            v2 = ADAM_B2 * v_ref[0] + (1.0 - ADAM_B2) * (gv * gv)
            m_hat = m2 / (1.0 - ADAM_B1 ** ADAM_STEP)
            v_hat = v2 / (1.0 - ADAM_B2 ** ADAM_STEP)
            g_out[0] = gv
            d_out[0] = -ADAM_LR * (m_hat / (jnp.sqrt(v_hat) + ADAM_EPS) + ADAM_WD * w_ref[0])
            m_out[0] = m2
            v_out[0] = v2

    def lay(a):
        return _bs((1, a.shape[1] // steps, a.shape[2]), lambda i: (layer, i, 0))

    in_specs = ([lay(a) for a in ws] + [_bs((g.shape[0] // steps, g.shape[1]), lambda i: (i, 0)) for g in gs]
                + [lay(a) for a in ms] + [lay(a) for a in vs] + [_ANY] * (n_prev + len(after)))
    return pl.pallas_call(
        body, grid=(steps,), in_specs=in_specs, out_specs=[lay(ws[t]) for t in range(n) for _ in range(4)],
        out_shape=[jax.ShapeDtypeStruct(ws[t].shape, F32) for t in range(n) for _ in range(4)],
        input_output_aliases={4 * n + q: q for q in range(n_prev)}, name=name, compiler_params=_cparams(1),
    )(*ws, *gs, *ms, *vs, *(prev if prev is not None else []), *after)


def _row_tile(rows):
    for cand in (512, 256, 128, 64, 32, 16, 8):
        if rows % cand == 0 and rows > cand:
            return cand
    return rows


def _pair_sum_call(grads, from_sibling, core, name):
    n = len(grads)

    def body(core_ref, *refs):
        for t in range(n):
            refs[2 * n + t][...] = (refs[t][...].astype(F32) + refs[n + t][...].astype(F32)).astype(MM)

    half = lambda g: (1, g.shape[1] // 2, g.shape[2])
    grid_spec = pltpu.PrefetchScalarGridSpec(
        num_scalar_prefetch=1, grid=(N_CHIPS,),
        in_specs=[pl.BlockSpec(half(g), lambda j, core_ref: (j, core_ref[0], 0)) for g in grads]
        + [pl.BlockSpec(half(g), lambda j, core_ref: (j, 0, 0)) for g in grads],
        out_specs=[pl.BlockSpec(half(g), lambda j, core_ref: (j, 0, 0)) for g in grads])
    return pl.pallas_call(
        body, grid_spec=grid_spec, out_shape=[jax.ShapeDtypeStruct((N_CHIPS,) + half(g)[1:], MM) for g in grads], name=name,
        compiler_params=_cparams(1))(core, *grads, *from_sibling)


def _owner_sum_call(chip_sums, from_chips, chip_core, name):
    n = len(chip_sums)
    steps = 4

    def body(ids_ref, *refs):
        for t in range(n):
            a, b = refs[t], refs[n + t]
            refs[2 * n + t][...] = ((a[0].astype(F32) + b[0].astype(F32)) + b[1].astype(F32)) + b[2].astype(F32)

    tile = lambda p: (p.shape[1] // steps, p.shape[2])
    grid_spec = pltpu.PrefetchScalarGridSpec(
        num_scalar_prefetch=1, grid=(steps,),
        in_specs=[pl.BlockSpec((1,) + tile(p), lambda i, ids_ref: (ids_ref[0], i, 0)) for p in chip_sums]
        + [pl.BlockSpec((3,) + tile(p), lambda i, ids_ref: (0, i, 0)) for p in chip_sums],
        out_specs=[pl.BlockSpec(tile(p), lambda i, ids_ref: (ids_ref[1] * steps + i, 0)) for p in chip_sums])
    return pl.pallas_call(
        body, grid_spec=grid_spec, out_shape=[jax.ShapeDtypeStruct((2 * p.shape[1], p.shape[2]), F32) for p in chip_sums],
        name=name, compiler_params=_cparams(1))(chip_core, *chip_sums, *from_chips)


def _sum8_call(parts):
    n, rows, cols = parts.shape
    tr = _row_tile(rows)

    def body(p_ref, o_ref):
        acc = p_ref[0]
        for k in range(1, n):
            acc = acc + p_ref[k]
        o_ref[...] = acc

    return pl.pallas_call(
        body, grid=(rows // tr,), in_specs=[_bs((n, tr, cols), lambda i: (0, i, 0))], out_specs=_bs((tr, cols), lambda i: (i, 0)),
        out_shape=jax.ShapeDtypeStruct((rows, cols), F32), name="sum_small_grads", compiler_params=_cparams(1))(parts)


_ANY = pl.BlockSpec(memory_space=pl.ANY)


def _half_rows(ref, lead, half, which):
    rows = pl.ds(pl.multiple_of(half * which, half), half)
    return ref.at[rows] if lead is None else ref.at[lead, rows]


_HBM = pl.BlockSpec(memory_space=pltpu.HBM)
_SEM = pl.BlockSpec(memory_space=pltpu.SEMAPHORE)
_ORDERED_EFFECT = pltpu.CompilerParams(has_side_effects=pltpu.SideEffectType.DATAFLOW_SIDE_EFFECTING)


_VMEM = pl.BlockSpec(memory_space=pltpu.VMEM)
_TOKEN = jax.ShapeDtypeStruct((8, LANES), F32)


def _in_hbm(a):
    return pltpu.with_memory_space_constraint(a, pltpu.HBM)


def _tie(small, token):
    return small + token[0:1, 0:1].reshape((1,) * small.ndim)


def _peer(k):
    x, y, c = lax.axis_index("x"), lax.axis_index("y"), lax.axis_index("c")
    bx, by, bc = (k >> 2) & 1, (k >> 1) & 1, k & 1
    return (x ^ bx if bx else x, y ^ by if by else y, c ^ bc if bc else c)


def _place_block_call(blk, index, name):
    rows, cols = blk.shape

    def body(idx_ref, b_ref, o_ref):
        o_ref[0] = b_ref[...]

    grid_spec = pltpu.PrefetchScalarGridSpec(
        num_scalar_prefetch=1, grid=(1,), in_specs=[pl.BlockSpec((rows, cols), lambda i, idx_ref: (0, 0))],
        out_specs=pl.BlockSpec((1, rows, cols), lambda i, idx_ref: (idx_ref[0], 0, 0)))
    return pl.pallas_call(body, grid_spec=grid_spec, out_shape=jax.ShapeDtypeStruct((8, rows, cols), blk.dtype), name=name,
                          compiler_params=_cparams(1))(index, blk)


def _small_gather_start_call(blk, buf, after, name):
    after = list(after)

    def body(*refs):
        b_ref, out_ref = refs[0], refs[2 + len(after)]
        send_sems, recv_sems, token = refs[3 + len(after):]
        x, y, c = lax.axis_index("x"), lax.axis_index("y"), lax.axis_index("c")
        for k in range(1, 8):
            pltpu.make_async_remote_copy(src_ref=b_ref, dst_ref=out_ref.at[4 * x + 2 * y + c], send_sem=send_sems.at[k - 1],
                                         recv_sem=recv_sems.at[k - 1], device_id=_peer(k), device_id_type=MESH_ID).start()
        token[...] = jnp.zeros_like(token)

    dma = pltpu.SemaphoreType.DMA
    return pl.pallas_call(
        body, out_shape=[pltpu.HBM(buf.shape, buf.dtype), dma((7,)), dma((7,)), _TOKEN],
        in_specs=[_HBM, _HBM] + [_ANY] * len(after), out_specs=[_HBM, _SEM, _SEM, _VMEM],
        input_output_aliases={1: 0}, name=name, compiler_params=_ORDERED_EFFECT)(_in_hbm(blk), _in_hbm(buf), *after)


def _small_gather_finish_call(blk, buf, send_sems, recv_sems, after, name):
    after = list(after)

    def body(*refs):
        b_ref, in_ref, send_ref, recv_ref = refs[:4]
        x, y, c = lax.axis_index("x"), lax.axis_index("y"), lax.axis_index("c")
        for k in range(1, 8):
            px, py, pc = _peer(k)
            pltpu.make_async_remote_copy(src_ref=b_ref, dst_ref=in_ref.at[4 * px + 2 * py + pc], send_sem=send_ref.at[k - 1],
                                         recv_sem=recv_ref.at[k - 1], device_id=(px, py, pc), device_id_type=MESH_ID).wait()

    return pl.pallas_call(
        body, out_shape=pltpu.HBM(buf.shape, buf.dtype), in_specs=[_HBM, _HBM, _SEM, _SEM] + [_ANY] * len(after),
        out_specs=_HBM, input_output_aliases={1: 0}, name=name, compiler_params=_ORDERED_EFFECT,
    )(_in_hbm(blk), buf, send_sems, recv_sems, *after)


def _pair_exchange_start_call(grads, name):
    n = len(grads)
    half = [g.shape[1] // 2 for g in grads]

    def body(*refs):
        srcs, outs = refs[:n], refs[n:2 * n]
        send_sems, recv_sems, token = refs[2 * n:]
        x, y, c = lax.axis_index("x"), lax.axis_index("y"), lax.axis_index("c")
        for t in range(n):
            pltpu.make_async_remote_copy(
                src_ref=srcs[t].at[:, pl.ds(pl.multiple_of(half[t] * (1 - c), half[t]), half[t])], dst_ref=outs[t],
                send_sem=send_sems.at[t], recv_sem=recv_sems.at[t], device_id=(x, y, 1 - c), device_id_type=MESH_ID).start()
        token[...] = jnp.zeros_like(token)

    dma = pltpu.SemaphoreType.DMA
    return pl.pallas_call(
        body, out_shape=[pltpu.HBM((g.shape[0], g.shape[1] // 2, g.shape[2]), g.dtype) for g in grads] + [dma((n,)), dma((n,)), _TOKEN],
        in_specs=[_HBM] * n, out_specs=[_HBM] * n + [_SEM, _SEM, _VMEM], name=name, compiler_params=_ORDERED_EFFECT,
    )(*[_in_hbm(g) for g in grads])


def _pair_exchange_finish_call(grads, bufs, send_sems, recv_sems, after, name):
    n = len(grads)
    after = list(after)
    half = [g.shape[1] // 2 for g in grads]

    def body(*refs):
        srcs, ins, send_ref, recv_ref = refs[:n], refs[n:2 * n], refs[2 * n], refs[2 * n + 1]
        x, y, c = lax.axis_index("x"), lax.axis_index("y"), lax.axis_index("c")
        for t in range(n):
            pltpu.make_async_remote_copy(
                src_ref=srcs[t].at[:, pl.ds(pl.multiple_of(half[t] * (1 - c), half[t]), half[t])], dst_ref=ins[t],
                send_sem=send_ref.at[t], recv_sem=recv_ref.at[t], device_id=(x, y, 1 - c), device_id_type=MESH_ID).wait()

    return pl.pallas_call(
        body, out_shape=[pltpu.HBM(b.shape, b.dtype) for b in bufs],
        in_specs=[_HBM] * (2 * n) + [_SEM, _SEM] + [_ANY] * len(after), out_specs=[_HBM] * n,
        input_output_aliases={n + t: t for t in range(n)}, name=name, compiler_params=_ORDERED_EFFECT,
    )(*[_in_hbm(g) for g in grads], *bufs, send_sems, recv_sems, *after)


def _chip_scatter_start_call(chip_sums, name):
    n = len(chip_sums)

    def body(*refs):
        srcs, outs = refs[:n], refs[n:2 * n]
        send_sems, recv_sems, token = refs[2 * n:]
        x, y, c = lax.axis_index("x"), lax.axis_index("y"), lax.axis_index("c")
        chips = [(1 - x, y), (x, 1 - y), (1 - x, 1 - y)]
        for k, (cx, cy) in enumerate(chips):
            for t in range(n):
                pltpu.make_async_remote_copy(
                    src_ref=srcs[t].at[2 * cx + cy], dst_ref=outs[t].at[k], send_sem=send_sems.at[3 * t + k],
                    recv_sem=recv_sems.at[3 * t + k], device_id=(cx, cy, c), device_id_type=MESH_ID).start()
        token[...] = jnp.zeros_like(token)

    dma = pltpu.SemaphoreType.DMA
    return pl.pallas_call(
        body, out_shape=[pltpu.HBM((3,) + p.shape[1:], p.dtype) for p in chip_sums] + [dma((3 * n,)), dma((3 * n,)), _TOKEN],
        in_specs=[_HBM] * n, out_specs=[_HBM] * n + [_SEM, _SEM, _VMEM], name=name, compiler_params=_ORDERED_EFFECT,
    )(*[_in_hbm(p) for p in chip_sums])


def _chip_scatter_finish_call(chip_sums, bufs, send_sems, recv_sems, after, name):
    n = len(chip_sums)
    after = list(after)

    def body(*refs):
        srcs, ins, send_ref, recv_ref = refs[:n], refs[n:2 * n], refs[2 * n], refs[2 * n + 1]
        x, y, c = lax.axis_index("x"), lax.axis_index("y"), lax.axis_index("c")
        chips = [(1 - x, y), (x, 1 - y), (1 - x, 1 - y)]
        for k, (cx, cy) in enumerate(chips):
            for t in range(n):
                pltpu.make_async_remote_copy(
                    src_ref=srcs[t].at[2 * cx + cy], dst_ref=ins[t].at[k], send_sem=send_ref.at[3 * t + k],
                    recv_sem=recv_ref.at[3 * t + k], device_id=(cx, cy, c), device_id_type=MESH_ID).wait()

    return pl.pallas_call(
        body, out_shape=[pltpu.HBM(b.shape, b.dtype) for b in bufs],
        in_specs=[_HBM] * (2 * n) + [_SEM, _SEM] + [_ANY] * len(after), out_specs=[_HBM] * n,
        input_output_aliases={n + t: t for t in range(n)}, name=name, compiler_params=_ORDERED_EFFECT,
    )(*[_in_hbm(p) for p in chip_sums], *bufs, send_sems, recv_sems, *after)


def _place_own_call(mine, chip_core, name):
    n = len(mine)

    def body(ids_ref, *refs):
        for t in range(n):
            refs[n + t][0] = refs[t][...]

    def imap_out(s):
        pad = (0,) * (s.ndim - 1)
        return lambda i, ids_ref: (ids_ref[0], ids_ref[1]) + pad

    grid_spec = pltpu.PrefetchScalarGridSpec(
        num_scalar_prefetch=1, grid=(1,), in_specs=[pl.BlockSpec(s.shape, lambda i, ids_ref, k=s.ndim: (0,) * k) for s in mine],
        out_specs=[pl.BlockSpec((1,) + s.shape, imap_out(s)) for s in mine])
    return pl.pallas_call(
        body, grid_spec=grid_spec,
        out_shape=[jax.ShapeDtypeStruct((N_CHIPS, 2 * s.shape[0]) + s.shape[1:], s.dtype) for s in mine],
        name=name, compiler_params=_cparams(1))(chip_core, *mine)


def _gather_start_call(mine, bufs, after, name):
    n = len(mine)
    half = [s.shape[0] for s in mine]

    def body(*refs):
        srcs, outs = refs[:n], refs[2 * n + 1:3 * n + 1]
        send_sems, recv_sib, recv_ici, token = refs[3 * n + 1:]
        x, y, c = lax.axis_index("x"), lax.axis_index("y"), lax.axis_index("c")
        chips = [(1 - x, y), (x, 1 - y), (1 - x, 1 - y)]
        for t in range(n):
            dst = _half_rows(outs[t], 2 * x + y, half[t], c)
            pltpu.make_async_remote_copy(src_ref=srcs[t], dst_ref=dst, send_sem=send_sems.at[4 * t], recv_sem=recv_sib.at[t],
                                         device_id=(x, y, 1 - c), device_id_type=MESH_ID).start()
            for j, chip in enumerate(chips):
                pltpu.make_async_remote_copy(src_ref=srcs[t], dst_ref=dst, send_sem=send_sems.at[4 * t + 1 + j],
                                             recv_sem=recv_ici.at[3 * t + j], device_id=(*chip, c), device_id_type=MESH_ID).start()
        token[...] = jnp.zeros_like(token)

    dma = pltpu.SemaphoreType.DMA
    return pl.pallas_call(
        body, out_shape=[pltpu.HBM(b.shape, b.dtype) for b in bufs] + [dma((4 * n,)), dma((n,)), dma((3 * n,)), _TOKEN],
        in_specs=[_HBM] * (2 * n) + [_ANY], out_specs=[_HBM] * n + [_SEM] * 3 + [_VMEM],
        input_output_aliases={n + t: t for t in range(n)}, name=name, compiler_params=_ORDERED_EFFECT,
    )(*[_in_hbm(s) for s in mine], *[_in_hbm(b) for b in bufs], after)


def _gather_forward_call(bufs, recv_ici, after, name):
    n = len(bufs)
    half = [b.shape[1] // 2 for b in bufs]

    def body(*refs):
        ins, recv_ici_ref = refs[:n], refs[n]
        outs = refs[n + 2:2 * n + 2]
        send_fwd, recv_fwd, token = refs[2 * n + 2:]
        x, y, c = lax.axis_index("x"), lax.axis_index("y"), lax.axis_index("c")
        chips = [(1 - x, y), (x, 1 - y), (1 - x, 1 - y)]
        for j, (cx, cy) in enumerate(chips):
            for t in range(n):
                landed = _half_rows(ins[t], 2 * cx + cy, half[t], c)
                dst = _half_rows(outs[t], 2 * cx + cy, half[t], c)
                pltpu.make_async_remote_copy(src_ref=landed, dst_ref=landed, send_sem=send_fwd.at[3 * t + j],
                                             recv_sem=recv_ici_ref.at[3 * t + j], device_id=(cx, cy, c),
                                             device_id_type=MESH_ID).wait_recv()
                pltpu.make_async_remote_copy(src_ref=landed, dst_ref=dst, send_sem=send_fwd.at[3 * t + j],
                                             recv_sem=recv_fwd.at[3 * t + j], device_id=(x, y, 1 - c),
                                             device_id_type=MESH_ID).start()
        token[...] = jnp.zeros_like(token)

    dma = pltpu.SemaphoreType.DMA
    return pl.pallas_call(
        body, out_shape=[pltpu.HBM(b.shape, b.dtype) for b in bufs] + [dma((3 * n,)), dma((3 * n,)), _TOKEN],
        in_specs=[_HBM] * n + [_SEM, _ANY], out_specs=[_HBM] * n + [_SEM] * 2 + [_VMEM],
        input_output_aliases={t: t for t in range(n)}, name=name, compiler_params=_ORDERED_EFFECT,
    )(*bufs, recv_ici, after)


def _gather_finish_call(shards, bufs, send_sems, recv_sib, send_fwd, recv_fwd, after, name):
    n = len(bufs)
    half = [b.shape[1] // 2 for b in bufs]

    def body(*refs):
        srcs, ins = refs[:n], refs[n:2 * n]
        send_ref, recv_sib_ref, send_fwd_ref, recv_fwd_ref = refs[2 * n:2 * n + 4]
        x, y, c = lax.axis_index("x"), lax.axis_index("y"), lax.axis_index("c")
        chips = [(1 - x, y), (x, 1 - y), (1 - x, 1 - y)]
        sibling = (x, y, 1 - c)
        for t in range(n):
            for k in range(4):
                pltpu.make_async_remote_copy(src_ref=srcs[t], dst_ref=srcs[t], send_sem=send_ref.at[4 * t + k],
                                             recv_sem=recv_sib_ref.at[t], device_id=sibling, device_id_type=MESH_ID).wait_send()
            from_sibling = _half_rows(ins[t], 2 * x + y, half[t], 1 - c)
            pltpu.make_async_remote_copy(src_ref=from_sibling, dst_ref=from_sibling, send_sem=send_ref.at[4 * t],
                                         recv_sem=recv_sib_ref.at[t], device_id=sibling, device_id_type=MESH_ID).wait_recv()
            for j, (cx, cy) in enumerate(chips):
                sent = _half_rows(ins[t], 2 * cx + cy, half[t], c)
                passed = _half_rows(ins[t], 2 * cx + cy, half[t], 1 - c)
                pltpu.make_async_remote_copy(src_ref=sent, dst_ref=passed, send_sem=send_fwd_ref.at[3 * t + j],
                                             recv_sem=recv_fwd_ref.at[3 * t + j], device_id=sibling, device_id_type=MESH_ID).wait()

    return pl.pallas_call(
        body, out_shape=[pltpu.HBM(b.shape, b.dtype) for b in bufs],
        in_specs=[_HBM] * (2 * n) + [_SEM] * 4 + [_ANY], out_specs=[_HBM] * n,
        input_output_aliases={n + t: t for t in range(n)}, name=name, compiler_params=_ORDERED_EFFECT,
    )(*[_in_hbm(s) for s in shards], *bufs, send_sems, recv_sib, send_fwd, recv_fwd, after)


def _pair_gather_call(bufs, name):
    n = len(bufs)
    half = [b.shape[0] // 2 for b in bufs]

    def body(*refs):
        srcs, outs, send_sems, recv_sems = refs[:n], refs[n:2 * n], refs[2 * n], refs[2 * n + 1]
        x, y, c = lax.axis_index("x"), lax.axis_index("y"), lax.axis_index("c")
        for t in range(n):
            pltpu.make_async_remote_copy(
                src_ref=_half_rows(srcs[t], None, half[t], c), dst_ref=_half_rows(outs[t], None, half[t], c),
                send_sem=send_sems.at[t], recv_sem=recv_sems.at[t], device_id=(x, y, 1 - c), device_id_type=MESH_ID).start()
        for t in range(n):
            pltpu.make_async_remote_copy(
                src_ref=_half_rows(srcs[t], None, half[t], c), dst_ref=_half_rows(outs[t], None, half[t], 1 - c),
                send_sem=send_sems.at[t], recv_sem=recv_sems.at[t], device_id=(x, y, 1 - c), device_id_type=MESH_ID).wait()

    return pl.pallas_call(
        body, out_shape=[jax.ShapeDtypeStruct(b.shape, b.dtype) for b in bufs], in_specs=[_ANY] * n, out_specs=[_ANY] * n,
        input_output_aliases={t: t for t in range(n)},
        scratch_shapes=[pltpu.SemaphoreType.DMA((n,)), pltpu.SemaphoreType.DMA((n,))], name=name)(*bufs)


def _pack_rows(flats, dtype, row_multiple):
    flat = jnp.concatenate([f.reshape(-1).astype(dtype) for f in flats])
    n = flat.shape[0]
    rows = -(-n // PACK_W)
    rows = -(-rows // row_multiple) * row_multiple
    return jnp.pad(flat, (0, rows * PACK_W - n)).reshape(rows, PACK_W)


def _unpack(flat, shapes):
    out, off = [], 0
    for shp in shapes:
        n = math.prod(shp)
        out.append(flat[off:off + n].reshape(shp))
        off += n
    return out


_W_IN_SEGMENTS = ((R_ML, R_END, OFF_ML), (R_SG, R_ML, OFF_SG), (R_CV, R_SGI, OFF_CV), (R_SGI, R_MQ, OFF_SGI), (R_MQ, R_SG, OFF_MQ),
                  (R_CQ, R_CKV, OFF_CQ), (R_CKV, R_KR, OFF_CKV), (R_KR, R_CV, OFF_KR + NOPE))
W_IN_SHARD = R_END // N_CHIPS


def _realign_call(wg):
    tr = 128

    def body(w_ref, o_ref):
        pieces, pos = [], 0
        for r0, r1, a0 in _W_IN_SEGMENTS:
            if a0 > pos:
                pieces.append(jnp.zeros((tr, a0 - pos), o_ref.dtype))
            while r0 < r1:
                j = r0 // W_IN_SHARD
                hi = min(r1, (j + 1) * W_IN_SHARD)
                pieces.append(w_ref[j, :, r0 - j * W_IN_SHARD:hi - j * W_IN_SHARD])
                a0, r0 = a0 + hi - r0, hi
            pos = a0
        pieces.append(jnp.zeros((tr, NP - pos), o_ref.dtype))
        o_ref[...] = jnp.concatenate(pieces, axis=1)

    return pl.pallas_call(
        body, grid=(D // tr,), in_specs=[_bs((N_CHIPS, tr, W_IN_SHARD), lambda i: (0, i, 0))],
        out_specs=_bs((tr, NP), lambda i: (i, 0)), out_shape=jax.ShapeDtypeStruct((D, NP), wg.dtype),
        name="w_in_realign", compiler_params=_cparams(1))(wg)


def _unalign_call(dw, out_dtype):
    tr = 128
    by_ref = sorted(_W_IN_SEGMENTS)

    def body(dw_ref, o_ref):
        for j in range(N_CHIPS):
            lo_j, hi_j = j * W_IN_SHARD, (j + 1) * W_IN_SHARD
            pieces = []
            for r0, r1, a0 in by_ref:
                lo, hi = max(r0, lo_j), min(r1, hi_j)
                if lo < hi:
                    pieces.append(dw_ref[:, a0 + lo - r0:a0 + hi - r0])
            o_ref[j] = jnp.concatenate(pieces, axis=1).astype(o_ref.dtype)

    return pl.pallas_call(
        body, grid=(D // tr,), in_specs=[_bs((tr, NP), lambda i: (i, 0))],
        out_specs=_bs((N_CHIPS, tr, W_IN_SHARD), lambda i: (0, i, 0)),
        out_shape=jax.ShapeDtypeStruct((N_CHIPS, D, W_IN_SHARD), out_dtype), name="w_in_unalign", compiler_params=_cparams(1))(dw)


def _wuq_to_heads(w):
    w3 = w.reshape(QL, H, QKH)
    w3 = jnp.pad(w3, ((0, 0), (0, 0), (0, LANES - QKH)))
    return jnp.transpose(w3, (1, 0, 2))


def _wuq_from_heads(wh):
    return jnp.transpose(wh[:, :, :QKH], (1, 0, 2)).reshape(QL, H * QKH)


def _wukv_to_heads(w):
    w3 = w.reshape(KVL, H, NOPE + VH)
    wkn = jnp.transpose(jnp.pad(w3[:, :, :NOPE], ((0, 0), (0, 0), (0, LANES - NOPE))), (1, 0, 2))
    wv3 = w3[:, :, NOPE:]
    z = jnp.zeros((KVL, VH), w.dtype)
    cols = []
    for h in range(H):
        cols += [wv3[:, h], z] if h % 2 == 0 else [z, wv3[:, h]]
    return wkn, jnp.concatenate(cols, axis=1)


def _wukv_from_heads(wkn, wv):
    kn = jnp.transpose(wkn[:, :, :NOPE], (1, 0, 2))
    vs = jnp.stack([wv[:, LANES * h + VH * (h % 2):LANES * h + VH * (h % 2) + VH] for h in range(H)], axis=1)
    return jnp.concatenate([kn, vs], axis=2).reshape(KVL, H * (NOPE + VH))


def _layer_fwd(x, mem, tabs, p):
    proj, h = _proj_call(x, p["norm_g"], p["w_in"])
    if p.get("late") is not None:
        p = dict(p, **p["late"](proj))
    q, k, v = _mla_prep_call(proj, tabs, p["cq_g"], p["ckv_g"], p["qg"], p["kg"], p["wuq"], p["wkn"], p["wv"])
    ya, attn_o, attn_lse = _attn_call(q, k, v, proj)
    bm = p["bm"]
    if p.get("after_attn") is not None:
        bm = _tie(bm, p["after_attn"](ya))
    yb = _conv_call(proj, p["conv_w"], p["conv_b"])
    yc = _sg_call(proj, p["ln_g"], p["ln_b"], p["ws"], p["bs"])
    mk, mv = _memkv_call(mem, p["mem_g"], p["wm"], p["mkg"])
    yd = _mem_call(proj, mk, mv, p["mqg"])
    out = _merge_call((ya, yb, yc, yd), proj, bm, p["wb"], p["wo"], x)
    return out, dict(p=p, x=x, proj=proj, h=h, q=q, k=k, v=v, attn_o=attn_o, attn_lse=attn_lse, ys=(ya, yb, yc, yd), mk=mk, mv=mv)


def _layer_bwd(dout, mem, tabs, p, sv, start_after=None, on_rest_grads=None, on_grads=None):
    proj = sv["proj"]
    bm = p["bm"] if start_after is None else _tie(p["bm"], start_after)
    dya, dyb, dyc, dyd, dml, dbm, dwb, dwo = _merge_bwd_call(sv["ys"], proj, bm, p["wb"], p["wo"], dout)
    dq, dk, dv, dsg_a = _attn_bwd_call(sv["q"], sv["k"], sv["v"], proj, dya, sv["attn_o"], sv["attn_lse"])
    dlat, dcqg, dckvg, dqg, dkg, dwuq, dwkn, dwv = _mla_prep_bwd_call(
        proj, tabs, p["cq_g"], p["ckv_g"], p["qg"], p["kg"], p["wuq"], p["wkn"], p["wv"], dq, dk, dv)
    dbg, dcg, dxi, dsg_b, dcw, dcb = _conv_bwd_call(proj, p["conv_w"], p["conv_b"], dyb)
    duv, dsg_c, dlg, dlb, dws, dbs = _sg_bwd_call(proj, p["ln_g"], p["ln_b"], p["ws"], p["bs"], dyc)
    dmq, dsg_d, dmk, dmv, dmqg = _mem_bwd_call(proj, sv["mk"], sv["mv"], p["mqg"], dyd)
    dmem_g, dwm, dmkg = _memkv_bwd_call(mem, p["mem_g"], p["wm"], p["mkg"], dmk, dmv)
    grads = dict(cq_norm_g=dcqg, ckv_norm_g=dckvg, mla_q_norm_g=dqg[:, :QKH], mla_k_norm_g=dkg[:, :QKH],
                 conv_w=dcw, conv_b=dcb, sg_ln_g=dlg, sg_ln_b=dlb, w_spatial=dws, b_spatial=dbs,
                 mem_norm_g=dmem_g, mem_q_norm_g=dmqg, mem_k_norm_g=dmkg, b_merge=dbm,
                 wuq_heads=dwuq, wkn_heads=dwkn, wv_heads=dwv, w_mem_kv=dwm, w_branch_chips=dwb, w_out=dwo)
    started = [on_rest_grads(grads)] if on_rest_grads is not None else []
    dproj, off = dml, NB * D
    for piece in (dsg_a, dsg_b, dsg_c, dsg_d, dbg, dcg, dxi, duv, dmq, dlat):
        dproj = lax.dynamic_update_slice(dproj, piece, (0, off))
        off += piece.shape[1]
    grads["w_in_aligned"] = _dw_call(sv["h"], dproj, started)
    tokens = on_grads(grads) if on_grads is not None else ()
    dx, dnorm_g = _dh_call(dproj, p["w_in"], sv["x"], p["norm_g"], dout, tokens)
    grads["norm_g"] = dnorm_g
    return dx, grads


def _chips_to_cols(a):
    return jnp.concatenate([a[j] for j in range(N_CHIPS)], axis=1)


def _cols_to_chips(a):
    cols = a.shape[1] // N_CHIPS
    return jnp.stack([a[:, cols * j:cols * (j + 1)] for j in range(N_CHIPS)])


def _layer_params_first(l, rep, w_in_gathered, conv_w, b_merge):
    pad_g = lambda g: jnp.pad(g, (0, LANES - QKH)).reshape(1, LANES)
    return dict(
        norm_g=rep["norm_g"][l].reshape(1, D), w_in=_realign_call(w_in_gathered),
        cq_g=rep["cq_norm_g"][l].reshape(1, QL), ckv_g=rep["ckv_norm_g"][l].reshape(1, KVL),
        qg=pad_g(rep["mla_q_norm_g"][l]), kg=pad_g(rep["mla_k_norm_g"][l]),
        conv_w=conv_w, conv_b=rep["conv_b"][l].reshape(1, CW),
        ln_g=rep["sg_ln_g"][l].reshape(1, SGW), ln_b=rep["sg_ln_b"][l].reshape(1, SGW),
        ws=rep["w_spatial"][l], bs=rep["b_spatial"][l].reshape(SGG, SGC, 1),
        mem_g=rep["mem_norm_g"][l].reshape(1, D),
        mqg=rep["mem_q_norm_g"][l].reshape(1, MHD), mkg=rep["mem_k_norm_g"][l].reshape(1, MHD), bm=b_merge)


def _layer_params_rest(gathered):
    wkn, wv = _wukv_to_heads(_chips_to_cols(gathered["w_ukv"]))
    return dict(wuq=_wuq_to_heads(_chips_to_cols(gathered["w_uq"])), wkn=wkn, wv=wv,
                wm=gathered["w_mem_kv"].reshape(D, 2 * MH * MHD), wb=gathered["w_branch"], wo=gathered["w_out"].reshape(D, D))


def _layer_params(l, rep, gathered, conv_w, b_merge):
    return dict(_layer_params_first(l, rep, gathered["w_in"], conv_w, b_merge), **_layer_params_rest(gathered))


def _forward_backward(x, mem, pos, target, params, bwd_hooks=None):
    tabs = _rope_tables(pos)
    params = list(params)
    saved = []
    act = x
    for l in range(DEPTH):
        if callable(params[l]):
            params[l] = params[l](saved[-1], act)
        act, sv = _layer_fwd(act, mem, tabs, params[l])
        saved.append(sv)
    dy, sq = _loss_call(act, target)
    grads = [None] * DEPTH
    token = None
    for l in reversed(range(DEPTH)):
        hooks = dict(bwd_hooks[l]) if bwd_hooks else {}
        after_layer = hooks.pop("after_layer", None)
        dy, grads[l] = _layer_bwd(dy, mem, tabs, saved[l]["p"], saved[l], start_after=token, **hooks)
        token = after_layer(dy) if after_layer is not None else None
    return sq, dy, grads


_SHARDED_MM = ("w_in", "w_branch", "w_out", "w_mem_kv", "w_uq", "w_ukv")
_SHARDED_F32 = ("conv_w", "b_merge")
_REPLICATED = ("norm_g", "cq_norm_g", "ckv_norm_g", "mla_q_norm_g", "mla_k_norm_g", "conv_b", "sg_ln_g", "sg_ln_b",
               "w_spatial", "b_spatial", "mem_norm_g", "mem_q_norm_g", "mem_k_norm_g")
_ALL_REDUCED = _REPLICATED + _SHARDED_F32
_WEIGHTS = ("norm_g", "w_in", "cq_norm_g", "ckv_norm_g", "w_uq", "w_ukv", "mla_q_norm_g", "mla_k_norm_g", "conv_w", "conv_b",
            "sg_ln_g", "sg_ln_b", "w_spatial", "b_spatial", "mem_norm_g", "w_mem_kv", "mem_q_norm_g", "mem_k_norm_g",
            "b_merge", "w_branch", "w_out")
_SMALL = tuple(n for n in _WEIGHTS if n not in _SHARDED_MM)


class _SmallGather:
    def __init__(self, blk, after, tag):
        self.blk, self.tag = blk, tag
        x, y, c = lax.axis_index("x"), lax.axis_index("y"), lax.axis_index("c")
        own = _place_block_call(blk, (4 * x + 2 * y + c).astype(jnp.int32).reshape(1), tag + "place_own")
        self.buf, self.send, self.recv, self.token = _small_gather_start_call(blk, own, after, tag + "start")

    def finish(self, after):
        return _small_gather_finish_call(self.blk, self.buf, self.send, self.recv, after, self.tag + "finish")


def _small_sharded_weights(w, got):
    names = _SHARDED_F32
    per_chip = [_unpack(got[2 * j].reshape(-1), [w[n].shape for n in names]) for j in range(N_CHIPS)]
    return {n: jnp.concatenate([per_chip[j][t] for j in range(N_CHIPS)], axis=2) for t, n in enumerate(names)}


class _Gather:
    def __init__(self, w, layer, names, after, tag):
        self.names, self.tag = names, tag
        x, y, c = lax.axis_index("x"), lax.axis_index("y"), lax.axis_index("c")
        chip_core = jnp.stack([2 * x + y, c]).astype(jnp.int32)
        halves = [w[n].shape[1] // 2 for n in names]
        self.srcs = [lax.dynamic_slice_in_dim(w[n][layer], c * h, h, axis=0).astype(MM) for n, h in zip(names, halves)]
        k = len(names)
        out = _gather_start_call(self.srcs, _place_own_call(self.srcs, chip_core, tag + "place_own"), after, tag + "start")
        self.bufs, self.send, self.recv_sib, self.recv_ici, self.token = out[:k], out[k], out[k + 1], out[k + 2], out[k + 3]

    def pass_on(self, after):
        k = len(self.names)
        out = _gather_forward_call(self.bufs, self.recv_ici, after, self.tag + "forward")
        self.bufs, self.send_fwd, self.recv_fwd = out[:k], out[k], out[k + 1]
        return out[k + 2]

    def finish(self, after):
        got = _gather_finish_call(self.srcs, self.bufs, self.send, self.recv_sib, self.send_fwd, self.recv_fwd, after,
                                  self.tag + "finish")
        return dict(zip(self.names, got))


class _ReduceScatter:
    SLABS = dict(
        w_in=lambda g: _unalign_call(g["w_in_aligned"], MM),
        w_branch=lambda g: g["w_branch_chips"].reshape(N_CHIPS, NB * BW, D // N_CHIPS),
        w_out=lambda g: g["w_out"].reshape(N_CHIPS, D // N_CHIPS, D),
        w_mem_kv=lambda g: g["w_mem_kv"].reshape(N_CHIPS, D // N_CHIPS, 2 * MH * MHD),
        w_uq=lambda g: _cols_to_chips(_wuq_from_heads(g["wuq_heads"])),
        w_ukv=lambda g: _cols_to_chips(_wukv_from_heads(g["wkn_heads"], g["wv_heads"])))

    def __init__(self, tag, names):
        self.tag, self.names = tag, names

    def exchange(self, grads):
        self.tensors = [self.SLABS[n](grads) for n in self.names]
        n = len(self.tensors)
        out = _pair_exchange_start_call(self.tensors, self.tag + "exchange_start")
        self.ex_bufs, self.ex_send, self.ex_recv = out[:n], out[n], out[n + 1]
        return out[n + 2]

    def scatter(self, after):
        n = len(self.tensors)
        c = lax.axis_index("c")
        from_sibling = _pair_exchange_finish_call(self.tensors, self.ex_bufs, self.ex_send, self.ex_recv, after,
                                                  self.tag + "exchange_finish")
        self.chip_sums = _pair_sum_call(self.tensors, from_sibling, c.astype(jnp.int32).reshape(1), self.tag + "pair_sum")
        out = _chip_scatter_start_call(self.chip_sums, self.tag + "scatter_start")
        self.bufs, self.send_sems, self.recv_sems, self.token = out[:n], out[n], out[n + 1], out[n + 2]
        return self.token

    def finish(self, after):
        x, y, c = lax.axis_index("x"), lax.axis_index("y"), lax.axis_index("c")
        chip_core = jnp.stack([2 * x + y, c]).astype(jnp.int32)
        from_chips = _chip_scatter_finish_call(self.chip_sums, self.bufs, self.send_sems, self.recv_sems, after,
                                               self.tag + "scatter_finish")
        mine = _owner_sum_call(self.chip_sums, from_chips, chip_core, self.tag + "owner_sum")
        return dict(zip(self.names, _pair_gather_call(mine, self.tag + "pair_gather")))


def _small_sums(shapes, sq, got):
    total = _sum8_call(got).reshape(-1)
    parts = _unpack(total, [shapes[n] for n in _ALL_REDUCED] + [sq.shape])
    out = dict(zip(_ALL_REDUCED, parts))
    sq_total = parts[-1]
    chip = 2 * lax.axis_index("x") + lax.axis_index("y")
    for n in _SHARDED_F32:
        size = out[n].shape[2] // N_CHIPS
        out[n] = lax.dynamic_slice_in_dim(out[n], chip * size, size, axis=2)
    return out, sq_total


def _adamw_small(w, g, m, v):
    pick = lambda t: [t[n] for n in _SMALL]
    out = _adamw_small_call(pick(w), pick(g), pick(m), pick(v), "adamw_small")
    return tuple({n: out[3 * t + k] for t, n in enumerate(_SMALL)} for k in range(3))


def kernel(x, mem, positions, norm_g, w_in, cq_norm_g, ckv_norm_g, w_uq, w_ukv, mla_q_norm_g, mla_k_norm_g, conv_w, conv_b, sg_ln_g, sg_ln_b, w_spatial, b_spatial, mem_norm_g, w_mem_kv, mem_q_norm_g, mem_k_norm_g, b_merge, w_branch, w_out, loss_target, m_norm_g, m_w_in, m_cq_norm_g, m_ckv_norm_g, m_w_uq, m_w_ukv, m_mla_q_norm_g, m_mla_k_norm_g, m_conv_w, m_conv_b, m_sg_ln_g, m_sg_ln_b, m_w_spatial, m_b_spatial, m_mem_norm_g, m_w_mem_kv, m_mem_q_norm_g, m_mem_k_norm_g, m_b_merge, m_w_branch, m_w_out, v_norm_g, v_w_in, v_cq_norm_g, v_ckv_norm_g, v_w_uq, v_w_ukv, v_mla_q_norm_g, v_mla_k_norm_g, v_conv_w, v_conv_b, v_sg_ln_g, v_sg_ln_b, v_w_spatial, v_b_spatial, v_mem_norm_g, v_w_mem_kv, v_mem_q_norm_g, v_mem_k_norm_g, v_b_merge, v_w_branch, v_w_out):
    w = dict(norm_g=norm_g, w_in=w_in, cq_norm_g=cq_norm_g, ckv_norm_g=ckv_norm_g, w_uq=w_uq, w_ukv=w_ukv,
             mla_q_norm_g=mla_q_norm_g, mla_k_norm_g=mla_k_norm_g, conv_w=conv_w, conv_b=conv_b, sg_ln_g=sg_ln_g,
             sg_ln_b=sg_ln_b, w_spatial=w_spatial, b_spatial=b_spatial, mem_norm_g=mem_norm_g, w_mem_kv=w_mem_kv,
             mem_q_norm_g=mem_q_norm_g, mem_k_norm_g=mem_k_norm_g, b_merge=b_merge, w_branch=w_branch, w_out=w_out)
    m = dict(norm_g=m_norm_g, w_in=m_w_in, cq_norm_g=m_cq_norm_g, ckv_norm_g=m_ckv_norm_g, w_uq=m_w_uq, w_ukv=m_w_ukv,
             mla_q_norm_g=m_mla_q_norm_g, mla_k_norm_g=m_mla_k_norm_g, conv_w=m_conv_w, conv_b=m_conv_b, sg_ln_g=m_sg_ln_g,
             sg_ln_b=m_sg_ln_b, w_spatial=m_w_spatial, b_spatial=m_b_spatial, mem_norm_g=m_mem_norm_g, w_mem_kv=m_w_mem_kv,
             mem_q_norm_g=m_mem_q_norm_g, mem_k_norm_g=m_mem_k_norm_g, b_merge=m_b_merge, w_branch=m_w_branch, w_out=m_w_out)
    v = dict(norm_g=v_norm_g, w_in=v_w_in, cq_norm_g=v_cq_norm_g, ckv_norm_g=v_ckv_norm_g, w_uq=v_w_uq, w_ukv=v_w_ukv,
             mla_q_norm_g=v_mla_q_norm_g, mla_k_norm_g=v_mla_k_norm_g, conv_w=v_conv_w, conv_b=v_conv_b, sg_ln_g=v_sg_ln_g,
             sg_ln_b=v_sg_ln_b, w_spatial=v_w_spatial, b_spatial=v_b_spatial, mem_norm_g=v_mem_norm_g, w_mem_kv=v_w_mem_kv,
             mem_q_norm_g=v_mem_q_norm_g, mem_k_norm_g=v_mem_k_norm_g, b_merge=v_b_merge, w_branch=v_w_branch, w_out=v_w_out)

    chip_core = jnp.stack([2 * lax.axis_index("x") + lax.axis_index("y"), lax.axis_index("c")]).astype(jnp.int32)

    first = _Gather(w, 0, ("w_in",), chip_core, "gather_l0_w_in_")
    rest = _Gather(w, 0, _SHARDED_MM[1:], first.token, "gather_l0_rest_")
    small_on_its_way = _SmallGather(_pack_rows([w[n] for n in _SHARDED_F32], F32, 8), [rest.token], "gather_small_weights_")
    later = _Gather(w, 1, _SHARDED_MM, small_on_its_way.token, "gather_l1_")
    w_in0 = first.finish(first.pass_on(later.token))["w_in"]
    small = {}

    def rest_of_layer0(proj0):
        landed = _layer_params_rest(rest.finish(rest.pass_on(proj0)))
        small.update(_small_sharded_weights(w, small_on_its_way.finish([landed["wo"]])))
        return dict(landed, conv_w=small["conv_w"][0], bm=small["b_merge"][0])

    def layer1_params(saved0, act0):
        return _layer_params(1, w, later.finish(act0), small["conv_w"][1], small["b_merge"][1])

    params0 = _layer_params_first(0, w, w_in0, None, None)
    params = [dict(params0, late=rest_of_layer0, after_attn=later.pass_on), layer1_params]
    others = _SHARDED_MM[1:]
    rs1 = _ReduceScatter("rs_l1_", _SHARDED_MM)
    rs0_rest, rs0_w_in = _ReduceScatter("rs_l0_rest_", others), _ReduceScatter("rs_l0_w_in_", ("w_in",))

    def layer0_grads_done(grads):
        return [rs0_rest.scatter([grads["w_in_aligned"]]), rs0_w_in.exchange(grads)]

    hooks = [dict(on_rest_grads=rs0_rest.exchange, on_grads=layer0_grads_done),
             dict(on_grads=lambda grads: [rs1.exchange(grads)], after_layer=lambda dy: rs1.scatter([dy]))]
    sq, grad_x, layer_grads = _forward_backward(x[0], mem[0], positions[0], loss_target[0], params, hooks)

    layered = [layer_grads[l][n] for n in _ALL_REDUCED for l in range(DEPTH)]
    small_grads = _SmallGather(_pack_rows(layered + [sq], F32, 64), [grad_x], "gather_small_grads_")
    scattering = rs0_w_in.scatter([grad_x, small_grads.token])
    shard_grads = {1: rs1.finish([scattering]), 0: rs0_rest.finish([scattering])}
    as3d = lambda a: a.reshape(DEPTH, -1, a.shape[-1])
    as2d = lambda a: a.reshape(-1, a.shape[-1])
    big = lambda t: [as3d(t[n]) for n in others]
    turned = lambda t: [jnp.swapaxes(t["w_in"], 1, 2)]
    assert W_IN_SHARD % (8 * 7) == 0

    def update_w_in(l, grad, prev):
        return _adamw_layer_call(l, turned(w), [grad.T], turned(m), turned(v), prev, [], "adamw_w_in_l%d" % l, steps=7)

    def update_others(l, prev):
        return _adamw_layer_call(l, big(w), [as2d(shard_grads[l][n]) for n in others], big(m), big(v), prev, [], "adamw_l%d" % l)

    upd = update_others(0, update_others(1, None))
    full_shapes = {n: w[n].shape for n in _REPLICATED}
    full_shapes.update(conv_w=(DEPTH, 3, CW), b_merge=(DEPTH, NB, D))
    g, sq_total = _small_sums(full_shapes, sq, small_grads.finish([upd[0]]))
    loss = 0.5 / D * jnp.sum(sq_total)
    delta, new_m, new_v = _adamw_small(w, g, m, v)
    upd_in1 = update_w_in(1, shard_grads[1]["w_in"], None)
    w_in_grad0 = rs0_w_in.finish([grad_x, upd_in1[0], upd[0], delta["norm_g"]])["w_in"]
    upd_in = update_w_in(0, w_in_grad0, upd_in1)
    g["w_in"], delta["w_in"], new_m["w_in"], new_v["w_in"] = [jnp.swapaxes(a, 1, 2) for a in upd_in]
    for t, n in enumerate(others):
        g[n], delta[n], new_m[n], new_v[n] = [a.reshape(w[n].shape) for a in upd[4 * t:4 * t + 4]]
    return (loss, grad_x[None], *[g[n] for n in _WEIGHTS], *[delta[n] for n in _WEIGHTS],
            *[new_m[n] for n in _WEIGHTS], *[new_v[n] for n in _WEIGHTS])
```

```python
import functools
import math

import jax
import jax.numpy as jnp
from jax import lax
from jax.experimental import pallas as pl
from jax.experimental.pallas import tpu as pltpu

F32 = jnp.float32
MM = jnp.bfloat16

D = 1024
DEPTH = 2
EPS = 1e-6
H = 8
NOPE = 64
ROPE = 32
QKH = 96
VH = 64
QL = 256
KVL = 128
ROPE_THETA = 10000.0
CW = 512
SGW = 512
SGG = 4
SGC = 128
MH = 4
MHD = 128
NB = 4
BW = 512
NEG_INF = -1e30
LANES = 128
N_CHIPS = 4

R_CQ, R_CKV, R_KR, R_CV, R_SGI, R_MQ, R_SG, R_ML, R_END = 0, 256, 384, 416, 1952, 2976, 3488, 5536, 9632
OFF_ML, OFF_SG, OFF_CV, OFF_SGI, OFF_MQ, OFF_CQ, OFF_CKV, OFF_KR, NP = 0, 4096, 6144, 7680, 8704, 9216, 9472, 9600, 9728

ADAM_LR = 0.001
ADAM_B1 = 0.9
ADAM_B2 = 0.999
ADAM_EPS = 1e-08
ADAM_WD = 0.01
ADAM_STEP = 10

VMEM_LIMIT = 56 * 1024 * 1024
PACK_W = 512
MESH_ID = pl.DeviceIdType.MESH


def _cparams(n_axes):
    return pltpu.CompilerParams(dimension_semantics=("arbitrary",) * n_axes, vmem_limit_bytes=VMEM_LIMIT)


def _bs(shape, imap):
    return pl.BlockSpec(shape, imap)


@jax.custom_vjp
def _mm_plain(a, b):
    return jnp.dot(a.astype(MM), b.astype(MM), preferred_element_type=F32)


def _mm_plain_fwd(a, b):
    return _mm_plain(a, b), (a, b)


def _mm_plain_bwd(res, g):
    a, b = res
    gm = g.astype(MM)
    da = lax.dot_general(gm, b.astype(MM), (((1,), (1,)), ((), ())), preferred_element_type=F32)
    db = lax.dot_general(a.astype(MM), gm, (((0,), (0,)), ((), ())), preferred_element_type=F32)
    return da.astype(a.dtype), db.astype(b.dtype)


_mm_plain.defvjp(_mm_plain_fwd, _mm_plain_bwd)


@jax.custom_vjp
def _mm_slot(a, w, slot):
    return jnp.dot(a.astype(MM), w.astype(MM), preferred_element_type=F32)


def _mm_slot_fwd(a, w, slot):
    return _mm_slot(a, w, slot), (a, w)


def _mm_slot_bwd(res, g):
    a, w = res
    gm = g.astype(MM)
    da = lax.dot_general(gm, w.astype(MM), (((1,), (1,)), ((), ())), preferred_element_type=F32)
    dw = lax.dot_general(a.astype(MM), gm, (((0,), (0,)), ((), ())), preferred_element_type=F32)
    return da.astype(a.dtype), jnp.zeros_like(w), dw


_mm_slot.defvjp(_mm_slot_fwd, _mm_slot_bwd)


def _mm(a, b):
    if isinstance(b, tuple):
        return _mm_slot(a, b[0], b[1])
    return _mm_plain(a, b)


def _with_slot(w):
    return (w, jnp.zeros(w.shape, F32))


@jax.custom_vjp
def _mm_nt(a, b):
    return lax.dot_general(a.astype(MM), b.astype(MM), (((1,), (1,)), ((), ())), preferred_element_type=F32)


def _mm_nt_fwd(a, b):
    return _mm_nt(a, b), (a, b)


def _mm_nt_bwd(res, g):
    a, b = res
    gm = g.astype(MM)
    da = jnp.dot(gm, b.astype(MM), preferred_element_type=F32)
    db = lax.dot_general(gm, a.astype(MM), (((0,), (0,)), ((), ())), preferred_element_type=F32)
    return da.astype(a.dtype), db.astype(b.dtype)


_mm_nt.defvjp(_mm_nt_fwd, _mm_nt_bwd)


@functools.partial(jax.custom_vjp, nondiff_argnums=(1,))
def _lane_roll(x, shift):
    return pltpu.roll(x, shift, 1)


def _lane_roll_fwd(x, shift):
    return pltpu.roll(x, shift, 1), None


def _lane_roll_bwd(shift, _, g):
    return (pltpu.roll(g, (LANES - shift) % LANES, 1),)


_lane_roll.defvjp(_lane_roll_fwd, _lane_roll_bwd)


def _rms_n(x, g, n):
    ms = jnp.sum(x * x, axis=-1, keepdims=True) * (1.0 / n)
    return x * lax.rsqrt(ms + EPS) * g


def _softmax(s):
    m = jnp.max(s, axis=-1, keepdims=True)
    e = jnp.exp(s - m)
    return e / jnp.sum(e, axis=-1, keepdims=True)


def _rope(t, cos_t, sin_a, sin_b):
    return t * cos_t + _lane_roll(t, LANES - 16) * sin_a + _lane_roll(t, 16) * sin_b


def _mla_prep_fn(cq, ckv, kr, cos_t, sin_a, sin_b, cq_g, ckv_g, qg, kg, wuq, wkn, wv):
    cqn = _rms_n(cq, cq_g, QL)
    ckvn = _rms_n(ckv, ckv_g, KVL)
    lane = lax.broadcasted_iota(jnp.int32, kr.shape, 1)
    krm = jnp.where((lane >= NOPE) & (lane < QKH), kr, 0.0)
    qs, ks = [], []
    for h in range(H):
        qh = _rms_n(_mm(cqn, wuq[h]), qg, QKH)
        qs.append(_rope(qh, cos_t, sin_a, sin_b) * (QKH ** -0.5))
        kh = _rms_n(_mm(ckvn, wkn[h]) + krm, kg, QKH)
        ks.append(_rope(kh, cos_t, sin_a, sin_b))
    return jnp.concatenate(qs, axis=-1), jnp.concatenate(ks, axis=-1), _mm(ckvn, wv)


def _dot_nt(a, b):
    return lax.dot_general(a.astype(MM), b.astype(MM), (((1,), (1,)), ((), ())), preferred_element_type=F32)


def _dot_tn(a, b):
    return lax.dot_general(a.astype(MM), b.astype(MM), (((0,), (0,)), ((), ())), preferred_element_type=F32)


def _causal_scores(qe, ke):
    tq, kl = qe.shape[0], ke.shape[0]
    s = _dot_nt(qe, ke)
    rows = lax.broadcasted_iota(jnp.int32, (tq, tq), 0)
    cols = lax.broadcasted_iota(jnp.int32, (tq, tq), 1)
    own = jnp.where(cols <= rows, s[:, kl - tq:], NEG_INF)
    return own if kl == tq else jnp.concatenate([s[:, :kl - tq], own], axis=1)


def _head_lanes(e, shape):
    lane = lax.broadcasted_iota(jnp.int32, shape, len(shape) - 1)
    return (lane >= VH * e) & (lane < VH * (e + 1))


def _attn_pair_fwd(q2, k2, v2):
    tq = q2.shape[0]
    o = jnp.zeros((tq, LANES), F32)
    lse = jnp.zeros((tq, LANES), F32)
    for e in range(2):
        sl = slice(LANES * e, LANES * (e + 1))
        s = _causal_scores(q2[:, sl], k2[:, sl])
        m = jnp.max(s, axis=-1, keepdims=True)
        ex = jnp.exp(s - m)
        l = jnp.sum(ex, axis=-1, keepdims=True)
        ve = jnp.where(_head_lanes(e, v2[:, sl].shape), v2[:, sl], 0.0)
        o = o + jnp.dot(ex.astype(MM), ve.astype(MM), preferred_element_type=F32) * (1.0 / l)
        lse = jnp.where(_head_lanes(e, lse.shape), m + jnp.log(l), lse)
    return o, lse


def _attn_pair_bwd(q2, k2, v2, sg, dys, o, lse):
    sig = jax.nn.sigmoid(sg)
    do = dys * (sg * sig)
    dsg = dys * o * (sig * (1.0 + sg * (1.0 - sig)))
    dqs, dks, dvs = [], [], []
    for e in range(2):
        sl = slice(LANES * e, LANES * (e + 1))
        qe, ke = q2[:, sl], k2[:, sl]
        hm = _head_lanes(e, o.shape)
        lse_e = jnp.max(jnp.where(hm, lse, NEG_INF), axis=-1, keepdims=True)
        do_e = jnp.where(hm, do, 0.0)
        delta = jnp.sum(do_e * o, axis=-1, keepdims=True)
        p = jnp.exp(_causal_scores(qe, ke) - lse_e)
        ve = jnp.where(_head_lanes(e, v2[:, sl].shape), v2[:, sl], 0.0)
        dvs.append(_dot_tn(p, do_e))
        ds = p * (_dot_nt(do_e, ve) - delta)
        dqs.append(jnp.dot(ds.astype(MM), ke.astype(MM), preferred_element_type=F32))
        dks.append(_dot_tn(ds, qe))
    return jnp.concatenate(dqs, axis=-1), jnp.concatenate(dks, axis=-1), jnp.concatenate(dvs, axis=-1), dsg


def _sg_fn(u, v, sgc, ln_g, ln_b, ws, bs):
    mu = jnp.mean(v, axis=-1, keepdims=True)
    xc = v - mu
    vn = xc * lax.rsqrt(jnp.mean(xc * xc, axis=-1, keepdims=True) + EPS) * ln_g + ln_b
    r = lax.broadcasted_iota(jnp.int32, (SGC, SGC), 0)
    c = lax.broadcasted_iota(jnp.int32, (SGC, SGC), 1)
    wt = [jnp.where(r >= c, w, 0.0) for w in ws]
    row_blocks = []
    for ch in range(u.shape[0] // SGC):
        col_blocks = []
        for g in range(SGG):
            blk = vn[SGC * ch:SGC * (ch + 1), LANES * g:LANES * (g + 1)]
            col_blocks.append(_mm(wt[g], blk) + bs[g])
        row_blocks.append(jnp.concatenate(col_blocks, axis=-1))
    mixed = jnp.concatenate(row_blocks, axis=0)
    return (u * mixed) * jax.nn.silu(sgc)


def _memkv_fn(mem, mem_g, wm, kg):
    kv = _mm(_rms_n(mem, mem_g, D), wm)
    ks = [_rms_n(kv[:, MHD * h:MHD * (h + 1)], kg, MHD) for h in range(MH)]
    return jnp.concatenate(ks, axis=-1), kv[:, MH * MHD:]


def _mem_fn(mq, sgd, k, v, qg):
    outs = []
    for h in range(MH):
        sl = slice(MHD * h, MHD * (h + 1))
        qh = _rms_n(mq[:, sl], qg, MHD)
        p = _softmax(_mm_nt(qh, k[:, sl]) * (MHD ** -0.5))
        outs.append(_mm(p, v[:, sl]))
    return jnp.concatenate(outs, axis=-1) * jax.nn.silu(sgd)


def _merge_fn(ys, logits, bm, wb, wo):
    merged = None
    for n in range(NB):
        z = jnp.concatenate([_mm(ys[n], wb[j][n]) for j in range(N_CHIPS)], axis=-1)
        gate = jax.nn.sigmoid(logits[:, D * n:D * (n + 1)] + bm[n])
        merged = gate * z if merged is None else merged + gate * z
    return _mm(merged, wo)


def _proj_call(x, g, w):
    s_len = x.shape[0]
    tm, tn = s_len, 512

    def body(x_ref, g_ref, w_ref, p_ref, h_ref):
        @pl.when(pl.program_id(1) == 0)
        def _():
            h_ref[...] = _rms_n(x_ref[...], g_ref[...], D).astype(h_ref.dtype)
        p_ref[...] = jnp.dot(h_ref[...], w_ref[...], preferred_element_type=F32)

    return pl.pallas_call(
        body, grid=(s_len // tm, NP // tn),
        in_specs=[_bs((tm, D), lambda i, j: (i, 0)), _bs((1, D), lambda i, j: (0, 0)), _bs((D, tn), lambda i, j: (0, j))],
        out_specs=[_bs((tm, tn), lambda i, j: (i, j)), _bs((tm, D), lambda i, j: (i, 0))],
        out_shape=[jax.ShapeDtypeStruct((s_len, NP), F32), jax.ShapeDtypeStruct((s_len, D), MM)],
        name="proj", compiler_params=_cparams(2))(x, g, w)


def _rope_tables(pos):
    half = ROPE // 2
    inv_freq = ROPE_THETA ** (-jnp.arange(half, dtype=F32) / half)
    ang = pos.astype(F32)[:, None] * inv_freq
    cos, sin = jnp.cos(ang), jnp.sin(ang)
    s_len = pos.shape[0]
    z = lambda n: jnp.zeros((s_len, n), F32)
    cos_t = jnp.concatenate([jnp.ones((s_len, NOPE), F32), cos, cos, z(LANES - QKH)], axis=1)
    sin_a = jnp.concatenate([z(NOPE), -sin, z(LANES - NOPE - half)], axis=1)
    sin_b = jnp.concatenate([z(NOPE + half), sin, z(LANES - QKH)], axis=1)
    return cos_t, sin_a, sin_b


def _mla_prep_specs(tm):
    row = lambda w, off: _bs((tm, w), lambda i: (i, off // w))
    full2 = lambda a, b: _bs((a, b), lambda i: (0, 0))
    full3 = lambda a, b, c: _bs((a, b, c), lambda i: (0, 0, 0))
    tab = _bs((tm, LANES), lambda i: (i, 0))
    return [row(QL, OFF_CQ), row(KVL, OFF_CKV), row(LANES, OFF_KR), tab, tab, tab,
            full2(1, QL), full2(1, KVL), full2(1, LANES), full2(1, LANES),
            full3(H, QL, LANES), full3(H, KVL, LANES), full2(KVL, H * LANES)]


def _mla_prep_args(body_refs, wrap=lambda w: w):
    (cq, ckv, kr, ct, sa, sb, cqg, ckvg, qg, kg, wuq, wkn, wv) = body_refs
    return (cq[...], ckv[...], kr[...], ct[...], sa[...], sb[...], cqg[...], ckvg[...], qg[...], kg[...],
            [wrap(wuq[h]) for h in range(H)], [wrap(wkn[h]) for h in range(H)], wrap(wv[...]))


def _mla_prep_call(proj, tabs, cq_g, ckv_g, qg, kg, wuq, wkn, wv):
    s_len = proj.shape[0]
    tm = min(s_len, 256)

    def body(*refs):
        q_ref, k_ref, v_ref = refs[13:]
        q, k, v = _mla_prep_fn(*_mla_prep_args(refs[:13]))
        q_ref[...] = q.astype(q_ref.dtype)
        k_ref[...] = k.astype(k_ref.dtype)
        v_ref[...] = v.astype(v_ref.dtype)

    out = _bs((tm, H * LANES), lambda i: (i, 0))
    return pl.pallas_call(
        body, grid=(s_len // tm,), in_specs=_mla_prep_specs(tm), out_specs=[out, out, out],
        out_shape=[jax.ShapeDtypeStruct((s_len, H * LANES), MM)] * 3,
        name="mla_prep", compiler_params=_cparams(1))(proj, proj, proj, *tabs, cq_g, ckv_g, qg, kg, wuq, wkn, wv)


def _mla_prep_bwd_call(proj, tabs, cq_g, ckv_g, qg, kg, wuq, wkn, wv, dq, dk, dv):
    s_len = proj.shape[0]
    tm = min(s_len, 256)

    def body(*refs):
        dq_ref, dk_ref, dv_ref = refs[13:16]
        dlat_ref, dcqg_ref, dckvg_ref, dqg_ref, dkg_ref, dwuq_ref, dwkn_ref, dwv_ref = refs[16:]
        _, vjp = jax.vjp(_mla_prep_fn, *_mla_prep_args(refs[:13], _with_slot))
        (dcq, dckv, dkr, _, _, _, dcqg, dckvg, dqg, dkg, dwuq, dwkn, dwv) = vjp((dq_ref[...], dk_ref[...], dv_ref[...]))
        dwuq, dwkn, dwv = [d[1] for d in dwuq], [d[1] for d in dwkn], dwv[1]
        dlat_ref[...] = jnp.concatenate([dcq, dckv, dkr], axis=-1).astype(dlat_ref.dtype)

        @pl.when(pl.program_id(0) == 0)
        def _():
            for r in (dcqg_ref, dckvg_ref, dqg_ref, dkg_ref, dwuq_ref, dwkn_ref, dwv_ref):
                r[...] = jnp.zeros_like(r)
        dcqg_ref[...] += dcqg
        dckvg_ref[...] += dckvg
        dqg_ref[...] += dqg
        dkg_ref[...] += dkg
        for h in range(H):
            dwuq_ref[h] += dwuq[h]
            dwkn_ref[h] += dwkn[h]
        dwv_ref[...] += dwv

    big = _bs((tm, H * LANES), lambda i: (i, 0))
    row = lambda w: _bs((tm, w), lambda i: (i, 0))
    full2 = lambda a, b: _bs((a, b), lambda i: (0, 0))
    full3 = lambda a, b, c: _bs((a, b, c), lambda i: (0, 0, 0))
    sd = jax.ShapeDtypeStruct
    return pl.pallas_call(
        body, grid=(s_len // tm,), in_specs=_mla_prep_specs(tm) + [big, big, big],
        out_specs=[row(QL + KVL + LANES), full2(1, QL), full2(1, KVL), full2(1, LANES), full2(1, LANES),
                   full3(H, QL, LANES), full3(H, KVL, LANES), full2(KVL, H * LANES)],
        out_shape=[sd((s_len, QL + KVL + LANES), MM), sd((1, QL), F32), sd((1, KVL), F32),
                   sd((1, LANES), F32), sd((1, LANES), F32), sd((H, QL, LANES), F32), sd((H, KVL, LANES), F32),
                   sd((KVL, H * LANES), F32)],
        name="mla_prep_bwd", compiler_params=_cparams(1))(proj, proj, proj, *tabs, cq_g, ckv_g, qg, kg, wuq, wkn, wv, dq, dk, dv)


def _attn_specs(s_len, tq):
    pair = 2 * LANES
    return [_bs((tq, pair), lambda p, i: (i, p)), _bs((s_len, pair), lambda p, i: (0, p)), _bs((s_len, pair), lambda p, i: (0, p)),
            _bs((tq, LANES), lambda p, i: (i, OFF_SG // LANES + p))]


def _attn_call(q, k, v, proj):
    s_len = q.shape[0]
    tq = min(s_len, 256)

    def body(q_ref, k_ref, v_ref, sg_ref, y_ref, o_ref, lse_ref):
        for n in range(s_len // tq):
            @pl.when(pl.program_id(1) == n)
            def _():
                kl = (n + 1) * tq
                o, lse = _attn_pair_fwd(q_ref[...], k_ref[:kl, :], v_ref[:kl, :])
                y_ref[...] = (o * jax.nn.silu(sg_ref[...])).astype(y_ref.dtype)
                o_ref[...] = o
                lse_ref[...] = lse

    tile = _bs((tq, LANES), lambda p, i: (i, p))
    sd = jax.ShapeDtypeStruct
    return pl.pallas_call(
        body, grid=(H // 2, s_len // tq), in_specs=_attn_specs(s_len, tq), out_specs=[tile, tile, tile],
        out_shape=[sd((s_len, BW), MM), sd((s_len, BW), F32), sd((s_len, BW), F32)],
        name="attn", compiler_params=_cparams(2))(q, k, v, proj)


def _attn_bwd_call(q, k, v, proj, dys, o, lse):
    s_len = q.shape[0]
    tq = min(s_len, 256)
    pair = 2 * LANES

    def body(q_ref, k_ref, v_ref, sg_ref, dy_ref, o_ref, lse_ref, dq_ref, dk_ref, dv_ref, dsg_ref):
        i = pl.program_id(1)

        @pl.when(i == 0)
        def _():
            dk_ref[...] = jnp.zeros_like(dk_ref)
            dv_ref[...] = jnp.zeros_like(dv_ref)

        for n in range(s_len // tq):
            @pl.when(i == n)
            def _():
                kl = (n + 1) * tq
                dq, dk, dv, dsg = _attn_pair_bwd(q_ref[...], k_ref[:kl, :], v_ref[:kl, :], sg_ref[...], dy_ref[...],
                                                 o_ref[...], lse_ref[...])
                dq_ref[...] = dq
                dsg_ref[...] = dsg.astype(dsg_ref.dtype)
                dk_ref[:kl, :] += dk
                dv_ref[:kl, :] += dv

    sd = jax.ShapeDtypeStruct
    tile = _bs((tq, LANES), lambda p, i: (i, p))
    return pl.pallas_call(
        body, grid=(H // 2, s_len // tq),
        in_specs=_attn_specs(s_len, tq) + [tile, tile, tile],
        out_specs=[_bs((tq, pair), lambda p, i: (i, p)), _bs((s_len, pair), lambda p, i: (0, p)),
                   _bs((s_len, pair), lambda p, i: (0, p)), tile],
        out_shape=[sd((s_len, H * LANES), F32), sd((s_len, H * LANES), F32), sd((s_len, H * LANES), F32), sd((s_len, BW), MM)],
        name="attn_bwd", compiler_params=_cparams(2))(q, k, v, proj, dys, o, lse)


def _shift_down(a, n):
    r = lax.broadcasted_iota(jnp.int32, a.shape, 0)
    return jnp.where(r >= n, pltpu.roll(a, n, 0), 0.0)


def _shift_up(a, n):
    s_len = a.shape[0]
    r = lax.broadcasted_iota(jnp.int32, a.shape, 0)
    return jnp.where(r < s_len - n, pltpu.roll(a, s_len - n, 0), 0.0)


def _conv_specs(s_len):
    col = lambda off: _bs((s_len, LANES), lambda j: (0, off // LANES + j))
    return [col(OFF_CV), col(OFF_CV + CW), col(OFF_CV + 2 * CW), col(OFF_SG + BW),
            _bs((3, LANES), lambda j: (0, j)), _bs((1, LANES), lambda j: (0, j))]


def _conv_call(proj, cw, cb):
    s_len = proj.shape[0]

    def body(bg_ref, cg_ref, xi_ref, sg_ref, w_ref, b_ref, y_ref):
        z = cg_ref[...] * xi_ref[...]
        y = b_ref[...] + w_ref[0:1, :] * _shift_down(z, 2)
        y = y + w_ref[1:2, :] * _shift_down(z, 1)
        y = y + w_ref[2:3, :] * z
        y_ref[...] = ((bg_ref[...] * y) * jax.nn.silu(sg_ref[...])).astype(y_ref.dtype)

    return pl.pallas_call(
        body, grid=(CW // LANES,), in_specs=_conv_specs(s_len), out_specs=_bs((s_len, LANES), lambda j: (0, j)),
        out_shape=jax.ShapeDtypeStruct((s_len, CW), MM), name="conv", compiler_params=_cparams(1))(proj, proj, proj, proj, cw, cb)


def _conv_bwd_call(proj, cw, cb, dys):
    s_len = proj.shape[0]

    def body(bg_ref, cg_ref, xi_ref, sg_ref, w_ref, b_ref, dys_ref, dbg_ref, dcg_ref, dxi_ref, dsg_ref, dw_ref, db_ref):
        bg, cg, xi, sg = bg_ref[...], cg_ref[...], xi_ref[...], sg_ref[...]
        w0, w1, w2 = w_ref[0:1, :], w_ref[1:2, :], w_ref[2:3, :]
        z = cg * xi
        z1, z2 = _shift_down(z, 1), _shift_down(z, 2)
        y = b_ref[...] + w0 * z2
        y = y + w1 * z1
        y = y + w2 * z
        yb = bg * y
        sig = jax.nn.sigmoid(sg)
        silu = sg * sig
        dys_v = dys_ref[...]
        dsg_ref[...] = (dys_v * yb * (sig * (1.0 + sg * (1.0 - sig)))).astype(dsg_ref.dtype)
        dyb = dys_v * silu
        dbg_ref[...] = (dyb * y).astype(dbg_ref.dtype)
        dy = dyb * bg
        db_ref[...] = jnp.sum(dy, axis=0, keepdims=True)
        dw_ref[0:1, :] = jnp.sum(dy * z2, axis=0, keepdims=True)
        dw_ref[1:2, :] = jnp.sum(dy * z1, axis=0, keepdims=True)
        dw_ref[2:3, :] = jnp.sum(dy * z, axis=0, keepdims=True)
        dz = w2 * dy + w1 * _shift_up(dy, 1) + w0 * _shift_up(dy, 2)
        dcg_ref[...] = (dz * xi).astype(dcg_ref.dtype)
        dxi_ref[...] = (dz * cg).astype(dxi_ref.dtype)

    col = _bs((s_len, LANES), lambda j: (0, j))
    sd = jax.ShapeDtypeStruct
    return pl.pallas_call(
        body, grid=(CW // LANES,), in_specs=_conv_specs(s_len) + [col],
        out_specs=[col, col, col, col, _bs((3, LANES), lambda j: (0, j)), _bs((1, LANES), lambda j: (0, j))],
        out_shape=[sd((s_len, CW), MM)] * 4 + [sd((3, CW), F32), sd((1, CW), F32)],
        name="conv_bwd", compiler_params=_cparams(1))(proj, proj, proj, proj, cw, cb, dys)


def _sg_specs(tm):
    row = lambda off: _bs((tm, SGW), lambda i: (i, off // SGW))
    return [row(OFF_SGI), row(OFF_SGI + SGW), row(OFF_SG + 2 * BW), _bs((1, SGW), lambda i: (0, 0)), _bs((1, SGW), lambda i: (0, 0)),
            _bs((SGG, SGC, SGC), lambda i: (0, 0, 0)), _bs((SGG, SGC, 1), lambda i: (0, 0, 0))]


def _sg_args(refs):
    u, v, sg, lg, lb, ws, bs = refs
    return (u[...], v[...], sg[...], lg[...], lb[...], [ws[g] for g in range(SGG)], [bs[g] for g in range(SGG)])


def _sg_call(proj, ln_g, ln_b, ws, bs):
    s_len = proj.shape[0]
    tm = min(s_len, 256)

    def body(*refs):
        refs[7][...] = _sg_fn(*_sg_args(refs[:7])).astype(refs[7].dtype)

    return pl.pallas_call(
        body, grid=(s_len // tm,), in_specs=_sg_specs(tm), out_specs=_bs((tm, SGW), lambda i: (i, 0)),
        out_shape=jax.ShapeDtypeStruct((s_len, SGW), MM), name="sgmlp", compiler_params=_cparams(1))(proj, proj, proj, ln_g, ln_b, ws, bs)


def _sg_bwd_call(proj, ln_g, ln_b, ws, bs, dys):
    s_len = proj.shape[0]
    tm = min(s_len, 256)

    def body(*refs):
        dys_ref = refs[7]
        duv_ref, dsg_ref, dlg_ref, dlb_ref, dws_ref, dbs_ref = refs[8:]
        _, vjp = jax.vjp(_sg_fn, *_sg_args(refs[:7]))
        du, dv, dsg, dlg, dlb, dws, dbs = vjp(dys_ref[...])
        duv_ref[...] = jnp.concatenate([du, dv], axis=-1).astype(duv_ref.dtype)
        dsg_ref[...] = dsg.astype(dsg_ref.dtype)

        @pl.when(pl.program_id(0) == 0)
        def _():
            for r in (dlg_ref, dlb_ref, dws_ref, dbs_ref):
                r[...] = jnp.zeros_like(r)
        dlg_ref[...] += dlg
        dlb_ref[...] += dlb
        for g in range(SGG):
            dws_ref[g] += dws[g]
            dbs_ref[g] += dbs[g]

    row = _bs((tm, SGW), lambda i: (i, 0))
    sd = jax.ShapeDtypeStruct
    return pl.pallas_call(
        body, grid=(s_len // tm,), in_specs=_sg_specs(tm) + [row],
        out_specs=[_bs((tm, 2 * SGW), lambda i: (i, 0)), row, _bs((1, SGW), lambda i: (0, 0)), _bs((1, SGW), lambda i: (0, 0)),
                   _bs((SGG, SGC, SGC), lambda i: (0, 0, 0)), _bs((SGG, SGC, 1), lambda i: (0, 0, 0))],
        out_shape=[sd((s_len, 2 * SGW), MM), sd((s_len, SGW), MM), sd((1, SGW), F32), sd((1, SGW), F32),
                   sd((SGG, SGC, SGC), F32), sd((SGG, SGC, 1), F32)],
        name="sgmlp_bwd", compiler_params=_cparams(1))(proj, proj, proj, ln_g, ln_b, ws, bs, dys)


def _memkv_call(mem, mem_g, wm, kg):
    m_len = mem.shape[0]

    def body(mem_ref, g_ref, w_ref, kg_ref, k_ref, v_ref):
        k, v = _memkv_fn(mem_ref[...], g_ref[...], w_ref[...], kg_ref[...])
        k_ref[...] = k.astype(k_ref.dtype)
        v_ref[...] = v.astype(v_ref.dtype)

    return pl.pallas_call(body, out_shape=[jax.ShapeDtypeStruct((m_len, MH * MHD), MM)] * 2, name="memkv",
                          compiler_params=pltpu.CompilerParams(vmem_limit_bytes=VMEM_LIMIT))(mem, mem_g, wm, kg)


def _memkv_bwd_call(mem, mem_g, wm, kg, dk, dv):
    def body(mem_ref, g_ref, w_ref, kg_ref, dk_ref, dv_ref, dg_ref, dw_ref, dkg_ref):
        _, vjp = jax.vjp(_memkv_fn, mem_ref[...], g_ref[...], _with_slot(w_ref[...]), kg_ref[...])
        _, dg, dw, dkg = vjp((dk_ref[...], dv_ref[...]))
        dg_ref[...] = dg
        dw_ref[...] = dw[1]
        dkg_ref[...] = dkg

    sd = jax.ShapeDtypeStruct
    return pl.pallas_call(body, out_shape=[sd((1, D), F32), sd((D, 2 * MH * MHD), F32), sd((1, MHD), F32)], name="memkv_bwd",
                          compiler_params=pltpu.CompilerParams(vmem_limit_bytes=VMEM_LIMIT))(mem, mem_g, wm, kg, dk, dv)


def _mem_specs(tm, m_len):
    w = MH * MHD
    return [_bs((tm, w), lambda i: (i, OFF_MQ // w)), _bs((tm, BW), lambda i: (i, (OFF_SG + 3 * BW) // BW)),
            _bs((m_len, w), lambda i: (0, 0)), _bs((m_len, w), lambda i: (0, 0)), _bs((1, MHD), lambda i: (0, 0))]


def _mem_call(proj, k, v, qg):
    s_len, m_len = proj.shape[0], k.shape[0]
    tm = min(s_len, 256)

    def body(mq_ref, sg_ref, k_ref, v_ref, qg_ref, y_ref):
        y_ref[...] = _mem_fn(mq_ref[...], sg_ref[...], k_ref[...], v_ref[...], qg_ref[...]).astype(y_ref.dtype)

    return pl.pallas_call(
        body, grid=(s_len // tm,), in_specs=_mem_specs(tm, m_len), out_specs=_bs((tm, BW), lambda i: (i, 0)),
        out_shape=jax.ShapeDtypeStruct((s_len, BW), MM), name="memattn", compiler_params=_cparams(1))(proj, proj, k, v, qg)


def _mem_bwd_call(proj, k, v, qg, dys):
    s_len, m_len = proj.shape[0], k.shape[0]
    tm = min(s_len, 256)
    w = MH * MHD

    def body(mq_ref, sg_ref, k_ref, v_ref, qg_ref, dys_ref, dmq_ref, dsg_ref, dk_ref, dv_ref, dqg_ref):
        _, vjp = jax.vjp(_mem_fn, mq_ref[...], sg_ref[...], k_ref[...].astype(F32), v_ref[...].astype(F32), qg_ref[...])
        dmq, dsg, dk, dv, dqg = vjp(dys_ref[...])
        dmq_ref[...] = dmq.astype(dmq_ref.dtype)
        dsg_ref[...] = dsg.astype(dsg_ref.dtype)

        @pl.when(pl.program_id(0) == 0)
        def _():
            for r in (dk_ref, dv_ref, dqg_ref):
                r[...] = jnp.zeros_like(r)
        dk_ref[...] += dk
        dv_ref[...] += dv
        dqg_ref[...] += dqg

    row = _bs((tm, BW), lambda i: (i, 0))
    kv = _bs((m_len, w), lambda i: (0, 0))
    sd = jax.ShapeDtypeStruct
    return pl.pallas_call(
        body, grid=(s_len // tm,), in_specs=_mem_specs(tm, m_len) + [row],
        out_specs=[row, row, kv, kv, _bs((1, MHD), lambda i: (0, 0))],
        out_shape=[sd((s_len, w), MM), sd((s_len, BW), MM), sd((m_len, w), F32), sd((m_len, w), F32), sd((1, MHD), F32)],
        name="memattn_bwd", compiler_params=_cparams(1))(proj, proj, k, v, qg, dys)


def _merge_specs(tm):
    row = _bs((tm, BW), lambda i: (i, 0))
    return [row, row, row, row, _bs((tm, NB * D), lambda i: (i, OFF_ML // (NB * D))), _bs((NB, D), lambda i: (0, 0)),
            _bs((N_CHIPS, NB, BW, D // N_CHIPS), lambda i: (0, 0, 0, 0)), _bs((D, D), lambda i: (0, 0))]


def _merge_call(ys, proj, bm, wb, wo, x):
    s_len = proj.shape[0]
    tm = min(s_len, 256)

    def body(ya, yb, yc, yd, lg_ref, bm_ref, wb_ref, wo_ref, x_ref, o_ref):
        out = _merge_fn([r[...] for r in (ya, yb, yc, yd)], lg_ref[...], [bm_ref[n:n + 1, :] for n in range(NB)],
                        [[wb_ref[j, n] for n in range(NB)] for j in range(N_CHIPS)], wo_ref[...])
        o_ref[...] = x_ref[...] + out

    xrow = _bs((tm, D), lambda i: (i, 0))
    return pl.pallas_call(
        body, grid=(s_len // tm,), in_specs=_merge_specs(tm) + [xrow], out_specs=xrow,
        out_shape=jax.ShapeDtypeStruct((s_len, D), F32), name="merge", compiler_params=_cparams(1))(*ys, proj, bm, wb, wo, x)


def _merge_bwd_call(ys, proj, bm, wb, wo, dout):
    s_len = proj.shape[0]
    tm = min(s_len, 256)

    def body(ya, yb, yc, yd, lg_ref, bm_ref, wb_ref, wo_ref, do_ref, dya, dyb, dyc, dyd, dlg_ref, dbm_ref, dwb_ref, dwo_ref):
        fn = lambda ys_, lg_, bm_, wb_, wo_: _merge_fn(ys_, lg_, bm_, wb_, wo_)
        _, vjp = jax.vjp(fn, [r[...].astype(F32) for r in (ya, yb, yc, yd)], lg_ref[...], [bm_ref[n:n + 1, :] for n in range(NB)],
                         [[_with_slot(wb_ref[j, n]) for n in range(NB)] for j in range(N_CHIPS)], _with_slot(wo_ref[...]))
        dys, dlg, dbm, dwb, dwo = vjp(do_ref[...])
        dwb, dwo = [[d[1] for d in row] for row in dwb], dwo[1]
        for r, d in zip((dya, dyb, dyc, dyd), dys):
            r[...] = d
        dlg_ref[...] = dlg.astype(dlg_ref.dtype)

        @pl.when(pl.program_id(0) == 0)
        def _():
            for r in (dbm_ref, dwb_ref, dwo_ref):
                r[...] = jnp.zeros_like(r)
        for n in range(NB):
            dbm_ref[n:n + 1, :] += dbm[n]
            for j in range(N_CHIPS):
                dwb_ref[j, n] += dwb[j][n]
        dwo_ref[...] += dwo

    row = _bs((tm, BW), lambda i: (i, 0))
    sd = jax.ShapeDtypeStruct
    wb_shape = (N_CHIPS, NB, BW, D // N_CHIPS)
    return pl.pallas_call(
        body, grid=(s_len // tm,), in_specs=_merge_specs(tm) + [_bs((tm, D), lambda i: (i, 0))],
        out_specs=[row, row, row, row, _bs((tm, NB * D), lambda i: (i, 0)), _bs((NB, D), lambda i: (0, 0)),
                   _bs(wb_shape, lambda i: (0, 0, 0, 0)), _bs((D, D), lambda i: (0, 0))],
        out_shape=[sd((s_len, BW), F32)] * 4 + [sd((s_len, NP), MM), sd((NB, D), F32), sd(wb_shape, F32), sd((D, D), F32)],
        name="merge_bwd", compiler_params=_cparams(1))(*ys, proj, bm, wb, wo, dout)


def _dh_call(dproj, w, x, g, dout, after=()):
    s_len = x.shape[0]
    tk = NP // 4
    after = list(after)

    def matmul_body(dp_ref, w_ref, *rest):
        o_ref = rest[-1]

        @pl.when(pl.program_id(0) == 0)
        def _():
            o_ref[...] = jnp.zeros_like(o_ref)
        o_ref[...] += lax.dot_general(dp_ref[...], w_ref[...], (((1,), (1,)), ((), ())), preferred_element_type=F32)

    dh = pl.pallas_call(
        matmul_body, grid=(NP // tk,),
        in_specs=[_bs((s_len, tk), lambda k: (0, k)), _bs((D, tk), lambda k: (0, k))] + [_ANY] * len(after),
        out_specs=_bs((s_len, D), lambda k: (0, 0)), out_shape=jax.ShapeDtypeStruct((s_len, D), F32),
        name="dh", compiler_params=_cparams(1))(dproj, w, *after)

    tm = min(s_len, 512)

    def norm_body(dh_ref, x_ref, g_ref, do_ref, dx_ref, dg_ref):
        _, vjp = jax.vjp(lambda x_, g_: _rms_n(x_, g_, D), x_ref[...], g_ref[...])
        dxr, dgr = vjp(dh_ref[...])
        dx_ref[...] = do_ref[...] + dxr

        @pl.when(pl.program_id(0) == 0)
        def _():
            dg_ref[...] = jnp.zeros_like(dg_ref)
        dg_ref[...] += dgr

    row = _bs((tm, D), lambda i: (i, 0))
    return pl.pallas_call(
        norm_body, grid=(s_len // tm,), in_specs=[row, row, _bs((1, D), lambda i: (0, 0)), row],
        out_specs=[row, _bs((1, D), lambda i: (0, 0))],
        out_shape=[jax.ShapeDtypeStruct((s_len, D), F32), jax.ShapeDtypeStruct((1, D), F32)],
        name="norm_bwd", compiler_params=_cparams(1))(dh, x, g, dout)


def _dw_call(h, dproj, after=()):
    s_len = h.shape[0]
    tn = 512
    after = list(after)

    def body(h_ref, dp_ref, *rest):
        o_ref, ht_ref = rest[-2], rest[-1]

        @pl.when(pl.program_id(0) == 0)
        def _():
            ht_ref[...] = h_ref[...].T
        o_ref[...] = jnp.dot(ht_ref[...], dp_ref[...], preferred_element_type=F32)

    return pl.pallas_call(
        body, grid=(NP // tn,),
        in_specs=[_bs((s_len, D), lambda j: (0, 0)), _bs((s_len, tn), lambda j: (0, j))] + [_ANY] * len(after),
        out_specs=_bs((D, tn), lambda j: (0, j)), out_shape=jax.ShapeDtypeStruct((D, NP), F32),
        scratch_shapes=[pltpu.VMEM((D, s_len), h.dtype)], name="dw_in", compiler_params=_cparams(1))(h, dproj, *after)


def _loss_call(y, target):
    s_len = y.shape[0]
    tm = min(s_len, 512)

    def body(y_ref, t_ref, dy_ref, l_ref):
        e = y_ref[...] - t_ref[...]
        dy_ref[...] = e * (1.0 / D)

        @pl.when(pl.program_id(0) == 0)
        def _():
            l_ref[...] = jnp.zeros_like(l_ref)
        l_ref[...] += jnp.sum(e * e, axis=0, keepdims=True)

    row = _bs((tm, D), lambda i: (i, 0))
    return pl.pallas_call(
        body, grid=(s_len // tm,), in_specs=[row, row], out_specs=[row, _bs((1, D), lambda i: (0, 0))],
        out_shape=[jax.ShapeDtypeStruct((s_len, D), F32), jax.ShapeDtypeStruct((1, D), F32)],
        name="loss", compiler_params=_cparams(1))(y, target)


def _adamw_small_call(ws, gs, ms, vs, name):
    n = len(ws)

    def body(*refs):
        for t in range(n):
            w_ref, g_ref, m_ref, v_ref = refs[t], refs[n + t], refs[2 * n + t], refs[3 * n + t]
            d_ref, nm_ref, nv_ref = refs[4 * n + 3 * t:4 * n + 3 * t + 3]
            gv = g_ref[...]
            m2 = ADAM_B1 * m_ref[...] + (1.0 - ADAM_B1) * gv
            v2 = ADAM_B2 * v_ref[...] + (1.0 - ADAM_B2) * (gv * gv)
            m_hat = m2 / (1.0 - ADAM_B1 ** ADAM_STEP)
            v_hat = v2 / (1.0 - ADAM_B2 ** ADAM_STEP)
            d_ref[...] = -ADAM_LR * (m_hat / (jnp.sqrt(v_hat) + ADAM_EPS) + ADAM_WD * w_ref[...])
            nm_ref[...] = m2
            nv_ref[...] = v2

    return pl.pallas_call(
        body, out_shape=[jax.ShapeDtypeStruct(w.shape, F32) for w in ws for _ in range(3)], name=name,
        compiler_params=pltpu.CompilerParams(vmem_limit_bytes=VMEM_LIMIT))(*ws, *gs, *ms, *vs)


def _adamw_layer_call(layer, ws, gs, ms, vs, prev, after, name, steps=8):
    n = len(ws)
    after = list(after)
    n_prev = 4 * n if prev is not None else 0

    def body(*refs):
        outs = refs[len(refs) - 4 * n:]
        for t in range(n):
            w_ref, g_ref, m_ref, v_ref = refs[t], refs[n + t], refs[2 * n + t], refs[3 * n + t]
            g_out, d_out, m_out, v_out = outs[4 * t:4 * t + 4]
            gv = g_ref[...]
            m2 = ADAM_B1 * m_ref[0] + (1.0 - ADAM_B1) * gv
            v2 = ADAM_B2 * v_ref[0] + (1.0 - ADAM_B2) * (gv * gv)
            m_hat = m2 / (1.0 - ADAM_B1 ** ADAM_STEP)
            v_hat = v2 / (1.0 - ADAM_B2 ** ADAM_STEP)
            g_out[0] = gv
            d_out[0] = -ADAM_LR * (m_hat / (jnp.sqrt(v_hat) + ADAM_EPS) + ADAM_WD * w_ref[0])
            m_out[0] = m2
            v_out[0] = v2

    def lay(a):
        return _bs((1, a.shape[1] // steps, a.shape[2]), lambda i: (layer, i, 0))

    in_specs = ([lay(a) for a in ws] + [_bs((g.shape[0] // steps, g.shape[1]), lambda i: (i, 0)) for g in gs]
                + [lay(a) for a in ms] + [lay(a) for a in vs] + [_ANY] * (n_prev + len(after)))
    return pl.pallas_call(
        body, grid=(steps,), in_specs=in_specs, out_specs=[lay(ws[t]) for t in range(n) for _ in range(4)],
        out_shape=[jax.ShapeDtypeStruct(ws[t].shape, F32) for t in range(n) for _ in range(4)],
        input_output_aliases={4 * n + q: q for q in range(n_prev)}, name=name, compiler_params=_cparams(1),
    )(*ws, *gs, *ms, *vs, *(prev if prev is not None else []), *after)


def _row_tile(rows):
    for cand in (512, 256, 128, 64, 32, 16, 8):
        if rows % cand == 0 and rows > cand:
            return cand
    return rows


def _pair_sum_call(grads, from_sibling, core, name):
    n = len(grads)

    def body(core_ref, *refs):
        for t in range(n):
            refs[2 * n + t][...] = (refs[t][...].astype(F32) + refs[n + t][...].astype(F32)).astype(MM)

    half = lambda g: (1, g.shape[1] // 2, g.shape[2])
    grid_spec = pltpu.PrefetchScalarGridSpec(
        num_scalar_prefetch=1, grid=(N_CHIPS,),
        in_specs=[pl.BlockSpec(half(g), lambda j, core_ref: (j, core_ref[0], 0)) for g in grads]
        + [pl.BlockSpec(half(g), lambda j, core_ref: (j, 0, 0)) for g in grads],
        out_specs=[pl.BlockSpec(half(g), lambda j, core_ref: (j, 0, 0)) for g in grads])
    return pl.pallas_call(
        body, grid_spec=grid_spec, out_shape=[jax.ShapeDtypeStruct((N_CHIPS,) + half(g)[1:], MM) for g in grads], name=name,
        compiler_params=_cparams(1))(core, *grads, *from_sibling)


def _owner_sum_call(chip_sums, from_chips, chip_core, name):
    n = len(chip_sums)
    steps = 4

    def body(ids_ref, *refs):
        for t in range(n):
            a, b = refs[t], refs[n + t]
            refs[2 * n + t][...] = ((a[0].astype(F32) + b[0].astype(F32)) + b[1].astype(F32)) + b[2].astype(F32)

    tile = lambda p: (p.shape[1] // steps, p.shape[2])
    grid_spec = pltpu.PrefetchScalarGridSpec(
        num_scalar_prefetch=1, grid=(steps,),
        in_specs=[pl.BlockSpec((1,) + tile(p), lambda i, ids_ref: (ids_ref[0], i, 0)) for p in chip_sums]
        + [pl.BlockSpec((3,) + tile(p), lambda i, ids_ref: (0, i, 0)) for p in chip_sums],
        out_specs=[pl.BlockSpec(tile(p), lambda i, ids_ref: (ids_ref[1] * steps + i, 0)) for p in chip_sums])
    return pl.pallas_call(
        body, grid_spec=grid_spec, out_shape=[jax.ShapeDtypeStruct((2 * p.shape[1], p.shape[2]), F32) for p in chip_sums],
        name=name, compiler_params=_cparams(1))(chip_core, *chip_sums, *from_chips)


def _sum8_call(parts):
    n, rows, cols = parts.shape
    tr = _row_tile(rows)

    def body(p_ref, o_ref):
        acc = p_ref[0]
        for k in range(1, n):
            acc = acc + p_ref[k]
        o_ref[...] = acc

    return pl.pallas_call(
        body, grid=(rows // tr,), in_specs=[_bs((n, tr, cols), lambda i: (0, i, 0))], out_specs=_bs((tr, cols), lambda i: (i, 0)),
        out_shape=jax.ShapeDtypeStruct((rows, cols), F32), name="sum_small_grads", compiler_params=_cparams(1))(parts)


_ANY = pl.BlockSpec(memory_space=pl.ANY)


def _half_rows(ref, lead, half, which):
    rows = pl.ds(pl.multiple_of(half * which, half), half)
    return ref.at[rows] if lead is None else ref.at[lead, rows]


_HBM = pl.BlockSpec(memory_space=pltpu.HBM)
_SEM = pl.BlockSpec(memory_space=pltpu.SEMAPHORE)
_ORDERED_EFFECT = pltpu.CompilerParams(has_side_effects=pltpu.SideEffectType.DATAFLOW_SIDE_EFFECTING)


_VMEM = pl.BlockSpec(memory_space=pltpu.VMEM)
_TOKEN = jax.ShapeDtypeStruct((8, LANES), F32)


def _in_hbm(a):
    return pltpu.with_memory_space_constraint(a, pltpu.HBM)


def _tie(small, token):
    return small + token[0:1, 0:1].reshape((1,) * small.ndim)


def _peer(k):
    x, y, c = lax.axis_index("x"), lax.axis_index("y"), lax.axis_index("c")
    bx, by, bc = (k >> 2) & 1, (k >> 1) & 1, k & 1
    return (x ^ bx if bx else x, y ^ by if by else y, c ^ bc if bc else c)


def _place_block_call(blk, index, name):
    rows, cols = blk.shape

    def body(idx_ref, b_ref, o_ref):
        o_ref[0] = b_ref[...]

    grid_spec = pltpu.PrefetchScalarGridSpec(
        num_scalar_prefetch=1, grid=(1,), in_specs=[pl.BlockSpec((rows, cols), lambda i, idx_ref: (0, 0))],
        out_specs=pl.BlockSpec((1, rows, cols), lambda i, idx_ref: (idx_ref[0], 0, 0)))
    return pl.pallas_call(body, grid_spec=grid_spec, out_shape=jax.ShapeDtypeStruct((8, rows, cols), blk.dtype), name=name,
                          compiler_params=_cparams(1))(index, blk)


def _small_gather_start_call(blk, buf, after, name):
    after = list(after)

    def body(*refs):
        b_ref, out_ref = refs[0], refs[2 + len(after)]
        send_sems, recv_sems, token = refs[3 + len(after):]
        x, y, c = lax.axis_index("x"), lax.axis_index("y"), lax.axis_index("c")
        for k in range(1, 8):
            pltpu.make_async_remote_copy(src_ref=b_ref, dst_ref=out_ref.at[4 * x + 2 * y + c], send_sem=send_sems.at[k - 1],
                                         recv_sem=recv_sems.at[k - 1], device_id=_peer(k), device_id_type=MESH_ID).start()
        token[...] = jnp.zeros_like(token)

    dma = pltpu.SemaphoreType.DMA
    return pl.pallas_call(
        body, out_shape=[pltpu.HBM(buf.shape, buf.dtype), dma((7,)), dma((7,)), _TOKEN],
        in_specs=[_HBM, _HBM] + [_ANY] * len(after), out_specs=[_HBM, _SEM, _SEM, _VMEM],
        input_output_aliases={1: 0}, name=name, compiler_params=_ORDERED_EFFECT)(_in_hbm(blk), _in_hbm(buf), *after)


def _small_gather_finish_call(blk, buf, send_sems, recv_sems, after, name):
    after = list(after)

    def body(*refs):
        b_ref, in_ref, send_ref, recv_ref = refs[:4]
        x, y, c = lax.axis_index("x"), lax.axis_index("y"), lax.axis_index("c")
        for k in range(1, 8):
            px, py, pc = _peer(k)
            pltpu.make_async_remote_copy(src_ref=b_ref, dst_ref=in_ref.at[4 * px + 2 * py + pc], send_sem=send_ref.at[k - 1],
                                         recv_sem=recv_ref.at[k - 1], device_id=(px, py, pc), device_id_type=MESH_ID).wait()

    return pl.pallas_call(
        body, out_shape=pltpu.HBM(buf.shape, buf.dtype), in_specs=[_HBM, _HBM, _SEM, _SEM] + [_ANY] * len(after),
        out_specs=_HBM, input_output_aliases={1: 0}, name=name, compiler_params=_ORDERED_EFFECT,
    )(_in_hbm(blk), buf, send_sems, recv_sems, *after)


def _pair_exchange_start_call(grads, name):
    n = len(grads)
    half = [g.shape[1] // 2 for g in grads]

    def body(*refs):
        srcs, outs = refs[:n], refs[n:2 * n]
        send_sems, recv_sems, token = refs[2 * n:]
        x, y, c = lax.axis_index("x"), lax.axis_index("y"), lax.axis_index("c")
        for t in range(n):
            pltpu.make_async_remote_copy(
                src_ref=srcs[t].at[:, pl.ds(pl.multiple_of(half[t] * (1 - c), half[t]), half[t])], dst_ref=outs[t],
                send_sem=send_sems.at[t], recv_sem=recv_sems.at[t], device_id=(x, y, 1 - c), device_id_type=MESH_ID).start()
        token[...] = jnp.zeros_like(token)

    dma = pltpu.SemaphoreType.DMA
    return pl.pallas_call(
        body, out_shape=[pltpu.HBM((g.shape[0], g.shape[1] // 2, g.shape[2]), g.dtype) for g in grads] + [dma((n,)), dma((n,)), _TOKEN],
        in_specs=[_HBM] * n, out_specs=[_HBM] * n + [_SEM, _SEM, _VMEM], name=name, compiler_params=_ORDERED_EFFECT,
    )(*[_in_hbm(g) for g in grads])


def _pair_exchange_finish_call(grads, bufs, send_sems, recv_sems, after, name):
    n = len(grads)
    after = list(after)
    half = [g.shape[1] // 2 for g in grads]

    def body(*refs):
        srcs, ins, send_ref, recv_ref = refs[:n], refs[n:2 * n], refs[2 * n], refs[2 * n + 1]
        x, y, c = lax.axis_index("x"), lax.axis_index("y"), lax.axis_index("c")
        for t in range(n):
            pltpu.make_async_remote_copy(
                src_ref=srcs[t].at[:, pl.ds(pl.multiple_of(half[t] * (1 - c), half[t]), half[t])], dst_ref=ins[t],
                send_sem=send_ref.at[t], recv_sem=recv_ref.at[t], device_id=(x, y, 1 - c), device_id_type=MESH_ID).wait()

    return pl.pallas_call(
        body, out_shape=[pltpu.HBM(b.shape, b.dtype) for b in bufs],
        in_specs=[_HBM] * (2 * n) + [_SEM, _SEM] + [_ANY] * len(after), out_specs=[_HBM] * n,
        input_output_aliases={n + t: t for t in range(n)}, name=name, compiler_params=_ORDERED_EFFECT,
    )(*[_in_hbm(g) for g in grads], *bufs, send_sems, recv_sems, *after)


def _chip_scatter_start_call(chip_sums, name):
    n = len(chip_sums)

    def body(*refs):
        srcs, outs = refs[:n], refs[n:2 * n]
        send_sems, recv_sems, token = refs[2 * n:]
        x, y, c = lax.axis_index("x"), lax.axis_index("y"), lax.axis_index("c")
        chips = [(1 - x, y), (x, 1 - y), (1 - x, 1 - y)]
        for k, (cx, cy) in enumerate(chips):
            for t in range(n):
                pltpu.make_async_remote_copy(
                    src_ref=srcs[t].at[2 * cx + cy], dst_ref=outs[t].at[k], send_sem=send_sems.at[3 * t + k],
                    recv_sem=recv_sems.at[3 * t + k], device_id=(cx, cy, c), device_id_type=MESH_ID).start()
        token[...] = jnp.zeros_like(token)

    dma = pltpu.SemaphoreType.DMA
    return pl.pallas_call(
        body, out_shape=[pltpu.HBM((3,) + p.shape[1:], p.dtype) for p in chip_sums] + [dma((3 * n,)), dma((3 * n,)), _TOKEN],
        in_specs=[_HBM] * n, out_specs=[_HBM] * n + [_SEM, _SEM, _VMEM], name=name, compiler_params=_ORDERED_EFFECT,
    )(*[_in_hbm(p) for p in chip_sums])


def _chip_scatter_finish_call(chip_sums, bufs, send_sems, recv_sems, after, name):
    n = len(chip_sums)
    after = list(after)

    def body(*refs):
        srcs, ins, send_ref, recv_ref = refs[:n], refs[n:2 * n], refs[2 * n], refs[2 * n + 1]
        x, y, c = lax.axis_index("x"), lax.axis_index("y"), lax.axis_index("c")
        chips = [(1 - x, y), (x, 1 - y), (1 - x, 1 - y)]
        for k, (cx, cy) in enumerate(chips):
            for t in range(n):
                pltpu.make_async_remote_copy(
                    src_ref=srcs[t].at[2 * cx + cy], dst_ref=ins[t].at[k], send_sem=send_ref.at[3 * t + k],
                    recv_sem=recv_ref.at[3 * t + k], device_id=(cx, cy, c), device_id_type=MESH_ID).wait()

    return pl.pallas_call(
        body, out_shape=[pltpu.HBM(b.shape, b.dtype) for b in bufs],
        in_specs=[_HBM] * (2 * n) + [_SEM, _SEM] + [_ANY] * len(after), out_specs=[_HBM] * n,
        input_output_aliases={n + t: t for t in range(n)}, name=name, compiler_params=_ORDERED_EFFECT,
    )(*[_in_hbm(p) for p in chip_sums], *bufs, send_sems, recv_sems, *after)


def _place_own_call(mine, chip_core, name):
    n = len(mine)

    def body(ids_ref, *refs):
        for t in range(n):
            refs[n + t][0] = refs[t][...]

    def imap_out(s):
        pad = (0,) * (s.ndim - 1)
        return lambda i, ids_ref: (ids_ref[0], ids_ref[1]) + pad

    grid_spec = pltpu.PrefetchScalarGridSpec(
        num_scalar_prefetch=1, grid=(1,), in_specs=[pl.BlockSpec(s.shape, lambda i, ids_ref, k=s.ndim: (0,) * k) for s in mine],
        out_specs=[pl.BlockSpec((1,) + s.shape, imap_out(s)) for s in mine])
    return pl.pallas_call(
        body, grid_spec=grid_spec,
        out_shape=[jax.ShapeDtypeStruct((N_CHIPS, 2 * s.shape[0]) + s.shape[1:], s.dtype) for s in mine],
        name=name, compiler_params=_cparams(1))(chip_core, *mine)


def _gather_start_call(mine, bufs, after, name):
    n = len(mine)
    half = [s.shape[0] for s in mine]

    def body(*refs):
        srcs, outs = refs[:n], refs[2 * n + 1:3 * n + 1]
        send_sems, recv_sib, recv_ici, token = refs[3 * n + 1:]
        x, y, c = lax.axis_index("x"), lax.axis_index("y"), lax.axis_index("c")
        chips = [(1 - x, y), (x, 1 - y), (1 - x, 1 - y)]
        for t in range(n):
            dst = _half_rows(outs[t], 2 * x + y, half[t], c)
            pltpu.make_async_remote_copy(src_ref=srcs[t], dst_ref=dst, send_sem=send_sems.at[4 * t], recv_sem=recv_sib.at[t],
                                         device_id=(x, y, 1 - c), device_id_type=MESH_ID).start()
            for j, chip in enumerate(chips):
                pltpu.make_async_remote_copy(src_ref=srcs[t], dst_ref=dst, send_sem=send_sems.at[4 * t + 1 + j],
                                             recv_sem=recv_ici.at[3 * t + j], device_id=(*chip, c), device_id_type=MESH_ID).start()
        token[...] = jnp.zeros_like(token)

    dma = pltpu.SemaphoreType.DMA
    return pl.pallas_call(
        body, out_shape=[pltpu.HBM(b.shape, b.dtype) for b in bufs] + [dma((4 * n,)), dma((n,)), dma((3 * n,)), _TOKEN],
        in_specs=[_HBM] * (2 * n) + [_ANY], out_specs=[_HBM] * n + [_SEM] * 3 + [_VMEM],
        input_output_aliases={n + t: t for t in range(n)}, name=name, compiler_params=_ORDERED_EFFECT,
    )(*[_in_hbm(s) for s in mine], *[_in_hbm(b) for b in bufs], after)


def _gather_forward_call(bufs, recv_ici, after, name):
    n = len(bufs)
    half = [b.shape[1] // 2 for b in bufs]

    def body(*refs):
        ins, recv_ici_ref = refs[:n], refs[n]
        outs = refs[n + 2:2 * n + 2]
        send_fwd, recv_fwd, token = refs[2 * n + 2:]
        x, y, c = lax.axis_index("x"), lax.axis_index("y"), lax.axis_index("c")
        chips = [(1 - x, y), (x, 1 - y), (1 - x, 1 - y)]
        for j, (cx, cy) in enumerate(chips):
            for t in range(n):
                landed = _half_rows(ins[t], 2 * cx + cy, half[t], c)
                dst = _half_rows(outs[t], 2 * cx + cy, half[t], c)
                pltpu.make_async_remote_copy(src_ref=landed, dst_ref=landed, send_sem=send_fwd.at[3 * t + j],
                                             recv_sem=recv_ici_ref.at[3 * t + j], device_id=(cx, cy, c),
                                             device_id_type=MESH_ID).wait_recv()
                pltpu.make_async_remote_copy(src_ref=landed, dst_ref=dst, send_sem=send_fwd.at[3 * t + j],
                                             recv_sem=recv_fwd.at[3 * t + j], device_id=(x, y, 1 - c),
                                             device_id_type=MESH_ID).start()
        token[...] = jnp.zeros_like(token)

    dma = pltpu.SemaphoreType.DMA
    return pl.pallas_call(
        body, out_shape=[pltpu.HBM(b.shape, b.dtype) for b in bufs] + [dma((3 * n,)), dma((3 * n,)), _TOKEN],
        in_specs=[_HBM] * n + [_SEM, _ANY], out_specs=[_HBM] * n + [_SEM] * 2 + [_VMEM],
        input_output_aliases={t: t for t in range(n)}, name=name, compiler_params=_ORDERED_EFFECT,
    )(*bufs, recv_ici, after)


def _gather_finish_call(shards, bufs, send_sems, recv_sib, send_fwd, recv_fwd, after, name):
    n = len(bufs)
    half = [b.shape[1] // 2 for b in bufs]

    def body(*refs):
        srcs, ins = refs[:n], refs[n:2 * n]
        send_ref, recv_sib_ref, send_fwd_ref, recv_fwd_ref = refs[2 * n:2 * n + 4]
        x, y, c = lax.axis_index("x"), lax.axis_index("y"), lax.axis_index("c")
        chips = [(1 - x, y), (x, 1 - y), (1 - x, 1 - y)]
        sibling = (x, y, 1 - c)
        for t in range(n):
            for k in range(4):
                pltpu.make_async_remote_copy(src_ref=srcs[t], dst_ref=srcs[t], send_sem=send_ref.at[4 * t + k],
                                             recv_sem=recv_sib_ref.at[t], device_id=sibling, device_id_type=MESH_ID).wait_send()
            from_sibling = _half_rows(ins[t], 2 * x + y, half[t], 1 - c)
            pltpu.make_async_remote_copy(src_ref=from_sibling, dst_ref=from_sibling, send_sem=send_ref.at[4 * t],
                                         recv_sem=recv_sib_ref.at[t], device_id=sibling, device_id_type=MESH_ID).wait_recv()
            for j, (cx, cy) in enumerate(chips):
                sent = _half_rows(ins[t], 2 * cx + cy, half[t], c)
                passed = _half_rows(ins[t], 2 * cx + cy, half[t], 1 - c)
                pltpu.make_async_remote_copy(src_ref=sent, dst_ref=passed, send_sem=send_fwd_ref.at[3 * t + j],
                                             recv_sem=recv_fwd_ref.at[3 * t + j], device_id=sibling, device_id_type=MESH_ID).wait()

    return pl.pallas_call(
        body, out_shape=[pltpu.HBM(b.shape, b.dtype) for b in bufs],
        in_specs=[_HBM] * (2 * n) + [_SEM] * 4 + [_ANY], out_specs=[_HBM] * n,
        input_output_aliases={n + t: t for t in range(n)}, name=name, compiler_params=_ORDERED_EFFECT,
    )(*[_in_hbm(s) for s in shards], *bufs, send_sems, recv_sib, send_fwd, recv_fwd, after)


def _pair_gather_call(bufs, name):
    n = len(bufs)
    half = [b.shape[0] // 2 for b in bufs]

    def body(*refs):
        srcs, outs, send_sems, recv_sems = refs[:n], refs[n:2 * n], refs[2 * n], refs[2 * n + 1]
        x, y, c = lax.axis_index("x"), lax.axis_index("y"), lax.axis_index("c")
        for t in range(n):
            pltpu.make_async_remote_copy(
                src_ref=_half_rows(srcs[t], None, half[t], c), dst_ref=_half_rows(outs[t], None, half[t], c),
                send_sem=send_sems.at[t], recv_sem=recv_sems.at[t], device_id=(x, y, 1 - c), device_id_type=MESH_ID).start()
        for t in range(n):
            pltpu.make_async_remote_copy(
                src_ref=_half_rows(srcs[t], None, half[t], c), dst_ref=_half_rows(outs[t], None, half[t], 1 - c),
                send_sem=send_sems.at[t], recv_sem=recv_sems.at[t], device_id=(x, y, 1 - c), device_id_type=MESH_ID).wait()

    return pl.pallas_call(
        body, out_shape=[jax.ShapeDtypeStruct(b.shape, b.dtype) for b in bufs], in_specs=[_ANY] * n, out_specs=[_ANY] * n,
        input_output_aliases={t: t for t in range(n)},
        scratch_shapes=[pltpu.SemaphoreType.DMA((n,)), pltpu.SemaphoreType.DMA((n,))], name=name)(*bufs)


def _pair_gather_start_call(bufs, name):
    n = len(bufs)
    half = [b.shape[0] // 2 for b in bufs]

    def body(*refs):
        srcs, outs = refs[:n], refs[n:2 * n]
        send_sems, recv_sems, token = refs[2 * n:]
        x, y, c = lax.axis_index("x"), lax.axis_index("y"), lax.axis_index("c")
        for t in range(n):
            pltpu.make_async_remote_copy(
                src_ref=_half_rows(srcs[t], None, half[t], c), dst_ref=_half_rows(outs[t], None, half[t], c),
                send_sem=send_sems.at[t], recv_sem=recv_sems.at[t], device_id=(x, y, 1 - c), device_id_type=MESH_ID).start()
        token[...] = jnp.zeros_like(token)

    dma = pltpu.SemaphoreType.DMA
    return pl.pallas_call(
        body, out_shape=[pltpu.HBM(b.shape, b.dtype) for b in bufs] + [dma((n,)), dma((n,)), _TOKEN],
        in_specs=[_HBM] * n, out_specs=[_HBM] * n + [_SEM, _SEM, _VMEM], input_output_aliases={t: t for t in range(n)},
        name=name, compiler_params=_ORDERED_EFFECT)(*[_in_hbm(b) for b in bufs])


def _pair_gather_finish_call(bufs, send_sems, recv_sems, after, name):
    n = len(bufs)
    after = list(after)
    half = [b.shape[0] // 2 for b in bufs]

    def body(*refs):
        ins, send_ref, recv_ref = refs[:n], refs[n], refs[n + 1]
        x, y, c = lax.axis_index("x"), lax.axis_index("y"), lax.axis_index("c")
        for t in range(n):
            pltpu.make_async_remote_copy(
                src_ref=_half_rows(ins[t], None, half[t], c), dst_ref=_half_rows(ins[t], None, half[t], 1 - c),
                send_sem=send_ref.at[t], recv_sem=recv_ref.at[t], device_id=(x, y, 1 - c), device_id_type=MESH_ID).wait()

    return pl.pallas_call(
        body, out_shape=[pltpu.HBM(b.shape, b.dtype) for b in bufs], in_specs=[_HBM] * n + [_SEM, _SEM] + [_ANY] * len(after),
        out_specs=[_HBM] * n, input_output_aliases={t: t for t in range(n)}, name=name, compiler_params=_ORDERED_EFFECT,
    )(*bufs, send_sems, recv_sems, *after)


def _pack_rows(flats, dtype, row_multiple):
    flat = jnp.concatenate([f.reshape(-1).astype(dtype) for f in flats])
    n = flat.shape[0]
    rows = -(-n // PACK_W)
    rows = -(-rows // row_multiple) * row_multiple
    return jnp.pad(flat, (0, rows * PACK_W - n)).reshape(rows, PACK_W)


def _unpack(flat, shapes):
    out, off = [], 0
    for shp in shapes:
        n = math.prod(shp)
        out.append(flat[off:off + n].reshape(shp))
        off += n
    return out


_W_IN_SEGMENTS = ((R_ML, R_END, OFF_ML), (R_SG, R_ML, OFF_SG), (R_CV, R_SGI, OFF_CV), (R_SGI, R_MQ, OFF_SGI), (R_MQ, R_SG, OFF_MQ),
                  (R_CQ, R_CKV, OFF_CQ), (R_CKV, R_KR, OFF_CKV), (R_KR, R_CV, OFF_KR + NOPE))
W_IN_SHARD = R_END // N_CHIPS


def _realign_call(wg):
    tr = 128

    def body(w_ref, o_ref):
        pieces, pos = [], 0
        for r0, r1, a0 in _W_IN_SEGMENTS:
            if a0 > pos:
                pieces.append(jnp.zeros((tr, a0 - pos), o_ref.dtype))
            while r0 < r1:
                j = r0 // W_IN_SHARD
                hi = min(r1, (j + 1) * W_IN_SHARD)
                pieces.append(w_ref[j, :, r0 - j * W_IN_SHARD:hi - j * W_IN_SHARD])
                a0, r0 = a0 + hi - r0, hi
            pos = a0
        pieces.append(jnp.zeros((tr, NP - pos), o_ref.dtype))
        o_ref[...] = jnp.concatenate(pieces, axis=1)

    return pl.pallas_call(
        body, grid=(D // tr,), in_specs=[_bs((N_CHIPS, tr, W_IN_SHARD), lambda i: (0, i, 0))],
        out_specs=_bs((tr, NP), lambda i: (i, 0)), out_shape=jax.ShapeDtypeStruct((D, NP), wg.dtype),
        name="w_in_realign", compiler_params=_cparams(1))(wg)


def _unalign_call(dw, out_dtype):
    tr = 128
    by_ref = sorted(_W_IN_SEGMENTS)

    def body(dw_ref, o_ref):
        for j in range(N_CHIPS):
            lo_j, hi_j = j * W_IN_SHARD, (j + 1) * W_IN_SHARD
            pieces = []
            for r0, r1, a0 in by_ref:
                lo, hi = max(r0, lo_j), min(r1, hi_j)
                if lo < hi:
                    pieces.append(dw_ref[:, a0 + lo - r0:a0 + hi - r0])
            o_ref[j] = jnp.concatenate(pieces, axis=1).astype(o_ref.dtype)

    return pl.pallas_call(
        body, grid=(D // tr,), in_specs=[_bs((tr, NP), lambda i: (i, 0))],
        out_specs=_bs((N_CHIPS, tr, W_IN_SHARD), lambda i: (0, i, 0)),
        out_shape=jax.ShapeDtypeStruct((N_CHIPS, D, W_IN_SHARD), out_dtype), name="w_in_unalign", compiler_params=_cparams(1))(dw)


def _wuq_to_heads(w):
    w3 = w.reshape(QL, H, QKH)
    w3 = jnp.pad(w3, ((0, 0), (0, 0), (0, LANES - QKH)))
    return jnp.transpose(w3, (1, 0, 2))


def _wuq_from_heads(wh):
    return jnp.transpose(wh[:, :, :QKH], (1, 0, 2)).reshape(QL, H * QKH)


def _wukv_to_heads(w):
    w3 = w.reshape(KVL, H, NOPE + VH)
    wkn = jnp.transpose(jnp.pad(w3[:, :, :NOPE], ((0, 0), (0, 0), (0, LANES - NOPE))), (1, 0, 2))
    wv3 = w3[:, :, NOPE:]
    z = jnp.zeros((KVL, VH), w.dtype)
    cols = []
    for h in range(H):
        cols += [wv3[:, h], z] if h % 2 == 0 else [z, wv3[:, h]]
    return wkn, jnp.concatenate(cols, axis=1)


def _wukv_from_heads(wkn, wv):
    kn = jnp.transpose(wkn[:, :, :NOPE], (1, 0, 2))
    vs = jnp.stack([wv[:, LANES * h + VH * (h % 2):LANES * h + VH * (h % 2) + VH] for h in range(H)], axis=1)
    return jnp.concatenate([kn, vs], axis=2).reshape(KVL, H * (NOPE + VH))


def _layer_fwd(x, mem, tabs, p):
    proj, h = _proj_call(x, p["norm_g"], p["w_in"])
    if p.get("late") is not None:
        p = dict(p, **p["late"](proj))
    q, k, v = _mla_prep_call(proj, tabs, p["cq_g"], p["ckv_g"], p["qg"], p["kg"], p["wuq"], p["wkn"], p["wv"])
    ya, attn_o, attn_lse = _attn_call(q, k, v, proj)
    bm = p["bm"]
    if p.get("after_attn") is not None:
        bm = _tie(bm, p["after_attn"](ya))
    yb = _conv_call(proj, p["conv_w"], p["conv_b"])
    yc = _sg_call(proj, p["ln_g"], p["ln_b"], p["ws"], p["bs"])
    mk, mv = _memkv_call(mem, p["mem_g"], p["wm"], p["mkg"])
    yd = _mem_call(proj, mk, mv, p["mqg"])
    out = _merge_call((ya, yb, yc, yd), proj, bm, p["wb"], p["wo"], x)
    return out, dict(p=p, x=x, proj=proj, h=h, q=q, k=k, v=v, attn_o=attn_o, attn_lse=attn_lse, ys=(ya, yb, yc, yd), mk=mk, mv=mv)


def _layer_bwd(dout, mem, tabs, p, sv, start_after=None, on_rest_grads=None, on_grads=None):
    proj = sv["proj"]
    bm = p["bm"] if start_after is None else _tie(p["bm"], start_after)
    dya, dyb, dyc, dyd, dml, dbm, dwb, dwo = _merge_bwd_call(sv["ys"], proj, bm, p["wb"], p["wo"], dout)
    dq, dk, dv, dsg_a = _attn_bwd_call(sv["q"], sv["k"], sv["v"], proj, dya, sv["attn_o"], sv["attn_lse"])
    dlat, dcqg, dckvg, dqg, dkg, dwuq, dwkn, dwv = _mla_prep_bwd_call(
        proj, tabs, p["cq_g"], p["ckv_g"], p["qg"], p["kg"], p["wuq"], p["wkn"], p["wv"], dq, dk, dv)
    dbg, dcg, dxi, dsg_b, dcw, dcb = _conv_bwd_call(proj, p["conv_w"], p["conv_b"], dyb)
    duv, dsg_c, dlg, dlb, dws, dbs = _sg_bwd_call(proj, p["ln_g"], p["ln_b"], p["ws"], p["bs"], dyc)
    dmq, dsg_d, dmk, dmv, dmqg = _mem_bwd_call(proj, sv["mk"], sv["mv"], p["mqg"], dyd)
    dmem_g, dwm, dmkg = _memkv_bwd_call(mem, p["mem_g"], p["wm"], p["mkg"], dmk, dmv)
    grads = dict(cq_norm_g=dcqg, ckv_norm_g=dckvg, mla_q_norm_g=dqg[:, :QKH], mla_k_norm_g=dkg[:, :QKH],
                 conv_w=dcw, conv_b=dcb, sg_ln_g=dlg, sg_ln_b=dlb, w_spatial=dws, b_spatial=dbs,
                 mem_norm_g=dmem_g, mem_q_norm_g=dmqg, mem_k_norm_g=dmkg, b_merge=dbm,
                 wuq_heads=dwuq, wkn_heads=dwkn, wv_heads=dwv, w_mem_kv=dwm, w_branch_chips=dwb, w_out=dwo)
    started = [on_rest_grads(grads)] if on_rest_grads is not None else []
    dproj, off = dml, NB * D
    for piece in (dsg_a, dsg_b, dsg_c, dsg_d, dbg, dcg, dxi, duv, dmq, dlat):
        dproj = lax.dynamic_update_slice(dproj, piece, (0, off))
        off += piece.shape[1]
    grads["w_in_aligned"] = _dw_call(sv["h"], dproj, started)
    tokens = on_grads(grads) if on_grads is not None else ()
    dx, dnorm_g = _dh_call(dproj, p["w_in"], sv["x"], p["norm_g"], dout, tokens)
    grads["norm_g"] = dnorm_g
    return dx, grads


def _chips_to_cols(a):
    return jnp.concatenate([a[j] for j in range(N_CHIPS)], axis=1)


def _cols_to_chips(a):
    cols = a.shape[1] // N_CHIPS
    return jnp.stack([a[:, cols * j:cols * (j + 1)] for j in range(N_CHIPS)])


def _layer_params_first(l, rep, w_in_gathered, conv_w, b_merge):
    pad_g = lambda g: jnp.pad(g, (0, LANES - QKH)).reshape(1, LANES)
    return dict(
        norm_g=rep["norm_g"][l].reshape(1, D), w_in=_realign_call(w_in_gathered),
        cq_g=rep["cq_norm_g"][l].reshape(1, QL), ckv_g=rep["ckv_norm_g"][l].reshape(1, KVL),
        qg=pad_g(rep["mla_q_norm_g"][l]), kg=pad_g(rep["mla_k_norm_g"][l]),
        conv_w=conv_w, conv_b=rep["conv_b"][l].reshape(1, CW),
        ln_g=rep["sg_ln_g"][l].reshape(1, SGW), ln_b=rep["sg_ln_b"][l].reshape(1, SGW),
        ws=rep["w_spatial"][l], bs=rep["b_spatial"][l].reshape(SGG, SGC, 1),
        mem_g=rep["mem_norm_g"][l].reshape(1, D),
        mqg=rep["mem_q_norm_g"][l].reshape(1, MHD), mkg=rep["mem_k_norm_g"][l].reshape(1, MHD), bm=b_merge)


def _layer_params_rest(gathered):
    wkn, wv = _wukv_to_heads(_chips_to_cols(gathered["w_ukv"]))
    return dict(wuq=_wuq_to_heads(_chips_to_cols(gathered["w_uq"])), wkn=wkn, wv=wv,
                wm=gathered["w_mem_kv"].reshape(D, 2 * MH * MHD), wb=gathered["w_branch"], wo=gathered["w_out"].reshape(D, D))


def _layer_params(l, rep, gathered, conv_w, b_merge):
    return dict(_layer_params_first(l, rep, gathered["w_in"], conv_w, b_merge), **_layer_params_rest(gathered))


def _forward_backward(x, mem, pos, target, params, bwd_hooks=None):
    tabs = _rope_tables(pos)
    params = list(params)
    saved = []
    act = x
    for l in range(DEPTH):
        if callable(params[l]):
            params[l] = params[l](saved[-1], act)
        act, sv = _layer_fwd(act, mem, tabs, params[l])
        saved.append(sv)
    dy, sq = _loss_call(act, target)
    grads = [None] * DEPTH
    token = None
    for l in reversed(range(DEPTH)):
        hooks = dict(bwd_hooks[l]) if bwd_hooks else {}
        after_layer = hooks.pop("after_layer", None)
        dy, grads[l] = _layer_bwd(dy, mem, tabs, saved[l]["p"], saved[l], start_after=token, **hooks)
        token = after_layer(dy) if after_layer is not None else None
    return sq, dy, grads


_SHARDED_MM = ("w_in", "w_branch", "w_out", "w_mem_kv", "w_uq", "w_ukv")
_SHARDED_F32 = ("conv_w", "b_merge")
_REPLICATED = ("norm_g", "cq_norm_g", "ckv_norm_g", "mla_q_norm_g", "mla_k_norm_g", "conv_b", "sg_ln_g", "sg_ln_b",
               "w_spatial", "b_spatial", "mem_norm_g", "mem_q_norm_g", "mem_k_norm_g")
_ALL_REDUCED = _REPLICATED + _SHARDED_F32
_WEIGHTS = ("norm_g", "w_in", "cq_norm_g", "ckv_norm_g", "w_uq", "w_ukv", "mla_q_norm_g", "mla_k_norm_g", "conv_w", "conv_b",
            "sg_ln_g", "sg_ln_b", "w_spatial", "b_spatial", "mem_norm_g", "w_mem_kv", "mem_q_norm_g", "mem_k_norm_g",
            "b_merge", "w_branch", "w_out")
_SMALL = tuple(n for n in _WEIGHTS if n not in _SHARDED_MM)


class _SmallGather:
    def __init__(self, blk, after, tag):
        self.blk, self.tag = blk, tag
        x, y, c = lax.axis_index("x"), lax.axis_index("y"), lax.axis_index("c")
        own = _place_block_call(blk, (4 * x + 2 * y + c).astype(jnp.int32).reshape(1), tag + "place_own")
        self.buf, self.send, self.recv, self.token = _small_gather_start_call(blk, own, after, tag + "start")

    def finish(self, after):
        return _small_gather_finish_call(self.blk, self.buf, self.send, self.recv, after, self.tag + "finish")


def _small_sharded_weights(w, got):
    names = _SHARDED_F32
    per_chip = [_unpack(got[2 * j].reshape(-1), [w[n].shape for n in names]) for j in range(N_CHIPS)]
    return {n: jnp.concatenate([per_chip[j][t] for j in range(N_CHIPS)], axis=2) for t, n in enumerate(names)}


class _Gather:
    def __init__(self, w, layer, names, after, tag):
        self.names, self.tag = names, tag
        x, y, c = lax.axis_index("x"), lax.axis_index("y"), lax.axis_index("c")
        chip_core = jnp.stack([2 * x + y, c]).astype(jnp.int32)
        halves = [w[n].shape[1] // 2 for n in names]
        self.srcs = [lax.dynamic_slice_in_dim(w[n][layer], c * h, h, axis=0).astype(MM) for n, h in zip(names, halves)]
        k = len(names)
        out = _gather_start_call(self.srcs, _place_own_call(self.srcs, chip_core, tag + "place_own"), after, tag + "start")
        self.bufs, self.send, self.recv_sib, self.recv_ici, self.token = out[:k], out[k], out[k + 1], out[k + 2], out[k + 3]

    def pass_on(self, after):
        k = len(self.names)
        out = _gather_forward_call(self.bufs, self.recv_ici, after, self.tag + "forward")
        self.bufs, self.send_fwd, self.recv_fwd = out[:k], out[k], out[k + 1]
        return out[k + 2]

    def finish(self, after):
        got = _gather_finish_call(self.srcs, self.bufs, self.send, self.recv_sib, self.send_fwd, self.recv_fwd, after,
                                  self.tag + "finish")
        return dict(zip(self.names, got))


class _ReduceScatter:
    SLABS = dict(
        w_in=lambda g: _unalign_call(g["w_in_aligned"], MM),
        w_branch=lambda g: g["w_branch_chips"].reshape(N_CHIPS, NB * BW, D // N_CHIPS),
        w_out=lambda g: g["w_out"].reshape(N_CHIPS, D // N_CHIPS, D),
        w_mem_kv=lambda g: g["w_mem_kv"].reshape(N_CHIPS, D // N_CHIPS, 2 * MH * MHD),
        w_uq=lambda g: _cols_to_chips(_wuq_from_heads(g["wuq_heads"])),
        w_ukv=lambda g: _cols_to_chips(_wukv_from_heads(g["wkn_heads"], g["wv_heads"])))

    def __init__(self, tag, names):
        self.tag, self.names = tag, names

    def exchange(self, grads):
        self.tensors = [self.SLABS[n](grads) for n in self.names]
        n = len(self.tensors)
        out = _pair_exchange_start_call(self.tensors, self.tag + "exchange_start")
        self.ex_bufs, self.ex_send, self.ex_recv = out[:n], out[n], out[n + 1]
        return out[n + 2]

    def scatter(self, after):
        n = len(self.tensors)
        c = lax.axis_index("c")
        from_sibling = _pair_exchange_finish_call(self.tensors, self.ex_bufs, self.ex_send, self.ex_recv, after,
                                                  self.tag + "exchange_finish")
        self.chip_sums = _pair_sum_call(self.tensors, from_sibling, c.astype(jnp.int32).reshape(1), self.tag + "pair_sum")
        out = _chip_scatter_start_call(self.chip_sums, self.tag + "scatter_start")
        self.bufs, self.send_sems, self.recv_sems, self.token = out[:n], out[n], out[n + 1], out[n + 2]
        return self.token

    def finish(self, after):
        x, y, c = lax.axis_index("x"), lax.axis_index("y"), lax.axis_index("c")
        chip_core = jnp.stack([2 * x + y, c]).astype(jnp.int32)
        from_chips = _chip_scatter_finish_call(self.chip_sums, self.bufs, self.send_sems, self.recv_sems, after,
                                               self.tag + "scatter_finish")
        self.mine = _owner_sum_call(self.chip_sums, from_chips, chip_core, self.tag + "owner_sum")
        return dict(zip(self.names, _pair_gather_call(self.mine, self.tag + "pair_gather")))

    def finish_but_swap(self, after):
        n = len(self.names)
        x, y, c = lax.axis_index("x"), lax.axis_index("y"), lax.axis_index("c")
        chip_core = jnp.stack([2 * x + y, c]).astype(jnp.int32)
        from_chips = _chip_scatter_finish_call(self.chip_sums, self.bufs, self.send_sems, self.recv_sems, after,
                                               self.tag + "scatter_finish")
        mine = _owner_sum_call(self.chip_sums, from_chips, chip_core, self.tag + "owner_sum")
        out = _pair_gather_start_call(mine, self.tag + "pair_gather_start")
        self.pg_bufs, self.pg_send, self.pg_recv = out[:n], out[n], out[n + 1]
        return out[n + 2]

    def swapped(self, after):
        got = _pair_gather_finish_call(self.pg_bufs, self.pg_send, self.pg_recv, after, self.tag + "pair_gather_finish")
        return dict(zip(self.names, got))


def _small_sums(shapes, sq, got):
    total = _sum8_call(got).reshape(-1)
    parts = _unpack(total, [shapes[n] for n in _ALL_REDUCED] + [sq.shape])
    out = dict(zip(_ALL_REDUCED, parts))
    sq_total = parts[-1]
    chip = 2 * lax.axis_index("x") + lax.axis_index("y")
    for n in _SHARDED_F32:
        size = out[n].shape[2] // N_CHIPS
        out[n] = lax.dynamic_slice_in_dim(out[n], chip * size, size, axis=2)
    return out, sq_total


def _adamw_small(w, g, m, v):
    pick = lambda t: [t[n] for n in _SMALL]
    out = _adamw_small_call(pick(w), pick(g), pick(m), pick(v), "adamw_small")
    return tuple({n: out[3 * t + k] for t, n in enumerate(_SMALL)} for k in range(3))


def kernel(x, mem, positions, norm_g, w_in, cq_norm_g, ckv_norm_g, w_uq, w_ukv, mla_q_norm_g, mla_k_norm_g, conv_w, conv_b, sg_ln_g, sg_ln_b, w_spatial, b_spatial, mem_norm_g, w_mem_kv, mem_q_norm_g, mem_k_norm_g, b_merge, w_branch, w_out, loss_target, m_norm_g, m_w_in, m_cq_norm_g, m_ckv_norm_g, m_w_uq, m_w_ukv, m_mla_q_norm_g, m_mla_k_norm_g, m_conv_w, m_conv_b, m_sg_ln_g, m_sg_ln_b, m_w_spatial, m_b_spatial, m_mem_norm_g, m_w_mem_kv, m_mem_q_norm_g, m_mem_k_norm_g, m_b_merge, m_w_branch, m_w_out, v_norm_g, v_w_in, v_cq_norm_g, v_ckv_norm_g, v_w_uq, v_w_ukv, v_mla_q_norm_g, v_mla_k_norm_g, v_conv_w, v_conv_b, v_sg_ln_g, v_sg_ln_b, v_w_spatial, v_b_spatial, v_mem_norm_g, v_w_mem_kv, v_mem_q_norm_g, v_mem_k_norm_g, v_b_merge, v_w_branch, v_w_out):
    w = dict(norm_g=norm_g, w_in=w_in, cq_norm_g=cq_norm_g, ckv_norm_g=ckv_norm_g, w_uq=w_uq, w_ukv=w_ukv,
             mla_q_norm_g=mla_q_norm_g, mla_k_norm_g=mla_k_norm_g, conv_w=conv_w, conv_b=conv_b, sg_ln_g=sg_ln_g,
             sg_ln_b=sg_ln_b, w_spatial=w_spatial, b_spatial=b_spatial, mem_norm_g=mem_norm_g, w_mem_kv=w_mem_kv,
             mem_q_norm_g=mem_q_norm_g, mem_k_norm_g=mem_k_norm_g, b_merge=b_merge, w_branch=w_branch, w_out=w_out)
    m = dict(norm_g=m_norm_g, w_in=m_w_in, cq_norm_g=m_cq_norm_g, ckv_norm_g=m_ckv_norm_g, w_uq=m_w_uq, w_ukv=m_w_ukv,
             mla_q_norm_g=m_mla_q_norm_g, mla_k_norm_g=m_mla_k_norm_g, conv_w=m_conv_w, conv_b=m_conv_b, sg_ln_g=m_sg_ln_g,
             sg_ln_b=m_sg_ln_b, w_spatial=m_w_spatial, b_spatial=m_b_spatial, mem_norm_g=m_mem_norm_g, w_mem_kv=m_w_mem_kv,
             mem_q_norm_g=m_mem_q_norm_g, mem_k_norm_g=m_mem_k_norm_g, b_merge=m_b_merge, w_branch=m_w_branch, w_out=m_w_out)
    v = dict(norm_g=v_norm_g, w_in=v_w_in, cq_norm_g=v_cq_norm_g, ckv_norm_g=v_ckv_norm_g, w_uq=v_w_uq, w_ukv=v_w_ukv,
             mla_q_norm_g=v_mla_q_norm_g, mla_k_norm_g=v_mla_k_norm_g, conv_w=v_conv_w, conv_b=v_conv_b, sg_ln_g=v_sg_ln_g,
             sg_ln_b=v_sg_ln_b, w_spatial=v_w_spatial, b_spatial=v_b_spatial, mem_norm_g=v_mem_norm_g, w_mem_kv=v_w_mem_kv,
             mem_q_norm_g=v_mem_q_norm_g, mem_k_norm_g=v_mem_k_norm_g, b_merge=v_b_merge, w_branch=v_w_branch, w_out=v_w_out)

    chip_core = jnp.stack([2 * lax.axis_index("x") + lax.axis_index("y"), lax.axis_index("c")]).astype(jnp.int32)

    first = _Gather(w, 0, ("w_in",), chip_core, "gather_l0_w_in_")
    rest = _Gather(w, 0, _SHARDED_MM[1:], first.token, "gather_l0_rest_")
    small_on_its_way = _SmallGather(_pack_rows([w[n] for n in _SHARDED_F32], F32, 8), [rest.token], "gather_small_weights_")
    later = _Gather(w, 1, _SHARDED_MM, small_on_its_way.token, "gather_l1_")
    w_in0 = first.finish(first.pass_on(later.token))["w_in"]
    small = {}

    def rest_of_layer0(proj0):
        landed = _layer_params_rest(rest.finish(rest.pass_on(proj0)))
        small.update(_small_sharded_weights(w, small_on_its_way.finish([landed["wo"]])))
        return dict(landed, conv_w=small["conv_w"][0], bm=small["b_merge"][0])

    def layer1_params(saved0, act0):
        return _layer_params(1, w, later.finish(act0), small["conv_w"][1], small["b_merge"][1])

    params0 = _layer_params_first(0, w, w_in0, None, None)
    params = [dict(params0, late=rest_of_layer0, after_attn=later.pass_on), layer1_params]
    others = _SHARDED_MM[1:]
    rs1 = _ReduceScatter("rs_l1_", _SHARDED_MM)
    rs0_rest, rs0_w_in = _ReduceScatter("rs_l0_rest_", others), _ReduceScatter("rs_l0_w_in_", ("w_in",))

    def layer0_grads_done(grads):
        return [rs0_rest.scatter([grads["w_in_aligned"]]), rs0_w_in.exchange(grads)]

    hooks = [dict(on_rest_grads=rs0_rest.exchange, on_grads=layer0_grads_done),
             dict(on_grads=lambda grads: [rs1.exchange(grads)], after_layer=lambda dy: rs1.scatter([dy]))]
    sq, grad_x, layer_grads = _forward_backward(x[0], mem[0], positions[0], loss_target[0], params, hooks)

    layered = [layer_grads[l][n] for n in _ALL_REDUCED for l in range(DEPTH)]
    small_grads = _SmallGather(_pack_rows(layered + [sq], F32, 64), [grad_x], "gather_small_grads_")
    scattering = rs0_w_in.scatter([grad_x, small_grads.token])
    swapping1 = rs1.finish_but_swap([scattering])
    swapping0 = rs0_rest.finish_but_swap([scattering, swapping1])
    shard_grads = {1: rs1.swapped([swapping0])}
    as3d = lambda a: a.reshape(DEPTH, -1, a.shape[-1])
    as2d = lambda a: a.reshape(-1, a.shape[-1])
    big = lambda t: [as3d(t[n]) for n in others]
    turned = lambda t: [jnp.swapaxes(t["w_in"], 1, 2)]
    assert W_IN_SHARD % (8 * 7) == 0

    def update_w_in(l, grad, prev):
        return _adamw_layer_call(l, turned(w), [grad.T], turned(m), turned(v), prev, [], "adamw_w_in_l%d" % l, steps=7)

    def update_others(l, prev):
        return _adamw_layer_call(l, big(w), [as2d(shard_grads[l][n]) for n in others], big(m), big(v), prev, [], "adamw_l%d" % l)

    upd1 = update_others(1, None)
    shard_grads[0] = rs0_rest.swapped([upd1[0]])
    upd = update_others(0, upd1)
    full_shapes = {n: w[n].shape for n in _REPLICATED}
    full_shapes.update(conv_w=(DEPTH, 3, CW), b_merge=(DEPTH, NB, D))
    g, sq_total = _small_sums(full_shapes, sq, small_grads.finish([upd[0]]))
    loss = 0.5 / D * jnp.sum(sq_total)
    delta, new_m, new_v = _adamw_small(w, g, m, v)
    upd_in1 = update_w_in(1, shard_grads[1]["w_in"], None)
    w_in_grad0 = rs0_w_in.finish([grad_x, upd_in1[0], upd[0], delta["norm_g"]])["w_in"]
    upd_in = update_w_in(0, w_in_grad0, upd_in1)
    g["w_in"], delta["w_in"], new_m["w_in"], new_v["w_in"] = [jnp.swapaxes(a, 1, 2) for a in upd_in]
    for t, n in enumerate(others):
        g[n], delta[n], new_m[n], new_v[n] = [a.reshape(w[n].shape) for a in upd[4 * t:4 * t + 4]]
    return (loss, grad_x[None], *[g[n] for n in _WEIGHTS], *[delta[n] for n in _WEIGHTS],
            *[new_m[n] for n in _WEIGHTS], *[new_v[n] for n in _WEIGHTS])
```

```python
import functools
import math

import jax
import jax.numpy as jnp
from jax import lax
from jax.experimental import pallas as pl
from jax.experimental.pallas import tpu as pltpu

F32 = jnp.float32
MM = jnp.bfloat16

D = 1024
DEPTH = 2
EPS = 1e-6
H = 8
NOPE = 64
ROPE = 32
QKH = 96
VH = 64
QL = 256
KVL = 128
ROPE_THETA = 10000.0
CW = 512
SGW = 512
SGG = 4
SGC = 128
MH = 4
MHD = 128
NB = 4
BW = 512
NEG_INF = -1e30
LANES = 128
N_CHIPS = 4

R_CQ, R_CKV, R_KR, R_CV, R_SGI, R_MQ, R_SG, R_ML, R_END = 0, 256, 384, 416, 1952, 2976, 3488, 5536, 9632
OFF_ML, OFF_SG, OFF_CV, OFF_SGI, OFF_MQ, OFF_CQ, OFF_CKV, OFF_KR, NP = 0, 4096, 6144, 7680, 8704, 9216, 9472, 9600, 9728

ADAM_LR = 0.001
ADAM_B1 = 0.9
ADAM_B2 = 0.999
ADAM_EPS = 1e-08
ADAM_WD = 0.01
ADAM_STEP = 10

VMEM_LIMIT = 56 * 1024 * 1024
PACK_W = 512
MESH_ID = pl.DeviceIdType.MESH


def _cparams(n_axes):
    return pltpu.CompilerParams(dimension_semantics=("arbitrary",) * n_axes, vmem_limit_bytes=VMEM_LIMIT)


def _bs(shape, imap):
    return pl.BlockSpec(shape, imap)


@jax.custom_vjp
def _mm_plain(a, b):
    return jnp.dot(a.astype(MM), b.astype(MM), preferred_element_type=F32)


def _mm_plain_fwd(a, b):
    return _mm_plain(a, b), (a, b)


def _mm_plain_bwd(res, g):
    a, b = res
    gm = g.astype(MM)
    da = lax.dot_general(gm, b.astype(MM), (((1,), (1,)), ((), ())), preferred_element_type=F32)
    db = lax.dot_general(a.astype(MM), gm, (((0,), (0,)), ((), ())), preferred_element_type=F32)
    return da.astype(a.dtype), db.astype(b.dtype)


_mm_plain.defvjp(_mm_plain_fwd, _mm_plain_bwd)


@jax.custom_vjp
def _mm_slot(a, w, slot):
    return jnp.dot(a.astype(MM), w.astype(MM), preferred_element_type=F32)


def _mm_slot_fwd(a, w, slot):
    return _mm_slot(a, w, slot), (a, w)


def _mm_slot_bwd(res, g):
    a, w = res
    gm = g.astype(MM)
    da = lax.dot_general(gm, w.astype(MM), (((1,), (1,)), ((), ())), preferred_element_type=F32)
    dw = lax.dot_general(a.astype(MM), gm, (((0,), (0,)), ((), ())), preferred_element_type=F32)
    return da.astype(a.dtype), jnp.zeros_like(w), dw


_mm_slot.defvjp(_mm_slot_fwd, _mm_slot_bwd)


def _mm(a, b):
    if isinstance(b, tuple):
        return _mm_slot(a, b[0], b[1])
    return _mm_plain(a, b)


def _with_slot(w):
    return (w, jnp.zeros(w.shape, F32))


@jax.custom_vjp
def _mm_nt(a, b):
    return lax.dot_general(a.astype(MM), b.astype(MM), (((1,), (1,)), ((), ())), preferred_element_type=F32)


def _mm_nt_fwd(a, b):
    return _mm_nt(a, b), (a, b)


def _mm_nt_bwd(res, g):
    a, b = res
    gm = g.astype(MM)
    da = jnp.dot(gm, b.astype(MM), preferred_element_type=F32)
    db = lax.dot_general(gm, a.astype(MM), (((0,), (0,)), ((), ())), preferred_element_type=F32)
    return da.astype(a.dtype), db.astype(b.dtype)


_mm_nt.defvjp(_mm_nt_fwd, _mm_nt_bwd)


@functools.partial(jax.custom_vjp, nondiff_argnums=(1,))
def _lane_roll(x, shift):
    return pltpu.roll(x, shift, 1)


def _lane_roll_fwd(x, shift):
    return pltpu.roll(x, shift, 1), None


def _lane_roll_bwd(shift, _, g):
    return (pltpu.roll(g, (LANES - shift) % LANES, 1),)


_lane_roll.defvjp(_lane_roll_fwd, _lane_roll_bwd)


def _rms_n(x, g, n):
    ms = jnp.sum(x * x, axis=-1, keepdims=True) * (1.0 / n)
    return x * lax.rsqrt(ms + EPS) * g


def _softmax(s):
    m = jnp.max(s, axis=-1, keepdims=True)
    e = jnp.exp(s - m)
    return e / jnp.sum(e, axis=-1, keepdims=True)


def _rope(t, cos_t, sin_a, sin_b):
    return t * cos_t + _lane_roll(t, LANES - 16) * sin_a + _lane_roll(t, 16) * sin_b


def _mla_prep_fn(cq, ckv, kr, cos_t, sin_a, sin_b, cq_g, ckv_g, qg, kg, wuq, wkn, wv):
    cqn = _rms_n(cq, cq_g, QL)
    ckvn = _rms_n(ckv, ckv_g, KVL)
    lane = lax.broadcasted_iota(jnp.int32, kr.shape, 1)
    krm = jnp.where((lane >= NOPE) & (lane < QKH), kr, 0.0)
    qs, ks = [], []
    for h in range(H):
        qh = _rms_n(_mm(cqn, wuq[h]), qg, QKH)
        qs.append(_rope(qh, cos_t, sin_a, sin_b) * (QKH ** -0.5))
        kh = _rms_n(_mm(ckvn, wkn[h]) + krm, kg, QKH)
        ks.append(_rope(kh, cos_t, sin_a, sin_b))
    return jnp.concatenate(qs, axis=-1), jnp.concatenate(ks, axis=-1), _mm(ckvn, wv)


def _dot_nt(a, b):
    return lax.dot_general(a.astype(MM), b.astype(MM), (((1,), (1,)), ((), ())), preferred_element_type=F32)


def _dot_tn(a, b):
    return lax.dot_general(a.astype(MM), b.astype(MM), (((0,), (0,)), ((), ())), preferred_element_type=F32)


def _causal_scores(qe, ke):
    tq, kl = qe.shape[0], ke.shape[0]
    s = _dot_nt(qe, ke)
    rows = lax.broadcasted_iota(jnp.int32, (tq, tq), 0)
    cols = lax.broadcasted_iota(jnp.int32, (tq, tq), 1)
    own = jnp.where(cols <= rows, s[:, kl - tq:], NEG_INF)
    return own if kl == tq else jnp.concatenate([s[:, :kl - tq], own], axis=1)


def _head_lanes(e, shape):
    lane = lax.broadcasted_iota(jnp.int32, shape, len(shape) - 1)
    return (lane >= VH * e) & (lane < VH * (e + 1))


def _attn_pair_fwd(q2, k2, v2):
    tq = q2.shape[0]
    o = jnp.zeros((tq, LANES), F32)
    lse = jnp.zeros((tq, LANES), F32)
    for e in range(2):
        sl = slice(LANES * e, LANES * (e + 1))
        s = _causal_scores(q2[:, sl], k2[:, sl])
        m = jnp.max(s, axis=-1, keepdims=True)
        ex = jnp.exp(s - m)
        l = jnp.sum(ex, axis=-1, keepdims=True)
        ve = jnp.where(_head_lanes(e, v2[:, sl].shape), v2[:, sl], 0.0)
        o = o + jnp.dot(ex.astype(MM), ve.astype(MM), preferred_element_type=F32) * (1.0 / l)
        lse = jnp.where(_head_lanes(e, lse.shape), m + jnp.log(l), lse)
    return o, lse


def _attn_pair_bwd(q2, k2, v2, sg, dys, o, lse):
    sig = jax.nn.sigmoid(sg)
    do = dys * (sg * sig)
    dsg = dys * o * (sig * (1.0 + sg * (1.0 - sig)))
    dqs, dks, dvs = [], [], []
    for e in range(2):
        sl = slice(LANES * e, LANES * (e + 1))
        qe, ke = q2[:, sl], k2[:, sl]
        hm = _head_lanes(e, o.shape)
        lse_e = jnp.max(jnp.where(hm, lse, NEG_INF), axis=-1, keepdims=True)
        do_e = jnp.where(hm, do, 0.0)
        delta = jnp.sum(do_e * o, axis=-1, keepdims=True)
        p = jnp.exp(_causal_scores(qe, ke) - lse_e)
        ve = jnp.where(_head_lanes(e, v2[:, sl].shape), v2[:, sl], 0.0)
        dvs.append(_dot_tn(p, do_e))
        ds = p * (_dot_nt(do_e, ve) - delta)
        dqs.append(jnp.dot(ds.astype(MM), ke.astype(MM), preferred_element_type=F32))
        dks.append(_dot_tn(ds, qe))
    return jnp.concatenate(dqs, axis=-1), jnp.concatenate(dks, axis=-1), jnp.concatenate(dvs, axis=-1), dsg


def _sg_fn(u, v, sgc, ln_g, ln_b, ws, bs):
    mu = jnp.mean(v, axis=-1, keepdims=True)
    xc = v - mu
    vn = xc * lax.rsqrt(jnp.mean(xc * xc, axis=-1, keepdims=True) + EPS) * ln_g + ln_b
    r = lax.broadcasted_iota(jnp.int32, (SGC, SGC), 0)
    c = lax.broadcasted_iota(jnp.int32, (SGC, SGC), 1)
    wt = [jnp.where(r >= c, w, 0.0) for w in ws]
    row_blocks = []
    for ch in range(u.shape[0] // SGC):
        col_blocks = []
        for g in range(SGG):
            blk = vn[SGC * ch:SGC * (ch + 1), LANES * g:LANES * (g + 1)]
            col_blocks.append(_mm(wt[g], blk) + bs[g])
        row_blocks.append(jnp.concatenate(col_blocks, axis=-1))
    mixed = jnp.concatenate(row_blocks, axis=0)
    return (u * mixed) * jax.nn.silu(sgc)


def _memkv_fn(mem, mem_g, wm, kg):
    kv = _mm(_rms_n(mem, mem_g, D), wm)
    ks = [_rms_n(kv[:, MHD * h:MHD * (h + 1)], kg, MHD) for h in range(MH)]
    return jnp.concatenate(ks, axis=-1), kv[:, MH * MHD:]


def _mem_fn(mq, sgd, k, v, qg):
    outs = []
    for h in range(MH):
        sl = slice(MHD * h, MHD * (h + 1))
        qh = _rms_n(mq[:, sl], qg, MHD)
        p = _softmax(_mm_nt(qh, k[:, sl]) * (MHD ** -0.5))
        outs.append(_mm(p, v[:, sl]))
    return jnp.concatenate(outs, axis=-1) * jax.nn.silu(sgd)


def _merge_fn(ys, logits, bm, wb, wo):
    merged = None
    for n in range(NB):
        z = jnp.concatenate([_mm(ys[n], wb[j][n]) for j in range(N_CHIPS)], axis=-1)
        gate = jax.nn.sigmoid(logits[:, D * n:D * (n + 1)] + bm[n])
        merged = gate * z if merged is None else merged + gate * z
    return _mm(merged, wo)


def _proj_call(x, g, w):
    s_len = x.shape[0]
    tm, tn = s_len, 512

    def body(x_ref, g_ref, w_ref, p_ref, h_ref):
        @pl.when(pl.program_id(1) == 0)
        def _():
            h_ref[...] = _rms_n(x_ref[...], g_ref[...], D).astype(h_ref.dtype)
        p_ref[...] = jnp.dot(h_ref[...], w_ref[...], preferred_element_type=F32)

    return pl.pallas_call(
        body, grid=(s_len // tm, NP // tn),
        in_specs=[_bs((tm, D), lambda i, j: (i, 0)), _bs((1, D), lambda i, j: (0, 0)), _bs((D, tn), lambda i, j: (0, j))],
        out_specs=[_bs((tm, tn), lambda i, j: (i, j)), _bs((tm, D), lambda i, j: (i, 0))],
        out_shape=[jax.ShapeDtypeStruct((s_len, NP), F32), jax.ShapeDtypeStruct((s_len, D), MM)],
        name="proj", compiler_params=_cparams(2))(x, g, w)


def _rope_tables(pos):
    half = ROPE // 2
    inv_freq = ROPE_THETA ** (-jnp.arange(half, dtype=F32) / half)
    ang = pos.astype(F32)[:, None] * inv_freq
    cos, sin = jnp.cos(ang), jnp.sin(ang)
    s_len = pos.shape[0]
    z = lambda n: jnp.zeros((s_len, n), F32)
    cos_t = jnp.concatenate([jnp.ones((s_len, NOPE), F32), cos, cos, z(LANES - QKH)], axis=1)
    sin_a = jnp.concatenate([z(NOPE), -sin, z(LANES - NOPE - half)], axis=1)
    sin_b = jnp.concatenate([z(NOPE + half), sin, z(LANES - QKH)], axis=1)
    return cos_t, sin_a, sin_b


def _mla_prep_specs(tm):
    row = lambda w, off: _bs((tm, w), lambda i: (i, off // w))
    full2 = lambda a, b: _bs((a, b), lambda i: (0, 0))
    full3 = lambda a, b, c: _bs((a, b, c), lambda i: (0, 0, 0))
    tab = _bs((tm, LANES), lambda i: (i, 0))
    return [row(QL, OFF_CQ), row(KVL, OFF_CKV), row(LANES, OFF_KR), tab, tab, tab,
            full2(1, QL), full2(1, KVL), full2(1, LANES), full2(1, LANES),
            full3(H, QL, LANES), full3(H, KVL, LANES), full2(KVL, H * LANES)]


def _mla_prep_args(body_refs, wrap=lambda w: w):
    (cq, ckv, kr, ct, sa, sb, cqg, ckvg, qg, kg, wuq, wkn, wv) = body_refs
    return (cq[...], ckv[...], kr[...], ct[...], sa[...], sb[...], cqg[...], ckvg[...], qg[...], kg[...],
            [wrap(wuq[h]) for h in range(H)], [wrap(wkn[h]) for h in range(H)], wrap(wv[...]))


def _mla_prep_call(proj, tabs, cq_g, ckv_g, qg, kg, wuq, wkn, wv):
    s_len = proj.shape[0]
    tm = min(s_len, 256)

    def body(*refs):
        q_ref, k_ref, v_ref = refs[13:]
        q, k, v = _mla_prep_fn(*_mla_prep_args(refs[:13]))
        q_ref[...] = q.astype(q_ref.dtype)
        k_ref[...] = k.astype(k_ref.dtype)
        v_ref[...] = v.astype(v_ref.dtype)

    out = _bs((tm, H * LANES), lambda i: (i, 0))
    return pl.pallas_call(
        body, grid=(s_len // tm,), in_specs=_mla_prep_specs(tm), out_specs=[out, out, out],
        out_shape=[jax.ShapeDtypeStruct((s_len, H * LANES), MM)] * 3,
        name="mla_prep", compiler_params=_cparams(1))(proj, proj, proj, *tabs, cq_g, ckv_g, qg, kg, wuq, wkn, wv)


def _mla_prep_bwd_call(proj, tabs, cq_g, ckv_g, qg, kg, wuq, wkn, wv, dq, dk, dv):
    s_len = proj.shape[0]
    tm = min(s_len, 256)

    def body(*refs):
        dq_ref, dk_ref, dv_ref = refs[13:16]
        dlat_ref, dcqg_ref, dckvg_ref, dqg_ref, dkg_ref, dwuq_ref, dwkn_ref, dwv_ref = refs[16:]
        _, vjp = jax.vjp(_mla_prep_fn, *_mla_prep_args(refs[:13], _with_slot))
        (dcq, dckv, dkr, _, _, _, dcqg, dckvg, dqg, dkg, dwuq, dwkn, dwv) = vjp((dq_ref[...], dk_ref[...], dv_ref[...]))
        dwuq, dwkn, dwv = [d[1] for d in dwuq], [d[1] for d in dwkn], dwv[1]
        dlat_ref[...] = jnp.concatenate([dcq, dckv, dkr], axis=-1).astype(dlat_ref.dtype)

        @pl.when(pl.program_id(0) == 0)
        def _():
            for r in (dcqg_ref, dckvg_ref, dqg_ref, dkg_ref, dwuq_ref, dwkn_ref, dwv_ref):
                r[...] = jnp.zeros_like(r)
        dcqg_ref[...] += dcqg
        dckvg_ref[...] += dckvg
        dqg_ref[...] += dqg
        dkg_ref[...] += dkg
        for h in range(H):
            dwuq_ref[h] += dwuq[h]
            dwkn_ref[h] += dwkn[h]
        dwv_ref[...] += dwv

    big = _bs((tm, H * LANES), lambda i: (i, 0))
    row = lambda w: _bs((tm, w), lambda i: (i, 0))
    full2 = lambda a, b: _bs((a, b), lambda i: (0, 0))
    full3 = lambda a, b, c: _bs((a, b, c), lambda i: (0, 0, 0))
    sd = jax.ShapeDtypeStruct
    return pl.pallas_call(
        body, grid=(s_len // tm,), in_specs=_mla_prep_specs(tm) + [big, big, big],
        out_specs=[row(QL + KVL + LANES), full2(1, QL), full2(1, KVL), full2(1, LANES), full2(1, LANES),
                   full3(H, QL, LANES), full3(H, KVL, LANES), full2(KVL, H * LANES)],
        out_shape=[sd((s_len, QL + KVL + LANES), MM), sd((1, QL), F32), sd((1, KVL), F32),
                   sd((1, LANES), F32), sd((1, LANES), F32), sd((H, QL, LANES), F32), sd((H, KVL, LANES), F32),
                   sd((KVL, H * LANES), F32)],
        name="mla_prep_bwd", compiler_params=_cparams(1))(proj, proj, proj, *tabs, cq_g, ckv_g, qg, kg, wuq, wkn, wv, dq, dk, dv)


def _attn_specs(s_len, tq):
    pair = 2 * LANES
    return [_bs((tq, pair), lambda p, i: (i, p)), _bs((s_len, pair), lambda p, i: (0, p)), _bs((s_len, pair), lambda p, i: (0, p)),
            _bs((tq, LANES), lambda p, i: (i, OFF_SG // LANES + p))]


def _attn_call(q, k, v, proj):
    s_len = q.shape[0]
    tq = min(s_len, 256)

    def body(q_ref, k_ref, v_ref, sg_ref, y_ref, o_ref, lse_ref):
        for n in range(s_len // tq):
            @pl.when(pl.program_id(1) == n)
            def _():
                kl = (n + 1) * tq
                o, lse = _attn_pair_fwd(q_ref[...], k_ref[:kl, :], v_ref[:kl, :])
                y_ref[...] = (o * jax.nn.silu(sg_ref[...])).astype(y_ref.dtype)
                o_ref[...] = o
                lse_ref[...] = lse

    tile = _bs((tq, LANES), lambda p, i: (i, p))
    sd = jax.ShapeDtypeStruct
    return pl.pallas_call(
        body, grid=(H // 2, s_len // tq), in_specs=_attn_specs(s_len, tq), out_specs=[tile, tile, tile],
        out_shape=[sd((s_len, BW), MM), sd((s_len, BW), F32), sd((s_len, BW), F32)],
        name="attn", compiler_params=_cparams(2))(q, k, v, proj)


def _attn_bwd_call(q, k, v, proj, dys, o, lse):
    s_len = q.shape[0]
    tq = min(s_len, 256)
    pair = 2 * LANES

    def body(q_ref, k_ref, v_ref, sg_ref, dy_ref, o_ref, lse_ref, dq_ref, dk_ref, dv_ref, dsg_ref):
        i = pl.program_id(1)

        @pl.when(i == 0)
        def _():
            dk_ref[...] = jnp.zeros_like(dk_ref)
            dv_ref[...] = jnp.zeros_like(dv_ref)

        for n in range(s_len // tq):
            @pl.when(i == n)
            def _():
                kl = (n + 1) * tq
                dq, dk, dv, dsg = _attn_pair_bwd(q_ref[...], k_ref[:kl, :], v_ref[:kl, :], sg_ref[...], dy_ref[...],
                                                 o_ref[...], lse_ref[...])
                dq_ref[...] = dq
                dsg_ref[...] = dsg.astype(dsg_ref.dtype)
                dk_ref[:kl, :] += dk
                dv_ref[:kl, :] += dv

    sd = jax.ShapeDtypeStruct
    tile = _bs((tq, LANES), lambda p, i: (i, p))
    return pl.pallas_call(
        body, grid=(H // 2, s_len // tq),
        in_specs=_attn_specs(s_len, tq) + [tile, tile, tile],
        out_specs=[_bs((tq, pair), lambda p, i: (i, p)), _bs((s_len, pair), lambda p, i: (0, p)),
                   _bs((s_len, pair), lambda p, i: (0, p)), tile],
        out_shape=[sd((s_len, H * LANES), F32), sd((s_len, H * LANES), F32), sd((s_len, H * LANES), F32), sd((s_len, BW), MM)],
        name="attn_bwd", compiler_params=_cparams(2))(q, k, v, proj, dys, o, lse)


def _shift_down(a, n):
    r = lax.broadcasted_iota(jnp.int32, a.shape, 0)
    return jnp.where(r >= n, pltpu.roll(a, n, 0), 0.0)


def _shift_up(a, n):
    s_len = a.shape[0]
    r = lax.broadcasted_iota(jnp.int32, a.shape, 0)
    return jnp.where(r < s_len - n, pltpu.roll(a, s_len - n, 0), 0.0)


def _conv_specs(s_len):
    col = lambda off: _bs((s_len, LANES), lambda j: (0, off // LANES + j))
    return [col(OFF_CV), col(OFF_CV + CW), col(OFF_CV + 2 * CW), col(OFF_SG + BW),
            _bs((3, LANES), lambda j: (0, j)), _bs((1, LANES), lambda j: (0, j))]


def _conv_call(proj, cw, cb):
    s_len = proj.shape[0]

    def body(bg_ref, cg_ref, xi_ref, sg_ref, w_ref, b_ref, y_ref):
        z = cg_ref[...] * xi_ref[...]
        y = b_ref[...] + w_ref[0:1, :] * _shift_down(z, 2)
        y = y + w_ref[1:2, :] * _shift_down(z, 1)
        y = y + w_ref[2:3, :] * z
        y_ref[...] = ((bg_ref[...] * y) * jax.nn.silu(sg_ref[...])).astype(y_ref.dtype)

    return pl.pallas_call(
        body, grid=(CW // LANES,), in_specs=_conv_specs(s_len), out_specs=_bs((s_len, LANES), lambda j: (0, j)),
        out_shape=jax.ShapeDtypeStruct((s_len, CW), MM), name="conv", compiler_params=_cparams(1))(proj, proj, proj, proj, cw, cb)


def _conv_bwd_call(proj, cw, cb, dys):
    s_len = proj.shape[0]

    def body(bg_ref, cg_ref, xi_ref, sg_ref, w_ref, b_ref, dys_ref, dbg_ref, dcg_ref, dxi_ref, dsg_ref, dw_ref, db_ref):
        bg, cg, xi, sg = bg_ref[...], cg_ref[...], xi_ref[...], sg_ref[...]
        w0, w1, w2 = w_ref[0:1, :], w_ref[1:2, :], w_ref[2:3, :]
        z = cg * xi
        z1, z2 = _shift_down(z, 1), _shift_down(z, 2)
        y = b_ref[...] + w0 * z2
        y = y + w1 * z1
        y = y + w2 * z
        yb = bg * y
        sig = jax.nn.sigmoid(sg)
        silu = sg * sig
        dys_v = dys_ref[...]
        dsg_ref[...] = (dys_v * yb * (sig * (1.0 + sg * (1.0 - sig)))).astype(dsg_ref.dtype)
        dyb = dys_v * silu
        dbg_ref[...] = (dyb * y).astype(dbg_ref.dtype)
        dy = dyb * bg
        db_ref[...] = jnp.sum(dy, axis=0, keepdims=True)
        dw_ref[0:1, :] = jnp.sum(dy * z2, axis=0, keepdims=True)
        dw_ref[1:2, :] = jnp.sum(dy * z1, axis=0, keepdims=True)
        dw_ref[2:3, :] = jnp.sum(dy * z, axis=0, keepdims=True)
        dz = w2 * dy + w1 * _shift_up(dy, 1) + w0 * _shift_up(dy, 2)
        dcg_ref[...] = (dz * xi).astype(dcg_ref.dtype)
        dxi_ref[...] = (dz * cg).astype(dxi_ref.dtype)

    col = _bs((s_len, LANES), lambda j: (0, j))
    sd = jax.ShapeDtypeStruct
    return pl.pallas_call(
        body, grid=(CW // LANES,), in_specs=_conv_specs(s_len) + [col],
        out_specs=[col, col, col, col, _bs((3, LANES), lambda j: (0, j)), _bs((1, LANES), lambda j: (0, j))],
        out_shape=[sd((s_len, CW), MM)] * 4 + [sd((3, CW), F32), sd((1, CW), F32)],
        name="conv_bwd", compiler_params=_cparams(1))(proj, proj, proj, proj, cw, cb, dys)


def _sg_specs(tm):
    row = lambda off: _bs((tm, SGW), lambda i: (i, off // SGW))
    return [row(OFF_SGI), row(OFF_SGI + SGW), row(OFF_SG + 2 * BW), _bs((1, SGW), lambda i: (0, 0)), _bs((1, SGW), lambda i: (0, 0)),
            _bs((SGG, SGC, SGC), lambda i: (0, 0, 0)), _bs((SGG, SGC, 1), lambda i: (0, 0, 0))]


def _sg_args(refs):
    u, v, sg, lg, lb, ws, bs = refs
    return (u[...], v[...], sg[...], lg[...], lb[...], [ws[g] for g in range(SGG)], [bs[g] for g in range(SGG)])


def _sg_call(proj, ln_g, ln_b, ws, bs):
    s_len = proj.shape[0]
    tm = min(s_len, 256)

    def body(*refs):
        refs[7][...] = _sg_fn(*_sg_args(refs[:7])).astype(refs[7].dtype)

    return pl.pallas_call(
        body, grid=(s_len // tm,), in_specs=_sg_specs(tm), out_specs=_bs((tm, SGW), lambda i: (i, 0)),
        out_shape=jax.ShapeDtypeStruct((s_len, SGW), MM), name="sgmlp", compiler_params=_cparams(1))(proj, proj, proj, ln_g, ln_b, ws, bs)


def _sg_bwd_call(proj, ln_g, ln_b, ws, bs, dys):
    s_len = proj.shape[0]
    tm = min(s_len, 256)

    def body(*refs):
        dys_ref = refs[7]
        duv_ref, dsg_ref, dlg_ref, dlb_ref, dws_ref, dbs_ref = refs[8:]
        _, vjp = jax.vjp(_sg_fn, *_sg_args(refs[:7]))
        du, dv, dsg, dlg, dlb, dws, dbs = vjp(dys_ref[...])
        duv_ref[...] = jnp.concatenate([du, dv], axis=-1).astype(duv_ref.dtype)
        dsg_ref[...] = dsg.astype(dsg_ref.dtype)

        @pl.when(pl.program_id(0) == 0)
        def _():
            for r in (dlg_ref, dlb_ref, dws_ref, dbs_ref):
                r[...] = jnp.zeros_like(r)
        dlg_ref[...] += dlg
        dlb_ref[...] += dlb
        for g in range(SGG):
            dws_ref[g] += dws[g]
            dbs_ref[g] += dbs[g]

    row = _bs((tm, SGW), lambda i: (i, 0))
    sd = jax.ShapeDtypeStruct
    return pl.pallas_call(
        body, grid=(s_len // tm,), in_specs=_sg_specs(tm) + [row],
        out_specs=[_bs((tm, 2 * SGW), lambda i: (i, 0)), row, _bs((1, SGW), lambda i: (0, 0)), _bs((1, SGW), lambda i: (0, 0)),
                   _bs((SGG, SGC, SGC), lambda i: (0, 0, 0)), _bs((SGG, SGC, 1), lambda i: (0, 0, 0))],
        out_shape=[sd((s_len, 2 * SGW), MM), sd((s_len, SGW), MM), sd((1, SGW), F32), sd((1, SGW), F32),
                   sd((SGG, SGC, SGC), F32), sd((SGG, SGC, 1), F32)],
        name="sgmlp_bwd", compiler_params=_cparams(1))(proj, proj, proj, ln_g, ln_b, ws, bs, dys)


def _memkv_call(mem, mem_g, wm, kg):
    m_len = mem.shape[0]

    def body(mem_ref, g_ref, w_ref, kg_ref, k_ref, v_ref):
        k, v = _memkv_fn(mem_ref[...], g_ref[...], w_ref[...], kg_ref[...])
        k_ref[...] = k.astype(k_ref.dtype)
        v_ref[...] = v.astype(v_ref.dtype)

    return pl.pallas_call(body, out_shape=[jax.ShapeDtypeStruct((m_len, MH * MHD), MM)] * 2, name="memkv",
                          compiler_params=pltpu.CompilerParams(vmem_limit_bytes=VMEM_LIMIT))(mem, mem_g, wm, kg)


def _memkv_bwd_call(mem, mem_g, wm, kg, dk, dv):
    def body(mem_ref, g_ref, w_ref, kg_ref, dk_ref, dv_ref, dg_ref, dw_ref, dkg_ref):
        _, vjp = jax.vjp(_memkv_fn, mem_ref[...], g_ref[...], _with_slot(w_ref[...]), kg_ref[...])
        _, dg, dw, dkg = vjp((dk_ref[...], dv_ref[...]))
        dg_ref[...] = dg
        dw_ref[...] = dw[1]
        dkg_ref[...] = dkg

    sd = jax.ShapeDtypeStruct
    return pl.pallas_call(body, out_shape=[sd((1, D), F32), sd((D, 2 * MH * MHD), F32), sd((1, MHD), F32)], name="memkv_bwd",
                          compiler_params=pltpu.CompilerParams(vmem_limit_bytes=VMEM_LIMIT))(mem, mem_g, wm, kg, dk, dv)


def _mem_specs(tm, m_len):
    w = MH * MHD
    return [_bs((tm, w), lambda i: (i, OFF_MQ // w)), _bs((tm, BW), lambda i: (i, (OFF_SG + 3 * BW) // BW)),
            _bs((m_len, w), lambda i: (0, 0)), _bs((m_len, w), lambda i: (0, 0)), _bs((1, MHD), lambda i: (0, 0))]


def _mem_call(proj, k, v, qg):
    s_len, m_len = proj.shape[0], k.shape[0]
    tm = min(s_len, 256)

    def body(mq_ref, sg_ref, k_ref, v_ref, qg_ref, y_ref):
        y_ref[...] = _mem_fn(mq_ref[...], sg_ref[...], k_ref[...], v_ref[...], qg_ref[...]).astype(y_ref.dtype)

    return pl.pallas_call(
        body, grid=(s_len // tm,), in_specs=_mem_specs(tm, m_len), out_specs=_bs((tm, BW), lambda i: (i, 0)),
        out_shape=jax.ShapeDtypeStruct((s_len, BW), MM), name="memattn", compiler_params=_cparams(1))(proj, proj, k, v, qg)


def _mem_bwd_call(proj, k, v, qg, dys):
    s_len, m_len = proj.shape[0], k.shape[0]
    tm = min(s_len, 256)
    w = MH * MHD

    def body(mq_ref, sg_ref, k_ref, v_ref, qg_ref, dys_ref, dmq_ref, dsg_ref, dk_ref, dv_ref, dqg_ref):
        _, vjp = jax.vjp(_mem_fn, mq_ref[...], sg_ref[...], k_ref[...].astype(F32), v_ref[...].astype(F32), qg_ref[...])
        dmq, dsg, dk, dv, dqg = vjp(dys_ref[...])
        dmq_ref[...] = dmq.astype(dmq_ref.dtype)
        dsg_ref[...] = dsg.astype(dsg_ref.dtype)

        @pl.when(pl.program_id(0) == 0)
        def _():
            for r in (dk_ref, dv_ref, dqg_ref):
                r[...] = jnp.zeros_like(r)
        dk_ref[...] += dk
        dv_ref[...] += dv
        dqg_ref[...] += dqg

    row = _bs((tm, BW), lambda i: (i, 0))
    kv = _bs((m_len, w), lambda i: (0, 0))
    sd = jax.ShapeDtypeStruct
    return pl.pallas_call(
        body, grid=(s_len // tm,), in_specs=_mem_specs(tm, m_len) + [row],
        out_specs=[row, row, kv, kv, _bs((1, MHD), lambda i: (0, 0))],
        out_shape=[sd((s_len, w), MM), sd((s_len, BW), MM), sd((m_len, w), F32), sd((m_len, w), F32), sd((1, MHD), F32)],
        name="memattn_bwd", compiler_params=_cparams(1))(proj, proj, k, v, qg, dys)


def _merge_specs(tm):
    row = _bs((tm, BW), lambda i: (i, 0))
    return [row, row, row, row, _bs((tm, NB * D), lambda i: (i, OFF_ML // (NB * D))), _bs((NB, D), lambda i: (0, 0)),
            _bs((N_CHIPS, NB, BW, D // N_CHIPS), lambda i: (0, 0, 0, 0)), _bs((D, D), lambda i: (0, 0))]


def _merge_call(ys, proj, bm, wb, wo, x):
    s_len = proj.shape[0]
    tm = min(s_len, 512)

    def body(ya, yb, yc, yd, lg_ref, bm_ref, wb_ref, wo_ref, x_ref, o_ref):
        out = _merge_fn([r[...] for r in (ya, yb, yc, yd)], lg_ref[...], [bm_ref[n:n + 1, :] for n in range(NB)],
                        [[wb_ref[j, n] for n in range(NB)] for j in range(N_CHIPS)], wo_ref[...])
        o_ref[...] = x_ref[...] + out

    xrow = _bs((tm, D), lambda i: (i, 0))
    return pl.pallas_call(
        body, grid=(s_len // tm,), in_specs=_merge_specs(tm) + [xrow], out_specs=xrow,
        out_shape=jax.ShapeDtypeStruct((s_len, D), F32), name="merge", compiler_params=_cparams(1))(*ys, proj, bm, wb, wo, x)


def _merge_bwd_call(ys, proj, bm, wb, wo, dout):
    s_len = proj.shape[0]
    tm = min(s_len, 256)

    def body(ya, yb, yc, yd, lg_ref, bm_ref, wb_ref, wo_ref, do_ref, dya, dyb, dyc, dyd, dlg_ref, dbm_ref, dwb_ref, dwo_ref):
        fn = lambda ys_, lg_, bm_, wb_, wo_: _merge_fn(ys_, lg_, bm_, wb_, wo_)
        _, vjp = jax.vjp(fn, [r[...].astype(F32) for r in (ya, yb, yc, yd)], lg_ref[...], [bm_ref[n:n + 1, :] for n in range(NB)],
                         [[_with_slot(wb_ref[j, n]) for n in range(NB)] for j in range(N_CHIPS)], _with_slot(wo_ref[...]))
        dys, dlg, dbm, dwb, dwo = vjp(do_ref[...])
        dwb, dwo = [[d[1] for d in row] for row in dwb], dwo[1]
        for r, d in zip((dya, dyb, dyc, dyd), dys):
            r[...] = d
        dlg_ref[...] = dlg.astype(dlg_ref.dtype)

        @pl.when(pl.program_id(0) == 0)
        def _():
            for r in (dbm_ref, dwb_ref, dwo_ref):
                r[...] = jnp.zeros_like(r)
        for n in range(NB):
            dbm_ref[n:n + 1, :] += dbm[n]
            for j in range(N_CHIPS):
                dwb_ref[j, n] += dwb[j][n]
        dwo_ref[...] += dwo

    row = _bs((tm, BW), lambda i: (i, 0))
    sd = jax.ShapeDtypeStruct
    wb_shape = (N_CHIPS, NB, BW, D // N_CHIPS)
    return pl.pallas_call(
        body, grid=(s_len // tm,), in_specs=_merge_specs(tm) + [_bs((tm, D), lambda i: (i, 0))],
        out_specs=[row, row, row, row, _bs((tm, NB * D), lambda i: (i, 0)), _bs((NB, D), lambda i: (0, 0)),
                   _bs(wb_shape, lambda i: (0, 0, 0, 0)), _bs((D, D), lambda i: (0, 0))],
        out_shape=[sd((s_len, BW), F32)] * 4 + [sd((s_len, NP), MM), sd((NB, D), F32), sd(wb_shape, F32), sd((D, D), F32)],
        name="merge_bwd", compiler_params=_cparams(1))(*ys, proj, bm, wb, wo, dout)


def _dh_call(dproj, w, x, g, dout, after=()):
    s_len = x.shape[0]
    tk = NP // 4
    after = list(after)

    def matmul_body(dp_ref, w_ref, *rest):
        o_ref = rest[-1]

        @pl.when(pl.program_id(0) == 0)
        def _():
            o_ref[...] = jnp.zeros_like(o_ref)
        o_ref[...] += lax.dot_general(dp_ref[...], w_ref[...], (((1,), (1,)), ((), ())), preferred_element_type=F32)

    dh = pl.pallas_call(
        matmul_body, grid=(NP // tk,),
        in_specs=[_bs((s_len, tk), lambda k: (0, k)), _bs((D, tk), lambda k: (0, k))] + [_ANY] * len(after),
        out_specs=_bs((s_len, D), lambda k: (0, 0)), out_shape=jax.ShapeDtypeStruct((s_len, D), F32),
        name="dh", compiler_params=_cparams(1))(dproj, w, *after)

    tm = min(s_len, 512)

    def norm_body(dh_ref, x_ref, g_ref, do_ref, dx_ref, dg_ref):
        _, vjp = jax.vjp(lambda x_, g_: _rms_n(x_, g_, D), x_ref[...], g_ref[...])
        dxr, dgr = vjp(dh_ref[...])
        dx_ref[...] = do_ref[...] + dxr

        @pl.when(pl.program_id(0) == 0)
        def _():
            dg_ref[...] = jnp.zeros_like(dg_ref)
        dg_ref[...] += dgr

    row = _bs((tm, D), lambda i: (i, 0))
    return pl.pallas_call(
        norm_body, grid=(s_len // tm,), in_specs=[row, row, _bs((1, D), lambda i: (0, 0)), row],
        out_specs=[row, _bs((1, D), lambda i: (0, 0))],
        out_shape=[jax.ShapeDtypeStruct((s_len, D), F32), jax.ShapeDtypeStruct((1, D), F32)],
        name="norm_bwd", compiler_params=_cparams(1))(dh, x, g, dout)


def _dw_call(h, dproj, after=()):
    s_len = h.shape[0]
    tn = 512
    after = list(after)

    def body(h_ref, dp_ref, *rest):
        o_ref, ht_ref = rest[-2], rest[-1]

        @pl.when(pl.program_id(0) == 0)
        def _():
            ht_ref[...] = h_ref[...].T
        o_ref[...] = jnp.dot(ht_ref[...], dp_ref[...], preferred_element_type=F32)

    return pl.pallas_call(
        body, grid=(NP // tn,),
        in_specs=[_bs((s_len, D), lambda j: (0, 0)), _bs((s_len, tn), lambda j: (0, j))] + [_ANY] * len(after),
        out_specs=_bs((D, tn), lambda j: (0, j)), out_shape=jax.ShapeDtypeStruct((D, NP), F32),
        scratch_shapes=[pltpu.VMEM((D, s_len), h.dtype)], name="dw_in", compiler_params=_cparams(1))(h, dproj, *after)


def _loss_call(y, target):
    s_len = y.shape[0]
    tm = min(s_len, 512)

    def body(y_ref, t_ref, dy_ref, l_ref):
        e = y_ref[...] - t_ref[...]
        dy_ref[...] = e * (1.0 / D)

        @pl.when(pl.program_id(0) == 0)
        def _():
            l_ref[...] = jnp.zeros_like(l_ref)
        l_ref[...] += jnp.sum(e * e, axis=0, keepdims=True)

    row = _bs((tm, D), lambda i: (i, 0))
    return pl.pallas_call(
        body, grid=(s_len // tm,), in_specs=[row, row], out_specs=[row, _bs((1, D), lambda i: (0, 0))],
        out_shape=[jax.ShapeDtypeStruct((s_len, D), F32), jax.ShapeDtypeStruct((1, D), F32)],
        name="loss", compiler_params=_cparams(1))(y, target)


def _adamw_small_call(ws, gs, ms, vs, name):
    n = len(ws)

    def body(*refs):
        for t in range(n):
            w_ref, g_ref, m_ref, v_ref = refs[t], refs[n + t], refs[2 * n + t], refs[3 * n + t]
            d_ref, nm_ref, nv_ref = refs[4 * n + 3 * t:4 * n + 3 * t + 3]
            gv = g_ref[...]
            m2 = ADAM_B1 * m_ref[...] + (1.0 - ADAM_B1) * gv
            v2 = ADAM_B2 * v_ref[...] + (1.0 - ADAM_B2) * (gv * gv)
            m_hat = m2 / (1.0 - ADAM_B1 ** ADAM_STEP)
            v_hat = v2 / (1.0 - ADAM_B2 ** ADAM_STEP)
            d_ref[...] = -ADAM_LR * (m_hat / (jnp.sqrt(v_hat) + ADAM_EPS) + ADAM_WD * w_ref[...])
            nm_ref[...] = m2
            nv_ref[...] = v2

    return pl.pallas_call(
        body, out_shape=[jax.ShapeDtypeStruct(w.shape, F32) for w in ws for _ in range(3)], name=name,
        compiler_params=pltpu.CompilerParams(vmem_limit_bytes=VMEM_LIMIT))(*ws, *gs, *ms, *vs)


def _adamw_layer_call(layer, ws, gs, ms, vs, prev, after, name, steps=8):
    n = len(ws)
    after = list(after)
    n_prev = 4 * n if prev is not None else 0

    def body(*refs):
        outs = refs[len(refs) - 4 * n:]
        for t in range(n):
            w_ref, g_ref, m_ref, v_ref = refs[t], refs[n + t], refs[2 * n + t], refs[3 * n + t]
            g_out, d_out, m_out, v_out = outs[4 * t:4 * t + 4]
            gv = g_ref[...]
            m2 = ADAM_B1 * m_ref[0] + (1.0 - ADAM_B1) * gv
            v2 = ADAM_B2 * v_ref[0] + (1.0 - ADAM_B2) * (gv * gv)
            m_hat = m2 / (1.0 - ADAM_B1 ** ADAM_STEP)
            v_hat = v2 / (1.0 - ADAM_B2 ** ADAM_STEP)
            g_out[0] = gv
            d_out[0] = -ADAM_LR * (m_hat / (jnp.sqrt(v_hat) + ADAM_EPS) + ADAM_WD * w_ref[0])
            m_out[0] = m2
            v_out[0] = v2

    def lay(a):
        return _bs((1, a.shape[1] // steps, a.shape[2]), lambda i: (layer, i, 0))

    in_specs = ([lay(a) for a in ws] + [_bs((g.shape[0] // steps, g.shape[1]), lambda i: (i, 0)) for g in gs]
                + [lay(a) for a in ms] + [lay(a) for a in vs] + [_ANY] * (n_prev + len(after)))
    return pl.pallas_call(
        body, grid=(steps,), in_specs=in_specs, out_specs=[lay(ws[t]) for t in range(n) for _ in range(4)],
        out_shape=[jax.ShapeDtypeStruct(ws[t].shape, F32) for t in range(n) for _ in range(4)],
        input_output_aliases={4 * n + q: q for q in range(n_prev)}, name=name, compiler_params=_cparams(1),
    )(*ws, *gs, *ms, *vs, *(prev if prev is not None else []), *after)


def _row_tile(rows):
    for cand in (512, 256, 128, 64, 32, 16, 8):
        if rows % cand == 0 and rows > cand:
            return cand
    return rows


def _pair_sum_call(grads, from_sibling, core, name):
    n = len(grads)

    def body(core_ref, *refs):
        for t in range(n):
            refs[2 * n + t][...] = (refs[t][...].astype(F32) + refs[n + t][...].astype(F32)).astype(MM)

    half = lambda g: (1, g.shape[1] // 2, g.shape[2])
    grid_spec = pltpu.PrefetchScalarGridSpec(
        num_scalar_prefetch=1, grid=(N_CHIPS,),
        in_specs=[pl.BlockSpec(half(g), lambda j, core_ref: (j, core_ref[0], 0)) for g in grads]
        + [pl.BlockSpec(half(g), lambda j, core_ref: (j, 0, 0)) for g in grads],
        out_specs=[pl.BlockSpec(half(g), lambda j, core_ref: (j, 0, 0)) for g in grads])
    return pl.pallas_call(
        body, grid_spec=grid_spec, out_shape=[jax.ShapeDtypeStruct((N_CHIPS,) + half(g)[1:], MM) for g in grads], name=name,
        compiler_params=_cparams(1))(core, *grads, *from_sibling)


def _owner_sum_call(chip_sums, from_chips, chip_core, name):
    n = len(chip_sums)
    steps = 4

    def body(ids_ref, *refs):
        for t in range(n):
            a, b = refs[t], refs[n + t]
            refs[2 * n + t][...] = ((a[0].astype(F32) + b[0].astype(F32)) + b[1].astype(F32)) + b[2].astype(F32)

    tile = lambda p: (p.shape[1] // steps, p.shape[2])
    grid_spec = pltpu.PrefetchScalarGridSpec(
        num_scalar_prefetch=1, grid=(steps,),
        in_specs=[pl.BlockSpec((1,) + tile(p), lambda i, ids_ref: (ids_ref[0], i, 0)) for p in chip_sums]
        + [pl.BlockSpec((3,) + tile(p), lambda i, ids_ref: (0, i, 0)) for p in chip_sums],
        out_specs=[pl.BlockSpec(tile(p), lambda i, ids_ref: (ids_ref[1] * steps + i, 0)) for p in chip_sums])
    return pl.pallas_call(
        body, grid_spec=grid_spec, out_shape=[jax.ShapeDtypeStruct((2 * p.shape[1], p.shape[2]), F32) for p in chip_sums],
        name=name, compiler_params=_cparams(1))(chip_core, *chip_sums, *from_chips)


def _sum8_call(parts):
    n, rows, cols = parts.shape
    tr = _row_tile(rows)

    def body(p_ref, o_ref):
        acc = p_ref[0]
        for k in range(1, n):
            acc = acc + p_ref[k]
        o_ref[...] = acc

    return pl.pallas_call(
        body, grid=(rows // tr,), in_specs=[_bs((n, tr, cols), lambda i: (0, i, 0))], out_specs=_bs((tr, cols), lambda i: (i, 0)),
        out_shape=jax.ShapeDtypeStruct((rows, cols), F32), name="sum_small_grads", compiler_params=_cparams(1))(parts)


_ANY = pl.BlockSpec(memory_space=pl.ANY)


def _half_rows(ref, lead, half, which):
    rows = pl.ds(pl.multiple_of(half * which, half), half)
    return ref.at[rows] if lead is None else ref.at[lead, rows]


_HBM = pl.BlockSpec(memory_space=pltpu.HBM)
_SEM = pl.BlockSpec(memory_space=pltpu.SEMAPHORE)
_ORDERED_EFFECT = pltpu.CompilerParams(has_side_effects=pltpu.SideEffectType.DATAFLOW_SIDE_EFFECTING)


_VMEM = pl.BlockSpec(memory_space=pltpu.VMEM)
_TOKEN = jax.ShapeDtypeStruct((8, LANES), F32)


def _in_hbm(a):
    return pltpu.with_memory_space_constraint(a, pltpu.HBM)


def _tie(small, token):
    return small + token[0:1, 0:1].reshape((1,) * small.ndim)


def _peer(k):
    x, y, c = lax.axis_index("x"), lax.axis_index("y"), lax.axis_index("c")
    bx, by, bc = (k >> 2) & 1, (k >> 1) & 1, k & 1
    return (x ^ bx if bx else x, y ^ by if by else y, c ^ bc if bc else c)


def _place_block_call(blk, index, name):
    rows, cols = blk.shape

    def body(idx_ref, b_ref, o_ref):
        o_ref[0] = b_ref[...]

    grid_spec = pltpu.PrefetchScalarGridSpec(
        num_scalar_prefetch=1, grid=(1,), in_specs=[pl.BlockSpec((rows, cols), lambda i, idx_ref: (0, 0))],
        out_specs=pl.BlockSpec((1, rows, cols), lambda i, idx_ref: (idx_ref[0], 0, 0)))
    return pl.pallas_call(body, grid_spec=grid_spec, out_shape=jax.ShapeDtypeStruct((8, rows, cols), blk.dtype), name=name,
                          compiler_params=_cparams(1))(index, blk)


def _small_gather_start_call(blk, buf, after, name):
    after = list(after)

    def body(*refs):
        b_ref, out_ref = refs[0], refs[2 + len(after)]
        send_sems, recv_sems, token = refs[3 + len(after):]
        x, y, c = lax.axis_index("x"), lax.axis_index("y"), lax.axis_index("c")
        for k in range(1, 8):
            pltpu.make_async_remote_copy(src_ref=b_ref, dst_ref=out_ref.at[4 * x + 2 * y + c], send_sem=send_sems.at[k - 1],
                                         recv_sem=recv_sems.at[k - 1], device_id=_peer(k), device_id_type=MESH_ID).start()
        token[...] = jnp.zeros_like(token)

    dma = pltpu.SemaphoreType.DMA
    return pl.pallas_call(
        body, out_shape=[pltpu.HBM(buf.shape, buf.dtype), dma((7,)), dma((7,)), _TOKEN],
        in_specs=[_HBM, _HBM] + [_ANY] * len(after), out_specs=[_HBM, _SEM, _SEM, _VMEM],
        input_output_aliases={1: 0}, name=name, compiler_params=_ORDERED_EFFECT)(_in_hbm(blk), _in_hbm(buf), *after)


def _small_gather_finish_call(blk, buf, send_sems, recv_sems, after, name):
    after = list(after)

    def body(*refs):
        b_ref, in_ref, send_ref, recv_ref = refs[:4]
        x, y, c = lax.axis_index("x"), lax.axis_index("y"), lax.axis_index("c")
        for k in range(1, 8):
            px, py, pc = _peer(k)
            pltpu.make_async_remote_copy(src_ref=b_ref, dst_ref=in_ref.at[4 * px + 2 * py + pc], send_sem=send_ref.at[k - 1],
                                         recv_sem=recv_ref.at[k - 1], device_id=(px, py, pc), device_id_type=MESH_ID).wait()

    return pl.pallas_call(
        body, out_shape=pltpu.HBM(buf.shape, buf.dtype), in_specs=[_HBM, _HBM, _SEM, _SEM] + [_ANY] * len(after),
        out_specs=_HBM, input_output_aliases={1: 0}, name=name, compiler_params=_ORDERED_EFFECT,
    )(_in_hbm(blk), buf, send_sems, recv_sems, *after)


def _pair_exchange_start_call(grads, name):
    n = len(grads)
    half = [g.shape[1] // 2 for g in grads]

    def body(*refs):
        srcs, outs = refs[:n], refs[n:2 * n]
        send_sems, recv_sems, token = refs[2 * n:]
        x, y, c = lax.axis_index("x"), lax.axis_index("y"), lax.axis_index("c")
        for t in range(n):
            pltpu.make_async_remote_copy(
                src_ref=srcs[t].at[:, pl.ds(pl.multiple_of(half[t] * (1 - c), half[t]), half[t])], dst_ref=outs[t],
                send_sem=send_sems.at[t], recv_sem=recv_sems.at[t], device_id=(x, y, 1 - c), device_id_type=MESH_ID).start()
        token[...] = jnp.zeros_like(token)

    dma = pltpu.SemaphoreType.DMA
    return pl.pallas_call(
        body, out_shape=[pltpu.HBM((g.shape[0], g.shape[1] // 2, g.shape[2]), g.dtype) for g in grads] + [dma((n,)), dma((n,)), _TOKEN],
        in_specs=[_HBM] * n, out_specs=[_HBM] * n + [_SEM, _SEM, _VMEM], name=name, compiler_params=_ORDERED_EFFECT,
    )(*[_in_hbm(g) for g in grads])


def _pair_exchange_finish_call(grads, bufs, send_sems, recv_sems, after, name):
    n = len(grads)
    after = list(after)
    half = [g.shape[1] // 2 for g in grads]

    def body(*refs):
        srcs, ins, send_ref, recv_ref = refs[:n], refs[n:2 * n], refs[2 * n], refs[2 * n + 1]
        x, y, c = lax.axis_index("x"), lax.axis_index("y"), lax.axis_index("c")
        for t in range(n):
            pltpu.make_async_remote_copy(
                src_ref=srcs[t].at[:, pl.ds(pl.multiple_of(half[t] * (1 - c), half[t]), half[t])], dst_ref=ins[t],
                send_sem=send_ref.at[t], recv_sem=recv_ref.at[t], device_id=(x, y, 1 - c), device_id_type=MESH_ID).wait()

    return pl.pallas_call(
        body, out_shape=[pltpu.HBM(b.shape, b.dtype) for b in bufs],
        in_specs=[_HBM] * (2 * n) + [_SEM, _SEM] + [_ANY] * len(after), out_specs=[_HBM] * n,
        input_output_aliases={n + t: t for t in range(n)}, name=name, compiler_params=_ORDERED_EFFECT,
    )(*[_in_hbm(g) for g in grads], *bufs, send_sems, recv_sems, *after)


def _chip_scatter_start_call(chip_sums, name):
    n = len(chip_sums)

    def body(*refs):
        srcs, outs = refs[:n], refs[n:2 * n]
        send_sems, recv_sems, token = refs[2 * n:]
        x, y, c = lax.axis_index("x"), lax.axis_index("y"), lax.axis_index("c")
        chips = [(1 - x, y), (x, 1 - y), (1 - x, 1 - y)]
        for k, (cx, cy) in enumerate(chips):
            for t in range(n):
                pltpu.make_async_remote_copy(
                    src_ref=srcs[t].at[2 * cx + cy], dst_ref=outs[t].at[k], send_sem=send_sems.at[3 * t + k],
                    recv_sem=recv_sems.at[3 * t + k], device_id=(cx, cy, c), device_id_type=MESH_ID).start()
        token[...] = jnp.zeros_like(token)

    dma = pltpu.SemaphoreType.DMA
    return pl.pallas_call(
        body, out_shape=[pltpu.HBM((3,) + p.shape[1:], p.dtype) for p in chip_sums] + [dma((3 * n,)), dma((3 * n,)), _TOKEN],
        in_specs=[_HBM] * n, out_specs=[_HBM] * n + [_SEM, _SEM, _VMEM], name=name, compiler_params=_ORDERED_EFFECT,
    )(*[_in_hbm(p) for p in chip_sums])


def _chip_scatter_finish_call(chip_sums, bufs, send_sems, recv_sems, after, name):
    n = len(chip_sums)
    after = list(after)

    def body(*refs):
        srcs, ins, send_ref, recv_ref = refs[:n], refs[n:2 * n], refs[2 * n], refs[2 * n + 1]
        x, y, c = lax.axis_index("x"), lax.axis_index("y"), lax.axis_index("c")
        chips = [(1 - x, y), (x, 1 - y), (1 - x, 1 - y)]
        for k, (cx, cy) in enumerate(chips):
            for t in range(n):
                pltpu.make_async_remote_copy(
                    src_ref=srcs[t].at[2 * cx + cy], dst_ref=ins[t].at[k], send_sem=send_ref.at[3 * t + k],
                    recv_sem=recv_ref.at[3 * t + k], device_id=(cx, cy, c), device_id_type=MESH_ID).wait()

    return pl.pallas_call(
        body, out_shape=[pltpu.HBM(b.shape, b.dtype) for b in bufs],
        in_specs=[_HBM] * (2 * n) + [_SEM, _SEM] + [_ANY] * len(after), out_specs=[_HBM] * n,
        input_output_aliases={n + t: t for t in range(n)}, name=name, compiler_params=_ORDERED_EFFECT,
    )(*[_in_hbm(p) for p in chip_sums], *bufs, send_sems, recv_sems, *after)


def _place_own_call(mine, chip_core, name):
    n = len(mine)

    def body(ids_ref, *refs):
        for t in range(n):
            refs[n + t][0] = refs[t][...]

    def imap_out(s):
        pad = (0,) * (s.ndim - 1)
        return lambda i, ids_ref: (ids_ref[0], ids_ref[1]) + pad

    grid_spec = pltpu.PrefetchScalarGridSpec(
        num_scalar_prefetch=1, grid=(1,), in_specs=[pl.BlockSpec(s.shape, lambda i, ids_ref, k=s.ndim: (0,) * k) for s in mine],
        out_specs=[pl.BlockSpec((1,) + s.shape, imap_out(s)) for s in mine])
    return pl.pallas_call(
        body, grid_spec=grid_spec,
        out_shape=[jax.ShapeDtypeStruct((N_CHIPS, 2 * s.shape[0]) + s.shape[1:], s.dtype) for s in mine],
        name=name, compiler_params=_cparams(1))(chip_core, *mine)


def _gather_start_call(mine, bufs, after, name):
    n = len(mine)
    half = [s.shape[0] for s in mine]

    def body(*refs):
        srcs, outs = refs[:n], refs[2 * n + 1:3 * n + 1]
        send_sems, recv_sib, recv_ici, token = refs[3 * n + 1:]
        x, y, c = lax.axis_index("x"), lax.axis_index("y"), lax.axis_index("c")
        chips = [(1 - x, y), (x, 1 - y), (1 - x, 1 - y)]
        for t in range(n):
            dst = _half_rows(outs[t], 2 * x + y, half[t], c)
            pltpu.make_async_remote_copy(src_ref=srcs[t], dst_ref=dst, send_sem=send_sems.at[4 * t], recv_sem=recv_sib.at[t],
                                         device_id=(x, y, 1 - c), device_id_type=MESH_ID).start()
            for j, chip in enumerate(chips):
                pltpu.make_async_remote_copy(src_ref=srcs[t], dst_ref=dst, send_sem=send_sems.at[4 * t + 1 + j],
                                             recv_sem=recv_ici.at[3 * t + j], device_id=(*chip, c), device_id_type=MESH_ID).start()
        token[...] = jnp.zeros_like(token)

    dma = pltpu.SemaphoreType.DMA
    return pl.pallas_call(
        body, out_shape=[pltpu.HBM(b.shape, b.dtype) for b in bufs] + [dma((4 * n,)), dma((n,)), dma((3 * n,)), _TOKEN],
        in_specs=[_HBM] * (2 * n) + [_ANY], out_specs=[_HBM] * n + [_SEM] * 3 + [_VMEM],
        input_output_aliases={n + t: t for t in range(n)}, name=name, compiler_params=_ORDERED_EFFECT,
    )(*[_in_hbm(s) for s in mine], *[_in_hbm(b) for b in bufs], after)


def _gather_forward_call(bufs, recv_ici, after, name):
    n = len(bufs)
    half = [b.shape[1] // 2 for b in bufs]

    def body(*refs):
        ins, recv_ici_ref = refs[:n], refs[n]
        outs = refs[n + 2:2 * n + 2]
        send_fwd, recv_fwd, token = refs[2 * n + 2:]
        x, y, c = lax.axis_index("x"), lax.axis_index("y"), lax.axis_index("c")
        chips = [(1 - x, y), (x, 1 - y), (1 - x, 1 - y)]
        for j, (cx, cy) in enumerate(chips):
            for t in range(n):
                landed = _half_rows(ins[t], 2 * cx + cy, half[t], c)
                dst = _half_rows(outs[t], 2 * cx + cy, half[t], c)
                pltpu.make_async_remote_copy(src_ref=landed, dst_ref=landed, send_sem=send_fwd.at[3 * t + j],
                                             recv_sem=recv_ici_ref.at[3 * t + j], device_id=(cx, cy, c),
                                             device_id_type=MESH_ID).wait_recv()
                pltpu.make_async_remote_copy(src_ref=landed, dst_ref=dst, send_sem=send_fwd.at[3 * t + j],
                                             recv_sem=recv_fwd.at[3 * t + j], device_id=(x, y, 1 - c),
                                             device_id_type=MESH_ID).start()
        token[...] = jnp.zeros_like(token)

    dma = pltpu.SemaphoreType.DMA
    return pl.pallas_call(
        body, out_shape=[pltpu.HBM(b.shape, b.dtype) for b in bufs] + [dma((3 * n,)), dma((3 * n,)), _TOKEN],
        in_specs=[_HBM] * n + [_SEM, _ANY], out_specs=[_HBM] * n + [_SEM] * 2 + [_VMEM],
        input_output_aliases={t: t for t in range(n)}, name=name, compiler_params=_ORDERED_EFFECT,
    )(*bufs, recv_ici, after)


def _gather_finish_call(shards, bufs, send_sems, recv_sib, send_fwd, recv_fwd, after, name):
    n = len(bufs)
    half = [b.shape[1] // 2 for b in bufs]

    def body(*refs):
        srcs, ins = refs[:n], refs[n:2 * n]
        send_ref, recv_sib_ref, send_fwd_ref, recv_fwd_ref = refs[2 * n:2 * n + 4]
        x, y, c = lax.axis_index("x"), lax.axis_index("y"), lax.axis_index("c")
        chips = [(1 - x, y), (x, 1 - y), (1 - x, 1 - y)]
        sibling = (x, y, 1 - c)
        for t in range(n):
            for k in range(4):
                pltpu.make_async_remote_copy(src_ref=srcs[t], dst_ref=srcs[t], send_sem=send_ref.at[4 * t + k],
                                             recv_sem=recv_sib_ref.at[t], device_id=sibling, device_id_type=MESH_ID).wait_send()
            from_sibling = _half_rows(ins[t], 2 * x + y, half[t], 1 - c)
            pltpu.make_async_remote_copy(src_ref=from_sibling, dst_ref=from_sibling, send_sem=send_ref.at[4 * t],
                                         recv_sem=recv_sib_ref.at[t], device_id=sibling, device_id_type=MESH_ID).wait_recv()
            for j, (cx, cy) in enumerate(chips):
                sent = _half_rows(ins[t], 2 * cx + cy, half[t], c)
                passed = _half_rows(ins[t], 2 * cx + cy, half[t], 1 - c)
                pltpu.make_async_remote_copy(src_ref=sent, dst_ref=passed, send_sem=send_fwd_ref.at[3 * t + j],
                                             recv_sem=recv_fwd_ref.at[3 * t + j], device_id=sibling, device_id_type=MESH_ID).wait()

    return pl.pallas_call(
        body, out_shape=[pltpu.HBM(b.shape, b.dtype) for b in bufs],
        in_specs=[_HBM] * (2 * n) + [_SEM] * 4 + [_ANY], out_specs=[_HBM] * n,
        input_output_aliases={n + t: t for t in range(n)}, name=name, compiler_params=_ORDERED_EFFECT,
    )(*[_in_hbm(s) for s in shards], *bufs, send_sems, recv_sib, send_fwd, recv_fwd, after)


def _pair_gather_call(bufs, name):
    n = len(bufs)
    half = [b.shape[0] // 2 for b in bufs]

    def body(*refs):
        srcs, outs, send_sems, recv_sems = refs[:n], refs[n:2 * n], refs[2 * n], refs[2 * n + 1]
        x, y, c = lax.axis_index("x"), lax.axis_index("y"), lax.axis_index("c")
        for t in range(n):
            pltpu.make_async_remote_copy(
                src_ref=_half_rows(srcs[t], None, half[t], c), dst_ref=_half_rows(outs[t], None, half[t], c),
                send_sem=send_sems.at[t], recv_sem=recv_sems.at[t], device_id=(x, y, 1 - c), device_id_type=MESH_ID).start()
        for t in range(n):
            pltpu.make_async_remote_copy(
                src_ref=_half_rows(srcs[t], None, half[t], c), dst_ref=_half_rows(outs[t], None, half[t], 1 - c),
                send_sem=send_sems.at[t], recv_sem=recv_sems.at[t], device_id=(x, y, 1 - c), device_id_type=MESH_ID).wait()

    return pl.pallas_call(
        body, out_shape=[jax.ShapeDtypeStruct(b.shape, b.dtype) for b in bufs], in_specs=[_ANY] * n, out_specs=[_ANY] * n,
        input_output_aliases={t: t for t in range(n)},
        scratch_shapes=[pltpu.SemaphoreType.DMA((n,)), pltpu.SemaphoreType.DMA((n,))], name=name)(*bufs)


def _pair_gather_start_call(bufs, name):
    n = len(bufs)
    half = [b.shape[0] // 2 for b in bufs]

    def body(*refs):
        srcs, outs = refs[:n], refs[n:2 * n]
        send_sems, recv_sems, token = refs[2 * n:]
        x, y, c = lax.axis_index("x"), lax.axis_index("y"), lax.axis_index("c")
        for t in range(n):
            pltpu.make_async_remote_copy(
                src_ref=_half_rows(srcs[t], None, half[t], c), dst_ref=_half_rows(outs[t], None, half[t], c),
                send_sem=send_sems.at[t], recv_sem=recv_sems.at[t], device_id=(x, y, 1 - c), device_id_type=MESH_ID).start()
        token[...] = jnp.zeros_like(token)

    dma = pltpu.SemaphoreType.DMA
    return pl.pallas_call(
        body, out_shape=[pltpu.HBM(b.shape, b.dtype) for b in bufs] + [dma((n,)), dma((n,)), _TOKEN],
        in_specs=[_HBM] * n, out_specs=[_HBM] * n + [_SEM, _SEM, _VMEM], input_output_aliases={t: t for t in range(n)},
        name=name, compiler_params=_ORDERED_EFFECT)(*[_in_hbm(b) for b in bufs])


def _pair_gather_finish_call(bufs, send_sems, recv_sems, after, name):
    n = len(bufs)
    after = list(after)
    half = [b.shape[0] // 2 for b in bufs]

    def body(*refs):
        ins, send_ref, recv_ref = refs[:n], refs[n], refs[n + 1]
        x, y, c = lax.axis_index("x"), lax.axis_index("y"), lax.axis_index("c")
        for t in range(n):
            pltpu.make_async_remote_copy(
                src_ref=_half_rows(ins[t], None, half[t], c), dst_ref=_half_rows(ins[t], None, half[t], 1 - c),
                send_sem=send_ref.at[t], recv_sem=recv_ref.at[t], device_id=(x, y, 1 - c), device_id_type=MESH_ID).wait()

    return pl.pallas_call(
        body, out_shape=[pltpu.HBM(b.shape, b.dtype) for b in bufs], in_specs=[_HBM] * n + [_SEM, _SEM] + [_ANY] * len(after),
        out_specs=[_HBM] * n, input_output_aliases={t: t for t in range(n)}, name=name, compiler_params=_ORDERED_EFFECT,
    )(*bufs, send_sems, recv_sems, *after)


def _pack_rows(flats, dtype, row_multiple):
    flat = jnp.concatenate([f.reshape(-1).astype(dtype) for f in flats])
    n = flat.shape[0]
    rows = -(-n // PACK_W)
    rows = -(-rows // row_multiple) * row_multiple
    return jnp.pad(flat, (0, rows * PACK_W - n)).reshape(rows, PACK_W)


def _unpack(flat, shapes):
    out, off = [], 0
    for shp in shapes:
        n = math.prod(shp)
        out.append(flat[off:off + n].reshape(shp))
        off += n
    return out


_W_IN_SEGMENTS = ((R_ML, R_END, OFF_ML), (R_SG, R_ML, OFF_SG), (R_CV, R_SGI, OFF_CV), (R_SGI, R_MQ, OFF_SGI), (R_MQ, R_SG, OFF_MQ),
                  (R_CQ, R_CKV, OFF_CQ), (R_CKV, R_KR, OFF_CKV), (R_KR, R_CV, OFF_KR + NOPE))
W_IN_SHARD = R_END // N_CHIPS


def _realign_call(wg):
    tr = 128

    def body(w_ref, o_ref):
        pieces, pos = [], 0
        for r0, r1, a0 in _W_IN_SEGMENTS:
            if a0 > pos:
                pieces.append(jnp.zeros((tr, a0 - pos), o_ref.dtype))
            while r0 < r1:
                j = r0 // W_IN_SHARD
                hi = min(r1, (j + 1) * W_IN_SHARD)
                pieces.append(w_ref[j, :, r0 - j * W_IN_SHARD:hi - j * W_IN_SHARD])
                a0, r0 = a0 + hi - r0, hi
            pos = a0
        pieces.append(jnp.zeros((tr, NP - pos), o_ref.dtype))
        o_ref[...] = jnp.concatenate(pieces, axis=1)

    return pl.pallas_call(
        body, grid=(D // tr,), in_specs=[_bs((N_CHIPS, tr, W_IN_SHARD), lambda i: (0, i, 0))],
        out_specs=_bs((tr, NP), lambda i: (i, 0)), out_shape=jax.ShapeDtypeStruct((D, NP), wg.dtype),
        name="w_in_realign", compiler_params=_cparams(1))(wg)


def _unalign_call(dw, out_dtype):
    tr = 128
    by_ref = sorted(_W_IN_SEGMENTS)

    def body(dw_ref, o_ref):
        for j in range(N_CHIPS):
            lo_j, hi_j = j * W_IN_SHARD, (j + 1) * W_IN_SHARD
            pieces = []
            for r0, r1, a0 in by_ref:
                lo, hi = max(r0, lo_j), min(r1, hi_j)
                if lo < hi:
                    pieces.append(dw_ref[:, a0 + lo - r0:a0 + hi - r0])
            o_ref[j] = jnp.concatenate(pieces, axis=1).astype(o_ref.dtype)

    return pl.pallas_call(
        body, grid=(D // tr,), in_specs=[_bs((tr, NP), lambda i: (i, 0))],
        out_specs=_bs((N_CHIPS, tr, W_IN_SHARD), lambda i: (0, i, 0)),
        out_shape=jax.ShapeDtypeStruct((N_CHIPS, D, W_IN_SHARD), out_dtype), name="w_in_unalign", compiler_params=_cparams(1))(dw)


def _wuq_to_heads(w):
    w3 = w.reshape(QL, H, QKH)
    w3 = jnp.pad(w3, ((0, 0), (0, 0), (0, LANES - QKH)))
    return jnp.transpose(w3, (1, 0, 2))


def _wuq_from_heads(wh):
    return jnp.transpose(wh[:, :, :QKH], (1, 0, 2)).reshape(QL, H * QKH)


def _wukv_to_heads(w):
    w3 = w.reshape(KVL, H, NOPE + VH)
    wkn = jnp.transpose(jnp.pad(w3[:, :, :NOPE], ((0, 0), (0, 0), (0, LANES - NOPE))), (1, 0, 2))
    wv3 = w3[:, :, NOPE:]
    z = jnp.zeros((KVL, VH), w.dtype)
    cols = []
    for h in range(H):
        cols += [wv3[:, h], z] if h % 2 == 0 else [z, wv3[:, h]]
    return wkn, jnp.concatenate(cols, axis=1)


def _wukv_from_heads(wkn, wv):
    kn = jnp.transpose(wkn[:, :, :NOPE], (1, 0, 2))
    vs = jnp.stack([wv[:, LANES * h + VH * (h % 2):LANES * h + VH * (h % 2) + VH] for h in range(H)], axis=1)
    return jnp.concatenate([kn, vs], axis=2).reshape(KVL, H * (NOPE + VH))


def _layer_fwd(x, mem, tabs, p):
    proj, h = _proj_call(x, p["norm_g"], p["w_in"])
    if p.get("late") is not None:
        p = dict(p, **p["late"](proj))
    q, k, v = _mla_prep_call(proj, tabs, p["cq_g"], p["ckv_g"], p["qg"], p["kg"], p["wuq"], p["wkn"], p["wv"])
    ya, attn_o, attn_lse = _attn_call(q, k, v, proj)
    bm = p["bm"]
    if p.get("after_attn") is not None:
        bm = _tie(bm, p["after_attn"](ya))
    yb = _conv_call(proj, p["conv_w"], p["conv_b"])
    yc = _sg_call(proj, p["ln_g"], p["ln_b"], p["ws"], p["bs"])
    mk, mv = _memkv_call(mem, p["mem_g"], p["wm"], p["mkg"])
    yd = _mem_call(proj, mk, mv, p["mqg"])
    out = _merge_call((ya, yb, yc, yd), proj, bm, p["wb"], p["wo"], x)
    return out, dict(p=p, x=x, proj=proj, h=h, q=q, k=k, v=v, attn_o=attn_o, attn_lse=attn_lse, ys=(ya, yb, yc, yd), mk=mk, mv=mv)


def _layer_bwd(dout, mem, tabs, p, sv, start_after=None, on_rest_grads=None, on_grads=None):
    proj = sv["proj"]
    bm = p["bm"] if start_after is None else _tie(p["bm"], start_after)
    dya, dyb, dyc, dyd, dml, dbm, dwb, dwo = _merge_bwd_call(sv["ys"], proj, bm, p["wb"], p["wo"], dout)
    dq, dk, dv, dsg_a = _attn_bwd_call(sv["q"], sv["k"], sv["v"], proj, dya, sv["attn_o"], sv["attn_lse"])
    dlat, dcqg, dckvg, dqg, dkg, dwuq, dwkn, dwv = _mla_prep_bwd_call(
        proj, tabs, p["cq_g"], p["ckv_g"], p["qg"], p["kg"], p["wuq"], p["wkn"], p["wv"], dq, dk, dv)
    dbg, dcg, dxi, dsg_b, dcw, dcb = _conv_bwd_call(proj, p["conv_w"], p["conv_b"], dyb)
    duv, dsg_c, dlg, dlb, dws, dbs = _sg_bwd_call(proj, p["ln_g"], p["ln_b"], p["ws"], p["bs"], dyc)
    dmq, dsg_d, dmk, dmv, dmqg = _mem_bwd_call(proj, sv["mk"], sv["mv"], p["mqg"], dyd)
    dmem_g, dwm, dmkg = _memkv_bwd_call(mem, p["mem_g"], p["wm"], p["mkg"], dmk, dmv)
    grads = dict(cq_norm_g=dcqg, ckv_norm_g=dckvg, mla_q_norm_g=dqg[:, :QKH], mla_k_norm_g=dkg[:, :QKH],
                 conv_w=dcw, conv_b=dcb, sg_ln_g=dlg, sg_ln_b=dlb, w_spatial=dws, b_spatial=dbs,
                 mem_norm_g=dmem_g, mem_q_norm_g=dmqg, mem_k_norm_g=dmkg, b_merge=dbm,
                 wuq_heads=dwuq, wkn_heads=dwkn, wv_heads=dwv, w_mem_kv=dwm, w_branch_chips=dwb, w_out=dwo)
    started = [on_rest_grads(grads)] if on_rest_grads is not None else []
    dproj, off = dml, NB * D
    for piece in (dsg_a, dsg_b, dsg_c, dsg_d, dbg, dcg, dxi, duv, dmq, dlat):
        dproj = lax.dynamic_update_slice(dproj, piece, (0, off))
        off += piece.shape[1]
    grads["w_in_aligned"] = _dw_call(sv["h"], dproj, started)
    tokens = on_grads(grads) if on_grads is not None else ()
    dx, dnorm_g = _dh_call(dproj, p["w_in"], sv["x"], p["norm_g"], dout, tokens)
    grads["norm_g"] = dnorm_g
    return dx, grads


def _chips_to_cols(a):
    return jnp.concatenate([a[j] for j in range(N_CHIPS)], axis=1)


def _cols_to_chips(a):
    cols = a.shape[1] // N_CHIPS
    return jnp.stack([a[:, cols * j:cols * (j + 1)] for j in range(N_CHIPS)])


def _layer_params_first(l, rep, w_in_gathered, conv_w, b_merge):
    pad_g = lambda g: jnp.pad(g, (0, LANES - QKH)).reshape(1, LANES)
    return dict(
        norm_g=rep["norm_g"][l].reshape(1, D), w_in=_realign_call(w_in_gathered),
        cq_g=rep["cq_norm_g"][l].reshape(1, QL), ckv_g=rep["ckv_norm_g"][l].reshape(1, KVL),
        qg=pad_g(rep["mla_q_norm_g"][l]), kg=pad_g(rep["mla_k_norm_g"][l]),
        conv_w=conv_w, conv_b=rep["conv_b"][l].reshape(1, CW),
        ln_g=rep["sg_ln_g"][l].reshape(1, SGW), ln_b=rep["sg_ln_b"][l].reshape(1, SGW),
        ws=rep["w_spatial"][l], bs=rep["b_spatial"][l].reshape(SGG, SGC, 1),
        mem_g=rep["mem_norm_g"][l].reshape(1, D),
        mqg=rep["mem_q_norm_g"][l].reshape(1, MHD), mkg=rep["mem_k_norm_g"][l].reshape(1, MHD), bm=b_merge)


def _layer_params_rest(gathered):
    wkn, wv = _wukv_to_heads(_chips_to_cols(gathered["w_ukv"]))
    return dict(wuq=_wuq_to_heads(_chips_to_cols(gathered["w_uq"])), wkn=wkn, wv=wv,
                wm=gathered["w_mem_kv"].reshape(D, 2 * MH * MHD), wb=gathered["w_branch"], wo=gathered["w_out"].reshape(D, D))


def _layer_params(l, rep, gathered, conv_w, b_merge):
    return dict(_layer_params_first(l, rep, gathered["w_in"], conv_w, b_merge), **_layer_params_rest(gathered))


def _forward_backward(x, mem, pos, target, params, bwd_hooks=None):
    tabs = _rope_tables(pos)
    params = list(params)
    saved = []
    act = x
    for l in range(DEPTH):
        if callable(params[l]):
            params[l] = params[l](saved[-1], act)
        act, sv = _layer_fwd(act, mem, tabs, params[l])
        saved.append(sv)
    dy, sq = _loss_call(act, target)
    grads = [None] * DEPTH
    token = None
    for l in reversed(range(DEPTH)):
        hooks = dict(bwd_hooks[l]) if bwd_hooks else {}
        after_layer = hooks.pop("after_layer", None)
        dy, grads[l] = _layer_bwd(dy, mem, tabs, saved[l]["p"], saved[l], start_after=token, **hooks)
        token = after_layer(dy) if after_layer is not None else None
    return sq, dy, grads


_SHARDED_MM = ("w_in", "w_branch", "w_out", "w_mem_kv", "w_uq", "w_ukv")
_SHARDED_F32 = ("conv_w", "b_merge")
_REPLICATED = ("norm_g", "cq_norm_g", "ckv_norm_g", "mla_q_norm_g", "mla_k_norm_g", "conv_b", "sg_ln_g", "sg_ln_b",
               "w_spatial", "b_spatial", "mem_norm_g", "mem_q_norm_g", "mem_k_norm_g")
_ALL_REDUCED = _REPLICATED + _SHARDED_F32
_WEIGHTS = ("norm_g", "w_in", "cq_norm_g", "ckv_norm_g", "w_uq", "w_ukv", "mla_q_norm_g", "mla_k_norm_g", "conv_w", "conv_b",
            "sg_ln_g", "sg_ln_b", "w_spatial", "b_spatial", "mem_norm_g", "w_mem_kv", "mem_q_norm_g", "mem_k_norm_g",
            "b_merge", "w_branch", "w_out")
_SMALL = tuple(n for n in _WEIGHTS if n not in _SHARDED_MM)


class _SmallGather:
    def __init__(self, blk, after, tag):
        self.blk, self.tag = blk, tag
        x, y, c = lax.axis_index("x"), lax.axis_index("y"), lax.axis_index("c")
        own = _place_block_call(blk, (4 * x + 2 * y + c).astype(jnp.int32).reshape(1), tag + "place_own")
        self.buf, self.send, self.recv, self.token = _small_gather_start_call(blk, own, after, tag + "start")

    def finish(self, after):
        return _small_gather_finish_call(self.blk, self.buf, self.send, self.recv, after, self.tag + "finish")


def _small_sharded_weights(w, got):
    names = _SHARDED_F32
    per_chip = [_unpack(got[2 * j].reshape(-1), [w[n].shape for n in names]) for j in range(N_CHIPS)]
    return {n: jnp.concatenate([per_chip[j][t] for j in range(N_CHIPS)], axis=2) for t, n in enumerate(names)}


class _Gather:
    def __init__(self, w, layer, names, after, tag):
        self.names, self.tag = names, tag
        x, y, c = lax.axis_index("x"), lax.axis_index("y"), lax.axis_index("c")
        chip_core = jnp.stack([2 * x + y, c]).astype(jnp.int32)
        halves = [w[n].shape[1] // 2 for n in names]
        self.srcs = [lax.dynamic_slice_in_dim(w[n][layer], c * h, h, axis=0).astype(MM) for n, h in zip(names, halves)]
        k = len(names)
        out = _gather_start_call(self.srcs, _place_own_call(self.srcs, chip_core, tag + "place_own"), after, tag + "start")
        self.bufs, self.send, self.recv_sib, self.recv_ici, self.token = out[:k], out[k], out[k + 1], out[k + 2], out[k + 3]

    def pass_on(self, after):
        k = len(self.names)
        out = _gather_forward_call(self.bufs, self.recv_ici, after, self.tag + "forward")
        self.bufs, self.send_fwd, self.recv_fwd = out[:k], out[k], out[k + 1]
        return out[k + 2]

    def finish(self, after):
        got = _gather_finish_call(self.srcs, self.bufs, self.send, self.recv_sib, self.send_fwd, self.recv_fwd, after,
                                  self.tag + "finish")
        return dict(zip(self.names, got))


class _ReduceScatter:
    SLABS = dict(
        w_in=lambda g: _unalign_call(g["w_in_aligned"], MM),
        w_branch=lambda g: g["w_branch_chips"].reshape(N_CHIPS, NB * BW, D // N_CHIPS),
        w_out=lambda g: g["w_out"].reshape(N_CHIPS, D // N_CHIPS, D),
        w_mem_kv=lambda g: g["w_mem_kv"].reshape(N_CHIPS, D // N_CHIPS, 2 * MH * MHD),
        w_uq=lambda g: _cols_to_chips(_wuq_from_heads(g["wuq_heads"])),
        w_ukv=lambda g: _cols_to_chips(_wukv_from_heads(g["wkn_heads"], g["wv_heads"])))

    def __init__(self, tag, names):
        self.tag, self.names = tag, names

    def exchange(self, grads):
        self.tensors = [self.SLABS[n](grads) for n in self.names]
        n = len(self.tensors)
        out = _pair_exchange_start_call(self.tensors, self.tag + "exchange_start")
        self.ex_bufs, self.ex_send, self.ex_recv = out[:n], out[n], out[n + 1]
        return out[n + 2]

    def scatter(self, after):
        n = len(self.tensors)
        c = lax.axis_index("c")
        from_sibling = _pair_exchange_finish_call(self.tensors, self.ex_bufs, self.ex_send, self.ex_recv, after,
                                                  self.tag + "exchange_finish")
        self.chip_sums = _pair_sum_call(self.tensors, from_sibling, c.astype(jnp.int32).reshape(1), self.tag + "pair_sum")
        out = _chip_scatter_start_call(self.chip_sums, self.tag + "scatter_start")
        self.bufs, self.send_sems, self.recv_sems, self.token = out[:n], out[n], out[n + 1], out[n + 2]
        return self.token

    def finish(self, after):
        x, y, c = lax.axis_index("x"), lax.axis_index("y"), lax.axis_index("c")
        chip_core = jnp.stack([2 * x + y, c]).astype(jnp.int32)
        from_chips = _chip_scatter_finish_call(self.chip_sums, self.bufs, self.send_sems, self.recv_sems, after,
                                               self.tag + "scatter_finish")
        self.mine = _owner_sum_call(self.chip_sums, from_chips, chip_core, self.tag + "owner_sum")
        return dict(zip(self.names, _pair_gather_call(self.mine, self.tag + "pair_gather")))

    def finish_but_swap(self, after):
        n = len(self.names)
        x, y, c = lax.axis_index("x"), lax.axis_index("y"), lax.axis_index("c")
        chip_core = jnp.stack([2 * x + y, c]).astype(jnp.int32)
        from_chips = _chip_scatter_finish_call(self.chip_sums, self.bufs, self.send_sems, self.recv_sems, after,
                                               self.tag + "scatter_finish")
        mine = _owner_sum_call(self.chip_sums, from_chips, chip_core, self.tag + "owner_sum")
        out = _pair_gather_start_call(mine, self.tag + "pair_gather_start")
        self.pg_bufs, self.pg_send, self.pg_recv = out[:n], out[n], out[n + 1]
        return out[n + 2]

    def swapped(self, after):
        got = _pair_gather_finish_call(self.pg_bufs, self.pg_send, self.pg_recv, after, self.tag + "pair_gather_finish")
        return dict(zip(self.names, got))


def _small_sums(shapes, sq, got):
    total = _sum8_call(got).reshape(-1)
    parts = _unpack(total, [shapes[n] for n in _ALL_REDUCED] + [sq.shape])
    out = dict(zip(_ALL_REDUCED, parts))
    sq_total = parts[-1]
    chip = 2 * lax.axis_index("x") + lax.axis_index("y")
    for n in _SHARDED_F32:
        size = out[n].shape[2] // N_CHIPS
        out[n] = lax.dynamic_slice_in_dim(out[n], chip * size, size, axis=2)
    return out, sq_total


def _adamw_small(w, g, m, v):
    pick = lambda t: [t[n] for n in _SMALL]
    out = _adamw_small_call(pick(w), pick(g), pick(m), pick(v), "adamw_small")
    return tuple({n: out[3 * t + k] for t, n in enumerate(_SMALL)} for k in range(3))


def kernel(x, mem, positions, norm_g, w_in, cq_norm_g, ckv_norm_g, w_uq, w_ukv, mla_q_norm_g, mla_k_norm_g, conv_w, conv_b, sg_ln_g, sg_ln_b, w_spatial, b_spatial, mem_norm_g, w_mem_kv, mem_q_norm_g, mem_k_norm_g, b_merge, w_branch, w_out, loss_target, m_norm_g, m_w_in, m_cq_norm_g, m_ckv_norm_g, m_w_uq, m_w_ukv, m_mla_q_norm_g, m_mla_k_norm_g, m_conv_w, m_conv_b, m_sg_ln_g, m_sg_ln_b, m_w_spatial, m_b_spatial, m_mem_norm_g, m_w_mem_kv, m_mem_q_norm_g, m_mem_k_norm_g, m_b_merge, m_w_branch, m_w_out, v_norm_g, v_w_in, v_cq_norm_g, v_ckv_norm_g, v_w_uq, v_w_ukv, v_mla_q_norm_g, v_mla_k_norm_g, v_conv_w, v_conv_b, v_sg_ln_g, v_sg_ln_b, v_w_spatial, v_b_spatial, v_mem_norm_g, v_w_mem_kv, v_mem_q_norm_g, v_mem_k_norm_g, v_b_merge, v_w_branch, v_w_out):
    w = dict(norm_g=norm_g, w_in=w_in, cq_norm_g=cq_norm_g, ckv_norm_g=ckv_norm_g, w_uq=w_uq, w_ukv=w_ukv,
             mla_q_norm_g=mla_q_norm_g, mla_k_norm_g=mla_k_norm_g, conv_w=conv_w, conv_b=conv_b, sg_ln_g=sg_ln_g,
             sg_ln_b=sg_ln_b, w_spatial=w_spatial, b_spatial=b_spatial, mem_norm_g=mem_norm_g, w_mem_kv=w_mem_kv,
             mem_q_norm_g=mem_q_norm_g, mem_k_norm_g=mem_k_norm_g, b_merge=b_merge, w_branch=w_branch, w_out=w_out)
    m = dict(norm_g=m_norm_g, w_in=m_w_in, cq_norm_g=m_cq_norm_g, ckv_norm_g=m_ckv_norm_g, w_uq=m_w_uq, w_ukv=m_w_ukv,
             mla_q_norm_g=m_mla_q_norm_g, mla_k_norm_g=m_mla_k_norm_g, conv_w=m_conv_w, conv_b=m_conv_b, sg_ln_g=m_sg_ln_g,
             sg_ln_b=m_sg_ln_b, w_spatial=m_w_spatial, b_spatial=m_b_spatial, mem_norm_g=m_mem_norm_g, w_mem_kv=m_w_mem_kv,
             mem_q_norm_g=m_mem_q_norm_g, mem_k_norm_g=m_mem_k_norm_g, b_merge=m_b_merge, w_branch=m_w_branch, w_out=m_w_out)
    v = dict(norm_g=v_norm_g, w_in=v_w_in, cq_norm_g=v_cq_norm_g, ckv_norm_g=v_ckv_norm_g, w_uq=v_w_uq, w_ukv=v_w_ukv,
             mla_q_norm_g=v_mla_q_norm_g, mla_k_norm_g=v_mla_k_norm_g, conv_w=v_conv_w, conv_b=v_conv_b, sg_ln_g=v_sg_ln_g,
             sg_ln_b=v_sg_ln_b, w_spatial=v_w_spatial, b_spatial=v_b_spatial, mem_norm_g=v_mem_norm_g, w_mem_kv=v_w_mem_kv,
             mem_q_norm_g=v_mem_q_norm_g, mem_k_norm_g=v_mem_k_norm_g, b_merge=v_b_merge, w_branch=v_w_branch, w_out=v_w_out)

    chip_core = jnp.stack([2 * lax.axis_index("x") + lax.axis_index("y"), lax.axis_index("c")]).astype(jnp.int32)

    first = _Gather(w, 0, ("w_in",), chip_core, "gather_l0_w_in_")
    rest = _Gather(w, 0, _SHARDED_MM[1:], first.token, "gather_l0_rest_")
    small_on_its_way = _SmallGather(_pack_rows([w[n] for n in _SHARDED_F32], F32, 8), [rest.token], "gather_small_weights_")
    later = _Gather(w, 1, _SHARDED_MM, small_on_its_way.token, "gather_l1_")
    w_in0 = first.finish(first.pass_on(later.token))["w_in"]
    small = {}

    def rest_of_layer0(proj0):
        landed = _layer_params_rest(rest.finish(rest.pass_on(proj0)))
        small.update(_small_sharded_weights(w, small_on_its_way.finish([landed["wo"]])))
        return dict(landed, conv_w=small["conv_w"][0], bm=small["b_merge"][0])

    def layer1_params(saved0, act0):
        return _layer_params(1, w, later.finish(act0), small["conv_w"][1], small["b_merge"][1])

    params0 = _layer_params_first(0, w, w_in0, None, None)
    params = [dict(params0, late=rest_of_layer0, after_attn=later.pass_on), layer1_params]
    others = _SHARDED_MM[1:]
    rs1 = _ReduceScatter("rs_l1_", _SHARDED_MM)
    rs0_rest, rs0_w_in = _ReduceScatter("rs_l0_rest_", others), _ReduceScatter("rs_l0_w_in_", ("w_in",))

    def layer0_grads_done(grads):
        return [rs0_rest.scatter([grads["w_in_aligned"]]), rs0_w_in.exchange(grads)]

    hooks = [dict(on_rest_grads=rs0_rest.exchange, on_grads=layer0_grads_done),
             dict(on_grads=lambda grads: [rs1.exchange(grads)], after_layer=lambda dy: rs1.scatter([dy]))]
    sq, grad_x, layer_grads = _forward_backward(x[0], mem[0], positions[0], loss_target[0], params, hooks)

    layered = [layer_grads[l][n] for n in _ALL_REDUCED for l in range(DEPTH)]
    small_grads = _SmallGather(_pack_rows(layered + [sq], F32, 64), [grad_x], "gather_small_grads_")
    scattering = rs0_w_in.scatter([grad_x, small_grads.token])
    swapping1 = rs1.finish_but_swap([scattering])
    swapping0 = rs0_rest.finish_but_swap([scattering, swapping1])
    shard_grads = {1: rs1.swapped([swapping0])}
    as3d = lambda a: a.reshape(DEPTH, -1, a.shape[-1])
    as2d = lambda a: a.reshape(-1, a.shape[-1])
    big = lambda t: [as3d(t[n]) for n in others]
    turned = lambda t: [jnp.swapaxes(t["w_in"], 1, 2)]
    assert W_IN_SHARD % (8 * 7) == 0

    def update_w_in(l, grad, prev):
        return _adamw_layer_call(l, turned(w), [grad.T], turned(m), turned(v), prev, [], "adamw_w_in_l%d" % l, steps=7)

    def update_others(l, prev):
        return _adamw_layer_call(l, big(w), [as2d(shard_grads[l][n]) for n in others], big(m), big(v), prev, [], "adamw_l%d" % l)

    upd1 = update_others(1, None)
    shard_grads[0] = rs0_rest.swapped([upd1[0]])
    upd = update_others(0, upd1)
    full_shapes = {n: w[n].shape for n in _REPLICATED}
    full_shapes.update(conv_w=(DEPTH, 3, CW), b_merge=(DEPTH, NB, D))
    g, sq_total = _small_sums(full_shapes, sq, small_grads.finish([upd[0]]))
    loss = 0.5 / D * jnp.sum(sq_total)
    delta, new_m, new_v = _adamw_small(w, g, m, v)
    upd_in1 = update_w_in(1, shard_grads[1]["w_in"], None)
    w_in_grad0 = rs0_w_in.finish([grad_x, upd_in1[0], upd[0], delta["norm_g"]])["w_in"]
    upd_in = update_w_in(0, w_in_grad0, upd_in1)
    g["w_in"], delta["w_in"], new_m["w_in"], new_v["w_in"] = [jnp.swapaxes(a, 1, 2) for a in upd_in]
    for t, n in enumerate(others):
        g[n], delta[n], new_m[n], new_v[n] = [a.reshape(w[n].shape) for a in upd[4 * t:4 * t + 4]]
    return (loss, grad_x[None], *[g[n] for n in _WEIGHTS], *[delta[n] for n in _WEIGHTS],
            *[new_m[n] for n in _WEIGHTS], *[new_v[n] for n in _WEIGHTS])
```

```python
import functools
import math

import jax
import jax.numpy as jnp
from jax import lax
from jax.experimental import pallas as pl
from jax.experimental.pallas import tpu as pltpu

F32 = jnp.float32
MM = jnp.bfloat16

D = 1024
DEPTH = 2
EPS = 1e-6
H = 8
NOPE = 64
ROPE = 32
QKH = 96
VH = 64
QL = 256
KVL = 128
ROPE_THETA = 10000.0
CW = 512
SGW = 512
SGG = 4
SGC = 128
MH = 4
MHD = 128
NB = 4
BW = 512
NEG_INF = -1e30
LANES = 128
N_CHIPS = 4

R_CQ, R_CKV, R_KR, R_CV, R_SGI, R_MQ, R_SG, R_ML, R_END = 0, 256, 384, 416, 1952, 2976, 3488, 5536, 9632
OFF_ML, OFF_SG, OFF_CV, OFF_SGI, OFF_MQ, OFF_CQ, OFF_CKV, OFF_KR, NP = 0, 4096, 6144, 7680, 8704, 9216, 9472, 9600, 9728

ADAM_LR = 0.001
ADAM_B1 = 0.9
ADAM_B2 = 0.999
ADAM_EPS = 1e-08
ADAM_WD = 0.01
ADAM_STEP = 10

VMEM_LIMIT = 56 * 1024 * 1024
PACK_W = 512
MESH_ID = pl.DeviceIdType.MESH


def _cparams(n_axes):
    return pltpu.CompilerParams(dimension_semantics=("arbitrary",) * n_axes, vmem_limit_bytes=VMEM_LIMIT)


def _bs(shape, imap):
    return pl.BlockSpec(shape, imap)


@jax.custom_vjp
def _mm_plain(a, b):
    return jnp.dot(a.astype(MM), b.astype(MM), preferred_element_type=F32)


def _mm_plain_fwd(a, b):
    return _mm_plain(a, b), (a, b)


def _mm_plain_bwd(res, g):
    a, b = res
    gm = g.astype(MM)
    da = lax.dot_general(gm, b.astype(MM), (((1,), (1,)), ((), ())), preferred_element_type=F32)
    db = lax.dot_general(a.astype(MM), gm, (((0,), (0,)), ((), ())), preferred_element_type=F32)
    return da.astype(a.dtype), db.astype(b.dtype)


_mm_plain.defvjp(_mm_plain_fwd, _mm_plain_bwd)


@jax.custom_vjp
def _mm_slot(a, w, slot):
    return jnp.dot(a.astype(MM), w.astype(MM), preferred_element_type=F32)


def _mm_slot_fwd(a, w, slot):
    return _mm_slot(a, w, slot), (a, w)


def _mm_slot_bwd(res, g):
    a, w = res
    gm = g.astype(MM)
    da = lax.dot_general(gm, w.astype(MM), (((1,), (1,)), ((), ())), preferred_element_type=F32)
    dw = lax.dot_general(a.astype(MM), gm, (((0,), (0,)), ((), ())), preferred_element_type=F32)
    return da.astype(a.dtype), jnp.zeros_like(w), dw


_mm_slot.defvjp(_mm_slot_fwd, _mm_slot_bwd)


def _mm(a, b):
    if isinstance(b, tuple):
        return _mm_slot(a, b[0], b[1])
    return _mm_plain(a, b)


def _with_slot(w):
    return (w, jnp.zeros(w.shape, F32))


@jax.custom_vjp
def _mm_nt(a, b):
    return lax.dot_general(a.astype(MM), b.astype(MM), (((1,), (1,)), ((), ())), preferred_element_type=F32)


def _mm_nt_fwd(a, b):
    return _mm_nt(a, b), (a, b)


def _mm_nt_bwd(res, g):
    a, b = res
    gm = g.astype(MM)
    da = jnp.dot(gm, b.astype(MM), preferred_element_type=F32)
    db = lax.dot_general(gm, a.astype(MM), (((0,), (0,)), ((), ())), preferred_element_type=F32)
    return da.astype(a.dtype), db.astype(b.dtype)


_mm_nt.defvjp(_mm_nt_fwd, _mm_nt_bwd)


@functools.partial(jax.custom_vjp, nondiff_argnums=(1,))
def _lane_roll(x, shift):
    return pltpu.roll(x, shift, 1)


def _lane_roll_fwd(x, shift):
    return pltpu.roll(x, shift, 1), None


def _lane_roll_bwd(shift, _, g):
    return (pltpu.roll(g, (LANES - shift) % LANES, 1),)


_lane_roll.defvjp(_lane_roll_fwd, _lane_roll_bwd)


def _rms_n(x, g, n):
    ms = jnp.sum(x * x, axis=-1, keepdims=True) * (1.0 / n)
    return x * lax.rsqrt(ms + EPS) * g


def _softmax(s):
    m = jnp.max(s, axis=-1, keepdims=True)
    e = jnp.exp(s - m)
    return e / jnp.sum(e, axis=-1, keepdims=True)


def _rope(t, cos_t, sin_a, sin_b):
    return t * cos_t + _lane_roll(t, LANES - 16) * sin_a + _lane_roll(t, 16) * sin_b


def _mla_prep_fn(cq, ckv, kr, cos_t, sin_a, sin_b, cq_g, ckv_g, qg, kg, wuq, wkn, wv):
    cqn = _rms_n(cq, cq_g, QL)
    ckvn = _rms_n(ckv, ckv_g, KVL)
    lane = lax.broadcasted_iota(jnp.int32, kr.shape, 1)
    krm = jnp.where((lane >= NOPE) & (lane < QKH), kr, 0.0)
    qs, ks = [], []
    for h in range(H):
        qh = _rms_n(_mm(cqn, wuq[h]), qg, QKH)
        qs.append(_rope(qh, cos_t, sin_a, sin_b) * (QKH ** -0.5))
        kh = _rms_n(_mm(ckvn, wkn[h]) + krm, kg, QKH)
        ks.append(_rope(kh, cos_t, sin_a, sin_b))
    return jnp.concatenate(qs, axis=-1), jnp.concatenate(ks, axis=-1), _mm(ckvn, wv)


def _dot_nt(a, b):
    return lax.dot_general(a.astype(MM), b.astype(MM), (((1,), (1,)), ((), ())), preferred_element_type=F32)


def _dot_tn(a, b):
    return lax.dot_general(a.astype(MM), b.astype(MM), (((0,), (0,)), ((), ())), preferred_element_type=F32)


def _causal_scores(qe, ke):
    tq, kl = qe.shape[0], ke.shape[0]
    s = _dot_nt(qe, ke)
    rows = lax.broadcasted_iota(jnp.int32, (tq, tq), 0)
    cols = lax.broadcasted_iota(jnp.int32, (tq, tq), 1)
    own = jnp.where(cols <= rows, s[:, kl - tq:], NEG_INF)
    return own if kl == tq else jnp.concatenate([s[:, :kl - tq], own], axis=1)


def _head_lanes(e, shape):
    lane = lax.broadcasted_iota(jnp.int32, shape, len(shape) - 1)
    return (lane >= VH * e) & (lane < VH * (e + 1))


def _attn_pair_fwd(q2, k2, v2):
    tq = q2.shape[0]
    o = jnp.zeros((tq, LANES), F32)
    lse = jnp.zeros((tq, LANES), F32)
    for e in range(2):
        sl = slice(LANES * e, LANES * (e + 1))
        s = _causal_scores(q2[:, sl], k2[:, sl])
        m = jnp.max(s, axis=-1, keepdims=True)
        ex = jnp.exp(s - m)
        l = jnp.sum(ex, axis=-1, keepdims=True)
        ve = jnp.where(_head_lanes(e, v2[:, sl].shape), v2[:, sl], 0.0)
        o = o + jnp.dot(ex.astype(MM), ve.astype(MM), preferred_element_type=F32) * (1.0 / l)
        lse = jnp.where(_head_lanes(e, lse.shape), m + jnp.log(l), lse)
    return o, lse


def _attn_pair_bwd(q2, k2, v2, sg, dys, o, lse):
    sig = jax.nn.sigmoid(sg)
    do = dys * (sg * sig)
    dsg = dys * o * (sig * (1.0 + sg * (1.0 - sig)))
    dqs, dks, dvs = [], [], []
    for e in range(2):
        sl = slice(LANES * e, LANES * (e + 1))
        qe, ke = q2[:, sl], k2[:, sl]
        hm = _head_lanes(e, o.shape)
        lse_e = jnp.max(jnp.where(hm, lse, NEG_INF), axis=-1, keepdims=True)
        do_e = jnp.where(hm, do, 0.0)
        delta = jnp.sum(do_e * o, axis=-1, keepdims=True)
        p = jnp.exp(_causal_scores(qe, ke) - lse_e)
        ve = jnp.where(_head_lanes(e, v2[:, sl].shape), v2[:, sl], 0.0)
        dvs.append(_dot_tn(p, do_e))
        ds = p * (_dot_nt(do_e, ve) - delta)
        dqs.append(jnp.dot(ds.astype(MM), ke.astype(MM), preferred_element_type=F32))
        dks.append(_dot_tn(ds, qe))
    return jnp.concatenate(dqs, axis=-1), jnp.concatenate(dks, axis=-1), jnp.concatenate(dvs, axis=-1), dsg


def _sg_fn(u, v, sgc, ln_g, ln_b, ws, bs):
    mu = jnp.mean(v, axis=-1, keepdims=True)
    xc = v - mu
    vn = xc * lax.rsqrt(jnp.mean(xc * xc, axis=-1, keepdims=True) + EPS) * ln_g + ln_b
    r = lax.broadcasted_iota(jnp.int32, (SGC, SGC), 0)
    c = lax.broadcasted_iota(jnp.int32, (SGC, SGC), 1)
    wt = [jnp.where(r >= c, w, 0.0) for w in ws]
    row_blocks = []
    for ch in range(u.shape[0] // SGC):
        col_blocks = []
        for g in range(SGG):
            blk = vn[SGC * ch:SGC * (ch + 1), LANES * g:LANES * (g + 1)]
            col_blocks.append(_mm(wt[g], blk) + bs[g])
        row_blocks.append(jnp.concatenate(col_blocks, axis=-1))
    mixed = jnp.concatenate(row_blocks, axis=0)
    return (u * mixed) * jax.nn.silu(sgc)


def _memkv_fn(mem, mem_g, wm, kg):
    kv = _mm(_rms_n(mem, mem_g, D), wm)
    ks = [_rms_n(kv[:, MHD * h:MHD * (h + 1)], kg, MHD) for h in range(MH)]
    return jnp.concatenate(ks, axis=-1), kv[:, MH * MHD:]


def _mem_fn(mq, sgd, k, v, qg):
    outs = []
    for h in range(MH):
        sl = slice(MHD * h, MHD * (h + 1))
        qh = _rms_n(mq[:, sl], qg, MHD)
        p = _softmax(_mm_nt(qh, k[:, sl]) * (MHD ** -0.5))
        outs.append(_mm(p, v[:, sl]))
    return jnp.concatenate(outs, axis=-1) * jax.nn.silu(sgd)


def _merge_fn(ys, logits, bm, wb, wo):
    merged = None
    for n in range(NB):
        z = jnp.concatenate([_mm(ys[n], wb[j][n]) for j in range(N_CHIPS)], axis=-1)
        gate = jax.nn.sigmoid(logits[:, D * n:D * (n + 1)] + bm[n])
        merged = gate * z if merged is None else merged + gate * z
    return _mm(merged, wo)


def _proj_call(x, g, w):
    s_len = x.shape[0]
    tm, tn = s_len, 512

    def body(x_ref, g_ref, w_ref, p_ref, h_ref):
        @pl.when(pl.program_id(1) == 0)
        def _():
            h_ref[...] = _rms_n(x_ref[...], g_ref[...], D).astype(h_ref.dtype)
        p_ref[...] = jnp.dot(h_ref[...], w_ref[...], preferred_element_type=F32)

    return pl.pallas_call(
        body, grid=(s_len // tm, NP // tn),
        in_specs=[_bs((tm, D), lambda i, j: (i, 0)), _bs((1, D), lambda i, j: (0, 0)), _bs((D, tn), lambda i, j: (0, j))],
        out_specs=[_bs((tm, tn), lambda i, j: (i, j)), _bs((tm, D), lambda i, j: (i, 0))],
        out_shape=[jax.ShapeDtypeStruct((s_len, NP), F32), jax.ShapeDtypeStruct((s_len, D), MM)],
        name="proj", compiler_params=_cparams(2))(x, g, w)


def _rope_tables(pos):
    half = ROPE // 2
    inv_freq = ROPE_THETA ** (-jnp.arange(half, dtype=F32) / half)
    ang = pos.astype(F32)[:, None] * inv_freq
    cos, sin = jnp.cos(ang), jnp.sin(ang)
    s_len = pos.shape[0]
    z = lambda n: jnp.zeros((s_len, n), F32)
    cos_t = jnp.concatenate([jnp.ones((s_len, NOPE), F32), cos, cos, z(LANES - QKH)], axis=1)
    sin_a = jnp.concatenate([z(NOPE), -sin, z(LANES - NOPE - half)], axis=1)
    sin_b = jnp.concatenate([z(NOPE + half), sin, z(LANES - QKH)], axis=1)
    return cos_t, sin_a, sin_b


def _mla_prep_specs(tm):
    row = lambda w, off: _bs((tm, w), lambda i: (i, off // w))
    full2 = lambda a, b: _bs((a, b), lambda i: (0, 0))
    full3 = lambda a, b, c: _bs((a, b, c), lambda i: (0, 0, 0))
    tab = _bs((tm, LANES), lambda i: (i, 0))
    return [row(QL, OFF_CQ), row(KVL, OFF_CKV), row(LANES, OFF_KR), tab, tab, tab,
            full2(1, QL), full2(1, KVL), full2(1, LANES), full2(1, LANES),
            full3(H, QL, LANES), full3(H, KVL, LANES), full2(KVL, H * LANES)]


def _mla_prep_args(body_refs, wrap=lambda w: w):
    (cq, ckv, kr, ct, sa, sb, cqg, ckvg, qg, kg, wuq, wkn, wv) = body_refs
    return (cq[...], ckv[...], kr[...], ct[...], sa[...], sb[...], cqg[...], ckvg[...], qg[...], kg[...],
            [wrap(wuq[h]) for h in range(H)], [wrap(wkn[h]) for h in range(H)], wrap(wv[...]))


def _mla_prep_call(proj, tabs, cq_g, ckv_g, qg, kg, wuq, wkn, wv):
    s_len = proj.shape[0]
    tm = min(s_len, 512)

    def body(*refs):
        q_ref, k_ref, v_ref = refs[13:]
        q, k, v = _mla_prep_fn(*_mla_prep_args(refs[:13]))
        q_ref[...] = q.astype(q_ref.dtype)
        k_ref[...] = k.astype(k_ref.dtype)
        v_ref[...] = v.astype(v_ref.dtype)

    out = _bs((tm, H * LANES), lambda i: (i, 0))
    return pl.pallas_call(
        body, grid=(s_len // tm,), in_specs=_mla_prep_specs(tm), out_specs=[out, out, out],
        out_shape=[jax.ShapeDtypeStruct((s_len, H * LANES), MM)] * 3,
        name="mla_prep", compiler_params=_cparams(1))(proj, proj, proj, *tabs, cq_g, ckv_g, qg, kg, wuq, wkn, wv)


def _mla_prep_bwd_call(proj, tabs, cq_g, ckv_g, qg, kg, wuq, wkn, wv, dq, dk, dv):
    s_len = proj.shape[0]
    tm = min(s_len, 256)

    def body(*refs):
        dq_ref, dk_ref, dv_ref = refs[13:16]
        dlat_ref, dcqg_ref, dckvg_ref, dqg_ref, dkg_ref, dwuq_ref, dwkn_ref, dwv_ref = refs[16:]
        _, vjp = jax.vjp(_mla_prep_fn, *_mla_prep_args(refs[:13], _with_slot))
        (dcq, dckv, dkr, _, _, _, dcqg, dckvg, dqg, dkg, dwuq, dwkn, dwv) = vjp((dq_ref[...], dk_ref[...], dv_ref[...]))
        dwuq, dwkn, dwv = [d[1] for d in dwuq], [d[1] for d in dwkn], dwv[1]
        dlat_ref[...] = jnp.concatenate([dcq, dckv, dkr], axis=-1).astype(dlat_ref.dtype)

        @pl.when(pl.program_id(0) == 0)
        def _():
            for r in (dcqg_ref, dckvg_ref, dqg_ref, dkg_ref, dwuq_ref, dwkn_ref, dwv_ref):
                r[...] = jnp.zeros_like(r)
        dcqg_ref[...] += dcqg
        dckvg_ref[...] += dckvg
        dqg_ref[...] += dqg
        dkg_ref[...] += dkg
        for h in range(H):
            dwuq_ref[h] += dwuq[h]
            dwkn_ref[h] += dwkn[h]
        dwv_ref[...] += dwv

    big = _bs((tm, H * LANES), lambda i: (i, 0))
    row = lambda w: _bs((tm, w), lambda i: (i, 0))
    full2 = lambda a, b: _bs((a, b), lambda i: (0, 0))
    full3 = lambda a, b, c: _bs((a, b, c), lambda i: (0, 0, 0))
    sd = jax.ShapeDtypeStruct
    return pl.pallas_call(
        body, grid=(s_len // tm,), in_specs=_mla_prep_specs(tm) + [big, big, big],
        out_specs=[row(QL + KVL + LANES), full2(1, QL), full2(1, KVL), full2(1, LANES), full2(1, LANES),
                   full3(H, QL, LANES), full3(H, KVL, LANES), full2(KVL, H * LANES)],
        out_shape=[sd((s_len, QL + KVL + LANES), MM), sd((1, QL), F32), sd((1, KVL), F32),
                   sd((1, LANES), F32), sd((1, LANES), F32), sd((H, QL, LANES), F32), sd((H, KVL, LANES), F32),
                   sd((KVL, H * LANES), F32)],
        name="mla_prep_bwd", compiler_params=_cparams(1))(proj, proj, proj, *tabs, cq_g, ckv_g, qg, kg, wuq, wkn, wv, dq, dk, dv)


def _attn_specs(s_len, tq):
    pair = 2 * LANES
    return [_bs((tq, pair), lambda p, i: (i, p)), _bs((s_len, pair), lambda p, i: (0, p)), _bs((s_len, pair), lambda p, i: (0, p)),
            _bs((tq, LANES), lambda p, i: (i, OFF_SG // LANES + p))]


def _attn_call(q, k, v, proj):
    s_len = q.shape[0]
    tq = min(s_len, 256)

    def body(q_ref, k_ref, v_ref, sg_ref, y_ref, o_ref, lse_ref):
        for n in range(s_len // tq):
            @pl.when(pl.program_id(1) == n)
            def _():
                kl = (n + 1) * tq
                o, lse = _attn_pair_fwd(q_ref[...], k_ref[:kl, :], v_ref[:kl, :])
                y_ref[...] = (o * jax.nn.silu(sg_ref[...])).astype(y_ref.dtype)
                o_ref[...] = o
                lse_ref[...] = lse

    tile = _bs((tq, LANES), lambda p, i: (i, p))
    sd = jax.ShapeDtypeStruct
    return pl.pallas_call(
        body, grid=(H // 2, s_len // tq), in_specs=_attn_specs(s_len, tq), out_specs=[tile, tile, tile],
        out_shape=[sd((s_len, BW), MM), sd((s_len, BW), F32), sd((s_len, BW), F32)],
        name="attn", compiler_params=_cparams(2))(q, k, v, proj)


def _attn_bwd_call(q, k, v, proj, dys, o, lse):
    s_len = q.shape[0]
    tq = min(s_len, 256)
    pair = 2 * LANES

    def body(q_ref, k_ref, v_ref, sg_ref, dy_ref, o_ref, lse_ref, dq_ref, dk_ref, dv_ref, dsg_ref):
        i = pl.program_id(1)

        @pl.when(i == 0)
        def _():
            dk_ref[...] = jnp.zeros_like(dk_ref)
            dv_ref[...] = jnp.zeros_like(dv_ref)

        for n in range(s_len // tq):
            @pl.when(i == n)
            def _():
                kl = (n + 1) * tq
                dq, dk, dv, dsg = _attn_pair_bwd(q_ref[...], k_ref[:kl, :], v_ref[:kl, :], sg_ref[...], dy_ref[...],
                                                 o_ref[...], lse_ref[...])
                dq_ref[...] = dq
                dsg_ref[...] = dsg.astype(dsg_ref.dtype)
                dk_ref[:kl, :] += dk
                dv_ref[:kl, :] += dv

    sd = jax.ShapeDtypeStruct
    tile = _bs((tq, LANES), lambda p, i: (i, p))
    return pl.pallas_call(
        body, grid=(H // 2, s_len // tq),
        in_specs=_attn_specs(s_len, tq) + [tile, tile, tile],
        out_specs=[_bs((tq, pair), lambda p, i: (i, p)), _bs((s_len, pair), lambda p, i: (0, p)),
                   _bs((s_len, pair), lambda p, i: (0, p)), tile],
        out_shape=[sd((s_len, H * LANES), F32), sd((s_len, H * LANES), F32), sd((s_len, H * LANES), F32), sd((s_len, BW), MM)],
        name="attn_bwd", compiler_params=_cparams(2))(q, k, v, proj, dys, o, lse)


def _shift_down(a, n):
    r = lax.broadcasted_iota(jnp.int32, a.shape, 0)
    return jnp.where(r >= n, pltpu.roll(a, n, 0), 0.0)


def _shift_up(a, n):
    s_len = a.shape[0]
    r = lax.broadcasted_iota(jnp.int32, a.shape, 0)
    return jnp.where(r < s_len - n, pltpu.roll(a, s_len - n, 0), 0.0)


def _conv_specs(s_len):
    col = lambda off: _bs((s_len, LANES), lambda j: (0, off // LANES + j))
    return [col(OFF_CV), col(OFF_CV + CW), col(OFF_CV + 2 * CW), col(OFF_SG + BW),
            _bs((3, LANES), lambda j: (0, j)), _bs((1, LANES), lambda j: (0, j))]


def _conv_call(proj, cw, cb):
    s_len = proj.shape[0]

    def body(bg_ref, cg_ref, xi_ref, sg_ref, w_ref, b_ref, y_ref):
        z = cg_ref[...] * xi_ref[...]
        y = b_ref[...] + w_ref[0:1, :] * _shift_down(z, 2)
        y = y + w_ref[1:2, :] * _shift_down(z, 1)
        y = y + w_ref[2:3, :] * z
        y_ref[...] = ((bg_ref[...] * y) * jax.nn.silu(sg_ref[...])).astype(y_ref.dtype)

    return pl.pallas_call(
        body, grid=(CW // LANES,), in_specs=_conv_specs(s_len), out_specs=_bs((s_len, LANES), lambda j: (0, j)),
        out_shape=jax.ShapeDtypeStruct((s_len, CW), MM), name="conv", compiler_params=_cparams(1))(proj, proj, proj, proj, cw, cb)


def _conv_bwd_call(proj, cw, cb, dys):
    s_len = proj.shape[0]

    def body(bg_ref, cg_ref, xi_ref, sg_ref, w_ref, b_ref, dys_ref, dbg_ref, dcg_ref, dxi_ref, dsg_ref, dw_ref, db_ref):
        bg, cg, xi, sg = bg_ref[...], cg_ref[...], xi_ref[...], sg_ref[...]
        w0, w1, w2 = w_ref[0:1, :], w_ref[1:2, :], w_ref[2:3, :]
        z = cg * xi
        z1, z2 = _shift_down(z, 1), _shift_down(z, 2)
        y = b_ref[...] + w0 * z2
        y = y + w1 * z1
        y = y + w2 * z
        yb = bg * y
        sig = jax.nn.sigmoid(sg)
        silu = sg * sig
        dys_v = dys_ref[...]
        dsg_ref[...] = (dys_v * yb * (sig * (1.0 + sg * (1.0 - sig)))).astype(dsg_ref.dtype)
        dyb = dys_v * silu
        dbg_ref[...] = (dyb * y).astype(dbg_ref.dtype)
        dy = dyb * bg
        db_ref[...] = jnp.sum(dy, axis=0, keepdims=True)
        dw_ref[0:1, :] = jnp.sum(dy * z2, axis=0, keepdims=True)
        dw_ref[1:2, :] = jnp.sum(dy * z1, axis=0, keepdims=True)
        dw_ref[2:3, :] = jnp.sum(dy * z, axis=0, keepdims=True)
        dz = w2 * dy + w1 * _shift_up(dy, 1) + w0 * _shift_up(dy, 2)
        dcg_ref[...] = (dz * xi).astype(dcg_ref.dtype)
        dxi_ref[...] = (dz * cg).astype(dxi_ref.dtype)

    col = _bs((s_len, LANES), lambda j: (0, j))
    sd = jax.ShapeDtypeStruct
    return pl.pallas_call(
        body, grid=(CW // LANES,), in_specs=_conv_specs(s_len) + [col],
        out_specs=[col, col, col, col, _bs((3, LANES), lambda j: (0, j)), _bs((1, LANES), lambda j: (0, j))],
        out_shape=[sd((s_len, CW), MM)] * 4 + [sd((3, CW), F32), sd((1, CW), F32)],
        name="conv_bwd", compiler_params=_cparams(1))(proj, proj, proj, proj, cw, cb, dys)


def _sg_specs(tm):
    row = lambda off: _bs((tm, SGW), lambda i: (i, off // SGW))
    return [row(OFF_SGI), row(OFF_SGI + SGW), row(OFF_SG + 2 * BW), _bs((1, SGW), lambda i: (0, 0)), _bs((1, SGW), lambda i: (0, 0)),
            _bs((SGG, SGC, SGC), lambda i: (0, 0, 0)), _bs((SGG, SGC, 1), lambda i: (0, 0, 0))]


def _sg_args(refs):
    u, v, sg, lg, lb, ws, bs = refs
    return (u[...], v[...], sg[...], lg[...], lb[...], [ws[g] for g in range(SGG)], [bs[g] for g in range(SGG)])


def _sg_call(proj, ln_g, ln_b, ws, bs):
    s_len = proj.shape[0]
    tm = min(s_len, 256)

    def body(*refs):
        refs[7][...] = _sg_fn(*_sg_args(refs[:7])).astype(refs[7].dtype)

    return pl.pallas_call(
        body, grid=(s_len // tm,), in_specs=_sg_specs(tm), out_specs=_bs((tm, SGW), lambda i: (i, 0)),
        out_shape=jax.ShapeDtypeStruct((s_len, SGW), MM), name="sgmlp", compiler_params=_cparams(1))(proj, proj, proj, ln_g, ln_b, ws, bs)


def _sg_bwd_call(proj, ln_g, ln_b, ws, bs, dys):
    s_len = proj.shape[0]
    tm = min(s_len, 256)

    def body(*refs):
        dys_ref = refs[7]
        duv_ref, dsg_ref, dlg_ref, dlb_ref, dws_ref, dbs_ref = refs[8:]
        _, vjp = jax.vjp(_sg_fn, *_sg_args(refs[:7]))
        du, dv, dsg, dlg, dlb, dws, dbs = vjp(dys_ref[...])
        duv_ref[...] = jnp.concatenate([du, dv], axis=-1).astype(duv_ref.dtype)
        dsg_ref[...] = dsg.astype(dsg_ref.dtype)

        @pl.when(pl.program_id(0) == 0)
        def _():
            for r in (dlg_ref, dlb_ref, dws_ref, dbs_ref):
                r[...] = jnp.zeros_like(r)
        dlg_ref[...] += dlg
        dlb_ref[...] += dlb
        for g in range(SGG):
            dws_ref[g] += dws[g]
            dbs_ref[g] += dbs[g]

    row = _bs((tm, SGW), lambda i: (i, 0))
    sd = jax.ShapeDtypeStruct
    return pl.pallas_call(
        body, grid=(s_len // tm,), in_specs=_sg_specs(tm) + [row],
        out_specs=[_bs((tm, 2 * SGW), lambda i: (i, 0)), row, _bs((1, SGW), lambda i: (0, 0)), _bs((1, SGW), lambda i: (0, 0)),
                   _bs((SGG, SGC, SGC), lambda i: (0, 0, 0)), _bs((SGG, SGC, 1), lambda i: (0, 0, 0))],
        out_shape=[sd((s_len, 2 * SGW), MM), sd((s_len, SGW), MM), sd((1, SGW), F32), sd((1, SGW), F32),
                   sd((SGG, SGC, SGC), F32), sd((SGG, SGC, 1), F32)],
        name="sgmlp_bwd", compiler_params=_cparams(1))(proj, proj, proj, ln_g, ln_b, ws, bs, dys)


def _memkv_call(mem, mem_g, wm, kg):
    m_len = mem.shape[0]

    def body(mem_ref, g_ref, w_ref, kg_ref, k_ref, v_ref):
        k, v = _memkv_fn(mem_ref[...], g_ref[...], w_ref[...], kg_ref[...])
        k_ref[...] = k.astype(k_ref.dtype)
        v_ref[...] = v.astype(v_ref.dtype)

    return pl.pallas_call(body, out_shape=[jax.ShapeDtypeStruct((m_len, MH * MHD), MM)] * 2, name="memkv",
                          compiler_params=pltpu.CompilerParams(vmem_limit_bytes=VMEM_LIMIT))(mem, mem_g, wm, kg)


def _memkv_bwd_call(mem, mem_g, wm, kg, dk, dv):
    def body(mem_ref, g_ref, w_ref, kg_ref, dk_ref, dv_ref, dg_ref, dw_ref, dkg_ref):
        _, vjp = jax.vjp(_memkv_fn, mem_ref[...], g_ref[...], _with_slot(w_ref[...]), kg_ref[...])
        _, dg, dw, dkg = vjp((dk_ref[...], dv_ref[...]))
        dg_ref[...] = dg
        dw_ref[...] = dw[1]
        dkg_ref[...] = dkg

    sd = jax.ShapeDtypeStruct
    return pl.pallas_call(body, out_shape=[sd((1, D), F32), sd((D, 2 * MH * MHD), F32), sd((1, MHD), F32)], name="memkv_bwd",
                          compiler_params=pltpu.CompilerParams(vmem_limit_bytes=VMEM_LIMIT))(mem, mem_g, wm, kg, dk, dv)


def _mem_specs(tm, m_len):
    w = MH * MHD
    return [_bs((tm, w), lambda i: (i, OFF_MQ // w)), _bs((tm, BW), lambda i: (i, (OFF_SG + 3 * BW) // BW)),
            _bs((m_len, w), lambda i: (0, 0)), _bs((m_len, w), lambda i: (0, 0)), _bs((1, MHD), lambda i: (0, 0))]


def _mem_call(proj, k, v, qg):
    s_len, m_len = proj.shape[0], k.shape[0]
    tm = min(s_len, 512)

    def body(mq_ref, sg_ref, k_ref, v_ref, qg_ref, y_ref):
        y_ref[...] = _mem_fn(mq_ref[...], sg_ref[...], k_ref[...], v_ref[...], qg_ref[...]).astype(y_ref.dtype)

    return pl.pallas_call(
        body, grid=(s_len // tm,), in_specs=_mem_specs(tm, m_len), out_specs=_bs((tm, BW), lambda i: (i, 0)),
        out_shape=jax.ShapeDtypeStruct((s_len, BW), MM), name="memattn", compiler_params=_cparams(1))(proj, proj, k, v, qg)


def _mem_bwd_call(proj, k, v, qg, dys):
    s_len, m_len = proj.shape[0], k.shape[0]
    tm = min(s_len, 256)
    w = MH * MHD

    def body(mq_ref, sg_ref, k_ref, v_ref, qg_ref, dys_ref, dmq_ref, dsg_ref, dk_ref, dv_ref, dqg_ref):
        _, vjp = jax.vjp(_mem_fn, mq_ref[...], sg_ref[...], k_ref[...].astype(F32), v_ref[...].astype(F32), qg_ref[...])
        dmq, dsg, dk, dv, dqg = vjp(dys_ref[...])
        dmq_ref[...] = dmq.astype(dmq_ref.dtype)
        dsg_ref[...] = dsg.astype(dsg_ref.dtype)

        @pl.when(pl.program_id(0) == 0)
        def _():
            for r in (dk_ref, dv_ref, dqg_ref):
                r[...] = jnp.zeros_like(r)
        dk_ref[...] += dk
        dv_ref[...] += dv
        dqg_ref[...] += dqg

    row = _bs((tm, BW), lambda i: (i, 0))
    kv = _bs((m_len, w), lambda i: (0, 0))
    sd = jax.ShapeDtypeStruct
    return pl.pallas_call(
        body, grid=(s_len // tm,), in_specs=_mem_specs(tm, m_len) + [row],
        out_specs=[row, row, kv, kv, _bs((1, MHD), lambda i: (0, 0))],
        out_shape=[sd((s_len, w), MM), sd((s_len, BW), MM), sd((m_len, w), F32), sd((m_len, w), F32), sd((1, MHD), F32)],
        name="memattn_bwd", compiler_params=_cparams(1))(proj, proj, k, v, qg, dys)


def _merge_specs(tm):
    row = _bs((tm, BW), lambda i: (i, 0))
    return [row, row, row, row, _bs((tm, NB * D), lambda i: (i, OFF_ML // (NB * D))), _bs((NB, D), lambda i: (0, 0)),
            _bs((N_CHIPS, NB, BW, D // N_CHIPS), lambda i: (0, 0, 0, 0)), _bs((D, D), lambda i: (0, 0))]


def _merge_call(ys, proj, bm, wb, wo, x):
    s_len = proj.shape[0]
    tm = min(s_len, 512)

    def body(ya, yb, yc, yd, lg_ref, bm_ref, wb_ref, wo_ref, x_ref, o_ref):
        out = _merge_fn([r[...] for r in (ya, yb, yc, yd)], lg_ref[...], [bm_ref[n:n + 1, :] for n in range(NB)],
                        [[wb_ref[j, n] for n in range(NB)] for j in range(N_CHIPS)], wo_ref[...])
        o_ref[...] = x_ref[...] + out

    xrow = _bs((tm, D), lambda i: (i, 0))
    return pl.pallas_call(
        body, grid=(s_len // tm,), in_specs=_merge_specs(tm) + [xrow], out_specs=xrow,
        out_shape=jax.ShapeDtypeStruct((s_len, D), F32), name="merge", compiler_params=_cparams(1))(*ys, proj, bm, wb, wo, x)


def _merge_bwd_call(ys, proj, bm, wb, wo, dout):
    s_len = proj.shape[0]
    tm = min(s_len, 256)

    def body(ya, yb, yc, yd, lg_ref, bm_ref, wb_ref, wo_ref, do_ref, dya, dyb, dyc, dyd, dlg_ref, dbm_ref, dwb_ref, dwo_ref):
        fn = lambda ys_, lg_, bm_, wb_, wo_: _merge_fn(ys_, lg_, bm_, wb_, wo_)
        _, vjp = jax.vjp(fn, [r[...].astype(F32) for r in (ya, yb, yc, yd)], lg_ref[...], [bm_ref[n:n + 1, :] for n in range(NB)],
                         [[_with_slot(wb_ref[j, n]) for n in range(NB)] for j in range(N_CHIPS)], _with_slot(wo_ref[...]))
        dys, dlg, dbm, dwb, dwo = vjp(do_ref[...])
        dwb, dwo = [[d[1] for d in row] for row in dwb], dwo[1]
        for r, d in zip((dya, dyb, dyc, dyd), dys):
            r[...] = d
        dlg_ref[...] = dlg.astype(dlg_ref.dtype)

        @pl.when(pl.program_id(0) == 0)
        def _():
            for r in (dbm_ref, dwb_ref, dwo_ref):
                r[...] = jnp.zeros_like(r)
        for n in range(NB):
            dbm_ref[n:n + 1, :] += dbm[n]
            for j in range(N_CHIPS):
                dwb_ref[j, n] += dwb[j][n]
        dwo_ref[...] += dwo

    row = _bs((tm, BW), lambda i: (i, 0))
    sd = jax.ShapeDtypeStruct
    wb_shape = (N_CHIPS, NB, BW, D // N_CHIPS)
    return pl.pallas_call(
        body, grid=(s_len // tm,), in_specs=_merge_specs(tm) + [_bs((tm, D), lambda i: (i, 0))],
        out_specs=[row, row, row, row, _bs((tm, NB * D), lambda i: (i, 0)), _bs((NB, D), lambda i: (0, 0)),
                   _bs(wb_shape, lambda i: (0, 0, 0, 0)), _bs((D, D), lambda i: (0, 0))],
        out_shape=[sd((s_len, BW), F32)] * 4 + [sd((s_len, NP), MM), sd((NB, D), F32), sd(wb_shape, F32), sd((D, D), F32)],
        name="merge_bwd", compiler_params=_cparams(1))(*ys, proj, bm, wb, wo, dout)


def _dh_call(dproj, w, x, g, dout, after=()):
    s_len = x.shape[0]
    tk = NP // 4
    after = list(after)

    def matmul_body(dp_ref, w_ref, *rest):
        o_ref = rest[-1]

        @pl.when(pl.program_id(0) == 0)
        def _():
            o_ref[...] = jnp.zeros_like(o_ref)
        o_ref[...] += lax.dot_general(dp_ref[...], w_ref[...], (((1,), (1,)), ((), ())), preferred_element_type=F32)

    dh = pl.pallas_call(
        matmul_body, grid=(NP // tk,),
        in_specs=[_bs((s_len, tk), lambda k: (0, k)), _bs((D, tk), lambda k: (0, k))] + [_ANY] * len(after),
        out_specs=_bs((s_len, D), lambda k: (0, 0)), out_shape=jax.ShapeDtypeStruct((s_len, D), F32),
        name="dh", compiler_params=_cparams(1))(dproj, w, *after)

    tm = min(s_len, 512)

    def norm_body(dh_ref, x_ref, g_ref, do_ref, dx_ref, dg_ref):
        _, vjp = jax.vjp(lambda x_, g_: _rms_n(x_, g_, D), x_ref[...], g_ref[...])
        dxr, dgr = vjp(dh_ref[...])
        dx_ref[...] = do_ref[...] + dxr

        @pl.when(pl.program_id(0) == 0)
        def _():
            dg_ref[...] = jnp.zeros_like(dg_ref)
        dg_ref[...] += dgr

    row = _bs((tm, D), lambda i: (i, 0))
    return pl.pallas_call(
        norm_body, grid=(s_len // tm,), in_specs=[row, row, _bs((1, D), lambda i: (0, 0)), row],
        out_specs=[row, _bs((1, D), lambda i: (0, 0))],
        out_shape=[jax.ShapeDtypeStruct((s_len, D), F32), jax.ShapeDtypeStruct((1, D), F32)],
        name="norm_bwd", compiler_params=_cparams(1))(dh, x, g, dout)


def _dw_call(h, dproj, after=()):
    s_len = h.shape[0]
    tn = 512
    after = list(after)

    def body(h_ref, dp_ref, *rest):
        o_ref, ht_ref = rest[-2], rest[-1]

        @pl.when(pl.program_id(0) == 0)
        def _():
            ht_ref[...] = h_ref[...].T
        o_ref[...] = jnp.dot(ht_ref[...], dp_ref[...], preferred_element_type=F32)

    return pl.pallas_call(
        body, grid=(NP // tn,),
        in_specs=[_bs((s_len, D), lambda j: (0, 0)), _bs((s_len, tn), lambda j: (0, j))] + [_ANY] * len(after),
        out_specs=_bs((D, tn), lambda j: (0, j)), out_shape=jax.ShapeDtypeStruct((D, NP), F32),
        scratch_shapes=[pltpu.VMEM((D, s_len), h.dtype)], name="dw_in", compiler_params=_cparams(1))(h, dproj, *after)


def _loss_call(y, target):
    s_len = y.shape[0]
    tm = min(s_len, 512)

    def body(y_ref, t_ref, dy_ref, l_ref):
        e = y_ref[...] - t_ref[...]
        dy_ref[...] = e * (1.0 / D)

        @pl.when(pl.program_id(0) == 0)
        def _():
            l_ref[...] = jnp.zeros_like(l_ref)
        l_ref[...] += jnp.sum(e * e, axis=0, keepdims=True)

    row = _bs((tm, D), lambda i: (i, 0))
    return pl.pallas_call(
        body, grid=(s_len // tm,), in_specs=[row, row], out_specs=[row, _bs((1, D), lambda i: (0, 0))],
        out_shape=[jax.ShapeDtypeStruct((s_len, D), F32), jax.ShapeDtypeStruct((1, D), F32)],
        name="loss", compiler_params=_cparams(1))(y, target)


def _adamw_small_call(ws, gs, ms, vs, name):
    n = len(ws)

    def body(*refs):
        for t in range(n):
            w_ref, g_ref, m_ref, v_ref = refs[t], refs[n + t], refs[2 * n + t], refs[3 * n + t]
            d_ref, nm_ref, nv_ref = refs[4 * n + 3 * t:4 * n + 3 * t + 3]
            gv = g_ref[...]
            m2 = ADAM_B1 * m_ref[...] + (1.0 - ADAM_B1) * gv
            v2 = ADAM_B2 * v_ref[...] + (1.0 - ADAM_B2) * (gv * gv)
            m_hat = m2 / (1.0 - ADAM_B1 ** ADAM_STEP)
            v_hat = v2 / (1.0 - ADAM_B2 ** ADAM_STEP)
            d_ref[...] = -ADAM_LR * (m_hat / (jnp.sqrt(v_hat) + ADAM_EPS) + ADAM_WD * w_ref[...])
            nm_ref[...] = m2
            nv_ref[...] = v2

    return pl.pallas_call(
        body, out_shape=[jax.ShapeDtypeStruct(w.shape, F32) for w in ws for _ in range(3)], name=name,
        compiler_params=pltpu.CompilerParams(vmem_limit_bytes=VMEM_LIMIT))(*ws, *gs, *ms, *vs)


def _adamw_layer_call(layer, ws, gs, ms, vs, prev, after, name, steps=8):
    n = len(ws)
    after = list(after)
    n_prev = 4 * n if prev is not None else 0

    def body(*refs):
        outs = refs[len(refs) - 4 * n:]
        for t in range(n):
            w_ref, g_ref, m_ref, v_ref = refs[t], refs[n + t], refs[2 * n + t], refs[3 * n + t]
            g_out, d_out, m_out, v_out = outs[4 * t:4 * t + 4]
            gv = g_ref[...]
            m2 = ADAM_B1 * m_ref[0] + (1.0 - ADAM_B1) * gv
            v2 = ADAM_B2 * v_ref[0] + (1.0 - ADAM_B2) * (gv * gv)
            m_hat = m2 / (1.0 - ADAM_B1 ** ADAM_STEP)
            v_hat = v2 / (1.0 - ADAM_B2 ** ADAM_STEP)
            g_out[0] = gv
            d_out[0] = -ADAM_LR * (m_hat / (jnp.sqrt(v_hat) + ADAM_EPS) + ADAM_WD * w_ref[0])
            m_out[0] = m2
            v_out[0] = v2

    def lay(a):
        return _bs((1, a.shape[1] // steps, a.shape[2]), lambda i: (layer, i, 0))

    in_specs = ([lay(a) for a in ws] + [_bs((g.shape[0] // steps, g.shape[1]), lambda i: (i, 0)) for g in gs]
                + [lay(a) for a in ms] + [lay(a) for a in vs] + [_ANY] * (n_prev + len(after)))
    return pl.pallas_call(
        body, grid=(steps,), in_specs=in_specs, out_specs=[lay(ws[t]) for t in range(n) for _ in range(4)],
        out_shape=[jax.ShapeDtypeStruct(ws[t].shape, F32) for t in range(n) for _ in range(4)],
        input_output_aliases={4 * n + q: q for q in range(n_prev)}, name=name, compiler_params=_cparams(1),
    )(*ws, *gs, *ms, *vs, *(prev if prev is not None else []), *after)


def _row_tile(rows):
    for cand in (512, 256, 128, 64, 32, 16, 8):
        if rows % cand == 0 and rows > cand:
            return cand
    return rows


def _pair_sum_call(grads, from_sibling, core, name):
    n = len(grads)

    def body(core_ref, *refs):
        for t in range(n):
            refs[2 * n + t][...] = (refs[t][...].astype(F32) + refs[n + t][...].astype(F32)).astype(MM)

    half = lambda g: (1, g.shape[1] // 2, g.shape[2])
    grid_spec = pltpu.PrefetchScalarGridSpec(
        num_scalar_prefetch=1, grid=(N_CHIPS,),
        in_specs=[pl.BlockSpec(half(g), lambda j, core_ref: (j, core_ref[0], 0)) for g in grads]
        + [pl.BlockSpec(half(g), lambda j, core_ref: (j, 0, 0)) for g in grads],
        out_specs=[pl.BlockSpec(half(g), lambda j, core_ref: (j, 0, 0)) for g in grads])
    return pl.pallas_call(
        body, grid_spec=grid_spec, out_shape=[jax.ShapeDtypeStruct((N_CHIPS,) + half(g)[1:], MM) for g in grads], name=name,
        compiler_params=_cparams(1))(core, *grads, *from_sibling)


def _owner_sum_call(chip_sums, from_chips, chip_core, name):
    n = len(chip_sums)
    steps = 4

    def body(ids_ref, *refs):
        for t in range(n):
            a, b = refs[t], refs[n + t]
            refs[2 * n + t][...] = ((a[0].astype(F32) + b[0].astype(F32)) + b[1].astype(F32)) + b[2].astype(F32)

    tile = lambda p: (p.shape[1] // steps, p.shape[2])
    grid_spec = pltpu.PrefetchScalarGridSpec(
        num_scalar_prefetch=1, grid=(steps,),
        in_specs=[pl.BlockSpec((1,) + tile(p), lambda i, ids_ref: (ids_ref[0], i, 0)) for p in chip_sums]
        + [pl.BlockSpec((3,) + tile(p), lambda i, ids_ref: (0, i, 0)) for p in chip_sums],
        out_specs=[pl.BlockSpec(tile(p), lambda i, ids_ref: (ids_ref[1] * steps + i, 0)) for p in chip_sums])
    return pl.pallas_call(
        body, grid_spec=grid_spec, out_shape=[jax.ShapeDtypeStruct((2 * p.shape[1], p.shape[2]), F32) for p in chip_sums],
        name=name, compiler_params=_cparams(1))(chip_core, *chip_sums, *from_chips)


def _sum8_call(parts):
    n, rows, cols = parts.shape
    tr = _row_tile(rows)

    def body(p_ref, o_ref):
        acc = p_ref[0]
        for k in range(1, n):
            acc = acc + p_ref[k]
        o_ref[...] = acc

    return pl.pallas_call(
        body, grid=(rows // tr,), in_specs=[_bs((n, tr, cols), lambda i: (0, i, 0))], out_specs=_bs((tr, cols), lambda i: (i, 0)),
        out_shape=jax.ShapeDtypeStruct((rows, cols), F32), name="sum_small_grads", compiler_params=_cparams(1))(parts)


_ANY = pl.BlockSpec(memory_space=pl.ANY)


def _half_rows(ref, lead, half, which):
    rows = pl.ds(pl.multiple_of(half * which, half), half)
    return ref.at[rows] if lead is None else ref.at[lead, rows]


_HBM = pl.BlockSpec(memory_space=pltpu.HBM)
_SEM = pl.BlockSpec(memory_space=pltpu.SEMAPHORE)
_ORDERED_EFFECT = pltpu.CompilerParams(has_side_effects=pltpu.SideEffectType.DATAFLOW_SIDE_EFFECTING)


_VMEM = pl.BlockSpec(memory_space=pltpu.VMEM)
_TOKEN = jax.ShapeDtypeStruct((8, LANES), F32)


def _in_hbm(a):
    return pltpu.with_memory_space_constraint(a, pltpu.HBM)


def _tie(small, token):
    return small + token[0:1, 0:1].reshape((1,) * small.ndim)


def _peer(k):
    x, y, c = lax.axis_index("x"), lax.axis_index("y"), lax.axis_index("c")
    bx, by, bc = (k >> 2) & 1, (k >> 1) & 1, k & 1
    return (x ^ bx if bx else x, y ^ by if by else y, c ^ bc if bc else c)


def _place_block_call(blk, index, name):
    rows, cols = blk.shape

    def body(idx_ref, b_ref, o_ref):
        o_ref[0] = b_ref[...]

    grid_spec = pltpu.PrefetchScalarGridSpec(
        num_scalar_prefetch=1, grid=(1,), in_specs=[pl.BlockSpec((rows, cols), lambda i, idx_ref: (0, 0))],
        out_specs=pl.BlockSpec((1, rows, cols), lambda i, idx_ref: (idx_ref[0], 0, 0)))
    return pl.pallas_call(body, grid_spec=grid_spec, out_shape=jax.ShapeDtypeStruct((8, rows, cols), blk.dtype), name=name,
                          compiler_params=_cparams(1))(index, blk)


def _small_gather_start_call(blk, buf, after, name):
    after = list(after)

    def body(*refs):
        b_ref, out_ref = refs[0], refs[2 + len(after)]
        send_sems, recv_sems, token = refs[3 + len(after):]
        x, y, c = lax.axis_index("x"), lax.axis_index("y"), lax.axis_index("c")
        for k in range(1, 8):
            pltpu.make_async_remote_copy(src_ref=b_ref, dst_ref=out_ref.at[4 * x + 2 * y + c], send_sem=send_sems.at[k - 1],
                                         recv_sem=recv_sems.at[k - 1], device_id=_peer(k), device_id_type=MESH_ID).start()
        token[...] = jnp.zeros_like(token)

    dma = pltpu.SemaphoreType.DMA
    return pl.pallas_call(
        body, out_shape=[pltpu.HBM(buf.shape, buf.dtype), dma((7,)), dma((7,)), _TOKEN],
        in_specs=[_HBM, _HBM] + [_ANY] * len(after), out_specs=[_HBM, _SEM, _SEM, _VMEM],
        input_output_aliases={1: 0}, name=name, compiler_params=_ORDERED_EFFECT)(_in_hbm(blk), _in_hbm(buf), *after)


def _small_gather_finish_call(blk, buf, send_sems, recv_sems, after, name):
    after = list(after)

    def body(*refs):
        b_ref, in_ref, send_ref, recv_ref = refs[:4]
        x, y, c = lax.axis_index("x"), lax.axis_index("y"), lax.axis_index("c")
        for k in range(1, 8):
            px, py, pc = _peer(k)
            pltpu.make_async_remote_copy(src_ref=b_ref, dst_ref=in_ref.at[4 * px + 2 * py + pc], send_sem=send_ref.at[k - 1],
                                         recv_sem=recv_ref.at[k - 1], device_id=(px, py, pc), device_id_type=MESH_ID).wait()

    return pl.pallas_call(
        body, out_shape=pltpu.HBM(buf.shape, buf.dtype), in_specs=[_HBM, _HBM, _SEM, _SEM] + [_ANY] * len(after),
        out_specs=_HBM, input_output_aliases={1: 0}, name=name, compiler_params=_ORDERED_EFFECT,
    )(_in_hbm(blk), buf, send_sems, recv_sems, *after)


def _pair_exchange_start_call(grads, name):
    n = len(grads)
    half = [g.shape[1] // 2 for g in grads]

    def body(*refs):
        srcs, outs = refs[:n], refs[n:2 * n]
        send_sems, recv_sems, token = refs[2 * n:]
        x, y, c = lax.axis_index("x"), lax.axis_index("y"), lax.axis_index("c")
        for t in range(n):
            pltpu.make_async_remote_copy(
                src_ref=srcs[t].at[:, pl.ds(pl.multiple_of(half[t] * (1 - c), half[t]), half[t])], dst_ref=outs[t],
                send_sem=send_sems.at[t], recv_sem=recv_sems.at[t], device_id=(x, y, 1 - c), device_id_type=MESH_ID).start()
        token[...] = jnp.zeros_like(token)

    dma = pltpu.SemaphoreType.DMA
    return pl.pallas_call(
        body, out_shape=[pltpu.HBM((g.shape[0], g.shape[1] // 2, g.shape[2]), g.dtype) for g in grads] + [dma((n,)), dma((n,)), _TOKEN],
        in_specs=[_HBM] * n, out_specs=[_HBM] * n + [_SEM, _SEM, _VMEM], name=name, compiler_params=_ORDERED_EFFECT,
    )(*[_in_hbm(g) for g in grads])


def _pair_exchange_finish_call(grads, bufs, send_sems, recv_sems, after, name):
    n = len(grads)
    after = list(after)
    half = [g.shape[1] // 2 for g in grads]

    def body(*refs):
        srcs, ins, send_ref, recv_ref = refs[:n], refs[n:2 * n], refs[2 * n], refs[2 * n + 1]
        x, y, c = lax.axis_index("x"), lax.axis_index("y"), lax.axis_index("c")
        for t in range(n):
            pltpu.make_async_remote_copy(
                src_ref=srcs[t].at[:, pl.ds(pl.multiple_of(half[t] * (1 - c), half[t]), half[t])], dst_ref=ins[t],
                send_sem=send_ref.at[t], recv_sem=recv_ref.at[t], device_id=(x, y, 1 - c), device_id_type=MESH_ID).wait()

    return pl.pallas_call(
        body, out_shape=[pltpu.HBM(b.shape, b.dtype) for b in bufs],
        in_specs=[_HBM] * (2 * n) + [_SEM, _SEM] + [_ANY] * len(after), out_specs=[_HBM] * n,
        input_output_aliases={n + t: t for t in range(n)}, name=name, compiler_params=_ORDERED_EFFECT,
    )(*[_in_hbm(g) for g in grads], *bufs, send_sems, recv_sems, *after)


def _chip_scatter_start_call(chip_sums, name):
    n = len(chip_sums)

    def body(*refs):
        srcs, outs = refs[:n], refs[n:2 * n]
        send_sems, recv_sems, token = refs[2 * n:]
        x, y, c = lax.axis_index("x"), lax.axis_index("y"), lax.axis_index("c")
        chips = [(1 - x, y), (x, 1 - y), (1 - x, 1 - y)]
        for k, (cx, cy) in enumerate(chips):
            for t in range(n):
                pltpu.make_async_remote_copy(
                    src_ref=srcs[t].at[2 * cx + cy], dst_ref=outs[t].at[k], send_sem=send_sems.at[3 * t + k],
                    recv_sem=recv_sems.at[3 * t + k], device_id=(cx, cy, c), device_id_type=MESH_ID).start()
        token[...] = jnp.zeros_like(token)

    dma = pltpu.SemaphoreType.DMA
    return pl.pallas_call(
        body, out_shape=[pltpu.HBM((3,) + p.shape[1:], p.dtype) for p in chip_sums] + [dma((3 * n,)), dma((3 * n,)), _TOKEN],
        in_specs=[_HBM] * n, out_specs=[_HBM] * n + [_SEM, _SEM, _VMEM], name=name, compiler_params=_ORDERED_EFFECT,
    )(*[_in_hbm(p) for p in chip_sums])


def _chip_scatter_finish_call(chip_sums, bufs, send_sems, recv_sems, after, name):
    n = len(chip_sums)
    after = list(after)

    def body(*refs):
        srcs, ins, send_ref, recv_ref = refs[:n], refs[n:2 * n], refs[2 * n], refs[2 * n + 1]
        x, y, c = lax.axis_index("x"), lax.axis_index("y"), lax.axis_index("c")
        chips = [(1 - x, y), (x, 1 - y), (1 - x, 1 - y)]
        for k, (cx, cy) in enumerate(chips):
            for t in range(n):
                pltpu.make_async_remote_copy(
                    src_ref=srcs[t].at[2 * cx + cy], dst_ref=ins[t].at[k], send_sem=send_ref.at[3 * t + k],
                    recv_sem=recv_ref.at[3 * t + k], device_id=(cx, cy, c), device_id_type=MESH_ID).wait()

    return pl.pallas_call(
        body, out_shape=[pltpu.HBM(b.shape, b.dtype) for b in bufs],
        in_specs=[_HBM] * (2 * n) + [_SEM, _SEM] + [_ANY] * len(after), out_specs=[_HBM] * n,
        input_output_aliases={n + t: t for t in range(n)}, name=name, compiler_params=_ORDERED_EFFECT,
    )(*[_in_hbm(p) for p in chip_sums], *bufs, send_sems, recv_sems, *after)


def _place_own_call(mine, chip_core, name):
    n = len(mine)

    def body(ids_ref, *refs):
        for t in range(n):
            refs[n + t][0] = refs[t][...]

    def imap_out(s):
        pad = (0,) * (s.ndim - 1)
        return lambda i, ids_ref: (ids_ref[0], ids_ref[1]) + pad

    grid_spec = pltpu.PrefetchScalarGridSpec(
        num_scalar_prefetch=1, grid=(1,), in_specs=[pl.BlockSpec(s.shape, lambda i, ids_ref, k=s.ndim: (0,) * k) for s in mine],
        out_specs=[pl.BlockSpec((1,) + s.shape, imap_out(s)) for s in mine])
    return pl.pallas_call(
        body, grid_spec=grid_spec,
        out_shape=[jax.ShapeDtypeStruct((N_CHIPS, 2 * s.shape[0]) + s.shape[1:], s.dtype) for s in mine],
        name=name, compiler_params=_cparams(1))(chip_core, *mine)


def _gather_start_call(mine, bufs, after, name):
    n = len(mine)
    half = [s.shape[0] for s in mine]

    def body(*refs):
        srcs, outs = refs[:n], refs[2 * n + 1:3 * n + 1]
        send_sems, recv_sib, recv_ici, token = refs[3 * n + 1:]
        x, y, c = lax.axis_index("x"), lax.axis_index("y"), lax.axis_index("c")
        chips = [(1 - x, y), (x, 1 - y), (1 - x, 1 - y)]
        for t in range(n):
            dst = _half_rows(outs[t], 2 * x + y, half[t], c)
            pltpu.make_async_remote_copy(src_ref=srcs[t], dst_ref=dst, send_sem=send_sems.at[4 * t], recv_sem=recv_sib.at[t],
                                         device_id=(x, y, 1 - c), device_id_type=MESH_ID).start()
            for j, chip in enumerate(chips):
                pltpu.make_async_remote_copy(src_ref=srcs[t], dst_ref=dst, send_sem=send_sems.at[4 * t + 1 + j],
                                             recv_sem=recv_ici.at[3 * t + j], device_id=(*chip, c), device_id_type=MESH_ID).start()
        token[...] = jnp.zeros_like(token)

    dma = pltpu.SemaphoreType.DMA
    return pl.pallas_call(
        body, out_shape=[pltpu.HBM(b.shape, b.dtype) for b in bufs] + [dma((4 * n,)), dma((n,)), dma((3 * n,)), _TOKEN],
        in_specs=[_HBM] * (2 * n) + [_ANY], out_specs=[_HBM] * n + [_SEM] * 3 + [_VMEM],
        input_output_aliases={n + t: t for t in range(n)}, name=name, compiler_params=_ORDERED_EFFECT,
    )(*[_in_hbm(s) for s in mine], *[_in_hbm(b) for b in bufs], after)


def _gather_forward_call(bufs, recv_ici, after, name):
    n = len(bufs)
    half = [b.shape[1] // 2 for b in bufs]

    def body(*refs):
        ins, recv_ici_ref = refs[:n], refs[n]
        outs = refs[n + 2:2 * n + 2]
        send_fwd, recv_fwd, token = refs[2 * n + 2:]
        x, y, c = lax.axis_index("x"), lax.axis_index("y"), lax.axis_index("c")
        chips = [(1 - x, y), (x, 1 - y), (1 - x, 1 - y)]
        for j, (cx, cy) in enumerate(chips):
            for t in range(n):
                landed = _half_rows(ins[t], 2 * cx + cy, half[t], c)
                dst = _half_rows(outs[t], 2 * cx + cy, half[t], c)
                pltpu.make_async_remote_copy(src_ref=landed, dst_ref=landed, send_sem=send_fwd.at[3 * t + j],
                                             recv_sem=recv_ici_ref.at[3 * t + j], device_id=(cx, cy, c),
                                             device_id_type=MESH_ID).wait_recv()
                pltpu.make_async_remote_copy(src_ref=landed, dst_ref=dst, send_sem=send_fwd.at[3 * t + j],
                                             recv_sem=recv_fwd.at[3 * t + j], device_id=(x, y, 1 - c),
                                             device_id_type=MESH_ID).start()
        token[...] = jnp.zeros_like(token)

    dma = pltpu.SemaphoreType.DMA
    return pl.pallas_call(
        body, out_shape=[pltpu.HBM(b.shape, b.dtype) for b in bufs] + [dma((3 * n,)), dma((3 * n,)), _TOKEN],
        in_specs=[_HBM] * n + [_SEM, _ANY], out_specs=[_HBM] * n + [_SEM] * 2 + [_VMEM],
        input_output_aliases={t: t for t in range(n)}, name=name, compiler_params=_ORDERED_EFFECT,
    )(*bufs, recv_ici, after)


def _gather_finish_call(shards, bufs, send_sems, recv_sib, send_fwd, recv_fwd, after, name):
    n = len(bufs)
    half = [b.shape[1] // 2 for b in bufs]

    def body(*refs):
        srcs, ins = refs[:n], refs[n:2 * n]
        send_ref, recv_sib_ref, send_fwd_ref, recv_fwd_ref = refs[2 * n:2 * n + 4]
        x, y, c = lax.axis_index("x"), lax.axis_index("y"), lax.axis_index("c")
        chips = [(1 - x, y), (x, 1 - y), (1 - x, 1 - y)]
        sibling = (x, y, 1 - c)
        for t in range(n):
            for k in range(4):
                pltpu.make_async_remote_copy(src_ref=srcs[t], dst_ref=srcs[t], send_sem=send_ref.at[4 * t + k],
                                             recv_sem=recv_sib_ref.at[t], device_id=sibling, device_id_type=MESH_ID).wait_send()
            from_sibling = _half_rows(ins[t], 2 * x + y, half[t], 1 - c)
            pltpu.make_async_remote_copy(src_ref=from_sibling, dst_ref=from_sibling, send_sem=send_ref.at[4 * t],
                                         recv_sem=recv_sib_ref.at[t], device_id=sibling, device_id_type=MESH_ID).wait_recv()
            for j, (cx, cy) in enumerate(chips):
                sent = _half_rows(ins[t], 2 * cx + cy, half[t], c)
                passed = _half_rows(ins[t], 2 * cx + cy, half[t], 1 - c)
                pltpu.make_async_remote_copy(src_ref=sent, dst_ref=passed, send_sem=send_fwd_ref.at[3 * t + j],
                                             recv_sem=recv_fwd_ref.at[3 * t + j], device_id=sibling, device_id_type=MESH_ID).wait()

    return pl.pallas_call(
        body, out_shape=[pltpu.HBM(b.shape, b.dtype) for b in bufs],
        in_specs=[_HBM] * (2 * n) + [_SEM] * 4 + [_ANY], out_specs=[_HBM] * n,
        input_output_aliases={n + t: t for t in range(n)}, name=name, compiler_params=_ORDERED_EFFECT,
    )(*[_in_hbm(s) for s in shards], *bufs, send_sems, recv_sib, send_fwd, recv_fwd, after)


def _pair_gather_call(bufs, name):
    n = len(bufs)
    half = [b.shape[0] // 2 for b in bufs]

    def body(*refs):
        srcs, outs, send_sems, recv_sems = refs[:n], refs[n:2 * n], refs[2 * n], refs[2 * n + 1]
        x, y, c = lax.axis_index("x"), lax.axis_index("y"), lax.axis_index("c")
        for t in range(n):
            pltpu.make_async_remote_copy(
                src_ref=_half_rows(srcs[t], None, half[t], c), dst_ref=_half_rows(outs[t], None, half[t], c),
                send_sem=send_sems.at[t], recv_sem=recv_sems.at[t], device_id=(x, y, 1 - c), device_id_type=MESH_ID).start()
        for t in range(n):
            pltpu.make_async_remote_copy(
                src_ref=_half_rows(srcs[t], None, half[t], c), dst_ref=_half_rows(outs[t], None, half[t], 1 - c),
                send_sem=send_sems.at[t], recv_sem=recv_sems.at[t], device_id=(x, y, 1 - c), device_id_type=MESH_ID).wait()

    return pl.pallas_call(
        body, out_shape=[jax.ShapeDtypeStruct(b.shape, b.dtype) for b in bufs], in_specs=[_ANY] * n, out_specs=[_ANY] * n,
        input_output_aliases={t: t for t in range(n)},
        scratch_shapes=[pltpu.SemaphoreType.DMA((n,)), pltpu.SemaphoreType.DMA((n,))], name=name)(*bufs)


def _pair_gather_start_call(bufs, name):
    n = len(bufs)
    half = [b.shape[0] // 2 for b in bufs]

    def body(*refs):
        srcs, outs = refs[:n], refs[n:2 * n]
        send_sems, recv_sems, token = refs[2 * n:]
        x, y, c = lax.axis_index("x"), lax.axis_index("y"), lax.axis_index("c")
        for t in range(n):
            pltpu.make_async_remote_copy(
                src_ref=_half_rows(srcs[t], None, half[t], c), dst_ref=_half_rows(outs[t], None, half[t], c),
                send_sem=send_sems.at[t], recv_sem=recv_sems.at[t], device_id=(x, y, 1 - c), device_id_type=MESH_ID).start()
        token[...] = jnp.zeros_like(token)

    dma = pltpu.SemaphoreType.DMA
    return pl.pallas_call(
        body, out_shape=[pltpu.HBM(b.shape, b.dtype) for b in bufs] + [dma((n,)), dma((n,)), _TOKEN],
        in_specs=[_HBM] * n, out_specs=[_HBM] * n + [_SEM, _SEM, _VMEM], input_output_aliases={t: t for t in range(n)},
        name=name, compiler_params=_ORDERED_EFFECT)(*[_in_hbm(b) for b in bufs])


def _pair_gather_finish_call(bufs, send_sems, recv_sems, after, name):
    n = len(bufs)
    after = list(after)
    half = [b.shape[0] // 2 for b in bufs]

    def body(*refs):
        ins, send_ref, recv_ref = refs[:n], refs[n], refs[n + 1]
        x, y, c = lax.axis_index("x"), lax.axis_index("y"), lax.axis_index("c")
        for t in range(n):
            pltpu.make_async_remote_copy(
                src_ref=_half_rows(ins[t], None, half[t], c), dst_ref=_half_rows(ins[t], None, half[t], 1 - c),
                send_sem=send_ref.at[t], recv_sem=recv_ref.at[t], device_id=(x, y, 1 - c), device_id_type=MESH_ID).wait()

    return pl.pallas_call(
        body, out_shape=[pltpu.HBM(b.shape, b.dtype) for b in bufs], in_specs=[_HBM] * n + [_SEM, _SEM] + [_ANY] * len(after),
        out_specs=[_HBM] * n, input_output_aliases={t: t for t in range(n)}, name=name, compiler_params=_ORDERED_EFFECT,
    )(*bufs, send_sems, recv_sems, *after)


def _pack_rows(flats, dtype, row_multiple):
    flat = jnp.concatenate([f.reshape(-1).astype(dtype) for f in flats])
    n = flat.shape[0]
    rows = -(-n // PACK_W)
    rows = -(-rows // row_multiple) * row_multiple
    return jnp.pad(flat, (0, rows * PACK_W - n)).reshape(rows, PACK_W)


def _unpack(flat, shapes):
    out, off = [], 0
    for shp in shapes:
        n = math.prod(shp)
        out.append(flat[off:off + n].reshape(shp))
        off += n
    return out


_W_IN_SEGMENTS = ((R_ML, R_END, OFF_ML), (R_SG, R_ML, OFF_SG), (R_CV, R_SGI, OFF_CV), (R_SGI, R_MQ, OFF_SGI), (R_MQ, R_SG, OFF_MQ),
                  (R_CQ, R_CKV, OFF_CQ), (R_CKV, R_KR, OFF_CKV), (R_KR, R_CV, OFF_KR + NOPE))
W_IN_SHARD = R_END // N_CHIPS


def _realign_call(wg):
    tr = 128

    def body(w_ref, o_ref):
        pieces, pos = [], 0
        for r0, r1, a0 in _W_IN_SEGMENTS:
            if a0 > pos:
                pieces.append(jnp.zeros((tr, a0 - pos), o_ref.dtype))
            while r0 < r1:
                j = r0 // W_IN_SHARD
                hi = min(r1, (j + 1) * W_IN_SHARD)
                pieces.append(w_ref[j, :, r0 - j * W_IN_SHARD:hi - j * W_IN_SHARD])
                a0, r0 = a0 + hi - r0, hi
            pos = a0
        pieces.append(jnp.zeros((tr, NP - pos), o_ref.dtype))
        o_ref[...] = jnp.concatenate(pieces, axis=1)

    return pl.pallas_call(
        body, grid=(D // tr,), in_specs=[_bs((N_CHIPS, tr, W_IN_SHARD), lambda i: (0, i, 0))],
        out_specs=_bs((tr, NP), lambda i: (i, 0)), out_shape=jax.ShapeDtypeStruct((D, NP), wg.dtype),
        name="w_in_realign", compiler_params=_cparams(1))(wg)


def _unalign_call(dw, out_dtype):
    tr = 128
    by_ref = sorted(_W_IN_SEGMENTS)

    def body(dw_ref, o_ref):
        for j in range(N_CHIPS):
            lo_j, hi_j = j * W_IN_SHARD, (j + 1) * W_IN_SHARD
            pieces = []
            for r0, r1, a0 in by_ref:
                lo, hi = max(r0, lo_j), min(r1, hi_j)
                if lo < hi:
                    pieces.append(dw_ref[:, a0 + lo - r0:a0 + hi - r0])
            o_ref[j] = jnp.concatenate(pieces, axis=1).astype(o_ref.dtype)

    return pl.pallas_call(
        body, grid=(D // tr,), in_specs=[_bs((tr, NP), lambda i: (i, 0))],
        out_specs=_bs((N_CHIPS, tr, W_IN_SHARD), lambda i: (0, i, 0)),
        out_shape=jax.ShapeDtypeStruct((N_CHIPS, D, W_IN_SHARD), out_dtype), name="w_in_unalign", compiler_params=_cparams(1))(dw)


def _wuq_to_heads(w):
    w3 = w.reshape(QL, H, QKH)
    w3 = jnp.pad(w3, ((0, 0), (0, 0), (0, LANES - QKH)))
    return jnp.transpose(w3, (1, 0, 2))


def _wuq_from_heads(wh):
    return jnp.transpose(wh[:, :, :QKH], (1, 0, 2)).reshape(QL, H * QKH)


def _wukv_to_heads(w):
    w3 = w.reshape(KVL, H, NOPE + VH)
    wkn = jnp.transpose(jnp.pad(w3[:, :, :NOPE], ((0, 0), (0, 0), (0, LANES - NOPE))), (1, 0, 2))
    wv3 = w3[:, :, NOPE:]
    z = jnp.zeros((KVL, VH), w.dtype)
    cols = []
    for h in range(H):
        cols += [wv3[:, h], z] if h % 2 == 0 else [z, wv3[:, h]]
    return wkn, jnp.concatenate(cols, axis=1)


def _wukv_from_heads(wkn, wv):
    kn = jnp.transpose(wkn[:, :, :NOPE], (1, 0, 2))
    vs = jnp.stack([wv[:, LANES * h + VH * (h % 2):LANES * h + VH * (h % 2) + VH] for h in range(H)], axis=1)
    return jnp.concatenate([kn, vs], axis=2).reshape(KVL, H * (NOPE + VH))


def _layer_fwd(x, mem, tabs, p):
    proj, h = _proj_call(x, p["norm_g"], p["w_in"])
    if p.get("late") is not None:
        p = dict(p, **p["late"](proj))
    q, k, v = _mla_prep_call(proj, tabs, p["cq_g"], p["ckv_g"], p["qg"], p["kg"], p["wuq"], p["wkn"], p["wv"])
    ya, attn_o, attn_lse = _attn_call(q, k, v, proj)
    bm = p["bm"]
    if p.get("after_attn") is not None:
        bm = _tie(bm, p["after_attn"](ya))
    yb = _conv_call(proj, p["conv_w"], p["conv_b"])
    yc = _sg_call(proj, p["ln_g"], p["ln_b"], p["ws"], p["bs"])
    mk, mv = _memkv_call(mem, p["mem_g"], p["wm"], p["mkg"])
    yd = _mem_call(proj, mk, mv, p["mqg"])
    out = _merge_call((ya, yb, yc, yd), proj, bm, p["wb"], p["wo"], x)
    return out, dict(p=p, x=x, proj=proj, h=h, q=q, k=k, v=v, attn_o=attn_o, attn_lse=attn_lse, ys=(ya, yb, yc, yd), mk=mk, mv=mv)


def _layer_bwd(dout, mem, tabs, p, sv, start_after=None, on_rest_grads=None, on_grads=None):
    proj = sv["proj"]
    bm = p["bm"] if start_after is None else _tie(p["bm"], start_after)
    dya, dyb, dyc, dyd, dml, dbm, dwb, dwo = _merge_bwd_call(sv["ys"], proj, bm, p["wb"], p["wo"], dout)
    dq, dk, dv, dsg_a = _attn_bwd_call(sv["q"], sv["k"], sv["v"], proj, dya, sv["attn_o"], sv["attn_lse"])
    dlat, dcqg, dckvg, dqg, dkg, dwuq, dwkn, dwv = _mla_prep_bwd_call(
        proj, tabs, p["cq_g"], p["ckv_g"], p["qg"], p["kg"], p["wuq"], p["wkn"], p["wv"], dq, dk, dv)
    dbg, dcg, dxi, dsg_b, dcw, dcb = _conv_bwd_call(proj, p["conv_w"], p["conv_b"], dyb)
    duv, dsg_c, dlg, dlb, dws, dbs = _sg_bwd_call(proj, p["ln_g"], p["ln_b"], p["ws"], p["bs"], dyc)
    dmq, dsg_d, dmk, dmv, dmqg = _mem_bwd_call(proj, sv["mk"], sv["mv"], p["mqg"], dyd)
    dmem_g, dwm, dmkg = _memkv_bwd_call(mem, p["mem_g"], p["wm"], p["mkg"], dmk, dmv)
    grads = dict(cq_norm_g=dcqg, ckv_norm_g=dckvg, mla_q_norm_g=dqg[:, :QKH], mla_k_norm_g=dkg[:, :QKH],
                 conv_w=dcw, conv_b=dcb, sg_ln_g=dlg, sg_ln_b=dlb, w_spatial=dws, b_spatial=dbs,
                 mem_norm_g=dmem_g, mem_q_norm_g=dmqg, mem_k_norm_g=dmkg, b_merge=dbm,
                 wuq_heads=dwuq, wkn_heads=dwkn, wv_heads=dwv, w_mem_kv=dwm, w_branch_chips=dwb, w_out=dwo)
    started = [on_rest_grads(grads)] if on_rest_grads is not None else []
    dproj, off = dml, NB * D
    for piece in (dsg_a, dsg_b, dsg_c, dsg_d, dbg, dcg, dxi, duv, dmq, dlat):
        dproj = lax.dynamic_update_slice(dproj, piece, (0, off))
        off += piece.shape[1]
    grads["w_in_aligned"] = _dw_call(sv["h"], dproj, started)
    tokens = on_grads(grads) if on_grads is not None else ()
    dx, dnorm_g = _dh_call(dproj, p["w_in"], sv["x"], p["norm_g"], dout, tokens)
    grads["norm_g"] = dnorm_g
    return dx, grads


def _chips_to_cols(a):
    return jnp.concatenate([a[j] for j in range(N_CHIPS)], axis=1)


def _cols_to_chips(a):
    cols = a.shape[1] // N_CHIPS
    return jnp.stack([a[:, cols * j:cols * (j + 1)] for j in range(N_CHIPS)])


def _layer_params_first(l, rep, w_in_gathered, conv_w, b_merge):
    pad_g = lambda g: jnp.pad(g, (0, LANES - QKH)).reshape(1, LANES)
    return dict(
        norm_g=rep["norm_g"][l].reshape(1, D), w_in=_realign_call(w_in_gathered),
        cq_g=rep["cq_norm_g"][l].reshape(1, QL), ckv_g=rep["ckv_norm_g"][l].reshape(1, KVL),
        qg=pad_g(rep["mla_q_norm_g"][l]), kg=pad_g(rep["mla_k_norm_g"][l]),
        conv_w=conv_w, conv_b=rep["conv_b"][l].reshape(1, CW),
        ln_g=rep["sg_ln_g"][l].reshape(1, SGW), ln_b=rep["sg_ln_b"][l].reshape(1, SGW),
        ws=rep["w_spatial"][l], bs=rep["b_spatial"][l].reshape(SGG, SGC, 1),
        mem_g=rep["mem_norm_g"][l].reshape(1, D),
        mqg=rep["mem_q_norm_g"][l].reshape(1, MHD), mkg=rep["mem_k_norm_g"][l].reshape(1, MHD), bm=b_merge)


def _layer_params_rest(gathered):
    wkn, wv = _wukv_to_heads(_chips_to_cols(gathered["w_ukv"]))
    return dict(wuq=_wuq_to_heads(_chips_to_cols(gathered["w_uq"])), wkn=wkn, wv=wv,
                wm=gathered["w_mem_kv"].reshape(D, 2 * MH * MHD), wb=gathered["w_branch"], wo=gathered["w_out"].reshape(D, D))


def _layer_params(l, rep, gathered, conv_w, b_merge):
    return dict(_layer_params_first(l, rep, gathered["w_in"], conv_w, b_merge), **_layer_params_rest(gathered))


def _forward_backward(x, mem, pos, target, params, bwd_hooks=None):
    tabs = _rope_tables(pos)
    params = list(params)
    saved = []
    act = x
    for l in range(DEPTH):
        if callable(params[l]):
            params[l] = params[l](saved[-1], act)
        act, sv = _layer_fwd(act, mem, tabs, params[l])
        saved.append(sv)
    dy, sq = _loss_call(act, target)
    grads = [None] * DEPTH
    token = None
    for l in reversed(range(DEPTH)):
        hooks = dict(bwd_hooks[l]) if bwd_hooks else {}
        after_layer = hooks.pop("after_layer", None)
        dy, grads[l] = _layer_bwd(dy, mem, tabs, saved[l]["p"], saved[l], start_after=token, **hooks)
        token = after_layer(dy) if after_layer is not None else None
    return sq, dy, grads


_SHARDED_MM = ("w_in", "w_branch", "w_out", "w_mem_kv", "w_uq", "w_ukv")
_SHARDED_F32 = ("conv_w", "b_merge")
_REPLICATED = ("norm_g", "cq_norm_g", "ckv_norm_g", "mla_q_norm_g", "mla_k_norm_g", "conv_b", "sg_ln_g", "sg_ln_b",
               "w_spatial", "b_spatial", "mem_norm_g", "mem_q_norm_g", "mem_k_norm_g")
_ALL_REDUCED = _REPLICATED + _SHARDED_F32
_WEIGHTS = ("norm_g", "w_in", "cq_norm_g", "ckv_norm_g", "w_uq", "w_ukv", "mla_q_norm_g", "mla_k_norm_g", "conv_w", "conv_b",
            "sg_ln_g", "sg_ln_b", "w_spatial", "b_spatial", "mem_norm_g", "w_mem_kv", "mem_q_norm_g", "mem_k_norm_g",
            "b_merge", "w_branch", "w_out")
_SMALL = tuple(n for n in _WEIGHTS if n not in _SHARDED_MM)


class _SmallGather:
    def __init__(self, blk, after, tag):
        self.blk, self.tag = blk, tag
        x, y, c = lax.axis_index("x"), lax.axis_index("y"), lax.axis_index("c")
        own = _place_block_call(blk, (4 * x + 2 * y + c).astype(jnp.int32).reshape(1), tag + "place_own")
        self.buf, self.send, self.recv, self.token = _small_gather_start_call(blk, own, after, tag + "start")

    def finish(self, after):
        return _small_gather_finish_call(self.blk, self.buf, self.send, self.recv, after, self.tag + "finish")


def _small_sharded_weights(w, got):
    names = _SHARDED_F32
    per_chip = [_unpack(got[2 * j].reshape(-1), [w[n].shape for n in names]) for j in range(N_CHIPS)]
    return {n: jnp.concatenate([per_chip[j][t] for j in range(N_CHIPS)], axis=2) for t, n in enumerate(names)}


class _Gather:
    def __init__(self, w, layer, names, after, tag):
        self.names, self.tag = names, tag
        x, y, c = lax.axis_index("x"), lax.axis_index("y"), lax.axis_index("c")
        chip_core = jnp.stack([2 * x + y, c]).astype(jnp.int32)
        halves = [w[n].shape[1] // 2 for n in names]
        self.srcs = [lax.dynamic_slice_in_dim(w[n][layer], c * h, h, axis=0).astype(MM) for n, h in zip(names, halves)]
        k = len(names)
        out = _gather_start_call(self.srcs, _place_own_call(self.srcs, chip_core, tag + "place_own"), after, tag + "start")
        self.bufs, self.send, self.recv_sib, self.recv_ici, self.token = out[:k], out[k], out[k + 1], out[k + 2], out[k + 3]

    def pass_on(self, after):
        k = len(self.names)
        out = _gather_forward_call(self.bufs, self.recv_ici, after, self.tag + "forward")
        self.bufs, self.send_fwd, self.recv_fwd = out[:k], out[k], out[k + 1]
        return out[k + 2]

    def finish(self, after):
        got = _gather_finish_call(self.srcs, self.bufs, self.send, self.recv_sib, self.send_fwd, self.recv_fwd, after,
                                  self.tag + "finish")
        return dict(zip(self.names, got))


class _ReduceScatter:
    SLABS = dict(
        w_in=lambda g: _unalign_call(g["w_in_aligned"], MM),
        w_branch=lambda g: g["w_branch_chips"].reshape(N_CHIPS, NB * BW, D // N_CHIPS),
        w_out=lambda g: g["w_out"].reshape(N_CHIPS, D // N_CHIPS, D),
        w_mem_kv=lambda g: g["w_mem_kv"].reshape(N_CHIPS, D // N_CHIPS, 2 * MH * MHD),
        w_uq=lambda g: _cols_to_chips(_wuq_from_heads(g["wuq_heads"])),
        w_ukv=lambda g: _cols_to_chips(_wukv_from_heads(g["wkn_heads"], g["wv_heads"])))

    def __init__(self, tag, names):
        self.tag, self.names = tag, names

    def exchange(self, grads):
        self.tensors = [self.SLABS[n](grads) for n in self.names]
        n = len(self.tensors)
        out = _pair_exchange_start_call(self.tensors, self.tag + "exchange_start")
        self.ex_bufs, self.ex_send, self.ex_recv = out[:n], out[n], out[n + 1]
        return out[n + 2]

    def scatter(self, after):
        n = len(self.tensors)
        c = lax.axis_index("c")
        from_sibling = _pair_exchange_finish_call(self.tensors, self.ex_bufs, self.ex_send, self.ex_recv, after,
                                                  self.tag + "exchange_finish")
        self.chip_sums = _pair_sum_call(self.tensors, from_sibling, c.astype(jnp.int32).reshape(1), self.tag + "pair_sum")
        out = _chip_scatter_start_call(self.chip_sums, self.tag + "scatter_start")
        self.bufs, self.send_sems, self.recv_sems, self.token = out[:n], out[n], out[n + 1], out[n + 2]
        return self.token

    def finish(self, after):
        x, y, c = lax.axis_index("x"), lax.axis_index("y"), lax.axis_index("c")
        chip_core = jnp.stack([2 * x + y, c]).astype(jnp.int32)
        from_chips = _chip_scatter_finish_call(self.chip_sums, self.bufs, self.send_sems, self.recv_sems, after,
                                               self.tag + "scatter_finish")
        self.mine = _owner_sum_call(self.chip_sums, from_chips, chip_core, self.tag + "owner_sum")
        return dict(zip(self.names, _pair_gather_call(self.mine, self.tag + "pair_gather")))

    def finish_but_swap(self, after):
        n = len(self.names)
        x, y, c = lax.axis_index("x"), lax.axis_index("y"), lax.axis_index("c")
        chip_core = jnp.stack([2 * x + y, c]).astype(jnp.int32)
        from_chips = _chip_scatter_finish_call(self.chip_sums, self.bufs, self.send_sems, self.recv_sems, after,
                                               self.tag + "scatter_finish")
        mine = _owner_sum_call(self.chip_sums, from_chips, chip_core, self.tag + "owner_sum")
        out = _pair_gather_start_call(mine, self.tag + "pair_gather_start")
        self.pg_bufs, self.pg_send, self.pg_recv = out[:n], out[n], out[n + 1]
        return out[n + 2]

    def swapped(self, after):
        got = _pair_gather_finish_call(self.pg_bufs, self.pg_send, self.pg_recv, after, self.tag + "pair_gather_finish")
        return dict(zip(self.names, got))


def _small_sums(shapes, sq, got):
    total = _sum8_call(got).reshape(-1)
    parts = _unpack(total, [shapes[n] for n in _ALL_REDUCED] + [sq.shape])
    out = dict(zip(_ALL_REDUCED, parts))
    sq_total = parts[-1]
    chip = 2 * lax.axis_index("x") + lax.axis_index("y")
    for n in _SHARDED_F32:
        size = out[n].shape[2] // N_CHIPS
        out[n] = lax.dynamic_slice_in_dim(out[n], chip * size, size, axis=2)
    return out, sq_total


def _adamw_small(w, g, m, v):
    pick = lambda t: [t[n] for n in _SMALL]
    out = _adamw_small_call(pick(w), pick(g), pick(m), pick(v), "adamw_small")
    return tuple({n: out[3 * t + k] for t, n in enumerate(_SMALL)} for k in range(3))


def kernel(x, mem, positions, norm_g, w_in, cq_norm_g, ckv_norm_g, w_uq, w_ukv, mla_q_norm_g, mla_k_norm_g, conv_w, conv_b, sg_ln_g, sg_ln_b, w_spatial, b_spatial, mem_norm_g, w_mem_kv, mem_q_norm_g, mem_k_norm_g, b_merge, w_branch, w_out, loss_target, m_norm_g, m_w_in, m_cq_norm_g, m_ckv_norm_g, m_w_uq, m_w_ukv, m_mla_q_norm_g, m_mla_k_norm_g, m_conv_w, m_conv_b, m_sg_ln_g, m_sg_ln_b, m_w_spatial, m_b_spatial, m_mem_norm_g, m_w_mem_kv, m_mem_q_norm_g, m_mem_k_norm_g, m_b_merge, m_w_branch, m_w_out, v_norm_g, v_w_in, v_cq_norm_g, v_ckv_norm_g, v_w_uq, v_w_ukv, v_mla_q_norm_g, v_mla_k_norm_g, v_conv_w, v_conv_b, v_sg_ln_g, v_sg_ln_b, v_w_spatial, v_b_spatial, v_mem_norm_g, v_w_mem_kv, v_mem_q_norm_g, v_mem_k_norm_g, v_b_merge, v_w_branch, v_w_out):
    w = dict(norm_g=norm_g, w_in=w_in, cq_norm_g=cq_norm_g, ckv_norm_g=ckv_norm_g, w_uq=w_uq, w_ukv=w_ukv,
             mla_q_norm_g=mla_q_norm_g, mla_k_norm_g=mla_k_norm_g, conv_w=conv_w, conv_b=conv_b, sg_ln_g=sg_ln_g,
             sg_ln_b=sg_ln_b, w_spatial=w_spatial, b_spatial=b_spatial, mem_norm_g=mem_norm_g, w_mem_kv=w_mem_kv,
             mem_q_norm_g=mem_q_norm_g, mem_k_norm_g=mem_k_norm_g, b_merge=b_merge, w_branch=w_branch, w_out=w_out)
    m = dict(norm_g=m_norm_g, w_in=m_w_in, cq_norm_g=m_cq_norm_g, ckv_norm_g=m_ckv_norm_g, w_uq=m_w_uq, w_ukv=m_w_ukv,
             mla_q_norm_g=m_mla_q_norm_g, mla_k_norm_g=m_mla_k_norm_g, conv_w=m_conv_w, conv_b=m_conv_b, sg_ln_g=m_sg_ln_g,
             sg_ln_b=m_sg_ln_b, w_spatial=m_w_spatial, b_spatial=m_b_spatial, mem_norm_g=m_mem_norm_g, w_mem_kv=m_w_mem_kv,
             mem_q_norm_g=m_mem_q_norm_g, mem_k_norm_g=m_mem_k_norm_g, b_merge=m_b_merge, w_branch=m_w_branch, w_out=m_w_out)
    v = dict(norm_g=v_norm_g, w_in=v_w_in, cq_norm_g=v_cq_norm_g, ckv_norm_g=v_ckv_norm_g, w_uq=v_w_uq, w_ukv=v_w_ukv,
             mla_q_norm_g=v_mla_q_norm_g, mla_k_norm_g=v_mla_k_norm_g, conv_w=v_conv_w, conv_b=v_conv_b, sg_ln_g=v_sg_ln_g,
             sg_ln_b=v_sg_ln_b, w_spatial=v_w_spatial, b_spatial=v_b_spatial, mem_norm_g=v_mem_norm_g, w_mem_kv=v_w_mem_kv,
             mem_q_norm_g=v_mem_q_norm_g, mem_k_norm_g=v_mem_k_norm_g, b_merge=v_b_merge, w_branch=v_w_branch, w_out=v_w_out)

    chip_core = jnp.stack([2 * lax.axis_index("x") + lax.axis_index("y"), lax.axis_index("c")]).astype(jnp.int32)

    first = _Gather(w, 0, ("w_in",), chip_core, "gather_l0_w_in_")
    rest = _Gather(w, 0, _SHARDED_MM[1:], first.token, "gather_l0_rest_")
    small_on_its_way = _SmallGather(_pack_rows([w[n] for n in _SHARDED_F32], F32, 8), [rest.token], "gather_small_weights_")
    later = _Gather(w, 1, _SHARDED_MM, small_on_its_way.token, "gather_l1_")
    w_in0 = first.finish(first.pass_on(later.token))["w_in"]
    small = {}

    def rest_of_layer0(proj0):
        landed = _layer_params_rest(rest.finish(rest.pass_on(proj0)))
        small.update(_small_sharded_weights(w, small_on_its_way.finish([landed["wo"]])))
        return dict(landed, conv_w=small["conv_w"][0], bm=small["b_merge"][0])

    def layer1_params(saved0, act0):
        return _layer_params(1, w, later.finish(act0), small["conv_w"][1], small["b_merge"][1])

    params0 = _layer_params_first(0, w, w_in0, None, None)
    params = [dict(params0, late=rest_of_layer0, after_attn=later.pass_on), layer1_params]
    others = _SHARDED_MM[1:]
    rs1 = _ReduceScatter("rs_l1_", _SHARDED_MM)
    rs0_rest, rs0_w_in = _ReduceScatter("rs_l0_rest_", others), _ReduceScatter("rs_l0_w_in_", ("w_in",))

    def layer0_grads_done(grads):
        return [rs0_rest.scatter([grads["w_in_aligned"]]), rs0_w_in.exchange(grads)]

    hooks = [dict(on_rest_grads=rs0_rest.exchange, on_grads=layer0_grads_done),
             dict(on_grads=lambda grads: [rs1.exchange(grads)], after_layer=lambda dy: rs1.scatter([dy]))]
    sq, grad_x, layer_grads = _forward_backward(x[0], mem[0], positions[0], loss_target[0], params, hooks)

    layered = [layer_grads[l][n] for n in _ALL_REDUCED for l in range(DEPTH)]
    small_grads = _SmallGather(_pack_rows(layered + [sq], F32, 64), [grad_x], "gather_small_grads_")
    scattering = rs0_w_in.scatter([grad_x, small_grads.token])
    swapping1 = rs1.finish_but_swap([scattering])
    swapping0 = rs0_rest.finish_but_swap([scattering, swapping1])
    shard_grads = {1: rs1.swapped([swapping0])}
    as3d = lambda a: a.reshape(DEPTH, -1, a.shape[-1])
    as2d = lambda a: a.reshape(-1, a.shape[-1])
    big = lambda t: [as3d(t[n]) for n in others]
    turned = lambda t: [jnp.swapaxes(t["w_in"], 1, 2)]
    assert W_IN_SHARD % (8 * 7) == 0

    def update_w_in(l, grad, prev):
        return _adamw_layer_call(l, turned(w), [grad.T], turned(m), turned(v), prev, [], "adamw_w_in_l%d" % l, steps=7)

    def update_others(l, prev):
        return _adamw_layer_call(l, big(w), [as2d(shard_grads[l][n]) for n in others], big(m), big(v), prev, [], "adamw_l%d" % l)

    upd1 = update_others(1, None)
    shard_grads[0] = rs0_rest.swapped([upd1[0]])
    upd = update_others(0, upd1)
    full_shapes = {n: w[n].shape for n in _REPLICATED}
    full_shapes.update(conv_w=(DEPTH, 3, CW), b_merge=(DEPTH, NB, D))
    g, sq_total = _small_sums(full_shapes, sq, small_grads.finish([upd[0]]))
    loss = 0.5 / D * jnp.sum(sq_total)
    delta, new_m, new_v = _adamw_small(w, g, m, v)
    upd_in1 = update_w_in(1, shard_grads[1]["w_in"], None)
    w_in_grad0 = rs0_w_in.finish([grad_x, upd_in1[0], upd[0], delta["norm_g"]])["w_in"]
    upd_in = update_w_in(0, w_in_grad0, upd_in1)
    g["w_in"], delta["w_in"], new_m["w_in"], new_v["w_in"] = [jnp.swapaxes(a, 1, 2) for a in upd_in]
    for t, n in enumerate(others):
        g[n], delta[n], new_m[n], new_v[n] = [a.reshape(w[n].shape) for a in upd[4 * t:4 * t + 4]]
    return (loss, grad_x[None], *[g[n] for n in _WEIGHTS], *[delta[n] for n in _WEIGHTS],
            *[new_m[n] for n in _WEIGHTS], *[new_v[n] for n in _WEIGHTS])
```
